```python
import jax
import jax.numpy as jnp
from jax import lax
import numpy as np

D_MODEL = 1024
BATCH = 32
SEQ = 2048
DEPTH = 2

GRID_W = 64
CTX_LEN = 256
N_MIXERS = 2
N_ATTN_LAYERS = (DEPTH + N_MIXERS - 1) // N_MIXERS
N_RET_LAYERS = DEPTH // N_MIXERS

HEAD_DIM = 64
N_HEADS = D_MODEL // HEAD_DIM
N_KV_HEADS = N_HEADS // 4
GQA_GROUP = N_HEADS // N_KV_HEADS
WINDOW = 128
ATTN_BLOCK = 128
ATTN_PROJ = (N_HEADS + 2 * N_KV_HEADS) * HEAD_DIM

RET_HEADS = D_MODEL // 256
RET_QK_DIM = D_MODEL // RET_HEADS
RET_V_DIM = 2 * D_MODEL // RET_HEADS
RET_VWIDTH = 2 * D_MODEL
RET_CHUNK = 128
RET_PROJ = 2 * D_MODEL + 2 * RET_VWIDTH

D_FF = -(-8 * D_MODEL // (3 * 256)) * 256

ROPE_BASE = 10000.0
EPS = 1e-6
NEG_INF = -1e30

kernel_name = 'hybrid_swa_sink_retention_dit'


def rms_norm(x, g):
    xf = x.astype(jnp.float32)
    y = xf * lax.rsqrt(jnp.mean(xf * xf, axis=-1, keepdims=True) + EPS)
    return (y * g.astype(jnp.float32)).astype(x.dtype)


def modulate(h, shift, scale):
    return h * (1 + scale) + shift


def grid_positions(n):
    rows = n // GRID_W
    row = jnp.broadcast_to(jnp.arange(rows, dtype=jnp.int32)[:, None], (rows, GRID_W)).reshape(n)
    col = jnp.broadcast_to(jnp.arange(GRID_W, dtype=jnp.int32)[None, :], (rows, GRID_W)).reshape(n)
    return row, col


def rope_tables(n, head_dim):
    row, col = grid_positions(n)
    axis_dim = head_dim // 2
    inv = ROPE_BASE ** (-jnp.arange(0, axis_dim, 2, dtype=jnp.float32) / axis_dim)
    ang_r = row.astype(jnp.float32)[:, None] * inv
    ang_c = col.astype(jnp.float32)[:, None] * inv
    return jnp.cos(ang_r), jnp.sin(ang_r), jnp.cos(ang_c), jnp.sin(ang_c)


def rotate_axis(x, cos, sin):
    x1, x2 = jnp.split(x, 2, axis=-1)
    cos = cos[:, None, :]
    sin = sin[:, None, :]
    return jnp.concatenate([x1 * cos - x2 * sin, x1 * sin + x2 * cos], axis=-1)


def rope_2d(x, tables):
    cos_r, sin_r, cos_c, sin_c = tables
    xr, xc = jnp.split(x.astype(jnp.float32), 2, axis=-1)
    out = jnp.concatenate([rotate_axis(xr, cos_r, sin_r), rotate_axis(xc, cos_c, sin_c)], axis=-1)
    return out.astype(x.dtype)


def swiglu(h, w_in, w_out):
    gate, up = jnp.split(h @ w_in, 2, axis=-1)
    return (jax.nn.silu(gate) * up) @ w_out


def windowed_gqa_sink(h_x, h_c, w_qkv, q_gain, k_gain, sink, w_o, need_ctx_out):
    B, S, _ = h_x.shape
    L = h_c.shape[1]
    nb = S // ATTN_BLOCK
    band = ATTN_BLOCK + 2 * WINDOW
    scale = HEAD_DIM ** -0.5
    qd = N_HEADS * HEAD_DIM
    kvd = N_KV_HEADS * HEAD_DIM
    tables = rope_tables(S, HEAD_DIM)
    sink_g = sink.astype(jnp.float32).reshape(N_KV_HEADS, GQA_GROUP)[None, :, :, None, None]

    q_x, k_x, v_x = jnp.split(h_x @ w_qkv, [qd, qd + kvd], axis=-1)
    q_x = rope_2d(rms_norm(q_x.reshape(B, S, N_HEADS, HEAD_DIM), q_gain), tables)
    q_x = q_x.reshape(B, S, N_KV_HEADS, GQA_GROUP, HEAD_DIM)
    k_x = rope_2d(rms_norm(k_x.reshape(B, S, N_KV_HEADS, HEAD_DIM), k_gain), tables)
    v_x = v_x.reshape(B, S, N_KV_HEADS, HEAD_DIM)
    k_c, v_c = jnp.split(h_c @ w_qkv[:, qd:], 2, axis=-1)
    k_c = rms_norm(k_c.reshape(B, L, N_KV_HEADS, HEAD_DIM), k_gain)
    v_c = v_c.reshape(B, L, N_KV_HEADS, HEAD_DIM)

    pad = ((0, 0), (WINDOW, WINDOW), (0, 0), (0, 0))
    k_pad = jnp.pad(k_x, pad)
    v_pad = jnp.pad(v_x, pad)
    r_idx = jnp.arange(ATTN_BLOCK, dtype=jnp.int32)[:, None]
    n_idx = jnp.arange(band, dtype=jnp.int32)[None, :]
    in_window = (n_idx >= r_idx) & (n_idx - r_idx <= 2 * WINDOW)

    def block(b):
        start = b * ATTN_BLOCK
        qb = lax.dynamic_slice_in_dim(q_x, start, ATTN_BLOCK, axis=1)
        kb = lax.dynamic_slice_in_dim(k_pad, start, band, axis=1)
        vb = lax.dynamic_slice_in_dim(v_pad, start, band, axis=1)
        key_pos = start - WINDOW + n_idx
        valid = in_window & (key_pos >= 0) & (key_pos < S)
        s_ctx = jnp.einsum('bqkgd,bnkd->bkgqn', qb, k_c, preferred_element_type=jnp.float32) * scale
        s_loc = jnp.einsum('bqkgd,bnkd->bkgqn', qb, kb, preferred_element_type=jnp.float32) * scale
        s_loc = jnp.where(valid, s_loc, NEG_INF)
        sink_col = jnp.broadcast_to(sink_g, s_ctx.shape[:-1] + (1,))
        p = jax.nn.softmax(jnp.concatenate([s_ctx, s_loc, sink_col], axis=-1), axis=-1).astype(vb.dtype)
        return (jnp.einsum('bkgqn,bnkd->bqkgd', p[..., :L], v_c)
                + jnp.einsum('bkgqn,bnkd->bqkgd', p[..., L:L + band], vb))

    o_x = jnp.moveaxis(lax.map(block, jnp.arange(nb, dtype=jnp.int32)), 0, 1).reshape(B, S, qd)
    out_x = o_x @ w_o

    out_c = None
    if need_ctx_out:
        q_c = rms_norm((h_c @ w_qkv[:, :qd]).reshape(B, L, N_HEADS, HEAD_DIM), q_gain)
        q_c = q_c.reshape(B, L, N_KV_HEADS, GQA_GROUP, HEAD_DIM)
        s_c = jnp.einsum('bqkgd,bnkd->bkgqn', q_c, k_c, preferred_element_type=jnp.float32) * scale
        sink_col = jnp.broadcast_to(sink_g, s_c.shape[:-1] + (1,))
        p_c = jax.nn.softmax(jnp.concatenate([s_c, sink_col], axis=-1), axis=-1).astype(v_c.dtype)
        o_c = jnp.einsum('bkgqn,bnkd->bqkgd', p_c[..., :L], v_c)
        out_c = o_c.reshape(B, L, qd) @ w_o
    return out_c, out_x


def retention_chunked(q, k, v, log_gamma, state0):
    B, T, H, _ = q.shape
    nc = T // RET_CHUNK
    pos = jnp.arange(RET_CHUNK, dtype=jnp.float32)
    diff = pos[:, None] - pos[None, :]
    intra = jnp.where(diff >= 0, jnp.exp(log_gamma[:, None, None] * jnp.maximum(diff, 0.0)), 0.0)
    q_decay = jnp.exp(log_gamma[None, :] * (pos + 1.0)[:, None])[None, :, :, None]
    k_decay = jnp.exp(log_gamma[None, :] * (RET_CHUNK - 1.0 - pos)[:, None])[None, :, :, None]
    chunk_decay = jnp.exp(log_gamma * RET_CHUNK)[None, :, None, None]

    def to_chunks(a):
        return jnp.moveaxis(a.reshape(B, nc, RET_CHUNK, H, a.shape[-1]), 1, 0)

    def step(state, inp):
        qc, kc, vc = inp
        scores = jnp.einsum('bnhd,bmhd->bhnm', qc, kc) * intra
        inner = jnp.einsum('bhnm,bmhe->bnhe', scores, vc)
        cross = jnp.einsum('bnhd,bhde->bnhe', qc, state) * q_decay
        new_state = state * chunk_decay + jnp.einsum('bmhd,bmhe->bhde', kc * k_decay, vc)
        return new_state, inner + cross

    state, out = lax.scan(step, state0, (to_chunks(q), to_chunks(k), to_chunks(v)))
    return jnp.moveaxis(out, 0, 1).reshape(B, T, H, v.shape[-1]), state


def retention_final_state(k, v, log_gamma):
    T = k.shape[1]
    pos = jnp.arange(T, dtype=jnp.float32)
    decay = jnp.exp(log_gamma[None, :] * (T - 1.0 - pos)[:, None])[None, :, :, None]
    return jnp.einsum('bthd,bthe->bhde', k * decay, v)


def bidir_retention(h_x, h_c, w_qkvg, decay_logit, gn_gain, w_o, need_ctx_out):
    B, S, _ = h_x.shape
    L = h_c.shape[1]
    qk = RET_HEADS * RET_QK_DIM
    f32 = jnp.float32
    tables = rope_tables(S, RET_QK_DIM)
    log_g = jax.nn.log_sigmoid(decay_logit.astype(f32))
    k_scale = RET_QK_DIM ** -0.5

    q_x, k_x, v_x, g_x = jnp.split(h_x @ w_qkvg, [qk, 2 * qk, 2 * qk + RET_VWIDTH], axis=-1)
    q_x = rope_2d(q_x.reshape(B, S, RET_HEADS, RET_QK_DIM), tables).astype(f32)
    k_x = rope_2d(k_x.reshape(B, S, RET_HEADS, RET_QK_DIM), tables).astype(f32) * k_scale
    v_x = v_x.reshape(B, S, RET_HEADS, RET_V_DIM).astype(f32)
    k_c, v_c = jnp.split(h_c @ w_qkvg[:, qk:2 * qk + RET_VWIDTH], [qk], axis=-1)
    k_c = k_c.reshape(B, L, RET_HEADS, RET_QK_DIM).astype(f32) * k_scale
    v_c = v_c.reshape(B, L, RET_HEADS, RET_V_DIM).astype(f32)

    def flip(a):
        return jnp.flip(a, axis=1)

    def gated_out(o, g):
        mu = jnp.mean(o, axis=-1, keepdims=True)
        var = jnp.mean(jnp.square(o - mu), axis=-1, keepdims=True)
        y = ((o - mu) * lax.rsqrt(var + EPS)).reshape(o.shape[0], o.shape[1], RET_VWIDTH) * gn_gain.astype(f32)
        return (jax.nn.silu(g) * y.astype(g.dtype)) @ w_o

    out_c = None
    if need_ctx_out:
        q_c = (h_c @ w_qkvg[:, :qk]).reshape(B, L, RET_HEADS, RET_QK_DIM).astype(f32)
        g_c = h_c @ w_qkvg[:, 2 * qk + RET_VWIDTH:]
        zero_state = jnp.zeros((B, RET_HEADS, RET_QK_DIM, RET_V_DIM), f32)
        oc_f, state_f = retention_chunked(q_c, k_c, v_c, log_g[0], zero_state)
        oc_b, state_b = retention_chunked(flip(q_c), flip(k_c), flip(v_c), log_g[1], zero_state)
        out_c = gated_out(oc_f + flip(oc_b), g_c)
    else:
        state_f = retention_final_state(k_c, v_c, log_g[0])
        state_b = retention_final_state(flip(k_c), flip(v_c), log_g[1])

    ox_f, _ = retention_chunked(q_x, k_x, v_x, log_g[0], state_f)
    ox_b, _ = retention_chunked(flip(q_x), flip(k_x), flip(v_x), log_g[1], state_b)
    out_x = gated_out(ox_f + flip(ox_b), g_x)
    return out_c, out_x


def _fwd_setup_inputs(seed: int = 0) -> dict:
    key = jax.random.key(seed)
    ks = jax.random.split(key, 19)
    f32 = jnp.float32

    def nrm(k, shape, scale):
        return jax.random.normal(k, shape, f32) * scale

    gamma = 1.0 - 2.0 ** (-5.0 - np.arange(RET_HEADS))
    decay_init = jnp.asarray(np.log(gamma / (1.0 - gamma)), dtype=f32)
    return {
        'x': nrm(ks[0], (BATCH, SEQ, D_MODEL), 1.0),
        'c': nrm(ks[1], (BATCH, D_MODEL), 1.0),
        'ctx': nrm(ks[2], (BATCH, CTX_LEN, D_MODEL), 1.0),
        'c_ctx': nrm(ks[3], (D_MODEL,), 1.0),
        'ada_w': nrm(ks[4], (DEPTH, D_MODEL, 6 * D_MODEL), 0.5 * D_MODEL ** -0.5),
        'ada_b': nrm(ks[5], (DEPTH, 6 * D_MODEL), 0.02),
        'norm1_g': 1.0 + nrm(ks[6], (DEPTH, D_MODEL), 0.02),
        'norm2_g': 1.0 + nrm(ks[7], (DEPTH, D_MODEL), 0.02),
        'ffn_w_in': nrm(ks[8], (DEPTH, D_MODEL, 2 * D_FF), D_MODEL ** -0.5),
        'ffn_w_out': nrm(ks[9], (DEPTH, D_FF, D_MODEL), D_FF ** -0.5),
        'attn_w_qkv': nrm(ks[10], (N_ATTN_LAYERS, D_MODEL, ATTN_PROJ), D_MODEL ** -0.5),
        'attn_q_norm': 1.0 + nrm(ks[11], (N_ATTN_LAYERS, HEAD_DIM), 0.02),
        'attn_k_norm': 1.0 + nrm(ks[12], (N_ATTN_LAYERS, HEAD_DIM), 0.02),
        'attn_sink': nrm(ks[13], (N_ATTN_LAYERS, N_HEADS), 0.5),
        'attn_w_o': nrm(ks[14], (N_ATTN_LAYERS, N_HEADS * HEAD_DIM, D_MODEL), (N_HEADS * HEAD_DIM) ** -0.5),
        'ret_w_qkvg': nrm(ks[15], (N_RET_LAYERS, D_MODEL, RET_PROJ), D_MODEL ** -0.5),
        'ret_decay_logit': decay_init[None, None, :] + nrm(ks[16], (N_RET_LAYERS, 2, RET_HEADS), 0.01),
        'ret_gn_g': 1.0 + nrm(ks[17], (N_RET_LAYERS, RET_VWIDTH), 0.02),
        'ret_w_o': nrm(ks[18], (N_RET_LAYERS, RET_VWIDTH, D_MODEL), RET_VWIDTH ** -0.5),
    }


def _fwd_reference(x, c, ctx, c_ctx, ada_w, ada_b, norm1_g, norm2_g, ffn_w_in, ffn_w_out,
              attn_w_qkv, attn_q_norm, attn_k_norm, attn_sink, attn_w_o,
              ret_w_qkvg, ret_decay_logit, ret_gn_g, ret_w_o):
    c_act = jax.nn.silu(c)[:, None, :]
    cc_act = jax.nn.silu(c_ctx)[None, None, :]
    y_ctx = ctx
    for i in range(DEPTH):
        need_ctx_out = i < DEPTH - 1
        mx = jnp.split(c_act @ ada_w[i] + ada_b[i], 6, axis=-1)
        mc = jnp.split(cc_act @ ada_w[i] + ada_b[i], 6, axis=-1)
        h_x = modulate(rms_norm(x, norm1_g[i]), mx[0], mx[1])
        h_c = modulate(rms_norm(y_ctx, norm1_g[i]), mc[0], mc[1])
        j = i // N_MIXERS
        if i % N_MIXERS == 0:
            out_c, out_x = windowed_gqa_sink(h_x, h_c, attn_w_qkv[j], attn_q_norm[j], attn_k_norm[j],
                                             attn_sink[j], attn_w_o[j], need_ctx_out)
        else:
            out_c, out_x = bidir_retention(h_x, h_c, ret_w_qkvg[j], ret_decay_logit[j], ret_gn_g[j],
                                           ret_w_o[j], need_ctx_out)
        x = x + mx[2] * out_x
        x = x + mx[5] * swiglu(modulate(rms_norm(x, norm2_g[i]), mx[3], mx[4]), ffn_w_in[i], ffn_w_out[i])
        if need_ctx_out:
            y_ctx = y_ctx + mc[2] * out_c
            y_ctx = y_ctx + mc[5] * swiglu(modulate(rms_norm(y_ctx, norm2_g[i]), mc[3], mc[4]),
                                           ffn_w_in[i], ffn_w_out[i])
    return x


import jax as _jax
import jax.numpy as _jnp

TWIN_FORMAT = 'train_step'
FWD_PARAMS = ['x', 'c', 'ctx', 'c_ctx', 'ada_w', 'ada_b', 'norm1_g', 'norm2_g', 'ffn_w_in', 'ffn_w_out', 'attn_w_qkv', 'attn_q_norm', 'attn_k_norm', 'attn_sink', 'attn_w_o', 'ret_w_qkvg', 'ret_decay_logit', 'ret_gn_g', 'ret_w_o']
TWIN_WEIGHTS = ['c_ctx', 'ada_w', 'ada_b', 'norm1_g', 'norm2_g', 'ffn_w_in', 'ffn_w_out', 'attn_w_qkv', 'attn_q_norm', 'attn_k_norm', 'attn_sink', 'attn_w_o', 'ret_w_qkvg', 'ret_decay_logit', 'ret_gn_g', 'ret_w_o']
TWIN_DIFF_INPUT = 'x'
TWIN_INPUTS = ['x', 'c', 'ctx', 'c_ctx', 'ada_w', 'ada_b', 'norm1_g', 'norm2_g', 'ffn_w_in', 'ffn_w_out', 'attn_w_qkv', 'attn_q_norm', 'attn_k_norm', 'attn_sink', 'attn_w_o', 'ret_w_qkvg', 'ret_decay_logit', 'ret_gn_g', 'ret_w_o', 'loss_target', 'm_c_ctx', 'm_ada_w', 'm_ada_b', 'm_norm1_g', 'm_norm2_g', 'm_ffn_w_in', 'm_ffn_w_out', 'm_attn_w_qkv', 'm_attn_q_norm', 'm_attn_k_norm', 'm_attn_sink', 'm_attn_w_o', 'm_ret_w_qkvg', 'm_ret_decay_logit', 'm_ret_gn_g', 'm_ret_w_o', 'v_c_ctx', 'v_ada_w', 'v_ada_b', 'v_norm1_g', 'v_norm2_g', 'v_ffn_w_in', 'v_ffn_w_out', 'v_attn_w_qkv', 'v_attn_q_norm', 'v_attn_k_norm', 'v_attn_sink', 'v_attn_w_o', 'v_ret_w_qkvg', 'v_ret_decay_logit', 'v_ret_gn_g', 'v_ret_w_o']
TWIN_OUTPUTS = ['loss', 'grad_x', 'grad_c_ctx', 'grad_ada_w', 'grad_ada_b', 'grad_norm1_g', 'grad_norm2_g', 'grad_ffn_w_in', 'grad_ffn_w_out', 'grad_attn_w_qkv', 'grad_attn_q_norm', 'grad_attn_k_norm', 'grad_attn_sink', 'grad_attn_w_o', 'grad_ret_w_qkvg', 'grad_ret_decay_logit', 'grad_ret_gn_g', 'grad_ret_w_o', 'delta_c_ctx', 'delta_ada_w', 'delta_ada_b', 'delta_norm1_g', 'delta_norm2_g', 'delta_ffn_w_in', 'delta_ffn_w_out', 'delta_attn_w_qkv', 'delta_attn_q_norm', 'delta_attn_k_norm', 'delta_attn_sink', 'delta_attn_w_o', 'delta_ret_w_qkvg', 'delta_ret_decay_logit', 'delta_ret_gn_g', 'delta_ret_w_o', 'new_m_c_ctx', 'new_m_ada_w', 'new_m_ada_b', 'new_m_norm1_g', 'new_m_norm2_g', 'new_m_ffn_w_in', 'new_m_ffn_w_out', 'new_m_attn_w_qkv', 'new_m_attn_q_norm', 'new_m_attn_k_norm', 'new_m_attn_sink', 'new_m_attn_w_o', 'new_m_ret_w_qkvg', 'new_m_ret_decay_logit', 'new_m_ret_gn_g', 'new_m_ret_w_o', 'new_v_c_ctx', 'new_v_ada_w', 'new_v_ada_b', 'new_v_norm1_g', 'new_v_norm2_g', 'new_v_ffn_w_in', 'new_v_ffn_w_out', 'new_v_attn_w_qkv', 'new_v_attn_q_norm', 'new_v_attn_k_norm', 'new_v_attn_sink', 'new_v_attn_w_o', 'new_v_ret_w_qkvg', 'new_v_ret_decay_logit', 'new_v_ret_gn_g', 'new_v_ret_w_o']
TWIN_LEAF_KINDS = {'loss': 'loss', 'grad_x': 'grad_x', 'grad_c_ctx': 'grad_w', 'grad_ada_w': 'grad_w', 'grad_ada_b': 'grad_w', 'grad_norm1_g': 'grad_w', 'grad_norm2_g': 'grad_w', 'grad_ffn_w_in': 'grad_w', 'grad_ffn_w_out': 'grad_w', 'grad_attn_w_qkv': 'grad_w', 'grad_attn_q_norm': 'grad_w', 'grad_attn_k_norm': 'grad_w', 'grad_attn_sink': 'grad_w', 'grad_attn_w_o': 'grad_w', 'grad_ret_w_qkvg': 'grad_w', 'grad_ret_decay_logit': 'grad_w', 'grad_ret_gn_g': 'grad_w', 'grad_ret_w_o': 'grad_w', 'delta_c_ctx': 'delta_w', 'delta_ada_w': 'delta_w', 'delta_ada_b': 'delta_w', 'delta_norm1_g': 'delta_w', 'delta_norm2_g': 'delta_w', 'delta_ffn_w_in': 'delta_w', 'delta_ffn_w_out': 'delta_w', 'delta_attn_w_qkv': 'delta_w', 'delta_attn_q_norm': 'delta_w', 'delta_attn_k_norm': 'delta_w', 'delta_attn_sink': 'delta_w', 'delta_attn_w_o': 'delta_w', 'delta_ret_w_qkvg': 'delta_w', 'delta_ret_decay_logit': 'delta_w', 'delta_ret_gn_g': 'delta_w', 'delta_ret_w_o': 'delta_w', 'new_m_c_ctx': 'new_m', 'new_m_ada_w': 'new_m', 'new_m_ada_b': 'new_m', 'new_m_norm1_g': 'new_m', 'new_m_norm2_g': 'new_m', 'new_m_ffn_w_in': 'new_m', 'new_m_ffn_w_out': 'new_m', 'new_m_attn_w_qkv': 'new_m', 'new_m_attn_q_norm': 'new_m', 'new_m_attn_k_norm': 'new_m', 'new_m_attn_sink': 'new_m', 'new_m_attn_w_o': 'new_m', 'new_m_ret_w_qkvg': 'new_m', 'new_m_ret_decay_logit': 'new_m', 'new_m_ret_gn_g': 'new_m', 'new_m_ret_w_o': 'new_m', 'new_v_c_ctx': 'new_v', 'new_v_ada_w': 'new_v', 'new_v_ada_b': 'new_v', 'new_v_norm1_g': 'new_v', 'new_v_norm2_g': 'new_v', 'new_v_ffn_w_in': 'new_v', 'new_v_ffn_w_out': 'new_v', 'new_v_attn_w_qkv': 'new_v', 'new_v_attn_q_norm': 'new_v', 'new_v_attn_k_norm': 'new_v', 'new_v_attn_sink': 'new_v', 'new_v_attn_w_o': 'new_v', 'new_v_ret_w_qkvg': 'new_v', 'new_v_ret_decay_logit': 'new_v', 'new_v_ret_gn_g': 'new_v', 'new_v_ret_w_o': 'new_v'}


def _forward(args):
    return _fwd_reference(*[args[k] for k in FWD_PARAMS])


def _output_shape():
    out = _jax.eval_shape(lambda: _forward(_fwd_setup_inputs(0)))
    return out.shape, out.dtype

N_MICROBATCH = 1
ADAM_LR = 0.001
ADAM_B1 = 0.9
ADAM_B2 = 0.999
ADAM_EPS = 1e-08
ADAM_WD = 0.01
ADAM_STEP = 10
PER_EXAMPLE_BATCH_AXIS = {'x': 0, 'c': 0, 'ctx': 0, 'loss_target': 0}
SHARED_INPUTS = []
_WEIGHT_DTYPES = {'c_ctx': _jnp.float32, 'ada_w': _jnp.float32, 'ada_b': _jnp.float32, 'norm1_g': _jnp.float32, 'norm2_g': _jnp.float32, 'ffn_w_in': _jnp.float32, 'ffn_w_out': _jnp.float32, 'attn_w_qkv': _jnp.float32, 'attn_q_norm': _jnp.float32, 'attn_k_norm': _jnp.float32, 'attn_sink': _jnp.float32, 'attn_w_o': _jnp.float32, 'ret_w_qkvg': _jnp.float32, 'ret_decay_logit': _jnp.float32, 'ret_gn_g': _jnp.float32, 'ret_w_o': _jnp.float32}
MOMENT_SCALE = {'c_ctx': 3.966430e-01, 'ada_w': 1.744475e+00, 'ada_b': 3.840612e+00, 'norm1_g': 1.891511e+00, 'norm2_g': 6.836904e+00, 'ffn_w_in': 9.447452e-02, 'ffn_w_out': 1.193380e-01, 'attn_w_qkv': 4.053822e-01, 'attn_q_norm': 5.501094e-01, 'attn_k_norm': 5.659409e-01, 'attn_sink': 3.554206e-02, 'attn_w_o': 2.961446e-01, 'ret_w_qkvg': 1.086084e-01, 'ret_decay_logit': 4.460498e-01, 'ret_gn_g': 1.386766e+00, 'ret_w_o': 1.029975e-01}


def _to_microbatches(a, axis):
    t = _jnp.moveaxis(a, axis, 0)
    t = t.reshape((N_MICROBATCH, t.shape[0] // N_MICROBATCH) + t.shape[1:])
    return _jnp.moveaxis(t, 1, axis + 1)


def setup_inputs(seed: int = 0) -> dict:
    inp = _fwd_setup_inputs(seed)
    key = _jax.random.fold_in(_jax.random.key(seed), 7919)
    shape, _ = _output_shape()
    out = dict(inp)
    out["loss_target"] = _jax.random.normal(_jax.random.fold_in(key, 0), shape, _jnp.float32)
    for i, name in enumerate(TWIN_WEIGHTS):
        w = inp[name].astype(_jnp.float32)
        if MOMENT_SCALE is None:
            s = _jnp.sqrt(_jnp.mean(_jnp.square(w)) + 1e-30)
        else:
            s = MOMENT_SCALE[name]
        km, kv = _jax.random.split(_jax.random.fold_in(key, i + 1))
        out[name] = w
        out["m_" + name] = s * _jax.random.normal(km, w.shape, _jnp.float32)
        out["v_" + name] = (s * s) * _jax.random.uniform(kv, w.shape, _jnp.float32, 0.5, 1.5)
    if N_MICROBATCH > 1:
        for name, axis in PER_EXAMPLE_BATCH_AXIS.items():
            out[name] = _to_microbatches(out[name], axis)
    return {'x': out['x'], 'c': out['c'], 'ctx': out['ctx'], 'c_ctx': out['c_ctx'], 'ada_w': out['ada_w'], 'ada_b': out['ada_b'], 'norm1_g': out['norm1_g'], 'norm2_g': out['norm2_g'], 'ffn_w_in': out['ffn_w_in'], 'ffn_w_out': out['ffn_w_out'], 'attn_w_qkv': out['attn_w_qkv'], 'attn_q_norm': out['attn_q_norm'], 'attn_k_norm': out['attn_k_norm'], 'attn_sink': out['attn_sink'], 'attn_w_o': out['attn_w_o'], 'ret_w_qkvg': out['ret_w_qkvg'], 'ret_decay_logit': out['ret_decay_logit'], 'ret_gn_g': out['ret_gn_g'], 'ret_w_o': out['ret_w_o'], 'loss_target': out['loss_target'], 'm_c_ctx': out['m_c_ctx'], 'm_ada_w': out['m_ada_w'], 'm_ada_b': out['m_ada_b'], 'm_norm1_g': out['m_norm1_g'], 'm_norm2_g': out['m_norm2_g'], 'm_ffn_w_in': out['m_ffn_w_in'], 'm_ffn_w_out': out['m_ffn_w_out'], 'm_attn_w_qkv': out['m_attn_w_qkv'], 'm_attn_q_norm': out['m_attn_q_norm'], 'm_attn_k_norm': out['m_attn_k_norm'], 'm_attn_sink': out['m_attn_sink'], 'm_attn_w_o': out['m_attn_w_o'], 'm_ret_w_qkvg': out['m_ret_w_qkvg'], 'm_ret_decay_logit': out['m_ret_decay_logit'], 'm_ret_gn_g': out['m_ret_gn_g'], 'm_ret_w_o': out['m_ret_w_o'], 'v_c_ctx': out['v_c_ctx'], 'v_ada_w': out['v_ada_w'], 'v_ada_b': out['v_ada_b'], 'v_norm1_g': out['v_norm1_g'], 'v_norm2_g': out['v_norm2_g'], 'v_ffn_w_in': out['v_ffn_w_in'], 'v_ffn_w_out': out['v_ffn_w_out'], 'v_attn_w_qkv': out['v_attn_w_qkv'], 'v_attn_q_norm': out['v_attn_q_norm'], 'v_attn_k_norm': out['v_attn_k_norm'], 'v_attn_sink': out['v_attn_sink'], 'v_attn_w_o': out['v_attn_w_o'], 'v_ret_w_qkvg': out['v_ret_w_qkvg'], 'v_ret_decay_logit': out['v_ret_decay_logit'], 'v_ret_gn_g': out['v_ret_gn_g'], 'v_ret_w_o': out['v_ret_w_o']}


def _loss(weights, diff, rest, loss_target):
    with _jax.named_scope("forward"):
        args = {**rest, TWIN_DIFF_INPUT: diff, **{k: w.astype(_WEIGHT_DTYPES[k]) for k, w in weights.items()}}
        y = _forward(args)
    with _jax.named_scope("loss_head"):
        err = _jnp.square(y.astype(_jnp.float32) - loss_target)
        return 0.5 * _jnp.sum(_jnp.mean(err, axis=-1)) if err.ndim else 0.5 * err


def _adamw(w, g, m, v):
    m = ADAM_B1 * m + (1.0 - ADAM_B1) * g
    v = ADAM_B2 * v + (1.0 - ADAM_B2) * _jnp.square(g)
    m_hat = m / (1.0 - ADAM_B1 ** ADAM_STEP)
    v_hat = v / (1.0 - ADAM_B2 ** ADAM_STEP)
    delta = -ADAM_LR * (m_hat / (_jnp.sqrt(v_hat) + ADAM_EPS) + ADAM_WD * w)
    return delta, m, v


def reference(x, c, ctx, c_ctx, ada_w, ada_b, norm1_g, norm2_g, ffn_w_in, ffn_w_out, attn_w_qkv, attn_q_norm, attn_k_norm, attn_sink, attn_w_o, ret_w_qkvg, ret_decay_logit, ret_gn_g, ret_w_o, loss_target, m_c_ctx, m_ada_w, m_ada_b, m_norm1_g, m_norm2_g, m_ffn_w_in, m_ffn_w_out, m_attn_w_qkv, m_attn_q_norm, m_attn_k_norm, m_attn_sink, m_attn_w_o, m_ret_w_qkvg, m_ret_decay_logit, m_ret_gn_g, m_ret_w_o, v_c_ctx, v_ada_w, v_ada_b, v_norm1_g, v_norm2_g, v_ffn_w_in, v_ffn_w_out, v_attn_w_qkv, v_attn_q_norm, v_attn_k_norm, v_attn_sink, v_attn_w_o, v_ret_w_qkvg, v_ret_decay_logit, v_ret_gn_g, v_ret_w_o):
    given = dict(x=x, c=c, ctx=ctx, c_ctx=c_ctx, ada_w=ada_w, ada_b=ada_b, norm1_g=norm1_g, norm2_g=norm2_g, ffn_w_in=ffn_w_in, ffn_w_out=ffn_w_out, attn_w_qkv=attn_w_qkv, attn_q_norm=attn_q_norm, attn_k_norm=attn_k_norm, attn_sink=attn_sink, attn_w_o=attn_w_o, ret_w_qkvg=ret_w_qkvg, ret_decay_logit=ret_decay_logit, ret_gn_g=ret_gn_g, ret_w_o=ret_w_o, loss_target=loss_target, m_c_ctx=m_c_ctx, m_ada_w=m_ada_w, m_ada_b=m_ada_b, m_norm1_g=m_norm1_g, m_norm2_g=m_norm2_g, m_ffn_w_in=m_ffn_w_in, m_ffn_w_out=m_ffn_w_out, m_attn_w_qkv=m_attn_w_qkv, m_attn_q_norm=m_attn_q_norm, m_attn_k_norm=m_attn_k_norm, m_attn_sink=m_attn_sink, m_attn_w_o=m_attn_w_o, m_ret_w_qkvg=m_ret_w_qkvg, m_ret_decay_logit=m_ret_decay_logit, m_ret_gn_g=m_ret_gn_g, m_ret_w_o=m_ret_w_o, v_c_ctx=v_c_ctx, v_ada_w=v_ada_w, v_ada_b=v_ada_b, v_norm1_g=v_norm1_g, v_norm2_g=v_norm2_g, v_ffn_w_in=v_ffn_w_in, v_ffn_w_out=v_ffn_w_out, v_attn_w_qkv=v_attn_w_qkv, v_attn_q_norm=v_attn_q_norm, v_attn_k_norm=v_attn_k_norm, v_attn_sink=v_attn_sink, v_attn_w_o=v_attn_w_o, v_ret_w_qkvg=v_ret_w_qkvg, v_ret_decay_logit=v_ret_decay_logit, v_ret_gn_g=v_ret_gn_g, v_ret_w_o=v_ret_w_o)
    weights = {n: given[n] for n in TWIN_WEIGHTS}
    shared = {n: given[n] for n in SHARED_INPUTS}
    per_example = {n: given[n] for n in ['x', 'c', 'ctx']}
    grad_fn = _jax.value_and_grad(_loss, argnums=(0, 1))

    def one_microbatch(ex, loss_target):
        ex = dict(ex)
        diff = ex.pop(TWIN_DIFF_INPUT)
        return grad_fn(weights, diff, {**shared, **ex}, loss_target)

    if N_MICROBATCH == 1:
        loss, (grad_w, grad_x) = one_microbatch(per_example, given["loss_target"])
    else:
        def body(carry, xs):
            loss_sum, grad_sum = carry
            l_k, (gw_k, gx_k) = one_microbatch(xs[0], xs[1])
            with _jax.named_scope("update"):
                return (loss_sum + l_k, _jax.tree.map(_jnp.add, grad_sum, gw_k)), gx_k

        init = (_jnp.zeros((), _jnp.float32), _jax.tree.map(_jnp.zeros_like, weights))
        (loss, grad_w), grad_x = _jax.lax.scan(body, init, (per_example, given["loss_target"]))
    with _jax.named_scope("update"):
        delta_w, new_m, new_v = {}, {}, {}
        for n in TWIN_WEIGHTS:
            delta_w[n], new_m[n], new_v[n] = _adamw(weights[n], grad_w[n], given["m_" + n], given["v_" + n])
    return (loss, grad_x, *[grad_w[n] for n in TWIN_WEIGHTS], *[delta_w[n] for n in TWIN_WEIGHTS],
            *[new_m[n] for n in TWIN_WEIGHTS], *[new_v[n] for n in TWIN_WEIGHTS])
```

```python
import functools

import jax
import jax.numpy as jnp
from jax import lax
from jax.experimental import pallas as pl
from jax.experimental.pallas import tpu as pltpu

F32 = jnp.float32
BF16 = jnp.bfloat16

D_MODEL = 1024
N_HEADS = 16
N_KV_HEADS = 4
HEAD_DIM = 64
WINDOW = 128
ATTN_BLOCK = 128
BAND = ATTN_BLOCK + 2 * WINDOW
RET_HEADS = 4
RET_QK_DIM = 256
RET_V_DIM = 512
RET_VWIDTH = 2048
RET_CHUNK = 128
D_FF = 2816
GRID_W = 64
ROPE_BASE = 10000.0
EPS = 1e-6
NEG_INF = -1e30
LANES = 128

ADAM_LR = 0.001
ADAM_B1 = 0.9
ADAM_B2 = 0.999
ADAM_EPS = 1e-08
ADAM_WD = 0.01
ADAM_STEP = 10

VMEM_LIMIT_BYTES = 56 * 1024 * 1024
MESH = pl.DeviceIdType.MESH
N_CHIPS = 4


def _cparams(*sem):
    return pltpu.CompilerParams(dimension_semantics=sem, vmem_limit_bytes=VMEM_LIMIT_BYTES)


_DIMS = {"nn": ((1,), (0,)), "nt": ((1,), (1,)), "tn": ((0,), (0,))}


def _dot(a, b, form):
    return lax.dot_general(a.astype(BF16), b.astype(BF16), (_DIMS[form], ((), ())), preferred_element_type=F32)


@functools.partial(jax.custom_vjp, nondiff_argnums=(2,))
def _mm(a, b, form):
    return _dot(a, b, form)


def _mm_fwd(a, b, form):
    return _dot(a, b, form), (a, b)


def _mm_bwd(form, res, ct):
    a, b = res
    if form == "nn":
        da, db = _dot(ct, b, "nt"), _dot(a, ct, "tn")
    elif form == "nt":
        da, db = _dot(ct, b, "nn"), _dot(ct, a, "tn")
    else:
        da, db = _dot(b, ct, "nt"), _dot(a, ct, "nn")
    return da.astype(a.dtype), db.astype(b.dtype)


_mm.defvjp(_mm_fwd, _mm_bwd)


def _swap_halves(x, half):
    w = x.shape[-1]
    lane = lax.broadcasted_iota(jnp.int32, x.shape, x.ndim - 1)
    return jnp.where(lane % (2 * half) < half, pltpu.roll(x, w - half, x.ndim - 1), pltpu.roll(x, half, x.ndim - 1))


@functools.partial(jax.custom_vjp, nondiff_argnums=(1,))
def _rot(x, half):
    return _swap_halves(x, half)


def _rot_fwd(x, half):
    return _swap_halves(x, half), None


def _rot_bwd(half, _, ct):
    return (_swap_halves(ct, half),)


_rot.defvjp(_rot_fwd, _rot_bwd)


def _rope(x, cos, sin_signed, half):
    return x * cos + _rot(x, half) * sin_signed


def _head_mean_square(x):
    r = lax.broadcasted_iota(jnp.int32, (LANES, LANES), 0) // HEAD_DIM
    c = lax.broadcasted_iota(jnp.int32, (LANES, LANES), 1) // HEAD_DIM
    g = jnp.where(r == c, 1.0 / HEAD_DIM, 0.0).astype(F32)
    return jnp.dot(x * x, g, precision=lax.Precision.HIGHEST, preferred_element_type=F32)


def _qk_chunk(x, gain, cos, sin_signed, scale):
    y = x * lax.rsqrt(_head_mean_square(x) + EPS) * gain
    return _rope(y, cos, sin_signed, HEAD_DIM // 4) * scale


def _sigmoid(x):
    return 1.0 / (1.0 + jnp.exp(-x))


def _silu(x):
    return x * _sigmoid(x)


def _mm_nn(a, w, out_dtype, name, tm, tn, tk, bias=None):
    m, k_dim = a.shape
    if w.ndim == 3:
        n = w.shape[0] * w.shape[2]
        per = w.shape[2] // tn
        assert w.shape[2] % tn == 0
        w_spec = pl.BlockSpec((None, tk, tn), lambda i, j, k: (j // per, k, j % per))
    else:
        n = w.shape[1]
        w_spec = pl.BlockSpec((tk, tn), lambda i, j, k: (k, j))
    assert m % tm == 0 and n % tn == 0 and k_dim % tk == 0, (name, a.shape, w.shape, tm, tn, tk)
    nk = k_dim // tk
    has_bias = bias is not None

    def body(*refs):
        a_ref, w_ref = refs[0], refs[1]
        b_ref = refs[2] if has_bias else None
        o_ref, acc_ref = refs[-2], refs[-1]
        k = pl.program_id(2)

        @pl.when(k == 0)
        def _():
            acc_ref[...] = jnp.zeros_like(acc_ref)

        acc_ref[...] += jnp.dot(a_ref[...].astype(BF16), w_ref[...], preferred_element_type=F32)

        @pl.when(k == nk - 1)
        def _():
            r = acc_ref[...]
            if has_bias:
                r = r + b_ref[...]
            o_ref[...] = r.astype(out_dtype)

    in_specs = [pl.BlockSpec((tm, tk), lambda i, j, k: (i, k)), w_spec]
    args = [a, w]
    if has_bias:
        in_specs.append(pl.BlockSpec((1, tn), lambda i, j, k: (0, j)))
        args.append(bias)
    return pl.pallas_call(
        body, name=name, grid=(m // tm, n // tn, nk), in_specs=in_specs,
        out_specs=pl.BlockSpec((tm, tn), lambda i, j, k: (i, j)),
        out_shape=jax.ShapeDtypeStruct((m, n), out_dtype),
        scratch_shapes=[pltpu.VMEM((tm, tn), F32)],
        compiler_params=_cparams("parallel", "parallel", "arbitrary"),
    )(*args)


def _mm_nt(a, w, out_dtype, name, tm, tn, tk):
    m, c_dim = a.shape
    if w.ndim == 3:
        k_out = w.shape[1]
        per = w.shape[2] // tk
        assert w.shape[2] % tk == 0 and w.shape[0] * w.shape[2] == c_dim
        w_spec = pl.BlockSpec((None, tn, tk), lambda i, j, k: (k // per, j, k % per))
    else:
        k_out = w.shape[0]
        assert w.shape[1] == c_dim
        w_spec = pl.BlockSpec((tn, tk), lambda i, j, k: (j, k))
    assert m % tm == 0 and k_out % tn == 0 and c_dim % tk == 0, (name, a.shape, w.shape, tm, tn, tk)
    nk = c_dim // tk

    def body(a_ref, w_ref, o_ref, acc_ref):
        k = pl.program_id(2)

        @pl.when(k == 0)
        def _():
            acc_ref[...] = jnp.zeros_like(acc_ref)

        acc_ref[...] += _dot(a_ref[...], w_ref[...], "nt")

        @pl.when(k == nk - 1)
        def _():
            o_ref[...] = acc_ref[...].astype(out_dtype)

    return pl.pallas_call(
        body, name=name, grid=(m // tm, k_out // tn, nk),
        in_specs=[pl.BlockSpec((tm, tk), lambda i, j, k: (i, k)), w_spec],
        out_specs=pl.BlockSpec((tm, tn), lambda i, j, k: (i, j)),
        out_shape=jax.ShapeDtypeStruct((m, k_out), out_dtype),
        scratch_shapes=[pltpu.VMEM((tm, tn), F32)],
        compiler_params=_cparams("parallel", "parallel", "arbitrary"),
    )(a, w)


def _mm_tn(a, b, name, tm, tn, tk, shards=None):
    r, k_dim = a.shape
    n = b.shape[1]
    assert r % tk == 0 and k_dim % tm == 0 and n % tn == 0, (name, a.shape, b.shape, tm, tn, tk)
    nk = r // tk
    if shards:
        per = n // shards // tn
        assert n % (shards * tn) == 0
        out_shape = jax.ShapeDtypeStruct((shards, k_dim, n // shards), F32)
        out_spec = pl.BlockSpec((None, tm, tn), lambda i, j, k: (j // per, i, j % per))
    else:
        out_shape = jax.ShapeDtypeStruct((k_dim, n), F32)
        out_spec = pl.BlockSpec((tm, tn), lambda i, j, k: (i, j))

    def body(a_ref, b_ref, o_ref):
        k = pl.program_id(2)

        @pl.when(k == 0)
        def _():
            o_ref[...] = jnp.zeros_like(o_ref)

        o_ref[...] += _dot(a_ref[...], b_ref[...], "tn")

    return pl.pallas_call(
        body, name=name, grid=(k_dim // tm, n // tn, nk),
        in_specs=[pl.BlockSpec((tk, tm), lambda i, j, k: (k, i)), pl.BlockSpec((tk, tn), lambda i, j, k: (k, j))],
        out_specs=out_spec, out_shape=out_shape,
        compiler_params=_cparams("parallel", "parallel", "arbitrary"),
    )(a, b)


class _Rows:
    def __init__(self, b, s, l):
        self.b, self.s, self.l = b, s, l
        self.seg = s + l
        self.r = b * self.seg


def _rowwise(name, body, geo, tm, ins, outs):
    seg_blocks, x_blocks = geo.seg // tm, geo.s // tm
    assert geo.seg % tm == 0 and geo.s % tm == 0
    nb = geo.b

    def is_ctx(i):
        return i % seg_blocks >= x_blocks

    in_specs, args = [], []
    for arr, kind in ins:
        args.append(arr)
        if kind == "row":
            in_specs.append(pl.BlockSpec((tm, arr.shape[1]), lambda i: (i, 0)))
        elif kind == "ex":
            in_specs.append(pl.BlockSpec((None, 1, arr.shape[2]), lambda i: (jnp.where(is_ctx(i), nb, i // seg_blocks), 0, 0)))
        elif kind == "full":
            in_specs.append(pl.BlockSpec(arr.shape, lambda i, nd=arr.ndim: (0,) * nd))
        elif kind == "tab":
            in_specs.append(pl.BlockSpec((tm, arr.shape[1]), lambda i: (i % seg_blocks, 0)))
        elif kind == "xrow":
            in_specs.append(pl.BlockSpec(
                (tm, arr.shape[1]), lambda i: ((i // seg_blocks) * x_blocks + jnp.minimum(i % seg_blocks, x_blocks - 1), 0)))
        else:
            _, width, cb = kind
            in_specs.append(pl.BlockSpec((tm, width), lambda i, cb=cb: (i, cb)))
    out_specs, out_shapes = [], []
    for o in outs:
        if o[0] == "row":
            out_specs.append(pl.BlockSpec((tm, o[1]), lambda i: (i, 0)))
            out_shapes.append(jax.ShapeDtypeStruct((geo.r, o[1]), o[2]))
        elif o[0] == "exacc":
            out_specs.append(pl.BlockSpec((None, 1, o[1]), lambda i: (jnp.where(is_ctx(i), nb, 0) + i // seg_blocks, 0, 0)))
            out_shapes.append(jax.ShapeDtypeStruct((2 * nb, 1, o[1]), F32))
        else:
            out_specs.append(pl.BlockSpec((o[1], o[2]), lambda i: (0, 0)))
            out_shapes.append(jax.ShapeDtypeStruct((o[1], o[2]), F32))
    n_in = len(ins)

    def kern(*refs):
        i = pl.program_id(0)
        res = body(i, *[r[...] for r in refs[:n_in]])
        if not isinstance(res, (tuple, list)):
            res = (res,)
        jj = i % seg_blocks
        first_of_part = (jj == 0) | (jj == x_blocks)
        for o, ref, val in zip(outs, refs[n_in:], res):
            if o[0] == "row":
                ref[...] = val.astype(ref.dtype)
            else:
                first = first_of_part if o[0] == "exacc" else i == 0

                @pl.when(first)
                def _(ref=ref, val=val):
                    ref[...] = val

                @pl.when(jnp.logical_not(first))
                def _(ref=ref, val=val):
                    ref[...] += val

    res = pl.pallas_call(
        kern, name=name, grid=(geo.r // tm,), in_specs=in_specs, out_specs=out_specs, out_shape=out_shapes,
        compiler_params=_cparams("arbitrary"),
    )(*args)
    return res[0] if len(res) == 1 else res


def _colsum(v):
    return jnp.sum(v, axis=0, keepdims=True)


def _norm_mod(geo, z, gain, mod, off, name):
    d = D_MODEL

    def body(i, zv, g, m):
        r = lax.rsqrt(jnp.mean(zv * zv, axis=-1, keepdims=True) + EPS)
        return (zv * r) * g * (1.0 + m[:, off + d:off + 2 * d]) + m[:, off:off + d]

    return _rowwise(name, body, geo, 256, [(z, "row"), (gain, "full"), (mod, "ex")], [("row", d, BF16)])


def _norm_mod_bwd(geo, z, gain, mod, off, dh, dz_skip, name):
    d = D_MODEL

    def body(i, zv, g, m, dhv, skip):
        r = lax.rsqrt(jnp.mean(zv * zv, axis=-1, keepdims=True) + EPS)
        n = zv * r
        dng = dhv * (1.0 + m[:, off + d:off + 2 * d])
        dn = dng * g
        dz = r * (dn - n * jnp.mean(dn * n, axis=-1, keepdims=True)) + skip
        return dz, _colsum(dhv), _colsum(dhv * (n * g)), _colsum(dng * n)

    return _rowwise(name, body, geo, 256, [(z, "row"), (gain, "full"), (mod, "ex"), (dh, "row"), (dz_skip, "row")],
                    [("row", d, F32), ("exacc", d), ("exacc", d), ("gacc", 1, d)])


def _gate_residual(geo, z, out, mod, off, name):
    d = D_MODEL

    def body(i, zv, ov, m):
        return zv + m[:, off:off + d] * ov

    return _rowwise(name, body, geo, 256, [(z, "row"), (out, "row"), (mod, "ex")], [("row", d, F32)])


def _gate_residual_bwd(geo, dz, out, mod, off, name):
    d = D_MODEL

    def body(i, dzv, ov, m):
        return dzv * m[:, off:off + d], _colsum(dzv * ov)

    return _rowwise(name, body, geo, 256, [(dz, "row"), (out, "row"), (mod, "ex")], [("row", d, BF16), ("exacc", d)])


def _swiglu(geo, u, name):
    def body(i, uv):
        return _silu(uv[:, :D_FF]) * uv[:, D_FF:]

    return _rowwise(name, body, geo, 128, [(u, "row")], [("row", D_FF, BF16)])


def _swiglu_bwd(geo, u, da, name):
    def body(i, uv, dav):
        g, up = uv[:, :D_FF], uv[:, D_FF:]
        s = _sigmoid(g)
        return jnp.concatenate([dav * up * (s * (1.0 + g * (1.0 - s))), dav * (g * s)], axis=1)

    return _rowwise(name, body, geo, 128, [(u, "row"), (da, "row")], [("row", 2 * D_FF, BF16)])


def _loss_head(geo, z, target, name):
    seg_blocks, x_blocks = geo.seg // 256, geo.s // 256

    def body(i, zv, tv):
        keep = jnp.where(i % seg_blocks >= x_blocks, 0.0, 1.0)
        err = (zv - tv) * keep
        part = 0.5 * jnp.sum(jnp.mean(err * err, axis=-1, keepdims=True), axis=0, keepdims=True)
        return err * (1.0 / D_MODEL), jnp.broadcast_to(part, (1, LANES))

    return _rowwise(name, body, geo, 256, [(z, "row"), (target, "xrow")], [("row", D_MODEL, F32), ("gacc", 1, LANES)])


Q_SCALE = HEAD_DIM ** -0.5
N_QK_CHUNKS = (N_HEADS + N_KV_HEADS) * HEAD_DIM // LANES
N_Q_CHUNKS = N_HEADS * HEAD_DIM // LANES


def _attn_prep(geo, proj, cos, sin_signed, q_gain, k_gain, name):
    def body(i, p, cs, sn, qg, kg):
        outs = []
        for ch in range(N_QK_CHUNKS):
            is_q = ch < N_Q_CHUNKS
            outs.append(_qk_chunk(p[:, ch * LANES:(ch + 1) * LANES], qg if is_q else kg, cs, sn, Q_SCALE if is_q else 1.0))
        outs.append(p[:, N_QK_CHUNKS * LANES:])
        return jnp.concatenate(outs, axis=1)

    return _rowwise(name, body, geo, 256, [(proj, "row"), (cos, "tab"), (sin_signed, "tab"), (q_gain, "full"), (k_gain, "full")],
                    [("row", proj.shape[1], BF16)])


def _attn_prep_bwd(geo, proj, cos, sin_signed, q_gain, k_gain, dq, dkv, name):
    kw = N_KV_HEADS * HEAD_DIM

    def body(i, p, cs, sn, qg, kg, dqv, dkvv):
        outs = []
        dgains = [jnp.zeros((1, LANES), F32), jnp.zeros((1, LANES), F32)]
        for ch in range(N_QK_CHUNKS):
            is_q = ch < N_Q_CHUNKS
            scale = Q_SCALE if is_q else 1.0
            ct = dqv[:, ch * LANES:(ch + 1) * LANES] if is_q else dkvv[:, (ch - N_Q_CHUNKS) * LANES:(ch - N_Q_CHUNKS + 1) * LANES]
            _, vjp = jax.vjp(lambda xx, gg, scale=scale: _qk_chunk(xx, gg, cs, sn, scale),
                             p[:, ch * LANES:(ch + 1) * LANES], qg if is_q else kg)
            dx, dg = vjp(ct)
            outs.append(dx)
            dgains[0 if is_q else 1] = dgains[0 if is_q else 1] + dg
        outs.append(dkvv[:, kw:])
        return jnp.concatenate(outs, axis=1), dgains[0], dgains[1]

    return _rowwise(name, body, geo, 256,
                    [(proj, "row"), (cos, "tab"), (sin_signed, "tab"), (q_gain, "full"), (k_gain, "full"), (dq, "row"), (dkv, "row")],
                    [("row", proj.shape[1], BF16), ("gacc", 1, LANES), ("gacc", 1, LANES)])


def _attn_geometry(geo):
    assert geo.s % ATTN_BLOCK == 0 and geo.l % ATTN_BLOCK == 0 and geo.seg >= BAND
    return geo.seg // ATTN_BLOCK, geo.s // ATTN_BLOCK


def _attn_mask(j, s0, geo, n_x_blocks):
    r = lax.broadcasted_iota(jnp.int32, (ATTN_BLOCK, BAND), 0)
    n = lax.broadcasted_iota(jnp.int32, (ATTN_BLOCK, BAND), 1)
    dist = (s0 - j * ATTN_BLOCK) + n - r
    return (jnp.abs(dist) <= WINDOW) & (s0 + n < geo.s) & (j < n_x_blocks)


def _attn_probs(q, kc, kb, valid, sink):
    s_ctx = _dot(q, kc, "nt")
    s_loc = jnp.where(valid, _dot(q, kb, "nt"), NEG_INF)
    m = jnp.maximum(jnp.maximum(jnp.max(s_ctx, axis=-1, keepdims=True), jnp.max(s_loc, axis=-1, keepdims=True)), sink)
    e_ctx, e_loc, e_sink = jnp.exp(s_ctx - m), jnp.exp(s_loc - m), jnp.exp(sink - m)
    inv = 1.0 / (jnp.sum(e_ctx, axis=-1, keepdims=True) + jnp.sum(e_loc, axis=-1, keepdims=True) + e_sink)
    return e_ctx * inv, e_loc * inv, e_sink * inv


def _attention(geo, qkv, sink, name):
    n_blocks, n_x_blocks = _attn_geometry(geo)
    qw, kw = N_HEADS * HEAD_DIM, N_KV_HEADS * HEAD_DIM
    group = N_HEADS // N_KV_HEADS

    def kern(sink_ref, q_ref, k_ref, v_ref, o_ref):
        j = pl.program_id(1)
        s0 = pl.multiple_of(jnp.clip((j - 1) * ATTN_BLOCK, 0, geo.seg - BAND), ATTN_BLOCK)
        valid = _attn_mask(j, s0, geo, n_x_blocks)
        kb_all, vb_all = k_ref[pl.ds(s0, BAND), :], v_ref[pl.ds(s0, BAND), :]
        kc_all, vc_all = k_ref[geo.s:geo.seg, :], v_ref[geo.s:geo.seg, :]
        for h in range(N_HEADS):
            kv = slice((h // group) * HEAD_DIM, (h // group + 1) * HEAD_DIM)
            q = q_ref[:, h * HEAD_DIM:(h + 1) * HEAD_DIM]
            p_ctx, p_loc, _ = _attn_probs(q, kc_all[:, kv], kb_all[:, kv], valid, sink_ref[h])
            o = _dot(p_ctx, vc_all[:, kv], "nn") + _dot(p_loc, vb_all[:, kv], "nn")
            o_ref[:, h * HEAD_DIM:(h + 1) * HEAD_DIM] = o.astype(BF16)

    return pl.pallas_call(
        kern, name=name, grid=(geo.b, n_blocks),
        in_specs=[pl.BlockSpec(memory_space=pltpu.SMEM),
                  pl.BlockSpec((ATTN_BLOCK, qw), lambda b, j: (b * n_blocks + j, 0)),
                  pl.BlockSpec((geo.seg, kw), lambda b, j: (b, qw // kw)),
                  pl.BlockSpec((geo.seg, kw), lambda b, j: (b, qw // kw + 1))],
        out_specs=pl.BlockSpec((ATTN_BLOCK, qw), lambda b, j: (b * n_blocks + j, 0)),
        out_shape=jax.ShapeDtypeStruct((geo.r, qw), BF16),
        compiler_params=_cparams("parallel", "arbitrary"),
    )(sink, qkv, qkv, qkv)


def _attention_bwd(geo, qkv, sink, do, name):
    n_blocks, n_x_blocks = _attn_geometry(geo)
    qw, kw = N_HEADS * HEAD_DIM, N_KV_HEADS * HEAD_DIM
    group = N_HEADS // N_KV_HEADS

    def kern(sink_ref, q_ref, k_ref, v_ref, do_ref, dq_ref, dkv_ref, dsink_ref):
        b, j = pl.program_id(0), pl.program_id(1)
        s0 = pl.multiple_of(jnp.clip((j - 1) * ATTN_BLOCK, 0, geo.seg - BAND), ATTN_BLOCK)
        valid = _attn_mask(j, s0, geo, n_x_blocks)

        @pl.when(j == 0)
        def _():
            dkv_ref[...] = jnp.zeros_like(dkv_ref)

        @pl.when((j == 0) & (b == 0))
        def _():
            dsink_ref[...] = jnp.zeros_like(dsink_ref)

        kb_all, vb_all = k_ref[pl.ds(s0, BAND), :], v_ref[pl.ds(s0, BAND), :]
        kc_all, vc_all = k_ref[geo.s:geo.seg, :], v_ref[geo.s:geo.seg, :]
        for g in range(N_KV_HEADS):
            kv = slice(g * HEAD_DIM, (g + 1) * HEAD_DIM)
            kc, kb, vc, vb = kc_all[:, kv], kb_all[:, kv], vc_all[:, kv], vb_all[:, kv]
            dkc = jnp.zeros((geo.l, HEAD_DIM), F32)
            dvc = jnp.zeros((geo.l, HEAD_DIM), F32)
            dkb = jnp.zeros((BAND, HEAD_DIM), F32)
            dvb = jnp.zeros((BAND, HEAD_DIM), F32)
            for h in range(g * group, (g + 1) * group):
                hs = slice(h * HEAD_DIM, (h + 1) * HEAD_DIM)
                q, dout = q_ref[:, hs], do_ref[:, hs]
                p_ctx, p_loc, p_sink = _attn_probs(q, kc, kb, valid, sink_ref[h])
                dp_ctx, dp_loc = _dot(dout, vc, "nt"), _dot(dout, vb, "nt")
                dsum = jnp.sum(p_ctx * dp_ctx, axis=-1, keepdims=True) + jnp.sum(p_loc * dp_loc, axis=-1, keepdims=True)
                ds_ctx, ds_loc = p_ctx * (dp_ctx - dsum), p_loc * (dp_loc - dsum)
                dq_ref[:, hs] = _dot(ds_ctx, kc, "nn") + _dot(ds_loc, kb, "nn")
                dkc += _dot(ds_ctx, q, "tn")
                dkb += _dot(ds_loc, q, "tn")
                dvc += _dot(p_ctx, dout, "tn")
                dvb += _dot(p_loc, dout, "tn")
                dsink_ref[h:h + 1, :] += jnp.broadcast_to(-jnp.sum(p_sink * dsum, axis=0, keepdims=True), (1, LANES))
            vv = slice(kw + g * HEAD_DIM, kw + (g + 1) * HEAD_DIM)
            dkv_ref[pl.ds(s0, BAND), kv] += dkb
            dkv_ref[pl.ds(s0, BAND), vv] += dvb
            dkv_ref[geo.s:geo.seg, kv] += dkc
            dkv_ref[geo.s:geo.seg, vv] += dvc

    return pl.pallas_call(
        kern, name=name, grid=(geo.b, n_blocks),
        in_specs=[pl.BlockSpec(memory_space=pltpu.SMEM),
                  pl.BlockSpec((ATTN_BLOCK, qw), lambda b, j: (b * n_blocks + j, 0)),
                  pl.BlockSpec((geo.seg, kw), lambda b, j: (b, qw // kw)),
                  pl.BlockSpec((geo.seg, kw), lambda b, j: (b, qw // kw + 1)),
                  pl.BlockSpec((ATTN_BLOCK, qw), lambda b, j: (b * n_blocks + j, 0))],
        out_specs=[pl.BlockSpec((ATTN_BLOCK, qw), lambda b, j: (b * n_blocks + j, 0)),
                   pl.BlockSpec((geo.seg, 2 * kw), lambda b, j: (b, 0)),
                   pl.BlockSpec((N_HEADS, LANES), lambda b, j: (0, 0))],
        out_shape=[jax.ShapeDtypeStruct((geo.r, qw), F32), jax.ShapeDtypeStruct((geo.r, 2 * kw), F32),
                   jax.ShapeDtypeStruct((N_HEADS, LANES), F32)],
        compiler_params=_cparams("arbitrary", "arbitrary"),
    )(sink, qkv, qkv, qkv, do)


RET_QK_W = RET_HEADS * RET_QK_DIM
K_SCALE = RET_QK_DIM ** -0.5


def _ret_prep(geo, proj, cos, sin_signed, name):
    def body(i, p, cs, sn):
        cs2, sn2 = jnp.concatenate([cs] * RET_HEADS, axis=1), jnp.concatenate([sn] * RET_HEADS, axis=1)
        q = _rope(p[:, :RET_QK_W], cs2, sn2, RET_QK_DIM // 4)
        k = _rope(p[:, RET_QK_W:2 * RET_QK_W], cs2, sn2, RET_QK_DIM // 4) * K_SCALE
        return jnp.concatenate([q, k, p[:, 2 * RET_QK_W:]], axis=1)

    return _rowwise(name, body, geo, 128, [(proj, ("rowc", 2 * RET_QK_W + RET_VWIDTH, 0)), (cos, "tab"), (sin_signed, "tab")],
                    [("row", 2 * RET_QK_W + RET_VWIDTH, BF16)])


def _ret_prep_bwd(geo, dq, dk, dv, dgate, cos, sin_signed, name):
    def body(i, dqv, dkv, dvv, dg, cs, sn):
        cs2, sn2 = jnp.concatenate([cs] * RET_HEADS, axis=1), jnp.concatenate([sn] * RET_HEADS, axis=1)
        dkv = dkv * K_SCALE
        dqv = dqv * cs2 + _swap_halves(dqv * sn2, RET_QK_DIM // 4)
        dkv = dkv * cs2 + _swap_halves(dkv * sn2, RET_QK_DIM // 4)
        return jnp.concatenate([dqv, dkv, dvv, dg], axis=1)

    return _rowwise(name, body, geo, 128,
                    [(dq, "row"), (dk, "row"), (dv, "row"), (dgate, "row"), (cos, "tab"), (sin_signed, "tab")],
                    [("row", 2 * RET_QK_W + 2 * RET_VWIDTH, BF16)])


def _ret_step(state, q, k, v, lg, rev):
    c = RET_CHUNK
    ri = lax.broadcasted_iota(jnp.int32, (c, 1), 0).astype(F32)
    cj = lax.broadcasted_iota(jnp.int32, (1, c), 1).astype(F32)
    if rev:
        dist, q_decay, k_decay = cj - ri, jnp.exp(lg * (c - ri)), jnp.exp(lg * ri)
    else:
        dist, q_decay, k_decay = ri - cj, jnp.exp(lg * (ri + 1.0)), jnp.exp(lg * (c - 1.0 - ri))
    intra = jnp.where(dist >= 0, jnp.exp(lg * jnp.maximum(dist, 0.0)), 0.0)
    scores = _mm(q, k, "nt") * intra
    out = _mm(scores, v, "nn") + _mm(q, state, "nn") * q_decay
    new_state = state * jnp.exp(lg * c) + _mm(k * k_decay, v, "tn")
    return new_state, out


def _ret_state0(kc, vc, lg, rev):
    n = kc.shape[0]
    t = lax.broadcasted_iota(jnp.int32, (n, 1), 0).astype(F32)
    decay = jnp.exp(lg * t) if rev else jnp.exp(lg * (n - 1.0 - t))
    return _mm(kc * decay, vc, "tn")


def _ret_specs(geo):
    nq = RET_HEADS
    return [pl.BlockSpec((2 * RET_HEADS, LANES), lambda b, h: (0, 0)),
            pl.BlockSpec((geo.seg, RET_QK_DIM), lambda b, h: (b, h)),
            pl.BlockSpec((geo.seg, RET_QK_DIM), lambda b, h: (b, nq + h)),
            pl.BlockSpec((geo.seg, RET_V_DIM), lambda b, h: (b, nq + h))]


def _retention(geo, qkv, log_g, name):
    nc = geo.s // RET_CHUNK

    def kern(lg_ref, q_ref, k_ref, v_ref, o_ref, st_ref):
        h = pl.program_id(1)
        for d, rev in ((0, False), (1, True)):
            lg = lg_ref[pl.ds(d * RET_HEADS + h, 1), 0:1]
            st_ref[...] = _ret_state0(k_ref[geo.s:geo.seg, :].astype(F32), v_ref[geo.s:geo.seg, :].astype(F32), lg, rev)

            def chunk(ci, carry, d=d, rev=rev, lg=lg):
                r0 = pl.multiple_of((nc - 1 - ci if rev else ci) * RET_CHUNK, RET_CHUNK)
                rows = pl.ds(r0, RET_CHUNK)
                new_state, out = _ret_step(st_ref[...], q_ref[rows, :].astype(F32), k_ref[rows, :].astype(F32),
                                           v_ref[rows, :].astype(F32), lg, rev)
                st_ref[...] = new_state
                if d == 0:
                    o_ref[rows, :] = out
                else:
                    o_ref[rows, :] += out
                return carry

            lax.fori_loop(0, nc, chunk, 0)
        o_ref[geo.s:geo.seg, :] = jnp.zeros((geo.l, RET_V_DIM), F32)

    return pl.pallas_call(
        kern, name=name, grid=(geo.b, RET_HEADS), in_specs=_ret_specs(geo),
        out_specs=pl.BlockSpec((geo.seg, RET_V_DIM), lambda b, h: (b, h)),
        out_shape=jax.ShapeDtypeStruct((geo.r, RET_VWIDTH), F32),
        scratch_shapes=[pltpu.VMEM((RET_QK_DIM, RET_V_DIM), F32)],
        compiler_params=_cparams("parallel", "arbitrary"),
    )(log_g, qkv, qkv, qkv)


def _retention_bwd(geo, qkv, log_g, do, name):
    nc = geo.s // RET_CHUNK
    ctx = slice(geo.s, geo.seg)

    def kern(lg_ref, q_ref, k_ref, v_ref, do_ref, dq_ref, dk_ref, dv_ref, dlg_ref, states_ref, cur_ref, dst_ref):
        b, h = pl.program_id(0), pl.program_id(1)

        @pl.when((b == 0) & (h == 0))
        def _():
            dlg_ref[...] = jnp.zeros_like(dlg_ref)

        for d, rev in ((0, False), (1, True)):
            row = pl.ds(d * RET_HEADS + h, 1)
            lg = lg_ref[row, 0:1]
            kc, vc = k_ref[ctx, :].astype(F32), v_ref[ctx, :].astype(F32)
            cur_ref[...] = _ret_state0(kc, vc, lg, rev)

            def rows_of(ci, rev=rev):
                return pl.ds(pl.multiple_of((nc - 1 - ci if rev else ci) * RET_CHUNK, RET_CHUNK), RET_CHUNK)

            def load(rows):
                return q_ref[rows, :].astype(F32), k_ref[rows, :].astype(F32), v_ref[rows, :].astype(F32)

            def replay(ci, carry, rev=rev, lg=lg, rows_of=rows_of, load=load):
                states_ref[ci] = cur_ref[...]
                cur_ref[...] = _ret_step(cur_ref[...], *load(rows_of(ci)), lg, rev)[0]
                return carry

            lax.fori_loop(0, nc, replay, 0)
            dst_ref[...] = jnp.zeros_like(dst_ref)

            def back(t, dlg, d=d, rev=rev, lg=lg, rows_of=rows_of, load=load):
                ci = nc - 1 - t
                rows = rows_of(ci)
                _, vjp = jax.vjp(lambda st, q, k, v, g: _ret_step(st, q, k, v, g, rev), states_ref[ci], *load(rows), lg)
                dstate, dq, dk, dv, dg = vjp((dst_ref[...], do_ref[rows, :]))
                dst_ref[...] = dstate
                if d == 0:
                    dq_ref[rows, :], dk_ref[rows, :], dv_ref[rows, :] = dq, dk, dv
                else:
                    dq_ref[rows, :] += dq
                    dk_ref[rows, :] += dk
                    dv_ref[rows, :] += dv
                return dlg + dg

            dlg = lax.fori_loop(0, nc, back, jnp.zeros((1, 1), F32))
            _, vjp = jax.vjp(lambda kk, vv, g: _ret_state0(kk, vv, g, rev), kc, vc, lg)
            dkc, dvc, dg = vjp(dst_ref[...])
            if d == 0:
                dk_ref[ctx, :], dv_ref[ctx, :] = dkc, dvc
            else:
                dk_ref[ctx, :] += dkc
                dv_ref[ctx, :] += dvc
            dlg_ref[row, :] += jnp.broadcast_to(dlg + dg, (1, LANES))
        dq_ref[ctx, :] = jnp.zeros((geo.l, RET_QK_DIM), F32)

    nq = RET_HEADS
    return pl.pallas_call(
        kern, name=name, grid=(geo.b, RET_HEADS),
        in_specs=_ret_specs(geo) + [pl.BlockSpec((geo.seg, RET_V_DIM), lambda b, h: (b, h))],
        out_specs=[pl.BlockSpec((geo.seg, RET_QK_DIM), lambda b, h: (b, h)),
                   pl.BlockSpec((geo.seg, RET_QK_DIM), lambda b, h: (b, h)),
                   pl.BlockSpec((geo.seg, RET_V_DIM), lambda b, h: (b, h)),
                   pl.BlockSpec((2 * RET_HEADS, LANES), lambda b, h: (0, 0))],
        out_shape=[jax.ShapeDtypeStruct((geo.r, RET_QK_W), F32), jax.ShapeDtypeStruct((geo.r, RET_QK_W), F32),
                   jax.ShapeDtypeStruct((geo.r, RET_VWIDTH), F32), jax.ShapeDtypeStruct((2 * RET_HEADS, LANES), F32)],
        scratch_shapes=[pltpu.VMEM((nc, RET_QK_DIM, RET_V_DIM), F32), pltpu.VMEM((RET_QK_DIM, RET_V_DIM), F32),
                        pltpu.VMEM((RET_QK_DIM, RET_V_DIM), F32)],
        compiler_params=_cparams("arbitrary", "arbitrary"),
    )(log_g, qkv, qkv, qkv, do)


def _gated(o, g, gain):
    outs = []
    for h in range(RET_HEADS):
        cols = slice(h * RET_V_DIM, (h + 1) * RET_V_DIM)
        oh = o[:, cols]
        mu = jnp.mean(oh, axis=-1, keepdims=True)
        var = jnp.mean(jnp.square(oh - mu), axis=-1, keepdims=True)
        outs.append(_silu(g[:, cols]) * ((oh - mu) * lax.rsqrt(var + EPS) * gain[:, cols]))
    return jnp.concatenate(outs, axis=1)


def _ret_gated(geo, o, proj, gain, name):
    def body(i, ov, gv, gn):
        return _gated(ov, gv, gn)

    gate_block = (2 * RET_QK_W + RET_VWIDTH) // RET_VWIDTH
    return _rowwise(name, body, geo, 128, [(o, "row"), (proj, ("rowc", RET_VWIDTH, gate_block)), (gain, "full")],
                    [("row", RET_VWIDTH, BF16)])


def _ret_gated_bwd(geo, o, proj, gain, dout, name):
    def body(i, ov, gv, gn, dv):
        _, vjp = jax.vjp(_gated, ov, gv, gn)
        return vjp(dv)

    gate_block = (2 * RET_QK_W + RET_VWIDTH) // RET_VWIDTH
    return _rowwise(name, body, geo, 128,
                    [(o, "row"), (proj, ("rowc", RET_VWIDTH, gate_block)), (gain, "full"), (dout, "row")],
                    [("row", RET_VWIDTH, F32), ("row", RET_VWIDTH, F32), ("gacc", 1, RET_VWIDTH)])


def _whole(name, fn, out_shapes, *arrays):
    n = len(arrays)

    def kern(*refs):
        res = fn(*[r[...] for r in refs[:n]])
        for ref, val in zip(refs[n:], res):
            ref[...] = val.astype(ref.dtype)

    return pl.pallas_call(kern, name=name, out_shape=out_shapes)(*arrays)


def _rope_tables(geo, head_dim):
    rows = geo.s // GRID_W
    row = jnp.broadcast_to(jnp.arange(rows, dtype=jnp.int32)[:, None], (rows, GRID_W)).reshape(geo.s)
    col = jnp.broadcast_to(jnp.arange(GRID_W, dtype=jnp.int32)[None, :], (rows, GRID_W)).reshape(geo.s)
    axis_dim = head_dim // 2
    inv = ROPE_BASE ** (-jnp.arange(0, axis_dim, 2, dtype=F32) / axis_dim)
    ang_r = row.astype(F32)[:, None] * inv
    ang_c = col.astype(F32)[:, None] * inv
    cos = jnp.concatenate([jnp.cos(ang_r)] * 2 + [jnp.cos(ang_c)] * 2, axis=1)
    sin = jnp.concatenate([-jnp.sin(ang_r), jnp.sin(ang_r), -jnp.sin(ang_c), jnp.sin(ang_c)], axis=1)
    cos = jnp.concatenate([cos, jnp.ones((geo.l, head_dim), F32)], axis=0)
    sin = jnp.concatenate([sin, jnp.zeros((geo.l, head_dim), F32)], axis=0)
    reps = max(1, LANES // head_dim)
    return jnp.tile(cos, (1, reps)), jnp.tile(sin, (1, reps))


def _row_tile(r):
    return next(t for t in (1024, 512, 256, 128) if r % t == 0)


MOD_ROWS = 8


def _local_step(x, c, ctx, target, sp, wts):
    nb, s, d = x.shape
    geo = _Rows(nb, s, ctx.shape[1])
    assert nb + 1 <= MOD_ROWS and d == D_MODEL
    tm = _row_tile(geo.r)
    z = jnp.concatenate([x, ctx], axis=1).reshape(geo.r, d)
    cvec = jnp.concatenate([c, sp["c_ctx"][None, :], jnp.zeros((MOD_ROWS - nb - 1, d), F32)], axis=0)
    cact, = _whole("cond_silu", lambda v: (_silu(v),), [jax.ShapeDtypeStruct(cvec.shape, F32)], cvec)
    cos64, sin64 = _rope_tables(geo, HEAD_DIM)
    cos256, sin256 = _rope_tables(geo, RET_QK_DIM)
    q_gain = jnp.tile(sp["q_norm"].reshape(1, HEAD_DIM), (1, LANES // HEAD_DIM))
    k_gain = jnp.tile(sp["k_norm"].reshape(1, HEAD_DIM), (1, LANES // HEAD_DIM))
    sink = sp["sink"].reshape(N_HEADS)
    log_g = jnp.broadcast_to(sp["log_g"].reshape(2 * RET_HEADS, 1), (2 * RET_HEADS, LANES))
    gn_g = sp["gn_g"].reshape(1, RET_VWIDTH)

    saved = []
    for i in range(2):
        mod = _mm_nn(cact, wts["ada"][i], F32, f"mod{i}", MOD_ROWS, wts["ada"][i].shape[2], d, bias=sp["ada_b"][i][None, :])
        mod3 = mod[:nb + 1, None, :]
        n1, n2 = sp["norm1_g"][i][None, :], sp["norm2_g"][i][None, :]
        h1 = _norm_mod(geo, z, n1, mod3, 0, f"norm1_{i}")
        if i == 0:
            proj = _mm_nn(h1, wts["attn_qkv"], F32, "attn_qkv", tm, wts["attn_qkv"].shape[2], d)
            prep = _attn_prep(geo, proj, cos64, sin64, q_gain, k_gain, "attn_prep")
            o = _attention(geo, prep, sink, "attn")
            oraw = None
            mix = _mm_nn(o, wts["attn_o"], F32, "attn_out", tm, 1024, 1024)
        else:
            proj = _mm_nn(h1, wts["ret_qkvg"], F32, "ret_qkvg", tm, 512, d)
            prep = _ret_prep(geo, proj, cos256, sin256, "ret_prep")
            oraw = _retention(geo, prep, log_g, "ret")
            o = _ret_gated(geo, oraw, proj, gn_g, "ret_gated")
            mix = _mm_nn(o, wts["ret_o"], F32, "ret_out", tm, 1024, 1024)
        zmid = _gate_residual(geo, z, mix, mod3, 2 * d, f"res_mix{i}")
        h2 = _norm_mod(geo, zmid, n2, mod3, 3 * d, f"norm2_{i}")
        u = _mm_nn(h2, wts["ffn_in"][i], F32, f"ffn_in{i}", tm, wts["ffn_in"][i].shape[2], d)
        a = _swiglu(geo, u, f"swiglu{i}")
        f = _mm_nn(a, wts["ffn_out"][i], F32, f"ffn_out{i}", tm, 1024, D_FF // 2)
        zout = _gate_residual(geo, zmid, f, mod3, 5 * d, f"res_ffn{i}")
        saved.append(dict(z=z, mod3=mod3, n1=n1, n2=n2, h1=h1, proj=proj, prep=prep, o=o, oraw=oraw, mix=mix, zmid=zmid,
                          h2=h2, u=u, a=a, f=f))
        z = zout

    dz, loss = _loss_head(geo, z, target.reshape(nb * s, d), "loss")

    big, small = {}, {}
    dmods = [None, None]
    for i in (1, 0):
        sv = saved[i]
        mod3 = sv["mod3"]
        df, dg2 = _gate_residual_bwd(geo, dz, sv["f"], mod3, 5 * d, f"res_ffn_bwd{i}")
        da = _mm_nt(df, wts["ffn_out"][i], F32, f"ffn_out_dx{i}", tm, D_FF // 2, 1024)
        big[f"ffn_out{i}"] = _mm_tn(sv["a"], df, f"ffn_out_dw{i}", D_FF // 2, 1024, tm).reshape(N_CHIPS, D_FF // N_CHIPS, d)
        du = _swiglu_bwd(geo, sv["u"], da, f"swiglu_bwd{i}")
        n4 = wts["ffn_in"][i].shape[2]
        dh2 = _mm_nt(du, wts["ffn_in"][i], F32, f"ffn_in_dx{i}", tm, 1024, n4)
        big[f"ffn_in{i}"] = _mm_tn(sv["h2"], du, f"ffn_in_dw{i}", 1024, n4, tm, shards=N_CHIPS)
        dzmid, dsh2, dsc2, dn2 = _norm_mod_bwd(geo, sv["zmid"], sv["n2"], mod3, 3 * d, dh2, dz, f"norm2_bwd{i}")
        dmix, dg1 = _gate_residual_bwd(geo, dzmid, sv["mix"], mod3, 2 * d, f"res_mix_bwd{i}")
        if i == 0:
            do = _mm_nt(dmix, wts["attn_o"], BF16, "attn_out_dx", tm, 1024, 1024)
            big["attn_o"] = _mm_tn(sv["o"], dmix, "attn_out_dw", 1024, 1024, tm).reshape(N_CHIPS, 1024 // N_CHIPS, d)
            dq, dkv, dsink = _attention_bwd(geo, sv["prep"], sink, do, "attn_bwd")
            dproj, dqg, dkg = _attn_prep_bwd(geo, sv["proj"], cos64, sin64, q_gain, k_gain, dq, dkv, "attn_prep_bwd")
            small["q_norm"] = dqg[0, :HEAD_DIM] + dqg[0, HEAD_DIM:]
            small["k_norm"] = dkg[0, :HEAD_DIM] + dkg[0, HEAD_DIM:]
            small["sink"] = dsink[:, 0]
            wq = wts["attn_qkv"]
            dh1 = _mm_nt(dproj, wq, F32, "attn_qkv_dx", tm, 1024, wq.shape[2])
            big["attn_qkv"] = _mm_tn(sv["h1"], dproj, "attn_qkv_dw", 1024, wq.shape[2], tm, shards=N_CHIPS)
        else:
            do = _mm_nt(dmix, wts["ret_o"], F32, "ret_out_dx", tm, 1024, 1024)
            big["ret_o"] = _mm_tn(sv["o"], dmix, "ret_out_dw", 1024, 1024, tm).reshape(N_CHIPS, RET_VWIDTH // N_CHIPS, d)
            doraw, dgate, dgn = _ret_gated_bwd(geo, sv["oraw"], sv["proj"], gn_g, do, "ret_gated_bwd")
            small["gn_g"] = dgn[0]
            dq, dk, dv, dlg = _retention_bwd(geo, sv["prep"], log_g, doraw, "ret_bwd")
            small["log_g"] = dlg[:, 0].reshape(2, RET_HEADS)
            dproj = _ret_prep_bwd(geo, dq, dk, dv, dgate, cos256, sin256, "ret_prep_bwd")
            wq = wts["ret_qkvg"]
            dh1 = _mm_nt(dproj, wq, F32, "ret_qkvg_dx", tm, 1024, 512)
            big["ret_qkvg"] = _mm_tn(sv["h1"], dproj, "ret_qkvg_dw", 1024, 512, tm, shards=N_CHIPS)
        dz, dsh1, dsc1, dn1 = _norm_mod_bwd(geo, sv["z"], sv["n1"], mod3, 0, dh1, dzmid, f"norm1_bwd{i}")
        small[f"norm1_g{i}"], small[f"norm2_g{i}"] = dn1[0], dn2[0]
        parts = [dsh1, dsc1, dg1, dsh2, dsc2, dg2]
        rows = jnp.concatenate([jnp.concatenate([p[:nb, 0, :] for p in parts], axis=1),
                                jnp.concatenate([jnp.sum(p[nb:, 0, :], axis=0, keepdims=True) for p in parts], axis=1),
                                jnp.zeros((MOD_ROWS - nb - 1, 6 * d), F32)], axis=0)
        dmods[i] = rows
        small[f"ada_b{i}"] = jnp.sum(rows, axis=0)
        big[f"ada{i}"] = _mm_tn(cact, rows, f"ada_dw{i}", 1024, wts["ada"][i].shape[2], MOD_ROWS, shards=N_CHIPS)

    dcact = [_mm_nt(dmods[i], wts["ada"][i], F32, f"ada_dx{i}", MOD_ROWS, 1024, wts["ada"][i].shape[2]) for i in range(2)]

    def silu_bwd(v, d0, d1):
        sg = _sigmoid(v)
        return ((d0 + d1) * (sg * (1.0 + v * (1.0 - sg))),)

    dcvec, = _whole("cond_silu_bwd", silu_bwd, [jax.ShapeDtypeStruct(cvec.shape, F32)], cvec, dcact[0], dcact[1])
    small["c_ctx"] = dcvec[nb]
    return loss, dz, big, small


def _adamw(w, g, m, v, name):
    rows, cols = w.shape
    tr = next((t for t in (256, 128, 64, 32, 16, 8) if rows % t == 0), rows)
    c1 = 1.0 - ADAM_B1 ** ADAM_STEP
    c2 = 1.0 - ADAM_B2 ** ADAM_STEP

    def kern(w_ref, g_ref, m_ref, v_ref, d_ref, nm_ref, nv_ref):
        gv = g_ref[...]
        nm = ADAM_B1 * m_ref[...] + (1.0 - ADAM_B1) * gv
        nv = ADAM_B2 * v_ref[...] + (1.0 - ADAM_B2) * jnp.square(gv)
        d_ref[...] = -ADAM_LR * ((nm / c1) / (jnp.sqrt(nv / c2) + ADAM_EPS) + ADAM_WD * w_ref[...])
        nm_ref[...] = nm
        nv_ref[...] = nv

    spec = pl.BlockSpec((tr, cols), lambda i: (i, 0))
    return pl.pallas_call(
        kern, name=name, grid=(rows // tr,), in_specs=[spec] * 4, out_specs=[spec] * 3,
        out_shape=[jax.ShapeDtypeStruct(w.shape, F32)] * 3, compiler_params=_cparams("parallel"),
    )(w, g, m, v)


N_DEVICES = 8


def _mesh_pos():
    return lax.axis_index("x"), lax.axis_index("y"), lax.axis_index("c")


def _other_chips(x, y):
    return [(1 - x, y), (x, 1 - y), (1 - x, 1 - y)]


def _hbm(n):
    return [pl.BlockSpec(memory_space=pl.ANY)] * n


def _remote(src, dst, send_sem, recv_sem, device):
    return pltpu.make_async_remote_copy(src_ref=src, dst_ref=dst, send_sem=send_sem, recv_sem=recv_sem,
                                        device_id=device, device_id_type=MESH)


def _gather_shards(shards):
    n = len(shards)

    def body(*refs):
        ins, outs = refs[:n], refs[n:2 * n]
        send_sems, recv_sems, fwd_send, fwd_recv, local_sems = refs[2 * n:]
        x, y, c = _mesh_pos()
        chip = 2 * x + y
        others = _other_chips(x, y)
        sibling = (x, y, 1 - c)

        def half(w, which):
            r2 = shards[w].shape[0] // 2
            return pl.ds(which * r2, r2)

        local, sends = [], []
        for w in range(n):
            cp = pltpu.make_async_copy(ins[w], outs[w].at[chip], local_sems.at[w])
            cp.start()
            local.append(cp)
            for k, (px, py) in enumerate(others):
                cp = _remote(ins[w].at[half(w, c)], outs[w].at[chip, half(w, c)], send_sems.at[w, k], recv_sems.at[w, k], (px, py, c))
                cp.start()
                sends.append(cp)
        for w in range(n):
            for k, (px, py) in enumerate(others):
                got = outs[w].at[2 * px + py, half(w, c)]
                _remote(got, got, send_sems.at[w, k], recv_sems.at[w, k], (px, py, c)).wait_recv()
                cp = _remote(got, got, fwd_send.at[w, k], fwd_recv.at[w, k], sibling)
                cp.start()
                sends.append(cp)
        for w in range(n):
            for k, (px, py) in enumerate(others):
                theirs = outs[w].at[2 * px + py, half(w, 1 - c)]
                _remote(theirs, theirs, fwd_send.at[w, k], fwd_recv.at[w, k], sibling).wait_recv()
        for cp in sends:
            cp.wait_send()
        for cp in local:
            cp.wait()

    return pl.pallas_call(
        body, name="gather_weights", in_specs=_hbm(n), out_specs=_hbm(n),
        out_shape=[jax.ShapeDtypeStruct((N_CHIPS,) + s.shape, s.dtype) for s in shards],
        scratch_shapes=[pltpu.SemaphoreType.DMA((n, 3))] * 4 + [pltpu.SemaphoreType.DMA((n,))],
    )(*shards)


def _pair_swap(grads):
    n = len(grads)

    def body(*refs):
        ins, own, land = refs[:n], refs[n:2 * n], refs[2 * n:3 * n]
        send_sems, recv_sems, local_sems = refs[3 * n:]
        x, y, c = _mesh_pos()
        copies = []
        for w in range(n):
            r2 = grads[w].shape[1] // 2
            lc = pltpu.make_async_copy(ins[w].at[:, pl.ds(c * r2, r2)], own[w], local_sems.at[w])
            lc.start()
            rc = _remote(ins[w].at[:, pl.ds((1 - c) * r2, r2)], land[w], send_sems.at[w], recv_sems.at[w], (x, y, 1 - c))
            rc.start()
            copies += [lc, rc]
        for cp in copies:
            cp.wait()

    half = [jax.ShapeDtypeStruct((N_CHIPS, g.shape[1] // 2, g.shape[2]), F32) for g in grads]
    res = pl.pallas_call(
        body, name="grads_pair_swap", in_specs=_hbm(n), out_specs=_hbm(2 * n), out_shape=half + half,
        scratch_shapes=[pltpu.SemaphoreType.DMA((n,))] * 3,
    )(*grads)
    return res[:n], res[n:]


def _chip_exchange(hs):
    n = len(hs)

    def body(*refs):
        ins, land = refs[:n], refs[n:2 * n]
        send_sems, recv_sems, local_sems = refs[2 * n:]
        x, y, c = _mesh_pos()
        chip = 2 * x + y
        others = _other_chips(x, y)
        local, sends = [], []
        for w in range(n):
            cp = pltpu.make_async_copy(ins[w].at[chip], land[w].at[chip], local_sems.at[w])
            cp.start()
            local.append(cp)
            for k, (px, py) in enumerate(others):
                cp = _remote(ins[w].at[2 * px + py], land[w].at[chip], send_sems.at[w, k], recv_sems.at[w, k], (px, py, c))
                cp.start()
                sends.append(cp)
        for w in range(n):
            for k, (px, py) in enumerate(others):
                got = land[w].at[2 * px + py]
                _remote(got, got, send_sems.at[w, k], recv_sems.at[w, k], (px, py, c)).wait_recv()
        for cp in sends:
            cp.wait_send()
        for cp in local:
            cp.wait()

    return pl.pallas_call(
        body, name="grads_chip_exchange", in_specs=_hbm(n), out_specs=_hbm(n),
        out_shape=[jax.ShapeDtypeStruct(h.shape, h.dtype) for h in hs],
        scratch_shapes=[pltpu.SemaphoreType.DMA((n, 3))] * 2 + [pltpu.SemaphoreType.DMA((n,))],
    )(*hs)


def _pair_share(ts):
    n = len(ts)

    def body(*refs):
        ins, outs = refs[:n], refs[n:2 * n]
        send_sems, recv_sems, local_sems = refs[2 * n:]
        x, y, c = _mesh_pos()
        copies = []
        for w in range(n):
            r2 = ts[w].shape[0]
            lc = pltpu.make_async_copy(ins[w], outs[w].at[pl.ds(c * r2, r2)], local_sems.at[w])
            lc.start()
            rc = _remote(ins[w], outs[w].at[pl.ds(c * r2, r2)], send_sems.at[w], recv_sems.at[w], (x, y, 1 - c))
            rc.start()
            copies.append(lc)
            copies.append(rc)
        for w in range(n):
            r2 = ts[w].shape[0]
            theirs = outs[w].at[pl.ds((1 - c) * r2, r2)]
            _remote(theirs, theirs, send_sems.at[w], recv_sems.at[w], (x, y, 1 - c)).wait_recv()
            copies[2 * w].wait()
            copies[2 * w + 1].wait_send()

    return pl.pallas_call(
        body, name="grads_pair_share", in_specs=_hbm(n), out_specs=_hbm(n),
        out_shape=[jax.ShapeDtypeStruct((2 * t.shape[0], t.shape[1]), F32) for t in ts],
        scratch_shapes=[pltpu.SemaphoreType.DMA((n,))] * 3,
    )(*ts)


def _slab_tile(rows):
    return next(t for t in (512, 256, 176, 128, 64, 32, 16) if rows % t == 0)


def _sum_pair(own, land, name):
    _, r2, cols = own.shape
    tr = _slab_tile(r2)

    def kern(a_ref, b_ref, o_ref):
        o_ref[...] = (a_ref[...] + b_ref[...]).astype(BF16)

    spec = pl.BlockSpec((None, tr, cols), lambda j, i: (j, i, 0))
    return pl.pallas_call(
        kern, name=name, grid=(N_CHIPS, r2 // tr), in_specs=[spec, spec], out_specs=spec,
        out_shape=jax.ShapeDtypeStruct(own.shape, BF16), compiler_params=_cparams("parallel", "parallel"),
    )(own, land)


def _sum_chips(land, name):
    _, r2, cols = land.shape
    tr = _slab_tile(r2)

    def kern(l_ref, o_ref):
        acc = l_ref[0].astype(F32)
        for k in range(1, N_CHIPS):
            acc = acc + l_ref[k].astype(F32)
        o_ref[...] = acc

    return pl.pallas_call(
        kern, name=name, grid=(r2 // tr,), in_specs=[pl.BlockSpec((N_CHIPS, tr, cols), lambda i: (0, i, 0))],
        out_specs=pl.BlockSpec((tr, cols), lambda i: (i, 0)),
        out_shape=jax.ShapeDtypeStruct((r2, cols), F32), compiler_params=_cparams("parallel"),
    )(land)


def _reduce_scatter(grads):
    own, land = _pair_swap(grads)
    hs = [_sum_pair(a, b, f"grads_pair_sum{w}") for w, (a, b) in enumerate(zip(own, land))]
    land2 = _chip_exchange(hs)
    ts = [_sum_chips(l, f"grads_chip_sum{w}") for w, l in enumerate(land2)]
    return _pair_share(ts)


def _all_reduce_small(v, name):
    def body(v_ref, o_ref, land_ref, send_sems, recv_sems):
        x, y, c = _mesh_pos()
        me = 4 * x + 2 * y + c
        land_ref[me] = v_ref[...]
        for t in range(N_DEVICES):
            @pl.when(t != me)
            def _(t=t):
                _remote(v_ref, land_ref.at[me], send_sems.at[t], recv_sems.at[me], (t // 4, (t // 2) % 2, t % 2)).start()
        for t in range(N_DEVICES):
            @pl.when(t != me)
            def _(t=t):
                _remote(v_ref, land_ref.at[t], send_sems.at[t], recv_sems.at[t], (t // 4, (t // 2) % 2, t % 2)).wait()
        acc = land_ref[0]
        for t in range(1, N_DEVICES):
            acc = acc + land_ref[t]
        o_ref[...] = acc

    vmem = pl.BlockSpec(memory_space=pltpu.VMEM)
    return pl.pallas_call(
        body, name=name, in_specs=[vmem], out_specs=vmem, out_shape=jax.ShapeDtypeStruct(v.shape, F32),
        scratch_shapes=[pltpu.VMEM((N_DEVICES,) + v.shape, F32), pltpu.SemaphoreType.DMA((N_DEVICES,)),
                        pltpu.SemaphoreType.DMA((N_DEVICES,))],
    )(v)


SMALL_ROWS = 24


def _pack_small(small, dlogit):
    d = D_MODEL
    misc = jnp.zeros((d,), F32)
    misc = misc.at[0:HEAD_DIM].set(small["q_norm"]).at[128:128 + HEAD_DIM].set(small["k_norm"])
    misc = misc.at[256:256 + N_HEADS].set(small["sink"]).at[384:384 + 2 * RET_HEADS].set(dlogit.reshape(-1))
    rows = [small["ada_b0"].reshape(6, d), small["ada_b1"].reshape(6, d), small["norm1_g0"][None], small["norm1_g1"][None],
            small["norm2_g0"][None], small["norm2_g1"][None], small["c_ctx"][None], small["gn_g"].reshape(2, d), misc[None]]
    buf = jnp.concatenate(rows, axis=0)
    return jnp.concatenate([buf, jnp.zeros((SMALL_ROWS - buf.shape[0], d), F32)], axis=0)


def _unpack_small(buf):
    d = D_MODEL
    misc = buf[19]
    return dict(ada_b=buf[0:12].reshape(2, 6 * d), norm1_g=buf[12:14], norm2_g=buf[14:16], c_ctx=buf[16],
                gn_g=buf[17:19].reshape(2 * d), q_norm=misc[0:HEAD_DIM], k_norm=misc[128:128 + HEAD_DIM],
                sink=misc[256:256 + N_HEADS], decay=misc[384:384 + 2 * RET_HEADS])


def kernel(x, c, ctx, c_ctx, ada_w, ada_b, norm1_g, norm2_g, ffn_w_in, ffn_w_out, attn_w_qkv, attn_q_norm, attn_k_norm, attn_sink, attn_w_o, ret_w_qkvg, ret_decay_logit, ret_gn_g, ret_w_o, loss_target, m_c_ctx, m_ada_w, m_ada_b, m_norm1_g, m_norm2_g, m_ffn_w_in, m_ffn_w_out, m_attn_w_qkv, m_attn_q_norm, m_attn_k_norm, m_attn_sink, m_attn_w_o, m_ret_w_qkvg, m_ret_decay_logit, m_ret_gn_g, m_ret_w_o, v_c_ctx, v_ada_w, v_ada_b, v_norm1_g, v_norm2_g, v_ffn_w_in, v_ffn_w_out, v_attn_w_qkv, v_attn_q_norm, v_attn_k_norm, v_attn_sink, v_attn_w_o, v_ret_w_qkvg, v_ret_decay_logit, v_ret_gn_g, v_ret_w_o):
    xi, yi, ci = _mesh_pos()
    chip = 2 * xi + yi
    nb, s, d = x.shape
    gn_shard = ret_gn_g.shape[1]

    shards = dict(ada0=ada_w[0], ada1=ada_w[1], ffn_in0=ffn_w_in[0], ffn_in1=ffn_w_in[1], ffn_out0=ffn_w_out[0],
                  ffn_out1=ffn_w_out[1], attn_qkv=attn_w_qkv[0], attn_o=attn_w_o[0], ret_qkvg=ret_w_qkvg[0], ret_o=ret_w_o[0])
    names = list(shards)
    full = dict(zip(names, _gather_shards([shards[k].astype(BF16) for k in names])))
    gn_mine = jnp.where(ci == 0, ret_gn_g[0], jnp.zeros_like(ret_gn_g[0]))
    gn_place = lax.dynamic_update_slice(jnp.zeros((RET_VWIDTH,), F32), gn_mine, (chip * gn_shard,))
    gn_full = _all_reduce_small(gn_place.reshape(2, d), "gather_gn_gain").reshape(RET_VWIDTH)

    wts = dict(ada=[full["ada0"], full["ada1"]], ffn_in=[full["ffn_in0"], full["ffn_in1"]],
               ffn_out=[full["ffn_out0"].reshape(D_FF, d), full["ffn_out1"].reshape(D_FF, d)],
               attn_qkv=full["attn_qkv"], attn_o=full["attn_o"].reshape(N_HEADS * HEAD_DIM, d),
               ret_qkvg=full["ret_qkvg"], ret_o=full["ret_o"].reshape(RET_VWIDTH, d))
    decay_logit = ret_decay_logit[0]
    sp = dict(c_ctx=c_ctx, ada_b=ada_b, norm1_g=norm1_g, norm2_g=norm2_g, q_norm=attn_q_norm[0], k_norm=attn_k_norm[0],
              sink=attn_sink[0], log_g=jax.nn.log_sigmoid(decay_logit), gn_g=gn_full)
    loss_part, dz, big, small = _local_step(x, c, ctx, loss_target, sp, wts)

    loss = lax.psum(loss_part[0, 0], ("x", "y", "c"))
    grad_x = dz.reshape(nb, -1, d)[:, :s]

    dlogit = small["log_g"] * jax.nn.sigmoid(-decay_logit)
    sg = _unpack_small(_all_reduce_small(_pack_small(small, dlogit), "reduce_small_grads"))
    reduced = dict(zip(names, _reduce_scatter([big[k] for k in names])))

    grads = dict(
        c_ctx=sg["c_ctx"], ada_w=jnp.stack([reduced["ada0"], reduced["ada1"]]), ada_b=sg["ada_b"], norm1_g=sg["norm1_g"],
        norm2_g=sg["norm2_g"], ffn_w_in=jnp.stack([reduced["ffn_in0"], reduced["ffn_in1"]]),
        ffn_w_out=jnp.stack([reduced["ffn_out0"], reduced["ffn_out1"]]), attn_w_qkv=reduced["attn_qkv"][None],
        attn_q_norm=sg["q_norm"][None], attn_k_norm=sg["k_norm"][None], attn_sink=sg["sink"][None],
        attn_w_o=reduced["attn_o"][None], ret_w_qkvg=reduced["ret_qkvg"][None], ret_decay_logit=sg["decay"].reshape(1, 2, RET_HEADS),
        ret_gn_g=lax.dynamic_slice(sg["gn_g"], (chip * gn_shard,), (gn_shard,))[None], ret_w_o=reduced["ret_o"][None])
    params = dict(c_ctx=(c_ctx, m_c_ctx, v_c_ctx), ada_w=(ada_w, m_ada_w, v_ada_w), ada_b=(ada_b, m_ada_b, v_ada_b),
                  norm1_g=(norm1_g, m_norm1_g, v_norm1_g), norm2_g=(norm2_g, m_norm2_g, v_norm2_g),
                  ffn_w_in=(ffn_w_in, m_ffn_w_in, v_ffn_w_in), ffn_w_out=(ffn_w_out, m_ffn_w_out, v_ffn_w_out),
                  attn_w_qkv=(attn_w_qkv, m_attn_w_qkv, v_attn_w_qkv), attn_q_norm=(attn_q_norm, m_attn_q_norm, v_attn_q_norm),
                  attn_k_norm=(attn_k_norm, m_attn_k_norm, v_attn_k_norm), attn_sink=(attn_sink, m_attn_sink, v_attn_sink),
                  attn_w_o=(attn_w_o, m_attn_w_o, v_attn_w_o), ret_w_qkvg=(ret_w_qkvg, m_ret_w_qkvg, v_ret_w_qkvg),
                  ret_decay_logit=(ret_decay_logit, m_ret_decay_logit, v_ret_decay_logit),
                  ret_gn_g=(ret_gn_g, m_ret_gn_g, v_ret_gn_g), ret_w_o=(ret_w_o, m_ret_w_o, v_ret_w_o))
    order = list(params)
    deltas, new_m, new_v = [], [], []
    for k in order:
        w, m, v = params[k]
        g = grads[k].reshape(w.shape)
        grads[k] = g
        flat = (-1, w.shape[-1]) if w.ndim > 1 else (1, -1)
        if k == "ret_decay_logit":
            flat = (1, -1)
        dw, nm, nv = _adamw(w.reshape(flat), g.reshape(flat), m.reshape(flat), v.reshape(flat), f"adamw_{k}")
        deltas.append(dw.reshape(w.shape))
        new_m.append(nm.reshape(w.shape))
        new_v.append(nv.reshape(w.shape))
    return (loss, grad_x, *[grads[k] for k in order], *deltas, *new_m, *new_v)
```

```python
import functools

import jax
import jax.numpy as jnp
from jax import lax
from jax.experimental import pallas as pl
from jax.experimental.pallas import tpu as pltpu

F32 = jnp.float32
BF16 = jnp.bfloat16

D_MODEL = 1024
N_HEADS = 16
N_KV_HEADS = 4
HEAD_DIM = 64
WINDOW = 128
ATTN_BLOCK = 128
BAND = ATTN_BLOCK + 2 * WINDOW
RET_HEADS = 4
RET_QK_DIM = 256
RET_V_DIM = 512
RET_VWIDTH = 2048
RET_CHUNK = 128
D_FF = 2816
GRID_W = 64
ROPE_BASE = 10000.0
EPS = 1e-6
NEG_INF = -1e30
LANES = 128

ADAM_LR = 0.001
ADAM_B1 = 0.9
ADAM_B2 = 0.999
ADAM_EPS = 1e-08
ADAM_WD = 0.01
ADAM_STEP = 10

VMEM_LIMIT_BYTES = 56 * 1024 * 1024
MESH = pl.DeviceIdType.MESH
N_CHIPS = 4


def _cparams(*sem):
    return pltpu.CompilerParams(dimension_semantics=sem, vmem_limit_bytes=VMEM_LIMIT_BYTES)


_DIMS = {"nn": ((1,), (0,)), "nt": ((1,), (1,)), "tn": ((0,), (0,))}


def _dot(a, b, form):
    return lax.dot_general(a.astype(BF16), b.astype(BF16), (_DIMS[form], ((), ())), preferred_element_type=F32)


@functools.partial(jax.custom_vjp, nondiff_argnums=(2,))
def _mm(a, b, form):
    return _dot(a, b, form)


def _mm_fwd(a, b, form):
    return _dot(a, b, form), (a, b)


def _mm_bwd(form, res, ct):
    a, b = res
    if form == "nn":
        da, db = _dot(ct, b, "nt"), _dot(a, ct, "tn")
    elif form == "nt":
        da, db = _dot(ct, b, "nn"), _dot(ct, a, "tn")
    else:
        da, db = _dot(b, ct, "nt"), _dot(a, ct, "nn")
    return da.astype(a.dtype), db.astype(b.dtype)


_mm.defvjp(_mm_fwd, _mm_bwd)


def _swap_halves(x, half):
    w = x.shape[-1]
    lane = lax.broadcasted_iota(jnp.int32, x.shape, x.ndim - 1)
    return jnp.where(lane % (2 * half) < half, pltpu.roll(x, w - half, x.ndim - 1), pltpu.roll(x, half, x.ndim - 1))


@functools.partial(jax.custom_vjp, nondiff_argnums=(1,))
def _rot(x, half):
    return _swap_halves(x, half)


def _rot_fwd(x, half):
    return _swap_halves(x, half), None


def _rot_bwd(half, _, ct):
    return (_swap_halves(ct, half),)


_rot.defvjp(_rot_fwd, _rot_bwd)


def _rope(x, cos, sin_signed, half):
    return x * cos + _rot(x, half) * sin_signed


def _head_mean_square(x):
    r = lax.broadcasted_iota(jnp.int32, (LANES, LANES), 0) // HEAD_DIM
    c = lax.broadcasted_iota(jnp.int32, (LANES, LANES), 1) // HEAD_DIM
    g = jnp.where(r == c, 1.0 / HEAD_DIM, 0.0).astype(F32)
    return jnp.dot(x * x, g, precision=lax.Precision.HIGHEST, preferred_element_type=F32)


def _qk_chunk(x, gain, cos, sin_signed, scale):
    y = x * lax.rsqrt(_head_mean_square(x) + EPS) * gain
    return _rope(y, cos, sin_signed, HEAD_DIM // 4) * scale


def _sigmoid(x):
    return 1.0 / (1.0 + jnp.exp(-x))


def _silu(x):
    return x * _sigmoid(x)


def _mm_nn(a, w, out_dtype, name, tm, tn, tk, bias=None):
    m, k_dim = a.shape
    if w.ndim == 3:
        n = w.shape[0] * w.shape[2]
        per = w.shape[2] // tn
        assert w.shape[2] % tn == 0
        w_spec = pl.BlockSpec((None, tk, tn), lambda i, j, k: (j // per, k, j % per))
    else:
        n = w.shape[1]
        w_spec = pl.BlockSpec((tk, tn), lambda i, j, k: (k, j))
    assert m % tm == 0 and n % tn == 0 and k_dim % tk == 0, (name, a.shape, w.shape, tm, tn, tk)
    nk = k_dim // tk
    has_bias = bias is not None

    def body(*refs):
        a_ref, w_ref = refs[0], refs[1]
        b_ref = refs[2] if has_bias else None
        o_ref, acc_ref = refs[-2], refs[-1]
        k = pl.program_id(2)

        @pl.when(k == 0)
        def _():
            acc_ref[...] = jnp.zeros_like(acc_ref)

        acc_ref[...] += jnp.dot(a_ref[...].astype(BF16), w_ref[...], preferred_element_type=F32)

        @pl.when(k == nk - 1)
        def _():
            r = acc_ref[...]
            if has_bias:
                r = r + b_ref[...]
            o_ref[...] = r.astype(out_dtype)

    in_specs = [pl.BlockSpec((tm, tk), lambda i, j, k: (i, k)), w_spec]
    args = [a, w]
    if has_bias:
        in_specs.append(pl.BlockSpec((1, tn), lambda i, j, k: (0, j)))
        args.append(bias)
    return pl.pallas_call(
        body, name=name, grid=(m // tm, n // tn, nk), in_specs=in_specs,
        out_specs=pl.BlockSpec((tm, tn), lambda i, j, k: (i, j)),
        out_shape=jax.ShapeDtypeStruct((m, n), out_dtype),
        scratch_shapes=[pltpu.VMEM((tm, tn), F32)],
        compiler_params=_cparams("parallel", "parallel", "arbitrary"),
    )(*args)


def _mm_nt(a, w, out_dtype, name, tm, tn, tk):
    m, c_dim = a.shape
    if w.ndim == 3:
        k_out = w.shape[1]
        per = w.shape[2] // tk
        assert w.shape[2] % tk == 0 and w.shape[0] * w.shape[2] == c_dim
        w_spec = pl.BlockSpec((None, tn, tk), lambda i, j, k: (k // per, j, k % per))
    else:
        k_out = w.shape[0]
        assert w.shape[1] == c_dim
        w_spec = pl.BlockSpec((tn, tk), lambda i, j, k: (j, k))
    assert m % tm == 0 and k_out % tn == 0 and c_dim % tk == 0, (name, a.shape, w.shape, tm, tn, tk)
    nk = c_dim // tk

    def body(a_ref, w_ref, o_ref, acc_ref):
        k = pl.program_id(2)

        @pl.when(k == 0)
        def _():
            acc_ref[...] = jnp.zeros_like(acc_ref)

        acc_ref[...] += _dot(a_ref[...], w_ref[...], "nt")

        @pl.when(k == nk - 1)
        def _():
            o_ref[...] = acc_ref[...].astype(out_dtype)

    return pl.pallas_call(
        body, name=name, grid=(m // tm, k_out // tn, nk),
        in_specs=[pl.BlockSpec((tm, tk), lambda i, j, k: (i, k)), w_spec],
        out_specs=pl.BlockSpec((tm, tn), lambda i, j, k: (i, j)),
        out_shape=jax.ShapeDtypeStruct((m, k_out), out_dtype),
        scratch_shapes=[pltpu.VMEM((tm, tn), F32)],
        compiler_params=_cparams("parallel", "parallel", "arbitrary"),
    )(a, w)


def _mm_tn(a, b, name, tm, tn, tk, shards=None):
    r, k_dim = a.shape
    n = b.shape[1]
    assert r % tk == 0 and k_dim % tm == 0 and n % tn == 0, (name, a.shape, b.shape, tm, tn, tk)
    nk = r // tk
    if shards:
        per = n // shards // tn
        assert n % (shards * tn) == 0
        out_shape = jax.ShapeDtypeStruct((shards, k_dim, n // shards), F32)
        out_spec = pl.BlockSpec((None, tm, tn), lambda i, j, k: (j // per, i, j % per))
    else:
        out_shape = jax.ShapeDtypeStruct((k_dim, n), F32)
        out_spec = pl.BlockSpec((tm, tn), lambda i, j, k: (i, j))

    def body(a_ref, b_ref, o_ref):
        k = pl.program_id(2)

        @pl.when(k == 0)
        def _():
            o_ref[...] = jnp.zeros_like(o_ref)

        o_ref[...] += _dot(a_ref[...], b_ref[...], "tn")

    return pl.pallas_call(
        body, name=name, grid=(k_dim // tm, n // tn, nk),
        in_specs=[pl.BlockSpec((tk, tm), lambda i, j, k: (k, i)), pl.BlockSpec((tk, tn), lambda i, j, k: (k, j))],
        out_specs=out_spec, out_shape=out_shape,
        compiler_params=_cparams("parallel", "parallel", "arbitrary"),
    )(a, b)


class _Rows:
    def __init__(self, b, s, l):
        self.b, self.s, self.l = b, s, l
        self.seg = s + l
        self.r = b * self.seg


def _rowwise(name, body, geo, tm, ins, outs):
    seg_blocks, x_blocks = geo.seg // tm, geo.s // tm
    assert geo.seg % tm == 0 and geo.s % tm == 0
    nb = geo.b

    def is_ctx(i):
        return i % seg_blocks >= x_blocks

    in_specs, args = [], []
    for arr, kind in ins:
        args.append(arr)
        if kind == "row":
            in_specs.append(pl.BlockSpec((tm, arr.shape[1]), lambda i: (i, 0)))
        elif kind == "ex":
            in_specs.append(pl.BlockSpec((None, 1, arr.shape[2]), lambda i: (jnp.where(is_ctx(i), nb, i // seg_blocks), 0, 0)))
        elif kind == "full":
            in_specs.append(pl.BlockSpec(arr.shape, lambda i, nd=arr.ndim: (0,) * nd))
        elif kind == "tab":
            in_specs.append(pl.BlockSpec((tm, arr.shape[1]), lambda i: (i % seg_blocks, 0)))
        elif kind == "xrow":
            in_specs.append(pl.BlockSpec(
                (tm, arr.shape[1]), lambda i: ((i // seg_blocks) * x_blocks + jnp.minimum(i % seg_blocks, x_blocks - 1), 0)))
        else:
            _, width, cb = kind
            in_specs.append(pl.BlockSpec((tm, width), lambda i, cb=cb: (i, cb)))
    out_specs, out_shapes = [], []
    for o in outs:
        if o[0] == "row":
            out_specs.append(pl.BlockSpec((tm, o[1]), lambda i: (i, 0)))
            out_shapes.append(jax.ShapeDtypeStruct((geo.r, o[1]), o[2]))
        elif o[0] == "exacc":
            out_specs.append(pl.BlockSpec((None, 1, o[1]), lambda i: (jnp.where(is_ctx(i), nb, 0) + i // seg_blocks, 0, 0)))
            out_shapes.append(jax.ShapeDtypeStruct((2 * nb, 1, o[1]), F32))
        else:
            out_specs.append(pl.BlockSpec((o[1], o[2]), lambda i: (0, 0)))
            out_shapes.append(jax.ShapeDtypeStruct((o[1], o[2]), F32))
    n_in = len(ins)

    def kern(*refs):
        i = pl.program_id(0)
        res = body(i, *[r[...] for r in refs[:n_in]])
        if not isinstance(res, (tuple, list)):
            res = (res,)
        jj = i % seg_blocks
        first_of_part = (jj == 0) | (jj == x_blocks)
        for o, ref, val in zip(outs, refs[n_in:], res):
            if o[0] == "row":
                ref[...] = val.astype(ref.dtype)
            else:
                first = first_of_part if o[0] == "exacc" else i == 0

                @pl.when(first)
                def _(ref=ref, val=val):
                    ref[...] = val

                @pl.when(jnp.logical_not(first))
                def _(ref=ref, val=val):
                    ref[...] += val

    res = pl.pallas_call(
        kern, name=name, grid=(geo.r // tm,), in_specs=in_specs, out_specs=out_specs, out_shape=out_shapes,
        compiler_params=_cparams("arbitrary"),
    )(*args)
    return res[0] if len(res) == 1 else res


def _colsum(v):
    return jnp.sum(v, axis=0, keepdims=True)


def _norm_mod(geo, z, gain, mod, off, name):
    d = D_MODEL

    def body(i, zv, g, m):
        r = lax.rsqrt(jnp.mean(zv * zv, axis=-1, keepdims=True) + EPS)
        return (zv * r) * g * (1.0 + m[:, off + d:off + 2 * d]) + m[:, off:off + d]

    return _rowwise(name, body, geo, 256, [(z, "row"), (gain, "full"), (mod, "ex")], [("row", d, BF16)])


def _norm_mod_bwd(geo, z, gain, mod, off, dh, dz_skip, name):
    d = D_MODEL

    def body(i, zv, g, m, dhv, skip):
        r = lax.rsqrt(jnp.mean(zv * zv, axis=-1, keepdims=True) + EPS)
        n = zv * r
        dng = dhv * (1.0 + m[:, off + d:off + 2 * d])
        dn = dng * g
        dz = r * (dn - n * jnp.mean(dn * n, axis=-1, keepdims=True)) + skip
        return dz, _colsum(dhv), _colsum(dhv * (n * g)), _colsum(dng * n)

    return _rowwise(name, body, geo, 256, [(z, "row"), (gain, "full"), (mod, "ex"), (dh, "row"), (dz_skip, "row")],
                    [("row", d, F32), ("exacc", d), ("exacc", d), ("gacc", 1, d)])


def _gate_residual(geo, z, out, mod, off, name):
    d = D_MODEL

    def body(i, zv, ov, m):
        return zv + m[:, off:off + d] * ov

    return _rowwise(name, body, geo, 256, [(z, "row"), (out, "row"), (mod, "ex")], [("row", d, F32)])


def _gate_residual_bwd(geo, dz, out, mod, off, name):
    d = D_MODEL

    def body(i, dzv, ov, m):
        return dzv * m[:, off:off + d], _colsum(dzv * ov)

    return _rowwise(name, body, geo, 256, [(dz, "row"), (out, "row"), (mod, "ex")], [("row", d, BF16), ("exacc", d)])


def _swiglu(geo, u, name):
    def body(i, uv):
        return _silu(uv[:, :D_FF]) * uv[:, D_FF:]

    return _rowwise(name, body, geo, 128, [(u, "row")], [("row", D_FF, BF16)])


def _swiglu_bwd(geo, u, da, name):
    def body(i, uv, dav):
        g, up = uv[:, :D_FF], uv[:, D_FF:]
        s = _sigmoid(g)
        return jnp.concatenate([dav * up * (s * (1.0 + g * (1.0 - s))), dav * (g * s)], axis=1)

    return _rowwise(name, body, geo, 128, [(u, "row"), (da, "row")], [("row", 2 * D_FF, BF16)])


def _loss_head(geo, z, target, name):
    seg_blocks, x_blocks = geo.seg // 256, geo.s // 256

    def body(i, zv, tv):
        keep = jnp.where(i % seg_blocks >= x_blocks, 0.0, 1.0)
        err = (zv - tv) * keep
        part = 0.5 * jnp.sum(jnp.mean(err * err, axis=-1, keepdims=True), axis=0, keepdims=True)
        return err * (1.0 / D_MODEL), jnp.broadcast_to(part, (1, LANES))

    return _rowwise(name, body, geo, 256, [(z, "row"), (target, "xrow")], [("row", D_MODEL, F32), ("gacc", 1, LANES)])


Q_SCALE = HEAD_DIM ** -0.5
N_QK_CHUNKS = (N_HEADS + N_KV_HEADS) * HEAD_DIM // LANES
N_Q_CHUNKS = N_HEADS * HEAD_DIM // LANES


def _attn_prep(geo, proj, cos, sin_signed, q_gain, k_gain, name):
    def body(i, p, cs, sn, qg, kg):
        outs = []
        for ch in range(N_QK_CHUNKS):
            is_q = ch < N_Q_CHUNKS
            outs.append(_qk_chunk(p[:, ch * LANES:(ch + 1) * LANES], qg if is_q else kg, cs, sn, Q_SCALE if is_q else 1.0))
        outs.append(p[:, N_QK_CHUNKS * LANES:])
        return jnp.concatenate(outs, axis=1)

    return _rowwise(name, body, geo, 256, [(proj, "row"), (cos, "tab"), (sin_signed, "tab"), (q_gain, "full"), (k_gain, "full")],
                    [("row", proj.shape[1], BF16)])


def _attn_prep_bwd(geo, proj, cos, sin_signed, q_gain, k_gain, dq, dkv, name):
    kw = N_KV_HEADS * HEAD_DIM

    def body(i, p, cs, sn, qg, kg, dqv, dkvv):
        outs = []
        dgains = [jnp.zeros((1, LANES), F32), jnp.zeros((1, LANES), F32)]
        for ch in range(N_QK_CHUNKS):
            is_q = ch < N_Q_CHUNKS
            scale = Q_SCALE if is_q else 1.0
            ct = dqv[:, ch * LANES:(ch + 1) * LANES] if is_q else dkvv[:, (ch - N_Q_CHUNKS) * LANES:(ch - N_Q_CHUNKS + 1) * LANES]
            _, vjp = jax.vjp(lambda xx, gg, scale=scale: _qk_chunk(xx, gg, cs, sn, scale),
                             p[:, ch * LANES:(ch + 1) * LANES], qg if is_q else kg)
            dx, dg = vjp(ct)
            outs.append(dx)
            dgains[0 if is_q else 1] = dgains[0 if is_q else 1] + dg
        outs.append(dkvv[:, kw:])
        return jnp.concatenate(outs, axis=1), dgains[0], dgains[1]

    return _rowwise(name, body, geo, 256,
                    [(proj, "row"), (cos, "tab"), (sin_signed, "tab"), (q_gain, "full"), (k_gain, "full"), (dq, "row"), (dkv, "row")],
                    [("row", proj.shape[1], BF16), ("gacc", 1, LANES), ("gacc", 1, LANES)])


def _attn_geometry(geo):
    assert geo.s % ATTN_BLOCK == 0 and geo.l % ATTN_BLOCK == 0 and geo.seg >= BAND
    return geo.seg // ATTN_BLOCK, geo.s // ATTN_BLOCK


def _attn_mask(j, s0, geo, n_x_blocks):
    r = lax.broadcasted_iota(jnp.int32, (ATTN_BLOCK, BAND), 0)
    n = lax.broadcasted_iota(jnp.int32, (ATTN_BLOCK, BAND), 1)
    dist = (s0 - j * ATTN_BLOCK) + n - r
    return (jnp.abs(dist) <= WINDOW) & (s0 + n < geo.s) & (j < n_x_blocks)


def _attn_probs(q, kc, kb, valid, sink):
    s_ctx = _dot(q, kc, "nt")
    s_loc = jnp.where(valid, _dot(q, kb, "nt"), NEG_INF)
    m = jnp.maximum(jnp.maximum(jnp.max(s_ctx, axis=-1, keepdims=True), jnp.max(s_loc, axis=-1, keepdims=True)), sink)
    e_ctx, e_loc, e_sink = jnp.exp(s_ctx - m), jnp.exp(s_loc - m), jnp.exp(sink - m)
    inv = 1.0 / (jnp.sum(e_ctx, axis=-1, keepdims=True) + jnp.sum(e_loc, axis=-1, keepdims=True) + e_sink)
    return e_ctx * inv, e_loc * inv, e_sink * inv


def _attention(geo, qkv, sink, name):
    n_blocks, n_x_blocks = _attn_geometry(geo)
    qw, kw = N_HEADS * HEAD_DIM, N_KV_HEADS * HEAD_DIM
    group = N_HEADS // N_KV_HEADS

    def kern(sink_ref, q_ref, k_ref, v_ref, o_ref):
        j = pl.program_id(1)
        s0 = pl.multiple_of(jnp.clip((j - 1) * ATTN_BLOCK, 0, geo.seg - BAND), ATTN_BLOCK)
        valid = _attn_mask(j, s0, geo, n_x_blocks)
        kb_all, vb_all = k_ref[pl.ds(s0, BAND), :], v_ref[pl.ds(s0, BAND), :]
        kc_all, vc_all = k_ref[geo.s:geo.seg, :], v_ref[geo.s:geo.seg, :]
        for h in range(N_HEADS):
            kv = slice((h // group) * HEAD_DIM, (h // group + 1) * HEAD_DIM)
            q = q_ref[:, h * HEAD_DIM:(h + 1) * HEAD_DIM]
            p_ctx, p_loc, _ = _attn_probs(q, kc_all[:, kv], kb_all[:, kv], valid, sink_ref[h])
            o = _dot(p_ctx, vc_all[:, kv], "nn") + _dot(p_loc, vb_all[:, kv], "nn")
            o_ref[:, h * HEAD_DIM:(h + 1) * HEAD_DIM] = o.astype(BF16)

    return pl.pallas_call(
        kern, name=name, grid=(geo.b, n_blocks),
        in_specs=[pl.BlockSpec(memory_space=pltpu.SMEM),
                  pl.BlockSpec((ATTN_BLOCK, qw), lambda b, j: (b * n_blocks + j, 0)),
                  pl.BlockSpec((geo.seg, kw), lambda b, j: (b, qw // kw)),
                  pl.BlockSpec((geo.seg, kw), lambda b, j: (b, qw // kw + 1))],
        out_specs=pl.BlockSpec((ATTN_BLOCK, qw), lambda b, j: (b * n_blocks + j, 0)),
        out_shape=jax.ShapeDtypeStruct((geo.r, qw), BF16),
        compiler_params=_cparams("parallel", "arbitrary"),
    )(sink, qkv, qkv, qkv)


def _attention_bwd(geo, qkv, sink, do, name):
    n_blocks, n_x_blocks = _attn_geometry(geo)
    qw, kw = N_HEADS * HEAD_DIM, N_KV_HEADS * HEAD_DIM
    group = N_HEADS // N_KV_HEADS

    def kern(sink_ref, q_ref, k_ref, v_ref, do_ref, dq_ref, dkv_ref, dsink_ref):
        b, j = pl.program_id(0), pl.program_id(1)
        s0 = pl.multiple_of(jnp.clip((j - 1) * ATTN_BLOCK, 0, geo.seg - BAND), ATTN_BLOCK)
        valid = _attn_mask(j, s0, geo, n_x_blocks)

        @pl.when(j == 0)
        def _():
            dkv_ref[...] = jnp.zeros_like(dkv_ref)

        @pl.when((j == 0) & (b == 0))
        def _():
            dsink_ref[...] = jnp.zeros_like(dsink_ref)

        kb_all, vb_all = k_ref[pl.ds(s0, BAND), :], v_ref[pl.ds(s0, BAND), :]
        kc_all, vc_all = k_ref[geo.s:geo.seg, :], v_ref[geo.s:geo.seg, :]
        for g in range(N_KV_HEADS):
            kv = slice(g * HEAD_DIM, (g + 1) * HEAD_DIM)
            kc, kb, vc, vb = kc_all[:, kv], kb_all[:, kv], vc_all[:, kv], vb_all[:, kv]
            dkc = jnp.zeros((geo.l, HEAD_DIM), F32)
            dvc = jnp.zeros((geo.l, HEAD_DIM), F32)
            dkb = jnp.zeros((BAND, HEAD_DIM), F32)
            dvb = jnp.zeros((BAND, HEAD_DIM), F32)
            for h in range(g * group, (g + 1) * group):
                hs = slice(h * HEAD_DIM, (h + 1) * HEAD_DIM)
                q, dout = q_ref[:, hs], do_ref[:, hs]
                p_ctx, p_loc, p_sink = _attn_probs(q, kc, kb, valid, sink_ref[h])
                dp_ctx, dp_loc = _dot(dout, vc, "nt"), _dot(dout, vb, "nt")
                dsum = jnp.sum(p_ctx * dp_ctx, axis=-1, keepdims=True) + jnp.sum(p_loc * dp_loc, axis=-1, keepdims=True)
                ds_ctx, ds_loc = p_ctx * (dp_ctx - dsum), p_loc * (dp_loc - dsum)
                dq_ref[:, hs] = _dot(ds_ctx, kc, "nn") + _dot(ds_loc, kb, "nn")
                dkc += _dot(ds_ctx, q, "tn")
                dkb += _dot(ds_loc, q, "tn")
                dvc += _dot(p_ctx, dout, "tn")
                dvb += _dot(p_loc, dout, "tn")
                dsink_ref[h:h + 1, :] += jnp.broadcast_to(-jnp.sum(p_sink * dsum, axis=0, keepdims=True), (1, LANES))
            vv = slice(kw + g * HEAD_DIM, kw + (g + 1) * HEAD_DIM)
            dkv_ref[pl.ds(s0, BAND), kv] += dkb
            dkv_ref[pl.ds(s0, BAND), vv] += dvb
            dkv_ref[geo.s:geo.seg, kv] += dkc
            dkv_ref[geo.s:geo.seg, vv] += dvc

    return pl.pallas_call(
        kern, name=name, grid=(geo.b, n_blocks),
        in_specs=[pl.BlockSpec(memory_space=pltpu.SMEM),
                  pl.BlockSpec((ATTN_BLOCK, qw), lambda b, j: (b * n_blocks + j, 0)),
                  pl.BlockSpec((geo.seg, kw), lambda b, j: (b, qw // kw)),
                  pl.BlockSpec((geo.seg, kw), lambda b, j: (b, qw // kw + 1)),
                  pl.BlockSpec((ATTN_BLOCK, qw), lambda b, j: (b * n_blocks + j, 0))],
        out_specs=[pl.BlockSpec((ATTN_BLOCK, qw), lambda b, j: (b * n_blocks + j, 0)),
                   pl.BlockSpec((geo.seg, 2 * kw), lambda b, j: (b, 0)),
                   pl.BlockSpec((N_HEADS, LANES), lambda b, j: (0, 0))],
        out_shape=[jax.ShapeDtypeStruct((geo.r, qw), F32), jax.ShapeDtypeStruct((geo.r, 2 * kw), F32),
                   jax.ShapeDtypeStruct((N_HEADS, LANES), F32)],
        compiler_params=_cparams("arbitrary", "arbitrary"),
    )(sink, qkv, qkv, qkv, do)


RET_QK_W = RET_HEADS * RET_QK_DIM
K_SCALE = RET_QK_DIM ** -0.5


def _ret_prep(geo, proj, cos, sin_signed, name):
    def body(i, p, cs, sn):
        cs2, sn2 = jnp.concatenate([cs] * RET_HEADS, axis=1), jnp.concatenate([sn] * RET_HEADS, axis=1)
        q = _rope(p[:, :RET_QK_W], cs2, sn2, RET_QK_DIM // 4)
        k = _rope(p[:, RET_QK_W:2 * RET_QK_W], cs2, sn2, RET_QK_DIM // 4) * K_SCALE
        return jnp.concatenate([q, k, p[:, 2 * RET_QK_W:]], axis=1)

    return _rowwise(name, body, geo, 128, [(proj, ("rowc", 2 * RET_QK_W + RET_VWIDTH, 0)), (cos, "tab"), (sin_signed, "tab")],
                    [("row", 2 * RET_QK_W + RET_VWIDTH, BF16)])


def _ret_prep_bwd(geo, dq, dk, dv, dgate, cos, sin_signed, name):
    def body(i, dqv, dkv, dvv, dg, cs, sn):
        cs2, sn2 = jnp.concatenate([cs] * RET_HEADS, axis=1), jnp.concatenate([sn] * RET_HEADS, axis=1)
        dkv = dkv * K_SCALE
        dqv = dqv * cs2 + _swap_halves(dqv * sn2, RET_QK_DIM // 4)
        dkv = dkv * cs2 + _swap_halves(dkv * sn2, RET_QK_DIM // 4)
        return jnp.concatenate([dqv, dkv, dvv, dg], axis=1)

    return _rowwise(name, body, geo, 128,
                    [(dq, "row"), (dk, "row"), (dv, "row"), (dgate, "row"), (cos, "tab"), (sin_signed, "tab")],
                    [("row", 2 * RET_QK_W + 2 * RET_VWIDTH, BF16)])


def _ret_step(state, q, k, v, lg, rev):
    c = RET_CHUNK
    ri = lax.broadcasted_iota(jnp.int32, (c, 1), 0).astype(F32)
    cj = lax.broadcasted_iota(jnp.int32, (1, c), 1).astype(F32)
    if rev:
        dist, q_decay, k_decay = cj - ri, jnp.exp(lg * (c - ri)), jnp.exp(lg * ri)
    else:
        dist, q_decay, k_decay = ri - cj, jnp.exp(lg * (ri + 1.0)), jnp.exp(lg * (c - 1.0 - ri))
    intra = jnp.where(dist >= 0, jnp.exp(lg * jnp.maximum(dist, 0.0)), 0.0)
    scores = _mm(q, k, "nt") * intra
    out = _mm(scores, v, "nn") + _mm(q, state, "nn") * q_decay
    new_state = state * jnp.exp(lg * c) + _mm(k * k_decay, v, "tn")
    return new_state, out


def _ret_state0(kc, vc, lg, rev):
    n = kc.shape[0]
    t = lax.broadcasted_iota(jnp.int32, (n, 1), 0).astype(F32)
    decay = jnp.exp(lg * t) if rev else jnp.exp(lg * (n - 1.0 - t))
    return _mm(kc * decay, vc, "tn")


def _ret_specs(geo):
    nq = RET_HEADS
    return [pl.BlockSpec((2 * RET_HEADS, LANES), lambda b, h: (0, 0)),
            pl.BlockSpec((geo.seg, RET_QK_DIM), lambda b, h: (b, h)),
            pl.BlockSpec((geo.seg, RET_QK_DIM), lambda b, h: (b, nq + h)),
            pl.BlockSpec((geo.seg, RET_V_DIM), lambda b, h: (b, nq + h))]


def _retention(geo, qkv, log_g, name):
    nc = geo.s // RET_CHUNK

    def kern(lg_ref, q_ref, k_ref, v_ref, o_ref, st_ref):
        h = pl.program_id(1)
        for d, rev in ((0, False), (1, True)):
            lg = lg_ref[pl.ds(d * RET_HEADS + h, 1), 0:1]
            st_ref[...] = _ret_state0(k_ref[geo.s:geo.seg, :].astype(F32), v_ref[geo.s:geo.seg, :].astype(F32), lg, rev)

            def chunk(ci, carry, d=d, rev=rev, lg=lg):
                r0 = pl.multiple_of((nc - 1 - ci if rev else ci) * RET_CHUNK, RET_CHUNK)
                rows = pl.ds(r0, RET_CHUNK)
                new_state, out = _ret_step(st_ref[...], q_ref[rows, :].astype(F32), k_ref[rows, :].astype(F32),
                                           v_ref[rows, :].astype(F32), lg, rev)
                st_ref[...] = new_state
                if d == 0:
                    o_ref[rows, :] = out
                else:
                    o_ref[rows, :] += out
                return carry

            lax.fori_loop(0, nc, chunk, 0)
        o_ref[geo.s:geo.seg, :] = jnp.zeros((geo.l, RET_V_DIM), F32)

    return pl.pallas_call(
        kern, name=name, grid=(geo.b, RET_HEADS), in_specs=_ret_specs(geo),
        out_specs=pl.BlockSpec((geo.seg, RET_V_DIM), lambda b, h: (b, h)),
        out_shape=jax.ShapeDtypeStruct((geo.r, RET_VWIDTH), F32),
        scratch_shapes=[pltpu.VMEM((RET_QK_DIM, RET_V_DIM), F32)],
        compiler_params=_cparams("parallel", "arbitrary"),
    )(log_g, qkv, qkv, qkv)


def _retention_bwd(geo, qkv, log_g, do, name):
    nc = geo.s // RET_CHUNK
    ctx = slice(geo.s, geo.seg)

    def kern(lg_ref, q_ref, k_ref, v_ref, do_ref, dq_ref, dk_ref, dv_ref, dlg_ref, states_ref, cur_ref, dst_ref):
        b, h = pl.program_id(0), pl.program_id(1)

        @pl.when((b == 0) & (h == 0))
        def _():
            dlg_ref[...] = jnp.zeros_like(dlg_ref)

        for d, rev in ((0, False), (1, True)):
            row = pl.ds(d * RET_HEADS + h, 1)
            lg = lg_ref[row, 0:1]
            kc, vc = k_ref[ctx, :].astype(F32), v_ref[ctx, :].astype(F32)
            cur_ref[...] = _ret_state0(kc, vc, lg, rev)

            def rows_of(ci, rev=rev):
                return pl.ds(pl.multiple_of((nc - 1 - ci if rev else ci) * RET_CHUNK, RET_CHUNK), RET_CHUNK)

            def load(rows):
                return q_ref[rows, :].astype(F32), k_ref[rows, :].astype(F32), v_ref[rows, :].astype(F32)

            def replay(ci, carry, rev=rev, lg=lg, rows_of=rows_of, load=load):
                states_ref[ci] = cur_ref[...]
                cur_ref[...] = _ret_step(cur_ref[...], *load(rows_of(ci)), lg, rev)[0]
                return carry

            lax.fori_loop(0, nc, replay, 0)
            dst_ref[...] = jnp.zeros_like(dst_ref)

            def back(t, dlg, d=d, rev=rev, lg=lg, rows_of=rows_of, load=load):
                ci = nc - 1 - t
                rows = rows_of(ci)
                _, vjp = jax.vjp(lambda st, q, k, v, g: _ret_step(st, q, k, v, g, rev), states_ref[ci], *load(rows), lg)
                dstate, dq, dk, dv, dg = vjp((dst_ref[...], do_ref[rows, :]))
                dst_ref[...] = dstate
                if d == 0:
                    dq_ref[rows, :], dk_ref[rows, :], dv_ref[rows, :] = dq, dk, dv
                else:
                    dq_ref[rows, :] += dq
                    dk_ref[rows, :] += dk
                    dv_ref[rows, :] += dv
                return dlg + dg

            dlg = lax.fori_loop(0, nc, back, jnp.zeros((1, 1), F32))
            _, vjp = jax.vjp(lambda kk, vv, g: _ret_state0(kk, vv, g, rev), kc, vc, lg)
            dkc, dvc, dg = vjp(dst_ref[...])
            if d == 0:
                dk_ref[ctx, :], dv_ref[ctx, :] = dkc, dvc
            else:
                dk_ref[ctx, :] += dkc
                dv_ref[ctx, :] += dvc
            dlg_ref[row, :] += jnp.broadcast_to(dlg + dg, (1, LANES))
        dq_ref[ctx, :] = jnp.zeros((geo.l, RET_QK_DIM), F32)

    nq = RET_HEADS
    return pl.pallas_call(
        kern, name=name, grid=(geo.b, RET_HEADS),
        in_specs=_ret_specs(geo) + [pl.BlockSpec((geo.seg, RET_V_DIM), lambda b, h: (b, h))],
        out_specs=[pl.BlockSpec((geo.seg, RET_QK_DIM), lambda b, h: (b, h)),
                   pl.BlockSpec((geo.seg, RET_QK_DIM), lambda b, h: (b, h)),
                   pl.BlockSpec((geo.seg, RET_V_DIM), lambda b, h: (b, h)),
                   pl.BlockSpec((2 * RET_HEADS, LANES), lambda b, h: (0, 0))],
        out_shape=[jax.ShapeDtypeStruct((geo.r, RET_QK_W), F32), jax.ShapeDtypeStruct((geo.r, RET_QK_W), F32),
                   jax.ShapeDtypeStruct((geo.r, RET_VWIDTH), F32), jax.ShapeDtypeStruct((2 * RET_HEADS, LANES), F32)],
        scratch_shapes=[pltpu.VMEM((nc, RET_QK_DIM, RET_V_DIM), F32), pltpu.VMEM((RET_QK_DIM, RET_V_DIM), F32),
                        pltpu.VMEM((RET_QK_DIM, RET_V_DIM), F32)],
        compiler_params=_cparams("arbitrary", "arbitrary"),
    )(log_g, qkv, qkv, qkv, do)


def _gated(o, g, gain):
    outs = []
    for h in range(RET_HEADS):
        cols = slice(h * RET_V_DIM, (h + 1) * RET_V_DIM)
        oh = o[:, cols]
        mu = jnp.mean(oh, axis=-1, keepdims=True)
        var = jnp.mean(jnp.square(oh - mu), axis=-1, keepdims=True)
        outs.append(_silu(g[:, cols]) * ((oh - mu) * lax.rsqrt(var + EPS) * gain[:, cols]))
    return jnp.concatenate(outs, axis=1)


def _ret_gated(geo, o, proj, gain, name):
    def body(i, ov, gv, gn):
        return _gated(ov, gv, gn)

    gate_block = (2 * RET_QK_W + RET_VWIDTH) // RET_VWIDTH
    return _rowwise(name, body, geo, 128, [(o, "row"), (proj, ("rowc", RET_VWIDTH, gate_block)), (gain, "full")],
                    [("row", RET_VWIDTH, BF16)])


def _ret_gated_bwd(geo, o, proj, gain, dout, name):
    def body(i, ov, gv, gn, dv):
        _, vjp = jax.vjp(_gated, ov, gv, gn)
        return vjp(dv)

    gate_block = (2 * RET_QK_W + RET_VWIDTH) // RET_VWIDTH
    return _rowwise(name, body, geo, 128,
                    [(o, "row"), (proj, ("rowc", RET_VWIDTH, gate_block)), (gain, "full"), (dout, "row")],
                    [("row", RET_VWIDTH, F32), ("row", RET_VWIDTH, F32), ("gacc", 1, RET_VWIDTH)])


def _whole(name, fn, out_shapes, *arrays):
    n = len(arrays)

    def kern(*refs):
        res = fn(*[r[...] for r in refs[:n]])
        for ref, val in zip(refs[n:], res):
            ref[...] = val.astype(ref.dtype)

    return pl.pallas_call(kern, name=name, out_shape=out_shapes)(*arrays)


def _rope_tables(geo, head_dim):
    rows = geo.s // GRID_W
    row = jnp.broadcast_to(jnp.arange(rows, dtype=jnp.int32)[:, None], (rows, GRID_W)).reshape(geo.s)
    col = jnp.broadcast_to(jnp.arange(GRID_W, dtype=jnp.int32)[None, :], (rows, GRID_W)).reshape(geo.s)
    axis_dim = head_dim // 2
    inv = ROPE_BASE ** (-jnp.arange(0, axis_dim, 2, dtype=F32) / axis_dim)
    ang_r = row.astype(F32)[:, None] * inv
    ang_c = col.astype(F32)[:, None] * inv
    cos = jnp.concatenate([jnp.cos(ang_r)] * 2 + [jnp.cos(ang_c)] * 2, axis=1)
    sin = jnp.concatenate([-jnp.sin(ang_r), jnp.sin(ang_r), -jnp.sin(ang_c), jnp.sin(ang_c)], axis=1)
    cos = jnp.concatenate([cos, jnp.ones((geo.l, head_dim), F32)], axis=0)
    sin = jnp.concatenate([sin, jnp.zeros((geo.l, head_dim), F32)], axis=0)
    reps = max(1, LANES // head_dim)
    return jnp.tile(cos, (1, reps)), jnp.tile(sin, (1, reps))


def _row_tile(r):
    return next(t for t in (1024, 512, 256, 128) if r % t == 0)


MOD_ROWS = 8


def _local_step(x, c, ctx, target, sp, wts):
    nb, s, d = x.shape
    geo = _Rows(nb, s, ctx.shape[1])
    assert nb + 1 <= MOD_ROWS and d == D_MODEL
    tm = _row_tile(geo.r)
    z = jnp.concatenate([x, ctx], axis=1).reshape(geo.r, d)
    cvec = jnp.concatenate([c, sp["c_ctx"][None, :], jnp.zeros((MOD_ROWS - nb - 1, d), F32)], axis=0)
    cact, = _whole("cond_silu", lambda v: (_silu(v),), [jax.ShapeDtypeStruct(cvec.shape, F32)], cvec)
    cos64, sin64 = _rope_tables(geo, HEAD_DIM)
    cos256, sin256 = _rope_tables(geo, RET_QK_DIM)
    q_gain = jnp.tile(sp["q_norm"].reshape(1, HEAD_DIM), (1, LANES // HEAD_DIM))
    k_gain = jnp.tile(sp["k_norm"].reshape(1, HEAD_DIM), (1, LANES // HEAD_DIM))
    sink = sp["sink"].reshape(N_HEADS)
    log_g = jnp.broadcast_to(sp["log_g"].reshape(2 * RET_HEADS, 1), (2 * RET_HEADS, LANES))
    gn_g = sp["gn_g"].reshape(1, RET_VWIDTH)

    saved = []
    for i in range(2):
        mod = _mm_nn(cact, wts["ada"][i], F32, f"mod{i}", MOD_ROWS, wts["ada"][i].shape[2], d, bias=sp["ada_b"][i][None, :])
        mod3 = mod[:nb + 1, None, :]
        n1, n2 = sp["norm1_g"][i][None, :], sp["norm2_g"][i][None, :]
        h1 = _norm_mod(geo, z, n1, mod3, 0, f"norm1_{i}")
        if i == 0:
            proj = _mm_nn(h1, wts["attn_qkv"], F32, "attn_qkv", tm, wts["attn_qkv"].shape[2], d)
            prep = _attn_prep(geo, proj, cos64, sin64, q_gain, k_gain, "attn_prep")
            o = _attention(geo, prep, sink, "attn")
            oraw = None
            mix = _mm_nn(o, wts["attn_o"], F32, "attn_out", tm, 1024, 1024)
        else:
            proj = _mm_nn(h1, wts["ret_qkvg"], F32, "ret_qkvg", tm, 512, d)
            prep = _ret_prep(geo, proj, cos256, sin256, "ret_prep")
            oraw = _retention(geo, prep, log_g, "ret")
            o = _ret_gated(geo, oraw, proj, gn_g, "ret_gated")
            mix = _mm_nn(o, wts["ret_o"], F32, "ret_out", tm, 1024, 1024)
        zmid = _gate_residual(geo, z, mix, mod3, 2 * d, f"res_mix{i}")
        h2 = _norm_mod(geo, zmid, n2, mod3, 3 * d, f"norm2_{i}")
        u = _mm_nn(h2, wts["ffn_in"][i], F32, f"ffn_in{i}", tm, wts["ffn_in"][i].shape[2], d)
        a = _swiglu(geo, u, f"swiglu{i}")
        f = _mm_nn(a, wts["ffn_out"][i], F32, f"ffn_out{i}", tm, 1024, D_FF // 2)
        zout = _gate_residual(geo, zmid, f, mod3, 5 * d, f"res_ffn{i}")
        saved.append(dict(z=z, mod3=mod3, n1=n1, n2=n2, h1=h1, proj=proj, prep=prep, o=o, oraw=oraw, mix=mix, zmid=zmid,
                          h2=h2, u=u, a=a, f=f))
        z = zout

    dz, loss = _loss_head(geo, z, target.reshape(nb * s, d), "loss")

    big, small = {}, {}
    dmods = [None, None]
    for i in (1, 0):
        sv = saved[i]
        mod3 = sv["mod3"]
        df, dg2 = _gate_residual_bwd(geo, dz, sv["f"], mod3, 5 * d, f"res_ffn_bwd{i}")
        da = _mm_nt(df, wts["ffn_out"][i], F32, f"ffn_out_dx{i}", tm, D_FF // 2, 1024)
        big[f"ffn_out{i}"] = _mm_tn(sv["a"], df, f"ffn_out_dw{i}", D_FF // 2, 1024, tm).reshape(N_CHIPS, D_FF // N_CHIPS, d)
        du = _swiglu_bwd(geo, sv["u"], da, f"swiglu_bwd{i}")
        n4 = wts["ffn_in"][i].shape[2]
        dh2 = _mm_nt(du, wts["ffn_in"][i], F32, f"ffn_in_dx{i}", tm, 1024, n4)
        big[f"ffn_in{i}"] = _mm_tn(sv["h2"], du, f"ffn_in_dw{i}", 1024, n4, tm, shards=N_CHIPS)
        dzmid, dsh2, dsc2, dn2 = _norm_mod_bwd(geo, sv["zmid"], sv["n2"], mod3, 3 * d, dh2, dz, f"norm2_bwd{i}")
        dmix, dg1 = _gate_residual_bwd(geo, dzmid, sv["mix"], mod3, 2 * d, f"res_mix_bwd{i}")
        if i == 0:
            do = _mm_nt(dmix, wts["attn_o"], BF16, "attn_out_dx", tm, 1024, 1024)
            big["attn_o"] = _mm_tn(sv["o"], dmix, "attn_out_dw", 1024, 1024, tm).reshape(N_CHIPS, 1024 // N_CHIPS, d)
            dq, dkv, dsink = _attention_bwd(geo, sv["prep"], sink, do, "attn_bwd")
            dproj, dqg, dkg = _attn_prep_bwd(geo, sv["proj"], cos64, sin64, q_gain, k_gain, dq, dkv, "attn_prep_bwd")
            small["q_norm"] = dqg[0, :HEAD_DIM] + dqg[0, HEAD_DIM:]
            small["k_norm"] = dkg[0, :HEAD_DIM] + dkg[0, HEAD_DIM:]
            small["sink"] = dsink[:, 0]
            wq = wts["attn_qkv"]
            dh1 = _mm_nt(dproj, wq, F32, "attn_qkv_dx", tm, 1024, wq.shape[2])
            big["attn_qkv"] = _mm_tn(sv["h1"], dproj, "attn_qkv_dw", 1024, wq.shape[2], tm, shards=N_CHIPS)
        else:
            do = _mm_nt(dmix, wts["ret_o"], F32, "ret_out_dx", tm, 1024, 1024)
            big["ret_o"] = _mm_tn(sv["o"], dmix, "ret_out_dw", 1024, 1024, tm).reshape(N_CHIPS, RET_VWIDTH // N_CHIPS, d)
            doraw, dgate, dgn = _ret_gated_bwd(geo, sv["oraw"], sv["proj"], gn_g, do, "ret_gated_bwd")
            small["gn_g"] = dgn[0]
            dq, dk, dv, dlg = _retention_bwd(geo, sv["prep"], log_g, doraw, "ret_bwd")
            small["log_g"] = dlg[:, 0].reshape(2, RET_HEADS)
            dproj = _ret_prep_bwd(geo, dq, dk, dv, dgate, cos256, sin256, "ret_prep_bwd")
            wq = wts["ret_qkvg"]
            dh1 = _mm_nt(dproj, wq, F32, "ret_qkvg_dx", tm, 1024, 512)
            big["ret_qkvg"] = _mm_tn(sv["h1"], dproj, "ret_qkvg_dw", 1024, 512, tm, shards=N_CHIPS)
        dz, dsh1, dsc1, dn1 = _norm_mod_bwd(geo, sv["z"], sv["n1"], mod3, 0, dh1, dzmid, f"norm1_bwd{i}")
        small[f"norm1_g{i}"], small[f"norm2_g{i}"] = dn1[0], dn2[0]
        parts = [dsh1, dsc1, dg1, dsh2, dsc2, dg2]
        rows = jnp.concatenate([jnp.concatenate([p[:nb, 0, :] for p in parts], axis=1),
                                jnp.concatenate([jnp.sum(p[nb:, 0, :], axis=0, keepdims=True) for p in parts], axis=1),
                                jnp.zeros((MOD_ROWS - nb - 1, 6 * d), F32)], axis=0)
        dmods[i] = rows
        small[f"ada_b{i}"] = jnp.sum(rows, axis=0)
        big[f"ada{i}"] = _mm_tn(cact, rows, f"ada_dw{i}", 1024, wts["ada"][i].shape[2], MOD_ROWS, shards=N_CHIPS)

    dcact = [_mm_nt(dmods[i], wts["ada"][i], F32, f"ada_dx{i}", MOD_ROWS, 1024, wts["ada"][i].shape[2]) for i in range(2)]

    def silu_bwd(v, d0, d1):
        sg = _sigmoid(v)
        return ((d0 + d1) * (sg * (1.0 + v * (1.0 - sg))),)

    dcvec, = _whole("cond_silu_bwd", silu_bwd, [jax.ShapeDtypeStruct(cvec.shape, F32)], cvec, dcact[0], dcact[1])
    small["c_ctx"] = dcvec[nb]
    return loss, dz, big, small


def _adamw(w, g, m, v, name):
    rows, cols = w.shape
    tr = next((t for t in (256, 128, 64, 32, 16, 8) if rows % t == 0), rows)
    c1 = 1.0 - ADAM_B1 ** ADAM_STEP
    c2 = 1.0 - ADAM_B2 ** ADAM_STEP

    def kern(w_ref, g_ref, m_ref, v_ref, d_ref, nm_ref, nv_ref):
        gv = g_ref[...]
        nm = ADAM_B1 * m_ref[...] + (1.0 - ADAM_B1) * gv
        nv = ADAM_B2 * v_ref[...] + (1.0 - ADAM_B2) * jnp.square(gv)
        d_ref[...] = -ADAM_LR * ((nm / c1) / (jnp.sqrt(nv / c2) + ADAM_EPS) + ADAM_WD * w_ref[...])
        nm_ref[...] = nm
        nv_ref[...] = nv

    spec = pl.BlockSpec((tr, cols), lambda i: (i, 0))
    return pl.pallas_call(
        kern, name=name, grid=(rows // tr,), in_specs=[spec] * 4, out_specs=[spec] * 3,
        out_shape=[jax.ShapeDtypeStruct(w.shape, F32)] * 3, compiler_params=_cparams("parallel"),
    )(w, g, m, v)


N_DEVICES = 8


def _mesh_pos():
    return lax.axis_index("x"), lax.axis_index("y"), lax.axis_index("c")


def _other_chips(x, y):
    return [(1 - x, y), (x, 1 - y), (1 - x, 1 - y)]


def _hbm(n):
    return [pl.BlockSpec(memory_space=pl.ANY)] * n


def _remote(src, dst, send_sem, recv_sem, device):
    return pltpu.make_async_remote_copy(src_ref=src, dst_ref=dst, send_sem=send_sem, recv_sem=recv_sem,
                                        device_id=device, device_id_type=MESH)


def _scalar_spec(grid, in_specs, out_specs):
    return pltpu.PrefetchScalarGridSpec(num_scalar_prefetch=1, grid=grid, in_specs=in_specs, out_specs=out_specs)


def _place_shard(shard, pos, name):
    r, cols = shard.shape
    tr = _slab_tile(r)

    def kern(pos_ref, s_ref, o_ref):
        o_ref[...] = s_ref[...].astype(BF16)

    return pl.pallas_call(
        kern, name=name, out_shape=jax.ShapeDtypeStruct((N_CHIPS, r, cols), BF16),
        grid_spec=_scalar_spec((r // tr,), [pl.BlockSpec((tr, cols), lambda i, p: (i, 0))],
                               pl.BlockSpec((None, tr, cols), lambda i, p: (p[1], i, 0))),
        compiler_params=_cparams("parallel"),
    )(pos, shard)


def _gather_shards(placed):
    n = len(placed)

    def body(*refs):
        outs = refs[n:2 * n]
        send_sems, recv_sems, fwd_send, fwd_recv = refs[2 * n:]
        x, y, c = _mesh_pos()
        chip = 2 * x + y
        others = _other_chips(x, y)
        sibling = (x, y, 1 - c)

        def half(w, which):
            r2 = placed[w].shape[1] // 2
            return pl.ds(which * r2, r2)

        sends = []
        for w in range(n):
            for k, (px, py) in enumerate(others):
                mine = outs[w].at[chip, half(w, c)]
                cp = _remote(mine, mine, send_sems.at[w, k], recv_sems.at[w, k], (px, py, c))
                cp.start()
                sends.append(cp)
        for w in range(n):
            for k, (px, py) in enumerate(others):
                got = outs[w].at[2 * px + py, half(w, c)]
                _remote(got, got, send_sems.at[w, k], recv_sems.at[w, k], (px, py, c)).wait_recv()
                cp = _remote(got, got, fwd_send.at[w, k], fwd_recv.at[w, k], sibling)
                cp.start()
                sends.append(cp)
        for w in range(n):
            for k, (px, py) in enumerate(others):
                theirs = outs[w].at[2 * px + py, half(w, 1 - c)]
                _remote(theirs, theirs, fwd_send.at[w, k], fwd_recv.at[w, k], sibling).wait_recv()
        for cp in sends:
            cp.wait_send()

    return pl.pallas_call(
        body, name="gather_weights", in_specs=_hbm(n), out_specs=_hbm(n),
        out_shape=[jax.ShapeDtypeStruct(p.shape, p.dtype) for p in placed],
        input_output_aliases={w: w for w in range(n)},
        scratch_shapes=[pltpu.SemaphoreType.DMA((n, 3))] * 4,
    )(*placed)


def _pair_swap(grads):
    n = len(grads)

    def body(*refs):
        ins, land = refs[:n], refs[n:2 * n]
        send_sems, recv_sems = refs[2 * n:]
        x, y, c = _mesh_pos()
        copies = []
        for w in range(n):
            r2 = grads[w].shape[1] // 2
            rc = _remote(ins[w].at[:, pl.ds((1 - c) * r2, r2)], land[w], send_sems.at[w], recv_sems.at[w], (x, y, 1 - c))
            rc.start()
            copies.append(rc)
        for cp in copies:
            cp.wait()

    return pl.pallas_call(
        body, name="grads_pair_swap", in_specs=_hbm(n), out_specs=_hbm(n),
        out_shape=[jax.ShapeDtypeStruct((N_CHIPS, g.shape[1] // 2, g.shape[2]), F32) for g in grads],
        scratch_shapes=[pltpu.SemaphoreType.DMA((n,))] * 2,
    )(*grads)


def _chip_exchange(hs):
    n = len(hs)

    def body(*refs):
        ins, land = refs[:n], refs[n:2 * n]
        send_sems, recv_sems = refs[2 * n:]
        x, y, c = _mesh_pos()
        chip = 2 * x + y
        others = _other_chips(x, y)
        sends = []
        for w in range(n):
            for k, (px, py) in enumerate(others):
                cp = _remote(ins[w].at[2 * px + py], land[w].at[chip], send_sems.at[w, k], recv_sems.at[w, k], (px, py, c))
                cp.start()
                sends.append(cp)
        for w in range(n):
            for k, (px, py) in enumerate(others):
                got = land[w].at[2 * px + py]
                _remote(got, got, send_sems.at[w, k], recv_sems.at[w, k], (px, py, c)).wait_recv()
        for cp in sends:
            cp.wait_send()

    return pl.pallas_call(
        body, name="grads_chip_exchange", in_specs=_hbm(n), out_specs=_hbm(n),
        out_shape=[jax.ShapeDtypeStruct(h.shape, h.dtype) for h in hs],
        scratch_shapes=[pltpu.SemaphoreType.DMA((n, 3))] * 2,
    )(*hs)


def _pair_share(ts):
    n = len(ts)

    def body(*refs):
        outs = refs[n:2 * n]
        send_sems, recv_sems = refs[2 * n:]
        x, y, c = _mesh_pos()
        sends = []
        for w in range(n):
            r2 = ts[w].shape[0] // 2
            mine = outs[w].at[pl.ds(c * r2, r2)]
            rc = _remote(mine, mine, send_sems.at[w], recv_sems.at[w], (x, y, 1 - c))
            rc.start()
            sends.append(rc)
        for w in range(n):
            r2 = ts[w].shape[0] // 2
            theirs = outs[w].at[pl.ds((1 - c) * r2, r2)]
            _remote(theirs, theirs, send_sems.at[w], recv_sems.at[w], (x, y, 1 - c)).wait_recv()
            sends[w].wait_send()

    return pl.pallas_call(
        body, name="grads_pair_share", in_specs=_hbm(n), out_specs=_hbm(n),
        out_shape=[jax.ShapeDtypeStruct(t.shape, F32) for t in ts],
        input_output_aliases={w: w for w in range(n)},
        scratch_shapes=[pltpu.SemaphoreType.DMA((n,))] * 2,
    )(*ts)


def _slab_tile(rows):
    return next(t for t in (512, 256, 176, 128, 64, 32, 16) if rows % t == 0)


def _sum_pair(grad, land, pos, name):
    _, r2, cols = land.shape
    tr = _slab_tile(r2)
    nt = r2 // tr

    def kern(pos_ref, a_ref, b_ref, o_ref):
        o_ref[...] = (a_ref[...] + b_ref[...]).astype(BF16)

    spec = pl.BlockSpec((None, tr, cols), lambda j, i, p: (j, i, 0))
    return pl.pallas_call(
        kern, name=name, out_shape=jax.ShapeDtypeStruct(land.shape, BF16),
        grid_spec=_scalar_spec((N_CHIPS, nt), [pl.BlockSpec((None, tr, cols), lambda j, i, p: (j, p[0] * nt + i, 0)), spec], spec),
        compiler_params=_cparams("parallel", "parallel"),
    )(pos, grad, land)


def _sum_chips(hs, land, pos, name):
    _, r2, cols = land.shape
    tr = _slab_tile(r2)
    nt = r2 // tr

    def kern(pos_ref, h_ref, l_ref, o_ref):
        acc = jnp.zeros((tr, cols), F32)
        own = h_ref[...].astype(F32)
        for k in range(N_CHIPS):
            acc = acc + jnp.where(pos_ref[1] == k, own, l_ref[k].astype(F32))
        o_ref[...] = acc

    return pl.pallas_call(
        kern, name=name, out_shape=jax.ShapeDtypeStruct((2 * r2, cols), F32),
        grid_spec=_scalar_spec((nt,), [pl.BlockSpec((None, tr, cols), lambda i, p: (p[1], i, 0)),
                                       pl.BlockSpec((N_CHIPS, tr, cols), lambda i, p: (0, i, 0))],
                               pl.BlockSpec((tr, cols), lambda i, p: (p[0] * nt + i, 0))),
        compiler_params=_cparams("parallel"),
    )(pos, hs, land)


def _reduce_scatter(grads, pos):
    land = _pair_swap(grads)
    hs = [_sum_pair(g, l, pos, f"grads_pair_sum{w}") for w, (g, l) in enumerate(zip(grads, land))]
    land2 = _chip_exchange(hs)
    ts = [_sum_chips(h, l, pos, f"grads_chip_sum{w}") for w, (h, l) in enumerate(zip(hs, land2))]
    return _pair_share(ts)


def _all_reduce_small(v, name):
    def body(v_ref, o_ref, land_ref, send_sems, recv_sems):
        x, y, c = _mesh_pos()
        me = 4 * x + 2 * y + c
        land_ref[me] = v_ref[...]
        for t in range(N_DEVICES):
            @pl.when(t != me)
            def _(t=t):
                _remote(v_ref, land_ref.at[me], send_sems.at[t], recv_sems.at[me], (t // 4, (t // 2) % 2, t % 2)).start()
        for t in range(N_DEVICES):
            @pl.when(t != me)
            def _(t=t):
                _remote(v_ref, land_ref.at[t], send_sems.at[t], recv_sems.at[t], (t // 4, (t // 2) % 2, t % 2)).wait()
        acc = land_ref[0]
        for t in range(1, N_DEVICES):
            acc = acc + land_ref[t]
        o_ref[...] = acc

    vmem = pl.BlockSpec(memory_space=pltpu.VMEM)
    return pl.pallas_call(
        body, name=name, in_specs=[vmem], out_specs=vmem, out_shape=jax.ShapeDtypeStruct(v.shape, F32),
        scratch_shapes=[pltpu.VMEM((N_DEVICES,) + v.shape, F32), pltpu.SemaphoreType.DMA((N_DEVICES,)),
                        pltpu.SemaphoreType.DMA((N_DEVICES,))],
    )(v)


SMALL_ROWS = 24


def _pack_small(small, dlogit):
    d = D_MODEL
    misc = jnp.zeros((d,), F32)
    misc = misc.at[0:HEAD_DIM].set(small["q_norm"]).at[128:128 + HEAD_DIM].set(small["k_norm"])
    misc = misc.at[256:256 + N_HEADS].set(small["sink"]).at[384:384 + 2 * RET_HEADS].set(dlogit.reshape(-1))
    rows = [small["ada_b0"].reshape(6, d), small["ada_b1"].reshape(6, d), small["norm1_g0"][None], small["norm1_g1"][None],
            small["norm2_g0"][None], small["norm2_g1"][None], small["c_ctx"][None], small["gn_g"].reshape(2, d), misc[None]]
    buf = jnp.concatenate(rows, axis=0)
    return jnp.concatenate([buf, jnp.zeros((SMALL_ROWS - buf.shape[0], d), F32)], axis=0)


def _unpack_small(buf):
    d = D_MODEL
    misc = buf[19]
    return dict(ada_b=buf[0:12].reshape(2, 6 * d), norm1_g=buf[12:14], norm2_g=buf[14:16], c_ctx=buf[16],
                gn_g=buf[17:19].reshape(2 * d), q_norm=misc[0:HEAD_DIM], k_norm=misc[128:128 + HEAD_DIM],
                sink=misc[256:256 + N_HEADS], decay=misc[384:384 + 2 * RET_HEADS])


def kernel(x, c, ctx, c_ctx, ada_w, ada_b, norm1_g, norm2_g, ffn_w_in, ffn_w_out, attn_w_qkv, attn_q_norm, attn_k_norm, attn_sink, attn_w_o, ret_w_qkvg, ret_decay_logit, ret_gn_g, ret_w_o, loss_target, m_c_ctx, m_ada_w, m_ada_b, m_norm1_g, m_norm2_g, m_ffn_w_in, m_ffn_w_out, m_attn_w_qkv, m_attn_q_norm, m_attn_k_norm, m_attn_sink, m_attn_w_o, m_ret_w_qkvg, m_ret_decay_logit, m_ret_gn_g, m_ret_w_o, v_c_ctx, v_ada_w, v_ada_b, v_norm1_g, v_norm2_g, v_ffn_w_in, v_ffn_w_out, v_attn_w_qkv, v_attn_q_norm, v_attn_k_norm, v_attn_sink, v_attn_w_o, v_ret_w_qkvg, v_ret_decay_logit, v_ret_gn_g, v_ret_w_o):
    xi, yi, ci = _mesh_pos()
    chip = 2 * xi + yi
    nb, s, d = x.shape
    gn_shard = ret_gn_g.shape[1]

    shards = dict(ada0=ada_w[0], ada1=ada_w[1], ffn_in0=ffn_w_in[0], ffn_in1=ffn_w_in[1], ffn_out0=ffn_w_out[0],
                  ffn_out1=ffn_w_out[1], attn_qkv=attn_w_qkv[0], attn_o=attn_w_o[0], ret_qkvg=ret_w_qkvg[0], ret_o=ret_w_o[0])
    names = list(shards)
    pos = jnp.stack([ci, chip]).astype(jnp.int32)
    full = dict(zip(names, _gather_shards([_place_shard(shards[k], pos, f"place_{k}") for k in names])))
    gn_mine = jnp.where(ci == 0, ret_gn_g[0], jnp.zeros_like(ret_gn_g[0]))
    gn_place = lax.dynamic_update_slice(jnp.zeros((RET_VWIDTH,), F32), gn_mine, (chip * gn_shard,))
    gn_full = _all_reduce_small(gn_place.reshape(2, d), "gather_gn_gain").reshape(RET_VWIDTH)

    wts = dict(ada=[full["ada0"], full["ada1"]], ffn_in=[full["ffn_in0"], full["ffn_in1"]],
               ffn_out=[full["ffn_out0"].reshape(D_FF, d), full["ffn_out1"].reshape(D_FF, d)],
               attn_qkv=full["attn_qkv"], attn_o=full["attn_o"].reshape(N_HEADS * HEAD_DIM, d),
               ret_qkvg=full["ret_qkvg"], ret_o=full["ret_o"].reshape(RET_VWIDTH, d))
    decay_logit = ret_decay_logit[0]
    sp = dict(c_ctx=c_ctx, ada_b=ada_b, norm1_g=norm1_g, norm2_g=norm2_g, q_norm=attn_q_norm[0], k_norm=attn_k_norm[0],
              sink=attn_sink[0], log_g=jax.nn.log_sigmoid(decay_logit), gn_g=gn_full)
    loss_part, dz, big, small = _local_step(x, c, ctx, loss_target, sp, wts)

    loss = lax.psum(loss_part[0, 0], ("x", "y", "c"))
    grad_x = dz.reshape(nb, -1, d)[:, :s]

    dlogit = small["log_g"] * jax.nn.sigmoid(-decay_logit)
    sg = _unpack_small(_all_reduce_small(_pack_small(small, dlogit), "reduce_small_grads"))
    reduced = dict(zip(names, _reduce_scatter([big[k] for k in names], pos)))

    grads = dict(
        c_ctx=sg["c_ctx"], ada_w=jnp.stack([reduced["ada0"], reduced["ada1"]]), ada_b=sg["ada_b"], norm1_g=sg["norm1_g"],
        norm2_g=sg["norm2_g"], ffn_w_in=jnp.stack([reduced["ffn_in0"], reduced["ffn_in1"]]),
        ffn_w_out=jnp.stack([reduced["ffn_out0"], reduced["ffn_out1"]]), attn_w_qkv=reduced["attn_qkv"][None],
        attn_q_norm=sg["q_norm"][None], attn_k_norm=sg["k_norm"][None], attn_sink=sg["sink"][None],
        attn_w_o=reduced["attn_o"][None], ret_w_qkvg=reduced["ret_qkvg"][None], ret_decay_logit=sg["decay"].reshape(1, 2, RET_HEADS),
        ret_gn_g=lax.dynamic_slice(sg["gn_g"], (chip * gn_shard,), (gn_shard,))[None], ret_w_o=reduced["ret_o"][None])
    params = dict(c_ctx=(c_ctx, m_c_ctx, v_c_ctx), ada_w=(ada_w, m_ada_w, v_ada_w), ada_b=(ada_b, m_ada_b, v_ada_b),
                  norm1_g=(norm1_g, m_norm1_g, v_norm1_g), norm2_g=(norm2_g, m_norm2_g, v_norm2_g),
                  ffn_w_in=(ffn_w_in, m_ffn_w_in, v_ffn_w_in), ffn_w_out=(ffn_w_out, m_ffn_w_out, v_ffn_w_out),
                  attn_w_qkv=(attn_w_qkv, m_attn_w_qkv, v_attn_w_qkv), attn_q_norm=(attn_q_norm, m_attn_q_norm, v_attn_q_norm),
                  attn_k_norm=(attn_k_norm, m_attn_k_norm, v_attn_k_norm), attn_sink=(attn_sink, m_attn_sink, v_attn_sink),
                  attn_w_o=(attn_w_o, m_attn_w_o, v_attn_w_o), ret_w_qkvg=(ret_w_qkvg, m_ret_w_qkvg, v_ret_w_qkvg),
                  ret_decay_logit=(ret_decay_logit, m_ret_decay_logit, v_ret_decay_logit),
                  ret_gn_g=(ret_gn_g, m_ret_gn_g, v_ret_gn_g), ret_w_o=(ret_w_o, m_ret_w_o, v_ret_w_o))
    order = list(params)
    deltas, new_m, new_v = [], [], []
    for k in order:
        w, m, v = params[k]
        g = grads[k].reshape(w.shape)
        grads[k] = g
        flat = (-1, w.shape[-1]) if w.ndim > 1 else (1, -1)
        if k == "ret_decay_logit":
            flat = (1, -1)
        dw, nm, nv = _adamw(w.reshape(flat), g.reshape(flat), m.reshape(flat), v.reshape(flat), f"adamw_{k}")
        deltas.append(dw.reshape(w.shape))
        new_m.append(nm.reshape(w.shape))
        new_v.append(nv.reshape(w.shape))
    return (loss, grad_x, *[grads[k] for k in order], *deltas, *new_m, *new_v)
```

```python
import functools

import jax
import jax.numpy as jnp
from jax import lax
from jax.experimental import pallas as pl
from jax.experimental.pallas import tpu as pltpu

F32 = jnp.float32
BF16 = jnp.bfloat16

D_MODEL = 1024
N_HEADS = 16
N_KV_HEADS = 4
HEAD_DIM = 64
WINDOW = 128
ATTN_BLOCK = 128
BAND = ATTN_BLOCK + 2 * WINDOW
RET_HEADS = 4
RET_QK_DIM = 256
RET_V_DIM = 512
RET_VWIDTH = 2048
RET_CHUNK = 128
D_FF = 2816
GRID_W = 64
ROPE_BASE = 10000.0
EPS = 1e-6
NEG_INF = -1e30
LANES = 128

ADAM_LR = 0.001
ADAM_B1 = 0.9
ADAM_B2 = 0.999
ADAM_EPS = 1e-08
ADAM_WD = 0.01
ADAM_STEP = 10

VMEM_LIMIT_BYTES = 56 * 1024 * 1024
MESH = pl.DeviceIdType.MESH
N_CHIPS = 4


def _cparams(*sem):
    return pltpu.CompilerParams(dimension_semantics=sem, vmem_limit_bytes=VMEM_LIMIT_BYTES)


_DIMS = {"nn": ((1,), (0,)), "nt": ((1,), (1,)), "tn": ((0,), (0,))}


def _dot(a, b, form):
    return lax.dot_general(a.astype(BF16), b.astype(BF16), (_DIMS[form], ((), ())), preferred_element_type=F32)


@functools.partial(jax.custom_vjp, nondiff_argnums=(2,))
def _mm(a, b, form):
    return _dot(a, b, form)


def _mm_fwd(a, b, form):
    return _dot(a, b, form), (a, b)


def _mm_bwd(form, res, ct):
    a, b = res
    if form == "nn":
        da, db = _dot(ct, b, "nt"), _dot(a, ct, "tn")
    elif form == "nt":
        da, db = _dot(ct, b, "nn"), _dot(ct, a, "tn")
    else:
        da, db = _dot(b, ct, "nt"), _dot(a, ct, "nn")
    return da.astype(a.dtype), db.astype(b.dtype)


_mm.defvjp(_mm_fwd, _mm_bwd)


def _swap_halves(x, half):
    w = x.shape[-1]
    lane = lax.broadcasted_iota(jnp.int32, x.shape, x.ndim - 1)
    return jnp.where(lane % (2 * half) < half, pltpu.roll(x, w - half, x.ndim - 1), pltpu.roll(x, half, x.ndim - 1))


@functools.partial(jax.custom_vjp, nondiff_argnums=(1,))
def _rot(x, half):
    return _swap_halves(x, half)


def _rot_fwd(x, half):
    return _swap_halves(x, half), None


def _rot_bwd(half, _, ct):
    return (_swap_halves(ct, half),)


_rot.defvjp(_rot_fwd, _rot_bwd)


def _rope(x, cos, sin_signed, half):
    return x * cos + _rot(x, half) * sin_signed


def _head_mean_square(x):
    r = lax.broadcasted_iota(jnp.int32, (LANES, LANES), 0) // HEAD_DIM
    c = lax.broadcasted_iota(jnp.int32, (LANES, LANES), 1) // HEAD_DIM
    g = jnp.where(r == c, 1.0 / HEAD_DIM, 0.0).astype(F32)
    return jnp.dot(x * x, g, precision=lax.Precision.HIGHEST, preferred_element_type=F32)


def _qk_chunk(x, gain, cos, sin_signed, scale):
    y = x * lax.rsqrt(_head_mean_square(x) + EPS) * gain
    return _rope(y, cos, sin_signed, HEAD_DIM // 4) * scale


def _sigmoid(x):
    return 1.0 / (1.0 + jnp.exp(-x))


def _silu(x):
    return x * _sigmoid(x)


def _mm_nn(a, w, out_dtype, name, tm, tn, tk, bias=None):
    m, k_dim = a.shape
    if w.ndim == 3:
        n = w.shape[0] * w.shape[2]
        per = w.shape[2] // tn
        assert w.shape[2] % tn == 0
        w_spec = pl.BlockSpec((None, tk, tn), lambda i, j, k: (j // per, k, j % per))
    else:
        n = w.shape[1]
        w_spec = pl.BlockSpec((tk, tn), lambda i, j, k: (k, j))
    assert m % tm == 0 and n % tn == 0 and k_dim % tk == 0, (name, a.shape, w.shape, tm, tn, tk)
    nk = k_dim // tk
    has_bias = bias is not None

    def body(*refs):
        a_ref, w_ref = refs[0], refs[1]
        b_ref = refs[2] if has_bias else None
        o_ref, acc_ref = refs[-2], refs[-1]
        k = pl.program_id(2)

        @pl.when(k == 0)
        def _():
            acc_ref[...] = jnp.zeros_like(acc_ref)

        acc_ref[...] += jnp.dot(a_ref[...].astype(BF16), w_ref[...], preferred_element_type=F32)

        @pl.when(k == nk - 1)
        def _():
            r = acc_ref[...]
            if has_bias:
                r = r + b_ref[...]
            o_ref[...] = r.astype(out_dtype)

    in_specs = [pl.BlockSpec((tm, tk), lambda i, j, k: (i, k)), w_spec]
    args = [a, w]
    if has_bias:
        in_specs.append(pl.BlockSpec((1, tn), lambda i, j, k: (0, j)))
        args.append(bias)
    return pl.pallas_call(
        body, name=name, grid=(m // tm, n // tn, nk), in_specs=in_specs,
        out_specs=pl.BlockSpec((tm, tn), lambda i, j, k: (i, j)),
        out_shape=jax.ShapeDtypeStruct((m, n), out_dtype),
        scratch_shapes=[pltpu.VMEM((tm, tn), F32)],
        compiler_params=_cparams("parallel", "parallel", "arbitrary"),
    )(*args)


def _mm_nt(a, w, out_dtype, name, tm, tn, tk):
    m, c_dim = a.shape
    if w.ndim == 3:
        k_out = w.shape[1]
        per = w.shape[2] // tk
        assert w.shape[2] % tk == 0 and w.shape[0] * w.shape[2] == c_dim
        w_spec = pl.BlockSpec((None, tn, tk), lambda i, j, k: (k // per, j, k % per))
    else:
        k_out = w.shape[0]
        assert w.shape[1] == c_dim
        w_spec = pl.BlockSpec((tn, tk), lambda i, j, k: (j, k))
    assert m % tm == 0 and k_out % tn == 0 and c_dim % tk == 0, (name, a.shape, w.shape, tm, tn, tk)
    nk = c_dim // tk

    def body(a_ref, w_ref, o_ref, acc_ref):
        k = pl.program_id(2)

        @pl.when(k == 0)
        def _():
            acc_ref[...] = jnp.zeros_like(acc_ref)

        acc_ref[...] += _dot(a_ref[...], w_ref[...], "nt")

        @pl.when(k == nk - 1)
        def _():
            o_ref[...] = acc_ref[...].astype(out_dtype)

    return pl.pallas_call(
        body, name=name, grid=(m // tm, k_out // tn, nk),
        in_specs=[pl.BlockSpec((tm, tk), lambda i, j, k: (i, k)), w_spec],
        out_specs=pl.BlockSpec((tm, tn), lambda i, j, k: (i, j)),
        out_shape=jax.ShapeDtypeStruct((m, k_out), out_dtype),
        scratch_shapes=[pltpu.VMEM((tm, tn), F32)],
        compiler_params=_cparams("parallel", "parallel", "arbitrary"),
    )(a, w)


def _mm_tn(a, b, name, tm, tn, tk, shards=None):
    r, k_dim = a.shape
    n = b.shape[1]
    assert r % tk == 0 and k_dim % tm == 0 and n % tn == 0, (name, a.shape, b.shape, tm, tn, tk)
    nk = r // tk
    if shards:
        per = n // shards // tn
        assert n % (shards * tn) == 0
        out_shape = jax.ShapeDtypeStruct((shards, k_dim, n // shards), F32)
        out_spec = pl.BlockSpec((None, tm, tn), lambda i, j, k: (j // per, i, j % per))
    else:
        out_shape = jax.ShapeDtypeStruct((k_dim, n), F32)
        out_spec = pl.BlockSpec((tm, tn), lambda i, j, k: (i, j))

    def body(a_ref, b_ref, o_ref):
        k = pl.program_id(2)

        @pl.when(k == 0)
        def _():
            o_ref[...] = jnp.zeros_like(o_ref)

        o_ref[...] += _dot(a_ref[...], b_ref[...], "tn")

    return pl.pallas_call(
        body, name=name, grid=(k_dim // tm, n // tn, nk),
        in_specs=[pl.BlockSpec((tk, tm), lambda i, j, k: (k, i)), pl.BlockSpec((tk, tn), lambda i, j, k: (k, j))],
        out_specs=out_spec, out_shape=out_shape,
        compiler_params=_cparams("parallel", "parallel", "arbitrary"),
    )(a, b)


class _Rows:
    def __init__(self, b, s, l):
        self.b, self.s, self.l = b, s, l
        self.seg = s + l
        self.r = b * self.seg


def _rowwise(name, body, geo, tm, ins, outs):
    seg_blocks, x_blocks = geo.seg // tm, geo.s // tm
    assert geo.seg % tm == 0 and geo.s % tm == 0
    nb = geo.b

    def is_ctx(i):
        return i % seg_blocks >= x_blocks

    in_specs, args = [], []
    for arr, kind in ins:
        args.append(arr)
        if kind == "row":
            in_specs.append(pl.BlockSpec((tm, arr.shape[1]), lambda i: (i, 0)))
        elif kind == "ex":
            in_specs.append(pl.BlockSpec((None, 1, arr.shape[2]), lambda i: (jnp.where(is_ctx(i), nb, i // seg_blocks), 0, 0)))
        elif kind == "full":
            in_specs.append(pl.BlockSpec(arr.shape, lambda i, nd=arr.ndim: (0,) * nd))
        elif kind == "tab":
            in_specs.append(pl.BlockSpec((tm, arr.shape[1]), lambda i: (i % seg_blocks, 0)))
        elif kind == "xrow":
            in_specs.append(pl.BlockSpec(
                (tm, arr.shape[1]), lambda i: ((i // seg_blocks) * x_blocks + jnp.minimum(i % seg_blocks, x_blocks - 1), 0)))
        else:
            _, width, cb = kind
            in_specs.append(pl.BlockSpec((tm, width), lambda i, cb=cb: (i, cb)))
    out_specs, out_shapes = [], []
    for o in outs:
        if o[0] == "row":
            out_specs.append(pl.BlockSpec((tm, o[1]), lambda i: (i, 0)))
            out_shapes.append(jax.ShapeDtypeStruct((geo.r, o[1]), o[2]))
        elif o[0] == "exacc":
            out_specs.append(pl.BlockSpec((None, 1, o[1]), lambda i: (jnp.where(is_ctx(i), nb, 0) + i // seg_blocks, 0, 0)))
            out_shapes.append(jax.ShapeDtypeStruct((2 * nb, 1, o[1]), F32))
        else:
            out_specs.append(pl.BlockSpec((o[1], o[2]), lambda i: (0, 0)))
            out_shapes.append(jax.ShapeDtypeStruct((o[1], o[2]), F32))
    n_in = len(ins)

    def kern(*refs):
        i = pl.program_id(0)
        res = body(i, *[r[...].astype(F32) for r in refs[:n_in]])
        if not isinstance(res, (tuple, list)):
            res = (res,)
        jj = i % seg_blocks
        first_of_part = (jj == 0) | (jj == x_blocks)
        for o, ref, val in zip(outs, refs[n_in:], res):
            if o[0] == "row":
                ref[...] = val.astype(ref.dtype)
            else:
                first = first_of_part if o[0] == "exacc" else i == 0

                @pl.when(first)
                def _(ref=ref, val=val):
                    ref[...] = val

                @pl.when(jnp.logical_not(first))
                def _(ref=ref, val=val):
                    ref[...] += val

    res = pl.pallas_call(
        kern, name=name, grid=(geo.r // tm,), in_specs=in_specs, out_specs=out_specs, out_shape=out_shapes,
        compiler_params=_cparams("arbitrary"),
    )(*args)
    return res[0] if len(res) == 1 else res


def _colsum(v):
    return jnp.sum(v, axis=0, keepdims=True)


def _norm_mod(geo, z, gain, mod, off, name):
    d = D_MODEL

    def body(i, zv, g, m):
        r = lax.rsqrt(jnp.mean(zv * zv, axis=-1, keepdims=True) + EPS)
        return (zv * r) * g * (1.0 + m[:, off + d:off + 2 * d]) + m[:, off:off + d]

    return _rowwise(name, body, geo, 256, [(z, "row"), (gain, "full"), (mod, "ex")], [("row", d, BF16)])


def _norm_mod_bwd(geo, z, gain, mod, off, dh, dz_skip, name):
    d = D_MODEL

    def body(i, zv, g, m, dhv, skip):
        r = lax.rsqrt(jnp.mean(zv * zv, axis=-1, keepdims=True) + EPS)
        n = zv * r
        dng = dhv * (1.0 + m[:, off + d:off + 2 * d])
        dn = dng * g
        dz = r * (dn - n * jnp.mean(dn * n, axis=-1, keepdims=True)) + skip
        return dz, _colsum(dhv), _colsum(dhv * (n * g)), _colsum(dng * n)

    return _rowwise(name, body, geo, 256, [(z, "row"), (gain, "full"), (mod, "ex"), (dh, "row"), (dz_skip, "row")],
                    [("row", d, F32), ("exacc", d), ("exacc", d), ("gacc", 1, d)])


def _gate_residual(geo, z, out, mod, off, name):
    d = D_MODEL

    def body(i, zv, ov, m):
        return zv + m[:, off:off + d] * ov

    return _rowwise(name, body, geo, 256, [(z, "row"), (out, "row"), (mod, "ex")], [("row", d, F32)])


def _gate_residual_bwd(geo, dz, out, mod, off, name):
    d = D_MODEL

    def body(i, dzv, ov, m):
        return dzv * m[:, off:off + d], _colsum(dzv * ov)

    return _rowwise(name, body, geo, 256, [(dz, "row"), (out, "row"), (mod, "ex")], [("row", d, BF16), ("exacc", d)])


def _swiglu(geo, u, name):
    def body(i, uv):
        return _silu(uv[:, :D_FF]) * uv[:, D_FF:]

    return _rowwise(name, body, geo, 128, [(u, "row")], [("row", D_FF, BF16)])


def _swiglu_bwd(geo, u, da, name):
    def body(i, uv, dav):
        g, up = uv[:, :D_FF], uv[:, D_FF:]
        s = _sigmoid(g)
        return jnp.concatenate([dav * up * (s * (1.0 + g * (1.0 - s))), dav * (g * s)], axis=1)

    return _rowwise(name, body, geo, 128, [(u, "row"), (da, "row")], [("row", 2 * D_FF, BF16)])


def _loss_head(geo, z, target, name):
    seg_blocks, x_blocks = geo.seg // 256, geo.s // 256

    def body(i, zv, tv):
        keep = jnp.where(i % seg_blocks >= x_blocks, 0.0, 1.0)
        err = (zv - tv) * keep
        part = 0.5 * jnp.sum(jnp.mean(err * err, axis=-1, keepdims=True), axis=0, keepdims=True)
        return err * (1.0 / D_MODEL), jnp.broadcast_to(part, (1, LANES))

    return _rowwise(name, body, geo, 256, [(z, "row"), (target, "xrow")], [("row", D_MODEL, F32), ("gacc", 1, LANES)])


Q_SCALE = HEAD_DIM ** -0.5
N_QK_CHUNKS = (N_HEADS + N_KV_HEADS) * HEAD_DIM // LANES
N_Q_CHUNKS = N_HEADS * HEAD_DIM // LANES


def _attn_prep(geo, proj, cos, sin_signed, q_gain, k_gain, name):
    def body(i, p, cs, sn, qg, kg):
        outs = []
        for ch in range(N_QK_CHUNKS):
            is_q = ch < N_Q_CHUNKS
            outs.append(_qk_chunk(p[:, ch * LANES:(ch + 1) * LANES], qg if is_q else kg, cs, sn, Q_SCALE if is_q else 1.0))
        outs.append(p[:, N_QK_CHUNKS * LANES:])
        return jnp.concatenate(outs, axis=1)

    return _rowwise(name, body, geo, 256, [(proj, "row"), (cos, "tab"), (sin_signed, "tab"), (q_gain, "full"), (k_gain, "full")],
                    [("row", proj.shape[1], BF16)])


def _attn_prep_bwd(geo, proj, cos, sin_signed, q_gain, k_gain, dq, dkv, name):
    kw = N_KV_HEADS * HEAD_DIM

    def body(i, p, cs, sn, qg, kg, dqv, dkvv):
        outs = []
        dgains = [jnp.zeros((1, LANES), F32), jnp.zeros((1, LANES), F32)]
        for ch in range(N_QK_CHUNKS):
            is_q = ch < N_Q_CHUNKS
            scale = Q_SCALE if is_q else 1.0
            ct = dqv[:, ch * LANES:(ch + 1) * LANES] if is_q else dkvv[:, (ch - N_Q_CHUNKS) * LANES:(ch - N_Q_CHUNKS + 1) * LANES]
            _, vjp = jax.vjp(lambda xx, gg, scale=scale: _qk_chunk(xx, gg, cs, sn, scale),
                             p[:, ch * LANES:(ch + 1) * LANES], qg if is_q else kg)
            dx, dg = vjp(ct)
            outs.append(dx)
            dgains[0 if is_q else 1] = dgains[0 if is_q else 1] + dg
        outs.append(dkvv[:, kw:])
        return jnp.concatenate(outs, axis=1), dgains[0], dgains[1]

    return _rowwise(name, body, geo, 256,
                    [(proj, "row"), (cos, "tab"), (sin_signed, "tab"), (q_gain, "full"), (k_gain, "full"), (dq, "row"), (dkv, "row")],
                    [("row", proj.shape[1], BF16), ("gacc", 1, LANES), ("gacc", 1, LANES)])


def _attn_geometry(geo):
    assert geo.s % ATTN_BLOCK == 0 and geo.l % ATTN_BLOCK == 0 and geo.seg >= BAND
    return geo.seg // ATTN_BLOCK, geo.s // ATTN_BLOCK


def _attn_mask(j, s0, geo, n_x_blocks):
    r = lax.broadcasted_iota(jnp.int32, (ATTN_BLOCK, BAND), 0)
    n = lax.broadcasted_iota(jnp.int32, (ATTN_BLOCK, BAND), 1)
    dist = (s0 - j * ATTN_BLOCK) + n - r
    return (jnp.abs(dist) <= WINDOW) & (s0 + n < geo.s) & (j < n_x_blocks)


def _attn_probs(q, kc, kb, valid, sink):
    s_ctx = _dot(q, kc, "nt")
    s_loc = jnp.where(valid, _dot(q, kb, "nt"), NEG_INF)
    m = jnp.maximum(jnp.maximum(jnp.max(s_ctx, axis=-1, keepdims=True), jnp.max(s_loc, axis=-1, keepdims=True)), sink)
    e_ctx, e_loc, e_sink = jnp.exp(s_ctx - m), jnp.exp(s_loc - m), jnp.exp(sink - m)
    inv = 1.0 / (jnp.sum(e_ctx, axis=-1, keepdims=True) + jnp.sum(e_loc, axis=-1, keepdims=True) + e_sink)
    return e_ctx * inv, e_loc * inv, e_sink * inv


def _attention(geo, qkv, sink, name):
    n_blocks, n_x_blocks = _attn_geometry(geo)
    qw, kw = N_HEADS * HEAD_DIM, N_KV_HEADS * HEAD_DIM
    group = N_HEADS // N_KV_HEADS

    def kern(sink_ref, q_ref, k_ref, v_ref, o_ref):
        j = pl.program_id(1)
        s0 = pl.multiple_of(jnp.clip((j - 1) * ATTN_BLOCK, 0, geo.seg - BAND), ATTN_BLOCK)
        valid = _attn_mask(j, s0, geo, n_x_blocks)
        kb_all, vb_all = k_ref[pl.ds(s0, BAND), :], v_ref[pl.ds(s0, BAND), :]
        kc_all, vc_all = k_ref[geo.s:geo.seg, :], v_ref[geo.s:geo.seg, :]
        for h in range(N_HEADS):
            kv = slice((h // group) * HEAD_DIM, (h // group + 1) * HEAD_DIM)
            q = q_ref[:, h * HEAD_DIM:(h + 1) * HEAD_DIM]
            p_ctx, p_loc, _ = _attn_probs(q, kc_all[:, kv], kb_all[:, kv], valid, sink_ref[h])
            o = _dot(p_ctx, vc_all[:, kv], "nn") + _dot(p_loc, vb_all[:, kv], "nn")
            o_ref[:, h * HEAD_DIM:(h + 1) * HEAD_DIM] = o.astype(BF16)

    return pl.pallas_call(
        kern, name=name, grid=(geo.b, n_blocks),
        in_specs=[pl.BlockSpec(memory_space=pltpu.SMEM),
                  pl.BlockSpec((ATTN_BLOCK, qw), lambda b, j: (b * n_blocks + j, 0)),
                  pl.BlockSpec((geo.seg, kw), lambda b, j: (b, qw // kw)),
                  pl.BlockSpec((geo.seg, kw), lambda b, j: (b, qw // kw + 1))],
        out_specs=pl.BlockSpec((ATTN_BLOCK, qw), lambda b, j: (b * n_blocks + j, 0)),
        out_shape=jax.ShapeDtypeStruct((geo.r, qw), BF16),
        compiler_params=_cparams("parallel", "arbitrary"),
    )(sink, qkv, qkv, qkv)


def _attention_bwd(geo, qkv, sink, do, name):
    n_blocks, n_x_blocks = _attn_geometry(geo)
    qw, kw = N_HEADS * HEAD_DIM, N_KV_HEADS * HEAD_DIM
    group = N_HEADS // N_KV_HEADS

    def kern(sink_ref, q_ref, k_ref, v_ref, do_ref, dq_ref, dkv_ref, dsink_ref):
        b, j = pl.program_id(0), pl.program_id(1)
        s0 = pl.multiple_of(jnp.clip((j - 1) * ATTN_BLOCK, 0, geo.seg - BAND), ATTN_BLOCK)
        valid = _attn_mask(j, s0, geo, n_x_blocks)

        @pl.when(j == 0)
        def _():
            dkv_ref[...] = jnp.zeros_like(dkv_ref)

        @pl.when((j == 0) & (b == 0))
        def _():
            dsink_ref[...] = jnp.zeros_like(dsink_ref)

        kb_all, vb_all = k_ref[pl.ds(s0, BAND), :], v_ref[pl.ds(s0, BAND), :]
        kc_all, vc_all = k_ref[geo.s:geo.seg, :], v_ref[geo.s:geo.seg, :]
        for g in range(N_KV_HEADS):
            kv = slice(g * HEAD_DIM, (g + 1) * HEAD_DIM)
            kc, kb, vc, vb = kc_all[:, kv], kb_all[:, kv], vc_all[:, kv], vb_all[:, kv]
            dkc = jnp.zeros((geo.l, HEAD_DIM), F32)
            dvc = jnp.zeros((geo.l, HEAD_DIM), F32)
            dkb = jnp.zeros((BAND, HEAD_DIM), F32)
            dvb = jnp.zeros((BAND, HEAD_DIM), F32)
            for h in range(g * group, (g + 1) * group):
                hs = slice(h * HEAD_DIM, (h + 1) * HEAD_DIM)
                q, dout = q_ref[:, hs], do_ref[:, hs]
                p_ctx, p_loc, p_sink = _attn_probs(q, kc, kb, valid, sink_ref[h])
                dp_ctx, dp_loc = _dot(dout, vc, "nt"), _dot(dout, vb, "nt")
                dsum = jnp.sum(p_ctx * dp_ctx, axis=-1, keepdims=True) + jnp.sum(p_loc * dp_loc, axis=-1, keepdims=True)
                ds_ctx, ds_loc = p_ctx * (dp_ctx - dsum), p_loc * (dp_loc - dsum)
                dq_ref[:, hs] = _dot(ds_ctx, kc, "nn") + _dot(ds_loc, kb, "nn")
                dkc += _dot(ds_ctx, q, "tn")
                dkb += _dot(ds_loc, q, "tn")
                dvc += _dot(p_ctx, dout, "tn")
                dvb += _dot(p_loc, dout, "tn")
                dsink_ref[h:h + 1, :] += jnp.broadcast_to(-jnp.sum(p_sink * dsum, axis=0, keepdims=True), (1, LANES))
            vv = slice(kw + g * HEAD_DIM, kw + (g + 1) * HEAD_DIM)
            dkv_ref[pl.ds(s0, BAND), kv] += dkb
            dkv_ref[pl.ds(s0, BAND), vv] += dvb
            dkv_ref[geo.s:geo.seg, kv] += dkc
            dkv_ref[geo.s:geo.seg, vv] += dvc

    return pl.pallas_call(
        kern, name=name, grid=(geo.b, n_blocks),
        in_specs=[pl.BlockSpec(memory_space=pltpu.SMEM),
                  pl.BlockSpec((ATTN_BLOCK, qw), lambda b, j: (b * n_blocks + j, 0)),
                  pl.BlockSpec((geo.seg, kw), lambda b, j: (b, qw // kw)),
                  pl.BlockSpec((geo.seg, kw), lambda b, j: (b, qw // kw + 1)),
                  pl.BlockSpec((ATTN_BLOCK, qw), lambda b, j: (b * n_blocks + j, 0))],
        out_specs=[pl.BlockSpec((ATTN_BLOCK, qw), lambda b, j: (b * n_blocks + j, 0)),
                   pl.BlockSpec((geo.seg, 2 * kw), lambda b, j: (b, 0)),
                   pl.BlockSpec((N_HEADS, LANES), lambda b, j: (0, 0))],
        out_shape=[jax.ShapeDtypeStruct((geo.r, qw), F32), jax.ShapeDtypeStruct((geo.r, 2 * kw), F32),
                   jax.ShapeDtypeStruct((N_HEADS, LANES), F32)],
        compiler_params=_cparams("arbitrary", "arbitrary"),
    )(sink, qkv, qkv, qkv, do)


RET_QK_W = RET_HEADS * RET_QK_DIM
K_SCALE = RET_QK_DIM ** -0.5


def _ret_prep(geo, proj, cos, sin_signed, name):
    def body(i, p, cs, sn):
        cs2, sn2 = jnp.concatenate([cs] * RET_HEADS, axis=1), jnp.concatenate([sn] * RET_HEADS, axis=1)
        q = _rope(p[:, :RET_QK_W], cs2, sn2, RET_QK_DIM // 4)
        k = _rope(p[:, RET_QK_W:2 * RET_QK_W], cs2, sn2, RET_QK_DIM // 4) * K_SCALE
        return jnp.concatenate([q, k, p[:, 2 * RET_QK_W:]], axis=1)

    return _rowwise(name, body, geo, 128, [(proj, ("rowc", 2 * RET_QK_W + RET_VWIDTH, 0)), (cos, "tab"), (sin_signed, "tab")],
                    [("row", 2 * RET_QK_W + RET_VWIDTH, BF16)])


def _ret_prep_bwd(geo, dq, dk, dv, dgate, cos, sin_signed, name):
    def body(i, dqv, dkv, dvv, dg, cs, sn):
        cs2, sn2 = jnp.concatenate([cs] * RET_HEADS, axis=1), jnp.concatenate([sn] * RET_HEADS, axis=1)
        dkv = dkv * K_SCALE
        dqv = dqv * cs2 + _swap_halves(dqv * sn2, RET_QK_DIM // 4)
        dkv = dkv * cs2 + _swap_halves(dkv * sn2, RET_QK_DIM // 4)
        return jnp.concatenate([dqv, dkv, dvv, dg], axis=1)

    return _rowwise(name, body, geo, 128,
                    [(dq, "row"), (dk, "row"), (dv, "row"), (dgate, "row"), (cos, "tab"), (sin_signed, "tab")],
                    [("row", 2 * RET_QK_W + 2 * RET_VWIDTH, BF16)])


def _ret_step(state, q, k, v, lg, rev):
    c = RET_CHUNK
    ri = lax.broadcasted_iota(jnp.int32, (c, 1), 0).astype(F32)
    cj = lax.broadcasted_iota(jnp.int32, (1, c), 1).astype(F32)
    if rev:
        dist, q_decay, k_decay = cj - ri, jnp.exp(lg * (c - ri)), jnp.exp(lg * ri)
    else:
        dist, q_decay, k_decay = ri - cj, jnp.exp(lg * (ri + 1.0)), jnp.exp(lg * (c - 1.0 - ri))
    intra = jnp.where(dist >= 0, jnp.exp(lg * jnp.maximum(dist, 0.0)), 0.0)
    scores = _mm(q, k, "nt") * intra
    out = _mm(scores, v, "nn") + _mm(q, state, "nn") * q_decay
    new_state = state * jnp.exp(lg * c) + _mm(k * k_decay, v, "tn")
    return new_state, out


def _ret_state0(kc, vc, lg, rev):
    n = kc.shape[0]
    t = lax.broadcasted_iota(jnp.int32, (n, 1), 0).astype(F32)
    decay = jnp.exp(lg * t) if rev else jnp.exp(lg * (n - 1.0 - t))
    return _mm(kc * decay, vc, "tn")


def _ret_specs(geo):
    nq = RET_HEADS
    return [pl.BlockSpec((2 * RET_HEADS, LANES), lambda b, h: (0, 0)),
            pl.BlockSpec((geo.seg, RET_QK_DIM), lambda b, h: (b, h)),
            pl.BlockSpec((geo.seg, RET_QK_DIM), lambda b, h: (b, nq + h)),
            pl.BlockSpec((geo.seg, RET_V_DIM), lambda b, h: (b, nq + h))]


def _retention(geo, qkv, log_g, name):
    nc = geo.s // RET_CHUNK

    def kern(lg_ref, q_ref, k_ref, v_ref, o_ref, st_ref):
        h = pl.program_id(1)
        for d, rev in ((0, False), (1, True)):
            lg = lg_ref[pl.ds(d * RET_HEADS + h, 1), 0:1]
            st_ref[...] = _ret_state0(k_ref[geo.s:geo.seg, :].astype(F32), v_ref[geo.s:geo.seg, :].astype(F32), lg, rev)

            def chunk(ci, carry, d=d, rev=rev, lg=lg):
                r0 = pl.multiple_of((nc - 1 - ci if rev else ci) * RET_CHUNK, RET_CHUNK)
                rows = pl.ds(r0, RET_CHUNK)
                new_state, out = _ret_step(st_ref[...], q_ref[rows, :].astype(F32), k_ref[rows, :].astype(F32),
                                           v_ref[rows, :].astype(F32), lg, rev)
                st_ref[...] = new_state
                if d == 0:
                    o_ref[rows, :] = out
                else:
                    o_ref[rows, :] += out
                return carry

            lax.fori_loop(0, nc, chunk, 0)
        o_ref[geo.s:geo.seg, :] = jnp.zeros((geo.l, RET_V_DIM), F32)

    return pl.pallas_call(
        kern, name=name, grid=(geo.b, RET_HEADS), in_specs=_ret_specs(geo),
        out_specs=pl.BlockSpec((geo.seg, RET_V_DIM), lambda b, h: (b, h)),
        out_shape=jax.ShapeDtypeStruct((geo.r, RET_VWIDTH), F32),
        scratch_shapes=[pltpu.VMEM((RET_QK_DIM, RET_V_DIM), F32)],
        compiler_params=_cparams("parallel", "arbitrary"),
    )(log_g, qkv, qkv, qkv)


def _retention_bwd(geo, qkv, log_g, do, name):
    nc = geo.s // RET_CHUNK
    ctx = slice(geo.s, geo.seg)

    def kern(lg_ref, q_ref, k_ref, v_ref, do_ref, dq_ref, dk_ref, dv_ref, dlg_ref, states_ref, cur_ref, dst_ref):
        b, h = pl.program_id(0), pl.program_id(1)

        @pl.when((b == 0) & (h == 0))
        def _():
            dlg_ref[...] = jnp.zeros_like(dlg_ref)

        for d, rev in ((0, False), (1, True)):
            row = pl.ds(d * RET_HEADS + h, 1)
            lg = lg_ref[row, 0:1]
            kc, vc = k_ref[ctx, :].astype(F32), v_ref[ctx, :].astype(F32)
            cur_ref[...] = _ret_state0(kc, vc, lg, rev)

            def rows_of(ci, rev=rev):
                return pl.ds(pl.multiple_of((nc - 1 - ci if rev else ci) * RET_CHUNK, RET_CHUNK), RET_CHUNK)

            def load(rows):
                return q_ref[rows, :].astype(F32), k_ref[rows, :].astype(F32), v_ref[rows, :].astype(F32)

            def replay(ci, carry, rev=rev, lg=lg, rows_of=rows_of, load=load):
                states_ref[ci] = cur_ref[...]
                cur_ref[...] = _ret_step(cur_ref[...], *load(rows_of(ci)), lg, rev)[0]
                return carry

            lax.fori_loop(0, nc, replay, 0)
            dst_ref[...] = jnp.zeros_like(dst_ref)

            def back(t, dlg, d=d, rev=rev, lg=lg, rows_of=rows_of, load=load):
                ci = nc - 1 - t
                rows = rows_of(ci)
                _, vjp = jax.vjp(lambda st, q, k, v, g: _ret_step(st, q, k, v, g, rev), states_ref[ci], *load(rows), lg)
                dstate, dq, dk, dv, dg = vjp((dst_ref[...], do_ref[rows, :]))
                dst_ref[...] = dstate
                if d == 0:
                    dq_ref[rows, :], dk_ref[rows, :], dv_ref[rows, :] = dq, dk, dv
                else:
                    dq_ref[rows, :] += dq
                    dk_ref[rows, :] += dk
                    dv_ref[rows, :] += dv
                return dlg + dg

            dlg = lax.fori_loop(0, nc, back, jnp.zeros((1, 1), F32))
            _, vjp = jax.vjp(lambda kk, vv, g: _ret_state0(kk, vv, g, rev), kc, vc, lg)
            dkc, dvc, dg = vjp(dst_ref[...])
            if d == 0:
                dk_ref[ctx, :], dv_ref[ctx, :] = dkc, dvc
            else:
                dk_ref[ctx, :] += dkc
                dv_ref[ctx, :] += dvc
            dlg_ref[row, :] += jnp.broadcast_to(dlg + dg, (1, LANES))
        dq_ref[ctx, :] = jnp.zeros((geo.l, RET_QK_DIM), F32)

    nq = RET_HEADS
    return pl.pallas_call(
        kern, name=name, grid=(geo.b, RET_HEADS),
        in_specs=_ret_specs(geo) + [pl.BlockSpec((geo.seg, RET_V_DIM), lambda b, h: (b, h))],
        out_specs=[pl.BlockSpec((geo.seg, RET_QK_DIM), lambda b, h: (b, h)),
                   pl.BlockSpec((geo.seg, RET_QK_DIM), lambda b, h: (b, h)),
                   pl.BlockSpec((geo.seg, RET_V_DIM), lambda b, h: (b, h)),
                   pl.BlockSpec((2 * RET_HEADS, LANES), lambda b, h: (0, 0))],
        out_shape=[jax.ShapeDtypeStruct((geo.r, RET_QK_W), F32), jax.ShapeDtypeStruct((geo.r, RET_QK_W), F32),
                   jax.ShapeDtypeStruct((geo.r, RET_VWIDTH), F32), jax.ShapeDtypeStruct((2 * RET_HEADS, LANES), F32)],
        scratch_shapes=[pltpu.VMEM((nc, RET_QK_DIM, RET_V_DIM), F32), pltpu.VMEM((RET_QK_DIM, RET_V_DIM), F32),
                        pltpu.VMEM((RET_QK_DIM, RET_V_DIM), F32)],
        compiler_params=_cparams("arbitrary", "arbitrary"),
    )(log_g, qkv, qkv, qkv, do)


def _gated(o, g, gain):
    outs = []
    for h in range(RET_HEADS):
        cols = slice(h * RET_V_DIM, (h + 1) * RET_V_DIM)
        oh = o[:, cols]
        mu = jnp.mean(oh, axis=-1, keepdims=True)
        var = jnp.mean(jnp.square(oh - mu), axis=-1, keepdims=True)
        outs.append(_silu(g[:, cols]) * ((oh - mu) * lax.rsqrt(var + EPS) * gain[:, cols]))
    return jnp.concatenate(outs, axis=1)


def _ret_gated(geo, o, proj, gain, name):
    def body(i, ov, gv, gn):
        return _gated(ov, gv, gn)

    gate_block = (2 * RET_QK_W + RET_VWIDTH) // RET_VWIDTH
    return _rowwise(name, body, geo, 128, [(o, "row"), (proj, ("rowc", RET_VWIDTH, gate_block)), (gain, "full")],
                    [("row", RET_VWIDTH, BF16)])


def _ret_gated_bwd(geo, o, proj, gain, dout, name):
    def body(i, ov, gv, gn, dv):
        _, vjp = jax.vjp(_gated, ov, gv, gn)
        return vjp(dv)

    gate_block = (2 * RET_QK_W + RET_VWIDTH) // RET_VWIDTH
    return _rowwise(name, body, geo, 128,
                    [(o, "row"), (proj, ("rowc", RET_VWIDTH, gate_block)), (gain, "full"), (dout, "row")],
                    [("row", RET_VWIDTH, F32), ("row", RET_VWIDTH, F32), ("gacc", 1, RET_VWIDTH)])


def _whole(name, fn, out_shapes, *arrays):
    n = len(arrays)

    def kern(*refs):
        res = fn(*[r[...] for r in refs[:n]])
        for ref, val in zip(refs[n:], res):
            ref[...] = val.astype(ref.dtype)

    return pl.pallas_call(kern, name=name, out_shape=out_shapes)(*arrays)


def _rope_tables(geo, head_dim):
    rows = geo.s // GRID_W
    row = jnp.broadcast_to(jnp.arange(rows, dtype=jnp.int32)[:, None], (rows, GRID_W)).reshape(geo.s)
    col = jnp.broadcast_to(jnp.arange(GRID_W, dtype=jnp.int32)[None, :], (rows, GRID_W)).reshape(geo.s)
    axis_dim = head_dim // 2
    inv = ROPE_BASE ** (-jnp.arange(0, axis_dim, 2, dtype=F32) / axis_dim)
    ang_r = row.astype(F32)[:, None] * inv
    ang_c = col.astype(F32)[:, None] * inv
    cos = jnp.concatenate([jnp.cos(ang_r)] * 2 + [jnp.cos(ang_c)] * 2, axis=1)
    sin = jnp.concatenate([-jnp.sin(ang_r), jnp.sin(ang_r), -jnp.sin(ang_c), jnp.sin(ang_c)], axis=1)
    cos = jnp.concatenate([cos, jnp.ones((geo.l, head_dim), F32)], axis=0)
    sin = jnp.concatenate([sin, jnp.zeros((geo.l, head_dim), F32)], axis=0)
    reps = max(1, LANES // head_dim)
    return jnp.tile(cos, (1, reps)), jnp.tile(sin, (1, reps))


def _row_tile(r):
    return next(t for t in (1024, 512, 256, 128) if r % t == 0)


MOD_ROWS = 8


def _local_step(x, c, ctx, target, sp, wts):
    nb, s, d = x.shape
    geo = _Rows(nb, s, ctx.shape[1])
    assert nb + 1 <= MOD_ROWS and d == D_MODEL
    tm = _row_tile(geo.r)
    z = jnp.concatenate([x, ctx], axis=1).reshape(geo.r, d)
    cvec = jnp.concatenate([c, sp["c_ctx"][None, :], jnp.zeros((MOD_ROWS - nb - 1, d), F32)], axis=0)
    cact, = _whole("cond_silu", lambda v: (_silu(v),), [jax.ShapeDtypeStruct(cvec.shape, F32)], cvec)
    cos64, sin64 = _rope_tables(geo, HEAD_DIM)
    cos256, sin256 = _rope_tables(geo, RET_QK_DIM)
    q_gain = jnp.tile(sp["q_norm"].reshape(1, HEAD_DIM), (1, LANES // HEAD_DIM))
    k_gain = jnp.tile(sp["k_norm"].reshape(1, HEAD_DIM), (1, LANES // HEAD_DIM))
    sink = sp["sink"].reshape(N_HEADS)
    log_g = jnp.broadcast_to(sp["log_g"].reshape(2 * RET_HEADS, 1), (2 * RET_HEADS, LANES))
    gn_g = sp["gn_g"].reshape(1, RET_VWIDTH)

    saved = []
    for i in range(2):
        mod = _mm_nn(cact, wts["ada"][i], F32, f"mod{i}", MOD_ROWS, wts["ada"][i].shape[2], d, bias=sp["ada_b"][i][None, :])
        mod3 = mod[:nb + 1, None, :]
        n1, n2 = sp["norm1_g"][i][None, :], sp["norm2_g"][i][None, :]
        h1 = _norm_mod(geo, z, n1, mod3, 0, f"norm1_{i}")
        if i == 0:
            proj = _mm_nn(h1, wts["attn_qkv"], F32, "attn_qkv", tm, wts["attn_qkv"].shape[2], d)
            prep = _attn_prep(geo, proj, cos64, sin64, q_gain, k_gain, "attn_prep")
            o = _attention(geo, prep, sink, "attn")
            oraw = None
            mix = _mm_nn(o, wts["attn_o"], F32, "attn_out", tm, 1024, 1024)
        else:
            proj = _mm_nn(h1, wts["ret_qkvg"], BF16, "ret_qkvg", tm, 512, d)
            prep = _ret_prep(geo, proj, cos256, sin256, "ret_prep")
            oraw = _retention(geo, prep, log_g, "ret")
            o = _ret_gated(geo, oraw, proj, gn_g, "ret_gated")
            mix = _mm_nn(o, wts["ret_o"], F32, "ret_out", tm, 1024, 1024)
        zmid = _gate_residual(geo, z, mix, mod3, 2 * d, f"res_mix{i}")
        h2 = _norm_mod(geo, zmid, n2, mod3, 3 * d, f"norm2_{i}")
        u = _mm_nn(h2, wts["ffn_in"][i], BF16, f"ffn_in{i}", tm, wts["ffn_in"][i].shape[2], d)
        a = _swiglu(geo, u, f"swiglu{i}")
        f = _mm_nn(a, wts["ffn_out"][i], F32, f"ffn_out{i}", tm, 1024, D_FF // 2)
        zout = _gate_residual(geo, zmid, f, mod3, 5 * d, f"res_ffn{i}")
        saved.append(dict(z=z, mod3=mod3, n1=n1, n2=n2, h1=h1, proj=proj, prep=prep, o=o, oraw=oraw, mix=mix, zmid=zmid,
                          h2=h2, u=u, a=a, f=f))
        z = zout

    dz, loss = _loss_head(geo, z, target.reshape(nb * s, d), "loss")

    big, small = {}, {}
    dmods = [None, None]
    for i in (1, 0):
        sv = saved[i]
        mod3 = sv["mod3"]
        df, dg2 = _gate_residual_bwd(geo, dz, sv["f"], mod3, 5 * d, f"res_ffn_bwd{i}")
        da = _mm_nt(df, wts["ffn_out"][i], BF16, f"ffn_out_dx{i}", tm, D_FF // 2, 1024)
        big[f"ffn_out{i}"] = _mm_tn(sv["a"], df, f"ffn_out_dw{i}", D_FF // 2, 1024, tm).reshape(N_CHIPS, D_FF // N_CHIPS, d)
        du = _swiglu_bwd(geo, sv["u"], da, f"swiglu_bwd{i}")
        n4 = wts["ffn_in"][i].shape[2]
        dh2 = _mm_nt(du, wts["ffn_in"][i], BF16, f"ffn_in_dx{i}", tm, 1024, n4)
        big[f"ffn_in{i}"] = _mm_tn(sv["h2"], du, f"ffn_in_dw{i}", 1024, n4, tm, shards=N_CHIPS)
        dzmid, dsh2, dsc2, dn2 = _norm_mod_bwd(geo, sv["zmid"], sv["n2"], mod3, 3 * d, dh2, dz, f"norm2_bwd{i}")
        dmix, dg1 = _gate_residual_bwd(geo, dzmid, sv["mix"], mod3, 2 * d, f"res_mix_bwd{i}")
        if i == 0:
            do = _mm_nt(dmix, wts["attn_o"], BF16, "attn_out_dx", tm, 1024, 1024)
            big["attn_o"] = _mm_tn(sv["o"], dmix, "attn_out_dw", 1024, 1024, tm).reshape(N_CHIPS, 1024 // N_CHIPS, d)
            dq, dkv, dsink = _attention_bwd(geo, sv["prep"], sink, do, "attn_bwd")
            dproj, dqg, dkg = _attn_prep_bwd(geo, sv["proj"], cos64, sin64, q_gain, k_gain, dq, dkv, "attn_prep_bwd")
            small["q_norm"] = dqg[0, :HEAD_DIM] + dqg[0, HEAD_DIM:]
            small["k_norm"] = dkg[0, :HEAD_DIM] + dkg[0, HEAD_DIM:]
            small["sink"] = dsink[:, 0]
            wq = wts["attn_qkv"]
            dh1 = _mm_nt(dproj, wq, BF16, "attn_qkv_dx", tm, 1024, wq.shape[2])
            big["attn_qkv"] = _mm_tn(sv["h1"], dproj, "attn_qkv_dw", 1024, wq.shape[2], tm, shards=N_CHIPS)
        else:
            do = _mm_nt(dmix, wts["ret_o"], BF16, "ret_out_dx", tm, 1024, 1024)
            big["ret_o"] = _mm_tn(sv["o"], dmix, "ret_out_dw", 1024, 1024, tm).reshape(N_CHIPS, RET_VWIDTH // N_CHIPS, d)
            doraw, dgate, dgn = _ret_gated_bwd(geo, sv["oraw"], sv["proj"], gn_g, do, "ret_gated_bwd")
            small["gn_g"] = dgn[0]
            dq, dk, dv, dlg = _retention_bwd(geo, sv["prep"], log_g, doraw, "ret_bwd")
            small["log_g"] = dlg[:, 0].reshape(2, RET_HEADS)
            dproj = _ret_prep_bwd(geo, dq, dk, dv, dgate, cos256, sin256, "ret_prep_bwd")
            wq = wts["ret_qkvg"]
            dh1 = _mm_nt(dproj, wq, BF16, "ret_qkvg_dx", tm, 1024, 512)
            big["ret_qkvg"] = _mm_tn(sv["h1"], dproj, "ret_qkvg_dw", 1024, 512, tm, shards=N_CHIPS)
        dz, dsh1, dsc1, dn1 = _norm_mod_bwd(geo, sv["z"], sv["n1"], mod3, 0, dh1, dzmid, f"norm1_bwd{i}")
        small[f"norm1_g{i}"], small[f"norm2_g{i}"] = dn1[0], dn2[0]
        parts = [dsh1, dsc1, dg1, dsh2, dsc2, dg2]
        rows = jnp.concatenate([jnp.concatenate([p[:nb, 0, :] for p in parts], axis=1),
                                jnp.concatenate([jnp.sum(p[nb:, 0, :], axis=0, keepdims=True) for p in parts], axis=1),
                                jnp.zeros((MOD_ROWS - nb - 1, 6 * d), F32)], axis=0)
        dmods[i] = rows
        small[f"ada_b{i}"] = jnp.sum(rows, axis=0)
        big[f"ada{i}"] = _mm_tn(cact, rows, f"ada_dw{i}", 1024, wts["ada"][i].shape[2], MOD_ROWS, shards=N_CHIPS)

    dcact = [_mm_nt(dmods[i], wts["ada"][i], F32, f"ada_dx{i}", MOD_ROWS, 1024, wts["ada"][i].shape[2]) for i in range(2)]

    def silu_bwd(v, d0, d1):
        sg = _sigmoid(v)
        return ((d0 + d1) * (sg * (1.0 + v * (1.0 - sg))),)

    dcvec, = _whole("cond_silu_bwd", silu_bwd, [jax.ShapeDtypeStruct(cvec.shape, F32)], cvec, dcact[0], dcact[1])
    small["c_ctx"] = dcvec[nb]
    return loss, dz, big, small


def _adamw(w, g, m, v, name):
    rows, cols = w.shape
    tr = next((t for t in (256, 128, 64, 32, 16, 8) if rows % t == 0), rows)
    c1 = 1.0 - ADAM_B1 ** ADAM_STEP
    c2 = 1.0 - ADAM_B2 ** ADAM_STEP

    def kern(w_ref, g_ref, m_ref, v_ref, d_ref, nm_ref, nv_ref):
        gv = g_ref[...]
        nm = ADAM_B1 * m_ref[...] + (1.0 - ADAM_B1) * gv
        nv = ADAM_B2 * v_ref[...] + (1.0 - ADAM_B2) * jnp.square(gv)
        d_ref[...] = -ADAM_LR * ((nm / c1) / (jnp.sqrt(nv / c2) + ADAM_EPS) + ADAM_WD * w_ref[...])
        nm_ref[...] = nm
        nv_ref[...] = nv

    spec = pl.BlockSpec((tr, cols), lambda i: (i, 0))
    return pl.pallas_call(
        kern, name=name, grid=(rows // tr,), in_specs=[spec] * 4, out_specs=[spec] * 3,
        out_shape=[jax.ShapeDtypeStruct(w.shape, F32)] * 3, compiler_params=_cparams("parallel"),
    )(w, g, m, v)


N_DEVICES = 8


def _mesh_pos():
    return lax.axis_index("x"), lax.axis_index("y"), lax.axis_index("c")


def _other_chips(x, y):
    return [(1 - x, y), (x, 1 - y), (1 - x, 1 - y)]


def _hbm(n):
    return [pl.BlockSpec(memory_space=pl.ANY)] * n


def _remote(src, dst, send_sem, recv_sem, device):
    return pltpu.make_async_remote_copy(src_ref=src, dst_ref=dst, send_sem=send_sem, recv_sem=recv_sem,
                                        device_id=device, device_id_type=MESH)


def _scalar_spec(grid, in_specs, out_specs):
    return pltpu.PrefetchScalarGridSpec(num_scalar_prefetch=1, grid=grid, in_specs=in_specs, out_specs=out_specs)


def _place_shard(shard, pos, name):
    r, cols = shard.shape
    tr = _slab_tile(r)

    def kern(pos_ref, s_ref, o_ref):
        o_ref[...] = s_ref[...].astype(BF16)

    return pl.pallas_call(
        kern, name=name, out_shape=jax.ShapeDtypeStruct((N_CHIPS, r, cols), BF16),
        grid_spec=_scalar_spec((r // tr,), [pl.BlockSpec((tr, cols), lambda i, p: (i, 0))],
                               pl.BlockSpec((None, tr, cols), lambda i, p: (p[1], i, 0))),
        compiler_params=_cparams("parallel"),
    )(pos, shard)


def _gather_shards(placed):
    n = len(placed)

    def body(*refs):
        outs = refs[n:2 * n]
        send_sems, recv_sems, fwd_send, fwd_recv = refs[2 * n:]
        x, y, c = _mesh_pos()
        chip = 2 * x + y
        others = _other_chips(x, y)
        sibling = (x, y, 1 - c)

        def half(w, which):
            r2 = placed[w].shape[1] // 2
            return pl.ds(which * r2, r2)

        sends = []
        for w in range(n):
            for k, (px, py) in enumerate(others):
                mine = outs[w].at[chip, half(w, c)]
                cp = _remote(mine, mine, send_sems.at[w, k], recv_sems.at[w, k], (px, py, c))
                cp.start()
                sends.append(cp)
        for w in range(n):
            for k, (px, py) in enumerate(others):
                got = outs[w].at[2 * px + py, half(w, c)]
                _remote(got, got, send_sems.at[w, k], recv_sems.at[w, k], (px, py, c)).wait_recv()
                cp = _remote(got, got, fwd_send.at[w, k], fwd_recv.at[w, k], sibling)
                cp.start()
                sends.append(cp)
        for w in range(n):
            for k, (px, py) in enumerate(others):
                theirs = outs[w].at[2 * px + py, half(w, 1 - c)]
                _remote(theirs, theirs, fwd_send.at[w, k], fwd_recv.at[w, k], sibling).wait_recv()
        for cp in sends:
            cp.wait_send()

    return pl.pallas_call(
        body, name="gather_weights", in_specs=_hbm(n), out_specs=_hbm(n),
        out_shape=[jax.ShapeDtypeStruct(p.shape, p.dtype) for p in placed],
        input_output_aliases={w: w for w in range(n)},
        scratch_shapes=[pltpu.SemaphoreType.DMA((n, 3))] * 4,
    )(*placed)


def _pair_swap(grads):
    n = len(grads)

    def body(*refs):
        ins, land = refs[:n], refs[n:2 * n]
        send_sems, recv_sems = refs[2 * n:]
        x, y, c = _mesh_pos()
        copies = []
        for w in range(n):
            r2 = grads[w].shape[1] // 2
            rc = _remote(ins[w].at[:, pl.ds((1 - c) * r2, r2)], land[w], send_sems.at[w], recv_sems.at[w], (x, y, 1 - c))
            rc.start()
            copies.append(rc)
        for cp in copies:
            cp.wait()

    return pl.pallas_call(
        body, name="grads_pair_swap", in_specs=_hbm(n), out_specs=_hbm(n),
        out_shape=[jax.ShapeDtypeStruct((N_CHIPS, g.shape[1] // 2, g.shape[2]), F32) for g in grads],
        scratch_shapes=[pltpu.SemaphoreType.DMA((n,))] * 2,
    )(*grads)


def _chip_exchange(hs):
    n = len(hs)

    def body(*refs):
        ins, land = refs[:n], refs[n:2 * n]
        send_sems, recv_sems = refs[2 * n:]
        x, y, c = _mesh_pos()
        chip = 2 * x + y
        others = _other_chips(x, y)
        sends = []
        for w in range(n):
            for k, (px, py) in enumerate(others):
                cp = _remote(ins[w].at[2 * px + py], land[w].at[chip], send_sems.at[w, k], recv_sems.at[w, k], (px, py, c))
                cp.start()
                sends.append(cp)
        for w in range(n):
            for k, (px, py) in enumerate(others):
                got = land[w].at[2 * px + py]
                _remote(got, got, send_sems.at[w, k], recv_sems.at[w, k], (px, py, c)).wait_recv()
        for cp in sends:
            cp.wait_send()

    return pl.pallas_call(
        body, name="grads_chip_exchange", in_specs=_hbm(n), out_specs=_hbm(n),
        out_shape=[jax.ShapeDtypeStruct(h.shape, h.dtype) for h in hs],
        scratch_shapes=[pltpu.SemaphoreType.DMA((n, 3))] * 2,
    )(*hs)


def _pair_share(ts):
    n = len(ts)

    def body(*refs):
        outs = refs[n:2 * n]
        send_sems, recv_sems = refs[2 * n:]
        x, y, c = _mesh_pos()
        sends = []
        for w in range(n):
            r2 = ts[w].shape[0] // 2
            mine = outs[w].at[pl.ds(c * r2, r2)]
            rc = _remote(mine, mine, send_sems.at[w], recv_sems.at[w], (x, y, 1 - c))
            rc.start()
            sends.append(rc)
        for w in range(n):
            r2 = ts[w].shape[0] // 2
            theirs = outs[w].at[pl.ds((1 - c) * r2, r2)]
            _remote(theirs, theirs, send_sems.at[w], recv_sems.at[w], (x, y, 1 - c)).wait_recv()
            sends[w].wait_send()

    return pl.pallas_call(
        body, name="grads_pair_share", in_specs=_hbm(n), out_specs=_hbm(n),
        out_shape=[jax.ShapeDtypeStruct(t.shape, F32) for t in ts],
        input_output_aliases={w: w for w in range(n)},
        scratch_shapes=[pltpu.SemaphoreType.DMA((n,))] * 2,
    )(*ts)


def _slab_tile(rows):
    return next(t for t in (512, 256, 176, 128, 64, 32, 16) if rows % t == 0)


def _sum_pair(grad, land, pos, name):
    _, r2, cols = land.shape
    tr = _slab_tile(r2)
    nt = r2 // tr

    def kern(pos_ref, a_ref, b_ref, o_ref):
        o_ref[...] = (a_ref[...] + b_ref[...]).astype(BF16)

    spec = pl.BlockSpec((None, tr, cols), lambda j, i, p: (j, i, 0))
    return pl.pallas_call(
        kern, name=name, out_shape=jax.ShapeDtypeStruct(land.shape, BF16),
        grid_spec=_scalar_spec((N_CHIPS, nt), [pl.BlockSpec((None, tr, cols), lambda j, i, p: (j, p[0] * nt + i, 0)), spec], spec),
        compiler_params=_cparams("parallel", "parallel"),
    )(pos, grad, land)


def _sum_chips(hs, land, pos, name):
    _, r2, cols = land.shape
    tr = _slab_tile(r2)
    nt = r2 // tr

    def kern(pos_ref, h_ref, l_ref, o_ref):
        acc = jnp.zeros((tr, cols), F32)
        own = h_ref[...].astype(F32)
        for k in range(N_CHIPS):
            acc = acc + jnp.where(pos_ref[1] == k, own, l_ref[k].astype(F32))
        o_ref[...] = acc

    return pl.pallas_call(
        kern, name=name, out_shape=jax.ShapeDtypeStruct((2 * r2, cols), F32),
        grid_spec=_scalar_spec((nt,), [pl.BlockSpec((None, tr, cols), lambda i, p: (p[1], i, 0)),
                                       pl.BlockSpec((N_CHIPS, tr, cols), lambda i, p: (0, i, 0))],
                               pl.BlockSpec((tr, cols), lambda i, p: (p[0] * nt + i, 0))),
        compiler_params=_cparams("parallel"),
    )(pos, hs, land)


def _reduce_scatter(grads, pos):
    land = _pair_swap(grads)
    hs = [_sum_pair(g, l, pos, f"grads_pair_sum{w}") for w, (g, l) in enumerate(zip(grads, land))]
    land2 = _chip_exchange(hs)
    ts = [_sum_chips(h, l, pos, f"grads_chip_sum{w}") for w, (h, l) in enumerate(zip(hs, land2))]
    return _pair_share(ts)


def _all_reduce_small(v, name):
    def body(v_ref, o_ref, land_ref, send_sems, recv_sems):
        x, y, c = _mesh_pos()
        me = 4 * x + 2 * y + c
        land_ref[me] = v_ref[...]
        for t in range(N_DEVICES):
            @pl.when(t != me)
            def _(t=t):
                _remote(v_ref, land_ref.at[me], send_sems.at[t], recv_sems.at[me], (t // 4, (t // 2) % 2, t % 2)).start()
        for t in range(N_DEVICES):
            @pl.when(t != me)
            def _(t=t):
                _remote(v_ref, land_ref.at[t], send_sems.at[t], recv_sems.at[t], (t // 4, (t // 2) % 2, t % 2)).wait()
        acc = land_ref[0]
        for t in range(1, N_DEVICES):
            acc = acc + land_ref[t]
        o_ref[...] = acc

    vmem = pl.BlockSpec(memory_space=pltpu.VMEM)
    return pl.pallas_call(
        body, name=name, in_specs=[vmem], out_specs=vmem, out_shape=jax.ShapeDtypeStruct(v.shape, F32),
        scratch_shapes=[pltpu.VMEM((N_DEVICES,) + v.shape, F32), pltpu.SemaphoreType.DMA((N_DEVICES,)),
                        pltpu.SemaphoreType.DMA((N_DEVICES,))],
    )(v)


SMALL_ROWS = 24


def _pack_small(small, dlogit):
    d = D_MODEL
    misc = jnp.zeros((d,), F32)
    misc = misc.at[0:HEAD_DIM].set(small["q_norm"]).at[128:128 + HEAD_DIM].set(small["k_norm"])
    misc = misc.at[256:256 + N_HEADS].set(small["sink"]).at[384:384 + 2 * RET_HEADS].set(dlogit.reshape(-1))
    rows = [small["ada_b0"].reshape(6, d), small["ada_b1"].reshape(6, d), small["norm1_g0"][None], small["norm1_g1"][None],
            small["norm2_g0"][None], small["norm2_g1"][None], small["c_ctx"][None], small["gn_g"].reshape(2, d), misc[None]]
    buf = jnp.concatenate(rows, axis=0)
    return jnp.concatenate([buf, jnp.zeros((SMALL_ROWS - buf.shape[0], d), F32)], axis=0)


def _unpack_small(buf):
    d = D_MODEL
    misc = buf[19]
    return dict(ada_b=buf[0:12].reshape(2, 6 * d), norm1_g=buf[12:14], norm2_g=buf[14:16], c_ctx=buf[16],
                gn_g=buf[17:19].reshape(2 * d), q_norm=misc[0:HEAD_DIM], k_norm=misc[128:128 + HEAD_DIM],
                sink=misc[256:256 + N_HEADS], decay=misc[384:384 + 2 * RET_HEADS])


def kernel(x, c, ctx, c_ctx, ada_w, ada_b, norm1_g, norm2_g, ffn_w_in, ffn_w_out, attn_w_qkv, attn_q_norm, attn_k_norm, attn_sink, attn_w_o, ret_w_qkvg, ret_decay_logit, ret_gn_g, ret_w_o, loss_target, m_c_ctx, m_ada_w, m_ada_b, m_norm1_g, m_norm2_g, m_ffn_w_in, m_ffn_w_out, m_attn_w_qkv, m_attn_q_norm, m_attn_k_norm, m_attn_sink, m_attn_w_o, m_ret_w_qkvg, m_ret_decay_logit, m_ret_gn_g, m_ret_w_o, v_c_ctx, v_ada_w, v_ada_b, v_norm1_g, v_norm2_g, v_ffn_w_in, v_ffn_w_out, v_attn_w_qkv, v_attn_q_norm, v_attn_k_norm, v_attn_sink, v_attn_w_o, v_ret_w_qkvg, v_ret_decay_logit, v_ret_gn_g, v_ret_w_o):
    xi, yi, ci = _mesh_pos()
    chip = 2 * xi + yi
    nb, s, d = x.shape
    gn_shard = ret_gn_g.shape[1]

    shards = dict(ada0=ada_w[0], ada1=ada_w[1], ffn_in0=ffn_w_in[0], ffn_in1=ffn_w_in[1], ffn_out0=ffn_w_out[0],
                  ffn_out1=ffn_w_out[1], attn_qkv=attn_w_qkv[0], attn_o=attn_w_o[0], ret_qkvg=ret_w_qkvg[0], ret_o=ret_w_o[0])
    names = list(shards)
    pos = jnp.stack([ci, chip]).astype(jnp.int32)
    full = dict(zip(names, _gather_shards([_place_shard(shards[k], pos, f"place_{k}") for k in names])))
    gn_mine = jnp.where(ci == 0, ret_gn_g[0], jnp.zeros_like(ret_gn_g[0]))
    gn_place = lax.dynamic_update_slice(jnp.zeros((RET_VWIDTH,), F32), gn_mine, (chip * gn_shard,))
    gn_full = _all_reduce_small(gn_place.reshape(2, d), "gather_gn_gain").reshape(RET_VWIDTH)

    wts = dict(ada=[full["ada0"], full["ada1"]], ffn_in=[full["ffn_in0"], full["ffn_in1"]],
               ffn_out=[full["ffn_out0"].reshape(D_FF, d), full["ffn_out1"].reshape(D_FF, d)],
               attn_qkv=full["attn_qkv"], attn_o=full["attn_o"].reshape(N_HEADS * HEAD_DIM, d),
               ret_qkvg=full["ret_qkvg"], ret_o=full["ret_o"].reshape(RET_VWIDTH, d))
    decay_logit = ret_decay_logit[0]
    sp = dict(c_ctx=c_ctx, ada_b=ada_b, norm1_g=norm1_g, norm2_g=norm2_g, q_norm=attn_q_norm[0], k_norm=attn_k_norm[0],
              sink=attn_sink[0], log_g=jax.nn.log_sigmoid(decay_logit), gn_g=gn_full)
    loss_part, dz, big, small = _local_step(x, c, ctx, loss_target, sp, wts)

    loss = lax.psum(loss_part[0, 0], ("x", "y", "c"))
    grad_x = dz.reshape(nb, -1, d)[:, :s]

    dlogit = small["log_g"] * jax.nn.sigmoid(-decay_logit)
    sg = _unpack_small(_all_reduce_small(_pack_small(small, dlogit), "reduce_small_grads"))
    reduced = dict(zip(names, _reduce_scatter([big[k] for k in names], pos)))

    grads = dict(
        c_ctx=sg["c_ctx"], ada_w=jnp.stack([reduced["ada0"], reduced["ada1"]]), ada_b=sg["ada_b"], norm1_g=sg["norm1_g"],
        norm2_g=sg["norm2_g"], ffn_w_in=jnp.stack([reduced["ffn_in0"], reduced["ffn_in1"]]),
        ffn_w_out=jnp.stack([reduced["ffn_out0"], reduced["ffn_out1"]]), attn_w_qkv=reduced["attn_qkv"][None],
        attn_q_norm=sg["q_norm"][None], attn_k_norm=sg["k_norm"][None], attn_sink=sg["sink"][None],
        attn_w_o=reduced["attn_o"][None], ret_w_qkvg=reduced["ret_qkvg"][None], ret_decay_logit=sg["decay"].reshape(1, 2, RET_HEADS),
        ret_gn_g=lax.dynamic_slice(sg["gn_g"], (chip * gn_shard,), (gn_shard,))[None], ret_w_o=reduced["ret_o"][None])
    params = dict(c_ctx=(c_ctx, m_c_ctx, v_c_ctx), ada_w=(ada_w, m_ada_w, v_ada_w), ada_b=(ada_b, m_ada_b, v_ada_b),
                  norm1_g=(norm1_g, m_norm1_g, v_norm1_g), norm2_g=(norm2_g, m_norm2_g, v_norm2_g),
                  ffn_w_in=(ffn_w_in, m_ffn_w_in, v_ffn_w_in), ffn_w_out=(ffn_w_out, m_ffn_w_out, v_ffn_w_out),
                  attn_w_qkv=(attn_w_qkv, m_attn_w_qkv, v_attn_w_qkv), attn_q_norm=(attn_q_norm, m_attn_q_norm, v_attn_q_norm),
                  attn_k_norm=(attn_k_norm, m_attn_k_norm, v_attn_k_norm), attn_sink=(attn_sink, m_attn_sink, v_attn_sink),
                  attn_w_o=(attn_w_o, m_attn_w_o, v_attn_w_o), ret_w_qkvg=(ret_w_qkvg, m_ret_w_qkvg, v_ret_w_qkvg),
                  ret_decay_logit=(ret_decay_logit, m_ret_decay_logit, v_ret_decay_logit),
                  ret_gn_g=(ret_gn_g, m_ret_gn_g, v_ret_gn_g), ret_w_o=(ret_w_o, m_ret_w_o, v_ret_w_o))
    order = list(params)
    deltas, new_m, new_v = [], [], []
    for k in order:
        w, m, v = params[k]
        g = grads[k].reshape(w.shape)
        grads[k] = g
        flat = (-1, w.shape[-1]) if w.ndim > 1 else (1, -1)
        if k == "ret_decay_logit":
            flat = (1, -1)
        dw, nm, nv = _adamw(w.reshape(flat), g.reshape(flat), m.reshape(flat), v.reshape(flat), f"adamw_{k}")
        deltas.append(dw.reshape(w.shape))
        new_m.append(nm.reshape(w.shape))
        new_v.append(nv.reshape(w.shape))
    return (loss, grad_x, *[grads[k] for k in order], *deltas, *new_m, *new_v)
```

```python
import functools

import jax
import jax.numpy as jnp
from jax import lax
from jax.experimental import pallas as pl
from jax.experimental.pallas import tpu as pltpu

F32 = jnp.float32
BF16 = jnp.bfloat16

D_MODEL = 1024
N_HEADS = 16
N_KV_HEADS = 4
HEAD_DIM = 64
WINDOW = 128
ATTN_BLOCK = 128
BAND = ATTN_BLOCK + 2 * WINDOW
RET_HEADS = 4
RET_QK_DIM = 256
RET_V_DIM = 512
RET_VWIDTH = 2048
RET_CHUNK = 128
D_FF = 2816
GRID_W = 64
ROPE_BASE = 10000.0
EPS = 1e-6
NEG_INF = -1e30
LANES = 128

ADAM_LR = 0.001
ADAM_B1 = 0.9
ADAM_B2 = 0.999
ADAM_EPS = 1e-08
ADAM_WD = 0.01
ADAM_STEP = 10

VMEM_LIMIT_BYTES = 56 * 1024 * 1024
MESH = pl.DeviceIdType.MESH
N_CHIPS = 4


def _cparams(*sem):
    return pltpu.CompilerParams(dimension_semantics=sem, vmem_limit_bytes=VMEM_LIMIT_BYTES)


_DIMS = {"nn": ((1,), (0,)), "nt": ((1,), (1,)), "tn": ((0,), (0,))}


def _dot(a, b, form):
    return lax.dot_general(a.astype(BF16), b.astype(BF16), (_DIMS[form], ((), ())), preferred_element_type=F32)


@functools.partial(jax.custom_vjp, nondiff_argnums=(2,))
def _mm(a, b, form):
    return _dot(a, b, form)


def _mm_fwd(a, b, form):
    return _dot(a, b, form), (a, b)


def _mm_bwd(form, res, ct):
    a, b = res
    if form == "nn":
        da, db = _dot(ct, b, "nt"), _dot(a, ct, "tn")
    elif form == "nt":
        da, db = _dot(ct, b, "nn"), _dot(ct, a, "tn")
    else:
        da, db = _dot(b, ct, "nt"), _dot(a, ct, "nn")
    return da.astype(a.dtype), db.astype(b.dtype)


_mm.defvjp(_mm_fwd, _mm_bwd)


def _swap_halves(x, half):
    w = x.shape[-1]
    lane = lax.broadcasted_iota(jnp.int32, x.shape, x.ndim - 1)
    return jnp.where(lane % (2 * half) < half, pltpu.roll(x, w - half, x.ndim - 1), pltpu.roll(x, half, x.ndim - 1))


@functools.partial(jax.custom_vjp, nondiff_argnums=(1,))
def _rot(x, half):
    return _swap_halves(x, half)


def _rot_fwd(x, half):
    return _swap_halves(x, half), None


def _rot_bwd(half, _, ct):
    return (_swap_halves(ct, half),)


_rot.defvjp(_rot_fwd, _rot_bwd)


def _rope(x, cos, sin_signed, half):
    return x * cos + _rot(x, half) * sin_signed


def _head_mean_square(x):
    r = lax.broadcasted_iota(jnp.int32, (LANES, LANES), 0) // HEAD_DIM
    c = lax.broadcasted_iota(jnp.int32, (LANES, LANES), 1) // HEAD_DIM
    g = jnp.where(r == c, 1.0 / HEAD_DIM, 0.0).astype(F32)
    return jnp.dot(x * x, g, precision=lax.Precision.HIGHEST, preferred_element_type=F32)


def _qk_chunk(x, gain, cos, sin_signed, scale):
    y = x * lax.rsqrt(_head_mean_square(x) + EPS) * gain
    return _rope(y, cos, sin_signed, HEAD_DIM // 4) * scale


def _sigmoid(x):
    return 1.0 / (1.0 + jnp.exp(-x))


def _silu(x):
    return x * _sigmoid(x)


def _mm_nn(a, w, out_dtype, name, tm, tn, tk, bias=None):
    m, k_dim = a.shape
    if w.ndim == 3:
        n = w.shape[0] * w.shape[2]
        per = w.shape[2] // tn
        assert w.shape[2] % tn == 0
        w_spec = pl.BlockSpec((None, tk, tn), lambda i, j, k: (j // per, k, j % per))
    else:
        n = w.shape[1]
        w_spec = pl.BlockSpec((tk, tn), lambda i, j, k: (k, j))
    assert m % tm == 0 and n % tn == 0 and k_dim % tk == 0, (name, a.shape, w.shape, tm, tn, tk)
    nk = k_dim // tk
    has_bias = bias is not None

    def body(*refs):
        a_ref, w_ref = refs[0], refs[1]
        b_ref = refs[2] if has_bias else None
        o_ref, acc_ref = refs[-2], refs[-1]
        part = jnp.dot(a_ref[...].astype(BF16), w_ref[...], preferred_element_type=F32)
        if nk == 1:
            o_ref[...] = (part + b_ref[...] if has_bias else part).astype(out_dtype)
            return
        k = pl.program_id(2)

        @pl.when(k == 0)
        def _():
            acc_ref[...] = part

        @pl.when(k > 0)
        def _():
            acc_ref[...] += part

        @pl.when(k == nk - 1)
        def _():
            r = acc_ref[...]
            if has_bias:
                r = r + b_ref[...]
            o_ref[...] = r.astype(out_dtype)

    in_specs = [pl.BlockSpec((tm, tk), lambda i, j, k: (i, k)), w_spec]
    args = [a, w]
    if has_bias:
        in_specs.append(pl.BlockSpec((1, tn), lambda i, j, k: (0, j)))
        args.append(bias)
    return pl.pallas_call(
        body, name=name, grid=(m // tm, n // tn, nk), in_specs=in_specs,
        out_specs=pl.BlockSpec((tm, tn), lambda i, j, k: (i, j)),
        out_shape=jax.ShapeDtypeStruct((m, n), out_dtype),
        scratch_shapes=[pltpu.VMEM((tm, tn), F32)],
        compiler_params=_cparams("parallel", "parallel", "arbitrary"),
    )(*args)


def _mm_nt(a, w, out_dtype, name, tm, tn, tk):
    m, c_dim = a.shape
    if w.ndim == 3:
        k_out = w.shape[1]
        per = w.shape[2] // tk
        assert w.shape[2] % tk == 0 and w.shape[0] * w.shape[2] == c_dim
        w_spec = pl.BlockSpec((None, tn, tk), lambda i, j, k: (k // per, j, k % per))
    else:
        k_out = w.shape[0]
        assert w.shape[1] == c_dim
        w_spec = pl.BlockSpec((tn, tk), lambda i, j, k: (j, k))
    assert m % tm == 0 and k_out % tn == 0 and c_dim % tk == 0, (name, a.shape, w.shape, tm, tn, tk)
    nk = c_dim // tk

    def body(a_ref, w_ref, o_ref, acc_ref):
        part = _dot(a_ref[...], w_ref[...], "nt")
        if nk == 1:
            o_ref[...] = part.astype(out_dtype)
            return
        k = pl.program_id(2)

        @pl.when(k == 0)
        def _():
            acc_ref[...] = part

        @pl.when(k > 0)
        def _():
            acc_ref[...] += part

        @pl.when(k == nk - 1)
        def _():
            o_ref[...] = acc_ref[...].astype(out_dtype)

    return pl.pallas_call(
        body, name=name, grid=(m // tm, k_out // tn, nk),
        in_specs=[pl.BlockSpec((tm, tk), lambda i, j, k: (i, k)), w_spec],
        out_specs=pl.BlockSpec((tm, tn), lambda i, j, k: (i, j)),
        out_shape=jax.ShapeDtypeStruct((m, k_out), out_dtype),
        scratch_shapes=[pltpu.VMEM((tm, tn), F32)],
        compiler_params=_cparams("parallel", "parallel", "arbitrary"),
    )(a, w)


def _mm_tn(a, b, name, tm, tn, tk, shards=None):
    r, k_dim = a.shape
    n = b.shape[1]
    assert r % tk == 0 and k_dim % tm == 0 and n % tn == 0, (name, a.shape, b.shape, tm, tn, tk)
    nk = r // tk
    if shards:
        per = n // shards // tn
        assert n % (shards * tn) == 0
        out_shape = jax.ShapeDtypeStruct((shards, k_dim, n // shards), F32)
        out_spec = pl.BlockSpec((None, tm, tn), lambda i, j, k: (j // per, i, j % per))
    else:
        out_shape = jax.ShapeDtypeStruct((k_dim, n), F32)
        out_spec = pl.BlockSpec((tm, tn), lambda i, j, k: (i, j))

    def body(a_ref, b_ref, o_ref):
        k = pl.program_id(2)

        part = _dot(a_ref[...], b_ref[...], "tn")

        @pl.when(k == 0)
        def _():
            o_ref[...] = part

        @pl.when(k > 0)
        def _():
            o_ref[...] += part

    return pl.pallas_call(
        body, name=name, grid=(k_dim // tm, n // tn, nk),
        in_specs=[pl.BlockSpec((tk, tm), lambda i, j, k: (k, i)), pl.BlockSpec((tk, tn), lambda i, j, k: (k, j))],
        out_specs=out_spec, out_shape=out_shape,
        compiler_params=_cparams("parallel", "parallel", "arbitrary"),
    )(a, b)


class _Rows:
    def __init__(self, b, s, l):
        self.b, self.s, self.l = b, s, l
        self.seg = s + l
        self.r = b * self.seg


def _rowwise(name, body, geo, tm, ins, outs):
    seg_blocks, x_blocks = geo.seg // tm, geo.s // tm
    assert geo.seg % tm == 0 and geo.s % tm == 0
    nb = geo.b

    def is_ctx(i):
        return i % seg_blocks >= x_blocks

    in_specs, args = [], []
    for arr, kind in ins:
        args.append(arr)
        if kind == "row":
            in_specs.append(pl.BlockSpec((tm, arr.shape[1]), lambda i: (i, 0)))
        elif kind == "ex":
            in_specs.append(pl.BlockSpec((None, 1, arr.shape[2]), lambda i: (jnp.where(is_ctx(i), nb, i // seg_blocks), 0, 0)))
        elif kind == "full":
            in_specs.append(pl.BlockSpec(arr.shape, lambda i, nd=arr.ndim: (0,) * nd))
        elif kind == "tab":
            in_specs.append(pl.BlockSpec((tm, arr.shape[1]), lambda i: (i % seg_blocks, 0)))
        elif kind == "xrow":
            in_specs.append(pl.BlockSpec(
                (tm, arr.shape[1]), lambda i: ((i // seg_blocks) * x_blocks + jnp.minimum(i % seg_blocks, x_blocks - 1), 0)))
        else:
            _, width, cb = kind
            in_specs.append(pl.BlockSpec((tm, width), lambda i, cb=cb: (i, cb)))
    out_specs, out_shapes = [], []
    for o in outs:
        if o[0] == "row":
            out_specs.append(pl.BlockSpec((tm, o[1]), lambda i: (i, 0)))
            out_shapes.append(jax.ShapeDtypeStruct((geo.r, o[1]), o[2]))
        elif o[0] == "exacc":
            out_specs.append(pl.BlockSpec((None, 1, o[1]), lambda i: (jnp.where(is_ctx(i), nb, 0) + i // seg_blocks, 0, 0)))
            out_shapes.append(jax.ShapeDtypeStruct((2 * nb, 1, o[1]), F32))
        else:
            out_specs.append(pl.BlockSpec((o[1], o[2]), lambda i: (0, 0)))
            out_shapes.append(jax.ShapeDtypeStruct((o[1], o[2]), F32))
    n_in = len(ins)

    def kern(*refs):
        i = pl.program_id(0)
        res = body(i, *[r[...].astype(F32) for r in refs[:n_in]])
        if not isinstance(res, (tuple, list)):
            res = (res,)
        jj = i % seg_blocks
        first_of_part = (jj == 0) | (jj == x_blocks)
        for o, ref, val in zip(outs, refs[n_in:], res):
            if o[0] == "row":
                ref[...] = val.astype(ref.dtype)
            else:
                first = first_of_part if o[0] == "exacc" else i == 0

                @pl.when(first)
                def _(ref=ref, val=val):
                    ref[...] = val

                @pl.when(jnp.logical_not(first))
                def _(ref=ref, val=val):
                    ref[...] += val

    res = pl.pallas_call(
        kern, name=name, grid=(geo.r // tm,), in_specs=in_specs, out_specs=out_specs, out_shape=out_shapes,
        compiler_params=_cparams("arbitrary"),
    )(*args)
    return res[0] if len(res) == 1 else res


def _colsum(v):
    return jnp.sum(v, axis=0, keepdims=True)


def _norm_mod(geo, z, gain, mod, off, name):
    d = D_MODEL

    def body(i, zv, g, m):
        r = lax.rsqrt(jnp.mean(zv * zv, axis=-1, keepdims=True) + EPS)
        return (zv * r) * g * (1.0 + m[:, off + d:off + 2 * d]) + m[:, off:off + d]

    return _rowwise(name, body, geo, 256, [(z, "row"), (gain, "full"), (mod, "ex")], [("row", d, BF16)])


def _norm_mod_bwd(geo, z, gain, mod, off, dh, dz_skip, name):
    d = D_MODEL

    def body(i, zv, g, m, dhv, skip):
        r = lax.rsqrt(jnp.mean(zv * zv, axis=-1, keepdims=True) + EPS)
        n = zv * r
        dng = dhv * (1.0 + m[:, off + d:off + 2 * d])
        dn = dng * g
        dz = r * (dn - n * jnp.mean(dn * n, axis=-1, keepdims=True)) + skip
        return dz, _colsum(dhv), _colsum(dhv * (n * g)), _colsum(dng * n)

    return _rowwise(name, body, geo, 256, [(z, "row"), (gain, "full"), (mod, "ex"), (dh, "row"), (dz_skip, "row")],
                    [("row", d, F32), ("exacc", d), ("exacc", d), ("gacc", 1, d)])


def _gate_residual(geo, z, out, mod, off, name):
    d = D_MODEL

    def body(i, zv, ov, m):
        return zv + m[:, off:off + d] * ov

    return _rowwise(name, body, geo, 256, [(z, "row"), (out, "row"), (mod, "ex")], [("row", d, F32)])


def _gate_residual_bwd(geo, dz, out, mod, off, name):
    d = D_MODEL

    def body(i, dzv, ov, m):
        return dzv * m[:, off:off + d], _colsum(dzv * ov)

    return _rowwise(name, body, geo, 256, [(dz, "row"), (out, "row"), (mod, "ex")], [("row", d, BF16), ("exacc", d)])


def _swiglu(geo, u, name):
    def body(i, uv):
        return _silu(uv[:, :D_FF]) * uv[:, D_FF:]

    return _rowwise(name, body, geo, 128, [(u, "row")], [("row", D_FF, BF16)])


def _swiglu_bwd(geo, u, da, name):
    def body(i, uv, dav):
        g, up = uv[:, :D_FF], uv[:, D_FF:]
        s = _sigmoid(g)
        return jnp.concatenate([dav * up * (s * (1.0 + g * (1.0 - s))), dav * (g * s)], axis=1)

    return _rowwise(name, body, geo, 128, [(u, "row"), (da, "row")], [("row", 2 * D_FF, BF16)])


def _loss_head(geo, z, target, name):
    seg_blocks, x_blocks = geo.seg // 256, geo.s // 256

    def body(i, zv, tv):
        keep = jnp.where(i % seg_blocks >= x_blocks, 0.0, 1.0)
        err = (zv - tv) * keep
        part = 0.5 * jnp.sum(jnp.mean(err * err, axis=-1, keepdims=True), axis=0, keepdims=True)
        return err * (1.0 / D_MODEL), jnp.broadcast_to(part, (1, LANES))

    return _rowwise(name, body, geo, 256, [(z, "row"), (target, "xrow")], [("row", D_MODEL, F32), ("gacc", 1, LANES)])


Q_SCALE = HEAD_DIM ** -0.5
N_QK_CHUNKS = (N_HEADS + N_KV_HEADS) * HEAD_DIM // LANES
N_Q_CHUNKS = N_HEADS * HEAD_DIM // LANES


def _attn_prep(geo, proj, cos, sin_signed, q_gain, k_gain, name):
    def body(i, p, cs, sn, qg, kg):
        outs = []
        for ch in range(N_QK_CHUNKS):
            is_q = ch < N_Q_CHUNKS
            outs.append(_qk_chunk(p[:, ch * LANES:(ch + 1) * LANES], qg if is_q else kg, cs, sn, Q_SCALE if is_q else 1.0))
        outs.append(p[:, N_QK_CHUNKS * LANES:])
        return jnp.concatenate(outs, axis=1)

    return _rowwise(name, body, geo, 256, [(proj, "row"), (cos, "tab"), (sin_signed, "tab"), (q_gain, "full"), (k_gain, "full")],
                    [("row", proj.shape[1], BF16)])


def _attn_prep_bwd(geo, proj, cos, sin_signed, q_gain, k_gain, dq, dkv, name):
    kw = N_KV_HEADS * HEAD_DIM

    def body(i, p, cs, sn, qg, kg, dqv, dkvv):
        outs = []
        dgains = [jnp.zeros((1, LANES), F32), jnp.zeros((1, LANES), F32)]
        for ch in range(N_QK_CHUNKS):
            is_q = ch < N_Q_CHUNKS
            scale = Q_SCALE if is_q else 1.0
            ct = dqv[:, ch * LANES:(ch + 1) * LANES] if is_q else dkvv[:, (ch - N_Q_CHUNKS) * LANES:(ch - N_Q_CHUNKS + 1) * LANES]
            _, vjp = jax.vjp(lambda xx, gg, scale=scale: _qk_chunk(xx, gg, cs, sn, scale),
                             p[:, ch * LANES:(ch + 1) * LANES], qg if is_q else kg)
            dx, dg = vjp(ct)
            outs.append(dx)
            dgains[0 if is_q else 1] = dgains[0 if is_q else 1] + dg
        outs.append(dkvv[:, kw:])
        return jnp.concatenate(outs, axis=1), dgains[0], dgains[1]

    return _rowwise(name, body, geo, 256,
                    [(proj, "row"), (cos, "tab"), (sin_signed, "tab"), (q_gain, "full"), (k_gain, "full"), (dq, "row"), (dkv, "row")],
                    [("row", proj.shape[1], BF16), ("gacc", 1, LANES), ("gacc", 1, LANES)])


def _attn_geometry(geo):
    assert geo.s % ATTN_BLOCK == 0 and geo.l % ATTN_BLOCK == 0 and geo.seg >= BAND
    return geo.seg // ATTN_BLOCK, geo.s // ATTN_BLOCK


def _attn_mask(j, s0, geo):
    r = lax.broadcasted_iota(jnp.int32, (ATTN_BLOCK, BAND), 0)
    n = lax.broadcasted_iota(jnp.int32, (ATTN_BLOCK, BAND), 1)
    dist = (s0 - j * ATTN_BLOCK) + n - r
    return (jnp.abs(dist) <= WINDOW) & (s0 + n < geo.s)


def _attn_probs(q, keys, valid, n_ctx, sink):
    s = _dot(q, keys, "nt")
    if valid is not None:
        s = jnp.concatenate([s[:, :n_ctx], jnp.where(valid, s[:, n_ctx:], NEG_INF)], axis=1)
    m = jnp.maximum(jnp.max(s, axis=-1, keepdims=True), sink)
    e, e_sink = jnp.exp(s - m), jnp.exp(sink - m)
    inv = 1.0 / (jnp.sum(e, axis=-1, keepdims=True) + e_sink)
    return e * inv, e_sink * inv


def _attn_keys(ref, s0, geo, with_band):
    ctx = ref[geo.s:geo.seg, :]
    return jnp.concatenate([ctx, ref[pl.ds(s0, BAND), :]], axis=0) if with_band else ctx


def _attention(geo, qkv, sink, name):
    n_blocks, n_x_blocks = _attn_geometry(geo)
    qw, kw = N_HEADS * HEAD_DIM, N_KV_HEADS * HEAD_DIM
    group = N_HEADS // N_KV_HEADS

    def kern(sink_ref, q_ref, k_ref, v_ref, o_ref):
        j = pl.program_id(1)
        s0 = pl.multiple_of(jnp.clip((j - 1) * ATTN_BLOCK, 0, geo.seg - BAND), ATTN_BLOCK)

        def heads(with_band):
            valid = _attn_mask(j, s0, geo) if with_band else None
            k_all, v_all = _attn_keys(k_ref, s0, geo, with_band), _attn_keys(v_ref, s0, geo, with_band)
            for h in range(N_HEADS):
                kv = slice((h // group) * HEAD_DIM, (h // group + 1) * HEAD_DIM)
                p, _ = _attn_probs(q_ref[:, h * HEAD_DIM:(h + 1) * HEAD_DIM], k_all[:, kv], valid, geo.l, sink_ref[h])
                o_ref[:, h * HEAD_DIM:(h + 1) * HEAD_DIM] = _dot(p, v_all[:, kv], "nn").astype(BF16)

        pl.when(j < n_x_blocks)(lambda: heads(True))
        pl.when(j >= n_x_blocks)(lambda: heads(False))

    return pl.pallas_call(
        kern, name=name, grid=(geo.b, n_blocks),
        in_specs=[pl.BlockSpec(memory_space=pltpu.SMEM),
                  pl.BlockSpec((ATTN_BLOCK, qw), lambda b, j: (b * n_blocks + j, 0)),
                  pl.BlockSpec((geo.seg, kw), lambda b, j: (b, qw // kw)),
                  pl.BlockSpec((geo.seg, kw), lambda b, j: (b, qw // kw + 1))],
        out_specs=pl.BlockSpec((ATTN_BLOCK, qw), lambda b, j: (b * n_blocks + j, 0)),
        out_shape=jax.ShapeDtypeStruct((geo.r, qw), BF16),
        compiler_params=_cparams("parallel", "arbitrary"),
    )(sink, qkv, qkv, qkv)


def _attention_bwd(geo, qkv, sink, do, name):
    n_blocks, n_x_blocks = _attn_geometry(geo)
    qw, kw = N_HEADS * HEAD_DIM, N_KV_HEADS * HEAD_DIM
    group = N_HEADS // N_KV_HEADS

    def kern(sink_ref, q_ref, k_ref, v_ref, do_ref, dq_ref, dkv_ref, dsink_ref):
        b, j = pl.program_id(0), pl.program_id(1)
        s0 = pl.multiple_of(jnp.clip((j - 1) * ATTN_BLOCK, 0, geo.seg - BAND), ATTN_BLOCK)

        @pl.when(j == 0)
        def _():
            dkv_ref[...] = jnp.zeros_like(dkv_ref)

        @pl.when((j == 0) & (b == 0))
        def _():
            dsink_ref[...] = jnp.zeros_like(dsink_ref)

        def heads(with_band):
            valid = _attn_mask(j, s0, geo) if with_band else None
            k_all, v_all = _attn_keys(k_ref, s0, geo, with_band), _attn_keys(v_ref, s0, geo, with_band)
            for g in range(N_KV_HEADS):
                kv = slice(g * HEAD_DIM, (g + 1) * HEAD_DIM)
                keys, vals = k_all[:, kv], v_all[:, kv]
                group_heads = [slice(h * HEAD_DIM, (h + 1) * HEAD_DIM) for h in range(g * group, (g + 1) * group)]
                ds_rows, p_rows = [], []
                for h, hs in zip(range(g * group, (g + 1) * group), group_heads):
                    dout = do_ref[:, hs]
                    p, p_sink = _attn_probs(q_ref[:, hs], keys, valid, geo.l, sink_ref[h])
                    dp = _dot(dout, vals, "nt")
                    dsum = jnp.sum(p * dp, axis=-1, keepdims=True)
                    ds = (p * (dp - dsum)).astype(BF16)
                    dq_ref[:, hs] = _dot(ds, keys, "nn")
                    ds_rows.append(ds)
                    p_rows.append(p.astype(BF16))
                    dsink_ref[h:h + 1, :] += jnp.broadcast_to(-jnp.sum(p_sink * dsum, axis=0, keepdims=True), (1, LANES))
                q_rows = jnp.concatenate([q_ref[:, hs] for hs in group_heads], axis=0)
                do_rows = jnp.concatenate([do_ref[:, hs] for hs in group_heads], axis=0)
                dk = _dot(jnp.concatenate(ds_rows, axis=0), q_rows, "tn")
                dv = _dot(jnp.concatenate(p_rows, axis=0), do_rows, "tn")
                vv = slice(kw + g * HEAD_DIM, kw + (g + 1) * HEAD_DIM)
                dkv_ref[geo.s:geo.seg, kv] += dk[:geo.l]
                dkv_ref[geo.s:geo.seg, vv] += dv[:geo.l]
                if with_band:
                    dkv_ref[pl.ds(s0, BAND), kv] += dk[geo.l:]
                    dkv_ref[pl.ds(s0, BAND), vv] += dv[geo.l:]

        pl.when(j < n_x_blocks)(lambda: heads(True))
        pl.when(j >= n_x_blocks)(lambda: heads(False))

    return pl.pallas_call(
        kern, name=name, grid=(geo.b, n_blocks),
        in_specs=[pl.BlockSpec(memory_space=pltpu.SMEM),
                  pl.BlockSpec((ATTN_BLOCK, qw), lambda b, j: (b * n_blocks + j, 0)),
                  pl.BlockSpec((geo.seg, kw), lambda b, j: (b, qw // kw)),
                  pl.BlockSpec((geo.seg, kw), lambda b, j: (b, qw // kw + 1)),
                  pl.BlockSpec((ATTN_BLOCK, qw), lambda b, j: (b * n_blocks + j, 0))],
        out_specs=[pl.BlockSpec((ATTN_BLOCK, qw), lambda b, j: (b * n_blocks + j, 0)),
                   pl.BlockSpec((geo.seg, 2 * kw), lambda b, j: (b, 0)),
                   pl.BlockSpec((N_HEADS, LANES), lambda b, j: (0, 0))],
        out_shape=[jax.ShapeDtypeStruct((geo.r, qw), F32), jax.ShapeDtypeStruct((geo.r, 2 * kw), F32),
                   jax.ShapeDtypeStruct((N_HEADS, LANES), F32)],
        compiler_params=_cparams("arbitrary", "arbitrary"),
    )(sink, qkv, qkv, qkv, do)


RET_QK_W = RET_HEADS * RET_QK_DIM
K_SCALE = RET_QK_DIM ** -0.5


def _ret_prep(geo, proj, cos, sin_signed, name):
    def body(i, p, cs, sn):
        cs2, sn2 = jnp.concatenate([cs] * RET_HEADS, axis=1), jnp.concatenate([sn] * RET_HEADS, axis=1)
        q = _rope(p[:, :RET_QK_W], cs2, sn2, RET_QK_DIM // 4)
        k = _rope(p[:, RET_QK_W:2 * RET_QK_W], cs2, sn2, RET_QK_DIM // 4) * K_SCALE
        return jnp.concatenate([q, k, p[:, 2 * RET_QK_W:]], axis=1)

    return _rowwise(name, body, geo, 128, [(proj, ("rowc", 2 * RET_QK_W + RET_VWIDTH, 0)), (cos, "tab"), (sin_signed, "tab")],
                    [("row", 2 * RET_QK_W + RET_VWIDTH, BF16)])


def _ret_prep_bwd(geo, dq, dk, dv, dgate, cos, sin_signed, name):
    def body(i, dqv, dkv, dvv, dg, cs, sn):
        cs2, sn2 = jnp.concatenate([cs] * RET_HEADS, axis=1), jnp.concatenate([sn] * RET_HEADS, axis=1)
        dkv = dkv * K_SCALE
        dqv = dqv * cs2 + _swap_halves(dqv * sn2, RET_QK_DIM // 4)
        dkv = dkv * cs2 + _swap_halves(dkv * sn2, RET_QK_DIM // 4)
        return jnp.concatenate([dqv, dkv, dvv, dg], axis=1)

    return _rowwise(name, body, geo, 128,
                    [(dq, "row"), (dk, "row"), (dv, "row"), (dgate, "row"), (cos, "tab"), (sin_signed, "tab")],
                    [("row", 2 * RET_QK_W + 2 * RET_VWIDTH, BF16)])


def _ret_step(state, q, k, v, lg, rev):
    c = RET_CHUNK
    ri = lax.broadcasted_iota(jnp.int32, (c, 1), 0).astype(F32)
    cj = lax.broadcasted_iota(jnp.int32, (1, c), 1).astype(F32)
    if rev:
        dist, q_decay, k_decay = cj - ri, jnp.exp(lg * (c - ri)), jnp.exp(lg * ri)
    else:
        dist, q_decay, k_decay = ri - cj, jnp.exp(lg * (ri + 1.0)), jnp.exp(lg * (c - 1.0 - ri))
    intra = jnp.where(dist >= 0, jnp.exp(lg * jnp.maximum(dist, 0.0)), 0.0)
    scores = _mm(q, k, "nt") * intra
    out = _mm(scores, v, "nn") + _mm(q, state, "nn") * q_decay
    new_state = state * jnp.exp(lg * c) + _mm(k * k_decay, v, "tn")
    return new_state, out


def _ret_state0(kc, vc, lg, rev):
    n = kc.shape[0]
    t = lax.broadcasted_iota(jnp.int32, (n, 1), 0).astype(F32)
    decay = jnp.exp(lg * t) if rev else jnp.exp(lg * (n - 1.0 - t))
    return _mm(kc * decay, vc, "tn")


def _ret_specs(geo):
    nq = RET_HEADS
    return [pl.BlockSpec((2 * RET_HEADS, LANES), lambda b, h: (0, 0)),
            pl.BlockSpec((geo.seg, RET_QK_DIM), lambda b, h: (b, h)),
            pl.BlockSpec((geo.seg, RET_QK_DIM), lambda b, h: (b, nq + h)),
            pl.BlockSpec((geo.seg, RET_V_DIM), lambda b, h: (b, nq + h))]


def _retention(geo, qkv, log_g, name):
    nc = geo.s // RET_CHUNK

    def kern(lg_ref, q_ref, k_ref, v_ref, o_ref, st_ref):
        h = pl.program_id(1)
        for d, rev in ((0, False), (1, True)):
            lg = lg_ref[pl.ds(d * RET_HEADS + h, 1), 0:1]
            st_ref[...] = _ret_state0(k_ref[geo.s:geo.seg, :].astype(F32), v_ref[geo.s:geo.seg, :].astype(F32), lg, rev)

            def chunk(ci, carry, d=d, rev=rev, lg=lg):
                r0 = pl.multiple_of((nc - 1 - ci if rev else ci) * RET_CHUNK, RET_CHUNK)
                rows = pl.ds(r0, RET_CHUNK)
                new_state, out = _ret_step(st_ref[...], q_ref[rows, :].astype(F32), k_ref[rows, :].astype(F32),
                                           v_ref[rows, :].astype(F32), lg, rev)
                st_ref[...] = new_state
                if d == 0:
                    o_ref[rows, :] = out
                else:
                    o_ref[rows, :] += out
                return carry

            lax.fori_loop(0, nc, chunk, 0)
        o_ref[geo.s:geo.seg, :] = jnp.zeros((geo.l, RET_V_DIM), F32)

    return pl.pallas_call(
        kern, name=name, grid=(geo.b, RET_HEADS), in_specs=_ret_specs(geo),
        out_specs=pl.BlockSpec((geo.seg, RET_V_DIM), lambda b, h: (b, h)),
        out_shape=jax.ShapeDtypeStruct((geo.r, RET_VWIDTH), F32),
        scratch_shapes=[pltpu.VMEM((RET_QK_DIM, RET_V_DIM), F32)],
        compiler_params=_cparams("parallel", "arbitrary"),
    )(log_g, qkv, qkv, qkv)


def _retention_bwd(geo, qkv, log_g, do, name):
    nc = geo.s // RET_CHUNK
    ctx = slice(geo.s, geo.seg)

    def kern(lg_ref, q_ref, k_ref, v_ref, do_ref, dq_ref, dk_ref, dv_ref, dlg_ref, states_ref, cur_ref, dst_ref):
        b, h = pl.program_id(0), pl.program_id(1)

        @pl.when((b == 0) & (h == 0))
        def _():
            dlg_ref[...] = jnp.zeros_like(dlg_ref)

        for d, rev in ((0, False), (1, True)):
            row = pl.ds(d * RET_HEADS + h, 1)
            lg = lg_ref[row, 0:1]
            kc, vc = k_ref[ctx, :].astype(F32), v_ref[ctx, :].astype(F32)
            cur_ref[...] = _ret_state0(kc, vc, lg, rev)

            def rows_of(ci, rev=rev):
                return pl.ds(pl.multiple_of((nc - 1 - ci if rev else ci) * RET_CHUNK, RET_CHUNK), RET_CHUNK)

            def load(rows):
                return q_ref[rows, :].astype(F32), k_ref[rows, :].astype(F32), v_ref[rows, :].astype(F32)

            def replay(ci, carry, rev=rev, lg=lg, rows_of=rows_of, load=load):
                states_ref[ci] = cur_ref[...]
                cur_ref[...] = _ret_step(cur_ref[...], *load(rows_of(ci)), lg, rev)[0]
                return carry

            lax.fori_loop(0, nc, replay, 0)
            dst_ref[...] = jnp.zeros_like(dst_ref)

            def back(t, dlg, d=d, rev=rev, lg=lg, rows_of=rows_of, load=load):
                ci = nc - 1 - t
                rows = rows_of(ci)
                _, vjp = jax.vjp(lambda st, q, k, v, g: _ret_step(st, q, k, v, g, rev), states_ref[ci], *load(rows), lg)
                dstate, dq, dk, dv, dg = vjp((dst_ref[...], do_ref[rows, :]))
                dst_ref[...] = dstate
                if d == 0:
                    dq_ref[rows, :], dk_ref[rows, :], dv_ref[rows, :] = dq, dk, dv
                else:
                    dq_ref[rows, :] += dq
                    dk_ref[rows, :] += dk
                    dv_ref[rows, :] += dv
                return dlg + dg

            dlg = lax.fori_loop(0, nc, back, jnp.zeros((1, 1), F32))
            _, vjp = jax.vjp(lambda kk, vv, g: _ret_state0(kk, vv, g, rev), kc, vc, lg)
            dkc, dvc, dg = vjp(dst_ref[...])
            if d == 0:
                dk_ref[ctx, :], dv_ref[ctx, :] = dkc, dvc
            else:
                dk_ref[ctx, :] += dkc
                dv_ref[ctx, :] += dvc
            dlg_ref[row, :] += jnp.broadcast_to(dlg + dg, (1, LANES))
        dq_ref[ctx, :] = jnp.zeros((geo.l, RET_QK_DIM), F32)

    nq = RET_HEADS
    return pl.pallas_call(
        kern, name=name, grid=(geo.b, RET_HEADS),
        in_specs=_ret_specs(geo) + [pl.BlockSpec((geo.seg, RET_V_DIM), lambda b, h: (b, h))],
        out_specs=[pl.BlockSpec((geo.seg, RET_QK_DIM), lambda b, h: (b, h)),
                   pl.BlockSpec((geo.seg, RET_QK_DIM), lambda b, h: (b, h)),
                   pl.BlockSpec((geo.seg, RET_V_DIM), lambda b, h: (b, h)),
                   pl.BlockSpec((2 * RET_HEADS, LANES), lambda b, h: (0, 0))],
        out_shape=[jax.ShapeDtypeStruct((geo.r, RET_QK_W), F32), jax.ShapeDtypeStruct((geo.r, RET_QK_W), F32),
                   jax.ShapeDtypeStruct((geo.r, RET_VWIDTH), F32), jax.ShapeDtypeStruct((2 * RET_HEADS, LANES), F32)],
        scratch_shapes=[pltpu.VMEM((nc, RET_QK_DIM, RET_V_DIM), F32), pltpu.VMEM((RET_QK_DIM, RET_V_DIM), F32),
                        pltpu.VMEM((RET_QK_DIM, RET_V_DIM), F32)],
        compiler_params=_cparams("arbitrary", "arbitrary"),
    )(log_g, qkv, qkv, qkv, do)


def _gated(o, g, gain):
    outs = []
    for h in range(RET_HEADS):
        cols = slice(h * RET_V_DIM, (h + 1) * RET_V_DIM)
        oh = o[:, cols]
        mu = jnp.mean(oh, axis=-1, keepdims=True)
        var = jnp.mean(jnp.square(oh - mu), axis=-1, keepdims=True)
        outs.append(_silu(g[:, cols]) * ((oh - mu) * lax.rsqrt(var + EPS) * gain[:, cols]))
    return jnp.concatenate(outs, axis=1)


def _ret_gated(geo, o, proj, gain, name):
    def body(i, ov, gv, gn):
        return _gated(ov, gv, gn)

    gate_block = (2 * RET_QK_W + RET_VWIDTH) // RET_VWIDTH
    return _rowwise(name, body, geo, 128, [(o, "row"), (proj, ("rowc", RET_VWIDTH, gate_block)), (gain, "full")],
                    [("row", RET_VWIDTH, BF16)])


def _ret_gated_bwd(geo, o, proj, gain, dout, name):
    def body(i, ov, gv, gn, dv):
        _, vjp = jax.vjp(_gated, ov, gv, gn)
        return vjp(dv)

    gate_block = (2 * RET_QK_W + RET_VWIDTH) // RET_VWIDTH
    return _rowwise(name, body, geo, 128,
                    [(o, "row"), (proj, ("rowc", RET_VWIDTH, gate_block)), (gain, "full"), (dout, "row")],
                    [("row", RET_VWIDTH, F32), ("row", RET_VWIDTH, F32), ("gacc", 1, RET_VWIDTH)])


def _whole(name, fn, out_shapes, *arrays):
    n = len(arrays)

    def kern(*refs):
        res = fn(*[r[...] for r in refs[:n]])
        for ref, val in zip(refs[n:], res):
            ref[...] = val.astype(ref.dtype)

    return pl.pallas_call(kern, name=name, out_shape=out_shapes)(*arrays)


def _rope_tables(geo, head_dim):
    rows = geo.s // GRID_W
    row = jnp.broadcast_to(jnp.arange(rows, dtype=jnp.int32)[:, None], (rows, GRID_W)).reshape(geo.s)
    col = jnp.broadcast_to(jnp.arange(GRID_W, dtype=jnp.int32)[None, :], (rows, GRID_W)).reshape(geo.s)
    axis_dim = head_dim // 2
    inv = ROPE_BASE ** (-jnp.arange(0, axis_dim, 2, dtype=F32) / axis_dim)
    ang_r = row.astype(F32)[:, None] * inv
    ang_c = col.astype(F32)[:, None] * inv
    cos = jnp.concatenate([jnp.cos(ang_r)] * 2 + [jnp.cos(ang_c)] * 2, axis=1)
    sin = jnp.concatenate([-jnp.sin(ang_r), jnp.sin(ang_r), -jnp.sin(ang_c), jnp.sin(ang_c)], axis=1)
    cos = jnp.concatenate([cos, jnp.ones((geo.l, head_dim), F32)], axis=0)
    sin = jnp.concatenate([sin, jnp.zeros((geo.l, head_dim), F32)], axis=0)
    reps = max(1, LANES // head_dim)
    return jnp.tile(cos, (1, reps)), jnp.tile(sin, (1, reps))


def _row_tile(r):
    return next(t for t in (1024, 512, 256, 128) if r % t == 0)


MOD_ROWS = 8


def _local_step(x, c, ctx, target, sp, wts):
    nb, s, d = x.shape
    geo = _Rows(nb, s, ctx.shape[1])
    assert nb + 1 <= MOD_ROWS and d == D_MODEL
    tm = _row_tile(geo.r)
    z = jnp.concatenate([x, ctx], axis=1).reshape(geo.r, d)
    cvec = jnp.concatenate([c, sp["c_ctx"][None, :], jnp.zeros((MOD_ROWS - nb - 1, d), F32)], axis=0)
    cact, = _whole("cond_silu", lambda v: (_silu(v),), [jax.ShapeDtypeStruct(cvec.shape, F32)], cvec)
    cos64, sin64 = _rope_tables(geo, HEAD_DIM)
    cos256, sin256 = _rope_tables(geo, RET_QK_DIM)
    q_gain = jnp.tile(sp["q_norm"].reshape(1, HEAD_DIM), (1, LANES // HEAD_DIM))
    k_gain = jnp.tile(sp["k_norm"].reshape(1, HEAD_DIM), (1, LANES // HEAD_DIM))
    sink = sp["sink"].reshape(N_HEADS)
    log_g = jnp.broadcast_to(sp["log_g"].reshape(2 * RET_HEADS, 1), (2 * RET_HEADS, LANES))
    gn_g = sp["gn_g"].reshape(1, RET_VWIDTH)

    saved = []
    for i in range(2):
        mod = _mm_nn(cact, wts["ada"][i], F32, f"mod{i}", MOD_ROWS, wts["ada"][i].shape[2], d, bias=sp["ada_b"][i][None, :])
        mod3 = mod[:nb + 1, None, :]
        n1, n2 = sp["norm1_g"][i][None, :], sp["norm2_g"][i][None, :]
        h1 = _norm_mod(geo, z, n1, mod3, 0, f"norm1_{i}")
        if i == 0:
            proj = _mm_nn(h1, wts["attn_qkv"], F32, "attn_qkv", tm, wts["attn_qkv"].shape[2], d)
            prep = _attn_prep(geo, proj, cos64, sin64, q_gain, k_gain, "attn_prep")
            o = _attention(geo, prep, sink, "attn")
            oraw = None
            mix = _mm_nn(o, wts["attn_o"], F32, "attn_out", tm, 1024, 1024)
        else:
            proj = _mm_nn(h1, wts["ret_qkvg"], BF16, "ret_qkvg", tm, 512, d)
            prep = _ret_prep(geo, proj, cos256, sin256, "ret_prep")
            oraw = _retention(geo, prep, log_g, "ret")
            o = _ret_gated(geo, oraw, proj, gn_g, "ret_gated")
            mix = _mm_nn(o, wts["ret_o"], F32, "ret_out", tm, 1024, 1024)
        zmid = _gate_residual(geo, z, mix, mod3, 2 * d, f"res_mix{i}")
        h2 = _norm_mod(geo, zmid, n2, mod3, 3 * d, f"norm2_{i}")
        u = _mm_nn(h2, wts["ffn_in"][i], BF16, f"ffn_in{i}", tm, wts["ffn_in"][i].shape[2], d)
        a = _swiglu(geo, u, f"swiglu{i}")
        f = _mm_nn(a, wts["ffn_out"][i], F32, f"ffn_out{i}", tm, 1024, D_FF // 2)
        zout = _gate_residual(geo, zmid, f, mod3, 5 * d, f"res_ffn{i}")
        saved.append(dict(z=z, mod3=mod3, n1=n1, n2=n2, h1=h1, proj=proj, prep=prep, o=o, oraw=oraw, mix=mix, zmid=zmid,
                          h2=h2, u=u, a=a, f=f))
        z = zout

    dz, loss = _loss_head(geo, z, target.reshape(nb * s, d), "loss")

    big, small = {}, {}
    dmods = [None, None]
    for i in (1, 0):
        sv = saved[i]
        mod3 = sv["mod3"]
        df, dg2 = _gate_residual_bwd(geo, dz, sv["f"], mod3, 5 * d, f"res_ffn_bwd{i}")
        da = _mm_nt(df, wts["ffn_out"][i], BF16, f"ffn_out_dx{i}", tm, D_FF // 2, 1024)
        big[f"ffn_out{i}"] = _mm_tn(sv["a"], df, f"ffn_out_dw{i}", D_FF // 2, 1024, tm).reshape(N_CHIPS, D_FF // N_CHIPS, d)
        du = _swiglu_bwd(geo, sv["u"], da, f"swiglu_bwd{i}")
        n4 = wts["ffn_in"][i].shape[2]
        dh2 = _mm_nt(du, wts["ffn_in"][i], BF16, f"ffn_in_dx{i}", tm, 1024, n4)
        big[f"ffn_in{i}"] = _mm_tn(sv["h2"], du, f"ffn_in_dw{i}", 1024, n4, tm, shards=N_CHIPS)
        dzmid, dsh2, dsc2, dn2 = _norm_mod_bwd(geo, sv["zmid"], sv["n2"], mod3, 3 * d, dh2, dz, f"norm2_bwd{i}")
        dmix, dg1 = _gate_residual_bwd(geo, dzmid, sv["mix"], mod3, 2 * d, f"res_mix_bwd{i}")
        if i == 0:
            do = _mm_nt(dmix, wts["attn_o"], BF16, "attn_out_dx", tm, 1024, 1024)
            big["attn_o"] = _mm_tn(sv["o"], dmix, "attn_out_dw", 1024, 1024, tm).reshape(N_CHIPS, 1024 // N_CHIPS, d)
            dq, dkv, dsink = _attention_bwd(geo, sv["prep"], sink, do, "attn_bwd")
            dproj, dqg, dkg = _attn_prep_bwd(geo, sv["proj"], cos64, sin64, q_gain, k_gain, dq, dkv, "attn_prep_bwd")
            small["q_norm"] = dqg[0, :HEAD_DIM] + dqg[0, HEAD_DIM:]
            small["k_norm"] = dkg[0, :HEAD_DIM] + dkg[0, HEAD_DIM:]
            small["sink"] = dsink[:, 0]
            wq = wts["attn_qkv"]
            dh1 = _mm_nt(dproj, wq, BF16, "attn_qkv_dx", tm, 1024, wq.shape[2])
            big["attn_qkv"] = _mm_tn(sv["h1"], dproj, "attn_qkv_dw", 1024, wq.shape[2], tm, shards=N_CHIPS)
        else:
            do = _mm_nt(dmix, wts["ret_o"], BF16, "ret_out_dx", tm, 1024, 1024)
            big["ret_o"] = _mm_tn(sv["o"], dmix, "ret_out_dw", 1024, 1024, tm).reshape(N_CHIPS, RET_VWIDTH // N_CHIPS, d)
            doraw, dgate, dgn = _ret_gated_bwd(geo, sv["oraw"], sv["proj"], gn_g, do, "ret_gated_bwd")
            small["gn_g"] = dgn[0]
            dq, dk, dv, dlg = _retention_bwd(geo, sv["prep"], log_g, doraw, "ret_bwd")
            small["log_g"] = dlg[:, 0].reshape(2, RET_HEADS)
            dproj = _ret_prep_bwd(geo, dq, dk, dv, dgate, cos256, sin256, "ret_prep_bwd")
            wq = wts["ret_qkvg"]
            dh1 = _mm_nt(dproj, wq, BF16, "ret_qkvg_dx", tm, 1024, 512)
            big["ret_qkvg"] = _mm_tn(sv["h1"], dproj, "ret_qkvg_dw", 1024, 512, tm, shards=N_CHIPS)
        dz, dsh1, dsc1, dn1 = _norm_mod_bwd(geo, sv["z"], sv["n1"], mod3, 0, dh1, dzmid, f"norm1_bwd{i}")
        small[f"norm1_g{i}"], small[f"norm2_g{i}"] = dn1[0], dn2[0]
        parts = [dsh1, dsc1, dg1, dsh2, dsc2, dg2]
        rows = jnp.concatenate([jnp.concatenate([p[:nb, 0, :] for p in parts], axis=1),
                                jnp.concatenate([jnp.sum(p[nb:, 0, :], axis=0, keepdims=True) for p in parts], axis=1),
                                jnp.zeros((MOD_ROWS - nb - 1, 6 * d), F32)], axis=0)
        dmods[i] = rows
        small[f"ada_b{i}"] = jnp.sum(rows, axis=0)
        big[f"ada{i}"] = _mm_tn(cact, rows, f"ada_dw{i}", 1024, wts["ada"][i].shape[2], MOD_ROWS, shards=N_CHIPS)

    dcact = [_mm_nt(dmods[i], wts["ada"][i], F32, f"ada_dx{i}", MOD_ROWS, 1024, wts["ada"][i].shape[2]) for i in range(2)]

    def silu_bwd(v, d0, d1):
        sg = _sigmoid(v)
        return ((d0 + d1) * (sg * (1.0 + v * (1.0 - sg))),)

    dcvec, = _whole("cond_silu_bwd", silu_bwd, [jax.ShapeDtypeStruct(cvec.shape, F32)], cvec, dcact[0], dcact[1])
    small["c_ctx"] = dcvec[nb]
    return loss, dz, big, small


def _adamw(w, g, m, v, name):
    rows, cols = w.shape
    tr = next((t for t in (256, 128, 64, 32, 16, 8) if rows % t == 0), rows)
    c1 = 1.0 - ADAM_B1 ** ADAM_STEP
    c2 = 1.0 - ADAM_B2 ** ADAM_STEP

    def kern(w_ref, g_ref, m_ref, v_ref, d_ref, nm_ref, nv_ref):
        gv = g_ref[...]
        nm = ADAM_B1 * m_ref[...] + (1.0 - ADAM_B1) * gv
        nv = ADAM_B2 * v_ref[...] + (1.0 - ADAM_B2) * jnp.square(gv)
        d_ref[...] = -ADAM_LR * ((nm / c1) / (jnp.sqrt(nv / c2) + ADAM_EPS) + ADAM_WD * w_ref[...])
        nm_ref[...] = nm
        nv_ref[...] = nv

    spec = pl.BlockSpec((tr, cols), lambda i: (i, 0))
    return pl.pallas_call(
        kern, name=name, grid=(rows // tr,), in_specs=[spec] * 4, out_specs=[spec] * 3,
        out_shape=[jax.ShapeDtypeStruct(w.shape, F32)] * 3, compiler_params=_cparams("parallel"),
    )(w, g, m, v)


N_DEVICES = 8


def _mesh_pos():
    return lax.axis_index("x"), lax.axis_index("y"), lax.axis_index("c")


def _other_chips(x, y):
    return [(1 - x, y), (x, 1 - y), (1 - x, 1 - y)]


def _hbm(n):
    return [pl.BlockSpec(memory_space=pl.ANY)] * n


def _remote(src, dst, send_sem, recv_sem, device):
    return pltpu.make_async_remote_copy(src_ref=src, dst_ref=dst, send_sem=send_sem, recv_sem=recv_sem,
                                        device_id=device, device_id_type=MESH)


def _scalar_spec(grid, in_specs, out_specs):
    return pltpu.PrefetchScalarGridSpec(num_scalar_prefetch=1, grid=grid, in_specs=in_specs, out_specs=out_specs)


def _place_shard(shard, pos, name):
    r, cols = shard.shape
    tr = _slab_tile(r)

    def kern(pos_ref, s_ref, o_ref):
        o_ref[...] = s_ref[...].astype(BF16)

    return pl.pallas_call(
        kern, name=name, out_shape=jax.ShapeDtypeStruct((N_CHIPS, r, cols), BF16),
        grid_spec=_scalar_spec((r // tr,), [pl.BlockSpec((tr, cols), lambda i, p: (i, 0))],
                               pl.BlockSpec((None, tr, cols), lambda i, p: (p[1], i, 0))),
        compiler_params=_cparams("parallel"),
    )(pos, shard)


def _gather_shards(placed):
    n = len(placed)

    def body(*refs):
        outs = refs[n:2 * n]
        send_sems, recv_sems, fwd_send, fwd_recv = refs[2 * n:]
        x, y, c = _mesh_pos()
        chip = 2 * x + y
        others = _other_chips(x, y)
        sibling = (x, y, 1 - c)

        def half(w, which):
            r2 = placed[w].shape[1] // 2
            return pl.ds(which * r2, r2)

        sends = []
        for w in range(n):
            for k, (px, py) in enumerate(others):
                mine = outs[w].at[chip, half(w, c)]
                cp = _remote(mine, mine, send_sems.at[w, k], recv_sems.at[w, k], (px, py, c))
                cp.start()
                sends.append(cp)
        for w in range(n):
            for k, (px, py) in enumerate(others):
                got = outs[w].at[2 * px + py, half(w, c)]
                _remote(got, got, send_sems.at[w, k], recv_sems.at[w, k], (px, py, c)).wait_recv()
                cp = _remote(got, got, fwd_send.at[w, k], fwd_recv.at[w, k], sibling)
                cp.start()
                sends.append(cp)
        for w in range(n):
            for k, (px, py) in enumerate(others):
                theirs = outs[w].at[2 * px + py, half(w, 1 - c)]
                _remote(theirs, theirs, fwd_send.at[w, k], fwd_recv.at[w, k], sibling).wait_recv()
        for cp in sends:
            cp.wait_send()

    return pl.pallas_call(
        body, name="gather_weights", in_specs=_hbm(n), out_specs=_hbm(n),
        out_shape=[jax.ShapeDtypeStruct(p.shape, p.dtype) for p in placed],
        input_output_aliases={w: w for w in range(n)},
        scratch_shapes=[pltpu.SemaphoreType.DMA((n, 3))] * 4,
    )(*placed)


def _pair_swap(grads):
    n = len(grads)

    def body(*refs):
        ins, land = refs[:n], refs[n:2 * n]
        send_sems, recv_sems = refs[2 * n:]
        x, y, c = _mesh_pos()
        copies = []
        for w in range(n):
            r2 = grads[w].shape[1] // 2
            rc = _remote(ins[w].at[:, pl.ds((1 - c) * r2, r2)], land[w], send_sems.at[w], recv_sems.at[w], (x, y, 1 - c))
            rc.start()
            copies.append(rc)
        for cp in copies:
            cp.wait()

    return pl.pallas_call(
        body, name="grads_pair_swap", in_specs=_hbm(n), out_specs=_hbm(n),
        out_shape=[jax.ShapeDtypeStruct((N_CHIPS, g.shape[1] // 2, g.shape[2]), F32) for g in grads],
        scratch_shapes=[pltpu.SemaphoreType.DMA((n,))] * 2,
    )(*grads)


def _chip_exchange(hs):
    n = len(hs)

    def body(*refs):
        ins, land = refs[:n], refs[n:2 * n]
        send_sems, recv_sems = refs[2 * n:]
        x, y, c = _mesh_pos()
        chip = 2 * x + y
        others = _other_chips(x, y)
        sends = []
        for w in range(n):
            for k, (px, py) in enumerate(others):
                cp = _remote(ins[w].at[2 * px + py], land[w].at[chip], send_sems.at[w, k], recv_sems.at[w, k], (px, py, c))
                cp.start()
                sends.append(cp)
        for w in range(n):
            for k, (px, py) in enumerate(others):
                got = land[w].at[2 * px + py]
                _remote(got, got, send_sems.at[w, k], recv_sems.at[w, k], (px, py, c)).wait_recv()
        for cp in sends:
            cp.wait_send()

    return pl.pallas_call(
        body, name="grads_chip_exchange", in_specs=_hbm(n), out_specs=_hbm(n),
        out_shape=[jax.ShapeDtypeStruct(h.shape, h.dtype) for h in hs],
        scratch_shapes=[pltpu.SemaphoreType.DMA((n, 3))] * 2,
    )(*hs)


def _pair_share(ts):
    n = len(ts)

    def body(*refs):
        outs = refs[n:2 * n]
        send_sems, recv_sems = refs[2 * n:]
        x, y, c = _mesh_pos()
        sends = []
        for w in range(n):
            r2 = ts[w].shape[0] // 2
            mine = outs[w].at[pl.ds(c * r2, r2)]
            rc = _remote(mine, mine, send_sems.at[w], recv_sems.at[w], (x, y, 1 - c))
            rc.start()
            sends.append(rc)
        for w in range(n):
            r2 = ts[w].shape[0] // 2
            theirs = outs[w].at[pl.ds((1 - c) * r2, r2)]
            _remote(theirs, theirs, send_sems.at[w], recv_sems.at[w], (x, y, 1 - c)).wait_recv()
            sends[w].wait_send()

    return pl.pallas_call(
        body, name="grads_pair_share", in_specs=_hbm(n), out_specs=_hbm(n),
        out_shape=[jax.ShapeDtypeStruct(t.shape, F32) for t in ts],
        input_output_aliases={w: w for w in range(n)},
        scratch_shapes=[pltpu.SemaphoreType.DMA((n,))] * 2,
    )(*ts)


def _slab_tile(rows):
    return next(t for t in (512, 256, 176, 128, 64, 32, 16) if rows % t == 0)


def _sum_pair(grad, land, pos, name):
    _, r2, cols = land.shape
    tr = _slab_tile(r2)
    nt = r2 // tr

    def kern(pos_ref, a_ref, b_ref, o_ref):
        o_ref[...] = (a_ref[...] + b_ref[...]).astype(BF16)

    spec = pl.BlockSpec((None, tr, cols), lambda j, i, p: (j, i, 0))
    return pl.pallas_call(
        kern, name=name, out_shape=jax.ShapeDtypeStruct(land.shape, BF16),
        grid_spec=_scalar_spec((N_CHIPS, nt), [pl.BlockSpec((None, tr, cols), lambda j, i, p: (j, p[0] * nt + i, 0)), spec], spec),
        compiler_params=_cparams("parallel", "parallel"),
    )(pos, grad, land)


def _sum_chips(hs, land, pos, name):
    _, r2, cols = land.shape
    tr = _slab_tile(r2)
    nt = r2 // tr

    def kern(pos_ref, h_ref, l_ref, o_ref):
        acc = jnp.zeros((tr, cols), F32)
        own = h_ref[...].astype(F32)
        for k in range(N_CHIPS):
            acc = acc + jnp.where(pos_ref[1] == k, own, l_ref[k].astype(F32))
        o_ref[...] = acc

    return pl.pallas_call(
        kern, name=name, out_shape=jax.ShapeDtypeStruct((2 * r2, cols), F32),
        grid_spec=_scalar_spec((nt,), [pl.BlockSpec((None, tr, cols), lambda i, p: (p[1], i, 0)),
                                       pl.BlockSpec((N_CHIPS, tr, cols), lambda i, p: (0, i, 0))],
                               pl.BlockSpec((tr, cols), lambda i, p: (p[0] * nt + i, 0))),
        compiler_params=_cparams("parallel"),
    )(pos, hs, land)


def _reduce_scatter(grads, pos):
    land = _pair_swap(grads)
    hs = [_sum_pair(g, l, pos, f"grads_pair_sum{w}") for w, (g, l) in enumerate(zip(grads, land))]
    land2 = _chip_exchange(hs)
    ts = [_sum_chips(h, l, pos, f"grads_chip_sum{w}") for w, (h, l) in enumerate(zip(hs, land2))]
    return _pair_share(ts)


def _all_reduce_small(v, name):
    def body(v_ref, o_ref, land_ref, send_sems, recv_sems):
        x, y, c = _mesh_pos()
        me = 4 * x + 2 * y + c
        land_ref[me] = v_ref[...]
        for t in range(N_DEVICES):
            @pl.when(t != me)
            def _(t=t):
                _remote(v_ref, land_ref.at[me], send_sems.at[t], recv_sems.at[me], (t // 4, (t // 2) % 2, t % 2)).start()
        for t in range(N_DEVICES):
            @pl.when(t != me)
            def _(t=t):
                _remote(v_ref, land_ref.at[t], send_sems.at[t], recv_sems.at[t], (t // 4, (t // 2) % 2, t % 2)).wait()
        acc = land_ref[0]
        for t in range(1, N_DEVICES):
            acc = acc + land_ref[t]
        o_ref[...] = acc

    vmem = pl.BlockSpec(memory_space=pltpu.VMEM)
    return pl.pallas_call(
        body, name=name, in_specs=[vmem], out_specs=vmem, out_shape=jax.ShapeDtypeStruct(v.shape, F32),
        scratch_shapes=[pltpu.VMEM((N_DEVICES,) + v.shape, F32), pltpu.SemaphoreType.DMA((N_DEVICES,)),
                        pltpu.SemaphoreType.DMA((N_DEVICES,))],
    )(v)


SMALL_ROWS = 24


def _pack_small(small, dlogit):
    d = D_MODEL
    misc = jnp.zeros((d,), F32)
    misc = misc.at[0:HEAD_DIM].set(small["q_norm"]).at[128:128 + HEAD_DIM].set(small["k_norm"])
    misc = misc.at[256:256 + N_HEADS].set(small["sink"]).at[384:384 + 2 * RET_HEADS].set(dlogit.reshape(-1))
    rows = [small["ada_b0"].reshape(6, d), small["ada_b1"].reshape(6, d), small["norm1_g0"][None], small["norm1_g1"][None],
            small["norm2_g0"][None], small["norm2_g1"][None], small["c_ctx"][None], small["gn_g"].reshape(2, d), misc[None]]
    buf = jnp.concatenate(rows, axis=0)
    return jnp.concatenate([buf, jnp.zeros((SMALL_ROWS - buf.shape[0], d), F32)], axis=0)


def _unpack_small(buf):
    d = D_MODEL
    misc = buf[19]
    return dict(ada_b=buf[0:12].reshape(2, 6 * d), norm1_g=buf[12:14], norm2_g=buf[14:16], c_ctx=buf[16],
                gn_g=buf[17:19].reshape(2 * d), q_norm=misc[0:HEAD_DIM], k_norm=misc[128:128 + HEAD_DIM],
                sink=misc[256:256 + N_HEADS], decay=misc[384:384 + 2 * RET_HEADS])


def kernel(x, c, ctx, c_ctx, ada_w, ada_b, norm1_g, norm2_g, ffn_w_in, ffn_w_out, attn_w_qkv, attn_q_norm, attn_k_norm, attn_sink, attn_w_o, ret_w_qkvg, ret_decay_logit, ret_gn_g, ret_w_o, loss_target, m_c_ctx, m_ada_w, m_ada_b, m_norm1_g, m_norm2_g, m_ffn_w_in, m_ffn_w_out, m_attn_w_qkv, m_attn_q_norm, m_attn_k_norm, m_attn_sink, m_attn_w_o, m_ret_w_qkvg, m_ret_decay_logit, m_ret_gn_g, m_ret_w_o, v_c_ctx, v_ada_w, v_ada_b, v_norm1_g, v_norm2_g, v_ffn_w_in, v_ffn_w_out, v_attn_w_qkv, v_attn_q_norm, v_attn_k_norm, v_attn_sink, v_attn_w_o, v_ret_w_qkvg, v_ret_decay_logit, v_ret_gn_g, v_ret_w_o):
    xi, yi, ci = _mesh_pos()
    chip = 2 * xi + yi
    nb, s, d = x.shape
    gn_shard = ret_gn_g.shape[1]

    shards = dict(ada0=ada_w[0], ada1=ada_w[1], ffn_in0=ffn_w_in[0], ffn_in1=ffn_w_in[1], ffn_out0=ffn_w_out[0],
                  ffn_out1=ffn_w_out[1], attn_qkv=attn_w_qkv[0], attn_o=attn_w_o[0], ret_qkvg=ret_w_qkvg[0], ret_o=ret_w_o[0])
    names = list(shards)
    pos = jnp.stack([ci, chip]).astype(jnp.int32)
    full = dict(zip(names, _gather_shards([_place_shard(shards[k], pos, f"place_{k}") for k in names])))
    gn_mine = jnp.where(ci == 0, ret_gn_g[0], jnp.zeros_like(ret_gn_g[0]))
    gn_place = lax.dynamic_update_slice(jnp.zeros((RET_VWIDTH,), F32), gn_mine, (chip * gn_shard,))
    gn_full = _all_reduce_small(gn_place.reshape(2, d), "gather_gn_gain").reshape(RET_VWIDTH)

    wts = dict(ada=[full["ada0"], full["ada1"]], ffn_in=[full["ffn_in0"], full["ffn_in1"]],
               ffn_out=[full["ffn_out0"].reshape(D_FF, d), full["ffn_out1"].reshape(D_FF, d)],
               attn_qkv=full["attn_qkv"], attn_o=full["attn_o"].reshape(N_HEADS * HEAD_DIM, d),
               ret_qkvg=full["ret_qkvg"], ret_o=full["ret_o"].reshape(RET_VWIDTH, d))
    decay_logit = ret_decay_logit[0]
    sp = dict(c_ctx=c_ctx, ada_b=ada_b, norm1_g=norm1_g, norm2_g=norm2_g, q_norm=attn_q_norm[0], k_norm=attn_k_norm[0],
              sink=attn_sink[0], log_g=jax.nn.log_sigmoid(decay_logit), gn_g=gn_full)
    loss_part, dz, big, small = _local_step(x, c, ctx, loss_target, sp, wts)

    loss = lax.psum(loss_part[0, 0], ("x", "y", "c"))
    grad_x = dz.reshape(nb, -1, d)[:, :s]

    dlogit = small["log_g"] * jax.nn.sigmoid(-decay_logit)
    sg = _unpack_small(_all_reduce_small(_pack_small(small, dlogit), "reduce_small_grads"))
    reduced = dict(zip(names, _reduce_scatter([big[k] for k in names], pos)))

    grads = dict(
        c_ctx=sg["c_ctx"], ada_w=jnp.stack([reduced["ada0"], reduced["ada1"]]), ada_b=sg["ada_b"], norm1_g=sg["norm1_g"],
        norm2_g=sg["norm2_g"], ffn_w_in=jnp.stack([reduced["ffn_in0"], reduced["ffn_in1"]]),
        ffn_w_out=jnp.stack([reduced["ffn_out0"], reduced["ffn_out1"]]), attn_w_qkv=reduced["attn_qkv"][None],
        attn_q_norm=sg["q_norm"][None], attn_k_norm=sg["k_norm"][None], attn_sink=sg["sink"][None],
        attn_w_o=reduced["attn_o"][None], ret_w_qkvg=reduced["ret_qkvg"][None], ret_decay_logit=sg["decay"].reshape(1, 2, RET_HEADS),
        ret_gn_g=lax.dynamic_slice(sg["gn_g"], (chip * gn_shard,), (gn_shard,))[None], ret_w_o=reduced["ret_o"][None])
    params = dict(c_ctx=(c_ctx, m_c_ctx, v_c_ctx), ada_w=(ada_w, m_ada_w, v_ada_w), ada_b=(ada_b, m_ada_b, v_ada_b),
                  norm1_g=(norm1_g, m_norm1_g, v_norm1_g), norm2_g=(norm2_g, m_norm2_g, v_norm2_g),
                  ffn_w_in=(ffn_w_in, m_ffn_w_in, v_ffn_w_in), ffn_w_out=(ffn_w_out, m_ffn_w_out, v_ffn_w_out),
                  attn_w_qkv=(attn_w_qkv, m_attn_w_qkv, v_attn_w_qkv), attn_q_norm=(attn_q_norm, m_attn_q_norm, v_attn_q_norm),
                  attn_k_norm=(attn_k_norm, m_attn_k_norm, v_attn_k_norm), attn_sink=(attn_sink, m_attn_sink, v_attn_sink),
                  attn_w_o=(attn_w_o, m_attn_w_o, v_attn_w_o), ret_w_qkvg=(ret_w_qkvg, m_ret_w_qkvg, v_ret_w_qkvg),
                  ret_decay_logit=(ret_decay_logit, m_ret_decay_logit, v_ret_decay_logit),
                  ret_gn_g=(ret_gn_g, m_ret_gn_g, v_ret_gn_g), ret_w_o=(ret_w_o, m_ret_w_o, v_ret_w_o))
    order = list(params)
    deltas, new_m, new_v = [], [], []
    for k in order:
        w, m, v = params[k]
        g = grads[k].reshape(w.shape)
        grads[k] = g
        flat = (-1, w.shape[-1]) if w.ndim > 1 else (1, -1)
        if k == "ret_decay_logit":
            flat = (1, -1)
        dw, nm, nv = _adamw(w.reshape(flat), g.reshape(flat), m.reshape(flat), v.reshape(flat), f"adamw_{k}")
        deltas.append(dw.reshape(w.shape))
        new_m.append(nm.reshape(w.shape))
        new_v.append(nv.reshape(w.shape))
    return (loss, grad_x, *[grads[k] for k in order], *deltas, *new_m, *new_v)
```

```python
import functools

import jax
import jax.numpy as jnp
from jax import lax
from jax.experimental import pallas as pl
from jax.experimental.pallas import tpu as pltpu

F32 = jnp.float32
BF16 = jnp.bfloat16

D_MODEL = 1024
N_HEADS = 16
N_KV_HEADS = 4
HEAD_DIM = 64
WINDOW = 128
ATTN_BLOCK = 128
BAND = ATTN_BLOCK + 2 * WINDOW
RET_HEADS = 4
RET_QK_DIM = 256
RET_V_DIM = 512
RET_VWIDTH = 2048
RET_CHUNK = 128
D_FF = 2816
GRID_W = 64
ROPE_BASE = 10000.0
EPS = 1e-6
NEG_INF = -1e30
LANES = 128

ADAM_LR = 0.001
ADAM_B1 = 0.9
ADAM_B2 = 0.999
ADAM_EPS = 1e-08
ADAM_WD = 0.01
ADAM_STEP = 10

VMEM_LIMIT_BYTES = 56 * 1024 * 1024
MESH = pl.DeviceIdType.MESH
N_CHIPS = 4


def _cparams(*sem):
    return pltpu.CompilerParams(dimension_semantics=sem, vmem_limit_bytes=VMEM_LIMIT_BYTES)


_DIMS = {"nn": ((1,), (0,)), "nt": ((1,), (1,)), "tn": ((0,), (0,))}


def _dot(a, b, form):
    return lax.dot_general(a.astype(BF16), b.astype(BF16), (_DIMS[form], ((), ())), preferred_element_type=F32)


@functools.partial(jax.custom_vjp, nondiff_argnums=(2,))
def _mm(a, b, form):
    return _dot(a, b, form)


def _mm_fwd(a, b, form):
    return _dot(a, b, form), (a, b)


def _mm_bwd(form, res, ct):
    a, b = res
    if form == "nn":
        da, db = _dot(ct, b, "nt"), _dot(a, ct, "tn")
    elif form == "nt":
        da, db = _dot(ct, b, "nn"), _dot(ct, a, "tn")
    else:
        da, db = _dot(b, ct, "nt"), _dot(a, ct, "nn")
    return da.astype(a.dtype), db.astype(b.dtype)


_mm.defvjp(_mm_fwd, _mm_bwd)


def _swap_halves(x, half):
    w = x.shape[-1]
    lane = lax.broadcasted_iota(jnp.int32, x.shape, x.ndim - 1)
    return jnp.where(lane % (2 * half) < half, pltpu.roll(x, w - half, x.ndim - 1), pltpu.roll(x, half, x.ndim - 1))


@functools.partial(jax.custom_vjp, nondiff_argnums=(1,))
def _rot(x, half):
    return _swap_halves(x, half)


def _rot_fwd(x, half):
    return _swap_halves(x, half), None


def _rot_bwd(half, _, ct):
    return (_swap_halves(ct, half),)


_rot.defvjp(_rot_fwd, _rot_bwd)


def _rope(x, cos, sin_signed, half):
    return x * cos + _rot(x, half) * sin_signed


def _head_mean_square(x):
    r = lax.broadcasted_iota(jnp.int32, (LANES, LANES), 0) // HEAD_DIM
    c = lax.broadcasted_iota(jnp.int32, (LANES, LANES), 1) // HEAD_DIM
    g = jnp.where(r == c, 1.0 / HEAD_DIM, 0.0).astype(F32)
    return jnp.dot(x * x, g, precision=lax.Precision.HIGHEST, preferred_element_type=F32)


def _qk_chunk(x, gain, cos, sin_signed, scale):
    y = x * lax.rsqrt(_head_mean_square(x) + EPS) * gain
    return _rope(y, cos, sin_signed, HEAD_DIM // 4) * scale


def _sigmoid(x):
    return 1.0 / (1.0 + jnp.exp(-x))


def _silu(x):
    return x * _sigmoid(x)


def _mm_nn(a, w, out_dtype, name, tm, tn, tk, bias=None):
    m, k_dim = a.shape
    if w.ndim == 3:
        n = w.shape[0] * w.shape[2]
        per = w.shape[2] // tn
        assert w.shape[2] % tn == 0
        w_spec = pl.BlockSpec((None, tk, tn), lambda i, j, k: (j // per, k, j % per))
    else:
        n = w.shape[1]
        w_spec = pl.BlockSpec((tk, tn), lambda i, j, k: (k, j))
    assert m % tm == 0 and n % tn == 0 and k_dim % tk == 0, (name, a.shape, w.shape, tm, tn, tk)
    nk = k_dim // tk
    has_bias = bias is not None

    def body(*refs):
        a_ref, w_ref = refs[0], refs[1]
        b_ref = refs[2] if has_bias else None
        o_ref, acc_ref = refs[-2], refs[-1]
        if nk == 1:
            part = jnp.dot(a_ref[...].astype(BF16), w_ref[...], preferred_element_type=F32)
            o_ref[...] = (part + b_ref[...] if has_bias else part).astype(out_dtype)
            return
        k = pl.program_id(2)

        @pl.when(k == 0)
        def _():
            acc_ref[...] = jnp.zeros_like(acc_ref)

        acc_ref[...] += jnp.dot(a_ref[...].astype(BF16), w_ref[...], preferred_element_type=F32)

        @pl.when(k == nk - 1)
        def _():
            r = acc_ref[...]
            if has_bias:
                r = r + b_ref[...]
            o_ref[...] = r.astype(out_dtype)

    in_specs = [pl.BlockSpec((tm, tk), lambda i, j, k: (i, k)), w_spec]
    args = [a, w]
    if has_bias:
        in_specs.append(pl.BlockSpec((1, tn), lambda i, j, k: (0, j)))
        args.append(bias)
    return pl.pallas_call(
        body, name=name, grid=(m // tm, n // tn, nk), in_specs=in_specs,
        out_specs=pl.BlockSpec((tm, tn), lambda i, j, k: (i, j)),
        out_shape=jax.ShapeDtypeStruct((m, n), out_dtype),
        scratch_shapes=[pltpu.VMEM((tm, tn), F32)],
        compiler_params=_cparams("parallel", "parallel", "arbitrary"),
    )(*args)


def _mm_nt(a, w, out_dtype, name, tm, tn, tk):
    if a.ndim == 3:
        planes, m, plane_w = a.shape
        c_dim = planes * plane_w
        a_per = plane_w // tk
        assert plane_w % tk == 0
        a_spec = pl.BlockSpec((None, tm, tk), lambda i, j, k: (k // a_per, i, k % a_per))
    else:
        m, c_dim = a.shape
        a_spec = pl.BlockSpec((tm, tk), lambda i, j, k: (i, k))
    if w.ndim == 3:
        k_out = w.shape[1]
        per = w.shape[2] // tk
        assert w.shape[2] % tk == 0 and w.shape[0] * w.shape[2] == c_dim
        w_spec = pl.BlockSpec((None, tn, tk), lambda i, j, k: (k // per, j, k % per))
    else:
        k_out = w.shape[0]
        assert w.shape[1] == c_dim
        w_spec = pl.BlockSpec((tn, tk), lambda i, j, k: (j, k))
    assert m % tm == 0 and k_out % tn == 0 and c_dim % tk == 0, (name, a.shape, w.shape, tm, tn, tk)
    nk = c_dim // tk

    def body(a_ref, w_ref, o_ref, acc_ref):
        if nk == 1:
            o_ref[...] = _dot(a_ref[...], w_ref[...], "nt").astype(out_dtype)
            return
        k = pl.program_id(2)

        @pl.when(k == 0)
        def _():
            acc_ref[...] = jnp.zeros_like(acc_ref)

        acc_ref[...] += _dot(a_ref[...], w_ref[...], "nt")

        @pl.when(k == nk - 1)
        def _():
            o_ref[...] = acc_ref[...].astype(out_dtype)

    return pl.pallas_call(
        body, name=name, grid=(m // tm, k_out // tn, nk),
        in_specs=[a_spec, w_spec],
        out_specs=pl.BlockSpec((tm, tn), lambda i, j, k: (i, j)),
        out_shape=jax.ShapeDtypeStruct((m, k_out), out_dtype),
        scratch_shapes=[pltpu.VMEM((tm, tn), F32)],
        compiler_params=_cparams("parallel", "parallel", "arbitrary"),
    )(a, w)


def _mm_tn(a, b, name, tm, tn, tk, shards=None):
    r, k_dim = a.shape
    if b.ndim == 3:
        n = b.shape[0] * b.shape[2]
        b_per = b.shape[2] // tn
        assert b.shape[2] % tn == 0
        b_spec = pl.BlockSpec((None, tk, tn), lambda i, j, k: (j // b_per, k, j % b_per))
    else:
        n = b.shape[1]
        b_spec = pl.BlockSpec((tk, tn), lambda i, j, k: (k, j))
    assert r % tk == 0 and k_dim % tm == 0 and n % tn == 0, (name, a.shape, b.shape, tm, tn, tk)
    nk = r // tk
    if shards:
        per = n // shards // tn
        assert n % (shards * tn) == 0
        out_shape = jax.ShapeDtypeStruct((shards, k_dim, n // shards), F32)
        out_spec = pl.BlockSpec((None, tm, tn), lambda i, j, k: (j // per, i, j % per))
    else:
        out_shape = jax.ShapeDtypeStruct((k_dim, n), F32)
        out_spec = pl.BlockSpec((tm, tn), lambda i, j, k: (i, j))

    def body(a_ref, b_ref, o_ref):
        k = pl.program_id(2)

        @pl.when(k == 0)
        def _():
            o_ref[...] = jnp.zeros_like(o_ref)

        o_ref[...] += _dot(a_ref[...], b_ref[...], "tn")

    return pl.pallas_call(
        body, name=name, grid=(k_dim // tm, n // tn, nk),
        in_specs=[pl.BlockSpec((tk, tm), lambda i, j, k: (k, i)), b_spec],
        out_specs=out_spec, out_shape=out_shape,
        compiler_params=_cparams("parallel", "parallel", "arbitrary"),
    )(a, b)


FFN_ROW_TILE = 768


def _ffn_tile(r):
    return FFN_ROW_TILE if r % FFN_ROW_TILE == 0 else _row_tile(r)


def _ffn_in_swiglu(h, w, name):
    r, k_dim = h.shape
    n4 = w.shape[2]
    tm = _ffn_tile(r)

    def body(h_ref, wg_ref, wu_ref, u_ref, a_ref):
        hv = h_ref[...]
        g = jnp.dot(hv, wg_ref[...], preferred_element_type=F32)
        up = jnp.dot(hv, wu_ref[...], preferred_element_type=F32)
        u_ref[0] = g.astype(BF16)
        u_ref[1] = up.astype(BF16)
        a_ref[...] = (_silu(g) * up).astype(BF16)

    return pl.pallas_call(
        body, name=name, grid=(r // tm, 2),
        in_specs=[pl.BlockSpec((tm, k_dim), lambda i, j: (i, 0)),
                  pl.BlockSpec((None, k_dim, n4), lambda i, j: (j, 0, 0)),
                  pl.BlockSpec((None, k_dim, n4), lambda i, j: (j + 2, 0, 0))],
        out_specs=[pl.BlockSpec((2, tm, n4), lambda i, j: (0, i, j)), pl.BlockSpec((tm, n4), lambda i, j: (i, j))],
        out_shape=[jax.ShapeDtypeStruct((2, r, 2 * n4), BF16), jax.ShapeDtypeStruct((r, 2 * n4), BF16)],
        compiler_params=_cparams("parallel", "parallel"),
    )(h, w, w)


def _mm_nn_gate_residual(geo, a, w, z, mod, off, name):
    r, k_dim = a.shape
    n = w.shape[1]
    tm = FFN_ROW_TILE if geo.seg % FFN_ROW_TILE == 0 else 256
    tiles = geo.seg // tm
    assert geo.seg % tm == 0 and r == geo.r

    def body(a_ref, w_ref, z_ref, mx_ref, mc_ref, zo_ref, raw_ref):
        out = jnp.dot(a_ref[...], w_ref[...], preferred_element_type=F32)
        row = (pl.program_id(0) % tiles) * tm + lax.broadcasted_iota(jnp.int32, (tm, 1), 0)
        gate = jnp.where(row < geo.s, mx_ref[:, off:off + n], mc_ref[:, off:off + n])
        zo_ref[...] = z_ref[...] + gate * out
        raw_ref[...] = out.astype(BF16)

    mod_w = mod.shape[2]
    return pl.pallas_call(
        body, name=name, grid=(r // tm,),
        in_specs=[pl.BlockSpec((tm, k_dim), lambda i: (i, 0)), pl.BlockSpec((k_dim, n), lambda i: (0, 0)),
                  pl.BlockSpec((tm, n), lambda i: (i, 0)),
                  pl.BlockSpec((None, 1, mod_w), lambda i: (i // tiles, 0, 0)),
                  pl.BlockSpec((None, 1, mod_w), lambda i: (geo.b, 0, 0))],
        out_specs=[pl.BlockSpec((tm, n), lambda i: (i, 0)), pl.BlockSpec((tm, n), lambda i: (i, 0))],
        out_shape=[jax.ShapeDtypeStruct((r, n), F32), jax.ShapeDtypeStruct((r, n), BF16)],
        compiler_params=_cparams("parallel"),
    )(a, w, z, mod, mod)


def _ffn_out_dx_swiglu_bwd(df, w_out, u, name):
    r, d = df.shape
    n4 = u.shape[2] // 2
    tm = _ffn_tile(r)

    def body(df_ref, w_ref, u_ref, du_ref):
        da = _dot(df_ref[...], w_ref[...], "nt")
        g, up = u_ref[0].astype(F32), u_ref[1].astype(F32)
        s = _sigmoid(g)
        du_ref[0] = (da * up * (s * (1.0 + g * (1.0 - s)))).astype(BF16)
        du_ref[1] = (da * (g * s)).astype(BF16)

    return pl.pallas_call(
        body, name=name, grid=(r // tm, 2),
        in_specs=[pl.BlockSpec((tm, d), lambda i, j: (i, 0)), pl.BlockSpec((n4, d), lambda i, j: (j, 0)),
                  pl.BlockSpec((2, tm, n4), lambda i, j: (0, i, j))],
        out_specs=pl.BlockSpec((2, tm, n4), lambda i, j: (0, i, j)),
        out_shape=jax.ShapeDtypeStruct(u.shape, BF16),
        compiler_params=_cparams("parallel", "parallel"),
    )(df, w_out, u)


class _Rows:
    def __init__(self, b, s, l):
        self.b, self.s, self.l = b, s, l
        self.seg = s + l
        self.r = b * self.seg


def _rowwise(name, body, geo, tm, ins, outs):
    seg_blocks, x_blocks = geo.seg // tm, geo.s // tm
    assert geo.seg % tm == 0 and geo.s % tm == 0
    nb = geo.b

    def is_ctx(i):
        return i % seg_blocks >= x_blocks

    in_specs, args = [], []
    for arr, kind in ins:
        args.append(arr)
        if kind == "row":
            in_specs.append(pl.BlockSpec((tm, arr.shape[1]), lambda i: (i, 0)))
        elif kind == "ex":
            in_specs.append(pl.BlockSpec((None, 1, arr.shape[2]), lambda i: (jnp.where(is_ctx(i), nb, i // seg_blocks), 0, 0)))
        elif kind == "full":
            in_specs.append(pl.BlockSpec(arr.shape, lambda i, nd=arr.ndim: (0,) * nd))
        elif kind == "tab":
            in_specs.append(pl.BlockSpec((tm, arr.shape[1]), lambda i: (i % seg_blocks, 0)))
        elif kind == "xrow":
            in_specs.append(pl.BlockSpec(
                (tm, arr.shape[1]), lambda i: ((i // seg_blocks) * x_blocks + jnp.minimum(i % seg_blocks, x_blocks - 1), 0)))
        else:
            _, width, cb = kind
            in_specs.append(pl.BlockSpec((tm, width), lambda i, cb=cb: (i, cb)))
    out_specs, out_shapes = [], []
    for o in outs:
        if o[0] == "row":
            out_specs.append(pl.BlockSpec((tm, o[1]), lambda i: (i, 0)))
            out_shapes.append(jax.ShapeDtypeStruct((geo.r, o[1]), o[2]))
        elif o[0] == "exacc":
            out_specs.append(pl.BlockSpec((None, 1, o[1]), lambda i: (jnp.where(is_ctx(i), nb, 0) + i // seg_blocks, 0, 0)))
            out_shapes.append(jax.ShapeDtypeStruct((2 * nb, 1, o[1]), F32))
        else:
            out_specs.append(pl.BlockSpec((o[1], o[2]), lambda i: (0, 0)))
            out_shapes.append(jax.ShapeDtypeStruct((o[1], o[2]), F32))
    n_in = len(ins)

    def kern(*refs):
        i = pl.program_id(0)
        res = body(i, *[r[...].astype(F32) for r in refs[:n_in]])
        if not isinstance(res, (tuple, list)):
            res = (res,)
        jj = i % seg_blocks
        first_of_part = (jj == 0) | (jj == x_blocks)
        for o, ref, val in zip(outs, refs[n_in:], res):
            if o[0] == "row":
                ref[...] = val.astype(ref.dtype)
            else:
                first = first_of_part if o[0] == "exacc" else i == 0

                @pl.when(first)
                def _(ref=ref, val=val):
                    ref[...] = val

                @pl.when(jnp.logical_not(first))
                def _(ref=ref, val=val):
                    ref[...] += val

    res = pl.pallas_call(
        kern, name=name, grid=(geo.r // tm,), in_specs=in_specs, out_specs=out_specs, out_shape=out_shapes,
        compiler_params=_cparams("arbitrary"),
    )(*args)
    return res[0] if len(res) == 1 else res


def _colsum(v):
    return jnp.sum(v, axis=0, keepdims=True)


def _norm_mod(geo, z, gain, mod, off, name):
    d = D_MODEL

    def body(i, zv, g, m):
        r = lax.rsqrt(jnp.mean(zv * zv, axis=-1, keepdims=True) + EPS)
        return (zv * r) * g * (1.0 + m[:, off + d:off + 2 * d]) + m[:, off:off + d]

    return _rowwise(name, body, geo, 256, [(z, "row"), (gain, "full"), (mod, "ex")], [("row", d, BF16)])


def _norm_mod_bwd(geo, z, gain, mod, off, dh, dz_skip, name):
    d = D_MODEL

    def body(i, zv, g, m, dhv, skip):
        r = lax.rsqrt(jnp.mean(zv * zv, axis=-1, keepdims=True) + EPS)
        n = zv * r
        dng = dhv * (1.0 + m[:, off + d:off + 2 * d])
        dn = dng * g
        dz = r * (dn - n * jnp.mean(dn * n, axis=-1, keepdims=True)) + skip
        return dz, _colsum(dhv), _colsum(dhv * (n * g)), _colsum(dng * n)

    return _rowwise(name, body, geo, 256, [(z, "row"), (gain, "full"), (mod, "ex"), (dh, "row"), (dz_skip, "row")],
                    [("row", d, F32), ("exacc", d), ("exacc", d), ("gacc", 1, d)])


def _gate_residual_bwd(geo, dz, out, mod, off, name):
    d = D_MODEL

    def body(i, dzv, ov, m):
        return dzv * m[:, off:off + d], _colsum(dzv * ov)

    return _rowwise(name, body, geo, 256, [(dz, "row"), (out, "row"), (mod, "ex")], [("row", d, BF16), ("exacc", d)])


def _loss_head(geo, z, target, name):
    seg_blocks, x_blocks = geo.seg // 256, geo.s // 256

    def body(i, zv, tv):
        keep = jnp.where(i % seg_blocks >= x_blocks, 0.0, 1.0)
        err = (zv - tv) * keep
        part = 0.5 * jnp.sum(jnp.mean(err * err, axis=-1, keepdims=True), axis=0, keepdims=True)
        return err * (1.0 / D_MODEL), jnp.broadcast_to(part, (1, LANES))

    return _rowwise(name, body, geo, 256, [(z, "row"), (target, "xrow")], [("row", D_MODEL, F32), ("gacc", 1, LANES)])


Q_SCALE = HEAD_DIM ** -0.5
N_QK_CHUNKS = (N_HEADS + N_KV_HEADS) * HEAD_DIM // LANES
N_Q_CHUNKS = N_HEADS * HEAD_DIM // LANES


def _attn_prep(geo, proj, cos, sin_signed, q_gain, k_gain, name):
    def body(i, p, cs, sn, qg, kg):
        outs = []
        for ch in range(N_QK_CHUNKS):
            is_q = ch < N_Q_CHUNKS
            outs.append(_qk_chunk(p[:, ch * LANES:(ch + 1) * LANES], qg if is_q else kg, cs, sn, Q_SCALE if is_q else 1.0))
        outs.append(p[:, N_QK_CHUNKS * LANES:])
        return jnp.concatenate(outs, axis=1)

    return _rowwise(name, body, geo, 256, [(proj, "row"), (cos, "tab"), (sin_signed, "tab"), (q_gain, "full"), (k_gain, "full")],
                    [("row", proj.shape[1], BF16)])


def _attn_prep_bwd(geo, proj, cos, sin_signed, q_gain, k_gain, dq, dkv, name):
    kw = N_KV_HEADS * HEAD_DIM

    def body(i, p, cs, sn, qg, kg, dqv, dkvv):
        outs = []
        dgains = [jnp.zeros((1, LANES), F32), jnp.zeros((1, LANES), F32)]
        for ch in range(N_QK_CHUNKS):
            is_q = ch < N_Q_CHUNKS
            scale = Q_SCALE if is_q else 1.0
            ct = dqv[:, ch * LANES:(ch + 1) * LANES] if is_q else dkvv[:, (ch - N_Q_CHUNKS) * LANES:(ch - N_Q_CHUNKS + 1) * LANES]
            _, vjp = jax.vjp(lambda xx, gg, scale=scale: _qk_chunk(xx, gg, cs, sn, scale),
                             p[:, ch * LANES:(ch + 1) * LANES], qg if is_q else kg)
            dx, dg = vjp(ct)
            outs.append(dx)
            dgains[0 if is_q else 1] = dgains[0 if is_q else 1] + dg
        outs.append(dkvv[:, kw:])
        return jnp.concatenate(outs, axis=1), dgains[0], dgains[1]

    return _rowwise(name, body, geo, 256,
                    [(proj, "row"), (cos, "tab"), (sin_signed, "tab"), (q_gain, "full"), (k_gain, "full"), (dq, "row"), (dkv, "row")],
                    [("row", proj.shape[1], BF16), ("gacc", 1, LANES), ("gacc", 1, LANES)])


def _attn_geometry(geo):
    assert geo.s % ATTN_BLOCK == 0 and geo.l % ATTN_BLOCK == 0 and geo.seg >= BAND
    return geo.seg // ATTN_BLOCK, geo.s // ATTN_BLOCK


def _attn_mask(j, s0, geo):
    r = lax.broadcasted_iota(jnp.int32, (ATTN_BLOCK, BAND), 0)
    n = lax.broadcasted_iota(jnp.int32, (ATTN_BLOCK, BAND), 1)
    dist = (s0 - j * ATTN_BLOCK) + n - r
    return (jnp.abs(dist) <= WINDOW) & (s0 + n < geo.s)


def _attn_probs(q, keys, valid, n_ctx, sink):
    s = _dot(q, keys, "nt")
    if valid is not None:
        s = jnp.concatenate([s[:, :n_ctx], jnp.where(valid, s[:, n_ctx:], NEG_INF)], axis=1)
    m = jnp.maximum(jnp.max(s, axis=-1, keepdims=True), sink)
    e, e_sink = jnp.exp(s - m), jnp.exp(sink - m)
    inv = 1.0 / (jnp.sum(e, axis=-1, keepdims=True) + e_sink)
    return e * inv, e_sink * inv


def _attn_keys(ref, s0, geo, with_band):
    ctx = ref[geo.s:geo.seg, :]
    return jnp.concatenate([ctx, ref[pl.ds(s0, BAND), :]], axis=0) if with_band else ctx


def _attention(geo, qkv, sink, name):
    n_blocks, n_x_blocks = _attn_geometry(geo)
    qw, kw = N_HEADS * HEAD_DIM, N_KV_HEADS * HEAD_DIM
    group = N_HEADS // N_KV_HEADS

    def kern(sink_ref, q_ref, k_ref, v_ref, o_ref):
        j = pl.program_id(1)
        s0 = pl.multiple_of(jnp.clip((j - 1) * ATTN_BLOCK, 0, geo.seg - BAND), ATTN_BLOCK)

        def heads(with_band):
            valid = _attn_mask(j, s0, geo) if with_band else None
            k_all, v_all = _attn_keys(k_ref, s0, geo, with_band), _attn_keys(v_ref, s0, geo, with_band)
            for h in range(N_HEADS):
                kv = slice((h // group) * HEAD_DIM, (h // group + 1) * HEAD_DIM)
                p, _ = _attn_probs(q_ref[:, h * HEAD_DIM:(h + 1) * HEAD_DIM], k_all[:, kv], valid, geo.l, sink_ref[h])
                o_ref[:, h * HEAD_DIM:(h + 1) * HEAD_DIM] = _dot(p, v_all[:, kv], "nn").astype(BF16)

        pl.when(j < n_x_blocks)(lambda: heads(True))
        pl.when(j >= n_x_blocks)(lambda: heads(False))

    return pl.pallas_call(
        kern, name=name, grid=(geo.b, n_blocks),
        in_specs=[pl.BlockSpec(memory_space=pltpu.SMEM),
                  pl.BlockSpec((ATTN_BLOCK, qw), lambda b, j: (b * n_blocks + j, 0)),
                  pl.BlockSpec((geo.seg, kw), lambda b, j: (b, qw // kw)),
                  pl.BlockSpec((geo.seg, kw), lambda b, j: (b, qw // kw + 1))],
        out_specs=pl.BlockSpec((ATTN_BLOCK, qw), lambda b, j: (b * n_blocks + j, 0)),
        out_shape=jax.ShapeDtypeStruct((geo.r, qw), BF16),
        compiler_params=_cparams("parallel", "arbitrary"),
    )(sink, qkv, qkv, qkv)


def _attention_bwd(geo, qkv, sink, do, name):
    n_blocks, n_x_blocks = _attn_geometry(geo)
    qw, kw = N_HEADS * HEAD_DIM, N_KV_HEADS * HEAD_DIM
    group = N_HEADS // N_KV_HEADS

    def kern(sink_ref, q_ref, k_ref, v_ref, do_ref, dq_ref, dkv_ref, dsink_ref):
        b, j = pl.program_id(0), pl.program_id(1)
        s0 = pl.multiple_of(jnp.clip((j - 1) * ATTN_BLOCK, 0, geo.seg - BAND), ATTN_BLOCK)

        @pl.when(j == 0)
        def _():
            dkv_ref[...] = jnp.zeros_like(dkv_ref)

        @pl.when((j == 0) & (b == 0))
        def _():
            dsink_ref[...] = jnp.zeros_like(dsink_ref)

        def heads(with_band):
            valid = _attn_mask(j, s0, geo) if with_band else None
            k_all, v_all = _attn_keys(k_ref, s0, geo, with_band), _attn_keys(v_ref, s0, geo, with_band)
            for g in range(N_KV_HEADS):
                kv = slice(g * HEAD_DIM, (g + 1) * HEAD_DIM)
                keys, vals = k_all[:, kv], v_all[:, kv]
                group_heads = [slice(h * HEAD_DIM, (h + 1) * HEAD_DIM) for h in range(g * group, (g + 1) * group)]
                ds_rows, p_rows = [], []
                for h, hs in zip(range(g * group, (g + 1) * group), group_heads):
                    dout = do_ref[:, hs]
                    p, p_sink = _attn_probs(q_ref[:, hs], keys, valid, geo.l, sink_ref[h])
                    dp = _dot(dout, vals, "nt")
                    dsum = jnp.sum(p * dp, axis=-1, keepdims=True)
                    ds = (p * (dp - dsum)).astype(BF16)
                    dq_ref[:, hs] = _dot(ds, keys, "nn")
                    ds_rows.append(ds)
                    p_rows.append(p.astype(BF16))
                    dsink_ref[h:h + 1, :] += jnp.broadcast_to(-jnp.sum(p_sink * dsum, axis=0, keepdims=True), (1, LANES))
                q_rows = jnp.concatenate([q_ref[:, hs] for hs in group_heads], axis=0)
                do_rows = jnp.concatenate([do_ref[:, hs] for hs in group_heads], axis=0)
                dk = _dot(jnp.concatenate(ds_rows, axis=0), q_rows, "tn")
                dv = _dot(jnp.concatenate(p_rows, axis=0), do_rows, "tn")
                vv = slice(kw + g * HEAD_DIM, kw + (g + 1) * HEAD_DIM)
                dkv_ref[geo.s:geo.seg, kv] += dk[:geo.l]
                dkv_ref[geo.s:geo.seg, vv] += dv[:geo.l]
                if with_band:
                    dkv_ref[pl.ds(s0, BAND), kv] += dk[geo.l:]
                    dkv_ref[pl.ds(s0, BAND), vv] += dv[geo.l:]

        pl.when(j < n_x_blocks)(lambda: heads(True))
        pl.when(j >= n_x_blocks)(lambda: heads(False))

    return pl.pallas_call(
        kern, name=name, grid=(geo.b, n_blocks),
        in_specs=[pl.BlockSpec(memory_space=pltpu.SMEM),
                  pl.BlockSpec((ATTN_BLOCK, qw), lambda b, j: (b * n_blocks + j, 0)),
                  pl.BlockSpec((geo.seg, kw), lambda b, j: (b, qw // kw)),
                  pl.BlockSpec((geo.seg, kw), lambda b, j: (b, qw // kw + 1)),
                  pl.BlockSpec((ATTN_BLOCK, qw), lambda b, j: (b * n_blocks + j, 0))],
        out_specs=[pl.BlockSpec((ATTN_BLOCK, qw), lambda b, j: (b * n_blocks + j, 0)),
                   pl.BlockSpec((geo.seg, 2 * kw), lambda b, j: (b, 0)),
                   pl.BlockSpec((N_HEADS, LANES), lambda b, j: (0, 0))],
        out_shape=[jax.ShapeDtypeStruct((geo.r, qw), F32), jax.ShapeDtypeStruct((geo.r, 2 * kw), F32),
                   jax.ShapeDtypeStruct((N_HEADS, LANES), F32)],
        compiler_params=_cparams("arbitrary", "arbitrary"),
    )(sink, qkv, qkv, qkv, do)


RET_QK_W = RET_HEADS * RET_QK_DIM
K_SCALE = RET_QK_DIM ** -0.5


def _ret_prep(geo, proj, cos, sin_signed, name):
    def body(i, p, cs, sn):
        cs2, sn2 = jnp.concatenate([cs] * RET_HEADS, axis=1), jnp.concatenate([sn] * RET_HEADS, axis=1)
        q = _rope(p[:, :RET_QK_W], cs2, sn2, RET_QK_DIM // 4)
        k = _rope(p[:, RET_QK_W:2 * RET_QK_W], cs2, sn2, RET_QK_DIM // 4) * K_SCALE
        return jnp.concatenate([q, k, p[:, 2 * RET_QK_W:]], axis=1)

    return _rowwise(name, body, geo, 128, [(proj, ("rowc", 2 * RET_QK_W + RET_VWIDTH, 0)), (cos, "tab"), (sin_signed, "tab")],
                    [("row", 2 * RET_QK_W + RET_VWIDTH, BF16)])


def _ret_prep_bwd(geo, dq, dk, dv, dgate, cos, sin_signed, name):
    def body(i, dqv, dkv, dvv, dg, cs, sn):
        cs2, sn2 = jnp.concatenate([cs] * RET_HEADS, axis=1), jnp.concatenate([sn] * RET_HEADS, axis=1)
        dkv = dkv * K_SCALE
        dqv = dqv * cs2 + _swap_halves(dqv * sn2, RET_QK_DIM // 4)
        dkv = dkv * cs2 + _swap_halves(dkv * sn2, RET_QK_DIM // 4)
        return jnp.concatenate([dqv, dkv, dvv, dg], axis=1)

    return _rowwise(name, body, geo, 128,
                    [(dq, "row"), (dk, "row"), (dv, "row"), (dgate, "row"), (cos, "tab"), (sin_signed, "tab")],
                    [("row", 2 * RET_QK_W + 2 * RET_VWIDTH, BF16)])


def _ret_step(state, q, k, v, lg, rev):
    c = RET_CHUNK
    ri = lax.broadcasted_iota(jnp.int32, (c, 1), 0).astype(F32)
    cj = lax.broadcasted_iota(jnp.int32, (1, c), 1).astype(F32)
    if rev:
        dist, q_decay, k_decay = cj - ri, jnp.exp(lg * (c - ri)), jnp.exp(lg * ri)
    else:
        dist, q_decay, k_decay = ri - cj, jnp.exp(lg * (ri + 1.0)), jnp.exp(lg * (c - 1.0 - ri))
    intra = jnp.where(dist >= 0, jnp.exp(lg * jnp.maximum(dist, 0.0)), 0.0)
    scores = _mm(q, k, "nt") * intra
    out = _mm(scores, v, "nn") + _mm(q, state, "nn") * q_decay
    new_state = state * jnp.exp(lg * c) + _mm(k * k_decay, v, "tn")
    return new_state, out


def _ret_state0(kc, vc, lg, rev):
    n = kc.shape[0]
    t = lax.broadcasted_iota(jnp.int32, (n, 1), 0).astype(F32)
    decay = jnp.exp(lg * t) if rev else jnp.exp(lg * (n - 1.0 - t))
    return _mm(kc * decay, vc, "tn")


def _ret_specs(geo):
    nq = RET_HEADS
    return [pl.BlockSpec((2 * RET_HEADS, LANES), lambda b, h: (0, 0)),
            pl.BlockSpec((geo.seg, RET_QK_DIM), lambda b, h: (b, h)),
            pl.BlockSpec((geo.seg, RET_QK_DIM), lambda b, h: (b, nq + h)),
            pl.BlockSpec((geo.seg, RET_V_DIM), lambda b, h: (b, nq + h))]


def _retention(geo, qkv, log_g, name):
    nc = geo.s // RET_CHUNK

    def kern(lg_ref, q_ref, k_ref, v_ref, o_ref, st_ref):
        h = pl.program_id(1)
        for d, rev in ((0, False), (1, True)):
            lg = lg_ref[pl.ds(d * RET_HEADS + h, 1), 0:1]
            st_ref[...] = _ret_state0(k_ref[geo.s:geo.seg, :].astype(F32), v_ref[geo.s:geo.seg, :].astype(F32), lg, rev)

            def chunk(ci, carry, d=d, rev=rev, lg=lg):
                r0 = pl.multiple_of((nc - 1 - ci if rev else ci) * RET_CHUNK, RET_CHUNK)
                rows = pl.ds(r0, RET_CHUNK)
                new_state, out = _ret_step(st_ref[...], q_ref[rows, :].astype(F32), k_ref[rows, :].astype(F32),
                                           v_ref[rows, :].astype(F32), lg, rev)
                st_ref[...] = new_state
                if d == 0:
                    o_ref[rows, :] = out
                else:
                    o_ref[rows, :] += out
                return carry

            lax.fori_loop(0, nc, chunk, 0)
        o_ref[geo.s:geo.seg, :] = jnp.zeros((geo.l, RET_V_DIM), F32)

    return pl.pallas_call(
        kern, name=name, grid=(geo.b, RET_HEADS), in_specs=_ret_specs(geo),
        out_specs=pl.BlockSpec((geo.seg, RET_V_DIM), lambda b, h: (b, h)),
        out_shape=jax.ShapeDtypeStruct((geo.r, RET_VWIDTH), F32),
        scratch_shapes=[pltpu.VMEM((RET_QK_DIM, RET_V_DIM), F32)],
        compiler_params=_cparams("parallel", "arbitrary"),
    )(log_g, qkv, qkv, qkv)


def _retention_bwd(geo, qkv, log_g, do, name):
    nc = geo.s // RET_CHUNK
    ctx = slice(geo.s, geo.seg)

    def kern(lg_ref, q_ref, k_ref, v_ref, do_ref, dq_ref, dk_ref, dv_ref, dlg_ref, states_ref, cur_ref, dst_ref):
        b, h = pl.program_id(0), pl.program_id(1)

        @pl.when((b == 0) & (h == 0))
        def _():
            dlg_ref[...] = jnp.zeros_like(dlg_ref)

        for d, rev in ((0, False), (1, True)):
            row = pl.ds(d * RET_HEADS + h, 1)
            lg = lg_ref[row, 0:1]
            kc, vc = k_ref[ctx, :].astype(F32), v_ref[ctx, :].astype(F32)
            cur_ref[...] = _ret_state0(kc, vc, lg, rev)

            def rows_of(ci, rev=rev):
                return pl.ds(pl.multiple_of((nc - 1 - ci if rev else ci) * RET_CHUNK, RET_CHUNK), RET_CHUNK)

            def load(rows):
                return q_ref[rows, :].astype(F32), k_ref[rows, :].astype(F32), v_ref[rows, :].astype(F32)

            def replay(ci, carry, rev=rev, lg=lg, rows_of=rows_of, load=load):
                states_ref[ci] = cur_ref[...]
                cur_ref[...] = _ret_step(cur_ref[...], *load(rows_of(ci)), lg, rev)[0]
                return carry

            lax.fori_loop(0, nc, replay, 0)
            dst_ref[...] = jnp.zeros_like(dst_ref)

            def back(t, dlg, d=d, rev=rev, lg=lg, rows_of=rows_of, load=load):
                ci = nc - 1 - t
                rows = rows_of(ci)
                _, vjp = jax.vjp(lambda st, q, k, v, g: _ret_step(st, q, k, v, g, rev), states_ref[ci], *load(rows), lg)
                dstate, dq, dk, dv, dg = vjp((dst_ref[...], do_ref[rows, :]))
                dst_ref[...] = dstate
                if d == 0:
                    dq_ref[rows, :], dk_ref[rows, :], dv_ref[rows, :] = dq, dk, dv
                else:
                    dq_ref[rows, :] += dq
                    dk_ref[rows, :] += dk
                    dv_ref[rows, :] += dv
                return dlg + dg

            dlg = lax.fori_loop(0, nc, back, jnp.zeros((1, 1), F32))
            _, vjp = jax.vjp(lambda kk, vv, g: _ret_state0(kk, vv, g, rev), kc, vc, lg)
            dkc, dvc, dg = vjp(dst_ref[...])
            if d == 0:
                dk_ref[ctx, :], dv_ref[ctx, :] = dkc, dvc
            else:
                dk_ref[ctx, :] += dkc
                dv_ref[ctx, :] += dvc
            dlg_ref[row, :] += jnp.broadcast_to(dlg + dg, (1, LANES))
        dq_ref[ctx, :] = jnp.zeros((geo.l, RET_QK_DIM), F32)

    nq = RET_HEADS
    return pl.pallas_call(
        kern, name=name, grid=(geo.b, RET_HEADS),
        in_specs=_ret_specs(geo) + [pl.BlockSpec((geo.seg, RET_V_DIM), lambda b, h: (b, h))],
        out_specs=[pl.BlockSpec((geo.seg, RET_QK_DIM), lambda b, h: (b, h)),
                   pl.BlockSpec((geo.seg, RET_QK_DIM), lambda b, h: (b, h)),
                   pl.BlockSpec((geo.seg, RET_V_DIM), lambda b, h: (b, h)),
                   pl.BlockSpec((2 * RET_HEADS, LANES), lambda b, h: (0, 0))],
        out_shape=[jax.ShapeDtypeStruct((geo.r, RET_QK_W), F32), jax.ShapeDtypeStruct((geo.r, RET_QK_W), F32),
                   jax.ShapeDtypeStruct((geo.r, RET_VWIDTH), F32), jax.ShapeDtypeStruct((2 * RET_HEADS, LANES), F32)],
        scratch_shapes=[pltpu.VMEM((nc, RET_QK_DIM, RET_V_DIM), F32), pltpu.VMEM((RET_QK_DIM, RET_V_DIM), F32),
                        pltpu.VMEM((RET_QK_DIM, RET_V_DIM), F32)],
        compiler_params=_cparams("arbitrary", "arbitrary"),
    )(log_g, qkv, qkv, qkv, do)


def _gated(o, g, gain):
    outs = []
    for h in range(RET_HEADS):
        cols = slice(h * RET_V_DIM, (h + 1) * RET_V_DIM)
        oh = o[:, cols]
        mu = jnp.mean(oh, axis=-1, keepdims=True)
        var = jnp.mean(jnp.square(oh - mu), axis=-1, keepdims=True)
        outs.append(_silu(g[:, cols]) * ((oh - mu) * lax.rsqrt(var + EPS) * gain[:, cols]))
    return jnp.concatenate(outs, axis=1)


def _ret_gated(geo, o, proj, gain, name):
    def body(i, ov, gv, gn):
        return _gated(ov, gv, gn)

    gate_block = (2 * RET_QK_W + RET_VWIDTH) // RET_VWIDTH
    return _rowwise(name, body, geo, 128, [(o, "row"), (proj, ("rowc", RET_VWIDTH, gate_block)), (gain, "full")],
                    [("row", RET_VWIDTH, BF16)])


def _ret_gated_bwd(geo, o, proj, gain, dout, name):
    def body(i, ov, gv, gn, dv):
        _, vjp = jax.vjp(_gated, ov, gv, gn)
        return vjp(dv)

    gate_block = (2 * RET_QK_W + RET_VWIDTH) // RET_VWIDTH
    return _rowwise(name, body, geo, 128,
                    [(o, "row"), (proj, ("rowc", RET_VWIDTH, gate_block)), (gain, "full"), (dout, "row")],
                    [("row", RET_VWIDTH, F32), ("row", RET_VWIDTH, F32), ("gacc", 1, RET_VWIDTH)])


def _whole(name, fn, out_shapes, *arrays):
    n = len(arrays)

    def kern(*refs):
        res = fn(*[r[...] for r in refs[:n]])
        for ref, val in zip(refs[n:], res):
            ref[...] = val.astype(ref.dtype)

    return pl.pallas_call(kern, name=name, out_shape=out_shapes)(*arrays)


def _rope_tables(geo, head_dim):
    rows = geo.s // GRID_W
    row = jnp.broadcast_to(jnp.arange(rows, dtype=jnp.int32)[:, None], (rows, GRID_W)).reshape(geo.s)
    col = jnp.broadcast_to(jnp.arange(GRID_W, dtype=jnp.int32)[None, :], (rows, GRID_W)).reshape(geo.s)
    axis_dim = head_dim // 2
    inv = ROPE_BASE ** (-jnp.arange(0, axis_dim, 2, dtype=F32) / axis_dim)
    ang_r = row.astype(F32)[:, None] * inv
    ang_c = col.astype(F32)[:, None] * inv
    cos = jnp.concatenate([jnp.cos(ang_r)] * 2 + [jnp.cos(ang_c)] * 2, axis=1)
    sin = jnp.concatenate([-jnp.sin(ang_r), jnp.sin(ang_r), -jnp.sin(ang_c), jnp.sin(ang_c)], axis=1)
    cos = jnp.concatenate([cos, jnp.ones((geo.l, head_dim), F32)], axis=0)
    sin = jnp.concatenate([sin, jnp.zeros((geo.l, head_dim), F32)], axis=0)
    reps = max(1, LANES // head_dim)
    return jnp.tile(cos, (1, reps)), jnp.tile(sin, (1, reps))


def _row_tile(r):
    return next(t for t in (1024, 512, 256, 128) if r % t == 0)


MOD_ROWS = 8


def _local_step(x, c, ctx, target, sp, wts):
    nb, s, d = x.shape
    geo = _Rows(nb, s, ctx.shape[1])
    assert nb + 1 <= MOD_ROWS and d == D_MODEL
    tm = _row_tile(geo.r)
    z = jnp.concatenate([x, ctx], axis=1).reshape(geo.r, d)
    cvec = jnp.concatenate([c, sp["c_ctx"][None, :], jnp.zeros((MOD_ROWS - nb - 1, d), F32)], axis=0)
    cact, = _whole("cond_silu", lambda v: (_silu(v),), [jax.ShapeDtypeStruct(cvec.shape, F32)], cvec)
    cos64, sin64 = _rope_tables(geo, HEAD_DIM)
    cos256, sin256 = _rope_tables(geo, RET_QK_DIM)
    q_gain = jnp.tile(sp["q_norm"].reshape(1, HEAD_DIM), (1, LANES // HEAD_DIM))
    k_gain = jnp.tile(sp["k_norm"].reshape(1, HEAD_DIM), (1, LANES // HEAD_DIM))
    sink = sp["sink"].reshape(N_HEADS)
    log_g = jnp.broadcast_to(sp["log_g"].reshape(2 * RET_HEADS, 1), (2 * RET_HEADS, LANES))
    gn_g = sp["gn_g"].reshape(1, RET_VWIDTH)

    saved = []
    for i in range(2):
        mod = _mm_nn(cact, wts["ada"][i], F32, f"mod{i}", MOD_ROWS, wts["ada"][i].shape[2], d, bias=sp["ada_b"][i][None, :])
        mod3 = mod[:nb + 1, None, :]
        n1, n2 = sp["norm1_g"][i][None, :], sp["norm2_g"][i][None, :]
        h1 = _norm_mod(geo, z, n1, mod3, 0, f"norm1_{i}")
        if i == 0:
            proj = _mm_nn(h1, wts["attn_qkv"], F32, "attn_qkv", tm, wts["attn_qkv"].shape[2], d)
            prep = _attn_prep(geo, proj, cos64, sin64, q_gain, k_gain, "attn_prep")
            o = _attention(geo, prep, sink, "attn")
            oraw = None
            w_o = wts["attn_o"]
        else:
            proj = _mm_nn(h1, wts["ret_qkvg"], BF16, "ret_qkvg", tm, 512, d)
            prep = _ret_prep(geo, proj, cos256, sin256, "ret_prep")
            oraw = _retention(geo, prep, log_g, "ret")
            o = _ret_gated(geo, oraw, proj, gn_g, "ret_gated")
            w_o = wts["ret_o"]
        zmid, mix = _mm_nn_gate_residual(geo, o, w_o, z, mod3, 2 * d, f"mix_out{i}")
        h2 = _norm_mod(geo, zmid, n2, mod3, 3 * d, f"norm2_{i}")
        u, a = _ffn_in_swiglu(h2, wts["ffn_in"][i], f"ffn_in{i}")
        zout, f = _mm_nn_gate_residual(geo, a, wts["ffn_out"][i], zmid, mod3, 5 * d, f"ffn_out{i}")
        saved.append(dict(z=z, mod3=mod3, n1=n1, n2=n2, h1=h1, proj=proj, prep=prep, o=o, oraw=oraw, mix=mix, zmid=zmid,
                          h2=h2, u=u, a=a, f=f))
        z = zout

    dz, loss = _loss_head(geo, z, target.reshape(nb * s, d), "loss")

    big, small = {}, {}
    dmods = [None, None]
    for i in (1, 0):
        sv = saved[i]
        mod3 = sv["mod3"]
        df, dg2 = _gate_residual_bwd(geo, dz, sv["f"], mod3, 5 * d, f"res_ffn_bwd{i}")
        du = _ffn_out_dx_swiglu_bwd(df, wts["ffn_out"][i], sv["u"], f"ffn_out_dx{i}")
        big[f"ffn_out{i}"] = _mm_tn(sv["a"], df, f"ffn_out_dw{i}", D_FF // 2, 1024, tm).reshape(N_CHIPS, D_FF // N_CHIPS, d)
        n4 = wts["ffn_in"][i].shape[2]
        dh2 = _mm_nt(du, wts["ffn_in"][i], BF16, f"ffn_in_dx{i}", tm, 1024, n4)
        big[f"ffn_in{i}"] = _mm_tn(sv["h2"], du, f"ffn_in_dw{i}", 1024, n4, tm, shards=N_CHIPS)
        dzmid, dsh2, dsc2, dn2 = _norm_mod_bwd(geo, sv["zmid"], sv["n2"], mod3, 3 * d, dh2, dz, f"norm2_bwd{i}")
        dmix, dg1 = _gate_residual_bwd(geo, dzmid, sv["mix"], mod3, 2 * d, f"res_mix_bwd{i}")
        if i == 0:
            do = _mm_nt(dmix, wts["attn_o"], BF16, "attn_out_dx", tm, 1024, 1024)
            big["attn_o"] = _mm_tn(sv["o"], dmix, "attn_out_dw", 1024, 1024, tm).reshape(N_CHIPS, 1024 // N_CHIPS, d)
            dq, dkv, dsink = _attention_bwd(geo, sv["prep"], sink, do, "attn_bwd")
            dproj, dqg, dkg = _attn_prep_bwd(geo, sv["proj"], cos64, sin64, q_gain, k_gain, dq, dkv, "attn_prep_bwd")
            small["q_norm"] = dqg[0, :HEAD_DIM] + dqg[0, HEAD_DIM:]
            small["k_norm"] = dkg[0, :HEAD_DIM] + dkg[0, HEAD_DIM:]
            small["sink"] = dsink[:, 0]
            wq = wts["attn_qkv"]
            dh1 = _mm_nt(dproj, wq, BF16, "attn_qkv_dx", tm, 1024, wq.shape[2])
            big["attn_qkv"] = _mm_tn(sv["h1"], dproj, "attn_qkv_dw", 1024, wq.shape[2], tm, shards=N_CHIPS)
        else:
            do = _mm_nt(dmix, wts["ret_o"], BF16, "ret_out_dx", tm, 1024, 1024)
            big["ret_o"] = _mm_tn(sv["o"], dmix, "ret_out_dw", 1024, 1024, tm).reshape(N_CHIPS, RET_VWIDTH // N_CHIPS, d)
            doraw, dgate, dgn = _ret_gated_bwd(geo, sv["oraw"], sv["proj"], gn_g, do, "ret_gated_bwd")
            small["gn_g"] = dgn[0]
            dq, dk, dv, dlg = _retention_bwd(geo, sv["prep"], log_g, doraw, "ret_bwd")
            small["log_g"] = dlg[:, 0].reshape(2, RET_HEADS)
            dproj = _ret_prep_bwd(geo, dq, dk, dv, dgate, cos256, sin256, "ret_prep_bwd")
            wq = wts["ret_qkvg"]
            dh1 = _mm_nt(dproj, wq, BF16, "ret_qkvg_dx", tm, 1024, 512)
            big["ret_qkvg"] = _mm_tn(sv["h1"], dproj, "ret_qkvg_dw", 1024, 512, tm, shards=N_CHIPS)
        dz, dsh1, dsc1, dn1 = _norm_mod_bwd(geo, sv["z"], sv["n1"], mod3, 0, dh1, dzmid, f"norm1_bwd{i}")
        small[f"norm1_g{i}"], small[f"norm2_g{i}"] = dn1[0], dn2[0]
        parts = [dsh1, dsc1, dg1, dsh2, dsc2, dg2]
        rows = jnp.concatenate([jnp.concatenate([p[:nb, 0, :] for p in parts], axis=1),
                                jnp.concatenate([jnp.sum(p[nb:, 0, :], axis=0, keepdims=True) for p in parts], axis=1),
                                jnp.zeros((MOD_ROWS - nb - 1, 6 * d), F32)], axis=0)
        dmods[i] = rows
        small[f"ada_b{i}"] = jnp.sum(rows, axis=0)
        big[f"ada{i}"] = _mm_tn(cact, rows, f"ada_dw{i}", 1024, wts["ada"][i].shape[2], MOD_ROWS, shards=N_CHIPS)

    dcact = [_mm_nt(dmods[i], wts["ada"][i], F32, f"ada_dx{i}", MOD_ROWS, 1024, wts["ada"][i].shape[2]) for i in range(2)]

    def silu_bwd(v, d0, d1):
        sg = _sigmoid(v)
        return ((d0 + d1) * (sg * (1.0 + v * (1.0 - sg))),)

    dcvec, = _whole("cond_silu_bwd", silu_bwd, [jax.ShapeDtypeStruct(cvec.shape, F32)], cvec, dcact[0], dcact[1])
    small["c_ctx"] = dcvec[nb]
    return loss, dz, big, small


def _adamw(w, g, m, v, name):
    rows, cols = w.shape
    tr = next((t for t in (256, 128, 64, 32, 16, 8) if rows % t == 0), rows)
    c1 = 1.0 - ADAM_B1 ** ADAM_STEP
    c2 = 1.0 - ADAM_B2 ** ADAM_STEP

    def kern(w_ref, g_ref, m_ref, v_ref, d_ref, nm_ref, nv_ref):
        gv = g_ref[...]
        nm = ADAM_B1 * m_ref[...] + (1.0 - ADAM_B1) * gv
        nv = ADAM_B2 * v_ref[...] + (1.0 - ADAM_B2) * jnp.square(gv)
        d_ref[...] = -ADAM_LR * ((nm / c1) / (jnp.sqrt(nv / c2) + ADAM_EPS) + ADAM_WD * w_ref[...])
        nm_ref[...] = nm
        nv_ref[...] = nv

    spec = pl.BlockSpec((tr, cols), lambda i: (i, 0))
    return pl.pallas_call(
        kern, name=name, grid=(rows // tr,), in_specs=[spec] * 4, out_specs=[spec] * 3,
        out_shape=[jax.ShapeDtypeStruct(w.shape, F32)] * 3, compiler_params=_cparams("parallel"),
    )(w, g, m, v)


N_DEVICES = 8


def _mesh_pos():
    return lax.axis_index("x"), lax.axis_index("y"), lax.axis_index("c")


def _other_chips(x, y):
    return [(1 - x, y), (x, 1 - y), (1 - x, 1 - y)]


def _hbm(n):
    return [pl.BlockSpec(memory_space=pl.ANY)] * n


def _remote(src, dst, send_sem, recv_sem, device):
    return pltpu.make_async_remote_copy(src_ref=src, dst_ref=dst, send_sem=send_sem, recv_sem=recv_sem,
                                        device_id=device, device_id_type=MESH)


def _scalar_spec(grid, in_specs, out_specs):
    return pltpu.PrefetchScalarGridSpec(num_scalar_prefetch=1, grid=grid, in_specs=in_specs, out_specs=out_specs)


def _place_shard(shard, pos, name):
    r, cols = shard.shape
    tr = _slab_tile(r)

    def kern(pos_ref, s_ref, o_ref):
        o_ref[...] = s_ref[...].astype(BF16)

    return pl.pallas_call(
        kern, name=name, out_shape=jax.ShapeDtypeStruct((N_CHIPS, r, cols), BF16),
        grid_spec=_scalar_spec((r // tr,), [pl.BlockSpec((tr, cols), lambda i, p: (i, 0))],
                               pl.BlockSpec((None, tr, cols), lambda i, p: (p[1], i, 0))),
        compiler_params=_cparams("parallel"),
    )(pos, shard)


def _gather_shards(placed):
    n = len(placed)

    def body(*refs):
        outs = refs[n:2 * n]
        send_sems, recv_sems, fwd_send, fwd_recv = refs[2 * n:]
        x, y, c = _mesh_pos()
        chip = 2 * x + y
        others = _other_chips(x, y)
        sibling = (x, y, 1 - c)

        def half(w, which):
            r2 = placed[w].shape[1] // 2
            return pl.ds(which * r2, r2)

        sends = []
        for w in range(n):
            for k, (px, py) in enumerate(others):
                mine = outs[w].at[chip, half(w, c)]
                cp = _remote(mine, mine, send_sems.at[w, k], recv_sems.at[w, k], (px, py, c))
                cp.start()
                sends.append(cp)
        for w in range(n):
            for k, (px, py) in enumerate(others):
                got = outs[w].at[2 * px + py, half(w, c)]
                _remote(got, got, send_sems.at[w, k], recv_sems.at[w, k], (px, py, c)).wait_recv()
                cp = _remote(got, got, fwd_send.at[w, k], fwd_recv.at[w, k], sibling)
                cp.start()
                sends.append(cp)
        for w in range(n):
            for k, (px, py) in enumerate(others):
                theirs = outs[w].at[2 * px + py, half(w, 1 - c)]
                _remote(theirs, theirs, fwd_send.at[w, k], fwd_recv.at[w, k], sibling).wait_recv()
        for cp in sends:
            cp.wait_send()

    return pl.pallas_call(
        body, name="gather_weights", in_specs=_hbm(n), out_specs=_hbm(n),
        out_shape=[jax.ShapeDtypeStruct(p.shape, p.dtype) for p in placed],
        input_output_aliases={w: w for w in range(n)},
        scratch_shapes=[pltpu.SemaphoreType.DMA((n, 3))] * 4,
    )(*placed)


def _pair_swap(grads):
    n = len(grads)

    def body(*refs):
        ins, land = refs[:n], refs[n:2 * n]
        send_sems, recv_sems = refs[2 * n:]
        x, y, c = _mesh_pos()
        copies = []
        for w in range(n):
            r2 = grads[w].shape[1] // 2
            rc = _remote(ins[w].at[:, pl.ds((1 - c) * r2, r2)], land[w], send_sems.at[w], recv_sems.at[w], (x, y, 1 - c))
            rc.start()
            copies.append(rc)
        for cp in copies:
            cp.wait()

    return pl.pallas_call(
        body, name="grads_pair_swap", in_specs=_hbm(n), out_specs=_hbm(n),
        out_shape=[jax.ShapeDtypeStruct((N_CHIPS, g.shape[1] // 2, g.shape[2]), F32) for g in grads],
        scratch_shapes=[pltpu.SemaphoreType.DMA((n,))] * 2,
    )(*grads)


def _chip_exchange(hs):
    n = len(hs)

    def body(*refs):
        ins, land = refs[:n], refs[n:2 * n]
        send_sems, recv_sems = refs[2 * n:]
        x, y, c = _mesh_pos()
        chip = 2 * x + y
        others = _other_chips(x, y)
        sends = []
        for w in range(n):
            for k, (px, py) in enumerate(others):
                cp = _remote(ins[w].at[2 * px + py], land[w].at[chip], send_sems.at[w, k], recv_sems.at[w, k], (px, py, c))
                cp.start()
                sends.append(cp)
        for w in range(n):
            for k, (px, py) in enumerate(others):
                got = land[w].at[2 * px + py]
                _remote(got, got, send_sems.at[w, k], recv_sems.at[w, k], (px, py, c)).wait_recv()
        for cp in sends:
            cp.wait_send()

    return pl.pallas_call(
        body, name="grads_chip_exchange", in_specs=_hbm(n), out_specs=_hbm(n),
        out_shape=[jax.ShapeDtypeStruct(h.shape, h.dtype) for h in hs],
        scratch_shapes=[pltpu.SemaphoreType.DMA((n, 3))] * 2,
    )(*hs)


def _pair_share(ts):
    n = len(ts)

    def body(*refs):
        outs = refs[n:2 * n]
        send_sems, recv_sems = refs[2 * n:]
        x, y, c = _mesh_pos()
        sends = []
        for w in range(n):
            r2 = ts[w].shape[0] // 2
            mine = outs[w].at[pl.ds(c * r2, r2)]
            rc = _remote(mine, mine, send_sems.at[w], recv_sems.at[w], (x, y, 1 - c))
            rc.start()
            sends.append(rc)
        for w in range(n):
            r2 = ts[w].shape[0] // 2
            theirs = outs[w].at[pl.ds((1 - c) * r2, r2)]
            _remote(theirs, theirs, send_sems.at[w], recv_sems.at[w], (x, y, 1 - c)).wait_recv()
            sends[w].wait_send()

    return pl.pallas_call(
        body, name="grads_pair_share", in_specs=_hbm(n), out_specs=_hbm(n),
        out_shape=[jax.ShapeDtypeStruct(t.shape, F32) for t in ts],
        input_output_aliases={w: w for w in range(n)},
        scratch_shapes=[pltpu.SemaphoreType.DMA((n,))] * 2,
    )(*ts)


def _slab_tile(rows):
    return next(t for t in (512, 256, 176, 128, 64, 32, 16) if rows % t == 0)


def _sum_pair(grad, land, pos, name):
    _, r2, cols = land.shape
    tr = _slab_tile(r2)
    nt = r2 // tr

    def kern(pos_ref, a_ref, b_ref, o_ref):
        o_ref[...] = (a_ref[...] + b_ref[...]).astype(BF16)

    spec = pl.BlockSpec((None, tr, cols), lambda j, i, p: (j, i, 0))
    return pl.pallas_call(
        kern, name=name, out_shape=jax.ShapeDtypeStruct(land.shape, BF16),
        grid_spec=_scalar_spec((N_CHIPS, nt), [pl.BlockSpec((None, tr, cols), lambda j, i, p: (j, p[0] * nt + i, 0)), spec], spec),
        compiler_params=_cparams("parallel", "parallel"),
    )(pos, grad, land)


def _sum_chips(hs, land, pos, name):
    _, r2, cols = land.shape
    tr = _slab_tile(r2)
    nt = r2 // tr

    def kern(pos_ref, h_ref, l_ref, o_ref):
        acc = jnp.zeros((tr, cols), F32)
        own = h_ref[...].astype(F32)
        for k in range(N_CHIPS):
            acc = acc + jnp.where(pos_ref[1] == k, own, l_ref[k].astype(F32))
        o_ref[...] = acc

    return pl.pallas_call(
        kern, name=name, out_shape=jax.ShapeDtypeStruct((2 * r2, cols), F32),
        grid_spec=_scalar_spec((nt,), [pl.BlockSpec((None, tr, cols), lambda i, p: (p[1], i, 0)),
                                       pl.BlockSpec((N_CHIPS, tr, cols), lambda i, p: (0, i, 0))],
                               pl.BlockSpec((tr, cols), lambda i, p: (p[0] * nt + i, 0))),
        compiler_params=_cparams("parallel"),
    )(pos, hs, land)


def _reduce_scatter(grads, pos):
    land = _pair_swap(grads)
    hs = [_sum_pair(g, l, pos, f"grads_pair_sum{w}") for w, (g, l) in enumerate(zip(grads, land))]
    land2 = _chip_exchange(hs)
    ts = [_sum_chips(h, l, pos, f"grads_chip_sum{w}") for w, (h, l) in enumerate(zip(hs, land2))]
    return _pair_share(ts)


def _all_reduce_small(v, name):
    def body(v_ref, o_ref, land_ref, send_sems, recv_sems):
        x, y, c = _mesh_pos()
        me = 4 * x + 2 * y + c
        land_ref[me] = v_ref[...]
        for t in range(N_DEVICES):
            @pl.when(t != me)
            def _(t=t):
                _remote(v_ref, land_ref.at[me], send_sems.at[t], recv_sems.at[me], (t // 4, (t // 2) % 2, t % 2)).start()
        for t in range(N_DEVICES):
            @pl.when(t != me)
            def _(t=t):
                _remote(v_ref, land_ref.at[t], send_sems.at[t], recv_sems.at[t], (t // 4, (t // 2) % 2, t % 2)).wait()
        acc = land_ref[0]
        for t in range(1, N_DEVICES):
            acc = acc + land_ref[t]
        o_ref[...] = acc

    vmem = pl.BlockSpec(memory_space=pltpu.VMEM)
    return pl.pallas_call(
        body, name=name, in_specs=[vmem], out_specs=vmem, out_shape=jax.ShapeDtypeStruct(v.shape, F32),
        scratch_shapes=[pltpu.VMEM((N_DEVICES,) + v.shape, F32), pltpu.SemaphoreType.DMA((N_DEVICES,)),
                        pltpu.SemaphoreType.DMA((N_DEVICES,))],
    )(v)


SMALL_ROWS = 24


def _pack_small(small, dlogit):
    d = D_MODEL
    misc = jnp.zeros((d,), F32)
    misc = misc.at[0:HEAD_DIM].set(small["q_norm"]).at[128:128 + HEAD_DIM].set(small["k_norm"])
    misc = misc.at[256:256 + N_HEADS].set(small["sink"]).at[384:384 + 2 * RET_HEADS].set(dlogit.reshape(-1))
    rows = [small["ada_b0"].reshape(6, d), small["ada_b1"].reshape(6, d), small["norm1_g0"][None], small["norm1_g1"][None],
            small["norm2_g0"][None], small["norm2_g1"][None], small["c_ctx"][None], small["gn_g"].reshape(2, d), misc[None]]
    buf = jnp.concatenate(rows, axis=0)
    return jnp.concatenate([buf, jnp.zeros((SMALL_ROWS - buf.shape[0], d), F32)], axis=0)


def _unpack_small(buf):
    d = D_MODEL
    misc = buf[19]
    return dict(ada_b=buf[0:12].reshape(2, 6 * d), norm1_g=buf[12:14], norm2_g=buf[14:16], c_ctx=buf[16],
                gn_g=buf[17:19].reshape(2 * d), q_norm=misc[0:HEAD_DIM], k_norm=misc[128:128 + HEAD_DIM],
                sink=misc[256:256 + N_HEADS], decay=misc[384:384 + 2 * RET_HEADS])


def kernel(x, c, ctx, c_ctx, ada_w, ada_b, norm1_g, norm2_g, ffn_w_in, ffn_w_out, attn_w_qkv, attn_q_norm, attn_k_norm, attn_sink, attn_w_o, ret_w_qkvg, ret_decay_logit, ret_gn_g, ret_w_o, loss_target, m_c_ctx, m_ada_w, m_ada_b, m_norm1_g, m_norm2_g, m_ffn_w_in, m_ffn_w_out, m_attn_w_qkv, m_attn_q_norm, m_attn_k_norm, m_attn_sink, m_attn_w_o, m_ret_w_qkvg, m_ret_decay_logit, m_ret_gn_g, m_ret_w_o, v_c_ctx, v_ada_w, v_ada_b, v_norm1_g, v_norm2_g, v_ffn_w_in, v_ffn_w_out, v_attn_w_qkv, v_attn_q_norm, v_attn_k_norm, v_attn_sink, v_attn_w_o, v_ret_w_qkvg, v_ret_decay_logit, v_ret_gn_g, v_ret_w_o):
    xi, yi, ci = _mesh_pos()
    chip = 2 * xi + yi
    nb, s, d = x.shape
    gn_shard = ret_gn_g.shape[1]

    shards = dict(ada0=ada_w[0], ada1=ada_w[1], ffn_in0=ffn_w_in[0], ffn_in1=ffn_w_in[1], ffn_out0=ffn_w_out[0],
                  ffn_out1=ffn_w_out[1], attn_qkv=attn_w_qkv[0], attn_o=attn_w_o[0], ret_qkvg=ret_w_qkvg[0], ret_o=ret_w_o[0])
    names = list(shards)
    pos = jnp.stack([ci, chip]).astype(jnp.int32)
    full = dict(zip(names, _gather_shards([_place_shard(shards[k], pos, f"place_{k}") for k in names])))
    gn_mine = jnp.where(ci == 0, ret_gn_g[0], jnp.zeros_like(ret_gn_g[0]))
    gn_place = lax.dynamic_update_slice(jnp.zeros((RET_VWIDTH,), F32), gn_mine, (chip * gn_shard,))
    gn_full = _all_reduce_small(gn_place.reshape(2, d), "gather_gn_gain").reshape(RET_VWIDTH)

    wts = dict(ada=[full["ada0"], full["ada1"]], ffn_in=[full["ffn_in0"], full["ffn_in1"]],
               ffn_out=[full["ffn_out0"].reshape(D_FF, d), full["ffn_out1"].reshape(D_FF, d)],
               attn_qkv=full["attn_qkv"], attn_o=full["attn_o"].reshape(N_HEADS * HEAD_DIM, d),
               ret_qkvg=full["ret_qkvg"], ret_o=full["ret_o"].reshape(RET_VWIDTH, d))
    decay_logit = ret_decay_logit[0]
    sp = dict(c_ctx=c_ctx, ada_b=ada_b, norm1_g=norm1_g, norm2_g=norm2_g, q_norm=attn_q_norm[0], k_norm=attn_k_norm[0],
              sink=attn_sink[0], log_g=jax.nn.log_sigmoid(decay_logit), gn_g=gn_full)
    loss_part, dz, big, small = _local_step(x, c, ctx, loss_target, sp, wts)

    loss = lax.psum(loss_part[0, 0], ("x", "y", "c"))
    grad_x = dz.reshape(nb, -1, d)[:, :s]

    dlogit = small["log_g"] * jax.nn.sigmoid(-decay_logit)
    sg = _unpack_small(_all_reduce_small(_pack_small(small, dlogit), "reduce_small_grads"))
    reduced = dict(zip(names, _reduce_scatter([big[k] for k in names], pos)))

    grads = dict(
        c_ctx=sg["c_ctx"], ada_w=jnp.stack([reduced["ada0"], reduced["ada1"]]), ada_b=sg["ada_b"], norm1_g=sg["norm1_g"],
        norm2_g=sg["norm2_g"], ffn_w_in=jnp.stack([reduced["ffn_in0"], reduced["ffn_in1"]]),
        ffn_w_out=jnp.stack([reduced["ffn_out0"], reduced["ffn_out1"]]), attn_w_qkv=reduced["attn_qkv"][None],
        attn_q_norm=sg["q_norm"][None], attn_k_norm=sg["k_norm"][None], attn_sink=sg["sink"][None],
        attn_w_o=reduced["attn_o"][None], ret_w_qkvg=reduced["ret_qkvg"][None], ret_decay_logit=sg["decay"].reshape(1, 2, RET_HEADS),
        ret_gn_g=lax.dynamic_slice(sg["gn_g"], (chip * gn_shard,), (gn_shard,))[None], ret_w_o=reduced["ret_o"][None])
    params = dict(c_ctx=(c_ctx, m_c_ctx, v_c_ctx), ada_w=(ada_w, m_ada_w, v_ada_w), ada_b=(ada_b, m_ada_b, v_ada_b),
                  norm1_g=(norm1_g, m_norm1_g, v_norm1_g), norm2_g=(norm2_g, m_norm2_g, v_norm2_g),
                  ffn_w_in=(ffn_w_in, m_ffn_w_in, v_ffn_w_in), ffn_w_out=(ffn_w_out, m_ffn_w_out, v_ffn_w_out),
                  attn_w_qkv=(attn_w_qkv, m_attn_w_qkv, v_attn_w_qkv), attn_q_norm=(attn_q_norm, m_attn_q_norm, v_attn_q_norm),
                  attn_k_norm=(attn_k_norm, m_attn_k_norm, v_attn_k_norm), attn_sink=(attn_sink, m_attn_sink, v_attn_sink),
                  attn_w_o=(attn_w_o, m_attn_w_o, v_attn_w_o), ret_w_qkvg=(ret_w_qkvg, m_ret_w_qkvg, v_ret_w_qkvg),
                  ret_decay_logit=(ret_decay_logit, m_ret_decay_logit, v_ret_decay_logit),
                  ret_gn_g=(ret_gn_g, m_ret_gn_g, v_ret_gn_g), ret_w_o=(ret_w_o, m_ret_w_o, v_ret_w_o))
    order = list(params)
    deltas, new_m, new_v = [], [], []
    for k in order:
        w, m, v = params[k]
        g = grads[k].reshape(w.shape)
        grads[k] = g
        flat = (-1, w.shape[-1]) if w.ndim > 1 else (1, -1)
        if k == "ret_decay_logit":
            flat = (1, -1)
        dw, nm, nv = _adamw(w.reshape(flat), g.reshape(flat), m.reshape(flat), v.reshape(flat), f"adamw_{k}")
        deltas.append(dw.reshape(w.shape))
        new_m.append(nm.reshape(w.shape))
        new_v.append(nv.reshape(w.shape))
    return (loss, grad_x, *[grads[k] for k in order], *deltas, *new_m, *new_v)
```

```python
import functools

import jax
import jax.numpy as jnp
from jax import lax
from jax.experimental import pallas as pl
from jax.experimental.pallas import tpu as pltpu

F32 = jnp.float32
BF16 = jnp.bfloat16

D_MODEL = 1024
N_HEADS = 16
N_KV_HEADS = 4
HEAD_DIM = 64
WINDOW = 128
ATTN_BLOCK = 128
BAND = ATTN_BLOCK + 2 * WINDOW
RET_HEADS = 4
RET_QK_DIM = 256
RET_V_DIM = 512
RET_VWIDTH = 2048
RET_CHUNK = 128
D_FF = 2816
GRID_W = 64
ROPE_BASE = 10000.0
EPS = 1e-6
NEG_INF = -1e30
LANES = 128

ADAM_LR = 0.001
ADAM_B1 = 0.9
ADAM_B2 = 0.999
ADAM_EPS = 1e-08
ADAM_WD = 0.01
ADAM_STEP = 10

VMEM_LIMIT_BYTES = 56 * 1024 * 1024
MESH = pl.DeviceIdType.MESH
N_CHIPS = 4


def _cparams(*sem):
    return pltpu.CompilerParams(dimension_semantics=sem, vmem_limit_bytes=VMEM_LIMIT_BYTES)


_DIMS = {"nn": ((1,), (0,)), "nt": ((1,), (1,)), "tn": ((0,), (0,))}


def _dot(a, b, form):
    return lax.dot_general(a.astype(BF16), b.astype(BF16), (_DIMS[form], ((), ())), preferred_element_type=F32)


@functools.partial(jax.custom_vjp, nondiff_argnums=(2,))
def _mm(a, b, form):
    return _dot(a, b, form)


def _mm_fwd(a, b, form):
    return _dot(a, b, form), (a, b)


def _mm_bwd(form, res, ct):
    a, b = res
    if form == "nn":
        da, db = _dot(ct, b, "nt"), _dot(a, ct, "tn")
    elif form == "nt":
        da, db = _dot(ct, b, "nn"), _dot(ct, a, "tn")
    else:
        da, db = _dot(b, ct, "nt"), _dot(a, ct, "nn")
    return da.astype(a.dtype), db.astype(b.dtype)


_mm.defvjp(_mm_fwd, _mm_bwd)


def _swap_halves(x, half):
    w = x.shape[-1]
    lane = lax.broadcasted_iota(jnp.int32, x.shape, x.ndim - 1)
    return jnp.where(lane % (2 * half) < half, pltpu.roll(x, w - half, x.ndim - 1), pltpu.roll(x, half, x.ndim - 1))


@functools.partial(jax.custom_vjp, nondiff_argnums=(1,))
def _rot(x, half):
    return _swap_halves(x, half)


def _rot_fwd(x, half):
    return _swap_halves(x, half), None


def _rot_bwd(half, _, ct):
    return (_swap_halves(ct, half),)


_rot.defvjp(_rot_fwd, _rot_bwd)


def _rope(x, cos, sin_signed, half):
    return x * cos + _rot(x, half) * sin_signed


def _head_mean_square(x):
    r = lax.broadcasted_iota(jnp.int32, (LANES, LANES), 0) // HEAD_DIM
    c = lax.broadcasted_iota(jnp.int32, (LANES, LANES), 1) // HEAD_DIM
    g = jnp.where(r == c, 1.0 / HEAD_DIM, 0.0).astype(F32)
    return jnp.dot(x * x, g, precision=lax.Precision.HIGHEST, preferred_element_type=F32)


def _qk_chunk(x, gain, cos, sin_signed, scale):
    y = x * lax.rsqrt(_head_mean_square(x) + EPS) * gain
    return _rope(y, cos, sin_signed, HEAD_DIM // 4) * scale


def _sigmoid(x):
    return 1.0 / (1.0 + jnp.exp(-x))


def _silu(x):
    return x * _sigmoid(x)


def _mm_nn(a, w, out_dtype, name, tm, tn, tk, bias=None):
    m, k_dim = a.shape
    if w.ndim == 3:
        n = w.shape[0] * w.shape[2]
        per = w.shape[2] // tn
        assert w.shape[2] % tn == 0
        w_spec = pl.BlockSpec((None, tk, tn), lambda i, j, k: (j // per, k, j % per))
    else:
        n = w.shape[1]
        w_spec = pl.BlockSpec((tk, tn), lambda i, j, k: (k, j))
    assert m % tm == 0 and n % tn == 0 and k_dim % tk == 0, (name, a.shape, w.shape, tm, tn, tk)
    nk = k_dim // tk
    has_bias = bias is not None

    def body(*refs):
        a_ref, w_ref = refs[0], refs[1]
        b_ref = refs[2] if has_bias else None
        o_ref, acc_ref = refs[-2], refs[-1]
        if nk == 1:
            part = jnp.dot(a_ref[...].astype(BF16), w_ref[...], preferred_element_type=F32)
            o_ref[...] = (part + b_ref[...] if has_bias else part).astype(out_dtype)
            return
        k = pl.program_id(2)

        @pl.when(k == 0)
        def _():
            acc_ref[...] = jnp.zeros_like(acc_ref)

        acc_ref[...] += jnp.dot(a_ref[...].astype(BF16), w_ref[...], preferred_element_type=F32)

        @pl.when(k == nk - 1)
        def _():
            r = acc_ref[...]
            if has_bias:
                r = r + b_ref[...]
            o_ref[...] = r.astype(out_dtype)

    in_specs = [pl.BlockSpec((tm, tk), lambda i, j, k: (i, k)), w_spec]
    args = [a, w]
    if has_bias:
        in_specs.append(pl.BlockSpec((1, tn), lambda i, j, k: (0, j)))
        args.append(bias)
    return pl.pallas_call(
        body, name=name, grid=(m // tm, n // tn, nk), in_specs=in_specs,
        out_specs=pl.BlockSpec((tm, tn), lambda i, j, k: (i, j)),
        out_shape=jax.ShapeDtypeStruct((m, n), out_dtype),
        scratch_shapes=[pltpu.VMEM((tm, tn), F32)],
        compiler_params=_cparams("parallel", "parallel", "arbitrary"),
    )(*args)


def _mm_nt(a, w, out_dtype, name, tm, tn, tk):
    if a.ndim == 3:
        planes, m, plane_w = a.shape
        c_dim = planes * plane_w
        a_per = plane_w // tk
        assert plane_w % tk == 0
        a_spec = pl.BlockSpec((None, tm, tk), lambda i, j, k: (k // a_per, i, k % a_per))
    else:
        m, c_dim = a.shape
        a_spec = pl.BlockSpec((tm, tk), lambda i, j, k: (i, k))
    if w.ndim == 3:
        k_out = w.shape[1]
        per = w.shape[2] // tk
        assert w.shape[2] % tk == 0 and w.shape[0] * w.shape[2] == c_dim
        w_spec = pl.BlockSpec((None, tn, tk), lambda i, j, k: (k // per, j, k % per))
    else:
        k_out = w.shape[0]
        assert w.shape[1] == c_dim
        w_spec = pl.BlockSpec((tn, tk), lambda i, j, k: (j, k))
    assert m % tm == 0 and k_out % tn == 0 and c_dim % tk == 0, (name, a.shape, w.shape, tm, tn, tk)
    nk = c_dim // tk

    def body(a_ref, w_ref, o_ref, acc_ref):
        if nk == 1:
            o_ref[...] = _dot(a_ref[...], w_ref[...], "nt").astype(out_dtype)
            return
        k = pl.program_id(2)

        @pl.when(k == 0)
        def _():
            acc_ref[...] = jnp.zeros_like(acc_ref)

        acc_ref[...] += _dot(a_ref[...], w_ref[...], "nt")

        @pl.when(k == nk - 1)
        def _():
            o_ref[...] = acc_ref[...].astype(out_dtype)

    return pl.pallas_call(
        body, name=name, grid=(m // tm, k_out // tn, nk),
        in_specs=[a_spec, w_spec],
        out_specs=pl.BlockSpec((tm, tn), lambda i, j, k: (i, j)),
        out_shape=jax.ShapeDtypeStruct((m, k_out), out_dtype),
        scratch_shapes=[pltpu.VMEM((tm, tn), F32)],
        compiler_params=_cparams("parallel", "parallel", "arbitrary"),
    )(a, w)


def _mm_tn(a, b, name, tm, tn, tk, shards=None):
    r, k_dim = a.shape
    if b.ndim == 3:
        n = b.shape[0] * b.shape[2]
        b_per = b.shape[2] // tn
        assert b.shape[2] % tn == 0
        b_spec = pl.BlockSpec((None, tk, tn), lambda i, j, k: (j // b_per, k, j % b_per))
    else:
        n = b.shape[1]
        b_spec = pl.BlockSpec((tk, tn), lambda i, j, k: (k, j))
    assert r % tk == 0 and k_dim % tm == 0 and n % tn == 0, (name, a.shape, b.shape, tm, tn, tk)
    nk = r // tk
    if shards:
        per = n // shards // tn
        assert n % (shards * tn) == 0
        out_shape = jax.ShapeDtypeStruct((shards, k_dim, n // shards), F32)
        out_spec = pl.BlockSpec((None, tm, tn), lambda i, j, k: (j // per, i, j % per))
    else:
        out_shape = jax.ShapeDtypeStruct((k_dim, n), F32)
        out_spec = pl.BlockSpec((tm, tn), lambda i, j, k: (i, j))

    def body(a_ref, b_ref, o_ref):
        k = pl.program_id(2)

        @pl.when(k == 0)
        def _():
            o_ref[...] = jnp.zeros_like(o_ref)

        o_ref[...] += _dot(a_ref[...], b_ref[...], "tn")

    return pl.pallas_call(
        body, name=name, grid=(k_dim // tm, n // tn, nk),
        in_specs=[pl.BlockSpec((tk, tm), lambda i, j, k: (k, i)), b_spec],
        out_specs=out_spec, out_shape=out_shape,
        compiler_params=_cparams("parallel", "parallel", "arbitrary"),
    )(a, b)


class _Carrier:
    def __init__(self, job, n_in, n_out, n_scratch):
        self.job, self.n_in, self.n_out, self.n_scratch = job, n_in, n_out, n_scratch
        self.ji = len(job.inputs) if job else 0
        self.jo = len(job.out_shapes) if job else 0

    def operands(self):
        return list(self.job.inputs) if self.job else []

    def in_specs(self):
        return [pl.BlockSpec(memory_space=pl.ANY)] * self.ji

    def out_specs(self):
        return [pl.BlockSpec(memory_space=pl.ANY)] * self.jo

    def out_shapes(self):
        return list(self.job.out_shapes) if self.job else []

    def scratch(self):
        return list(self.job.sem_shapes) if self.job else []

    def aliases(self):
        return {self.n_in + a: self.n_out + b for a, b in self.job.aliases.items()} if self.job else {}

    def split(self, refs):
        a = self.n_in
        b = a + self.ji
        c = b + self.n_out
        d = c + self.jo
        e = d + self.n_scratch
        return list(refs[:a]) + list(refs[b:c]) + list(refs[d:e]), (refs[a:b], refs[c:d], refs[e:])

    def run(self, job_refs, step, steps):
        if not self.job:
            return
        for stage, mark in zip(self.job.stages, _job_marks(self.job, steps)):
            pl.when(step == mark)(functools.partial(stage, *job_refs))

    def results(self, res):
        res = list(res)
        return res[:self.n_out], res[self.n_out:]


FFN_ROW_TILE = 768


def _ffn_tile(r):
    return FFN_ROW_TILE if r % FFN_ROW_TILE == 0 else _row_tile(r)


def _ffn_in_swiglu(h, w, name):
    r, k_dim = h.shape
    n4 = w.shape[2]
    tm = _ffn_tile(r)

    def body(h_ref, wg_ref, wu_ref, u_ref, a_ref):
        hv = h_ref[...]
        g = jnp.dot(hv, wg_ref[...], preferred_element_type=F32)
        up = jnp.dot(hv, wu_ref[...], preferred_element_type=F32)
        u_ref[0] = g.astype(BF16)
        u_ref[1] = up.astype(BF16)
        a_ref[...] = (_silu(g) * up).astype(BF16)

    return pl.pallas_call(
        body, name=name, grid=(r // tm, 2),
        in_specs=[pl.BlockSpec((tm, k_dim), lambda i, j: (i, 0)),
                  pl.BlockSpec((None, k_dim, n4), lambda i, j: (j, 0, 0)),
                  pl.BlockSpec((None, k_dim, n4), lambda i, j: (j + 2, 0, 0))],
        out_specs=[pl.BlockSpec((2, tm, n4), lambda i, j: (0, i, j)), pl.BlockSpec((tm, n4), lambda i, j: (i, j))],
        out_shape=[jax.ShapeDtypeStruct((2, r, 2 * n4), BF16), jax.ShapeDtypeStruct((r, 2 * n4), BF16)],
        compiler_params=_cparams("parallel", "parallel"),
    )(h, w, w)


def _mm_nn_gate_residual(geo, a, w, z, mod, off, name):
    r, k_dim = a.shape
    n = w.shape[1]
    tm = FFN_ROW_TILE if geo.seg % FFN_ROW_TILE == 0 else 256
    tiles = geo.seg // tm
    assert geo.seg % tm == 0 and r == geo.r

    def body(a_ref, w_ref, z_ref, mx_ref, mc_ref, zo_ref, raw_ref):
        out = jnp.dot(a_ref[...], w_ref[...], preferred_element_type=F32)
        row = (pl.program_id(0) % tiles) * tm + lax.broadcasted_iota(jnp.int32, (tm, 1), 0)
        gate = jnp.where(row < geo.s, mx_ref[:, off:off + n], mc_ref[:, off:off + n])
        zo_ref[...] = z_ref[...] + gate * out
        raw_ref[...] = out.astype(BF16)

    mod_w = mod.shape[2]
    return pl.pallas_call(
        body, name=name, grid=(r // tm,),
        in_specs=[pl.BlockSpec((tm, k_dim), lambda i: (i, 0)), pl.BlockSpec((k_dim, n), lambda i: (0, 0)),
                  pl.BlockSpec((tm, n), lambda i: (i, 0)),
                  pl.BlockSpec((None, 1, mod_w), lambda i: (i // tiles, 0, 0)),
                  pl.BlockSpec((None, 1, mod_w), lambda i: (geo.b, 0, 0))],
        out_specs=[pl.BlockSpec((tm, n), lambda i: (i, 0)), pl.BlockSpec((tm, n), lambda i: (i, 0))],
        out_shape=[jax.ShapeDtypeStruct((r, n), F32), jax.ShapeDtypeStruct((r, n), BF16)],
        compiler_params=_cparams("parallel"),
    )(a, w, z, mod, mod)


def _ffn_out_dx_swiglu_bwd(df, w_out, u, name, job=None):
    r, d = df.shape
    n4 = u.shape[2] // 2
    tm = _ffn_tile(r)
    carrier = _Carrier(job, 3, 1, 0)
    steps = (r // tm) * 2

    def body(*refs):
        (df_ref, w_ref, u_ref, du_ref), job_refs = carrier.split(refs)
        carrier.run(job_refs, pl.program_id(0) * 2 + pl.program_id(1), steps)
        da = _dot(df_ref[...], w_ref[...], "nt")
        g, up = u_ref[0].astype(F32), u_ref[1].astype(F32)
        s = _sigmoid(g)
        du_ref[0] = (da * up * (s * (1.0 + g * (1.0 - s)))).astype(BF16)
        du_ref[1] = (da * (g * s)).astype(BF16)

    res = pl.pallas_call(
        body, name=name, grid=(r // tm, 2),
        in_specs=[pl.BlockSpec((tm, d), lambda i, j: (i, 0)), pl.BlockSpec((n4, d), lambda i, j: (j, 0)),
                  pl.BlockSpec((2, tm, n4), lambda i, j: (0, i, j))] + carrier.in_specs(),
        out_specs=[pl.BlockSpec((2, tm, n4), lambda i, j: (0, i, j))] + carrier.out_specs(),
        out_shape=[jax.ShapeDtypeStruct(u.shape, BF16)] + carrier.out_shapes(),
        scratch_shapes=carrier.scratch(), input_output_aliases=carrier.aliases(),
        compiler_params=_cparams("arbitrary", "arbitrary"),
    )(df, w_out, u, *carrier.operands())
    (du,), extra = carrier.results(res)
    return du, extra


class _Rows:
    def __init__(self, b, s, l):
        self.b, self.s, self.l = b, s, l
        self.seg = s + l
        self.r = b * self.seg


def _rowwise(name, body, geo, tm, ins, outs):
    seg_blocks, x_blocks = geo.seg // tm, geo.s // tm
    assert geo.seg % tm == 0 and geo.s % tm == 0
    nb = geo.b

    def is_ctx(i):
        return i % seg_blocks >= x_blocks

    in_specs, args = [], []
    for arr, kind in ins:
        args.append(arr)
        if kind == "row":
            in_specs.append(pl.BlockSpec((tm, arr.shape[1]), lambda i: (i, 0)))
        elif kind == "ex":
            in_specs.append(pl.BlockSpec((None, 1, arr.shape[2]), lambda i: (jnp.where(is_ctx(i), nb, i // seg_blocks), 0, 0)))
        elif kind == "full":
            in_specs.append(pl.BlockSpec(arr.shape, lambda i, nd=arr.ndim: (0,) * nd))
        elif kind == "tab":
            in_specs.append(pl.BlockSpec((tm, arr.shape[1]), lambda i: (i % seg_blocks, 0)))
        elif kind == "xrow":
            in_specs.append(pl.BlockSpec(
                (tm, arr.shape[1]), lambda i: ((i // seg_blocks) * x_blocks + jnp.minimum(i % seg_blocks, x_blocks - 1), 0)))
        else:
            _, width, cb = kind
            in_specs.append(pl.BlockSpec((tm, width), lambda i, cb=cb: (i, cb)))
    out_specs, out_shapes = [], []
    for o in outs:
        if o[0] == "row":
            out_specs.append(pl.BlockSpec((tm, o[1]), lambda i: (i, 0)))
            out_shapes.append(jax.ShapeDtypeStruct((geo.r, o[1]), o[2]))
        elif o[0] == "exacc":
            out_specs.append(pl.BlockSpec((None, 1, o[1]), lambda i: (jnp.where(is_ctx(i), nb, 0) + i // seg_blocks, 0, 0)))
            out_shapes.append(jax.ShapeDtypeStruct((2 * nb, 1, o[1]), F32))
        else:
            out_specs.append(pl.BlockSpec((o[1], o[2]), lambda i: (0, 0)))
            out_shapes.append(jax.ShapeDtypeStruct((o[1], o[2]), F32))
    n_in = len(ins)

    def kern(*refs):
        i = pl.program_id(0)
        res = body(i, *[r[...].astype(F32) for r in refs[:n_in]])
        if not isinstance(res, (tuple, list)):
            res = (res,)
        jj = i % seg_blocks
        first_of_part = (jj == 0) | (jj == x_blocks)
        for o, ref, val in zip(outs, refs[n_in:], res):
            if o[0] == "row":
                ref[...] = val.astype(ref.dtype)
            else:
                first = first_of_part if o[0] == "exacc" else i == 0

                @pl.when(first)
                def _(ref=ref, val=val):
                    ref[...] = val

                @pl.when(jnp.logical_not(first))
                def _(ref=ref, val=val):
                    ref[...] += val

    res = pl.pallas_call(
        kern, name=name, grid=(geo.r // tm,), in_specs=in_specs, out_specs=out_specs, out_shape=out_shapes,
        compiler_params=_cparams("arbitrary"),
    )(*args)
    return res[0] if len(res) == 1 else res


def _colsum(v):
    return jnp.sum(v, axis=0, keepdims=True)


def _norm_mod(geo, z, gain, mod, off, name):
    d = D_MODEL

    def body(i, zv, g, m):
        r = lax.rsqrt(jnp.mean(zv * zv, axis=-1, keepdims=True) + EPS)
        return (zv * r) * g * (1.0 + m[:, off + d:off + 2 * d]) + m[:, off:off + d]

    return _rowwise(name, body, geo, 256, [(z, "row"), (gain, "full"), (mod, "ex")], [("row", d, BF16)])


def _norm_mod_bwd(geo, z, gain, mod, off, dh, dz_skip, name):
    d = D_MODEL

    def body(i, zv, g, m, dhv, skip):
        r = lax.rsqrt(jnp.mean(zv * zv, axis=-1, keepdims=True) + EPS)
        n = zv * r
        dng = dhv * (1.0 + m[:, off + d:off + 2 * d])
        dn = dng * g
        dz = r * (dn - n * jnp.mean(dn * n, axis=-1, keepdims=True)) + skip
        return dz, _colsum(dhv), _colsum(dhv * (n * g)), _colsum(dng * n)

    return _rowwise(name, body, geo, 256, [(z, "row"), (gain, "full"), (mod, "ex"), (dh, "row"), (dz_skip, "row")],
                    [("row", d, F32), ("exacc", d), ("exacc", d), ("gacc", 1, d)])


def _gate_residual_bwd(geo, dz, out, mod, off, name):
    d = D_MODEL

    def body(i, dzv, ov, m):
        return dzv * m[:, off:off + d], _colsum(dzv * ov)

    return _rowwise(name, body, geo, 256, [(dz, "row"), (out, "row"), (mod, "ex")], [("row", d, BF16), ("exacc", d)])


def _loss_head(geo, z, target, name):
    seg_blocks, x_blocks = geo.seg // 256, geo.s // 256

    def body(i, zv, tv):
        keep = jnp.where(i % seg_blocks >= x_blocks, 0.0, 1.0)
        err = (zv - tv) * keep
        part = 0.5 * jnp.sum(jnp.mean(err * err, axis=-1, keepdims=True), axis=0, keepdims=True)
        return err * (1.0 / D_MODEL), jnp.broadcast_to(part, (1, LANES))

    return _rowwise(name, body, geo, 256, [(z, "row"), (target, "xrow")], [("row", D_MODEL, F32), ("gacc", 1, LANES)])


Q_SCALE = HEAD_DIM ** -0.5
N_QK_CHUNKS = (N_HEADS + N_KV_HEADS) * HEAD_DIM // LANES
N_Q_CHUNKS = N_HEADS * HEAD_DIM // LANES


def _attn_prep(geo, proj, cos, sin_signed, q_gain, k_gain, name):
    def body(i, p, cs, sn, qg, kg):
        outs = []
        for ch in range(N_QK_CHUNKS):
            is_q = ch < N_Q_CHUNKS
            outs.append(_qk_chunk(p[:, ch * LANES:(ch + 1) * LANES], qg if is_q else kg, cs, sn, Q_SCALE if is_q else 1.0))
        outs.append(p[:, N_QK_CHUNKS * LANES:])
        return jnp.concatenate(outs, axis=1)

    return _rowwise(name, body, geo, 256, [(proj, "row"), (cos, "tab"), (sin_signed, "tab"), (q_gain, "full"), (k_gain, "full")],
                    [("row", proj.shape[1], BF16)])


def _attn_prep_bwd(geo, proj, cos, sin_signed, q_gain, k_gain, dq, dkv, name):
    kw = N_KV_HEADS * HEAD_DIM

    def body(i, p, cs, sn, qg, kg, dqv, dkvv):
        outs = []
        dgains = [jnp.zeros((1, LANES), F32), jnp.zeros((1, LANES), F32)]
        for ch in range(N_QK_CHUNKS):
            is_q = ch < N_Q_CHUNKS
            scale = Q_SCALE if is_q else 1.0
            ct = dqv[:, ch * LANES:(ch + 1) * LANES] if is_q else dkvv[:, (ch - N_Q_CHUNKS) * LANES:(ch - N_Q_CHUNKS + 1) * LANES]
            _, vjp = jax.vjp(lambda xx, gg, scale=scale: _qk_chunk(xx, gg, cs, sn, scale),
                             p[:, ch * LANES:(ch + 1) * LANES], qg if is_q else kg)
            dx, dg = vjp(ct)
            outs.append(dx)
            dgains[0 if is_q else 1] = dgains[0 if is_q else 1] + dg
        outs.append(dkvv[:, kw:])
        return jnp.concatenate(outs, axis=1), dgains[0], dgains[1]

    return _rowwise(name, body, geo, 256,
                    [(proj, "row"), (cos, "tab"), (sin_signed, "tab"), (q_gain, "full"), (k_gain, "full"), (dq, "row"), (dkv, "row")],
                    [("row", proj.shape[1], BF16), ("gacc", 1, LANES), ("gacc", 1, LANES)])


def _attn_geometry(geo):
    assert geo.s % ATTN_BLOCK == 0 and geo.l % ATTN_BLOCK == 0 and geo.seg >= BAND
    return geo.seg // ATTN_BLOCK, geo.s // ATTN_BLOCK


def _attn_mask(j, s0, geo):
    r = lax.broadcasted_iota(jnp.int32, (ATTN_BLOCK, BAND), 0)
    n = lax.broadcasted_iota(jnp.int32, (ATTN_BLOCK, BAND), 1)
    dist = (s0 - j * ATTN_BLOCK) + n - r
    return (jnp.abs(dist) <= WINDOW) & (s0 + n < geo.s)


def _attn_probs(q, keys, valid, n_ctx, sink):
    s = _dot(q, keys, "nt")
    if valid is not None:
        s = jnp.concatenate([s[:, :n_ctx], jnp.where(valid, s[:, n_ctx:], NEG_INF)], axis=1)
    m = jnp.maximum(jnp.max(s, axis=-1, keepdims=True), sink)
    e, e_sink = jnp.exp(s - m), jnp.exp(sink - m)
    inv = 1.0 / (jnp.sum(e, axis=-1, keepdims=True) + e_sink)
    return e * inv, e_sink * inv


def _attn_keys(ref, s0, geo, with_band):
    ctx = ref[geo.s:geo.seg, :]
    return jnp.concatenate([ctx, ref[pl.ds(s0, BAND), :]], axis=0) if with_band else ctx


def _attention(geo, qkv, sink, name, job=None):
    n_blocks, n_x_blocks = _attn_geometry(geo)
    qw, kw = N_HEADS * HEAD_DIM, N_KV_HEADS * HEAD_DIM
    group = N_HEADS // N_KV_HEADS
    carrier = _Carrier(job, 4, 1, 0)

    def kern(*refs):
        (sink_ref, q_ref, k_ref, v_ref, o_ref), job_refs = carrier.split(refs)
        j = pl.program_id(1)
        carrier.run(job_refs, pl.program_id(0) * n_blocks + j, geo.b * n_blocks)
        s0 = pl.multiple_of(jnp.clip((j - 1) * ATTN_BLOCK, 0, geo.seg - BAND), ATTN_BLOCK)

        def heads(with_band):
            valid = _attn_mask(j, s0, geo) if with_band else None
            k_all, v_all = _attn_keys(k_ref, s0, geo, with_band), _attn_keys(v_ref, s0, geo, with_band)
            for h in range(N_HEADS):
                kv = slice((h // group) * HEAD_DIM, (h // group + 1) * HEAD_DIM)
                p, _ = _attn_probs(q_ref[:, h * HEAD_DIM:(h + 1) * HEAD_DIM], k_all[:, kv], valid, geo.l, sink_ref[h])
                o_ref[:, h * HEAD_DIM:(h + 1) * HEAD_DIM] = _dot(p, v_all[:, kv], "nn").astype(BF16)

        pl.when(j < n_x_blocks)(lambda: heads(True))
        pl.when(j >= n_x_blocks)(lambda: heads(False))

    res = pl.pallas_call(
        kern, name=name, grid=(geo.b, n_blocks),
        in_specs=[pl.BlockSpec(memory_space=pltpu.SMEM),
                  pl.BlockSpec((ATTN_BLOCK, qw), lambda b, j: (b * n_blocks + j, 0)),
                  pl.BlockSpec((geo.seg, kw), lambda b, j: (b, qw // kw)),
                  pl.BlockSpec((geo.seg, kw), lambda b, j: (b, qw // kw + 1))] + carrier.in_specs(),
        out_specs=[pl.BlockSpec((ATTN_BLOCK, qw), lambda b, j: (b * n_blocks + j, 0))] + carrier.out_specs(),
        out_shape=[jax.ShapeDtypeStruct((geo.r, qw), BF16)] + carrier.out_shapes(),
        scratch_shapes=carrier.scratch(), input_output_aliases=carrier.aliases(),
        compiler_params=_cparams("arbitrary", "arbitrary"),
    )(sink, qkv, qkv, qkv, *carrier.operands())
    (o,), extra = carrier.results(res)
    return o, extra


def _attention_bwd(geo, qkv, sink, do, name, job=None):
    n_blocks, n_x_blocks = _attn_geometry(geo)
    qw, kw = N_HEADS * HEAD_DIM, N_KV_HEADS * HEAD_DIM
    group = N_HEADS // N_KV_HEADS

    carrier = _Carrier(job, 5, 3, 0)

    def kern(*refs):
        (sink_ref, q_ref, k_ref, v_ref, do_ref, dq_ref, dkv_ref, dsink_ref), job_refs = carrier.split(refs)
        b, j = pl.program_id(0), pl.program_id(1)
        carrier.run(job_refs, b * n_blocks + j, geo.b * n_blocks)
        s0 = pl.multiple_of(jnp.clip((j - 1) * ATTN_BLOCK, 0, geo.seg - BAND), ATTN_BLOCK)

        @pl.when(j == 0)
        def _():
            dkv_ref[...] = jnp.zeros_like(dkv_ref)

        @pl.when((j == 0) & (b == 0))
        def _():
            dsink_ref[...] = jnp.zeros_like(dsink_ref)

        def heads(with_band):
            valid = _attn_mask(j, s0, geo) if with_band else None
            k_all, v_all = _attn_keys(k_ref, s0, geo, with_band), _attn_keys(v_ref, s0, geo, with_band)
            for g in range(N_KV_HEADS):
                kv = slice(g * HEAD_DIM, (g + 1) * HEAD_DIM)
                keys, vals = k_all[:, kv], v_all[:, kv]
                group_heads = [slice(h * HEAD_DIM, (h + 1) * HEAD_DIM) for h in range(g * group, (g + 1) * group)]
                ds_rows, p_rows = [], []
                for h, hs in zip(range(g * group, (g + 1) * group), group_heads):
                    dout = do_ref[:, hs]
                    p, p_sink = _attn_probs(q_ref[:, hs], keys, valid, geo.l, sink_ref[h])
                    dp = _dot(dout, vals, "nt")
                    dsum = jnp.sum(p * dp, axis=-1, keepdims=True)
                    ds = (p * (dp - dsum)).astype(BF16)
                    dq_ref[:, hs] = _dot(ds, keys, "nn")
                    ds_rows.append(ds)
                    p_rows.append(p.astype(BF16))
                    dsink_ref[h:h + 1, :] += jnp.broadcast_to(-jnp.sum(p_sink * dsum, axis=0, keepdims=True), (1, LANES))
                q_rows = jnp.concatenate([q_ref[:, hs] for hs in group_heads], axis=0)
                do_rows = jnp.concatenate([do_ref[:, hs] for hs in group_heads], axis=0)
                dk = _dot(jnp.concatenate(ds_rows, axis=0), q_rows, "tn")
                dv = _dot(jnp.concatenate(p_rows, axis=0), do_rows, "tn")
                vv = slice(kw + g * HEAD_DIM, kw + (g + 1) * HEAD_DIM)
                dkv_ref[geo.s:geo.seg, kv] += dk[:geo.l]
                dkv_ref[geo.s:geo.seg, vv] += dv[:geo.l]
                if with_band:
                    dkv_ref[pl.ds(s0, BAND), kv] += dk[geo.l:]
                    dkv_ref[pl.ds(s0, BAND), vv] += dv[geo.l:]

        pl.when(j < n_x_blocks)(lambda: heads(True))
        pl.when(j >= n_x_blocks)(lambda: heads(False))

    res = pl.pallas_call(
        kern, name=name, grid=(geo.b, n_blocks),
        in_specs=[pl.BlockSpec(memory_space=pltpu.SMEM),
                  pl.BlockSpec((ATTN_BLOCK, qw), lambda b, j: (b * n_blocks + j, 0)),
                  pl.BlockSpec((geo.seg, kw), lambda b, j: (b, qw // kw)),
                  pl.BlockSpec((geo.seg, kw), lambda b, j: (b, qw // kw + 1)),
                  pl.BlockSpec((ATTN_BLOCK, qw), lambda b, j: (b * n_blocks + j, 0))] + carrier.in_specs(),
        out_specs=[pl.BlockSpec((ATTN_BLOCK, qw), lambda b, j: (b * n_blocks + j, 0)),
                   pl.BlockSpec((geo.seg, 2 * kw), lambda b, j: (b, 0)),
                   pl.BlockSpec((N_HEADS, LANES), lambda b, j: (0, 0))] + carrier.out_specs(),
        out_shape=[jax.ShapeDtypeStruct((geo.r, qw), F32), jax.ShapeDtypeStruct((geo.r, 2 * kw), F32),
                   jax.ShapeDtypeStruct((N_HEADS, LANES), F32)] + carrier.out_shapes(),
        scratch_shapes=carrier.scratch(), input_output_aliases=carrier.aliases(),
        compiler_params=_cparams("arbitrary", "arbitrary"),
    )(sink, qkv, qkv, qkv, do, *carrier.operands())
    (dq, dkv, dsink), extra = carrier.results(res)
    return dq, dkv, dsink, extra


RET_QK_W = RET_HEADS * RET_QK_DIM
K_SCALE = RET_QK_DIM ** -0.5


def _ret_prep(geo, proj, cos, sin_signed, name):
    def body(i, p, cs, sn):
        cs2, sn2 = jnp.concatenate([cs] * RET_HEADS, axis=1), jnp.concatenate([sn] * RET_HEADS, axis=1)
        q = _rope(p[:, :RET_QK_W], cs2, sn2, RET_QK_DIM // 4)
        k = _rope(p[:, RET_QK_W:2 * RET_QK_W], cs2, sn2, RET_QK_DIM // 4) * K_SCALE
        return jnp.concatenate([q, k, p[:, 2 * RET_QK_W:]], axis=1)

    return _rowwise(name, body, geo, 128, [(proj, ("rowc", 2 * RET_QK_W + RET_VWIDTH, 0)), (cos, "tab"), (sin_signed, "tab")],
                    [("row", 2 * RET_QK_W + RET_VWIDTH, BF16)])


def _ret_prep_bwd(geo, dq, dk, dv, dgate, cos, sin_signed, name):
    def body(i, dqv, dkv, dvv, dg, cs, sn):
        cs2, sn2 = jnp.concatenate([cs] * RET_HEADS, axis=1), jnp.concatenate([sn] * RET_HEADS, axis=1)
        dkv = dkv * K_SCALE
        dqv = dqv * cs2 + _swap_halves(dqv * sn2, RET_QK_DIM // 4)
        dkv = dkv * cs2 + _swap_halves(dkv * sn2, RET_QK_DIM // 4)
        return jnp.concatenate([dqv, dkv, dvv, dg], axis=1)

    return _rowwise(name, body, geo, 128,
                    [(dq, "row"), (dk, "row"), (dv, "row"), (dgate, "row"), (cos, "tab"), (sin_signed, "tab")],
                    [("row", 2 * RET_QK_W + 2 * RET_VWIDTH, BF16)])


def _ret_step(state, q, k, v, lg, rev):
    c = RET_CHUNK
    ri = lax.broadcasted_iota(jnp.int32, (c, 1), 0).astype(F32)
    cj = lax.broadcasted_iota(jnp.int32, (1, c), 1).astype(F32)
    if rev:
        dist, q_decay, k_decay = cj - ri, jnp.exp(lg * (c - ri)), jnp.exp(lg * ri)
    else:
        dist, q_decay, k_decay = ri - cj, jnp.exp(lg * (ri + 1.0)), jnp.exp(lg * (c - 1.0 - ri))
    intra = jnp.where(dist >= 0, jnp.exp(lg * jnp.maximum(dist, 0.0)), 0.0)
    scores = _mm(q, k, "nt") * intra
    out = _mm(scores, v, "nn") + _mm(q, state, "nn") * q_decay
    new_state = state * jnp.exp(lg * c) + _mm(k * k_decay, v, "tn")
    return new_state, out


def _ret_state0(kc, vc, lg, rev):
    n = kc.shape[0]
    t = lax.broadcasted_iota(jnp.int32, (n, 1), 0).astype(F32)
    decay = jnp.exp(lg * t) if rev else jnp.exp(lg * (n - 1.0 - t))
    return _mm(kc * decay, vc, "tn")


def _ret_specs(geo):
    nq = RET_HEADS
    return [pl.BlockSpec((2 * RET_HEADS, LANES), lambda b, h: (0, 0)),
            pl.BlockSpec((geo.seg, RET_QK_DIM), lambda b, h: (b, h)),
            pl.BlockSpec((geo.seg, RET_QK_DIM), lambda b, h: (b, nq + h)),
            pl.BlockSpec((geo.seg, RET_V_DIM), lambda b, h: (b, nq + h))]


def _retention(geo, qkv, log_g, name):
    nc = geo.s // RET_CHUNK

    def kern(lg_ref, q_ref, k_ref, v_ref, o_ref, st_ref):
        h = pl.program_id(1)
        for d, rev in ((0, False), (1, True)):
            lg = lg_ref[pl.ds(d * RET_HEADS + h, 1), 0:1]
            st_ref[...] = _ret_state0(k_ref[geo.s:geo.seg, :].astype(F32), v_ref[geo.s:geo.seg, :].astype(F32), lg, rev)

            def chunk(ci, carry, d=d, rev=rev, lg=lg):
                r0 = pl.multiple_of((nc - 1 - ci if rev else ci) * RET_CHUNK, RET_CHUNK)
                rows = pl.ds(r0, RET_CHUNK)
                new_state, out = _ret_step(st_ref[...], q_ref[rows, :].astype(F32), k_ref[rows, :].astype(F32),
                                           v_ref[rows, :].astype(F32), lg, rev)
                st_ref[...] = new_state
                if d == 0:
                    o_ref[rows, :] = out
                else:
                    o_ref[rows, :] += out
                return carry

            lax.fori_loop(0, nc, chunk, 0)
        o_ref[geo.s:geo.seg, :] = jnp.zeros((geo.l, RET_V_DIM), F32)

    return pl.pallas_call(
        kern, name=name, grid=(geo.b, RET_HEADS), in_specs=_ret_specs(geo),
        out_specs=pl.BlockSpec((geo.seg, RET_V_DIM), lambda b, h: (b, h)),
        out_shape=jax.ShapeDtypeStruct((geo.r, RET_VWIDTH), F32),
        scratch_shapes=[pltpu.VMEM((RET_QK_DIM, RET_V_DIM), F32)],
        compiler_params=_cparams("parallel", "arbitrary"),
    )(log_g, qkv, qkv, qkv)


def _retention_bwd(geo, qkv, log_g, do, name):
    nc = geo.s // RET_CHUNK
    ctx = slice(geo.s, geo.seg)

    def kern(lg_ref, q_ref, k_ref, v_ref, do_ref, dq_ref, dk_ref, dv_ref, dlg_ref, states_ref, cur_ref, dst_ref):
        b, h = pl.program_id(0), pl.program_id(1)

        @pl.when((b == 0) & (h == 0))
        def _():
            dlg_ref[...] = jnp.zeros_like(dlg_ref)

        for d, rev in ((0, False), (1, True)):
            row = pl.ds(d * RET_HEADS + h, 1)
            lg = lg_ref[row, 0:1]
            kc, vc = k_ref[ctx, :].astype(F32), v_ref[ctx, :].astype(F32)
            cur_ref[...] = _ret_state0(kc, vc, lg, rev)

            def rows_of(ci, rev=rev):
                return pl.ds(pl.multiple_of((nc - 1 - ci if rev else ci) * RET_CHUNK, RET_CHUNK), RET_CHUNK)

            def load(rows):
                return q_ref[rows, :].astype(F32), k_ref[rows, :].astype(F32), v_ref[rows, :].astype(F32)

            def replay(ci, carry, rev=rev, lg=lg, rows_of=rows_of, load=load):
                states_ref[ci] = cur_ref[...]
                cur_ref[...] = _ret_step(cur_ref[...], *load(rows_of(ci)), lg, rev)[0]
                return carry

            lax.fori_loop(0, nc, replay, 0)
            dst_ref[...] = jnp.zeros_like(dst_ref)

            def back(t, dlg, d=d, rev=rev, lg=lg, rows_of=rows_of, load=load):
                ci = nc - 1 - t
                rows = rows_of(ci)
                _, vjp = jax.vjp(lambda st, q, k, v, g: _ret_step(st, q, k, v, g, rev), states_ref[ci], *load(rows), lg)
                dstate, dq, dk, dv, dg = vjp((dst_ref[...], do_ref[rows, :]))
                dst_ref[...] = dstate
                if d == 0:
                    dq_ref[rows, :], dk_ref[rows, :], dv_ref[rows, :] = dq, dk, dv
                else:
                    dq_ref[rows, :] += dq
                    dk_ref[rows, :] += dk
                    dv_ref[rows, :] += dv
                return dlg + dg

            dlg = lax.fori_loop(0, nc, back, jnp.zeros((1, 1), F32))
            _, vjp = jax.vjp(lambda kk, vv, g: _ret_state0(kk, vv, g, rev), kc, vc, lg)
            dkc, dvc, dg = vjp(dst_ref[...])
            if d == 0:
                dk_ref[ctx, :], dv_ref[ctx, :] = dkc, dvc
            else:
                dk_ref[ctx, :] += dkc
                dv_ref[ctx, :] += dvc
            dlg_ref[row, :] += jnp.broadcast_to(dlg + dg, (1, LANES))
        dq_ref[ctx, :] = jnp.zeros((geo.l, RET_QK_DIM), F32)

    nq = RET_HEADS
    return pl.pallas_call(
        kern, name=name, grid=(geo.b, RET_HEADS),
        in_specs=_ret_specs(geo) + [pl.BlockSpec((geo.seg, RET_V_DIM), lambda b, h: (b, h))],
        out_specs=[pl.BlockSpec((geo.seg, RET_QK_DIM), lambda b, h: (b, h)),
                   pl.BlockSpec((geo.seg, RET_QK_DIM), lambda b, h: (b, h)),
                   pl.BlockSpec((geo.seg, RET_V_DIM), lambda b, h: (b, h)),
                   pl.BlockSpec((2 * RET_HEADS, LANES), lambda b, h: (0, 0))],
        out_shape=[jax.ShapeDtypeStruct((geo.r, RET_QK_W), F32), jax.ShapeDtypeStruct((geo.r, RET_QK_W), F32),
                   jax.ShapeDtypeStruct((geo.r, RET_VWIDTH), F32), jax.ShapeDtypeStruct((2 * RET_HEADS, LANES), F32)],
        scratch_shapes=[pltpu.VMEM((nc, RET_QK_DIM, RET_V_DIM), F32), pltpu.VMEM((RET_QK_DIM, RET_V_DIM), F32),
                        pltpu.VMEM((RET_QK_DIM, RET_V_DIM), F32)],
        compiler_params=_cparams("arbitrary", "arbitrary"),
    )(log_g, qkv, qkv, qkv, do)


def _gated(o, g, gain):
    outs = []
    for h in range(RET_HEADS):
        cols = slice(h * RET_V_DIM, (h + 1) * RET_V_DIM)
        oh = o[:, cols]
        mu = jnp.mean(oh, axis=-1, keepdims=True)
        var = jnp.mean(jnp.square(oh - mu), axis=-1, keepdims=True)
        outs.append(_silu(g[:, cols]) * ((oh - mu) * lax.rsqrt(var + EPS) * gain[:, cols]))
    return jnp.concatenate(outs, axis=1)


def _ret_gated(geo, o, proj, gain, name):
    def body(i, ov, gv, gn):
        return _gated(ov, gv, gn)

    gate_block = (2 * RET_QK_W + RET_VWIDTH) // RET_VWIDTH
    return _rowwise(name, body, geo, 128, [(o, "row"), (proj, ("rowc", RET_VWIDTH, gate_block)), (gain, "full")],
                    [("row", RET_VWIDTH, BF16)])


def _ret_gated_bwd(geo, o, proj, gain, dout, name):
    def body(i, ov, gv, gn, dv):
        _, vjp = jax.vjp(_gated, ov, gv, gn)
        return vjp(dv)

    gate_block = (2 * RET_QK_W + RET_VWIDTH) // RET_VWIDTH
    return _rowwise(name, body, geo, 128,
                    [(o, "row"), (proj, ("rowc", RET_VWIDTH, gate_block)), (gain, "full"), (dout, "row")],
                    [("row", RET_VWIDTH, F32), ("row", RET_VWIDTH, F32), ("gacc", 1, RET_VWIDTH)])


def _whole(name, fn, out_shapes, *arrays):
    n = len(arrays)

    def kern(*refs):
        res = fn(*[r[...] for r in refs[:n]])
        for ref, val in zip(refs[n:], res):
            ref[...] = val.astype(ref.dtype)

    return pl.pallas_call(kern, name=name, out_shape=out_shapes)(*arrays)


def _rope_tables(geo, head_dim):
    rows = geo.s // GRID_W
    row = jnp.broadcast_to(jnp.arange(rows, dtype=jnp.int32)[:, None], (rows, GRID_W)).reshape(geo.s)
    col = jnp.broadcast_to(jnp.arange(GRID_W, dtype=jnp.int32)[None, :], (rows, GRID_W)).reshape(geo.s)
    axis_dim = head_dim // 2
    inv = ROPE_BASE ** (-jnp.arange(0, axis_dim, 2, dtype=F32) / axis_dim)
    ang_r = row.astype(F32)[:, None] * inv
    ang_c = col.astype(F32)[:, None] * inv
    cos = jnp.concatenate([jnp.cos(ang_r)] * 2 + [jnp.cos(ang_c)] * 2, axis=1)
    sin = jnp.concatenate([-jnp.sin(ang_r), jnp.sin(ang_r), -jnp.sin(ang_c), jnp.sin(ang_c)], axis=1)
    cos = jnp.concatenate([cos, jnp.ones((geo.l, head_dim), F32)], axis=0)
    sin = jnp.concatenate([sin, jnp.zeros((geo.l, head_dim), F32)], axis=0)
    reps = max(1, LANES // head_dim)
    return jnp.tile(cos, (1, reps)), jnp.tile(sin, (1, reps))


def _row_tile(r):
    return next(t for t in (1024, 512, 256, 128) if r % t == 0)


MOD_ROWS = 8


def _local_step(x, c, ctx, target, sp, wts, plan=None):
    nb, s, d = x.shape
    geo = _Rows(nb, s, ctx.shape[1])
    assert nb + 1 <= MOD_ROWS and d == D_MODEL
    tm = _row_tile(geo.r)
    z = jnp.concatenate([x, ctx], axis=1).reshape(geo.r, d)
    cvec = jnp.concatenate([c, sp["c_ctx"][None, :], jnp.zeros((MOD_ROWS - nb - 1, d), F32)], axis=0)
    cact, = _whole("cond_silu", lambda v: (_silu(v),), [jax.ShapeDtypeStruct(cvec.shape, F32)], cvec)
    cos64, sin64 = _rope_tables(geo, HEAD_DIM)
    cos256, sin256 = _rope_tables(geo, RET_QK_DIM)
    q_gain = jnp.tile(sp["q_norm"].reshape(1, HEAD_DIM), (1, LANES // HEAD_DIM))
    k_gain = jnp.tile(sp["k_norm"].reshape(1, HEAD_DIM), (1, LANES // HEAD_DIM))
    sink = sp["sink"].reshape(N_HEADS)
    log_g = jnp.broadcast_to(sp["log_g"].reshape(2 * RET_HEADS, 1), (2 * RET_HEADS, LANES))
    gn_g = sp["gn_g"].reshape(1, RET_VWIDTH)

    saved = []
    for i in range(2):
        mod = _mm_nn(cact, wts["ada"][i], F32, f"mod{i}", MOD_ROWS, wts["ada"][i].shape[2], d, bias=sp["ada_b"][i][None, :])
        mod3 = mod[:nb + 1, None, :]
        n1, n2 = sp["norm1_g"][i][None, :], sp["norm2_g"][i][None, :]
        h1 = _norm_mod(geo, z, n1, mod3, 0, f"norm1_{i}")
        if i == 0:
            proj = _mm_nn(h1, wts["attn_qkv"], F32, "attn_qkv", tm, wts["attn_qkv"].shape[2], d)
            prep = _attn_prep(geo, proj, cos64, sin64, q_gain, k_gain, "attn_prep")
            o, late = _attention(geo, prep, sink, "attn", plan.gather_job() if plan else None)
            if plan:
                plan.late_weights(late, wts)
            oraw = None
            w_o = wts["attn_o"]
        else:
            proj = _mm_nn(h1, wts["ret_qkvg"], BF16, "ret_qkvg", tm, 512, d)
            prep = _ret_prep(geo, proj, cos256, sin256, "ret_prep")
            oraw = _retention(geo, prep, log_g, "ret")
            o = _ret_gated(geo, oraw, proj, gn_g, "ret_gated")
            w_o = wts["ret_o"]
        zmid, mix = _mm_nn_gate_residual(geo, o, w_o, z, mod3, 2 * d, f"mix_out{i}")
        h2 = _norm_mod(geo, zmid, n2, mod3, 3 * d, f"norm2_{i}")
        u, a = _ffn_in_swiglu(h2, wts["ffn_in"][i], f"ffn_in{i}")
        zout, f = _mm_nn_gate_residual(geo, a, wts["ffn_out"][i], zmid, mod3, 5 * d, f"ffn_out{i}")
        saved.append(dict(z=z, mod3=mod3, n1=n1, n2=n2, h1=h1, proj=proj, prep=prep, o=o, oraw=oraw, mix=mix, zmid=zmid,
                          h2=h2, u=u, a=a, f=f))
        z = zout

    dz, loss = _loss_head(geo, z, target.reshape(nb * s, d), "loss")

    big, small = {}, {}
    dmods = [None, None]
    for i in (1, 0):
        sv = saved[i]
        mod3 = sv["mod3"]
        df, dg2 = _gate_residual_bwd(geo, dz, sv["f"], mod3, 5 * d, f"res_ffn_bwd{i}")
        carry = plan is not None and i == 0
        du, land = _ffn_out_dx_swiglu_bwd(df, wts["ffn_out"][i], sv["u"], f"ffn_out_dx{i}", plan.layer1.swap_job() if carry else None)
        if carry:
            plan.layer1.after_swap(land)
        big[f"ffn_out{i}"] = _mm_tn(sv["a"], df, f"ffn_out_dw{i}", D_FF // 2, 1024, tm).reshape(N_CHIPS, D_FF // N_CHIPS, d)
        n4 = wts["ffn_in"][i].shape[2]
        dh2 = _mm_nt(du, wts["ffn_in"][i], BF16, f"ffn_in_dx{i}", tm, 1024, n4)
        big[f"ffn_in{i}"] = _mm_tn(sv["h2"], du, f"ffn_in_dw{i}", 1024, n4, tm, shards=N_CHIPS)
        dzmid, dsh2, dsc2, dn2 = _norm_mod_bwd(geo, sv["zmid"], sv["n2"], mod3, 3 * d, dh2, dz, f"norm2_bwd{i}")
        dmix, dg1 = _gate_residual_bwd(geo, dzmid, sv["mix"], mod3, 2 * d, f"res_mix_bwd{i}")
        if i == 0:
            do = _mm_nt(dmix, wts["attn_o"], BF16, "attn_out_dx", tm, 1024, 1024)
            big["attn_o"] = _mm_tn(sv["o"], dmix, "attn_out_dw", 1024, 1024, tm).reshape(N_CHIPS, 1024 // N_CHIPS, d)
            dq, dkv, dsink, land = _attention_bwd(geo, sv["prep"], sink, do, "attn_bwd", plan.layer1.exchange_job() if plan else None)
            if plan:
                plan.layer1_reduced = plan.layer1.after_exchange(land)
            dproj, dqg, dkg = _attn_prep_bwd(geo, sv["proj"], cos64, sin64, q_gain, k_gain, dq, dkv, "attn_prep_bwd")
            small["q_norm"] = dqg[0, :HEAD_DIM] + dqg[0, HEAD_DIM:]
            small["k_norm"] = dkg[0, :HEAD_DIM] + dkg[0, HEAD_DIM:]
            small["sink"] = dsink[:, 0]
            wq = wts["attn_qkv"]
            dh1 = _mm_nt(dproj, wq, BF16, "attn_qkv_dx", tm, 1024, wq.shape[2])
            big["attn_qkv"] = _mm_tn(sv["h1"], dproj, "attn_qkv_dw", 1024, wq.shape[2], tm, shards=N_CHIPS)
        else:
            do = _mm_nt(dmix, wts["ret_o"], BF16, "ret_out_dx", tm, 1024, 1024)
            big["ret_o"] = _mm_tn(sv["o"], dmix, "ret_out_dw", 1024, 1024, tm).reshape(N_CHIPS, RET_VWIDTH // N_CHIPS, d)
            doraw, dgate, dgn = _ret_gated_bwd(geo, sv["oraw"], sv["proj"], gn_g, do, "ret_gated_bwd")
            small["gn_g"] = dgn[0]
            dq, dk, dv, dlg = _retention_bwd(geo, sv["prep"], log_g, doraw, "ret_bwd")
            small["log_g"] = dlg[:, 0].reshape(2, RET_HEADS)
            dproj = _ret_prep_bwd(geo, dq, dk, dv, dgate, cos256, sin256, "ret_prep_bwd")
            wq = wts["ret_qkvg"]
            dh1 = _mm_nt(dproj, wq, BF16, "ret_qkvg_dx", tm, 1024, 512)
            big["ret_qkvg"] = _mm_tn(sv["h1"], dproj, "ret_qkvg_dw", 1024, 512, tm, shards=N_CHIPS)
        dz, dsh1, dsc1, dn1 = _norm_mod_bwd(geo, sv["z"], sv["n1"], mod3, 0, dh1, dzmid, f"norm1_bwd{i}")
        small[f"norm1_g{i}"], small[f"norm2_g{i}"] = dn1[0], dn2[0]
        parts = [dsh1, dsc1, dg1, dsh2, dsc2, dg2]
        rows = jnp.concatenate([jnp.concatenate([p[:nb, 0, :] for p in parts], axis=1),
                                jnp.concatenate([jnp.sum(p[nb:, 0, :], axis=0, keepdims=True) for p in parts], axis=1),
                                jnp.zeros((MOD_ROWS - nb - 1, 6 * d), F32)], axis=0)
        dmods[i] = rows
        small[f"ada_b{i}"] = jnp.sum(rows, axis=0)
        big[f"ada{i}"] = _mm_tn(cact, rows, f"ada_dw{i}", 1024, wts["ada"][i].shape[2], MOD_ROWS, shards=N_CHIPS)
        if plan and i == 1:
            plan.start_layer1(big)

    dcact = [_mm_nt(dmods[i], wts["ada"][i], F32, f"ada_dx{i}", MOD_ROWS, 1024, wts["ada"][i].shape[2]) for i in range(2)]

    def silu_bwd(v, d0, d1):
        sg = _sigmoid(v)
        return ((d0 + d1) * (sg * (1.0 + v * (1.0 - sg))),)

    dcvec, = _whole("cond_silu_bwd", silu_bwd, [jax.ShapeDtypeStruct(cvec.shape, F32)], cvec, dcact[0], dcact[1])
    small["c_ctx"] = dcvec[nb]
    return loss, dz, big, small


def _adamw(w, g, m, v, name):
    rows, cols = w.shape
    tr = next((t for t in (256, 128, 64, 32, 16, 8) if rows % t == 0), rows)
    c1 = 1.0 - ADAM_B1 ** ADAM_STEP
    c2 = 1.0 - ADAM_B2 ** ADAM_STEP

    def kern(w_ref, g_ref, m_ref, v_ref, d_ref, nm_ref, nv_ref):
        gv = g_ref[...]
        nm = ADAM_B1 * m_ref[...] + (1.0 - ADAM_B1) * gv
        nv = ADAM_B2 * v_ref[...] + (1.0 - ADAM_B2) * jnp.square(gv)
        d_ref[...] = -ADAM_LR * ((nm / c1) / (jnp.sqrt(nv / c2) + ADAM_EPS) + ADAM_WD * w_ref[...])
        nm_ref[...] = nm
        nv_ref[...] = nv

    spec = pl.BlockSpec((tr, cols), lambda i: (i, 0))
    return pl.pallas_call(
        kern, name=name, grid=(rows // tr,), in_specs=[spec] * 4, out_specs=[spec] * 3,
        out_shape=[jax.ShapeDtypeStruct(w.shape, F32)] * 3, compiler_params=_cparams("parallel"),
    )(w, g, m, v)


N_DEVICES = 8


def _mesh_pos():
    return lax.axis_index("x"), lax.axis_index("y"), lax.axis_index("c")


def _other_chips(x, y):
    return [(1 - x, y), (x, 1 - y), (1 - x, 1 - y)]


def _hbm(n):
    return [pl.BlockSpec(memory_space=pl.ANY)] * n


def _remote(src, dst, send_sem, recv_sem, device):
    return pltpu.make_async_remote_copy(src_ref=src, dst_ref=dst, send_sem=send_sem, recv_sem=recv_sem,
                                        device_id=device, device_id_type=MESH)


def _scalar_spec(grid, in_specs, out_specs):
    return pltpu.PrefetchScalarGridSpec(num_scalar_prefetch=1, grid=grid, in_specs=in_specs, out_specs=out_specs)


def _place_shard(shard, pos, name):
    r, cols = shard.shape
    tr = _slab_tile(r)

    def kern(pos_ref, s_ref, o_ref):
        o_ref[...] = s_ref[...].astype(BF16)

    return pl.pallas_call(
        kern, name=name, out_shape=jax.ShapeDtypeStruct((N_CHIPS, r, cols), BF16),
        grid_spec=_scalar_spec((r // tr,), [pl.BlockSpec((tr, cols), lambda i, p: (i, 0))],
                               pl.BlockSpec((None, tr, cols), lambda i, p: (p[1], i, 0))),
        compiler_params=_cparams("parallel"),
    )(pos, shard)


class _CommJob:
    def __init__(self, inputs, out_shapes, aliases, sem_shapes, stages):
        self.inputs, self.out_shapes, self.aliases, self.sem_shapes, self.stages = inputs, out_shapes, aliases, sem_shapes, stages


def _run_job(job, name):
    n_in, n_out = len(job.inputs), len(job.out_shapes)

    def body(*refs):
        for stage in job.stages:
            stage(refs[:n_in], refs[n_in:n_in + n_out], refs[n_in + n_out:])

    return pl.pallas_call(
        body, name=name, in_specs=_hbm(n_in), out_specs=_hbm(n_out), out_shape=job.out_shapes,
        input_output_aliases=job.aliases, scratch_shapes=job.sem_shapes,
    )(*job.inputs)


def _job_marks(job, steps):
    return {2: [0, steps - 1], 3: [0, (5 * steps) // 8, steps - 1]}[len(job.stages)]


def _gather_job(placed):
    n = len(placed)

    def half(w, which):
        r2 = placed[w].shape[1] // 2
        return pl.ds(which * r2, r2)

    def ici_copies(outs, sems, slot_of):
        x, y, c = _mesh_pos()
        res = []
        for w in range(n):
            for k, (px, py) in enumerate(_other_chips(x, y)):
                slab = outs[w].at[slot_of(x, y, px, py), half(w, c)]
                res.append((slab, _remote(slab, slab, sems[0].at[w, k], sems[1].at[w, k], (px, py, c))))
        return res

    def forwards(outs, sems, which_core):
        x, y, c = _mesh_pos()
        res = []
        for w in range(n):
            for k, (px, py) in enumerate(_other_chips(x, y)):
                slab = outs[w].at[2 * px + py, half(w, which_core(c))]
                res.append(_remote(slab, slab, sems[2].at[w, k], sems[3].at[w, k], (x, y, 1 - c)))
        return res

    def send(ins, outs, sems):
        for _, cp in ici_copies(outs, sems, lambda x, y, px, py: 2 * x + y):
            cp.start()

    def forward(ins, outs, sems):
        arrivals = ici_copies(outs, sems, lambda x, y, px, py: 2 * px + py)
        for (_, arrival), fwd in zip(arrivals, forwards(outs, sems, lambda c: c)):
            arrival.wait_recv()
            fwd.start()

    def finish(ins, outs, sems):
        for cp in forwards(outs, sems, lambda c: 1 - c):
            cp.wait_recv()
        for _, cp in ici_copies(outs, sems, lambda x, y, px, py: 2 * x + y):
            cp.wait_send()
        for cp in forwards(outs, sems, lambda c: c):
            cp.wait_send()

    return _CommJob(list(placed), [jax.ShapeDtypeStruct(p.shape, p.dtype) for p in placed], {w: w for w in range(n)},
                    [pltpu.SemaphoreType.DMA((n, 3))] * 4, [send, forward, finish])


def _pair_swap_job(grads):
    n = len(grads)

    def copies(ins, outs, sems):
        x, y, c = _mesh_pos()
        res = []
        for w in range(n):
            r2 = grads[w].shape[1] // 2
            res.append(_remote(ins[w].at[:, pl.ds((1 - c) * r2, r2)], outs[w], sems[0].at[w], sems[1].at[w], (x, y, 1 - c)))
        return res

    def send(ins, outs, sems):
        for cp in copies(ins, outs, sems):
            cp.start()

    def finish(ins, outs, sems):
        for cp in copies(ins, outs, sems):
            cp.wait()

    return _CommJob(list(grads), [jax.ShapeDtypeStruct((N_CHIPS, g.shape[1] // 2, g.shape[2]), F32) for g in grads], {},
                    [pltpu.SemaphoreType.DMA((n,))] * 2, [send, finish])


def _chip_exchange_job(hs):
    n = len(hs)

    def send(ins, outs, sems):
        x, y, c = _mesh_pos()
        for w in range(n):
            for k, (px, py) in enumerate(_other_chips(x, y)):
                _remote(ins[w].at[2 * px + py], outs[w].at[2 * x + y], sems[0].at[w, k], sems[1].at[w, k], (px, py, c)).start()

    def finish(ins, outs, sems):
        x, y, c = _mesh_pos()
        for w in range(n):
            for k, (px, py) in enumerate(_other_chips(x, y)):
                got = outs[w].at[2 * px + py]
                cp = _remote(ins[w].at[2 * px + py], got, sems[0].at[w, k], sems[1].at[w, k], (px, py, c))
                cp.wait_recv()
                cp.wait_send()

    return _CommJob(list(hs), [jax.ShapeDtypeStruct(h.shape, h.dtype) for h in hs], {},
                    [pltpu.SemaphoreType.DMA((n, 3))] * 2, [send, finish])


def _pair_share(ts, name):
    n = len(ts)

    def body(*refs):
        outs = refs[n:2 * n]
        send_sems, recv_sems = refs[2 * n:]
        x, y, c = _mesh_pos()
        sends = []
        for w in range(n):
            r2 = ts[w].shape[0] // 2
            mine = outs[w].at[pl.ds(c * r2, r2)]
            rc = _remote(mine, mine, send_sems.at[w], recv_sems.at[w], (x, y, 1 - c))
            rc.start()
            sends.append(rc)
        for w in range(n):
            r2 = ts[w].shape[0] // 2
            theirs = outs[w].at[pl.ds((1 - c) * r2, r2)]
            _remote(theirs, theirs, send_sems.at[w], recv_sems.at[w], (x, y, 1 - c)).wait_recv()
            sends[w].wait_send()

    return pl.pallas_call(
        body, name=name, in_specs=_hbm(n), out_specs=_hbm(n),
        out_shape=[jax.ShapeDtypeStruct(t.shape, F32) for t in ts],
        input_output_aliases={w: w for w in range(n)},
        scratch_shapes=[pltpu.SemaphoreType.DMA((n,))] * 2,
    )(*ts)


def _slab_tile(rows):
    return next(t for t in (512, 256, 176, 128, 64, 32, 16) if rows % t == 0)


def _sum_pair(grad, land, pos, name):
    _, r2, cols = land.shape
    tr = _slab_tile(r2)
    nt = r2 // tr

    def kern(pos_ref, a_ref, b_ref, o_ref):
        o_ref[...] = (a_ref[...] + b_ref[...]).astype(BF16)

    spec = pl.BlockSpec((None, tr, cols), lambda j, i, p: (j, i, 0))
    return pl.pallas_call(
        kern, name=name, out_shape=jax.ShapeDtypeStruct(land.shape, BF16),
        grid_spec=_scalar_spec((N_CHIPS, nt), [pl.BlockSpec((None, tr, cols), lambda j, i, p: (j, p[0] * nt + i, 0)), spec], spec),
        compiler_params=_cparams("parallel", "parallel"),
    )(pos, grad, land)


def _sum_chips(hs, land, pos, name):
    _, r2, cols = land.shape
    tr = _slab_tile(r2)
    nt = r2 // tr

    def kern(pos_ref, h_ref, l_ref, o_ref):
        acc = jnp.zeros((tr, cols), F32)
        own = h_ref[...].astype(F32)
        for k in range(N_CHIPS):
            acc = acc + jnp.where(pos_ref[1] == k, own, l_ref[k].astype(F32))
        o_ref[...] = acc

    return pl.pallas_call(
        kern, name=name, out_shape=jax.ShapeDtypeStruct((2 * r2, cols), F32),
        grid_spec=_scalar_spec((nt,), [pl.BlockSpec((None, tr, cols), lambda i, p: (p[1], i, 0)),
                                       pl.BlockSpec((N_CHIPS, tr, cols), lambda i, p: (0, i, 0))],
                               pl.BlockSpec((tr, cols), lambda i, p: (p[0] * nt + i, 0))),
        compiler_params=_cparams("parallel"),
    )(pos, hs, land)


class _ReduceScatter:
    def __init__(self, grads, pos, tag):
        self.grads, self.pos, self.tag = list(grads), pos, tag

    def swap_job(self):
        return _pair_swap_job(self.grads)

    def after_swap(self, land):
        self.hs = [_sum_pair(g, l, self.pos, f"grads_pair_sum_{self.tag}{w}") for w, (g, l) in enumerate(zip(self.grads, land))]

    def exchange_job(self):
        return _chip_exchange_job(self.hs)

    def after_exchange(self, land2):
        ts = [_sum_chips(h, l, self.pos, f"grads_chip_sum_{self.tag}{w}") for w, (h, l) in enumerate(zip(self.hs, land2))]
        return _pair_share(ts, f"grads_pair_share_{self.tag}")

    def run(self):
        self.after_swap(_run_job(self.swap_job(), f"grads_pair_swap_{self.tag}"))
        return self.after_exchange(_run_job(self.exchange_job(), f"grads_chip_exchange_{self.tag}"))


EARLY_WEIGHTS = ("ada0", "attn_qkv")
LATE_WEIGHTS = ("ada1", "ffn_in0", "ffn_in1", "ffn_out0", "ffn_out1", "attn_o", "ret_qkvg", "ret_o")
LAYER1_GRADS = ("ffn_out1", "ffn_in1", "ret_o", "ret_qkvg", "ada1")
LAYER0_GRADS = ("ffn_out0", "ffn_in0", "attn_o", "attn_qkv", "ada0")


def _fill_weights(wts, full):
    for name, w in full.items():
        if name[:-1] in ("ada", "ffn_in"):
            wts[name[:-1]][int(name[-1])] = w
        elif name[:-1] == "ffn_out":
            wts["ffn_out"][int(name[-1])] = w.reshape(-1, w.shape[2])
        elif name in ("attn_o", "ret_o"):
            wts[name] = w.reshape(-1, w.shape[2])
        else:
            wts[name] = w


class _StepPlan:
    def __init__(self, placed, pos):
        self.placed, self.pos = placed, pos
        self.layer1 = None
        self.layer1_reduced = None

    def gather_job(self):
        return _gather_job([self.placed[k] for k in LATE_WEIGHTS])

    def late_weights(self, outs, wts):
        _fill_weights(wts, dict(zip(LATE_WEIGHTS, outs)))

    def start_layer1(self, big):
        self.layer1 = _ReduceScatter([big[k] for k in LAYER1_GRADS], self.pos, "l1_")


def _all_reduce_small(v, name):
    def body(v_ref, o_ref, land_ref, send_sems, recv_sems):
        x, y, c = _mesh_pos()
        me = 4 * x + 2 * y + c
        land_ref[me] = v_ref[...]
        for t in range(N_DEVICES):
            @pl.when(t != me)
            def _(t=t):
                _remote(v_ref, land_ref.at[me], send_sems.at[t], recv_sems.at[me], (t // 4, (t // 2) % 2, t % 2)).start()
        for t in range(N_DEVICES):
            @pl.when(t != me)
            def _(t=t):
                _remote(v_ref, land_ref.at[t], send_sems.at[t], recv_sems.at[t], (t // 4, (t // 2) % 2, t % 2)).wait()
        acc = land_ref[0]
        for t in range(1, N_DEVICES):
            acc = acc + land_ref[t]
        o_ref[...] = acc

    vmem = pl.BlockSpec(memory_space=pltpu.VMEM)
    return pl.pallas_call(
        body, name=name, in_specs=[vmem], out_specs=vmem, out_shape=jax.ShapeDtypeStruct(v.shape, F32),
        scratch_shapes=[pltpu.VMEM((N_DEVICES,) + v.shape, F32), pltpu.SemaphoreType.DMA((N_DEVICES,)),
                        pltpu.SemaphoreType.DMA((N_DEVICES,))],
    )(v)


SMALL_ROWS = 24


def _pack_small(small, dlogit):
    d = D_MODEL
    misc = jnp.zeros((d,), F32)
    misc = misc.at[0:HEAD_DIM].set(small["q_norm"]).at[128:128 + HEAD_DIM].set(small["k_norm"])
    misc = misc.at[256:256 + N_HEADS].set(small["sink"]).at[384:384 + 2 * RET_HEADS].set(dlogit.reshape(-1))
    rows = [small["ada_b0"].reshape(6, d), small["ada_b1"].reshape(6, d), small["norm1_g0"][None], small["norm1_g1"][None],
            small["norm2_g0"][None], small["norm2_g1"][None], small["c_ctx"][None], small["gn_g"].reshape(2, d), misc[None]]
    buf = jnp.concatenate(rows, axis=0)
    return jnp.concatenate([buf, jnp.zeros((SMALL_ROWS - buf.shape[0], d), F32)], axis=0)


def _unpack_small(buf):
    d = D_MODEL
    misc = buf[19]
    return dict(ada_b=buf[0:12].reshape(2, 6 * d), norm1_g=buf[12:14], norm2_g=buf[14:16], c_ctx=buf[16],
                gn_g=buf[17:19].reshape(2 * d), q_norm=misc[0:HEAD_DIM], k_norm=misc[128:128 + HEAD_DIM],
                sink=misc[256:256 + N_HEADS], decay=misc[384:384 + 2 * RET_HEADS])


def kernel(x, c, ctx, c_ctx, ada_w, ada_b, norm1_g, norm2_g, ffn_w_in, ffn_w_out, attn_w_qkv, attn_q_norm, attn_k_norm, attn_sink, attn_w_o, ret_w_qkvg, ret_decay_logit, ret_gn_g, ret_w_o, loss_target, m_c_ctx, m_ada_w, m_ada_b, m_norm1_g, m_norm2_g, m_ffn_w_in, m_ffn_w_out, m_attn_w_qkv, m_attn_q_norm, m_attn_k_norm, m_attn_sink, m_attn_w_o, m_ret_w_qkvg, m_ret_decay_logit, m_ret_gn_g, m_ret_w_o, v_c_ctx, v_ada_w, v_ada_b, v_norm1_g, v_norm2_g, v_ffn_w_in, v_ffn_w_out, v_attn_w_qkv, v_attn_q_norm, v_attn_k_norm, v_attn_sink, v_attn_w_o, v_ret_w_qkvg, v_ret_decay_logit, v_ret_gn_g, v_ret_w_o):
    xi, yi, ci = _mesh_pos()
    chip = 2 * xi + yi
    nb, s, d = x.shape
    gn_shard = ret_gn_g.shape[1]

    shards = dict(ada0=ada_w[0], ada1=ada_w[1], ffn_in0=ffn_w_in[0], ffn_in1=ffn_w_in[1], ffn_out0=ffn_w_out[0],
                  ffn_out1=ffn_w_out[1], attn_qkv=attn_w_qkv[0], attn_o=attn_w_o[0], ret_qkvg=ret_w_qkvg[0], ret_o=ret_w_o[0])
    names = list(shards)
    pos = jnp.stack([ci, chip]).astype(jnp.int32)
    placed = {k: _place_shard(shards[k], pos, f"place_{k}") for k in names}
    early = _run_job(_gather_job([placed[k] for k in EARLY_WEIGHTS]), "gather_early_weights")
    gn_mine = jnp.where(ci == 0, ret_gn_g[0], jnp.zeros_like(ret_gn_g[0]))
    gn_place = lax.dynamic_update_slice(jnp.zeros((RET_VWIDTH,), F32), gn_mine, (chip * gn_shard,))
    gn_full = _all_reduce_small(gn_place.reshape(2, d), "gather_gn_gain").reshape(RET_VWIDTH)

    wts = dict(ada=[None, None], ffn_in=[None, None], ffn_out=[None, None], attn_qkv=None, attn_o=None, ret_qkvg=None, ret_o=None)
    _fill_weights(wts, dict(zip(EARLY_WEIGHTS, early)))
    plan = _StepPlan(placed, pos)
    decay_logit = ret_decay_logit[0]
    sp = dict(c_ctx=c_ctx, ada_b=ada_b, norm1_g=norm1_g, norm2_g=norm2_g, q_norm=attn_q_norm[0], k_norm=attn_k_norm[0],
              sink=attn_sink[0], log_g=jax.nn.log_sigmoid(decay_logit), gn_g=gn_full)
    loss_part, dz, big, small = _local_step(x, c, ctx, loss_target, sp, wts, plan)

    loss = lax.psum(loss_part[0, 0], ("x", "y", "c"))
    grad_x = dz.reshape(nb, -1, d)[:, :s]

    dlogit = small["log_g"] * jax.nn.sigmoid(-decay_logit)
    sg = _unpack_small(_all_reduce_small(_pack_small(small, dlogit), "reduce_small_grads"))
    reduced = dict(zip(LAYER1_GRADS, plan.layer1_reduced))
    reduced.update(zip(LAYER0_GRADS, _ReduceScatter([big[k] for k in LAYER0_GRADS], pos, "l0_").run()))

    grads = dict(
        c_ctx=sg["c_ctx"], ada_w=jnp.stack([reduced["ada0"], reduced["ada1"]]), ada_b=sg["ada_b"], norm1_g=sg["norm1_g"],
        norm2_g=sg["norm2_g"], ffn_w_in=jnp.stack([reduced["ffn_in0"], reduced["ffn_in1"]]),
        ffn_w_out=jnp.stack([reduced["ffn_out0"], reduced["ffn_out1"]]), attn_w_qkv=reduced["attn_qkv"][None],
        attn_q_norm=sg["q_norm"][None], attn_k_norm=sg["k_norm"][None], attn_sink=sg["sink"][None],
        attn_w_o=reduced["attn_o"][None], ret_w_qkvg=reduced["ret_qkvg"][None], ret_decay_logit=sg["decay"].reshape(1, 2, RET_HEADS),
        ret_gn_g=lax.dynamic_slice(sg["gn_g"], (chip * gn_shard,), (gn_shard,))[None], ret_w_o=reduced["ret_o"][None])
    params = dict(c_ctx=(c_ctx, m_c_ctx, v_c_ctx), ada_w=(ada_w, m_ada_w, v_ada_w), ada_b=(ada_b, m_ada_b, v_ada_b),
                  norm1_g=(norm1_g, m_norm1_g, v_norm1_g), norm2_g=(norm2_g, m_norm2_g, v_norm2_g),
                  ffn_w_in=(ffn_w_in, m_ffn_w_in, v_ffn_w_in), ffn_w_out=(ffn_w_out, m_ffn_w_out, v_ffn_w_out),
                  attn_w_qkv=(attn_w_qkv, m_attn_w_qkv, v_attn_w_qkv), attn_q_norm=(attn_q_norm, m_attn_q_norm, v_attn_q_norm),
                  attn_k_norm=(attn_k_norm, m_attn_k_norm, v_attn_k_norm), attn_sink=(attn_sink, m_attn_sink, v_attn_sink),
                  attn_w_o=(attn_w_o, m_attn_w_o, v_attn_w_o), ret_w_qkvg=(ret_w_qkvg, m_ret_w_qkvg, v_ret_w_qkvg),
                  ret_decay_logit=(ret_decay_logit, m_ret_decay_logit, v_ret_decay_logit),
                  ret_gn_g=(ret_gn_g, m_ret_gn_g, v_ret_gn_g), ret_w_o=(ret_w_o, m_ret_w_o, v_ret_w_o))
    order = list(params)
    deltas, new_m, new_v = [], [], []
    for k in order:
        w, m, v = params[k]
        g = grads[k].reshape(w.shape)
        grads[k] = g
        flat = (-1, w.shape[-1]) if w.ndim > 1 else (1, -1)
        if k == "ret_decay_logit":
            flat = (1, -1)
        dw, nm, nv = _adamw(w.reshape(flat), g.reshape(flat), m.reshape(flat), v.reshape(flat), f"adamw_{k}")
        deltas.append(dw.reshape(w.shape))
        new_m.append(nm.reshape(w.shape))
        new_v.append(nv.reshape(w.shape))
    return (loss, grad_x, *[grads[k] for k in order], *deltas, *new_m, *new_v)
```

```python
import functools

import jax
import jax.numpy as jnp
from jax import lax
from jax.experimental import pallas as pl
from jax.experimental.pallas import tpu as pltpu

F32 = jnp.float32
BF16 = jnp.bfloat16

D_MODEL = 1024
N_HEADS = 16
N_KV_HEADS = 4
HEAD_DIM = 64
WINDOW = 128
ATTN_BLOCK = 128
BAND = ATTN_BLOCK + 2 * WINDOW
RET_HEADS = 4
RET_QK_DIM = 256
RET_V_DIM = 512
RET_VWIDTH = 2048
RET_CHUNK = 128
D_FF = 2816
GRID_W = 64
ROPE_BASE = 10000.0
EPS = 1e-6
NEG_INF = -1e30
LANES = 128

ADAM_LR = 0.001
ADAM_B1 = 0.9
ADAM_B2 = 0.999
ADAM_EPS = 1e-08
ADAM_WD = 0.01
ADAM_STEP = 10

VMEM_LIMIT_BYTES = 56 * 1024 * 1024
MESH = pl.DeviceIdType.MESH
N_CHIPS = 4


def _cparams(*sem):
    return pltpu.CompilerParams(dimension_semantics=sem, vmem_limit_bytes=VMEM_LIMIT_BYTES)


_DIMS = {"nn": ((1,), (0,)), "nt": ((1,), (1,)), "tn": ((0,), (0,))}


def _dot(a, b, form):
    return lax.dot_general(a.astype(BF16), b.astype(BF16), (_DIMS[form], ((), ())), preferred_element_type=F32)


@functools.partial(jax.custom_vjp, nondiff_argnums=(2,))
def _mm(a, b, form):
    return _dot(a, b, form)


def _mm_fwd(a, b, form):
    return _dot(a, b, form), (a, b)


def _mm_bwd(form, res, ct):
    a, b = res
    if form == "nn":
        da, db = _dot(ct, b, "nt"), _dot(a, ct, "tn")
    elif form == "nt":
        da, db = _dot(ct, b, "nn"), _dot(ct, a, "tn")
    else:
        da, db = _dot(b, ct, "nt"), _dot(a, ct, "nn")
    return da.astype(a.dtype), db.astype(b.dtype)


_mm.defvjp(_mm_fwd, _mm_bwd)


def _swap_halves(x, half):
    w = x.shape[-1]
    lane = lax.broadcasted_iota(jnp.int32, x.shape, x.ndim - 1)
    return jnp.where(lane % (2 * half) < half, pltpu.roll(x, w - half, x.ndim - 1), pltpu.roll(x, half, x.ndim - 1))


@functools.partial(jax.custom_vjp, nondiff_argnums=(1,))
def _rot(x, half):
    return _swap_halves(x, half)


def _rot_fwd(x, half):
    return _swap_halves(x, half), None


def _rot_bwd(half, _, ct):
    return (_swap_halves(ct, half),)


_rot.defvjp(_rot_fwd, _rot_bwd)


def _rope(x, cos, sin_signed, half):
    return x * cos + _rot(x, half) * sin_signed


def _head_mean_square(x):
    r = lax.broadcasted_iota(jnp.int32, (LANES, LANES), 0) // HEAD_DIM
    c = lax.broadcasted_iota(jnp.int32, (LANES, LANES), 1) // HEAD_DIM
    g = jnp.where(r == c, 1.0 / HEAD_DIM, 0.0).astype(F32)
    return jnp.dot(x * x, g, precision=lax.Precision.HIGHEST, preferred_element_type=F32)


def _qk_chunk(x, gain, cos, sin_signed, scale):
    y = x * lax.rsqrt(_head_mean_square(x) + EPS) * gain
    return _rope(y, cos, sin_signed, HEAD_DIM // 4) * scale


def _sigmoid(x):
    return 1.0 / (1.0 + jnp.exp(-x))


def _silu(x):
    return x * _sigmoid(x)


def _mm_nn(a, w, out_dtype, name, tm, tn, tk, bias=None):
    m, k_dim = a.shape
    if w.ndim == 3:
        n = w.shape[0] * w.shape[2]
        per = w.shape[2] // tn
        assert w.shape[2] % tn == 0
        w_spec = pl.BlockSpec((None, tk, tn), lambda i, j, k: (j // per, k, j % per))
    else:
        n = w.shape[1]
        w_spec = pl.BlockSpec((tk, tn), lambda i, j, k: (k, j))
    assert m % tm == 0 and n % tn == 0 and k_dim % tk == 0, (name, a.shape, w.shape, tm, tn, tk)
    nk = k_dim // tk
    has_bias = bias is not None

    def body(*refs):
        a_ref, w_ref = refs[0], refs[1]
        b_ref = refs[2] if has_bias else None
        o_ref, acc_ref = (refs[-1], None) if nk == 1 else (refs[-2], refs[-1])
        if nk == 1:
            part = jnp.dot(a_ref[...].astype(BF16), w_ref[...], preferred_element_type=F32)
            o_ref[...] = (part + b_ref[...] if has_bias else part).astype(out_dtype)
            return
        k = pl.program_id(2)

        @pl.when(k == 0)
        def _():
            acc_ref[...] = jnp.zeros_like(acc_ref)

        acc_ref[...] += jnp.dot(a_ref[...].astype(BF16), w_ref[...], preferred_element_type=F32)

        @pl.when(k == nk - 1)
        def _():
            r = acc_ref[...]
            if has_bias:
                r = r + b_ref[...]
            o_ref[...] = r.astype(out_dtype)

    in_specs = [pl.BlockSpec((tm, tk), lambda i, j, k: (i, k)), w_spec]
    args = [a, w]
    if has_bias:
        in_specs.append(pl.BlockSpec((1, tn), lambda i, j, k: (0, j)))
        args.append(bias)
    return pl.pallas_call(
        body, name=name, grid=(m // tm, n // tn, nk), in_specs=in_specs,
        out_specs=pl.BlockSpec((tm, tn), lambda i, j, k: (i, j)),
        out_shape=jax.ShapeDtypeStruct((m, n), out_dtype),
        scratch_shapes=[pltpu.VMEM((tm, tn), F32)] if nk > 1 else [],
        compiler_params=_cparams("parallel", "parallel", "arbitrary"),
    )(*args)


def _mm_nt(a, w, out_dtype, name, tm, tn, tk):
    if a.ndim == 3:
        planes, m, plane_w = a.shape
        c_dim = planes * plane_w
        a_per = plane_w // tk
        assert plane_w % tk == 0
        a_spec = pl.BlockSpec((None, tm, tk), lambda i, j, k: (k // a_per, i, k % a_per))
    else:
        m, c_dim = a.shape
        a_spec = pl.BlockSpec((tm, tk), lambda i, j, k: (i, k))
    if w.ndim == 3:
        k_out = w.shape[1]
        per = w.shape[2] // tk
        assert w.shape[2] % tk == 0 and w.shape[0] * w.shape[2] == c_dim
        w_spec = pl.BlockSpec((None, tn, tk), lambda i, j, k: (k // per, j, k % per))
    else:
        k_out = w.shape[0]
        assert w.shape[1] == c_dim
        w_spec = pl.BlockSpec((tn, tk), lambda i, j, k: (j, k))
    assert m % tm == 0 and k_out % tn == 0 and c_dim % tk == 0, (name, a.shape, w.shape, tm, tn, tk)
    nk = c_dim // tk

    def body(a_ref, w_ref, o_ref, acc_ref=None):
        if nk == 1:
            o_ref[...] = _dot(a_ref[...], w_ref[...], "nt").astype(out_dtype)
            return
        k = pl.program_id(2)

        @pl.when(k == 0)
        def _():
            acc_ref[...] = jnp.zeros_like(acc_ref)

        acc_ref[...] += _dot(a_ref[...], w_ref[...], "nt")

        @pl.when(k == nk - 1)
        def _():
            o_ref[...] = acc_ref[...].astype(out_dtype)

    return pl.pallas_call(
        body, name=name, grid=(m // tm, k_out // tn, nk),
        in_specs=[a_spec, w_spec],
        out_specs=pl.BlockSpec((tm, tn), lambda i, j, k: (i, j)),
        out_shape=jax.ShapeDtypeStruct((m, k_out), out_dtype),
        scratch_shapes=[pltpu.VMEM((tm, tn), F32)] if nk > 1 else [],
        compiler_params=_cparams("parallel", "parallel", "arbitrary"),
    )(a, w)


def _mm_tn(a, b, name, tm, tn, tk, shards=None):
    r, k_dim = a.shape
    if b.ndim == 3:
        n = b.shape[0] * b.shape[2]
        b_per = b.shape[2] // tn
        assert b.shape[2] % tn == 0
        b_spec = pl.BlockSpec((None, tk, tn), lambda i, j, k: (j // b_per, k, j % b_per))
    else:
        n = b.shape[1]
        b_spec = pl.BlockSpec((tk, tn), lambda i, j, k: (k, j))
    assert r % tk == 0 and k_dim % tm == 0 and n % tn == 0, (name, a.shape, b.shape, tm, tn, tk)
    nk = r // tk
    if shards:
        per = n // shards // tn
        assert n % (shards * tn) == 0
        out_shape = jax.ShapeDtypeStruct((shards, k_dim, n // shards), F32)
        out_spec = pl.BlockSpec((None, tm, tn), lambda i, j, k: (j // per, i, j % per))
    else:
        out_shape = jax.ShapeDtypeStruct((k_dim, n), F32)
        out_spec = pl.BlockSpec((tm, tn), lambda i, j, k: (i, j))

    def body(a_ref, b_ref, o_ref):
        k = pl.program_id(2)

        @pl.when(k == 0)
        def _():
            o_ref[...] = jnp.zeros_like(o_ref)

        o_ref[...] += _dot(a_ref[...], b_ref[...], "tn")

    return pl.pallas_call(
        body, name=name, grid=(k_dim // tm, n // tn, nk),
        in_specs=[pl.BlockSpec((tk, tm), lambda i, j, k: (k, i)), b_spec],
        out_specs=out_spec, out_shape=out_shape,
        compiler_params=_cparams("parallel", "parallel", "arbitrary"),
    )(a, b)


class _Carrier:
    def __init__(self, job, n_in, n_out, n_scratch):
        self.job, self.n_in, self.n_out, self.n_scratch = job, n_in, n_out, n_scratch
        self.ji = len(job.inputs) if job else 0
        self.jo = len(job.out_shapes) if job else 0

    def operands(self):
        return list(self.job.inputs) if self.job else []

    def in_specs(self):
        return [pl.BlockSpec(memory_space=pl.ANY)] * self.ji

    def out_specs(self):
        return [pl.BlockSpec(memory_space=pl.ANY)] * self.jo

    def out_shapes(self):
        return list(self.job.out_shapes) if self.job else []

    def scratch(self):
        return list(self.job.sem_shapes) if self.job else []

    def aliases(self):
        return {self.n_in + a: self.n_out + b for a, b in self.job.aliases.items()} if self.job else {}

    def split(self, refs):
        a = self.n_in
        b = a + self.ji
        c = b + self.n_out
        d = c + self.jo
        e = d + self.n_scratch
        return list(refs[:a]) + list(refs[b:c]) + list(refs[d:e]), (refs[a:b], refs[c:d], refs[e:])

    def run(self, job_refs, step, steps):
        if not self.job:
            return
        for stage, mark in zip(self.job.stages, _job_marks(self.job, steps)):
            pl.when(step == mark)(functools.partial(stage, *job_refs))

    def results(self, res):
        res = list(res)
        return res[:self.n_out], res[self.n_out:]


FFN_ROW_TILE = 768


def _ffn_tile(r):
    return FFN_ROW_TILE if r % FFN_ROW_TILE == 0 else _row_tile(r)


def _ffn_in_swiglu(h, w, name):
    r, k_dim = h.shape
    n4 = w.shape[2]
    tm = _ffn_tile(r)

    def body(h_ref, wg_ref, wu_ref, u_ref, a_ref):
        hv = h_ref[...]
        g = jnp.dot(hv, wg_ref[...], preferred_element_type=F32)
        up = jnp.dot(hv, wu_ref[...], preferred_element_type=F32)
        u_ref[0] = g.astype(BF16)
        u_ref[1] = up.astype(BF16)
        a_ref[...] = (_silu(g) * up).astype(BF16)

    return pl.pallas_call(
        body, name=name, grid=(r // tm, 2),
        in_specs=[pl.BlockSpec((tm, k_dim), lambda i, j: (i, 0)),
                  pl.BlockSpec((None, k_dim, n4), lambda i, j: (j, 0, 0)),
                  pl.BlockSpec((None, k_dim, n4), lambda i, j: (j + 2, 0, 0))],
        out_specs=[pl.BlockSpec((2, tm, n4), lambda i, j: (0, i, j)), pl.BlockSpec((tm, n4), lambda i, j: (i, j))],
        out_shape=[jax.ShapeDtypeStruct((2, r, 2 * n4), BF16), jax.ShapeDtypeStruct((r, 2 * n4), BF16)],
        compiler_params=_cparams("parallel", "parallel"),
    )(h, w, w)


def _mm_nn_gate_residual(geo, a, w, z, mod, off, name, norm=None):
    r, k_dim = a.shape
    n = w.shape[1]
    tm = FFN_ROW_TILE if geo.seg % FFN_ROW_TILE == 0 else 256
    tiles = geo.seg // tm
    assert geo.seg % tm == 0 and r == geo.r and n == D_MODEL

    def body(a_ref, w_ref, z_ref, mx_ref, mc_ref, *rest):
        out = jnp.dot(a_ref[...], w_ref[...], preferred_element_type=F32)
        is_x = (pl.program_id(0) % tiles) * tm + lax.broadcasted_iota(jnp.int32, (tm, 1), 0) < geo.s
        zo = z_ref[...] + jnp.where(is_x, mx_ref[:, off:off + n], mc_ref[:, off:off + n]) * out
        if norm:
            g_ref, nx_ref, nc_ref, zo_ref, raw_ref, h_ref = rest
            no = norm[2]
            shift = jnp.where(is_x, nx_ref[:, no:no + n], nc_ref[:, no:no + n])
            scale = jnp.where(is_x, nx_ref[:, no + n:no + 2 * n], nc_ref[:, no + n:no + 2 * n])
            rs = lax.rsqrt(jnp.mean(zo * zo, axis=-1, keepdims=True) + EPS)
            h_ref[...] = ((zo * rs) * g_ref[...] * (1.0 + scale) + shift).astype(BF16)
        else:
            zo_ref, raw_ref = rest
        zo_ref[...] = zo
        raw_ref[...] = out.astype(BF16)

    def mod_specs(m):
        return [pl.BlockSpec((None, 1, m.shape[2]), lambda i: (i // tiles, 0, 0)), pl.BlockSpec((None, 1, m.shape[2]), lambda i: (geo.b, 0, 0))]

    row = pl.BlockSpec((tm, n), lambda i: (i, 0))
    in_specs = [pl.BlockSpec((tm, k_dim), lambda i: (i, 0)), pl.BlockSpec((k_dim, n), lambda i: (0, 0)), row] + mod_specs(mod)
    args = [a, w, z, mod, mod]
    out_specs, out_shape = [row, row], [jax.ShapeDtypeStruct((r, n), F32), jax.ShapeDtypeStruct((r, n), BF16)]
    if norm:
        in_specs += [pl.BlockSpec((1, n), lambda i: (0, 0))] + mod_specs(norm[1])
        args += [norm[0], norm[1], norm[1]]
        out_specs.append(row)
        out_shape.append(jax.ShapeDtypeStruct((r, n), BF16))
    res = pl.pallas_call(body, name=name, grid=(r // tm,), in_specs=in_specs, out_specs=out_specs, out_shape=out_shape,
                         compiler_params=_cparams("parallel"))(*args)
    return res if norm else (*res, None)


def _ffn_out_dx_swiglu_bwd(df, w_out, u, name, job=None):
    r, d = df.shape
    n4 = u.shape[2] // 2
    tm = _ffn_tile(r)
    carrier = _Carrier(job, 3, 1, 0)
    steps = (r // tm) * 2

    def body(*refs):
        (df_ref, w_ref, u_ref, du_ref), job_refs = carrier.split(refs)
        carrier.run(job_refs, pl.program_id(0) * 2 + pl.program_id(1), steps)
        da = _dot(df_ref[...], w_ref[...], "nt")
        g, up = u_ref[0].astype(F32), u_ref[1].astype(F32)
        s = _sigmoid(g)
        du_ref[0] = (da * up * (s * (1.0 + g * (1.0 - s)))).astype(BF16)
        du_ref[1] = (da * (g * s)).astype(BF16)

    res = pl.pallas_call(
        body, name=name, grid=(r // tm, 2),
        in_specs=[pl.BlockSpec((tm, d), lambda i, j: (i, 0)), pl.BlockSpec((n4, d), lambda i, j: (j, 0)),
                  pl.BlockSpec((2, tm, n4), lambda i, j: (0, i, j))] + carrier.in_specs(),
        out_specs=[pl.BlockSpec((2, tm, n4), lambda i, j: (0, i, j))] + carrier.out_specs(),
        out_shape=[jax.ShapeDtypeStruct(u.shape, BF16)] + carrier.out_shapes(),
        scratch_shapes=carrier.scratch(), input_output_aliases=carrier.aliases(),
        compiler_params=_cparams("arbitrary", "arbitrary"),
    )(df, w_out, u, *carrier.operands())
    (du,), extra = carrier.results(res)
    return du, extra


class _Rows:
    def __init__(self, b, s, l):
        self.b, self.s, self.l = b, s, l
        self.seg = s + l
        self.r = b * self.seg


def _rowwise(name, body, geo, tm, ins, outs):
    seg_blocks, x_blocks = geo.seg // tm, geo.s // tm
    assert geo.seg % tm == 0 and geo.s % tm == 0
    nb = geo.b

    def is_ctx(i):
        return i % seg_blocks >= x_blocks

    in_specs, args = [], []
    for arr, kind in ins:
        args.append(arr)
        if kind == "row":
            in_specs.append(pl.BlockSpec((tm, arr.shape[1]), lambda i: (i, 0)))
        elif kind == "ex":
            in_specs.append(pl.BlockSpec((None, 1, arr.shape[2]), lambda i: (jnp.where(is_ctx(i), nb, i // seg_blocks), 0, 0)))
        elif kind == "full":
            in_specs.append(pl.BlockSpec(arr.shape, lambda i, nd=arr.ndim: (0,) * nd))
        elif kind == "tab":
            in_specs.append(pl.BlockSpec((tm, arr.shape[1]), lambda i: (i % seg_blocks, 0)))
        elif kind == "xrow":
            in_specs.append(pl.BlockSpec(
                (tm, arr.shape[1]), lambda i: ((i // seg_blocks) * x_blocks + jnp.minimum(i % seg_blocks, x_blocks - 1), 0)))
        else:
            _, width, cb = kind
            in_specs.append(pl.BlockSpec((tm, width), lambda i, cb=cb: (i, cb)))
    out_specs, out_shapes = [], []
    for o in outs:
        if o[0] == "row":
            out_specs.append(pl.BlockSpec((tm, o[1]), lambda i: (i, 0)))
            out_shapes.append(jax.ShapeDtypeStruct((geo.r, o[1]), o[2]))
        elif o[0] == "exacc":
            out_specs.append(pl.BlockSpec((None, 1, o[1]), lambda i: (jnp.where(is_ctx(i), nb, 0) + i // seg_blocks, 0, 0)))
            out_shapes.append(jax.ShapeDtypeStruct((2 * nb, 1, o[1]), F32))
        else:
            out_specs.append(pl.BlockSpec((o[1], o[2]), lambda i: (0, 0)))
            out_shapes.append(jax.ShapeDtypeStruct((o[1], o[2]), F32))
    n_in = len(ins)

    def kern(*refs):
        i = pl.program_id(0)
        res = body(i, *[r[...].astype(F32) for r in refs[:n_in]])
        if not isinstance(res, (tuple, list)):
            res = (res,)
        jj = i % seg_blocks
        first_of_part = (jj == 0) | (jj == x_blocks)
        for o, ref, val in zip(outs, refs[n_in:], res):
            if o[0] == "row":
                ref[...] = val.astype(ref.dtype)
            else:
                first = first_of_part if o[0] == "exacc" else i == 0

                @pl.when(first)
                def _(ref=ref, val=val):
                    ref[...] = val

                @pl.when(jnp.logical_not(first))
                def _(ref=ref, val=val):
                    ref[...] += val

    res = pl.pallas_call(
        kern, name=name, grid=(geo.r // tm,), in_specs=in_specs, out_specs=out_specs, out_shape=out_shapes,
        compiler_params=_cparams("arbitrary"),
    )(*args)
    return res[0] if len(res) == 1 else res


def _colsum(v):
    return jnp.sum(v, axis=0, keepdims=True)


def _norm_mod(geo, z, gain, mod, off, name):
    d = D_MODEL

    def body(i, zv, g, m):
        r = lax.rsqrt(jnp.mean(zv * zv, axis=-1, keepdims=True) + EPS)
        return (zv * r) * g * (1.0 + m[:, off + d:off + 2 * d]) + m[:, off:off + d]

    return _rowwise(name, body, geo, 256, [(z, "row"), (gain, "full"), (mod, "ex")], [("row", d, BF16)])


def _norm_mod_bwd(geo, z, gain, mod, off, dh, dz_skip, name, gated=None):
    d = D_MODEL

    def body(i, zv, g, m, dhv, skip, *rest):
        r = lax.rsqrt(jnp.mean(zv * zv, axis=-1, keepdims=True) + EPS)
        n = zv * r
        dng = dhv * (1.0 + m[:, off + d:off + 2 * d])
        dn = dng * g
        dz = r * (dn - n * jnp.mean(dn * n, axis=-1, keepdims=True)) + skip
        res = (dz, _colsum(dhv), _colsum(dhv * (n * g)), _colsum(dng * n))
        if gated:
            ov, gm = rest
            res += (dz * gm[:, gated[2]:gated[2] + d], _colsum(dz * ov))
        return res

    ins = [(z, "row"), (gain, "full"), (mod, "ex"), (dh, "row"), (dz_skip, "row")]
    outs = [("row", d, F32), ("exacc", d), ("exacc", d), ("gacc", 1, d)]
    if gated:
        ins += [(gated[0], "row"), (gated[1], "ex")]
        outs += [("row", d, BF16), ("exacc", d)]
    return _rowwise(name, body, geo, 256, ins, outs)


def _loss_head(geo, z, target, out, mod, off, name):
    seg_blocks, x_blocks = geo.seg // 256, geo.s // 256
    d = D_MODEL

    def body(i, zv, tv, ov, m):
        keep = jnp.where(i % seg_blocks >= x_blocks, 0.0, 1.0)
        err = (zv - tv) * keep
        part = 0.5 * jnp.sum(jnp.mean(err * err, axis=-1, keepdims=True), axis=0, keepdims=True)
        dz = err * (1.0 / d)
        return dz, jnp.broadcast_to(part, (1, LANES)), dz * m[:, off:off + d], _colsum(dz * ov)

    return _rowwise(name, body, geo, 256, [(z, "row"), (target, "xrow"), (out, "row"), (mod, "ex")],
                    [("row", d, F32), ("gacc", 1, LANES), ("row", d, BF16), ("exacc", d)])


Q_SCALE = HEAD_DIM ** -0.5
N_QK_CHUNKS = (N_HEADS + N_KV_HEADS) * HEAD_DIM // LANES
N_Q_CHUNKS = N_HEADS * HEAD_DIM // LANES


def _attn_prep(geo, proj, cos, sin_signed, q_gain, k_gain, name):
    def body(i, p, cs, sn, qg, kg):
        outs = []
        for ch in range(N_QK_CHUNKS):
            is_q = ch < N_Q_CHUNKS
            outs.append(_qk_chunk(p[:, ch * LANES:(ch + 1) * LANES], qg if is_q else kg, cs, sn, Q_SCALE if is_q else 1.0))
        outs.append(p[:, N_QK_CHUNKS * LANES:])
        return jnp.concatenate(outs, axis=1)

    return _rowwise(name, body, geo, 256, [(proj, "row"), (cos, "tab"), (sin_signed, "tab"), (q_gain, "full"), (k_gain, "full")],
                    [("row", proj.shape[1], BF16)])


def _attn_prep_bwd(geo, proj, cos, sin_signed, q_gain, k_gain, dq, dkv, name):
    kw = N_KV_HEADS * HEAD_DIM

    def body(i, p, cs, sn, qg, kg, dqv, dkvv):
        outs = []
        dgains = [jnp.zeros((1, LANES), F32), jnp.zeros((1, LANES), F32)]
        for ch in range(N_QK_CHUNKS):
            is_q = ch < N_Q_CHUNKS
            scale = Q_SCALE if is_q else 1.0
            ct = dqv[:, ch * LANES:(ch + 1) * LANES] if is_q else dkvv[:, (ch - N_Q_CHUNKS) * LANES:(ch - N_Q_CHUNKS + 1) * LANES]
            _, vjp = jax.vjp(lambda xx, gg, scale=scale: _qk_chunk(xx, gg, cs, sn, scale),
                             p[:, ch * LANES:(ch + 1) * LANES], qg if is_q else kg)
            dx, dg = vjp(ct)
            outs.append(dx)
            dgains[0 if is_q else 1] = dgains[0 if is_q else 1] + dg
        outs.append(dkvv[:, kw:])
        return jnp.concatenate(outs, axis=1), dgains[0], dgains[1]

    return _rowwise(name, body, geo, 256,
                    [(proj, "row"), (cos, "tab"), (sin_signed, "tab"), (q_gain, "full"), (k_gain, "full"), (dq, "row"), (dkv, "row")],
                    [("row", proj.shape[1], BF16), ("gacc", 1, LANES), ("gacc", 1, LANES)])


def _attn_geometry(geo):
    assert geo.s % ATTN_BLOCK == 0 and geo.l % ATTN_BLOCK == 0 and geo.seg >= BAND
    return geo.seg // ATTN_BLOCK, geo.s // ATTN_BLOCK


def _attn_mask(j, s0, geo):
    r = lax.broadcasted_iota(jnp.int32, (ATTN_BLOCK, BAND), 0)
    n = lax.broadcasted_iota(jnp.int32, (ATTN_BLOCK, BAND), 1)
    dist = (s0 - j * ATTN_BLOCK) + n - r
    return (jnp.abs(dist) <= WINDOW) & (s0 + n < geo.s)


def _attn_probs(q, keys, valid, n_ctx, sink):
    s = _dot(q, keys, "nt")
    if valid is not None:
        s = jnp.concatenate([s[:, :n_ctx], jnp.where(valid, s[:, n_ctx:], NEG_INF)], axis=1)
    m = jnp.maximum(jnp.max(s, axis=-1, keepdims=True), sink)
    e, e_sink = jnp.exp(s - m), jnp.exp(sink - m)
    inv = 1.0 / (jnp.sum(e, axis=-1, keepdims=True) + e_sink)
    return e * inv, e_sink * inv


def _attn_keys(ref, s0, geo, with_band):
    ctx = ref[geo.s:geo.seg, :]
    return jnp.concatenate([ctx, ref[pl.ds(s0, BAND), :]], axis=0) if with_band else ctx


def _attention(geo, qkv, sink, name, job=None):
    n_blocks, n_x_blocks = _attn_geometry(geo)
    qw, kw = N_HEADS * HEAD_DIM, N_KV_HEADS * HEAD_DIM
    group = N_HEADS // N_KV_HEADS
    carrier = _Carrier(job, 4, 1, 0)

    def kern(*refs):
        (sink_ref, q_ref, k_ref, v_ref, o_ref), job_refs = carrier.split(refs)
        j = pl.program_id(1)
        carrier.run(job_refs, pl.program_id(0) * n_blocks + j, geo.b * n_blocks)
        s0 = pl.multiple_of(jnp.clip((j - 1) * ATTN_BLOCK, 0, geo.seg - BAND), ATTN_BLOCK)

        def heads(with_band):
            valid = _attn_mask(j, s0, geo) if with_band else None
            k_all, v_all = _attn_keys(k_ref, s0, geo, with_band), _attn_keys(v_ref, s0, geo, with_band)
            for h in range(N_HEADS):
                kv = slice((h // group) * HEAD_DIM, (h // group + 1) * HEAD_DIM)
                p, _ = _attn_probs(q_ref[:, h * HEAD_DIM:(h + 1) * HEAD_DIM], k_all[:, kv], valid, geo.l, sink_ref[h])
                o_ref[:, h * HEAD_DIM:(h + 1) * HEAD_DIM] = _dot(p, v_all[:, kv], "nn").astype(BF16)

        pl.when(j < n_x_blocks)(lambda: heads(True))
        pl.when(j >= n_x_blocks)(lambda: heads(False))

    res = pl.pallas_call(
        kern, name=name, grid=(geo.b, n_blocks),
        in_specs=[pl.BlockSpec(memory_space=pltpu.SMEM),
                  pl.BlockSpec((ATTN_BLOCK, qw), lambda b, j: (b * n_blocks + j, 0)),
                  pl.BlockSpec((geo.seg, kw), lambda b, j: (b, qw // kw)),
                  pl.BlockSpec((geo.seg, kw), lambda b, j: (b, qw // kw + 1))] + carrier.in_specs(),
        out_specs=[pl.BlockSpec((ATTN_BLOCK, qw), lambda b, j: (b * n_blocks + j, 0))] + carrier.out_specs(),
        out_shape=[jax.ShapeDtypeStruct((geo.r, qw), BF16)] + carrier.out_shapes(),
        scratch_shapes=carrier.scratch(), input_output_aliases=carrier.aliases(),
        compiler_params=_cparams("arbitrary", "arbitrary"),
    )(sink, qkv, qkv, qkv, *carrier.operands())
    (o,), extra = carrier.results(res)
    return o, extra


def _attention_bwd(geo, qkv, sink, do, name, job=None):
    n_blocks, n_x_blocks = _attn_geometry(geo)
    qw, kw = N_HEADS * HEAD_DIM, N_KV_HEADS * HEAD_DIM
    group = N_HEADS // N_KV_HEADS

    carrier = _Carrier(job, 5, 3, 0)

    def kern(*refs):
        (sink_ref, q_ref, k_ref, v_ref, do_ref, dq_ref, dkv_ref, dsink_ref), job_refs = carrier.split(refs)
        b, j = pl.program_id(0), pl.program_id(1)
        carrier.run(job_refs, b * n_blocks + j, geo.b * n_blocks)
        s0 = pl.multiple_of(jnp.clip((j - 1) * ATTN_BLOCK, 0, geo.seg - BAND), ATTN_BLOCK)

        @pl.when(j == 0)
        def _():
            dkv_ref[...] = jnp.zeros_like(dkv_ref)

        @pl.when((j == 0) & (b == 0))
        def _():
            dsink_ref[...] = jnp.zeros_like(dsink_ref)

        def heads(with_band):
            valid = _attn_mask(j, s0, geo) if with_band else None
            k_all, v_all = _attn_keys(k_ref, s0, geo, with_band), _attn_keys(v_ref, s0, geo, with_band)
            for g in range(N_KV_HEADS):
                kv = slice(g * HEAD_DIM, (g + 1) * HEAD_DIM)
                keys, vals = k_all[:, kv], v_all[:, kv]
                group_heads = [slice(h * HEAD_DIM, (h + 1) * HEAD_DIM) for h in range(g * group, (g + 1) * group)]
                ds_rows, p_rows = [], []
                for h, hs in zip(range(g * group, (g + 1) * group), group_heads):
                    dout = do_ref[:, hs]
                    p, p_sink = _attn_probs(q_ref[:, hs], keys, valid, geo.l, sink_ref[h])
                    dp = _dot(dout, vals, "nt")
                    dsum = jnp.sum(p * dp, axis=-1, keepdims=True)
                    ds = (p * (dp - dsum)).astype(BF16)
                    dq_ref[:, hs] = _dot(ds, keys, "nn")
                    ds_rows.append(ds)
                    p_rows.append(p.astype(BF16))
                    dsink_ref[h:h + 1, :] += jnp.broadcast_to(-jnp.sum(p_sink * dsum, axis=0, keepdims=True), (1, LANES))
                q_rows = jnp.concatenate([q_ref[:, hs] for hs in group_heads], axis=0)
                do_rows = jnp.concatenate([do_ref[:, hs] for hs in group_heads], axis=0)
                dk = _dot(jnp.concatenate(ds_rows, axis=0), q_rows, "tn")
                dv = _dot(jnp.concatenate(p_rows, axis=0), do_rows, "tn")
                vv = slice(kw + g * HEAD_DIM, kw + (g + 1) * HEAD_DIM)
                dkv_ref[geo.s:geo.seg, kv] += dk[:geo.l]
                dkv_ref[geo.s:geo.seg, vv] += dv[:geo.l]
                if with_band:
                    dkv_ref[pl.ds(s0, BAND), kv] += dk[geo.l:]
                    dkv_ref[pl.ds(s0, BAND), vv] += dv[geo.l:]

        pl.when(j < n_x_blocks)(lambda: heads(True))
        pl.when(j >= n_x_blocks)(lambda: heads(False))

    res = pl.pallas_call(
        kern, name=name, grid=(geo.b, n_blocks),
        in_specs=[pl.BlockSpec(memory_space=pltpu.SMEM),
                  pl.BlockSpec((ATTN_BLOCK, qw), lambda b, j: (b * n_blocks + j, 0)),
                  pl.BlockSpec((geo.seg, kw), lambda b, j: (b, qw // kw)),
                  pl.BlockSpec((geo.seg, kw), lambda b, j: (b, qw // kw + 1)),
                  pl.BlockSpec((ATTN_BLOCK, qw), lambda b, j: (b * n_blocks + j, 0))] + carrier.in_specs(),
        out_specs=[pl.BlockSpec((ATTN_BLOCK, qw), lambda b, j: (b * n_blocks + j, 0)),
                   pl.BlockSpec((geo.seg, 2 * kw), lambda b, j: (b, 0)),
                   pl.BlockSpec((N_HEADS, LANES), lambda b, j: (0, 0))] + carrier.out_specs(),
        out_shape=[jax.ShapeDtypeStruct((geo.r, qw), F32), jax.ShapeDtypeStruct((geo.r, 2 * kw), F32),
                   jax.ShapeDtypeStruct((N_HEADS, LANES), F32)] + carrier.out_shapes(),
        scratch_shapes=carrier.scratch(), input_output_aliases=carrier.aliases(),
        compiler_params=_cparams("arbitrary", "arbitrary"),
    )(sink, qkv, qkv, qkv, do, *carrier.operands())
    (dq, dkv, dsink), extra = carrier.results(res)
    return dq, dkv, dsink, extra


RET_QK_W = RET_HEADS * RET_QK_DIM
K_SCALE = RET_QK_DIM ** -0.5


def _ret_prep(geo, proj, cos, sin_signed, name):
    def body(i, p, cs, sn):
        cs2, sn2 = jnp.concatenate([cs] * RET_HEADS, axis=1), jnp.concatenate([sn] * RET_HEADS, axis=1)
        q = _rope(p[:, :RET_QK_W], cs2, sn2, RET_QK_DIM // 4)
        k = _rope(p[:, RET_QK_W:2 * RET_QK_W], cs2, sn2, RET_QK_DIM // 4) * K_SCALE
        return jnp.concatenate([q, k, p[:, 2 * RET_QK_W:]], axis=1)

    return _rowwise(name, body, geo, 128, [(proj, ("rowc", 2 * RET_QK_W + RET_VWIDTH, 0)), (cos, "tab"), (sin_signed, "tab")],
                    [("row", 2 * RET_QK_W + RET_VWIDTH, BF16)])


def _ret_prep_bwd(geo, dq, dk, dv, dgate, cos, sin_signed, name):
    def body(i, dqv, dkv, dvv, dg, cs, sn):
        cs2, sn2 = jnp.concatenate([cs] * RET_HEADS, axis=1), jnp.concatenate([sn] * RET_HEADS, axis=1)
        dkv = dkv * K_SCALE
        dqv = dqv * cs2 + _swap_halves(dqv * sn2, RET_QK_DIM // 4)
        dkv = dkv * cs2 + _swap_halves(dkv * sn2, RET_QK_DIM // 4)
        return jnp.concatenate([dqv, dkv, dvv, dg], axis=1)

    return _rowwise(name, body, geo, 128,
                    [(dq, "row"), (dk, "row"), (dv, "row"), (dgate, "row"), (cos, "tab"), (sin_signed, "tab")],
                    [("row", 2 * RET_QK_W + 2 * RET_VWIDTH, BF16)])


def _ret_step(state, q, k, v, lg, rev):
    c = RET_CHUNK
    ri = lax.broadcasted_iota(jnp.int32, (c, 1), 0).astype(F32)
    cj = lax.broadcasted_iota(jnp.int32, (1, c), 1).astype(F32)
    if rev:
        dist, q_decay, k_decay = cj - ri, jnp.exp(lg * (c - ri)), jnp.exp(lg * ri)
    else:
        dist, q_decay, k_decay = ri - cj, jnp.exp(lg * (ri + 1.0)), jnp.exp(lg * (c - 1.0 - ri))
    intra = jnp.where(dist >= 0, jnp.exp(lg * jnp.maximum(dist, 0.0)), 0.0)
    scores = _mm(q, k, "nt") * intra
    out = _mm(scores, v, "nn") + _mm(q, state, "nn") * q_decay
    new_state = state * jnp.exp(lg * c) + _mm(k * k_decay, v, "tn")
    return new_state, out


def _ret_state0(kc, vc, lg, rev):
    n = kc.shape[0]
    t = lax.broadcasted_iota(jnp.int32, (n, 1), 0).astype(F32)
    decay = jnp.exp(lg * t) if rev else jnp.exp(lg * (n - 1.0 - t))
    return _mm(kc * decay, vc, "tn")


def _ret_specs(geo):
    nq = RET_HEADS
    return [pl.BlockSpec((2 * RET_HEADS, LANES), lambda b, h: (0, 0)),
            pl.BlockSpec((geo.seg, RET_QK_DIM), lambda b, h: (b, h)),
            pl.BlockSpec((geo.seg, RET_QK_DIM), lambda b, h: (b, nq + h)),
            pl.BlockSpec((geo.seg, RET_V_DIM), lambda b, h: (b, nq + h))]


def _retention(geo, qkv, log_g, name):
    nc = geo.s // RET_CHUNK

    def kern(lg_ref, q_ref, k_ref, v_ref, o_ref, st_ref):
        h = pl.program_id(1)
        for d, rev in ((0, False), (1, True)):
            lg = lg_ref[pl.ds(d * RET_HEADS + h, 1), 0:1]
            st_ref[...] = _ret_state0(k_ref[geo.s:geo.seg, :].astype(F32), v_ref[geo.s:geo.seg, :].astype(F32), lg, rev)

            def chunk(ci, carry, d=d, rev=rev, lg=lg):
                r0 = pl.multiple_of((nc - 1 - ci if rev else ci) * RET_CHUNK, RET_CHUNK)
                rows = pl.ds(r0, RET_CHUNK)
                new_state, out = _ret_step(st_ref[...], q_ref[rows, :].astype(F32), k_ref[rows, :].astype(F32),
                                           v_ref[rows, :].astype(F32), lg, rev)
                st_ref[...] = new_state
                if d == 0:
                    o_ref[rows, :] = out
                else:
                    o_ref[rows, :] += out
                return carry

            lax.fori_loop(0, nc, chunk, 0)
        o_ref[geo.s:geo.seg, :] = jnp.zeros((geo.l, RET_V_DIM), F32)

    return pl.pallas_call(
        kern, name=name, grid=(geo.b, RET_HEADS), in_specs=_ret_specs(geo),
        out_specs=pl.BlockSpec((geo.seg, RET_V_DIM), lambda b, h: (b, h)),
        out_shape=jax.ShapeDtypeStruct((geo.r, RET_VWIDTH), F32),
        scratch_shapes=[pltpu.VMEM((RET_QK_DIM, RET_V_DIM), F32)],
        compiler_params=_cparams("parallel", "arbitrary"),
    )(log_g, qkv, qkv, qkv)


def _retention_bwd(geo, qkv, log_g, do, name):
    nc = geo.s // RET_CHUNK
    ctx = slice(geo.s, geo.seg)

    def kern(lg_ref, q_ref, k_ref, v_ref, do_ref, dq_ref, dk_ref, dv_ref, dlg_ref, states_ref, cur_ref, dst_ref):
        b, h = pl.program_id(0), pl.program_id(1)

        @pl.when((b == 0) & (h == 0))
        def _():
            dlg_ref[...] = jnp.zeros_like(dlg_ref)

        for d, rev in ((0, False), (1, True)):
            row = pl.ds(d * RET_HEADS + h, 1)
            lg = lg_ref[row, 0:1]
            kc, vc = k_ref[ctx, :].astype(F32), v_ref[ctx, :].astype(F32)
            cur_ref[...] = _ret_state0(kc, vc, lg, rev)

            def rows_of(ci, rev=rev):
                return pl.ds(pl.multiple_of((nc - 1 - ci if rev else ci) * RET_CHUNK, RET_CHUNK), RET_CHUNK)

            def load(rows):
                return q_ref[rows, :].astype(F32), k_ref[rows, :].astype(F32), v_ref[rows, :].astype(F32)

            def replay(ci, carry, rev=rev, lg=lg, rows_of=rows_of, load=load):
                states_ref[ci] = cur_ref[...]
                cur_ref[...] = _ret_step(cur_ref[...], *load(rows_of(ci)), lg, rev)[0]
                return carry

            lax.fori_loop(0, nc, replay, 0)
            dst_ref[...] = jnp.zeros_like(dst_ref)

            def back(t, dlg, d=d, rev=rev, lg=lg, rows_of=rows_of, load=load):
                ci = nc - 1 - t
                rows = rows_of(ci)
                _, vjp = jax.vjp(lambda st, q, k, v, g: _ret_step(st, q, k, v, g, rev), states_ref[ci], *load(rows), lg)
                dstate, dq, dk, dv, dg = vjp((dst_ref[...], do_ref[rows, :]))
                dst_ref[...] = dstate
                if d == 0:
                    dq_ref[rows, :], dk_ref[rows, :], dv_ref[rows, :] = dq, dk, dv
                else:
                    dq_ref[rows, :] += dq
                    dk_ref[rows, :] += dk
                    dv_ref[rows, :] += dv
                return dlg + dg

            dlg = lax.fori_loop(0, nc, back, jnp.zeros((1, 1), F32))
            _, vjp = jax.vjp(lambda kk, vv, g: _ret_state0(kk, vv, g, rev), kc, vc, lg)
            dkc, dvc, dg = vjp(dst_ref[...])
            if d == 0:
                dk_ref[ctx, :], dv_ref[ctx, :] = dkc, dvc
            else:
                dk_ref[ctx, :] += dkc
                dv_ref[ctx, :] += dvc
            dlg_ref[row, :] += jnp.broadcast_to(dlg + dg, (1, LANES))
        dq_ref[ctx, :] = jnp.zeros((geo.l, RET_QK_DIM), F32)

    nq = RET_HEADS
    return pl.pallas_call(
        kern, name=name, grid=(geo.b, RET_HEADS),
        in_specs=_ret_specs(geo) + [pl.BlockSpec((geo.seg, RET_V_DIM), lambda b, h: (b, h))],
        out_specs=[pl.BlockSpec((geo.seg, RET_QK_DIM), lambda b, h: (b, h)),
                   pl.BlockSpec((geo.seg, RET_QK_DIM), lambda b, h: (b, h)),
                   pl.BlockSpec((geo.seg, RET_V_DIM), lambda b, h: (b, h)),
                   pl.BlockSpec((2 * RET_HEADS, LANES), lambda b, h: (0, 0))],
        out_shape=[jax.ShapeDtypeStruct((geo.r, RET_QK_W), F32), jax.ShapeDtypeStruct((geo.r, RET_QK_W), F32),
                   jax.ShapeDtypeStruct((geo.r, RET_VWIDTH), F32), jax.ShapeDtypeStruct((2 * RET_HEADS, LANES), F32)],
        scratch_shapes=[pltpu.VMEM((nc, RET_QK_DIM, RET_V_DIM), F32), pltpu.VMEM((RET_QK_DIM, RET_V_DIM), F32),
                        pltpu.VMEM((RET_QK_DIM, RET_V_DIM), F32)],
        compiler_params=_cparams("arbitrary", "arbitrary"),
    )(log_g, qkv, qkv, qkv, do)


def _gated(o, g, gain):
    outs = []
    for h in range(RET_HEADS):
        cols = slice(h * RET_V_DIM, (h + 1) * RET_V_DIM)
        oh = o[:, cols]
        mu = jnp.mean(oh, axis=-1, keepdims=True)
        var = jnp.mean(jnp.square(oh - mu), axis=-1, keepdims=True)
        outs.append(_silu(g[:, cols]) * ((oh - mu) * lax.rsqrt(var + EPS) * gain[:, cols]))
    return jnp.concatenate(outs, axis=1)


def _ret_gated(geo, o, proj, gain, name):
    def body(i, ov, gv, gn):
        return _gated(ov, gv, gn)

    gate_block = (2 * RET_QK_W + RET_VWIDTH) // RET_VWIDTH
    return _rowwise(name, body, geo, 128, [(o, "row"), (proj, ("rowc", RET_VWIDTH, gate_block)), (gain, "full")],
                    [("row", RET_VWIDTH, BF16)])


def _ret_gated_bwd(geo, o, proj, gain, dout, name):
    def body(i, ov, gv, gn, dv):
        _, vjp = jax.vjp(_gated, ov, gv, gn)
        return vjp(dv)

    gate_block = (2 * RET_QK_W + RET_VWIDTH) // RET_VWIDTH
    return _rowwise(name, body, geo, 128,
                    [(o, "row"), (proj, ("rowc", RET_VWIDTH, gate_block)), (gain, "full"), (dout, "row")],
                    [("row", RET_VWIDTH, F32), ("row", RET_VWIDTH, F32), ("gacc", 1, RET_VWIDTH)])


def _whole(name, fn, out_shapes, *arrays):
    n = len(arrays)

    def kern(*refs):
        res = fn(*[r[...] for r in refs[:n]])
        for ref, val in zip(refs[n:], res):
            ref[...] = val.astype(ref.dtype)

    return pl.pallas_call(kern, name=name, out_shape=out_shapes)(*arrays)


def _rope_tables(geo, head_dim):
    rows = geo.s // GRID_W
    row = jnp.broadcast_to(jnp.arange(rows, dtype=jnp.int32)[:, None], (rows, GRID_W)).reshape(geo.s)
    col = jnp.broadcast_to(jnp.arange(GRID_W, dtype=jnp.int32)[None, :], (rows, GRID_W)).reshape(geo.s)
    axis_dim = head_dim // 2
    inv = ROPE_BASE ** (-jnp.arange(0, axis_dim, 2, dtype=F32) / axis_dim)
    ang_r = row.astype(F32)[:, None] * inv
    ang_c = col.astype(F32)[:, None] * inv
    cos = jnp.concatenate([jnp.cos(ang_r)] * 2 + [jnp.cos(ang_c)] * 2, axis=1)
    sin = jnp.concatenate([-jnp.sin(ang_r), jnp.sin(ang_r), -jnp.sin(ang_c), jnp.sin(ang_c)], axis=1)
    cos = jnp.concatenate([cos, jnp.ones((geo.l, head_dim), F32)], axis=0)
    sin = jnp.concatenate([sin, jnp.zeros((geo.l, head_dim), F32)], axis=0)
    reps = max(1, LANES // head_dim)
    return jnp.tile(cos, (1, reps)), jnp.tile(sin, (1, reps))


def _row_tile(r):
    return next(t for t in (1024, 512, 256, 128) if r % t == 0)


MOD_ROWS = 8


def _local_step(x, c, ctx, target, sp, wts, plan=None):
    nb, s, d = x.shape
    geo = _Rows(nb, s, ctx.shape[1])
    assert nb + 1 <= MOD_ROWS and d == D_MODEL
    tm = _row_tile(geo.r)
    z = jnp.concatenate([x, ctx], axis=1).reshape(geo.r, d)
    cvec = jnp.concatenate([c, sp["c_ctx"][None, :], jnp.zeros((MOD_ROWS - nb - 1, d), F32)], axis=0)
    cact, = _whole("cond_silu", lambda v: (_silu(v),), [jax.ShapeDtypeStruct(cvec.shape, F32)], cvec)
    cos64, sin64 = _rope_tables(geo, HEAD_DIM)
    cos256, sin256 = _rope_tables(geo, RET_QK_DIM)
    q_gain = jnp.tile(sp["q_norm"].reshape(1, HEAD_DIM), (1, LANES // HEAD_DIM))
    k_gain = jnp.tile(sp["k_norm"].reshape(1, HEAD_DIM), (1, LANES // HEAD_DIM))
    sink = sp["sink"].reshape(N_HEADS)
    log_g = jnp.broadcast_to(sp["log_g"].reshape(2 * RET_HEADS, 1), (2 * RET_HEADS, LANES))
    gn_g = sp["gn_g"].reshape(1, RET_VWIDTH)

    def modulation(i):
        mod = _mm_nn(cact, wts["ada"][i], F32, f"mod{i}", MOD_ROWS, wts["ada"][i].shape[2], d, bias=sp["ada_b"][i][None, :])
        return mod[:nb + 1, None, :]

    saved = []
    mods = [modulation(0), None]
    h1 = _norm_mod(geo, z, sp["norm1_g"][0][None, :], mods[0], 0, "norm1_0")
    for i in range(2):
        mod3 = mods[i]
        n1, n2 = sp["norm1_g"][i][None, :], sp["norm2_g"][i][None, :]
        if i == 0:
            proj = _mm_nn(h1, wts["attn_qkv"], F32, "attn_qkv", tm, wts["attn_qkv"].shape[2], d)
            prep = _attn_prep(geo, proj, cos64, sin64, q_gain, k_gain, "attn_prep")
            o, late = _attention(geo, prep, sink, "attn", plan.gather_job() if plan else None)
            if plan:
                plan.late_weights(late, wts)
            mods[1] = modulation(1)
            oraw = None
            w_o = wts["attn_o"]
        else:
            proj = _mm_nn(h1, wts["ret_qkvg"], BF16, "ret_qkvg", tm, wts["ret_qkvg"].shape[2], d)
            prep = _ret_prep(geo, proj, cos256, sin256, "ret_prep")
            oraw = _retention(geo, prep, log_g, "ret")
            o = _ret_gated(geo, oraw, proj, gn_g, "ret_gated")
            w_o = wts["ret_o"]
        zmid, mix, h2 = _mm_nn_gate_residual(geo, o, w_o, z, mod3, 2 * d, f"mix_out{i}", norm=(n2, mod3, 3 * d))
        u, a = _ffn_in_swiglu(h2, wts["ffn_in"][i], f"ffn_in{i}")
        next_norm = (sp["norm1_g"][1][None, :], mods[1], 0) if i == 0 else None
        zout, f, h1_next = _mm_nn_gate_residual(geo, a, wts["ffn_out"][i], zmid, mod3, 5 * d, f"ffn_out{i}", norm=next_norm)
        saved.append(dict(z=z, mod3=mod3, n1=n1, n2=n2, h1=h1, proj=proj, prep=prep, o=o, oraw=oraw, mix=mix, zmid=zmid,
                          h2=h2, u=u, a=a, f=f))
        z, h1 = zout, h1_next

    dz, loss, df, dg2 = _loss_head(geo, z, target.reshape(nb * s, d), saved[1]["f"], saved[1]["mod3"], 5 * d, "loss")

    big, small = {}, {}
    dmods = [None, None]
    for i in (1, 0):
        sv = saved[i]
        mod3 = sv["mod3"]
        carry = plan is not None and i == 0
        du, land = _ffn_out_dx_swiglu_bwd(df, wts["ffn_out"][i], sv["u"], f"ffn_out_dx{i}", plan.layer1.swap_job() if carry else None)
        if carry:
            plan.layer1.after_swap(land)
        big[f"ffn_out{i}"] = _mm_tn(sv["a"], df, f"ffn_out_dw{i}", D_FF // 2, 1024, tm).reshape(N_CHIPS, D_FF // N_CHIPS, d)
        n4 = wts["ffn_in"][i].shape[2]
        dh2 = _mm_nt(du, wts["ffn_in"][i], BF16, f"ffn_in_dx{i}", tm, 1024, n4)
        big[f"ffn_in{i}"] = _mm_tn(sv["h2"], du, f"ffn_in_dw{i}", 1024, n4, tm, shards=N_CHIPS)
        dzmid, dsh2, dsc2, dn2, dmix, dg1 = _norm_mod_bwd(geo, sv["zmid"], sv["n2"], mod3, 3 * d, dh2, dz, f"norm2_bwd{i}",
                                                          gated=(sv["mix"], mod3, 2 * d))
        if i == 0:
            do = _mm_nt(dmix, wts["attn_o"], BF16, "attn_out_dx", tm, 1024, 1024)
            big["attn_o"] = _mm_tn(sv["o"], dmix, "attn_out_dw", 1024, 1024, tm).reshape(N_CHIPS, 1024 // N_CHIPS, d)
            dq, dkv, dsink, land = _attention_bwd(geo, sv["prep"], sink, do, "attn_bwd", plan.layer1.exchange_job() if plan else None)
            if plan:
                plan.layer1_reduced = plan.layer1.after_exchange(land)
            dproj, dqg, dkg = _attn_prep_bwd(geo, sv["proj"], cos64, sin64, q_gain, k_gain, dq, dkv, "attn_prep_bwd")
            small["q_norm"] = dqg[0, :HEAD_DIM] + dqg[0, HEAD_DIM:]
            small["k_norm"] = dkg[0, :HEAD_DIM] + dkg[0, HEAD_DIM:]
            small["sink"] = dsink[:, 0]
            wq = wts["attn_qkv"]
            dh1 = _mm_nt(dproj, wq, BF16, "attn_qkv_dx", tm, 1024, wq.shape[2])
            big["attn_qkv"] = _mm_tn(sv["h1"], dproj, "attn_qkv_dw", 1024, wq.shape[2], tm, shards=N_CHIPS)
        else:
            do = _mm_nt(dmix, wts["ret_o"], BF16, "ret_out_dx", tm, 1024, 1024)
            big["ret_o"] = _mm_tn(sv["o"], dmix, "ret_out_dw", 1024, 1024, tm).reshape(N_CHIPS, RET_VWIDTH // N_CHIPS, d)
            doraw, dgate, dgn = _ret_gated_bwd(geo, sv["oraw"], sv["proj"], gn_g, do, "ret_gated_bwd")
            small["gn_g"] = dgn[0]
            dq, dk, dv, dlg = _retention_bwd(geo, sv["prep"], log_g, doraw, "ret_bwd")
            small["log_g"] = dlg[:, 0].reshape(2, RET_HEADS)
            dproj = _ret_prep_bwd(geo, dq, dk, dv, dgate, cos256, sin256, "ret_prep_bwd")
            wq = wts["ret_qkvg"]
            dh1 = _mm_nt(dproj, wq, BF16, "ret_qkvg_dx", tm, 1024, wq.shape[2])
            big["ret_qkvg"] = _mm_tn(sv["h1"], dproj, "ret_qkvg_dw", 1024, wq.shape[2], tm, shards=N_CHIPS)
        below = (saved[0]["f"], saved[0]["mod3"], 5 * d) if i == 1 else None
        dz, dsh1, dsc1, dn1, *below_grads = _norm_mod_bwd(geo, sv["z"], sv["n1"], mod3, 0, dh1, dzmid, f"norm1_bwd{i}", gated=below)
        small[f"norm1_g{i}"], small[f"norm2_g{i}"] = dn1[0], dn2[0]
        parts = [dsh1, dsc1, dg1, dsh2, dsc2, dg2]
        rows = jnp.concatenate([jnp.concatenate([p[:nb, 0, :] for p in parts], axis=1),
                                jnp.concatenate([jnp.sum(p[nb:, 0, :], axis=0, keepdims=True) for p in parts], axis=1),
                                jnp.zeros((MOD_ROWS - nb - 1, 6 * d), F32)], axis=0)
        dmods[i] = rows
        if below_grads:
            df, dg2 = below_grads
        small[f"ada_b{i}"] = jnp.sum(rows, axis=0)
        big[f"ada{i}"] = _mm_tn(cact, rows, f"ada_dw{i}", 1024, wts["ada"][i].shape[2], MOD_ROWS, shards=N_CHIPS)
        if plan and i == 1:
            plan.start_layer1(big)

    dcact = [_mm_nt(dmods[i], wts["ada"][i], F32, f"ada_dx{i}", MOD_ROWS, 1024, wts["ada"][i].shape[2]) for i in range(2)]

    def silu_bwd(v, d0, d1):
        sg = _sigmoid(v)
        return ((d0 + d1) * (sg * (1.0 + v * (1.0 - sg))),)

    dcvec, = _whole("cond_silu_bwd", silu_bwd, [jax.ShapeDtypeStruct(cvec.shape, F32)], cvec, dcact[0], dcact[1])
    small["c_ctx"] = dcvec[nb]
    return loss, dz, big, small


def _adamw(w, g, m, v, name):
    rows, cols = w.shape
    tr = next((t for t in (256, 128, 64, 32, 16, 8) if rows % t == 0), rows)
    c1 = 1.0 - ADAM_B1 ** ADAM_STEP
    c2 = 1.0 - ADAM_B2 ** ADAM_STEP

    def kern(w_ref, g_ref, m_ref, v_ref, d_ref, nm_ref, nv_ref):
        gv = g_ref[...]
        nm = ADAM_B1 * m_ref[...] + (1.0 - ADAM_B1) * gv
        nv = ADAM_B2 * v_ref[...] + (1.0 - ADAM_B2) * jnp.square(gv)
        d_ref[...] = -ADAM_LR * ((nm / c1) / (jnp.sqrt(nv / c2) + ADAM_EPS) + ADAM_WD * w_ref[...])
        nm_ref[...] = nm
        nv_ref[...] = nv

    spec = pl.BlockSpec((tr, cols), lambda i: (i, 0))
    return pl.pallas_call(
        kern, name=name, grid=(rows // tr,), in_specs=[spec] * 4, out_specs=[spec] * 3,
        out_shape=[jax.ShapeDtypeStruct(w.shape, F32)] * 3, compiler_params=_cparams("parallel"),
    )(w, g, m, v)


N_DEVICES = 8


def _mesh_pos():
    return lax.axis_index("x"), lax.axis_index("y"), lax.axis_index("c")


def _other_chips(x, y):
    return [(1 - x, y), (x, 1 - y), (1 - x, 1 - y)]


def _hbm(n):
    return [pl.BlockSpec(memory_space=pl.ANY)] * n


def _remote(src, dst, send_sem, recv_sem, device):
    return pltpu.make_async_remote_copy(src_ref=src, dst_ref=dst, send_sem=send_sem, recv_sem=recv_sem,
                                        device_id=device, device_id_type=MESH)


def _scalar_spec(grid, in_specs, out_specs):
    return pltpu.PrefetchScalarGridSpec(num_scalar_prefetch=1, grid=grid, in_specs=in_specs, out_specs=out_specs)


def _place_shard(shard, pos, name):
    r, cols = shard.shape
    tr = _slab_tile(r)

    def kern(pos_ref, s_ref, o_ref):
        o_ref[...] = s_ref[...].astype(BF16)

    return pl.pallas_call(
        kern, name=name, out_shape=jax.ShapeDtypeStruct((N_CHIPS, r, cols), BF16),
        grid_spec=_scalar_spec((r // tr,), [pl.BlockSpec((tr, cols), lambda i, p: (i, 0))],
                               pl.BlockSpec((None, tr, cols), lambda i, p: (p[1], i, 0))),
        compiler_params=_cparams("parallel"),
    )(pos, shard)


class _CommJob:
    def __init__(self, inputs, out_shapes, aliases, sem_shapes, stages):
        self.inputs, self.out_shapes, self.aliases, self.sem_shapes, self.stages = inputs, out_shapes, aliases, sem_shapes, stages


def _run_job(job, name):
    n_in, n_out = len(job.inputs), len(job.out_shapes)

    def body(*refs):
        for stage in job.stages:
            stage(refs[:n_in], refs[n_in:n_in + n_out], refs[n_in + n_out:])

    return pl.pallas_call(
        body, name=name, in_specs=_hbm(n_in), out_specs=_hbm(n_out), out_shape=job.out_shapes,
        input_output_aliases=job.aliases, scratch_shapes=job.sem_shapes,
    )(*job.inputs)


def _job_marks(job, steps):
    return {2: [0, steps - 1], 3: [0, (5 * steps) // 8, steps - 1]}[len(job.stages)]


def _gather_job(placed):
    n = len(placed)

    def half(w, which):
        r2 = placed[w].shape[1] // 2
        return pl.ds(which * r2, r2)

    def ici_copies(outs, sems, slot_of):
        x, y, c = _mesh_pos()
        res = []
        for w in range(n):
            for k, (px, py) in enumerate(_other_chips(x, y)):
                slab = outs[w].at[slot_of(x, y, px, py), half(w, c)]
                res.append((slab, _remote(slab, slab, sems[0].at[w, k], sems[1].at[w, k], (px, py, c))))
        return res

    def forwards(outs, sems, which_core):
        x, y, c = _mesh_pos()
        res = []
        for w in range(n):
            for k, (px, py) in enumerate(_other_chips(x, y)):
                slab = outs[w].at[2 * px + py, half(w, which_core(c))]
                res.append(_remote(slab, slab, sems[2].at[w, k], sems[3].at[w, k], (x, y, 1 - c)))
        return res

    def send(ins, outs, sems):
        for _, cp in ici_copies(outs, sems, lambda x, y, px, py: 2 * x + y):
            cp.start()

    def forward(ins, outs, sems):
        arrivals = ici_copies(outs, sems, lambda x, y, px, py: 2 * px + py)
        for (_, arrival), fwd in zip(arrivals, forwards(outs, sems, lambda c: c)):
            arrival.wait_recv()
            fwd.start()

    def finish(ins, outs, sems):
        for cp in forwards(outs, sems, lambda c: 1 - c):
            cp.wait_recv()
        for _, cp in ici_copies(outs, sems, lambda x, y, px, py: 2 * x + y):
            cp.wait_send()
        for cp in forwards(outs, sems, lambda c: c):
            cp.wait_send()

    return _CommJob(list(placed), [jax.ShapeDtypeStruct(p.shape, p.dtype) for p in placed], {w: w for w in range(n)},
                    [pltpu.SemaphoreType.DMA((n, 3))] * 4, [send, forward, finish])


def _pair_swap_job(grads):
    n = len(grads)

    def copies(ins, outs, sems):
        x, y, c = _mesh_pos()
        res = []
        for w in range(n):
            r2 = grads[w].shape[1] // 2
            res.append(_remote(ins[w].at[:, pl.ds((1 - c) * r2, r2)], outs[w], sems[0].at[w], sems[1].at[w], (x, y, 1 - c)))
        return res

    def send(ins, outs, sems):
        for cp in copies(ins, outs, sems):
            cp.start()

    def finish(ins, outs, sems):
        for cp in copies(ins, outs, sems):
            cp.wait()

    return _CommJob(list(grads), [jax.ShapeDtypeStruct((N_CHIPS, g.shape[1] // 2, g.shape[2]), F32) for g in grads], {},
                    [pltpu.SemaphoreType.DMA((n,))] * 2, [send, finish])


def _chip_exchange_job(hs):
    n = len(hs)

    def send(ins, outs, sems):
        x, y, c = _mesh_pos()
        for w in range(n):
            for k, (px, py) in enumerate(_other_chips(x, y)):
                _remote(ins[w].at[2 * px + py], outs[w].at[2 * x + y], sems[0].at[w, k], sems[1].at[w, k], (px, py, c)).start()

    def finish(ins, outs, sems):
        x, y, c = _mesh_pos()
        for w in range(n):
            for k, (px, py) in enumerate(_other_chips(x, y)):
                got = outs[w].at[2 * px + py]
                cp = _remote(ins[w].at[2 * px + py], got, sems[0].at[w, k], sems[1].at[w, k], (px, py, c))
                cp.wait_recv()
                cp.wait_send()

    return _CommJob(list(hs), [jax.ShapeDtypeStruct(h.shape, h.dtype) for h in hs], {},
                    [pltpu.SemaphoreType.DMA((n, 3))] * 2, [send, finish])


def _pair_share(ts, name):
    n = len(ts)

    def body(*refs):
        outs = refs[n:2 * n]
        send_sems, recv_sems = refs[2 * n:]
        x, y, c = _mesh_pos()
        sends = []
        for w in range(n):
            r2 = ts[w].shape[0] // 2
            mine = outs[w].at[pl.ds(c * r2, r2)]
            rc = _remote(mine, mine, send_sems.at[w], recv_sems.at[w], (x, y, 1 - c))
            rc.start()
            sends.append(rc)
        for w in range(n):
            r2 = ts[w].shape[0] // 2
            theirs = outs[w].at[pl.ds((1 - c) * r2, r2)]
            _remote(theirs, theirs, send_sems.at[w], recv_sems.at[w], (x, y, 1 - c)).wait_recv()
            sends[w].wait_send()

    return pl.pallas_call(
        body, name=name, in_specs=_hbm(n), out_specs=_hbm(n),
        out_shape=[jax.ShapeDtypeStruct(t.shape, F32) for t in ts],
        input_output_aliases={w: w for w in range(n)},
        scratch_shapes=[pltpu.SemaphoreType.DMA((n,))] * 2,
    )(*ts)


def _slab_tile(rows):
    return next(t for t in (512, 256, 176, 128, 64, 32, 16) if rows % t == 0)


def _sum_pair(grad, land, pos, name):
    _, r2, cols = land.shape
    tr = _slab_tile(r2)
    nt = r2 // tr

    def kern(pos_ref, a_ref, b_ref, o_ref):
        o_ref[...] = (a_ref[...] + b_ref[...]).astype(BF16)

    spec = pl.BlockSpec((None, tr, cols), lambda j, i, p: (j, i, 0))
    return pl.pallas_call(
        kern, name=name, out_shape=jax.ShapeDtypeStruct(land.shape, BF16),
        grid_spec=_scalar_spec((N_CHIPS, nt), [pl.BlockSpec((None, tr, cols), lambda j, i, p: (j, p[0] * nt + i, 0)), spec], spec),
        compiler_params=_cparams("parallel", "parallel"),
    )(pos, grad, land)


def _sum_chips(hs, land, pos, name):
    _, r2, cols = land.shape
    tr = _slab_tile(r2)
    nt = r2 // tr

    def kern(pos_ref, h_ref, l_ref, o_ref):
        acc = jnp.zeros((tr, cols), F32)
        own = h_ref[...].astype(F32)
        for k in range(N_CHIPS):
            acc = acc + jnp.where(pos_ref[1] == k, own, l_ref[k].astype(F32))
        o_ref[...] = acc

    return pl.pallas_call(
        kern, name=name, out_shape=jax.ShapeDtypeStruct((2 * r2, cols), F32),
        grid_spec=_scalar_spec((nt,), [pl.BlockSpec((None, tr, cols), lambda i, p: (p[1], i, 0)),
                                       pl.BlockSpec((N_CHIPS, tr, cols), lambda i, p: (0, i, 0))],
                               pl.BlockSpec((tr, cols), lambda i, p: (p[0] * nt + i, 0))),
        compiler_params=_cparams("parallel"),
    )(pos, hs, land)


class _ReduceScatter:
    def __init__(self, grads, pos, tag):
        self.grads, self.pos, self.tag = list(grads), pos, tag

    def swap_job(self):
        return _pair_swap_job(self.grads)

    def after_swap(self, land):
        self.hs = [_sum_pair(g, l, self.pos, f"grads_pair_sum_{self.tag}{w}") for w, (g, l) in enumerate(zip(self.grads, land))]

    def exchange_job(self):
        return _chip_exchange_job(self.hs)

    def after_exchange(self, land2):
        ts = [_sum_chips(h, l, self.pos, f"grads_chip_sum_{self.tag}{w}") for w, (h, l) in enumerate(zip(self.hs, land2))]
        return _pair_share(ts, f"grads_pair_share_{self.tag}")

    def run(self):
        self.after_swap(_run_job(self.swap_job(), f"grads_pair_swap_{self.tag}"))
        return self.after_exchange(_run_job(self.exchange_job(), f"grads_chip_exchange_{self.tag}"))


EARLY_WEIGHTS = ("ada0", "attn_qkv")
LATE_WEIGHTS = ("ada1", "ffn_in0", "ffn_in1", "ffn_out0", "ffn_out1", "attn_o", "ret_qkvg", "ret_o")
LAYER1_GRADS = ("ffn_out1", "ffn_in1", "ret_o", "ret_qkvg", "ada1")
LAYER0_GRADS = ("ffn_out0", "ffn_in0", "attn_o", "attn_qkv", "ada0")


def _fill_weights(wts, full):
    for name, w in full.items():
        if name[:-1] in ("ada", "ffn_in"):
            wts[name[:-1]][int(name[-1])] = w
        elif name[:-1] == "ffn_out":
            wts["ffn_out"][int(name[-1])] = w.reshape(-1, w.shape[2])
        elif name in ("attn_o", "ret_o"):
            wts[name] = w.reshape(-1, w.shape[2])
        else:
            wts[name] = w


class _StepPlan:
    def __init__(self, placed, pos):
        self.placed, self.pos = placed, pos
        self.layer1 = None
        self.layer1_reduced = None

    def gather_job(self):
        return _gather_job([self.placed[k] for k in LATE_WEIGHTS])

    def late_weights(self, outs, wts):
        _fill_weights(wts, dict(zip(LATE_WEIGHTS, outs)))

    def start_layer1(self, big):
        self.layer1 = _ReduceScatter([big[k] for k in LAYER1_GRADS], self.pos, "l1_")


def _all_reduce_small(v, name):
    def body(v_ref, o_ref, land_ref, send_sems, recv_sems):
        x, y, c = _mesh_pos()
        me = 4 * x + 2 * y + c
        land_ref[me] = v_ref[...]
        for t in range(N_DEVICES):
            @pl.when(t != me)
            def _(t=t):
                _remote(v_ref, land_ref.at[me], send_sems.at[t], recv_sems.at[me], (t // 4, (t // 2) % 2, t % 2)).start()
        for t in range(N_DEVICES):
            @pl.when(t != me)
            def _(t=t):
                _remote(v_ref, land_ref.at[t], send_sems.at[t], recv_sems.at[t], (t // 4, (t // 2) % 2, t % 2)).wait()
        acc = land_ref[0]
        for t in range(1, N_DEVICES):
            acc = acc + land_ref[t]
        o_ref[...] = acc

    vmem = pl.BlockSpec(memory_space=pltpu.VMEM)
    return pl.pallas_call(
        body, name=name, in_specs=[vmem], out_specs=vmem, out_shape=jax.ShapeDtypeStruct(v.shape, F32),
        scratch_shapes=[pltpu.VMEM((N_DEVICES,) + v.shape, F32), pltpu.SemaphoreType.DMA((N_DEVICES,)),
                        pltpu.SemaphoreType.DMA((N_DEVICES,))],
    )(v)


SMALL_ROWS = 24


def _pack_small(small, dlogit):
    d = D_MODEL
    misc = jnp.zeros((d,), F32)
    misc = misc.at[0:HEAD_DIM].set(small["q_norm"]).at[128:128 + HEAD_DIM].set(small["k_norm"])
    misc = misc.at[256:256 + N_HEADS].set(small["sink"]).at[384:384 + 2 * RET_HEADS].set(dlogit.reshape(-1))
    rows = [small["ada_b0"].reshape(6, d), small["ada_b1"].reshape(6, d), small["norm1_g0"][None], small["norm1_g1"][None],
            small["norm2_g0"][None], small["norm2_g1"][None], small["c_ctx"][None], small["gn_g"].reshape(2, d), misc[None]]
    buf = jnp.concatenate(rows, axis=0)
    return jnp.concatenate([buf, jnp.zeros((SMALL_ROWS - buf.shape[0], d), F32)], axis=0)


def _unpack_small(buf):
    d = D_MODEL
    misc = buf[19]
    return dict(ada_b=buf[0:12].reshape(2, 6 * d), norm1_g=buf[12:14], norm2_g=buf[14:16], c_ctx=buf[16],
                gn_g=buf[17:19].reshape(2 * d), q_norm=misc[0:HEAD_DIM], k_norm=misc[128:128 + HEAD_DIM],
                sink=misc[256:256 + N_HEADS], decay=misc[384:384 + 2 * RET_HEADS])


def kernel(x, c, ctx, c_ctx, ada_w, ada_b, norm1_g, norm2_g, ffn_w_in, ffn_w_out, attn_w_qkv, attn_q_norm, attn_k_norm, attn_sink, attn_w_o, ret_w_qkvg, ret_decay_logit, ret_gn_g, ret_w_o, loss_target, m_c_ctx, m_ada_w, m_ada_b, m_norm1_g, m_norm2_g, m_ffn_w_in, m_ffn_w_out, m_attn_w_qkv, m_attn_q_norm, m_attn_k_norm, m_attn_sink, m_attn_w_o, m_ret_w_qkvg, m_ret_decay_logit, m_ret_gn_g, m_ret_w_o, v_c_ctx, v_ada_w, v_ada_b, v_norm1_g, v_norm2_g, v_ffn_w_in, v_ffn_w_out, v_attn_w_qkv, v_attn_q_norm, v_attn_k_norm, v_attn_sink, v_attn_w_o, v_ret_w_qkvg, v_ret_decay_logit, v_ret_gn_g, v_ret_w_o):
    xi, yi, ci = _mesh_pos()
    chip = 2 * xi + yi
    nb, s, d = x.shape
    gn_shard = ret_gn_g.shape[1]

    shards = dict(ada0=ada_w[0], ada1=ada_w[1], ffn_in0=ffn_w_in[0], ffn_in1=ffn_w_in[1], ffn_out0=ffn_w_out[0],
                  ffn_out1=ffn_w_out[1], attn_qkv=attn_w_qkv[0], attn_o=attn_w_o[0], ret_qkvg=ret_w_qkvg[0], ret_o=ret_w_o[0])
    names = list(shards)
    pos = jnp.stack([ci, chip]).astype(jnp.int32)
    placed = {k: _place_shard(shards[k], pos, f"place_{k}") for k in names}
    early = _run_job(_gather_job([placed[k] for k in EARLY_WEIGHTS]), "gather_early_weights")
    gn_mine = jnp.where(ci == 0, ret_gn_g[0], jnp.zeros_like(ret_gn_g[0]))
    gn_place = lax.dynamic_update_slice(jnp.zeros((RET_VWIDTH,), F32), gn_mine, (chip * gn_shard,))
    gn_full = _all_reduce_small(gn_place.reshape(2, d), "gather_gn_gain").reshape(RET_VWIDTH)

    wts = dict(ada=[None, None], ffn_in=[None, None], ffn_out=[None, None], attn_qkv=None, attn_o=None, ret_qkvg=None, ret_o=None)
    _fill_weights(wts, dict(zip(EARLY_WEIGHTS, early)))
    plan = _StepPlan(placed, pos)
    decay_logit = ret_decay_logit[0]
    sp = dict(c_ctx=c_ctx, ada_b=ada_b, norm1_g=norm1_g, norm2_g=norm2_g, q_norm=attn_q_norm[0], k_norm=attn_k_norm[0],
              sink=attn_sink[0], log_g=jax.nn.log_sigmoid(decay_logit), gn_g=gn_full)
    loss_part, dz, big, small = _local_step(x, c, ctx, loss_target, sp, wts, plan)

    loss = lax.psum(loss_part[0, 0], ("x", "y", "c"))
    grad_x = dz.reshape(nb, -1, d)[:, :s]

    dlogit = small["log_g"] * jax.nn.sigmoid(-decay_logit)
    sg = _unpack_small(_all_reduce_small(_pack_small(small, dlogit), "reduce_small_grads"))
    reduced = dict(zip(LAYER1_GRADS, plan.layer1_reduced))
    reduced.update(zip(LAYER0_GRADS, _ReduceScatter([big[k] for k in LAYER0_GRADS], pos, "l0_").run()))

    grads = dict(
        c_ctx=sg["c_ctx"], ada_w=jnp.stack([reduced["ada0"], reduced["ada1"]]), ada_b=sg["ada_b"], norm1_g=sg["norm1_g"],
        norm2_g=sg["norm2_g"], ffn_w_in=jnp.stack([reduced["ffn_in0"], reduced["ffn_in1"]]),
        ffn_w_out=jnp.stack([reduced["ffn_out0"], reduced["ffn_out1"]]), attn_w_qkv=reduced["attn_qkv"][None],
        attn_q_norm=sg["q_norm"][None], attn_k_norm=sg["k_norm"][None], attn_sink=sg["sink"][None],
        attn_w_o=reduced["attn_o"][None], ret_w_qkvg=reduced["ret_qkvg"][None], ret_decay_logit=sg["decay"].reshape(1, 2, RET_HEADS),
        ret_gn_g=lax.dynamic_slice(sg["gn_g"], (chip * gn_shard,), (gn_shard,))[None], ret_w_o=reduced["ret_o"][None])
    params = dict(c_ctx=(c_ctx, m_c_ctx, v_c_ctx), ada_w=(ada_w, m_ada_w, v_ada_w), ada_b=(ada_b, m_ada_b, v_ada_b),
                  norm1_g=(norm1_g, m_norm1_g, v_norm1_g), norm2_g=(norm2_g, m_norm2_g, v_norm2_g),
                  ffn_w_in=(ffn_w_in, m_ffn_w_in, v_ffn_w_in), ffn_w_out=(ffn_w_out, m_ffn_w_out, v_ffn_w_out),
                  attn_w_qkv=(attn_w_qkv, m_attn_w_qkv, v_attn_w_qkv), attn_q_norm=(attn_q_norm, m_attn_q_norm, v_attn_q_norm),
                  attn_k_norm=(attn_k_norm, m_attn_k_norm, v_attn_k_norm), attn_sink=(attn_sink, m_attn_sink, v_attn_sink),
                  attn_w_o=(attn_w_o, m_attn_w_o, v_attn_w_o), ret_w_qkvg=(ret_w_qkvg, m_ret_w_qkvg, v_ret_w_qkvg),
                  ret_decay_logit=(ret_decay_logit, m_ret_decay_logit, v_ret_decay_logit),
                  ret_gn_g=(ret_gn_g, m_ret_gn_g, v_ret_gn_g), ret_w_o=(ret_w_o, m_ret_w_o, v_ret_w_o))
    order = list(params)
    deltas, new_m, new_v = [], [], []
    for k in order:
        w, m, v = params[k]
        g = grads[k].reshape(w.shape)
        grads[k] = g
        flat = (-1, w.shape[-1]) if w.ndim > 1 else (1, -1)
        if k == "ret_decay_logit":
            flat = (1, -1)
        dw, nm, nv = _adamw(w.reshape(flat), g.reshape(flat), m.reshape(flat), v.reshape(flat), f"adamw_{k}")
        deltas.append(dw.reshape(w.shape))
        new_m.append(nm.reshape(w.shape))
        new_v.append(nv.reshape(w.shape))
    return (loss, grad_x, *[grads[k] for k in order], *deltas, *new_m, *new_v)
```

```python
import functools

import jax
import jax.numpy as jnp
from jax import lax
from jax.experimental import pallas as pl
from jax.experimental.pallas import tpu as pltpu

F32 = jnp.float32
BF16 = jnp.bfloat16

D_MODEL = 1024
N_HEADS = 16
N_KV_HEADS = 4
HEAD_DIM = 64
WINDOW = 128
ATTN_BLOCK = 128
BAND = ATTN_BLOCK + 2 * WINDOW
RET_HEADS = 4
RET_QK_DIM = 256
RET_V_DIM = 512
RET_VWIDTH = 2048
RET_CHUNK = 128
D_FF = 2816
GRID_W = 64
ROPE_BASE = 10000.0
EPS = 1e-6
NEG_INF = -1e30
LANES = 128

ADAM_LR = 0.001
ADAM_B1 = 0.9
ADAM_B2 = 0.999
ADAM_EPS = 1e-08
ADAM_WD = 0.01
ADAM_STEP = 10

VMEM_LIMIT_BYTES = 56 * 1024 * 1024
MESH = pl.DeviceIdType.MESH
N_CHIPS = 4


def _cparams(*sem):
    return pltpu.CompilerParams(dimension_semantics=sem, vmem_limit_bytes=VMEM_LIMIT_BYTES)


_DIMS = {"nn": ((1,), (0,)), "nt": ((1,), (1,)), "tn": ((0,), (0,))}


def _dot(a, b, form):
    return lax.dot_general(a.astype(BF16), b.astype(BF16), (_DIMS[form], ((), ())), preferred_element_type=F32)


@functools.partial(jax.custom_vjp, nondiff_argnums=(2,))
def _mm(a, b, form):
    return _dot(a, b, form)


def _mm_fwd(a, b, form):
    return _dot(a, b, form), (a, b)


def _mm_bwd(form, res, ct):
    a, b = res
    if form == "nn":
        da, db = _dot(ct, b, "nt"), _dot(a, ct, "tn")
    elif form == "nt":
        da, db = _dot(ct, b, "nn"), _dot(ct, a, "tn")
    else:
        da, db = _dot(b, ct, "nt"), _dot(a, ct, "nn")
    return da.astype(a.dtype), db.astype(b.dtype)


_mm.defvjp(_mm_fwd, _mm_bwd)


def _swap_halves(x, half):
    w = x.shape[-1]
    lane = lax.broadcasted_iota(jnp.int32, x.shape, x.ndim - 1)
    return jnp.where(lane % (2 * half) < half, pltpu.roll(x, w - half, x.ndim - 1), pltpu.roll(x, half, x.ndim - 1))


@functools.partial(jax.custom_vjp, nondiff_argnums=(1,))
def _rot(x, half):
    return _swap_halves(x, half)


def _rot_fwd(x, half):
    return _swap_halves(x, half), None


def _rot_bwd(half, _, ct):
    return (_swap_halves(ct, half),)


_rot.defvjp(_rot_fwd, _rot_bwd)


def _rope(x, cos, sin_signed, half):
    return x * cos + _rot(x, half) * sin_signed


def _head_mean_square(x):
    r = lax.broadcasted_iota(jnp.int32, (LANES, LANES), 0) // HEAD_DIM
    c = lax.broadcasted_iota(jnp.int32, (LANES, LANES), 1) // HEAD_DIM
    g = jnp.where(r == c, 1.0 / HEAD_DIM, 0.0).astype(F32)
    return jnp.dot(x * x, g, precision=lax.Precision.HIGHEST, preferred_element_type=F32)


def _qk_chunk(x, gain, cos, sin_signed, scale):
    y = x * lax.rsqrt(_head_mean_square(x) + EPS) * gain
    return _rope(y, cos, sin_signed, HEAD_DIM // 4) * scale


def _sigmoid(x):
    return 1.0 / (1.0 + jnp.exp(-x))


def _silu(x):
    return x * _sigmoid(x)


def _mm_nn(a, w, out_dtype, name, tm, tn, tk, bias=None):
    m, k_dim = a.shape
    if w.ndim == 3:
        n = w.shape[0] * w.shape[2]
        per = w.shape[2] // tn
        assert w.shape[2] % tn == 0
        w_spec = pl.BlockSpec((None, tk, tn), lambda i, j, k: (j // per, k, j % per))
    else:
        n = w.shape[1]
        w_spec = pl.BlockSpec((tk, tn), lambda i, j, k: (k, j))
    assert m % tm == 0 and n % tn == 0 and k_dim % tk == 0, (name, a.shape, w.shape, tm, tn, tk)
    nk = k_dim // tk
    has_bias = bias is not None

    def body(*refs):
        a_ref, w_ref = refs[0], refs[1]
        b_ref = refs[2] if has_bias else None
        o_ref, acc_ref = (refs[-1], None) if nk == 1 else (refs[-2], refs[-1])
        if nk == 1:
            part = jnp.dot(a_ref[...].astype(BF16), w_ref[...], preferred_element_type=F32)
            o_ref[...] = (part + b_ref[...] if has_bias else part).astype(out_dtype)
            return
        k = pl.program_id(2)

        @pl.when(k == 0)
        def _():
            acc_ref[...] = jnp.zeros_like(acc_ref)

        acc_ref[...] += jnp.dot(a_ref[...].astype(BF16), w_ref[...], preferred_element_type=F32)

        @pl.when(k == nk - 1)
        def _():
            r = acc_ref[...]
            if has_bias:
                r = r + b_ref[...]
            o_ref[...] = r.astype(out_dtype)

    in_specs = [pl.BlockSpec((tm, tk), lambda i, j, k: (i, k)), w_spec]
    args = [a, w]
    if has_bias:
        in_specs.append(pl.BlockSpec((1, tn), lambda i, j, k: (0, j)))
        args.append(bias)
    return pl.pallas_call(
        body, name=name, grid=(m // tm, n // tn, nk), in_specs=in_specs,
        out_specs=pl.BlockSpec((tm, tn), lambda i, j, k: (i, j)),
        out_shape=jax.ShapeDtypeStruct((m, n), out_dtype),
        scratch_shapes=[pltpu.VMEM((tm, tn), F32)] if nk > 1 else [],
        compiler_params=_cparams("parallel", "parallel", "arbitrary"),
    )(*args)


def _mm_nt(a, w, out_dtype, name, tm, tn, tk):
    if a.ndim == 3:
        planes, m, plane_w = a.shape
        c_dim = planes * plane_w
        a_per = plane_w // tk
        assert plane_w % tk == 0
        a_spec = pl.BlockSpec((None, tm, tk), lambda i, j, k: (k // a_per, i, k % a_per))
    else:
        m, c_dim = a.shape
        a_spec = pl.BlockSpec((tm, tk), lambda i, j, k: (i, k))
    if w.ndim == 3:
        k_out = w.shape[1]
        per = w.shape[2] // tk
        assert w.shape[2] % tk == 0 and w.shape[0] * w.shape[2] == c_dim
        w_spec = pl.BlockSpec((None, tn, tk), lambda i, j, k: (k // per, j, k % per))
    else:
        k_out = w.shape[0]
        assert w.shape[1] == c_dim
        w_spec = pl.BlockSpec((tn, tk), lambda i, j, k: (j, k))
    assert m % tm == 0 and k_out % tn == 0 and c_dim % tk == 0, (name, a.shape, w.shape, tm, tn, tk)
    nk = c_dim // tk

    def body(a_ref, w_ref, o_ref, acc_ref=None):
        if nk == 1:
            o_ref[...] = _dot(a_ref[...], w_ref[...], "nt").astype(out_dtype)
            return
        k = pl.program_id(2)

        @pl.when(k == 0)
        def _():
            acc_ref[...] = jnp.zeros_like(acc_ref)

        acc_ref[...] += _dot(a_ref[...], w_ref[...], "nt")

        @pl.when(k == nk - 1)
        def _():
            o_ref[...] = acc_ref[...].astype(out_dtype)

    return pl.pallas_call(
        body, name=name, grid=(m // tm, k_out // tn, nk),
        in_specs=[a_spec, w_spec],
        out_specs=pl.BlockSpec((tm, tn), lambda i, j, k: (i, j)),
        out_shape=jax.ShapeDtypeStruct((m, k_out), out_dtype),
        scratch_shapes=[pltpu.VMEM((tm, tn), F32)] if nk > 1 else [],
        compiler_params=_cparams("parallel", "parallel", "arbitrary"),
    )(a, w)


def _mm_tn(a, b, name, tm, tn, tk, shards=None):
    r, k_dim = a.shape
    if b.ndim == 3:
        n = b.shape[0] * b.shape[2]
        b_per = b.shape[2] // tn
        assert b.shape[2] % tn == 0
        b_spec = pl.BlockSpec((None, tk, tn), lambda i, j, k: (j // b_per, k, j % b_per))
    else:
        n = b.shape[1]
        b_spec = pl.BlockSpec((tk, tn), lambda i, j, k: (k, j))
    assert r % tk == 0 and k_dim % tm == 0 and n % tn == 0, (name, a.shape, b.shape, tm, tn, tk)
    nk = r // tk
    if shards:
        per = n // shards // tn
        assert n % (shards * tn) == 0
        out_shape = jax.ShapeDtypeStruct((shards, k_dim, n // shards), F32)
        out_spec = pl.BlockSpec((None, tm, tn), lambda i, j, k: (j // per, i, j % per))
    else:
        out_shape = jax.ShapeDtypeStruct((k_dim, n), F32)
        out_spec = pl.BlockSpec((tm, tn), lambda i, j, k: (i, j))

    def body(a_ref, b_ref, o_ref):
        k = pl.program_id(2)

        @pl.when(k == 0)
        def _():
            o_ref[...] = jnp.zeros_like(o_ref)

        o_ref[...] += _dot(a_ref[...], b_ref[...], "tn")

    return pl.pallas_call(
        body, name=name, grid=(k_dim // tm, n // tn, nk),
        in_specs=[pl.BlockSpec((tk, tm), lambda i, j, k: (k, i)), b_spec],
        out_specs=out_spec, out_shape=out_shape,
        compiler_params=_cparams("parallel", "parallel", "arbitrary"),
    )(a, b)


class _Carrier:
    def __init__(self, job, n_in, n_out, n_scratch):
        self.job, self.n_in, self.n_out, self.n_scratch = job, n_in, n_out, n_scratch
        self.ji = len(job.inputs) if job else 0
        self.jo = len(job.out_shapes) if job else 0

    def operands(self):
        return list(self.job.inputs) if self.job else []

    def in_specs(self):
        return [pl.BlockSpec(memory_space=pl.ANY)] * self.ji

    def out_specs(self):
        return [pl.BlockSpec(memory_space=pl.ANY)] * self.jo

    def out_shapes(self):
        return list(self.job.out_shapes) if self.job else []

    def scratch(self):
        return list(self.job.sem_shapes) if self.job else []

    def aliases(self):
        return {self.n_in + a: self.n_out + b for a, b in self.job.aliases.items()} if self.job else {}

    def split(self, refs):
        a = self.n_in
        b = a + self.ji
        c = b + self.n_out
        d = c + self.jo
        e = d + self.n_scratch
        return list(refs[:a]) + list(refs[b:c]) + list(refs[d:e]), (refs[a:b], refs[c:d], refs[e:])

    def run(self, job_refs, step, steps):
        if not self.job:
            return
        for stage, mark in zip(self.job.stages, _job_marks(self.job, steps)):
            pl.when(step == mark)(functools.partial(stage, *job_refs))

    def results(self, res):
        res = list(res)
        return res[:self.n_out], res[self.n_out:]


FFN_ROW_TILE = 768


def _ffn_tile(r):
    return FFN_ROW_TILE if r % FFN_ROW_TILE == 0 else _row_tile(r)


def _ffn_in_swiglu(h, w, name):
    r, k_dim = h.shape
    n4 = w.shape[2]
    tm = _ffn_tile(r)

    def body(h_ref, wg_ref, wu_ref, u_ref, a_ref):
        hv = h_ref[...]
        g = jnp.dot(hv, wg_ref[...], preferred_element_type=F32)
        up = jnp.dot(hv, wu_ref[...], preferred_element_type=F32)
        u_ref[0] = g.astype(BF16)
        u_ref[1] = up.astype(BF16)
        a_ref[...] = (_silu(g) * up).astype(BF16)

    return pl.pallas_call(
        body, name=name, grid=(r // tm, 2),
        in_specs=[pl.BlockSpec((tm, k_dim), lambda i, j: (i, 0)),
                  pl.BlockSpec((None, k_dim, n4), lambda i, j: (j, 0, 0)),
                  pl.BlockSpec((None, k_dim, n4), lambda i, j: (j + 2, 0, 0))],
        out_specs=[pl.BlockSpec((2, tm, n4), lambda i, j: (0, i, j)), pl.BlockSpec((tm, n4), lambda i, j: (i, j))],
        out_shape=[jax.ShapeDtypeStruct((2, r, 2 * n4), BF16), jax.ShapeDtypeStruct((r, 2 * n4), BF16)],
        compiler_params=_cparams("parallel", "parallel"),
    )(h, w, w)


def _mm_nn_gate_residual(geo, a, w, z, mod, off, name, norm=None):
    r, k_dim = a.shape
    n = w.shape[1]
    tm = FFN_ROW_TILE if geo.seg % FFN_ROW_TILE == 0 else 256
    tiles = geo.seg // tm
    assert geo.seg % tm == 0 and r == geo.r and n == D_MODEL

    def body(a_ref, w_ref, z_ref, mx_ref, mc_ref, *rest):
        out = jnp.dot(a_ref[...], w_ref[...], preferred_element_type=F32)
        is_x = (pl.program_id(0) % tiles) * tm + lax.broadcasted_iota(jnp.int32, (tm, 1), 0) < geo.s
        zo = z_ref[...] + jnp.where(is_x, mx_ref[:, off:off + n], mc_ref[:, off:off + n]) * out
        if norm:
            g_ref, nx_ref, nc_ref, zo_ref, raw_ref, h_ref = rest
            no = norm[2]
            shift = jnp.where(is_x, nx_ref[:, no:no + n], nc_ref[:, no:no + n])
            scale = jnp.where(is_x, nx_ref[:, no + n:no + 2 * n], nc_ref[:, no + n:no + 2 * n])
            rs = lax.rsqrt(jnp.mean(zo * zo, axis=-1, keepdims=True) + EPS)
            h_ref[...] = ((zo * rs) * g_ref[...] * (1.0 + scale) + shift).astype(BF16)
        else:
            zo_ref, raw_ref = rest
        zo_ref[...] = zo
        raw_ref[...] = out.astype(BF16)

    def mod_specs(m):
        return [pl.BlockSpec((None, 1, m.shape[2]), lambda i: (i // tiles, 0, 0)), pl.BlockSpec((None, 1, m.shape[2]), lambda i: (geo.b, 0, 0))]

    row = pl.BlockSpec((tm, n), lambda i: (i, 0))
    in_specs = [pl.BlockSpec((tm, k_dim), lambda i: (i, 0)), pl.BlockSpec((k_dim, n), lambda i: (0, 0)), row] + mod_specs(mod)
    args = [a, w, z, mod, mod]
    out_specs, out_shape = [row, row], [jax.ShapeDtypeStruct((r, n), F32), jax.ShapeDtypeStruct((r, n), BF16)]
    if norm:
        in_specs += [pl.BlockSpec((1, n), lambda i: (0, 0))] + mod_specs(norm[1])
        args += [norm[0], norm[1], norm[1]]
        out_specs.append(row)
        out_shape.append(jax.ShapeDtypeStruct((r, n), BF16))
    res = pl.pallas_call(body, name=name, grid=(r // tm,), in_specs=in_specs, out_specs=out_specs, out_shape=out_shape,
                         compiler_params=_cparams("parallel"))(*args)
    return res if norm else (*res, None)


def _ffn_out_dx_swiglu_bwd(df, w_out, u, name, job=None):
    r, d = df.shape
    n4 = u.shape[2] // 2
    tm = _ffn_tile(r)
    carrier = _Carrier(job, 3, 1, 0)
    steps = (r // tm) * 2

    def body(*refs):
        (df_ref, w_ref, u_ref, du_ref), job_refs = carrier.split(refs)
        carrier.run(job_refs, pl.program_id(0) * 2 + pl.program_id(1), steps)
        da = _dot(df_ref[...], w_ref[...], "nt")
        g, up = u_ref[0].astype(F32), u_ref[1].astype(F32)
        s = _sigmoid(g)
        du_ref[0] = (da * up * (s * (1.0 + g * (1.0 - s)))).astype(BF16)
        du_ref[1] = (da * (g * s)).astype(BF16)

    res = pl.pallas_call(
        body, name=name, grid=(r // tm, 2),
        in_specs=[pl.BlockSpec((tm, d), lambda i, j: (i, 0)), pl.BlockSpec((n4, d), lambda i, j: (j, 0)),
                  pl.BlockSpec((2, tm, n4), lambda i, j: (0, i, j))] + carrier.in_specs(),
        out_specs=[pl.BlockSpec((2, tm, n4), lambda i, j: (0, i, j))] + carrier.out_specs(),
        out_shape=[jax.ShapeDtypeStruct(u.shape, BF16)] + carrier.out_shapes(),
        scratch_shapes=carrier.scratch(), input_output_aliases=carrier.aliases(),
        compiler_params=_cparams("arbitrary", "arbitrary"),
    )(df, w_out, u, *carrier.operands())
    (du,), extra = carrier.results(res)
    return du, extra


class _Rows:
    def __init__(self, b, s, l):
        self.b, self.s, self.l = b, s, l
        self.seg = s + l
        self.r = b * self.seg


def _rowwise(name, body, geo, tm, ins, outs):
    seg_blocks, x_blocks = geo.seg // tm, geo.s // tm
    assert geo.seg % tm == 0 and geo.s % tm == 0
    nb = geo.b

    def is_ctx(i):
        return i % seg_blocks >= x_blocks

    in_specs, args = [], []
    for arr, kind in ins:
        args.append(arr)
        if kind == "row":
            in_specs.append(pl.BlockSpec((tm, arr.shape[1]), lambda i: (i, 0)))
        elif kind == "ex":
            in_specs.append(pl.BlockSpec((None, 1, arr.shape[2]), lambda i: (jnp.where(is_ctx(i), nb, i // seg_blocks), 0, 0)))
        elif kind == "full":
            in_specs.append(pl.BlockSpec(arr.shape, lambda i, nd=arr.ndim: (0,) * nd))
        elif kind == "tab":
            in_specs.append(pl.BlockSpec((tm, arr.shape[1]), lambda i: (i % seg_blocks, 0)))
        elif kind == "xrow":
            in_specs.append(pl.BlockSpec(
                (tm, arr.shape[1]), lambda i: ((i // seg_blocks) * x_blocks + jnp.minimum(i % seg_blocks, x_blocks - 1), 0)))
        else:
            _, width, cb = kind
            in_specs.append(pl.BlockSpec((tm, width), lambda i, cb=cb: (i, cb)))
    out_specs, out_shapes = [], []
    for o in outs:
        if o[0] == "row":
            out_specs.append(pl.BlockSpec((tm, o[1]), lambda i: (i, 0)))
            out_shapes.append(jax.ShapeDtypeStruct((geo.r, o[1]), o[2]))
        elif o[0] == "xrow":
            out_specs.append(pl.BlockSpec(
                (tm, o[1]), lambda i: ((i // seg_blocks) * x_blocks + jnp.minimum(i % seg_blocks, x_blocks - 1), 0)))
            out_shapes.append(jax.ShapeDtypeStruct((geo.b * geo.s, o[1]), o[2]))
        elif o[0] == "exacc":
            out_specs.append(pl.BlockSpec((None, 1, o[1]), lambda i: (jnp.where(is_ctx(i), nb, 0) + i // seg_blocks, 0, 0)))
            out_shapes.append(jax.ShapeDtypeStruct((2 * nb, 1, o[1]), F32))
        else:
            out_specs.append(pl.BlockSpec((o[1], o[2]), lambda i: (0, 0)))
            out_shapes.append(jax.ShapeDtypeStruct((o[1], o[2]), F32))
    n_in = len(ins)

    def kern(*refs):
        i = pl.program_id(0)
        res = body(i, *[r[...].astype(F32) for r in refs[:n_in]])
        if not isinstance(res, (tuple, list)):
            res = (res,)
        jj = i % seg_blocks
        first_of_part = (jj == 0) | (jj == x_blocks)
        for o, ref, val in zip(outs, refs[n_in:], res):
            if o[0] == "row":
                ref[...] = val.astype(ref.dtype)
            elif o[0] == "xrow":
                @pl.when(jj < x_blocks)
                def _(ref=ref, val=val):
                    ref[...] = val.astype(ref.dtype)
            else:
                first = first_of_part if o[0] == "exacc" else i == 0

                @pl.when(first)
                def _(ref=ref, val=val):
                    ref[...] = val

                @pl.when(jnp.logical_not(first))
                def _(ref=ref, val=val):
                    ref[...] += val

    res = pl.pallas_call(
        kern, name=name, grid=(geo.r // tm,), in_specs=in_specs, out_specs=out_specs, out_shape=out_shapes,
        compiler_params=_cparams("arbitrary"),
    )(*args)
    return res[0] if len(res) == 1 else res


def _colsum(v):
    return jnp.sum(v, axis=0, keepdims=True)


def _norm_mod(geo, z, gain, mod, off, name):
    d = D_MODEL

    def body(i, zv, g, m):
        r = lax.rsqrt(jnp.mean(zv * zv, axis=-1, keepdims=True) + EPS)
        return (zv * r) * g * (1.0 + m[:, off + d:off + 2 * d]) + m[:, off:off + d]

    return _rowwise(name, body, geo, 256, [(z, "row"), (gain, "full"), (mod, "ex")], [("row", d, BF16)])


def _norm_mod_bwd(geo, z, gain, mod, off, dh, dz_skip, name, gated=None, latent_only=False):
    d = D_MODEL

    def body(i, zv, g, m, dhv, skip, *rest):
        r = lax.rsqrt(jnp.mean(zv * zv, axis=-1, keepdims=True) + EPS)
        n = zv * r
        dng = dhv * (1.0 + m[:, off + d:off + 2 * d])
        dn = dng * g
        dz = r * (dn - n * jnp.mean(dn * n, axis=-1, keepdims=True)) + skip
        res = (dz, _colsum(dhv), _colsum(dhv * (n * g)), _colsum(dng * n))
        if gated:
            ov, gm = rest
            res += (dz * gm[:, gated[2]:gated[2] + d], _colsum(dz * ov))
        return res

    ins = [(z, "row"), (gain, "full"), (mod, "ex"), (dh, "row"), (dz_skip, "row")]
    outs = [("xrow" if latent_only else "row", d, F32), ("exacc", d), ("exacc", d), ("gacc", 1, d)]
    if gated:
        ins += [(gated[0], "row"), (gated[1], "ex")]
        outs += [("row", d, BF16), ("exacc", d)]
    return _rowwise(name, body, geo, 256, ins, outs)


def _loss_head(geo, z, target, out, mod, off, name):
    seg_blocks, x_blocks = geo.seg // 256, geo.s // 256
    d = D_MODEL

    def body(i, zv, tv, ov, m):
        keep = jnp.where(i % seg_blocks >= x_blocks, 0.0, 1.0)
        err = (zv - tv) * keep
        part = 0.5 * jnp.sum(jnp.mean(err * err, axis=-1, keepdims=True), axis=0, keepdims=True)
        dz = err * (1.0 / d)
        return dz, jnp.broadcast_to(part, (1, LANES)), dz * m[:, off:off + d], _colsum(dz * ov)

    return _rowwise(name, body, geo, 256, [(z, "row"), (target, "xrow"), (out, "row"), (mod, "ex")],
                    [("row", d, F32), ("gacc", 1, LANES), ("row", d, BF16), ("exacc", d)])


Q_SCALE = HEAD_DIM ** -0.5
N_QK_CHUNKS = (N_HEADS + N_KV_HEADS) * HEAD_DIM // LANES
N_Q_CHUNKS = N_HEADS * HEAD_DIM // LANES


def _attn_prep(geo, proj, cos, sin_signed, q_gain, k_gain, name):
    def body(i, p, cs, sn, qg, kg):
        outs = []
        for ch in range(N_QK_CHUNKS):
            is_q = ch < N_Q_CHUNKS
            outs.append(_qk_chunk(p[:, ch * LANES:(ch + 1) * LANES], qg if is_q else kg, cs, sn, Q_SCALE if is_q else 1.0))
        outs.append(p[:, N_QK_CHUNKS * LANES:])
        return jnp.concatenate(outs, axis=1)

    return _rowwise(name, body, geo, 256, [(proj, "row"), (cos, "tab"), (sin_signed, "tab"), (q_gain, "full"), (k_gain, "full")],
                    [("row", proj.shape[1], BF16)])


def _attn_prep_bwd(geo, proj, cos, sin_signed, q_gain, k_gain, dq, dkv, name):
    kw = N_KV_HEADS * HEAD_DIM

    def body(i, p, cs, sn, qg, kg, dqv, dkvv):
        outs = []
        dgains = [jnp.zeros((1, LANES), F32), jnp.zeros((1, LANES), F32)]
        for ch in range(N_QK_CHUNKS):
            is_q = ch < N_Q_CHUNKS
            scale = Q_SCALE if is_q else 1.0
            ct = dqv[:, ch * LANES:(ch + 1) * LANES] if is_q else dkvv[:, (ch - N_Q_CHUNKS) * LANES:(ch - N_Q_CHUNKS + 1) * LANES]
            _, vjp = jax.vjp(lambda xx, gg, scale=scale: _qk_chunk(xx, gg, cs, sn, scale),
                             p[:, ch * LANES:(ch + 1) * LANES], qg if is_q else kg)
            dx, dg = vjp(ct)
            outs.append(dx)
            dgains[0 if is_q else 1] = dgains[0 if is_q else 1] + dg
        outs.append(dkvv[:, kw:])
        return jnp.concatenate(outs, axis=1), dgains[0], dgains[1]

    return _rowwise(name, body, geo, 256,
                    [(proj, "row"), (cos, "tab"), (sin_signed, "tab"), (q_gain, "full"), (k_gain, "full"), (dq, "row"), (dkv, "row")],
                    [("row", proj.shape[1], BF16), ("gacc", 1, LANES), ("gacc", 1, LANES)])


def _attn_geometry(geo):
    assert geo.s % ATTN_BLOCK == 0 and geo.l % ATTN_BLOCK == 0 and geo.seg >= BAND
    return geo.seg // ATTN_BLOCK, geo.s // ATTN_BLOCK


def _attn_mask(j, s0, geo):
    r = lax.broadcasted_iota(jnp.int32, (ATTN_BLOCK, BAND), 0)
    n = lax.broadcasted_iota(jnp.int32, (ATTN_BLOCK, BAND), 1)
    dist = (s0 - j * ATTN_BLOCK) + n - r
    return (jnp.abs(dist) <= WINDOW) & (s0 + n < geo.s)


def _attn_probs(q, keys, valid, n_ctx, sink):
    s = _dot(q, keys, "nt")
    if valid is not None:
        s = jnp.concatenate([s[:, :n_ctx], jnp.where(valid, s[:, n_ctx:], NEG_INF)], axis=1)
    m = jnp.maximum(jnp.max(s, axis=-1, keepdims=True), sink)
    e, e_sink = jnp.exp(s - m), jnp.exp(sink - m)
    inv = 1.0 / (jnp.sum(e, axis=-1, keepdims=True) + e_sink)
    return e * inv, e_sink * inv


def _attn_keys(ref, s0, geo, with_band):
    ctx = ref[geo.s:geo.seg, :]
    return jnp.concatenate([ctx, ref[pl.ds(s0, BAND), :]], axis=0) if with_band else ctx


def _attention(geo, qkv, sink, name, job=None):
    n_blocks, n_x_blocks = _attn_geometry(geo)
    qw, kw = N_HEADS * HEAD_DIM, N_KV_HEADS * HEAD_DIM
    group = N_HEADS // N_KV_HEADS
    carrier = _Carrier(job, 4, 1, 0)

    def kern(*refs):
        (sink_ref, q_ref, k_ref, v_ref, o_ref), job_refs = carrier.split(refs)
        j = pl.program_id(1)
        carrier.run(job_refs, pl.program_id(0) * n_blocks + j, geo.b * n_blocks)
        s0 = pl.multiple_of(jnp.clip((j - 1) * ATTN_BLOCK, 0, geo.seg - BAND), ATTN_BLOCK)

        def heads(with_band):
            valid = _attn_mask(j, s0, geo) if with_band else None
            k_all, v_all = _attn_keys(k_ref, s0, geo, with_band), _attn_keys(v_ref, s0, geo, with_band)
            for h in range(N_HEADS):
                kv = slice((h // group) * HEAD_DIM, (h // group + 1) * HEAD_DIM)
                p, _ = _attn_probs(q_ref[:, h * HEAD_DIM:(h + 1) * HEAD_DIM], k_all[:, kv], valid, geo.l, sink_ref[h])
                o_ref[:, h * HEAD_DIM:(h + 1) * HEAD_DIM] = _dot(p, v_all[:, kv], "nn").astype(BF16)

        pl.when(j < n_x_blocks)(lambda: heads(True))
        pl.when(j >= n_x_blocks)(lambda: heads(False))

    res = pl.pallas_call(
        kern, name=name, grid=(geo.b, n_blocks),
        in_specs=[pl.BlockSpec(memory_space=pltpu.SMEM),
                  pl.BlockSpec((ATTN_BLOCK, qw), lambda b, j: (b * n_blocks + j, 0)),
                  pl.BlockSpec((geo.seg, kw), lambda b, j: (b, qw // kw)),
                  pl.BlockSpec((geo.seg, kw), lambda b, j: (b, qw // kw + 1))] + carrier.in_specs(),
        out_specs=[pl.BlockSpec((ATTN_BLOCK, qw), lambda b, j: (b * n_blocks + j, 0))] + carrier.out_specs(),
        out_shape=[jax.ShapeDtypeStruct((geo.r, qw), BF16)] + carrier.out_shapes(),
        scratch_shapes=carrier.scratch(), input_output_aliases=carrier.aliases(),
        compiler_params=_cparams("arbitrary", "arbitrary"),
    )(sink, qkv, qkv, qkv, *carrier.operands())
    (o,), extra = carrier.results(res)
    return o, extra


def _attention_bwd(geo, qkv, sink, do, name, job=None):
    n_blocks, n_x_blocks = _attn_geometry(geo)
    qw, kw = N_HEADS * HEAD_DIM, N_KV_HEADS * HEAD_DIM
    group = N_HEADS // N_KV_HEADS

    carrier = _Carrier(job, 5, 3, 1)

    def kern(*refs):
        (sink_ref, q_ref, k_ref, v_ref, do_ref, dq_ref, dkv_out_ref, dsink_ref, dkv_ref), job_refs = carrier.split(refs)
        b, j = pl.program_id(0), pl.program_id(1)
        carrier.run(job_refs, b * n_blocks + j, geo.b * n_blocks)
        s0 = pl.multiple_of(jnp.clip((j - 1) * ATTN_BLOCK, 0, geo.seg - BAND), ATTN_BLOCK)

        @pl.when(j == 0)
        def _():
            dkv_ref[...] = jnp.zeros_like(dkv_ref)

        @pl.when((j == 0) & (b == 0))
        def _():
            dsink_ref[...] = jnp.zeros_like(dsink_ref)

        def heads(with_band):
            valid = _attn_mask(j, s0, geo) if with_band else None
            k_all, v_all = _attn_keys(k_ref, s0, geo, with_band), _attn_keys(v_ref, s0, geo, with_band)
            for g in range(N_KV_HEADS):
                kv = slice(g * HEAD_DIM, (g + 1) * HEAD_DIM)
                keys, vals = k_all[:, kv], v_all[:, kv]
                group_heads = [slice(h * HEAD_DIM, (h + 1) * HEAD_DIM) for h in range(g * group, (g + 1) * group)]
                ds_rows, p_rows = [], []
                for h, hs in zip(range(g * group, (g + 1) * group), group_heads):
                    dout = do_ref[:, hs]
                    p, p_sink = _attn_probs(q_ref[:, hs], keys, valid, geo.l, sink_ref[h])
                    dp = _dot(dout, vals, "nt")
                    dsum = jnp.sum(p * dp, axis=-1, keepdims=True)
                    ds = (p * (dp - dsum)).astype(BF16)
                    dq_ref[:, hs] = _dot(ds, keys, "nn").astype(BF16)
                    ds_rows.append(ds)
                    p_rows.append(p.astype(BF16))
                    dsink_ref[h:h + 1, :] += jnp.broadcast_to(-jnp.sum(p_sink * dsum, axis=0, keepdims=True), (1, LANES))
                q_rows = jnp.concatenate([q_ref[:, hs] for hs in group_heads], axis=0)
                do_rows = jnp.concatenate([do_ref[:, hs] for hs in group_heads], axis=0)
                dk = _dot(jnp.concatenate(ds_rows, axis=0), q_rows, "tn")
                dv = _dot(jnp.concatenate(p_rows, axis=0), do_rows, "tn")
                vv = slice(kw + g * HEAD_DIM, kw + (g + 1) * HEAD_DIM)
                dkv_ref[geo.s:geo.seg, kv] += dk[:geo.l]
                dkv_ref[geo.s:geo.seg, vv] += dv[:geo.l]
                if with_band:
                    dkv_ref[pl.ds(s0, BAND), kv] += dk[geo.l:]
                    dkv_ref[pl.ds(s0, BAND), vv] += dv[geo.l:]

        pl.when(j < n_x_blocks)(lambda: heads(True))
        pl.when(j >= n_x_blocks)(lambda: heads(False))

        @pl.when(j == n_blocks - 1)
        def _():
            dkv_out_ref[...] = dkv_ref[...].astype(BF16)

    res = pl.pallas_call(
        kern, name=name, grid=(geo.b, n_blocks),
        in_specs=[pl.BlockSpec(memory_space=pltpu.SMEM),
                  pl.BlockSpec((ATTN_BLOCK, qw), lambda b, j: (b * n_blocks + j, 0)),
                  pl.BlockSpec((geo.seg, kw), lambda b, j: (b, qw // kw)),
                  pl.BlockSpec((geo.seg, kw), lambda b, j: (b, qw // kw + 1)),
                  pl.BlockSpec((ATTN_BLOCK, qw), lambda b, j: (b * n_blocks + j, 0))] + carrier.in_specs(),
        out_specs=[pl.BlockSpec((ATTN_BLOCK, qw), lambda b, j: (b * n_blocks + j, 0)),
                   pl.BlockSpec((geo.seg, 2 * kw), lambda b, j: (b, 0)),
                   pl.BlockSpec((N_HEADS, LANES), lambda b, j: (0, 0))] + carrier.out_specs(),
        out_shape=[jax.ShapeDtypeStruct((geo.r, qw), BF16), jax.ShapeDtypeStruct((geo.r, 2 * kw), BF16),
                   jax.ShapeDtypeStruct((N_HEADS, LANES), F32)] + carrier.out_shapes(),
        scratch_shapes=[pltpu.VMEM((geo.seg, 2 * kw), F32)] + carrier.scratch(), input_output_aliases=carrier.aliases(),
        compiler_params=_cparams("arbitrary", "arbitrary"),
    )(sink, qkv, qkv, qkv, do, *carrier.operands())
    (dq, dkv, dsink), extra = carrier.results(res)
    return dq, dkv, dsink, extra


RET_QK_W = RET_HEADS * RET_QK_DIM
K_SCALE = RET_QK_DIM ** -0.5


def _ret_prep(geo, proj, cos, sin_signed, name):
    def body(i, p, cs, sn):
        cs2, sn2 = jnp.concatenate([cs] * RET_HEADS, axis=1), jnp.concatenate([sn] * RET_HEADS, axis=1)
        q = _rope(p[:, :RET_QK_W], cs2, sn2, RET_QK_DIM // 4)
        k = _rope(p[:, RET_QK_W:2 * RET_QK_W], cs2, sn2, RET_QK_DIM // 4) * K_SCALE
        return jnp.concatenate([q, k, p[:, 2 * RET_QK_W:]], axis=1)

    return _rowwise(name, body, geo, 128, [(proj, ("rowc", 2 * RET_QK_W + RET_VWIDTH, 0)), (cos, "tab"), (sin_signed, "tab")],
                    [("row", 2 * RET_QK_W + RET_VWIDTH, BF16)])


def _ret_prep_bwd(geo, dq, dk, dv, dgate, cos, sin_signed, name):
    def body(i, dqv, dkv, dvv, dg, cs, sn):
        cs2, sn2 = jnp.concatenate([cs] * RET_HEADS, axis=1), jnp.concatenate([sn] * RET_HEADS, axis=1)
        dkv = dkv * K_SCALE
        dqv = dqv * cs2 + _swap_halves(dqv * sn2, RET_QK_DIM // 4)
        dkv = dkv * cs2 + _swap_halves(dkv * sn2, RET_QK_DIM // 4)
        return jnp.concatenate([dqv, dkv, dvv, dg], axis=1)

    return _rowwise(name, body, geo, 128,
                    [(dq, "row"), (dk, "row"), (dv, "row"), (dgate, "row"), (cos, "tab"), (sin_signed, "tab")],
                    [("row", 2 * RET_QK_W + 2 * RET_VWIDTH, BF16)])


def _ret_step(state, q, k, v, lg, rev):
    c = RET_CHUNK
    ri = lax.broadcasted_iota(jnp.int32, (c, 1), 0).astype(F32)
    cj = lax.broadcasted_iota(jnp.int32, (1, c), 1).astype(F32)
    if rev:
        dist, q_decay, k_decay = cj - ri, jnp.exp(lg * (c - ri)), jnp.exp(lg * ri)
    else:
        dist, q_decay, k_decay = ri - cj, jnp.exp(lg * (ri + 1.0)), jnp.exp(lg * (c - 1.0 - ri))
    intra = jnp.where(dist >= 0, jnp.exp(lg * jnp.maximum(dist, 0.0)), 0.0)
    scores = _mm(q, k, "nt") * intra
    out = _mm(scores, v, "nn") + _mm(q, state, "nn") * q_decay
    new_state = state * jnp.exp(lg * c) + _mm(k * k_decay, v, "tn")
    return new_state, out


def _ret_state0(kc, vc, lg, rev):
    n = kc.shape[0]
    t = lax.broadcasted_iota(jnp.int32, (n, 1), 0).astype(F32)
    decay = jnp.exp(lg * t) if rev else jnp.exp(lg * (n - 1.0 - t))
    return _mm(kc * decay, vc, "tn")


def _ret_specs(geo):
    nq = RET_HEADS
    return [pl.BlockSpec((2 * RET_HEADS, LANES), lambda b, h: (0, 0)),
            pl.BlockSpec((geo.seg, RET_QK_DIM), lambda b, h: (b, h)),
            pl.BlockSpec((geo.seg, RET_QK_DIM), lambda b, h: (b, nq + h)),
            pl.BlockSpec((geo.seg, RET_V_DIM), lambda b, h: (b, nq + h))]


def _retention(geo, qkv, log_g, name):
    nc = geo.s // RET_CHUNK

    def kern(lg_ref, q_ref, k_ref, v_ref, o_ref, st_ref):
        h = pl.program_id(1)
        for d, rev in ((0, False), (1, True)):
            lg = lg_ref[pl.ds(d * RET_HEADS + h, 1), 0:1]
            st_ref[...] = _ret_state0(k_ref[geo.s:geo.seg, :].astype(F32), v_ref[geo.s:geo.seg, :].astype(F32), lg, rev)

            def chunk(ci, carry, d=d, rev=rev, lg=lg):
                r0 = pl.multiple_of((nc - 1 - ci if rev else ci) * RET_CHUNK, RET_CHUNK)
                rows = pl.ds(r0, RET_CHUNK)
                new_state, out = _ret_step(st_ref[...], q_ref[rows, :].astype(F32), k_ref[rows, :].astype(F32),
                                           v_ref[rows, :].astype(F32), lg, rev)
                st_ref[...] = new_state
                if d == 0:
                    o_ref[rows, :] = out
                else:
                    o_ref[rows, :] += out
                return carry

            lax.fori_loop(0, nc, chunk, 0)
        o_ref[geo.s:geo.seg, :] = jnp.zeros((geo.l, RET_V_DIM), F32)

    return pl.pallas_call(
        kern, name=name, grid=(geo.b, RET_HEADS), in_specs=_ret_specs(geo),
        out_specs=pl.BlockSpec((geo.seg, RET_V_DIM), lambda b, h: (b, h)),
        out_shape=jax.ShapeDtypeStruct((geo.r, RET_VWIDTH), F32),
        scratch_shapes=[pltpu.VMEM((RET_QK_DIM, RET_V_DIM), F32)],
        compiler_params=_cparams("parallel", "arbitrary"),
    )(log_g, qkv, qkv, qkv)


def _retention_bwd(geo, qkv, log_g, do, name):
    nc = geo.s // RET_CHUNK
    ctx = slice(geo.s, geo.seg)

    def kern(lg_ref, q_ref, k_ref, v_ref, do_ref, dq_ref, dk_ref, dv_ref, dlg_ref, states_ref, dst_ref, aq_ref, ak_ref, av_ref):
        b, h = pl.program_id(0), pl.program_id(1)

        @pl.when((b == 0) & (h == 0))
        def _():
            dlg_ref[...] = jnp.zeros_like(dlg_ref)

        for d, rev in ((0, False), (1, True)):
            row = pl.ds(d * RET_HEADS + h, 1)
            lg = lg_ref[row, 0:1]
            kc, vc = k_ref[ctx, :].astype(F32), v_ref[ctx, :].astype(F32)
            states_ref[0] = _ret_state0(kc, vc, lg, rev)

            def rows_of(ci, rev=rev):
                return pl.ds(pl.multiple_of((nc - 1 - ci if rev else ci) * RET_CHUNK, RET_CHUNK), RET_CHUNK)

            def load(rows):
                return q_ref[rows, :].astype(F32), k_ref[rows, :].astype(F32), v_ref[rows, :].astype(F32)

            def replay(ci, carry, rev=rev, lg=lg, rows_of=rows_of, load=load):
                states_ref[ci + 1] = _ret_step(states_ref[ci], *load(rows_of(ci)), lg, rev)[0]
                return carry

            lax.fori_loop(0, nc - 1, replay, 0)
            dst_ref[...] = jnp.zeros_like(dst_ref)

            def emit(rows, dq, dk, dv, d=d):
                if d == 0:
                    ak_ref[rows, :], av_ref[rows, :] = dk, dv
                    if dq is not None:
                        aq_ref[rows, :] = dq
                else:
                    dk_ref[rows, :] = (ak_ref[rows, :] + dk).astype(BF16)
                    dv_ref[rows, :] = (av_ref[rows, :] + dv).astype(BF16)
                    if dq is not None:
                        dq_ref[rows, :] = (aq_ref[rows, :] + dq).astype(BF16)

            def back(t, dlg, rev=rev, lg=lg, rows_of=rows_of, load=load, emit=emit):
                ci = nc - 1 - t
                rows = rows_of(ci)
                _, vjp = jax.vjp(lambda st, q, k, v, g: _ret_step(st, q, k, v, g, rev), states_ref[ci], *load(rows), lg)
                dstate, dq, dk, dv, dg = vjp((dst_ref[...], do_ref[rows, :].astype(F32)))
                dst_ref[...] = dstate
                emit(rows, dq, dk, dv)
                return dlg + dg

            dlg = lax.fori_loop(0, nc, back, jnp.zeros((1, 1), F32))
            _, vjp = jax.vjp(lambda kk, vv, g: _ret_state0(kk, vv, g, rev), kc, vc, lg)
            dkc, dvc, dg = vjp(dst_ref[...])
            emit(ctx, None, dkc, dvc)
            dlg_ref[row, :] += jnp.broadcast_to(dlg + dg, (1, LANES))
        dq_ref[ctx, :] = jnp.zeros((geo.l, RET_QK_DIM), BF16)

    nq = RET_HEADS
    return pl.pallas_call(
        kern, name=name, grid=(geo.b, RET_HEADS),
        in_specs=_ret_specs(geo) + [pl.BlockSpec((geo.seg, RET_V_DIM), lambda b, h: (b, h))],
        out_specs=[pl.BlockSpec((geo.seg, RET_QK_DIM), lambda b, h: (b, h)),
                   pl.BlockSpec((geo.seg, RET_QK_DIM), lambda b, h: (b, h)),
                   pl.BlockSpec((geo.seg, RET_V_DIM), lambda b, h: (b, h)),
                   pl.BlockSpec((2 * RET_HEADS, LANES), lambda b, h: (0, 0))],
        out_shape=[jax.ShapeDtypeStruct((geo.r, RET_QK_W), BF16), jax.ShapeDtypeStruct((geo.r, RET_QK_W), BF16),
                   jax.ShapeDtypeStruct((geo.r, RET_VWIDTH), BF16), jax.ShapeDtypeStruct((2 * RET_HEADS, LANES), F32)],
        scratch_shapes=[pltpu.VMEM((nc, RET_QK_DIM, RET_V_DIM), F32), pltpu.VMEM((RET_QK_DIM, RET_V_DIM), F32),
                        pltpu.VMEM((geo.seg, RET_QK_DIM), F32), pltpu.VMEM((geo.seg, RET_QK_DIM), F32),
                        pltpu.VMEM((geo.seg, RET_V_DIM), F32)],
        compiler_params=_cparams("arbitrary", "arbitrary"),
    )(log_g, qkv, qkv, qkv, do)


def _gated(o, g, gain):
    outs = []
    for h in range(RET_HEADS):
        cols = slice(h * RET_V_DIM, (h + 1) * RET_V_DIM)
        oh = o[:, cols]
        mu = jnp.mean(oh, axis=-1, keepdims=True)
        var = jnp.mean(jnp.square(oh - mu), axis=-1, keepdims=True)
        outs.append(_silu(g[:, cols]) * ((oh - mu) * lax.rsqrt(var + EPS) * gain[:, cols]))
    return jnp.concatenate(outs, axis=1)


def _ret_gated(geo, o, proj, gain, name):
    def body(i, ov, gv, gn):
        return _gated(ov, gv, gn)

    gate_block = (2 * RET_QK_W + RET_VWIDTH) // RET_VWIDTH
    return _rowwise(name, body, geo, 128, [(o, "row"), (proj, ("rowc", RET_VWIDTH, gate_block)), (gain, "full")],
                    [("row", RET_VWIDTH, BF16)])


def _ret_gated_bwd(geo, o, proj, gain, dout, name):
    def body(i, ov, gv, gn, dv):
        _, vjp = jax.vjp(_gated, ov, gv, gn)
        return vjp(dv)

    gate_block = (2 * RET_QK_W + RET_VWIDTH) // RET_VWIDTH
    return _rowwise(name, body, geo, 128,
                    [(o, "row"), (proj, ("rowc", RET_VWIDTH, gate_block)), (gain, "full"), (dout, "row")],
                    [("row", RET_VWIDTH, BF16), ("row", RET_VWIDTH, BF16), ("gacc", 1, RET_VWIDTH)])


def _whole(name, fn, out_shapes, *arrays):
    n = len(arrays)

    def kern(*refs):
        res = fn(*[r[...] for r in refs[:n]])
        for ref, val in zip(refs[n:], res):
            ref[...] = val.astype(ref.dtype)

    return pl.pallas_call(kern, name=name, out_shape=out_shapes)(*arrays)


def _rope_tables(geo, head_dim):
    rows = geo.s // GRID_W
    row = jnp.broadcast_to(jnp.arange(rows, dtype=jnp.int32)[:, None], (rows, GRID_W)).reshape(geo.s)
    col = jnp.broadcast_to(jnp.arange(GRID_W, dtype=jnp.int32)[None, :], (rows, GRID_W)).reshape(geo.s)
    axis_dim = head_dim // 2
    inv = ROPE_BASE ** (-jnp.arange(0, axis_dim, 2, dtype=F32) / axis_dim)
    ang_r = row.astype(F32)[:, None] * inv
    ang_c = col.astype(F32)[:, None] * inv
    cos = jnp.concatenate([jnp.cos(ang_r)] * 2 + [jnp.cos(ang_c)] * 2, axis=1)
    sin = jnp.concatenate([-jnp.sin(ang_r), jnp.sin(ang_r), -jnp.sin(ang_c), jnp.sin(ang_c)], axis=1)
    cos = jnp.concatenate([cos, jnp.ones((geo.l, head_dim), F32)], axis=0)
    sin = jnp.concatenate([sin, jnp.zeros((geo.l, head_dim), F32)], axis=0)
    reps = max(1, LANES // head_dim)
    return jnp.tile(cos, (1, reps)), jnp.tile(sin, (1, reps))


def _row_tile(r):
    return next(t for t in (1024, 512, 256, 128) if r % t == 0)


MOD_ROWS = 8


def _local_step(x, c, ctx, target, sp, wts, plan=None):
    nb, s, d = x.shape
    geo = _Rows(nb, s, ctx.shape[1])
    assert nb + 1 <= MOD_ROWS and d == D_MODEL
    tm = _row_tile(geo.r)
    z = jnp.concatenate([x, ctx], axis=1).reshape(geo.r, d)
    cvec = jnp.concatenate([c, sp["c_ctx"][None, :], jnp.zeros((MOD_ROWS - nb - 1, d), F32)], axis=0)
    cact, = _whole("cond_silu", lambda v: (_silu(v),), [jax.ShapeDtypeStruct(cvec.shape, F32)], cvec)
    cos64, sin64 = _rope_tables(geo, HEAD_DIM)
    cos256, sin256 = _rope_tables(geo, RET_QK_DIM)
    q_gain = jnp.tile(sp["q_norm"].reshape(1, HEAD_DIM), (1, LANES // HEAD_DIM))
    k_gain = jnp.tile(sp["k_norm"].reshape(1, HEAD_DIM), (1, LANES // HEAD_DIM))
    sink = sp["sink"].reshape(N_HEADS)
    log_g = jnp.broadcast_to(sp["log_g"].reshape(2 * RET_HEADS, 1), (2 * RET_HEADS, LANES))
    gn_g = sp["gn_g"].reshape(1, RET_VWIDTH)

    def modulation(i):
        mod = _mm_nn(cact, wts["ada"][i], F32, f"mod{i}", MOD_ROWS, wts["ada"][i].shape[2], d, bias=sp["ada_b"][i][None, :])
        return mod[:nb + 1, None, :]

    saved = []
    mods = [modulation(0), None]
    h1 = _norm_mod(geo, z, sp["norm1_g"][0][None, :], mods[0], 0, "norm1_0")
    for i in range(2):
        mod3 = mods[i]
        n1, n2 = sp["norm1_g"][i][None, :], sp["norm2_g"][i][None, :]
        if i == 0:
            proj = _mm_nn(h1, wts["attn_qkv"], F32, "attn_qkv", tm, wts["attn_qkv"].shape[2], d)
            prep = _attn_prep(geo, proj, cos64, sin64, q_gain, k_gain, "attn_prep")
            o, late = _attention(geo, prep, sink, "attn", plan.gather_job() if plan else None)
            if plan:
                plan.late_weights(late, wts)
            mods[1] = modulation(1)
            oraw = None
            w_o = wts["attn_o"]
        else:
            proj = _mm_nn(h1, wts["ret_qkvg"], BF16, "ret_qkvg", tm, wts["ret_qkvg"].shape[2], d)
            prep = _ret_prep(geo, proj, cos256, sin256, "ret_prep")
            oraw = _retention(geo, prep, log_g, "ret")
            o = _ret_gated(geo, oraw, proj, gn_g, "ret_gated")
            w_o = wts["ret_o"]
        zmid, mix, h2 = _mm_nn_gate_residual(geo, o, w_o, z, mod3, 2 * d, f"mix_out{i}", norm=(n2, mod3, 3 * d))
        u, a = _ffn_in_swiglu(h2, wts["ffn_in"][i], f"ffn_in{i}")
        next_norm = (sp["norm1_g"][1][None, :], mods[1], 0) if i == 0 else None
        zout, f, h1_next = _mm_nn_gate_residual(geo, a, wts["ffn_out"][i], zmid, mod3, 5 * d, f"ffn_out{i}", norm=next_norm)
        saved.append(dict(z=z, mod3=mod3, n1=n1, n2=n2, h1=h1, proj=proj, prep=prep, o=o, oraw=oraw, mix=mix, zmid=zmid,
                          h2=h2, u=u, a=a, f=f))
        z, h1 = zout, h1_next

    dz, loss, df, dg2 = _loss_head(geo, z, target.reshape(nb * s, d), saved[1]["f"], saved[1]["mod3"], 5 * d, "loss")

    big, small = {}, {}
    dmods = [None, None]
    for i in (1, 0):
        sv = saved[i]
        mod3 = sv["mod3"]
        carry = plan is not None and i == 0
        du, land = _ffn_out_dx_swiglu_bwd(df, wts["ffn_out"][i], sv["u"], f"ffn_out_dx{i}", plan.layer1.swap_job() if carry else None)
        if carry:
            plan.layer1.after_swap(land)
        big[f"ffn_out{i}"] = _mm_tn(sv["a"], df, f"ffn_out_dw{i}", D_FF // 2, 1024, tm).reshape(N_CHIPS, D_FF // N_CHIPS, d)
        n4 = wts["ffn_in"][i].shape[2]
        dh2 = _mm_nt(du, wts["ffn_in"][i], BF16, f"ffn_in_dx{i}", tm, 1024, n4)
        big[f"ffn_in{i}"] = _mm_tn(sv["h2"], du, f"ffn_in_dw{i}", 1024, n4, tm, shards=N_CHIPS)
        dzmid, dsh2, dsc2, dn2, dmix, dg1 = _norm_mod_bwd(geo, sv["zmid"], sv["n2"], mod3, 3 * d, dh2, dz, f"norm2_bwd{i}",
                                                          gated=(sv["mix"], mod3, 2 * d))
        if i == 0:
            do = _mm_nt(dmix, wts["attn_o"], BF16, "attn_out_dx", tm, 1024, 1024)
            big["attn_o"] = _mm_tn(sv["o"], dmix, "attn_out_dw", 1024, 1024, tm).reshape(N_CHIPS, 1024 // N_CHIPS, d)
            dq, dkv, dsink, land = _attention_bwd(geo, sv["prep"], sink, do, "attn_bwd", plan.layer1.exchange_job() if plan else None)
            if plan:
                plan.layer1_reduced = plan.layer1.after_exchange(land)
            dproj, dqg, dkg = _attn_prep_bwd(geo, sv["proj"], cos64, sin64, q_gain, k_gain, dq, dkv, "attn_prep_bwd")
            small["q_norm"] = dqg[0, :HEAD_DIM] + dqg[0, HEAD_DIM:]
            small["k_norm"] = dkg[0, :HEAD_DIM] + dkg[0, HEAD_DIM:]
            small["sink"] = dsink[:, 0]
            wq = wts["attn_qkv"]
            dh1 = _mm_nt(dproj, wq, BF16, "attn_qkv_dx", tm, 1024, wq.shape[2])
            big["attn_qkv"] = _mm_tn(sv["h1"], dproj, "attn_qkv_dw", 1024, wq.shape[2], tm, shards=N_CHIPS)
        else:
            do = _mm_nt(dmix, wts["ret_o"], BF16, "ret_out_dx", tm, 1024, 1024)
            big["ret_o"] = _mm_tn(sv["o"], dmix, "ret_out_dw", 1024, 1024, tm).reshape(N_CHIPS, RET_VWIDTH // N_CHIPS, d)
            doraw, dgate, dgn = _ret_gated_bwd(geo, sv["oraw"], sv["proj"], gn_g, do, "ret_gated_bwd")
            small["gn_g"] = dgn[0]
            dq, dk, dv, dlg = _retention_bwd(geo, sv["prep"], log_g, doraw, "ret_bwd")
            small["log_g"] = dlg[:, 0].reshape(2, RET_HEADS)
            dproj = _ret_prep_bwd(geo, dq, dk, dv, dgate, cos256, sin256, "ret_prep_bwd")
            wq = wts["ret_qkvg"]
            dh1 = _mm_nt(dproj, wq, BF16, "ret_qkvg_dx", tm, 1024, wq.shape[2])
            big["ret_qkvg"] = _mm_tn(sv["h1"], dproj, "ret_qkvg_dw", 1024, wq.shape[2], tm, shards=N_CHIPS)
        below = (saved[0]["f"], saved[0]["mod3"], 5 * d) if i == 1 else None
        dz, dsh1, dsc1, dn1, *below_grads = _norm_mod_bwd(geo, sv["z"], sv["n1"], mod3, 0, dh1, dzmid, f"norm1_bwd{i}", gated=below,
                                                              latent_only=i == 0)
        small[f"norm1_g{i}"], small[f"norm2_g{i}"] = dn1[0], dn2[0]
        parts = [dsh1, dsc1, dg1, dsh2, dsc2, dg2]
        rows = jnp.concatenate([jnp.concatenate([p[:nb, 0, :] for p in parts], axis=1),
                                jnp.concatenate([jnp.sum(p[nb:, 0, :], axis=0, keepdims=True) for p in parts], axis=1),
                                jnp.zeros((MOD_ROWS - nb - 1, 6 * d), F32)], axis=0)
        dmods[i] = rows
        if below_grads:
            df, dg2 = below_grads
        small[f"ada_b{i}"] = jnp.sum(rows, axis=0)
        big[f"ada{i}"] = _mm_tn(cact, rows, f"ada_dw{i}", 1024, wts["ada"][i].shape[2], MOD_ROWS, shards=N_CHIPS)
        if plan and i == 1:
            plan.start_layer1(big)

    dcact = [_mm_nt(dmods[i], wts["ada"][i], F32, f"ada_dx{i}", MOD_ROWS, 1024, wts["ada"][i].shape[2]) for i in range(2)]

    def silu_bwd(v, d0, d1):
        sg = _sigmoid(v)
        return ((d0 + d1) * (sg * (1.0 + v * (1.0 - sg))),)

    dcvec, = _whole("cond_silu_bwd", silu_bwd, [jax.ShapeDtypeStruct(cvec.shape, F32)], cvec, dcact[0], dcact[1])
    small["c_ctx"] = dcvec[nb]
    return loss, dz, big, small


def _adamw(w, g, m, v, name):
    rows, cols = w.shape
    tr = next((t for t in (256, 128, 64, 32, 16, 8) if rows % t == 0), rows)
    c1 = 1.0 - ADAM_B1 ** ADAM_STEP
    c2 = 1.0 - ADAM_B2 ** ADAM_STEP

    def kern(w_ref, g_ref, m_ref, v_ref, d_ref, nm_ref, nv_ref):
        gv = g_ref[...]
        nm = ADAM_B1 * m_ref[...] + (1.0 - ADAM_B1) * gv
        nv = ADAM_B2 * v_ref[...] + (1.0 - ADAM_B2) * jnp.square(gv)
        d_ref[...] = -ADAM_LR * ((nm / c1) / (jnp.sqrt(nv / c2) + ADAM_EPS) + ADAM_WD * w_ref[...])
        nm_ref[...] = nm
        nv_ref[...] = nv

    spec = pl.BlockSpec((tr, cols), lambda i: (i, 0))
    return pl.pallas_call(
        kern, name=name, grid=(rows // tr,), in_specs=[spec] * 4, out_specs=[spec] * 3,
        out_shape=[jax.ShapeDtypeStruct(w.shape, F32)] * 3, compiler_params=_cparams("parallel"),
    )(w, g, m, v)


N_DEVICES = 8


def _mesh_pos():
    return lax.axis_index("x"), lax.axis_index("y"), lax.axis_index("c")


def _other_chips(x, y):
    return [(1 - x, y), (x, 1 - y), (1 - x, 1 - y)]


def _hbm(n):
    return [pl.BlockSpec(memory_space=pl.ANY)] * n


def _remote(src, dst, send_sem, recv_sem, device):
    return pltpu.make_async_remote_copy(src_ref=src, dst_ref=dst, send_sem=send_sem, recv_sem=recv_sem,
                                        device_id=device, device_id_type=MESH)


def _scalar_spec(grid, in_specs, out_specs):
    return pltpu.PrefetchScalarGridSpec(num_scalar_prefetch=1, grid=grid, in_specs=in_specs, out_specs=out_specs)


def _place_shard(param, layer, pos, name):
    _, r, cols = param.shape
    tr = _slab_tile(r)

    def kern(pos_ref, s_ref, o_ref):
        o_ref[...] = s_ref[...].astype(BF16)

    return pl.pallas_call(
        kern, name=name, out_shape=jax.ShapeDtypeStruct((N_CHIPS, r, cols), BF16),
        grid_spec=_scalar_spec((r // tr,), [pl.BlockSpec((None, tr, cols), lambda i, p: (layer, i, 0))],
                               pl.BlockSpec((None, tr, cols), lambda i, p: (p[1], i, 0))),
        compiler_params=_cparams("parallel"),
    )(pos, param)


class _CommJob:
    def __init__(self, inputs, out_shapes, aliases, sem_shapes, stages):
        self.inputs, self.out_shapes, self.aliases, self.sem_shapes, self.stages = inputs, out_shapes, aliases, sem_shapes, stages


def _run_job(job, name):
    n_in, n_out = len(job.inputs), len(job.out_shapes)

    def body(*refs):
        for stage in job.stages:
            stage(refs[:n_in], refs[n_in:n_in + n_out], refs[n_in + n_out:])

    return pl.pallas_call(
        body, name=name, in_specs=_hbm(n_in), out_specs=_hbm(n_out), out_shape=job.out_shapes,
        input_output_aliases=job.aliases, scratch_shapes=job.sem_shapes,
    )(*job.inputs)


def _job_marks(job, steps):
    return {2: [0, steps - 1], 3: [0, (5 * steps) // 8, steps - 1]}[len(job.stages)]


def _gather_job(placed):
    n = len(placed)

    def half(w, which):
        r2 = placed[w].shape[1] // 2
        return pl.ds(which * r2, r2)

    def ici_copies(outs, sems, slot_of):
        x, y, c = _mesh_pos()
        res = []
        for w in range(n):
            for k, (px, py) in enumerate(_other_chips(x, y)):
                slab = outs[w].at[slot_of(x, y, px, py), half(w, c)]
                res.append((slab, _remote(slab, slab, sems[0].at[w, k], sems[1].at[w, k], (px, py, c))))
        return res

    def forwards(outs, sems, which_core):
        x, y, c = _mesh_pos()
        res = []
        for w in range(n):
            for k, (px, py) in enumerate(_other_chips(x, y)):
                slab = outs[w].at[2 * px + py, half(w, which_core(c))]
                res.append(_remote(slab, slab, sems[2].at[w, k], sems[3].at[w, k], (x, y, 1 - c)))
        return res

    def send(ins, outs, sems):
        for _, cp in ici_copies(outs, sems, lambda x, y, px, py: 2 * x + y):
            cp.start()

    def forward(ins, outs, sems):
        arrivals = ici_copies(outs, sems, lambda x, y, px, py: 2 * px + py)
        for (_, arrival), fwd in zip(arrivals, forwards(outs, sems, lambda c: c)):
            arrival.wait_recv()
            fwd.start()

    def finish(ins, outs, sems):
        for cp in forwards(outs, sems, lambda c: 1 - c):
            cp.wait_recv()
        for _, cp in ici_copies(outs, sems, lambda x, y, px, py: 2 * x + y):
            cp.wait_send()
        for cp in forwards(outs, sems, lambda c: c):
            cp.wait_send()

    return _CommJob(list(placed), [jax.ShapeDtypeStruct(p.shape, p.dtype) for p in placed], {w: w for w in range(n)},
                    [pltpu.SemaphoreType.DMA((n, 3))] * 4, [send, forward, finish])


def _pair_swap_job(grads):
    n = len(grads)

    def copies(ins, outs, sems):
        x, y, c = _mesh_pos()
        res = []
        for w in range(n):
            r2 = grads[w].shape[1] // 2
            res.append(_remote(ins[w].at[:, pl.ds((1 - c) * r2, r2)], outs[w], sems[0].at[w], sems[1].at[w], (x, y, 1 - c)))
        return res

    def send(ins, outs, sems):
        for cp in copies(ins, outs, sems):
            cp.start()

    def finish(ins, outs, sems):
        for cp in copies(ins, outs, sems):
            cp.wait()

    return _CommJob(list(grads), [jax.ShapeDtypeStruct((N_CHIPS, g.shape[1] // 2, g.shape[2]), F32) for g in grads], {},
                    [pltpu.SemaphoreType.DMA((n,))] * 2, [send, finish])


def _chip_exchange_job(hs):
    n = len(hs)

    def send(ins, outs, sems):
        x, y, c = _mesh_pos()
        for w in range(n):
            for k, (px, py) in enumerate(_other_chips(x, y)):
                _remote(ins[w].at[2 * px + py], outs[w].at[2 * x + y], sems[0].at[w, k], sems[1].at[w, k], (px, py, c)).start()

    def finish(ins, outs, sems):
        x, y, c = _mesh_pos()
        for w in range(n):
            for k, (px, py) in enumerate(_other_chips(x, y)):
                got = outs[w].at[2 * px + py]
                cp = _remote(ins[w].at[2 * px + py], got, sems[0].at[w, k], sems[1].at[w, k], (px, py, c))
                cp.wait_recv()
                cp.wait_send()

    return _CommJob(list(hs), [jax.ShapeDtypeStruct(h.shape, h.dtype) for h in hs], {},
                    [pltpu.SemaphoreType.DMA((n, 3))] * 2, [send, finish])


def _pair_share(ts, name):
    n = len(ts)

    def body(*refs):
        outs = refs[n:2 * n]
        send_sems, recv_sems = refs[2 * n:]
        x, y, c = _mesh_pos()
        sends = []
        for w in range(n):
            r2 = ts[w].shape[0] // 2
            mine = outs[w].at[pl.ds(c * r2, r2)]
            rc = _remote(mine, mine, send_sems.at[w], recv_sems.at[w], (x, y, 1 - c))
            rc.start()
            sends.append(rc)
        for w in range(n):
            r2 = ts[w].shape[0] // 2
            theirs = outs[w].at[pl.ds((1 - c) * r2, r2)]
            _remote(theirs, theirs, send_sems.at[w], recv_sems.at[w], (x, y, 1 - c)).wait_recv()
            sends[w].wait_send()

    return pl.pallas_call(
        body, name=name, in_specs=_hbm(n), out_specs=_hbm(n),
        out_shape=[jax.ShapeDtypeStruct(t.shape, F32) for t in ts],
        input_output_aliases={w: w for w in range(n)},
        scratch_shapes=[pltpu.SemaphoreType.DMA((n,))] * 2,
    )(*ts)


def _slab_tile(rows):
    return next(t for t in (512, 256, 176, 128, 64, 32, 16) if rows % t == 0)


def _sum_pair(grad, land, pos, name):
    _, r2, cols = land.shape
    tr = _slab_tile(r2)
    nt = r2 // tr

    def kern(pos_ref, a_ref, b_ref, o_ref):
        o_ref[...] = (a_ref[...] + b_ref[...]).astype(BF16)

    spec = pl.BlockSpec((None, tr, cols), lambda j, i, p: (j, i, 0))
    return pl.pallas_call(
        kern, name=name, out_shape=jax.ShapeDtypeStruct(land.shape, BF16),
        grid_spec=_scalar_spec((N_CHIPS, nt), [pl.BlockSpec((None, tr, cols), lambda j, i, p: (j, p[0] * nt + i, 0)), spec], spec),
        compiler_params=_cparams("parallel", "parallel"),
    )(pos, grad, land)


def _sum_chips(hs, land, pos, name):
    _, r2, cols = land.shape
    tr = _slab_tile(r2)
    nt = r2 // tr

    def kern(pos_ref, h_ref, l_ref, o_ref):
        acc = jnp.zeros((tr, cols), F32)
        own = h_ref[...].astype(F32)
        for k in range(N_CHIPS):
            acc = acc + jnp.where(pos_ref[1] == k, own, l_ref[k].astype(F32))
        o_ref[...] = acc

    return pl.pallas_call(
        kern, name=name, out_shape=jax.ShapeDtypeStruct((2 * r2, cols), F32),
        grid_spec=_scalar_spec((nt,), [pl.BlockSpec((None, tr, cols), lambda i, p: (p[1], i, 0)),
                                       pl.BlockSpec((N_CHIPS, tr, cols), lambda i, p: (0, i, 0))],
                               pl.BlockSpec((tr, cols), lambda i, p: (p[0] * nt + i, 0))),
        compiler_params=_cparams("parallel"),
    )(pos, hs, land)


class _ReduceScatter:
    def __init__(self, grads, pos, tag):
        self.grads, self.pos, self.tag = list(grads), pos, tag

    def swap_job(self):
        return _pair_swap_job(self.grads)

    def after_swap(self, land):
        self.hs = [_sum_pair(g, l, self.pos, f"grads_pair_sum_{self.tag}{w}") for w, (g, l) in enumerate(zip(self.grads, land))]

    def exchange_job(self):
        return _chip_exchange_job(self.hs)

    def after_exchange(self, land2):
        ts = [_sum_chips(h, l, self.pos, f"grads_chip_sum_{self.tag}{w}") for w, (h, l) in enumerate(zip(self.hs, land2))]
        return _pair_share(ts, f"grads_pair_share_{self.tag}")

    def run(self):
        self.after_swap(_run_job(self.swap_job(), f"grads_pair_swap_{self.tag}"))
        return self.after_exchange(_run_job(self.exchange_job(), f"grads_chip_exchange_{self.tag}"))


EARLY_WEIGHTS = ("ada0", "attn_qkv")
LATE_WEIGHTS = ("ada1", "ffn_in0", "ffn_in1", "ffn_out0", "ffn_out1", "attn_o", "ret_qkvg", "ret_o")
LAYER1_GRADS = ("ffn_out1", "ffn_in1", "ret_o", "ret_qkvg", "ada1")
LAYER0_GRADS = ("ffn_out0", "ffn_in0", "attn_o", "attn_qkv", "ada0")


def _fill_weights(wts, full):
    for name, w in full.items():
        if name[:-1] in ("ada", "ffn_in"):
            wts[name[:-1]][int(name[-1])] = w
        elif name[:-1] == "ffn_out":
            wts["ffn_out"][int(name[-1])] = w.reshape(-1, w.shape[2])
        elif name in ("attn_o", "ret_o"):
            wts[name] = w.reshape(-1, w.shape[2])
        else:
            wts[name] = w


class _StepPlan:
    def __init__(self, placed, pos):
        self.placed, self.pos = placed, pos
        self.layer1 = None
        self.layer1_reduced = None

    def gather_job(self):
        return _gather_job([self.placed[k] for k in LATE_WEIGHTS])

    def late_weights(self, outs, wts):
        _fill_weights(wts, dict(zip(LATE_WEIGHTS, outs)))

    def start_layer1(self, big):
        self.layer1 = _ReduceScatter([big[k] for k in LAYER1_GRADS], self.pos, "l1_")


def _all_reduce_small(v, name):
    def body(v_ref, o_ref, land_ref, send_sems, recv_sems):
        x, y, c = _mesh_pos()
        me = 4 * x + 2 * y + c
        land_ref[me] = v_ref[...]
        for t in range(N_DEVICES):
            @pl.when(t != me)
            def _(t=t):
                _remote(v_ref, land_ref.at[me], send_sems.at[t], recv_sems.at[me], (t // 4, (t // 2) % 2, t % 2)).start()
        for t in range(N_DEVICES):
            @pl.when(t != me)
            def _(t=t):
                _remote(v_ref, land_ref.at[t], send_sems.at[t], recv_sems.at[t], (t // 4, (t // 2) % 2, t % 2)).wait()
        acc = land_ref[0]
        for t in range(1, N_DEVICES):
            acc = acc + land_ref[t]
        o_ref[...] = acc

    vmem = pl.BlockSpec(memory_space=pltpu.VMEM)
    return pl.pallas_call(
        body, name=name, in_specs=[vmem], out_specs=vmem, out_shape=jax.ShapeDtypeStruct(v.shape, F32),
        scratch_shapes=[pltpu.VMEM((N_DEVICES,) + v.shape, F32), pltpu.SemaphoreType.DMA((N_DEVICES,)),
                        pltpu.SemaphoreType.DMA((N_DEVICES,))],
    )(v)


SMALL_ROWS = 24


def _pack_small(small, dlogit):
    d = D_MODEL
    misc = jnp.zeros((d,), F32)
    misc = misc.at[0:HEAD_DIM].set(small["q_norm"]).at[128:128 + HEAD_DIM].set(small["k_norm"])
    misc = misc.at[256:256 + N_HEADS].set(small["sink"]).at[384:384 + 2 * RET_HEADS].set(dlogit.reshape(-1))
    rows = [small["ada_b0"].reshape(6, d), small["ada_b1"].reshape(6, d), small["norm1_g0"][None], small["norm1_g1"][None],
            small["norm2_g0"][None], small["norm2_g1"][None], small["c_ctx"][None], small["gn_g"].reshape(2, d), misc[None]]
    buf = jnp.concatenate(rows, axis=0)
    return jnp.concatenate([buf, jnp.zeros((SMALL_ROWS - buf.shape[0], d), F32)], axis=0)


def _unpack_small(buf):
    d = D_MODEL
    misc = buf[19]
    return dict(ada_b=buf[0:12].reshape(2, 6 * d), norm1_g=buf[12:14], norm2_g=buf[14:16], c_ctx=buf[16],
                gn_g=buf[17:19].reshape(2 * d), q_norm=misc[0:HEAD_DIM], k_norm=misc[128:128 + HEAD_DIM],
                sink=misc[256:256 + N_HEADS], decay=misc[384:384 + 2 * RET_HEADS])


def kernel(x, c, ctx, c_ctx, ada_w, ada_b, norm1_g, norm2_g, ffn_w_in, ffn_w_out, attn_w_qkv, attn_q_norm, attn_k_norm, attn_sink, attn_w_o, ret_w_qkvg, ret_decay_logit, ret_gn_g, ret_w_o, loss_target, m_c_ctx, m_ada_w, m_ada_b, m_norm1_g, m_norm2_g, m_ffn_w_in, m_ffn_w_out, m_attn_w_qkv, m_attn_q_norm, m_attn_k_norm, m_attn_sink, m_attn_w_o, m_ret_w_qkvg, m_ret_decay_logit, m_ret_gn_g, m_ret_w_o, v_c_ctx, v_ada_w, v_ada_b, v_norm1_g, v_norm2_g, v_ffn_w_in, v_ffn_w_out, v_attn_w_qkv, v_attn_q_norm, v_attn_k_norm, v_attn_sink, v_attn_w_o, v_ret_w_qkvg, v_ret_decay_logit, v_ret_gn_g, v_ret_w_o):
    xi, yi, ci = _mesh_pos()
    chip = 2 * xi + yi
    nb, s, d = x.shape
    gn_shard = ret_gn_g.shape[1]

    shards = dict(ada0=(ada_w, 0), ada1=(ada_w, 1), ffn_in0=(ffn_w_in, 0), ffn_in1=(ffn_w_in, 1), ffn_out0=(ffn_w_out, 0),
                  ffn_out1=(ffn_w_out, 1), attn_qkv=(attn_w_qkv, 0), attn_o=(attn_w_o, 0), ret_qkvg=(ret_w_qkvg, 0), ret_o=(ret_w_o, 0))
    names = list(shards)
    pos = jnp.stack([ci, chip]).astype(jnp.int32)
    placed = {k: _place_shard(*shards[k], pos, f"place_{k}") for k in names}
    early = _run_job(_gather_job([placed[k] for k in EARLY_WEIGHTS]), "gather_early_weights")
    gn_mine = jnp.where(ci == 0, ret_gn_g[0], jnp.zeros_like(ret_gn_g[0]))
    gn_place = lax.dynamic_update_slice(jnp.zeros((RET_VWIDTH,), F32), gn_mine, (chip * gn_shard,))
    gn_full = _all_reduce_small(gn_place.reshape(2, d), "gather_gn_gain").reshape(RET_VWIDTH)

    wts = dict(ada=[None, None], ffn_in=[None, None], ffn_out=[None, None], attn_qkv=None, attn_o=None, ret_qkvg=None, ret_o=None)
    _fill_weights(wts, dict(zip(EARLY_WEIGHTS, early)))
    plan = _StepPlan(placed, pos)
    decay_logit = ret_decay_logit[0]
    sp = dict(c_ctx=c_ctx, ada_b=ada_b, norm1_g=norm1_g, norm2_g=norm2_g, q_norm=attn_q_norm[0], k_norm=attn_k_norm[0],
              sink=attn_sink[0], log_g=jax.nn.log_sigmoid(decay_logit), gn_g=gn_full)
    loss_part, dz, big, small = _local_step(x, c, ctx, loss_target, sp, wts, plan)

    loss = lax.psum(loss_part[0, 0], ("x", "y", "c"))
    grad_x = dz.reshape(nb, s, d)

    dlogit = small["log_g"] * jax.nn.sigmoid(-decay_logit)
    sg = _unpack_small(_all_reduce_small(_pack_small(small, dlogit), "reduce_small_grads"))
    reduced = dict(zip(LAYER1_GRADS, plan.layer1_reduced))
    reduced.update(zip(LAYER0_GRADS, _ReduceScatter([big[k] for k in LAYER0_GRADS], pos, "l0_").run()))

    grads = dict(
        c_ctx=sg["c_ctx"], ada_w=jnp.stack([reduced["ada0"], reduced["ada1"]]), ada_b=sg["ada_b"], norm1_g=sg["norm1_g"],
        norm2_g=sg["norm2_g"], ffn_w_in=jnp.stack([reduced["ffn_in0"], reduced["ffn_in1"]]),
        ffn_w_out=jnp.stack([reduced["ffn_out0"], reduced["ffn_out1"]]), attn_w_qkv=reduced["attn_qkv"][None],
        attn_q_norm=sg["q_norm"][None], attn_k_norm=sg["k_norm"][None], attn_sink=sg["sink"][None],
        attn_w_o=reduced["attn_o"][None], ret_w_qkvg=reduced["ret_qkvg"][None], ret_decay_logit=sg["decay"].reshape(1, 2, RET_HEADS),
        ret_gn_g=lax.dynamic_slice(sg["gn_g"], (chip * gn_shard,), (gn_shard,))[None], ret_w_o=reduced["ret_o"][None])
    params = dict(c_ctx=(c_ctx, m_c_ctx, v_c_ctx), ada_w=(ada_w, m_ada_w, v_ada_w), ada_b=(ada_b, m_ada_b, v_ada_b),
                  norm1_g=(norm1_g, m_norm1_g, v_norm1_g), norm2_g=(norm2_g, m_norm2_g, v_norm2_g),
                  ffn_w_in=(ffn_w_in, m_ffn_w_in, v_ffn_w_in), ffn_w_out=(ffn_w_out, m_ffn_w_out, v_ffn_w_out),
                  attn_w_qkv=(attn_w_qkv, m_attn_w_qkv, v_attn_w_qkv), attn_q_norm=(attn_q_norm, m_attn_q_norm, v_attn_q_norm),
                  attn_k_norm=(attn_k_norm, m_attn_k_norm, v_attn_k_norm), attn_sink=(attn_sink, m_attn_sink, v_attn_sink),
                  attn_w_o=(attn_w_o, m_attn_w_o, v_attn_w_o), ret_w_qkvg=(ret_w_qkvg, m_ret_w_qkvg, v_ret_w_qkvg),
                  ret_decay_logit=(ret_decay_logit, m_ret_decay_logit, v_ret_decay_logit),
                  ret_gn_g=(ret_gn_g, m_ret_gn_g, v_ret_gn_g), ret_w_o=(ret_w_o, m_ret_w_o, v_ret_w_o))
    order = list(params)
    deltas, new_m, new_v = [], [], []
    for k in order:
        w, m, v = params[k]
        g = grads[k].reshape(w.shape)
        grads[k] = g
        flat = (-1, w.shape[-1]) if w.ndim > 1 else (1, -1)
        if k == "ret_decay_logit":
            flat = (1, -1)
        dw, nm, nv = _adamw(w.reshape(flat), g.reshape(flat), m.reshape(flat), v.reshape(flat), f"adamw_{k}")
        deltas.append(dw.reshape(w.shape))
        new_m.append(nm.reshape(w.shape))
        new_v.append(nv.reshape(w.shape))
    return (loss, grad_x, *[grads[k] for k in order], *deltas, *new_m, *new_v)
```

```python
import functools

import jax
import jax.numpy as jnp
from jax import lax
from jax.experimental import pallas as pl
from jax.experimental.pallas import tpu as pltpu

F32 = jnp.float32
BF16 = jnp.bfloat16

D_MODEL = 1024
N_HEADS = 16
N_KV_HEADS = 4
HEAD_DIM = 64
WINDOW = 128
ATTN_BLOCK = 128
BAND = ATTN_BLOCK + 2 * WINDOW
RET_HEADS = 4
RET_QK_DIM = 256
RET_V_DIM = 512
RET_VWIDTH = 2048
RET_CHUNK = 128
D_FF = 2816
GRID_W = 64
ROPE_BASE = 10000.0
EPS = 1e-6
NEG_INF = -1e30
LANES = 128

ADAM_LR = 0.001
ADAM_B1 = 0.9
ADAM_B2 = 0.999
ADAM_EPS = 1e-08
ADAM_WD = 0.01
ADAM_STEP = 10

VMEM_LIMIT_BYTES = 56 * 1024 * 1024
MESH = pl.DeviceIdType.MESH
N_CHIPS = 4


def _cparams(*sem):
    return pltpu.CompilerParams(dimension_semantics=sem, vmem_limit_bytes=VMEM_LIMIT_BYTES)


_DIMS = {"nn": ((1,), (0,)), "nt": ((1,), (1,)), "tn": ((0,), (0,))}


def _dot(a, b, form):
    return lax.dot_general(a.astype(BF16), b.astype(BF16), (_DIMS[form], ((), ())), preferred_element_type=F32)


@functools.partial(jax.custom_vjp, nondiff_argnums=(2,))
def _mm(a, b, form):
    return _dot(a, b, form)


def _mm_fwd(a, b, form):
    return _dot(a, b, form), (a, b)


def _mm_bwd(form, res, ct):
    a, b = res
    if form == "nn":
        da, db = _dot(ct, b, "nt"), _dot(a, ct, "tn")
    elif form == "nt":
        da, db = _dot(ct, b, "nn"), _dot(ct, a, "tn")
    else:
        da, db = _dot(b, ct, "nt"), _dot(a, ct, "nn")
    return da.astype(a.dtype), db.astype(b.dtype)


_mm.defvjp(_mm_fwd, _mm_bwd)


def _swap_halves(x, half):
    w = x.shape[-1]
    lane = lax.broadcasted_iota(jnp.int32, x.shape, x.ndim - 1)
    return jnp.where(lane % (2 * half) < half, pltpu.roll(x, w - half, x.ndim - 1), pltpu.roll(x, half, x.ndim - 1))


@functools.partial(jax.custom_vjp, nondiff_argnums=(1,))
def _rot(x, half):
    return _swap_halves(x, half)


def _rot_fwd(x, half):
    return _swap_halves(x, half), None


def _rot_bwd(half, _, ct):
    return (_swap_halves(ct, half),)


_rot.defvjp(_rot_fwd, _rot_bwd)


def _rope(x, cos, sin_signed, half):
    return x * cos + _rot(x, half) * sin_signed


def _head_mean_square(x):
    r = lax.broadcasted_iota(jnp.int32, (LANES, LANES), 0) // HEAD_DIM
    c = lax.broadcasted_iota(jnp.int32, (LANES, LANES), 1) // HEAD_DIM
    g = jnp.where(r == c, 1.0 / HEAD_DIM, 0.0).astype(F32)
    return jnp.dot(x * x, g, precision=lax.Precision.HIGHEST, preferred_element_type=F32)


def _qk_chunk(x, gain, cos, sin_signed, scale):
    y = x * lax.rsqrt(_head_mean_square(x) + EPS) * gain
    return _rope(y, cos, sin_signed, HEAD_DIM // 4) * scale


def _sigmoid(x):
    return 1.0 / (1.0 + jnp.exp(-x))


def _silu(x):
    return x * _sigmoid(x)


def _mm_nn(a, w, out_dtype, name, tm, tn, tk, bias=None):
    m, k_dim = a.shape
    if w.ndim == 3:
        n = w.shape[0] * w.shape[2]
        per = w.shape[2] // tn
        assert w.shape[2] % tn == 0
        w_spec = pl.BlockSpec((None, tk, tn), lambda i, j, k: (j // per, k, j % per))
    else:
        n = w.shape[1]
        w_spec = pl.BlockSpec((tk, tn), lambda i, j, k: (k, j))
    assert m % tm == 0 and n % tn == 0 and k_dim % tk == 0, (name, a.shape, w.shape, tm, tn, tk)
    nk = k_dim // tk
    has_bias = bias is not None

    def body(*refs):
        a_ref, w_ref = refs[0], refs[1]
        b_ref = refs[2] if has_bias else None
        o_ref, acc_ref = (refs[-1], None) if nk == 1 else (refs[-2], refs[-1])
        if nk == 1:
            part = jnp.dot(a_ref[...].astype(BF16), w_ref[...], preferred_element_type=F32)
            o_ref[...] = (part + b_ref[...] if has_bias else part).astype(out_dtype)
            return
        k = pl.program_id(2)

        @pl.when(k == 0)
        def _():
            acc_ref[...] = jnp.zeros_like(acc_ref)

        acc_ref[...] += jnp.dot(a_ref[...].astype(BF16), w_ref[...], preferred_element_type=F32)

        @pl.when(k == nk - 1)
        def _():
            r = acc_ref[...]
            if has_bias:
                r = r + b_ref[...]
            o_ref[...] = r.astype(out_dtype)

    in_specs = [pl.BlockSpec((tm, tk), lambda i, j, k: (i, k)), w_spec]
    args = [a, w]
    if has_bias:
        in_specs.append(pl.BlockSpec((1, tn), lambda i, j, k: (0, j)))
        args.append(bias)
    return pl.pallas_call(
        body, name=name, grid=(m // tm, n // tn, nk), in_specs=in_specs,
        out_specs=pl.BlockSpec((tm, tn), lambda i, j, k: (i, j)),
        out_shape=jax.ShapeDtypeStruct((m, n), out_dtype),
        scratch_shapes=[pltpu.VMEM((tm, tn), F32)] if nk > 1 else [],
        compiler_params=_cparams("parallel", "parallel", "arbitrary"),
    )(*args)


def _mm_nt(a, w, out_dtype, name, tm, tn, tk):
    if a.ndim == 3:
        planes, m, plane_w = a.shape
        c_dim = planes * plane_w
        a_per = plane_w // tk
        assert plane_w % tk == 0
        a_spec = pl.BlockSpec((None, tm, tk), lambda i, j, k: (k // a_per, i, k % a_per))
    else:
        m, c_dim = a.shape
        a_spec = pl.BlockSpec((tm, tk), lambda i, j, k: (i, k))
    if w.ndim == 3:
        k_out = w.shape[1]
        per = w.shape[2] // tk
        assert w.shape[2] % tk == 0 and w.shape[0] * w.shape[2] == c_dim
        w_spec = pl.BlockSpec((None, tn, tk), lambda i, j, k: (k // per, j, k % per))
    else:
        k_out = w.shape[0]
        assert w.shape[1] == c_dim
        w_spec = pl.BlockSpec((tn, tk), lambda i, j, k: (j, k))
    assert m % tm == 0 and k_out % tn == 0 and c_dim % tk == 0, (name, a.shape, w.shape, tm, tn, tk)
    nk = c_dim // tk

    def body(a_ref, w_ref, o_ref, acc_ref=None):
        if nk == 1:
            o_ref[...] = _dot(a_ref[...], w_ref[...], "nt").astype(out_dtype)
            return
        k = pl.program_id(2)

        @pl.when(k == 0)
        def _():
            acc_ref[...] = jnp.zeros_like(acc_ref)

        acc_ref[...] += _dot(a_ref[...], w_ref[...], "nt")

        @pl.when(k == nk - 1)
        def _():
            o_ref[...] = acc_ref[...].astype(out_dtype)

    return pl.pallas_call(
        body, name=name, grid=(m // tm, k_out // tn, nk),
        in_specs=[a_spec, w_spec],
        out_specs=pl.BlockSpec((tm, tn), lambda i, j, k: (i, j)),
        out_shape=jax.ShapeDtypeStruct((m, k_out), out_dtype),
        scratch_shapes=[pltpu.VMEM((tm, tn), F32)] if nk > 1 else [],
        compiler_params=_cparams("parallel", "parallel", "arbitrary"),
    )(a, w)


def _mm_tn(a, b, name, tm, tn, tk, shards=None):
    r, k_dim = a.shape
    if b.ndim == 3:
        n = b.shape[0] * b.shape[2]
        b_per = b.shape[2] // tn
        assert b.shape[2] % tn == 0
        b_spec = pl.BlockSpec((None, tk, tn), lambda i, j, k: (j // b_per, k, j % b_per))
    else:
        n = b.shape[1]
        b_spec = pl.BlockSpec((tk, tn), lambda i, j, k: (k, j))
    assert r % tk == 0 and k_dim % tm == 0 and n % tn == 0, (name, a.shape, b.shape, tm, tn, tk)
    nk = r // tk
    if shards:
        per = n // shards // tn
        assert n % (shards * tn) == 0
        out_shape = jax.ShapeDtypeStruct((shards, k_dim, n // shards), F32)
        out_spec = pl.BlockSpec((None, tm, tn), lambda i, j, k: (j // per, i, j % per))
    else:
        out_shape = jax.ShapeDtypeStruct((k_dim, n), F32)
        out_spec = pl.BlockSpec((tm, tn), lambda i, j, k: (i, j))

    def body(a_ref, b_ref, o_ref):
        k = pl.program_id(2)

        @pl.when(k == 0)
        def _():
            o_ref[...] = jnp.zeros_like(o_ref)

        o_ref[...] += _dot(a_ref[...], b_ref[...], "tn")

    return pl.pallas_call(
        body, name=name, grid=(k_dim // tm, n // tn, nk),
        in_specs=[pl.BlockSpec((tk, tm), lambda i, j, k: (k, i)), b_spec],
        out_specs=out_spec, out_shape=out_shape,
        compiler_params=_cparams("parallel", "parallel", "arbitrary"),
    )(a, b)


class _Carrier:
    def __init__(self, job, n_in, n_out, n_scratch):
        self.job, self.n_in, self.n_out, self.n_scratch = job, n_in, n_out, n_scratch
        self.ji = len(job.inputs) if job else 0
        self.jo = len(job.out_shapes) if job else 0

    def operands(self):
        return list(self.job.inputs) if self.job else []

    def in_specs(self):
        return [pl.BlockSpec(memory_space=pl.ANY)] * self.ji

    def out_specs(self):
        return [pl.BlockSpec(memory_space=pl.ANY)] * self.jo

    def out_shapes(self):
        return list(self.job.out_shapes) if self.job else []

    def scratch(self):
        return list(self.job.sem_shapes) if self.job else []

    def aliases(self):
        return {self.n_in + a: self.n_out + b for a, b in self.job.aliases.items()} if self.job else {}

    def split(self, refs):
        a = self.n_in
        b = a + self.ji
        c = b + self.n_out
        d = c + self.jo
        e = d + self.n_scratch
        return list(refs[:a]) + list(refs[b:c]) + list(refs[d:e]), (refs[a:b], refs[c:d], refs[e:])

    def run(self, job_refs, step, steps):
        if not self.job:
            return
        for stage, mark in zip(self.job.stages, _job_marks(self.job, steps)):
            pl.when(step == mark)(functools.partial(stage, *job_refs))

    def results(self, res):
        res = list(res)
        return res[:self.n_out], res[self.n_out:]


FFN_ROW_TILE = 768


def _ffn_tile(r):
    return FFN_ROW_TILE if r % FFN_ROW_TILE == 0 else _row_tile(r)


def _ffn_in_swiglu(h, w, name):
    r, k_dim = h.shape
    n4 = w.shape[2]
    tm = _ffn_tile(r)

    def body(h_ref, wg_ref, wu_ref, u_ref, a_ref):
        hv = h_ref[...]
        g = jnp.dot(hv, wg_ref[...], preferred_element_type=F32)
        up = jnp.dot(hv, wu_ref[...], preferred_element_type=F32)
        u_ref[0] = g.astype(BF16)
        u_ref[1] = up.astype(BF16)
        a_ref[...] = (_silu(g) * up).astype(BF16)

    return pl.pallas_call(
        body, name=name, grid=(r // tm, 2),
        in_specs=[pl.BlockSpec((tm, k_dim), lambda i, j: (i, 0)),
                  pl.BlockSpec((None, k_dim, n4), lambda i, j: (j, 0, 0)),
                  pl.BlockSpec((None, k_dim, n4), lambda i, j: (j + 2, 0, 0))],
        out_specs=[pl.BlockSpec((2, tm, n4), lambda i, j: (0, i, j)), pl.BlockSpec((tm, n4), lambda i, j: (i, j))],
        out_shape=[jax.ShapeDtypeStruct((2, r, 2 * n4), BF16), jax.ShapeDtypeStruct((r, 2 * n4), BF16)],
        compiler_params=_cparams("parallel", "parallel"),
    )(h, w, w)


def _mm_nn_gate_residual(geo, a, w, z, mod, off, name, norm=None):
    r, k_dim = a.shape
    n = w.shape[1]
    tm = FFN_ROW_TILE if geo.seg % FFN_ROW_TILE == 0 else 256
    tiles = geo.seg // tm
    assert geo.seg % tm == 0 and r == geo.r and n == D_MODEL

    def body(a_ref, w_ref, z_ref, mx_ref, mc_ref, *rest):
        out = jnp.dot(a_ref[...], w_ref[...], preferred_element_type=F32)
        is_x = (pl.program_id(0) % tiles) * tm + lax.broadcasted_iota(jnp.int32, (tm, 1), 0) < geo.s
        zo = z_ref[...] + jnp.where(is_x, mx_ref[:, off:off + n], mc_ref[:, off:off + n]) * out
        if norm:
            g_ref, nx_ref, nc_ref, zo_ref, raw_ref, h_ref = rest
            no = norm[2]
            shift = jnp.where(is_x, nx_ref[:, no:no + n], nc_ref[:, no:no + n])
            scale = jnp.where(is_x, nx_ref[:, no + n:no + 2 * n], nc_ref[:, no + n:no + 2 * n])
            rs = lax.rsqrt(jnp.mean(zo * zo, axis=-1, keepdims=True) + EPS)
            h_ref[...] = ((zo * rs) * g_ref[...] * (1.0 + scale) + shift).astype(BF16)
        else:
            zo_ref, raw_ref = rest
        zo_ref[...] = zo
        raw_ref[...] = out.astype(BF16)

    def mod_specs(m):
        return [pl.BlockSpec((None, 1, m.shape[2]), lambda i: (i // tiles, 0, 0)), pl.BlockSpec((None, 1, m.shape[2]), lambda i: (geo.b, 0, 0))]

    row = pl.BlockSpec((tm, n), lambda i: (i, 0))
    in_specs = [pl.BlockSpec((tm, k_dim), lambda i: (i, 0)), pl.BlockSpec((k_dim, n), lambda i: (0, 0)), row] + mod_specs(mod)
    args = [a, w, z, mod, mod]
    out_specs, out_shape = [row, row], [jax.ShapeDtypeStruct((r, n), F32), jax.ShapeDtypeStruct((r, n), BF16)]
    if norm:
        in_specs += [pl.BlockSpec((1, n), lambda i: (0, 0))] + mod_specs(norm[1])
        args += [norm[0], norm[1], norm[1]]
        out_specs.append(row)
        out_shape.append(jax.ShapeDtypeStruct((r, n), BF16))
    res = pl.pallas_call(body, name=name, grid=(r // tm,), in_specs=in_specs, out_specs=out_specs, out_shape=out_shape,
                         compiler_params=_cparams("parallel"))(*args)
    return res if norm else (*res, None)


def _ffn_out_dx_swiglu_bwd(df, w_out, u, name, job=None):
    r, d = df.shape
    n4 = u.shape[2] // 2
    tm = _ffn_tile(r)
    carrier = _Carrier(job, 3, 1, 0)
    steps = (r // tm) * 2

    def body(*refs):
        (df_ref, w_ref, u_ref, du_ref), job_refs = carrier.split(refs)
        carrier.run(job_refs, pl.program_id(0) * 2 + pl.program_id(1), steps)
        da = _dot(df_ref[...], w_ref[...], "nt")
        g, up = u_ref[0].astype(F32), u_ref[1].astype(F32)
        s = _sigmoid(g)
        du_ref[0] = (da * up * (s * (1.0 + g * (1.0 - s)))).astype(BF16)
        du_ref[1] = (da * (g * s)).astype(BF16)

    res = pl.pallas_call(
        body, name=name, grid=(r // tm, 2),
        in_specs=[pl.BlockSpec((tm, d), lambda i, j: (i, 0)), pl.BlockSpec((n4, d), lambda i, j: (j, 0)),
                  pl.BlockSpec((2, tm, n4), lambda i, j: (0, i, j))] + carrier.in_specs(),
        out_specs=[pl.BlockSpec((2, tm, n4), lambda i, j: (0, i, j))] + carrier.out_specs(),
        out_shape=[jax.ShapeDtypeStruct(u.shape, BF16)] + carrier.out_shapes(),
        scratch_shapes=carrier.scratch(), input_output_aliases=carrier.aliases(),
        compiler_params=_cparams("arbitrary", "arbitrary"),
    )(df, w_out, u, *carrier.operands())
    (du,), extra = carrier.results(res)
    return du, extra


class _Rows:
    def __init__(self, b, s, l):
        self.b, self.s, self.l = b, s, l
        self.seg = s + l
        self.r = b * self.seg


def _rowwise(name, body, geo, tm, ins, outs, job=None):
    seg_blocks, x_blocks = geo.seg // tm, geo.s // tm
    assert geo.seg % tm == 0 and geo.s % tm == 0
    nb = geo.b

    def is_ctx(i):
        return i % seg_blocks >= x_blocks

    in_specs, args = [], []
    for arr, kind in ins:
        args.append(arr)
        if kind == "row":
            in_specs.append(pl.BlockSpec((tm, arr.shape[1]), lambda i: (i, 0)))
        elif kind == "ex":
            in_specs.append(pl.BlockSpec((None, 1, arr.shape[2]), lambda i: (jnp.where(is_ctx(i), nb, i // seg_blocks), 0, 0)))
        elif kind == "full":
            in_specs.append(pl.BlockSpec(arr.shape, lambda i, nd=arr.ndim: (0,) * nd))
        elif kind == "tab":
            in_specs.append(pl.BlockSpec((tm, arr.shape[1]), lambda i: (i % seg_blocks, 0)))
        elif kind == "xrow":
            in_specs.append(pl.BlockSpec(
                (tm, arr.shape[1]), lambda i: ((i // seg_blocks) * x_blocks + jnp.minimum(i % seg_blocks, x_blocks - 1), 0)))
        else:
            _, width, cb = kind
            in_specs.append(pl.BlockSpec((tm, width), lambda i, cb=cb: (i, cb)))
    out_specs, out_shapes = [], []
    for o in outs:
        if o[0] == "row":
            out_specs.append(pl.BlockSpec((tm, o[1]), lambda i: (i, 0)))
            out_shapes.append(jax.ShapeDtypeStruct((geo.r, o[1]), o[2]))
        elif o[0] == "xrow":
            out_specs.append(pl.BlockSpec(
                (tm, o[1]), lambda i: ((i // seg_blocks) * x_blocks + jnp.minimum(i % seg_blocks, x_blocks - 1), 0)))
            out_shapes.append(jax.ShapeDtypeStruct((geo.b * geo.s, o[1]), o[2]))
        elif o[0] == "exacc":
            out_specs.append(pl.BlockSpec((None, 1, o[1]), lambda i: (jnp.where(is_ctx(i), nb, 0) + i // seg_blocks, 0, 0)))
            out_shapes.append(jax.ShapeDtypeStruct((2 * nb, 1, o[1]), F32))
        else:
            out_specs.append(pl.BlockSpec((o[1], o[2]), lambda i: (0, 0)))
            out_shapes.append(jax.ShapeDtypeStruct((o[1], o[2]), F32))
    n_in = len(ins)
    carrier = _Carrier(job, n_in, len(outs), 0)

    def kern(*refs):
        i = pl.program_id(0)
        refs, job_refs = carrier.split(refs)
        carrier.run(job_refs, i, geo.r // tm)
        res = body(i, *[r[...].astype(F32) for r in refs[:n_in]])
        if not isinstance(res, (tuple, list)):
            res = (res,)
        jj = i % seg_blocks
        first_of_part = (jj == 0) | (jj == x_blocks)
        for o, ref, val in zip(outs, refs[n_in:], res):
            if o[0] == "row":
                ref[...] = val.astype(ref.dtype)
            elif o[0] == "xrow":
                @pl.when(jj < x_blocks)
                def _(ref=ref, val=val):
                    ref[...] = val.astype(ref.dtype)
            else:
                first = first_of_part if o[0] == "exacc" else i == 0

                @pl.when(first)
                def _(ref=ref, val=val):
                    ref[...] = val

                @pl.when(jnp.logical_not(first))
                def _(ref=ref, val=val):
                    ref[...] += val

    res = pl.pallas_call(
        kern, name=name, grid=(geo.r // tm,), in_specs=in_specs + carrier.in_specs(), out_specs=out_specs + carrier.out_specs(),
        out_shape=out_shapes + carrier.out_shapes(), scratch_shapes=carrier.scratch(), input_output_aliases=carrier.aliases(),
        compiler_params=_cparams("arbitrary"),
    )(*args, *carrier.operands())
    own, extra = carrier.results(res)
    if job:
        return (*own, extra)
    return own[0] if len(own) == 1 else own


def _colsum(v):
    return jnp.sum(v, axis=0, keepdims=True)


def _norm_mod(geo, z, gain, mod, off, name):
    d = D_MODEL

    def body(i, zv, g, m):
        r = lax.rsqrt(jnp.mean(zv * zv, axis=-1, keepdims=True) + EPS)
        return (zv * r) * g * (1.0 + m[:, off + d:off + 2 * d]) + m[:, off:off + d]

    return _rowwise(name, body, geo, 256, [(z, "row"), (gain, "full"), (mod, "ex")], [("row", d, BF16)])


def _norm_mod_bwd(geo, z, gain, mod, off, dh, dz_skip, name, gated=None, latent_only=False, job=None):
    d = D_MODEL

    def body(i, zv, g, m, dhv, skip, *rest):
        r = lax.rsqrt(jnp.mean(zv * zv, axis=-1, keepdims=True) + EPS)
        n = zv * r
        dng = dhv * (1.0 + m[:, off + d:off + 2 * d])
        dn = dng * g
        dz = r * (dn - n * jnp.mean(dn * n, axis=-1, keepdims=True)) + skip
        res = (dz, _colsum(dhv), _colsum(dhv * (n * g)), _colsum(dng * n))
        if gated:
            ov, gm = rest
            res += (dz * gm[:, gated[2]:gated[2] + d], _colsum(dz * ov))
        return res

    ins = [(z, "row"), (gain, "full"), (mod, "ex"), (dh, "row"), (dz_skip, "row")]
    outs = [("xrow" if latent_only else "row", d, F32), ("exacc", d), ("exacc", d), ("gacc", 1, d)]
    if gated:
        ins += [(gated[0], "row"), (gated[1], "ex")]
        outs += [("row", d, BF16), ("exacc", d)]
    return _rowwise(name, body, geo, 256, ins, outs, job)


def _loss_head(geo, z, target, out, mod, off, name):
    seg_blocks, x_blocks = geo.seg // 256, geo.s // 256
    d = D_MODEL

    def body(i, zv, tv, ov, m):
        keep = jnp.where(i % seg_blocks >= x_blocks, 0.0, 1.0)
        err = (zv - tv) * keep
        part = 0.5 * jnp.sum(jnp.mean(err * err, axis=-1, keepdims=True), axis=0, keepdims=True)
        dz = err * (1.0 / d)
        return dz, jnp.broadcast_to(part, (1, LANES)), dz * m[:, off:off + d], _colsum(dz * ov)

    return _rowwise(name, body, geo, 256, [(z, "row"), (target, "xrow"), (out, "row"), (mod, "ex")],
                    [("row", d, F32), ("gacc", 1, LANES), ("row", d, BF16), ("exacc", d)])


Q_SCALE = HEAD_DIM ** -0.5
N_QK_CHUNKS = (N_HEADS + N_KV_HEADS) * HEAD_DIM // LANES
N_Q_CHUNKS = N_HEADS * HEAD_DIM // LANES


def _attn_prep(geo, proj, cos, sin_signed, q_gain, k_gain, name):
    def body(i, p, cs, sn, qg, kg):
        outs = []
        for ch in range(N_QK_CHUNKS):
            is_q = ch < N_Q_CHUNKS
            outs.append(_qk_chunk(p[:, ch * LANES:(ch + 1) * LANES], qg if is_q else kg, cs, sn, Q_SCALE if is_q else 1.0))
        outs.append(p[:, N_QK_CHUNKS * LANES:])
        return jnp.concatenate(outs, axis=1)

    return _rowwise(name, body, geo, 256, [(proj, "row"), (cos, "tab"), (sin_signed, "tab"), (q_gain, "full"), (k_gain, "full")],
                    [("row", proj.shape[1], BF16)])


def _attn_prep_bwd(geo, proj, cos, sin_signed, q_gain, k_gain, dq, dkv, name):
    kw = N_KV_HEADS * HEAD_DIM

    def body(i, p, cs, sn, qg, kg, dqv, dkvv):
        outs = []
        dgains = [jnp.zeros((1, LANES), F32), jnp.zeros((1, LANES), F32)]
        for ch in range(N_QK_CHUNKS):
            is_q = ch < N_Q_CHUNKS
            scale = Q_SCALE if is_q else 1.0
            ct = dqv[:, ch * LANES:(ch + 1) * LANES] if is_q else dkvv[:, (ch - N_Q_CHUNKS) * LANES:(ch - N_Q_CHUNKS + 1) * LANES]
            _, vjp = jax.vjp(lambda xx, gg, scale=scale: _qk_chunk(xx, gg, cs, sn, scale),
                             p[:, ch * LANES:(ch + 1) * LANES], qg if is_q else kg)
            dx, dg = vjp(ct)
            outs.append(dx)
            dgains[0 if is_q else 1] = dgains[0 if is_q else 1] + dg
        outs.append(dkvv[:, kw:])
        return jnp.concatenate(outs, axis=1), dgains[0], dgains[1]

    return _rowwise(name, body, geo, 256,
                    [(proj, "row"), (cos, "tab"), (sin_signed, "tab"), (q_gain, "full"), (k_gain, "full"), (dq, "row"), (dkv, "row")],
                    [("row", proj.shape[1], BF16), ("gacc", 1, LANES), ("gacc", 1, LANES)])


def _attn_geometry(geo):
    assert geo.s % ATTN_BLOCK == 0 and geo.l % ATTN_BLOCK == 0 and geo.seg >= BAND
    return geo.seg // ATTN_BLOCK, geo.s // ATTN_BLOCK


def _attn_mask(j, s0, geo):
    r = lax.broadcasted_iota(jnp.int32, (ATTN_BLOCK, BAND), 0)
    n = lax.broadcasted_iota(jnp.int32, (ATTN_BLOCK, BAND), 1)
    dist = (s0 - j * ATTN_BLOCK) + n - r
    return (jnp.abs(dist) <= WINDOW) & (s0 + n < geo.s)


def _attn_probs(q, keys, valid, n_ctx, sink):
    s = _dot(q, keys, "nt")
    if valid is not None:
        s = jnp.concatenate([s[:, :n_ctx], jnp.where(valid, s[:, n_ctx:], NEG_INF)], axis=1)
    m = jnp.maximum(jnp.max(s, axis=-1, keepdims=True), sink)
    e, e_sink = jnp.exp(s - m), jnp.exp(sink - m)
    inv = 1.0 / (jnp.sum(e, axis=-1, keepdims=True) + e_sink)
    return e * inv, e_sink * inv


def _attn_keys(ref, s0, geo, with_band):
    ctx = ref[geo.s:geo.seg, :]
    return jnp.concatenate([ctx, ref[pl.ds(s0, BAND), :]], axis=0) if with_band else ctx


def _attention(geo, qkv, sink, name, job=None):
    n_blocks, n_x_blocks = _attn_geometry(geo)
    qw, kw = N_HEADS * HEAD_DIM, N_KV_HEADS * HEAD_DIM
    group = N_HEADS // N_KV_HEADS
    carrier = _Carrier(job, 4, 1, 0)

    def kern(*refs):
        (sink_ref, q_ref, k_ref, v_ref, o_ref), job_refs = carrier.split(refs)
        j = pl.program_id(1)
        carrier.run(job_refs, pl.program_id(0) * n_blocks + j, geo.b * n_blocks)
        s0 = pl.multiple_of(jnp.clip((j - 1) * ATTN_BLOCK, 0, geo.seg - BAND), ATTN_BLOCK)

        def heads(with_band):
            valid = _attn_mask(j, s0, geo) if with_band else None
            k_all, v_all = _attn_keys(k_ref, s0, geo, with_band), _attn_keys(v_ref, s0, geo, with_band)
            for h in range(N_HEADS):
                kv = slice((h // group) * HEAD_DIM, (h // group + 1) * HEAD_DIM)
                p, _ = _attn_probs(q_ref[:, h * HEAD_DIM:(h + 1) * HEAD_DIM], k_all[:, kv], valid, geo.l, sink_ref[h])
                o_ref[:, h * HEAD_DIM:(h + 1) * HEAD_DIM] = _dot(p, v_all[:, kv], "nn").astype(BF16)

        pl.when(j < n_x_blocks)(lambda: heads(True))
        pl.when(j >= n_x_blocks)(lambda: heads(False))

    res = pl.pallas_call(
        kern, name=name, grid=(geo.b, n_blocks),
        in_specs=[pl.BlockSpec(memory_space=pltpu.SMEM),
                  pl.BlockSpec((ATTN_BLOCK, qw), lambda b, j: (b * n_blocks + j, 0)),
                  pl.BlockSpec((geo.seg, kw), lambda b, j: (b, qw // kw)),
                  pl.BlockSpec((geo.seg, kw), lambda b, j: (b, qw // kw + 1))] + carrier.in_specs(),
        out_specs=[pl.BlockSpec((ATTN_BLOCK, qw), lambda b, j: (b * n_blocks + j, 0))] + carrier.out_specs(),
        out_shape=[jax.ShapeDtypeStruct((geo.r, qw), BF16)] + carrier.out_shapes(),
        scratch_shapes=carrier.scratch(), input_output_aliases=carrier.aliases(),
        compiler_params=_cparams("arbitrary", "arbitrary"),
    )(sink, qkv, qkv, qkv, *carrier.operands())
    (o,), extra = carrier.results(res)
    return o, extra


def _attention_bwd(geo, qkv, sink, do, name, job=None):
    n_blocks, n_x_blocks = _attn_geometry(geo)
    qw, kw = N_HEADS * HEAD_DIM, N_KV_HEADS * HEAD_DIM
    group = N_HEADS // N_KV_HEADS

    carrier = _Carrier(job, 5, 3, 1)

    def kern(*refs):
        (sink_ref, q_ref, k_ref, v_ref, do_ref, dq_ref, dkv_out_ref, dsink_ref, dkv_ref), job_refs = carrier.split(refs)
        b, j = pl.program_id(0), pl.program_id(1)
        carrier.run(job_refs, b * n_blocks + j, geo.b * n_blocks)
        s0 = pl.multiple_of(jnp.clip((j - 1) * ATTN_BLOCK, 0, geo.seg - BAND), ATTN_BLOCK)

        @pl.when(j == 0)
        def _():
            dkv_ref[...] = jnp.zeros_like(dkv_ref)

        @pl.when((j == 0) & (b == 0))
        def _():
            dsink_ref[...] = jnp.zeros_like(dsink_ref)

        def heads(with_band):
            valid = _attn_mask(j, s0, geo) if with_band else None
            k_all, v_all = _attn_keys(k_ref, s0, geo, with_band), _attn_keys(v_ref, s0, geo, with_band)
            for g in range(N_KV_HEADS):
                kv = slice(g * HEAD_DIM, (g + 1) * HEAD_DIM)
                keys, vals = k_all[:, kv], v_all[:, kv]
                group_heads = [slice(h * HEAD_DIM, (h + 1) * HEAD_DIM) for h in range(g * group, (g + 1) * group)]
                ds_rows, p_rows = [], []
                for h, hs in zip(range(g * group, (g + 1) * group), group_heads):
                    dout = do_ref[:, hs]
                    p, p_sink = _attn_probs(q_ref[:, hs], keys, valid, geo.l, sink_ref[h])
                    dp = _dot(dout, vals, "nt")
                    dsum = jnp.sum(p * dp, axis=-1, keepdims=True)
                    ds = (p * (dp - dsum)).astype(BF16)
                    dq_ref[:, hs] = _dot(ds, keys, "nn").astype(BF16)
                    ds_rows.append(ds)
                    p_rows.append(p.astype(BF16))
                    dsink_ref[h:h + 1, :] += jnp.broadcast_to(-jnp.sum(p_sink * dsum, axis=0, keepdims=True), (1, LANES))
                q_rows = jnp.concatenate([q_ref[:, hs] for hs in group_heads], axis=0)
                do_rows = jnp.concatenate([do_ref[:, hs] for hs in group_heads], axis=0)
                dk = _dot(jnp.concatenate(ds_rows, axis=0), q_rows, "tn")
                dv = _dot(jnp.concatenate(p_rows, axis=0), do_rows, "tn")
                vv = slice(kw + g * HEAD_DIM, kw + (g + 1) * HEAD_DIM)
                dkv_ref[geo.s:geo.seg, kv] += dk[:geo.l]
                dkv_ref[geo.s:geo.seg, vv] += dv[:geo.l]
                if with_band:
                    dkv_ref[pl.ds(s0, BAND), kv] += dk[geo.l:]
                    dkv_ref[pl.ds(s0, BAND), vv] += dv[geo.l:]

        pl.when(j < n_x_blocks)(lambda: heads(True))
        pl.when(j >= n_x_blocks)(lambda: heads(False))

        @pl.when(j == n_blocks - 1)
        def _():
            dkv_out_ref[...] = dkv_ref[...].astype(BF16)

    res = pl.pallas_call(
        kern, name=name, grid=(geo.b, n_blocks),
        in_specs=[pl.BlockSpec(memory_space=pltpu.SMEM),
                  pl.BlockSpec((ATTN_BLOCK, qw), lambda b, j: (b * n_blocks + j, 0)),
                  pl.BlockSpec((geo.seg, kw), lambda b, j: (b, qw // kw)),
                  pl.BlockSpec((geo.seg, kw), lambda b, j: (b, qw // kw + 1)),
                  pl.BlockSpec((ATTN_BLOCK, qw), lambda b, j: (b * n_blocks + j, 0))] + carrier.in_specs(),
        out_specs=[pl.BlockSpec((ATTN_BLOCK, qw), lambda b, j: (b * n_blocks + j, 0)),
                   pl.BlockSpec((geo.seg, 2 * kw), lambda b, j: (b, 0)),
                   pl.BlockSpec((N_HEADS, LANES), lambda b, j: (0, 0))] + carrier.out_specs(),
        out_shape=[jax.ShapeDtypeStruct((geo.r, qw), BF16), jax.ShapeDtypeStruct((geo.r, 2 * kw), BF16),
                   jax.ShapeDtypeStruct((N_HEADS, LANES), F32)] + carrier.out_shapes(),
        scratch_shapes=[pltpu.VMEM((geo.seg, 2 * kw), F32)] + carrier.scratch(), input_output_aliases=carrier.aliases(),
        compiler_params=_cparams("arbitrary", "arbitrary"),
    )(sink, qkv, qkv, qkv, do, *carrier.operands())
    (dq, dkv, dsink), extra = carrier.results(res)
    return dq, dkv, dsink, extra


RET_QK_W = RET_HEADS * RET_QK_DIM
K_SCALE = RET_QK_DIM ** -0.5


def _ret_prep(geo, proj, cos, sin_signed, name):
    def body(i, p, cs, sn):
        cs2, sn2 = jnp.concatenate([cs] * RET_HEADS, axis=1), jnp.concatenate([sn] * RET_HEADS, axis=1)
        q = _rope(p[:, :RET_QK_W], cs2, sn2, RET_QK_DIM // 4)
        k = _rope(p[:, RET_QK_W:2 * RET_QK_W], cs2, sn2, RET_QK_DIM // 4) * K_SCALE
        return jnp.concatenate([q, k, p[:, 2 * RET_QK_W:]], axis=1)

    return _rowwise(name, body, geo, 128, [(proj, ("rowc", 2 * RET_QK_W + RET_VWIDTH, 0)), (cos, "tab"), (sin_signed, "tab")],
                    [("row", 2 * RET_QK_W + RET_VWIDTH, BF16)])


def _ret_prep_bwd(geo, dq, dk, dv, dgate, cos, sin_signed, name):
    def body(i, dqv, dkv, dvv, dg, cs, sn):
        cs2, sn2 = jnp.concatenate([cs] * RET_HEADS, axis=1), jnp.concatenate([sn] * RET_HEADS, axis=1)
        dkv = dkv * K_SCALE
        dqv = dqv * cs2 + _swap_halves(dqv * sn2, RET_QK_DIM // 4)
        dkv = dkv * cs2 + _swap_halves(dkv * sn2, RET_QK_DIM // 4)
        return jnp.concatenate([dqv, dkv, dvv, dg], axis=1)

    return _rowwise(name, body, geo, 128,
                    [(dq, "row"), (dk, "row"), (dv, "row"), (dgate, "row"), (cos, "tab"), (sin_signed, "tab")],
                    [("row", 2 * RET_QK_W + 2 * RET_VWIDTH, BF16)])


def _ret_step(state, q, k, v, lg, rev):
    c = RET_CHUNK
    ri = lax.broadcasted_iota(jnp.int32, (c, 1), 0).astype(F32)
    cj = lax.broadcasted_iota(jnp.int32, (1, c), 1).astype(F32)
    if rev:
        dist, q_decay, k_decay = cj - ri, jnp.exp(lg * (c - ri)), jnp.exp(lg * ri)
    else:
        dist, q_decay, k_decay = ri - cj, jnp.exp(lg * (ri + 1.0)), jnp.exp(lg * (c - 1.0 - ri))
    intra = jnp.where(dist >= 0, jnp.exp(lg * jnp.maximum(dist, 0.0)), 0.0)
    scores = _mm(q, k, "nt") * intra
    out = _mm(scores, v, "nn") + _mm(q, state, "nn") * q_decay
    new_state = state * jnp.exp(lg * c) + _mm(k * k_decay, v, "tn")
    return new_state, out


def _ret_state0(kc, vc, lg, rev):
    n = kc.shape[0]
    t = lax.broadcasted_iota(jnp.int32, (n, 1), 0).astype(F32)
    decay = jnp.exp(lg * t) if rev else jnp.exp(lg * (n - 1.0 - t))
    return _mm(kc * decay, vc, "tn")


def _ret_specs(geo):
    nq = RET_HEADS
    return [pl.BlockSpec((2 * RET_HEADS, LANES), lambda b, h: (0, 0)),
            pl.BlockSpec((geo.seg, RET_QK_DIM), lambda b, h: (b, h)),
            pl.BlockSpec((geo.seg, RET_QK_DIM), lambda b, h: (b, nq + h)),
            pl.BlockSpec((geo.seg, RET_V_DIM), lambda b, h: (b, nq + h))]


def _retention(geo, qkv, log_g, name):
    nc = geo.s // RET_CHUNK

    def kern(lg_ref, q_ref, k_ref, v_ref, o_ref, st_ref):
        h = pl.program_id(1)
        for d, rev in ((0, False), (1, True)):
            lg = lg_ref[pl.ds(d * RET_HEADS + h, 1), 0:1]
            st_ref[...] = _ret_state0(k_ref[geo.s:geo.seg, :].astype(F32), v_ref[geo.s:geo.seg, :].astype(F32), lg, rev)

            def chunk(ci, carry, d=d, rev=rev, lg=lg):
                r0 = pl.multiple_of((nc - 1 - ci if rev else ci) * RET_CHUNK, RET_CHUNK)
                rows = pl.ds(r0, RET_CHUNK)
                new_state, out = _ret_step(st_ref[...], q_ref[rows, :].astype(F32), k_ref[rows, :].astype(F32),
                                           v_ref[rows, :].astype(F32), lg, rev)
                st_ref[...] = new_state
                if d == 0:
                    o_ref[rows, :] = out
                else:
                    o_ref[rows, :] += out
                return carry

            lax.fori_loop(0, nc, chunk, 0)
        o_ref[geo.s:geo.seg, :] = jnp.zeros((geo.l, RET_V_DIM), F32)

    return pl.pallas_call(
        kern, name=name, grid=(geo.b, RET_HEADS), in_specs=_ret_specs(geo),
        out_specs=pl.BlockSpec((geo.seg, RET_V_DIM), lambda b, h: (b, h)),
        out_shape=jax.ShapeDtypeStruct((geo.r, RET_VWIDTH), F32),
        scratch_shapes=[pltpu.VMEM((RET_QK_DIM, RET_V_DIM), F32)],
        compiler_params=_cparams("parallel", "arbitrary"),
    )(log_g, qkv, qkv, qkv)


def _retention_bwd(geo, qkv, log_g, do, name):
    nc = geo.s // RET_CHUNK
    ctx = slice(geo.s, geo.seg)

    def kern(lg_ref, q_ref, k_ref, v_ref, do_ref, dq_ref, dk_ref, dv_ref, dlg_ref, states_ref, dst_ref, aq_ref, ak_ref, av_ref):
        b, h = pl.program_id(0), pl.program_id(1)

        @pl.when((b == 0) & (h == 0))
        def _():
            dlg_ref[...] = jnp.zeros_like(dlg_ref)

        for d, rev in ((0, False), (1, True)):
            row = pl.ds(d * RET_HEADS + h, 1)
            lg = lg_ref[row, 0:1]
            kc, vc = k_ref[ctx, :].astype(F32), v_ref[ctx, :].astype(F32)
            states_ref[0] = _ret_state0(kc, vc, lg, rev)

            def rows_of(ci, rev=rev):
                return pl.ds(pl.multiple_of((nc - 1 - ci if rev else ci) * RET_CHUNK, RET_CHUNK), RET_CHUNK)

            def load(rows):
                return q_ref[rows, :].astype(F32), k_ref[rows, :].astype(F32), v_ref[rows, :].astype(F32)

            def replay(ci, carry, rev=rev, lg=lg, rows_of=rows_of, load=load):
                states_ref[ci + 1] = _ret_step(states_ref[ci], *load(rows_of(ci)), lg, rev)[0]
                return carry

            lax.fori_loop(0, nc - 1, replay, 0)
            dst_ref[...] = jnp.zeros_like(dst_ref)

            def emit(rows, dq, dk, dv, d=d):
                if d == 0:
                    ak_ref[rows, :], av_ref[rows, :] = dk, dv
                    if dq is not None:
                        aq_ref[rows, :] = dq
                else:
                    dk_ref[rows, :] = (ak_ref[rows, :] + dk).astype(BF16)
                    dv_ref[rows, :] = (av_ref[rows, :] + dv).astype(BF16)
                    if dq is not None:
                        dq_ref[rows, :] = (aq_ref[rows, :] + dq).astype(BF16)

            def back(t, dlg, rev=rev, lg=lg, rows_of=rows_of, load=load, emit=emit):
                ci = nc - 1 - t
                rows = rows_of(ci)
                _, vjp = jax.vjp(lambda st, q, k, v, g: _ret_step(st, q, k, v, g, rev), states_ref[ci], *load(rows), lg)
                dstate, dq, dk, dv, dg = vjp((dst_ref[...], do_ref[rows, :].astype(F32)))
                dst_ref[...] = dstate
                emit(rows, dq, dk, dv)
                return dlg + dg

            dlg = lax.fori_loop(0, nc, back, jnp.zeros((1, 1), F32))
            _, vjp = jax.vjp(lambda kk, vv, g: _ret_state0(kk, vv, g, rev), kc, vc, lg)
            dkc, dvc, dg = vjp(dst_ref[...])
            emit(ctx, None, dkc, dvc)
            dlg_ref[row, :] += jnp.broadcast_to(dlg + dg, (1, LANES))
        dq_ref[ctx, :] = jnp.zeros((geo.l, RET_QK_DIM), BF16)

    nq = RET_HEADS
    return pl.pallas_call(
        kern, name=name, grid=(geo.b, RET_HEADS),
        in_specs=_ret_specs(geo) + [pl.BlockSpec((geo.seg, RET_V_DIM), lambda b, h: (b, h))],
        out_specs=[pl.BlockSpec((geo.seg, RET_QK_DIM), lambda b, h: (b, h)),
                   pl.BlockSpec((geo.seg, RET_QK_DIM), lambda b, h: (b, h)),
                   pl.BlockSpec((geo.seg, RET_V_DIM), lambda b, h: (b, h)),
                   pl.BlockSpec((2 * RET_HEADS, LANES), lambda b, h: (0, 0))],
        out_shape=[jax.ShapeDtypeStruct((geo.r, RET_QK_W), BF16), jax.ShapeDtypeStruct((geo.r, RET_QK_W), BF16),
                   jax.ShapeDtypeStruct((geo.r, RET_VWIDTH), BF16), jax.ShapeDtypeStruct((2 * RET_HEADS, LANES), F32)],
        scratch_shapes=[pltpu.VMEM((nc, RET_QK_DIM, RET_V_DIM), F32), pltpu.VMEM((RET_QK_DIM, RET_V_DIM), F32),
                        pltpu.VMEM((geo.seg, RET_QK_DIM), F32), pltpu.VMEM((geo.seg, RET_QK_DIM), F32),
                        pltpu.VMEM((geo.seg, RET_V_DIM), F32)],
        compiler_params=_cparams("arbitrary", "arbitrary"),
    )(log_g, qkv, qkv, qkv, do)


def _gated(o, g, gain):
    outs = []
    for h in range(RET_HEADS):
        cols = slice(h * RET_V_DIM, (h + 1) * RET_V_DIM)
        oh = o[:, cols]
        mu = jnp.mean(oh, axis=-1, keepdims=True)
        var = jnp.mean(jnp.square(oh - mu), axis=-1, keepdims=True)
        outs.append(_silu(g[:, cols]) * ((oh - mu) * lax.rsqrt(var + EPS) * gain[:, cols]))
    return jnp.concatenate(outs, axis=1)


def _ret_gated(geo, o, proj, gain, name):
    def body(i, ov, gv, gn):
        return _gated(ov, gv, gn)

    gate_block = (2 * RET_QK_W + RET_VWIDTH) // RET_VWIDTH
    return _rowwise(name, body, geo, 128, [(o, "row"), (proj, ("rowc", RET_VWIDTH, gate_block)), (gain, "full")],
                    [("row", RET_VWIDTH, BF16)])


def _ret_gated_bwd(geo, o, proj, gain, dout, name):
    def body(i, ov, gv, gn, dv):
        _, vjp = jax.vjp(_gated, ov, gv, gn)
        return vjp(dv)

    gate_block = (2 * RET_QK_W + RET_VWIDTH) // RET_VWIDTH
    return _rowwise(name, body, geo, 128,
                    [(o, "row"), (proj, ("rowc", RET_VWIDTH, gate_block)), (gain, "full"), (dout, "row")],
                    [("row", RET_VWIDTH, BF16), ("row", RET_VWIDTH, BF16), ("gacc", 1, RET_VWIDTH)])


def _whole(name, fn, out_shapes, *arrays):
    n = len(arrays)

    def kern(*refs):
        res = fn(*[r[...] for r in refs[:n]])
        for ref, val in zip(refs[n:], res):
            ref[...] = val.astype(ref.dtype)

    return pl.pallas_call(kern, name=name, out_shape=out_shapes)(*arrays)


def _rope_tables(geo, head_dim):
    rows = geo.s // GRID_W
    row = jnp.broadcast_to(jnp.arange(rows, dtype=jnp.int32)[:, None], (rows, GRID_W)).reshape(geo.s)
    col = jnp.broadcast_to(jnp.arange(GRID_W, dtype=jnp.int32)[None, :], (rows, GRID_W)).reshape(geo.s)
    axis_dim = head_dim // 2
    inv = ROPE_BASE ** (-jnp.arange(0, axis_dim, 2, dtype=F32) / axis_dim)
    ang_r = row.astype(F32)[:, None] * inv
    ang_c = col.astype(F32)[:, None] * inv
    cos = jnp.concatenate([jnp.cos(ang_r)] * 2 + [jnp.cos(ang_c)] * 2, axis=1)
    sin = jnp.concatenate([-jnp.sin(ang_r), jnp.sin(ang_r), -jnp.sin(ang_c), jnp.sin(ang_c)], axis=1)
    cos = jnp.concatenate([cos, jnp.ones((geo.l, head_dim), F32)], axis=0)
    sin = jnp.concatenate([sin, jnp.zeros((geo.l, head_dim), F32)], axis=0)
    reps = max(1, LANES // head_dim)
    return jnp.tile(cos, (1, reps)), jnp.tile(sin, (1, reps))


def _row_tile(r):
    return next(t for t in (1024, 512, 256, 128) if r % t == 0)


MOD_ROWS = 8


def _local_step(x, c, ctx, target, sp, wts, plan=None):
    nb, s, d = x.shape
    geo = _Rows(nb, s, ctx.shape[1])
    assert nb + 1 <= MOD_ROWS and d == D_MODEL
    tm = _row_tile(geo.r)
    z = jnp.concatenate([x, ctx], axis=1).reshape(geo.r, d)
    cvec = jnp.concatenate([c, sp["c_ctx"][None, :], jnp.zeros((MOD_ROWS - nb - 1, d), F32)], axis=0)
    cact, = _whole("cond_silu", lambda v: (_silu(v),), [jax.ShapeDtypeStruct(cvec.shape, F32)], cvec)
    cos64, sin64 = _rope_tables(geo, HEAD_DIM)
    cos256, sin256 = _rope_tables(geo, RET_QK_DIM)
    q_gain = jnp.tile(sp["q_norm"].reshape(1, HEAD_DIM), (1, LANES // HEAD_DIM))
    k_gain = jnp.tile(sp["k_norm"].reshape(1, HEAD_DIM), (1, LANES // HEAD_DIM))
    sink = sp["sink"].reshape(N_HEADS)
    log_g = jnp.broadcast_to(sp["log_g"].reshape(2 * RET_HEADS, 1), (2 * RET_HEADS, LANES))
    gn_g = sp["gn_g"].reshape(1, RET_VWIDTH)

    def modulation(i):
        mod = _mm_nn(cact, wts["ada"][i], F32, f"mod{i}", MOD_ROWS, wts["ada"][i].shape[2], d, bias=sp["ada_b"][i][None, :])
        return mod[:nb + 1, None, :]

    saved = []
    mods = [modulation(0), None]
    h1 = _norm_mod(geo, z, sp["norm1_g"][0][None, :], mods[0], 0, "norm1_0")
    for i in range(2):
        mod3 = mods[i]
        n1, n2 = sp["norm1_g"][i][None, :], sp["norm2_g"][i][None, :]
        if i == 0:
            proj = _mm_nn(h1, wts["attn_qkv"], F32, "attn_qkv", tm, wts["attn_qkv"].shape[1], d)
            prep = _attn_prep(geo, proj, cos64, sin64, q_gain, k_gain, "attn_prep")
            o, late = _attention(geo, prep, sink, "attn", plan.gather_job() if plan else None)
            if plan:
                plan.late_weights(late, wts)
            mods[1] = modulation(1)
            oraw = None
            w_o = wts["attn_o"]
        else:
            proj = _mm_nn(h1, wts["ret_qkvg"], BF16, "ret_qkvg", tm, wts["ret_qkvg"].shape[2], d)
            prep = _ret_prep(geo, proj, cos256, sin256, "ret_prep")
            oraw = _retention(geo, prep, log_g, "ret")
            o = _ret_gated(geo, oraw, proj, gn_g, "ret_gated")
            w_o = wts["ret_o"]
        zmid, mix, h2 = _mm_nn_gate_residual(geo, o, w_o, z, mod3, 2 * d, f"mix_out{i}", norm=(n2, mod3, 3 * d))
        u, a = _ffn_in_swiglu(h2, wts["ffn_in"][i], f"ffn_in{i}")
        next_norm = (sp["norm1_g"][1][None, :], mods[1], 0) if i == 0 else None
        zout, f, h1_next = _mm_nn_gate_residual(geo, a, wts["ffn_out"][i], zmid, mod3, 5 * d, f"ffn_out{i}", norm=next_norm)
        saved.append(dict(z=z, mod3=mod3, n1=n1, n2=n2, h1=h1, proj=proj, prep=prep, o=o, oraw=oraw, mix=mix, zmid=zmid,
                          h2=h2, u=u, a=a, f=f))
        z, h1 = zout, h1_next

    dz, loss, df, dg2 = _loss_head(geo, z, target.reshape(nb * s, d), saved[1]["f"], saved[1]["mod3"], 5 * d, "loss")

    big, small = {}, {}
    dmods = [None, None]
    for i in (1, 0):
        sv = saved[i]
        mod3 = sv["mod3"]
        carry = plan is not None and i == 0
        du, land = _ffn_out_dx_swiglu_bwd(df, wts["ffn_out"][i], sv["u"], f"ffn_out_dx{i}", plan.layer1.swap_job() if carry else None)
        if carry:
            plan.layer1.after_swap(land)
        big[f"ffn_out{i}"] = _mm_tn(sv["a"], df, f"ffn_out_dw{i}", D_FF // 2, 1024, tm).reshape(N_CHIPS, D_FF // N_CHIPS, d)
        n4 = wts["ffn_in"][i].shape[2]
        dh2 = _mm_nt(du, wts["ffn_in"][i], BF16, f"ffn_in_dx{i}", tm, 1024, n4)
        big[f"ffn_in{i}"] = _mm_tn(sv["h2"], du, f"ffn_in_dw{i}", 1024, n4, tm, shards=N_CHIPS)
        if carry:
            plan.start_layer0_ffn(big)
        dzmid, dsh2, dsc2, dn2, dmix, dg1, *land = _norm_mod_bwd(geo, sv["zmid"], sv["n2"], mod3, 3 * d, dh2, dz, f"norm2_bwd{i}",
                                                                 gated=(sv["mix"], mod3, 2 * d),
                                                                 job=plan.layer0_ffn.swap_job() if carry else None)
        if carry:
            plan.layer0_ffn.after_swap(land[0])
        if i == 0:
            do = _mm_nt(dmix, wts["attn_o"], BF16, "attn_out_dx", tm, 1024, 1024)
            big["attn_o"] = _mm_tn(sv["o"], dmix, "attn_out_dw", 1024, 1024, tm).reshape(N_CHIPS, 1024 // N_CHIPS, d)
            dq, dkv, dsink, land = _attention_bwd(geo, sv["prep"], sink, do, "attn_bwd", plan.exchange_job() if plan else None)
            if plan:
                plan.after_exchange(land)
            dproj, dqg, dkg = _attn_prep_bwd(geo, sv["proj"], cos64, sin64, q_gain, k_gain, dq, dkv, "attn_prep_bwd")
            small["q_norm"] = dqg[0, :HEAD_DIM] + dqg[0, HEAD_DIM:]
            small["k_norm"] = dkg[0, :HEAD_DIM] + dkg[0, HEAD_DIM:]
            small["sink"] = dsink[:, 0]
            wq = wts["attn_qkv"]
            dh1 = _mm_nt(dproj, wq, BF16, "attn_qkv_dx", tm, 1024, wq.shape[1])
            dwq = _mm_tn(sv["h1"], dproj, "attn_qkv_dw", 1024, wq.shape[1], tm)
            big["attn_qkv"] = dwq.reshape(d, N_CHIPS, -1).transpose(1, 0, 2)
        else:
            do = _mm_nt(dmix, wts["ret_o"], BF16, "ret_out_dx", tm, 1024, 1024)
            big["ret_o"] = _mm_tn(sv["o"], dmix, "ret_out_dw", 1024, 1024, tm).reshape(N_CHIPS, RET_VWIDTH // N_CHIPS, d)
            doraw, dgate, dgn = _ret_gated_bwd(geo, sv["oraw"], sv["proj"], gn_g, do, "ret_gated_bwd")
            small["gn_g"] = dgn[0]
            dq, dk, dv, dlg = _retention_bwd(geo, sv["prep"], log_g, doraw, "ret_bwd")
            small["log_g"] = dlg[:, 0].reshape(2, RET_HEADS)
            dproj = _ret_prep_bwd(geo, dq, dk, dv, dgate, cos256, sin256, "ret_prep_bwd")
            wq = wts["ret_qkvg"]
            dh1 = _mm_nt(dproj, wq, BF16, "ret_qkvg_dx", tm, 1024, wq.shape[2])
            big["ret_qkvg"] = _mm_tn(sv["h1"], dproj, "ret_qkvg_dw", 1024, wq.shape[2], tm, shards=N_CHIPS)
        below = (saved[0]["f"], saved[0]["mod3"], 5 * d) if i == 1 else None
        dz, dsh1, dsc1, dn1, *below_grads = _norm_mod_bwd(geo, sv["z"], sv["n1"], mod3, 0, dh1, dzmid, f"norm1_bwd{i}", gated=below,
                                                              latent_only=i == 0)
        small[f"norm1_g{i}"], small[f"norm2_g{i}"] = dn1[0], dn2[0]
        parts = [dsh1, dsc1, dg1, dsh2, dsc2, dg2]
        rows = jnp.concatenate([jnp.concatenate([p[:nb, 0, :] for p in parts], axis=1),
                                jnp.concatenate([jnp.sum(p[nb:, 0, :], axis=0, keepdims=True) for p in parts], axis=1),
                                jnp.zeros((MOD_ROWS - nb - 1, 6 * d), F32)], axis=0)
        dmods[i] = rows
        if below_grads:
            df, dg2 = below_grads
        small[f"ada_b{i}"] = jnp.sum(rows, axis=0)
        big[f"ada{i}"] = _mm_tn(cact, rows, f"ada_dw{i}", 1024, wts["ada"][i].shape[2], MOD_ROWS, shards=N_CHIPS)
        if plan and i == 1:
            plan.start_layer1(big)

    dcact = [_mm_nt(dmods[i], wts["ada"][i], F32, f"ada_dx{i}", MOD_ROWS, 1024, wts["ada"][i].shape[2]) for i in range(2)]

    def silu_bwd(v, d0, d1):
        sg = _sigmoid(v)
        return ((d0 + d1) * (sg * (1.0 + v * (1.0 - sg))),)

    dcvec, = _whole("cond_silu_bwd", silu_bwd, [jax.ShapeDtypeStruct(cvec.shape, F32)], cvec, dcact[0], dcact[1])
    small["c_ctx"] = dcvec[nb]
    return loss, dz, big, small


def _adamw(w, g, m, v, name):
    rows, cols = w.shape
    tr = next((t for t in (256, 128, 64, 32, 16, 8) if rows % t == 0), rows)
    c1 = 1.0 - ADAM_B1 ** ADAM_STEP
    c2 = 1.0 - ADAM_B2 ** ADAM_STEP

    def kern(w_ref, g_ref, m_ref, v_ref, d_ref, nm_ref, nv_ref):
        gv = g_ref[...]
        nm = ADAM_B1 * m_ref[...] + (1.0 - ADAM_B1) * gv
        nv = ADAM_B2 * v_ref[...] + (1.0 - ADAM_B2) * jnp.square(gv)
        d_ref[...] = -ADAM_LR * ((nm / c1) / (jnp.sqrt(nv / c2) + ADAM_EPS) + ADAM_WD * w_ref[...])
        nm_ref[...] = nm
        nv_ref[...] = nv

    spec = pl.BlockSpec((tr, cols), lambda i: (i, 0))
    return pl.pallas_call(
        kern, name=name, grid=(rows // tr,), in_specs=[spec] * 4, out_specs=[spec] * 3,
        out_shape=[jax.ShapeDtypeStruct(w.shape, F32)] * 3, compiler_params=_cparams("parallel"),
    )(w, g, m, v)


N_DEVICES = 8


def _mesh_pos():
    return lax.axis_index("x"), lax.axis_index("y"), lax.axis_index("c")


def _other_chips(x, y):
    return [(1 - x, y), (x, 1 - y), (1 - x, 1 - y)]


def _hbm(n):
    return [pl.BlockSpec(memory_space=pl.ANY)] * n


def _remote(src, dst, send_sem, recv_sem, device):
    return pltpu.make_async_remote_copy(src_ref=src, dst_ref=dst, send_sem=send_sem, recv_sem=recv_sem,
                                        device_id=device, device_id_type=MESH)


def _scalar_spec(grid, in_specs, out_specs):
    return pltpu.PrefetchScalarGridSpec(num_scalar_prefetch=1, grid=grid, in_specs=in_specs, out_specs=out_specs)


def _place_shard(param, layer, pos, name):
    _, r, cols = param.shape
    tr = _slab_tile(r)

    def kern(pos_ref, s_ref, o_ref):
        o_ref[...] = s_ref[...].astype(BF16)

    return pl.pallas_call(
        kern, name=name, out_shape=jax.ShapeDtypeStruct((N_CHIPS, r, cols), BF16),
        grid_spec=_scalar_spec((r // tr,), [pl.BlockSpec((None, tr, cols), lambda i, p: (layer, i, 0))],
                               pl.BlockSpec((None, tr, cols), lambda i, p: (p[1], i, 0))),
        compiler_params=_cparams("parallel"),
    )(pos, param)


class _CommJob:
    def __init__(self, inputs, out_shapes, aliases, sem_shapes, stages):
        self.inputs, self.out_shapes, self.aliases, self.sem_shapes, self.stages = inputs, out_shapes, aliases, sem_shapes, stages


def _merge_jobs(a, b):
    assert len(a.stages) == len(b.stages)
    ni, no, ns = len(a.inputs), len(a.out_shapes), len(a.sem_shapes)

    def both(sa, sb):
        def stage(ins, outs, sems):
            sa(ins[:ni], outs[:no], sems[:ns])
            sb(ins[ni:], outs[no:], sems[ns:])
        return stage

    aliases = dict(a.aliases)
    aliases.update({ni + i: no + o for i, o in b.aliases.items()})
    return _CommJob(a.inputs + b.inputs, a.out_shapes + b.out_shapes, aliases, a.sem_shapes + b.sem_shapes,
                    [both(sa, sb) for sa, sb in zip(a.stages, b.stages)])


def _run_job(job, name):
    n_in, n_out = len(job.inputs), len(job.out_shapes)

    def body(*refs):
        for stage in job.stages:
            stage(refs[:n_in], refs[n_in:n_in + n_out], refs[n_in + n_out:])

    return pl.pallas_call(
        body, name=name, in_specs=_hbm(n_in), out_specs=_hbm(n_out), out_shape=job.out_shapes,
        input_output_aliases=job.aliases, scratch_shapes=job.sem_shapes,
    )(*job.inputs)


def _job_marks(job, steps):
    return {2: [0, steps - 1], 3: [0, (5 * steps) // 8, steps - 1]}[len(job.stages)]


def _gather_job(placed):
    n = len(placed)

    def half(w, which):
        r2 = placed[w].shape[1] // 2
        return pl.ds(which * r2, r2)

    def ici_copies(outs, sems, slot_of):
        x, y, c = _mesh_pos()
        res = []
        for w in range(n):
            for k, (px, py) in enumerate(_other_chips(x, y)):
                slab = outs[w].at[slot_of(x, y, px, py), half(w, c)]
                res.append((slab, _remote(slab, slab, sems[0].at[w, k], sems[1].at[w, k], (px, py, c))))
        return res

    def forwards(outs, sems, which_core):
        x, y, c = _mesh_pos()
        res = []
        for w in range(n):
            for k, (px, py) in enumerate(_other_chips(x, y)):
                slab = outs[w].at[2 * px + py, half(w, which_core(c))]
                res.append(_remote(slab, slab, sems[2].at[w, k], sems[3].at[w, k], (x, y, 1 - c)))
        return res

    def send(ins, outs, sems):
        for _, cp in ici_copies(outs, sems, lambda x, y, px, py: 2 * x + y):
            cp.start()

    def forward(ins, outs, sems):
        arrivals = ici_copies(outs, sems, lambda x, y, px, py: 2 * px + py)
        for (_, arrival), fwd in zip(arrivals, forwards(outs, sems, lambda c: c)):
            arrival.wait_recv()
            fwd.start()

    def finish(ins, outs, sems):
        for cp in forwards(outs, sems, lambda c: 1 - c):
            cp.wait_recv()
        for _, cp in ici_copies(outs, sems, lambda x, y, px, py: 2 * x + y):
            cp.wait_send()
        for cp in forwards(outs, sems, lambda c: c):
            cp.wait_send()

    return _CommJob(list(placed), [jax.ShapeDtypeStruct(p.shape, p.dtype) for p in placed], {w: w for w in range(n)},
                    [pltpu.SemaphoreType.DMA((n, 3))] * 4, [send, forward, finish])


def _pair_swap_job(grads):
    n = len(grads)

    def copies(ins, outs, sems):
        x, y, c = _mesh_pos()
        res = []
        for w in range(n):
            r2 = grads[w].shape[1] // 2
            res.append(_remote(ins[w].at[:, pl.ds((1 - c) * r2, r2)], outs[w], sems[0].at[w], sems[1].at[w], (x, y, 1 - c)))
        return res

    def send(ins, outs, sems):
        for cp in copies(ins, outs, sems):
            cp.start()

    def finish(ins, outs, sems):
        for cp in copies(ins, outs, sems):
            cp.wait()

    return _CommJob(list(grads), [jax.ShapeDtypeStruct((N_CHIPS, g.shape[1] // 2, g.shape[2]), F32) for g in grads], {},
                    [pltpu.SemaphoreType.DMA((n,))] * 2, [send, finish])


def _chip_exchange_job(hs):
    n = len(hs)

    def send(ins, outs, sems):
        x, y, c = _mesh_pos()
        for w in range(n):
            for k, (px, py) in enumerate(_other_chips(x, y)):
                _remote(ins[w].at[2 * px + py], outs[w].at[2 * x + y], sems[0].at[w, k], sems[1].at[w, k], (px, py, c)).start()

    def finish(ins, outs, sems):
        x, y, c = _mesh_pos()
        for w in range(n):
            for k, (px, py) in enumerate(_other_chips(x, y)):
                got = outs[w].at[2 * px + py]
                cp = _remote(ins[w].at[2 * px + py], got, sems[0].at[w, k], sems[1].at[w, k], (px, py, c))
                cp.wait_recv()
                cp.wait_send()

    return _CommJob(list(hs), [jax.ShapeDtypeStruct(h.shape, h.dtype) for h in hs], {},
                    [pltpu.SemaphoreType.DMA((n, 3))] * 2, [send, finish])


def _pair_share(ts, name):
    n = len(ts)

    def body(*refs):
        outs = refs[n:2 * n]
        send_sems, recv_sems = refs[2 * n:]
        x, y, c = _mesh_pos()
        sends = []
        for w in range(n):
            r2 = ts[w].shape[0] // 2
            mine = outs[w].at[pl.ds(c * r2, r2)]
            rc = _remote(mine, mine, send_sems.at[w], recv_sems.at[w], (x, y, 1 - c))
            rc.start()
            sends.append(rc)
        for w in range(n):
            r2 = ts[w].shape[0] // 2
            theirs = outs[w].at[pl.ds((1 - c) * r2, r2)]
            _remote(theirs, theirs, send_sems.at[w], recv_sems.at[w], (x, y, 1 - c)).wait_recv()
            sends[w].wait_send()

    return pl.pallas_call(
        body, name=name, in_specs=_hbm(n), out_specs=_hbm(n),
        out_shape=[jax.ShapeDtypeStruct(t.shape, F32) for t in ts],
        input_output_aliases={w: w for w in range(n)},
        scratch_shapes=[pltpu.SemaphoreType.DMA((n,))] * 2,
    )(*ts)


def _slab_tile(rows):
    return next(t for t in (512, 256, 176, 128, 64, 32, 16) if rows % t == 0)


def _sum_pair(grad, land, pos, name):
    _, r2, cols = land.shape
    tr = _slab_tile(r2)
    nt = r2 // tr

    def kern(pos_ref, a_ref, b_ref, o_ref):
        o_ref[...] = (a_ref[...] + b_ref[...]).astype(BF16)

    spec = pl.BlockSpec((None, tr, cols), lambda j, i, p: (j, i, 0))
    return pl.pallas_call(
        kern, name=name, out_shape=jax.ShapeDtypeStruct(land.shape, BF16),
        grid_spec=_scalar_spec((N_CHIPS, nt), [pl.BlockSpec((None, tr, cols), lambda j, i, p: (j, p[0] * nt + i, 0)), spec], spec),
        compiler_params=_cparams("parallel", "parallel"),
    )(pos, grad, land)


def _sum_chips(hs, land, pos, name):
    _, r2, cols = land.shape
    tr = _slab_tile(r2)
    nt = r2 // tr

    def kern(pos_ref, h_ref, l_ref, o_ref):
        acc = jnp.zeros((tr, cols), F32)
        own = h_ref[...].astype(F32)
        for k in range(N_CHIPS):
            acc = acc + jnp.where(pos_ref[1] == k, own, l_ref[k].astype(F32))
        o_ref[...] = acc

    return pl.pallas_call(
        kern, name=name, out_shape=jax.ShapeDtypeStruct((2 * r2, cols), F32),
        grid_spec=_scalar_spec((nt,), [pl.BlockSpec((None, tr, cols), lambda i, p: (p[1], i, 0)),
                                       pl.BlockSpec((N_CHIPS, tr, cols), lambda i, p: (0, i, 0))],
                               pl.BlockSpec((tr, cols), lambda i, p: (p[0] * nt + i, 0))),
        compiler_params=_cparams("parallel"),
    )(pos, hs, land)


class _ReduceScatter:
    def __init__(self, grads, pos, tag):
        self.grads, self.pos, self.tag = list(grads), pos, tag

    def swap_job(self):
        return _pair_swap_job(self.grads)

    def after_swap(self, land):
        self.hs = [_sum_pair(g, l, self.pos, f"grads_pair_sum_{self.tag}{w}") for w, (g, l) in enumerate(zip(self.grads, land))]

    def exchange_job(self):
        return _chip_exchange_job(self.hs)

    def after_exchange(self, land2):
        ts = [_sum_chips(h, l, self.pos, f"grads_chip_sum_{self.tag}{w}") for w, (h, l) in enumerate(zip(self.hs, land2))]
        return _pair_share(ts, f"grads_pair_share_{self.tag}")

    def run(self):
        self.after_swap(_run_job(self.swap_job(), f"grads_pair_swap_{self.tag}"))
        return self.after_exchange(_run_job(self.exchange_job(), f"grads_chip_exchange_{self.tag}"))


EARLY_WEIGHTS = ("ada0", "attn_qkv")
LATE_WEIGHTS = ("ada1", "ffn_in0", "ffn_in1", "ffn_out0", "ffn_out1", "attn_o", "ret_qkvg", "ret_o")
LAYER1_GRADS = ("ffn_out1", "ffn_in1", "ret_o", "ret_qkvg", "ada1")
LAYER0_FFN_GRADS = ("ffn_out0", "ffn_in0")
LAST_GRADS = ("attn_o", "attn_qkv", "ada0")


def _fill_weights(wts, full):
    for name, w in full.items():
        if name[:-1] in ("ada", "ffn_in"):
            wts[name[:-1]][int(name[-1])] = w
        elif name[:-1] == "ffn_out":
            wts["ffn_out"][int(name[-1])] = w.reshape(-1, w.shape[2])
        elif name in ("attn_o", "ret_o"):
            wts[name] = w.reshape(-1, w.shape[2])
        elif name == "attn_qkv":
            wts[name] = w.transpose(1, 0, 2).reshape(w.shape[1], -1)
        else:
            wts[name] = w


class _StepPlan:
    def __init__(self, placed, pos):
        self.placed, self.pos = placed, pos
        self.layer1 = self.layer0_ffn = None
        self.reduced = {}

    def gather_job(self):
        return _gather_job([self.placed[k] for k in LATE_WEIGHTS])

    def late_weights(self, outs, wts):
        _fill_weights(wts, dict(zip(LATE_WEIGHTS, outs)))

    def start_layer1(self, big):
        self.layer1 = _ReduceScatter([big[k] for k in LAYER1_GRADS], self.pos, "l1_")

    def start_layer0_ffn(self, big):
        self.layer0_ffn = _ReduceScatter([big[k] for k in LAYER0_FFN_GRADS], self.pos, "l0f_")

    def exchange_job(self):
        return _merge_jobs(self.layer1.exchange_job(), self.layer0_ffn.exchange_job())

    def after_exchange(self, land):
        n1 = len(LAYER1_GRADS)
        self.reduced.update(zip(LAYER1_GRADS, self.layer1.after_exchange(land[:n1])))
        self.reduced.update(zip(LAYER0_FFN_GRADS, self.layer0_ffn.after_exchange(land[n1:])))


def _all_reduce_small(v, name):
    def body(v_ref, o_ref, land_ref, send_sems, recv_sems):
        x, y, c = _mesh_pos()
        me = 4 * x + 2 * y + c
        land_ref[me] = v_ref[...]
        for t in range(N_DEVICES):
            @pl.when(t != me)
            def _(t=t):
                _remote(v_ref, land_ref.at[me], send_sems.at[t], recv_sems.at[me], (t // 4, (t // 2) % 2, t % 2)).start()
        for t in range(N_DEVICES):
            @pl.when(t != me)
            def _(t=t):
                _remote(v_ref, land_ref.at[t], send_sems.at[t], recv_sems.at[t], (t // 4, (t // 2) % 2, t % 2)).wait()
        acc = land_ref[0]
        for t in range(1, N_DEVICES):
            acc = acc + land_ref[t]
        o_ref[...] = acc

    vmem = pl.BlockSpec(memory_space=pltpu.VMEM)
    return pl.pallas_call(
        body, name=name, in_specs=[vmem], out_specs=vmem, out_shape=jax.ShapeDtypeStruct(v.shape, F32),
        scratch_shapes=[pltpu.VMEM((N_DEVICES,) + v.shape, F32), pltpu.SemaphoreType.DMA((N_DEVICES,)),
                        pltpu.SemaphoreType.DMA((N_DEVICES,))],
    )(v)


SMALL_ROWS = 24


def _pack_small(small, dlogit):
    d = D_MODEL
    misc = jnp.zeros((d,), F32)
    misc = misc.at[0:HEAD_DIM].set(small["q_norm"]).at[128:128 + HEAD_DIM].set(small["k_norm"])
    misc = misc.at[256:256 + N_HEADS].set(small["sink"]).at[384:384 + 2 * RET_HEADS].set(dlogit.reshape(-1))
    rows = [small["ada_b0"].reshape(6, d), small["ada_b1"].reshape(6, d), small["norm1_g0"][None], small["norm1_g1"][None],
            small["norm2_g0"][None], small["norm2_g1"][None], small["c_ctx"][None], small["gn_g"].reshape(2, d), misc[None]]
    buf = jnp.concatenate(rows, axis=0)
    return jnp.concatenate([buf, jnp.zeros((SMALL_ROWS - buf.shape[0], d), F32)], axis=0)


def _unpack_small(buf):
    d = D_MODEL
    misc = buf[19]
    return dict(ada_b=buf[0:12].reshape(2, 6 * d), norm1_g=buf[12:14], norm2_g=buf[14:16], c_ctx=buf[16],
                gn_g=buf[17:19].reshape(2 * d), q_norm=misc[0:HEAD_DIM], k_norm=misc[128:128 + HEAD_DIM],
                sink=misc[256:256 + N_HEADS], decay=misc[384:384 + 2 * RET_HEADS])


def kernel(x, c, ctx, c_ctx, ada_w, ada_b, norm1_g, norm2_g, ffn_w_in, ffn_w_out, attn_w_qkv, attn_q_norm, attn_k_norm, attn_sink, attn_w_o, ret_w_qkvg, ret_decay_logit, ret_gn_g, ret_w_o, loss_target, m_c_ctx, m_ada_w, m_ada_b, m_norm1_g, m_norm2_g, m_ffn_w_in, m_ffn_w_out, m_attn_w_qkv, m_attn_q_norm, m_attn_k_norm, m_attn_sink, m_attn_w_o, m_ret_w_qkvg, m_ret_decay_logit, m_ret_gn_g, m_ret_w_o, v_c_ctx, v_ada_w, v_ada_b, v_norm1_g, v_norm2_g, v_ffn_w_in, v_ffn_w_out, v_attn_w_qkv, v_attn_q_norm, v_attn_k_norm, v_attn_sink, v_attn_w_o, v_ret_w_qkvg, v_ret_decay_logit, v_ret_gn_g, v_ret_w_o):
    xi, yi, ci = _mesh_pos()
    chip = 2 * xi + yi
    nb, s, d = x.shape
    gn_shard = ret_gn_g.shape[1]

    shards = dict(ada0=(ada_w, 0), ada1=(ada_w, 1), ffn_in0=(ffn_w_in, 0), ffn_in1=(ffn_w_in, 1), ffn_out0=(ffn_w_out, 0),
                  ffn_out1=(ffn_w_out, 1), attn_qkv=(attn_w_qkv, 0), attn_o=(attn_w_o, 0), ret_qkvg=(ret_w_qkvg, 0), ret_o=(ret_w_o, 0))
    names = list(shards)
    pos = jnp.stack([ci, chip]).astype(jnp.int32)
    placed = {k: _place_shard(*shards[k], pos, f"place_{k}") for k in names}
    early = _run_job(_gather_job([placed[k] for k in EARLY_WEIGHTS]), "gather_early_weights")
    gn_mine = jnp.where(ci == 0, ret_gn_g[0], jnp.zeros_like(ret_gn_g[0]))
    gn_place = lax.dynamic_update_slice(jnp.zeros((RET_VWIDTH,), F32), gn_mine, (chip * gn_shard,))
    gn_full = _all_reduce_small(gn_place.reshape(2, d), "gather_gn_gain").reshape(RET_VWIDTH)

    wts = dict(ada=[None, None], ffn_in=[None, None], ffn_out=[None, None], attn_qkv=None, attn_o=None, ret_qkvg=None, ret_o=None)
    _fill_weights(wts, dict(zip(EARLY_WEIGHTS, early)))
    plan = _StepPlan(placed, pos)
    decay_logit = ret_decay_logit[0]
    sp = dict(c_ctx=c_ctx, ada_b=ada_b, norm1_g=norm1_g, norm2_g=norm2_g, q_norm=attn_q_norm[0], k_norm=attn_k_norm[0],
              sink=attn_sink[0], log_g=jax.nn.log_sigmoid(decay_logit), gn_g=gn_full)
    loss_part, dz, big, small = _local_step(x, c, ctx, loss_target, sp, wts, plan)

    loss = lax.psum(loss_part[0, 0], ("x", "y", "c"))
    grad_x = dz.reshape(nb, s, d)

    dlogit = small["log_g"] * jax.nn.sigmoid(-decay_logit)
    sg = _unpack_small(_all_reduce_small(_pack_small(small, dlogit), "reduce_small_grads"))
    reduced = dict(plan.reduced)
    reduced.update(zip(LAST_GRADS, _ReduceScatter([big[k] for k in LAST_GRADS], pos, "last_").run()))

    grads = dict(
        c_ctx=sg["c_ctx"], ada_w=jnp.stack([reduced["ada0"], reduced["ada1"]]), ada_b=sg["ada_b"], norm1_g=sg["norm1_g"],
        norm2_g=sg["norm2_g"], ffn_w_in=jnp.stack([reduced["ffn_in0"], reduced["ffn_in1"]]),
        ffn_w_out=jnp.stack([reduced["ffn_out0"], reduced["ffn_out1"]]), attn_w_qkv=reduced["attn_qkv"][None],
        attn_q_norm=sg["q_norm"][None], attn_k_norm=sg["k_norm"][None], attn_sink=sg["sink"][None],
        attn_w_o=reduced["attn_o"][None], ret_w_qkvg=reduced["ret_qkvg"][None], ret_decay_logit=sg["decay"].reshape(1, 2, RET_HEADS),
        ret_gn_g=lax.dynamic_slice(sg["gn_g"], (chip * gn_shard,), (gn_shard,))[None], ret_w_o=reduced["ret_o"][None])
    params = dict(c_ctx=(c_ctx, m_c_ctx, v_c_ctx), ada_w=(ada_w, m_ada_w, v_ada_w), ada_b=(ada_b, m_ada_b, v_ada_b),
                  norm1_g=(norm1_g, m_norm1_g, v_norm1_g), norm2_g=(norm2_g, m_norm2_g, v_norm2_g),
                  ffn_w_in=(ffn_w_in, m_ffn_w_in, v_ffn_w_in), ffn_w_out=(ffn_w_out, m_ffn_w_out, v_ffn_w_out),
                  attn_w_qkv=(attn_w_qkv, m_attn_w_qkv, v_attn_w_qkv), attn_q_norm=(attn_q_norm, m_attn_q_norm, v_attn_q_norm),
                  attn_k_norm=(attn_k_norm, m_attn_k_norm, v_attn_k_norm), attn_sink=(attn_sink, m_attn_sink, v_attn_sink),
                  attn_w_o=(attn_w_o, m_attn_w_o, v_attn_w_o), ret_w_qkvg=(ret_w_qkvg, m_ret_w_qkvg, v_ret_w_qkvg),
                  ret_decay_logit=(ret_decay_logit, m_ret_decay_logit, v_ret_decay_logit),
                  ret_gn_g=(ret_gn_g, m_ret_gn_g, v_ret_gn_g), ret_w_o=(ret_w_o, m_ret_w_o, v_ret_w_o))
    order = list(params)
    deltas, new_m, new_v = [], [], []
    for k in order:
        w, m, v = params[k]
        g = grads[k].reshape(w.shape)
        grads[k] = g
        flat = (-1, w.shape[-1]) if w.ndim > 1 else (1, -1)
        if k == "ret_decay_logit":
            flat = (1, -1)
        dw, nm, nv = _adamw(w.reshape(flat), g.reshape(flat), m.reshape(flat), v.reshape(flat), f"adamw_{k}")
        deltas.append(dw.reshape(w.shape))
        new_m.append(nm.reshape(w.shape))
        new_v.append(nv.reshape(w.shape))
    return (loss, grad_x, *[grads[k] for k in order], *deltas, *new_m, *new_v)
```

```python
import functools

import jax
import jax.numpy as jnp
from jax import lax
from jax.experimental import pallas as pl
from jax.experimental.pallas import tpu as pltpu

F32 = jnp.float32
BF16 = jnp.bfloat16

D_MODEL = 1024
N_HEADS = 16
N_KV_HEADS = 4
HEAD_DIM = 64
WINDOW = 128
ATTN_BLOCK = 128
BAND = ATTN_BLOCK + 2 * WINDOW
RET_HEADS = 4
RET_QK_DIM = 256
RET_V_DIM = 512
RET_VWIDTH = 2048
RET_CHUNK = 128
D_FF = 2816
GRID_W = 64
ROPE_BASE = 10000.0
EPS = 1e-6
NEG_INF = -1e30
LANES = 128

ADAM_LR = 0.001
ADAM_B1 = 0.9
ADAM_B2 = 0.999
ADAM_EPS = 1e-08
ADAM_WD = 0.01
ADAM_STEP = 10

VMEM_LIMIT_BYTES = 56 * 1024 * 1024
MESH = pl.DeviceIdType.MESH
N_CHIPS = 4


def _cparams(*sem):
    return pltpu.CompilerParams(dimension_semantics=sem, vmem_limit_bytes=VMEM_LIMIT_BYTES)


_DIMS = {"nn": ((1,), (0,)), "nt": ((1,), (1,)), "tn": ((0,), (0,))}


def _dot(a, b, form):
    return lax.dot_general(a.astype(BF16), b.astype(BF16), (_DIMS[form], ((), ())), preferred_element_type=F32)


@functools.partial(jax.custom_vjp, nondiff_argnums=(2,))
def _mm(a, b, form):
    return _dot(a, b, form)


def _mm_fwd(a, b, form):
    return _dot(a, b, form), (a, b)


def _mm_bwd(form, res, ct):
    a, b = res
    if form == "nn":
        da, db = _dot(ct, b, "nt"), _dot(a, ct, "tn")
    elif form == "nt":
        da, db = _dot(ct, b, "nn"), _dot(ct, a, "tn")
    else:
        da, db = _dot(b, ct, "nt"), _dot(a, ct, "nn")
    return da.astype(a.dtype), db.astype(b.dtype)


_mm.defvjp(_mm_fwd, _mm_bwd)


def _swap_halves(x, half):
    w = x.shape[-1]
    lane = lax.broadcasted_iota(jnp.int32, x.shape, x.ndim - 1)
    return jnp.where(lane % (2 * half) < half, pltpu.roll(x, w - half, x.ndim - 1), pltpu.roll(x, half, x.ndim - 1))


@functools.partial(jax.custom_vjp, nondiff_argnums=(1,))
def _rot(x, half):
    return _swap_halves(x, half)


def _rot_fwd(x, half):
    return _swap_halves(x, half), None


def _rot_bwd(half, _, ct):
    return (_swap_halves(ct, half),)


_rot.defvjp(_rot_fwd, _rot_bwd)


def _rope(x, cos, sin_signed, half):
    return x * cos + _rot(x, half) * sin_signed


def _head_mean_square(x):
    r = lax.broadcasted_iota(jnp.int32, (LANES, LANES), 0) // HEAD_DIM
    c = lax.broadcasted_iota(jnp.int32, (LANES, LANES), 1) // HEAD_DIM
    g = jnp.where(r == c, 1.0 / HEAD_DIM, 0.0).astype(F32)
    return jnp.dot(x * x, g, precision=lax.Precision.HIGHEST, preferred_element_type=F32)


def _qk_chunk(x, gain, cos, sin_signed, scale):
    y = x * lax.rsqrt(_head_mean_square(x) + EPS) * gain
    return _rope(y, cos, sin_signed, HEAD_DIM // 4) * scale


def _sigmoid(x):
    return 1.0 / (1.0 + jnp.exp(-x))


def _silu(x):
    return x * _sigmoid(x)


def _mm_nn(a, w, out_dtype, name, tm, tn, tk, bias=None):
    m, k_dim = a.shape
    if w.ndim == 3:
        n = w.shape[0] * w.shape[2]
        per = w.shape[2] // tn
        assert w.shape[2] % tn == 0
        w_spec = pl.BlockSpec((None, tk, tn), lambda i, j, k: (j // per, k, j % per))
    else:
        n = w.shape[1]
        w_spec = pl.BlockSpec((tk, tn), lambda i, j, k: (k, j))
    assert m % tm == 0 and n % tn == 0 and k_dim % tk == 0, (name, a.shape, w.shape, tm, tn, tk)
    nk = k_dim // tk
    has_bias = bias is not None

    def body(*refs):
        a_ref, w_ref = refs[0], refs[1]
        b_ref = refs[2] if has_bias else None
        o_ref, acc_ref = (refs[-1], None) if nk == 1 else (refs[-2], refs[-1])
        if nk == 1:
            part = jnp.dot(a_ref[...].astype(BF16), w_ref[...], preferred_element_type=F32)
            o_ref[...] = (part + b_ref[...] if has_bias else part).astype(out_dtype)
            return
        k = pl.program_id(2)

        @pl.when(k == 0)
        def _():
            acc_ref[...] = jnp.zeros_like(acc_ref)

        acc_ref[...] += jnp.dot(a_ref[...].astype(BF16), w_ref[...], preferred_element_type=F32)

        @pl.when(k == nk - 1)
        def _():
            r = acc_ref[...]
            if has_bias:
                r = r + b_ref[...]
            o_ref[...] = r.astype(out_dtype)

    in_specs = [pl.BlockSpec((tm, tk), lambda i, j, k: (i, k)), w_spec]
    args = [a, w]
    if has_bias:
        in_specs.append(pl.BlockSpec((1, tn), lambda i, j, k: (0, j)))
        args.append(bias)
    return pl.pallas_call(
        body, name=name, grid=(m // tm, n // tn, nk), in_specs=in_specs,
        out_specs=pl.BlockSpec((tm, tn), lambda i, j, k: (i, j)),
        out_shape=jax.ShapeDtypeStruct((m, n), out_dtype),
        scratch_shapes=[pltpu.VMEM((tm, tn), F32)] if nk > 1 else [],
        compiler_params=_cparams("parallel", "parallel", "arbitrary"),
    )(*args)


def _mm_nt(a, w, out_dtype, name, tm, tn, tk):
    if a.ndim == 3:
        planes, m, plane_w = a.shape
        c_dim = planes * plane_w
        a_per = plane_w // tk
        assert plane_w % tk == 0
        a_spec = pl.BlockSpec((None, tm, tk), lambda i, j, k: (k // a_per, i, k % a_per))
    else:
        m, c_dim = a.shape
        a_spec = pl.BlockSpec((tm, tk), lambda i, j, k: (i, k))
    if w.ndim == 3:
        k_out = w.shape[1]
        per = w.shape[2] // tk
        assert w.shape[2] % tk == 0 and w.shape[0] * w.shape[2] == c_dim
        w_spec = pl.BlockSpec((None, tn, tk), lambda i, j, k: (k // per, j, k % per))
    else:
        k_out = w.shape[0]
        assert w.shape[1] == c_dim
        w_spec = pl.BlockSpec((tn, tk), lambda i, j, k: (j, k))
    assert m % tm == 0 and k_out % tn == 0 and c_dim % tk == 0, (name, a.shape, w.shape, tm, tn, tk)
    nk = c_dim // tk

    def body(a_ref, w_ref, o_ref, acc_ref=None):
        if nk == 1:
            o_ref[...] = _dot(a_ref[...], w_ref[...], "nt").astype(out_dtype)
            return
        k = pl.program_id(2)

        @pl.when(k == 0)
        def _():
            acc_ref[...] = jnp.zeros_like(acc_ref)

        acc_ref[...] += _dot(a_ref[...], w_ref[...], "nt")

        @pl.when(k == nk - 1)
        def _():
            o_ref[...] = acc_ref[...].astype(out_dtype)

    return pl.pallas_call(
        body, name=name, grid=(m // tm, k_out // tn, nk),
        in_specs=[a_spec, w_spec],
        out_specs=pl.BlockSpec((tm, tn), lambda i, j, k: (i, j)),
        out_shape=jax.ShapeDtypeStruct((m, k_out), out_dtype),
        scratch_shapes=[pltpu.VMEM((tm, tn), F32)] if nk > 1 else [],
        compiler_params=_cparams("parallel", "parallel", "arbitrary"),
    )(a, w)


def _mm_tn(a, b, name, tm, tn, tk, shards=None):
    r, k_dim = a.shape
    if b.ndim == 3:
        n = b.shape[0] * b.shape[2]
        b_per = b.shape[2] // tn
        assert b.shape[2] % tn == 0
        b_spec = pl.BlockSpec((None, tk, tn), lambda i, j, k: (j // b_per, k, j % b_per))
    else:
        n = b.shape[1]
        b_spec = pl.BlockSpec((tk, tn), lambda i, j, k: (k, j))
    assert r % tk == 0 and k_dim % tm == 0 and n % tn == 0, (name, a.shape, b.shape, tm, tn, tk)
    nk = r // tk
    if shards:
        per = n // shards // tn
        assert n % (shards * tn) == 0
        out_shape = jax.ShapeDtypeStruct((shards, k_dim, n // shards), F32)
        out_spec = pl.BlockSpec((None, tm, tn), lambda i, j, k: (j // per, i, j % per))
    else:
        out_shape = jax.ShapeDtypeStruct((k_dim, n), F32)
        out_spec = pl.BlockSpec((tm, tn), lambda i, j, k: (i, j))

    def body(a_ref, b_ref, o_ref):
        k = pl.program_id(2)

        @pl.when(k == 0)
        def _():
            o_ref[...] = jnp.zeros_like(o_ref)

        o_ref[...] += _dot(a_ref[...], b_ref[...], "tn")

    return pl.pallas_call(
        body, name=name, grid=(k_dim // tm, n // tn, nk),
        in_specs=[pl.BlockSpec((tk, tm), lambda i, j, k: (k, i)), b_spec],
        out_specs=out_spec, out_shape=out_shape,
        compiler_params=_cparams("parallel", "parallel", "arbitrary"),
    )(a, b)


class _Carrier:
    def __init__(self, job, n_in, n_out, n_scratch):
        self.job, self.n_in, self.n_out, self.n_scratch = job, n_in, n_out, n_scratch
        self.ji = len(job.inputs) if job else 0
        self.jo = len(job.out_shapes) if job else 0

    def operands(self):
        return list(self.job.inputs) if self.job else []

    def in_specs(self):
        return [pl.BlockSpec(memory_space=pl.ANY)] * self.ji

    def out_specs(self):
        return [pl.BlockSpec(memory_space=pl.ANY)] * self.jo

    def out_shapes(self):
        return list(self.job.out_shapes) if self.job else []

    def scratch(self):
        return list(self.job.sem_shapes) if self.job else []

    def aliases(self):
        return {self.n_in + a: self.n_out + b for a, b in self.job.aliases.items()} if self.job else {}

    def split(self, refs):
        a = self.n_in
        b = a + self.ji
        c = b + self.n_out
        d = c + self.jo
        e = d + self.n_scratch
        return list(refs[:a]) + list(refs[b:c]) + list(refs[d:e]), (refs[a:b], refs[c:d], refs[e:])

    def run(self, job_refs, step, steps):
        if not self.job:
            return
        for stage, mark in zip(self.job.stages, _job_marks(self.job, steps)):
            pl.when(step == mark)(functools.partial(stage, *job_refs))

    def results(self, res):
        res = list(res)
        return res[:self.n_out], res[self.n_out:]


FFN_ROW_TILE = 768


def _ffn_tile(r):
    return FFN_ROW_TILE if r % FFN_ROW_TILE == 0 else _row_tile(r)


def _ffn_in_swiglu(h, w, name):
    r, k_dim = h.shape
    n4 = w.shape[2]
    tm = _ffn_tile(r)

    def body(h_ref, wg_ref, wu_ref, u_ref, a_ref):
        hv = h_ref[...]
        g = jnp.dot(hv, wg_ref[...], preferred_element_type=F32)
        up = jnp.dot(hv, wu_ref[...], preferred_element_type=F32)
        u_ref[0] = g.astype(BF16)
        u_ref[1] = up.astype(BF16)
        a_ref[...] = (_silu(g) * up).astype(BF16)

    return pl.pallas_call(
        body, name=name, grid=(r // tm, 2),
        in_specs=[pl.BlockSpec((tm, k_dim), lambda i, j: (i, 0)),
                  pl.BlockSpec((None, k_dim, n4), lambda i, j: (j, 0, 0)),
                  pl.BlockSpec((None, k_dim, n4), lambda i, j: (j + 2, 0, 0))],
        out_specs=[pl.BlockSpec((2, tm, n4), lambda i, j: (0, i, j)), pl.BlockSpec((tm, n4), lambda i, j: (i, j))],
        out_shape=[jax.ShapeDtypeStruct((2, r, 2 * n4), BF16), jax.ShapeDtypeStruct((r, 2 * n4), BF16)],
        compiler_params=_cparams("parallel", "parallel"),
    )(h, w, w)


def _mm_nn_gate_residual(geo, a, w, z, mod, off, name, norm=None):
    r, k_dim = a.shape
    n = w.shape[1]
    tm = FFN_ROW_TILE if geo.seg % FFN_ROW_TILE == 0 else 256
    tiles = geo.seg // tm
    assert geo.seg % tm == 0 and r == geo.r and n == D_MODEL

    def body(a_ref, w_ref, z_ref, mx_ref, mc_ref, *rest):
        out = jnp.dot(a_ref[...], w_ref[...], preferred_element_type=F32)
        is_x = (pl.program_id(0) % tiles) * tm + lax.broadcasted_iota(jnp.int32, (tm, 1), 0) < geo.s
        zo = z_ref[...] + jnp.where(is_x, mx_ref[:, off:off + n], mc_ref[:, off:off + n]) * out
        if norm:
            g_ref, nx_ref, nc_ref, zo_ref, raw_ref, h_ref = rest
            no = norm[2]
            shift = jnp.where(is_x, nx_ref[:, no:no + n], nc_ref[:, no:no + n])
            scale = jnp.where(is_x, nx_ref[:, no + n:no + 2 * n], nc_ref[:, no + n:no + 2 * n])
            rs = lax.rsqrt(jnp.mean(zo * zo, axis=-1, keepdims=True) + EPS)
            h_ref[...] = ((zo * rs) * g_ref[...] * (1.0 + scale) + shift).astype(BF16)
        else:
            zo_ref, raw_ref = rest
        zo_ref[...] = zo
        raw_ref[...] = out.astype(BF16)

    def mod_specs(m):
        return [pl.BlockSpec((None, 1, m.shape[2]), lambda i: (i // tiles, 0, 0)), pl.BlockSpec((None, 1, m.shape[2]), lambda i: (geo.b, 0, 0))]

    row = pl.BlockSpec((tm, n), lambda i: (i, 0))
    in_specs = [pl.BlockSpec((tm, k_dim), lambda i: (i, 0)), pl.BlockSpec((k_dim, n), lambda i: (0, 0)), row] + mod_specs(mod)
    args = [a, w, z, mod, mod]
    out_specs, out_shape = [row, row], [jax.ShapeDtypeStruct((r, n), F32), jax.ShapeDtypeStruct((r, n), BF16)]
    if norm:
        in_specs += [pl.BlockSpec((1, n), lambda i: (0, 0))] + mod_specs(norm[1])
        args += [norm[0], norm[1], norm[1]]
        out_specs.append(row)
        out_shape.append(jax.ShapeDtypeStruct((r, n), BF16))
    res = pl.pallas_call(body, name=name, grid=(r // tm,), in_specs=in_specs, out_specs=out_specs, out_shape=out_shape,
                         compiler_params=_cparams("parallel"))(*args)
    return res if norm else (*res, None)


def _mm_nt_norm_bwd(geo, a, w, tk, z, gain, mod, off, skip, name, gated=None, job=None):
    d = D_MODEL
    tm = FFN_ROW_TILE if geo.seg % FFN_ROW_TILE == 0 else 256
    tiles = geo.seg // tm
    if a.ndim == 3:
        c_dim = a.shape[0] * a.shape[2]
        a_per = a.shape[2] // tk
        a_spec = pl.BlockSpec((None, tm, tk), lambda i, k: (k // a_per, i, k % a_per))
    else:
        c_dim = a.shape[1]
        a_spec = pl.BlockSpec((tm, tk), lambda i, k: (i, k))
    if w.ndim == 3:
        per = w.shape[2] // tk
        w_spec = pl.BlockSpec((None, d, tk), lambda i, k: (k // per, 0, k % per))
    else:
        w_spec = pl.BlockSpec((d, tk), lambda i, k: (0, k))
    assert geo.seg % tm == 0 and c_dim % tk == 0 and (w.shape[1] if w.ndim == 3 else w.shape[0]) == d
    nk = c_dim // tk
    n_in, n_out = (10, 9) if gated else (7, 6)
    carrier = _Carrier(job, n_in, n_out, 1 if nk > 1 else 0)

    def body(*refs):
        own, job_refs = carrier.split(refs)
        a_ref, w_ref, z_ref, g_ref, mx_ref, mc_ref, skip_ref = own[:7]
        outs = own[n_in:n_in + n_out]
        dz_ref, shx_ref, shc_ref, scx_ref, scc_ref, dg_ref = outs[:6]
        i, k = pl.program_id(0), pl.program_id(1)
        carrier.run(job_refs, i * nk + k, (geo.r // tm) * nk)
        part = _dot(a_ref[...], w_ref[...], "nt")

        def epilogue(dhv):
            is_x = (i % tiles) * tm + lax.broadcasted_iota(jnp.int32, (tm, 1), 0) < geo.s
            latent = is_x.astype(F32)
            first = i % tiles == 0

            def per_example(x_ref, c_ref, v):
                sx = _colsum(v * latent)
                sc = _colsum(v) - sx

                @pl.when(first)
                def _():
                    x_ref[...] = sx
                    c_ref[...] = sc

                @pl.when(jnp.logical_not(first))
                def _():
                    x_ref[...] += sx
                    c_ref[...] += sc

            zv, g = z_ref[...], g_ref[...]
            scale = jnp.where(is_x, mx_ref[:, off + d:off + 2 * d], mc_ref[:, off + d:off + 2 * d])
            r = lax.rsqrt(jnp.mean(zv * zv, axis=-1, keepdims=True) + EPS)
            n = zv * r
            dng = dhv * (1.0 + scale)
            dn = dng * g
            dz = r * (dn - n * jnp.mean(dn * n, axis=-1, keepdims=True)) + skip_ref[...]
            dz_ref[...] = dz
            per_example(shx_ref, shc_ref, dhv)
            per_example(scx_ref, scc_ref, dhv * (n * g))
            dgain = _colsum(dng * n)

            @pl.when(i == 0)
            def _():
                dg_ref[...] = dgain

            @pl.when(i > 0)
            def _():
                dg_ref[...] += dgain

            if gated:
                out_ref, gx_ref, gc_ref = own[7:10]
                dnext_ref, dgx_ref, dgc_ref = outs[6:9]
                go = gated[2]
                dnext_ref[...] = (dz * jnp.where(is_x, gx_ref[:, go:go + d], gc_ref[:, go:go + d])).astype(BF16)
                per_example(dgx_ref, dgc_ref, dz * out_ref[...].astype(F32))

        if nk == 1:
            epilogue(part)
            return
        acc_ref = own[n_in + n_out]

        @pl.when(k == 0)
        def _():
            acc_ref[...] = jnp.zeros_like(acc_ref)

        acc_ref[...] += part
        pl.when(k == nk - 1)(lambda: epilogue(acc_ref[...]))

    def mod_specs(m):
        return [pl.BlockSpec((None, 1, m.shape[2]), lambda i, k: (i // tiles, 0, 0)), pl.BlockSpec((None, 1, m.shape[2]), lambda i, k: (geo.b, 0, 0))]

    row = pl.BlockSpec((tm, d), lambda i, k: (i, 0))
    per_ex = pl.BlockSpec((None, 1, d), lambda i, k: (i // tiles, 0, 0))
    ex_shape = jax.ShapeDtypeStruct((geo.b, 1, d), F32)
    in_specs = [a_spec, w_spec, row, pl.BlockSpec((1, d), lambda i, k: (0, 0))] + mod_specs(mod) + [row]
    args = [a, w, z, gain, mod, mod, skip]
    out_specs = [row, per_ex, per_ex, per_ex, per_ex, pl.BlockSpec((1, d), lambda i, k: (0, 0))]
    out_shape = [jax.ShapeDtypeStruct((geo.r, d), F32)] + [ex_shape] * 4 + [jax.ShapeDtypeStruct((1, d), F32)]
    if gated:
        in_specs += [row] + mod_specs(gated[1])
        args += [gated[0], gated[1], gated[1]]
        out_specs += [row, per_ex, per_ex]
        out_shape += [jax.ShapeDtypeStruct((geo.r, d), BF16), ex_shape, ex_shape]
    res = pl.pallas_call(
        body, name=name, grid=(geo.r // tm, nk), in_specs=in_specs + carrier.in_specs(), out_specs=out_specs + carrier.out_specs(),
        out_shape=out_shape + carrier.out_shapes(),
        scratch_shapes=([pltpu.VMEM((tm, d), F32)] if nk > 1 else []) + carrier.scratch(), input_output_aliases=carrier.aliases(),
        compiler_params=_cparams("arbitrary", "arbitrary"),
    )(*args, *carrier.operands())
    own, extra = carrier.results(res)

    def both(latent_part, ctx_part):
        return jnp.concatenate([latent_part, ctx_part], axis=0)

    result = (own[0], both(own[1], own[2]), both(own[3], own[4]), own[5])
    if gated:
        result += (own[6], both(own[7], own[8]))
    return (*result, extra) if job else result


def _ffn_out_dx_swiglu_bwd(df, w_out, u, name, job=None):
    r, d = df.shape
    n4 = u.shape[2] // 2
    tm = _ffn_tile(r)
    carrier = _Carrier(job, 3, 1, 0)
    steps = (r // tm) * 2

    def body(*refs):
        (df_ref, w_ref, u_ref, du_ref), job_refs = carrier.split(refs)
        carrier.run(job_refs, pl.program_id(0) * 2 + pl.program_id(1), steps)
        da = _dot(df_ref[...], w_ref[...], "nt")
        g, up = u_ref[0].astype(F32), u_ref[1].astype(F32)
        s = _sigmoid(g)
        du_ref[0] = (da * up * (s * (1.0 + g * (1.0 - s)))).astype(BF16)
        du_ref[1] = (da * (g * s)).astype(BF16)

    res = pl.pallas_call(
        body, name=name, grid=(r // tm, 2),
        in_specs=[pl.BlockSpec((tm, d), lambda i, j: (i, 0)), pl.BlockSpec((n4, d), lambda i, j: (j, 0)),
                  pl.BlockSpec((2, tm, n4), lambda i, j: (0, i, j))] + carrier.in_specs(),
        out_specs=[pl.BlockSpec((2, tm, n4), lambda i, j: (0, i, j))] + carrier.out_specs(),
        out_shape=[jax.ShapeDtypeStruct(u.shape, BF16)] + carrier.out_shapes(),
        scratch_shapes=carrier.scratch(), input_output_aliases=carrier.aliases(),
        compiler_params=_cparams("arbitrary", "arbitrary"),
    )(df, w_out, u, *carrier.operands())
    (du,), extra = carrier.results(res)
    return du, extra


class _Rows:
    def __init__(self, b, s, l):
        self.b, self.s, self.l = b, s, l
        self.seg = s + l
        self.r = b * self.seg


def _rowwise(name, body, geo, tm, ins, outs, job=None):
    seg_blocks, x_blocks = geo.seg // tm, geo.s // tm
    assert geo.seg % tm == 0 and geo.s % tm == 0
    nb = geo.b

    def is_ctx(i):
        return i % seg_blocks >= x_blocks

    in_specs, args = [], []
    for arr, kind in ins:
        args.append(arr)
        if kind == "row":
            in_specs.append(pl.BlockSpec((tm, arr.shape[1]), lambda i: (i, 0)))
        elif kind == "ex":
            in_specs.append(pl.BlockSpec((None, 1, arr.shape[2]), lambda i: (jnp.where(is_ctx(i), nb, i // seg_blocks), 0, 0)))
        elif kind == "full":
            in_specs.append(pl.BlockSpec(arr.shape, lambda i, nd=arr.ndim: (0,) * nd))
        elif kind == "tab":
            in_specs.append(pl.BlockSpec((tm, arr.shape[1]), lambda i: (i % seg_blocks, 0)))
        elif kind == "xrow":
            in_specs.append(pl.BlockSpec(
                (tm, arr.shape[1]), lambda i: ((i // seg_blocks) * x_blocks + jnp.minimum(i % seg_blocks, x_blocks - 1), 0)))
        else:
            _, width, cb = kind
            in_specs.append(pl.BlockSpec((tm, width), lambda i, cb=cb: (i, cb)))
    out_specs, out_shapes = [], []
    for o in outs:
        if o[0] == "row":
            out_specs.append(pl.BlockSpec((tm, o[1]), lambda i: (i, 0)))
            out_shapes.append(jax.ShapeDtypeStruct((geo.r, o[1]), o[2]))
        elif o[0] == "xrow":
            out_specs.append(pl.BlockSpec(
                (tm, o[1]), lambda i: ((i // seg_blocks) * x_blocks + jnp.minimum(i % seg_blocks, x_blocks - 1), 0)))
            out_shapes.append(jax.ShapeDtypeStruct((geo.b * geo.s, o[1]), o[2]))
        elif o[0] == "exacc":
            out_specs.append(pl.BlockSpec((None, 1, o[1]), lambda i: (jnp.where(is_ctx(i), nb, 0) + i // seg_blocks, 0, 0)))
            out_shapes.append(jax.ShapeDtypeStruct((2 * nb, 1, o[1]), F32))
        else:
            out_specs.append(pl.BlockSpec((o[1], o[2]), lambda i: (0, 0)))
            out_shapes.append(jax.ShapeDtypeStruct((o[1], o[2]), F32))
    n_in = len(ins)
    carrier = _Carrier(job, n_in, len(outs), 0)

    def kern(*refs):
        i = pl.program_id(0)
        refs, job_refs = carrier.split(refs)
        carrier.run(job_refs, i, geo.r // tm)
        res = body(i, *[r[...].astype(F32) for r in refs[:n_in]])
        if not isinstance(res, (tuple, list)):
            res = (res,)
        jj = i % seg_blocks
        first_of_part = (jj == 0) | (jj == x_blocks)
        for o, ref, val in zip(outs, refs[n_in:], res):
            if o[0] == "row":
                ref[...] = val.astype(ref.dtype)
            elif o[0] == "xrow":
                @pl.when(jj < x_blocks)
                def _(ref=ref, val=val):
                    ref[...] = val.astype(ref.dtype)
            else:
                first = first_of_part if o[0] == "exacc" else i == 0

                @pl.when(first)
                def _(ref=ref, val=val):
                    ref[...] = val

                @pl.when(jnp.logical_not(first))
                def _(ref=ref, val=val):
                    ref[...] += val

    res = pl.pallas_call(
        kern, name=name, grid=(geo.r // tm,), in_specs=in_specs + carrier.in_specs(), out_specs=out_specs + carrier.out_specs(),
        out_shape=out_shapes + carrier.out_shapes(), scratch_shapes=carrier.scratch(), input_output_aliases=carrier.aliases(),
        compiler_params=_cparams("arbitrary"),
    )(*args, *carrier.operands())
    own, extra = carrier.results(res)
    if job:
        return (*own, extra)
    return own[0] if len(own) == 1 else own


def _colsum(v):
    return jnp.sum(v, axis=0, keepdims=True)


def _norm_mod(geo, z, gain, mod, off, name):
    d = D_MODEL

    def body(i, zv, g, m):
        r = lax.rsqrt(jnp.mean(zv * zv, axis=-1, keepdims=True) + EPS)
        return (zv * r) * g * (1.0 + m[:, off + d:off + 2 * d]) + m[:, off:off + d]

    return _rowwise(name, body, geo, 256, [(z, "row"), (gain, "full"), (mod, "ex")], [("row", d, BF16)])


def _norm_mod_bwd(geo, z, gain, mod, off, dh, dz_skip, name, gated=None, latent_only=False, job=None):
    d = D_MODEL

    def body(i, zv, g, m, dhv, skip, *rest):
        r = lax.rsqrt(jnp.mean(zv * zv, axis=-1, keepdims=True) + EPS)
        n = zv * r
        dng = dhv * (1.0 + m[:, off + d:off + 2 * d])
        dn = dng * g
        dz = r * (dn - n * jnp.mean(dn * n, axis=-1, keepdims=True)) + skip
        res = (dz, _colsum(dhv), _colsum(dhv * (n * g)), _colsum(dng * n))
        if gated:
            ov, gm = rest
            res += (dz * gm[:, gated[2]:gated[2] + d], _colsum(dz * ov))
        return res

    ins = [(z, "row"), (gain, "full"), (mod, "ex"), (dh, "row"), (dz_skip, "row")]
    outs = [("xrow" if latent_only else "row", d, F32), ("exacc", d), ("exacc", d), ("gacc", 1, d)]
    if gated:
        ins += [(gated[0], "row"), (gated[1], "ex")]
        outs += [("row", d, BF16), ("exacc", d)]
    return _rowwise(name, body, geo, 256, ins, outs, job)


def _loss_head(geo, z, target, out, mod, off, name):
    seg_blocks, x_blocks = geo.seg // 256, geo.s // 256
    d = D_MODEL

    def body(i, zv, tv, ov, m):
        keep = jnp.where(i % seg_blocks >= x_blocks, 0.0, 1.0)
        err = (zv - tv) * keep
        part = 0.5 * jnp.sum(jnp.mean(err * err, axis=-1, keepdims=True), axis=0, keepdims=True)
        dz = err * (1.0 / d)
        return dz, jnp.broadcast_to(part, (1, LANES)), dz * m[:, off:off + d], _colsum(dz * ov)

    return _rowwise(name, body, geo, 256, [(z, "row"), (target, "xrow"), (out, "row"), (mod, "ex")],
                    [("row", d, F32), ("gacc", 1, LANES), ("row", d, BF16), ("exacc", d)])


Q_SCALE = HEAD_DIM ** -0.5
N_QK_CHUNKS = (N_HEADS + N_KV_HEADS) * HEAD_DIM // LANES
N_Q_CHUNKS = N_HEADS * HEAD_DIM // LANES


def _attn_prep(geo, proj, cos, sin_signed, q_gain, k_gain, name):
    def body(i, p, cs, sn, qg, kg):
        outs = []
        for ch in range(N_QK_CHUNKS):
            is_q = ch < N_Q_CHUNKS
            outs.append(_qk_chunk(p[:, ch * LANES:(ch + 1) * LANES], qg if is_q else kg, cs, sn, Q_SCALE if is_q else 1.0))
        outs.append(p[:, N_QK_CHUNKS * LANES:])
        return jnp.concatenate(outs, axis=1)

    return _rowwise(name, body, geo, 256, [(proj, "row"), (cos, "tab"), (sin_signed, "tab"), (q_gain, "full"), (k_gain, "full")],
                    [("row", proj.shape[1], BF16)])


def _attn_prep_bwd(geo, proj, cos, sin_signed, q_gain, k_gain, dq, dkv, name):
    kw = N_KV_HEADS * HEAD_DIM

    def body(i, p, cs, sn, qg, kg, dqv, dkvv):
        outs = []
        dgains = [jnp.zeros((1, LANES), F32), jnp.zeros((1, LANES), F32)]
        for ch in range(N_QK_CHUNKS):
            is_q = ch < N_Q_CHUNKS
            scale = Q_SCALE if is_q else 1.0
            ct = dqv[:, ch * LANES:(ch + 1) * LANES] if is_q else dkvv[:, (ch - N_Q_CHUNKS) * LANES:(ch - N_Q_CHUNKS + 1) * LANES]
            _, vjp = jax.vjp(lambda xx, gg, scale=scale: _qk_chunk(xx, gg, cs, sn, scale),
                             p[:, ch * LANES:(ch + 1) * LANES], qg if is_q else kg)
            dx, dg = vjp(ct)
            outs.append(dx)
            dgains[0 if is_q else 1] = dgains[0 if is_q else 1] + dg
        outs.append(dkvv[:, kw:])
        return jnp.concatenate(outs, axis=1), dgains[0], dgains[1]

    return _rowwise(name, body, geo, 256,
                    [(proj, "row"), (cos, "tab"), (sin_signed, "tab"), (q_gain, "full"), (k_gain, "full"), (dq, "row"), (dkv, "row")],
                    [("row", proj.shape[1], BF16), ("gacc", 1, LANES), ("gacc", 1, LANES)])


def _attn_geometry(geo):
    assert geo.s % ATTN_BLOCK == 0 and geo.l % ATTN_BLOCK == 0 and geo.seg >= BAND
    return geo.seg // ATTN_BLOCK, geo.s // ATTN_BLOCK


def _attn_mask(j, s0, geo):
    r = lax.broadcasted_iota(jnp.int32, (ATTN_BLOCK, BAND), 0)
    n = lax.broadcasted_iota(jnp.int32, (ATTN_BLOCK, BAND), 1)
    dist = (s0 - j * ATTN_BLOCK) + n - r
    return (jnp.abs(dist) <= WINDOW) & (s0 + n < geo.s)


def _attn_probs(q, keys, valid, n_ctx, sink):
    s = _dot(q, keys, "nt")
    if valid is not None:
        s = jnp.concatenate([s[:, :n_ctx], jnp.where(valid, s[:, n_ctx:], NEG_INF)], axis=1)
    m = jnp.maximum(jnp.max(s, axis=-1, keepdims=True), sink)
    e, e_sink = jnp.exp(s - m), jnp.exp(sink - m)
    inv = 1.0 / (jnp.sum(e, axis=-1, keepdims=True) + e_sink)
    return e * inv, e_sink * inv


def _attn_keys(ref, s0, geo, with_band):
    ctx = ref[geo.s:geo.seg, :]
    return jnp.concatenate([ctx, ref[pl.ds(s0, BAND), :]], axis=0) if with_band else ctx


def _attention(geo, qkv, sink, name, job=None):
    n_blocks, n_x_blocks = _attn_geometry(geo)
    qw, kw = N_HEADS * HEAD_DIM, N_KV_HEADS * HEAD_DIM
    group = N_HEADS // N_KV_HEADS
    carrier = _Carrier(job, 4, 1, 0)

    def kern(*refs):
        (sink_ref, q_ref, k_ref, v_ref, o_ref), job_refs = carrier.split(refs)
        j = pl.program_id(1)
        carrier.run(job_refs, pl.program_id(0) * n_blocks + j, geo.b * n_blocks)
        s0 = pl.multiple_of(jnp.clip((j - 1) * ATTN_BLOCK, 0, geo.seg - BAND), ATTN_BLOCK)

        def heads(with_band):
            valid = _attn_mask(j, s0, geo) if with_band else None
            k_all, v_all = _attn_keys(k_ref, s0, geo, with_band), _attn_keys(v_ref, s0, geo, with_band)
            for h in range(N_HEADS):
                kv = slice((h // group) * HEAD_DIM, (h // group + 1) * HEAD_DIM)
                p, _ = _attn_probs(q_ref[:, h * HEAD_DIM:(h + 1) * HEAD_DIM], k_all[:, kv], valid, geo.l, sink_ref[h])
                o_ref[:, h * HEAD_DIM:(h + 1) * HEAD_DIM] = _dot(p, v_all[:, kv], "nn").astype(BF16)

        pl.when(j < n_x_blocks)(lambda: heads(True))
        pl.when(j >= n_x_blocks)(lambda: heads(False))

    res = pl.pallas_call(
        kern, name=name, grid=(geo.b, n_blocks),
        in_specs=[pl.BlockSpec(memory_space=pltpu.SMEM),
                  pl.BlockSpec((ATTN_BLOCK, qw), lambda b, j: (b * n_blocks + j, 0)),
                  pl.BlockSpec((geo.seg, kw), lambda b, j: (b, qw // kw)),
                  pl.BlockSpec((geo.seg, kw), lambda b, j: (b, qw // kw + 1))] + carrier.in_specs(),
        out_specs=[pl.BlockSpec((ATTN_BLOCK, qw), lambda b, j: (b * n_blocks + j, 0))] + carrier.out_specs(),
        out_shape=[jax.ShapeDtypeStruct((geo.r, qw), BF16)] + carrier.out_shapes(),
        scratch_shapes=carrier.scratch(), input_output_aliases=carrier.aliases(),
        compiler_params=_cparams("arbitrary", "arbitrary"),
    )(sink, qkv, qkv, qkv, *carrier.operands())
    (o,), extra = carrier.results(res)
    return o, extra


def _attention_bwd(geo, qkv, sink, do, name, job=None):
    n_blocks, n_x_blocks = _attn_geometry(geo)
    qw, kw = N_HEADS * HEAD_DIM, N_KV_HEADS * HEAD_DIM
    group = N_HEADS // N_KV_HEADS

    carrier = _Carrier(job, 5, 3, 1)

    def kern(*refs):
        (sink_ref, q_ref, k_ref, v_ref, do_ref, dq_ref, dkv_out_ref, dsink_ref, dkv_ref), job_refs = carrier.split(refs)
        b, j = pl.program_id(0), pl.program_id(1)
        carrier.run(job_refs, b * n_blocks + j, geo.b * n_blocks)
        s0 = pl.multiple_of(jnp.clip((j - 1) * ATTN_BLOCK, 0, geo.seg - BAND), ATTN_BLOCK)

        @pl.when(j == 0)
        def _():
            dkv_ref[...] = jnp.zeros_like(dkv_ref)

        @pl.when((j == 0) & (b == 0))
        def _():
            dsink_ref[...] = jnp.zeros_like(dsink_ref)

        def heads(with_band):
            valid = _attn_mask(j, s0, geo) if with_band else None
            k_all, v_all = _attn_keys(k_ref, s0, geo, with_band), _attn_keys(v_ref, s0, geo, with_band)
            for g in range(N_KV_HEADS):
                kv = slice(g * HEAD_DIM, (g + 1) * HEAD_DIM)
                keys, vals = k_all[:, kv], v_all[:, kv]
                group_heads = [slice(h * HEAD_DIM, (h + 1) * HEAD_DIM) for h in range(g * group, (g + 1) * group)]
                ds_rows, p_rows = [], []
                for h, hs in zip(range(g * group, (g + 1) * group), group_heads):
                    dout = do_ref[:, hs]
                    p, p_sink = _attn_probs(q_ref[:, hs], keys, valid, geo.l, sink_ref[h])
                    dp = _dot(dout, vals, "nt")
                    dsum = jnp.sum(p * dp, axis=-1, keepdims=True)
                    ds = (p * (dp - dsum)).astype(BF16)
                    dq_ref[:, hs] = _dot(ds, keys, "nn").astype(BF16)
                    ds_rows.append(ds)
                    p_rows.append(p.astype(BF16))
                    dsink_ref[h:h + 1, :] += jnp.broadcast_to(-jnp.sum(p_sink * dsum, axis=0, keepdims=True), (1, LANES))
                q_rows = jnp.concatenate([q_ref[:, hs] for hs in group_heads], axis=0)
                do_rows = jnp.concatenate([do_ref[:, hs] for hs in group_heads], axis=0)
                dk = _dot(jnp.concatenate(ds_rows, axis=0), q_rows, "tn")
                dv = _dot(jnp.concatenate(p_rows, axis=0), do_rows, "tn")
                vv = slice(kw + g * HEAD_DIM, kw + (g + 1) * HEAD_DIM)
                dkv_ref[geo.s:geo.seg, kv] += dk[:geo.l]
                dkv_ref[geo.s:geo.seg, vv] += dv[:geo.l]
                if with_band:
                    dkv_ref[pl.ds(s0, BAND), kv] += dk[geo.l:]
                    dkv_ref[pl.ds(s0, BAND), vv] += dv[geo.l:]

        pl.when(j < n_x_blocks)(lambda: heads(True))
        pl.when(j >= n_x_blocks)(lambda: heads(False))

        @pl.when(j == n_blocks - 1)
        def _():
            dkv_out_ref[...] = dkv_ref[...].astype(BF16)

    res = pl.pallas_call(
        kern, name=name, grid=(geo.b, n_blocks),
        in_specs=[pl.BlockSpec(memory_space=pltpu.SMEM),
                  pl.BlockSpec((ATTN_BLOCK, qw), lambda b, j: (b * n_blocks + j, 0)),
                  pl.BlockSpec((geo.seg, kw), lambda b, j: (b, qw // kw)),
                  pl.BlockSpec((geo.seg, kw), lambda b, j: (b, qw // kw + 1)),
                  pl.BlockSpec((ATTN_BLOCK, qw), lambda b, j: (b * n_blocks + j, 0))] + carrier.in_specs(),
        out_specs=[pl.BlockSpec((ATTN_BLOCK, qw), lambda b, j: (b * n_blocks + j, 0)),
                   pl.BlockSpec((geo.seg, 2 * kw), lambda b, j: (b, 0)),
                   pl.BlockSpec((N_HEADS, LANES), lambda b, j: (0, 0))] + carrier.out_specs(),
        out_shape=[jax.ShapeDtypeStruct((geo.r, qw), BF16), jax.ShapeDtypeStruct((geo.r, 2 * kw), BF16),
                   jax.ShapeDtypeStruct((N_HEADS, LANES), F32)] + carrier.out_shapes(),
        scratch_shapes=[pltpu.VMEM((geo.seg, 2 * kw), F32)] + carrier.scratch(), input_output_aliases=carrier.aliases(),
        compiler_params=_cparams("arbitrary", "arbitrary"),
    )(sink, qkv, qkv, qkv, do, *carrier.operands())
    (dq, dkv, dsink), extra = carrier.results(res)
    return dq, dkv, dsink, extra


RET_QK_W = RET_HEADS * RET_QK_DIM
K_SCALE = RET_QK_DIM ** -0.5


def _ret_prep(geo, proj, cos, sin_signed, name):
    def body(i, p, cs, sn):
        cs2, sn2 = jnp.concatenate([cs] * RET_HEADS, axis=1), jnp.concatenate([sn] * RET_HEADS, axis=1)
        q = _rope(p[:, :RET_QK_W], cs2, sn2, RET_QK_DIM // 4)
        k = _rope(p[:, RET_QK_W:2 * RET_QK_W], cs2, sn2, RET_QK_DIM // 4) * K_SCALE
        return jnp.concatenate([q, k, p[:, 2 * RET_QK_W:]], axis=1)

    return _rowwise(name, body, geo, 128, [(proj, ("rowc", 2 * RET_QK_W + RET_VWIDTH, 0)), (cos, "tab"), (sin_signed, "tab")],
                    [("row", 2 * RET_QK_W + RET_VWIDTH, BF16)])


def _ret_prep_bwd(geo, dq, dk, dv, dgate, cos, sin_signed, name):
    def body(i, dqv, dkv, dvv, dg, cs, sn):
        cs2, sn2 = jnp.concatenate([cs] * RET_HEADS, axis=1), jnp.concatenate([sn] * RET_HEADS, axis=1)
        dkv = dkv * K_SCALE
        dqv = dqv * cs2 + _swap_halves(dqv * sn2, RET_QK_DIM // 4)
        dkv = dkv * cs2 + _swap_halves(dkv * sn2, RET_QK_DIM // 4)
        return jnp.concatenate([dqv, dkv, dvv, dg], axis=1)

    return _rowwise(name, body, geo, 128,
                    [(dq, "row"), (dk, "row"), (dv, "row"), (dgate, "row"), (cos, "tab"), (sin_signed, "tab")],
                    [("row", 2 * RET_QK_W + 2 * RET_VWIDTH, BF16)])


def _ret_step(state, q, k, v, lg, rev):
    c = RET_CHUNK
    ri = lax.broadcasted_iota(jnp.int32, (c, 1), 0).astype(F32)
    cj = lax.broadcasted_iota(jnp.int32, (1, c), 1).astype(F32)
    if rev:
        dist, q_decay, k_decay = cj - ri, jnp.exp(lg * (c - ri)), jnp.exp(lg * ri)
    else:
        dist, q_decay, k_decay = ri - cj, jnp.exp(lg * (ri + 1.0)), jnp.exp(lg * (c - 1.0 - ri))
    intra = jnp.where(dist >= 0, jnp.exp(lg * jnp.maximum(dist, 0.0)), 0.0)
    scores = _mm(q, k, "nt") * intra
    out = _mm(scores, v, "nn") + _mm(q, state, "nn") * q_decay
    new_state = state * jnp.exp(lg * c) + _mm(k * k_decay, v, "tn")
    return new_state, out


def _ret_state0(kc, vc, lg, rev):
    n = kc.shape[0]
    t = lax.broadcasted_iota(jnp.int32, (n, 1), 0).astype(F32)
    decay = jnp.exp(lg * t) if rev else jnp.exp(lg * (n - 1.0 - t))
    return _mm(kc * decay, vc, "tn")


def _ret_specs(geo):
    nq = RET_HEADS
    return [pl.BlockSpec((2 * RET_HEADS, LANES), lambda b, h: (0, 0)),
            pl.BlockSpec((geo.seg, RET_QK_DIM), lambda b, h: (b, h)),
            pl.BlockSpec((geo.seg, RET_QK_DIM), lambda b, h: (b, nq + h)),
            pl.BlockSpec((geo.seg, RET_V_DIM), lambda b, h: (b, nq + h))]


def _retention(geo, qkv, log_g, name):
    nc = geo.s // RET_CHUNK

    def kern(lg_ref, q_ref, k_ref, v_ref, o_ref, st_ref):
        h = pl.program_id(1)
        for d, rev in ((0, False), (1, True)):
            lg = lg_ref[pl.ds(d * RET_HEADS + h, 1), 0:1]
            st_ref[...] = _ret_state0(k_ref[geo.s:geo.seg, :].astype(F32), v_ref[geo.s:geo.seg, :].astype(F32), lg, rev)

            def chunk(ci, carry, d=d, rev=rev, lg=lg):
                r0 = pl.multiple_of((nc - 1 - ci if rev else ci) * RET_CHUNK, RET_CHUNK)
                rows = pl.ds(r0, RET_CHUNK)
                new_state, out = _ret_step(st_ref[...], q_ref[rows, :].astype(F32), k_ref[rows, :].astype(F32),
                                           v_ref[rows, :].astype(F32), lg, rev)
                st_ref[...] = new_state
                if d == 0:
                    o_ref[rows, :] = out
                else:
                    o_ref[rows, :] += out
                return carry

            lax.fori_loop(0, nc, chunk, 0)
        o_ref[geo.s:geo.seg, :] = jnp.zeros((geo.l, RET_V_DIM), F32)

    return pl.pallas_call(
        kern, name=name, grid=(geo.b, RET_HEADS), in_specs=_ret_specs(geo),
        out_specs=pl.BlockSpec((geo.seg, RET_V_DIM), lambda b, h: (b, h)),
        out_shape=jax.ShapeDtypeStruct((geo.r, RET_VWIDTH), F32),
        scratch_shapes=[pltpu.VMEM((RET_QK_DIM, RET_V_DIM), F32)],
        compiler_params=_cparams("parallel", "arbitrary"),
    )(log_g, qkv, qkv, qkv)


def _retention_bwd(geo, qkv, log_g, do, name):
    nc = geo.s // RET_CHUNK
    ctx = slice(geo.s, geo.seg)

    def kern(lg_ref, q_ref, k_ref, v_ref, do_ref, dq_ref, dk_ref, dv_ref, dlg_ref, states_ref, dst_ref, aq_ref, ak_ref, av_ref):
        b, h = pl.program_id(0), pl.program_id(1)

        @pl.when((b == 0) & (h == 0))
        def _():
            dlg_ref[...] = jnp.zeros_like(dlg_ref)

        for d, rev in ((0, False), (1, True)):
            row = pl.ds(d * RET_HEADS + h, 1)
            lg = lg_ref[row, 0:1]
            kc, vc = k_ref[ctx, :].astype(F32), v_ref[ctx, :].astype(F32)
            states_ref[0] = _ret_state0(kc, vc, lg, rev)

            def rows_of(ci, rev=rev):
                return pl.ds(pl.multiple_of((nc - 1 - ci if rev else ci) * RET_CHUNK, RET_CHUNK), RET_CHUNK)

            def load(rows):
                return q_ref[rows, :].astype(F32), k_ref[rows, :].astype(F32), v_ref[rows, :].astype(F32)

            def replay(ci, carry, rev=rev, lg=lg, rows_of=rows_of, load=load):
                states_ref[ci + 1] = _ret_step(states_ref[ci], *load(rows_of(ci)), lg, rev)[0]
                return carry

            lax.fori_loop(0, nc - 1, replay, 0)
            dst_ref[...] = jnp.zeros_like(dst_ref)

            def emit(rows, dq, dk, dv, d=d):
                if d == 0:
                    ak_ref[rows, :], av_ref[rows, :] = dk, dv
                    if dq is not None:
                        aq_ref[rows, :] = dq
                else:
                    dk_ref[rows, :] = (ak_ref[rows, :] + dk).astype(BF16)
                    dv_ref[rows, :] = (av_ref[rows, :] + dv).astype(BF16)
                    if dq is not None:
                        dq_ref[rows, :] = (aq_ref[rows, :] + dq).astype(BF16)

            def back(t, dlg, rev=rev, lg=lg, rows_of=rows_of, load=load, emit=emit):
                ci = nc - 1 - t
                rows = rows_of(ci)
                _, vjp = jax.vjp(lambda st, q, k, v, g: _ret_step(st, q, k, v, g, rev), states_ref[ci], *load(rows), lg)
                dstate, dq, dk, dv, dg = vjp((dst_ref[...], do_ref[rows, :].astype(F32)))
                dst_ref[...] = dstate
                emit(rows, dq, dk, dv)
                return dlg + dg

            dlg = lax.fori_loop(0, nc, back, jnp.zeros((1, 1), F32))
            _, vjp = jax.vjp(lambda kk, vv, g: _ret_state0(kk, vv, g, rev), kc, vc, lg)
            dkc, dvc, dg = vjp(dst_ref[...])
            emit(ctx, None, dkc, dvc)
            dlg_ref[row, :] += jnp.broadcast_to(dlg + dg, (1, LANES))
        dq_ref[ctx, :] = jnp.zeros((geo.l, RET_QK_DIM), BF16)

    nq = RET_HEADS
    return pl.pallas_call(
        kern, name=name, grid=(geo.b, RET_HEADS),
        in_specs=_ret_specs(geo) + [pl.BlockSpec((geo.seg, RET_V_DIM), lambda b, h: (b, h))],
        out_specs=[pl.BlockSpec((geo.seg, RET_QK_DIM), lambda b, h: (b, h)),
                   pl.BlockSpec((geo.seg, RET_QK_DIM), lambda b, h: (b, h)),
                   pl.BlockSpec((geo.seg, RET_V_DIM), lambda b, h: (b, h)),
                   pl.BlockSpec((2 * RET_HEADS, LANES), lambda b, h: (0, 0))],
        out_shape=[jax.ShapeDtypeStruct((geo.r, RET_QK_W), BF16), jax.ShapeDtypeStruct((geo.r, RET_QK_W), BF16),
                   jax.ShapeDtypeStruct((geo.r, RET_VWIDTH), BF16), jax.ShapeDtypeStruct((2 * RET_HEADS, LANES), F32)],
        scratch_shapes=[pltpu.VMEM((nc, RET_QK_DIM, RET_V_DIM), F32), pltpu.VMEM((RET_QK_DIM, RET_V_DIM), F32),
                        pltpu.VMEM((geo.seg, RET_QK_DIM), F32), pltpu.VMEM((geo.seg, RET_QK_DIM), F32),
                        pltpu.VMEM((geo.seg, RET_V_DIM), F32)],
        compiler_params=_cparams("arbitrary", "arbitrary"),
    )(log_g, qkv, qkv, qkv, do)


def _gated(o, g, gain):
    outs = []
    for h in range(RET_HEADS):
        cols = slice(h * RET_V_DIM, (h + 1) * RET_V_DIM)
        oh = o[:, cols]
        mu = jnp.mean(oh, axis=-1, keepdims=True)
        var = jnp.mean(jnp.square(oh - mu), axis=-1, keepdims=True)
        outs.append(_silu(g[:, cols]) * ((oh - mu) * lax.rsqrt(var + EPS) * gain[:, cols]))
    return jnp.concatenate(outs, axis=1)


def _ret_gated(geo, o, proj, gain, name):
    def body(i, ov, gv, gn):
        return _gated(ov, gv, gn)

    gate_block = (2 * RET_QK_W + RET_VWIDTH) // RET_VWIDTH
    return _rowwise(name, body, geo, 128, [(o, "row"), (proj, ("rowc", RET_VWIDTH, gate_block)), (gain, "full")],
                    [("row", RET_VWIDTH, BF16)])


def _ret_gated_bwd(geo, o, proj, gain, dout, name):
    def body(i, ov, gv, gn, dv):
        _, vjp = jax.vjp(_gated, ov, gv, gn)
        return vjp(dv)

    gate_block = (2 * RET_QK_W + RET_VWIDTH) // RET_VWIDTH
    return _rowwise(name, body, geo, 128,
                    [(o, "row"), (proj, ("rowc", RET_VWIDTH, gate_block)), (gain, "full"), (dout, "row")],
                    [("row", RET_VWIDTH, BF16), ("row", RET_VWIDTH, BF16), ("gacc", 1, RET_VWIDTH)])


def _whole(name, fn, out_shapes, *arrays):
    n = len(arrays)

    def kern(*refs):
        res = fn(*[r[...] for r in refs[:n]])
        for ref, val in zip(refs[n:], res):
            ref[...] = val.astype(ref.dtype)

    return pl.pallas_call(kern, name=name, out_shape=out_shapes)(*arrays)


def _rope_tables(geo, head_dim):
    rows = geo.s // GRID_W
    row = jnp.broadcast_to(jnp.arange(rows, dtype=jnp.int32)[:, None], (rows, GRID_W)).reshape(geo.s)
    col = jnp.broadcast_to(jnp.arange(GRID_W, dtype=jnp.int32)[None, :], (rows, GRID_W)).reshape(geo.s)
    axis_dim = head_dim // 2
    inv = ROPE_BASE ** (-jnp.arange(0, axis_dim, 2, dtype=F32) / axis_dim)
    ang_r = row.astype(F32)[:, None] * inv
    ang_c = col.astype(F32)[:, None] * inv
    cos = jnp.concatenate([jnp.cos(ang_r)] * 2 + [jnp.cos(ang_c)] * 2, axis=1)
    sin = jnp.concatenate([-jnp.sin(ang_r), jnp.sin(ang_r), -jnp.sin(ang_c), jnp.sin(ang_c)], axis=1)
    cos = jnp.concatenate([cos, jnp.ones((geo.l, head_dim), F32)], axis=0)
    sin = jnp.concatenate([sin, jnp.zeros((geo.l, head_dim), F32)], axis=0)
    reps = max(1, LANES // head_dim)
    return jnp.tile(cos, (1, reps)), jnp.tile(sin, (1, reps))


def _row_tile(r):
    return next(t for t in (1024, 512, 256, 128) if r % t == 0)


MOD_ROWS = 8


def _local_step(x, c, ctx, target, sp, wts, plan=None):
    nb, s, d = x.shape
    geo = _Rows(nb, s, ctx.shape[1])
    assert nb + 1 <= MOD_ROWS and d == D_MODEL
    tm = _row_tile(geo.r)
    z = jnp.concatenate([x, ctx], axis=1).reshape(geo.r, d)
    cvec = jnp.concatenate([c, sp["c_ctx"][None, :], jnp.zeros((MOD_ROWS - nb - 1, d), F32)], axis=0)
    cact, = _whole("cond_silu", lambda v: (_silu(v),), [jax.ShapeDtypeStruct(cvec.shape, F32)], cvec)
    cos64, sin64 = _rope_tables(geo, HEAD_DIM)
    cos256, sin256 = _rope_tables(geo, RET_QK_DIM)
    q_gain = jnp.tile(sp["q_norm"].reshape(1, HEAD_DIM), (1, LANES // HEAD_DIM))
    k_gain = jnp.tile(sp["k_norm"].reshape(1, HEAD_DIM), (1, LANES // HEAD_DIM))
    sink = sp["sink"].reshape(N_HEADS)
    log_g = jnp.broadcast_to(sp["log_g"].reshape(2 * RET_HEADS, 1), (2 * RET_HEADS, LANES))
    gn_g = sp["gn_g"].reshape(1, RET_VWIDTH)

    def modulation(i):
        mod = _mm_nn(cact, wts["ada"][i], F32, f"mod{i}", MOD_ROWS, wts["ada"][i].shape[2], d, bias=sp["ada_b"][i][None, :])
        return mod[:nb + 1, None, :]

    saved = []
    mods = [modulation(0), None]
    h1 = _norm_mod(geo, z, sp["norm1_g"][0][None, :], mods[0], 0, "norm1_0")
    for i in range(2):
        mod3 = mods[i]
        n1, n2 = sp["norm1_g"][i][None, :], sp["norm2_g"][i][None, :]
        if i == 0:
            proj = _mm_nn(h1, wts["attn_qkv"], F32, "attn_qkv", tm, wts["attn_qkv"].shape[1], d)
            prep = _attn_prep(geo, proj, cos64, sin64, q_gain, k_gain, "attn_prep")
            o, late = _attention(geo, prep, sink, "attn", plan.gather_job() if plan else None)
            if plan:
                plan.late_weights(late, wts)
            mods[1] = modulation(1)
            oraw = None
            w_o = wts["attn_o"]
        else:
            proj = _mm_nn(h1, wts["ret_qkvg"], BF16, "ret_qkvg", tm, wts["ret_qkvg"].shape[2], d)
            prep = _ret_prep(geo, proj, cos256, sin256, "ret_prep")
            oraw = _retention(geo, prep, log_g, "ret")
            o = _ret_gated(geo, oraw, proj, gn_g, "ret_gated")
            w_o = wts["ret_o"]
        zmid, mix, h2 = _mm_nn_gate_residual(geo, o, w_o, z, mod3, 2 * d, f"mix_out{i}", norm=(n2, mod3, 3 * d))
        u, a = _ffn_in_swiglu(h2, wts["ffn_in"][i], f"ffn_in{i}")
        next_norm = (sp["norm1_g"][1][None, :], mods[1], 0) if i == 0 else None
        zout, f, h1_next = _mm_nn_gate_residual(geo, a, wts["ffn_out"][i], zmid, mod3, 5 * d, f"ffn_out{i}", norm=next_norm)
        saved.append(dict(z=z, mod3=mod3, n1=n1, n2=n2, h1=h1, proj=proj, prep=prep, o=o, oraw=oraw, mix=mix, zmid=zmid,
                          h2=h2, u=u, a=a, f=f))
        z, h1 = zout, h1_next

    dz, loss, df, dg2 = _loss_head(geo, z, target.reshape(nb * s, d), saved[1]["f"], saved[1]["mod3"], 5 * d, "loss")

    big, small = {}, {}
    dmods = [None, None]
    for i in (1, 0):
        sv = saved[i]
        mod3 = sv["mod3"]
        carry = plan is not None and i == 0
        du, land = _ffn_out_dx_swiglu_bwd(df, wts["ffn_out"][i], sv["u"], f"ffn_out_dx{i}", plan.layer1.swap_job() if carry else None)
        if carry:
            plan.layer1.after_swap(land)
        big[f"ffn_out{i}"] = _mm_tn(sv["a"], df, f"ffn_out_dw{i}", D_FF // 2, 1024, tm).reshape(N_CHIPS, D_FF // N_CHIPS, d)
        n4 = wts["ffn_in"][i].shape[2]
        big[f"ffn_in{i}"] = _mm_tn(sv["h2"], du, f"ffn_in_dw{i}", 1024, n4, tm, shards=N_CHIPS)
        if carry:
            plan.start_layer0_ffn(big)
        dzmid, dsh2, dsc2, dn2, dmix, dg1, *land = _mm_nt_norm_bwd(
            geo, du, wts["ffn_in"][i], n4, sv["zmid"], sv["n2"], mod3, 3 * d, dz, f"ffn_in_dx{i}",
            gated=(sv["mix"], mod3, 2 * d), job=plan.layer0_ffn.swap_job() if carry else None)
        if carry:
            plan.layer0_ffn.after_swap(land[0])
        if i == 0:
            do = _mm_nt(dmix, wts["attn_o"], BF16, "attn_out_dx", tm, 1024, 1024)
            big["attn_o"] = _mm_tn(sv["o"], dmix, "attn_out_dw", 1024, 1024, tm).reshape(N_CHIPS, 1024 // N_CHIPS, d)
            dq, dkv, dsink, land = _attention_bwd(geo, sv["prep"], sink, do, "attn_bwd", plan.exchange_job() if plan else None)
            if plan:
                plan.after_exchange(land)
            dproj, dqg, dkg = _attn_prep_bwd(geo, sv["proj"], cos64, sin64, q_gain, k_gain, dq, dkv, "attn_prep_bwd")
            small["q_norm"] = dqg[0, :HEAD_DIM] + dqg[0, HEAD_DIM:]
            small["k_norm"] = dkg[0, :HEAD_DIM] + dkg[0, HEAD_DIM:]
            small["sink"] = dsink[:, 0]
            wq = wts["attn_qkv"]
            dh1 = _mm_nt(dproj, wq, BF16, "attn_qkv_dx", tm, 1024, wq.shape[1])
            dwq = _mm_tn(sv["h1"], dproj, "attn_qkv_dw", 1024, wq.shape[1], tm)
            big["attn_qkv"] = dwq.reshape(d, N_CHIPS, -1).transpose(1, 0, 2)
            dz, dsh1, dsc1, dn1 = _norm_mod_bwd(geo, sv["z"], sv["n1"], mod3, 0, dh1, dzmid, "norm1_bwd0", latent_only=True)
            below_grads = []
        else:
            do = _mm_nt(dmix, wts["ret_o"], BF16, "ret_out_dx", tm, 1024, 1024)
            big["ret_o"] = _mm_tn(sv["o"], dmix, "ret_out_dw", 1024, 1024, tm).reshape(N_CHIPS, RET_VWIDTH // N_CHIPS, d)
            doraw, dgate, dgn = _ret_gated_bwd(geo, sv["oraw"], sv["proj"], gn_g, do, "ret_gated_bwd")
            small["gn_g"] = dgn[0]
            dq, dk, dv, dlg = _retention_bwd(geo, sv["prep"], log_g, doraw, "ret_bwd")
            small["log_g"] = dlg[:, 0].reshape(2, RET_HEADS)
            dproj = _ret_prep_bwd(geo, dq, dk, dv, dgate, cos256, sin256, "ret_prep_bwd")
            wq = wts["ret_qkvg"]
            big["ret_qkvg"] = _mm_tn(sv["h1"], dproj, "ret_qkvg_dw", 1024, wq.shape[2], tm, shards=N_CHIPS)
            dz, dsh1, dsc1, dn1, *below_grads = _mm_nt_norm_bwd(
                geo, dproj, wq, wq.shape[2], sv["z"], sv["n1"], mod3, 0, dzmid, "ret_qkvg_dx",
                gated=(saved[0]["f"], saved[0]["mod3"], 5 * d))
        small[f"norm1_g{i}"], small[f"norm2_g{i}"] = dn1[0], dn2[0]
        parts = [dsh1, dsc1, dg1, dsh2, dsc2, dg2]
        rows = jnp.concatenate([jnp.concatenate([p[:nb, 0, :] for p in parts], axis=1),
                                jnp.concatenate([jnp.sum(p[nb:, 0, :], axis=0, keepdims=True) for p in parts], axis=1),
                                jnp.zeros((MOD_ROWS - nb - 1, 6 * d), F32)], axis=0)
        dmods[i] = rows
        if below_grads:
            df, dg2 = below_grads
        small[f"ada_b{i}"] = jnp.sum(rows, axis=0)
        big[f"ada{i}"] = _mm_tn(cact, rows, f"ada_dw{i}", 1024, wts["ada"][i].shape[2], MOD_ROWS, shards=N_CHIPS)
        if plan and i == 1:
            plan.start_layer1(big)

    dcact = [_mm_nt(dmods[i], wts["ada"][i], F32, f"ada_dx{i}", MOD_ROWS, 1024, wts["ada"][i].shape[2]) for i in range(2)]

    def silu_bwd(v, d0, d1):
        sg = _sigmoid(v)
        return ((d0 + d1) * (sg * (1.0 + v * (1.0 - sg))),)

    dcvec, = _whole("cond_silu_bwd", silu_bwd, [jax.ShapeDtypeStruct(cvec.shape, F32)], cvec, dcact[0], dcact[1])
    small["c_ctx"] = dcvec[nb]
    return loss, dz, big, small


def _adamw(w, g, m, v, name):
    rows, cols = w.shape
    tr = next((t for t in (256, 128, 64, 32, 16, 8) if rows % t == 0), rows)
    c1 = 1.0 - ADAM_B1 ** ADAM_STEP
    c2 = 1.0 - ADAM_B2 ** ADAM_STEP

    def kern(w_ref, g_ref, m_ref, v_ref, d_ref, nm_ref, nv_ref):
        gv = g_ref[...]
        nm = ADAM_B1 * m_ref[...] + (1.0 - ADAM_B1) * gv
        nv = ADAM_B2 * v_ref[...] + (1.0 - ADAM_B2) * jnp.square(gv)
        d_ref[...] = -ADAM_LR * ((nm / c1) / (jnp.sqrt(nv / c2) + ADAM_EPS) + ADAM_WD * w_ref[...])
        nm_ref[...] = nm
        nv_ref[...] = nv

    spec = pl.BlockSpec((tr, cols), lambda i: (i, 0))
    return pl.pallas_call(
        kern, name=name, grid=(rows // tr,), in_specs=[spec] * 4, out_specs=[spec] * 3,
        out_shape=[jax.ShapeDtypeStruct(w.shape, F32)] * 3, compiler_params=_cparams("parallel"),
    )(w, g, m, v)


N_DEVICES = 8


def _mesh_pos():
    return lax.axis_index("x"), lax.axis_index("y"), lax.axis_index("c")


def _other_chips(x, y):
    return [(1 - x, y), (x, 1 - y), (1 - x, 1 - y)]


def _hbm(n):
    return [pl.BlockSpec(memory_space=pl.ANY)] * n


def _remote(src, dst, send_sem, recv_sem, device):
    return pltpu.make_async_remote_copy(src_ref=src, dst_ref=dst, send_sem=send_sem, recv_sem=recv_sem,
                                        device_id=device, device_id_type=MESH)


def _scalar_spec(grid, in_specs, out_specs):
    return pltpu.PrefetchScalarGridSpec(num_scalar_prefetch=1, grid=grid, in_specs=in_specs, out_specs=out_specs)


def _place_shard(param, layer, pos, name):
    _, r, cols = param.shape
    tr = _slab_tile(r)

    def kern(pos_ref, s_ref, o_ref):
        o_ref[...] = s_ref[...].astype(BF16)

    return pl.pallas_call(
        kern, name=name, out_shape=jax.ShapeDtypeStruct((N_CHIPS, r, cols), BF16),
        grid_spec=_scalar_spec((r // tr,), [pl.BlockSpec((None, tr, cols), lambda i, p: (layer, i, 0))],
                               pl.BlockSpec((None, tr, cols), lambda i, p: (p[1], i, 0))),
        compiler_params=_cparams("parallel"),
    )(pos, param)


class _CommJob:
    def __init__(self, inputs, out_shapes, aliases, sem_shapes, stages):
        self.inputs, self.out_shapes, self.aliases, self.sem_shapes, self.stages = inputs, out_shapes, aliases, sem_shapes, stages


def _merge_jobs(a, b):
    assert len(a.stages) == len(b.stages)
    ni, no, ns = len(a.inputs), len(a.out_shapes), len(a.sem_shapes)

    def both(sa, sb):
        def stage(ins, outs, sems):
            sa(ins[:ni], outs[:no], sems[:ns])
            sb(ins[ni:], outs[no:], sems[ns:])
        return stage

    aliases = dict(a.aliases)
    aliases.update({ni + i: no + o for i, o in b.aliases.items()})
    return _CommJob(a.inputs + b.inputs, a.out_shapes + b.out_shapes, aliases, a.sem_shapes + b.sem_shapes,
                    [both(sa, sb) for sa, sb in zip(a.stages, b.stages)])


def _run_job(job, name):
    n_in, n_out = len(job.inputs), len(job.out_shapes)

    def body(*refs):
        for stage in job.stages:
            stage(refs[:n_in], refs[n_in:n_in + n_out], refs[n_in + n_out:])

    return pl.pallas_call(
        body, name=name, in_specs=_hbm(n_in), out_specs=_hbm(n_out), out_shape=job.out_shapes,
        input_output_aliases=job.aliases, scratch_shapes=job.sem_shapes,
    )(*job.inputs)


def _job_marks(job, steps):
    return {2: [0, steps - 1], 3: [0, (5 * steps) // 8, steps - 1]}[len(job.stages)]


def _gather_job(placed):
    n = len(placed)

    def half(w, which):
        r2 = placed[w].shape[1] // 2
        return pl.ds(which * r2, r2)

    def ici_copies(outs, sems, slot_of):
        x, y, c = _mesh_pos()
        res = []
        for w in range(n):
            for k, (px, py) in enumerate(_other_chips(x, y)):
                slab = outs[w].at[slot_of(x, y, px, py), half(w, c)]
                res.append((slab, _remote(slab, slab, sems[0].at[w, k], sems[1].at[w, k], (px, py, c))))
        return res

    def forwards(outs, sems, which_core):
        x, y, c = _mesh_pos()
        res = []
        for w in range(n):
            for k, (px, py) in enumerate(_other_chips(x, y)):
                slab = outs[w].at[2 * px + py, half(w, which_core(c))]
                res.append(_remote(slab, slab, sems[2].at[w, k], sems[3].at[w, k], (x, y, 1 - c)))
        return res

    def send(ins, outs, sems):
        for _, cp in ici_copies(outs, sems, lambda x, y, px, py: 2 * x + y):
            cp.start()

    def forward(ins, outs, sems):
        arrivals = ici_copies(outs, sems, lambda x, y, px, py: 2 * px + py)
        for (_, arrival), fwd in zip(arrivals, forwards(outs, sems, lambda c: c)):
            arrival.wait_recv()
            fwd.start()

    def finish(ins, outs, sems):
        for cp in forwards(outs, sems, lambda c: 1 - c):
            cp.wait_recv()
        for _, cp in ici_copies(outs, sems, lambda x, y, px, py: 2 * x + y):
            cp.wait_send()
        for cp in forwards(outs, sems, lambda c: c):
            cp.wait_send()

    return _CommJob(list(placed), [jax.ShapeDtypeStruct(p.shape, p.dtype) for p in placed], {w: w for w in range(n)},
                    [pltpu.SemaphoreType.DMA((n, 3))] * 4, [send, forward, finish])


def _pair_swap_job(grads):
    n = len(grads)

    def copies(ins, outs, sems):
        x, y, c = _mesh_pos()
        res = []
        for w in range(n):
            r2 = grads[w].shape[1] // 2
            res.append(_remote(ins[w].at[:, pl.ds((1 - c) * r2, r2)], outs[w], sems[0].at[w], sems[1].at[w], (x, y, 1 - c)))
        return res

    def send(ins, outs, sems):
        for cp in copies(ins, outs, sems):
            cp.start()

    def finish(ins, outs, sems):
        for cp in copies(ins, outs, sems):
            cp.wait()

    return _CommJob(list(grads), [jax.ShapeDtypeStruct((N_CHIPS, g.shape[1] // 2, g.shape[2]), F32) for g in grads], {},
                    [pltpu.SemaphoreType.DMA((n,))] * 2, [send, finish])


def _chip_exchange_job(hs):
    n = len(hs)

    def send(ins, outs, sems):
        x, y, c = _mesh_pos()
        for w in range(n):
            for k, (px, py) in enumerate(_other_chips(x, y)):
                _remote(ins[w].at[2 * px + py], outs[w].at[2 * x + y], sems[0].at[w, k], sems[1].at[w, k], (px, py, c)).start()

    def finish(ins, outs, sems):
        x, y, c = _mesh_pos()
        for w in range(n):
            for k, (px, py) in enumerate(_other_chips(x, y)):
                got = outs[w].at[2 * px + py]
                cp = _remote(ins[w].at[2 * px + py], got, sems[0].at[w, k], sems[1].at[w, k], (px, py, c))
                cp.wait_recv()
                cp.wait_send()

    return _CommJob(list(hs), [jax.ShapeDtypeStruct(h.shape, h.dtype) for h in hs], {},
                    [pltpu.SemaphoreType.DMA((n, 3))] * 2, [send, finish])


def _pair_share(ts, name):
    n = len(ts)

    def body(*refs):
        outs = refs[n:2 * n]
        send_sems, recv_sems = refs[2 * n:]
        x, y, c = _mesh_pos()
        sends = []
        for w in range(n):
            r2 = ts[w].shape[0] // 2
            mine = outs[w].at[pl.ds(c * r2, r2)]
            rc = _remote(mine, mine, send_sems.at[w], recv_sems.at[w], (x, y, 1 - c))
            rc.start()
            sends.append(rc)
        for w in range(n):
            r2 = ts[w].shape[0] // 2
            theirs = outs[w].at[pl.ds((1 - c) * r2, r2)]
            _remote(theirs, theirs, send_sems.at[w], recv_sems.at[w], (x, y, 1 - c)).wait_recv()
            sends[w].wait_send()

    return pl.pallas_call(
        body, name=name, in_specs=_hbm(n), out_specs=_hbm(n),
        out_shape=[jax.ShapeDtypeStruct(t.shape, F32) for t in ts],
        input_output_aliases={w: w for w in range(n)},
        scratch_shapes=[pltpu.SemaphoreType.DMA((n,))] * 2,
    )(*ts)


def _slab_tile(rows):
    return next(t for t in (512, 256, 176, 128, 64, 32, 16) if rows % t == 0)


def _sum_pair(grad, land, pos, name):
    _, r2, cols = land.shape
    tr = _slab_tile(r2)
    nt = r2 // tr

    def kern(pos_ref, a_ref, b_ref, o_ref):
        o_ref[...] = (a_ref[...] + b_ref[...]).astype(BF16)

    spec = pl.BlockSpec((None, tr, cols), lambda j, i, p: (j, i, 0))
    return pl.pallas_call(
        kern, name=name, out_shape=jax.ShapeDtypeStruct(land.shape, BF16),
        grid_spec=_scalar_spec((N_CHIPS, nt), [pl.BlockSpec((None, tr, cols), lambda j, i, p: (j, p[0] * nt + i, 0)), spec], spec),
        compiler_params=_cparams("parallel", "parallel"),
    )(pos, grad, land)


def _sum_chips(hs, land, pos, name):
    _, r2, cols = land.shape
    tr = _slab_tile(r2)
    nt = r2 // tr

    def kern(pos_ref, h_ref, l_ref, o_ref):
        acc = jnp.zeros((tr, cols), F32)
        own = h_ref[...].astype(F32)
        for k in range(N_CHIPS):
            acc = acc + jnp.where(pos_ref[1] == k, own, l_ref[k].astype(F32))
        o_ref[...] = acc

    return pl.pallas_call(
        kern, name=name, out_shape=jax.ShapeDtypeStruct((2 * r2, cols), F32),
        grid_spec=_scalar_spec((nt,), [pl.BlockSpec((None, tr, cols), lambda i, p: (p[1], i, 0)),
                                       pl.BlockSpec((N_CHIPS, tr, cols), lambda i, p: (0, i, 0))],
                               pl.BlockSpec((tr, cols), lambda i, p: (p[0] * nt + i, 0))),
        compiler_params=_cparams("parallel"),
    )(pos, hs, land)


class _ReduceScatter:
    def __init__(self, grads, pos, tag):
        self.grads, self.pos, self.tag = list(grads), pos, tag

    def swap_job(self):
        return _pair_swap_job(self.grads)

    def after_swap(self, land):
        self.hs = [_sum_pair(g, l, self.pos, f"grads_pair_sum_{self.tag}{w}") for w, (g, l) in enumerate(zip(self.grads, land))]

    def exchange_job(self):
        return _chip_exchange_job(self.hs)

    def after_exchange(self, land2):
        ts = [_sum_chips(h, l, self.pos, f"grads_chip_sum_{self.tag}{w}") for w, (h, l) in enumerate(zip(self.hs, land2))]
        return _pair_share(ts, f"grads_pair_share_{self.tag}")

    def run(self):
        self.after_swap(_run_job(self.swap_job(), f"grads_pair_swap_{self.tag}"))
        return self.after_exchange(_run_job(self.exchange_job(), f"grads_chip_exchange_{self.tag}"))


EARLY_WEIGHTS = ("ada0", "attn_qkv")
LATE_WEIGHTS = ("ada1", "ffn_in0", "ffn_in1", "ffn_out0", "ffn_out1", "attn_o", "ret_qkvg", "ret_o")
LAYER1_GRADS = ("ffn_out1", "ffn_in1", "ret_o", "ret_qkvg", "ada1")
LAYER0_FFN_GRADS = ("ffn_out0", "ffn_in0")
LAST_GRADS = ("attn_o", "attn_qkv", "ada0")


def _fill_weights(wts, full):
    for name, w in full.items():
        if name[:-1] in ("ada", "ffn_in"):
            wts[name[:-1]][int(name[-1])] = w
        elif name[:-1] == "ffn_out":
            wts["ffn_out"][int(name[-1])] = w.reshape(-1, w.shape[2])
        elif name in ("attn_o", "ret_o"):
            wts[name] = w.reshape(-1, w.shape[2])
        elif name == "attn_qkv":
            wts[name] = w.transpose(1, 0, 2).reshape(w.shape[1], -1)
        else:
            wts[name] = w


class _StepPlan:
    def __init__(self, placed, pos):
        self.placed, self.pos = placed, pos
        self.layer1 = self.layer0_ffn = None
        self.reduced = {}

    def gather_job(self):
        return _gather_job([self.placed[k] for k in LATE_WEIGHTS])

    def late_weights(self, outs, wts):
        _fill_weights(wts, dict(zip(LATE_WEIGHTS, outs)))

    def start_layer1(self, big):
        self.layer1 = _ReduceScatter([big[k] for k in LAYER1_GRADS], self.pos, "l1_")

    def start_layer0_ffn(self, big):
        self.layer0_ffn = _ReduceScatter([big[k] for k in LAYER0_FFN_GRADS], self.pos, "l0f_")

    def exchange_job(self):
        return _merge_jobs(self.layer1.exchange_job(), self.layer0_ffn.exchange_job())

    def after_exchange(self, land):
        n1 = len(LAYER1_GRADS)
        self.reduced.update(zip(LAYER1_GRADS, self.layer1.after_exchange(land[:n1])))
        self.reduced.update(zip(LAYER0_FFN_GRADS, self.layer0_ffn.after_exchange(land[n1:])))


def _all_reduce_small(v, name):
    def body(v_ref, o_ref, land_ref, send_sems, recv_sems):
        x, y, c = _mesh_pos()
        me = 4 * x + 2 * y + c
        land_ref[me] = v_ref[...]
        for t in range(N_DEVICES):
            @pl.when(t != me)
            def _(t=t):
                _remote(v_ref, land_ref.at[me], send_sems.at[t], recv_sems.at[me], (t // 4, (t // 2) % 2, t % 2)).start()
        for t in range(N_DEVICES):
            @pl.when(t != me)
            def _(t=t):
                _remote(v_ref, land_ref.at[t], send_sems.at[t], recv_sems.at[t], (t // 4, (t // 2) % 2, t % 2)).wait()
        acc = land_ref[0]
        for t in range(1, N_DEVICES):
            acc = acc + land_ref[t]
        o_ref[...] = acc

    vmem = pl.BlockSpec(memory_space=pltpu.VMEM)
    return pl.pallas_call(
        body, name=name, in_specs=[vmem], out_specs=vmem, out_shape=jax.ShapeDtypeStruct(v.shape, F32),
        scratch_shapes=[pltpu.VMEM((N_DEVICES,) + v.shape, F32), pltpu.SemaphoreType.DMA((N_DEVICES,)),
                        pltpu.SemaphoreType.DMA((N_DEVICES,))],
    )(v)


SMALL_ROWS = 24


def _pack_small(small, dlogit):
    d = D_MODEL
    misc = jnp.zeros((d,), F32)
    misc = misc.at[0:HEAD_DIM].set(small["q_norm"]).at[128:128 + HEAD_DIM].set(small["k_norm"])
    misc = misc.at[256:256 + N_HEADS].set(small["sink"]).at[384:384 + 2 * RET_HEADS].set(dlogit.reshape(-1))
    rows = [small["ada_b0"].reshape(6, d), small["ada_b1"].reshape(6, d), small["norm1_g0"][None], small["norm1_g1"][None],
            small["norm2_g0"][None], small["norm2_g1"][None], small["c_ctx"][None], small["gn_g"].reshape(2, d), misc[None]]
    buf = jnp.concatenate(rows, axis=0)
    return jnp.concatenate([buf, jnp.zeros((SMALL_ROWS - buf.shape[0], d), F32)], axis=0)


def _unpack_small(buf):
    d = D_MODEL
    misc = buf[19]
    return dict(ada_b=buf[0:12].reshape(2, 6 * d), norm1_g=buf[12:14], norm2_g=buf[14:16], c_ctx=buf[16],
                gn_g=buf[17:19].reshape(2 * d), q_norm=misc[0:HEAD_DIM], k_norm=misc[128:128 + HEAD_DIM],
                sink=misc[256:256 + N_HEADS], decay=misc[384:384 + 2 * RET_HEADS])


def kernel(x, c, ctx, c_ctx, ada_w, ada_b, norm1_g, norm2_g, ffn_w_in, ffn_w_out, attn_w_qkv, attn_q_norm, attn_k_norm, attn_sink, attn_w_o, ret_w_qkvg, ret_decay_logit, ret_gn_g, ret_w_o, loss_target, m_c_ctx, m_ada_w, m_ada_b, m_norm1_g, m_norm2_g, m_ffn_w_in, m_ffn_w_out, m_attn_w_qkv, m_attn_q_norm, m_attn_k_norm, m_attn_sink, m_attn_w_o, m_ret_w_qkvg, m_ret_decay_logit, m_ret_gn_g, m_ret_w_o, v_c_ctx, v_ada_w, v_ada_b, v_norm1_g, v_norm2_g, v_ffn_w_in, v_ffn_w_out, v_attn_w_qkv, v_attn_q_norm, v_attn_k_norm, v_attn_sink, v_attn_w_o, v_ret_w_qkvg, v_ret_decay_logit, v_ret_gn_g, v_ret_w_o):
    xi, yi, ci = _mesh_pos()
    chip = 2 * xi + yi
    nb, s, d = x.shape
    gn_shard = ret_gn_g.shape[1]

    shards = dict(ada0=(ada_w, 0), ada1=(ada_w, 1), ffn_in0=(ffn_w_in, 0), ffn_in1=(ffn_w_in, 1), ffn_out0=(ffn_w_out, 0),
                  ffn_out1=(ffn_w_out, 1), attn_qkv=(attn_w_qkv, 0), attn_o=(attn_w_o, 0), ret_qkvg=(ret_w_qkvg, 0), ret_o=(ret_w_o, 0))
    names = list(shards)
    pos = jnp.stack([ci, chip]).astype(jnp.int32)
    placed = {k: _place_shard(*shards[k], pos, f"place_{k}") for k in names}
    early = _run_job(_gather_job([placed[k] for k in EARLY_WEIGHTS]), "gather_early_weights")
    gn_mine = jnp.where(ci == 0, ret_gn_g[0], jnp.zeros_like(ret_gn_g[0]))
    gn_place = lax.dynamic_update_slice(jnp.zeros((RET_VWIDTH,), F32), gn_mine, (chip * gn_shard,))
    gn_full = _all_reduce_small(gn_place.reshape(2, d), "gather_gn_gain").reshape(RET_VWIDTH)

    wts = dict(ada=[None, None], ffn_in=[None, None], ffn_out=[None, None], attn_qkv=None, attn_o=None, ret_qkvg=None, ret_o=None)
    _fill_weights(wts, dict(zip(EARLY_WEIGHTS, early)))
    plan = _StepPlan(placed, pos)
    decay_logit = ret_decay_logit[0]
    sp = dict(c_ctx=c_ctx, ada_b=ada_b, norm1_g=norm1_g, norm2_g=norm2_g, q_norm=attn_q_norm[0], k_norm=attn_k_norm[0],
              sink=attn_sink[0], log_g=jax.nn.log_sigmoid(decay_logit), gn_g=gn_full)
    loss_part, dz, big, small = _local_step(x, c, ctx, loss_target, sp, wts, plan)

    loss = lax.psum(loss_part[0, 0], ("x", "y", "c"))
    grad_x = dz.reshape(nb, s, d)

    dlogit = small["log_g"] * jax.nn.sigmoid(-decay_logit)
    sg = _unpack_small(_all_reduce_small(_pack_small(small, dlogit), "reduce_small_grads"))
    reduced = dict(plan.reduced)
    reduced.update(zip(LAST_GRADS, _ReduceScatter([big[k] for k in LAST_GRADS], pos, "last_").run()))

    grads = dict(
        c_ctx=sg["c_ctx"], ada_w=jnp.stack([reduced["ada0"], reduced["ada1"]]), ada_b=sg["ada_b"], norm1_g=sg["norm1_g"],
        norm2_g=sg["norm2_g"], ffn_w_in=jnp.stack([reduced["ffn_in0"], reduced["ffn_in1"]]),
        ffn_w_out=jnp.stack([reduced["ffn_out0"], reduced["ffn_out1"]]), attn_w_qkv=reduced["attn_qkv"][None],
        attn_q_norm=sg["q_norm"][None], attn_k_norm=sg["k_norm"][None], attn_sink=sg["sink"][None],
        attn_w_o=reduced["attn_o"][None], ret_w_qkvg=reduced["ret_qkvg"][None], ret_decay_logit=sg["decay"].reshape(1, 2, RET_HEADS),
        ret_gn_g=lax.dynamic_slice(sg["gn_g"], (chip * gn_shard,), (gn_shard,))[None], ret_w_o=reduced["ret_o"][None])
    params = dict(c_ctx=(c_ctx, m_c_ctx, v_c_ctx), ada_w=(ada_w, m_ada_w, v_ada_w), ada_b=(ada_b, m_ada_b, v_ada_b),
                  norm1_g=(norm1_g, m_norm1_g, v_norm1_g), norm2_g=(norm2_g, m_norm2_g, v_norm2_g),
                  ffn_w_in=(ffn_w_in, m_ffn_w_in, v_ffn_w_in), ffn_w_out=(ffn_w_out, m_ffn_w_out, v_ffn_w_out),
                  attn_w_qkv=(attn_w_qkv, m_attn_w_qkv, v_attn_w_qkv), attn_q_norm=(attn_q_norm, m_attn_q_norm, v_attn_q_norm),
                  attn_k_norm=(attn_k_norm, m_attn_k_norm, v_attn_k_norm), attn_sink=(attn_sink, m_attn_sink, v_attn_sink),
                  attn_w_o=(attn_w_o, m_attn_w_o, v_attn_w_o), ret_w_qkvg=(ret_w_qkvg, m_ret_w_qkvg, v_ret_w_qkvg),
                  ret_decay_logit=(ret_decay_logit, m_ret_decay_logit, v_ret_decay_logit),
                  ret_gn_g=(ret_gn_g, m_ret_gn_g, v_ret_gn_g), ret_w_o=(ret_w_o, m_ret_w_o, v_ret_w_o))
    order = list(params)
    deltas, new_m, new_v = [], [], []
    for k in order:
        w, m, v = params[k]
        g = grads[k].reshape(w.shape)
        grads[k] = g
        flat = (-1, w.shape[-1]) if w.ndim > 1 else (1, -1)
        if k == "ret_decay_logit":
            flat = (1, -1)
        dw, nm, nv = _adamw(w.reshape(flat), g.reshape(flat), m.reshape(flat), v.reshape(flat), f"adamw_{k}")
        deltas.append(dw.reshape(w.shape))
        new_m.append(nm.reshape(w.shape))
        new_v.append(nv.reshape(w.shape))
    return (loss, grad_x, *[grads[k] for k in order], *deltas, *new_m, *new_v)
```

```python
import functools

import jax
import jax.numpy as jnp
from jax import lax
from jax.experimental import pallas as pl
from jax.experimental.pallas import tpu as pltpu

F32 = jnp.float32
BF16 = jnp.bfloat16

D_MODEL = 1024
N_HEADS = 16
N_KV_HEADS = 4
HEAD_DIM = 64
WINDOW = 128
ATTN_BLOCK = 128
BAND = ATTN_BLOCK + 2 * WINDOW
RET_HEADS = 4
RET_QK_DIM = 256
RET_V_DIM = 512
RET_VWIDTH = 2048
RET_CHUNK = 128
D_FF = 2816
GRID_W = 64
ROPE_BASE = 10000.0
EPS = 1e-6
NEG_INF = -1e30
LANES = 128

ADAM_LR = 0.001
ADAM_B1 = 0.9
ADAM_B2 = 0.999
ADAM_EPS = 1e-08
ADAM_WD = 0.01
ADAM_STEP = 10

VMEM_LIMIT_BYTES = 56 * 1024 * 1024
MESH = pl.DeviceIdType.MESH
N_CHIPS = 4


def _cparams(*sem):
    return pltpu.CompilerParams(dimension_semantics=sem, vmem_limit_bytes=VMEM_LIMIT_BYTES)


_DIMS = {"nn": ((1,), (0,)), "nt": ((1,), (1,)), "tn": ((0,), (0,))}


def _dot(a, b, form):
    return lax.dot_general(a.astype(BF16), b.astype(BF16), (_DIMS[form], ((), ())), preferred_element_type=F32)


@functools.partial(jax.custom_vjp, nondiff_argnums=(2,))
def _mm(a, b, form):
    return _dot(a, b, form)


def _mm_fwd(a, b, form):
    return _dot(a, b, form), (a, b)


def _mm_bwd(form, res, ct):
    a, b = res
    if form == "nn":
        da, db = _dot(ct, b, "nt"), _dot(a, ct, "tn")
    elif form == "nt":
        da, db = _dot(ct, b, "nn"), _dot(ct, a, "tn")
    else:
        da, db = _dot(b, ct, "nt"), _dot(a, ct, "nn")
    return da.astype(a.dtype), db.astype(b.dtype)


_mm.defvjp(_mm_fwd, _mm_bwd)


def _swap_halves(x, half):
    w = x.shape[-1]
    lane = lax.broadcasted_iota(jnp.int32, x.shape, x.ndim - 1)
    return jnp.where(lane % (2 * half) < half, pltpu.roll(x, w - half, x.ndim - 1), pltpu.roll(x, half, x.ndim - 1))


@functools.partial(jax.custom_vjp, nondiff_argnums=(1,))
def _rot(x, half):
    return _swap_halves(x, half)


def _rot_fwd(x, half):
    return _swap_halves(x, half), None


def _rot_bwd(half, _, ct):
    return (_swap_halves(ct, half),)


_rot.defvjp(_rot_fwd, _rot_bwd)


def _rope(x, cos, sin_signed, half):
    return x * cos + _rot(x, half) * sin_signed


def _head_mean_square(x):
    r = lax.broadcasted_iota(jnp.int32, (LANES, LANES), 0) // HEAD_DIM
    c = lax.broadcasted_iota(jnp.int32, (LANES, LANES), 1) // HEAD_DIM
    g = jnp.where(r == c, 1.0 / HEAD_DIM, 0.0).astype(F32)
    return jnp.dot(x * x, g, precision=lax.Precision.HIGHEST, preferred_element_type=F32)


def _qk_chunk(x, gain, cos, sin_signed, scale):
    y = x * lax.rsqrt(_head_mean_square(x) + EPS) * gain
    return _rope(y, cos, sin_signed, HEAD_DIM // 4) * scale


def _sigmoid(x):
    return 1.0 / (1.0 + jnp.exp(-x))


def _silu(x):
    return x * _sigmoid(x)


def _mm_nn(a, w, out_dtype, name, tm, tn, tk, bias=None):
    m, k_dim = a.shape
    if w.ndim == 3:
        n = w.shape[0] * w.shape[2]
        per = w.shape[2] // tn
        assert w.shape[2] % tn == 0
        w_spec = pl.BlockSpec((None, tk, tn), lambda i, j, k: (j // per, k, j % per))
    else:
        n = w.shape[1]
        w_spec = pl.BlockSpec((tk, tn), lambda i, j, k: (k, j))
    assert m % tm == 0 and n % tn == 0 and k_dim % tk == 0, (name, a.shape, w.shape, tm, tn, tk)
    nk = k_dim // tk
    has_bias = bias is not None

    def body(*refs):
        a_ref, w_ref = refs[0], refs[1]
        b_ref = refs[2] if has_bias else None
        o_ref, acc_ref = (refs[-1], None) if nk == 1 else (refs[-2], refs[-1])
        if nk == 1:
            part = jnp.dot(a_ref[...].astype(BF16), w_ref[...], preferred_element_type=F32)
            o_ref[...] = (part + b_ref[...] if has_bias else part).astype(out_dtype)
            return
        k = pl.program_id(2)

        @pl.when(k == 0)
        def _():
            acc_ref[...] = jnp.zeros_like(acc_ref)

        acc_ref[...] += jnp.dot(a_ref[...].astype(BF16), w_ref[...], preferred_element_type=F32)

        @pl.when(k == nk - 1)
        def _():
            r = acc_ref[...]
            if has_bias:
                r = r + b_ref[...]
            o_ref[...] = r.astype(out_dtype)

    in_specs = [pl.BlockSpec((tm, tk), lambda i, j, k: (i, k)), w_spec]
    args = [a, w]
    if has_bias:
        in_specs.append(pl.BlockSpec((1, tn), lambda i, j, k: (0, j)))
        args.append(bias)
    return pl.pallas_call(
        body, name=name, grid=(m // tm, n // tn, nk), in_specs=in_specs,
        out_specs=pl.BlockSpec((tm, tn), lambda i, j, k: (i, j)),
        out_shape=jax.ShapeDtypeStruct((m, n), out_dtype),
        scratch_shapes=[pltpu.VMEM((tm, tn), F32)] if nk > 1 else [],
        compiler_params=_cparams("parallel", "parallel", "arbitrary"),
    )(*args)


def _mm_nt(a, w, out_dtype, name, tm, tn, tk):
    if a.ndim == 3:
        planes, m, plane_w = a.shape
        c_dim = planes * plane_w
        a_per = plane_w // tk
        assert plane_w % tk == 0
        a_spec = pl.BlockSpec((None, tm, tk), lambda i, j, k: (k // a_per, i, k % a_per))
    else:
        m, c_dim = a.shape
        a_spec = pl.BlockSpec((tm, tk), lambda i, j, k: (i, k))
    if w.ndim == 3:
        k_out = w.shape[1]
        per = w.shape[2] // tk
        assert w.shape[2] % tk == 0 and w.shape[0] * w.shape[2] == c_dim
        w_spec = pl.BlockSpec((None, tn, tk), lambda i, j, k: (k // per, j, k % per))
    else:
        k_out = w.shape[0]
        assert w.shape[1] == c_dim
        w_spec = pl.BlockSpec((tn, tk), lambda i, j, k: (j, k))
    assert m % tm == 0 and k_out % tn == 0 and c_dim % tk == 0, (name, a.shape, w.shape, tm, tn, tk)
    nk = c_dim // tk

    def body(a_ref, w_ref, o_ref, acc_ref=None):
        if nk == 1:
            o_ref[...] = _dot(a_ref[...], w_ref[...], "nt").astype(out_dtype)
            return
        k = pl.program_id(2)

        @pl.when(k == 0)
        def _():
            acc_ref[...] = jnp.zeros_like(acc_ref)

        acc_ref[...] += _dot(a_ref[...], w_ref[...], "nt")

        @pl.when(k == nk - 1)
        def _():
            o_ref[...] = acc_ref[...].astype(out_dtype)

    return pl.pallas_call(
        body, name=name, grid=(m // tm, k_out // tn, nk),
        in_specs=[a_spec, w_spec],
        out_specs=pl.BlockSpec((tm, tn), lambda i, j, k: (i, j)),
        out_shape=jax.ShapeDtypeStruct((m, k_out), out_dtype),
        scratch_shapes=[pltpu.VMEM((tm, tn), F32)] if nk > 1 else [],
        compiler_params=_cparams("parallel", "parallel", "arbitrary"),
    )(a, w)


def _mm_tn(a, b, name, tm, tn, tk, shards=None):
    r, k_dim = a.shape
    if b.ndim == 3:
        n = b.shape[0] * b.shape[2]
        b_per = b.shape[2] // tn
        assert b.shape[2] % tn == 0
        b_spec = pl.BlockSpec((None, tk, tn), lambda i, j, k: (j // b_per, k, j % b_per))
    else:
        n = b.shape[1]
        b_spec = pl.BlockSpec((tk, tn), lambda i, j, k: (k, j))
    assert r % tk == 0 and k_dim % tm == 0 and n % tn == 0, (name, a.shape, b.shape, tm, tn, tk)
    nk = r // tk
    if shards:
        per = n // shards // tn
        assert n % (shards * tn) == 0
        out_shape = jax.ShapeDtypeStruct((shards, k_dim, n // shards), F32)
        out_spec = pl.BlockSpec((None, tm, tn), lambda i, j, k: (j // per, i, j % per))
    else:
        out_shape = jax.ShapeDtypeStruct((k_dim, n), F32)
        out_spec = pl.BlockSpec((tm, tn), lambda i, j, k: (i, j))

    def body(a_ref, b_ref, o_ref):
        k = pl.program_id(2)

        @pl.when(k == 0)
        def _():
            o_ref[...] = jnp.zeros_like(o_ref)

        o_ref[...] += _dot(a_ref[...], b_ref[...], "tn")

    return pl.pallas_call(
        body, name=name, grid=(k_dim // tm, n // tn, nk),
        in_specs=[pl.BlockSpec((tk, tm), lambda i, j, k: (k, i)), b_spec],
        out_specs=out_spec, out_shape=out_shape,
        compiler_params=_cparams("parallel", "parallel", "arbitrary"),
    )(a, b)


class _Carrier:
    def __init__(self, job, n_in, n_out, n_scratch):
        self.job, self.n_in, self.n_out, self.n_scratch = job, n_in, n_out, n_scratch
        self.ji = len(job.inputs) if job else 0
        self.jo = len(job.out_shapes) if job else 0

    def operands(self):
        return list(self.job.inputs) if self.job else []

    def in_specs(self):
        return [pl.BlockSpec(memory_space=pl.ANY)] * self.ji

    def out_specs(self):
        return [pl.BlockSpec(memory_space=pl.ANY)] * self.jo

    def out_shapes(self):
        return list(self.job.out_shapes) if self.job else []

    def scratch(self):
        return list(self.job.sem_shapes) if self.job else []

    def aliases(self):
        return {self.n_in + a: self.n_out + b for a, b in self.job.aliases.items()} if self.job else {}

    def split(self, refs):
        a = self.n_in
        b = a + self.ji
        c = b + self.n_out
        d = c + self.jo
        e = d + self.n_scratch
        return list(refs[:a]) + list(refs[b:c]) + list(refs[d:e]), (refs[a:b], refs[c:d], refs[e:])

    def run(self, job_refs, step, steps):
        if not self.job:
            return
        for stage, mark in zip(self.job.stages, _job_marks(self.job, steps)):
            pl.when(step == mark)(functools.partial(stage, *job_refs))

    def results(self, res):
        res = list(res)
        return res[:self.n_out], res[self.n_out:]


FFN_ROW_TILE = 768


def _ffn_tile(r):
    return FFN_ROW_TILE if r % FFN_ROW_TILE == 0 else _row_tile(r)


def _ffn_in_swiglu(h, w, name):
    r, k_dim = h.shape
    n4 = w.shape[2]
    tm = _ffn_tile(r)

    def body(h_ref, wg_ref, wu_ref, u_ref, a_ref):
        hv = h_ref[...]
        g = jnp.dot(hv, wg_ref[...], preferred_element_type=F32)
        up = jnp.dot(hv, wu_ref[...], preferred_element_type=F32)
        u_ref[0] = g.astype(BF16)
        u_ref[1] = up.astype(BF16)
        a_ref[...] = (_silu(g) * up).astype(BF16)

    return pl.pallas_call(
        body, name=name, grid=(r // tm, 2),
        in_specs=[pl.BlockSpec((tm, k_dim), lambda i, j: (i, 0)),
                  pl.BlockSpec((None, k_dim, n4), lambda i, j: (j, 0, 0)),
                  pl.BlockSpec((None, k_dim, n4), lambda i, j: (j + 2, 0, 0))],
        out_specs=[pl.BlockSpec((2, tm, n4), lambda i, j: (0, i, j)), pl.BlockSpec((tm, n4), lambda i, j: (i, j))],
        out_shape=[jax.ShapeDtypeStruct((2, r, 2 * n4), BF16), jax.ShapeDtypeStruct((r, 2 * n4), BF16)],
        compiler_params=_cparams("parallel", "parallel"),
    )(h, w, w)


def _mm_nn_gate_residual(geo, a, w, z, mod, off, name, norm=None):
    r, k_dim = a.shape
    n = w.shape[1]
    tm = FFN_ROW_TILE if geo.seg % FFN_ROW_TILE == 0 else 256
    tiles = geo.seg // tm
    assert geo.seg % tm == 0 and r == geo.r and n == D_MODEL

    def body(a_ref, w_ref, z_ref, mx_ref, mc_ref, *rest):
        out = jnp.dot(a_ref[...], w_ref[...], preferred_element_type=F32)
        is_x = (pl.program_id(0) % tiles) * tm + lax.broadcasted_iota(jnp.int32, (tm, 1), 0) < geo.s
        zo = z_ref[...] + jnp.where(is_x, mx_ref[:, off:off + n], mc_ref[:, off:off + n]) * out
        if norm:
            g_ref, nx_ref, nc_ref, zo_ref, raw_ref, h_ref = rest
            no = norm[2]
            shift = jnp.where(is_x, nx_ref[:, no:no + n], nc_ref[:, no:no + n])
            scale = jnp.where(is_x, nx_ref[:, no + n:no + 2 * n], nc_ref[:, no + n:no + 2 * n])
            rs = lax.rsqrt(jnp.mean(zo * zo, axis=-1, keepdims=True) + EPS)
            h_ref[...] = ((zo * rs) * g_ref[...] * (1.0 + scale) + shift).astype(BF16)
        else:
            zo_ref, raw_ref = rest
        zo_ref[...] = zo
        raw_ref[...] = out.astype(BF16)

    def mod_specs(m):
        return [pl.BlockSpec((None, 1, m.shape[2]), lambda i: (i // tiles, 0, 0)), pl.BlockSpec((None, 1, m.shape[2]), lambda i: (geo.b, 0, 0))]

    row = pl.BlockSpec((tm, n), lambda i: (i, 0))
    in_specs = [pl.BlockSpec((tm, k_dim), lambda i: (i, 0)), pl.BlockSpec((k_dim, n), lambda i: (0, 0)), row] + mod_specs(mod)
    args = [a, w, z, mod, mod]
    out_specs, out_shape = [row, row], [jax.ShapeDtypeStruct((r, n), F32), jax.ShapeDtypeStruct((r, n), BF16)]
    if norm:
        in_specs += [pl.BlockSpec((1, n), lambda i: (0, 0))] + mod_specs(norm[1])
        args += [norm[0], norm[1], norm[1]]
        out_specs.append(row)
        out_shape.append(jax.ShapeDtypeStruct((r, n), BF16))
    res = pl.pallas_call(body, name=name, grid=(r // tm,), in_specs=in_specs, out_specs=out_specs, out_shape=out_shape,
                         compiler_params=_cparams("parallel"))(*args)
    return res if norm else (*res, None)


def _ffn_out_dx_swiglu_bwd(df, w_out, u, name, job=None):
    r, d = df.shape
    n4 = u.shape[2] // 2
    tm = _ffn_tile(r)
    carrier = _Carrier(job, 3, 1, 0)
    steps = (r // tm) * 2

    def body(*refs):
        (df_ref, w_ref, u_ref, du_ref), job_refs = carrier.split(refs)
        carrier.run(job_refs, pl.program_id(0) * 2 + pl.program_id(1), steps)
        da = _dot(df_ref[...], w_ref[...], "nt")
        g, up = u_ref[0].astype(F32), u_ref[1].astype(F32)
        s = _sigmoid(g)
        du_ref[0] = (da * up * (s * (1.0 + g * (1.0 - s)))).astype(BF16)
        du_ref[1] = (da * (g * s)).astype(BF16)

    res = pl.pallas_call(
        body, name=name, grid=(r // tm, 2),
        in_specs=[pl.BlockSpec((tm, d), lambda i, j: (i, 0)), pl.BlockSpec((n4, d), lambda i, j: (j, 0)),
                  pl.BlockSpec((2, tm, n4), lambda i, j: (0, i, j))] + carrier.in_specs(),
        out_specs=[pl.BlockSpec((2, tm, n4), lambda i, j: (0, i, j))] + carrier.out_specs(),
        out_shape=[jax.ShapeDtypeStruct(u.shape, BF16)] + carrier.out_shapes(),
        scratch_shapes=carrier.scratch(), input_output_aliases=carrier.aliases(),
        compiler_params=_cparams("arbitrary", "arbitrary"),
    )(df, w_out, u, *carrier.operands())
    (du,), extra = carrier.results(res)
    return du, extra


class _Rows:
    def __init__(self, b, s, l):
        self.b, self.s, self.l = b, s, l
        self.seg = s + l
        self.r = b * self.seg


def _rowwise(name, body, geo, tm, ins, outs, job=None):
    seg_blocks, x_blocks = geo.seg // tm, geo.s // tm
    assert geo.seg % tm == 0 and geo.s % tm == 0
    nb = geo.b

    def is_ctx(i):
        return i % seg_blocks >= x_blocks

    in_specs, args = [], []
    for arr, kind in ins:
        args.append(arr)
        if kind == "row":
            in_specs.append(pl.BlockSpec((tm, arr.shape[1]), lambda i: (i, 0)))
        elif kind == "ex":
            in_specs.append(pl.BlockSpec((None, 1, arr.shape[2]), lambda i: (jnp.where(is_ctx(i), nb, i // seg_blocks), 0, 0)))
        elif kind == "full":
            in_specs.append(pl.BlockSpec(arr.shape, lambda i, nd=arr.ndim: (0,) * nd))
        elif kind == "tab":
            in_specs.append(pl.BlockSpec((tm, arr.shape[1]), lambda i: (i % seg_blocks, 0)))
        elif kind == "xrow":
            in_specs.append(pl.BlockSpec(
                (tm, arr.shape[1]), lambda i: ((i // seg_blocks) * x_blocks + jnp.minimum(i % seg_blocks, x_blocks - 1), 0)))
        else:
            _, width, cb = kind
            in_specs.append(pl.BlockSpec((tm, width), lambda i, cb=cb: (i, cb)))
    out_specs, out_shapes = [], []
    for o in outs:
        if o[0] == "row":
            out_specs.append(pl.BlockSpec((tm, o[1]), lambda i: (i, 0)))
            out_shapes.append(jax.ShapeDtypeStruct((geo.r, o[1]), o[2]))
        elif o[0] == "xrow":
            out_specs.append(pl.BlockSpec(
                (tm, o[1]), lambda i: ((i // seg_blocks) * x_blocks + jnp.minimum(i % seg_blocks, x_blocks - 1), 0)))
            out_shapes.append(jax.ShapeDtypeStruct((geo.b * geo.s, o[1]), o[2]))
        elif o[0] == "exacc":
            out_specs.append(pl.BlockSpec((None, 1, o[1]), lambda i: (jnp.where(is_ctx(i), nb, 0) + i // seg_blocks, 0, 0)))
            out_shapes.append(jax.ShapeDtypeStruct((2 * nb, 1, o[1]), F32))
        else:
            out_specs.append(pl.BlockSpec((o[1], o[2]), lambda i: (0, 0)))
            out_shapes.append(jax.ShapeDtypeStruct((o[1], o[2]), F32))
    n_in = len(ins)
    carrier = _Carrier(job, n_in, len(outs), 0)

    def kern(*refs):
        i = pl.program_id(0)
        refs, job_refs = carrier.split(refs)
        carrier.run(job_refs, i, geo.r // tm)
        res = body(i, *[r[...].astype(F32) for r in refs[:n_in]])
        if not isinstance(res, (tuple, list)):
            res = (res,)
        jj = i % seg_blocks
        first_of_part = (jj == 0) | (jj == x_blocks)
        for o, ref, val in zip(outs, refs[n_in:], res):
            if o[0] == "row":
                ref[...] = val.astype(ref.dtype)
            elif o[0] == "xrow":
                @pl.when(jj < x_blocks)
                def _(ref=ref, val=val):
                    ref[...] = val.astype(ref.dtype)
            else:
                first = first_of_part if o[0] == "exacc" else i == 0

                @pl.when(first)
                def _(ref=ref, val=val):
                    ref[...] = val

                @pl.when(jnp.logical_not(first))
                def _(ref=ref, val=val):
                    ref[...] += val

    res = pl.pallas_call(
        kern, name=name, grid=(geo.r // tm,), in_specs=in_specs + carrier.in_specs(), out_specs=out_specs + carrier.out_specs(),
        out_shape=out_shapes + carrier.out_shapes(), scratch_shapes=carrier.scratch(), input_output_aliases=carrier.aliases(),
        compiler_params=_cparams("arbitrary"),
    )(*args, *carrier.operands())
    own, extra = carrier.results(res)
    if job:
        return (*own, extra)
    return own[0] if len(own) == 1 else own


def _colsum(v):
    return jnp.sum(v, axis=0, keepdims=True)


def _norm_mod(geo, z, gain, mod, off, name):
    d = D_MODEL

    def body(i, zv, g, m):
        r = lax.rsqrt(jnp.mean(zv * zv, axis=-1, keepdims=True) + EPS)
        return (zv * r) * g * (1.0 + m[:, off + d:off + 2 * d]) + m[:, off:off + d]

    return _rowwise(name, body, geo, 256, [(z, "row"), (gain, "full"), (mod, "ex")], [("row", d, BF16)])


def _norm_mod_bwd(geo, z, gain, mod, off, dh, dz_skip, name, gated=None, latent_only=False, job=None):
    d = D_MODEL

    def body(i, zv, g, m, dhv, skip, *rest):
        r = lax.rsqrt(jnp.mean(zv * zv, axis=-1, keepdims=True) + EPS)
        n = zv * r
        dng = dhv * (1.0 + m[:, off + d:off + 2 * d])
        dn = dng * g
        dz = r * (dn - n * jnp.mean(dn * n, axis=-1, keepdims=True)) + skip
        res = (dz, _colsum(dhv), _colsum(dhv * (n * g)), _colsum(dng * n))
        if gated:
            ov, gm = rest
            res += (dz * gm[:, gated[2]:gated[2] + d], _colsum(dz * ov))
        return res

    ins = [(z, "row"), (gain, "full"), (mod, "ex"), (dh, "row"), (dz_skip, "row")]
    outs = [("xrow" if latent_only else "row", d, F32), ("exacc", d), ("exacc", d), ("gacc", 1, d)]
    if gated:
        ins += [(gated[0], "row"), (gated[1], "ex")]
        outs += [("row", d, BF16), ("exacc", d)]
    return _rowwise(name, body, geo, 256, ins, outs, job)


def _loss_head(geo, z, target, out, mod, off, name):
    seg_blocks, x_blocks = geo.seg // 256, geo.s // 256
    d = D_MODEL

    def body(i, zv, tv, ov, m):
        keep = jnp.where(i % seg_blocks >= x_blocks, 0.0, 1.0)
        err = (zv - tv) * keep
        part = 0.5 * jnp.sum(jnp.mean(err * err, axis=-1, keepdims=True), axis=0, keepdims=True)
        dz = err * (1.0 / d)
        return dz, jnp.broadcast_to(part, (1, LANES)), dz * m[:, off:off + d], _colsum(dz * ov)

    return _rowwise(name, body, geo, 256, [(z, "row"), (target, "xrow"), (out, "row"), (mod, "ex")],
                    [("row", d, F32), ("gacc", 1, LANES), ("row", d, BF16), ("exacc", d)])


Q_SCALE = HEAD_DIM ** -0.5
N_QK_CHUNKS = (N_HEADS + N_KV_HEADS) * HEAD_DIM // LANES
N_Q_CHUNKS = N_HEADS * HEAD_DIM // LANES


def _attn_prep(geo, proj, cos, sin_signed, q_gain, k_gain, name):
    def body(i, p, cs, sn, qg, kg):
        outs = []
        for ch in range(N_QK_CHUNKS):
            is_q = ch < N_Q_CHUNKS
            outs.append(_qk_chunk(p[:, ch * LANES:(ch + 1) * LANES], qg if is_q else kg, cs, sn, Q_SCALE if is_q else 1.0))
        outs.append(p[:, N_QK_CHUNKS * LANES:])
        return jnp.concatenate(outs, axis=1)

    return _rowwise(name, body, geo, 256, [(proj, "row"), (cos, "tab"), (sin_signed, "tab"), (q_gain, "full"), (k_gain, "full")],
                    [("row", proj.shape[1], BF16)])


def _attn_prep_bwd(geo, proj, cos, sin_signed, q_gain, k_gain, dq, dkv, name):
    kw = N_KV_HEADS * HEAD_DIM

    def body(i, p, cs, sn, qg, kg, dqv, dkvv):
        outs = []
        dgains = [jnp.zeros((1, LANES), F32), jnp.zeros((1, LANES), F32)]
        for ch in range(N_QK_CHUNKS):
            is_q = ch < N_Q_CHUNKS
            scale = Q_SCALE if is_q else 1.0
            ct = dqv[:, ch * LANES:(ch + 1) * LANES] if is_q else dkvv[:, (ch - N_Q_CHUNKS) * LANES:(ch - N_Q_CHUNKS + 1) * LANES]
            _, vjp = jax.vjp(lambda xx, gg, scale=scale: _qk_chunk(xx, gg, cs, sn, scale),
                             p[:, ch * LANES:(ch + 1) * LANES], qg if is_q else kg)
            dx, dg = vjp(ct)
            outs.append(dx)
            dgains[0 if is_q else 1] = dgains[0 if is_q else 1] + dg
        outs.append(dkvv[:, kw:])
        return jnp.concatenate(outs, axis=1), dgains[0], dgains[1]

    return _rowwise(name, body, geo, 256,
                    [(proj, "row"), (cos, "tab"), (sin_signed, "tab"), (q_gain, "full"), (k_gain, "full"), (dq, "row"), (dkv, "row")],
                    [("row", proj.shape[1], BF16), ("gacc", 1, LANES), ("gacc", 1, LANES)])


def _attn_geometry(geo):
    assert geo.s % ATTN_BLOCK == 0 and geo.l % ATTN_BLOCK == 0 and geo.seg >= BAND
    return geo.seg // ATTN_BLOCK, geo.s // ATTN_BLOCK


def _attn_mask(j, s0, geo):
    r = lax.broadcasted_iota(jnp.int32, (ATTN_BLOCK, BAND), 0)
    n = lax.broadcasted_iota(jnp.int32, (ATTN_BLOCK, BAND), 1)
    dist = (s0 - j * ATTN_BLOCK) + n - r
    return (jnp.abs(dist) <= WINDOW) & (s0 + n < geo.s)


def _attn_probs(q, keys, valid, n_ctx, sink):
    s = _dot(q, keys, "nt")
    if valid is not None:
        s = jnp.concatenate([s[:, :n_ctx], jnp.where(valid, s[:, n_ctx:], NEG_INF)], axis=1)
    m = jnp.maximum(jnp.max(s, axis=-1, keepdims=True), sink)
    e, e_sink = jnp.exp(s - m), jnp.exp(sink - m)
    inv = 1.0 / (jnp.sum(e, axis=-1, keepdims=True) + e_sink)
    return e * inv, e_sink * inv


def _attn_keys(ref, s0, geo, with_band):
    ctx = ref[geo.s:geo.seg, :]
    return jnp.concatenate([ctx, ref[pl.ds(s0, BAND), :]], axis=0) if with_band else ctx


def _attention(geo, qkv, sink, name, job=None):
    n_blocks, n_x_blocks = _attn_geometry(geo)
    qw, kw = N_HEADS * HEAD_DIM, N_KV_HEADS * HEAD_DIM
    group = N_HEADS // N_KV_HEADS
    carrier = _Carrier(job, 4, 1, 0)

    def kern(*refs):
        (sink_ref, q_ref, k_ref, v_ref, o_ref), job_refs = carrier.split(refs)
        j = pl.program_id(1)
        carrier.run(job_refs, pl.program_id(0) * n_blocks + j, geo.b * n_blocks)
        s0 = pl.multiple_of(jnp.clip((j - 1) * ATTN_BLOCK, 0, geo.seg - BAND), ATTN_BLOCK)

        def heads(with_band):
            valid = _attn_mask(j, s0, geo) if with_band else None
            k_all, v_all = _attn_keys(k_ref, s0, geo, with_band), _attn_keys(v_ref, s0, geo, with_band)
            for h in range(N_HEADS):
                kv = slice((h // group) * HEAD_DIM, (h // group + 1) * HEAD_DIM)
                p, _ = _attn_probs(q_ref[:, h * HEAD_DIM:(h + 1) * HEAD_DIM], k_all[:, kv], valid, geo.l, sink_ref[h])
                o_ref[:, h * HEAD_DIM:(h + 1) * HEAD_DIM] = _dot(p, v_all[:, kv], "nn").astype(BF16)

        pl.when(j < n_x_blocks)(lambda: heads(True))
        pl.when(j >= n_x_blocks)(lambda: heads(False))

    res = pl.pallas_call(
        kern, name=name, grid=(geo.b, n_blocks),
        in_specs=[pl.BlockSpec(memory_space=pltpu.SMEM),
                  pl.BlockSpec((ATTN_BLOCK, qw), lambda b, j: (b * n_blocks + j, 0)),
                  pl.BlockSpec((geo.seg, kw), lambda b, j: (b, qw // kw)),
                  pl.BlockSpec((geo.seg, kw), lambda b, j: (b, qw // kw + 1))] + carrier.in_specs(),
        out_specs=[pl.BlockSpec((ATTN_BLOCK, qw), lambda b, j: (b * n_blocks + j, 0))] + carrier.out_specs(),
        out_shape=[jax.ShapeDtypeStruct((geo.r, qw), BF16)] + carrier.out_shapes(),
        scratch_shapes=carrier.scratch(), input_output_aliases=carrier.aliases(),
        compiler_params=_cparams("arbitrary", "arbitrary"),
    )(sink, qkv, qkv, qkv, *carrier.operands())
    (o,), extra = carrier.results(res)
    return o, extra


def _attention_bwd(geo, qkv, sink, do, name, job=None):
    n_blocks, n_x_blocks = _attn_geometry(geo)
    qw, kw = N_HEADS * HEAD_DIM, N_KV_HEADS * HEAD_DIM
    group = N_HEADS // N_KV_HEADS

    carrier = _Carrier(job, 5, 3, 1)

    def kern(*refs):
        (sink_ref, q_ref, k_ref, v_ref, do_ref, dq_ref, dkv_out_ref, dsink_ref, dkv_ref), job_refs = carrier.split(refs)
        b, j = pl.program_id(0), pl.program_id(1)
        carrier.run(job_refs, b * n_blocks + j, geo.b * n_blocks)
        s0 = pl.multiple_of(jnp.clip((j - 1) * ATTN_BLOCK, 0, geo.seg - BAND), ATTN_BLOCK)

        @pl.when(j == 0)
        def _():
            dkv_ref[...] = jnp.zeros_like(dkv_ref)

        @pl.when((j == 0) & (b == 0))
        def _():
            dsink_ref[...] = jnp.zeros_like(dsink_ref)

        def heads(with_band):
            valid = _attn_mask(j, s0, geo) if with_band else None
            k_all, v_all = _attn_keys(k_ref, s0, geo, with_band), _attn_keys(v_ref, s0, geo, with_band)
            for g in range(N_KV_HEADS):
                kv = slice(g * HEAD_DIM, (g + 1) * HEAD_DIM)
                keys, vals = k_all[:, kv], v_all[:, kv]
                group_heads = [slice(h * HEAD_DIM, (h + 1) * HEAD_DIM) for h in range(g * group, (g + 1) * group)]
                ds_rows, p_rows = [], []
                for h, hs in zip(range(g * group, (g + 1) * group), group_heads):
                    dout = do_ref[:, hs]
                    p, p_sink = _attn_probs(q_ref[:, hs], keys, valid, geo.l, sink_ref[h])
                    dp = _dot(dout, vals, "nt")
                    dsum = jnp.sum(p * dp, axis=-1, keepdims=True)
                    ds = (p * (dp - dsum)).astype(BF16)
                    dq_ref[:, hs] = _dot(ds, keys, "nn").astype(BF16)
                    ds_rows.append(ds)
                    p_rows.append(p.astype(BF16))
                    dsink_ref[h:h + 1, :] += jnp.broadcast_to(-jnp.sum(p_sink * dsum, axis=0, keepdims=True), (1, LANES))
                q_rows = jnp.concatenate([q_ref[:, hs] for hs in group_heads], axis=0)
                do_rows = jnp.concatenate([do_ref[:, hs] for hs in group_heads], axis=0)
                dk = _dot(jnp.concatenate(ds_rows, axis=0), q_rows, "tn")
                dv = _dot(jnp.concatenate(p_rows, axis=0), do_rows, "tn")
                vv = slice(kw + g * HEAD_DIM, kw + (g + 1) * HEAD_DIM)
                dkv_ref[geo.s:geo.seg, kv] += dk[:geo.l]
                dkv_ref[geo.s:geo.seg, vv] += dv[:geo.l]
                if with_band:
                    dkv_ref[pl.ds(s0, BAND), kv] += dk[geo.l:]
                    dkv_ref[pl.ds(s0, BAND), vv] += dv[geo.l:]

        pl.when(j < n_x_blocks)(lambda: heads(True))
        pl.when(j >= n_x_blocks)(lambda: heads(False))

        @pl.when(j == n_blocks - 1)
        def _():
            dkv_out_ref[...] = dkv_ref[...].astype(BF16)

    res = pl.pallas_call(
        kern, name=name, grid=(geo.b, n_blocks),
        in_specs=[pl.BlockSpec(memory_space=pltpu.SMEM),
                  pl.BlockSpec((ATTN_BLOCK, qw), lambda b, j: (b * n_blocks + j, 0)),
                  pl.BlockSpec((geo.seg, kw), lambda b, j: (b, qw // kw)),
                  pl.BlockSpec((geo.seg, kw), lambda b, j: (b, qw // kw + 1)),
                  pl.BlockSpec((ATTN_BLOCK, qw), lambda b, j: (b * n_blocks + j, 0))] + carrier.in_specs(),
        out_specs=[pl.BlockSpec((ATTN_BLOCK, qw), lambda b, j: (b * n_blocks + j, 0)),
                   pl.BlockSpec((geo.seg, 2 * kw), lambda b, j: (b, 0)),
                   pl.BlockSpec((N_HEADS, LANES), lambda b, j: (0, 0))] + carrier.out_specs(),
        out_shape=[jax.ShapeDtypeStruct((geo.r, qw), BF16), jax.ShapeDtypeStruct((geo.r, 2 * kw), BF16),
                   jax.ShapeDtypeStruct((N_HEADS, LANES), F32)] + carrier.out_shapes(),
        scratch_shapes=[pltpu.VMEM((geo.seg, 2 * kw), F32)] + carrier.scratch(), input_output_aliases=carrier.aliases(),
        compiler_params=_cparams("arbitrary", "arbitrary"),
    )(sink, qkv, qkv, qkv, do, *carrier.operands())
    (dq, dkv, dsink), extra = carrier.results(res)
    return dq, dkv, dsink, extra


RET_QK_W = RET_HEADS * RET_QK_DIM
K_SCALE = RET_QK_DIM ** -0.5


def _ret_prep(geo, proj, cos, sin_signed, name):
    def body(i, p, cs, sn):
        cs2, sn2 = jnp.concatenate([cs] * RET_HEADS, axis=1), jnp.concatenate([sn] * RET_HEADS, axis=1)
        q = _rope(p[:, :RET_QK_W], cs2, sn2, RET_QK_DIM // 4)
        k = _rope(p[:, RET_QK_W:2 * RET_QK_W], cs2, sn2, RET_QK_DIM // 4) * K_SCALE
        return jnp.concatenate([q, k, p[:, 2 * RET_QK_W:]], axis=1)

    return _rowwise(name, body, geo, 128, [(proj, ("rowc", 2 * RET_QK_W + RET_VWIDTH, 0)), (cos, "tab"), (sin_signed, "tab")],
                    [("row", 2 * RET_QK_W + RET_VWIDTH, BF16)])


def _ret_prep_bwd(geo, dq, dk, dv, dgate, cos, sin_signed, name):
    def body(i, dqv, dkv, dvv, dg, cs, sn):
        cs2, sn2 = jnp.concatenate([cs] * RET_HEADS, axis=1), jnp.concatenate([sn] * RET_HEADS, axis=1)
        dkv = dkv * K_SCALE
        dqv = dqv * cs2 + _swap_halves(dqv * sn2, RET_QK_DIM // 4)
        dkv = dkv * cs2 + _swap_halves(dkv * sn2, RET_QK_DIM // 4)
        return jnp.concatenate([dqv, dkv, dvv, dg], axis=1)

    return _rowwise(name, body, geo, 128,
                    [(dq, "row"), (dk, "row"), (dv, "row"), (dgate, "row"), (cos, "tab"), (sin_signed, "tab")],
                    [("row", 2 * RET_QK_W + 2 * RET_VWIDTH, BF16)])


def _ret_step(state, q, k, v, lg, rev):
    c = RET_CHUNK
    ri = lax.broadcasted_iota(jnp.int32, (c, 1), 0).astype(F32)
    cj = lax.broadcasted_iota(jnp.int32, (1, c), 1).astype(F32)
    if rev:
        dist, q_decay, k_decay = cj - ri, jnp.exp(lg * (c - ri)), jnp.exp(lg * ri)
    else:
        dist, q_decay, k_decay = ri - cj, jnp.exp(lg * (ri + 1.0)), jnp.exp(lg * (c - 1.0 - ri))
    intra = jnp.where(dist >= 0, jnp.exp(lg * jnp.maximum(dist, 0.0)), 0.0)
    scores = _mm(q, k, "nt") * intra
    out = _mm(scores, v, "nn") + _mm(q, state, "nn") * q_decay
    new_state = state * jnp.exp(lg * c) + _mm(k * k_decay, v, "tn")
    return new_state, out


def _ret_state0(kc, vc, lg, rev):
    n = kc.shape[0]
    t = lax.broadcasted_iota(jnp.int32, (n, 1), 0).astype(F32)
    decay = jnp.exp(lg * t) if rev else jnp.exp(lg * (n - 1.0 - t))
    return _mm(kc * decay, vc, "tn")


def _ret_specs(geo):
    nq = RET_HEADS
    return [pl.BlockSpec((2 * RET_HEADS, LANES), lambda b, h: (0, 0)),
            pl.BlockSpec((geo.seg, RET_QK_DIM), lambda b, h: (b, h)),
            pl.BlockSpec((geo.seg, RET_QK_DIM), lambda b, h: (b, nq + h)),
            pl.BlockSpec((geo.seg, RET_V_DIM), lambda b, h: (b, nq + h))]


def _retention(geo, qkv, log_g, name):
    nc = geo.s // RET_CHUNK

    def kern(lg_ref, q_ref, k_ref, v_ref, o_ref, st_ref):
        h = pl.program_id(1)
        for d, rev in ((0, False), (1, True)):
            lg = lg_ref[pl.ds(d * RET_HEADS + h, 1), 0:1]
            st_ref[...] = _ret_state0(k_ref[geo.s:geo.seg, :].astype(F32), v_ref[geo.s:geo.seg, :].astype(F32), lg, rev)

            def chunk(ci, carry, d=d, rev=rev, lg=lg):
                r0 = pl.multiple_of((nc - 1 - ci if rev else ci) * RET_CHUNK, RET_CHUNK)
                rows = pl.ds(r0, RET_CHUNK)
                new_state, out = _ret_step(st_ref[...], q_ref[rows, :].astype(F32), k_ref[rows, :].astype(F32),
                                           v_ref[rows, :].astype(F32), lg, rev)
                st_ref[...] = new_state
                if d == 0:
                    o_ref[rows, :] = out
                else:
                    o_ref[rows, :] += out
                return carry

            lax.fori_loop(0, nc, chunk, 0)
        o_ref[geo.s:geo.seg, :] = jnp.zeros((geo.l, RET_V_DIM), F32)

    return pl.pallas_call(
        kern, name=name, grid=(geo.b, RET_HEADS), in_specs=_ret_specs(geo),
        out_specs=pl.BlockSpec((geo.seg, RET_V_DIM), lambda b, h: (b, h)),
        out_shape=jax.ShapeDtypeStruct((geo.r, RET_VWIDTH), F32),
        scratch_shapes=[pltpu.VMEM((RET_QK_DIM, RET_V_DIM), F32)],
        compiler_params=_cparams("parallel", "arbitrary"),
    )(log_g, qkv, qkv, qkv)


def _retention_bwd(geo, qkv, log_g, do, name):
    nc = geo.s // RET_CHUNK
    ctx = slice(geo.s, geo.seg)

    def kern(lg_ref, q_ref, k_ref, v_ref, do_ref, dq_ref, dk_ref, dv_ref, dlg_ref, states_ref, dst_ref, aq_ref, ak_ref, av_ref):
        b, h = pl.program_id(0), pl.program_id(1)

        @pl.when((b == 0) & (h == 0))
        def _():
            dlg_ref[...] = jnp.zeros_like(dlg_ref)

        for d, rev in ((0, False), (1, True)):
            row = pl.ds(d * RET_HEADS + h, 1)
            lg = lg_ref[row, 0:1]
            kc, vc = k_ref[ctx, :].astype(F32), v_ref[ctx, :].astype(F32)
            states_ref[0] = _ret_state0(kc, vc, lg, rev)

            def rows_of(ci, rev=rev):
                return pl.ds(pl.multiple_of((nc - 1 - ci if rev else ci) * RET_CHUNK, RET_CHUNK), RET_CHUNK)

            def load(rows):
                return q_ref[rows, :].astype(F32), k_ref[rows, :].astype(F32), v_ref[rows, :].astype(F32)

            def replay(ci, carry, rev=rev, lg=lg, rows_of=rows_of, load=load):
                states_ref[ci + 1] = _ret_step(states_ref[ci], *load(rows_of(ci)), lg, rev)[0]
                return carry

            lax.fori_loop(0, nc - 1, replay, 0)
            dst_ref[...] = jnp.zeros_like(dst_ref)

            def emit(rows, dq, dk, dv, d=d):
                if d == 0:
                    ak_ref[rows, :], av_ref[rows, :] = dk, dv
                    if dq is not None:
                        aq_ref[rows, :] = dq
                else:
                    dk_ref[rows, :] = (ak_ref[rows, :] + dk).astype(BF16)
                    dv_ref[rows, :] = (av_ref[rows, :] + dv).astype(BF16)
                    if dq is not None:
                        dq_ref[rows, :] = (aq_ref[rows, :] + dq).astype(BF16)

            def back(t, dlg, rev=rev, lg=lg, rows_of=rows_of, load=load, emit=emit):
                ci = nc - 1 - t
                rows = rows_of(ci)
                _, vjp = jax.vjp(lambda st, q, k, v, g: _ret_step(st, q, k, v, g, rev), states_ref[ci], *load(rows), lg)
                dstate, dq, dk, dv, dg = vjp((dst_ref[...], do_ref[rows, :].astype(F32)))
                dst_ref[...] = dstate
                emit(rows, dq, dk, dv)
                return dlg + dg

            dlg = lax.fori_loop(0, nc, back, jnp.zeros((1, 1), F32))
            _, vjp = jax.vjp(lambda kk, vv, g: _ret_state0(kk, vv, g, rev), kc, vc, lg)
            dkc, dvc, dg = vjp(dst_ref[...])
            emit(ctx, None, dkc, dvc)
            dlg_ref[row, :] += jnp.broadcast_to(dlg + dg, (1, LANES))
        dq_ref[ctx, :] = jnp.zeros((geo.l, RET_QK_DIM), BF16)

    nq = RET_HEADS
    return pl.pallas_call(
        kern, name=name, grid=(geo.b, RET_HEADS),
        in_specs=_ret_specs(geo) + [pl.BlockSpec((geo.seg, RET_V_DIM), lambda b, h: (b, h))],
        out_specs=[pl.BlockSpec((geo.seg, RET_QK_DIM), lambda b, h: (b, h)),
                   pl.BlockSpec((geo.seg, RET_QK_DIM), lambda b, h: (b, h)),
                   pl.BlockSpec((geo.seg, RET_V_DIM), lambda b, h: (b, h)),
                   pl.BlockSpec((2 * RET_HEADS, LANES), lambda b, h: (0, 0))],
        out_shape=[jax.ShapeDtypeStruct((geo.r, RET_QK_W), BF16), jax.ShapeDtypeStruct((geo.r, RET_QK_W), BF16),
                   jax.ShapeDtypeStruct((geo.r, RET_VWIDTH), BF16), jax.ShapeDtypeStruct((2 * RET_HEADS, LANES), F32)],
        scratch_shapes=[pltpu.VMEM((nc, RET_QK_DIM, RET_V_DIM), F32), pltpu.VMEM((RET_QK_DIM, RET_V_DIM), F32),
                        pltpu.VMEM((geo.seg, RET_QK_DIM), F32), pltpu.VMEM((geo.seg, RET_QK_DIM), F32),
                        pltpu.VMEM((geo.seg, RET_V_DIM), F32)],
        compiler_params=_cparams("arbitrary", "arbitrary"),
    )(log_g, qkv, qkv, qkv, do)


def _gated(o, g, gain):
    outs = []
    for h in range(RET_HEADS):
        cols = slice(h * RET_V_DIM, (h + 1) * RET_V_DIM)
        oh = o[:, cols]
        mu = jnp.mean(oh, axis=-1, keepdims=True)
        var = jnp.mean(jnp.square(oh - mu), axis=-1, keepdims=True)
        outs.append(_silu(g[:, cols]) * ((oh - mu) * lax.rsqrt(var + EPS) * gain[:, cols]))
    return jnp.concatenate(outs, axis=1)


def _ret_gated(geo, o, proj, gain, name):
    def body(i, ov, gv, gn):
        return _gated(ov, gv, gn)

    gate_block = (2 * RET_QK_W + RET_VWIDTH) // RET_VWIDTH
    return _rowwise(name, body, geo, 128, [(o, "row"), (proj, ("rowc", RET_VWIDTH, gate_block)), (gain, "full")],
                    [("row", RET_VWIDTH, BF16)])


def _ret_gated_bwd(geo, o, proj, gain, dout, name):
    def body(i, ov, gv, gn, dv):
        _, vjp = jax.vjp(_gated, ov, gv, gn)
        return vjp(dv)

    gate_block = (2 * RET_QK_W + RET_VWIDTH) // RET_VWIDTH
    return _rowwise(name, body, geo, 128,
                    [(o, "row"), (proj, ("rowc", RET_VWIDTH, gate_block)), (gain, "full"), (dout, "row")],
                    [("row", RET_VWIDTH, BF16), ("row", RET_VWIDTH, BF16), ("gacc", 1, RET_VWIDTH)])


def _whole(name, fn, out_shapes, *arrays):
    n = len(arrays)

    def kern(*refs):
        res = fn(*[r[...] for r in refs[:n]])
        for ref, val in zip(refs[n:], res):
            ref[...] = val.astype(ref.dtype)

    return pl.pallas_call(kern, name=name, out_shape=out_shapes)(*arrays)


def _rope_tables(geo, head_dim):
    rows = geo.s // GRID_W
    row = jnp.broadcast_to(jnp.arange(rows, dtype=jnp.int32)[:, None], (rows, GRID_W)).reshape(geo.s)
    col = jnp.broadcast_to(jnp.arange(GRID_W, dtype=jnp.int32)[None, :], (rows, GRID_W)).reshape(geo.s)
    axis_dim = head_dim // 2
    inv = ROPE_BASE ** (-jnp.arange(0, axis_dim, 2, dtype=F32) / axis_dim)
    ang_r = row.astype(F32)[:, None] * inv
    ang_c = col.astype(F32)[:, None] * inv
    cos = jnp.concatenate([jnp.cos(ang_r)] * 2 + [jnp.cos(ang_c)] * 2, axis=1)
    sin = jnp.concatenate([-jnp.sin(ang_r), jnp.sin(ang_r), -jnp.sin(ang_c), jnp.sin(ang_c)], axis=1)
    cos = jnp.concatenate([cos, jnp.ones((geo.l, head_dim), F32)], axis=0)
    sin = jnp.concatenate([sin, jnp.zeros((geo.l, head_dim), F32)], axis=0)
    reps = max(1, LANES // head_dim)
    return jnp.tile(cos, (1, reps)), jnp.tile(sin, (1, reps))


def _row_tile(r):
    return next(t for t in (1024, 512, 256, 128) if r % t == 0)


MOD_ROWS = 8


def _local_step(x, c, ctx, target, sp, wts, plan=None):
    nb, s, d = x.shape
    geo = _Rows(nb, s, ctx.shape[1])
    assert nb + 1 <= MOD_ROWS and d == D_MODEL
    tm = _row_tile(geo.r)
    z = jnp.concatenate([x, ctx], axis=1).reshape(geo.r, d)
    cvec = jnp.concatenate([c, sp["c_ctx"][None, :], jnp.zeros((MOD_ROWS - nb - 1, d), F32)], axis=0)
    cact, = _whole("cond_silu", lambda v: (_silu(v),), [jax.ShapeDtypeStruct(cvec.shape, F32)], cvec)
    cos64, sin64 = _rope_tables(geo, HEAD_DIM)
    cos256, sin256 = _rope_tables(geo, RET_QK_DIM)
    q_gain = jnp.tile(sp["q_norm"].reshape(1, HEAD_DIM), (1, LANES // HEAD_DIM))
    k_gain = jnp.tile(sp["k_norm"].reshape(1, HEAD_DIM), (1, LANES // HEAD_DIM))
    sink = sp["sink"].reshape(N_HEADS)
    log_g = jnp.broadcast_to(sp["log_g"].reshape(2 * RET_HEADS, 1), (2 * RET_HEADS, LANES))
    gn_g = sp["gn_g"].reshape(1, RET_VWIDTH)

    def modulation(i):
        mod = _mm_nn(cact, wts["ada"][i], F32, f"mod{i}", MOD_ROWS, wts["ada"][i].shape[2], d, bias=sp["ada_b"][i][None, :])
        return mod[:nb + 1, None, :]

    saved = []
    mods = [modulation(0), None]
    h1 = _norm_mod(geo, z, sp["norm1_g"][0][None, :], mods[0], 0, "norm1_0")
    for i in range(2):
        mod3 = mods[i]
        n1, n2 = sp["norm1_g"][i][None, :], sp["norm2_g"][i][None, :]
        if i == 0:
            proj = _mm_nn(h1, wts["attn_qkv"], F32, "attn_qkv", tm, wts["attn_qkv"].shape[1], d)
            prep = _attn_prep(geo, proj, cos64, sin64, q_gain, k_gain, "attn_prep")
            o, late = _attention(geo, prep, sink, "attn", plan.gather_job() if plan else None)
            if plan:
                plan.late_weights(late, wts)
            mods[1] = modulation(1)
            oraw = None
            w_o = wts["attn_o"]
        else:
            proj = _mm_nn(h1, wts["ret_qkvg"], BF16, "ret_qkvg", tm, wts["ret_qkvg"].shape[2], d)
            prep = _ret_prep(geo, proj, cos256, sin256, "ret_prep")
            oraw = _retention(geo, prep, log_g, "ret")
            o = _ret_gated(geo, oraw, proj, gn_g, "ret_gated")
            w_o = wts["ret_o"]
        zmid, mix, h2 = _mm_nn_gate_residual(geo, o, w_o, z, mod3, 2 * d, f"mix_out{i}", norm=(n2, mod3, 3 * d))
        u, a = _ffn_in_swiglu(h2, wts["ffn_in"][i], f"ffn_in{i}")
        next_norm = (sp["norm1_g"][1][None, :], mods[1], 0) if i == 0 else None
        zout, f, h1_next = _mm_nn_gate_residual(geo, a, wts["ffn_out"][i], zmid, mod3, 5 * d, f"ffn_out{i}", norm=next_norm)
        saved.append(dict(z=z, mod3=mod3, n1=n1, n2=n2, h1=h1, proj=proj, prep=prep, o=o, oraw=oraw, mix=mix, zmid=zmid,
                          h2=h2, u=u, a=a, f=f))
        z, h1 = zout, h1_next

    dz, loss, df, dg2 = _loss_head(geo, z, target.reshape(nb * s, d), saved[1]["f"], saved[1]["mod3"], 5 * d, "loss")

    big, small = {}, {}
    dmods = [None, None]
    for i in (1, 0):
        sv = saved[i]
        mod3 = sv["mod3"]
        carry = plan is not None and i == 0
        du, land = _ffn_out_dx_swiglu_bwd(df, wts["ffn_out"][i], sv["u"], f"ffn_out_dx{i}", plan.layer1.swap_job() if carry else None)
        if carry:
            plan.layer1.after_swap(land)
        big[f"ffn_out{i}"] = _mm_tn(sv["a"], df, f"ffn_out_dw{i}", D_FF // 2, 1024, tm).reshape(N_CHIPS, D_FF // N_CHIPS, d)
        n4 = wts["ffn_in"][i].shape[2]
        dh2 = _mm_nt(du, wts["ffn_in"][i], BF16, f"ffn_in_dx{i}", tm, 1024, n4)
        big[f"ffn_in{i}"] = _mm_tn(sv["h2"], du, f"ffn_in_dw{i}", 1024, n4, tm, shards=N_CHIPS)
        if carry:
            plan.start_layer0_ffn(big)
        dzmid, dsh2, dsc2, dn2, dmix, dg1, *land = _norm_mod_bwd(geo, sv["zmid"], sv["n2"], mod3, 3 * d, dh2, dz, f"norm2_bwd{i}",
                                                                 gated=(sv["mix"], mod3, 2 * d),
                                                                 job=plan.layer0_ffn.swap_job() if carry else None)
        if carry:
            plan.layer0_ffn.after_swap(land[0])
        if i == 0:
            do = _mm_nt(dmix, wts["attn_o"], BF16, "attn_out_dx", tm, 1024, 1024)
            big["attn_o"] = _mm_tn(sv["o"], dmix, "attn_out_dw", 1024, 1024, tm).reshape(N_CHIPS, 1024 // N_CHIPS, d)
            dq, dkv, dsink, land = _attention_bwd(geo, sv["prep"], sink, do, "attn_bwd", plan.exchange_job() if plan else None)
            if plan:
                plan.after_exchange(land)
            dproj, dqg, dkg = _attn_prep_bwd(geo, sv["proj"], cos64, sin64, q_gain, k_gain, dq, dkv, "attn_prep_bwd")
            small["q_norm"] = dqg[0, :HEAD_DIM] + dqg[0, HEAD_DIM:]
            small["k_norm"] = dkg[0, :HEAD_DIM] + dkg[0, HEAD_DIM:]
            small["sink"] = dsink[:, 0]
            wq = wts["attn_qkv"]
            dh1 = _mm_nt(dproj, wq, BF16, "attn_qkv_dx", tm, 1024, wq.shape[1])
            dwq = _mm_tn(sv["h1"], dproj, "attn_qkv_dw", 1024, wq.shape[1], tm)
            big["attn_qkv"] = dwq.reshape(d, N_CHIPS, -1).transpose(1, 0, 2)
        else:
            do = _mm_nt(dmix, wts["ret_o"], BF16, "ret_out_dx", tm, 1024, 1024)
            big["ret_o"] = _mm_tn(sv["o"], dmix, "ret_out_dw", 1024, 1024, tm).reshape(N_CHIPS, RET_VWIDTH // N_CHIPS, d)
            doraw, dgate, dgn = _ret_gated_bwd(geo, sv["oraw"], sv["proj"], gn_g, do, "ret_gated_bwd")
            small["gn_g"] = dgn[0]
            dq, dk, dv, dlg = _retention_bwd(geo, sv["prep"], log_g, doraw, "ret_bwd")
            small["log_g"] = dlg[:, 0].reshape(2, RET_HEADS)
            dproj = _ret_prep_bwd(geo, dq, dk, dv, dgate, cos256, sin256, "ret_prep_bwd")
            wq = wts["ret_qkvg"]
            dh1 = _mm_nt(dproj, wq, BF16, "ret_qkvg_dx", tm, 1024, wq.shape[2])
            big["ret_qkvg"] = _mm_tn(sv["h1"], dproj, "ret_qkvg_dw", 1024, wq.shape[2], tm, shards=N_CHIPS)
        below = (saved[0]["f"], saved[0]["mod3"], 5 * d) if i == 1 else None
        dz, dsh1, dsc1, dn1, *below_grads = _norm_mod_bwd(geo, sv["z"], sv["n1"], mod3, 0, dh1, dzmid, f"norm1_bwd{i}", gated=below,
                                                              latent_only=i == 0)
        small[f"norm1_g{i}"], small[f"norm2_g{i}"] = dn1[0], dn2[0]
        parts = [dsh1, dsc1, dg1, dsh2, dsc2, dg2]
        rows = jnp.concatenate([jnp.concatenate([p[:nb, 0, :] for p in parts], axis=1),
                                jnp.concatenate([jnp.sum(p[nb:, 0, :], axis=0, keepdims=True) for p in parts], axis=1),
                                jnp.zeros((MOD_ROWS - nb - 1, 6 * d), F32)], axis=0)
        dmods[i] = rows
        if below_grads:
            df, dg2 = below_grads
        small[f"ada_b{i}"] = jnp.sum(rows, axis=0)
        big[f"ada{i}"] = _mm_tn(cact, rows, f"ada_dw{i}", 1024, wts["ada"][i].shape[2], MOD_ROWS, shards=N_CHIPS)
        if plan and i == 1:
            plan.start_layer1(big)

    dcact = [_mm_nt(dmods[i], wts["ada"][i], F32, f"ada_dx{i}", MOD_ROWS, 1024, wts["ada"][i].shape[2]) for i in range(2)]

    def silu_bwd(v, d0, d1):
        sg = _sigmoid(v)
        return ((d0 + d1) * (sg * (1.0 + v * (1.0 - sg))),)

    dcvec, = _whole("cond_silu_bwd", silu_bwd, [jax.ShapeDtypeStruct(cvec.shape, F32)], cvec, dcact[0], dcact[1])
    small["c_ctx"] = dcvec[nb]
    return loss, dz, big, small


def _adamw(w, g, m, v, name):
    rows, cols = w.shape
    tr = next((t for t in (256, 128, 64, 32, 16, 8) if rows % t == 0), rows)
    c1 = 1.0 - ADAM_B1 ** ADAM_STEP
    c2 = 1.0 - ADAM_B2 ** ADAM_STEP

    def kern(w_ref, g_ref, m_ref, v_ref, d_ref, nm_ref, nv_ref):
        gv = g_ref[...]
        nm = ADAM_B1 * m_ref[...] + (1.0 - ADAM_B1) * gv
        nv = ADAM_B2 * v_ref[...] + (1.0 - ADAM_B2) * jnp.square(gv)
        d_ref[...] = -ADAM_LR * ((nm / c1) / (jnp.sqrt(nv / c2) + ADAM_EPS) + ADAM_WD * w_ref[...])
        nm_ref[...] = nm
        nv_ref[...] = nv

    spec = pl.BlockSpec((tr, cols), lambda i: (i, 0))
    return pl.pallas_call(
        kern, name=name, grid=(rows // tr,), in_specs=[spec] * 4, out_specs=[spec] * 3,
        out_shape=[jax.ShapeDtypeStruct(w.shape, F32)] * 3, compiler_params=_cparams("parallel"),
    )(w, g, m, v)


N_DEVICES = 8


def _mesh_pos():
    return lax.axis_index("x"), lax.axis_index("y"), lax.axis_index("c")


def _other_chips(x, y):
    return [(1 - x, y), (x, 1 - y), (1 - x, 1 - y)]


def _hbm(n):
    return [pl.BlockSpec(memory_space=pl.ANY)] * n


def _remote(src, dst, send_sem, recv_sem, device):
    return pltpu.make_async_remote_copy(src_ref=src, dst_ref=dst, send_sem=send_sem, recv_sem=recv_sem,
                                        device_id=device, device_id_type=MESH)


def _scalar_spec(grid, in_specs, out_specs):
    return pltpu.PrefetchScalarGridSpec(num_scalar_prefetch=1, grid=grid, in_specs=in_specs, out_specs=out_specs)


def _place_shard(param, layer, pos, name):
    _, r, cols = param.shape
    tr = _slab_tile(r)

    def kern(pos_ref, s_ref, o_ref):
        o_ref[...] = s_ref[...].astype(BF16)

    return pl.pallas_call(
        kern, name=name, out_shape=jax.ShapeDtypeStruct((N_CHIPS, r, cols), BF16),
        grid_spec=_scalar_spec((r // tr,), [pl.BlockSpec((None, tr, cols), lambda i, p: (layer, i, 0))],
                               pl.BlockSpec((None, tr, cols), lambda i, p: (p[1], i, 0))),
        compiler_params=_cparams("parallel"),
    )(pos, param)


class _CommJob:
    def __init__(self, inputs, out_shapes, aliases, sem_shapes, stages, fractions=None):
        self.inputs, self.out_shapes, self.aliases, self.sem_shapes, self.stages = inputs, out_shapes, aliases, sem_shapes, stages
        self.fractions = fractions


def _merge_jobs(a, b):
    assert len(a.stages) == len(b.stages)
    ni, no, ns = len(a.inputs), len(a.out_shapes), len(a.sem_shapes)

    def both(sa, sb):
        def stage(ins, outs, sems):
            sa(ins[:ni], outs[:no], sems[:ns])
            sb(ins[ni:], outs[no:], sems[ns:])
        return stage

    aliases = dict(a.aliases)
    aliases.update({ni + i: no + o for i, o in b.aliases.items()})
    return _CommJob(a.inputs + b.inputs, a.out_shapes + b.out_shapes, aliases, a.sem_shapes + b.sem_shapes,
                    [both(sa, sb) for sa, sb in zip(a.stages, b.stages)])


def _run_job(job, name):
    n_in, n_out = len(job.inputs), len(job.out_shapes)

    def body(*refs):
        for stage in job.stages:
            stage(refs[:n_in], refs[n_in:n_in + n_out], refs[n_in + n_out:])

    return pl.pallas_call(
        body, name=name, in_specs=_hbm(n_in), out_specs=_hbm(n_out), out_shape=job.out_shapes,
        input_output_aliases=job.aliases, scratch_shapes=job.sem_shapes,
    )(*job.inputs)


def _job_marks(job, steps):
    mid = len(job.stages) - 2
    fractions = job.fractions or [(s + 1) / (mid + 1) for s in range(mid)]
    return [0] + [min(steps - 1, 1 + int((steps - 1) * f)) for f in fractions] + [steps - 1]


def _gather_job(placed):
    n = len(placed)

    def half(w, which):
        r2 = placed[w].shape[1] // 2
        return pl.ds(which * r2, r2)

    def ici_copies(outs, sems, slot_of):
        x, y, c = _mesh_pos()
        res = []
        for w in range(n):
            for k, (px, py) in enumerate(_other_chips(x, y)):
                slab = outs[w].at[slot_of(x, y, px, py), half(w, c)]
                res.append((slab, _remote(slab, slab, sems[0].at[w, k], sems[1].at[w, k], (px, py, c))))
        return res

    def forwards(outs, sems, which_core):
        x, y, c = _mesh_pos()
        res = []
        for w in range(n):
            for k, (px, py) in enumerate(_other_chips(x, y)):
                slab = outs[w].at[2 * px + py, half(w, which_core(c))]
                res.append(_remote(slab, slab, sems[2].at[w, k], sems[3].at[w, k], (x, y, 1 - c)))
        return res

    def send(ins, outs, sems):
        for _, cp in ici_copies(outs, sems, lambda x, y, px, py: 2 * x + y):
            cp.start()

    def forward_of(w):
        def forward(ins, outs, sems):
            arrivals = ici_copies(outs, sems, lambda x, y, px, py: 2 * px + py)[3 * w:3 * w + 3]
            for (_, arrival), fwd in zip(arrivals, forwards(outs, sems, lambda c: c)[3 * w:3 * w + 3]):
                arrival.wait_recv()
                fwd.start()
        return forward

    def finish(ins, outs, sems):
        for cp in forwards(outs, sems, lambda c: 1 - c):
            cp.wait_recv()
        for _, cp in ici_copies(outs, sems, lambda x, y, px, py: 2 * x + y):
            cp.wait_send()
        for cp in forwards(outs, sems, lambda c: c):
            cp.wait_send()

    sizes = [p.shape[1] * p.shape[2] for p in placed]
    fractions = [sum(sizes[:w + 1]) / sum(sizes) for w in range(n)]
    return _CommJob(list(placed), [jax.ShapeDtypeStruct(p.shape, p.dtype) for p in placed], {w: w for w in range(n)},
                    [pltpu.SemaphoreType.DMA((n, 3))] * 4, [send] + [forward_of(w) for w in range(n)] + [finish], fractions)


def _pair_swap_job(grads):
    n = len(grads)

    def copies(ins, outs, sems):
        x, y, c = _mesh_pos()
        res = []
        for w in range(n):
            r2 = grads[w].shape[1] // 2
            res.append(_remote(ins[w].at[:, pl.ds((1 - c) * r2, r2)], outs[w], sems[0].at[w], sems[1].at[w], (x, y, 1 - c)))
        return res

    def send(ins, outs, sems):
        for cp in copies(ins, outs, sems):
            cp.start()

    def finish(ins, outs, sems):
        for cp in copies(ins, outs, sems):
            cp.wait()

    return _CommJob(list(grads), [jax.ShapeDtypeStruct((N_CHIPS, g.shape[1] // 2, g.shape[2]), F32) for g in grads], {},
                    [pltpu.SemaphoreType.DMA((n,))] * 2, [send, finish])


def _chip_exchange_job(hs):
    n = len(hs)

    def send(ins, outs, sems):
        x, y, c = _mesh_pos()
        for w in range(n):
            for k, (px, py) in enumerate(_other_chips(x, y)):
                _remote(ins[w].at[2 * px + py], outs[w].at[2 * x + y], sems[0].at[w, k], sems[1].at[w, k], (px, py, c)).start()

    def finish(ins, outs, sems):
        x, y, c = _mesh_pos()
        for w in range(n):
            for k, (px, py) in enumerate(_other_chips(x, y)):
                got = outs[w].at[2 * px + py]
                cp = _remote(ins[w].at[2 * px + py], got, sems[0].at[w, k], sems[1].at[w, k], (px, py, c))
                cp.wait_recv()
                cp.wait_send()

    return _CommJob(list(hs), [jax.ShapeDtypeStruct(h.shape, h.dtype) for h in hs], {},
                    [pltpu.SemaphoreType.DMA((n, 3))] * 2, [send, finish])


def _pair_share(ts, name):
    n = len(ts)

    def body(*refs):
        outs = refs[n:2 * n]
        send_sems, recv_sems = refs[2 * n:]
        x, y, c = _mesh_pos()
        sends = []
        for w in range(n):
            r2 = ts[w].shape[0] // 2
            mine = outs[w].at[pl.ds(c * r2, r2)]
            rc = _remote(mine, mine, send_sems.at[w], recv_sems.at[w], (x, y, 1 - c))
            rc.start()
            sends.append(rc)
        for w in range(n):
            r2 = ts[w].shape[0] // 2
            theirs = outs[w].at[pl.ds((1 - c) * r2, r2)]
            _remote(theirs, theirs, send_sems.at[w], recv_sems.at[w], (x, y, 1 - c)).wait_recv()
            sends[w].wait_send()

    return pl.pallas_call(
        body, name=name, in_specs=_hbm(n), out_specs=_hbm(n),
        out_shape=[jax.ShapeDtypeStruct(t.shape, F32) for t in ts],
        input_output_aliases={w: w for w in range(n)},
        scratch_shapes=[pltpu.SemaphoreType.DMA((n,))] * 2,
    )(*ts)


def _slab_tile(rows):
    return next(t for t in (512, 256, 176, 128, 64, 32, 16) if rows % t == 0)


def _sum_pair(grad, land, pos, name):
    _, r2, cols = land.shape
    tr = _slab_tile(r2)
    nt = r2 // tr

    def kern(pos_ref, a_ref, b_ref, o_ref):
        o_ref[...] = (a_ref[...] + b_ref[...]).astype(BF16)

    spec = pl.BlockSpec((None, tr, cols), lambda j, i, p: (j, i, 0))
    return pl.pallas_call(
        kern, name=name, out_shape=jax.ShapeDtypeStruct(land.shape, BF16),
        grid_spec=_scalar_spec((N_CHIPS, nt), [pl.BlockSpec((None, tr, cols), lambda j, i, p: (j, p[0] * nt + i, 0)), spec], spec),
        compiler_params=_cparams("parallel", "parallel"),
    )(pos, grad, land)


def _sum_chips(hs, land, pos, name):
    _, r2, cols = land.shape
    tr = _slab_tile(r2)
    nt = r2 // tr

    def kern(pos_ref, h_ref, l_ref, o_ref):
        acc = jnp.zeros((tr, cols), F32)
        own = h_ref[...].astype(F32)
        for k in range(N_CHIPS):
            acc = acc + jnp.where(pos_ref[1] == k, own, l_ref[k].astype(F32))
        o_ref[...] = acc

    return pl.pallas_call(
        kern, name=name, out_shape=jax.ShapeDtypeStruct((2 * r2, cols), F32),
        grid_spec=_scalar_spec((nt,), [pl.BlockSpec((None, tr, cols), lambda i, p: (p[1], i, 0)),
                                       pl.BlockSpec((N_CHIPS, tr, cols), lambda i, p: (0, i, 0))],
                               pl.BlockSpec((tr, cols), lambda i, p: (p[0] * nt + i, 0))),
        compiler_params=_cparams("parallel"),
    )(pos, hs, land)


class _ReduceScatter:
    def __init__(self, grads, pos, tag):
        self.grads, self.pos, self.tag = list(grads), pos, tag

    def swap_job(self):
        return _pair_swap_job(self.grads)

    def after_swap(self, land):
        self.hs = [_sum_pair(g, l, self.pos, f"grads_pair_sum_{self.tag}{w}") for w, (g, l) in enumerate(zip(self.grads, land))]

    def exchange_job(self):
        return _chip_exchange_job(self.hs)

    def after_exchange(self, land2):
        ts = [_sum_chips(h, l, self.pos, f"grads_chip_sum_{self.tag}{w}") for w, (h, l) in enumerate(zip(self.hs, land2))]
        return _pair_share(ts, f"grads_pair_share_{self.tag}")

    def run(self):
        self.after_swap(_run_job(self.swap_job(), f"grads_pair_swap_{self.tag}"))
        return self.after_exchange(_run_job(self.exchange_job(), f"grads_chip_exchange_{self.tag}"))


EARLY_WEIGHTS = ("ada0", "attn_qkv")
LATE_WEIGHTS = ("ada1", "ffn_in0", "ffn_in1", "ffn_out0", "ffn_out1", "attn_o", "ret_qkvg", "ret_o")
LAYER1_GRADS = ("ffn_out1", "ffn_in1", "ret_o", "ret_qkvg", "ada1")
LAYER0_FFN_GRADS = ("ffn_out0", "ffn_in0")
LAST_GRADS = ("attn_o", "attn_qkv", "ada0")


def _fill_weights(wts, full):
    for name, w in full.items():
        if name[:-1] in ("ada", "ffn_in"):
            wts[name[:-1]][int(name[-1])] = w
        elif name[:-1] == "ffn_out":
            wts["ffn_out"][int(name[-1])] = w.reshape(-1, w.shape[2])
        elif name in ("attn_o", "ret_o"):
            wts[name] = w.reshape(-1, w.shape[2])
        elif name == "attn_qkv":
            wts[name] = w.transpose(1, 0, 2).reshape(w.shape[1], -1)
        else:
            wts[name] = w


class _StepPlan:
    def __init__(self, placed, pos):
        self.placed, self.pos = placed, pos
        self.layer1 = self.layer0_ffn = None
        self.reduced = {}

    def gather_job(self):
        return _gather_job([self.placed[k] for k in LATE_WEIGHTS])

    def late_weights(self, outs, wts):
        _fill_weights(wts, dict(zip(LATE_WEIGHTS, outs)))

    def start_layer1(self, big):
        self.layer1 = _ReduceScatter([big[k] for k in LAYER1_GRADS], self.pos, "l1_")

    def start_layer0_ffn(self, big):
        self.layer0_ffn = _ReduceScatter([big[k] for k in LAYER0_FFN_GRADS], self.pos, "l0f_")

    def exchange_job(self):
        return _merge_jobs(self.layer1.exchange_job(), self.layer0_ffn.exchange_job())

    def after_exchange(self, land):
        n1 = len(LAYER1_GRADS)
        self.reduced.update(zip(LAYER1_GRADS, self.layer1.after_exchange(land[:n1])))
        self.reduced.update(zip(LAYER0_FFN_GRADS, self.layer0_ffn.after_exchange(land[n1:])))


def _all_reduce_small(v, name):
    def body(v_ref, o_ref, land_ref, send_sems, recv_sems):
        x, y, c = _mesh_pos()
        me = 4 * x + 2 * y + c
        land_ref[me] = v_ref[...]
        for t in range(N_DEVICES):
            @pl.when(t != me)
            def _(t=t):
                _remote(v_ref, land_ref.at[me], send_sems.at[t], recv_sems.at[me], (t // 4, (t // 2) % 2, t % 2)).start()
        for t in range(N_DEVICES):
            @pl.when(t != me)
            def _(t=t):
                _remote(v_ref, land_ref.at[t], send_sems.at[t], recv_sems.at[t], (t // 4, (t // 2) % 2, t % 2)).wait()
        acc = land_ref[0]
        for t in range(1, N_DEVICES):
            acc = acc + land_ref[t]
        o_ref[...] = acc

    vmem = pl.BlockSpec(memory_space=pltpu.VMEM)
    return pl.pallas_call(
        body, name=name, in_specs=[vmem], out_specs=vmem, out_shape=jax.ShapeDtypeStruct(v.shape, F32),
        scratch_shapes=[pltpu.VMEM((N_DEVICES,) + v.shape, F32), pltpu.SemaphoreType.DMA((N_DEVICES,)),
                        pltpu.SemaphoreType.DMA((N_DEVICES,))],
    )(v)


SMALL_ROWS = 24


def _pack_small(small, dlogit):
    d = D_MODEL
    misc = jnp.zeros((d,), F32)
    misc = misc.at[0:HEAD_DIM].set(small["q_norm"]).at[128:128 + HEAD_DIM].set(small["k_norm"])
    misc = misc.at[256:256 + N_HEADS].set(small["sink"]).at[384:384 + 2 * RET_HEADS].set(dlogit.reshape(-1))
    rows = [small["ada_b0"].reshape(6, d), small["ada_b1"].reshape(6, d), small["norm1_g0"][None], small["norm1_g1"][None],
            small["norm2_g0"][None], small["norm2_g1"][None], small["c_ctx"][None], small["gn_g"].reshape(2, d), misc[None]]
    buf = jnp.concatenate(rows, axis=0)
    return jnp.concatenate([buf, jnp.zeros((SMALL_ROWS - buf.shape[0], d), F32)], axis=0)


def _unpack_small(buf):
    d = D_MODEL
    misc = buf[19]
    return dict(ada_b=buf[0:12].reshape(2, 6 * d), norm1_g=buf[12:14], norm2_g=buf[14:16], c_ctx=buf[16],
                gn_g=buf[17:19].reshape(2 * d), q_norm=misc[0:HEAD_DIM], k_norm=misc[128:128 + HEAD_DIM],
                sink=misc[256:256 + N_HEADS], decay=misc[384:384 + 2 * RET_HEADS])


def kernel(x, c, ctx, c_ctx, ada_w, ada_b, norm1_g, norm2_g, ffn_w_in, ffn_w_out, attn_w_qkv, attn_q_norm, attn_k_norm, attn_sink, attn_w_o, ret_w_qkvg, ret_decay_logit, ret_gn_g, ret_w_o, loss_target, m_c_ctx, m_ada_w, m_ada_b, m_norm1_g, m_norm2_g, m_ffn_w_in, m_ffn_w_out, m_attn_w_qkv, m_attn_q_norm, m_attn_k_norm, m_attn_sink, m_attn_w_o, m_ret_w_qkvg, m_ret_decay_logit, m_ret_gn_g, m_ret_w_o, v_c_ctx, v_ada_w, v_ada_b, v_norm1_g, v_norm2_g, v_ffn_w_in, v_ffn_w_out, v_attn_w_qkv, v_attn_q_norm, v_attn_k_norm, v_attn_sink, v_attn_w_o, v_ret_w_qkvg, v_ret_decay_logit, v_ret_gn_g, v_ret_w_o):
    xi, yi, ci = _mesh_pos()
    chip = 2 * xi + yi
    nb, s, d = x.shape
    gn_shard = ret_gn_g.shape[1]

    shards = dict(ada0=(ada_w, 0), ada1=(ada_w, 1), ffn_in0=(ffn_w_in, 0), ffn_in1=(ffn_w_in, 1), ffn_out0=(ffn_w_out, 0),
                  ffn_out1=(ffn_w_out, 1), attn_qkv=(attn_w_qkv, 0), attn_o=(attn_w_o, 0), ret_qkvg=(ret_w_qkvg, 0), ret_o=(ret_w_o, 0))
    names = list(shards)
    pos = jnp.stack([ci, chip]).astype(jnp.int32)
    placed = {k: _place_shard(*shards[k], pos, f"place_{k}") for k in names}
    early = _run_job(_gather_job([placed[k] for k in EARLY_WEIGHTS]), "gather_early_weights")
    gn_mine = jnp.where(ci == 0, ret_gn_g[0], jnp.zeros_like(ret_gn_g[0]))
    gn_place = lax.dynamic_update_slice(jnp.zeros((RET_VWIDTH,), F32), gn_mine, (chip * gn_shard,))
    gn_full = _all_reduce_small(gn_place.reshape(2, d), "gather_gn_gain").reshape(RET_VWIDTH)

    wts = dict(ada=[None, None], ffn_in=[None, None], ffn_out=[None, None], attn_qkv=None, attn_o=None, ret_qkvg=None, ret_o=None)
    _fill_weights(wts, dict(zip(EARLY_WEIGHTS, early)))
    plan = _StepPlan(placed, pos)
    decay_logit = ret_decay_logit[0]
    sp = dict(c_ctx=c_ctx, ada_b=ada_b, norm1_g=norm1_g, norm2_g=norm2_g, q_norm=attn_q_norm[0], k_norm=attn_k_norm[0],
              sink=attn_sink[0], log_g=jax.nn.log_sigmoid(decay_logit), gn_g=gn_full)
    loss_part, dz, big, small = _local_step(x, c, ctx, loss_target, sp, wts, plan)

    loss = lax.psum(loss_part[0, 0], ("x", "y", "c"))
    grad_x = dz.reshape(nb, s, d)

    dlogit = small["log_g"] * jax.nn.sigmoid(-decay_logit)
    sg = _unpack_small(_all_reduce_small(_pack_small(small, dlogit), "reduce_small_grads"))
    reduced = dict(plan.reduced)
    reduced.update(zip(LAST_GRADS, _ReduceScatter([big[k] for k in LAST_GRADS], pos, "last_").run()))

    grads = dict(
        c_ctx=sg["c_ctx"], ada_w=jnp.stack([reduced["ada0"], reduced["ada1"]]), ada_b=sg["ada_b"], norm1_g=sg["norm1_g"],
        norm2_g=sg["norm2_g"], ffn_w_in=jnp.stack([reduced["ffn_in0"], reduced["ffn_in1"]]),
        ffn_w_out=jnp.stack([reduced["ffn_out0"], reduced["ffn_out1"]]), attn_w_qkv=reduced["attn_qkv"][None],
        attn_q_norm=sg["q_norm"][None], attn_k_norm=sg["k_norm"][None], attn_sink=sg["sink"][None],
        attn_w_o=reduced["attn_o"][None], ret_w_qkvg=reduced["ret_qkvg"][None], ret_decay_logit=sg["decay"].reshape(1, 2, RET_HEADS),
        ret_gn_g=lax.dynamic_slice(sg["gn_g"], (chip * gn_shard,), (gn_shard,))[None], ret_w_o=reduced["ret_o"][None])
    params = dict(c_ctx=(c_ctx, m_c_ctx, v_c_ctx), ada_w=(ada_w, m_ada_w, v_ada_w), ada_b=(ada_b, m_ada_b, v_ada_b),
                  norm1_g=(norm1_g, m_norm1_g, v_norm1_g), norm2_g=(norm2_g, m_norm2_g, v_norm2_g),
                  ffn_w_in=(ffn_w_in, m_ffn_w_in, v_ffn_w_in), ffn_w_out=(ffn_w_out, m_ffn_w_out, v_ffn_w_out),
                  attn_w_qkv=(attn_w_qkv, m_attn_w_qkv, v_attn_w_qkv), attn_q_norm=(attn_q_norm, m_attn_q_norm, v_attn_q_norm),
                  attn_k_norm=(attn_k_norm, m_attn_k_norm, v_attn_k_norm), attn_sink=(attn_sink, m_attn_sink, v_attn_sink),
                  attn_w_o=(attn_w_o, m_attn_w_o, v_attn_w_o), ret_w_qkvg=(ret_w_qkvg, m_ret_w_qkvg, v_ret_w_qkvg),
                  ret_decay_logit=(ret_decay_logit, m_ret_decay_logit, v_ret_decay_logit),
                  ret_gn_g=(ret_gn_g, m_ret_gn_g, v_ret_gn_g), ret_w_o=(ret_w_o, m_ret_w_o, v_ret_w_o))
    order = list(params)
    deltas, new_m, new_v = [], [], []
    for k in order:
        w, m, v = params[k]
        g = grads[k].reshape(w.shape)
        grads[k] = g
        flat = (-1, w.shape[-1]) if w.ndim > 1 else (1, -1)
        if k == "ret_decay_logit":
            flat = (1, -1)
        dw, nm, nv = _adamw(w.reshape(flat), g.reshape(flat), m.reshape(flat), v.reshape(flat), f"adamw_{k}")
        deltas.append(dw.reshape(w.shape))
        new_m.append(nm.reshape(w.shape))
        new_v.append(nv.reshape(w.shape))
    return (loss, grad_x, *[grads[k] for k in order], *deltas, *new_m, *new_v)
```

```python
import functools

import jax
import jax.numpy as jnp
from jax import lax
from jax.experimental import pallas as pl
from jax.experimental.pallas import tpu as pltpu

F32 = jnp.float32
BF16 = jnp.bfloat16

D_MODEL = 1024
N_HEADS = 16
N_KV_HEADS = 4
HEAD_DIM = 64
WINDOW = 128
ATTN_BLOCK = 128
BAND = ATTN_BLOCK + 2 * WINDOW
RET_HEADS = 4
RET_QK_DIM = 256
RET_V_DIM = 512
RET_VWIDTH = 2048
RET_CHUNK = 128
D_FF = 2816
GRID_W = 64
ROPE_BASE = 10000.0
EPS = 1e-6
NEG_INF = -1e30
LANES = 128

ADAM_LR = 0.001
ADAM_B1 = 0.9
ADAM_B2 = 0.999
ADAM_EPS = 1e-08
ADAM_WD = 0.01
ADAM_STEP = 10

VMEM_LIMIT_BYTES = 56 * 1024 * 1024
MESH = pl.DeviceIdType.MESH
N_CHIPS = 4


def _cparams(*sem):
    return pltpu.CompilerParams(dimension_semantics=sem, vmem_limit_bytes=VMEM_LIMIT_BYTES)


_DIMS = {"nn": ((1,), (0,)), "nt": ((1,), (1,)), "tn": ((0,), (0,))}


def _dot(a, b, form):
    return lax.dot_general(a.astype(BF16), b.astype(BF16), (_DIMS[form], ((), ())), preferred_element_type=F32)


@functools.partial(jax.custom_vjp, nondiff_argnums=(2,))
def _mm(a, b, form):
    return _dot(a, b, form)


def _mm_fwd(a, b, form):
    return _dot(a, b, form), (a, b)


def _mm_bwd(form, res, ct):
    a, b = res
    if form == "nn":
        da, db = _dot(ct, b, "nt"), _dot(a, ct, "tn")
    elif form == "nt":
        da, db = _dot(ct, b, "nn"), _dot(ct, a, "tn")
    else:
        da, db = _dot(b, ct, "nt"), _dot(a, ct, "nn")
    return da.astype(a.dtype), db.astype(b.dtype)


_mm.defvjp(_mm_fwd, _mm_bwd)


def _swap_halves(x, half):
    w = x.shape[-1]
    lane = lax.broadcasted_iota(jnp.int32, x.shape, x.ndim - 1)
    return jnp.where(lane % (2 * half) < half, pltpu.roll(x, w - half, x.ndim - 1), pltpu.roll(x, half, x.ndim - 1))


@functools.partial(jax.custom_vjp, nondiff_argnums=(1,))
def _rot(x, half):
    return _swap_halves(x, half)


def _rot_fwd(x, half):
    return _swap_halves(x, half), None


def _rot_bwd(half, _, ct):
    return (_swap_halves(ct, half),)


_rot.defvjp(_rot_fwd, _rot_bwd)


def _rope(x, cos, sin_signed, half):
    return x * cos + _rot(x, half) * sin_signed


def _head_mean_square(x):
    r = lax.broadcasted_iota(jnp.int32, (LANES, LANES), 0) // HEAD_DIM
    c = lax.broadcasted_iota(jnp.int32, (LANES, LANES), 1) // HEAD_DIM
    g = jnp.where(r == c, 1.0 / HEAD_DIM, 0.0).astype(F32)
    return jnp.dot(x * x, g, precision=lax.Precision.HIGHEST, preferred_element_type=F32)


def _qk_chunk(x, gain, cos, sin_signed, scale):
    y = x * lax.rsqrt(_head_mean_square(x) + EPS) * gain
    return _rope(y, cos, sin_signed, HEAD_DIM // 4) * scale


def _sigmoid(x):
    return 1.0 / (1.0 + jnp.exp(-x))


def _silu(x):
    return x * _sigmoid(x)


def _mm_nn(a, w, out_dtype, name, tm, tn, tk, bias=None):
    m, k_dim = a.shape
    if w.ndim == 3:
        n = w.shape[0] * w.shape[2]
        per = w.shape[2] // tn
        assert w.shape[2] % tn == 0
        w_spec = pl.BlockSpec((None, tk, tn), lambda i, j, k: (j // per, k, j % per))
    else:
        n = w.shape[1]
        w_spec = pl.BlockSpec((tk, tn), lambda i, j, k: (k, j))
    assert m % tm == 0 and n % tn == 0 and k_dim % tk == 0, (name, a.shape, w.shape, tm, tn, tk)
    nk = k_dim // tk
    has_bias = bias is not None

    def body(*refs):
        a_ref, w_ref = refs[0], refs[1]
        b_ref = refs[2] if has_bias else None
        o_ref, acc_ref = (refs[-1], None) if nk == 1 else (refs[-2], refs[-1])
        if nk == 1:
            part = jnp.dot(a_ref[...].astype(BF16), w_ref[...].astype(BF16), preferred_element_type=F32)
            o_ref[...] = (part + b_ref[...] if has_bias else part).astype(out_dtype)
            return
        k = pl.program_id(2)

        @pl.when(k == 0)
        def _():
            acc_ref[...] = jnp.zeros_like(acc_ref)

        acc_ref[...] += jnp.dot(a_ref[...].astype(BF16), w_ref[...].astype(BF16), preferred_element_type=F32)

        @pl.when(k == nk - 1)
        def _():
            r = acc_ref[...]
            if has_bias:
                r = r + b_ref[...]
            o_ref[...] = r.astype(out_dtype)

    in_specs = [pl.BlockSpec((tm, tk), lambda i, j, k: (i, k)), w_spec]
    args = [a, w]
    if has_bias:
        in_specs.append(pl.BlockSpec((1, tn), lambda i, j, k: (0, j)))
        args.append(bias)
    return pl.pallas_call(
        body, name=name, grid=(m // tm, n // tn, nk), in_specs=in_specs,
        out_specs=pl.BlockSpec((tm, tn), lambda i, j, k: (i, j)),
        out_shape=jax.ShapeDtypeStruct((m, n), out_dtype),
        scratch_shapes=[pltpu.VMEM((tm, tn), F32)] if nk > 1 else [],
        compiler_params=_cparams("parallel", "parallel", "arbitrary"),
    )(*args)


def _mm_nt(a, w, out_dtype, name, tm, tn, tk):
    if a.ndim == 3:
        planes, m, plane_w = a.shape
        c_dim = planes * plane_w
        a_per = plane_w // tk
        assert plane_w % tk == 0
        a_spec = pl.BlockSpec((None, tm, tk), lambda i, j, k: (k // a_per, i, k % a_per))
    else:
        m, c_dim = a.shape
        a_spec = pl.BlockSpec((tm, tk), lambda i, j, k: (i, k))
    if w.ndim == 3:
        k_out = w.shape[1]
        per = w.shape[2] // tk
        assert w.shape[2] % tk == 0 and w.shape[0] * w.shape[2] == c_dim
        w_spec = pl.BlockSpec((None, tn, tk), lambda i, j, k: (k // per, j, k % per))
    else:
        k_out = w.shape[0]
        assert w.shape[1] == c_dim
        w_spec = pl.BlockSpec((tn, tk), lambda i, j, k: (j, k))
    assert m % tm == 0 and k_out % tn == 0 and c_dim % tk == 0, (name, a.shape, w.shape, tm, tn, tk)
    nk = c_dim // tk

    def body(a_ref, w_ref, o_ref, acc_ref=None):
        if nk == 1:
            o_ref[...] = _dot(a_ref[...], w_ref[...], "nt").astype(out_dtype)
            return
        k = pl.program_id(2)

        @pl.when(k == 0)
        def _():
            acc_ref[...] = jnp.zeros_like(acc_ref)

        acc_ref[...] += _dot(a_ref[...], w_ref[...], "nt")

        @pl.when(k == nk - 1)
        def _():
            o_ref[...] = acc_ref[...].astype(out_dtype)

    return pl.pallas_call(
        body, name=name, grid=(m // tm, k_out // tn, nk),
        in_specs=[a_spec, w_spec],
        out_specs=pl.BlockSpec((tm, tn), lambda i, j, k: (i, j)),
        out_shape=jax.ShapeDtypeStruct((m, k_out), out_dtype),
        scratch_shapes=[pltpu.VMEM((tm, tn), F32)] if nk > 1 else [],
        compiler_params=_cparams("parallel", "parallel", "arbitrary"),
    )(a, w)


def _mm_tn(a, b, name, tm, tn, tk, shards=None):
    r, k_dim = a.shape
    if b.ndim == 3:
        n = b.shape[0] * b.shape[2]
        b_per = b.shape[2] // tn
        assert b.shape[2] % tn == 0
        b_spec = pl.BlockSpec((None, tk, tn), lambda i, j, k: (j // b_per, k, j % b_per))
    else:
        n = b.shape[1]
        b_spec = pl.BlockSpec((tk, tn), lambda i, j, k: (k, j))
    assert r % tk == 0 and k_dim % tm == 0 and n % tn == 0, (name, a.shape, b.shape, tm, tn, tk)
    nk = r // tk
    if shards:
        per = n // shards // tn
        assert n % (shards * tn) == 0
        out_shape = jax.ShapeDtypeStruct((shards, k_dim, n // shards), F32)
        out_spec = pl.BlockSpec((None, tm, tn), lambda i, j, k: (j // per, i, j % per))
    else:
        out_shape = jax.ShapeDtypeStruct((k_dim, n), F32)
        out_spec = pl.BlockSpec((tm, tn), lambda i, j, k: (i, j))

    def body(a_ref, b_ref, o_ref):
        k = pl.program_id(2)

        @pl.when(k == 0)
        def _():
            o_ref[...] = jnp.zeros_like(o_ref)

        o_ref[...] += _dot(a_ref[...], b_ref[...], "tn")

    return pl.pallas_call(
        body, name=name, grid=(k_dim // tm, n // tn, nk),
        in_specs=[pl.BlockSpec((tk, tm), lambda i, j, k: (k, i)), b_spec],
        out_specs=out_spec, out_shape=out_shape,
        compiler_params=_cparams("parallel", "parallel", "arbitrary"),
    )(a, b)


class _Carrier:
    def __init__(self, job, n_in, n_out, n_scratch):
        self.job, self.n_in, self.n_out, self.n_scratch = job, n_in, n_out, n_scratch
        self.ji = len(job.inputs) if job else 0
        self.jo = len(job.out_shapes) if job else 0

    def operands(self):
        return list(self.job.inputs) if self.job else []

    def in_specs(self):
        return [pl.BlockSpec(memory_space=pl.ANY)] * self.ji

    def out_specs(self):
        return [pl.BlockSpec(memory_space=pl.ANY)] * self.jo

    def out_shapes(self):
        return list(self.job.out_shapes) if self.job else []

    def scratch(self):
        return list(self.job.sem_shapes) if self.job else []

    def aliases(self):
        return {self.n_in + a: self.n_out + b for a, b in self.job.aliases.items()} if self.job else {}

    def split(self, refs):
        a = self.n_in
        b = a + self.ji
        c = b + self.n_out
        d = c + self.jo
        e = d + self.n_scratch
        return list(refs[:a]) + list(refs[b:c]) + list(refs[d:e]), (refs[a:b], refs[c:d], refs[e:])

    def run(self, job_refs, step, steps):
        if not self.job:
            return
        for stage, mark in zip(self.job.stages, _job_marks(self.job, steps)):
            pl.when(step == mark)(functools.partial(stage, *job_refs))

    def results(self, res):
        res = list(res)
        return res[:self.n_out], res[self.n_out:]


FFN_ROW_TILE = 768


def _ffn_tile(r):
    return FFN_ROW_TILE if r % FFN_ROW_TILE == 0 else _row_tile(r)


def _ffn_in_swiglu(h, w, name):
    r, k_dim = h.shape
    n4 = w.shape[2]
    tm = _ffn_tile(r)

    def body(h_ref, wg_ref, wu_ref, u_ref, a_ref):
        hv = h_ref[...]
        g = jnp.dot(hv, wg_ref[...], preferred_element_type=F32)
        up = jnp.dot(hv, wu_ref[...], preferred_element_type=F32)
        u_ref[0] = g.astype(BF16)
        u_ref[1] = up.astype(BF16)
        a_ref[...] = (_silu(g) * up).astype(BF16)

    return pl.pallas_call(
        body, name=name, grid=(r // tm, 2),
        in_specs=[pl.BlockSpec((tm, k_dim), lambda i, j: (i, 0)),
                  pl.BlockSpec((None, k_dim, n4), lambda i, j: (j, 0, 0)),
                  pl.BlockSpec((None, k_dim, n4), lambda i, j: (j + 2, 0, 0))],
        out_specs=[pl.BlockSpec((2, tm, n4), lambda i, j: (0, i, j)), pl.BlockSpec((tm, n4), lambda i, j: (i, j))],
        out_shape=[jax.ShapeDtypeStruct((2, r, 2 * n4), BF16), jax.ShapeDtypeStruct((r, 2 * n4), BF16)],
        compiler_params=_cparams("parallel", "parallel"),
    )(h, w, w)


def _mm_nn_gate_residual(geo, a, w, z, mod, off, name, norm=None):
    r, k_dim = a.shape
    n = w.shape[1]
    tm = FFN_ROW_TILE if geo.seg % FFN_ROW_TILE == 0 else 256
    tiles = geo.seg // tm
    assert geo.seg % tm == 0 and r == geo.r and n == D_MODEL

    def body(a_ref, w_ref, z_ref, mx_ref, mc_ref, *rest):
        out = jnp.dot(a_ref[...], w_ref[...], preferred_element_type=F32)
        is_x = (pl.program_id(0) % tiles) * tm + lax.broadcasted_iota(jnp.int32, (tm, 1), 0) < geo.s
        zo = z_ref[...] + jnp.where(is_x, mx_ref[:, off:off + n], mc_ref[:, off:off + n]) * out
        if norm:
            g_ref, nx_ref, nc_ref, zo_ref, raw_ref, h_ref = rest
            no = norm[2]
            shift = jnp.where(is_x, nx_ref[:, no:no + n], nc_ref[:, no:no + n])
            scale = jnp.where(is_x, nx_ref[:, no + n:no + 2 * n], nc_ref[:, no + n:no + 2 * n])
            rs = lax.rsqrt(jnp.mean(zo * zo, axis=-1, keepdims=True) + EPS)
            h_ref[...] = ((zo * rs) * g_ref[...] * (1.0 + scale) + shift).astype(BF16)
        else:
            zo_ref, raw_ref = rest
        zo_ref[...] = zo
        raw_ref[...] = out.astype(BF16)

    def mod_specs(m):
        return [pl.BlockSpec((None, 1, m.shape[2]), lambda i: (i // tiles, 0, 0)), pl.BlockSpec((None, 1, m.shape[2]), lambda i: (geo.b, 0, 0))]

    row = pl.BlockSpec((tm, n), lambda i: (i, 0))
    in_specs = [pl.BlockSpec((tm, k_dim), lambda i: (i, 0)), pl.BlockSpec((k_dim, n), lambda i: (0, 0)), row] + mod_specs(mod)
    args = [a, w, z, mod, mod]
    out_specs, out_shape = [row, row], [jax.ShapeDtypeStruct((r, n), F32), jax.ShapeDtypeStruct((r, n), BF16)]
    if norm:
        in_specs += [pl.BlockSpec((1, n), lambda i: (0, 0))] + mod_specs(norm[1])
        args += [norm[0], norm[1], norm[1]]
        out_specs.append(row)
        out_shape.append(jax.ShapeDtypeStruct((r, n), BF16))
    res = pl.pallas_call(body, name=name, grid=(r // tm,), in_specs=in_specs, out_specs=out_specs, out_shape=out_shape,
                         compiler_params=_cparams("parallel"))(*args)
    return res if norm else (*res, None)


def _ffn_out_dx_swiglu_bwd(df, w_out, u, name, job=None):
    r, d = df.shape
    n4 = u.shape[2] // 2
    tm = _ffn_tile(r)
    carrier = _Carrier(job, 3, 1, 0)
    steps = (r // tm) * 2

    def body(*refs):
        (df_ref, w_ref, u_ref, du_ref), job_refs = carrier.split(refs)
        carrier.run(job_refs, pl.program_id(0) * 2 + pl.program_id(1), steps)
        da = _dot(df_ref[...], w_ref[...], "nt")
        g, up = u_ref[0].astype(F32), u_ref[1].astype(F32)
        s = _sigmoid(g)
        du_ref[0] = (da * up * (s * (1.0 + g * (1.0 - s)))).astype(BF16)
        du_ref[1] = (da * (g * s)).astype(BF16)

    res = pl.pallas_call(
        body, name=name, grid=(r // tm, 2),
        in_specs=[pl.BlockSpec((tm, d), lambda i, j: (i, 0)), pl.BlockSpec((n4, d), lambda i, j: (j, 0)),
                  pl.BlockSpec((2, tm, n4), lambda i, j: (0, i, j))] + carrier.in_specs(),
        out_specs=[pl.BlockSpec((2, tm, n4), lambda i, j: (0, i, j))] + carrier.out_specs(),
        out_shape=[jax.ShapeDtypeStruct(u.shape, BF16)] + carrier.out_shapes(),
        scratch_shapes=carrier.scratch(), input_output_aliases=carrier.aliases(),
        compiler_params=_cparams("arbitrary", "arbitrary"),
    )(df, w_out, u, *carrier.operands())
    (du,), extra = carrier.results(res)
    return du, extra


class _Rows:
    def __init__(self, b, s, l):
        self.b, self.s, self.l = b, s, l
        self.seg = s + l
        self.r = b * self.seg


def _rowwise(name, body, geo, tm, ins, outs, job=None):
    seg_blocks, x_blocks = geo.seg // tm, geo.s // tm
    assert geo.seg % tm == 0 and geo.s % tm == 0
    nb = geo.b

    def is_ctx(i):
        return i % seg_blocks >= x_blocks

    in_specs, args = [], []
    for arr, kind in ins:
        args.append(arr)
        if kind == "row":
            in_specs.append(pl.BlockSpec((tm, arr.shape[1]), lambda i: (i, 0)))
        elif kind == "ex":
            in_specs.append(pl.BlockSpec((None, 1, arr.shape[2]), lambda i: (jnp.where(is_ctx(i), nb, i // seg_blocks), 0, 0)))
        elif kind == "full":
            in_specs.append(pl.BlockSpec(arr.shape, lambda i, nd=arr.ndim: (0,) * nd))
        elif kind == "tab":
            in_specs.append(pl.BlockSpec((tm, arr.shape[1]), lambda i: (i % seg_blocks, 0)))
        elif kind == "xrow":
            in_specs.append(pl.BlockSpec(
                (tm, arr.shape[1]), lambda i: ((i // seg_blocks) * x_blocks + jnp.minimum(i % seg_blocks, x_blocks - 1), 0)))
        else:
            _, width, cb = kind
            in_specs.append(pl.BlockSpec((tm, width), lambda i, cb=cb: (i, cb)))
    out_specs, out_shapes = [], []
    for o in outs:
        if o[0] == "row":
            out_specs.append(pl.BlockSpec((tm, o[1]), lambda i: (i, 0)))
            out_shapes.append(jax.ShapeDtypeStruct((geo.r, o[1]), o[2]))
        elif o[0] == "xrow":
            out_specs.append(pl.BlockSpec(
                (tm, o[1]), lambda i: ((i // seg_blocks) * x_blocks + jnp.minimum(i % seg_blocks, x_blocks - 1), 0)))
            out_shapes.append(jax.ShapeDtypeStruct((geo.b * geo.s, o[1]), o[2]))
        elif o[0] == "exacc":
            out_specs.append(pl.BlockSpec((None, 1, o[1]), lambda i: (jnp.where(is_ctx(i), nb, 0) + i // seg_blocks, 0, 0)))
            out_shapes.append(jax.ShapeDtypeStruct((2 * nb, 1, o[1]), F32))
        else:
            out_specs.append(pl.BlockSpec((o[1], o[2]), lambda i: (0, 0)))
            out_shapes.append(jax.ShapeDtypeStruct((o[1], o[2]), F32))
    n_in = len(ins)
    carrier = _Carrier(job, n_in, len(outs), 0)

    def kern(*refs):
        i = pl.program_id(0)
        refs, job_refs = carrier.split(refs)
        carrier.run(job_refs, i, geo.r // tm)
        res = body(i, *[r[...].astype(F32) for r in refs[:n_in]])
        if not isinstance(res, (tuple, list)):
            res = (res,)
        jj = i % seg_blocks
        first_of_part = (jj == 0) | (jj == x_blocks)
        for o, ref, val in zip(outs, refs[n_in:], res):
            if o[0] == "row":
                ref[...] = val.astype(ref.dtype)
            elif o[0] == "xrow":
                @pl.when(jj < x_blocks)
                def _(ref=ref, val=val):
                    ref[...] = val.astype(ref.dtype)
            else:
                first = first_of_part if o[0] == "exacc" else i == 0

                @pl.when(first)
                def _(ref=ref, val=val):
                    ref[...] = val

                @pl.when(jnp.logical_not(first))
                def _(ref=ref, val=val):
                    ref[...] += val

    res = pl.pallas_call(
        kern, name=name, grid=(geo.r // tm,), in_specs=in_specs + carrier.in_specs(), out_specs=out_specs + carrier.out_specs(),
        out_shape=out_shapes + carrier.out_shapes(), scratch_shapes=carrier.scratch(), input_output_aliases=carrier.aliases(),
        compiler_params=_cparams("arbitrary"),
    )(*args, *carrier.operands())
    own, extra = carrier.results(res)
    if job:
        return (*own, extra)
    return own[0] if len(own) == 1 else own


def _colsum(v):
    return jnp.sum(v, axis=0, keepdims=True)


def _norm_mod(geo, z, gain, mod, off, name):
    d = D_MODEL

    def body(i, zv, g, m):
        r = lax.rsqrt(jnp.mean(zv * zv, axis=-1, keepdims=True) + EPS)
        return (zv * r) * g * (1.0 + m[:, off + d:off + 2 * d]) + m[:, off:off + d]

    return _rowwise(name, body, geo, 256, [(z, "row"), (gain, "full"), (mod, "ex")], [("row", d, BF16)])


def _norm_mod_bwd(geo, z, gain, mod, off, dh, dz_skip, name, gated=None, latent_only=False, job=None):
    d = D_MODEL

    def body(i, zv, g, m, dhv, skip, *rest):
        r = lax.rsqrt(jnp.mean(zv * zv, axis=-1, keepdims=True) + EPS)
        n = zv * r
        dng = dhv * (1.0 + m[:, off + d:off + 2 * d])
        dn = dng * g
        dz = r * (dn - n * jnp.mean(dn * n, axis=-1, keepdims=True)) + skip
        res = (dz, _colsum(dhv), _colsum(dhv * (n * g)), _colsum(dng * n))
        if gated:
            ov, gm = rest
            res += (dz * gm[:, gated[2]:gated[2] + d], _colsum(dz * ov))
        return res

    ins = [(z, "row"), (gain, "full"), (mod, "ex"), (dh, "row"), (dz_skip, "row")]
    outs = [("xrow" if latent_only else "row", d, F32), ("exacc", d), ("exacc", d), ("gacc", 1, d)]
    if gated:
        ins += [(gated[0], "row"), (gated[1], "ex")]
        outs += [("row", d, BF16), ("exacc", d)]
    return _rowwise(name, body, geo, 256, ins, outs, job)


def _loss_head(geo, z, target, out, mod, off, name):
    seg_blocks, x_blocks = geo.seg // 256, geo.s // 256
    d = D_MODEL

    def body(i, zv, tv, ov, m):
        keep = jnp.where(i % seg_blocks >= x_blocks, 0.0, 1.0)
        err = (zv - tv) * keep
        part = 0.5 * jnp.sum(jnp.mean(err * err, axis=-1, keepdims=True), axis=0, keepdims=True)
        dz = err * (1.0 / d)
        return dz, jnp.broadcast_to(part, (1, LANES)), dz * m[:, off:off + d], _colsum(dz * ov)

    return _rowwise(name, body, geo, 256, [(z, "row"), (target, "xrow"), (out, "row"), (mod, "ex")],
                    [("row", d, F32), ("gacc", 1, LANES), ("row", d, BF16), ("exacc", d)])


Q_SCALE = HEAD_DIM ** -0.5
N_QK_CHUNKS = (N_HEADS + N_KV_HEADS) * HEAD_DIM // LANES
N_Q_CHUNKS = N_HEADS * HEAD_DIM // LANES


def _attn_prep(geo, proj, cos, sin_signed, q_gain, k_gain, name):
    def body(i, p, cs, sn, qg, kg):
        outs = []
        for ch in range(N_QK_CHUNKS):
            is_q = ch < N_Q_CHUNKS
            outs.append(_qk_chunk(p[:, ch * LANES:(ch + 1) * LANES], qg if is_q else kg, cs, sn, Q_SCALE if is_q else 1.0))
        outs.append(p[:, N_QK_CHUNKS * LANES:])
        return jnp.concatenate(outs, axis=1)

    return _rowwise(name, body, geo, 256, [(proj, "row"), (cos, "tab"), (sin_signed, "tab"), (q_gain, "full"), (k_gain, "full")],
                    [("row", proj.shape[1], BF16)])


def _attn_prep_bwd(geo, proj, cos, sin_signed, q_gain, k_gain, dq, dkv, name):
    kw = N_KV_HEADS * HEAD_DIM

    def body(i, p, cs, sn, qg, kg, dqv, dkvv):
        outs = []
        dgains = [jnp.zeros((1, LANES), F32), jnp.zeros((1, LANES), F32)]
        for ch in range(N_QK_CHUNKS):
            is_q = ch < N_Q_CHUNKS
            scale = Q_SCALE if is_q else 1.0
            ct = dqv[:, ch * LANES:(ch + 1) * LANES] if is_q else dkvv[:, (ch - N_Q_CHUNKS) * LANES:(ch - N_Q_CHUNKS + 1) * LANES]
            _, vjp = jax.vjp(lambda xx, gg, scale=scale: _qk_chunk(xx, gg, cs, sn, scale),
                             p[:, ch * LANES:(ch + 1) * LANES], qg if is_q else kg)
            dx, dg = vjp(ct)
            outs.append(dx)
            dgains[0 if is_q else 1] = dgains[0 if is_q else 1] + dg
        outs.append(dkvv[:, kw:])
        return jnp.concatenate(outs, axis=1), dgains[0], dgains[1]

    return _rowwise(name, body, geo, 256,
                    [(proj, "row"), (cos, "tab"), (sin_signed, "tab"), (q_gain, "full"), (k_gain, "full"), (dq, "row"), (dkv, "row")],
                    [("row", proj.shape[1], BF16), ("gacc", 1, LANES), ("gacc", 1, LANES)])


def _attn_geometry(geo):
    assert geo.s % ATTN_BLOCK == 0 and geo.l % ATTN_BLOCK == 0 and geo.seg >= BAND
    return geo.seg // ATTN_BLOCK, geo.s // ATTN_BLOCK


def _attn_mask(j, s0, geo):
    r = lax.broadcasted_iota(jnp.int32, (ATTN_BLOCK, BAND), 0)
    n = lax.broadcasted_iota(jnp.int32, (ATTN_BLOCK, BAND), 1)
    dist = (s0 - j * ATTN_BLOCK) + n - r
    return (jnp.abs(dist) <= WINDOW) & (s0 + n < geo.s)


def _attn_probs(q, keys, valid, n_ctx, sink):
    s = _dot(q, keys, "nt")
    if valid is not None:
        s = jnp.concatenate([s[:, :n_ctx], jnp.where(valid, s[:, n_ctx:], NEG_INF)], axis=1)
    m = jnp.maximum(jnp.max(s, axis=-1, keepdims=True), sink)
    e, e_sink = jnp.exp(s - m), jnp.exp(sink - m)
    inv = 1.0 / (jnp.sum(e, axis=-1, keepdims=True) + e_sink)
    return e * inv, e_sink * inv


def _attn_keys(ref, s0, geo, with_band):
    ctx = ref[geo.s:geo.seg, :]
    return jnp.concatenate([ctx, ref[pl.ds(s0, BAND), :]], axis=0) if with_band else ctx


def _attention(geo, qkv, sink, name, job=None):
    n_blocks, n_x_blocks = _attn_geometry(geo)
    qw, kw = N_HEADS * HEAD_DIM, N_KV_HEADS * HEAD_DIM
    group = N_HEADS // N_KV_HEADS
    carrier = _Carrier(job, 4, 1, 0)

    def kern(*refs):
        (sink_ref, q_ref, k_ref, v_ref, o_ref), job_refs = carrier.split(refs)
        j = pl.program_id(1)
        carrier.run(job_refs, pl.program_id(0) * n_blocks + j, geo.b * n_blocks)
        s0 = pl.multiple_of(jnp.clip((j - 1) * ATTN_BLOCK, 0, geo.seg - BAND), ATTN_BLOCK)

        def heads(with_band):
            valid = _attn_mask(j, s0, geo) if with_band else None
            k_all, v_all = _attn_keys(k_ref, s0, geo, with_band), _attn_keys(v_ref, s0, geo, with_band)
            for h in range(N_HEADS):
                kv = slice((h // group) * HEAD_DIM, (h // group + 1) * HEAD_DIM)
                p, _ = _attn_probs(q_ref[:, h * HEAD_DIM:(h + 1) * HEAD_DIM], k_all[:, kv], valid, geo.l, sink_ref[h])
                o_ref[:, h * HEAD_DIM:(h + 1) * HEAD_DIM] = _dot(p, v_all[:, kv], "nn").astype(BF16)

        pl.when(j < n_x_blocks)(lambda: heads(True))
        pl.when(j >= n_x_blocks)(lambda: heads(False))

    res = pl.pallas_call(
        kern, name=name, grid=(geo.b, n_blocks),
        in_specs=[pl.BlockSpec(memory_space=pltpu.SMEM),
                  pl.BlockSpec((ATTN_BLOCK, qw), lambda b, j: (b * n_blocks + j, 0)),
                  pl.BlockSpec((geo.seg, kw), lambda b, j: (b, qw // kw)),
                  pl.BlockSpec((geo.seg, kw), lambda b, j: (b, qw // kw + 1))] + carrier.in_specs(),
        out_specs=[pl.BlockSpec((ATTN_BLOCK, qw), lambda b, j: (b * n_blocks + j, 0))] + carrier.out_specs(),
        out_shape=[jax.ShapeDtypeStruct((geo.r, qw), BF16)] + carrier.out_shapes(),
        scratch_shapes=carrier.scratch(), input_output_aliases=carrier.aliases(),
        compiler_params=_cparams("arbitrary", "arbitrary"),
    )(sink, qkv, qkv, qkv, *carrier.operands())
    (o,), extra = carrier.results(res)
    return o, extra


def _attention_bwd(geo, qkv, sink, do, name, job=None):
    n_blocks, n_x_blocks = _attn_geometry(geo)
    qw, kw = N_HEADS * HEAD_DIM, N_KV_HEADS * HEAD_DIM
    group = N_HEADS // N_KV_HEADS

    carrier = _Carrier(job, 5, 3, 1)

    def kern(*refs):
        (sink_ref, q_ref, k_ref, v_ref, do_ref, dq_ref, dkv_out_ref, dsink_ref, dkv_ref), job_refs = carrier.split(refs)
        b, j = pl.program_id(0), pl.program_id(1)
        carrier.run(job_refs, b * n_blocks + j, geo.b * n_blocks)
        s0 = pl.multiple_of(jnp.clip((j - 1) * ATTN_BLOCK, 0, geo.seg - BAND), ATTN_BLOCK)

        @pl.when(j == 0)
        def _():
            dkv_ref[...] = jnp.zeros_like(dkv_ref)

        @pl.when((j == 0) & (b == 0))
        def _():
            dsink_ref[...] = jnp.zeros_like(dsink_ref)

        def heads(with_band):
            valid = _attn_mask(j, s0, geo) if with_band else None
            k_all, v_all = _attn_keys(k_ref, s0, geo, with_band), _attn_keys(v_ref, s0, geo, with_band)
            for g in range(N_KV_HEADS):
                kv = slice(g * HEAD_DIM, (g + 1) * HEAD_DIM)
                keys, vals = k_all[:, kv], v_all[:, kv]
                group_heads = [slice(h * HEAD_DIM, (h + 1) * HEAD_DIM) for h in range(g * group, (g + 1) * group)]
                ds_rows, p_rows = [], []
                for h, hs in zip(range(g * group, (g + 1) * group), group_heads):
                    dout = do_ref[:, hs]
                    p, p_sink = _attn_probs(q_ref[:, hs], keys, valid, geo.l, sink_ref[h])
                    dp = _dot(dout, vals, "nt")
                    dsum = jnp.sum(p * dp, axis=-1, keepdims=True)
                    ds = (p * (dp - dsum)).astype(BF16)
                    dq_ref[:, hs] = _dot(ds, keys, "nn").astype(BF16)
                    ds_rows.append(ds)
                    p_rows.append(p.astype(BF16))
                    dsink_ref[h:h + 1, :] += jnp.broadcast_to(-jnp.sum(p_sink * dsum, axis=0, keepdims=True), (1, LANES))
                q_rows = jnp.concatenate([q_ref[:, hs] for hs in group_heads], axis=0)
                do_rows = jnp.concatenate([do_ref[:, hs] for hs in group_heads], axis=0)
                dk = _dot(jnp.concatenate(ds_rows, axis=0), q_rows, "tn")
                dv = _dot(jnp.concatenate(p_rows, axis=0), do_rows, "tn")
                vv = slice(kw + g * HEAD_DIM, kw + (g + 1) * HEAD_DIM)
                dkv_ref[geo.s:geo.seg, kv] += dk[:geo.l]
                dkv_ref[geo.s:geo.seg, vv] += dv[:geo.l]
                if with_band:
                    dkv_ref[pl.ds(s0, BAND), kv] += dk[geo.l:]
                    dkv_ref[pl.ds(s0, BAND), vv] += dv[geo.l:]

        pl.when(j < n_x_blocks)(lambda: heads(True))
        pl.when(j >= n_x_blocks)(lambda: heads(False))

        @pl.when(j == n_blocks - 1)
        def _():
            dkv_out_ref[...] = dkv_ref[...].astype(BF16)

    res = pl.pallas_call(
        kern, name=name, grid=(geo.b, n_blocks),
        in_specs=[pl.BlockSpec(memory_space=pltpu.SMEM),
                  pl.BlockSpec((ATTN_BLOCK, qw), lambda b, j: (b * n_blocks + j, 0)),
                  pl.BlockSpec((geo.seg, kw), lambda b, j: (b, qw // kw)),
                  pl.BlockSpec((geo.seg, kw), lambda b, j: (b, qw // kw + 1)),
                  pl.BlockSpec((ATTN_BLOCK, qw), lambda b, j: (b * n_blocks + j, 0))] + carrier.in_specs(),
        out_specs=[pl.BlockSpec((ATTN_BLOCK, qw), lambda b, j: (b * n_blocks + j, 0)),
                   pl.BlockSpec((geo.seg, 2 * kw), lambda b, j: (b, 0)),
                   pl.BlockSpec((N_HEADS, LANES), lambda b, j: (0, 0))] + carrier.out_specs(),
        out_shape=[jax.ShapeDtypeStruct((geo.r, qw), BF16), jax.ShapeDtypeStruct((geo.r, 2 * kw), BF16),
                   jax.ShapeDtypeStruct((N_HEADS, LANES), F32)] + carrier.out_shapes(),
        scratch_shapes=[pltpu.VMEM((geo.seg, 2 * kw), F32)] + carrier.scratch(), input_output_aliases=carrier.aliases(),
        compiler_params=_cparams("arbitrary", "arbitrary"),
    )(sink, qkv, qkv, qkv, do, *carrier.operands())
    (dq, dkv, dsink), extra = carrier.results(res)
    return dq, dkv, dsink, extra


RET_QK_W = RET_HEADS * RET_QK_DIM
K_SCALE = RET_QK_DIM ** -0.5


def _ret_prep(geo, proj, cos, sin_signed, name):
    def body(i, p, cs, sn):
        cs2, sn2 = jnp.concatenate([cs] * RET_HEADS, axis=1), jnp.concatenate([sn] * RET_HEADS, axis=1)
        q = _rope(p[:, :RET_QK_W], cs2, sn2, RET_QK_DIM // 4)
        k = _rope(p[:, RET_QK_W:2 * RET_QK_W], cs2, sn2, RET_QK_DIM // 4) * K_SCALE
        return jnp.concatenate([q, k, p[:, 2 * RET_QK_W:]], axis=1)

    return _rowwise(name, body, geo, 128, [(proj, ("rowc", 2 * RET_QK_W + RET_VWIDTH, 0)), (cos, "tab"), (sin_signed, "tab")],
                    [("row", 2 * RET_QK_W + RET_VWIDTH, BF16)])


def _ret_prep_bwd(geo, dq, dk, dv, dgate, cos, sin_signed, name):
    def body(i, dqv, dkv, dvv, dg, cs, sn):
        cs2, sn2 = jnp.concatenate([cs] * RET_HEADS, axis=1), jnp.concatenate([sn] * RET_HEADS, axis=1)
        dkv = dkv * K_SCALE
        dqv = dqv * cs2 + _swap_halves(dqv * sn2, RET_QK_DIM // 4)
        dkv = dkv * cs2 + _swap_halves(dkv * sn2, RET_QK_DIM // 4)
        return jnp.concatenate([dqv, dkv, dvv, dg], axis=1)

    return _rowwise(name, body, geo, 128,
                    [(dq, "row"), (dk, "row"), (dv, "row"), (dgate, "row"), (cos, "tab"), (sin_signed, "tab")],
                    [("row", 2 * RET_QK_W + 2 * RET_VWIDTH, BF16)])


def _ret_step(state, q, k, v, lg, rev):
    c = RET_CHUNK
    ri = lax.broadcasted_iota(jnp.int32, (c, 1), 0).astype(F32)
    cj = lax.broadcasted_iota(jnp.int32, (1, c), 1).astype(F32)
    if rev:
        dist, q_decay, k_decay = cj - ri, jnp.exp(lg * (c - ri)), jnp.exp(lg * ri)
    else:
        dist, q_decay, k_decay = ri - cj, jnp.exp(lg * (ri + 1.0)), jnp.exp(lg * (c - 1.0 - ri))
    intra = jnp.where(dist >= 0, jnp.exp(lg * jnp.maximum(dist, 0.0)), 0.0)
    scores = _mm(q, k, "nt") * intra
    out = _mm(scores, v, "nn") + _mm(q, state, "nn") * q_decay
    new_state = state * jnp.exp(lg * c) + _mm(k * k_decay, v, "tn")
    return new_state, out


def _ret_state0(kc, vc, lg, rev):
    n = kc.shape[0]
    t = lax.broadcasted_iota(jnp.int32, (n, 1), 0).astype(F32)
    decay = jnp.exp(lg * t) if rev else jnp.exp(lg * (n - 1.0 - t))
    return _mm(kc * decay, vc, "tn")


def _ret_specs(geo):
    nq = RET_HEADS
    return [pl.BlockSpec((2 * RET_HEADS, LANES), lambda b, h: (0, 0)),
            pl.BlockSpec((geo.seg, RET_QK_DIM), lambda b, h: (b, h)),
            pl.BlockSpec((geo.seg, RET_QK_DIM), lambda b, h: (b, nq + h)),
            pl.BlockSpec((geo.seg, RET_V_DIM), lambda b, h: (b, nq + h))]


def _retention(geo, qkv, log_g, name):
    nc = geo.s // RET_CHUNK

    def kern(lg_ref, q_ref, k_ref, v_ref, o_ref, st_ref):
        h = pl.program_id(1)
        for d, rev in ((0, False), (1, True)):
            lg = lg_ref[pl.ds(d * RET_HEADS + h, 1), 0:1]
            st_ref[...] = _ret_state0(k_ref[geo.s:geo.seg, :].astype(F32), v_ref[geo.s:geo.seg, :].astype(F32), lg, rev)

            def chunk(ci, carry, d=d, rev=rev, lg=lg):
                r0 = pl.multiple_of((nc - 1 - ci if rev else ci) * RET_CHUNK, RET_CHUNK)
                rows = pl.ds(r0, RET_CHUNK)
                new_state, out = _ret_step(st_ref[...], q_ref[rows, :].astype(F32), k_ref[rows, :].astype(F32),
                                           v_ref[rows, :].astype(F32), lg, rev)
                st_ref[...] = new_state
                if d == 0:
                    o_ref[rows, :] = out
                else:
                    o_ref[rows, :] += out
                return carry

            lax.fori_loop(0, nc, chunk, 0)
        o_ref[geo.s:geo.seg, :] = jnp.zeros((geo.l, RET_V_DIM), F32)

    return pl.pallas_call(
        kern, name=name, grid=(geo.b, RET_HEADS), in_specs=_ret_specs(geo),
        out_specs=pl.BlockSpec((geo.seg, RET_V_DIM), lambda b, h: (b, h)),
        out_shape=jax.ShapeDtypeStruct((geo.r, RET_VWIDTH), F32),
        scratch_shapes=[pltpu.VMEM((RET_QK_DIM, RET_V_DIM), F32)],
        compiler_params=_cparams("parallel", "arbitrary"),
    )(log_g, qkv, qkv, qkv)


def _retention_bwd(geo, qkv, log_g, do, name):
    nc = geo.s // RET_CHUNK
    ctx = slice(geo.s, geo.seg)

    def kern(lg_ref, q_ref, k_ref, v_ref, do_ref, dq_ref, dk_ref, dv_ref, dlg_ref, states_ref, dst_ref, aq_ref, ak_ref, av_ref):
        b, h = pl.program_id(0), pl.program_id(1)

        @pl.when((b == 0) & (h == 0))
        def _():
            dlg_ref[...] = jnp.zeros_like(dlg_ref)

        for d, rev in ((0, False), (1, True)):
            row = pl.ds(d * RET_HEADS + h, 1)
            lg = lg_ref[row, 0:1]
            kc, vc = k_ref[ctx, :].astype(F32), v_ref[ctx, :].astype(F32)
            states_ref[0] = _ret_state0(kc, vc, lg, rev)

            def rows_of(ci, rev=rev):
                return pl.ds(pl.multiple_of((nc - 1 - ci if rev else ci) * RET_CHUNK, RET_CHUNK), RET_CHUNK)

            def load(rows):
                return q_ref[rows, :].astype(F32), k_ref[rows, :].astype(F32), v_ref[rows, :].astype(F32)

            def replay(ci, carry, rev=rev, lg=lg, rows_of=rows_of, load=load):
                states_ref[ci + 1] = _ret_step(states_ref[ci], *load(rows_of(ci)), lg, rev)[0]
                return carry

            lax.fori_loop(0, nc - 1, replay, 0)
            dst_ref[...] = jnp.zeros_like(dst_ref)

            def emit(rows, dq, dk, dv, d=d):
                if d == 0:
                    ak_ref[rows, :], av_ref[rows, :] = dk, dv
                    if dq is not None:
                        aq_ref[rows, :] = dq
                else:
                    dk_ref[rows, :] = (ak_ref[rows, :] + dk).astype(BF16)
                    dv_ref[rows, :] = (av_ref[rows, :] + dv).astype(BF16)
                    if dq is not None:
                        dq_ref[rows, :] = (aq_ref[rows, :] + dq).astype(BF16)

            def back(t, dlg, rev=rev, lg=lg, rows_of=rows_of, load=load, emit=emit):
                ci = nc - 1 - t
                rows = rows_of(ci)
                _, vjp = jax.vjp(lambda st, q, k, v, g: _ret_step(st, q, k, v, g, rev), states_ref[ci], *load(rows), lg)
                dstate, dq, dk, dv, dg = vjp((dst_ref[...], do_ref[rows, :].astype(F32)))
                dst_ref[...] = dstate
                emit(rows, dq, dk, dv)
                return dlg + dg

            dlg = lax.fori_loop(0, nc, back, jnp.zeros((1, 1), F32))
            _, vjp = jax.vjp(lambda kk, vv, g: _ret_state0(kk, vv, g, rev), kc, vc, lg)
            dkc, dvc, dg = vjp(dst_ref[...])
            emit(ctx, None, dkc, dvc)
            dlg_ref[row, :] += jnp.broadcast_to(dlg + dg, (1, LANES))
        dq_ref[ctx, :] = jnp.zeros((geo.l, RET_QK_DIM), BF16)

    nq = RET_HEADS
    return pl.pallas_call(
        kern, name=name, grid=(geo.b, RET_HEADS),
        in_specs=_ret_specs(geo) + [pl.BlockSpec((geo.seg, RET_V_DIM), lambda b, h: (b, h))],
        out_specs=[pl.BlockSpec((geo.seg, RET_QK_DIM), lambda b, h: (b, h)),
                   pl.BlockSpec((geo.seg, RET_QK_DIM), lambda b, h: (b, h)),
                   pl.BlockSpec((geo.seg, RET_V_DIM), lambda b, h: (b, h)),
                   pl.BlockSpec((2 * RET_HEADS, LANES), lambda b, h: (0, 0))],
        out_shape=[jax.ShapeDtypeStruct((geo.r, RET_QK_W), BF16), jax.ShapeDtypeStruct((geo.r, RET_QK_W), BF16),
                   jax.ShapeDtypeStruct((geo.r, RET_VWIDTH), BF16), jax.ShapeDtypeStruct((2 * RET_HEADS, LANES), F32)],
        scratch_shapes=[pltpu.VMEM((nc, RET_QK_DIM, RET_V_DIM), F32), pltpu.VMEM((RET_QK_DIM, RET_V_DIM), F32),
                        pltpu.VMEM((geo.seg, RET_QK_DIM), F32), pltpu.VMEM((geo.seg, RET_QK_DIM), F32),
                        pltpu.VMEM((geo.seg, RET_V_DIM), F32)],
        compiler_params=_cparams("arbitrary", "arbitrary"),
    )(log_g, qkv, qkv, qkv, do)


def _gated(o, g, gain):
    outs = []
    for h in range(RET_HEADS):
        cols = slice(h * RET_V_DIM, (h + 1) * RET_V_DIM)
        oh = o[:, cols]
        mu = jnp.mean(oh, axis=-1, keepdims=True)
        var = jnp.mean(jnp.square(oh - mu), axis=-1, keepdims=True)
        outs.append(_silu(g[:, cols]) * ((oh - mu) * lax.rsqrt(var + EPS) * gain[:, cols]))
    return jnp.concatenate(outs, axis=1)


def _ret_gated(geo, o, proj, gain, name):
    def body(i, ov, gv, gn):
        return _gated(ov, gv, gn)

    gate_block = (2 * RET_QK_W + RET_VWIDTH) // RET_VWIDTH
    return _rowwise(name, body, geo, 128, [(o, "row"), (proj, ("rowc", RET_VWIDTH, gate_block)), (gain, "full")],
                    [("row", RET_VWIDTH, BF16)])


def _ret_gated_bwd(geo, o, proj, gain, dout, name):
    def body(i, ov, gv, gn, dv):
        _, vjp = jax.vjp(_gated, ov, gv, gn)
        return vjp(dv)

    gate_block = (2 * RET_QK_W + RET_VWIDTH) // RET_VWIDTH
    return _rowwise(name, body, geo, 128,
                    [(o, "row"), (proj, ("rowc", RET_VWIDTH, gate_block)), (gain, "full"), (dout, "row")],
                    [("row", RET_VWIDTH, BF16), ("row", RET_VWIDTH, BF16), ("gacc", 1, RET_VWIDTH)])


def _whole(name, fn, out_shapes, *arrays):
    n = len(arrays)

    def kern(*refs):
        res = fn(*[r[...] for r in refs[:n]])
        for ref, val in zip(refs[n:], res):
            ref[...] = val.astype(ref.dtype)

    return pl.pallas_call(kern, name=name, out_shape=out_shapes)(*arrays)


def _rope_tables(geo, head_dim):
    rows = geo.s // GRID_W
    row = jnp.broadcast_to(jnp.arange(rows, dtype=jnp.int32)[:, None], (rows, GRID_W)).reshape(geo.s)
    col = jnp.broadcast_to(jnp.arange(GRID_W, dtype=jnp.int32)[None, :], (rows, GRID_W)).reshape(geo.s)
    axis_dim = head_dim // 2
    inv = ROPE_BASE ** (-jnp.arange(0, axis_dim, 2, dtype=F32) / axis_dim)
    ang_r = row.astype(F32)[:, None] * inv
    ang_c = col.astype(F32)[:, None] * inv
    cos = jnp.concatenate([jnp.cos(ang_r)] * 2 + [jnp.cos(ang_c)] * 2, axis=1)
    sin = jnp.concatenate([-jnp.sin(ang_r), jnp.sin(ang_r), -jnp.sin(ang_c), jnp.sin(ang_c)], axis=1)
    cos = jnp.concatenate([cos, jnp.ones((geo.l, head_dim), F32)], axis=0)
    sin = jnp.concatenate([sin, jnp.zeros((geo.l, head_dim), F32)], axis=0)
    reps = max(1, LANES // head_dim)
    return jnp.tile(cos, (1, reps)), jnp.tile(sin, (1, reps))


def _row_tile(r):
    return next(t for t in (1024, 512, 256, 128) if r % t == 0)


MOD_ROWS = 8


def _local_step(x, ctx, target, sp, wts, mods, plan=None):
    nb, s, d = x.shape
    geo = _Rows(nb, s, ctx.shape[1])
    assert nb + 1 <= MOD_ROWS and d == D_MODEL
    tm = _row_tile(geo.r)
    z = jnp.concatenate([x, ctx], axis=1).reshape(geo.r, d)
    cos64, sin64 = _rope_tables(geo, HEAD_DIM)
    cos256, sin256 = _rope_tables(geo, RET_QK_DIM)
    q_gain = jnp.tile(sp["q_norm"].reshape(1, HEAD_DIM), (1, LANES // HEAD_DIM))
    k_gain = jnp.tile(sp["k_norm"].reshape(1, HEAD_DIM), (1, LANES // HEAD_DIM))
    sink = sp["sink"].reshape(N_HEADS)
    log_g = jnp.broadcast_to(sp["log_g"].reshape(2 * RET_HEADS, 1), (2 * RET_HEADS, LANES))
    gn_g = sp["gn_g"].reshape(1, RET_VWIDTH)

    saved = []
    h1 = _norm_mod(geo, z, sp["norm1_g"][0][None, :], mods[0], 0, "norm1_0")
    for i in range(2):
        mod3 = mods[i]
        n1, n2 = sp["norm1_g"][i][None, :], sp["norm2_g"][i][None, :]
        if i == 0:
            proj = _mm_nn(h1, wts["attn_qkv"], F32, "attn_qkv", tm, wts["attn_qkv"].shape[1], d)
            prep = _attn_prep(geo, proj, cos64, sin64, q_gain, k_gain, "attn_prep")
            o, late = _attention(geo, prep, sink, "attn", plan.gather_job() if plan else None)
            if plan:
                plan.late_weights(late, wts)
            oraw = None
            w_o = wts["attn_o"]
        else:
            proj = _mm_nn(h1, wts["ret_qkvg"], BF16, "ret_qkvg", tm, wts["ret_qkvg"].shape[2], d)
            prep = _ret_prep(geo, proj, cos256, sin256, "ret_prep")
            oraw = _retention(geo, prep, log_g, "ret")
            o = _ret_gated(geo, oraw, proj, gn_g, "ret_gated")
            w_o = wts["ret_o"]
        zmid, mix, h2 = _mm_nn_gate_residual(geo, o, w_o, z, mod3, 2 * d, f"mix_out{i}", norm=(n2, mod3, 3 * d))
        u, a = _ffn_in_swiglu(h2, wts["ffn_in"][i], f"ffn_in{i}")
        next_norm = (sp["norm1_g"][1][None, :], mods[1], 0) if i == 0 else None
        zout, f, h1_next = _mm_nn_gate_residual(geo, a, wts["ffn_out"][i], zmid, mod3, 5 * d, f"ffn_out{i}", norm=next_norm)
        saved.append(dict(z=z, mod3=mod3, n1=n1, n2=n2, h1=h1, proj=proj, prep=prep, o=o, oraw=oraw, mix=mix, zmid=zmid,
                          h2=h2, u=u, a=a, f=f))
        z, h1 = zout, h1_next

    dz, loss, df, dg2 = _loss_head(geo, z, target.reshape(nb * s, d), saved[1]["f"], saved[1]["mod3"], 5 * d, "loss")

    big, small = {}, {}
    dmods = [None, None]
    for i in (1, 0):
        sv = saved[i]
        mod3 = sv["mod3"]
        carry = plan is not None and i == 0
        du, land = _ffn_out_dx_swiglu_bwd(df, wts["ffn_out"][i], sv["u"], f"ffn_out_dx{i}", plan.layer1.swap_job() if carry else None)
        if carry:
            plan.layer1.after_swap(land)
        big[f"ffn_out{i}"] = _mm_tn(sv["a"], df, f"ffn_out_dw{i}", D_FF // 2, 1024, tm).reshape(N_CHIPS, D_FF // N_CHIPS, d)
        n4 = wts["ffn_in"][i].shape[2]
        dh2 = _mm_nt(du, wts["ffn_in"][i], BF16, f"ffn_in_dx{i}", tm, 1024, n4)
        big[f"ffn_in{i}"] = _mm_tn(sv["h2"], du, f"ffn_in_dw{i}", 1024, n4, tm, shards=N_CHIPS)
        if carry:
            plan.start_layer0_ffn(big)
        dzmid, dsh2, dsc2, dn2, dmix, dg1, *land = _norm_mod_bwd(geo, sv["zmid"], sv["n2"], mod3, 3 * d, dh2, dz, f"norm2_bwd{i}",
                                                                 gated=(sv["mix"], mod3, 2 * d),
                                                                 job=plan.layer0_ffn.swap_job() if carry else None)
        if carry:
            plan.layer0_ffn.after_swap(land[0])
        if i == 0:
            do = _mm_nt(dmix, wts["attn_o"], BF16, "attn_out_dx", tm, 1024, 1024)
            big["attn_o"] = _mm_tn(sv["o"], dmix, "attn_out_dw", 1024, 1024, tm).reshape(N_CHIPS, 1024 // N_CHIPS, d)
            dq, dkv, dsink, land = _attention_bwd(geo, sv["prep"], sink, do, "attn_bwd", plan.exchange_job() if plan else None)
            if plan:
                plan.after_exchange(land)
            dproj, dqg, dkg = _attn_prep_bwd(geo, sv["proj"], cos64, sin64, q_gain, k_gain, dq, dkv, "attn_prep_bwd")
            small["q_norm"] = dqg[0, :HEAD_DIM] + dqg[0, HEAD_DIM:]
            small["k_norm"] = dkg[0, :HEAD_DIM] + dkg[0, HEAD_DIM:]
            small["sink"] = dsink[:, 0]
            wq = wts["attn_qkv"]
            dh1 = _mm_nt(dproj, wq, BF16, "attn_qkv_dx", tm, 1024, wq.shape[1])
            dwq = _mm_tn(sv["h1"], dproj, "attn_qkv_dw", 1024, wq.shape[1], tm)
            big["attn_qkv"] = dwq.reshape(d, N_CHIPS, -1).transpose(1, 0, 2)
        else:
            do = _mm_nt(dmix, wts["ret_o"], BF16, "ret_out_dx", tm, 1024, 1024)
            big["ret_o"] = _mm_tn(sv["o"], dmix, "ret_out_dw", 1024, 1024, tm).reshape(N_CHIPS, RET_VWIDTH // N_CHIPS, d)
            doraw, dgate, dgn = _ret_gated_bwd(geo, sv["oraw"], sv["proj"], gn_g, do, "ret_gated_bwd")
            small["gn_g"] = dgn[0]
            dq, dk, dv, dlg = _retention_bwd(geo, sv["prep"], log_g, doraw, "ret_bwd")
            small["log_g"] = dlg[:, 0].reshape(2, RET_HEADS)
            dproj = _ret_prep_bwd(geo, dq, dk, dv, dgate, cos256, sin256, "ret_prep_bwd")
            wq = wts["ret_qkvg"]
            dh1 = _mm_nt(dproj, wq, BF16, "ret_qkvg_dx", tm, 1024, wq.shape[2])
            big["ret_qkvg"] = _mm_tn(sv["h1"], dproj, "ret_qkvg_dw", 1024, wq.shape[2], tm, shards=N_CHIPS)
        below = (saved[0]["f"], saved[0]["mod3"], 5 * d) if i == 1 else None
        dz, dsh1, dsc1, dn1, *below_grads = _norm_mod_bwd(geo, sv["z"], sv["n1"], mod3, 0, dh1, dzmid, f"norm1_bwd{i}", gated=below,
                                                              latent_only=i == 0)
        small[f"norm1_g{i}"], small[f"norm2_g{i}"] = dn1[0], dn2[0]
        parts = [dsh1, dsc1, dg1, dsh2, dsc2, dg2]
        rows = jnp.concatenate([jnp.concatenate([p[:nb, 0, :] for p in parts], axis=1),
                                jnp.concatenate([jnp.sum(p[nb:, 0, :], axis=0, keepdims=True) for p in parts], axis=1),
                                jnp.zeros((MOD_ROWS - nb - 1, 6 * d), F32)], axis=0)
        dmods[i] = rows
        if below_grads:
            df, dg2 = below_grads
        small[f"ada_b{i}"] = jnp.sum(rows, axis=0)
        if plan and i == 1:
            plan.start_layer1(big)
    return loss, dz, big, small, dmods


def _adamw(w, g, m, v, name):
    rows, cols = w.shape
    tr = next((t for t in (256, 128, 64, 32, 16, 8) if rows % t == 0), rows)
    c1 = 1.0 - ADAM_B1 ** ADAM_STEP
    c2 = 1.0 - ADAM_B2 ** ADAM_STEP

    def kern(w_ref, g_ref, m_ref, v_ref, d_ref, nm_ref, nv_ref):
        gv = g_ref[...]
        nm = ADAM_B1 * m_ref[...] + (1.0 - ADAM_B1) * gv
        nv = ADAM_B2 * v_ref[...] + (1.0 - ADAM_B2) * jnp.square(gv)
        d_ref[...] = -ADAM_LR * ((nm / c1) / (jnp.sqrt(nv / c2) + ADAM_EPS) + ADAM_WD * w_ref[...])
        nm_ref[...] = nm
        nv_ref[...] = nv

    spec = pl.BlockSpec((tr, cols), lambda i: (i, 0))
    return pl.pallas_call(
        kern, name=name, grid=(rows // tr,), in_specs=[spec] * 4, out_specs=[spec] * 3,
        out_shape=[jax.ShapeDtypeStruct(w.shape, F32)] * 3, compiler_params=_cparams("parallel"),
    )(w, g, m, v)


N_DEVICES = 8


def _mesh_pos():
    return lax.axis_index("x"), lax.axis_index("y"), lax.axis_index("c")


def _other_chips(x, y):
    return [(1 - x, y), (x, 1 - y), (1 - x, 1 - y)]


def _hbm(n):
    return [pl.BlockSpec(memory_space=pl.ANY)] * n


def _remote(src, dst, send_sem, recv_sem, device):
    return pltpu.make_async_remote_copy(src_ref=src, dst_ref=dst, send_sem=send_sem, recv_sem=recv_sem,
                                        device_id=device, device_id_type=MESH)


def _scalar_spec(grid, in_specs, out_specs):
    return pltpu.PrefetchScalarGridSpec(num_scalar_prefetch=1, grid=grid, in_specs=in_specs, out_specs=out_specs)


def _place_shard(param, layer, pos, name):
    _, r, cols = param.shape
    tr = _slab_tile(r)

    def kern(pos_ref, s_ref, o_ref):
        o_ref[...] = s_ref[...].astype(BF16)

    return pl.pallas_call(
        kern, name=name, out_shape=jax.ShapeDtypeStruct((N_CHIPS, r, cols), BF16),
        grid_spec=_scalar_spec((r // tr,), [pl.BlockSpec((None, tr, cols), lambda i, p: (layer, i, 0))],
                               pl.BlockSpec((None, tr, cols), lambda i, p: (p[1], i, 0))),
        compiler_params=_cparams("parallel"),
    )(pos, param)


class _CommJob:
    def __init__(self, inputs, out_shapes, aliases, sem_shapes, stages, fractions=None):
        self.inputs, self.out_shapes, self.aliases, self.sem_shapes, self.stages = inputs, out_shapes, aliases, sem_shapes, stages
        self.fractions = fractions


def _merge_jobs(a, b):
    assert len(a.stages) == len(b.stages)
    ni, no, ns = len(a.inputs), len(a.out_shapes), len(a.sem_shapes)

    def both(sa, sb):
        def stage(ins, outs, sems):
            sa(ins[:ni], outs[:no], sems[:ns])
            sb(ins[ni:], outs[no:], sems[ns:])
        return stage

    aliases = dict(a.aliases)
    aliases.update({ni + i: no + o for i, o in b.aliases.items()})
    return _CommJob(a.inputs + b.inputs, a.out_shapes + b.out_shapes, aliases, a.sem_shapes + b.sem_shapes,
                    [both(sa, sb) for sa, sb in zip(a.stages, b.stages)])


def _run_job(job, name):
    n_in, n_out = len(job.inputs), len(job.out_shapes)

    def body(*refs):
        for stage in job.stages:
            stage(refs[:n_in], refs[n_in:n_in + n_out], refs[n_in + n_out:])

    return pl.pallas_call(
        body, name=name, in_specs=_hbm(n_in), out_specs=_hbm(n_out), out_shape=job.out_shapes,
        input_output_aliases=job.aliases, scratch_shapes=job.sem_shapes,
    )(*job.inputs)


def _job_marks(job, steps):
    mid = len(job.stages) - 2
    fractions = job.fractions or [(s + 1) / (mid + 1) for s in range(mid)]
    return [0] + [min(steps - 1, 1 + int((steps - 1) * f)) for f in fractions] + [steps - 1]


def _gather_job(placed):
    n = len(placed)

    def half(w, which):
        r2 = placed[w].shape[1] // 2
        return pl.ds(which * r2, r2)

    def ici_copies(outs, sems, slot_of, arrays=range(n)):
        x, y, c = _mesh_pos()
        res = []
        for w in arrays:
            for k, (px, py) in enumerate(_other_chips(x, y)):
                slab = outs[w].at[slot_of(x, y, px, py), half(w, c)]
                res.append((slab, _remote(slab, slab, sems[0].at[w, k], sems[1].at[w, k], (px, py, c))))
        return res

    def forwards(outs, sems, which_core, arrays=range(n)):
        x, y, c = _mesh_pos()
        res = []
        for w in arrays:
            for k, (px, py) in enumerate(_other_chips(x, y)):
                slab = outs[w].at[2 * px + py, half(w, which_core(c))]
                res.append(_remote(slab, slab, sems[2].at[w, k], sems[3].at[w, k], (x, y, 1 - c)))
        return res

    def send(ins, outs, sems):
        for _, cp in ici_copies(outs, sems, lambda x, y, px, py: 2 * x + y):
            cp.start()

    def forward_of(w):
        def forward(ins, outs, sems):
            arrivals = ici_copies(outs, sems, lambda x, y, px, py: 2 * px + py, [w])
            for (_, arrival), fwd in zip(arrivals, forwards(outs, sems, lambda c: c, [w])):
                arrival.wait_recv()
                fwd.start()
        return forward

    def finish(ins, outs, sems):
        for cp in forwards(outs, sems, lambda c: 1 - c):
            cp.wait_recv()
        for _, cp in ici_copies(outs, sems, lambda x, y, px, py: 2 * x + y):
            cp.wait_send()
        for cp in forwards(outs, sems, lambda c: c):
            cp.wait_send()

    sizes = [p.shape[1] * p.shape[2] for p in placed]
    fractions = [sum(sizes[:w + 1]) / sum(sizes) for w in range(n)]
    return _CommJob(list(placed), [jax.ShapeDtypeStruct(p.shape, p.dtype) for p in placed], {w: w for w in range(n)},
                    [pltpu.SemaphoreType.DMA((n, 3))] * 4, [send] + [forward_of(w) for w in range(n)] + [finish], fractions)


def _pair_swap_job(grads):
    n = len(grads)

    def copies(ins, outs, sems):
        x, y, c = _mesh_pos()
        res = []
        for w in range(n):
            r2 = grads[w].shape[1] // 2
            res.append(_remote(ins[w].at[:, pl.ds((1 - c) * r2, r2)], outs[w], sems[0].at[w], sems[1].at[w], (x, y, 1 - c)))
        return res

    def send(ins, outs, sems):
        for cp in copies(ins, outs, sems):
            cp.start()

    def finish(ins, outs, sems):
        for cp in copies(ins, outs, sems):
            cp.wait()

    return _CommJob(list(grads), [jax.ShapeDtypeStruct((N_CHIPS, g.shape[1] // 2, g.shape[2]), F32) for g in grads], {},
                    [pltpu.SemaphoreType.DMA((n,))] * 2, [send, finish])


def _chip_exchange_job(hs):
    n = len(hs)

    def send(ins, outs, sems):
        x, y, c = _mesh_pos()
        for w in range(n):
            for k, (px, py) in enumerate(_other_chips(x, y)):
                _remote(ins[w].at[2 * px + py], outs[w].at[2 * x + y], sems[0].at[w, k], sems[1].at[w, k], (px, py, c)).start()

    def finish(ins, outs, sems):
        x, y, c = _mesh_pos()
        for w in range(n):
            for k, (px, py) in enumerate(_other_chips(x, y)):
                got = outs[w].at[2 * px + py]
                cp = _remote(ins[w].at[2 * px + py], got, sems[0].at[w, k], sems[1].at[w, k], (px, py, c))
                cp.wait_recv()
                cp.wait_send()

    return _CommJob(list(hs), [jax.ShapeDtypeStruct(h.shape, h.dtype) for h in hs], {},
                    [pltpu.SemaphoreType.DMA((n, 3))] * 2, [send, finish])


def _pair_share(ts, name):
    n = len(ts)

    def body(*refs):
        outs = refs[n:2 * n]
        send_sems, recv_sems = refs[2 * n:]
        x, y, c = _mesh_pos()
        sends = []
        for w in range(n):
            r2 = ts[w].shape[0] // 2
            mine = outs[w].at[pl.ds(c * r2, r2)]
            rc = _remote(mine, mine, send_sems.at[w], recv_sems.at[w], (x, y, 1 - c))
            rc.start()
            sends.append(rc)
        for w in range(n):
            r2 = ts[w].shape[0] // 2
            theirs = outs[w].at[pl.ds((1 - c) * r2, r2)]
            _remote(theirs, theirs, send_sems.at[w], recv_sems.at[w], (x, y, 1 - c)).wait_recv()
            sends[w].wait_send()

    return pl.pallas_call(
        body, name=name, in_specs=_hbm(n), out_specs=_hbm(n),
        out_shape=[jax.ShapeDtypeStruct(t.shape, F32) for t in ts],
        input_output_aliases={w: w for w in range(n)},
        scratch_shapes=[pltpu.SemaphoreType.DMA((n,))] * 2,
    )(*ts)


def _slab_tile(rows):
    return next(t for t in (512, 256, 176, 128, 64, 32, 16) if rows % t == 0)


def _sum_pair(grad, land, pos, name):
    _, r2, cols = land.shape
    tr = _slab_tile(r2)
    nt = r2 // tr

    def kern(pos_ref, a_ref, b_ref, o_ref):
        o_ref[...] = (a_ref[...] + b_ref[...]).astype(BF16)

    spec = pl.BlockSpec((None, tr, cols), lambda j, i, p: (j, i, 0))
    return pl.pallas_call(
        kern, name=name, out_shape=jax.ShapeDtypeStruct(land.shape, BF16),
        grid_spec=_scalar_spec((N_CHIPS, nt), [pl.BlockSpec((None, tr, cols), lambda j, i, p: (j, p[0] * nt + i, 0)), spec], spec),
        compiler_params=_cparams("parallel", "parallel"),
    )(pos, grad, land)


def _sum_chips(hs, land, pos, name):
    _, r2, cols = land.shape
    tr = _slab_tile(r2)
    nt = r2 // tr

    def kern(pos_ref, h_ref, l_ref, o_ref):
        acc = jnp.zeros((tr, cols), F32)
        own = h_ref[...].astype(F32)
        for k in range(N_CHIPS):
            acc = acc + jnp.where(pos_ref[1] == k, own, l_ref[k].astype(F32))
        o_ref[...] = acc

    return pl.pallas_call(
        kern, name=name, out_shape=jax.ShapeDtypeStruct((2 * r2, cols), F32),
        grid_spec=_scalar_spec((nt,), [pl.BlockSpec((None, tr, cols), lambda i, p: (p[1], i, 0)),
                                       pl.BlockSpec((N_CHIPS, tr, cols), lambda i, p: (0, i, 0))],
                               pl.BlockSpec((tr, cols), lambda i, p: (p[0] * nt + i, 0))),
        compiler_params=_cparams("parallel"),
    )(pos, hs, land)


class _ReduceScatter:
    def __init__(self, grads, pos, tag):
        self.grads, self.pos, self.tag = list(grads), pos, tag

    def swap_job(self):
        return _pair_swap_job(self.grads)

    def after_swap(self, land):
        self.hs = [_sum_pair(g, l, self.pos, f"grads_pair_sum_{self.tag}{w}") for w, (g, l) in enumerate(zip(self.grads, land))]

    def exchange_job(self):
        return _chip_exchange_job(self.hs)

    def after_exchange(self, land2):
        ts = [_sum_chips(h, l, self.pos, f"grads_chip_sum_{self.tag}{w}") for w, (h, l) in enumerate(zip(self.hs, land2))]
        return _pair_share(ts, f"grads_pair_share_{self.tag}")

    def run(self):
        self.after_swap(_run_job(self.swap_job(), f"grads_pair_swap_{self.tag}"))
        return self.after_exchange(_run_job(self.exchange_job(), f"grads_chip_exchange_{self.tag}"))


EARLY_WEIGHTS = ("attn_qkv",)
LATE_WEIGHTS = ("ffn_in0", "ffn_in1", "ffn_out0", "ffn_out1", "attn_o", "ret_qkvg", "ret_o")
LAYER1_GRADS = ("ffn_out1", "ffn_in1", "ret_o", "ret_qkvg")
LAYER0_FFN_GRADS = ("ffn_out0", "ffn_in0")
LAST_GRADS = ("attn_o", "attn_qkv")


def _fill_weights(wts, full):
    for name, w in full.items():
        if name[:-1] == "ffn_in":
            wts[name[:-1]][int(name[-1])] = w
        elif name[:-1] == "ffn_out":
            wts["ffn_out"][int(name[-1])] = w.reshape(-1, w.shape[2])
        elif name in ("attn_o", "ret_o"):
            wts[name] = w.reshape(-1, w.shape[2])
        elif name == "attn_qkv":
            wts[name] = w.transpose(1, 0, 2).reshape(w.shape[1], -1)
        else:
            wts[name] = w


class _StepPlan:
    def __init__(self, placed, pos):
        self.placed, self.pos = placed, pos
        self.layer1 = self.layer0_ffn = None
        self.reduced = {}

    def gather_job(self):
        return _gather_job([self.placed[k] for k in LATE_WEIGHTS])

    def late_weights(self, outs, wts):
        _fill_weights(wts, dict(zip(LATE_WEIGHTS, outs)))

    def start_layer1(self, big):
        self.layer1 = _ReduceScatter([big[k] for k in LAYER1_GRADS], self.pos, "l1_")

    def start_layer0_ffn(self, big):
        self.layer0_ffn = _ReduceScatter([big[k] for k in LAYER0_FFN_GRADS], self.pos, "l0f_")

    def exchange_job(self):
        return _merge_jobs(self.layer1.exchange_job(), self.layer0_ffn.exchange_job())

    def after_exchange(self, land):
        n1 = len(LAYER1_GRADS)
        self.reduced.update(zip(LAYER1_GRADS, self.layer1.after_exchange(land[:n1])))
        self.reduced.update(zip(LAYER0_FFN_GRADS, self.layer0_ffn.after_exchange(land[n1:])))


def _all_reduce_small(v, name):
    def body(v_ref, o_ref, land_ref, send_sems, recv_sems):
        x, y, c = _mesh_pos()
        me = 4 * x + 2 * y + c
        land_ref[me] = v_ref[...]
        for t in range(N_DEVICES):
            @pl.when(t != me)
            def _(t=t):
                _remote(v_ref, land_ref.at[me], send_sems.at[t], recv_sems.at[me], (t // 4, (t // 2) % 2, t % 2)).start()
        for t in range(N_DEVICES):
            @pl.when(t != me)
            def _(t=t):
                _remote(v_ref, land_ref.at[t], send_sems.at[t], recv_sems.at[t], (t // 4, (t // 2) % 2, t % 2)).wait()
        acc = land_ref[0]
        for t in range(1, N_DEVICES):
            acc = acc + land_ref[t]
        o_ref[...] = acc

    vmem = pl.BlockSpec(memory_space=pltpu.VMEM)
    return pl.pallas_call(
        body, name=name, in_specs=[vmem], out_specs=vmem, out_shape=jax.ShapeDtypeStruct(v.shape, F32),
        scratch_shapes=[pltpu.VMEM((N_DEVICES,) + v.shape, F32), pltpu.SemaphoreType.DMA((N_DEVICES,)),
                        pltpu.SemaphoreType.DMA((N_DEVICES,))],
    )(v)


def _all_to_all_small(v, name):
    def body(v_ref, o_ref, send_sems, recv_sems):
        x, y, c = _mesh_pos()
        me = 4 * x + 2 * y + c
        o_ref[me] = v_ref[me]
        for t in range(N_DEVICES):
            @pl.when(t != me)
            def _(t=t):
                _remote(v_ref.at[t], o_ref.at[me], send_sems.at[t], recv_sems.at[me], (t // 4, (t // 2) % 2, t % 2)).start()
        for t in range(N_DEVICES):
            @pl.when(t != me)
            def _(t=t):
                _remote(v_ref.at[t], o_ref.at[t], send_sems.at[t], recv_sems.at[t], (t // 4, (t // 2) % 2, t % 2)).wait()

    vmem = pl.BlockSpec(memory_space=pltpu.VMEM)
    return pl.pallas_call(
        body, name=name, in_specs=[vmem], out_specs=vmem, out_shape=jax.ShapeDtypeStruct(v.shape, F32),
        scratch_shapes=[pltpu.SemaphoreType.DMA((N_DEVICES,)), pltpu.SemaphoreType.DMA((N_DEVICES,))],
    )(v)


ALL_ROWS = 40


class _AdaLN:
    def __init__(self, c, c_ctx, ada_w, ada_b):
        xi, yi, ci = _mesh_pos()
        self.me, self.chip, self.core = 4 * xi + 2 * yi + ci, 2 * xi + yi, ci
        self.nb, d = c.shape
        self.ada_w, self.c_ctx = ada_w, c_ctx
        self.cols = ada_w.shape[2]
        assert self.nb * N_DEVICES + 1 <= ALL_ROWS
        placed = lax.dynamic_update_slice(jnp.zeros((ALL_ROWS, d), F32), c, (self.me * self.nb, 0))
        c_all = _all_reduce_small(placed, "gather_conditioning").at[self.nb * N_DEVICES].set(c_ctx)
        self.cact, = _whole("cond_silu", lambda v: (_silu(v),), [jax.ShapeDtypeStruct(c_all.shape, F32)], c_all)
        parts = []
        for i in range(2):
            bias = lax.dynamic_slice(ada_b[i], (self.chip * self.cols,), (self.cols,))[None, :]
            parts.append(_mm_nn(self.cact, ada_w[i], F32, f"mod{i}", ALL_ROWS, self.cols, d, bias=bias))
        part = jnp.concatenate(parts, axis=1)
        ctx_row = self.nb * N_DEVICES
        rows = [[t * self.nb + b for b in range(self.nb)] + [ctx_row] * (MOD_ROWS - self.nb) for t in range(N_DEVICES)]
        got = _all_to_all_small(part[jnp.asarray(rows)], "mod_exchange")
        self.mods = [jnp.concatenate([got[2 * j][:self.nb + 1, i * self.cols:(i + 1) * self.cols] for j in range(N_CHIPS)], axis=1)[:, None, :]
                     for i in range(2)]

    def backward(self, dmods):
        nb, cols, d = self.nb, self.cols, self.ada_w.shape[1]
        blocks = [jnp.concatenate([dm[:, j * cols:(j + 1) * cols] for dm in dmods], axis=1) for j in range(N_CHIPS)]
        got = _all_to_all_small(jnp.stack([blocks[t // 2] for t in range(N_DEVICES)]), "dmod_exchange")
        dall = jnp.concatenate([got[:, :nb].reshape(N_DEVICES * nb, 2 * cols), jnp.sum(got[:, nb], axis=0, keepdims=True),
                                jnp.zeros((ALL_ROWS - N_DEVICES * nb - 1, 2 * cols), F32)], axis=0)
        dctx = jnp.concatenate([dall[N_DEVICES * nb][None, :], jnp.zeros((MOD_ROWS - 1, 2 * cols), F32)], axis=0)
        grads, dcact = [], []
        for i in range(2):
            grads.append(_mm_tn(self.cact, dall[:, i * cols:(i + 1) * cols], f"ada_dw{i}", d, cols, ALL_ROWS))
            dcact.append(_mm_nt(dctx[:, i * cols:(i + 1) * cols], self.ada_w[i], F32, f"ada_dx{i}", MOD_ROWS, d, cols))

        def silu_bwd(v, d0, d1):
            sg = _sigmoid(v)
            return ((d0 + d1)[0:1] * (sg * (1.0 + v * (1.0 - sg))),)

        dc_ctx, = _whole("cond_silu_bwd", silu_bwd, [jax.ShapeDtypeStruct((1, d), F32)], self.c_ctx[None, :], dcact[0], dcact[1])
        return grads, jnp.where(self.core == 0, dc_ctx[0], jnp.zeros((d,), F32))


SMALL_ROWS = 24


def _pack_small(small, dlogit):
    d = D_MODEL
    misc = jnp.zeros((d,), F32)
    misc = misc.at[0:HEAD_DIM].set(small["q_norm"]).at[128:128 + HEAD_DIM].set(small["k_norm"])
    misc = misc.at[256:256 + N_HEADS].set(small["sink"]).at[384:384 + 2 * RET_HEADS].set(dlogit.reshape(-1))
    rows = [small["ada_b0"].reshape(6, d), small["ada_b1"].reshape(6, d), small["norm1_g0"][None], small["norm1_g1"][None],
            small["norm2_g0"][None], small["norm2_g1"][None], small["c_ctx"][None], small["gn_g"].reshape(2, d), misc[None]]
    buf = jnp.concatenate(rows, axis=0)
    return jnp.concatenate([buf, jnp.zeros((SMALL_ROWS - buf.shape[0], d), F32)], axis=0)


def _unpack_small(buf):
    d = D_MODEL
    misc = buf[19]
    return dict(ada_b=buf[0:12].reshape(2, 6 * d), norm1_g=buf[12:14], norm2_g=buf[14:16], c_ctx=buf[16],
                gn_g=buf[17:19].reshape(2 * d), q_norm=misc[0:HEAD_DIM], k_norm=misc[128:128 + HEAD_DIM],
                sink=misc[256:256 + N_HEADS], decay=misc[384:384 + 2 * RET_HEADS])


def kernel(x, c, ctx, c_ctx, ada_w, ada_b, norm1_g, norm2_g, ffn_w_in, ffn_w_out, attn_w_qkv, attn_q_norm, attn_k_norm, attn_sink, attn_w_o, ret_w_qkvg, ret_decay_logit, ret_gn_g, ret_w_o, loss_target, m_c_ctx, m_ada_w, m_ada_b, m_norm1_g, m_norm2_g, m_ffn_w_in, m_ffn_w_out, m_attn_w_qkv, m_attn_q_norm, m_attn_k_norm, m_attn_sink, m_attn_w_o, m_ret_w_qkvg, m_ret_decay_logit, m_ret_gn_g, m_ret_w_o, v_c_ctx, v_ada_w, v_ada_b, v_norm1_g, v_norm2_g, v_ffn_w_in, v_ffn_w_out, v_attn_w_qkv, v_attn_q_norm, v_attn_k_norm, v_attn_sink, v_attn_w_o, v_ret_w_qkvg, v_ret_decay_logit, v_ret_gn_g, v_ret_w_o):
    xi, yi, ci = _mesh_pos()
    chip = 2 * xi + yi
    nb, s, d = x.shape
    gn_shard = ret_gn_g.shape[1]

    shards = dict(ffn_in0=(ffn_w_in, 0), ffn_in1=(ffn_w_in, 1), ffn_out0=(ffn_w_out, 0),
                  ffn_out1=(ffn_w_out, 1), attn_qkv=(attn_w_qkv, 0), attn_o=(attn_w_o, 0), ret_qkvg=(ret_w_qkvg, 0), ret_o=(ret_w_o, 0))
    names = list(shards)
    pos = jnp.stack([ci, chip]).astype(jnp.int32)
    placed = {k: _place_shard(*shards[k], pos, f"place_{k}") for k in names}
    early = _run_job(_gather_job([placed[k] for k in EARLY_WEIGHTS]), "gather_early_weights")
    gn_mine = jnp.where(ci == 0, ret_gn_g[0], jnp.zeros_like(ret_gn_g[0]))
    gn_place = lax.dynamic_update_slice(jnp.zeros((RET_VWIDTH,), F32), gn_mine, (chip * gn_shard,))
    gn_full = _all_reduce_small(gn_place.reshape(2, d), "gather_gn_gain").reshape(RET_VWIDTH)

    wts = dict(ffn_in=[None, None], ffn_out=[None, None], attn_qkv=None, attn_o=None, ret_qkvg=None, ret_o=None)
    ada = _AdaLN(c, c_ctx, ada_w, ada_b)
    _fill_weights(wts, dict(zip(EARLY_WEIGHTS, early)))
    plan = _StepPlan(placed, pos)
    decay_logit = ret_decay_logit[0]
    sp = dict(norm1_g=norm1_g, norm2_g=norm2_g, q_norm=attn_q_norm[0], k_norm=attn_k_norm[0],
              sink=attn_sink[0], log_g=jax.nn.log_sigmoid(decay_logit), gn_g=gn_full)
    loss_part, dz, big, small, dmods = _local_step(x, ctx, loss_target, sp, wts, ada.mods, plan)
    ada_grads, small["c_ctx"] = ada.backward(dmods)

    loss = lax.psum(loss_part[0, 0], ("x", "y", "c"))
    grad_x = dz.reshape(nb, s, d)

    dlogit = small["log_g"] * jax.nn.sigmoid(-decay_logit)
    sg = _unpack_small(_all_reduce_small(_pack_small(small, dlogit), "reduce_small_grads"))
    reduced = dict(plan.reduced)
    reduced.update(zip(LAST_GRADS, _ReduceScatter([big[k] for k in LAST_GRADS], pos, "last_").run()))

    grads = dict(
        c_ctx=sg["c_ctx"], ada_w=jnp.stack(ada_grads), ada_b=sg["ada_b"], norm1_g=sg["norm1_g"],
        norm2_g=sg["norm2_g"], ffn_w_in=jnp.stack([reduced["ffn_in0"], reduced["ffn_in1"]]),
        ffn_w_out=jnp.stack([reduced["ffn_out0"], reduced["ffn_out1"]]), attn_w_qkv=reduced["attn_qkv"][None],
        attn_q_norm=sg["q_norm"][None], attn_k_norm=sg["k_norm"][None], attn_sink=sg["sink"][None],
        attn_w_o=reduced["attn_o"][None], ret_w_qkvg=reduced["ret_qkvg"][None], ret_decay_logit=sg["decay"].reshape(1, 2, RET_HEADS),
        ret_gn_g=lax.dynamic_slice(sg["gn_g"], (chip * gn_shard,), (gn_shard,))[None], ret_w_o=reduced["ret_o"][None])
    params = dict(c_ctx=(c_ctx, m_c_ctx, v_c_ctx), ada_w=(ada_w, m_ada_w, v_ada_w), ada_b=(ada_b, m_ada_b, v_ada_b),
                  norm1_g=(norm1_g, m_norm1_g, v_norm1_g), norm2_g=(norm2_g, m_norm2_g, v_norm2_g),
                  ffn_w_in=(ffn_w_in, m_ffn_w_in, v_ffn_w_in), ffn_w_out=(ffn_w_out, m_ffn_w_out, v_ffn_w_out),
                  attn_w_qkv=(attn_w_qkv, m_attn_w_qkv, v_attn_w_qkv), attn_q_norm=(attn_q_norm, m_attn_q_norm, v_attn_q_norm),
                  attn_k_norm=(attn_k_norm, m_attn_k_norm, v_attn_k_norm), attn_sink=(attn_sink, m_attn_sink, v_attn_sink),
                  attn_w_o=(attn_w_o, m_attn_w_o, v_attn_w_o), ret_w_qkvg=(ret_w_qkvg, m_ret_w_qkvg, v_ret_w_qkvg),
                  ret_decay_logit=(ret_decay_logit, m_ret_decay_logit, v_ret_decay_logit),
                  ret_gn_g=(ret_gn_g, m_ret_gn_g, v_ret_gn_g), ret_w_o=(ret_w_o, m_ret_w_o, v_ret_w_o))
    order = list(params)
    deltas, new_m, new_v = [], [], []
    for k in order:
        w, m, v = params[k]
        g = grads[k].reshape(w.shape)
        grads[k] = g
        flat = (-1, w.shape[-1]) if w.ndim > 1 else (1, -1)
        if k == "ret_decay_logit":
            flat = (1, -1)
        dw, nm, nv = _adamw(w.reshape(flat), g.reshape(flat), m.reshape(flat), v.reshape(flat), f"adamw_{k}")
        deltas.append(dw.reshape(w.shape))
        new_m.append(nm.reshape(w.shape))
        new_v.append(nv.reshape(w.shape))
    return (loss, grad_x, *[grads[k] for k in order], *deltas, *new_m, *new_v)
```

```python
import functools

import jax
import jax.numpy as jnp
from jax import lax
from jax.experimental import pallas as pl
from jax.experimental.pallas import tpu as pltpu

F32 = jnp.float32
BF16 = jnp.bfloat16

D_MODEL = 1024
N_HEADS = 16
N_KV_HEADS = 4
HEAD_DIM = 64
WINDOW = 128
ATTN_BLOCK = 128
BAND = ATTN_BLOCK + 2 * WINDOW
RET_HEADS = 4
RET_QK_DIM = 256
RET_V_DIM = 512
RET_VWIDTH = 2048
RET_CHUNK = 128
D_FF = 2816
GRID_W = 64
ROPE_BASE = 10000.0
EPS = 1e-6
NEG_INF = -1e30
LANES = 128

ADAM_LR = 0.001
ADAM_B1 = 0.9
ADAM_B2 = 0.999
ADAM_EPS = 1e-08
ADAM_WD = 0.01
ADAM_STEP = 10

VMEM_LIMIT_BYTES = 56 * 1024 * 1024
MESH = pl.DeviceIdType.MESH
N_CHIPS = 4


def _cparams(*sem):
    return pltpu.CompilerParams(dimension_semantics=sem, vmem_limit_bytes=VMEM_LIMIT_BYTES)


_DIMS = {"nn": ((1,), (0,)), "nt": ((1,), (1,)), "tn": ((0,), (0,))}


def _dot(a, b, form):
    return lax.dot_general(a.astype(BF16), b.astype(BF16), (_DIMS[form], ((), ())), preferred_element_type=F32)


@functools.partial(jax.custom_vjp, nondiff_argnums=(2,))
def _mm(a, b, form):
    return _dot(a, b, form)


def _mm_fwd(a, b, form):
    return _dot(a, b, form), (a, b)


def _mm_bwd(form, res, ct):
    a, b = res
    if form == "nn":
        da, db = _dot(ct, b, "nt"), _dot(a, ct, "tn")
    elif form == "nt":
        da, db = _dot(ct, b, "nn"), _dot(ct, a, "tn")
    else:
        da, db = _dot(b, ct, "nt"), _dot(a, ct, "nn")
    return da.astype(a.dtype), db.astype(b.dtype)


_mm.defvjp(_mm_fwd, _mm_bwd)


def _swap_halves(x, half):
    w = x.shape[-1]
    lane = lax.broadcasted_iota(jnp.int32, x.shape, x.ndim - 1)
    return jnp.where(lane % (2 * half) < half, pltpu.roll(x, w - half, x.ndim - 1), pltpu.roll(x, half, x.ndim - 1))


@functools.partial(jax.custom_vjp, nondiff_argnums=(1,))
def _rot(x, half):
    return _swap_halves(x, half)


def _rot_fwd(x, half):
    return _swap_halves(x, half), None


def _rot_bwd(half, _, ct):
    return (_swap_halves(ct, half),)


_rot.defvjp(_rot_fwd, _rot_bwd)


def _rope(x, cos, sin_signed, half):
    return x * cos + _rot(x, half) * sin_signed


def _head_mean_square(x):
    r = lax.broadcasted_iota(jnp.int32, (LANES, LANES), 0) // HEAD_DIM
    c = lax.broadcasted_iota(jnp.int32, (LANES, LANES), 1) // HEAD_DIM
    g = jnp.where(r == c, 1.0 / HEAD_DIM, 0.0).astype(F32)
    return jnp.dot(x * x, g, precision=lax.Precision.HIGHEST, preferred_element_type=F32)


def _qk_chunk(x, gain, cos, sin_signed, scale):
    y = x * lax.rsqrt(_head_mean_square(x) + EPS) * gain
    return _rope(y, cos, sin_signed, HEAD_DIM // 4) * scale


def _sigmoid(x):
    return 1.0 / (1.0 + jnp.exp(-x))


def _silu(x):
    return x * _sigmoid(x)


def _mm_nn(a, w, out_dtype, name, tm, tn, tk, bias=None):
    m, k_dim = a.shape
    if w.ndim == 3:
        n = w.shape[0] * w.shape[2]
        per = w.shape[2] // tn
        assert w.shape[2] % tn == 0
        w_spec = pl.BlockSpec((None, tk, tn), lambda i, j, k: (j // per, k, j % per))
    else:
        n = w.shape[1]
        w_spec = pl.BlockSpec((tk, tn), lambda i, j, k: (k, j))
    assert m % tm == 0 and n % tn == 0 and k_dim % tk == 0, (name, a.shape, w.shape, tm, tn, tk)
    nk = k_dim // tk
    has_bias = bias is not None

    def body(*refs):
        a_ref, w_ref = refs[0], refs[1]
        b_ref = refs[2] if has_bias else None
        o_ref, acc_ref = (refs[-1], None) if nk == 1 else (refs[-2], refs[-1])
        if nk == 1:
            part = jnp.dot(a_ref[...].astype(BF16), w_ref[...].astype(BF16), preferred_element_type=F32)
            o_ref[...] = (part + b_ref[...] if has_bias else part).astype(out_dtype)
            return
        k = pl.program_id(2)

        @pl.when(k == 0)
        def _():
            acc_ref[...] = jnp.zeros_like(acc_ref)

        acc_ref[...] += jnp.dot(a_ref[...].astype(BF16), w_ref[...].astype(BF16), preferred_element_type=F32)

        @pl.when(k == nk - 1)
        def _():
            r = acc_ref[...]
            if has_bias:
                r = r + b_ref[...]
            o_ref[...] = r.astype(out_dtype)

    in_specs = [pl.BlockSpec((tm, tk), lambda i, j, k: (i, k)), w_spec]
    args = [a, w]
    if has_bias:
        in_specs.append(pl.BlockSpec((1, tn), lambda i, j, k: (0, j)))
        args.append(bias)
    return pl.pallas_call(
        body, name=name, grid=(m // tm, n // tn, nk), in_specs=in_specs,
        out_specs=pl.BlockSpec((tm, tn), lambda i, j, k: (i, j)),
        out_shape=jax.ShapeDtypeStruct((m, n), out_dtype),
        scratch_shapes=[pltpu.VMEM((tm, tn), F32)] if nk > 1 else [],
        compiler_params=_cparams("parallel", "parallel", "arbitrary"),
    )(*args)


def _mm_nt(a, w, out_dtype, name, tm, tn, tk):
    if a.ndim == 3:
        planes, m, plane_w = a.shape
        c_dim = planes * plane_w
        a_per = plane_w // tk
        assert plane_w % tk == 0
        a_spec = pl.BlockSpec((None, tm, tk), lambda i, j, k: (k // a_per, i, k % a_per))
    else:
        m, c_dim = a.shape
        a_spec = pl.BlockSpec((tm, tk), lambda i, j, k: (i, k))
    if w.ndim == 3:
        k_out = w.shape[1]
        per = w.shape[2] // tk
        assert w.shape[2] % tk == 0 and w.shape[0] * w.shape[2] == c_dim
        w_spec = pl.BlockSpec((None, tn, tk), lambda i, j, k: (k // per, j, k % per))
    else:
        k_out = w.shape[0]
        assert w.shape[1] == c_dim
        w_spec = pl.BlockSpec((tn, tk), lambda i, j, k: (j, k))
    assert m % tm == 0 and k_out % tn == 0 and c_dim % tk == 0, (name, a.shape, w.shape, tm, tn, tk)
    nk = c_dim // tk

    def body(a_ref, w_ref, o_ref, acc_ref=None):
        if nk == 1:
            o_ref[...] = _dot(a_ref[...], w_ref[...], "nt").astype(out_dtype)
            return
        k = pl.program_id(2)

        @pl.when(k == 0)
        def _():
            acc_ref[...] = jnp.zeros_like(acc_ref)

        acc_ref[...] += _dot(a_ref[...], w_ref[...], "nt")

        @pl.when(k == nk - 1)
        def _():
            o_ref[...] = acc_ref[...].astype(out_dtype)

    return pl.pallas_call(
        body, name=name, grid=(m // tm, k_out // tn, nk),
        in_specs=[a_spec, w_spec],
        out_specs=pl.BlockSpec((tm, tn), lambda i, j, k: (i, j)),
        out_shape=jax.ShapeDtypeStruct((m, k_out), out_dtype),
        scratch_shapes=[pltpu.VMEM((tm, tn), F32)] if nk > 1 else [],
        compiler_params=_cparams("parallel", "parallel", "arbitrary"),
    )(a, w)


def _mm_tn(a, b, name, tm, tn, tk, shards=None):
    r, k_dim = a.shape
    if b.ndim == 3:
        n = b.shape[0] * b.shape[2]
        b_per = b.shape[2] // tn
        assert b.shape[2] % tn == 0
        b_spec = pl.BlockSpec((None, tk, tn), lambda i, j, k: (j // b_per, k, j % b_per))
    else:
        n = b.shape[1]
        b_spec = pl.BlockSpec((tk, tn), lambda i, j, k: (k, j))
    assert r % tk == 0 and k_dim % tm == 0 and n % tn == 0, (name, a.shape, b.shape, tm, tn, tk)
    nk = r // tk
    if shards:
        per = n // shards // tn
        assert n % (shards * tn) == 0
        out_shape = jax.ShapeDtypeStruct((shards, k_dim, n // shards), F32)
        out_spec = pl.BlockSpec((None, tm, tn), lambda i, j, k: (j // per, i, j % per))
    else:
        out_shape = jax.ShapeDtypeStruct((k_dim, n), F32)
        out_spec = pl.BlockSpec((tm, tn), lambda i, j, k: (i, j))

    def body(a_ref, b_ref, o_ref):
        k = pl.program_id(2)

        @pl.when(k == 0)
        def _():
            o_ref[...] = jnp.zeros_like(o_ref)

        o_ref[...] += _dot(a_ref[...], b_ref[...], "tn")

    return pl.pallas_call(
        body, name=name, grid=(k_dim // tm, n // tn, nk),
        in_specs=[pl.BlockSpec((tk, tm), lambda i, j, k: (k, i)), b_spec],
        out_specs=out_spec, out_shape=out_shape,
        compiler_params=_cparams("parallel", "parallel", "arbitrary"),
    )(a, b)


class _Carrier:
    def __init__(self, job, n_in, n_out, n_scratch):
        self.job, self.n_in, self.n_out, self.n_scratch = job, n_in, n_out, n_scratch
        self.ji = len(job.inputs) if job else 0
        self.jo = len(job.out_shapes) if job else 0

    def operands(self):
        return list(self.job.inputs) if self.job else []

    def in_specs(self):
        return [pl.BlockSpec(memory_space=pl.ANY)] * self.ji

    def out_specs(self):
        return [pl.BlockSpec(memory_space=pl.ANY)] * self.jo

    def out_shapes(self):
        return list(self.job.out_shapes) if self.job else []

    def scratch(self):
        return list(self.job.sem_shapes) if self.job else []

    def aliases(self):
        return {self.n_in + a: self.n_out + b for a, b in self.job.aliases.items()} if self.job else {}

    def split(self, refs):
        a = self.n_in
        b = a + self.ji
        c = b + self.n_out
        d = c + self.jo
        e = d + self.n_scratch
        return list(refs[:a]) + list(refs[b:c]) + list(refs[d:e]), (refs[a:b], refs[c:d], refs[e:])

    def run(self, job_refs, step, steps):
        if not self.job:
            return
        for stage, mark in zip(self.job.stages, _job_marks(self.job, steps)):
            pl.when(step == mark)(functools.partial(stage, *job_refs))

    def results(self, res):
        res = list(res)
        return res[:self.n_out], res[self.n_out:]


FFN_ROW_TILE = 768


def _ffn_tile(r):
    return FFN_ROW_TILE if r % FFN_ROW_TILE == 0 else _row_tile(r)


def _ffn_in_swiglu(h, w, name):
    r, k_dim = h.shape
    n4 = w.shape[2]
    tm = _ffn_tile(r)

    def body(h_ref, wg_ref, wu_ref, u_ref, a_ref):
        hv = h_ref[...]
        g = jnp.dot(hv, wg_ref[...], preferred_element_type=F32)
        up = jnp.dot(hv, wu_ref[...], preferred_element_type=F32)
        u_ref[0] = g.astype(BF16)
        u_ref[1] = up.astype(BF16)
        a_ref[...] = (_silu(g) * up).astype(BF16)

    return pl.pallas_call(
        body, name=name, grid=(r // tm, 2),
        in_specs=[pl.BlockSpec((tm, k_dim), lambda i, j: (i, 0)),
                  pl.BlockSpec((None, k_dim, n4), lambda i, j: (j, 0, 0)),
                  pl.BlockSpec((None, k_dim, n4), lambda i, j: (j + 2, 0, 0))],
        out_specs=[pl.BlockSpec((2, tm, n4), lambda i, j: (0, i, j)), pl.BlockSpec((tm, n4), lambda i, j: (i, j))],
        out_shape=[jax.ShapeDtypeStruct((2, r, 2 * n4), BF16), jax.ShapeDtypeStruct((r, 2 * n4), BF16)],
        compiler_params=_cparams("parallel", "parallel"),
    )(h, w, w)


def _mm_nn_gate_residual(geo, a, w, z, mod, off, name, norm=None):
    r, k_dim = a.shape
    n = w.shape[1]
    tm = FFN_ROW_TILE if geo.seg % FFN_ROW_TILE == 0 else 256
    tiles = geo.seg // tm
    assert geo.seg % tm == 0 and r == geo.r and n == D_MODEL

    def body(a_ref, w_ref, z_ref, mx_ref, mc_ref, *rest):
        out = jnp.dot(a_ref[...], w_ref[...], preferred_element_type=F32)
        is_x = (pl.program_id(0) % tiles) * tm + lax.broadcasted_iota(jnp.int32, (tm, 1), 0) < geo.s
        zo = z_ref[...] + jnp.where(is_x, mx_ref[:, off:off + n], mc_ref[:, off:off + n]) * out
        if norm:
            g_ref, nx_ref, nc_ref, zo_ref, raw_ref, h_ref = rest
            no = norm[2]
            shift = jnp.where(is_x, nx_ref[:, no:no + n], nc_ref[:, no:no + n])
            scale = jnp.where(is_x, nx_ref[:, no + n:no + 2 * n], nc_ref[:, no + n:no + 2 * n])
            rs = lax.rsqrt(jnp.mean(zo * zo, axis=-1, keepdims=True) + EPS)
            h_ref[...] = ((zo * rs) * g_ref[...] * (1.0 + scale) + shift).astype(BF16)
        else:
            zo_ref, raw_ref = rest
        zo_ref[...] = zo
        raw_ref[...] = out.astype(BF16)

    def mod_specs(m):
        return [pl.BlockSpec((None, 1, m.shape[2]), lambda i: (i // tiles, 0, 0)), pl.BlockSpec((None, 1, m.shape[2]), lambda i: (geo.b, 0, 0))]

    row = pl.BlockSpec((tm, n), lambda i: (i, 0))
    in_specs = [pl.BlockSpec((tm, k_dim), lambda i: (i, 0)), pl.BlockSpec((k_dim, n), lambda i: (0, 0)), row] + mod_specs(mod)
    args = [a, w, z, mod, mod]
    out_specs, out_shape = [row, row], [jax.ShapeDtypeStruct((r, n), F32), jax.ShapeDtypeStruct((r, n), BF16)]
    if norm:
        in_specs += [pl.BlockSpec((1, n), lambda i: (0, 0))] + mod_specs(norm[1])
        args += [norm[0], norm[1], norm[1]]
        out_specs.append(row)
        out_shape.append(jax.ShapeDtypeStruct((r, n), BF16))
    res = pl.pallas_call(body, name=name, grid=(r // tm,), in_specs=in_specs, out_specs=out_specs, out_shape=out_shape,
                         compiler_params=_cparams("parallel"))(*args)
    return res if norm else (*res, None)


def _ffn_out_dx_swiglu_bwd(df, w_out, u, name, job=None):
    r, d = df.shape
    n4 = u.shape[2] // 2
    tm = _ffn_tile(r)
    carrier = _Carrier(job, 3, 1, 0)
    steps = (r // tm) * 2

    def body(*refs):
        (df_ref, w_ref, u_ref, du_ref), job_refs = carrier.split(refs)
        carrier.run(job_refs, pl.program_id(0) * 2 + pl.program_id(1), steps)
        da = _dot(df_ref[...], w_ref[...], "nt")
        g, up = u_ref[0].astype(F32), u_ref[1].astype(F32)
        s = _sigmoid(g)
        du_ref[0] = (da * up * (s * (1.0 + g * (1.0 - s)))).astype(BF16)
        du_ref[1] = (da * (g * s)).astype(BF16)

    res = pl.pallas_call(
        body, name=name, grid=(r // tm, 2),
        in_specs=[pl.BlockSpec((tm, d), lambda i, j: (i, 0)), pl.BlockSpec((n4, d), lambda i, j: (j, 0)),
                  pl.BlockSpec((2, tm, n4), lambda i, j: (0, i, j))] + carrier.in_specs(),
        out_specs=[pl.BlockSpec((2, tm, n4), lambda i, j: (0, i, j))] + carrier.out_specs(),
        out_shape=[jax.ShapeDtypeStruct(u.shape, BF16)] + carrier.out_shapes(),
        scratch_shapes=carrier.scratch(), input_output_aliases=carrier.aliases(),
        compiler_params=_cparams("arbitrary", "arbitrary"),
    )(df, w_out, u, *carrier.operands())
    (du,), extra = carrier.results(res)
    return du, extra


class _Rows:
    def __init__(self, b, s, l):
        self.b, self.s, self.l = b, s, l
        self.seg = s + l
        self.r = b * self.seg


def _rowwise(name, body, geo, tm, ins, outs, job=None):
    seg_blocks, x_blocks = geo.seg // tm, geo.s // tm
    assert geo.seg % tm == 0 and geo.s % tm == 0
    nb = geo.b

    def is_ctx(i):
        return i % seg_blocks >= x_blocks

    in_specs, args = [], []
    for arr, kind in ins:
        args.append(arr)
        if kind == "row":
            in_specs.append(pl.BlockSpec((tm, arr.shape[1]), lambda i: (i, 0)))
        elif kind == "ex":
            in_specs.append(pl.BlockSpec((None, 1, arr.shape[2]), lambda i: (jnp.where(is_ctx(i), nb, i // seg_blocks), 0, 0)))
        elif kind == "full":
            in_specs.append(pl.BlockSpec(arr.shape, lambda i, nd=arr.ndim: (0,) * nd))
        elif kind == "tab":
            in_specs.append(pl.BlockSpec((tm, arr.shape[1]), lambda i: (i % seg_blocks, 0)))
        elif kind == "xrow":
            in_specs.append(pl.BlockSpec(
                (tm, arr.shape[1]), lambda i: ((i // seg_blocks) * x_blocks + jnp.minimum(i % seg_blocks, x_blocks - 1), 0)))
        else:
            _, width, cb = kind
            in_specs.append(pl.BlockSpec((tm, width), lambda i, cb=cb: (i, cb)))
    out_specs, out_shapes = [], []
    for o in outs:
        if o[0] == "row":
            out_specs.append(pl.BlockSpec((tm, o[1]), lambda i: (i, 0)))
            out_shapes.append(jax.ShapeDtypeStruct((geo.r, o[1]), o[2]))
        elif o[0] == "xrow":
            out_specs.append(pl.BlockSpec(
                (tm, o[1]), lambda i: ((i // seg_blocks) * x_blocks + jnp.minimum(i % seg_blocks, x_blocks - 1), 0)))
            out_shapes.append(jax.ShapeDtypeStruct((geo.b * geo.s, o[1]), o[2]))
        elif o[0] == "exacc":
            out_specs.append(pl.BlockSpec((None, 1, o[1]), lambda i: (jnp.where(is_ctx(i), nb, 0) + i // seg_blocks, 0, 0)))
            out_shapes.append(jax.ShapeDtypeStruct((2 * nb, 1, o[1]), F32))
        else:
            out_specs.append(pl.BlockSpec((o[1], o[2]), lambda i: (0, 0)))
            out_shapes.append(jax.ShapeDtypeStruct((o[1], o[2]), F32))
    n_in = len(ins)
    carrier = _Carrier(job, n_in, len(outs), 0)

    def kern(*refs):
        i = pl.program_id(0)
        refs, job_refs = carrier.split(refs)
        carrier.run(job_refs, i, geo.r // tm)
        res = body(i, *[r[...].astype(F32) for r in refs[:n_in]])
        if not isinstance(res, (tuple, list)):
            res = (res,)
        jj = i % seg_blocks
        first_of_part = (jj == 0) | (jj == x_blocks)
        for o, ref, val in zip(outs, refs[n_in:], res):
            if o[0] == "row":
                ref[...] = val.astype(ref.dtype)
            elif o[0] == "xrow":
                @pl.when(jj < x_blocks)
                def _(ref=ref, val=val):
                    ref[...] = val.astype(ref.dtype)
            else:
                first = first_of_part if o[0] == "exacc" else i == 0

                @pl.when(first)
                def _(ref=ref, val=val):
                    ref[...] = val

                @pl.when(jnp.logical_not(first))
                def _(ref=ref, val=val):
                    ref[...] += val

    res = pl.pallas_call(
        kern, name=name, grid=(geo.r // tm,), in_specs=in_specs + carrier.in_specs(), out_specs=out_specs + carrier.out_specs(),
        out_shape=out_shapes + carrier.out_shapes(), scratch_shapes=carrier.scratch(), input_output_aliases=carrier.aliases(),
        compiler_params=_cparams("arbitrary"),
    )(*args, *carrier.operands())
    own, extra = carrier.results(res)
    if job:
        return (*own, extra)
    return own[0] if len(own) == 1 else own


def _colsum(v):
    return jnp.sum(v, axis=0, keepdims=True)


def _norm_mod(geo, z, gain, mod, off, name):
    d = D_MODEL

    def body(i, zv, g, m):
        r = lax.rsqrt(jnp.mean(zv * zv, axis=-1, keepdims=True) + EPS)
        return (zv * r) * g * (1.0 + m[:, off + d:off + 2 * d]) + m[:, off:off + d]

    return _rowwise(name, body, geo, 256, [(z, "row"), (gain, "full"), (mod, "ex")], [("row", d, BF16)])


def _norm_mod_bwd(geo, z, gain, mod, off, dh, dz_skip, name, gated=None, latent_only=False, job=None):
    d = D_MODEL

    def body(i, zv, g, m, dhv, skip, *rest):
        r = lax.rsqrt(jnp.mean(zv * zv, axis=-1, keepdims=True) + EPS)
        n = zv * r
        dng = dhv * (1.0 + m[:, off + d:off + 2 * d])
        dn = dng * g
        dz = r * (dn - n * jnp.mean(dn * n, axis=-1, keepdims=True)) + skip
        res = (dz, _colsum(dhv), _colsum(dhv * (n * g)), _colsum(dng * n))
        if gated:
            ov, gm = rest
            res += (dz * gm[:, gated[2]:gated[2] + d], _colsum(dz * ov))
        return res

    ins = [(z, "row"), (gain, "full"), (mod, "ex"), (dh, "row"), (dz_skip, "row")]
    outs = [("xrow" if latent_only else "row", d, F32), ("exacc", d), ("exacc", d), ("gacc", 1, d)]
    if gated:
        ins += [(gated[0], "row"), (gated[1], "ex")]
        outs += [("row", d, BF16), ("exacc", d)]
    return _rowwise(name, body, geo, 256, ins, outs, job)


def _loss_head(geo, z, target, out, mod, off, name):
    seg_blocks, x_blocks = geo.seg // 256, geo.s // 256
    d = D_MODEL

    def body(i, zv, tv, ov, m):
        keep = jnp.where(i % seg_blocks >= x_blocks, 0.0, 1.0)
        err = (zv - tv) * keep
        part = 0.5 * jnp.sum(jnp.mean(err * err, axis=-1, keepdims=True), axis=0, keepdims=True)
        dz = err * (1.0 / d)
        return dz, jnp.broadcast_to(part, (1, LANES)), dz * m[:, off:off + d], _colsum(dz * ov)

    return _rowwise(name, body, geo, 256, [(z, "row"), (target, "xrow"), (out, "row"), (mod, "ex")],
                    [("row", d, F32), ("gacc", 1, LANES), ("row", d, BF16), ("exacc", d)])


Q_SCALE = HEAD_DIM ** -0.5
N_QK_CHUNKS = (N_HEADS + N_KV_HEADS) * HEAD_DIM // LANES
N_Q_CHUNKS = N_HEADS * HEAD_DIM // LANES


def _attn_prep(geo, proj, cos, sin_signed, q_gain, k_gain, name):
    def body(i, p, cs, sn, qg, kg):
        outs = []
        for ch in range(N_QK_CHUNKS):
            is_q = ch < N_Q_CHUNKS
            outs.append(_qk_chunk(p[:, ch * LANES:(ch + 1) * LANES], qg if is_q else kg, cs, sn, Q_SCALE if is_q else 1.0))
        outs.append(p[:, N_QK_CHUNKS * LANES:])
        return jnp.concatenate(outs, axis=1)

    return _rowwise(name, body, geo, 256, [(proj, "row"), (cos, "tab"), (sin_signed, "tab"), (q_gain, "full"), (k_gain, "full")],
                    [("row", proj.shape[1], BF16)])


def _attn_prep_bwd(geo, proj, cos, sin_signed, q_gain, k_gain, dq, dkv, name):
    kw = N_KV_HEADS * HEAD_DIM

    def body(i, p, cs, sn, qg, kg, dqv, dkvv):
        outs = []
        dgains = [jnp.zeros((1, LANES), F32), jnp.zeros((1, LANES), F32)]
        for ch in range(N_QK_CHUNKS):
            is_q = ch < N_Q_CHUNKS
            scale = Q_SCALE if is_q else 1.0
            ct = dqv[:, ch * LANES:(ch + 1) * LANES] if is_q else dkvv[:, (ch - N_Q_CHUNKS) * LANES:(ch - N_Q_CHUNKS + 1) * LANES]
            _, vjp = jax.vjp(lambda xx, gg, scale=scale: _qk_chunk(xx, gg, cs, sn, scale),
                             p[:, ch * LANES:(ch + 1) * LANES], qg if is_q else kg)
            dx, dg = vjp(ct)
            outs.append(dx)
            dgains[0 if is_q else 1] = dgains[0 if is_q else 1] + dg
        outs.append(dkvv[:, kw:])
        return jnp.concatenate(outs, axis=1), dgains[0], dgains[1]

    return _rowwise(name, body, geo, 256,
                    [(proj, "row"), (cos, "tab"), (sin_signed, "tab"), (q_gain, "full"), (k_gain, "full"), (dq, "row"), (dkv, "row")],
                    [("row", proj.shape[1], BF16), ("gacc", 1, LANES), ("gacc", 1, LANES)])


def _attn_geometry(geo):
    assert geo.s % ATTN_BLOCK == 0 and geo.l % ATTN_BLOCK == 0 and geo.seg >= BAND
    return geo.seg // ATTN_BLOCK, geo.s // ATTN_BLOCK


def _attn_mask(j, s0, geo):
    r = lax.broadcasted_iota(jnp.int32, (ATTN_BLOCK, geo.l + BAND), 0)
    n = lax.broadcasted_iota(jnp.int32, (ATTN_BLOCK, geo.l + BAND), 1) - geo.l
    dist = (s0 - j * ATTN_BLOCK) + n - r
    return (n < 0) | ((jnp.abs(dist) <= WINDOW) & (s0 + n < geo.s))


def _attn_probs(q, keys, valid, n_ctx, sink):
    s = _dot(q, keys, "nt")
    if valid is not None:
        s = jnp.where(valid, s, NEG_INF)
    m = jnp.maximum(jnp.max(s, axis=-1, keepdims=True), sink)
    e, e_sink = jnp.exp(s - m), jnp.exp(sink - m)
    inv = 1.0 / (jnp.sum(e, axis=-1, keepdims=True) + e_sink)
    return e * inv, e_sink * inv


def _attn_keys(ref, s0, geo, with_band):
    ctx = ref[geo.s:geo.seg, :]
    return jnp.concatenate([ctx, ref[pl.ds(s0, BAND), :]], axis=0) if with_band else ctx


def _attention(geo, qkv, sink, name, job=None):
    n_blocks, n_x_blocks = _attn_geometry(geo)
    qw, kw = N_HEADS * HEAD_DIM, N_KV_HEADS * HEAD_DIM
    group = N_HEADS // N_KV_HEADS
    carrier = _Carrier(job, 4, 1, 0)

    def kern(*refs):
        (sink_ref, q_ref, k_ref, v_ref, o_ref), job_refs = carrier.split(refs)
        j = pl.program_id(1)
        carrier.run(job_refs, pl.program_id(0) * n_blocks + j, geo.b * n_blocks)
        s0 = pl.multiple_of(jnp.clip((j - 1) * ATTN_BLOCK, 0, geo.seg - BAND), ATTN_BLOCK)

        def heads(with_band):
            valid = _attn_mask(j, s0, geo) if with_band else None
            k_all, v_all = _attn_keys(k_ref, s0, geo, with_band), _attn_keys(v_ref, s0, geo, with_band)
            for h in range(N_HEADS):
                kv = slice((h // group) * HEAD_DIM, (h // group + 1) * HEAD_DIM)
                p, _ = _attn_probs(q_ref[:, h * HEAD_DIM:(h + 1) * HEAD_DIM], k_all[:, kv], valid, geo.l, sink_ref[h])
                o_ref[:, h * HEAD_DIM:(h + 1) * HEAD_DIM] = _dot(p, v_all[:, kv], "nn").astype(BF16)

        pl.when(j < n_x_blocks)(lambda: heads(True))
        pl.when(j >= n_x_blocks)(lambda: heads(False))

    res = pl.pallas_call(
        kern, name=name, grid=(geo.b, n_blocks),
        in_specs=[pl.BlockSpec(memory_space=pltpu.SMEM),
                  pl.BlockSpec((ATTN_BLOCK, qw), lambda b, j: (b * n_blocks + j, 0)),
                  pl.BlockSpec((geo.seg, kw), lambda b, j: (b, qw // kw)),
                  pl.BlockSpec((geo.seg, kw), lambda b, j: (b, qw // kw + 1))] + carrier.in_specs(),
        out_specs=[pl.BlockSpec((ATTN_BLOCK, qw), lambda b, j: (b * n_blocks + j, 0))] + carrier.out_specs(),
        out_shape=[jax.ShapeDtypeStruct((geo.r, qw), BF16)] + carrier.out_shapes(),
        scratch_shapes=carrier.scratch(), input_output_aliases=carrier.aliases(),
        compiler_params=_cparams("arbitrary", "arbitrary"),
    )(sink, qkv, qkv, qkv, *carrier.operands())
    (o,), extra = carrier.results(res)
    return o, extra


def _attention_bwd(geo, qkv, sink, do, name, job=None):
    n_blocks, n_x_blocks = _attn_geometry(geo)
    qw, kw = N_HEADS * HEAD_DIM, N_KV_HEADS * HEAD_DIM
    group = N_HEADS // N_KV_HEADS

    carrier = _Carrier(job, 5, 3, 1)

    def kern(*refs):
        (sink_ref, q_ref, k_ref, v_ref, do_ref, dq_ref, dkv_out_ref, dsink_ref, dkv_ref), job_refs = carrier.split(refs)
        b, j = pl.program_id(0), pl.program_id(1)
        carrier.run(job_refs, b * n_blocks + j, geo.b * n_blocks)
        s0 = pl.multiple_of(jnp.clip((j - 1) * ATTN_BLOCK, 0, geo.seg - BAND), ATTN_BLOCK)

        @pl.when(j == 0)
        def _():
            dkv_ref[...] = jnp.zeros_like(dkv_ref)

        @pl.when((j == 0) & (b == 0))
        def _():
            dsink_ref[...] = jnp.zeros_like(dsink_ref)

        def heads(with_band):
            valid = _attn_mask(j, s0, geo) if with_band else None
            k_all, v_all = _attn_keys(k_ref, s0, geo, with_band), _attn_keys(v_ref, s0, geo, with_band)
            for g in range(N_KV_HEADS):
                kv = slice(g * HEAD_DIM, (g + 1) * HEAD_DIM)
                keys, vals = k_all[:, kv], v_all[:, kv]
                group_heads = [slice(h * HEAD_DIM, (h + 1) * HEAD_DIM) for h in range(g * group, (g + 1) * group)]
                ds_rows, p_rows = [], []
                for h, hs in zip(range(g * group, (g + 1) * group), group_heads):
                    dout = do_ref[:, hs]
                    p, p_sink = _attn_probs(q_ref[:, hs], keys, valid, geo.l, sink_ref[h])
                    dp = _dot(dout, vals, "nt")
                    dsum = jnp.sum(p * dp, axis=-1, keepdims=True)
                    ds = (p * (dp - dsum)).astype(BF16)
                    dq_ref[:, hs] = _dot(ds, keys, "nn").astype(BF16)
                    ds_rows.append(ds)
                    p_rows.append(p.astype(BF16))
                    dsink_ref[h:h + 1, :] += jnp.broadcast_to(-jnp.sum(p_sink * dsum, axis=0, keepdims=True), (1, LANES))
                q_rows = jnp.concatenate([q_ref[:, hs] for hs in group_heads], axis=0)
                do_rows = jnp.concatenate([do_ref[:, hs] for hs in group_heads], axis=0)
                dk = _dot(jnp.concatenate(ds_rows, axis=0), q_rows, "tn")
                dv = _dot(jnp.concatenate(p_rows, axis=0), do_rows, "tn")
                vv = slice(kw + g * HEAD_DIM, kw + (g + 1) * HEAD_DIM)
                dkv_ref[geo.s:geo.seg, kv] += dk[:geo.l]
                dkv_ref[geo.s:geo.seg, vv] += dv[:geo.l]
                if with_band:
                    dkv_ref[pl.ds(s0, BAND), kv] += dk[geo.l:]
                    dkv_ref[pl.ds(s0, BAND), vv] += dv[geo.l:]

        pl.when(j < n_x_blocks)(lambda: heads(True))
        pl.when(j >= n_x_blocks)(lambda: heads(False))

        @pl.when(j == n_blocks - 1)
        def _():
            dkv_out_ref[...] = dkv_ref[...].astype(BF16)

    res = pl.pallas_call(
        kern, name=name, grid=(geo.b, n_blocks),
        in_specs=[pl.BlockSpec(memory_space=pltpu.SMEM),
                  pl.BlockSpec((ATTN_BLOCK, qw), lambda b, j: (b * n_blocks + j, 0)),
                  pl.BlockSpec((geo.seg, kw), lambda b, j: (b, qw // kw)),
                  pl.BlockSpec((geo.seg, kw), lambda b, j: (b, qw // kw + 1)),
                  pl.BlockSpec((ATTN_BLOCK, qw), lambda b, j: (b * n_blocks + j, 0))] + carrier.in_specs(),
        out_specs=[pl.BlockSpec((ATTN_BLOCK, qw), lambda b, j: (b * n_blocks + j, 0)),
                   pl.BlockSpec((geo.seg, 2 * kw), lambda b, j: (b, 0)),
                   pl.BlockSpec((N_HEADS, LANES), lambda b, j: (0, 0))] + carrier.out_specs(),
        out_shape=[jax.ShapeDtypeStruct((geo.r, qw), BF16), jax.ShapeDtypeStruct((geo.r, 2 * kw), BF16),
                   jax.ShapeDtypeStruct((N_HEADS, LANES), F32)] + carrier.out_shapes(),
        scratch_shapes=[pltpu.VMEM((geo.seg, 2 * kw), F32)] + carrier.scratch(), input_output_aliases=carrier.aliases(),
        compiler_params=_cparams("arbitrary", "arbitrary"),
    )(sink, qkv, qkv, qkv, do, *carrier.operands())
    (dq, dkv, dsink), extra = carrier.results(res)
    return dq, dkv, dsink, extra


RET_QK_W = RET_HEADS * RET_QK_DIM
K_SCALE = RET_QK_DIM ** -0.5


def _ret_prep(geo, proj, cos, sin_signed, name):
    def body(i, p, cs, sn):
        cs2, sn2 = jnp.concatenate([cs] * RET_HEADS, axis=1), jnp.concatenate([sn] * RET_HEADS, axis=1)
        q = _rope(p[:, :RET_QK_W], cs2, sn2, RET_QK_DIM // 4)
        k = _rope(p[:, RET_QK_W:2 * RET_QK_W], cs2, sn2, RET_QK_DIM // 4) * K_SCALE
        return jnp.concatenate([q, k, p[:, 2 * RET_QK_W:]], axis=1)

    return _rowwise(name, body, geo, 128, [(proj, ("rowc", 2 * RET_QK_W + RET_VWIDTH, 0)), (cos, "tab"), (sin_signed, "tab")],
                    [("row", 2 * RET_QK_W + RET_VWIDTH, BF16)])


def _ret_prep_bwd(geo, dq, dk, dv, dgate, cos, sin_signed, name):
    def body(i, dqv, dkv, dvv, dg, cs, sn):
        cs2, sn2 = jnp.concatenate([cs] * RET_HEADS, axis=1), jnp.concatenate([sn] * RET_HEADS, axis=1)
        dkv = dkv * K_SCALE
        dqv = dqv * cs2 + _swap_halves(dqv * sn2, RET_QK_DIM // 4)
        dkv = dkv * cs2 + _swap_halves(dkv * sn2, RET_QK_DIM // 4)
        return jnp.concatenate([dqv, dkv, dvv, dg], axis=1)

    return _rowwise(name, body, geo, 128,
                    [(dq, "row"), (dk, "row"), (dv, "row"), (dgate, "row"), (cos, "tab"), (sin_signed, "tab")],
                    [("row", 2 * RET_QK_W + 2 * RET_VWIDTH, BF16)])


def _ret_step(state, q, k, v, lg, rev):
    c = RET_CHUNK
    ri = lax.broadcasted_iota(jnp.int32, (c, 1), 0).astype(F32)
    cj = lax.broadcasted_iota(jnp.int32, (1, c), 1).astype(F32)
    if rev:
        dist, q_decay, k_decay = cj - ri, jnp.exp(lg * (c - ri)), jnp.exp(lg * ri)
    else:
        dist, q_decay, k_decay = ri - cj, jnp.exp(lg * (ri + 1.0)), jnp.exp(lg * (c - 1.0 - ri))
    intra = jnp.where(dist >= 0, jnp.exp(lg * jnp.maximum(dist, 0.0)), 0.0)
    scores = _mm(q, k, "nt") * intra
    out = _mm(scores, v, "nn") + _mm(q, state, "nn") * q_decay
    new_state = state * jnp.exp(lg * c) + _mm(k * k_decay, v, "tn")
    return new_state, out


def _ret_state0(kc, vc, lg, rev):
    n = kc.shape[0]
    t = lax.broadcasted_iota(jnp.int32, (n, 1), 0).astype(F32)
    decay = jnp.exp(lg * t) if rev else jnp.exp(lg * (n - 1.0 - t))
    return _mm(kc * decay, vc, "tn")


def _ret_specs(geo):
    nq = RET_HEADS
    return [pl.BlockSpec((2 * RET_HEADS, LANES), lambda b, h: (0, 0)),
            pl.BlockSpec((geo.seg, RET_QK_DIM), lambda b, h: (b, h)),
            pl.BlockSpec((geo.seg, RET_QK_DIM), lambda b, h: (b, nq + h)),
            pl.BlockSpec((geo.seg, RET_V_DIM), lambda b, h: (b, nq + h))]


def _retention(geo, qkv, log_g, name):
    nc = geo.s // RET_CHUNK

    def kern(lg_ref, q_ref, k_ref, v_ref, o_ref, st_ref):
        h = pl.program_id(1)
        for d, rev in ((0, False), (1, True)):
            lg = lg_ref[pl.ds(d * RET_HEADS + h, 1), 0:1]
            st_ref[...] = _ret_state0(k_ref[geo.s:geo.seg, :].astype(F32), v_ref[geo.s:geo.seg, :].astype(F32), lg, rev)

            def chunk(ci, carry, d=d, rev=rev, lg=lg):
                r0 = pl.multiple_of((nc - 1 - ci if rev else ci) * RET_CHUNK, RET_CHUNK)
                rows = pl.ds(r0, RET_CHUNK)
                new_state, out = _ret_step(st_ref[...], q_ref[rows, :], k_ref[rows, :], v_ref[rows, :], lg, rev)
                st_ref[...] = new_state
                if d == 0:
                    o_ref[rows, :] = out
                else:
                    o_ref[rows, :] += out
                return carry

            lax.fori_loop(0, nc, chunk, 0)
        o_ref[geo.s:geo.seg, :] = jnp.zeros((geo.l, RET_V_DIM), F32)

    return pl.pallas_call(
        kern, name=name, grid=(geo.b, RET_HEADS), in_specs=_ret_specs(geo),
        out_specs=pl.BlockSpec((geo.seg, RET_V_DIM), lambda b, h: (b, h)),
        out_shape=jax.ShapeDtypeStruct((geo.r, RET_VWIDTH), F32),
        scratch_shapes=[pltpu.VMEM((RET_QK_DIM, RET_V_DIM), F32)],
        compiler_params=_cparams("parallel", "arbitrary"),
    )(log_g, qkv, qkv, qkv)


def _retention_bwd(geo, qkv, log_g, do, name):
    nc = geo.s // RET_CHUNK
    ctx = slice(geo.s, geo.seg)

    def kern(lg_ref, q_ref, k_ref, v_ref, do_ref, dq_ref, dk_ref, dv_ref, dlg_ref, states_ref, dst_ref, aq_ref, ak_ref, av_ref):
        b, h = pl.program_id(0), pl.program_id(1)

        @pl.when((b == 0) & (h == 0))
        def _():
            dlg_ref[...] = jnp.zeros_like(dlg_ref)

        for d, rev in ((0, False), (1, True)):
            row = pl.ds(d * RET_HEADS + h, 1)
            lg = lg_ref[row, 0:1]
            kc, vc = k_ref[ctx, :].astype(F32), v_ref[ctx, :].astype(F32)
            states_ref[0] = _ret_state0(kc, vc, lg, rev)

            def rows_of(ci, rev=rev):
                return pl.ds(pl.multiple_of((nc - 1 - ci if rev else ci) * RET_CHUNK, RET_CHUNK), RET_CHUNK)

            def load(rows):
                return q_ref[rows, :].astype(F32), k_ref[rows, :].astype(F32), v_ref[rows, :].astype(F32)

            def replay(ci, carry, rev=rev, lg=lg, rows_of=rows_of, load=load):
                states_ref[ci + 1] = _ret_step(states_ref[ci], *load(rows_of(ci)), lg, rev)[0]
                return carry

            lax.fori_loop(0, nc - 1, replay, 0)
            dst_ref[...] = jnp.zeros_like(dst_ref)

            def emit(rows, dq, dk, dv, d=d):
                if d == 0:
                    ak_ref[rows, :], av_ref[rows, :] = dk, dv
                    if dq is not None:
                        aq_ref[rows, :] = dq
                else:
                    dk_ref[rows, :] = (ak_ref[rows, :] + dk).astype(BF16)
                    dv_ref[rows, :] = (av_ref[rows, :] + dv).astype(BF16)
                    if dq is not None:
                        dq_ref[rows, :] = (aq_ref[rows, :] + dq).astype(BF16)

            def back(t, dlg, rev=rev, lg=lg, rows_of=rows_of, load=load, emit=emit):
                ci = nc - 1 - t
                rows = rows_of(ci)
                _, vjp = jax.vjp(lambda st, q, k, v, g: _ret_step(st, q, k, v, g, rev), states_ref[ci], *load(rows), lg)
                dstate, dq, dk, dv, dg = vjp((dst_ref[...], do_ref[rows, :].astype(F32)))
                dst_ref[...] = dstate
                emit(rows, dq, dk, dv)
                return dlg + dg

            dlg = lax.fori_loop(0, nc, back, jnp.zeros((1, 1), F32))
            _, vjp = jax.vjp(lambda kk, vv, g: _ret_state0(kk, vv, g, rev), kc, vc, lg)
            dkc, dvc, dg = vjp(dst_ref[...])
            emit(ctx, None, dkc, dvc)
            dlg_ref[row, :] += jnp.broadcast_to(dlg + dg, (1, LANES))
        dq_ref[ctx, :] = jnp.zeros((geo.l, RET_QK_DIM), BF16)

    nq = RET_HEADS
    return pl.pallas_call(
        kern, name=name, grid=(geo.b, RET_HEADS),
        in_specs=_ret_specs(geo) + [pl.BlockSpec((geo.seg, RET_V_DIM), lambda b, h: (b, h))],
        out_specs=[pl.BlockSpec((geo.seg, RET_QK_DIM), lambda b, h: (b, h)),
                   pl.BlockSpec((geo.seg, RET_QK_DIM), lambda b, h: (b, h)),
                   pl.BlockSpec((geo.seg, RET_V_DIM), lambda b, h: (b, h)),
                   pl.BlockSpec((2 * RET_HEADS, LANES), lambda b, h: (0, 0))],
        out_shape=[jax.ShapeDtypeStruct((geo.r, RET_QK_W), BF16), jax.ShapeDtypeStruct((geo.r, RET_QK_W), BF16),
                   jax.ShapeDtypeStruct((geo.r, RET_VWIDTH), BF16), jax.ShapeDtypeStruct((2 * RET_HEADS, LANES), F32)],
        scratch_shapes=[pltpu.VMEM((nc, RET_QK_DIM, RET_V_DIM), F32), pltpu.VMEM((RET_QK_DIM, RET_V_DIM), F32),
                        pltpu.VMEM((geo.seg, RET_QK_DIM), F32), pltpu.VMEM((geo.seg, RET_QK_DIM), F32),
                        pltpu.VMEM((geo.seg, RET_V_DIM), F32)],
        compiler_params=_cparams("arbitrary", "arbitrary"),
    )(log_g, qkv, qkv, qkv, do)


def _gated(o, g, gain):
    outs = []
    for h in range(RET_HEADS):
        cols = slice(h * RET_V_DIM, (h + 1) * RET_V_DIM)
        oh = o[:, cols]
        mu = jnp.mean(oh, axis=-1, keepdims=True)
        var = jnp.mean(jnp.square(oh - mu), axis=-1, keepdims=True)
        outs.append(_silu(g[:, cols]) * ((oh - mu) * lax.rsqrt(var + EPS) * gain[:, cols]))
    return jnp.concatenate(outs, axis=1)


def _ret_gated(geo, o, proj, gain, name):
    def body(i, ov, gv, gn):
        return _gated(ov, gv, gn)

    gate_block = (2 * RET_QK_W + RET_VWIDTH) // RET_VWIDTH
    return _rowwise(name, body, geo, 128, [(o, "row"), (proj, ("rowc", RET_VWIDTH, gate_block)), (gain, "full")],
                    [("row", RET_VWIDTH, BF16)])


def _ret_gated_bwd(geo, o, proj, gain, dout, name):
    def body(i, ov, gv, gn, dv):
        _, vjp = jax.vjp(_gated, ov, gv, gn)
        return vjp(dv)

    gate_block = (2 * RET_QK_W + RET_VWIDTH) // RET_VWIDTH
    return _rowwise(name, body, geo, 128,
                    [(o, "row"), (proj, ("rowc", RET_VWIDTH, gate_block)), (gain, "full"), (dout, "row")],
                    [("row", RET_VWIDTH, BF16), ("row", RET_VWIDTH, BF16), ("gacc", 1, RET_VWIDTH)])


def _whole(name, fn, out_shapes, *arrays):
    n = len(arrays)

    def kern(*refs):
        res = fn(*[r[...] for r in refs[:n]])
        for ref, val in zip(refs[n:], res):
            ref[...] = val.astype(ref.dtype)

    return pl.pallas_call(kern, name=name, out_shape=out_shapes)(*arrays)


def _rope_tables(geo, head_dim):
    rows = geo.s // GRID_W
    row = jnp.broadcast_to(jnp.arange(rows, dtype=jnp.int32)[:, None], (rows, GRID_W)).reshape(geo.s)
    col = jnp.broadcast_to(jnp.arange(GRID_W, dtype=jnp.int32)[None, :], (rows, GRID_W)).reshape(geo.s)
    axis_dim = head_dim // 2
    inv = ROPE_BASE ** (-jnp.arange(0, axis_dim, 2, dtype=F32) / axis_dim)
    ang_r = row.astype(F32)[:, None] * inv
    ang_c = col.astype(F32)[:, None] * inv
    cos = jnp.concatenate([jnp.cos(ang_r)] * 2 + [jnp.cos(ang_c)] * 2, axis=1)
    sin = jnp.concatenate([-jnp.sin(ang_r), jnp.sin(ang_r), -jnp.sin(ang_c), jnp.sin(ang_c)], axis=1)
    cos = jnp.concatenate([cos, jnp.ones((geo.l, head_dim), F32)], axis=0)
    sin = jnp.concatenate([sin, jnp.zeros((geo.l, head_dim), F32)], axis=0)
    reps = max(1, LANES // head_dim)
    return jnp.tile(cos, (1, reps)), jnp.tile(sin, (1, reps))


def _row_tile(r):
    return next(t for t in (1024, 512, 256, 128) if r % t == 0)


MOD_ROWS = 8


def _local_step(x, ctx, target, sp, wts, mods, plan=None):
    nb, s, d = x.shape
    geo = _Rows(nb, s, ctx.shape[1])
    assert nb + 1 <= MOD_ROWS and d == D_MODEL
    tm = _row_tile(geo.r)
    z = jnp.concatenate([x, ctx], axis=1).reshape(geo.r, d)
    cos64, sin64 = _rope_tables(geo, HEAD_DIM)
    cos256, sin256 = _rope_tables(geo, RET_QK_DIM)
    q_gain = jnp.tile(sp["q_norm"].reshape(1, HEAD_DIM), (1, LANES // HEAD_DIM))
    k_gain = jnp.tile(sp["k_norm"].reshape(1, HEAD_DIM), (1, LANES // HEAD_DIM))
    sink = sp["sink"].reshape(N_HEADS)
    log_g = jnp.broadcast_to(sp["log_g"].reshape(2 * RET_HEADS, 1), (2 * RET_HEADS, LANES))
    gn_g = sp["gn_g"].reshape(1, RET_VWIDTH)

    saved = []
    h1 = _norm_mod(geo, z, sp["norm1_g"][0][None, :], mods[0], 0, "norm1_0")
    for i in range(2):
        mod3 = mods[i]
        n1, n2 = sp["norm1_g"][i][None, :], sp["norm2_g"][i][None, :]
        if i == 0:
            proj = _mm_nn(h1, wts["attn_qkv"], F32, "attn_qkv", tm, wts["attn_qkv"].shape[1], d)
            prep = _attn_prep(geo, proj, cos64, sin64, q_gain, k_gain, "attn_prep")
            o, late = _attention(geo, prep, sink, "attn", plan.gather_job() if plan else None)
            if plan:
                plan.late_weights(late, wts)
            oraw = None
            w_o = wts["attn_o"]
        else:
            proj = _mm_nn(h1, wts["ret_qkvg"], BF16, "ret_qkvg", tm, wts["ret_qkvg"].shape[2], d)
            prep = _ret_prep(geo, proj, cos256, sin256, "ret_prep")
            oraw = _retention(geo, prep, log_g, "ret")
            o = _ret_gated(geo, oraw, proj, gn_g, "ret_gated")
            w_o = wts["ret_o"]
        zmid, mix, h2 = _mm_nn_gate_residual(geo, o, w_o, z, mod3, 2 * d, f"mix_out{i}", norm=(n2, mod3, 3 * d))
        u, a = _ffn_in_swiglu(h2, wts["ffn_in"][i], f"ffn_in{i}")
        next_norm = (sp["norm1_g"][1][None, :], mods[1], 0) if i == 0 else None
        zout, f, h1_next = _mm_nn_gate_residual(geo, a, wts["ffn_out"][i], zmid, mod3, 5 * d, f"ffn_out{i}", norm=next_norm)
        saved.append(dict(z=z, mod3=mod3, n1=n1, n2=n2, h1=h1, proj=proj, prep=prep, o=o, oraw=oraw, mix=mix, zmid=zmid,
                          h2=h2, u=u, a=a, f=f))
        z, h1 = zout, h1_next

    dz, loss, df, dg2 = _loss_head(geo, z, target.reshape(nb * s, d), saved[1]["f"], saved[1]["mod3"], 5 * d, "loss")

    big, small = {}, {}
    dmods = [None, None]
    for i in (1, 0):
        sv = saved[i]
        mod3 = sv["mod3"]
        carry = plan is not None and i == 0
        du, land = _ffn_out_dx_swiglu_bwd(df, wts["ffn_out"][i], sv["u"], f"ffn_out_dx{i}", plan.layer1.swap_job() if carry else None)
        if carry:
            plan.layer1.after_swap(land)
        big[f"ffn_out{i}"] = _mm_tn(sv["a"], df, f"ffn_out_dw{i}", D_FF // 2, 1024, tm).reshape(N_CHIPS, D_FF // N_CHIPS, d)
        n4 = wts["ffn_in"][i].shape[2]
        dh2 = _mm_nt(du, wts["ffn_in"][i], BF16, f"ffn_in_dx{i}", tm, 1024, n4)
        big[f"ffn_in{i}"] = _mm_tn(sv["h2"], du, f"ffn_in_dw{i}", 1024, n4, tm, shards=N_CHIPS)
        if carry:
            plan.start_layer0_ffn(big)
        dzmid, dsh2, dsc2, dn2, dmix, dg1, *land = _norm_mod_bwd(geo, sv["zmid"], sv["n2"], mod3, 3 * d, dh2, dz, f"norm2_bwd{i}",
                                                                 gated=(sv["mix"], mod3, 2 * d),
                                                                 job=plan.layer0_ffn.swap_job() if carry else None)
        if carry:
            plan.layer0_ffn.after_swap(land[0])
        if i == 0:
            do = _mm_nt(dmix, wts["attn_o"], BF16, "attn_out_dx", tm, 1024, 1024)
            big["attn_o"] = _mm_tn(sv["o"], dmix, "attn_out_dw", 1024, 1024, tm).reshape(N_CHIPS, 1024 // N_CHIPS, d)
            dq, dkv, dsink, land = _attention_bwd(geo, sv["prep"], sink, do, "attn_bwd", plan.exchange_job() if plan else None)
            if plan:
                plan.after_exchange(land)
            dproj, dqg, dkg = _attn_prep_bwd(geo, sv["proj"], cos64, sin64, q_gain, k_gain, dq, dkv, "attn_prep_bwd")
            small["q_norm"] = dqg[0, :HEAD_DIM] + dqg[0, HEAD_DIM:]
            small["k_norm"] = dkg[0, :HEAD_DIM] + dkg[0, HEAD_DIM:]
            small["sink"] = dsink[:, 0]
            wq = wts["attn_qkv"]
            dh1 = _mm_nt(dproj, wq, BF16, "attn_qkv_dx", tm, 1024, wq.shape[1])
            dwq = _mm_tn(sv["h1"], dproj, "attn_qkv_dw", 1024, wq.shape[1], tm)
            big["attn_qkv"] = dwq.reshape(d, N_CHIPS, -1).transpose(1, 0, 2)
        else:
            do = _mm_nt(dmix, wts["ret_o"], BF16, "ret_out_dx", tm, 1024, 1024)
            big["ret_o"] = _mm_tn(sv["o"], dmix, "ret_out_dw", 1024, 1024, tm).reshape(N_CHIPS, RET_VWIDTH // N_CHIPS, d)
            doraw, dgate, dgn = _ret_gated_bwd(geo, sv["oraw"], sv["proj"], gn_g, do, "ret_gated_bwd")
            small["gn_g"] = dgn[0]
            dq, dk, dv, dlg = _retention_bwd(geo, sv["prep"], log_g, doraw, "ret_bwd")
            small["log_g"] = dlg[:, 0].reshape(2, RET_HEADS)
            dproj = _ret_prep_bwd(geo, dq, dk, dv, dgate, cos256, sin256, "ret_prep_bwd")
            wq = wts["ret_qkvg"]
            dh1 = _mm_nt(dproj, wq, BF16, "ret_qkvg_dx", tm, 1024, wq.shape[2])
            big["ret_qkvg"] = _mm_tn(sv["h1"], dproj, "ret_qkvg_dw", 1024, wq.shape[2], tm, shards=N_CHIPS)
        below = (saved[0]["f"], saved[0]["mod3"], 5 * d) if i == 1 else None
        dz, dsh1, dsc1, dn1, *below_grads = _norm_mod_bwd(geo, sv["z"], sv["n1"], mod3, 0, dh1, dzmid, f"norm1_bwd{i}", gated=below,
                                                              latent_only=i == 0)
        small[f"norm1_g{i}"], small[f"norm2_g{i}"] = dn1[0], dn2[0]
        parts = [dsh1, dsc1, dg1, dsh2, dsc2, dg2]
        rows = jnp.concatenate([jnp.concatenate([p[:nb, 0, :] for p in parts], axis=1),
                                jnp.concatenate([jnp.sum(p[nb:, 0, :], axis=0, keepdims=True) for p in parts], axis=1),
                                jnp.zeros((MOD_ROWS - nb - 1, 6 * d), F32)], axis=0)
        dmods[i] = rows
        if below_grads:
            df, dg2 = below_grads
        small[f"ada_b{i}"] = jnp.sum(rows, axis=0)
        if plan and i == 1:
            plan.start_layer1(big)
    return loss, dz, big, small, dmods


def _adamw(w, g, m, v, name):
    rows, cols = w.shape
    tr = next((t for t in (256, 128, 64, 32, 16, 8) if rows % t == 0), rows)
    c1 = 1.0 - ADAM_B1 ** ADAM_STEP
    c2 = 1.0 - ADAM_B2 ** ADAM_STEP

    def kern(w_ref, g_ref, m_ref, v_ref, d_ref, nm_ref, nv_ref):
        gv = g_ref[...]
        nm = ADAM_B1 * m_ref[...] + (1.0 - ADAM_B1) * gv
        nv = ADAM_B2 * v_ref[...] + (1.0 - ADAM_B2) * jnp.square(gv)
        d_ref[...] = -ADAM_LR * ((nm / c1) / (jnp.sqrt(nv / c2) + ADAM_EPS) + ADAM_WD * w_ref[...])
        nm_ref[...] = nm
        nv_ref[...] = nv

    spec = pl.BlockSpec((tr, cols), lambda i: (i, 0))
    return pl.pallas_call(
        kern, name=name, grid=(rows // tr,), in_specs=[spec] * 4, out_specs=[spec] * 3,
        out_shape=[jax.ShapeDtypeStruct(w.shape, F32)] * 3, compiler_params=_cparams("parallel"),
    )(w, g, m, v)


N_DEVICES = 8


def _mesh_pos():
    return lax.axis_index("x"), lax.axis_index("y"), lax.axis_index("c")


def _other_chips(x, y):
    return [(1 - x, y), (x, 1 - y), (1 - x, 1 - y)]


def _hbm(n):
    return [pl.BlockSpec(memory_space=pl.ANY)] * n


def _remote(src, dst, send_sem, recv_sem, device):
    return pltpu.make_async_remote_copy(src_ref=src, dst_ref=dst, send_sem=send_sem, recv_sem=recv_sem,
                                        device_id=device, device_id_type=MESH)


def _scalar_spec(grid, in_specs, out_specs):
    return pltpu.PrefetchScalarGridSpec(num_scalar_prefetch=1, grid=grid, in_specs=in_specs, out_specs=out_specs)


def _place_shard(param, layer, pos, name):
    _, r, cols = param.shape
    tr = _slab_tile(r)

    def kern(pos_ref, s_ref, o_ref):
        o_ref[...] = s_ref[...].astype(BF16)

    return pl.pallas_call(
        kern, name=name, out_shape=jax.ShapeDtypeStruct((N_CHIPS, r, cols), BF16),
        grid_spec=_scalar_spec((r // tr,), [pl.BlockSpec((None, tr, cols), lambda i, p: (layer, i, 0))],
                               pl.BlockSpec((None, tr, cols), lambda i, p: (p[1], i, 0))),
        compiler_params=_cparams("parallel"),
    )(pos, param)


class _CommJob:
    def __init__(self, inputs, out_shapes, aliases, sem_shapes, stages, fractions=None):
        self.inputs, self.out_shapes, self.aliases, self.sem_shapes, self.stages = inputs, out_shapes, aliases, sem_shapes, stages
        self.fractions = fractions


def _merge_jobs(a, b):
    assert len(a.stages) == len(b.stages)
    ni, no, ns = len(a.inputs), len(a.out_shapes), len(a.sem_shapes)

    def both(sa, sb):
        def stage(ins, outs, sems):
            sa(ins[:ni], outs[:no], sems[:ns])
            sb(ins[ni:], outs[no:], sems[ns:])
        return stage

    aliases = dict(a.aliases)
    aliases.update({ni + i: no + o for i, o in b.aliases.items()})
    return _CommJob(a.inputs + b.inputs, a.out_shapes + b.out_shapes, aliases, a.sem_shapes + b.sem_shapes,
                    [both(sa, sb) for sa, sb in zip(a.stages, b.stages)])


def _run_job(job, name):
    n_in, n_out = len(job.inputs), len(job.out_shapes)

    def body(*refs):
        for stage in job.stages:
            stage(refs[:n_in], refs[n_in:n_in + n_out], refs[n_in + n_out:])

    return pl.pallas_call(
        body, name=name, in_specs=_hbm(n_in), out_specs=_hbm(n_out), out_shape=job.out_shapes,
        input_output_aliases=job.aliases, scratch_shapes=job.sem_shapes,
    )(*job.inputs)


def _job_marks(job, steps):
    mid = len(job.stages) - 2
    fractions = job.fractions or [(s + 1) / (mid + 1) for s in range(mid)]
    return [0] + [min(steps - 1, 1 + int((steps - 1) * f)) for f in fractions] + [steps - 1]


def _gather_job(placed):
    n = len(placed)

    def half(w, which):
        r2 = placed[w].shape[1] // 2
        return pl.ds(which * r2, r2)

    def ici_copies(outs, sems, slot_of, arrays=range(n)):
        x, y, c = _mesh_pos()
        res = []
        for w in arrays:
            for k, (px, py) in enumerate(_other_chips(x, y)):
                slab = outs[w].at[slot_of(x, y, px, py), half(w, c)]
                res.append((slab, _remote(slab, slab, sems[0].at[w, k], sems[1].at[w, k], (px, py, c))))
        return res

    def forwards(outs, sems, which_core, arrays=range(n)):
        x, y, c = _mesh_pos()
        res = []
        for w in arrays:
            for k, (px, py) in enumerate(_other_chips(x, y)):
                slab = outs[w].at[2 * px + py, half(w, which_core(c))]
                res.append(_remote(slab, slab, sems[2].at[w, k], sems[3].at[w, k], (x, y, 1 - c)))
        return res

    def send(ins, outs, sems):
        for _, cp in ici_copies(outs, sems, lambda x, y, px, py: 2 * x + y):
            cp.start()

    def forward_of(w):
        def forward(ins, outs, sems):
            arrivals = ici_copies(outs, sems, lambda x, y, px, py: 2 * px + py, [w])
            for (_, arrival), fwd in zip(arrivals, forwards(outs, sems, lambda c: c, [w])):
                arrival.wait_recv()
                fwd.start()
        return forward

    def finish(ins, outs, sems):
        for cp in forwards(outs, sems, lambda c: 1 - c):
            cp.wait_recv()
        for _, cp in ici_copies(outs, sems, lambda x, y, px, py: 2 * x + y):
            cp.wait_send()
        for cp in forwards(outs, sems, lambda c: c):
            cp.wait_send()

    sizes = [p.shape[1] * p.shape[2] for p in placed]
    fractions = [sum(sizes[:w + 1]) / sum(sizes) for w in range(n)]
    return _CommJob(list(placed), [jax.ShapeDtypeStruct(p.shape, p.dtype) for p in placed], {w: w for w in range(n)},
                    [pltpu.SemaphoreType.DMA((n, 3))] * 4, [send] + [forward_of(w) for w in range(n)] + [finish], fractions)


def _pair_swap_job(grads):
    n = len(grads)

    def copies(ins, outs, sems):
        x, y, c = _mesh_pos()
        res = []
        for w in range(n):
            r2 = grads[w].shape[1] // 2
            res.append(_remote(ins[w].at[:, pl.ds((1 - c) * r2, r2)], outs[w], sems[0].at[w], sems[1].at[w], (x, y, 1 - c)))
        return res

    def send(ins, outs, sems):
        for cp in copies(ins, outs, sems):
            cp.start()

    def finish(ins, outs, sems):
        for cp in copies(ins, outs, sems):
            cp.wait()

    return _CommJob(list(grads), [jax.ShapeDtypeStruct((N_CHIPS, g.shape[1] // 2, g.shape[2]), F32) for g in grads], {},
                    [pltpu.SemaphoreType.DMA((n,))] * 2, [send, finish])


def _chip_exchange_job(hs):
    n = len(hs)

    def send(ins, outs, sems):
        x, y, c = _mesh_pos()
        for w in range(n):
            for k, (px, py) in enumerate(_other_chips(x, y)):
                _remote(ins[w].at[2 * px + py], outs[w].at[2 * x + y], sems[0].at[w, k], sems[1].at[w, k], (px, py, c)).start()

    def finish(ins, outs, sems):
        x, y, c = _mesh_pos()
        for w in range(n):
            for k, (px, py) in enumerate(_other_chips(x, y)):
                got = outs[w].at[2 * px + py]
                cp = _remote(ins[w].at[2 * px + py], got, sems[0].at[w, k], sems[1].at[w, k], (px, py, c))
                cp.wait_recv()
                cp.wait_send()

    return _CommJob(list(hs), [jax.ShapeDtypeStruct(h.shape, h.dtype) for h in hs], {},
                    [pltpu.SemaphoreType.DMA((n, 3))] * 2, [send, finish])


def _pair_share(ts, name):
    n = len(ts)

    def body(*refs):
        outs = refs[n:2 * n]
        send_sems, recv_sems = refs[2 * n:]
        x, y, c = _mesh_pos()
        sends = []
        for w in range(n):
            r2 = ts[w].shape[0] // 2
            mine = outs[w].at[pl.ds(c * r2, r2)]
            rc = _remote(mine, mine, send_sems.at[w], recv_sems.at[w], (x, y, 1 - c))
            rc.start()
            sends.append(rc)
        for w in range(n):
            r2 = ts[w].shape[0] // 2
            theirs = outs[w].at[pl.ds((1 - c) * r2, r2)]
            _remote(theirs, theirs, send_sems.at[w], recv_sems.at[w], (x, y, 1 - c)).wait_recv()
            sends[w].wait_send()

    return pl.pallas_call(
        body, name=name, in_specs=_hbm(n), out_specs=_hbm(n),
        out_shape=[jax.ShapeDtypeStruct(t.shape, F32) for t in ts],
        input_output_aliases={w: w for w in range(n)},
        scratch_shapes=[pltpu.SemaphoreType.DMA((n,))] * 2,
    )(*ts)


def _slab_tile(rows):
    return next(t for t in (512, 256, 176, 128, 64, 32, 16) if rows % t == 0)


def _sum_pair(grad, land, pos, name):
    _, r2, cols = land.shape
    tr = _slab_tile(r2)
    nt = r2 // tr

    def kern(pos_ref, a_ref, b_ref, o_ref):
        o_ref[...] = (a_ref[...] + b_ref[...]).astype(BF16)

    spec = pl.BlockSpec((None, tr, cols), lambda j, i, p: (j, i, 0))
    return pl.pallas_call(
        kern, name=name, out_shape=jax.ShapeDtypeStruct(land.shape, BF16),
        grid_spec=_scalar_spec((N_CHIPS, nt), [pl.BlockSpec((None, tr, cols), lambda j, i, p: (j, p[0] * nt + i, 0)), spec], spec),
        compiler_params=_cparams("parallel", "parallel"),
    )(pos, grad, land)


def _sum_chips(hs, land, pos, name):
    _, r2, cols = land.shape
    tr = _slab_tile(r2)
    nt = r2 // tr

    def kern(pos_ref, h_ref, l_ref, o_ref):
        acc = jnp.zeros((tr, cols), F32)
        own = h_ref[...].astype(F32)
        for k in range(N_CHIPS):
            acc = acc + jnp.where(pos_ref[1] == k, own, l_ref[k].astype(F32))
        o_ref[...] = acc

    return pl.pallas_call(
        kern, name=name, out_shape=jax.ShapeDtypeStruct((2 * r2, cols), F32),
        grid_spec=_scalar_spec((nt,), [pl.BlockSpec((None, tr, cols), lambda i, p: (p[1], i, 0)),
                                       pl.BlockSpec((N_CHIPS, tr, cols), lambda i, p: (0, i, 0))],
                               pl.BlockSpec((tr, cols), lambda i, p: (p[0] * nt + i, 0))),
        compiler_params=_cparams("parallel"),
    )(pos, hs, land)


class _ReduceScatter:
    def __init__(self, grads, pos, tag):
        self.grads, self.pos, self.tag = list(grads), pos, tag

    def swap_job(self):
        return _pair_swap_job(self.grads)

    def after_swap(self, land):
        self.hs = [_sum_pair(g, l, self.pos, f"grads_pair_sum_{self.tag}{w}") for w, (g, l) in enumerate(zip(self.grads, land))]

    def exchange_job(self):
        return _chip_exchange_job(self.hs)

    def after_exchange(self, land2):
        return [_sum_chips(h, l, self.pos, f"grads_chip_sum_{self.tag}{w}") for w, (h, l) in enumerate(zip(self.hs, land2))]

    def run(self):
        self.after_swap(_run_job(self.swap_job(), f"grads_pair_swap_{self.tag}"))
        return self.after_exchange(_run_job(self.exchange_job(), f"grads_chip_exchange_{self.tag}"))


EARLY_WEIGHTS = ("attn_qkv",)
LATE_WEIGHTS = ("ffn_in0", "ffn_in1", "ffn_out0", "ffn_out1", "attn_o", "ret_qkvg", "ret_o")
LAYER1_GRADS = ("ffn_out1", "ffn_in1", "ret_o", "ret_qkvg")
LAYER0_FFN_GRADS = ("ffn_out0", "ffn_in0")
LAST_GRADS = ("attn_o", "attn_qkv")


def _fill_weights(wts, full):
    for name, w in full.items():
        if name[:-1] == "ffn_in":
            wts[name[:-1]][int(name[-1])] = w
        elif name[:-1] == "ffn_out":
            wts["ffn_out"][int(name[-1])] = w.reshape(-1, w.shape[2])
        elif name in ("attn_o", "ret_o"):
            wts[name] = w.reshape(-1, w.shape[2])
        elif name == "attn_qkv":
            wts[name] = w.transpose(1, 0, 2).reshape(w.shape[1], -1)
        else:
            wts[name] = w


class _StepPlan:
    def __init__(self, placed, pos):
        self.placed, self.pos = placed, pos
        self.layer1 = self.layer0_ffn = None
        self.reduced = {}

    def gather_job(self):
        return _gather_job([self.placed[k] for k in LATE_WEIGHTS])

    def late_weights(self, outs, wts):
        _fill_weights(wts, dict(zip(LATE_WEIGHTS, outs)))

    def start_layer1(self, big):
        self.layer1 = _ReduceScatter([big[k] for k in LAYER1_GRADS], self.pos, "l1_")

    def start_layer0_ffn(self, big):
        self.layer0_ffn = _ReduceScatter([big[k] for k in LAYER0_FFN_GRADS], self.pos, "l0f_")

    def exchange_job(self):
        return _merge_jobs(self.layer1.exchange_job(), self.layer0_ffn.exchange_job())

    def after_exchange(self, land):
        n1 = len(LAYER1_GRADS)
        self.reduced.update(zip(LAYER1_GRADS, self.layer1.after_exchange(land[:n1])))
        self.reduced.update(zip(LAYER0_FFN_GRADS, self.layer0_ffn.after_exchange(land[n1:])))


def _all_reduce_small(v, name):
    def body(v_ref, o_ref, land_ref, send_sems, recv_sems):
        x, y, c = _mesh_pos()
        me = 4 * x + 2 * y + c
        land_ref[me] = v_ref[...]
        for t in range(N_DEVICES):
            @pl.when(t != me)
            def _(t=t):
                _remote(v_ref, land_ref.at[me], send_sems.at[t], recv_sems.at[me], (t // 4, (t // 2) % 2, t % 2)).start()
        for t in range(N_DEVICES):
            @pl.when(t != me)
            def _(t=t):
                _remote(v_ref, land_ref.at[t], send_sems.at[t], recv_sems.at[t], (t // 4, (t // 2) % 2, t % 2)).wait()
        acc = land_ref[0]
        for t in range(1, N_DEVICES):
            acc = acc + land_ref[t]
        o_ref[...] = acc

    vmem = pl.BlockSpec(memory_space=pltpu.VMEM)
    return pl.pallas_call(
        body, name=name, in_specs=[vmem], out_specs=vmem, out_shape=jax.ShapeDtypeStruct(v.shape, F32),
        scratch_shapes=[pltpu.VMEM((N_DEVICES,) + v.shape, F32), pltpu.SemaphoreType.DMA((N_DEVICES,)),
                        pltpu.SemaphoreType.DMA((N_DEVICES,))],
    )(v)


def _all_to_all_small(v, name):
    def body(v_ref, o_ref, send_sems, recv_sems):
        x, y, c = _mesh_pos()
        me = 4 * x + 2 * y + c
        o_ref[me] = v_ref[me]
        for t in range(N_DEVICES):
            @pl.when(t != me)
            def _(t=t):
                _remote(v_ref.at[t], o_ref.at[me], send_sems.at[t], recv_sems.at[me], (t // 4, (t // 2) % 2, t % 2)).start()
        for t in range(N_DEVICES):
            @pl.when(t != me)
            def _(t=t):
                _remote(v_ref.at[t], o_ref.at[t], send_sems.at[t], recv_sems.at[t], (t // 4, (t // 2) % 2, t % 2)).wait()

    vmem = pl.BlockSpec(memory_space=pltpu.VMEM)
    return pl.pallas_call(
        body, name=name, in_specs=[vmem], out_specs=vmem, out_shape=jax.ShapeDtypeStruct(v.shape, F32),
        scratch_shapes=[pltpu.SemaphoreType.DMA((N_DEVICES,)), pltpu.SemaphoreType.DMA((N_DEVICES,))],
    )(v)


ALL_ROWS = 40


class _AdaLN:
    def __init__(self, c, c_ctx, ada_w, ada_b, riders):
        xi, yi, ci = _mesh_pos()
        self.me, self.chip, self.core = 4 * xi + 2 * yi + ci, 2 * xi + yi, ci
        self.nb, d = c.shape
        self.ada_w, self.c_ctx = ada_w, c_ctx
        self.cols = ada_w.shape[2]
        ctx_row = self.nb * N_DEVICES
        assert ctx_row + 1 + riders.shape[0] <= ALL_ROWS
        placed = lax.dynamic_update_slice(jnp.zeros((ALL_ROWS, d), F32), c, (self.me * self.nb, 0))
        placed = lax.dynamic_update_slice(placed, riders, (ctx_row + 1, 0))
        summed = _all_reduce_small(placed, "gather_conditioning")
        self.riders = summed[ctx_row + 1:ctx_row + 1 + riders.shape[0]]
        c_all = summed.at[ctx_row].set(c_ctx)
        self.cact, = _whole("cond_silu", lambda v: (_silu(v),), [jax.ShapeDtypeStruct(c_all.shape, F32)], c_all)
        parts = []
        for i in range(2):
            bias = lax.dynamic_slice(ada_b[i], (self.chip * self.cols,), (self.cols,))[None, :]
            parts.append(_mm_nn(self.cact, ada_w[i], F32, f"mod{i}", ALL_ROWS, self.cols, d, bias=bias))
        part = jnp.concatenate(parts, axis=1)
        rows =[[t * self.nb + b for b in range(self.nb)] + [ctx_row] * (MOD_ROWS - self.nb) for t in range(N_DEVICES)]
        got = _all_to_all_small(part[jnp.asarray(rows)], "mod_exchange")
        self.mods = [jnp.concatenate([got[2 * j][:self.nb + 1, i * self.cols:(i + 1) * self.cols] for j in range(N_CHIPS)], axis=1)[:, None, :]
                     for i in range(2)]

    def backward(self, dmods):
        nb, cols, d = self.nb, self.cols, self.ada_w.shape[1]
        blocks = [jnp.concatenate([dm[:, j * cols:(j + 1) * cols] for dm in dmods], axis=1) for j in range(N_CHIPS)]
        got = _all_to_all_small(jnp.stack([blocks[t // 2] for t in range(N_DEVICES)]), "dmod_exchange")
        dall = jnp.concatenate([got[:, :nb].reshape(N_DEVICES * nb, 2 * cols), jnp.sum(got[:, nb], axis=0, keepdims=True),
                                jnp.zeros((ALL_ROWS - N_DEVICES * nb - 1, 2 * cols), F32)], axis=0)
        dctx = jnp.concatenate([dall[N_DEVICES * nb][None, :], jnp.zeros((MOD_ROWS - 1, 2 * cols), F32)], axis=0)
        grads, dcact = [], []
        for i in range(2):
            grads.append(_mm_tn(self.cact, dall[:, i * cols:(i + 1) * cols], f"ada_dw{i}", d, cols, ALL_ROWS))
            dcact.append(_mm_nt(dctx[:, i * cols:(i + 1) * cols], self.ada_w[i], F32, f"ada_dx{i}", MOD_ROWS, d, cols))

        def silu_bwd(v, d0, d1):
            sg = _sigmoid(v)
            return ((d0 + d1)[0:1] * (sg * (1.0 + v * (1.0 - sg))),)

        dc_ctx, = _whole("cond_silu_bwd", silu_bwd, [jax.ShapeDtypeStruct((1, d), F32)], self.c_ctx[None, :], dcact[0], dcact[1])
        return grads, jnp.where(self.core == 0, dc_ctx[0], jnp.zeros((d,), F32))


SMALL_ROWS = 24


def _pack_small(small, dlogit):
    d = D_MODEL
    misc = jnp.zeros((d,), F32)
    misc = misc.at[0:HEAD_DIM].set(small["q_norm"]).at[128:128 + HEAD_DIM].set(small["k_norm"])
    misc = misc.at[256:256 + N_HEADS].set(small["sink"]).at[384:384 + 2 * RET_HEADS].set(dlogit.reshape(-1))
    rows = [small["ada_b0"].reshape(6, d), small["ada_b1"].reshape(6, d), small["norm1_g0"][None], small["norm1_g1"][None],
            small["norm2_g0"][None], small["norm2_g1"][None], small["c_ctx"][None], small["gn_g"].reshape(2, d), misc[None]]
    buf = jnp.concatenate(rows, axis=0)
    return jnp.concatenate([buf, jnp.zeros((SMALL_ROWS - buf.shape[0], d), F32)], axis=0)


def _unpack_small(buf):
    d = D_MODEL
    misc = buf[19]
    return dict(ada_b=buf[0:12].reshape(2, 6 * d), norm1_g=buf[12:14], norm2_g=buf[14:16], c_ctx=buf[16],
                gn_g=buf[17:19].reshape(2 * d), q_norm=misc[0:HEAD_DIM], k_norm=misc[128:128 + HEAD_DIM],
                sink=misc[256:256 + N_HEADS], decay=misc[384:384 + 2 * RET_HEADS])


def kernel(x, c, ctx, c_ctx, ada_w, ada_b, norm1_g, norm2_g, ffn_w_in, ffn_w_out, attn_w_qkv, attn_q_norm, attn_k_norm, attn_sink, attn_w_o, ret_w_qkvg, ret_decay_logit, ret_gn_g, ret_w_o, loss_target, m_c_ctx, m_ada_w, m_ada_b, m_norm1_g, m_norm2_g, m_ffn_w_in, m_ffn_w_out, m_attn_w_qkv, m_attn_q_norm, m_attn_k_norm, m_attn_sink, m_attn_w_o, m_ret_w_qkvg, m_ret_decay_logit, m_ret_gn_g, m_ret_w_o, v_c_ctx, v_ada_w, v_ada_b, v_norm1_g, v_norm2_g, v_ffn_w_in, v_ffn_w_out, v_attn_w_qkv, v_attn_q_norm, v_attn_k_norm, v_attn_sink, v_attn_w_o, v_ret_w_qkvg, v_ret_decay_logit, v_ret_gn_g, v_ret_w_o):
    xi, yi, ci = _mesh_pos()
    chip = 2 * xi + yi
    nb, s, d = x.shape
    gn_shard = ret_gn_g.shape[1]

    shards = dict(ffn_in0=(ffn_w_in, 0), ffn_in1=(ffn_w_in, 1), ffn_out0=(ffn_w_out, 0),
                  ffn_out1=(ffn_w_out, 1), attn_qkv=(attn_w_qkv, 0), attn_o=(attn_w_o, 0), ret_qkvg=(ret_w_qkvg, 0), ret_o=(ret_w_o, 0))
    names = list(shards)
    pos = jnp.stack([ci, chip]).astype(jnp.int32)
    placed = {k: _place_shard(*shards[k], pos, f"place_{k}") for k in names}
    early = _run_job(_gather_job([placed[k] for k in EARLY_WEIGHTS]), "gather_early_weights")
    gn_mine = jnp.where(ci == 0, ret_gn_g[0], jnp.zeros_like(ret_gn_g[0]))
    gn_place = lax.dynamic_update_slice(jnp.zeros((RET_VWIDTH,), F32), gn_mine, (chip * gn_shard,))

    wts = dict(ffn_in=[None, None], ffn_out=[None, None], attn_qkv=None, attn_o=None, ret_qkvg=None, ret_o=None)
    ada = _AdaLN(c, c_ctx, ada_w, ada_b, riders=gn_place.reshape(2, d))
    gn_full = ada.riders.reshape(RET_VWIDTH)
    _fill_weights(wts, dict(zip(EARLY_WEIGHTS, early)))
    plan = _StepPlan(placed, pos)
    decay_logit = ret_decay_logit[0]
    sp = dict(norm1_g=norm1_g, norm2_g=norm2_g, q_norm=attn_q_norm[0], k_norm=attn_k_norm[0],
              sink=attn_sink[0], log_g=jax.nn.log_sigmoid(decay_logit), gn_g=gn_full)
    loss_part, dz, big, small, dmods = _local_step(x, ctx, loss_target, sp, wts, ada.mods, plan)
    ada_grads, small["c_ctx"] = ada.backward(dmods)

    loss = lax.psum(loss_part[0, 0], ("x", "y", "c"))
    grad_x = dz.reshape(nb, s, d)

    dlogit = small["log_g"] * jax.nn.sigmoid(-decay_logit)
    sg = _unpack_small(_all_reduce_small(_pack_small(small, dlogit), "reduce_small_grads"))
    halves = dict(plan.reduced)
    halves.update(zip(LAST_GRADS, _ReduceScatter([big[k] for k in LAST_GRADS], pos, "last_").run()))
    reduced = dict(zip(halves, _pair_share(list(halves.values()), "grads_pair_share")))

    grads = dict(
        c_ctx=sg["c_ctx"], ada_w=jnp.stack(ada_grads), ada_b=sg["ada_b"], norm1_g=sg["norm1_g"],
        norm2_g=sg["norm2_g"], ffn_w_in=jnp.stack([reduced["ffn_in0"], reduced["ffn_in1"]]),
        ffn_w_out=jnp.stack([reduced["ffn_out0"], reduced["ffn_out1"]]), attn_w_qkv=reduced["attn_qkv"][None],
        attn_q_norm=sg["q_norm"][None], attn_k_norm=sg["k_norm"][None], attn_sink=sg["sink"][None],
        attn_w_o=reduced["attn_o"][None], ret_w_qkvg=reduced["ret_qkvg"][None], ret_decay_logit=sg["decay"].reshape(1, 2, RET_HEADS),
        ret_gn_g=lax.dynamic_slice(sg["gn_g"], (chip * gn_shard,), (gn_shard,))[None], ret_w_o=reduced["ret_o"][None])
    params = dict(c_ctx=(c_ctx, m_c_ctx, v_c_ctx), ada_w=(ada_w, m_ada_w, v_ada_w), ada_b=(ada_b, m_ada_b, v_ada_b),
                  norm1_g=(norm1_g, m_norm1_g, v_norm1_g), norm2_g=(norm2_g, m_norm2_g, v_norm2_g),
                  ffn_w_in=(ffn_w_in, m_ffn_w_in, v_ffn_w_in), ffn_w_out=(ffn_w_out, m_ffn_w_out, v_ffn_w_out),
                  attn_w_qkv=(attn_w_qkv, m_attn_w_qkv, v_attn_w_qkv), attn_q_norm=(attn_q_norm, m_attn_q_norm, v_attn_q_norm),
                  attn_k_norm=(attn_k_norm, m_attn_k_norm, v_attn_k_norm), attn_sink=(attn_sink, m_attn_sink, v_attn_sink),
                  attn_w_o=(attn_w_o, m_attn_w_o, v_attn_w_o), ret_w_qkvg=(ret_w_qkvg, m_ret_w_qkvg, v_ret_w_qkvg),
                  ret_decay_logit=(ret_decay_logit, m_ret_decay_logit, v_ret_decay_logit),
                  ret_gn_g=(ret_gn_g, m_ret_gn_g, v_ret_gn_g), ret_w_o=(ret_w_o, m_ret_w_o, v_ret_w_o))
    order = list(params)
    deltas, new_m, new_v = [], [], []
    for k in order:
        w, m, v = params[k]
        g = grads[k].reshape(w.shape)
        grads[k] = g
        flat = (-1, w.shape[-1]) if w.ndim > 1 else (1, -1)
        if k == "ret_decay_logit":
            flat = (1, -1)
        dw, nm, nv = _adamw(w.reshape(flat), g.reshape(flat), m.reshape(flat), v.reshape(flat), f"adamw_{k}")
        deltas.append(dw.reshape(w.shape))
        new_m.append(nm.reshape(w.shape))
        new_v.append(nv.reshape(w.shape))
    return (loss, grad_x, *[grads[k] for k in order], *deltas, *new_m, *new_v)
```

```python
import functools

import jax
import jax.numpy as jnp
from jax import lax
from jax.experimental import pallas as pl
from jax.experimental.pallas import tpu as pltpu

F32 = jnp.float32
BF16 = jnp.bfloat16

D_MODEL = 1024
N_HEADS = 16
N_KV_HEADS = 4
HEAD_DIM = 64
WINDOW = 128
ATTN_BLOCK = 128
BAND = ATTN_BLOCK + 2 * WINDOW
RET_HEADS = 4
RET_QK_DIM = 256
RET_V_DIM = 512
RET_VWIDTH = 2048
RET_CHUNK = 128
D_FF = 2816
GRID_W = 64
ROPE_BASE = 10000.0
EPS = 1e-6
NEG_INF = -1e30
LANES = 128

ADAM_LR = 0.001
ADAM_B1 = 0.9
ADAM_B2 = 0.999
ADAM_EPS = 1e-08
ADAM_WD = 0.01
ADAM_STEP = 10

VMEM_LIMIT_BYTES = 56 * 1024 * 1024
MESH = pl.DeviceIdType.MESH
N_CHIPS = 4


def _cparams(*sem):
    return pltpu.CompilerParams(dimension_semantics=sem, vmem_limit_bytes=VMEM_LIMIT_BYTES)


_DIMS = {"nn": ((1,), (0,)), "nt": ((1,), (1,)), "tn": ((0,), (0,))}


def _dot(a, b, form):
    return lax.dot_general(a.astype(BF16), b.astype(BF16), (_DIMS[form], ((), ())), preferred_element_type=F32)


@functools.partial(jax.custom_vjp, nondiff_argnums=(2,))
def _mm(a, b, form):
    return _dot(a, b, form)


def _mm_fwd(a, b, form):
    return _dot(a, b, form), (a, b)


def _mm_bwd(form, res, ct):
    a, b = res
    if form == "nn":
        da, db = _dot(ct, b, "nt"), _dot(a, ct, "tn")
    elif form == "nt":
        da, db = _dot(ct, b, "nn"), _dot(ct, a, "tn")
    else:
        da, db = _dot(b, ct, "nt"), _dot(a, ct, "nn")
    return da.astype(a.dtype), db.astype(b.dtype)


_mm.defvjp(_mm_fwd, _mm_bwd)


def _swap_halves(x, half):
    w = x.shape[-1]
    lane = lax.broadcasted_iota(jnp.int32, x.shape, x.ndim - 1)
    return jnp.where(lane % (2 * half) < half, pltpu.roll(x, w - half, x.ndim - 1), pltpu.roll(x, half, x.ndim - 1))


@functools.partial(jax.custom_vjp, nondiff_argnums=(1,))
def _rot(x, half):
    return _swap_halves(x, half)


def _rot_fwd(x, half):
    return _swap_halves(x, half), None


def _rot_bwd(half, _, ct):
    return (_swap_halves(ct, half),)


_rot.defvjp(_rot_fwd, _rot_bwd)


def _rope(x, cos, sin_signed, half):
    return x * cos + _rot(x, half) * sin_signed


def _head_mean_square(x):
    r = lax.broadcasted_iota(jnp.int32, (LANES, LANES), 0) // HEAD_DIM
    c = lax.broadcasted_iota(jnp.int32, (LANES, LANES), 1) // HEAD_DIM
    g = jnp.where(r == c, 1.0 / HEAD_DIM, 0.0).astype(F32)
    return jnp.dot(x * x, g, precision=lax.Precision.HIGHEST, preferred_element_type=F32)


def _qk_chunk(x, gain, cos, sin_signed, scale):
    y = x * lax.rsqrt(_head_mean_square(x) + EPS) * gain
    return _rope(y, cos, sin_signed, HEAD_DIM // 4) * scale


def _sigmoid(x):
    return 1.0 / (1.0 + jnp.exp(-x))


def _silu(x):
    return x * _sigmoid(x)


def _mm_nn(a, w, out_dtype, name, tm, tn, tk, bias=None):
    m, k_dim = a.shape
    if w.ndim == 3:
        n = w.shape[0] * w.shape[2]
        per = w.shape[2] // tn
        assert w.shape[2] % tn == 0
        w_spec = pl.BlockSpec((None, tk, tn), lambda i, j, k: (j // per, k, j % per))
    else:
        n = w.shape[1]
        w_spec = pl.BlockSpec((tk, tn), lambda i, j, k: (k, j))
    assert m % tm == 0 and n % tn == 0 and k_dim % tk == 0, (name, a.shape, w.shape, tm, tn, tk)
    nk = k_dim // tk
    has_bias = bias is not None

    def body(*refs):
        a_ref, w_ref = refs[0], refs[1]
        b_ref = refs[2] if has_bias else None
        o_ref, acc_ref = (refs[-1], None) if nk == 1 else (refs[-2], refs[-1])
        if nk == 1:
            part = jnp.dot(a_ref[...].astype(BF16), w_ref[...].astype(BF16), preferred_element_type=F32)
            o_ref[...] = (part + b_ref[...] if has_bias else part).astype(out_dtype)
            return
        k = pl.program_id(2)

        @pl.when(k == 0)
        def _():
            acc_ref[...] = jnp.zeros_like(acc_ref)

        acc_ref[...] += jnp.dot(a_ref[...].astype(BF16), w_ref[...].astype(BF16), preferred_element_type=F32)

        @pl.when(k == nk - 1)
        def _():
            r = acc_ref[...]
            if has_bias:
                r = r + b_ref[...]
            o_ref[...] = r.astype(out_dtype)

    in_specs = [pl.BlockSpec((tm, tk), lambda i, j, k: (i, k)), w_spec]
    args = [a, w]
    if has_bias:
        in_specs.append(pl.BlockSpec((1, tn), lambda i, j, k: (0, j)))
        args.append(bias)
    return pl.pallas_call(
        body, name=name, grid=(m // tm, n // tn, nk), in_specs=in_specs,
        out_specs=pl.BlockSpec((tm, tn), lambda i, j, k: (i, j)),
        out_shape=jax.ShapeDtypeStruct((m, n), out_dtype),
        scratch_shapes=[pltpu.VMEM((tm, tn), F32)] if nk > 1 else [],
        compiler_params=_cparams("parallel", "parallel", "arbitrary"),
    )(*args)


def _mm_nt(a, w, out_dtype, name, tm, tn, tk):
    if a.ndim == 3:
        planes, m, plane_w = a.shape
        c_dim = planes * plane_w
        a_per = plane_w // tk
        assert plane_w % tk == 0
        a_spec = pl.BlockSpec((None, tm, tk), lambda i, j, k: (k // a_per, i, k % a_per))
    else:
        m, c_dim = a.shape
        a_spec = pl.BlockSpec((tm, tk), lambda i, j, k: (i, k))
    if w.ndim == 3:
        k_out = w.shape[1]
        per = w.shape[2] // tk
        assert w.shape[2] % tk == 0 and w.shape[0] * w.shape[2] == c_dim
        w_spec = pl.BlockSpec((None, tn, tk), lambda i, j, k: (k // per, j, k % per))
    else:
        k_out = w.shape[0]
        assert w.shape[1] == c_dim
        w_spec = pl.BlockSpec((tn, tk), lambda i, j, k: (j, k))
    assert m % tm == 0 and k_out % tn == 0 and c_dim % tk == 0, (name, a.shape, w.shape, tm, tn, tk)
    nk = c_dim // tk

    def body(a_ref, w_ref, o_ref, acc_ref=None):
        if nk == 1:
            o_ref[...] = _dot(a_ref[...], w_ref[...], "nt").astype(out_dtype)
            return
        k = pl.program_id(2)

        @pl.when(k == 0)
        def _():
            acc_ref[...] = jnp.zeros_like(acc_ref)

        acc_ref[...] += _dot(a_ref[...], w_ref[...], "nt")

        @pl.when(k == nk - 1)
        def _():
            o_ref[...] = acc_ref[...].astype(out_dtype)

    return pl.pallas_call(
        body, name=name, grid=(m // tm, k_out // tn, nk),
        in_specs=[a_spec, w_spec],
        out_specs=pl.BlockSpec((tm, tn), lambda i, j, k: (i, j)),
        out_shape=jax.ShapeDtypeStruct((m, k_out), out_dtype),
        scratch_shapes=[pltpu.VMEM((tm, tn), F32)] if nk > 1 else [],
        compiler_params=_cparams("parallel", "parallel", "arbitrary"),
    )(a, w)


def _mm_tn(a, b, name, tm, tn, tk, shards=None, out_dtype=F32):
    r, k_dim = a.shape
    if b.ndim == 3:
        n = b.shape[0] * b.shape[2]
        b_per = b.shape[2] // tn
        assert b.shape[2] % tn == 0
        b_spec = pl.BlockSpec((None, tk, tn), lambda i, j, k: (j // b_per, k, j % b_per))
    else:
        n = b.shape[1]
        b_spec = pl.BlockSpec((tk, tn), lambda i, j, k: (k, j))
    assert r % tk == 0 and k_dim % tm == 0 and n % tn == 0, (name, a.shape, b.shape, tm, tn, tk)
    nk = r // tk
    if shards:
        per = n // shards // tn
        assert n % (shards * tn) == 0
        out_shape = jax.ShapeDtypeStruct((shards, k_dim, n // shards), out_dtype)
        out_spec = pl.BlockSpec((None, tm, tn), lambda i, j, k: (j // per, i, j % per))
    else:
        out_shape = jax.ShapeDtypeStruct((k_dim, n), out_dtype)
        out_spec = pl.BlockSpec((tm, tn), lambda i, j, k: (i, j))
    direct = out_dtype == F32

    def body(a_ref, b_ref, o_ref, *scratch):
        acc_ref = o_ref if direct else scratch[0]
        k = pl.program_id(2)

        @pl.when(k == 0)
        def _():
            acc_ref[...] = jnp.zeros_like(acc_ref)

        acc_ref[...] += _dot(a_ref[...], b_ref[...], "tn")
        if not direct:
            @pl.when(k == nk - 1)
            def _():
                o_ref[...] = acc_ref[...].astype(out_dtype)

    return pl.pallas_call(
        body, name=name, grid=(k_dim // tm, n // tn, nk),
        in_specs=[pl.BlockSpec((tk, tm), lambda i, j, k: (k, i)), b_spec],
        out_specs=out_spec, out_shape=out_shape,
        scratch_shapes=[] if direct else [pltpu.VMEM((tm, tn), F32)],
        compiler_params=_cparams("parallel", "parallel", "arbitrary"),
    )(a, b)


class _Carrier:
    def __init__(self, job, n_in, n_out, n_scratch):
        self.job, self.n_in, self.n_out, self.n_scratch = job, n_in, n_out, n_scratch
        self.ji = len(job.inputs) if job else 0
        self.jo = len(job.out_shapes) if job else 0

    def operands(self):
        return list(self.job.inputs) if self.job else []

    def in_specs(self):
        return [pl.BlockSpec(memory_space=pl.ANY)] * self.ji

    def out_specs(self):
        return [pl.BlockSpec(memory_space=pl.ANY)] * self.jo

    def out_shapes(self):
        return list(self.job.out_shapes) if self.job else []

    def scratch(self):
        return list(self.job.sem_shapes) if self.job else []

    def aliases(self):
        return {self.n_in + a: self.n_out + b for a, b in self.job.aliases.items()} if self.job else {}

    def split(self, refs):
        a = self.n_in
        b = a + self.ji
        c = b + self.n_out
        d = c + self.jo
        e = d + self.n_scratch
        return list(refs[:a]) + list(refs[b:c]) + list(refs[d:e]), (refs[a:b], refs[c:d], refs[e:])

    def run(self, job_refs, step, steps):
        if not self.job:
            return
        for stage, mark in zip(self.job.stages, _job_marks(self.job, steps)):
            pl.when(step == mark)(functools.partial(stage, *job_refs))

    def results(self, res):
        res = list(res)
        return res[:self.n_out], res[self.n_out:]


FFN_ROW_TILE = 768


def _ffn_tile(r):
    return FFN_ROW_TILE if r % FFN_ROW_TILE == 0 else _row_tile(r)


def _ffn_in_swiglu(h, w, name):
    r, k_dim = h.shape
    n4 = w.shape[2]
    tm = _ffn_tile(r)

    def body(h_ref, wg_ref, wu_ref, u_ref, a_ref):
        hv = h_ref[...]
        g = jnp.dot(hv, wg_ref[...], preferred_element_type=F32)
        up = jnp.dot(hv, wu_ref[...], preferred_element_type=F32)
        u_ref[0] = g.astype(BF16)
        u_ref[1] = up.astype(BF16)
        a_ref[...] = (_silu(g) * up).astype(BF16)

    return pl.pallas_call(
        body, name=name, grid=(r // tm, 2),
        in_specs=[pl.BlockSpec((tm, k_dim), lambda i, j: (i, 0)),
                  pl.BlockSpec((None, k_dim, n4), lambda i, j: (j, 0, 0)),
                  pl.BlockSpec((None, k_dim, n4), lambda i, j: (j + 2, 0, 0))],
        out_specs=[pl.BlockSpec((2, tm, n4), lambda i, j: (0, i, j)), pl.BlockSpec((tm, n4), lambda i, j: (i, j))],
        out_shape=[jax.ShapeDtypeStruct((2, r, 2 * n4), BF16), jax.ShapeDtypeStruct((r, 2 * n4), BF16)],
        compiler_params=_cparams("parallel", "parallel"),
    )(h, w, w)


def _mm_nn_gate_residual(geo, a, w, z, mod, off, name, norm=None):
    r, k_dim = a.shape
    n = w.shape[1]
    tm = FFN_ROW_TILE if geo.seg % FFN_ROW_TILE == 0 else 256
    tiles = geo.seg // tm
    assert geo.seg % tm == 0 and r == geo.r and n == D_MODEL

    def body(a_ref, w_ref, z_ref, mx_ref, mc_ref, *rest):
        out = jnp.dot(a_ref[...], w_ref[...], preferred_element_type=F32)
        is_x = (pl.program_id(0) % tiles) * tm + lax.broadcasted_iota(jnp.int32, (tm, 1), 0) < geo.s
        zo = z_ref[...] + jnp.where(is_x, mx_ref[:, off:off + n], mc_ref[:, off:off + n]) * out
        if norm:
            g_ref, nx_ref, nc_ref, zo_ref, raw_ref, h_ref = rest
            no = norm[2]
            shift = jnp.where(is_x, nx_ref[:, no:no + n], nc_ref[:, no:no + n])
            scale = jnp.where(is_x, nx_ref[:, no + n:no + 2 * n], nc_ref[:, no + n:no + 2 * n])
            rs = lax.rsqrt(jnp.mean(zo * zo, axis=-1, keepdims=True) + EPS)
            h_ref[...] = ((zo * rs) * g_ref[...] * (1.0 + scale) + shift).astype(BF16)
        else:
            zo_ref, raw_ref = rest
        zo_ref[...] = zo
        raw_ref[...] = out.astype(BF16)

    def mod_specs(m):
        return [pl.BlockSpec((None, 1, m.shape[2]), lambda i: (i // tiles, 0, 0)), pl.BlockSpec((None, 1, m.shape[2]), lambda i: (geo.b, 0, 0))]

    row = pl.BlockSpec((tm, n), lambda i: (i, 0))
    in_specs = [pl.BlockSpec((tm, k_dim), lambda i: (i, 0)), pl.BlockSpec((k_dim, n), lambda i: (0, 0)), row] + mod_specs(mod)
    args = [a, w, z, mod, mod]
    out_specs, out_shape = [row, row], [jax.ShapeDtypeStruct((r, n), F32), jax.ShapeDtypeStruct((r, n), BF16)]
    if norm:
        in_specs += [pl.BlockSpec((1, n), lambda i: (0, 0))] + mod_specs(norm[1])
        args += [norm[0], norm[1], norm[1]]
        out_specs.append(row)
        out_shape.append(jax.ShapeDtypeStruct((r, n), BF16))
    res = pl.pallas_call(body, name=name, grid=(r // tm,), in_specs=in_specs, out_specs=out_specs, out_shape=out_shape,
                         compiler_params=_cparams("parallel"))(*args)
    return res if norm else (*res, None)


def _ffn_out_dx_swiglu_bwd(df, w_out, u, name, job=None):
    r, d = df.shape
    n4 = u.shape[2] // 2
    tm = _ffn_tile(r)
    carrier = _Carrier(job, 3, 1, 0)
    steps = (r // tm) * 2

    def body(*refs):
        (df_ref, w_ref, u_ref, du_ref), job_refs = carrier.split(refs)
        carrier.run(job_refs, pl.program_id(0) * 2 + pl.program_id(1), steps)
        da = _dot(df_ref[...], w_ref[...], "nt")
        g, up = u_ref[0].astype(F32), u_ref[1].astype(F32)
        s = _sigmoid(g)
        du_ref[0] = (da * up * (s * (1.0 + g * (1.0 - s)))).astype(BF16)
        du_ref[1] = (da * (g * s)).astype(BF16)

    res = pl.pallas_call(
        body, name=name, grid=(r // tm, 2),
        in_specs=[pl.BlockSpec((tm, d), lambda i, j: (i, 0)), pl.BlockSpec((n4, d), lambda i, j: (j, 0)),
                  pl.BlockSpec((2, tm, n4), lambda i, j: (0, i, j))] + carrier.in_specs(),
        out_specs=[pl.BlockSpec((2, tm, n4), lambda i, j: (0, i, j))] + carrier.out_specs(),
        out_shape=[jax.ShapeDtypeStruct(u.shape, BF16)] + carrier.out_shapes(),
        scratch_shapes=carrier.scratch(), input_output_aliases=carrier.aliases(),
        compiler_params=_cparams("arbitrary", "arbitrary"),
    )(df, w_out, u, *carrier.operands())
    (du,), extra = carrier.results(res)
    return du, extra


class _Rows:
    def __init__(self, b, s, l):
        self.b, self.s, self.l = b, s, l
        self.seg = s + l
        self.r = b * self.seg


def _rowwise(name, body, geo, tm, ins, outs, job=None):
    seg_blocks, x_blocks = geo.seg // tm, geo.s // tm
    assert geo.seg % tm == 0 and geo.s % tm == 0
    nb = geo.b

    def is_ctx(i):
        return i % seg_blocks >= x_blocks

    in_specs, args = [], []
    for arr, kind in ins:
        args.append(arr)
        if kind == "row":
            in_specs.append(pl.BlockSpec((tm, arr.shape[1]), lambda i: (i, 0)))
        elif kind == "ex":
            in_specs.append(pl.BlockSpec((None, 1, arr.shape[2]), lambda i: (jnp.where(is_ctx(i), nb, i // seg_blocks), 0, 0)))
        elif kind == "full":
            in_specs.append(pl.BlockSpec(arr.shape, lambda i, nd=arr.ndim: (0,) * nd))
        elif kind == "tab":
            in_specs.append(pl.BlockSpec((tm, arr.shape[1]), lambda i: (i % seg_blocks, 0)))
        elif kind == "xrow":
            in_specs.append(pl.BlockSpec(
                (tm, arr.shape[1]), lambda i: ((i // seg_blocks) * x_blocks + jnp.minimum(i % seg_blocks, x_blocks - 1), 0)))
        else:
            _, width, cb = kind
            in_specs.append(pl.BlockSpec((tm, width), lambda i, cb=cb: (i, cb)))
    out_specs, out_shapes = [], []
    for o in outs:
        if o[0] == "row":
            out_specs.append(pl.BlockSpec((tm, o[1]), lambda i: (i, 0)))
            out_shapes.append(jax.ShapeDtypeStruct((geo.r, o[1]), o[2]))
        elif o[0] == "xrow":
            out_specs.append(pl.BlockSpec(
                (tm, o[1]), lambda i: ((i // seg_blocks) * x_blocks + jnp.minimum(i % seg_blocks, x_blocks - 1), 0)))
            out_shapes.append(jax.ShapeDtypeStruct((geo.b * geo.s, o[1]), o[2]))
        elif o[0] == "exacc":
            out_specs.append(pl.BlockSpec((None, 1, o[1]), lambda i: (jnp.where(is_ctx(i), nb, 0) + i // seg_blocks, 0, 0)))
            out_shapes.append(jax.ShapeDtypeStruct((2 * nb, 1, o[1]), F32))
        else:
            out_specs.append(pl.BlockSpec((o[1], o[2]), lambda i: (0, 0)))
            out_shapes.append(jax.ShapeDtypeStruct((o[1], o[2]), F32))
    n_in = len(ins)
    carrier = _Carrier(job, n_in, len(outs), 0)

    def kern(*refs):
        i = pl.program_id(0)
        refs, job_refs = carrier.split(refs)
        carrier.run(job_refs, i, geo.r // tm)
        res = body(i, *[r[...].astype(F32) for r in refs[:n_in]])
        if not isinstance(res, (tuple, list)):
            res = (res,)
        jj = i % seg_blocks
        first_of_part = (jj == 0) | (jj == x_blocks)
        for o, ref, val in zip(outs, refs[n_in:], res):
            if o[0] == "row":
                ref[...] = val.astype(ref.dtype)
            elif o[0] == "xrow":
                @pl.when(jj < x_blocks)
                def _(ref=ref, val=val):
                    ref[...] = val.astype(ref.dtype)
            else:
                first = first_of_part if o[0] == "exacc" else i == 0

                @pl.when(first)
                def _(ref=ref, val=val):
                    ref[...] = val

                @pl.when(jnp.logical_not(first))
                def _(ref=ref, val=val):
                    ref[...] += val

    res = pl.pallas_call(
        kern, name=name, grid=(geo.r // tm,), in_specs=in_specs + carrier.in_specs(), out_specs=out_specs + carrier.out_specs(),
        out_shape=out_shapes + carrier.out_shapes(), scratch_shapes=carrier.scratch(), input_output_aliases=carrier.aliases(),
        compiler_params=_cparams("arbitrary"),
    )(*args, *carrier.operands())
    own, extra = carrier.results(res)
    if job:
        return (*own, extra)
    return own[0] if len(own) == 1 else own


def _colsum(v):
    return jnp.sum(v, axis=0, keepdims=True)


def _norm_mod(geo, z, gain, mod, off, name):
    d = D_MODEL

    def body(i, zv, g, m):
        r = lax.rsqrt(jnp.mean(zv * zv, axis=-1, keepdims=True) + EPS)
        return (zv * r) * g * (1.0 + m[:, off + d:off + 2 * d]) + m[:, off:off + d]

    return _rowwise(name, body, geo, 256, [(z, "row"), (gain, "full"), (mod, "ex")], [("row", d, BF16)])


def _norm_mod_bwd(geo, z, gain, mod, off, dh, dz_skip, name, gated=None, latent_only=False, job=None):
    d = D_MODEL

    def body(i, zv, g, m, dhv, skip, *rest):
        r = lax.rsqrt(jnp.mean(zv * zv, axis=-1, keepdims=True) + EPS)
        n = zv * r
        dng = dhv * (1.0 + m[:, off + d:off + 2 * d])
        dn = dng * g
        dz = r * (dn - n * jnp.mean(dn * n, axis=-1, keepdims=True)) + skip
        res = (dz, _colsum(dhv), _colsum(dhv * (n * g)), _colsum(dng * n))
        if gated:
            ov, gm = rest
            res += (dz * gm[:, gated[2]:gated[2] + d], _colsum(dz * ov))
        return res

    ins = [(z, "row"), (gain, "full"), (mod, "ex"), (dh, "row"), (dz_skip, "row")]
    outs = [("xrow" if latent_only else "row", d, F32), ("exacc", d), ("exacc", d), ("gacc", 1, d)]
    if gated:
        ins += [(gated[0], "row"), (gated[1], "ex")]
        outs += [("row", d, BF16), ("exacc", d)]
    return _rowwise(name, body, geo, 256, ins, outs, job)


def _loss_head(geo, z, target, out, mod, off, name):
    seg_blocks, x_blocks = geo.seg // 256, geo.s // 256
    d = D_MODEL

    def body(i, zv, tv, ov, m):
        keep = jnp.where(i % seg_blocks >= x_blocks, 0.0, 1.0)
        err = (zv - tv) * keep
        part = 0.5 * jnp.sum(jnp.mean(err * err, axis=-1, keepdims=True), axis=0, keepdims=True)
        dz = err * (1.0 / d)
        return dz, jnp.broadcast_to(part, (1, LANES)), dz * m[:, off:off + d], _colsum(dz * ov)

    return _rowwise(name, body, geo, 256, [(z, "row"), (target, "xrow"), (out, "row"), (mod, "ex")],
                    [("row", d, F32), ("gacc", 1, LANES), ("row", d, BF16), ("exacc", d)])


Q_SCALE = HEAD_DIM ** -0.5
N_QK_CHUNKS = (N_HEADS + N_KV_HEADS) * HEAD_DIM // LANES
N_Q_CHUNKS = N_HEADS * HEAD_DIM // LANES


def _attn_prep(geo, proj, cos, sin_signed, q_gain, k_gain, name):
    def body(i, p, cs, sn, qg, kg):
        outs = []
        for ch in range(N_QK_CHUNKS):
            is_q = ch < N_Q_CHUNKS
            outs.append(_qk_chunk(p[:, ch * LANES:(ch + 1) * LANES], qg if is_q else kg, cs, sn, Q_SCALE if is_q else 1.0))
        outs.append(p[:, N_QK_CHUNKS * LANES:])
        return jnp.concatenate(outs, axis=1)

    return _rowwise(name, body, geo, 256, [(proj, "row"), (cos, "tab"), (sin_signed, "tab"), (q_gain, "full"), (k_gain, "full")],
                    [("row", proj.shape[1], BF16)])


def _attn_prep_bwd(geo, proj, cos, sin_signed, q_gain, k_gain, dq, dkv, name):
    kw = N_KV_HEADS * HEAD_DIM

    def body(i, p, cs, sn, qg, kg, dqv, dkvv):
        outs = []
        dgains = [jnp.zeros((1, LANES), F32), jnp.zeros((1, LANES), F32)]
        for ch in range(N_QK_CHUNKS):
            is_q = ch < N_Q_CHUNKS
            scale = Q_SCALE if is_q else 1.0
            ct = dqv[:, ch * LANES:(ch + 1) * LANES] if is_q else dkvv[:, (ch - N_Q_CHUNKS) * LANES:(ch - N_Q_CHUNKS + 1) * LANES]
            _, vjp = jax.vjp(lambda xx, gg, scale=scale: _qk_chunk(xx, gg, cs, sn, scale),
                             p[:, ch * LANES:(ch + 1) * LANES], qg if is_q else kg)
            dx, dg = vjp(ct)
            outs.append(dx)
            dgains[0 if is_q else 1] = dgains[0 if is_q else 1] + dg
        outs.append(dkvv[:, kw:])
        return jnp.concatenate(outs, axis=1), dgains[0], dgains[1]

    return _rowwise(name, body, geo, 256,
                    [(proj, "row"), (cos, "tab"), (sin_signed, "tab"), (q_gain, "full"), (k_gain, "full"), (dq, "row"), (dkv, "row")],
                    [("row", proj.shape[1], BF16), ("gacc", 1, LANES), ("gacc", 1, LANES)])


def _attn_geometry(geo):
    assert geo.s % ATTN_BLOCK == 0 and geo.l % ATTN_BLOCK == 0 and geo.seg >= BAND
    return geo.seg // ATTN_BLOCK, geo.s // ATTN_BLOCK


def _attn_mask(j, s0, geo):
    r = lax.broadcasted_iota(jnp.int32, (ATTN_BLOCK, geo.l + BAND), 0)
    n = lax.broadcasted_iota(jnp.int32, (ATTN_BLOCK, geo.l + BAND), 1) - geo.l
    dist = (s0 - j * ATTN_BLOCK) + n - r
    return (n < 0) | ((jnp.abs(dist) <= WINDOW) & (s0 + n < geo.s))


def _attn_probs(q, keys, valid, n_ctx, sink):
    s = _dot(q, keys, "nt")
    if valid is not None:
        s = jnp.where(valid, s, NEG_INF)
    m = jnp.maximum(jnp.max(s, axis=-1, keepdims=True), sink)
    e, e_sink = jnp.exp(s - m), jnp.exp(sink - m)
    inv = 1.0 / (jnp.sum(e, axis=-1, keepdims=True) + e_sink)
    return e * inv, e_sink * inv


def _attn_keys(ref, s0, geo, with_band):
    ctx = ref[geo.s:geo.seg, :]
    return jnp.concatenate([ctx, ref[pl.ds(s0, BAND), :]], axis=0) if with_band else ctx


def _attention(geo, qkv, sink, name, job=None):
    n_blocks, n_x_blocks = _attn_geometry(geo)
    qw, kw = N_HEADS * HEAD_DIM, N_KV_HEADS * HEAD_DIM
    group = N_HEADS // N_KV_HEADS
    carrier = _Carrier(job, 4, 1, 0)

    def kern(*refs):
        (sink_ref, q_ref, k_ref, v_ref, o_ref), job_refs = carrier.split(refs)
        j = pl.program_id(1)
        carrier.run(job_refs, pl.program_id(0) * n_blocks + j, geo.b * n_blocks)
        s0 = pl.multiple_of(jnp.clip((j - 1) * ATTN_BLOCK, 0, geo.seg - BAND), ATTN_BLOCK)

        def heads(with_band):
            valid = _attn_mask(j, s0, geo) if with_band else None
            k_all, v_all = _attn_keys(k_ref, s0, geo, with_band), _attn_keys(v_ref, s0, geo, with_band)
            for h in range(N_HEADS):
                kv = slice((h // group) * HEAD_DIM, (h // group + 1) * HEAD_DIM)
                p, _ = _attn_probs(q_ref[:, h * HEAD_DIM:(h + 1) * HEAD_DIM], k_all[:, kv], valid, geo.l, sink_ref[h])
                o_ref[:, h * HEAD_DIM:(h + 1) * HEAD_DIM] = _dot(p, v_all[:, kv], "nn").astype(BF16)

        pl.when(j < n_x_blocks)(lambda: heads(True))
        pl.when(j >= n_x_blocks)(lambda: heads(False))

    res = pl.pallas_call(
        kern, name=name, grid=(geo.b, n_blocks),
        in_specs=[pl.BlockSpec(memory_space=pltpu.SMEM),
                  pl.BlockSpec((ATTN_BLOCK, qw), lambda b, j: (b * n_blocks + j, 0)),
                  pl.BlockSpec((geo.seg, kw), lambda b, j: (b, qw // kw)),
                  pl.BlockSpec((geo.seg, kw), lambda b, j: (b, qw // kw + 1))] + carrier.in_specs(),
        out_specs=[pl.BlockSpec((ATTN_BLOCK, qw), lambda b, j: (b * n_blocks + j, 0))] + carrier.out_specs(),
        out_shape=[jax.ShapeDtypeStruct((geo.r, qw), BF16)] + carrier.out_shapes(),
        scratch_shapes=carrier.scratch(), input_output_aliases=carrier.aliases(),
        compiler_params=_cparams("arbitrary", "arbitrary"),
    )(sink, qkv, qkv, qkv, *carrier.operands())
    (o,), extra = carrier.results(res)
    return o, extra


def _attention_bwd(geo, qkv, sink, do, name, job=None):
    n_blocks, n_x_blocks = _attn_geometry(geo)
    qw, kw = N_HEADS * HEAD_DIM, N_KV_HEADS * HEAD_DIM
    group = N_HEADS // N_KV_HEADS

    carrier = _Carrier(job, 5, 3, 1)

    def kern(*refs):
        (sink_ref, q_ref, k_ref, v_ref, do_ref, dq_ref, dkv_out_ref, dsink_ref, dkv_ref), job_refs = carrier.split(refs)
        b, j = pl.program_id(0), pl.program_id(1)
        carrier.run(job_refs, b * n_blocks + j, geo.b * n_blocks)
        s0 = pl.multiple_of(jnp.clip((j - 1) * ATTN_BLOCK, 0, geo.seg - BAND), ATTN_BLOCK)

        @pl.when(j == 0)
        def _():
            dkv_ref[...] = jnp.zeros_like(dkv_ref)

        @pl.when((j == 0) & (b == 0))
        def _():
            dsink_ref[...] = jnp.zeros_like(dsink_ref)

        def heads(with_band):
            valid = _attn_mask(j, s0, geo) if with_band else None
            k_all, v_all = _attn_keys(k_ref, s0, geo, with_band), _attn_keys(v_ref, s0, geo, with_band)
            for g in range(N_KV_HEADS):
                kv = slice(g * HEAD_DIM, (g + 1) * HEAD_DIM)
                keys, vals = k_all[:, kv], v_all[:, kv]
                group_heads = [slice(h * HEAD_DIM, (h + 1) * HEAD_DIM) for h in range(g * group, (g + 1) * group)]
                ds_rows, p_rows = [], []
                for h, hs in zip(range(g * group, (g + 1) * group), group_heads):
                    dout = do_ref[:, hs]
                    p, p_sink = _attn_probs(q_ref[:, hs], keys, valid, geo.l, sink_ref[h])
                    dp = _dot(dout, vals, "nt")
                    dsum = jnp.sum(p * dp, axis=-1, keepdims=True)
                    ds = (p * (dp - dsum)).astype(BF16)
                    dq_ref[:, hs] = _dot(ds, keys, "nn").astype(BF16)
                    ds_rows.append(ds)
                    p_rows.append(p.astype(BF16))
                    dsink_ref[h:h + 1, :] += jnp.broadcast_to(-jnp.sum(p_sink * dsum, axis=0, keepdims=True), (1, LANES))
                q_rows = jnp.concatenate([q_ref[:, hs] for hs in group_heads], axis=0)
                do_rows = jnp.concatenate([do_ref[:, hs] for hs in group_heads], axis=0)
                dk = _dot(jnp.concatenate(ds_rows, axis=0), q_rows, "tn")
                dv = _dot(jnp.concatenate(p_rows, axis=0), do_rows, "tn")
                vv = slice(kw + g * HEAD_DIM, kw + (g + 1) * HEAD_DIM)
                dkv_ref[geo.s:geo.seg, kv] += dk[:geo.l]
                dkv_ref[geo.s:geo.seg, vv] += dv[:geo.l]
                if with_band:
                    dkv_ref[pl.ds(s0, BAND), kv] += dk[geo.l:]
                    dkv_ref[pl.ds(s0, BAND), vv] += dv[geo.l:]

        pl.when(j < n_x_blocks)(lambda: heads(True))
        pl.when(j >= n_x_blocks)(lambda: heads(False))

        @pl.when(j == n_blocks - 1)
        def _():
            dkv_out_ref[...] = dkv_ref[...].astype(BF16)

    res = pl.pallas_call(
        kern, name=name, grid=(geo.b, n_blocks),
        in_specs=[pl.BlockSpec(memory_space=pltpu.SMEM),
                  pl.BlockSpec((ATTN_BLOCK, qw), lambda b, j: (b * n_blocks + j, 0)),
                  pl.BlockSpec((geo.seg, kw), lambda b, j: (b, qw // kw)),
                  pl.BlockSpec((geo.seg, kw), lambda b, j: (b, qw // kw + 1)),
                  pl.BlockSpec((ATTN_BLOCK, qw), lambda b, j: (b * n_blocks + j, 0))] + carrier.in_specs(),
        out_specs=[pl.BlockSpec((ATTN_BLOCK, qw), lambda b, j: (b * n_blocks + j, 0)),
                   pl.BlockSpec((geo.seg, 2 * kw), lambda b, j: (b, 0)),
                   pl.BlockSpec((N_HEADS, LANES), lambda b, j: (0, 0))] + carrier.out_specs(),
        out_shape=[jax.ShapeDtypeStruct((geo.r, qw), BF16), jax.ShapeDtypeStruct((geo.r, 2 * kw), BF16),
                   jax.ShapeDtypeStruct((N_HEADS, LANES), F32)] + carrier.out_shapes(),
        scratch_shapes=[pltpu.VMEM((geo.seg, 2 * kw), F32)] + carrier.scratch(), input_output_aliases=carrier.aliases(),
        compiler_params=_cparams("arbitrary", "arbitrary"),
    )(sink, qkv, qkv, qkv, do, *carrier.operands())
    (dq, dkv, dsink), extra = carrier.results(res)
    return dq, dkv, dsink, extra


RET_QK_W = RET_HEADS * RET_QK_DIM
K_SCALE = RET_QK_DIM ** -0.5


def _ret_prep(geo, proj, cos, sin_signed, name):
    def body(i, p, cs, sn):
        cs2, sn2 = jnp.concatenate([cs] * RET_HEADS, axis=1), jnp.concatenate([sn] * RET_HEADS, axis=1)
        q = _rope(p[:, :RET_QK_W], cs2, sn2, RET_QK_DIM // 4)
        k = _rope(p[:, RET_QK_W:2 * RET_QK_W], cs2, sn2, RET_QK_DIM // 4) * K_SCALE
        return jnp.concatenate([q, k, p[:, 2 * RET_QK_W:]], axis=1)

    return _rowwise(name, body, geo, 128, [(proj, ("rowc", 2 * RET_QK_W + RET_VWIDTH, 0)), (cos, "tab"), (sin_signed, "tab")],
                    [("row", 2 * RET_QK_W + RET_VWIDTH, BF16)])


def _ret_prep_bwd(geo, dq, dk, dv, dgate, cos, sin_signed, name):
    def body(i, dqv, dkv, dvv, dg, cs, sn):
        cs2, sn2 = jnp.concatenate([cs] * RET_HEADS, axis=1), jnp.concatenate([sn] * RET_HEADS, axis=1)
        dkv = dkv * K_SCALE
        dqv = dqv * cs2 + _swap_halves(dqv * sn2, RET_QK_DIM // 4)
        dkv = dkv * cs2 + _swap_halves(dkv * sn2, RET_QK_DIM // 4)
        return jnp.concatenate([dqv, dkv, dvv, dg], axis=1)

    return _rowwise(name, body, geo, 128,
                    [(dq, "row"), (dk, "row"), (dv, "row"), (dgate, "row"), (cos, "tab"), (sin_signed, "tab")],
                    [("row", 2 * RET_QK_W + 2 * RET_VWIDTH, BF16)])


def _ret_step(state, q, k, v, lg, rev):
    c = RET_CHUNK
    ri = lax.broadcasted_iota(jnp.int32, (c, 1), 0).astype(F32)
    cj = lax.broadcasted_iota(jnp.int32, (1, c), 1).astype(F32)
    if rev:
        dist, q_decay, k_decay = cj - ri, jnp.exp(lg * (c - ri)), jnp.exp(lg * ri)
    else:
        dist, q_decay, k_decay = ri - cj, jnp.exp(lg * (ri + 1.0)), jnp.exp(lg * (c - 1.0 - ri))
    intra = jnp.where(dist >= 0, jnp.exp(lg * jnp.maximum(dist, 0.0)), 0.0)
    scores = _mm(q, k, "nt") * intra
    out = _mm(scores, v, "nn") + _mm(q, state, "nn") * q_decay
    new_state = state * jnp.exp(lg * c) + _mm(k * k_decay, v, "tn")
    return new_state, out


def _ret_state0(kc, vc, lg, rev):
    n = kc.shape[0]
    t = lax.broadcasted_iota(jnp.int32, (n, 1), 0).astype(F32)
    decay = jnp.exp(lg * t) if rev else jnp.exp(lg * (n - 1.0 - t))
    return _mm(kc * decay, vc, "tn")


def _ret_specs(geo):
    nq = RET_HEADS
    return [pl.BlockSpec((2 * RET_HEADS, LANES), lambda b, h: (0, 0)),
            pl.BlockSpec((geo.seg, RET_QK_DIM), lambda b, h: (b, h)),
            pl.BlockSpec((geo.seg, RET_QK_DIM), lambda b, h: (b, nq + h)),
            pl.BlockSpec((geo.seg, RET_V_DIM), lambda b, h: (b, nq + h))]


def _retention(geo, qkv, log_g, name):
    nc = geo.s // RET_CHUNK

    def kern(lg_ref, q_ref, k_ref, v_ref, o_ref, st_ref):
        h = pl.program_id(1)
        for d, rev in ((0, False), (1, True)):
            lg = lg_ref[pl.ds(d * RET_HEADS + h, 1), 0:1]
            st_ref[...] = _ret_state0(k_ref[geo.s:geo.seg, :].astype(F32), v_ref[geo.s:geo.seg, :].astype(F32), lg, rev)

            def chunk(ci, carry, d=d, rev=rev, lg=lg):
                r0 = pl.multiple_of((nc - 1 - ci if rev else ci) * RET_CHUNK, RET_CHUNK)
                rows = pl.ds(r0, RET_CHUNK)
                new_state, out = _ret_step(st_ref[...], q_ref[rows, :], k_ref[rows, :], v_ref[rows, :], lg, rev)
                st_ref[...] = new_state
                if d == 0:
                    o_ref[rows, :] = out
                else:
                    o_ref[rows, :] += out
                return carry

            lax.fori_loop(0, nc, chunk, 0)
        o_ref[geo.s:geo.seg, :] = jnp.zeros((geo.l, RET_V_DIM), F32)

    return pl.pallas_call(
        kern, name=name, grid=(geo.b, RET_HEADS), in_specs=_ret_specs(geo),
        out_specs=pl.BlockSpec((geo.seg, RET_V_DIM), lambda b, h: (b, h)),
        out_shape=jax.ShapeDtypeStruct((geo.r, RET_VWIDTH), F32),
        scratch_shapes=[pltpu.VMEM((RET_QK_DIM, RET_V_DIM), F32)],
        compiler_params=_cparams("parallel", "arbitrary"),
    )(log_g, qkv, qkv, qkv)


def _retention_bwd(geo, qkv, log_g, do, name):
    nc = geo.s // RET_CHUNK
    ctx = slice(geo.s, geo.seg)

    def kern(lg_ref, q_ref, k_ref, v_ref, do_ref, dq_ref, dk_ref, dv_ref, dlg_ref, states_ref, dst_ref, aq_ref, ak_ref, av_ref):
        b, h = pl.program_id(0), pl.program_id(1)

        @pl.when((b == 0) & (h == 0))
        def _():
            dlg_ref[...] = jnp.zeros_like(dlg_ref)

        for d, rev in ((0, False), (1, True)):
            row = pl.ds(d * RET_HEADS + h, 1)
            lg = lg_ref[row, 0:1]
            kc, vc = k_ref[ctx, :].astype(F32), v_ref[ctx, :].astype(F32)
            states_ref[0] = _ret_state0(kc, vc, lg, rev)

            def rows_of(ci, rev=rev):
                return pl.ds(pl.multiple_of((nc - 1 - ci if rev else ci) * RET_CHUNK, RET_CHUNK), RET_CHUNK)

            def load(rows):
                return q_ref[rows, :].astype(F32), k_ref[rows, :].astype(F32), v_ref[rows, :].astype(F32)

            def replay(ci, carry, rev=rev, lg=lg, rows_of=rows_of, load=load):
                states_ref[ci + 1] = _ret_step(states_ref[ci], *load(rows_of(ci)), lg, rev)[0]
                return carry

            lax.fori_loop(0, nc - 1, replay, 0)
            dst_ref[...] = jnp.zeros_like(dst_ref)

            def emit(rows, dq, dk, dv, d=d):
                if d == 0:
                    ak_ref[rows, :], av_ref[rows, :] = dk, dv
                    if dq is not None:
                        aq_ref[rows, :] = dq
                else:
                    dk_ref[rows, :] = (ak_ref[rows, :] + dk).astype(BF16)
                    dv_ref[rows, :] = (av_ref[rows, :] + dv).astype(BF16)
                    if dq is not None:
                        dq_ref[rows, :] = (aq_ref[rows, :] + dq).astype(BF16)

            def back(t, dlg, rev=rev, lg=lg, rows_of=rows_of, load=load, emit=emit):
                ci = nc - 1 - t
                rows = rows_of(ci)
                _, vjp = jax.vjp(lambda st, q, k, v, g: _ret_step(st, q, k, v, g, rev), states_ref[ci], *load(rows), lg)
                dstate, dq, dk, dv, dg = vjp((dst_ref[...], do_ref[rows, :].astype(F32)))
                dst_ref[...] = dstate
                emit(rows, dq, dk, dv)
                return dlg + dg

            dlg = lax.fori_loop(0, nc, back, jnp.zeros((1, 1), F32))
            _, vjp = jax.vjp(lambda kk, vv, g: _ret_state0(kk, vv, g, rev), kc, vc, lg)
            dkc, dvc, dg = vjp(dst_ref[...])
            emit(ctx, None, dkc, dvc)
            dlg_ref[row, :] += jnp.broadcast_to(dlg + dg, (1, LANES))
        dq_ref[ctx, :] = jnp.zeros((geo.l, RET_QK_DIM), BF16)

    nq = RET_HEADS
    return pl.pallas_call(
        kern, name=name, grid=(geo.b, RET_HEADS),
        in_specs=_ret_specs(geo) + [pl.BlockSpec((geo.seg, RET_V_DIM), lambda b, h: (b, h))],
        out_specs=[pl.BlockSpec((geo.seg, RET_QK_DIM), lambda b, h: (b, h)),
                   pl.BlockSpec((geo.seg, RET_QK_DIM), lambda b, h: (b, h)),
                   pl.BlockSpec((geo.seg, RET_V_DIM), lambda b, h: (b, h)),
                   pl.BlockSpec((2 * RET_HEADS, LANES), lambda b, h: (0, 0))],
        out_shape=[jax.ShapeDtypeStruct((geo.r, RET_QK_W), BF16), jax.ShapeDtypeStruct((geo.r, RET_QK_W), BF16),
                   jax.ShapeDtypeStruct((geo.r, RET_VWIDTH), BF16), jax.ShapeDtypeStruct((2 * RET_HEADS, LANES), F32)],
        scratch_shapes=[pltpu.VMEM((nc, RET_QK_DIM, RET_V_DIM), F32), pltpu.VMEM((RET_QK_DIM, RET_V_DIM), F32),
                        pltpu.VMEM((geo.seg, RET_QK_DIM), F32), pltpu.VMEM((geo.seg, RET_QK_DIM), F32),
                        pltpu.VMEM((geo.seg, RET_V_DIM), F32)],
        compiler_params=_cparams("arbitrary", "arbitrary"),
    )(log_g, qkv, qkv, qkv, do)


def _gated(o, g, gain):
    outs = []
    for h in range(RET_HEADS):
        cols = slice(h * RET_V_DIM, (h + 1) * RET_V_DIM)
        oh = o[:, cols]
        mu = jnp.mean(oh, axis=-1, keepdims=True)
        var = jnp.mean(jnp.square(oh - mu), axis=-1, keepdims=True)
        outs.append(_silu(g[:, cols]) * ((oh - mu) * lax.rsqrt(var + EPS) * gain[:, cols]))
    return jnp.concatenate(outs, axis=1)


def _ret_gated(geo, o, proj, gain, name):
    def body(i, ov, gv, gn):
        return _gated(ov, gv, gn)

    gate_block = (2 * RET_QK_W + RET_VWIDTH) // RET_VWIDTH
    return _rowwise(name, body, geo, 128, [(o, "row"), (proj, ("rowc", RET_VWIDTH, gate_block)), (gain, "full")],
                    [("row", RET_VWIDTH, BF16)])


def _ret_gated_bwd(geo, o, proj, gain, dout, name):
    def body(i, ov, gv, gn, dv):
        _, vjp = jax.vjp(_gated, ov, gv, gn)
        return vjp(dv)

    gate_block = (2 * RET_QK_W + RET_VWIDTH) // RET_VWIDTH
    return _rowwise(name, body, geo, 128,
                    [(o, "row"), (proj, ("rowc", RET_VWIDTH, gate_block)), (gain, "full"), (dout, "row")],
                    [("row", RET_VWIDTH, BF16), ("row", RET_VWIDTH, BF16), ("gacc", 1, RET_VWIDTH)])


def _whole(name, fn, out_shapes, *arrays):
    n = len(arrays)

    def kern(*refs):
        res = fn(*[r[...] for r in refs[:n]])
        for ref, val in zip(refs[n:], res):
            ref[...] = val.astype(ref.dtype)

    return pl.pallas_call(kern, name=name, out_shape=out_shapes)(*arrays)


def _rope_tables(geo, head_dim):
    rows = geo.s // GRID_W
    row = jnp.broadcast_to(jnp.arange(rows, dtype=jnp.int32)[:, None], (rows, GRID_W)).reshape(geo.s)
    col = jnp.broadcast_to(jnp.arange(GRID_W, dtype=jnp.int32)[None, :], (rows, GRID_W)).reshape(geo.s)
    axis_dim = head_dim // 2
    inv = ROPE_BASE ** (-jnp.arange(0, axis_dim, 2, dtype=F32) / axis_dim)
    ang_r = row.astype(F32)[:, None] * inv
    ang_c = col.astype(F32)[:, None] * inv
    cos = jnp.concatenate([jnp.cos(ang_r)] * 2 + [jnp.cos(ang_c)] * 2, axis=1)
    sin = jnp.concatenate([-jnp.sin(ang_r), jnp.sin(ang_r), -jnp.sin(ang_c), jnp.sin(ang_c)], axis=1)
    cos = jnp.concatenate([cos, jnp.ones((geo.l, head_dim), F32)], axis=0)
    sin = jnp.concatenate([sin, jnp.zeros((geo.l, head_dim), F32)], axis=0)
    reps = max(1, LANES // head_dim)
    return jnp.tile(cos, (1, reps)), jnp.tile(sin, (1, reps))


def _row_tile(r):
    return next(t for t in (1024, 512, 256, 128) if r % t == 0)


MOD_ROWS = 8


def _local_step(x, ctx, target, sp, wts, mods, plan=None):
    nb, s, d = x.shape
    geo = _Rows(nb, s, ctx.shape[1])
    assert nb + 1 <= MOD_ROWS and d == D_MODEL
    tm = _row_tile(geo.r)
    z = jnp.concatenate([x, ctx], axis=1).reshape(geo.r, d)
    cos64, sin64 = _rope_tables(geo, HEAD_DIM)
    cos256, sin256 = _rope_tables(geo, RET_QK_DIM)
    q_gain = jnp.tile(sp["q_norm"].reshape(1, HEAD_DIM), (1, LANES // HEAD_DIM))
    k_gain = jnp.tile(sp["k_norm"].reshape(1, HEAD_DIM), (1, LANES // HEAD_DIM))
    sink = sp["sink"].reshape(N_HEADS)
    log_g = jnp.broadcast_to(sp["log_g"].reshape(2 * RET_HEADS, 1), (2 * RET_HEADS, LANES))
    gn_g = sp["gn_g"].reshape(1, RET_VWIDTH)

    saved = []
    h1 = _norm_mod(geo, z, sp["norm1_g"][0][None, :], mods[0], 0, "norm1_0")
    for i in range(2):
        mod3 = mods[i]
        n1, n2 = sp["norm1_g"][i][None, :], sp["norm2_g"][i][None, :]
        if i == 0:
            proj = _mm_nn(h1, wts["attn_qkv"], F32, "attn_qkv", tm, wts["attn_qkv"].shape[1], d)
            prep = _attn_prep(geo, proj, cos64, sin64, q_gain, k_gain, "attn_prep")
            o, late = _attention(geo, prep, sink, "attn", plan.gather_job() if plan else None)
            if plan:
                plan.late_weights(late, wts)
            oraw = None
            w_o = wts["attn_o"]
        else:
            proj = _mm_nn(h1, wts["ret_qkvg"], BF16, "ret_qkvg", tm, wts["ret_qkvg"].shape[2], d)
            prep = _ret_prep(geo, proj, cos256, sin256, "ret_prep")
            oraw = _retention(geo, prep, log_g, "ret")
            o = _ret_gated(geo, oraw, proj, gn_g, "ret_gated")
            w_o = wts["ret_o"]
        zmid, mix, h2 = _mm_nn_gate_residual(geo, o, w_o, z, mod3, 2 * d, f"mix_out{i}", norm=(n2, mod3, 3 * d))
        u, a = _ffn_in_swiglu(h2, wts["ffn_in"][i], f"ffn_in{i}")
        next_norm = (sp["norm1_g"][1][None, :], mods[1], 0) if i == 0 else None
        zout, f, h1_next = _mm_nn_gate_residual(geo, a, wts["ffn_out"][i], zmid, mod3, 5 * d, f"ffn_out{i}", norm=next_norm)
        saved.append(dict(z=z, mod3=mod3, n1=n1, n2=n2, h1=h1, proj=proj, prep=prep, o=o, oraw=oraw, mix=mix, zmid=zmid,
                          h2=h2, u=u, a=a, f=f))
        z, h1 = zout, h1_next

    dz, loss, df, dg2 = _loss_head(geo, z, target.reshape(nb * s, d), saved[1]["f"], saved[1]["mod3"], 5 * d, "loss")

    big, small = {}, {}
    dmods = [None, None]
    for i in (1, 0):
        sv = saved[i]
        mod3 = sv["mod3"]
        carry = plan is not None and i == 0
        du, land = _ffn_out_dx_swiglu_bwd(df, wts["ffn_out"][i], sv["u"], f"ffn_out_dx{i}", plan.layer1.swap_job() if carry else None)
        if carry:
            plan.layer1.after_swap(land)
        big[f"ffn_out{i}"] = _mm_tn(sv["a"], df, f"ffn_out_dw{i}", D_FF // 2, 1024, tm, out_dtype=BF16).reshape(N_CHIPS, D_FF // N_CHIPS, d)
        n4 = wts["ffn_in"][i].shape[2]
        dh2 = _mm_nt(du, wts["ffn_in"][i], BF16, f"ffn_in_dx{i}", tm, 1024, n4)
        big[f"ffn_in{i}"] = _mm_tn(sv["h2"], du, f"ffn_in_dw{i}", 1024, n4, tm, shards=N_CHIPS, out_dtype=BF16)
        if carry:
            plan.start_layer0_ffn(big)
        dzmid, dsh2, dsc2, dn2, dmix, dg1, *land = _norm_mod_bwd(geo, sv["zmid"], sv["n2"], mod3, 3 * d, dh2, dz, f"norm2_bwd{i}",
                                                                 gated=(sv["mix"], mod3, 2 * d),
                                                                 job=plan.layer0_ffn.swap_job() if carry else None)
        if carry:
            plan.layer0_ffn.after_swap(land[0])
        if i == 0:
            do = _mm_nt(dmix, wts["attn_o"], BF16, "attn_out_dx", tm, 1024, 1024)
            big["attn_o"] = _mm_tn(sv["o"], dmix, "attn_out_dw", 1024, 1024, tm, out_dtype=BF16).reshape(N_CHIPS, 1024 // N_CHIPS, d)
            dq, dkv, dsink, land = _attention_bwd(geo, sv["prep"], sink, do, "attn_bwd", plan.exchange_job() if plan else None)
            if plan:
                plan.after_exchange(land)
            dproj, dqg, dkg = _attn_prep_bwd(geo, sv["proj"], cos64, sin64, q_gain, k_gain, dq, dkv, "attn_prep_bwd")
            small["q_norm"] = dqg[0, :HEAD_DIM] + dqg[0, HEAD_DIM:]
            small["k_norm"] = dkg[0, :HEAD_DIM] + dkg[0, HEAD_DIM:]
            small["sink"] = dsink[:, 0]
            wq = wts["attn_qkv"]
            dh1 = _mm_nt(dproj, wq, BF16, "attn_qkv_dx", tm, 1024, wq.shape[1])
            dwq = _mm_tn(sv["h1"], dproj, "attn_qkv_dw", 1024, wq.shape[1], tm, out_dtype=BF16)
            big["attn_qkv"] = dwq.reshape(d, N_CHIPS, -1).transpose(1, 0, 2)
        else:
            do = _mm_nt(dmix, wts["ret_o"], BF16, "ret_out_dx", tm, 1024, 1024)
            big["ret_o"] = _mm_tn(sv["o"], dmix, "ret_out_dw", 1024, 1024, tm, out_dtype=BF16).reshape(N_CHIPS, RET_VWIDTH // N_CHIPS, d)
            doraw, dgate, dgn = _ret_gated_bwd(geo, sv["oraw"], sv["proj"], gn_g, do, "ret_gated_bwd")
            small["gn_g"] = dgn[0]
            dq, dk, dv, dlg = _retention_bwd(geo, sv["prep"], log_g, doraw, "ret_bwd")
            small["log_g"] = dlg[:, 0].reshape(2, RET_HEADS)
            dproj = _ret_prep_bwd(geo, dq, dk, dv, dgate, cos256, sin256, "ret_prep_bwd")
            wq = wts["ret_qkvg"]
            dh1 = _mm_nt(dproj, wq, BF16, "ret_qkvg_dx", tm, 1024, wq.shape[2])
            big["ret_qkvg"] = _mm_tn(sv["h1"], dproj, "ret_qkvg_dw", 1024, wq.shape[2], tm, shards=N_CHIPS, out_dtype=BF16)
        below = (saved[0]["f"], saved[0]["mod3"], 5 * d) if i == 1 else None
        dz, dsh1, dsc1, dn1, *below_grads = _norm_mod_bwd(geo, sv["z"], sv["n1"], mod3, 0, dh1, dzmid, f"norm1_bwd{i}", gated=below,
                                                              latent_only=i == 0)
        small[f"norm1_g{i}"], small[f"norm2_g{i}"] = dn1[0], dn2[0]
        parts = [dsh1, dsc1, dg1, dsh2, dsc2, dg2]
        rows = jnp.concatenate([jnp.concatenate([p[:nb, 0, :] for p in parts], axis=1),
                                jnp.concatenate([jnp.sum(p[nb:, 0, :], axis=0, keepdims=True) for p in parts], axis=1),
                                jnp.zeros((MOD_ROWS - nb - 1, 6 * d), F32)], axis=0)
        dmods[i] = rows
        if below_grads:
            df, dg2 = below_grads
        small[f"ada_b{i}"] = jnp.sum(rows, axis=0)
        if plan and i == 1:
            plan.start_layer1(big)
    return loss, dz, big, small, dmods


def _adamw(w, g, m, v, name):
    rows, cols = w.shape
    tr = next((t for t in (256, 128, 64, 32, 16, 8) if rows % t == 0), rows)
    c1 = 1.0 - ADAM_B1 ** ADAM_STEP
    c2 = 1.0 - ADAM_B2 ** ADAM_STEP

    def kern(w_ref, g_ref, m_ref, v_ref, d_ref, nm_ref, nv_ref):
        gv = g_ref[...]
        nm = ADAM_B1 * m_ref[...] + (1.0 - ADAM_B1) * gv
        nv = ADAM_B2 * v_ref[...] + (1.0 - ADAM_B2) * jnp.square(gv)
        d_ref[...] = -ADAM_LR * ((nm / c1) / (jnp.sqrt(nv / c2) + ADAM_EPS) + ADAM_WD * w_ref[...])
        nm_ref[...] = nm
        nv_ref[...] = nv

    spec = pl.BlockSpec((tr, cols), lambda i: (i, 0))
    return pl.pallas_call(
        kern, name=name, grid=(rows // tr,), in_specs=[spec] * 4, out_specs=[spec] * 3,
        out_shape=[jax.ShapeDtypeStruct(w.shape, F32)] * 3, compiler_params=_cparams("parallel"),
    )(w, g, m, v)


N_DEVICES = 8


def _mesh_pos():
    return lax.axis_index("x"), lax.axis_index("y"), lax.axis_index("c")


def _other_chips(x, y):
    return [(1 - x, y), (x, 1 - y), (1 - x, 1 - y)]


def _hbm(n):
    return [pl.BlockSpec(memory_space=pl.ANY)] * n


def _remote(src, dst, send_sem, recv_sem, device):
    return pltpu.make_async_remote_copy(src_ref=src, dst_ref=dst, send_sem=send_sem, recv_sem=recv_sem,
                                        device_id=device, device_id_type=MESH)


def _scalar_spec(grid, in_specs, out_specs):
    return pltpu.PrefetchScalarGridSpec(num_scalar_prefetch=1, grid=grid, in_specs=in_specs, out_specs=out_specs)


def _place_shard(param, layer, pos, name):
    _, r, cols = param.shape
    tr = _slab_tile(r)

    def kern(pos_ref, s_ref, o_ref):
        o_ref[...] = s_ref[...].astype(BF16)

    return pl.pallas_call(
        kern, name=name, out_shape=jax.ShapeDtypeStruct((N_CHIPS, r, cols), BF16),
        grid_spec=_scalar_spec((r // tr,), [pl.BlockSpec((None, tr, cols), lambda i, p: (layer, i, 0))],
                               pl.BlockSpec((None, tr, cols), lambda i, p: (p[1], i, 0))),
        compiler_params=_cparams("parallel"),
    )(pos, param)


class _CommJob:
    def __init__(self, inputs, out_shapes, aliases, sem_shapes, stages, fractions=None):
        self.inputs, self.out_shapes, self.aliases, self.sem_shapes, self.stages = inputs, out_shapes, aliases, sem_shapes, stages
        self.fractions = fractions


def _merge_jobs(a, b):
    assert len(a.stages) == len(b.stages)
    ni, no, ns = len(a.inputs), len(a.out_shapes), len(a.sem_shapes)

    def both(sa, sb):
        def stage(ins, outs, sems):
            sa(ins[:ni], outs[:no], sems[:ns])
            sb(ins[ni:], outs[no:], sems[ns:])
        return stage

    aliases = dict(a.aliases)
    aliases.update({ni + i: no + o for i, o in b.aliases.items()})
    return _CommJob(a.inputs + b.inputs, a.out_shapes + b.out_shapes, aliases, a.sem_shapes + b.sem_shapes,
                    [both(sa, sb) for sa, sb in zip(a.stages, b.stages)])


def _run_job(job, name):
    n_in, n_out = len(job.inputs), len(job.out_shapes)

    def body(*refs):
        for stage in job.stages:
            stage(refs[:n_in], refs[n_in:n_in + n_out], refs[n_in + n_out:])

    return pl.pallas_call(
        body, name=name, in_specs=_hbm(n_in), out_specs=_hbm(n_out), out_shape=job.out_shapes,
        input_output_aliases=job.aliases, scratch_shapes=job.sem_shapes,
    )(*job.inputs)


def _job_marks(job, steps):
    mid = len(job.stages) - 2
    fractions = job.fractions or [(s + 1) / (mid + 1) for s in range(mid)]
    return [0] + [min(steps - 1, 1 + int((steps - 1) * f)) for f in fractions] + [steps - 1]


def _gather_job(placed):
    n = len(placed)

    def half(w, which):
        r2 = placed[w].shape[1] // 2
        return pl.ds(which * r2, r2)

    def ici_copies(outs, sems, slot_of, arrays=range(n)):
        x, y, c = _mesh_pos()
        res = []
        for w in arrays:
            for k, (px, py) in enumerate(_other_chips(x, y)):
                slab = outs[w].at[slot_of(x, y, px, py), half(w, c)]
                res.append((slab, _remote(slab, slab, sems[0].at[w, k], sems[1].at[w, k], (px, py, c))))
        return res

    def forwards(outs, sems, which_core, arrays=range(n)):
        x, y, c = _mesh_pos()
        res = []
        for w in arrays:
            for k, (px, py) in enumerate(_other_chips(x, y)):
                slab = outs[w].at[2 * px + py, half(w, which_core(c))]
                res.append(_remote(slab, slab, sems[2].at[w, k], sems[3].at[w, k], (x, y, 1 - c)))
        return res

    def send(ins, outs, sems):
        for _, cp in ici_copies(outs, sems, lambda x, y, px, py: 2 * x + y):
            cp.start()

    def forward_of(w):
        def forward(ins, outs, sems):
            arrivals = ici_copies(outs, sems, lambda x, y, px, py: 2 * px + py, [w])
            for (_, arrival), fwd in zip(arrivals, forwards(outs, sems, lambda c: c, [w])):
                arrival.wait_recv()
                fwd.start()
        return forward

    def finish(ins, outs, sems):
        for cp in forwards(outs, sems, lambda c: 1 - c):
            cp.wait_recv()
        for _, cp in ici_copies(outs, sems, lambda x, y, px, py: 2 * x + y):
            cp.wait_send()
        for cp in forwards(outs, sems, lambda c: c):
            cp.wait_send()

    sizes = [p.shape[1] * p.shape[2] for p in placed]
    fractions = [sum(sizes[:w + 1]) / sum(sizes) for w in range(n)]
    return _CommJob(list(placed), [jax.ShapeDtypeStruct(p.shape, p.dtype) for p in placed], {w: w for w in range(n)},
                    [pltpu.SemaphoreType.DMA((n, 3))] * 4, [send] + [forward_of(w) for w in range(n)] + [finish], fractions)


def _pair_swap_job(grads):
    n = len(grads)

    def copies(ins, outs, sems):
        x, y, c = _mesh_pos()
        res = []
        for w in range(n):
            r2 = grads[w].shape[1] // 2
            res.append(_remote(ins[w].at[:, pl.ds((1 - c) * r2, r2)], outs[w], sems[0].at[w], sems[1].at[w], (x, y, 1 - c)))
        return res

    def send(ins, outs, sems):
        for cp in copies(ins, outs, sems):
            cp.start()

    def finish(ins, outs, sems):
        for cp in copies(ins, outs, sems):
            cp.wait()

    return _CommJob(list(grads), [jax.ShapeDtypeStruct((N_CHIPS, g.shape[1] // 2, g.shape[2]), g.dtype) for g in grads], {},
                    [pltpu.SemaphoreType.DMA((n,))] * 2, [send, finish])


def _chip_exchange_job(hs):
    n = len(hs)

    def send(ins, outs, sems):
        x, y, c = _mesh_pos()
        for w in range(n):
            for k, (px, py) in enumerate(_other_chips(x, y)):
                _remote(ins[w].at[2 * px + py], outs[w].at[2 * x + y], sems[0].at[w, k], sems[1].at[w, k], (px, py, c)).start()

    def finish(ins, outs, sems):
        x, y, c = _mesh_pos()
        for w in range(n):
            for k, (px, py) in enumerate(_other_chips(x, y)):
                got = outs[w].at[2 * px + py]
                cp = _remote(ins[w].at[2 * px + py], got, sems[0].at[w, k], sems[1].at[w, k], (px, py, c))
                cp.wait_recv()
                cp.wait_send()

    return _CommJob(list(hs), [jax.ShapeDtypeStruct(h.shape, h.dtype) for h in hs], {},
                    [pltpu.SemaphoreType.DMA((n, 3))] * 2, [send, finish])


def _pair_share(ts, name):
    n = len(ts)

    def body(*refs):
        outs = refs[n:2 * n]
        send_sems, recv_sems = refs[2 * n:]
        x, y, c = _mesh_pos()
        sends = []
        for w in range(n):
            r2 = ts[w].shape[0] // 2
            mine = outs[w].at[pl.ds(c * r2, r2)]
            rc = _remote(mine, mine, send_sems.at[w], recv_sems.at[w], (x, y, 1 - c))
            rc.start()
            sends.append(rc)
        for w in range(n):
            r2 = ts[w].shape[0] // 2
            theirs = outs[w].at[pl.ds((1 - c) * r2, r2)]
            _remote(theirs, theirs, send_sems.at[w], recv_sems.at[w], (x, y, 1 - c)).wait_recv()
            sends[w].wait_send()

    return pl.pallas_call(
        body, name=name, in_specs=_hbm(n), out_specs=_hbm(n),
        out_shape=[jax.ShapeDtypeStruct(t.shape, F32) for t in ts],
        input_output_aliases={w: w for w in range(n)},
        scratch_shapes=[pltpu.SemaphoreType.DMA((n,))] * 2,
    )(*ts)


def _slab_tile(rows):
    return next(t for t in (512, 256, 176, 128, 64, 32, 16) if rows % t == 0)


def _sum_pair(grad, land, pos, name):
    _, r2, cols = land.shape
    tr = _slab_tile(r2)
    nt = r2 // tr

    def kern(pos_ref, a_ref, b_ref, o_ref):
        o_ref[...] = (a_ref[...].astype(F32) + b_ref[...].astype(F32)).astype(BF16)

    spec = pl.BlockSpec((None, tr, cols), lambda j, i, p: (j, i, 0))
    return pl.pallas_call(
        kern, name=name, out_shape=jax.ShapeDtypeStruct(land.shape, BF16),
        grid_spec=_scalar_spec((N_CHIPS, nt), [pl.BlockSpec((None, tr, cols), lambda j, i, p: (j, p[0] * nt + i, 0)), spec], spec),
        compiler_params=_cparams("parallel", "parallel"),
    )(pos, grad, land)


def _sum_chips(hs, land, pos, name):
    _, r2, cols = land.shape
    tr = _slab_tile(r2)
    nt = r2 // tr

    def kern(pos_ref, h_ref, l_ref, o_ref):
        acc = jnp.zeros((tr, cols), F32)
        own = h_ref[...].astype(F32)
        for k in range(N_CHIPS):
            acc = acc + jnp.where(pos_ref[1] == k, own, l_ref[k].astype(F32))
        o_ref[...] = acc

    return pl.pallas_call(
        kern, name=name, out_shape=jax.ShapeDtypeStruct((2 * r2, cols), F32),
        grid_spec=_scalar_spec((nt,), [pl.BlockSpec((None, tr, cols), lambda i, p: (p[1], i, 0)),
                                       pl.BlockSpec((N_CHIPS, tr, cols), lambda i, p: (0, i, 0))],
                               pl.BlockSpec((tr, cols), lambda i, p: (p[0] * nt + i, 0))),
        compiler_params=_cparams("parallel"),
    )(pos, hs, land)


class _ReduceScatter:
    def __init__(self, grads, pos, tag):
        self.grads, self.pos, self.tag = list(grads), pos, tag

    def swap_job(self):
        return _pair_swap_job(self.grads)

    def after_swap(self, land):
        self.hs = [_sum_pair(g, l, self.pos, f"grads_pair_sum_{self.tag}{w}") for w, (g, l) in enumerate(zip(self.grads, land))]

    def exchange_job(self):
        return _chip_exchange_job(self.hs)

    def after_exchange(self, land2):
        return [_sum_chips(h, l, self.pos, f"grads_chip_sum_{self.tag}{w}") for w, (h, l) in enumerate(zip(self.hs, land2))]

    def run(self):
        self.after_swap(_run_job(self.swap_job(), f"grads_pair_swap_{self.tag}"))
        return self.after_exchange(_run_job(self.exchange_job(), f"grads_chip_exchange_{self.tag}"))


EARLY_WEIGHTS = ("attn_qkv",)
LATE_WEIGHTS = ("ffn_in0", "ffn_in1", "ffn_out0", "ffn_out1", "attn_o", "ret_qkvg", "ret_o")
LAYER1_GRADS = ("ffn_out1", "ffn_in1", "ret_o", "ret_qkvg")
LAYER0_FFN_GRADS = ("ffn_out0", "ffn_in0")
LAST_GRADS = ("attn_o", "attn_qkv")


def _fill_weights(wts, full):
    for name, w in full.items():
        if name[:-1] == "ffn_in":
            wts[name[:-1]][int(name[-1])] = w
        elif name[:-1] == "ffn_out":
            wts["ffn_out"][int(name[-1])] = w.reshape(-1, w.shape[2])
        elif name in ("attn_o", "ret_o"):
            wts[name] = w.reshape(-1, w.shape[2])
        elif name == "attn_qkv":
            wts[name] = w.transpose(1, 0, 2).reshape(w.shape[1], -1)
        else:
            wts[name] = w


class _StepPlan:
    def __init__(self, placed, pos):
        self.placed, self.pos = placed, pos
        self.layer1 = self.layer0_ffn = None
        self.reduced = {}

    def gather_job(self):
        return _gather_job([self.placed[k] for k in LATE_WEIGHTS])

    def late_weights(self, outs, wts):
        _fill_weights(wts, dict(zip(LATE_WEIGHTS, outs)))

    def start_layer1(self, big):
        self.layer1 = _ReduceScatter([big[k] for k in LAYER1_GRADS], self.pos, "l1_")

    def start_layer0_ffn(self, big):
        self.layer0_ffn = _ReduceScatter([big[k] for k in LAYER0_FFN_GRADS], self.pos, "l0f_")

    def exchange_job(self):
        return _merge_jobs(self.layer1.exchange_job(), self.layer0_ffn.exchange_job())

    def after_exchange(self, land):
        n1 = len(LAYER1_GRADS)
        self.reduced.update(zip(LAYER1_GRADS, self.layer1.after_exchange(land[:n1])))
        self.reduced.update(zip(LAYER0_FFN_GRADS, self.layer0_ffn.after_exchange(land[n1:])))


def _all_reduce_small(v, name):
    def body(v_ref, o_ref, land_ref, send_sems, recv_sems):
        x, y, c = _mesh_pos()
        me = 4 * x + 2 * y + c
        land_ref[me] = v_ref[...]
        for t in range(N_DEVICES):
            @pl.when(t != me)
            def _(t=t):
                _remote(v_ref, land_ref.at[me], send_sems.at[t], recv_sems.at[me], (t // 4, (t // 2) % 2, t % 2)).start()
        for t in range(N_DEVICES):
            @pl.when(t != me)
            def _(t=t):
                _remote(v_ref, land_ref.at[t], send_sems.at[t], recv_sems.at[t], (t // 4, (t // 2) % 2, t % 2)).wait()
        acc = land_ref[0]
        for t in range(1, N_DEVICES):
            acc = acc + land_ref[t]
        o_ref[...] = acc

    vmem = pl.BlockSpec(memory_space=pltpu.VMEM)
    return pl.pallas_call(
        body, name=name, in_specs=[vmem], out_specs=vmem, out_shape=jax.ShapeDtypeStruct(v.shape, F32),
        scratch_shapes=[pltpu.VMEM((N_DEVICES,) + v.shape, F32), pltpu.SemaphoreType.DMA((N_DEVICES,)),
                        pltpu.SemaphoreType.DMA((N_DEVICES,))],
    )(v)


def _all_to_all_small(v, name):
    def body(v_ref, o_ref, send_sems, recv_sems):
        x, y, c = _mesh_pos()
        me = 4 * x + 2 * y + c
        o_ref[me] = v_ref[me]
        for t in range(N_DEVICES):
            @pl.when(t != me)
            def _(t=t):
                _remote(v_ref.at[t], o_ref.at[me], send_sems.at[t], recv_sems.at[me], (t // 4, (t // 2) % 2, t % 2)).start()
        for t in range(N_DEVICES):
            @pl.when(t != me)
            def _(t=t):
                _remote(v_ref.at[t], o_ref.at[t], send_sems.at[t], recv_sems.at[t], (t // 4, (t // 2) % 2, t % 2)).wait()

    vmem = pl.BlockSpec(memory_space=pltpu.VMEM)
    return pl.pallas_call(
        body, name=name, in_specs=[vmem], out_specs=vmem, out_shape=jax.ShapeDtypeStruct(v.shape, F32),
        scratch_shapes=[pltpu.SemaphoreType.DMA((N_DEVICES,)), pltpu.SemaphoreType.DMA((N_DEVICES,))],
    )(v)


ALL_ROWS = 40


class _AdaLN:
    def __init__(self, c, c_ctx, ada_w, ada_b, riders):
        xi, yi, ci = _mesh_pos()
        self.me, self.chip, self.core = 4 * xi + 2 * yi + ci, 2 * xi + yi, ci
        self.nb, d = c.shape
        self.ada_w, self.c_ctx = ada_w, c_ctx
        self.cols = ada_w.shape[2]
        ctx_row = self.nb * N_DEVICES
        assert ctx_row + 1 + riders.shape[0] <= ALL_ROWS
        placed = lax.dynamic_update_slice(jnp.zeros((ALL_ROWS, d), F32), c, (self.me * self.nb, 0))
        placed = lax.dynamic_update_slice(placed, riders, (ctx_row + 1, 0))
        summed = _all_reduce_small(placed, "gather_conditioning")
        self.riders = summed[ctx_row + 1:ctx_row + 1 + riders.shape[0]]
        c_all = summed.at[ctx_row].set(c_ctx)
        self.cact, = _whole("cond_silu", lambda v: (_silu(v),), [jax.ShapeDtypeStruct(c_all.shape, F32)], c_all)
        parts = []
        for i in range(2):
            bias = lax.dynamic_slice(ada_b[i], (self.chip * self.cols,), (self.cols,))[None, :]
            parts.append(_mm_nn(self.cact, ada_w[i], F32, f"mod{i}", ALL_ROWS, self.cols, d, bias=bias))
        part = jnp.concatenate(parts, axis=1)
        rows = [[t * self.nb + b for b in range(self.nb)] + [ctx_row] * (MOD_ROWS - self.nb) for t in range(N_DEVICES)]
        got = _all_to_all_small(part[jnp.asarray(rows)], "mod_exchange")
        self.mods = [jnp.concatenate([got[2 * j][:self.nb + 1, i * self.cols:(i + 1) * self.cols] for j in range(N_CHIPS)], axis=1)[:, None, :]
                     for i in range(2)]

    def backward(self, dmods):
        nb, cols, d = self.nb, self.cols, self.ada_w.shape[1]
        blocks = [jnp.concatenate([dm[:, j * cols:(j + 1) * cols] for dm in dmods], axis=1) for j in range(N_CHIPS)]
        got = _all_to_all_small(jnp.stack([blocks[t // 2] for t in range(N_DEVICES)]), "dmod_exchange")
        dall = jnp.concatenate([got[:, :nb].reshape(N_DEVICES * nb, 2 * cols), jnp.sum(got[:, nb], axis=0, keepdims=True),
                                jnp.zeros((ALL_ROWS - N_DEVICES * nb - 1, 2 * cols), F32)], axis=0)
        dctx = jnp.concatenate([dall[N_DEVICES * nb][None, :], jnp.zeros((MOD_ROWS - 1, 2 * cols), F32)], axis=0)
        grads, dcact = [], []
        for i in range(2):
            grads.append(_mm_tn(self.cact, dall[:, i * cols:(i + 1) * cols], f"ada_dw{i}", d, cols, ALL_ROWS))
            dcact.append(_mm_nt(dctx[:, i * cols:(i + 1) * cols], self.ada_w[i], F32, f"ada_dx{i}", MOD_ROWS, d, cols))

        def silu_bwd(v, d0, d1):
            sg = _sigmoid(v)
            return ((d0 + d1)[0:1] * (sg * (1.0 + v * (1.0 - sg))),)

        dc_ctx, = _whole("cond_silu_bwd", silu_bwd, [jax.ShapeDtypeStruct((1, d), F32)], self.c_ctx[None, :], dcact[0], dcact[1])
        return grads, jnp.where(self.core == 0, dc_ctx[0], jnp.zeros((d,), F32))


SMALL_ROWS = 24


def _pack_small(small, dlogit):
    d = D_MODEL
    misc = jnp.zeros((d,), F32)
    misc = misc.at[0:HEAD_DIM].set(small["q_norm"]).at[128:128 + HEAD_DIM].set(small["k_norm"])
    misc = misc.at[256:256 + N_HEADS].set(small["sink"]).at[384:384 + 2 * RET_HEADS].set(dlogit.reshape(-1))
    rows = [small["ada_b0"].reshape(6, d), small["ada_b1"].reshape(6, d), small["norm1_g0"][None], small["norm1_g1"][None],
            small["norm2_g0"][None], small["norm2_g1"][None], small["c_ctx"][None], small["gn_g"].reshape(2, d), misc[None]]
    buf = jnp.concatenate(rows, axis=0)
    return jnp.concatenate([buf, jnp.zeros((SMALL_ROWS - buf.shape[0], d), F32)], axis=0)


def _unpack_small(buf):
    d = D_MODEL
    misc = buf[19]
    return dict(ada_b=buf[0:12].reshape(2, 6 * d), norm1_g=buf[12:14], norm2_g=buf[14:16], c_ctx=buf[16],
                gn_g=buf[17:19].reshape(2 * d), q_norm=misc[0:HEAD_DIM], k_norm=misc[128:128 + HEAD_DIM],
                sink=misc[256:256 + N_HEADS], decay=misc[384:384 + 2 * RET_HEADS])


def kernel(x, c, ctx, c_ctx, ada_w, ada_b, norm1_g, norm2_g, ffn_w_in, ffn_w_out, attn_w_qkv, attn_q_norm, attn_k_norm, attn_sink, attn_w_o, ret_w_qkvg, ret_decay_logit, ret_gn_g, ret_w_o, loss_target, m_c_ctx, m_ada_w, m_ada_b, m_norm1_g, m_norm2_g, m_ffn_w_in, m_ffn_w_out, m_attn_w_qkv, m_attn_q_norm, m_attn_k_norm, m_attn_sink, m_attn_w_o, m_ret_w_qkvg, m_ret_decay_logit, m_ret_gn_g, m_ret_w_o, v_c_ctx, v_ada_w, v_ada_b, v_norm1_g, v_norm2_g, v_ffn_w_in, v_ffn_w_out, v_attn_w_qkv, v_attn_q_norm, v_attn_k_norm, v_attn_sink, v_attn_w_o, v_ret_w_qkvg, v_ret_decay_logit, v_ret_gn_g, v_ret_w_o):
    xi, yi, ci = _mesh_pos()
    chip = 2 * xi + yi
    nb, s, d = x.shape
    gn_shard = ret_gn_g.shape[1]

    shards = dict(ffn_in0=(ffn_w_in, 0), ffn_in1=(ffn_w_in, 1), ffn_out0=(ffn_w_out, 0),
                  ffn_out1=(ffn_w_out, 1), attn_qkv=(attn_w_qkv, 0), attn_o=(attn_w_o, 0), ret_qkvg=(ret_w_qkvg, 0), ret_o=(ret_w_o, 0))
    names = list(shards)
    pos = jnp.stack([ci, chip]).astype(jnp.int32)
    placed = {k: _place_shard(*shards[k], pos, f"place_{k}") for k in names}
    early = _run_job(_gather_job([placed[k] for k in EARLY_WEIGHTS]), "gather_early_weights")
    gn_mine = jnp.where(ci == 0, ret_gn_g[0], jnp.zeros_like(ret_gn_g[0]))
    gn_place = lax.dynamic_update_slice(jnp.zeros((RET_VWIDTH,), F32), gn_mine, (chip * gn_shard,))

    wts = dict(ffn_in=[None, None], ffn_out=[None, None], attn_qkv=None, attn_o=None, ret_qkvg=None, ret_o=None)
    ada = _AdaLN(c, c_ctx, ada_w, ada_b, riders=gn_place.reshape(2, d))
    gn_full = ada.riders.reshape(RET_VWIDTH)
    _fill_weights(wts, dict(zip(EARLY_WEIGHTS, early)))
    plan = _StepPlan(placed, pos)
    decay_logit = ret_decay_logit[0]
    sp = dict(norm1_g=norm1_g, norm2_g=norm2_g, q_norm=attn_q_norm[0], k_norm=attn_k_norm[0],
              sink=attn_sink[0], log_g=jax.nn.log_sigmoid(decay_logit), gn_g=gn_full)
    loss_part, dz, big, small, dmods = _local_step(x, ctx, loss_target, sp, wts, ada.mods, plan)
    ada_grads, small["c_ctx"] = ada.backward(dmods)

    loss = lax.psum(loss_part[0, 0], ("x", "y", "c"))
    grad_x = dz.reshape(nb, s, d)

    dlogit = small["log_g"] * jax.nn.sigmoid(-decay_logit)
    sg = _unpack_small(_all_reduce_small(_pack_small(small, dlogit), "reduce_small_grads"))
    halves = dict(plan.reduced)
    halves.update(zip(LAST_GRADS, _ReduceScatter([big[k] for k in LAST_GRADS], pos, "last_").run()))
    reduced = dict(zip(halves, _pair_share(list(halves.values()), "grads_pair_share")))

    grads = dict(
        c_ctx=sg["c_ctx"], ada_w=jnp.stack(ada_grads), ada_b=sg["ada_b"], norm1_g=sg["norm1_g"],
        norm2_g=sg["norm2_g"], ffn_w_in=jnp.stack([reduced["ffn_in0"], reduced["ffn_in1"]]),
        ffn_w_out=jnp.stack([reduced["ffn_out0"], reduced["ffn_out1"]]), attn_w_qkv=reduced["attn_qkv"][None],
        attn_q_norm=sg["q_norm"][None], attn_k_norm=sg["k_norm"][None], attn_sink=sg["sink"][None],
        attn_w_o=reduced["attn_o"][None], ret_w_qkvg=reduced["ret_qkvg"][None], ret_decay_logit=sg["decay"].reshape(1, 2, RET_HEADS),
        ret_gn_g=lax.dynamic_slice(sg["gn_g"], (chip * gn_shard,), (gn_shard,))[None], ret_w_o=reduced["ret_o"][None])
    params = dict(c_ctx=(c_ctx, m_c_ctx, v_c_ctx), ada_w=(ada_w, m_ada_w, v_ada_w), ada_b=(ada_b, m_ada_b, v_ada_b),
                  norm1_g=(norm1_g, m_norm1_g, v_norm1_g), norm2_g=(norm2_g, m_norm2_g, v_norm2_g),
                  ffn_w_in=(ffn_w_in, m_ffn_w_in, v_ffn_w_in), ffn_w_out=(ffn_w_out, m_ffn_w_out, v_ffn_w_out),
                  attn_w_qkv=(attn_w_qkv, m_attn_w_qkv, v_attn_w_qkv), attn_q_norm=(attn_q_norm, m_attn_q_norm, v_attn_q_norm),
                  attn_k_norm=(attn_k_norm, m_attn_k_norm, v_attn_k_norm), attn_sink=(attn_sink, m_attn_sink, v_attn_sink),
                  attn_w_o=(attn_w_o, m_attn_w_o, v_attn_w_o), ret_w_qkvg=(ret_w_qkvg, m_ret_w_qkvg, v_ret_w_qkvg),
                  ret_decay_logit=(ret_decay_logit, m_ret_decay_logit, v_ret_decay_logit),
                  ret_gn_g=(ret_gn_g, m_ret_gn_g, v_ret_gn_g), ret_w_o=(ret_w_o, m_ret_w_o, v_ret_w_o))
    order = list(params)
    deltas, new_m, new_v = [], [], []
    for k in order:
        w, m, v = params[k]
        g = grads[k].reshape(w.shape)
        grads[k] = g
        flat = (-1, w.shape[-1]) if w.ndim > 1 else (1, -1)
        if k == "ret_decay_logit":
            flat = (1, -1)
        dw, nm, nv = _adamw(w.reshape(flat), g.reshape(flat), m.reshape(flat), v.reshape(flat), f"adamw_{k}")
        deltas.append(dw.reshape(w.shape))
        new_m.append(nm.reshape(w.shape))
        new_v.append(nv.reshape(w.shape))
    return (loss, grad_x, *[grads[k] for k in order], *deltas, *new_m, *new_v)
```

```python
import functools

import jax
import jax.numpy as jnp
from jax import lax
from jax.experimental import pallas as pl
from jax.experimental.pallas import tpu as pltpu

F32 = jnp.float32
BF16 = jnp.bfloat16

D_MODEL = 1024
N_HEADS = 16
N_KV_HEADS = 4
HEAD_DIM = 64
WINDOW = 128
ATTN_BLOCK = 128
BAND = ATTN_BLOCK + 2 * WINDOW
RET_HEADS = 4
RET_QK_DIM = 256
RET_V_DIM = 512
RET_VWIDTH = 2048
RET_CHUNK = 128
D_FF = 2816
GRID_W = 64
ROPE_BASE = 10000.0
EPS = 1e-6
NEG_INF = -1e30
LANES = 128

ADAM_LR = 0.001
ADAM_B1 = 0.9
ADAM_B2 = 0.999
ADAM_EPS = 1e-08
ADAM_WD = 0.01
ADAM_STEP = 10

VMEM_LIMIT_BYTES = 56 * 1024 * 1024
MESH = pl.DeviceIdType.MESH
N_CHIPS = 4


def _cparams(*sem):
    return pltpu.CompilerParams(dimension_semantics=sem, vmem_limit_bytes=VMEM_LIMIT_BYTES)


_DIMS = {"nn": ((1,), (0,)), "nt": ((1,), (1,)), "tn": ((0,), (0,))}


def _dot(a, b, form):
    return lax.dot_general(a.astype(BF16), b.astype(BF16), (_DIMS[form], ((), ())), preferred_element_type=F32)


@functools.partial(jax.custom_vjp, nondiff_argnums=(2,))
def _mm(a, b, form):
    return _dot(a, b, form)


def _mm_fwd(a, b, form):
    return _dot(a, b, form), (a, b)


def _mm_bwd(form, res, ct):
    a, b = res
    if form == "nn":
        da, db = _dot(ct, b, "nt"), _dot(a, ct, "tn")
    elif form == "nt":
        da, db = _dot(ct, b, "nn"), _dot(ct, a, "tn")
    else:
        da, db = _dot(b, ct, "nt"), _dot(a, ct, "nn")
    return da.astype(a.dtype), db.astype(b.dtype)


_mm.defvjp(_mm_fwd, _mm_bwd)


def _swap_halves(x, half):
    w = x.shape[-1]
    lane = lax.broadcasted_iota(jnp.int32, x.shape, x.ndim - 1)
    return jnp.where(lane % (2 * half) < half, pltpu.roll(x, w - half, x.ndim - 1), pltpu.roll(x, half, x.ndim - 1))


@functools.partial(jax.custom_vjp, nondiff_argnums=(1,))
def _rot(x, half):
    return _swap_halves(x, half)


def _rot_fwd(x, half):
    return _swap_halves(x, half), None


def _rot_bwd(half, _, ct):
    return (_swap_halves(ct, half),)


_rot.defvjp(_rot_fwd, _rot_bwd)


def _rope(x, cos, sin_signed, half):
    return x * cos + _rot(x, half) * sin_signed


def _head_mean_square(x):
    r = lax.broadcasted_iota(jnp.int32, (LANES, LANES), 0) // HEAD_DIM
    c = lax.broadcasted_iota(jnp.int32, (LANES, LANES), 1) // HEAD_DIM
    g = jnp.where(r == c, 1.0 / HEAD_DIM, 0.0).astype(F32)
    return jnp.dot(x * x, g, precision=lax.Precision.HIGHEST, preferred_element_type=F32)


def _qk_chunk(x, gain, cos, sin_signed, scale):
    y = x * lax.rsqrt(_head_mean_square(x) + EPS) * gain
    return _rope(y, cos, sin_signed, HEAD_DIM // 4) * scale


def _sigmoid(x):
    return 1.0 / (1.0 + jnp.exp(-x))


def _silu(x):
    return x * _sigmoid(x)


def _mm_nn(a, w, out_dtype, name, tm, tn, tk, bias=None):
    m, k_dim = a.shape
    if w.ndim == 3:
        n = w.shape[0] * w.shape[2]
        per = w.shape[2] // tn
        assert w.shape[2] % tn == 0
        w_spec = pl.BlockSpec((None, tk, tn), lambda i, j, k: (j // per, k, j % per))
    else:
        n = w.shape[1]
        w_spec = pl.BlockSpec((tk, tn), lambda i, j, k: (k, j))
    assert m % tm == 0 and n % tn == 0 and k_dim % tk == 0, (name, a.shape, w.shape, tm, tn, tk)
    nk = k_dim // tk
    has_bias = bias is not None

    def body(*refs):
        a_ref, w_ref = refs[0], refs[1]
        b_ref = refs[2] if has_bias else None
        o_ref, acc_ref = (refs[-1], None) if nk == 1 else (refs[-2], refs[-1])
        if nk == 1:
            part = jnp.dot(a_ref[...].astype(BF16), w_ref[...].astype(BF16), preferred_element_type=F32)
            o_ref[...] = (part + b_ref[...] if has_bias else part).astype(out_dtype)
            return
        k = pl.program_id(2)

        @pl.when(k == 0)
        def _():
            acc_ref[...] = jnp.zeros_like(acc_ref)

        acc_ref[...] += jnp.dot(a_ref[...].astype(BF16), w_ref[...].astype(BF16), preferred_element_type=F32)

        @pl.when(k == nk - 1)
        def _():
            r = acc_ref[...]
            if has_bias:
                r = r + b_ref[...]
            o_ref[...] = r.astype(out_dtype)

    in_specs = [pl.BlockSpec((tm, tk), lambda i, j, k: (i, k)), w_spec]
    args = [a, w]
    if has_bias:
        in_specs.append(pl.BlockSpec((1, tn), lambda i, j, k: (0, j)))
        args.append(bias)
    return pl.pallas_call(
        body, name=name, grid=(m // tm, n // tn, nk), in_specs=in_specs,
        out_specs=pl.BlockSpec((tm, tn), lambda i, j, k: (i, j)),
        out_shape=jax.ShapeDtypeStruct((m, n), out_dtype),
        scratch_shapes=[pltpu.VMEM((tm, tn), F32)] if nk > 1 else [],
        compiler_params=_cparams("parallel", "parallel", "arbitrary"),
    )(*args)


def _mm_nt(a, w, out_dtype, name, tm, tn, tk):
    if a.ndim == 3:
        planes, m, plane_w = a.shape
        c_dim = planes * plane_w
        a_per = plane_w // tk
        assert plane_w % tk == 0
        a_spec = pl.BlockSpec((None, tm, tk), lambda i, j, k: (k // a_per, i, k % a_per))
    else:
        m, c_dim = a.shape
        a_spec = pl.BlockSpec((tm, tk), lambda i, j, k: (i, k))
    if w.ndim == 3:
        k_out = w.shape[1]
        per = w.shape[2] // tk
        assert w.shape[2] % tk == 0 and w.shape[0] * w.shape[2] == c_dim
        w_spec = pl.BlockSpec((None, tn, tk), lambda i, j, k: (k // per, j, k % per))
    else:
        k_out = w.shape[0]
        assert w.shape[1] == c_dim
        w_spec = pl.BlockSpec((tn, tk), lambda i, j, k: (j, k))
    assert m % tm == 0 and k_out % tn == 0 and c_dim % tk == 0, (name, a.shape, w.shape, tm, tn, tk)
    nk = c_dim // tk

    def body(a_ref, w_ref, o_ref, acc_ref=None):
        if nk == 1:
            o_ref[...] = _dot(a_ref[...], w_ref[...], "nt").astype(out_dtype)
            return
        k = pl.program_id(2)

        @pl.when(k == 0)
        def _():
            acc_ref[...] = jnp.zeros_like(acc_ref)

        acc_ref[...] += _dot(a_ref[...], w_ref[...], "nt")

        @pl.when(k == nk - 1)
        def _():
            o_ref[...] = acc_ref[...].astype(out_dtype)

    return pl.pallas_call(
        body, name=name, grid=(m // tm, k_out // tn, nk),
        in_specs=[a_spec, w_spec],
        out_specs=pl.BlockSpec((tm, tn), lambda i, j, k: (i, j)),
        out_shape=jax.ShapeDtypeStruct((m, k_out), out_dtype),
        scratch_shapes=[pltpu.VMEM((tm, tn), F32)] if nk > 1 else [],
        compiler_params=_cparams("parallel", "parallel", "arbitrary"),
    )(a, w)


def _mm_tn(a, b, name, tm, tn, tk, shards=None, out_dtype=F32):
    r, k_dim = a.shape
    if b.ndim == 3:
        n = b.shape[0] * b.shape[2]
        b_per = b.shape[2] // tn
        assert b.shape[2] % tn == 0
        b_spec = pl.BlockSpec((None, tk, tn), lambda i, j, k: (j // b_per, k, j % b_per))
    else:
        n = b.shape[1]
        b_spec = pl.BlockSpec((tk, tn), lambda i, j, k: (k, j))
    assert r % tk == 0 and k_dim % tm == 0 and n % tn == 0, (name, a.shape, b.shape, tm, tn, tk)
    nk = r // tk
    if shards:
        per = n // shards // tn
        assert n % (shards * tn) == 0
        out_shape = jax.ShapeDtypeStruct((shards, k_dim, n // shards), out_dtype)
        out_spec = pl.BlockSpec((None, tm, tn), lambda i, j, k: (j // per, i, j % per))
    else:
        out_shape = jax.ShapeDtypeStruct((k_dim, n), out_dtype)
        out_spec = pl.BlockSpec((tm, tn), lambda i, j, k: (i, j))
    direct = out_dtype == F32

    def body(a_ref, b_ref, o_ref, *scratch):
        acc_ref = o_ref if direct else scratch[0]
        k = pl.program_id(2)

        @pl.when(k == 0)
        def _():
            acc_ref[...] = jnp.zeros_like(acc_ref)

        acc_ref[...] += _dot(a_ref[...], b_ref[...], "tn")
        if not direct:
            @pl.when(k == nk - 1)
            def _():
                o_ref[...] = acc_ref[...].astype(out_dtype)

    return pl.pallas_call(
        body, name=name, grid=(k_dim // tm, n // tn, nk),
        in_specs=[pl.BlockSpec((tk, tm), lambda i, j, k: (k, i)), b_spec],
        out_specs=out_spec, out_shape=out_shape,
        scratch_shapes=[] if direct else [pltpu.VMEM((tm, tn), F32)],
        compiler_params=_cparams("parallel", "parallel", "arbitrary"),
    )(a, b)


class _Carrier:
    def __init__(self, job, n_in, n_out, n_scratch):
        self.job, self.n_in, self.n_out, self.n_scratch = job, n_in, n_out, n_scratch
        self.ji = len(job.inputs) if job else 0
        self.jo = len(job.out_shapes) if job else 0

    def operands(self):
        return list(self.job.inputs) if self.job else []

    def in_specs(self):
        return [pl.BlockSpec(memory_space=pl.ANY)] * self.ji

    def out_specs(self):
        return [pl.BlockSpec(memory_space=pl.ANY)] * self.jo

    def out_shapes(self):
        return list(self.job.out_shapes) if self.job else []

    def scratch(self):
        return list(self.job.sem_shapes) if self.job else []

    def aliases(self):
        return {self.n_in + a: self.n_out + b for a, b in self.job.aliases.items()} if self.job else {}

    def split(self, refs):
        a = self.n_in
        b = a + self.ji
        c = b + self.n_out
        d = c + self.jo
        e = d + self.n_scratch
        return list(refs[:a]) + list(refs[b:c]) + list(refs[d:e]), (refs[a:b], refs[c:d], refs[e:])

    def run(self, job_refs, step, steps):
        if not self.job:
            return
        for stage, mark in zip(self.job.stages, _job_marks(self.job, steps)):
            pl.when(step == mark)(functools.partial(stage, *job_refs))

    def results(self, res):
        res = list(res)
        return res[:self.n_out], res[self.n_out:]


FFN_ROW_TILE = 768


def _ffn_tile(r):
    return FFN_ROW_TILE if r % FFN_ROW_TILE == 0 else _row_tile(r)


def _ffn_in_swiglu(h, w, name):
    r, k_dim = h.shape
    n4 = w.shape[2]
    tm = _ffn_tile(r)

    def body(h_ref, wg_ref, wu_ref, u_ref, a_ref):
        hv = h_ref[...]
        g = jnp.dot(hv, wg_ref[...], preferred_element_type=F32)
        up = jnp.dot(hv, wu_ref[...], preferred_element_type=F32)
        u_ref[0] = g.astype(BF16)
        u_ref[1] = up.astype(BF16)
        a_ref[...] = (_silu(g) * up).astype(BF16)

    return pl.pallas_call(
        body, name=name, grid=(r // tm, 2),
        in_specs=[pl.BlockSpec((tm, k_dim), lambda i, j: (i, 0)),
                  pl.BlockSpec((None, k_dim, n4), lambda i, j: (j, 0, 0)),
                  pl.BlockSpec((None, k_dim, n4), lambda i, j: (j + 2, 0, 0))],
        out_specs=[pl.BlockSpec((2, tm, n4), lambda i, j: (0, i, j)), pl.BlockSpec((tm, n4), lambda i, j: (i, j))],
        out_shape=[jax.ShapeDtypeStruct((2, r, 2 * n4), BF16), jax.ShapeDtypeStruct((r, 2 * n4), BF16)],
        compiler_params=_cparams("parallel", "parallel"),
    )(h, w, w)


def _mm_nn_gate_residual(geo, a, w, z, mod, off, name, norm=None):
    r, k_dim = a.shape
    n = w.shape[1]
    tm = FFN_ROW_TILE if geo.seg % FFN_ROW_TILE == 0 else 256
    tiles = geo.seg // tm
    assert geo.seg % tm == 0 and r == geo.r and n == D_MODEL

    def body(a_ref, w_ref, z_ref, mx_ref, mc_ref, *rest):
        out = jnp.dot(a_ref[...], w_ref[...], preferred_element_type=F32)
        is_x = (pl.program_id(0) % tiles) * tm + lax.broadcasted_iota(jnp.int32, (tm, 1), 0) < geo.s
        zo = z_ref[...] + jnp.where(is_x, mx_ref[:, off:off + n], mc_ref[:, off:off + n]) * out
        if norm:
            g_ref, nx_ref, nc_ref, zo_ref, raw_ref, h_ref = rest
            no = norm[2]
            shift = jnp.where(is_x, nx_ref[:, no:no + n], nc_ref[:, no:no + n])
            scale = jnp.where(is_x, nx_ref[:, no + n:no + 2 * n], nc_ref[:, no + n:no + 2 * n])
            rs = lax.rsqrt(jnp.mean(zo * zo, axis=-1, keepdims=True) + EPS)
            h_ref[...] = ((zo * rs) * g_ref[...] * (1.0 + scale) + shift).astype(BF16)
        else:
            zo_ref, raw_ref = rest
        zo_ref[...] = zo
        raw_ref[...] = out.astype(BF16)

    def mod_specs(m):
        return [pl.BlockSpec((None, 1, m.shape[2]), lambda i: (i // tiles, 0, 0)), pl.BlockSpec((None, 1, m.shape[2]), lambda i: (geo.b, 0, 0))]

    row = pl.BlockSpec((tm, n), lambda i: (i, 0))
    in_specs = [pl.BlockSpec((tm, k_dim), lambda i: (i, 0)), pl.BlockSpec((k_dim, n), lambda i: (0, 0)), row] + mod_specs(mod)
    args = [a, w, z, mod, mod]
    out_specs, out_shape = [row, row], [jax.ShapeDtypeStruct((r, n), F32), jax.ShapeDtypeStruct((r, n), BF16)]
    if norm:
        in_specs += [pl.BlockSpec((1, n), lambda i: (0, 0))] + mod_specs(norm[1])
        args += [norm[0], norm[1], norm[1]]
        out_specs.append(row)
        out_shape.append(jax.ShapeDtypeStruct((r, n), BF16))
    res = pl.pallas_call(body, name=name, grid=(r // tm,), in_specs=in_specs, out_specs=out_specs, out_shape=out_shape,
                         compiler_params=_cparams("parallel"))(*args)
    return res if norm else (*res, None)


def _ffn_out_dx_swiglu_bwd(df, w_out, u, name, job=None):
    r, d = df.shape
    n4 = u.shape[2] // 2
    tm = _ffn_tile(r)
    carrier = _Carrier(job, 3, 1, 0)
    steps = (r // tm) * 2

    def body(*refs):
        (df_ref, w_ref, u_ref, du_ref), job_refs = carrier.split(refs)
        carrier.run(job_refs, pl.program_id(0) * 2 + pl.program_id(1), steps)
        da = _dot(df_ref[...], w_ref[...], "nt")
        g, up = u_ref[0].astype(F32), u_ref[1].astype(F32)
        s = _sigmoid(g)
        du_ref[0] = (da * up * (s * (1.0 + g * (1.0 - s)))).astype(BF16)
        du_ref[1] = (da * (g * s)).astype(BF16)

    res = pl.pallas_call(
        body, name=name, grid=(r // tm, 2),
        in_specs=[pl.BlockSpec((tm, d), lambda i, j: (i, 0)), pl.BlockSpec((n4, d), lambda i, j: (j, 0)),
                  pl.BlockSpec((2, tm, n4), lambda i, j: (0, i, j))] + carrier.in_specs(),
        out_specs=[pl.BlockSpec((2, tm, n4), lambda i, j: (0, i, j))] + carrier.out_specs(),
        out_shape=[jax.ShapeDtypeStruct(u.shape, BF16)] + carrier.out_shapes(),
        scratch_shapes=carrier.scratch(), input_output_aliases=carrier.aliases(),
        compiler_params=_cparams("arbitrary", "arbitrary"),
    )(df, w_out, u, *carrier.operands())
    (du,), extra = carrier.results(res)
    return du, extra


class _Rows:
    def __init__(self, b, s, l):
        self.b, self.s, self.l = b, s, l
        self.seg = s + l
        self.r = b * self.seg


def _rowwise(name, body, geo, tm, ins, outs, job=None):
    seg_blocks, x_blocks = geo.seg // tm, geo.s // tm
    assert geo.seg % tm == 0 and geo.s % tm == 0
    nb = geo.b

    def is_ctx(i):
        return i % seg_blocks >= x_blocks

    in_specs, args = [], []
    for arr, kind in ins:
        args.append(arr)
        if kind == "row":
            in_specs.append(pl.BlockSpec((tm, arr.shape[1]), lambda i: (i, 0)))
        elif kind == "ex":
            in_specs.append(pl.BlockSpec((None, 1, arr.shape[2]), lambda i: (jnp.where(is_ctx(i), nb, i // seg_blocks), 0, 0)))
        elif kind == "full":
            in_specs.append(pl.BlockSpec(arr.shape, lambda i, nd=arr.ndim: (0,) * nd))
        elif kind == "tab":
            in_specs.append(pl.BlockSpec((tm, arr.shape[1]), lambda i: (i % seg_blocks, 0)))
        elif kind == "xrow":
            in_specs.append(pl.BlockSpec(
                (tm, arr.shape[1]), lambda i: ((i // seg_blocks) * x_blocks + jnp.minimum(i % seg_blocks, x_blocks - 1), 0)))
        else:
            _, width, cb = kind
            in_specs.append(pl.BlockSpec((tm, width), lambda i, cb=cb: (i, cb)))
    out_specs, out_shapes = [], []
    for o in outs:
        if o[0] == "row":
            out_specs.append(pl.BlockSpec((tm, o[1]), lambda i: (i, 0)))
            out_shapes.append(jax.ShapeDtypeStruct((geo.r, o[1]), o[2]))
        elif o[0] == "xrow":
            out_specs.append(pl.BlockSpec(
                (tm, o[1]), lambda i: ((i // seg_blocks) * x_blocks + jnp.minimum(i % seg_blocks, x_blocks - 1), 0)))
            out_shapes.append(jax.ShapeDtypeStruct((geo.b * geo.s, o[1]), o[2]))
        elif o[0] == "exacc":
            out_specs.append(pl.BlockSpec((None, 1, o[1]), lambda i: (jnp.where(is_ctx(i), nb, 0) + i // seg_blocks, 0, 0)))
            out_shapes.append(jax.ShapeDtypeStruct((2 * nb, 1, o[1]), F32))
        else:
            out_specs.append(pl.BlockSpec((o[1], o[2]), lambda i: (0, 0)))
            out_shapes.append(jax.ShapeDtypeStruct((o[1], o[2]), F32))
    n_in = len(ins)
    carrier = _Carrier(job, n_in, len(outs), 0)

    def kern(*refs):
        i = pl.program_id(0)
        refs, job_refs = carrier.split(refs)
        carrier.run(job_refs, i, geo.r // tm)
        res = body(i, *[r[...].astype(F32) for r in refs[:n_in]])
        if not isinstance(res, (tuple, list)):
            res = (res,)
        jj = i % seg_blocks
        first_of_part = (jj == 0) | (jj == x_blocks)
        for o, ref, val in zip(outs, refs[n_in:], res):
            if o[0] == "row":
                ref[...] = val.astype(ref.dtype)
            elif o[0] == "xrow":
                @pl.when(jj < x_blocks)
                def _(ref=ref, val=val):
                    ref[...] = val.astype(ref.dtype)
            else:
                first = first_of_part if o[0] == "exacc" else i == 0

                @pl.when(first)
                def _(ref=ref, val=val):
                    ref[...] = val

                @pl.when(jnp.logical_not(first))
                def _(ref=ref, val=val):
                    ref[...] += val

    res = pl.pallas_call(
        kern, name=name, grid=(geo.r // tm,), in_specs=in_specs + carrier.in_specs(), out_specs=out_specs + carrier.out_specs(),
        out_shape=out_shapes + carrier.out_shapes(), scratch_shapes=carrier.scratch(), input_output_aliases=carrier.aliases(),
        compiler_params=_cparams("arbitrary"),
    )(*args, *carrier.operands())
    own, extra = carrier.results(res)
    if job:
        return (*own, extra)
    return own[0] if len(own) == 1 else own


def _colsum(v):
    return jnp.sum(v, axis=0, keepdims=True)


def _norm_mod(geo, z, gain, mod, off, name):
    d = D_MODEL

    def body(i, zv, g, m):
        r = lax.rsqrt(jnp.mean(zv * zv, axis=-1, keepdims=True) + EPS)
        return (zv * r) * g * (1.0 + m[:, off + d:off + 2 * d]) + m[:, off:off + d]

    return _rowwise(name, body, geo, 256, [(z, "row"), (gain, "full"), (mod, "ex")], [("row", d, BF16)])


def _norm_mod_bwd(geo, z, gain, mod, off, dh, dz_skip, name, gated=None, latent_only=False, job=None):
    d = D_MODEL

    def body(i, zv, g, m, dhv, skip, *rest):
        r = lax.rsqrt(jnp.mean(zv * zv, axis=-1, keepdims=True) + EPS)
        n = zv * r
        dng = dhv * (1.0 + m[:, off + d:off + 2 * d])
        dn = dng * g
        dz = r * (dn - n * jnp.mean(dn * n, axis=-1, keepdims=True)) + skip
        res = (dz, _colsum(dhv), _colsum(dhv * (n * g)), _colsum(dng * n))
        if gated:
            ov, gm = rest
            res += (dz * gm[:, gated[2]:gated[2] + d], _colsum(dz * ov))
        return res

    ins = [(z, "row"), (gain, "full"), (mod, "ex"), (dh, "row"), (dz_skip, "row")]
    outs = [("xrow" if latent_only else "row", d, F32), ("exacc", d), ("exacc", d), ("gacc", 1, d)]
    if gated:
        ins += [(gated[0], "row"), (gated[1], "ex")]
        outs += [("row", d, BF16), ("exacc", d)]
    return _rowwise(name, body, geo, 256, ins, outs, job)


def _loss_head(geo, z, target, out, mod, off, name):
    seg_blocks, x_blocks = geo.seg // 256, geo.s // 256
    d = D_MODEL

    def body(i, zv, tv, ov, m):
        keep = jnp.where(i % seg_blocks >= x_blocks, 0.0, 1.0)
        err = (zv - tv) * keep
        part = 0.5 * jnp.sum(jnp.mean(err * err, axis=-1, keepdims=True), axis=0, keepdims=True)
        dz = err * (1.0 / d)
        return dz, jnp.broadcast_to(part, (1, LANES)), dz * m[:, off:off + d], _colsum(dz * ov)

    return _rowwise(name, body, geo, 256, [(z, "row"), (target, "xrow"), (out, "row"), (mod, "ex")],
                    [("row", d, F32), ("gacc", 1, LANES), ("row", d, BF16), ("exacc", d)])


Q_SCALE = HEAD_DIM ** -0.5
N_QK_CHUNKS = (N_HEADS + N_KV_HEADS) * HEAD_DIM // LANES
N_Q_CHUNKS = N_HEADS * HEAD_DIM // LANES


def _attn_prep(geo, proj, cos, sin_signed, q_gain, k_gain, name):
    def body(i, p, cs, sn, qg, kg):
        outs = []
        for ch in range(N_QK_CHUNKS):
            is_q = ch < N_Q_CHUNKS
            outs.append(_qk_chunk(p[:, ch * LANES:(ch + 1) * LANES], qg if is_q else kg, cs, sn, Q_SCALE if is_q else 1.0))
        outs.append(p[:, N_QK_CHUNKS * LANES:])
        return jnp.concatenate(outs, axis=1)

    return _rowwise(name, body, geo, 256, [(proj, "row"), (cos, "tab"), (sin_signed, "tab"), (q_gain, "full"), (k_gain, "full")],
                    [("row", proj.shape[1], BF16)])


def _attn_prep_bwd(geo, proj, cos, sin_signed, q_gain, k_gain, dq, dkv, name):
    kw = N_KV_HEADS * HEAD_DIM

    def body(i, p, cs, sn, qg, kg, dqv, dkvv):
        outs = []
        dgains = [jnp.zeros((1, LANES), F32), jnp.zeros((1, LANES), F32)]
        for ch in range(N_QK_CHUNKS):
            is_q = ch < N_Q_CHUNKS
            scale = Q_SCALE if is_q else 1.0
            ct = dqv[:, ch * LANES:(ch + 1) * LANES] if is_q else dkvv[:, (ch - N_Q_CHUNKS) * LANES:(ch - N_Q_CHUNKS + 1) * LANES]
            _, vjp = jax.vjp(lambda xx, gg, scale=scale: _qk_chunk(xx, gg, cs, sn, scale),
                             p[:, ch * LANES:(ch + 1) * LANES], qg if is_q else kg)
            dx, dg = vjp(ct)
            outs.append(dx)
            dgains[0 if is_q else 1] = dgains[0 if is_q else 1] + dg
        outs.append(dkvv[:, kw:])
        return jnp.concatenate(outs, axis=1), dgains[0], dgains[1]

    return _rowwise(name, body, geo, 256,
                    [(proj, "row"), (cos, "tab"), (sin_signed, "tab"), (q_gain, "full"), (k_gain, "full"), (dq, "row"), (dkv, "row")],
                    [("row", proj.shape[1], BF16), ("gacc", 1, LANES), ("gacc", 1, LANES)])


def _attn_geometry(geo):
    assert geo.s % ATTN_BLOCK == 0 and geo.l % ATTN_BLOCK == 0 and geo.seg >= BAND
    return geo.seg // ATTN_BLOCK, geo.s // ATTN_BLOCK


def _attn_mask(j, s0, geo):
    r = lax.broadcasted_iota(jnp.int32, (ATTN_BLOCK, geo.l + BAND), 0)
    n = lax.broadcasted_iota(jnp.int32, (ATTN_BLOCK, geo.l + BAND), 1) - geo.l
    dist = (s0 - j * ATTN_BLOCK) + n - r
    return (n < 0) | ((jnp.abs(dist) <= WINDOW) & (s0 + n < geo.s))


def _attn_probs(q, keys, valid, n_ctx, sink):
    s = _dot(q, keys, "nt")
    if valid is not None:
        s = jnp.where(valid, s, NEG_INF)
    m = jnp.maximum(jnp.max(s, axis=-1, keepdims=True), sink)
    e, e_sink = jnp.exp(s - m), jnp.exp(sink - m)
    inv = 1.0 / (jnp.sum(e, axis=-1, keepdims=True) + e_sink)
    return e * inv, e_sink * inv


def _attn_keys(ref, s0, geo, with_band):
    ctx = ref[geo.s:geo.seg, :]
    return jnp.concatenate([ctx, ref[pl.ds(s0, BAND), :]], axis=0) if with_band else ctx


def _attention(geo, qkv, sink, name, job=None):
    n_blocks, n_x_blocks = _attn_geometry(geo)
    qw, kw = N_HEADS * HEAD_DIM, N_KV_HEADS * HEAD_DIM
    group = N_HEADS // N_KV_HEADS
    carrier = _Carrier(job, 4, 1, 0)

    def kern(*refs):
        (sink_ref, q_ref, k_ref, v_ref, o_ref), job_refs = carrier.split(refs)
        j = pl.program_id(1)
        carrier.run(job_refs, pl.program_id(0) * n_blocks + j, geo.b * n_blocks)
        s0 = pl.multiple_of(jnp.clip((j - 1) * ATTN_BLOCK, 0, geo.seg - BAND), ATTN_BLOCK)

        def heads(with_band):
            valid = _attn_mask(j, s0, geo) if with_band else None
            k_all, v_all = _attn_keys(k_ref, s0, geo, with_band), _attn_keys(v_ref, s0, geo, with_band)
            for h in range(N_HEADS):
                kv = slice((h // group) * HEAD_DIM, (h // group + 1) * HEAD_DIM)
                p, _ = _attn_probs(q_ref[:, h * HEAD_DIM:(h + 1) * HEAD_DIM], k_all[:, kv], valid, geo.l, sink_ref[h])
                o_ref[:, h * HEAD_DIM:(h + 1) * HEAD_DIM] = _dot(p, v_all[:, kv], "nn").astype(BF16)

        pl.when(j < n_x_blocks)(lambda: heads(True))
        pl.when(j >= n_x_blocks)(lambda: heads(False))

    res = pl.pallas_call(
        kern, name=name, grid=(geo.b, n_blocks),
        in_specs=[pl.BlockSpec(memory_space=pltpu.SMEM),
                  pl.BlockSpec((ATTN_BLOCK, qw), lambda b, j: (b * n_blocks + j, 0)),
                  pl.BlockSpec((geo.seg, kw), lambda b, j: (b, qw // kw)),
                  pl.BlockSpec((geo.seg, kw), lambda b, j: (b, qw // kw + 1))] + carrier.in_specs(),
        out_specs=[pl.BlockSpec((ATTN_BLOCK, qw), lambda b, j: (b * n_blocks + j, 0))] + carrier.out_specs(),
        out_shape=[jax.ShapeDtypeStruct((geo.r, qw), BF16)] + carrier.out_shapes(),
        scratch_shapes=carrier.scratch(), input_output_aliases=carrier.aliases(),
        compiler_params=_cparams("arbitrary", "arbitrary"),
    )(sink, qkv, qkv, qkv, *carrier.operands())
    (o,), extra = carrier.results(res)
    return o, extra


def _attention_bwd(geo, qkv, sink, do, name, job=None):
    n_blocks, n_x_blocks = _attn_geometry(geo)
    qw, kw = N_HEADS * HEAD_DIM, N_KV_HEADS * HEAD_DIM
    group = N_HEADS // N_KV_HEADS

    carrier = _Carrier(job, 5, 3, 1)

    def kern(*refs):
        (sink_ref, q_ref, k_ref, v_ref, do_ref, dq_ref, dkv_out_ref, dsink_ref, dkv_ref), job_refs = carrier.split(refs)
        b, j = pl.program_id(0), pl.program_id(1)
        carrier.run(job_refs, b * n_blocks + j, geo.b * n_blocks)
        s0 = pl.multiple_of(jnp.clip((j - 1) * ATTN_BLOCK, 0, geo.seg - BAND), ATTN_BLOCK)

        @pl.when(j == 0)
        def _():
            dkv_ref[...] = jnp.zeros_like(dkv_ref)

        @pl.when((j == 0) & (b == 0))
        def _():
            dsink_ref[...] = jnp.zeros_like(dsink_ref)

        def heads(with_band):
            valid = _attn_mask(j, s0, geo) if with_band else None
            k_all, v_all = _attn_keys(k_ref, s0, geo, with_band), _attn_keys(v_ref, s0, geo, with_band)
            for g in range(N_KV_HEADS):
                kv = slice(g * HEAD_DIM, (g + 1) * HEAD_DIM)
                keys, vals = k_all[:, kv], v_all[:, kv]
                group_heads = [slice(h * HEAD_DIM, (h + 1) * HEAD_DIM) for h in range(g * group, (g + 1) * group)]
                ds_rows, p_rows = [], []
                for h, hs in zip(range(g * group, (g + 1) * group), group_heads):
                    dout = do_ref[:, hs]
                    p, p_sink = _attn_probs(q_ref[:, hs], keys, valid, geo.l, sink_ref[h])
                    dp = _dot(dout, vals, "nt")
                    dsum = jnp.sum(p * dp, axis=-1, keepdims=True)
                    ds = (p * (dp - dsum)).astype(BF16)
                    dq_ref[:, hs] = _dot(ds, keys, "nn").astype(BF16)
                    ds_rows.append(ds)
                    p_rows.append(p.astype(BF16))
                    dsink_ref[h:h + 1, :] += jnp.broadcast_to(-jnp.sum(p_sink * dsum, axis=0, keepdims=True), (1, LANES))
                q_rows = jnp.concatenate([q_ref[:, hs] for hs in group_heads], axis=0)
                do_rows = jnp.concatenate([do_ref[:, hs] for hs in group_heads], axis=0)
                dk = _dot(jnp.concatenate(ds_rows, axis=0), q_rows, "tn")
                dv = _dot(jnp.concatenate(p_rows, axis=0), do_rows, "tn")
                vv = slice(kw + g * HEAD_DIM, kw + (g + 1) * HEAD_DIM)
                dkv_ref[geo.s:geo.seg, kv] += dk[:geo.l]
                dkv_ref[geo.s:geo.seg, vv] += dv[:geo.l]
                if with_band:
                    dkv_ref[pl.ds(s0, BAND), kv] += dk[geo.l:]
                    dkv_ref[pl.ds(s0, BAND), vv] += dv[geo.l:]

        pl.when(j < n_x_blocks)(lambda: heads(True))
        pl.when(j >= n_x_blocks)(lambda: heads(False))

        @pl.when(j == n_blocks - 1)
        def _():
            dkv_out_ref[...] = dkv_ref[...].astype(BF16)

    res = pl.pallas_call(
        kern, name=name, grid=(geo.b, n_blocks),
        in_specs=[pl.BlockSpec(memory_space=pltpu.SMEM),
                  pl.BlockSpec((ATTN_BLOCK, qw), lambda b, j: (b * n_blocks + j, 0)),
                  pl.BlockSpec((geo.seg, kw), lambda b, j: (b, qw // kw)),
                  pl.BlockSpec((geo.seg, kw), lambda b, j: (b, qw // kw + 1)),
                  pl.BlockSpec((ATTN_BLOCK, qw), lambda b, j: (b * n_blocks + j, 0))] + carrier.in_specs(),
        out_specs=[pl.BlockSpec((ATTN_BLOCK, qw), lambda b, j: (b * n_blocks + j, 0)),
                   pl.BlockSpec((geo.seg, 2 * kw), lambda b, j: (b, 0)),
                   pl.BlockSpec((N_HEADS, LANES), lambda b, j: (0, 0))] + carrier.out_specs(),
        out_shape=[jax.ShapeDtypeStruct((geo.r, qw), BF16), jax.ShapeDtypeStruct((geo.r, 2 * kw), BF16),
                   jax.ShapeDtypeStruct((N_HEADS, LANES), F32)] + carrier.out_shapes(),
        scratch_shapes=[pltpu.VMEM((geo.seg, 2 * kw), F32)] + carrier.scratch(), input_output_aliases=carrier.aliases(),
        compiler_params=_cparams("arbitrary", "arbitrary"),
    )(sink, qkv, qkv, qkv, do, *carrier.operands())
    (dq, dkv, dsink), extra = carrier.results(res)
    return dq, dkv, dsink, extra


RET_QK_W = RET_HEADS * RET_QK_DIM
K_SCALE = RET_QK_DIM ** -0.5


def _ret_prep(geo, proj, cos, sin_signed, name):
    def body(i, p, cs, sn):
        cs2, sn2 = jnp.concatenate([cs] * RET_HEADS, axis=1), jnp.concatenate([sn] * RET_HEADS, axis=1)
        q = _rope(p[:, :RET_QK_W], cs2, sn2, RET_QK_DIM // 4)
        k = _rope(p[:, RET_QK_W:2 * RET_QK_W], cs2, sn2, RET_QK_DIM // 4) * K_SCALE
        return jnp.concatenate([q, k, p[:, 2 * RET_QK_W:]], axis=1)

    return _rowwise(name, body, geo, 256, [(proj, ("rowc", 2 * RET_QK_W + RET_VWIDTH, 0)), (cos, "tab"), (sin_signed, "tab")],
                    [("row", 2 * RET_QK_W + RET_VWIDTH, BF16)])


def _ret_prep_bwd(geo, dq, dk, dv, dgate, cos, sin_signed, name):
    def body(i, dqv, dkv, dvv, dg, cs, sn):
        cs2, sn2 = jnp.concatenate([cs] * RET_HEADS, axis=1), jnp.concatenate([sn] * RET_HEADS, axis=1)
        dkv = dkv * K_SCALE
        dqv = dqv * cs2 + _swap_halves(dqv * sn2, RET_QK_DIM // 4)
        dkv = dkv * cs2 + _swap_halves(dkv * sn2, RET_QK_DIM // 4)
        return jnp.concatenate([dqv, dkv, dvv, dg], axis=1)

    return _rowwise(name, body, geo, 256,
                    [(dq, "row"), (dk, "row"), (dv, "row"), (dgate, "row"), (cos, "tab"), (sin_signed, "tab")],
                    [("row", 2 * RET_QK_W + 2 * RET_VWIDTH, BF16)])


def _ret_step(state, q, k, v, lg, rev):
    c = RET_CHUNK
    ri = lax.broadcasted_iota(jnp.int32, (c, 1), 0).astype(F32)
    cj = lax.broadcasted_iota(jnp.int32, (1, c), 1).astype(F32)
    if rev:
        dist, q_decay, k_decay = cj - ri, jnp.exp(lg * (c - ri)), jnp.exp(lg * ri)
    else:
        dist, q_decay, k_decay = ri - cj, jnp.exp(lg * (ri + 1.0)), jnp.exp(lg * (c - 1.0 - ri))
    intra = jnp.where(dist >= 0, jnp.exp(lg * jnp.maximum(dist, 0.0)), 0.0)
    scores = _mm(q, k, "nt") * intra
    out = _mm(scores, v, "nn") + _mm(q, state, "nn") * q_decay
    new_state = state * jnp.exp(lg * c) + _mm(k * k_decay, v, "tn")
    return new_state, out


def _ret_state0(kc, vc, lg, rev):
    n = kc.shape[0]
    t = lax.broadcasted_iota(jnp.int32, (n, 1), 0).astype(F32)
    decay = jnp.exp(lg * t) if rev else jnp.exp(lg * (n - 1.0 - t))
    return _mm(kc * decay, vc, "tn")


def _ret_specs(geo):
    nq = RET_HEADS
    return [pl.BlockSpec((2 * RET_HEADS, LANES), lambda b, h: (0, 0)),
            pl.BlockSpec((geo.seg, RET_QK_DIM), lambda b, h: (b, h)),
            pl.BlockSpec((geo.seg, RET_QK_DIM), lambda b, h: (b, nq + h)),
            pl.BlockSpec((geo.seg, RET_V_DIM), lambda b, h: (b, nq + h))]


def _retention(geo, qkv, log_g, name):
    nc = geo.s // RET_CHUNK

    def kern(lg_ref, q_ref, k_ref, v_ref, o_ref, st_ref):
        h = pl.program_id(1)
        for d, rev in ((0, False), (1, True)):
            lg = lg_ref[pl.ds(d * RET_HEADS + h, 1), 0:1]
            st_ref[...] = _ret_state0(k_ref[geo.s:geo.seg, :].astype(F32), v_ref[geo.s:geo.seg, :].astype(F32), lg, rev)

            def chunk(ci, carry, d=d, rev=rev, lg=lg):
                r0 = pl.multiple_of((nc - 1 - ci if rev else ci) * RET_CHUNK, RET_CHUNK)
                rows = pl.ds(r0, RET_CHUNK)
                new_state, out = _ret_step(st_ref[...], q_ref[rows, :], k_ref[rows, :], v_ref[rows, :], lg, rev)
                st_ref[...] = new_state
                if d == 0:
                    o_ref[rows, :] = out
                else:
                    o_ref[rows, :] += out
                return carry

            lax.fori_loop(0, nc, chunk, 0)
        o_ref[geo.s:geo.seg, :] = jnp.zeros((geo.l, RET_V_DIM), F32)

    return pl.pallas_call(
        kern, name=name, grid=(geo.b, RET_HEADS), in_specs=_ret_specs(geo),
        out_specs=pl.BlockSpec((geo.seg, RET_V_DIM), lambda b, h: (b, h)),
        out_shape=jax.ShapeDtypeStruct((geo.r, RET_VWIDTH), F32),
        scratch_shapes=[pltpu.VMEM((RET_QK_DIM, RET_V_DIM), F32)],
        compiler_params=_cparams("parallel", "arbitrary"),
    )(log_g, qkv, qkv, qkv)


def _retention_bwd(geo, qkv, log_g, do, name):
    nc = geo.s // RET_CHUNK
    ctx = slice(geo.s, geo.seg)

    def kern(lg_ref, q_ref, k_ref, v_ref, do_ref, dq_ref, dk_ref, dv_ref, dlg_ref, states_ref, dst_ref, aq_ref, ak_ref, av_ref):
        b, h = pl.program_id(0), pl.program_id(1)

        @pl.when((b == 0) & (h == 0))
        def _():
            dlg_ref[...] = jnp.zeros_like(dlg_ref)

        for d, rev in ((0, False), (1, True)):
            row = pl.ds(d * RET_HEADS + h, 1)
            lg = lg_ref[row, 0:1]
            kc, vc = k_ref[ctx, :].astype(F32), v_ref[ctx, :].astype(F32)
            states_ref[0] = _ret_state0(kc, vc, lg, rev)

            def rows_of(ci, rev=rev):
                return pl.ds(pl.multiple_of((nc - 1 - ci if rev else ci) * RET_CHUNK, RET_CHUNK), RET_CHUNK)

            def load(rows):
                return q_ref[rows, :].astype(F32), k_ref[rows, :].astype(F32), v_ref[rows, :].astype(F32)

            def replay(ci, carry, rev=rev, lg=lg, rows_of=rows_of, load=load):
                states_ref[ci + 1] = _ret_step(states_ref[ci], *load(rows_of(ci)), lg, rev)[0]
                return carry

            lax.fori_loop(0, nc - 1, replay, 0)
            dst_ref[...] = jnp.zeros_like(dst_ref)

            def emit(rows, dq, dk, dv, d=d):
                if d == 0:
                    ak_ref[rows, :], av_ref[rows, :] = dk, dv
                    if dq is not None:
                        aq_ref[rows, :] = dq
                else:
                    dk_ref[rows, :] = (ak_ref[rows, :] + dk).astype(BF16)
                    dv_ref[rows, :] = (av_ref[rows, :] + dv).astype(BF16)
                    if dq is not None:
                        dq_ref[rows, :] = (aq_ref[rows, :] + dq).astype(BF16)

            def back(t, dlg, rev=rev, lg=lg, rows_of=rows_of, load=load, emit=emit):
                ci = nc - 1 - t
                rows = rows_of(ci)
                _, vjp = jax.vjp(lambda st, q, k, v, g: _ret_step(st, q, k, v, g, rev), states_ref[ci], *load(rows), lg)
                dstate, dq, dk, dv, dg = vjp((dst_ref[...], do_ref[rows, :].astype(F32)))
                dst_ref[...] = dstate
                emit(rows, dq, dk, dv)
                return dlg + dg

            dlg = lax.fori_loop(0, nc, back, jnp.zeros((1, 1), F32))
            _, vjp = jax.vjp(lambda kk, vv, g: _ret_state0(kk, vv, g, rev), kc, vc, lg)
            dkc, dvc, dg = vjp(dst_ref[...])
            emit(ctx, None, dkc, dvc)
            dlg_ref[row, :] += jnp.broadcast_to(dlg + dg, (1, LANES))
        dq_ref[ctx, :] = jnp.zeros((geo.l, RET_QK_DIM), BF16)

    nq = RET_HEADS
    return pl.pallas_call(
        kern, name=name, grid=(geo.b, RET_HEADS),
        in_specs=_ret_specs(geo) + [pl.BlockSpec((geo.seg, RET_V_DIM), lambda b, h: (b, h))],
        out_specs=[pl.BlockSpec((geo.seg, RET_QK_DIM), lambda b, h: (b, h)),
                   pl.BlockSpec((geo.seg, RET_QK_DIM), lambda b, h: (b, h)),
                   pl.BlockSpec((geo.seg, RET_V_DIM), lambda b, h: (b, h)),
                   pl.BlockSpec((2 * RET_HEADS, LANES), lambda b, h: (0, 0))],
        out_shape=[jax.ShapeDtypeStruct((geo.r, RET_QK_W), BF16), jax.ShapeDtypeStruct((geo.r, RET_QK_W), BF16),
                   jax.ShapeDtypeStruct((geo.r, RET_VWIDTH), BF16), jax.ShapeDtypeStruct((2 * RET_HEADS, LANES), F32)],
        scratch_shapes=[pltpu.VMEM((nc, RET_QK_DIM, RET_V_DIM), F32), pltpu.VMEM((RET_QK_DIM, RET_V_DIM), F32),
                        pltpu.VMEM((geo.seg, RET_QK_DIM), F32), pltpu.VMEM((geo.seg, RET_QK_DIM), F32),
                        pltpu.VMEM((geo.seg, RET_V_DIM), F32)],
        compiler_params=_cparams("arbitrary", "arbitrary"),
    )(log_g, qkv, qkv, qkv, do)


def _gated(o, g, gain):
    outs = []
    for h in range(RET_HEADS):
        cols = slice(h * RET_V_DIM, (h + 1) * RET_V_DIM)
        oh = o[:, cols]
        mu = jnp.mean(oh, axis=-1, keepdims=True)
        var = jnp.mean(jnp.square(oh - mu), axis=-1, keepdims=True)
        outs.append(_silu(g[:, cols]) * ((oh - mu) * lax.rsqrt(var + EPS) * gain[:, cols]))
    return jnp.concatenate(outs, axis=1)


def _ret_gated(geo, o, proj, gain, name):
    def body(i, ov, gv, gn):
        return _gated(ov, gv, gn)

    gate_block = (2 * RET_QK_W + RET_VWIDTH) // RET_VWIDTH
    return _rowwise(name, body, geo, 256, [(o, "row"), (proj, ("rowc", RET_VWIDTH, gate_block)), (gain, "full")],
                    [("row", RET_VWIDTH, BF16)])


def _ret_gated_bwd(geo, o, proj, gain, dout, name):
    def body(i, ov, gv, gn, dv):
        _, vjp = jax.vjp(_gated, ov, gv, gn)
        return vjp(dv)

    gate_block = (2 * RET_QK_W + RET_VWIDTH) // RET_VWIDTH
    return _rowwise(name, body, geo, 256,
                    [(o, "row"), (proj, ("rowc", RET_VWIDTH, gate_block)), (gain, "full"), (dout, "row")],
                    [("row", RET_VWIDTH, BF16), ("row", RET_VWIDTH, BF16), ("gacc", 1, RET_VWIDTH)])


def _whole(name, fn, out_shapes, *arrays):
    n = len(arrays)

    def kern(*refs):
        res = fn(*[r[...] for r in refs[:n]])
        for ref, val in zip(refs[n:], res):
            ref[...] = val.astype(ref.dtype)

    return pl.pallas_call(kern, name=name, out_shape=out_shapes)(*arrays)


def _rope_tables(geo, head_dim):
    rows = geo.s // GRID_W
    row = jnp.broadcast_to(jnp.arange(rows, dtype=jnp.int32)[:, None], (rows, GRID_W)).reshape(geo.s)
    col = jnp.broadcast_to(jnp.arange(GRID_W, dtype=jnp.int32)[None, :], (rows, GRID_W)).reshape(geo.s)
    axis_dim = head_dim // 2
    inv = ROPE_BASE ** (-jnp.arange(0, axis_dim, 2, dtype=F32) / axis_dim)
    ang_r = row.astype(F32)[:, None] * inv
    ang_c = col.astype(F32)[:, None] * inv
    cos = jnp.concatenate([jnp.cos(ang_r)] * 2 + [jnp.cos(ang_c)] * 2, axis=1)
    sin = jnp.concatenate([-jnp.sin(ang_r), jnp.sin(ang_r), -jnp.sin(ang_c), jnp.sin(ang_c)], axis=1)
    cos = jnp.concatenate([cos, jnp.ones((geo.l, head_dim), F32)], axis=0)
    sin = jnp.concatenate([sin, jnp.zeros((geo.l, head_dim), F32)], axis=0)
    reps = max(1, LANES // head_dim)
    return jnp.tile(cos, (1, reps)), jnp.tile(sin, (1, reps))


def _row_tile(r):
    return next(t for t in (1536, 1024, 512, 256, 128) if r % t == 0)


MOD_ROWS = 8


def _local_step(x, ctx, target, sp, wts, mods, plan=None):
    nb, s, d = x.shape
    geo = _Rows(nb, s, ctx.shape[1])
    assert nb + 1 <= MOD_ROWS and d == D_MODEL
    tm = _row_tile(geo.r)
    z = jnp.concatenate([x, ctx], axis=1).reshape(geo.r, d)
    cos64, sin64 = _rope_tables(geo, HEAD_DIM)
    cos256, sin256 = _rope_tables(geo, RET_QK_DIM)
    q_gain = jnp.tile(sp["q_norm"].reshape(1, HEAD_DIM), (1, LANES // HEAD_DIM))
    k_gain = jnp.tile(sp["k_norm"].reshape(1, HEAD_DIM), (1, LANES // HEAD_DIM))
    sink = sp["sink"].reshape(N_HEADS)
    log_g = jnp.broadcast_to(sp["log_g"].reshape(2 * RET_HEADS, 1), (2 * RET_HEADS, LANES))
    gn_g = sp["gn_g"].reshape(1, RET_VWIDTH)

    saved = []
    h1 = _norm_mod(geo, z, sp["norm1_g"][0][None, :], mods[0], 0, "norm1_0")
    for i in range(2):
        mod3 = mods[i]
        n1, n2 = sp["norm1_g"][i][None, :], sp["norm2_g"][i][None, :]
        if i == 0:
            proj = _mm_nn(h1, wts["attn_qkv"], F32, "attn_qkv", tm, wts["attn_qkv"].shape[1], d)
            prep = _attn_prep(geo, proj, cos64, sin64, q_gain, k_gain, "attn_prep")
            o, late = _attention(geo, prep, sink, "attn", plan.gather_job() if plan else None)
            if plan:
                plan.late_weights(late, wts)
            oraw = None
            w_o = wts["attn_o"]
        else:
            proj = _mm_nn(h1, wts["ret_qkvg"], BF16, "ret_qkvg", tm, wts["ret_qkvg"].shape[2], d)
            prep = _ret_prep(geo, proj, cos256, sin256, "ret_prep")
            oraw = _retention(geo, prep, log_g, "ret")
            o = _ret_gated(geo, oraw, proj, gn_g, "ret_gated")
            w_o = wts["ret_o"]
        zmid, mix, h2 = _mm_nn_gate_residual(geo, o, w_o, z, mod3, 2 * d, f"mix_out{i}", norm=(n2, mod3, 3 * d))
        u, a = _ffn_in_swiglu(h2, wts["ffn_in"][i], f"ffn_in{i}")
        next_norm = (sp["norm1_g"][1][None, :], mods[1], 0) if i == 0 else None
        zout, f, h1_next = _mm_nn_gate_residual(geo, a, wts["ffn_out"][i], zmid, mod3, 5 * d, f"ffn_out{i}", norm=next_norm)
        saved.append(dict(z=z, mod3=mod3, n1=n1, n2=n2, h1=h1, proj=proj, prep=prep, o=o, oraw=oraw, mix=mix, zmid=zmid,
                          h2=h2, u=u, a=a, f=f))
        z, h1 = zout, h1_next

    dz, loss, df, dg2 = _loss_head(geo, z, target.reshape(nb * s, d), saved[1]["f"], saved[1]["mod3"], 5 * d, "loss")

    big, small = {}, {}
    dmods = [None, None]
    for i in (1, 0):
        sv = saved[i]
        mod3 = sv["mod3"]
        carry = plan is not None and i == 0
        du, land = _ffn_out_dx_swiglu_bwd(df, wts["ffn_out"][i], sv["u"], f"ffn_out_dx{i}", plan.layer1.swap_job() if carry else None)
        if carry:
            plan.layer1.after_swap(land)
        big[f"ffn_out{i}"] = _mm_tn(sv["a"], df, f"ffn_out_dw{i}", D_FF // 2, 1024, tm, out_dtype=BF16).reshape(N_CHIPS, D_FF // N_CHIPS, d)
        n4 = wts["ffn_in"][i].shape[2]
        dh2 = _mm_nt(du, wts["ffn_in"][i], BF16, f"ffn_in_dx{i}", tm, 1024, n4)
        big[f"ffn_in{i}"] = _mm_tn(sv["h2"], du, f"ffn_in_dw{i}", 1024, n4, tm, shards=N_CHIPS, out_dtype=BF16)
        if carry:
            plan.start_layer0_ffn(big)
        dzmid, dsh2, dsc2, dn2, dmix, dg1, *land = _norm_mod_bwd(geo, sv["zmid"], sv["n2"], mod3, 3 * d, dh2, dz, f"norm2_bwd{i}",
                                                                 gated=(sv["mix"], mod3, 2 * d),
                                                                 job=plan.layer0_ffn.swap_job() if carry else None)
        if carry:
            plan.layer0_ffn.after_swap(land[0])
        if i == 0:
            do = _mm_nt(dmix, wts["attn_o"], BF16, "attn_out_dx", tm, 1024, 1024)
            big["attn_o"] = _mm_tn(sv["o"], dmix, "attn_out_dw", 1024, 1024, tm, out_dtype=BF16).reshape(N_CHIPS, 1024 // N_CHIPS, d)
            dq, dkv, dsink, land = _attention_bwd(geo, sv["prep"], sink, do, "attn_bwd", plan.exchange_job() if plan else None)
            if plan:
                plan.after_exchange(land)
            dproj, dqg, dkg = _attn_prep_bwd(geo, sv["proj"], cos64, sin64, q_gain, k_gain, dq, dkv, "attn_prep_bwd")
            small["q_norm"] = dqg[0, :HEAD_DIM] + dqg[0, HEAD_DIM:]
            small["k_norm"] = dkg[0, :HEAD_DIM] + dkg[0, HEAD_DIM:]
            small["sink"] = dsink[:, 0]
            wq = wts["attn_qkv"]
            dh1 = _mm_nt(dproj, wq, BF16, "attn_qkv_dx", tm, 1024, wq.shape[1])
            dwq = _mm_tn(sv["h1"], dproj, "attn_qkv_dw", 1024, wq.shape[1], tm, out_dtype=BF16)
            big["attn_qkv"] = dwq.reshape(d, N_CHIPS, -1).transpose(1, 0, 2)
        else:
            do = _mm_nt(dmix, wts["ret_o"], BF16, "ret_out_dx", tm, 1024, 1024)
            big["ret_o"] = _mm_tn(sv["o"], dmix, "ret_out_dw", 1024, 1024, tm, out_dtype=BF16).reshape(N_CHIPS, RET_VWIDTH // N_CHIPS, d)
            doraw, dgate, dgn = _ret_gated_bwd(geo, sv["oraw"], sv["proj"], gn_g, do, "ret_gated_bwd")
            small["gn_g"] = dgn[0]
            dq, dk, dv, dlg = _retention_bwd(geo, sv["prep"], log_g, doraw, "ret_bwd")
            small["log_g"] = dlg[:, 0].reshape(2, RET_HEADS)
            dproj = _ret_prep_bwd(geo, dq, dk, dv, dgate, cos256, sin256, "ret_prep_bwd")
            wq = wts["ret_qkvg"]
            dh1 = _mm_nt(dproj, wq, BF16, "ret_qkvg_dx", tm, 1024, wq.shape[2])
            big["ret_qkvg"] = _mm_tn(sv["h1"], dproj, "ret_qkvg_dw", 1024, wq.shape[2], tm, shards=N_CHIPS, out_dtype=BF16)
        below = (saved[0]["f"], saved[0]["mod3"], 5 * d) if i == 1 else None
        dz, dsh1, dsc1, dn1, *below_grads = _norm_mod_bwd(geo, sv["z"], sv["n1"], mod3, 0, dh1, dzmid, f"norm1_bwd{i}", gated=below,
                                                              latent_only=i == 0)
        small[f"norm1_g{i}"], small[f"norm2_g{i}"] = dn1[0], dn2[0]
        parts = [dsh1, dsc1, dg1, dsh2, dsc2, dg2]
        rows = jnp.concatenate([jnp.concatenate([p[:nb, 0, :] for p in parts], axis=1),
                                jnp.concatenate([jnp.sum(p[nb:, 0, :], axis=0, keepdims=True) for p in parts], axis=1),
                                jnp.zeros((MOD_ROWS - nb - 1, 6 * d), F32)], axis=0)
        dmods[i] = rows
        if below_grads:
            df, dg2 = below_grads
        small[f"ada_b{i}"] = jnp.sum(rows, axis=0)
        if plan and i == 1:
            plan.start_layer1(big)
    return loss, dz, big, small, dmods


def _adamw(w, g, m, v, name):
    rows, cols = w.shape
    tr = next((t for t in (256, 128, 64, 32, 16, 8) if rows % t == 0), rows)
    c1 = 1.0 - ADAM_B1 ** ADAM_STEP
    c2 = 1.0 - ADAM_B2 ** ADAM_STEP

    def kern(w_ref, g_ref, m_ref, v_ref, d_ref, nm_ref, nv_ref):
        gv = g_ref[...]
        nm = ADAM_B1 * m_ref[...] + (1.0 - ADAM_B1) * gv
        nv = ADAM_B2 * v_ref[...] + (1.0 - ADAM_B2) * jnp.square(gv)
        d_ref[...] = -ADAM_LR * ((nm / c1) / (jnp.sqrt(nv / c2) + ADAM_EPS) + ADAM_WD * w_ref[...])
        nm_ref[...] = nm
        nv_ref[...] = nv

    spec = pl.BlockSpec((tr, cols), lambda i: (i, 0))
    return pl.pallas_call(
        kern, name=name, grid=(rows // tr,), in_specs=[spec] * 4, out_specs=[spec] * 3,
        out_shape=[jax.ShapeDtypeStruct(w.shape, F32)] * 3, compiler_params=_cparams("parallel"),
    )(w, g, m, v)


N_DEVICES = 8


def _mesh_pos():
    return lax.axis_index("x"), lax.axis_index("y"), lax.axis_index("c")


def _other_chips(x, y):
    return [(1 - x, y), (x, 1 - y), (1 - x, 1 - y)]


def _hbm(n):
    return [pl.BlockSpec(memory_space=pl.ANY)] * n


def _remote(src, dst, send_sem, recv_sem, device):
    return pltpu.make_async_remote_copy(src_ref=src, dst_ref=dst, send_sem=send_sem, recv_sem=recv_sem,
                                        device_id=device, device_id_type=MESH)


def _scalar_spec(grid, in_specs, out_specs):
    return pltpu.PrefetchScalarGridSpec(num_scalar_prefetch=1, grid=grid, in_specs=in_specs, out_specs=out_specs)


def _place_shard(param, layer, pos, name):
    _, r, cols = param.shape
    tr = _slab_tile(r)

    def kern(pos_ref, s_ref, o_ref):
        o_ref[...] = s_ref[...].astype(BF16)

    return pl.pallas_call(
        kern, name=name, out_shape=jax.ShapeDtypeStruct((N_CHIPS, r, cols), BF16),
        grid_spec=_scalar_spec((r // tr,), [pl.BlockSpec((None, tr, cols), lambda i, p: (layer, i, 0))],
                               pl.BlockSpec((None, tr, cols), lambda i, p: (p[1], i, 0))),
        compiler_params=_cparams("parallel"),
    )(pos, param)


class _CommJob:
    def __init__(self, inputs, out_shapes, aliases, sem_shapes, stages, fractions=None):
        self.inputs, self.out_shapes, self.aliases, self.sem_shapes, self.stages = inputs, out_shapes, aliases, sem_shapes, stages
        self.fractions = fractions


def _merge_jobs(a, b):
    assert len(a.stages) == len(b.stages)
    ni, no, ns = len(a.inputs), len(a.out_shapes), len(a.sem_shapes)

    def both(sa, sb):
        def stage(ins, outs, sems):
            sa(ins[:ni], outs[:no], sems[:ns])
            sb(ins[ni:], outs[no:], sems[ns:])
        return stage

    aliases = dict(a.aliases)
    aliases.update({ni + i: no + o for i, o in b.aliases.items()})
    return _CommJob(a.inputs + b.inputs, a.out_shapes + b.out_shapes, aliases, a.sem_shapes + b.sem_shapes,
                    [both(sa, sb) for sa, sb in zip(a.stages, b.stages)])


def _run_job(job, name):
    n_in, n_out = len(job.inputs), len(job.out_shapes)

    def body(*refs):
        for stage in job.stages:
            stage(refs[:n_in], refs[n_in:n_in + n_out], refs[n_in + n_out:])

    return pl.pallas_call(
        body, name=name, in_specs=_hbm(n_in), out_specs=_hbm(n_out), out_shape=job.out_shapes,
        input_output_aliases=job.aliases, scratch_shapes=job.sem_shapes,
    )(*job.inputs)


def _job_marks(job, steps):
    mid = len(job.stages) - 2
    fractions = job.fractions or [(s + 1) / (mid + 1) for s in range(mid)]
    return [0] + [min(steps - 1, 1 + int((steps - 1) * f)) for f in fractions] + [steps - 1]


def _gather_job(placed):
    n = len(placed)

    def half(w, which):
        r2 = placed[w].shape[1] // 2
        return pl.ds(which * r2, r2)

    def ici_copies(outs, sems, slot_of, arrays=range(n)):
        x, y, c = _mesh_pos()
        res = []
        for w in arrays:
            for k, (px, py) in enumerate(_other_chips(x, y)):
                slab = outs[w].at[slot_of(x, y, px, py), half(w, c)]
                res.append((slab, _remote(slab, slab, sems[0].at[w, k], sems[1].at[w, k], (px, py, c))))
        return res

    def forwards(outs, sems, which_core, arrays=range(n)):
        x, y, c = _mesh_pos()
        res = []
        for w in arrays:
            for k, (px, py) in enumerate(_other_chips(x, y)):
                slab = outs[w].at[2 * px + py, half(w, which_core(c))]
                res.append(_remote(slab, slab, sems[2].at[w, k], sems[3].at[w, k], (x, y, 1 - c)))
        return res

    def send(ins, outs, sems):
        for _, cp in ici_copies(outs, sems, lambda x, y, px, py: 2 * x + y):
            cp.start()

    def forward_of(w):
        def forward(ins, outs, sems):
            arrivals = ici_copies(outs, sems, lambda x, y, px, py: 2 * px + py, [w])
            for (_, arrival), fwd in zip(arrivals, forwards(outs, sems, lambda c: c, [w])):
                arrival.wait_recv()
                fwd.start()
        return forward

    def finish(ins, outs, sems):
        for cp in forwards(outs, sems, lambda c: 1 - c):
            cp.wait_recv()
        for _, cp in ici_copies(outs, sems, lambda x, y, px, py: 2 * x + y):
            cp.wait_send()
        for cp in forwards(outs, sems, lambda c: c):
            cp.wait_send()

    sizes = [p.shape[1] * p.shape[2] for p in placed]
    fractions = [sum(sizes[:w + 1]) / sum(sizes) for w in range(n)]
    return _CommJob(list(placed), [jax.ShapeDtypeStruct(p.shape, p.dtype) for p in placed], {w: w for w in range(n)},
                    [pltpu.SemaphoreType.DMA((n, 3))] * 4, [send] + [forward_of(w) for w in range(n)] + [finish], fractions)


def _pair_swap_job(grads):
    n = len(grads)

    def copies(ins, outs, sems):
        x, y, c = _mesh_pos()
        res = []
        for w in range(n):
            r2 = grads[w].shape[1] // 2
            res.append(_remote(ins[w].at[:, pl.ds((1 - c) * r2, r2)], outs[w], sems[0].at[w], sems[1].at[w], (x, y, 1 - c)))
        return res

    def send(ins, outs, sems):
        for cp in copies(ins, outs, sems):
            cp.start()

    def finish(ins, outs, sems):
        for cp in copies(ins, outs, sems):
            cp.wait()

    return _CommJob(list(grads), [jax.ShapeDtypeStruct((N_CHIPS, g.shape[1] // 2, g.shape[2]), g.dtype) for g in grads], {},
                    [pltpu.SemaphoreType.DMA((n,))] * 2, [send, finish])


def _chip_exchange_job(hs):
    n = len(hs)

    def send(ins, outs, sems):
        x, y, c = _mesh_pos()
        for w in range(n):
            for k, (px, py) in enumerate(_other_chips(x, y)):
                _remote(ins[w].at[2 * px + py], outs[w].at[2 * x + y], sems[0].at[w, k], sems[1].at[w, k], (px, py, c)).start()

    def finish(ins, outs, sems):
        x, y, c = _mesh_pos()
        for w in range(n):
            for k, (px, py) in enumerate(_other_chips(x, y)):
                got = outs[w].at[2 * px + py]
                cp = _remote(ins[w].at[2 * px + py], got, sems[0].at[w, k], sems[1].at[w, k], (px, py, c))
                cp.wait_recv()
                cp.wait_send()

    return _CommJob(list(hs), [jax.ShapeDtypeStruct(h.shape, h.dtype) for h in hs], {},
                    [pltpu.SemaphoreType.DMA((n, 3))] * 2, [send, finish])


def _pair_share(ts, name):
    n = len(ts)

    def body(*refs):
        outs = refs[n:2 * n]
        send_sems, recv_sems = refs[2 * n:]
        x, y, c = _mesh_pos()
        sends = []
        for w in range(n):
            r2 = ts[w].shape[0] // 2
            mine = outs[w].at[pl.ds(c * r2, r2)]
            rc = _remote(mine, mine, send_sems.at[w], recv_sems.at[w], (x, y, 1 - c))
            rc.start()
            sends.append(rc)
        for w in range(n):
            r2 = ts[w].shape[0] // 2
            theirs = outs[w].at[pl.ds((1 - c) * r2, r2)]
            _remote(theirs, theirs, send_sems.at[w], recv_sems.at[w], (x, y, 1 - c)).wait_recv()
            sends[w].wait_send()

    return pl.pallas_call(
        body, name=name, in_specs=_hbm(n), out_specs=_hbm(n),
        out_shape=[jax.ShapeDtypeStruct(t.shape, F32) for t in ts],
        input_output_aliases={w: w for w in range(n)},
        scratch_shapes=[pltpu.SemaphoreType.DMA((n,))] * 2,
    )(*ts)


def _slab_tile(rows):
    return next(t for t in (512, 256, 176, 128, 64, 32, 16) if rows % t == 0)


def _sum_pair(grad, land, pos, name):
    _, r2, cols = land.shape
    tr = _slab_tile(r2)
    nt = r2 // tr

    def kern(pos_ref, a_ref, b_ref, o_ref):
        o_ref[...] = (a_ref[...].astype(F32) + b_ref[...].astype(F32)).astype(BF16)

    spec = pl.BlockSpec((None, tr, cols), lambda j, i, p: (j, i, 0))
    return pl.pallas_call(
        kern, name=name, out_shape=jax.ShapeDtypeStruct(land.shape, BF16),
        grid_spec=_scalar_spec((N_CHIPS, nt), [pl.BlockSpec((None, tr, cols), lambda j, i, p: (j, p[0] * nt + i, 0)), spec], spec),
        compiler_params=_cparams("parallel", "parallel"),
    )(pos, grad, land)


def _sum_chips(hs, land, pos, name):
    _, r2, cols = land.shape
    tr = _slab_tile(r2)
    nt = r2 // tr

    def kern(pos_ref, h_ref, l_ref, o_ref):
        acc = jnp.zeros((tr, cols), F32)
        own = h_ref[...].astype(F32)
        for k in range(N_CHIPS):
            acc = acc + jnp.where(pos_ref[1] == k, own, l_ref[k].astype(F32))
        o_ref[...] = acc

    return pl.pallas_call(
        kern, name=name, out_shape=jax.ShapeDtypeStruct((2 * r2, cols), F32),
        grid_spec=_scalar_spec((nt,), [pl.BlockSpec((None, tr, cols), lambda i, p: (p[1], i, 0)),
                                       pl.BlockSpec((N_CHIPS, tr, cols), lambda i, p: (0, i, 0))],
                               pl.BlockSpec((tr, cols), lambda i, p: (p[0] * nt + i, 0))),
        compiler_params=_cparams("parallel"),
    )(pos, hs, land)


class _ReduceScatter:
    def __init__(self, grads, pos, tag):
        self.grads, self.pos, self.tag = list(grads), pos, tag

    def swap_job(self):
        return _pair_swap_job(self.grads)

    def after_swap(self, land):
        self.hs = [_sum_pair(g, l, self.pos, f"grads_pair_sum_{self.tag}{w}") for w, (g, l) in enumerate(zip(self.grads, land))]

    def exchange_job(self):
        return _chip_exchange_job(self.hs)

    def after_exchange(self, land2):
        return [_sum_chips(h, l, self.pos, f"grads_chip_sum_{self.tag}{w}") for w, (h, l) in enumerate(zip(self.hs, land2))]

    def run(self):
        self.after_swap(_run_job(self.swap_job(), f"grads_pair_swap_{self.tag}"))
        return self.after_exchange(_run_job(self.exchange_job(), f"grads_chip_exchange_{self.tag}"))


EARLY_WEIGHTS = ("attn_qkv",)
LATE_WEIGHTS = ("ffn_in0", "ffn_in1", "ffn_out0", "ffn_out1", "attn_o", "ret_qkvg", "ret_o")
LAYER1_GRADS = ("ffn_out1", "ffn_in1", "ret_o", "ret_qkvg")
LAYER0_FFN_GRADS = ("ffn_out0", "ffn_in0")
LAST_GRADS = ("attn_o", "attn_qkv")


def _fill_weights(wts, full):
    for name, w in full.items():
        if name[:-1] == "ffn_in":
            wts[name[:-1]][int(name[-1])] = w
        elif name[:-1] == "ffn_out":
            wts["ffn_out"][int(name[-1])] = w.reshape(-1, w.shape[2])
        elif name in ("attn_o", "ret_o"):
            wts[name] = w.reshape(-1, w.shape[2])
        elif name == "attn_qkv":
            wts[name] = w.transpose(1, 0, 2).reshape(w.shape[1], -1)
        else:
            wts[name] = w


class _StepPlan:
    def __init__(self, placed, pos):
        self.placed, self.pos = placed, pos
        self.layer1 = self.layer0_ffn = None
        self.reduced = {}

    def gather_job(self):
        return _gather_job([self.placed[k] for k in LATE_WEIGHTS])

    def late_weights(self, outs, wts):
        _fill_weights(wts, dict(zip(LATE_WEIGHTS, outs)))

    def start_layer1(self, big):
        self.layer1 = _ReduceScatter([big[k] for k in LAYER1_GRADS], self.pos, "l1_")

    def start_layer0_ffn(self, big):
        self.layer0_ffn = _ReduceScatter([big[k] for k in LAYER0_FFN_GRADS], self.pos, "l0f_")

    def exchange_job(self):
        return _merge_jobs(self.layer1.exchange_job(), self.layer0_ffn.exchange_job())

    def after_exchange(self, land):
        n1 = len(LAYER1_GRADS)
        self.reduced.update(zip(LAYER1_GRADS, self.layer1.after_exchange(land[:n1])))
        self.reduced.update(zip(LAYER0_FFN_GRADS, self.layer0_ffn.after_exchange(land[n1:])))


def _all_reduce_small(v, name):
    def body(v_ref, o_ref, land_ref, send_sems, recv_sems):
        x, y, c = _mesh_pos()
        me = 4 * x + 2 * y + c
        land_ref[me] = v_ref[...]
        for t in range(N_DEVICES):
            @pl.when(t != me)
            def _(t=t):
                _remote(v_ref, land_ref.at[me], send_sems.at[t], recv_sems.at[me], (t // 4, (t // 2) % 2, t % 2)).start()
        for t in range(N_DEVICES):
            @pl.when(t != me)
            def _(t=t):
                _remote(v_ref, land_ref.at[t], send_sems.at[t], recv_sems.at[t], (t // 4, (t // 2) % 2, t % 2)).wait()
        acc = land_ref[0]
        for t in range(1, N_DEVICES):
            acc = acc + land_ref[t]
        o_ref[...] = acc

    vmem = pl.BlockSpec(memory_space=pltpu.VMEM)
    return pl.pallas_call(
        body, name=name, in_specs=[vmem], out_specs=vmem, out_shape=jax.ShapeDtypeStruct(v.shape, F32),
        scratch_shapes=[pltpu.VMEM((N_DEVICES,) + v.shape, F32), pltpu.SemaphoreType.DMA((N_DEVICES,)),
                        pltpu.SemaphoreType.DMA((N_DEVICES,))],
    )(v)


def _all_to_all_small(v, name):
    def body(v_ref, o_ref, send_sems, recv_sems):
        x, y, c = _mesh_pos()
        me = 4 * x + 2 * y + c
        o_ref[me] = v_ref[me]
        for t in range(N_DEVICES):
            @pl.when(t != me)
            def _(t=t):
                _remote(v_ref.at[t], o_ref.at[me], send_sems.at[t], recv_sems.at[me], (t // 4, (t // 2) % 2, t % 2)).start()
        for t in range(N_DEVICES):
            @pl.when(t != me)
            def _(t=t):
                _remote(v_ref.at[t], o_ref.at[t], send_sems.at[t], recv_sems.at[t], (t // 4, (t // 2) % 2, t % 2)).wait()

    vmem = pl.BlockSpec(memory_space=pltpu.VMEM)
    return pl.pallas_call(
        body, name=name, in_specs=[vmem], out_specs=vmem, out_shape=jax.ShapeDtypeStruct(v.shape, F32),
        scratch_shapes=[pltpu.SemaphoreType.DMA((N_DEVICES,)), pltpu.SemaphoreType.DMA((N_DEVICES,))],
    )(v)


ALL_ROWS = 40


class _AdaLN:
    def __init__(self, c, c_ctx, ada_w, ada_b, riders):
        xi, yi, ci = _mesh_pos()
        self.me, self.chip, self.core = 4 * xi + 2 * yi + ci, 2 * xi + yi, ci
        self.nb, d = c.shape
        self.ada_w, self.c_ctx = ada_w, c_ctx
        self.cols = ada_w.shape[2]
        ctx_row = self.nb * N_DEVICES
        assert ctx_row + 1 + riders.shape[0] <= ALL_ROWS
        placed = lax.dynamic_update_slice(jnp.zeros((ALL_ROWS, d), F32), c, (self.me * self.nb, 0))
        placed = lax.dynamic_update_slice(placed, riders, (ctx_row + 1, 0))
        summed = _all_reduce_small(placed, "gather_conditioning")
        self.riders = summed[ctx_row + 1:ctx_row + 1 + riders.shape[0]]
        c_all = summed.at[ctx_row].set(c_ctx)
        self.cact, = _whole("cond_silu", lambda v: (_silu(v),), [jax.ShapeDtypeStruct(c_all.shape, F32)], c_all)
        parts = []
        for i in range(2):
            bias = lax.dynamic_slice(ada_b[i], (self.chip * self.cols,), (self.cols,))[None, :]
            parts.append(_mm_nn(self.cact, ada_w[i], F32, f"mod{i}", ALL_ROWS, self.cols, d, bias=bias))
        part = jnp.concatenate(parts, axis=1)
        rows = [[t * self.nb + b for b in range(self.nb)] + [ctx_row] * (MOD_ROWS - self.nb) for t in range(N_DEVICES)]
        got = _all_to_all_small(part[jnp.asarray(rows)], "mod_exchange")
        self.mods = [jnp.concatenate([got[2 * j][:self.nb + 1, i * self.cols:(i + 1) * self.cols] for j in range(N_CHIPS)], axis=1)[:, None, :]
                     for i in range(2)]

    def backward(self, dmods):
        nb, cols, d = self.nb, self.cols, self.ada_w.shape[1]
        blocks = [jnp.concatenate([dm[:, j * cols:(j + 1) * cols] for dm in dmods], axis=1) for j in range(N_CHIPS)]
        got = _all_to_all_small(jnp.stack([blocks[t // 2] for t in range(N_DEVICES)]), "dmod_exchange")
        dall = jnp.concatenate([got[:, :nb].reshape(N_DEVICES * nb, 2 * cols), jnp.sum(got[:, nb], axis=0, keepdims=True),
                                jnp.zeros((ALL_ROWS - N_DEVICES * nb - 1, 2 * cols), F32)], axis=0)
        dctx = jnp.concatenate([dall[N_DEVICES * nb][None, :], jnp.zeros((MOD_ROWS - 1, 2 * cols), F32)], axis=0)
        grads, dcact = [], []
        for i in range(2):
            grads.append(_mm_tn(self.cact, dall[:, i * cols:(i + 1) * cols], f"ada_dw{i}", d, cols, ALL_ROWS))
            dcact.append(_mm_nt(dctx[:, i * cols:(i + 1) * cols], self.ada_w[i], F32, f"ada_dx{i}", MOD_ROWS, d, cols))

        def silu_bwd(v, d0, d1):
            sg = _sigmoid(v)
            return ((d0 + d1)[0:1] * (sg * (1.0 + v * (1.0 - sg))),)

        dc_ctx, = _whole("cond_silu_bwd", silu_bwd, [jax.ShapeDtypeStruct((1, d), F32)], self.c_ctx[None, :], dcact[0], dcact[1])
        return grads, jnp.where(self.core == 0, dc_ctx[0], jnp.zeros((d,), F32))


SMALL_ROWS = 24


def _pack_small(small, dlogit):
    d = D_MODEL
    misc = jnp.zeros((d,), F32)
    misc = misc.at[0:HEAD_DIM].set(small["q_norm"]).at[128:128 + HEAD_DIM].set(small["k_norm"])
    misc = misc.at[256:256 + N_HEADS].set(small["sink"]).at[384:384 + 2 * RET_HEADS].set(dlogit.reshape(-1))
    rows = [small["ada_b0"].reshape(6, d), small["ada_b1"].reshape(6, d), small["norm1_g0"][None], small["norm1_g1"][None],
            small["norm2_g0"][None], small["norm2_g1"][None], small["c_ctx"][None], small["gn_g"].reshape(2, d), misc[None]]
    buf = jnp.concatenate(rows, axis=0)
    return jnp.concatenate([buf, jnp.zeros((SMALL_ROWS - buf.shape[0], d), F32)], axis=0)


def _unpack_small(buf):
    d = D_MODEL
    misc = buf[19]
    return dict(ada_b=buf[0:12].reshape(2, 6 * d), norm1_g=buf[12:14], norm2_g=buf[14:16], c_ctx=buf[16],
                gn_g=buf[17:19].reshape(2 * d), q_norm=misc[0:HEAD_DIM], k_norm=misc[128:128 + HEAD_DIM],
                sink=misc[256:256 + N_HEADS], decay=misc[384:384 + 2 * RET_HEADS])


def kernel(x, c, ctx, c_ctx, ada_w, ada_b, norm1_g, norm2_g, ffn_w_in, ffn_w_out, attn_w_qkv, attn_q_norm, attn_k_norm, attn_sink, attn_w_o, ret_w_qkvg, ret_decay_logit, ret_gn_g, ret_w_o, loss_target, m_c_ctx, m_ada_w, m_ada_b, m_norm1_g, m_norm2_g, m_ffn_w_in, m_ffn_w_out, m_attn_w_qkv, m_attn_q_norm, m_attn_k_norm, m_attn_sink, m_attn_w_o, m_ret_w_qkvg, m_ret_decay_logit, m_ret_gn_g, m_ret_w_o, v_c_ctx, v_ada_w, v_ada_b, v_norm1_g, v_norm2_g, v_ffn_w_in, v_ffn_w_out, v_attn_w_qkv, v_attn_q_norm, v_attn_k_norm, v_attn_sink, v_attn_w_o, v_ret_w_qkvg, v_ret_decay_logit, v_ret_gn_g, v_ret_w_o):
    xi, yi, ci = _mesh_pos()
    chip = 2 * xi + yi
    nb, s, d = x.shape
    gn_shard = ret_gn_g.shape[1]

    shards = dict(ffn_in0=(ffn_w_in, 0), ffn_in1=(ffn_w_in, 1), ffn_out0=(ffn_w_out, 0),
                  ffn_out1=(ffn_w_out, 1), attn_qkv=(attn_w_qkv, 0), attn_o=(attn_w_o, 0), ret_qkvg=(ret_w_qkvg, 0), ret_o=(ret_w_o, 0))
    names = list(shards)
    pos = jnp.stack([ci, chip]).astype(jnp.int32)
    placed = {k: _place_shard(*shards[k], pos, f"place_{k}") for k in names}
    early = _run_job(_gather_job([placed[k] for k in EARLY_WEIGHTS]), "gather_early_weights")
    gn_mine = jnp.where(ci == 0, ret_gn_g[0], jnp.zeros_like(ret_gn_g[0]))
    gn_place = lax.dynamic_update_slice(jnp.zeros((RET_VWIDTH,), F32), gn_mine, (chip * gn_shard,))

    wts = dict(ffn_in=[None, None], ffn_out=[None, None], attn_qkv=None, attn_o=None, ret_qkvg=None, ret_o=None)
    ada = _AdaLN(c, c_ctx, ada_w, ada_b, riders=gn_place.reshape(2, d))
    gn_full = ada.riders.reshape(RET_VWIDTH)
    _fill_weights(wts, dict(zip(EARLY_WEIGHTS, early)))
    plan = _StepPlan(placed, pos)
    decay_logit = ret_decay_logit[0]
    sp = dict(norm1_g=norm1_g, norm2_g=norm2_g, q_norm=attn_q_norm[0], k_norm=attn_k_norm[0],
              sink=attn_sink[0], log_g=jax.nn.log_sigmoid(decay_logit), gn_g=gn_full)
    loss_part, dz, big, small, dmods = _local_step(x, ctx, loss_target, sp, wts, ada.mods, plan)
    ada_grads, small["c_ctx"] = ada.backward(dmods)

    loss = lax.psum(loss_part[0, 0], ("x", "y", "c"))
    grad_x = dz.reshape(nb, s, d)

    dlogit = small["log_g"] * jax.nn.sigmoid(-decay_logit)
    sg = _unpack_small(_all_reduce_small(_pack_small(small, dlogit), "reduce_small_grads"))
    halves = dict(plan.reduced)
    halves.update(zip(LAST_GRADS, _ReduceScatter([big[k] for k in LAST_GRADS], pos, "last_").run()))
    reduced = dict(zip(halves, _pair_share(list(halves.values()), "grads_pair_share")))

    grads = dict(
        c_ctx=sg["c_ctx"], ada_w=jnp.stack(ada_grads), ada_b=sg["ada_b"], norm1_g=sg["norm1_g"],
        norm2_g=sg["norm2_g"], ffn_w_in=jnp.stack([reduced["ffn_in0"], reduced["ffn_in1"]]),
        ffn_w_out=jnp.stack([reduced["ffn_out0"], reduced["ffn_out1"]]), attn_w_qkv=reduced["attn_qkv"][None],
        attn_q_norm=sg["q_norm"][None], attn_k_norm=sg["k_norm"][None], attn_sink=sg["sink"][None],
        attn_w_o=reduced["attn_o"][None], ret_w_qkvg=reduced["ret_qkvg"][None], ret_decay_logit=sg["decay"].reshape(1, 2, RET_HEADS),
        ret_gn_g=lax.dynamic_slice(sg["gn_g"], (chip * gn_shard,), (gn_shard,))[None], ret_w_o=reduced["ret_o"][None])
    params = dict(c_ctx=(c_ctx, m_c_ctx, v_c_ctx), ada_w=(ada_w, m_ada_w, v_ada_w), ada_b=(ada_b, m_ada_b, v_ada_b),
                  norm1_g=(norm1_g, m_norm1_g, v_norm1_g), norm2_g=(norm2_g, m_norm2_g, v_norm2_g),
                  ffn_w_in=(ffn_w_in, m_ffn_w_in, v_ffn_w_in), ffn_w_out=(ffn_w_out, m_ffn_w_out, v_ffn_w_out),
                  attn_w_qkv=(attn_w_qkv, m_attn_w_qkv, v_attn_w_qkv), attn_q_norm=(attn_q_norm, m_attn_q_norm, v_attn_q_norm),
                  attn_k_norm=(attn_k_norm, m_attn_k_norm, v_attn_k_norm), attn_sink=(attn_sink, m_attn_sink, v_attn_sink),
                  attn_w_o=(attn_w_o, m_attn_w_o, v_attn_w_o), ret_w_qkvg=(ret_w_qkvg, m_ret_w_qkvg, v_ret_w_qkvg),
                  ret_decay_logit=(ret_decay_logit, m_ret_decay_logit, v_ret_decay_logit),
                  ret_gn_g=(ret_gn_g, m_ret_gn_g, v_ret_gn_g), ret_w_o=(ret_w_o, m_ret_w_o, v_ret_w_o))
    order = list(params)
    deltas, new_m, new_v = [], [], []
    for k in order:
        w, m, v = params[k]
        g = grads[k].reshape(w.shape)
        grads[k] = g
        flat = (-1, w.shape[-1]) if w.ndim > 1 else (1, -1)
        if k == "ret_decay_logit":
            flat = (1, -1)
        dw, nm, nv = _adamw(w.reshape(flat), g.reshape(flat), m.reshape(flat), v.reshape(flat), f"adamw_{k}")
        deltas.append(dw.reshape(w.shape))
        new_m.append(nm.reshape(w.shape))
        new_v.append(nv.reshape(w.shape))
    return (loss, grad_x, *[grads[k] for k in order], *deltas, *new_m, *new_v)
```

```python
import functools

import jax
import jax.numpy as jnp
from jax import lax
from jax.experimental import pallas as pl
from jax.experimental.pallas import tpu as pltpu

F32 = jnp.float32
BF16 = jnp.bfloat16

D_MODEL = 1024
N_HEADS = 16
N_KV_HEADS = 4
HEAD_DIM = 64
WINDOW = 128
ATTN_BLOCK = 128
BAND = ATTN_BLOCK + 2 * WINDOW
RET_HEADS = 4
RET_QK_DIM = 256
RET_V_DIM = 512
RET_VWIDTH = 2048
RET_CHUNK = 128
D_FF = 2816
GRID_W = 64
ROPE_BASE = 10000.0
EPS = 1e-6
NEG_INF = -1e30
LANES = 128

ADAM_LR = 0.001
ADAM_B1 = 0.9
ADAM_B2 = 0.999
ADAM_EPS = 1e-08
ADAM_WD = 0.01
ADAM_STEP = 10

VMEM_LIMIT_BYTES = 56 * 1024 * 1024
MESH = pl.DeviceIdType.MESH
N_CHIPS = 4


def _cparams(*sem):
    return pltpu.CompilerParams(dimension_semantics=sem, vmem_limit_bytes=VMEM_LIMIT_BYTES)


_DIMS = {"nn": ((1,), (0,)), "nt": ((1,), (1,)), "tn": ((0,), (0,))}


def _dot(a, b, form):
    return lax.dot_general(a.astype(BF16), b.astype(BF16), (_DIMS[form], ((), ())), preferred_element_type=F32)


@functools.partial(jax.custom_vjp, nondiff_argnums=(2,))
def _mm(a, b, form):
    return _dot(a, b, form)


def _mm_fwd(a, b, form):
    return _dot(a, b, form), (a, b)


def _mm_bwd(form, res, ct):
    a, b = res
    if form == "nn":
        da, db = _dot(ct, b, "nt"), _dot(a, ct, "tn")
    elif form == "nt":
        da, db = _dot(ct, b, "nn"), _dot(ct, a, "tn")
    else:
        da, db = _dot(b, ct, "nt"), _dot(a, ct, "nn")
    return da.astype(a.dtype), db.astype(b.dtype)


_mm.defvjp(_mm_fwd, _mm_bwd)


def _swap_halves(x, half):
    w = x.shape[-1]
    lane = lax.broadcasted_iota(jnp.int32, x.shape, x.ndim - 1)
    return jnp.where(lane % (2 * half) < half, pltpu.roll(x, w - half, x.ndim - 1), pltpu.roll(x, half, x.ndim - 1))


@functools.partial(jax.custom_vjp, nondiff_argnums=(1,))
def _rot(x, half):
    return _swap_halves(x, half)


def _rot_fwd(x, half):
    return _swap_halves(x, half), None


def _rot_bwd(half, _, ct):
    return (_swap_halves(ct, half),)


_rot.defvjp(_rot_fwd, _rot_bwd)


def _rope(x, cos, sin_signed, half):
    return x * cos + _rot(x, half) * sin_signed


def _head_mean_square(x):
    r = lax.broadcasted_iota(jnp.int32, (LANES, LANES), 0) // HEAD_DIM
    c = lax.broadcasted_iota(jnp.int32, (LANES, LANES), 1) // HEAD_DIM
    g = jnp.where(r == c, 1.0 / HEAD_DIM, 0.0).astype(F32)
    return jnp.dot(x * x, g, precision=lax.Precision.HIGHEST, preferred_element_type=F32)


def _qk_chunk(x, gain, cos, sin_signed, scale):
    y = x * lax.rsqrt(_head_mean_square(x) + EPS) * gain
    return _rope(y, cos, sin_signed, HEAD_DIM // 4) * scale


def _sigmoid(x):
    return 1.0 / (1.0 + jnp.exp(-x))


def _silu(x):
    return x * _sigmoid(x)


def _mm_nn(a, w, out_dtype, name, tm, tn, tk, bias=None):
    m, k_dim = a.shape
    if w.ndim == 3:
        n = w.shape[0] * w.shape[2]
        per = w.shape[2] // tn
        assert w.shape[2] % tn == 0
        w_spec = pl.BlockSpec((None, tk, tn), lambda i, j, k: (j // per, k, j % per))
    else:
        n = w.shape[1]
        w_spec = pl.BlockSpec((tk, tn), lambda i, j, k: (k, j))
    assert m % tm == 0 and n % tn == 0 and k_dim % tk == 0, (name, a.shape, w.shape, tm, tn, tk)
    nk = k_dim // tk
    has_bias = bias is not None

    def body(*refs):
        a_ref, w_ref = refs[0], refs[1]
        b_ref = refs[2] if has_bias else None
        o_ref, acc_ref = (refs[-1], None) if nk == 1 else (refs[-2], refs[-1])
        if nk == 1:
            part = jnp.dot(a_ref[...].astype(BF16), w_ref[...].astype(BF16), preferred_element_type=F32)
            o_ref[...] = (part + b_ref[...] if has_bias else part).astype(out_dtype)
            return
        k = pl.program_id(2)

        @pl.when(k == 0)
        def _():
            acc_ref[...] = jnp.zeros_like(acc_ref)

        acc_ref[...] += jnp.dot(a_ref[...].astype(BF16), w_ref[...].astype(BF16), preferred_element_type=F32)

        @pl.when(k == nk - 1)
        def _():
            r = acc_ref[...]
            if has_bias:
                r = r + b_ref[...]
            o_ref[...] = r.astype(out_dtype)

    in_specs = [pl.BlockSpec((tm, tk), lambda i, j, k: (i, k)), w_spec]
    args = [a, w]
    if has_bias:
        in_specs.append(pl.BlockSpec((1, tn), lambda i, j, k: (0, j)))
        args.append(bias)
    return pl.pallas_call(
        body, name=name, grid=(m // tm, n // tn, nk), in_specs=in_specs,
        out_specs=pl.BlockSpec((tm, tn), lambda i, j, k: (i, j)),
        out_shape=jax.ShapeDtypeStruct((m, n), out_dtype),
        scratch_shapes=[pltpu.VMEM((tm, tn), F32)] if nk > 1 else [],
        compiler_params=_cparams("parallel", "parallel", "arbitrary"),
    )(*args)


def _mm_nt(a, w, out_dtype, name, tm, tn, tk):
    if a.ndim == 3:
        planes, m, plane_w = a.shape
        c_dim = planes * plane_w
        a_per = plane_w // tk
        assert plane_w % tk == 0
        a_spec = pl.BlockSpec((None, tm, tk), lambda i, j, k: (k // a_per, i, k % a_per))
    else:
        m, c_dim = a.shape
        a_spec = pl.BlockSpec((tm, tk), lambda i, j, k: (i, k))
    if w.ndim == 3:
        k_out = w.shape[1]
        per = w.shape[2] // tk
        assert w.shape[2] % tk == 0 and w.shape[0] * w.shape[2] == c_dim
        w_spec = pl.BlockSpec((None, tn, tk), lambda i, j, k: (k // per, j, k % per))
    else:
        k_out = w.shape[0]
        assert w.shape[1] == c_dim
        w_spec = pl.BlockSpec((tn, tk), lambda i, j, k: (j, k))
    assert m % tm == 0 and k_out % tn == 0 and c_dim % tk == 0, (name, a.shape, w.shape, tm, tn, tk)
    nk = c_dim // tk

    def body(a_ref, w_ref, o_ref, acc_ref=None):
        if nk == 1:
            o_ref[...] = _dot(a_ref[...], w_ref[...], "nt").astype(out_dtype)
            return
        k = pl.program_id(2)

        @pl.when(k == 0)
        def _():
            acc_ref[...] = jnp.zeros_like(acc_ref)

        acc_ref[...] += _dot(a_ref[...], w_ref[...], "nt")

        @pl.when(k == nk - 1)
        def _():
            o_ref[...] = acc_ref[...].astype(out_dtype)

    return pl.pallas_call(
        body, name=name, grid=(m // tm, k_out // tn, nk),
        in_specs=[a_spec, w_spec],
        out_specs=pl.BlockSpec((tm, tn), lambda i, j, k: (i, j)),
        out_shape=jax.ShapeDtypeStruct((m, k_out), out_dtype),
        scratch_shapes=[pltpu.VMEM((tm, tn), F32)] if nk > 1 else [],
        compiler_params=_cparams("parallel", "parallel", "arbitrary"),
    )(a, w)


def _mm_tn(a, b, name, tm, tn, tk, shards=None, out_dtype=F32):
    r, k_dim = a.shape
    if b.ndim == 3:
        n = b.shape[0] * b.shape[2]
        b_per = b.shape[2] // tn
        assert b.shape[2] % tn == 0
        b_spec = pl.BlockSpec((None, tk, tn), lambda i, j, k: (j // b_per, k, j % b_per))
    else:
        n = b.shape[1]
        b_spec = pl.BlockSpec((tk, tn), lambda i, j, k: (k, j))
    assert r % tk == 0 and k_dim % tm == 0 and n % tn == 0, (name, a.shape, b.shape, tm, tn, tk)
    nk = r // tk
    if shards:
        per = n // shards // tn
        assert n % (shards * tn) == 0
        out_shape = jax.ShapeDtypeStruct((shards, k_dim, n // shards), out_dtype)
        out_spec = pl.BlockSpec((None, tm, tn), lambda i, j, k: (j // per, i, j % per))
    else:
        out_shape = jax.ShapeDtypeStruct((k_dim, n), out_dtype)
        out_spec = pl.BlockSpec((tm, tn), lambda i, j, k: (i, j))
    direct = out_dtype == F32

    def body(a_ref, b_ref, o_ref, *scratch):
        acc_ref = o_ref if direct else scratch[0]
        k = pl.program_id(2)

        @pl.when(k == 0)
        def _():
            acc_ref[...] = jnp.zeros_like(acc_ref)

        acc_ref[...] += _dot(a_ref[...], b_ref[...], "tn")
        if not direct:
            @pl.when(k == nk - 1)
            def _():
                o_ref[...] = acc_ref[...].astype(out_dtype)

    return pl.pallas_call(
        body, name=name, grid=(k_dim // tm, n // tn, nk),
        in_specs=[pl.BlockSpec((tk, tm), lambda i, j, k: (k, i)), b_spec],
        out_specs=out_spec, out_shape=out_shape,
        scratch_shapes=[] if direct else [pltpu.VMEM((tm, tn), F32)],
        compiler_params=_cparams("parallel", "parallel", "arbitrary"),
    )(a, b)


class _Carrier:
    def __init__(self, job, n_in, n_out, n_scratch):
        self.job, self.n_in, self.n_out, self.n_scratch = job, n_in, n_out, n_scratch
        self.ji = len(job.inputs) if job else 0
        self.jo = len(job.out_shapes) if job else 0

    def operands(self):
        return list(self.job.inputs) if self.job else []

    def in_specs(self):
        return [pl.BlockSpec(memory_space=pl.ANY)] * self.ji

    def out_specs(self):
        return [pl.BlockSpec(memory_space=pl.ANY)] * self.jo

    def out_shapes(self):
        return list(self.job.out_shapes) if self.job else []

    def scratch(self):
        return list(self.job.sem_shapes) if self.job else []

    def aliases(self):
        return {self.n_in + a: self.n_out + b for a, b in self.job.aliases.items()} if self.job else {}

    def split(self, refs):
        a = self.n_in
        b = a + self.ji
        c = b + self.n_out
        d = c + self.jo
        e = d + self.n_scratch
        return list(refs[:a]) + list(refs[b:c]) + list(refs[d:e]), (refs[a:b], refs[c:d], refs[e:])

    def run(self, job_refs, step, steps):
        if not self.job:
            return
        for stage, mark in zip(self.job.stages, _job_marks(self.job, steps)):
            pl.when(step == mark)(functools.partial(stage, *job_refs))

    def results(self, res):
        res = list(res)
        return res[:self.n_out], res[self.n_out:]


FFN_ROW_TILE = 768


def _ffn_tile(r):
    return FFN_ROW_TILE if r % FFN_ROW_TILE == 0 else _row_tile(r)


def _ffn_in_swiglu(h, w, name):
    r, k_dim = h.shape
    n4 = w.shape[2]
    tm = _ffn_tile(r)

    def body(h_ref, wg_ref, wu_ref, u_ref, a_ref):
        hv = h_ref[...]
        g = jnp.dot(hv, wg_ref[...], preferred_element_type=F32)
        up = jnp.dot(hv, wu_ref[...], preferred_element_type=F32)
        u_ref[0] = g.astype(BF16)
        u_ref[1] = up.astype(BF16)
        a_ref[...] = (_silu(g) * up).astype(BF16)

    return pl.pallas_call(
        body, name=name, grid=(r // tm, 2),
        in_specs=[pl.BlockSpec((tm, k_dim), lambda i, j: (i, 0)),
                  pl.BlockSpec((None, k_dim, n4), lambda i, j: (j, 0, 0)),
                  pl.BlockSpec((None, k_dim, n4), lambda i, j: (j + 2, 0, 0))],
        out_specs=[pl.BlockSpec((2, tm, n4), lambda i, j: (0, i, j)), pl.BlockSpec((tm, n4), lambda i, j: (i, j))],
        out_shape=[jax.ShapeDtypeStruct((2, r, 2 * n4), BF16), jax.ShapeDtypeStruct((r, 2 * n4), BF16)],
        compiler_params=_cparams("parallel", "parallel"),
    )(h, w, w)


def _mm_nn_gate_residual(geo, a, w, z, mod, off, name, norm=None):
    r, k_dim = a.shape
    n = w.shape[1]
    tm = FFN_ROW_TILE if geo.seg % FFN_ROW_TILE == 0 else 256
    tiles = geo.seg // tm
    assert geo.seg % tm == 0 and r == geo.r and n == D_MODEL

    def body(a_ref, w_ref, z_ref, mx_ref, mc_ref, *rest):
        out = jnp.dot(a_ref[...], w_ref[...], preferred_element_type=F32)
        is_x = (pl.program_id(0) % tiles) * tm + lax.broadcasted_iota(jnp.int32, (tm, 1), 0) < geo.s
        zo = z_ref[...] + jnp.where(is_x, mx_ref[:, off:off + n], mc_ref[:, off:off + n]) * out
        if norm:
            g_ref, nx_ref, nc_ref, zo_ref, raw_ref, h_ref = rest
            no = norm[2]
            shift = jnp.where(is_x, nx_ref[:, no:no + n], nc_ref[:, no:no + n])
            scale = jnp.where(is_x, nx_ref[:, no + n:no + 2 * n], nc_ref[:, no + n:no + 2 * n])
            rs = lax.rsqrt(jnp.mean(zo * zo, axis=-1, keepdims=True) + EPS)
            h_ref[...] = ((zo * rs) * g_ref[...] * (1.0 + scale) + shift).astype(BF16)
        else:
            zo_ref, raw_ref = rest
        zo_ref[...] = zo
        raw_ref[...] = out.astype(BF16)

    def mod_specs(m):
        return [pl.BlockSpec((None, 1, m.shape[2]), lambda i: (i // tiles, 0, 0)), pl.BlockSpec((None, 1, m.shape[2]), lambda i: (geo.b, 0, 0))]

    row = pl.BlockSpec((tm, n), lambda i: (i, 0))
    in_specs = [pl.BlockSpec((tm, k_dim), lambda i: (i, 0)), pl.BlockSpec((k_dim, n), lambda i: (0, 0)), row] + mod_specs(mod)
    args = [a, w, z, mod, mod]
    out_specs, out_shape = [row, row], [jax.ShapeDtypeStruct((r, n), F32), jax.ShapeDtypeStruct((r, n), BF16)]
    if norm:
        in_specs += [pl.BlockSpec((1, n), lambda i: (0, 0))] + mod_specs(norm[1])
        args += [norm[0], norm[1], norm[1]]
        out_specs.append(row)
        out_shape.append(jax.ShapeDtypeStruct((r, n), BF16))
    res = pl.pallas_call(body, name=name, grid=(r // tm,), in_specs=in_specs, out_specs=out_specs, out_shape=out_shape,
                         compiler_params=_cparams("parallel"))(*args)
    return res if norm else (*res, None)


def _ffn_out_dx_swiglu_bwd(df, w_out, u, name, job=None):
    r, d = df.shape
    n4 = u.shape[2] // 2
    tm = _ffn_tile(r)
    carrier = _Carrier(job, 3, 1, 0)
    steps = (r // tm) * 2

    def body(*refs):
        (df_ref, w_ref, u_ref, du_ref), job_refs = carrier.split(refs)
        carrier.run(job_refs, pl.program_id(0) * 2 + pl.program_id(1), steps)
        da = _dot(df_ref[...], w_ref[...], "nt")
        g, up = u_ref[0].astype(F32), u_ref[1].astype(F32)
        s = _sigmoid(g)
        du_ref[0] = (da * up * (s * (1.0 + g * (1.0 - s)))).astype(BF16)
        du_ref[1] = (da * (g * s)).astype(BF16)

    res = pl.pallas_call(
        body, name=name, grid=(r // tm, 2),
        in_specs=[pl.BlockSpec((tm, d), lambda i, j: (i, 0)), pl.BlockSpec((n4, d), lambda i, j: (j, 0)),
                  pl.BlockSpec((2, tm, n4), lambda i, j: (0, i, j))] + carrier.in_specs(),
        out_specs=[pl.BlockSpec((2, tm, n4), lambda i, j: (0, i, j))] + carrier.out_specs(),
        out_shape=[jax.ShapeDtypeStruct(u.shape, BF16)] + carrier.out_shapes(),
        scratch_shapes=carrier.scratch(), input_output_aliases=carrier.aliases(),
        compiler_params=_cparams("arbitrary", "arbitrary"),
    )(df, w_out, u, *carrier.operands())
    (du,), extra = carrier.results(res)
    return du, extra


class _Rows:
    def __init__(self, b, s, l):
        self.b, self.s, self.l = b, s, l
        self.seg = s + l
        self.r = b * self.seg


def _rowwise(name, body, geo, tm, ins, outs, job=None):
    seg_blocks, x_blocks = geo.seg // tm, geo.s // tm
    per_part = {"ex", "xrow"} & {k for _, k in ins if isinstance(k, str)} or {"exacc", "xrow"} & {o[0] for o in outs}
    assert geo.seg % tm == 0 and (geo.s % tm == 0 or not per_part), (name, tm)
    nb = geo.b

    def is_ctx(i):
        return i % seg_blocks >= x_blocks

    in_specs, args = [], []
    for arr, kind in ins:
        args.append(arr)
        if kind == "row":
            in_specs.append(pl.BlockSpec((tm, arr.shape[1]), lambda i: (i, 0)))
        elif kind == "ex":
            in_specs.append(pl.BlockSpec((None, 1, arr.shape[2]), lambda i: (jnp.where(is_ctx(i), nb, i // seg_blocks), 0, 0)))
        elif kind == "full":
            in_specs.append(pl.BlockSpec(arr.shape, lambda i, nd=arr.ndim: (0,) * nd))
        elif kind == "tab":
            in_specs.append(pl.BlockSpec((tm, arr.shape[1]), lambda i: (i % seg_blocks, 0)))
        elif kind == "xrow":
            in_specs.append(pl.BlockSpec(
                (tm, arr.shape[1]), lambda i: ((i // seg_blocks) * x_blocks + jnp.minimum(i % seg_blocks, x_blocks - 1), 0)))
        else:
            _, width, cb = kind
            in_specs.append(pl.BlockSpec((tm, width), lambda i, cb=cb: (i, cb)))
    out_specs, out_shapes = [], []
    for o in outs:
        if o[0] == "row":
            out_specs.append(pl.BlockSpec((tm, o[1]), lambda i: (i, 0)))
            out_shapes.append(jax.ShapeDtypeStruct((geo.r, o[1]), o[2]))
        elif o[0] == "xrow":
            out_specs.append(pl.BlockSpec(
                (tm, o[1]), lambda i: ((i // seg_blocks) * x_blocks + jnp.minimum(i % seg_blocks, x_blocks - 1), 0)))
            out_shapes.append(jax.ShapeDtypeStruct((geo.b * geo.s, o[1]), o[2]))
        elif o[0] == "exacc":
            out_specs.append(pl.BlockSpec((None, 1, o[1]), lambda i: (jnp.where(is_ctx(i), nb, 0) + i // seg_blocks, 0, 0)))
            out_shapes.append(jax.ShapeDtypeStruct((2 * nb, 1, o[1]), F32))
        else:
            out_specs.append(pl.BlockSpec((o[1], o[2]), lambda i: (0, 0)))
            out_shapes.append(jax.ShapeDtypeStruct((o[1], o[2]), F32))
    n_in = len(ins)
    carrier = _Carrier(job, n_in, len(outs), 0)

    def kern(*refs):
        i = pl.program_id(0)
        refs, job_refs = carrier.split(refs)
        carrier.run(job_refs, i, geo.r // tm)
        res = body(i, *[r[...].astype(F32) for r in refs[:n_in]])
        if not isinstance(res, (tuple, list)):
            res = (res,)
        jj = i % seg_blocks
        first_of_part = (jj == 0) | (jj == x_blocks)
        for o, ref, val in zip(outs, refs[n_in:], res):
            if o[0] == "row":
                ref[...] = val.astype(ref.dtype)
            elif o[0] == "xrow":
                @pl.when(jj < x_blocks)
                def _(ref=ref, val=val):
                    ref[...] = val.astype(ref.dtype)
            else:
                first = first_of_part if o[0] == "exacc" else i == 0

                @pl.when(first)
                def _(ref=ref, val=val):
                    ref[...] = val

                @pl.when(jnp.logical_not(first))
                def _(ref=ref, val=val):
                    ref[...] += val

    res = pl.pallas_call(
        kern, name=name, grid=(geo.r // tm,), in_specs=in_specs + carrier.in_specs(), out_specs=out_specs + carrier.out_specs(),
        out_shape=out_shapes + carrier.out_shapes(), scratch_shapes=carrier.scratch(), input_output_aliases=carrier.aliases(),
        compiler_params=_cparams("arbitrary"),
    )(*args, *carrier.operands())
    own, extra = carrier.results(res)
    if job:
        return (*own, extra)
    return own[0] if len(own) == 1 else own


def _colsum(v):
    return jnp.sum(v, axis=0, keepdims=True)


def _norm_mod(geo, z, gain, mod, off, name):
    d = D_MODEL

    def body(i, zv, g, m):
        r = lax.rsqrt(jnp.mean(zv * zv, axis=-1, keepdims=True) + EPS)
        return (zv * r) * g * (1.0 + m[:, off + d:off + 2 * d]) + m[:, off:off + d]

    return _rowwise(name, body, geo, 256, [(z, "row"), (gain, "full"), (mod, "ex")], [("row", d, BF16)])


def _norm_mod_bwd(geo, z, gain, mod, off, dh, dz_skip, name, gated=None, latent_only=False, job=None):
    d = D_MODEL

    def body(i, zv, g, m, dhv, skip, *rest):
        r = lax.rsqrt(jnp.mean(zv * zv, axis=-1, keepdims=True) + EPS)
        n = zv * r
        dng = dhv * (1.0 + m[:, off + d:off + 2 * d])
        dn = dng * g
        dz = r * (dn - n * jnp.mean(dn * n, axis=-1, keepdims=True)) + skip
        res = (dz, _colsum(dhv), _colsum(dhv * (n * g)), _colsum(dng * n))
        if gated:
            ov, gm = rest
            res += (dz * gm[:, gated[2]:gated[2] + d], _colsum(dz * ov))
        return res

    ins = [(z, "row"), (gain, "full"), (mod, "ex"), (dh, "row"), (dz_skip, "row")]
    outs = [("xrow" if latent_only else "row", d, F32), ("exacc", d), ("exacc", d), ("gacc", 1, d)]
    if gated:
        ins += [(gated[0], "row"), (gated[1], "ex")]
        outs += [("row", d, BF16), ("exacc", d)]
    return _rowwise(name, body, geo, 256, ins, outs, job)


def _loss_head(geo, z, target, out, mod, off, name):
    seg_blocks, x_blocks = geo.seg // 256, geo.s // 256
    d = D_MODEL

    def body(i, zv, tv, ov, m):
        keep = jnp.where(i % seg_blocks >= x_blocks, 0.0, 1.0)
        err = (zv - tv) * keep
        part = 0.5 * jnp.sum(jnp.mean(err * err, axis=-1, keepdims=True), axis=0, keepdims=True)
        dz = err * (1.0 / d)
        return dz, jnp.broadcast_to(part, (1, LANES)), dz * m[:, off:off + d], _colsum(dz * ov)

    return _rowwise(name, body, geo, 256, [(z, "row"), (target, "xrow"), (out, "row"), (mod, "ex")],
                    [("row", d, F32), ("gacc", 1, LANES), ("row", d, BF16), ("exacc", d)])


Q_SCALE = HEAD_DIM ** -0.5
N_QK_CHUNKS = (N_HEADS + N_KV_HEADS) * HEAD_DIM // LANES
N_Q_CHUNKS = N_HEADS * HEAD_DIM // LANES


def _prep_tile(geo):
    return FFN_ROW_TILE if geo.seg % FFN_ROW_TILE == 0 else 256


def _attn_prep(geo, proj, cos, sin_signed, q_gain, k_gain, name):
    def body(i, p, cs, sn, qg, kg):
        outs = []
        for ch in range(N_QK_CHUNKS):
            is_q = ch < N_Q_CHUNKS
            outs.append(_qk_chunk(p[:, ch * LANES:(ch + 1) * LANES], qg if is_q else kg, cs, sn, Q_SCALE if is_q else 1.0))
        outs.append(p[:, N_QK_CHUNKS * LANES:])
        return jnp.concatenate(outs, axis=1)

    return _rowwise(name, body, geo, _prep_tile(geo), [(proj, "row"), (cos, "tab"), (sin_signed, "tab"), (q_gain, "full"), (k_gain, "full")],
                    [("row", proj.shape[1], BF16)])


def _attn_prep_bwd(geo, proj, cos, sin_signed, q_gain, k_gain, dq, dkv, name):
    kw = N_KV_HEADS * HEAD_DIM

    def body(i, p, cs, sn, qg, kg, dqv, dkvv):
        outs = []
        dgains = [jnp.zeros((1, LANES), F32), jnp.zeros((1, LANES), F32)]
        for ch in range(N_QK_CHUNKS):
            is_q = ch < N_Q_CHUNKS
            scale = Q_SCALE if is_q else 1.0
            ct = dqv[:, ch * LANES:(ch + 1) * LANES] if is_q else dkvv[:, (ch - N_Q_CHUNKS) * LANES:(ch - N_Q_CHUNKS + 1) * LANES]
            _, vjp = jax.vjp(lambda xx, gg, scale=scale: _qk_chunk(xx, gg, cs, sn, scale),
                             p[:, ch * LANES:(ch + 1) * LANES], qg if is_q else kg)
            dx, dg = vjp(ct)
            outs.append(dx)
            dgains[0 if is_q else 1] = dgains[0 if is_q else 1] + dg
        outs.append(dkvv[:, kw:])
        return jnp.concatenate(outs, axis=1), dgains[0], dgains[1]

    return _rowwise(name, body, geo, _prep_tile(geo),
                    [(proj, "row"), (cos, "tab"), (sin_signed, "tab"), (q_gain, "full"), (k_gain, "full"), (dq, "row"), (dkv, "row")],
                    [("row", proj.shape[1], BF16), ("gacc", 1, LANES), ("gacc", 1, LANES)])


def _attn_geometry(geo):
    assert geo.s % ATTN_BLOCK == 0 and geo.l % ATTN_BLOCK == 0 and geo.seg >= BAND
    return geo.seg // ATTN_BLOCK, geo.s // ATTN_BLOCK


def _attn_mask(j, s0, geo):
    r = lax.broadcasted_iota(jnp.int32, (ATTN_BLOCK, geo.l + BAND), 0)
    n = lax.broadcasted_iota(jnp.int32, (ATTN_BLOCK, geo.l + BAND), 1) - geo.l
    dist = (s0 - j * ATTN_BLOCK) + n - r
    return (n < 0) | ((jnp.abs(dist) <= WINDOW) & (s0 + n < geo.s))


def _attn_probs(q, keys, valid, n_ctx, sink):
    s = _dot(q, keys, "nt")
    if valid is not None:
        s = jnp.where(valid, s, NEG_INF)
    m = jnp.maximum(jnp.max(s, axis=-1, keepdims=True), sink)
    e, e_sink = jnp.exp(s - m), jnp.exp(sink - m)
    inv = 1.0 / (jnp.sum(e, axis=-1, keepdims=True) + e_sink)
    return e * inv, e_sink * inv


def _attn_keys(ref, s0, geo, with_band):
    ctx = ref[geo.s:geo.seg, :]
    return jnp.concatenate([ctx, ref[pl.ds(s0, BAND), :]], axis=0) if with_band else ctx


def _attention(geo, qkv, sink, name, job=None):
    n_blocks, n_x_blocks = _attn_geometry(geo)
    qw, kw = N_HEADS * HEAD_DIM, N_KV_HEADS * HEAD_DIM
    group = N_HEADS // N_KV_HEADS
    carrier = _Carrier(job, 4, 1, 0)

    def kern(*refs):
        (sink_ref, q_ref, k_ref, v_ref, o_ref), job_refs = carrier.split(refs)
        j = pl.program_id(1)
        carrier.run(job_refs, pl.program_id(0) * n_blocks + j, geo.b * n_blocks)
        s0 = pl.multiple_of(jnp.clip((j - 1) * ATTN_BLOCK, 0, geo.seg - BAND), ATTN_BLOCK)

        def heads(with_band):
            valid = _attn_mask(j, s0, geo) if with_band else None
            k_all, v_all = _attn_keys(k_ref, s0, geo, with_band), _attn_keys(v_ref, s0, geo, with_band)
            for h in range(N_HEADS):
                kv = slice((h // group) * HEAD_DIM, (h // group + 1) * HEAD_DIM)
                p, _ = _attn_probs(q_ref[:, h * HEAD_DIM:(h + 1) * HEAD_DIM], k_all[:, kv], valid, geo.l, sink_ref[h])
                o_ref[:, h * HEAD_DIM:(h + 1) * HEAD_DIM] = _dot(p, v_all[:, kv], "nn").astype(BF16)

        pl.when(j < n_x_blocks)(lambda: heads(True))
        pl.when(j >= n_x_blocks)(lambda: heads(False))

    res = pl.pallas_call(
        kern, name=name, grid=(geo.b, n_blocks),
        in_specs=[pl.BlockSpec(memory_space=pltpu.SMEM),
                  pl.BlockSpec((ATTN_BLOCK, qw), lambda b, j: (b * n_blocks + j, 0)),
                  pl.BlockSpec((geo.seg, kw), lambda b, j: (b, qw // kw)),
                  pl.BlockSpec((geo.seg, kw), lambda b, j: (b, qw // kw + 1))] + carrier.in_specs(),
        out_specs=[pl.BlockSpec((ATTN_BLOCK, qw), lambda b, j: (b * n_blocks + j, 0))] + carrier.out_specs(),
        out_shape=[jax.ShapeDtypeStruct((geo.r, qw), BF16)] + carrier.out_shapes(),
        scratch_shapes=carrier.scratch(), input_output_aliases=carrier.aliases(),
        compiler_params=_cparams("arbitrary", "arbitrary"),
    )(sink, qkv, qkv, qkv, *carrier.operands())
    (o,), extra = carrier.results(res)
    return o, extra


def _attention_bwd(geo, qkv, sink, do, name, job=None):
    n_blocks, n_x_blocks = _attn_geometry(geo)
    qw, kw = N_HEADS * HEAD_DIM, N_KV_HEADS * HEAD_DIM
    group = N_HEADS // N_KV_HEADS

    carrier = _Carrier(job, 5, 3, 1)

    def kern(*refs):
        (sink_ref, q_ref, k_ref, v_ref, do_ref, dq_ref, dkv_out_ref, dsink_ref, dkv_ref), job_refs = carrier.split(refs)
        b, j = pl.program_id(0), pl.program_id(1)
        carrier.run(job_refs, b * n_blocks + j, geo.b * n_blocks)
        s0 = pl.multiple_of(jnp.clip((j - 1) * ATTN_BLOCK, 0, geo.seg - BAND), ATTN_BLOCK)

        @pl.when(j == 0)
        def _():
            dkv_ref[...] = jnp.zeros_like(dkv_ref)

        @pl.when((j == 0) & (b == 0))
        def _():
            dsink_ref[...] = jnp.zeros_like(dsink_ref)

        def heads(with_band):
            valid = _attn_mask(j, s0, geo) if with_band else None
            k_all, v_all = _attn_keys(k_ref, s0, geo, with_band), _attn_keys(v_ref, s0, geo, with_band)
            for g in range(N_KV_HEADS):
                kv = slice(g * HEAD_DIM, (g + 1) * HEAD_DIM)
                keys, vals = k_all[:, kv], v_all[:, kv]
                group_heads = [slice(h * HEAD_DIM, (h + 1) * HEAD_DIM) for h in range(g * group, (g + 1) * group)]
                ds_rows, p_rows = [], []
                for h, hs in zip(range(g * group, (g + 1) * group), group_heads):
                    dout = do_ref[:, hs]
                    p, p_sink = _attn_probs(q_ref[:, hs], keys, valid, geo.l, sink_ref[h])
                    dp = _dot(dout, vals, "nt")
                    dsum = jnp.sum(p * dp, axis=-1, keepdims=True)
                    ds = (p * (dp - dsum)).astype(BF16)
                    dq_ref[:, hs] = _dot(ds, keys, "nn").astype(BF16)
                    ds_rows.append(ds)
                    p_rows.append(p.astype(BF16))
                    dsink_ref[h:h + 1, :] += jnp.broadcast_to(-jnp.sum(p_sink * dsum, axis=0, keepdims=True), (1, LANES))
                q_rows = jnp.concatenate([q_ref[:, hs] for hs in group_heads], axis=0)
                do_rows = jnp.concatenate([do_ref[:, hs] for hs in group_heads], axis=0)
                dk = _dot(jnp.concatenate(ds_rows, axis=0), q_rows, "tn")
                dv = _dot(jnp.concatenate(p_rows, axis=0), do_rows, "tn")
                vv = slice(kw + g * HEAD_DIM, kw + (g + 1) * HEAD_DIM)
                dkv_ref[geo.s:geo.seg, kv] += dk[:geo.l]
                dkv_ref[geo.s:geo.seg, vv] += dv[:geo.l]
                if with_band:
                    dkv_ref[pl.ds(s0, BAND), kv] += dk[geo.l:]
                    dkv_ref[pl.ds(s0, BAND), vv] += dv[geo.l:]

        pl.when(j < n_x_blocks)(lambda: heads(True))
        pl.when(j >= n_x_blocks)(lambda: heads(False))

        @pl.when(j == n_blocks - 1)
        def _():
            dkv_out_ref[...] = dkv_ref[...].astype(BF16)

    res = pl.pallas_call(
        kern, name=name, grid=(geo.b, n_blocks),
        in_specs=[pl.BlockSpec(memory_space=pltpu.SMEM),
                  pl.BlockSpec((ATTN_BLOCK, qw), lambda b, j: (b * n_blocks + j, 0)),
                  pl.BlockSpec((geo.seg, kw), lambda b, j: (b, qw // kw)),
                  pl.BlockSpec((geo.seg, kw), lambda b, j: (b, qw // kw + 1)),
                  pl.BlockSpec((ATTN_BLOCK, qw), lambda b, j: (b * n_blocks + j, 0))] + carrier.in_specs(),
        out_specs=[pl.BlockSpec((ATTN_BLOCK, qw), lambda b, j: (b * n_blocks + j, 0)),
                   pl.BlockSpec((geo.seg, 2 * kw), lambda b, j: (b, 0)),
                   pl.BlockSpec((N_HEADS, LANES), lambda b, j: (0, 0))] + carrier.out_specs(),
        out_shape=[jax.ShapeDtypeStruct((geo.r, qw), BF16), jax.ShapeDtypeStruct((geo.r, 2 * kw), BF16),
                   jax.ShapeDtypeStruct((N_HEADS, LANES), F32)] + carrier.out_shapes(),
        scratch_shapes=[pltpu.VMEM((geo.seg, 2 * kw), F32)] + carrier.scratch(), input_output_aliases=carrier.aliases(),
        compiler_params=_cparams("arbitrary", "arbitrary"),
    )(sink, qkv, qkv, qkv, do, *carrier.operands())
    (dq, dkv, dsink), extra = carrier.results(res)
    return dq, dkv, dsink, extra


RET_QK_W = RET_HEADS * RET_QK_DIM
K_SCALE = RET_QK_DIM ** -0.5


def _ret_prep(geo, proj, cos, sin_signed, name):
    def body(i, p, cs, sn):
        cs2, sn2 = jnp.concatenate([cs] * RET_HEADS, axis=1), jnp.concatenate([sn] * RET_HEADS, axis=1)
        q = _rope(p[:, :RET_QK_W], cs2, sn2, RET_QK_DIM // 4)
        k = _rope(p[:, RET_QK_W:2 * RET_QK_W], cs2, sn2, RET_QK_DIM // 4) * K_SCALE
        return jnp.concatenate([q, k, p[:, 2 * RET_QK_W:]], axis=1)

    return _rowwise(name, body, geo, 256, [(proj, ("rowc", 2 * RET_QK_W + RET_VWIDTH, 0)), (cos, "tab"), (sin_signed, "tab")],
                    [("row", 2 * RET_QK_W + RET_VWIDTH, BF16)])


def _ret_prep_bwd(geo, dq, dk, dv, dgate, cos, sin_signed, name):
    def body(i, dqv, dkv, dvv, dg, cs, sn):
        cs2, sn2 = jnp.concatenate([cs] * RET_HEADS, axis=1), jnp.concatenate([sn] * RET_HEADS, axis=1)
        dkv = dkv * K_SCALE
        dqv = dqv * cs2 + _swap_halves(dqv * sn2, RET_QK_DIM // 4)
        dkv = dkv * cs2 + _swap_halves(dkv * sn2, RET_QK_DIM // 4)
        return jnp.concatenate([dqv, dkv, dvv, dg], axis=1)

    return _rowwise(name, body, geo, 256,
                    [(dq, "row"), (dk, "row"), (dv, "row"), (dgate, "row"), (cos, "tab"), (sin_signed, "tab")],
                    [("row", 2 * RET_QK_W + 2 * RET_VWIDTH, BF16)])


def _ret_step(state, q, k, v, lg, rev):
    c = RET_CHUNK
    ri = lax.broadcasted_iota(jnp.int32, (c, 1), 0).astype(F32)
    cj = lax.broadcasted_iota(jnp.int32, (1, c), 1).astype(F32)
    if rev:
        dist, q_decay, k_decay = cj - ri, jnp.exp(lg * (c - ri)), jnp.exp(lg * ri)
    else:
        dist, q_decay, k_decay = ri - cj, jnp.exp(lg * (ri + 1.0)), jnp.exp(lg * (c - 1.0 - ri))
    intra = jnp.where(dist >= 0, jnp.exp(lg * jnp.maximum(dist, 0.0)), 0.0)
    scores = _mm(q, k, "nt") * intra
    out = _mm(scores, v, "nn") + _mm(q, state, "nn") * q_decay
    new_state = state * jnp.exp(lg * c) + _mm(k * k_decay, v, "tn")
    return new_state, out


def _ret_state0(kc, vc, lg, rev):
    n = kc.shape[0]
    t = lax.broadcasted_iota(jnp.int32, (n, 1), 0).astype(F32)
    decay = jnp.exp(lg * t) if rev else jnp.exp(lg * (n - 1.0 - t))
    return _mm(kc * decay, vc, "tn")


def _ret_specs(geo):
    nq = RET_HEADS
    return [pl.BlockSpec((2 * RET_HEADS, LANES), lambda b, h: (0, 0)),
            pl.BlockSpec((geo.seg, RET_QK_DIM), lambda b, h: (b, h)),
            pl.BlockSpec((geo.seg, RET_QK_DIM), lambda b, h: (b, nq + h)),
            pl.BlockSpec((geo.seg, RET_V_DIM), lambda b, h: (b, nq + h))]


def _retention(geo, qkv, log_g, name):
    nc = geo.s // RET_CHUNK

    def kern(lg_ref, q_ref, k_ref, v_ref, o_ref, st_ref):
        h = pl.program_id(1)
        for d, rev in ((0, False), (1, True)):
            lg = lg_ref[pl.ds(d * RET_HEADS + h, 1), 0:1]
            st_ref[...] = _ret_state0(k_ref[geo.s:geo.seg, :].astype(F32), v_ref[geo.s:geo.seg, :].astype(F32), lg, rev)

            def chunk(ci, carry, d=d, rev=rev, lg=lg):
                r0 = pl.multiple_of((nc - 1 - ci if rev else ci) * RET_CHUNK, RET_CHUNK)
                rows = pl.ds(r0, RET_CHUNK)
                new_state, out = _ret_step(st_ref[...], q_ref[rows, :], k_ref[rows, :], v_ref[rows, :], lg, rev)
                st_ref[...] = new_state
                if d == 0:
                    o_ref[rows, :] = out
                else:
                    o_ref[rows, :] += out
                return carry

            lax.fori_loop(0, nc, chunk, 0)
        o_ref[geo.s:geo.seg, :] = jnp.zeros((geo.l, RET_V_DIM), F32)

    return pl.pallas_call(
        kern, name=name, grid=(geo.b, RET_HEADS), in_specs=_ret_specs(geo),
        out_specs=pl.BlockSpec((geo.seg, RET_V_DIM), lambda b, h: (b, h)),
        out_shape=jax.ShapeDtypeStruct((geo.r, RET_VWIDTH), F32),
        scratch_shapes=[pltpu.VMEM((RET_QK_DIM, RET_V_DIM), F32)],
        compiler_params=_cparams("parallel", "arbitrary"),
    )(log_g, qkv, qkv, qkv)


def _retention_bwd(geo, qkv, log_g, do, name):
    nc = geo.s // RET_CHUNK
    ctx = slice(geo.s, geo.seg)

    def kern(lg_ref, q_ref, k_ref, v_ref, do_ref, dq_ref, dk_ref, dv_ref, dlg_ref, states_ref, dst_ref, aq_ref, ak_ref, av_ref):
        b, h = pl.program_id(0), pl.program_id(1)

        @pl.when((b == 0) & (h == 0))
        def _():
            dlg_ref[...] = jnp.zeros_like(dlg_ref)

        for d, rev in ((0, False), (1, True)):
            row = pl.ds(d * RET_HEADS + h, 1)
            lg = lg_ref[row, 0:1]
            kc, vc = k_ref[ctx, :].astype(F32), v_ref[ctx, :].astype(F32)
            states_ref[0] = _ret_state0(kc, vc, lg, rev)

            def rows_of(ci, rev=rev):
                return pl.ds(pl.multiple_of((nc - 1 - ci if rev else ci) * RET_CHUNK, RET_CHUNK), RET_CHUNK)

            def load(rows):
                return q_ref[rows, :].astype(F32), k_ref[rows, :].astype(F32), v_ref[rows, :].astype(F32)

            def replay(ci, carry, rev=rev, lg=lg, rows_of=rows_of, load=load):
                states_ref[ci + 1] = _ret_step(states_ref[ci], *load(rows_of(ci)), lg, rev)[0]
                return carry

            lax.fori_loop(0, nc - 1, replay, 0)
            dst_ref[...] = jnp.zeros_like(dst_ref)

            def emit(rows, dq, dk, dv, d=d):
                if d == 0:
                    ak_ref[rows, :], av_ref[rows, :] = dk, dv
                    if dq is not None:
                        aq_ref[rows, :] = dq
                else:
                    dk_ref[rows, :] = (ak_ref[rows, :] + dk).astype(BF16)
                    dv_ref[rows, :] = (av_ref[rows, :] + dv).astype(BF16)
                    if dq is not None:
                        dq_ref[rows, :] = (aq_ref[rows, :] + dq).astype(BF16)

            def back(t, dlg, rev=rev, lg=lg, rows_of=rows_of, load=load, emit=emit):
                ci = nc - 1 - t
                rows = rows_of(ci)
                _, vjp = jax.vjp(lambda st, q, k, v, g: _ret_step(st, q, k, v, g, rev), states_ref[ci], *load(rows), lg)
                dstate, dq, dk, dv, dg = vjp((dst_ref[...], do_ref[rows, :].astype(F32)))
                dst_ref[...] = dstate
                emit(rows, dq, dk, dv)
                return dlg + dg

            dlg = lax.fori_loop(0, nc, back, jnp.zeros((1, 1), F32))
            _, vjp = jax.vjp(lambda kk, vv, g: _ret_state0(kk, vv, g, rev), kc, vc, lg)
            dkc, dvc, dg = vjp(dst_ref[...])
            emit(ctx, None, dkc, dvc)
            dlg_ref[row, :] += jnp.broadcast_to(dlg + dg, (1, LANES))
        dq_ref[ctx, :] = jnp.zeros((geo.l, RET_QK_DIM), BF16)

    nq = RET_HEADS
    return pl.pallas_call(
        kern, name=name, grid=(geo.b, RET_HEADS),
        in_specs=_ret_specs(geo) + [pl.BlockSpec((geo.seg, RET_V_DIM), lambda b, h: (b, h))],
        out_specs=[pl.BlockSpec((geo.seg, RET_QK_DIM), lambda b, h: (b, h)),
                   pl.BlockSpec((geo.seg, RET_QK_DIM), lambda b, h: (b, h)),
                   pl.BlockSpec((geo.seg, RET_V_DIM), lambda b, h: (b, h)),
                   pl.BlockSpec((2 * RET_HEADS, LANES), lambda b, h: (0, 0))],
        out_shape=[jax.ShapeDtypeStruct((geo.r, RET_QK_W), BF16), jax.ShapeDtypeStruct((geo.r, RET_QK_W), BF16),
                   jax.ShapeDtypeStruct((geo.r, RET_VWIDTH), BF16), jax.ShapeDtypeStruct((2 * RET_HEADS, LANES), F32)],
        scratch_shapes=[pltpu.VMEM((nc, RET_QK_DIM, RET_V_DIM), F32), pltpu.VMEM((RET_QK_DIM, RET_V_DIM), F32),
                        pltpu.VMEM((geo.seg, RET_QK_DIM), F32), pltpu.VMEM((geo.seg, RET_QK_DIM), F32),
                        pltpu.VMEM((geo.seg, RET_V_DIM), F32)],
        compiler_params=_cparams("arbitrary", "arbitrary"),
    )(log_g, qkv, qkv, qkv, do)


def _gated(o, g, gain):
    outs = []
    for h in range(RET_HEADS):
        cols = slice(h * RET_V_DIM, (h + 1) * RET_V_DIM)
        oh = o[:, cols]
        mu = jnp.mean(oh, axis=-1, keepdims=True)
        var = jnp.mean(jnp.square(oh - mu), axis=-1, keepdims=True)
        outs.append(_silu(g[:, cols]) * ((oh - mu) * lax.rsqrt(var + EPS) * gain[:, cols]))
    return jnp.concatenate(outs, axis=1)


def _ret_gated(geo, o, proj, gain, name):
    def body(i, ov, gv, gn):
        return _gated(ov, gv, gn)

    gate_block = (2 * RET_QK_W + RET_VWIDTH) // RET_VWIDTH
    return _rowwise(name, body, geo, 256, [(o, "row"), (proj, ("rowc", RET_VWIDTH, gate_block)), (gain, "full")],
                    [("row", RET_VWIDTH, BF16)])


def _ret_gated_bwd(geo, o, proj, gain, dout, name):
    def body(i, ov, gv, gn, dv):
        _, vjp = jax.vjp(_gated, ov, gv, gn)
        return vjp(dv)

    gate_block = (2 * RET_QK_W + RET_VWIDTH) // RET_VWIDTH
    return _rowwise(name, body, geo, 256,
                    [(o, "row"), (proj, ("rowc", RET_VWIDTH, gate_block)), (gain, "full"), (dout, "row")],
                    [("row", RET_VWIDTH, BF16), ("row", RET_VWIDTH, BF16), ("gacc", 1, RET_VWIDTH)])


def _whole(name, fn, out_shapes, *arrays):
    n = len(arrays)

    def kern(*refs):
        res = fn(*[r[...] for r in refs[:n]])
        for ref, val in zip(refs[n:], res):
            ref[...] = val.astype(ref.dtype)

    return pl.pallas_call(kern, name=name, out_shape=out_shapes)(*arrays)


def _rope_tables(geo, head_dim):
    rows = geo.s // GRID_W
    row = jnp.broadcast_to(jnp.arange(rows, dtype=jnp.int32)[:, None], (rows, GRID_W)).reshape(geo.s)
    col = jnp.broadcast_to(jnp.arange(GRID_W, dtype=jnp.int32)[None, :], (rows, GRID_W)).reshape(geo.s)
    axis_dim = head_dim // 2
    inv = ROPE_BASE ** (-jnp.arange(0, axis_dim, 2, dtype=F32) / axis_dim)
    ang_r = row.astype(F32)[:, None] * inv
    ang_c = col.astype(F32)[:, None] * inv
    cos = jnp.concatenate([jnp.cos(ang_r)] * 2 + [jnp.cos(ang_c)] * 2, axis=1)
    sin = jnp.concatenate([-jnp.sin(ang_r), jnp.sin(ang_r), -jnp.sin(ang_c), jnp.sin(ang_c)], axis=1)
    cos = jnp.concatenate([cos, jnp.ones((geo.l, head_dim), F32)], axis=0)
    sin = jnp.concatenate([sin, jnp.zeros((geo.l, head_dim), F32)], axis=0)
    reps = max(1, LANES // head_dim)
    return jnp.tile(cos, (1, reps)), jnp.tile(sin, (1, reps))


def _row_tile(r):
    return next(t for t in (1536, 1024, 512, 256, 128) if r % t == 0)


MOD_ROWS = 8


def _local_step(x, ctx, target, sp, wts, mods, plan=None):
    nb, s, d = x.shape
    geo = _Rows(nb, s, ctx.shape[1])
    assert nb + 1 <= MOD_ROWS and d == D_MODEL
    tm = _row_tile(geo.r)
    z = jnp.concatenate([x, ctx], axis=1).reshape(geo.r, d)
    cos64, sin64 = _rope_tables(geo, HEAD_DIM)
    cos256, sin256 = _rope_tables(geo, RET_QK_DIM)
    q_gain = jnp.tile(sp["q_norm"].reshape(1, HEAD_DIM), (1, LANES // HEAD_DIM))
    k_gain = jnp.tile(sp["k_norm"].reshape(1, HEAD_DIM), (1, LANES // HEAD_DIM))
    sink = sp["sink"].reshape(N_HEADS)
    log_g = jnp.broadcast_to(sp["log_g"].reshape(2 * RET_HEADS, 1), (2 * RET_HEADS, LANES))
    gn_g = sp["gn_g"].reshape(1, RET_VWIDTH)

    saved = []
    h1 = _norm_mod(geo, z, sp["norm1_g"][0][None, :], mods[0], 0, "norm1_0")
    for i in range(2):
        mod3 = mods[i]
        n1, n2 = sp["norm1_g"][i][None, :], sp["norm2_g"][i][None, :]
        if i == 0:
            proj = _mm_nn(h1, wts["attn_qkv"], F32, "attn_qkv", tm, wts["attn_qkv"].shape[1], d)
            prep = _attn_prep(geo, proj, cos64, sin64, q_gain, k_gain, "attn_prep")
            o, late = _attention(geo, prep, sink, "attn", plan.gather_job() if plan else None)
            if plan:
                plan.late_weights(late, wts)
            oraw = None
            w_o = wts["attn_o"]
        else:
            proj = _mm_nn(h1, wts["ret_qkvg"], BF16, "ret_qkvg", tm, wts["ret_qkvg"].shape[2], d)
            prep = _ret_prep(geo, proj, cos256, sin256, "ret_prep")
            oraw = _retention(geo, prep, log_g, "ret")
            o = _ret_gated(geo, oraw, proj, gn_g, "ret_gated")
            w_o = wts["ret_o"]
        zmid, mix, h2 = _mm_nn_gate_residual(geo, o, w_o, z, mod3, 2 * d, f"mix_out{i}", norm=(n2, mod3, 3 * d))
        u, a = _ffn_in_swiglu(h2, wts["ffn_in"][i], f"ffn_in{i}")
        next_norm = (sp["norm1_g"][1][None, :], mods[1], 0) if i == 0 else None
        zout, f, h1_next = _mm_nn_gate_residual(geo, a, wts["ffn_out"][i], zmid, mod3, 5 * d, f"ffn_out{i}", norm=next_norm)
        saved.append(dict(z=z, mod3=mod3, n1=n1, n2=n2, h1=h1, proj=proj, prep=prep, o=o, oraw=oraw, mix=mix, zmid=zmid,
                          h2=h2, u=u, a=a, f=f))
        z, h1 = zout, h1_next

    dz, loss, df, dg2 = _loss_head(geo, z, target.reshape(nb * s, d), saved[1]["f"], saved[1]["mod3"], 5 * d, "loss")

    big, small = {}, {}
    dmods = [None, None]
    for i in (1, 0):
        sv = saved[i]
        mod3 = sv["mod3"]
        carry = plan is not None and i == 0
        du, land = _ffn_out_dx_swiglu_bwd(df, wts["ffn_out"][i], sv["u"], f"ffn_out_dx{i}", plan.layer1.swap_job() if carry else None)
        if carry:
            plan.layer1.after_swap(land)
        big[f"ffn_out{i}"] = _mm_tn(sv["a"], df, f"ffn_out_dw{i}", D_FF // 2, 1024, tm, out_dtype=BF16).reshape(N_CHIPS, D_FF // N_CHIPS, d)
        n4 = wts["ffn_in"][i].shape[2]
        dh2 = _mm_nt(du, wts["ffn_in"][i], BF16, f"ffn_in_dx{i}", tm, 1024, n4)
        big[f"ffn_in{i}"] = _mm_tn(sv["h2"], du, f"ffn_in_dw{i}", 1024, n4, tm, shards=N_CHIPS, out_dtype=BF16)
        if carry:
            plan.start_layer0_ffn(big)
        dzmid, dsh2, dsc2, dn2, dmix, dg1, *land = _norm_mod_bwd(geo, sv["zmid"], sv["n2"], mod3, 3 * d, dh2, dz, f"norm2_bwd{i}",
                                                                 gated=(sv["mix"], mod3, 2 * d),
                                                                 job=plan.layer0_ffn.swap_job() if carry else None)
        if carry:
            plan.layer0_ffn.after_swap(land[0])
        if i == 0:
            do = _mm_nt(dmix, wts["attn_o"], BF16, "attn_out_dx", tm, 1024, 1024)
            big["attn_o"] = _mm_tn(sv["o"], dmix, "attn_out_dw", 1024, 1024, tm, out_dtype=BF16).reshape(N_CHIPS, 1024 // N_CHIPS, d)
            dq, dkv, dsink, land = _attention_bwd(geo, sv["prep"], sink, do, "attn_bwd", plan.exchange_job() if plan else None)
            if plan:
                plan.after_exchange(land)
            dproj, dqg, dkg = _attn_prep_bwd(geo, sv["proj"], cos64, sin64, q_gain, k_gain, dq, dkv, "attn_prep_bwd")
            small["q_norm"] = dqg[0, :HEAD_DIM] + dqg[0, HEAD_DIM:]
            small["k_norm"] = dkg[0, :HEAD_DIM] + dkg[0, HEAD_DIM:]
            small["sink"] = dsink[:, 0]
            wq = wts["attn_qkv"]
            dh1 = _mm_nt(dproj, wq, BF16, "attn_qkv_dx", tm, 1024, wq.shape[1])
            dwq = _mm_tn(sv["h1"], dproj, "attn_qkv_dw", 1024, wq.shape[1], tm, out_dtype=BF16)
            big["attn_qkv"] = dwq.reshape(d, N_CHIPS, -1).transpose(1, 0, 2)
        else:
            do = _mm_nt(dmix, wts["ret_o"], BF16, "ret_out_dx", tm, 1024, 1024)
            big["ret_o"] = _mm_tn(sv["o"], dmix, "ret_out_dw", 1024, 1024, tm, out_dtype=BF16).reshape(N_CHIPS, RET_VWIDTH // N_CHIPS, d)
            doraw, dgate, dgn = _ret_gated_bwd(geo, sv["oraw"], sv["proj"], gn_g, do, "ret_gated_bwd")
            small["gn_g"] = dgn[0]
            dq, dk, dv, dlg = _retention_bwd(geo, sv["prep"], log_g, doraw, "ret_bwd")
            small["log_g"] = dlg[:, 0].reshape(2, RET_HEADS)
            dproj = _ret_prep_bwd(geo, dq, dk, dv, dgate, cos256, sin256, "ret_prep_bwd")
            wq = wts["ret_qkvg"]
            dh1 = _mm_nt(dproj, wq, BF16, "ret_qkvg_dx", tm, 1024, wq.shape[2])
            big["ret_qkvg"] = _mm_tn(sv["h1"], dproj, "ret_qkvg_dw", 1024, wq.shape[2], tm, shards=N_CHIPS, out_dtype=BF16)
        below = (saved[0]["f"], saved[0]["mod3"], 5 * d) if i == 1 else None
        dz, dsh1, dsc1, dn1, *below_grads = _norm_mod_bwd(geo, sv["z"], sv["n1"], mod3, 0, dh1, dzmid, f"norm1_bwd{i}", gated=below,
                                                              latent_only=i == 0)
        small[f"norm1_g{i}"], small[f"norm2_g{i}"] = dn1[0], dn2[0]
        parts = [dsh1, dsc1, dg1, dsh2, dsc2, dg2]
        rows = jnp.concatenate([jnp.concatenate([p[:nb, 0, :] for p in parts], axis=1),
                                jnp.concatenate([jnp.sum(p[nb:, 0, :], axis=0, keepdims=True) for p in parts], axis=1),
                                jnp.zeros((MOD_ROWS - nb - 1, 6 * d), F32)], axis=0)
        dmods[i] = rows
        if below_grads:
            df, dg2 = below_grads
        small[f"ada_b{i}"] = jnp.sum(rows, axis=0)
        if plan and i == 1:
            plan.start_layer1(big)
    return loss, dz, big, small, dmods


def _adamw(w, g, m, v, name):
    rows, cols = w.shape
    tr = next((t for t in (256, 128, 64, 32, 16, 8) if rows % t == 0), rows)
    c1 = 1.0 - ADAM_B1 ** ADAM_STEP
    c2 = 1.0 - ADAM_B2 ** ADAM_STEP

    def kern(w_ref, g_ref, m_ref, v_ref, d_ref, nm_ref, nv_ref):
        gv = g_ref[...]
        nm = ADAM_B1 * m_ref[...] + (1.0 - ADAM_B1) * gv
        nv = ADAM_B2 * v_ref[...] + (1.0 - ADAM_B2) * jnp.square(gv)
        d_ref[...] = -ADAM_LR * ((nm / c1) / (jnp.sqrt(nv / c2) + ADAM_EPS) + ADAM_WD * w_ref[...])
        nm_ref[...] = nm
        nv_ref[...] = nv

    spec = pl.BlockSpec((tr, cols), lambda i: (i, 0))
    return pl.pallas_call(
        kern, name=name, grid=(rows // tr,), in_specs=[spec] * 4, out_specs=[spec] * 3,
        out_shape=[jax.ShapeDtypeStruct(w.shape, F32)] * 3, compiler_params=_cparams("parallel"),
    )(w, g, m, v)


N_DEVICES = 8


def _mesh_pos():
    return lax.axis_index("x"), lax.axis_index("y"), lax.axis_index("c")


def _other_chips(x, y):
    return [(1 - x, y), (x, 1 - y), (1 - x, 1 - y)]


def _hbm(n):
    return [pl.BlockSpec(memory_space=pl.ANY)] * n


def _remote(src, dst, send_sem, recv_sem, device):
    return pltpu.make_async_remote_copy(src_ref=src, dst_ref=dst, send_sem=send_sem, recv_sem=recv_sem,
                                        device_id=device, device_id_type=MESH)


def _scalar_spec(grid, in_specs, out_specs):
    return pltpu.PrefetchScalarGridSpec(num_scalar_prefetch=1, grid=grid, in_specs=in_specs, out_specs=out_specs)


def _place_shard(param, layer, pos, name):
    _, r, cols = param.shape
    tr = _slab_tile(r)

    def kern(pos_ref, s_ref, o_ref):
        o_ref[...] = s_ref[...].astype(BF16)

    return pl.pallas_call(
        kern, name=name, out_shape=jax.ShapeDtypeStruct((N_CHIPS, r, cols), BF16),
        grid_spec=_scalar_spec((r // tr,), [pl.BlockSpec((None, tr, cols), lambda i, p: (layer, i, 0))],
                               pl.BlockSpec((None, tr, cols), lambda i, p: (p[1], i, 0))),
        compiler_params=_cparams("parallel"),
    )(pos, param)


class _CommJob:
    def __init__(self, inputs, out_shapes, aliases, sem_shapes, stages, fractions=None):
        self.inputs, self.out_shapes, self.aliases, self.sem_shapes, self.stages = inputs, out_shapes, aliases, sem_shapes, stages
        self.fractions = fractions


def _merge_jobs(a, b):
    assert len(a.stages) == len(b.stages)
    ni, no, ns = len(a.inputs), len(a.out_shapes), len(a.sem_shapes)

    def both(sa, sb):
        def stage(ins, outs, sems):
            sa(ins[:ni], outs[:no], sems[:ns])
            sb(ins[ni:], outs[no:], sems[ns:])
        return stage

    aliases = dict(a.aliases)
    aliases.update({ni + i: no + o for i, o in b.aliases.items()})
    return _CommJob(a.inputs + b.inputs, a.out_shapes + b.out_shapes, aliases, a.sem_shapes + b.sem_shapes,
                    [both(sa, sb) for sa, sb in zip(a.stages, b.stages)])


def _run_job(job, name):
    n_in, n_out = len(job.inputs), len(job.out_shapes)

    def body(*refs):
        for stage in job.stages:
            stage(refs[:n_in], refs[n_in:n_in + n_out], refs[n_in + n_out:])

    return pl.pallas_call(
        body, name=name, in_specs=_hbm(n_in), out_specs=_hbm(n_out), out_shape=job.out_shapes,
        input_output_aliases=job.aliases, scratch_shapes=job.sem_shapes,
    )(*job.inputs)


def _job_marks(job, steps):
    mid = len(job.stages) - 2
    fractions = job.fractions or [(s + 1) / (mid + 1) for s in range(mid)]
    return [0] + [min(steps - 1, 1 + int((steps - 1) * f)) for f in fractions] + [steps - 1]


def _gather_job(placed):
    n = len(placed)

    def half(w, which):
        r2 = placed[w].shape[1] // 2
        return pl.ds(which * r2, r2)

    def ici_copies(outs, sems, slot_of, arrays=range(n)):
        x, y, c = _mesh_pos()
        res = []
        for w in arrays:
            for k, (px, py) in enumerate(_other_chips(x, y)):
                slab = outs[w].at[slot_of(x, y, px, py), half(w, c)]
                res.append((slab, _remote(slab, slab, sems[0].at[w, k], sems[1].at[w, k], (px, py, c))))
        return res

    def forwards(outs, sems, which_core, arrays=range(n)):
        x, y, c = _mesh_pos()
        res = []
        for w in arrays:
            for k, (px, py) in enumerate(_other_chips(x, y)):
                slab = outs[w].at[2 * px + py, half(w, which_core(c))]
                res.append(_remote(slab, slab, sems[2].at[w, k], sems[3].at[w, k], (x, y, 1 - c)))
        return res

    def send(ins, outs, sems):
        for _, cp in ici_copies(outs, sems, lambda x, y, px, py: 2 * x + y):
            cp.start()

    def forward_of(w):
        def forward(ins, outs, sems):
            arrivals = ici_copies(outs, sems, lambda x, y, px, py: 2 * px + py, [w])
            for (_, arrival), fwd in zip(arrivals, forwards(outs, sems, lambda c: c, [w])):
                arrival.wait_recv()
                fwd.start()
        return forward

    def finish(ins, outs, sems):
        for cp in forwards(outs, sems, lambda c: 1 - c):
            cp.wait_recv()
        for _, cp in ici_copies(outs, sems, lambda x, y, px, py: 2 * x + y):
            cp.wait_send()
        for cp in forwards(outs, sems, lambda c: c):
            cp.wait_send()

    sizes = [p.shape[1] * p.shape[2] for p in placed]
    fractions = [sum(sizes[:w + 1]) / sum(sizes) for w in range(n)]
    return _CommJob(list(placed), [jax.ShapeDtypeStruct(p.shape, p.dtype) for p in placed], {w: w for w in range(n)},
                    [pltpu.SemaphoreType.DMA((n, 3))] * 4, [send] + [forward_of(w) for w in range(n)] + [finish], fractions)


def _pair_swap_job(grads):
    n = len(grads)

    def copies(ins, outs, sems):
        x, y, c = _mesh_pos()
        res = []
        for w in range(n):
            r2 = grads[w].shape[1] // 2
            res.append(_remote(ins[w].at[:, pl.ds((1 - c) * r2, r2)], outs[w], sems[0].at[w], sems[1].at[w], (x, y, 1 - c)))
        return res

    def send(ins, outs, sems):
        for cp in copies(ins, outs, sems):
            cp.start()

    def finish(ins, outs, sems):
        for cp in copies(ins, outs, sems):
            cp.wait()

    return _CommJob(list(grads), [jax.ShapeDtypeStruct((N_CHIPS, g.shape[1] // 2, g.shape[2]), g.dtype) for g in grads], {},
                    [pltpu.SemaphoreType.DMA((n,))] * 2, [send, finish])


def _chip_exchange_job(hs):
    n = len(hs)

    def send(ins, outs, sems):
        x, y, c = _mesh_pos()
        for w in range(n):
            for k, (px, py) in enumerate(_other_chips(x, y)):
                _remote(ins[w].at[2 * px + py], outs[w].at[2 * x + y], sems[0].at[w, k], sems[1].at[w, k], (px, py, c)).start()

    def finish(ins, outs, sems):
        x, y, c = _mesh_pos()
        for w in range(n):
            for k, (px, py) in enumerate(_other_chips(x, y)):
                got = outs[w].at[2 * px + py]
                cp = _remote(ins[w].at[2 * px + py], got, sems[0].at[w, k], sems[1].at[w, k], (px, py, c))
                cp.wait_recv()
                cp.wait_send()

    return _CommJob(list(hs), [jax.ShapeDtypeStruct(h.shape, h.dtype) for h in hs], {},
                    [pltpu.SemaphoreType.DMA((n, 3))] * 2, [send, finish])


def _pair_share(ts, name):
    n = len(ts)

    def body(*refs):
        outs = refs[n:2 * n]
        send_sems, recv_sems = refs[2 * n:]
        x, y, c = _mesh_pos()
        sends = []
        for w in range(n):
            r2 = ts[w].shape[0] // 2
            mine = outs[w].at[pl.ds(c * r2, r2)]
            rc = _remote(mine, mine, send_sems.at[w], recv_sems.at[w], (x, y, 1 - c))
            rc.start()
            sends.append(rc)
        for w in range(n):
            r2 = ts[w].shape[0] // 2
            theirs = outs[w].at[pl.ds((1 - c) * r2, r2)]
            _remote(theirs, theirs, send_sems.at[w], recv_sems.at[w], (x, y, 1 - c)).wait_recv()
            sends[w].wait_send()

    return pl.pallas_call(
        body, name=name, in_specs=_hbm(n), out_specs=_hbm(n),
        out_shape=[jax.ShapeDtypeStruct(t.shape, F32) for t in ts],
        input_output_aliases={w: w for w in range(n)},
        scratch_shapes=[pltpu.SemaphoreType.DMA((n,))] * 2,
    )(*ts)


def _slab_tile(rows):
    return next(t for t in (512, 256, 176, 128, 64, 32, 16) if rows % t == 0)


def _sum_pair(grad, land, pos, name):
    _, r2, cols = land.shape
    tr = _slab_tile(r2)
    nt = r2 // tr

    def kern(pos_ref, a_ref, b_ref, o_ref):
        o_ref[...] = (a_ref[...].astype(F32) + b_ref[...].astype(F32)).astype(BF16)

    spec = pl.BlockSpec((None, tr, cols), lambda j, i, p: (j, i, 0))
    return pl.pallas_call(
        kern, name=name, out_shape=jax.ShapeDtypeStruct(land.shape, BF16),
        grid_spec=_scalar_spec((N_CHIPS, nt), [pl.BlockSpec((None, tr, cols), lambda j, i, p: (j, p[0] * nt + i, 0)), spec], spec),
        compiler_params=_cparams("parallel", "parallel"),
    )(pos, grad, land)


def _sum_chips(hs, land, pos, name):
    _, r2, cols = land.shape
    tr = _slab_tile(r2)
    nt = r2 // tr

    def kern(pos_ref, h_ref, l_ref, o_ref):
        acc = jnp.zeros((tr, cols), F32)
        own = h_ref[...].astype(F32)
        for k in range(N_CHIPS):
            acc = acc + jnp.where(pos_ref[1] == k, own, l_ref[k].astype(F32))
        o_ref[...] = acc

    return pl.pallas_call(
        kern, name=name, out_shape=jax.ShapeDtypeStruct((2 * r2, cols), F32),
        grid_spec=_scalar_spec((nt,), [pl.BlockSpec((None, tr, cols), lambda i, p: (p[1], i, 0)),
                                       pl.BlockSpec((N_CHIPS, tr, cols), lambda i, p: (0, i, 0))],
                               pl.BlockSpec((tr, cols), lambda i, p: (p[0] * nt + i, 0))),
        compiler_params=_cparams("parallel"),
    )(pos, hs, land)


class _ReduceScatter:
    def __init__(self, grads, pos, tag):
        self.grads, self.pos, self.tag = list(grads), pos, tag

    def swap_job(self):
        return _pair_swap_job(self.grads)

    def after_swap(self, land):
        self.hs = [_sum_pair(g, l, self.pos, f"grads_pair_sum_{self.tag}{w}") for w, (g, l) in enumerate(zip(self.grads, land))]

    def exchange_job(self):
        return _chip_exchange_job(self.hs)

    def after_exchange(self, land2):
        return [_sum_chips(h, l, self.pos, f"grads_chip_sum_{self.tag}{w}") for w, (h, l) in enumerate(zip(self.hs, land2))]

    def run(self):
        self.after_swap(_run_job(self.swap_job(), f"grads_pair_swap_{self.tag}"))
        return self.after_exchange(_run_job(self.exchange_job(), f"grads_chip_exchange_{self.tag}"))


EARLY_WEIGHTS = ("attn_qkv",)
LATE_WEIGHTS = ("ffn_in0", "ffn_in1", "ffn_out0", "ffn_out1", "attn_o", "ret_qkvg", "ret_o")
LAYER1_GRADS = ("ffn_out1", "ffn_in1", "ret_o", "ret_qkvg")
LAYER0_FFN_GRADS = ("ffn_out0", "ffn_in0")
LAST_GRADS = ("attn_o", "attn_qkv")


def _fill_weights(wts, full):
    for name, w in full.items():
        if name[:-1] == "ffn_in":
            wts[name[:-1]][int(name[-1])] = w
        elif name[:-1] == "ffn_out":
            wts["ffn_out"][int(name[-1])] = w.reshape(-1, w.shape[2])
        elif name in ("attn_o", "ret_o"):
            wts[name] = w.reshape(-1, w.shape[2])
        elif name == "attn_qkv":
            wts[name] = w.transpose(1, 0, 2).reshape(w.shape[1], -1)
        else:
            wts[name] = w


class _StepPlan:
    def __init__(self, placed, pos):
        self.placed, self.pos = placed, pos
        self.layer1 = self.layer0_ffn = None
        self.reduced = {}

    def gather_job(self):
        return _gather_job([self.placed[k] for k in LATE_WEIGHTS])

    def late_weights(self, outs, wts):
        _fill_weights(wts, dict(zip(LATE_WEIGHTS, outs)))

    def start_layer1(self, big):
        self.layer1 = _ReduceScatter([big[k] for k in LAYER1_GRADS], self.pos, "l1_")

    def start_layer0_ffn(self, big):
        self.layer0_ffn = _ReduceScatter([big[k] for k in LAYER0_FFN_GRADS], self.pos, "l0f_")

    def exchange_job(self):
        return _merge_jobs(self.layer1.exchange_job(), self.layer0_ffn.exchange_job())

    def after_exchange(self, land):
        n1 = len(LAYER1_GRADS)
        self.reduced.update(zip(LAYER1_GRADS, self.layer1.after_exchange(land[:n1])))
        self.reduced.update(zip(LAYER0_FFN_GRADS, self.layer0_ffn.after_exchange(land[n1:])))


def _all_reduce_small(v, name):
    def body(v_ref, o_ref, land_ref, send_sems, recv_sems):
        x, y, c = _mesh_pos()
        me = 4 * x + 2 * y + c
        land_ref[me] = v_ref[...]
        for t in range(N_DEVICES):
            @pl.when(t != me)
            def _(t=t):
                _remote(v_ref, land_ref.at[me], send_sems.at[t], recv_sems.at[me], (t // 4, (t // 2) % 2, t % 2)).start()
        for t in range(N_DEVICES):
            @pl.when(t != me)
            def _(t=t):
                _remote(v_ref, land_ref.at[t], send_sems.at[t], recv_sems.at[t], (t // 4, (t // 2) % 2, t % 2)).wait()
        acc = land_ref[0]
        for t in range(1, N_DEVICES):
            acc = acc + land_ref[t]
        o_ref[...] = acc

    vmem = pl.BlockSpec(memory_space=pltpu.VMEM)
    return pl.pallas_call(
        body, name=name, in_specs=[vmem], out_specs=vmem, out_shape=jax.ShapeDtypeStruct(v.shape, F32),
        scratch_shapes=[pltpu.VMEM((N_DEVICES,) + v.shape, F32), pltpu.SemaphoreType.DMA((N_DEVICES,)),
                        pltpu.SemaphoreType.DMA((N_DEVICES,))],
    )(v)


def _all_to_all_small(v, name):
    def body(v_ref, o_ref, send_sems, recv_sems):
        x, y, c = _mesh_pos()
        me = 4 * x + 2 * y + c
        o_ref[me] = v_ref[me]
        for t in range(N_DEVICES):
            @pl.when(t != me)
            def _(t=t):
                _remote(v_ref.at[t], o_ref.at[me], send_sems.at[t], recv_sems.at[me], (t // 4, (t // 2) % 2, t % 2)).start()
        for t in range(N_DEVICES):
            @pl.when(t != me)
            def _(t=t):
                _remote(v_ref.at[t], o_ref.at[t], send_sems.at[t], recv_sems.at[t], (t // 4, (t // 2) % 2, t % 2)).wait()

    vmem = pl.BlockSpec(memory_space=pltpu.VMEM)
    return pl.pallas_call(
        body, name=name, in_specs=[vmem], out_specs=vmem, out_shape=jax.ShapeDtypeStruct(v.shape, F32),
        scratch_shapes=[pltpu.SemaphoreType.DMA((N_DEVICES,)), pltpu.SemaphoreType.DMA((N_DEVICES,))],
    )(v)


ALL_ROWS = 40


class _AdaLN:
    def __init__(self, c, c_ctx, ada_w, ada_b, riders):
        xi, yi, ci = _mesh_pos()
        self.me, self.chip, self.core = 4 * xi + 2 * yi + ci, 2 * xi + yi, ci
        self.nb, d = c.shape
        self.ada_w, self.c_ctx = ada_w, c_ctx
        self.cols = ada_w.shape[2]
        ctx_row = self.nb * N_DEVICES
        assert ctx_row + 1 + riders.shape[0] <= ALL_ROWS
        placed = lax.dynamic_update_slice(jnp.zeros((ALL_ROWS, d), F32), c, (self.me * self.nb, 0))
        placed = lax.dynamic_update_slice(placed, riders, (ctx_row + 1, 0))
        summed = _all_reduce_small(placed, "gather_conditioning")
        self.riders = summed[ctx_row + 1:ctx_row + 1 + riders.shape[0]]
        c_all = summed.at[ctx_row].set(c_ctx)
        self.cact, = _whole("cond_silu", lambda v: (_silu(v),), [jax.ShapeDtypeStruct(c_all.shape, F32)], c_all)
        parts = []
        for i in range(2):
            bias = lax.dynamic_slice(ada_b[i], (self.chip * self.cols,), (self.cols,))[None, :]
            parts.append(_mm_nn(self.cact, ada_w[i], F32, f"mod{i}", ALL_ROWS, self.cols, d, bias=bias))
        part = jnp.concatenate(parts, axis=1)
        rows = [[t * self.nb + b for b in range(self.nb)] + [ctx_row] * (MOD_ROWS - self.nb) for t in range(N_DEVICES)]
        got = _all_to_all_small(part[jnp.asarray(rows)], "mod_exchange")
        self.mods = [jnp.concatenate([got[2 * j][:self.nb + 1, i * self.cols:(i + 1) * self.cols] for j in range(N_CHIPS)], axis=1)[:, None, :]
                     for i in range(2)]

    def backward(self, dmods):
        nb, cols, d = self.nb, self.cols, self.ada_w.shape[1]
        blocks = [jnp.concatenate([dm[:, j * cols:(j + 1) * cols] for dm in dmods], axis=1) for j in range(N_CHIPS)]
        got = _all_to_all_small(jnp.stack([blocks[t // 2] for t in range(N_DEVICES)]), "dmod_exchange")
        dall = jnp.concatenate([got[:, :nb].reshape(N_DEVICES * nb, 2 * cols), jnp.sum(got[:, nb], axis=0, keepdims=True),
                                jnp.zeros((ALL_ROWS - N_DEVICES * nb - 1, 2 * cols), F32)], axis=0)
        dctx = jnp.concatenate([dall[N_DEVICES * nb][None, :], jnp.zeros((MOD_ROWS - 1, 2 * cols), F32)], axis=0)
        grads, dcact = [], []
        for i in range(2):
            grads.append(_mm_tn(self.cact, dall[:, i * cols:(i + 1) * cols], f"ada_dw{i}", d, cols, ALL_ROWS))
            dcact.append(_mm_nt(dctx[:, i * cols:(i + 1) * cols], self.ada_w[i], F32, f"ada_dx{i}", MOD_ROWS, d, cols))

        def silu_bwd(v, d0, d1):
            sg = _sigmoid(v)
            return ((d0 + d1)[0:1] * (sg * (1.0 + v * (1.0 - sg))),)

        dc_ctx, = _whole("cond_silu_bwd", silu_bwd, [jax.ShapeDtypeStruct((1, d), F32)], self.c_ctx[None, :], dcact[0], dcact[1])
        return grads, jnp.where(self.core == 0, dc_ctx[0], jnp.zeros((d,), F32))


SMALL_ROWS = 24


def _pack_small(small, dlogit):
    d = D_MODEL
    misc = jnp.zeros((d,), F32)
    misc = misc.at[0:HEAD_DIM].set(small["q_norm"]).at[128:128 + HEAD_DIM].set(small["k_norm"])
    misc = misc.at[256:256 + N_HEADS].set(small["sink"]).at[384:384 + 2 * RET_HEADS].set(dlogit.reshape(-1))
    rows = [small["ada_b0"].reshape(6, d), small["ada_b1"].reshape(6, d), small["norm1_g0"][None], small["norm1_g1"][None],
            small["norm2_g0"][None], small["norm2_g1"][None], small["c_ctx"][None], small["gn_g"].reshape(2, d), misc[None]]
    buf = jnp.concatenate(rows, axis=0)
    return jnp.concatenate([buf, jnp.zeros((SMALL_ROWS - buf.shape[0], d), F32)], axis=0)


def _unpack_small(buf):
    d = D_MODEL
    misc = buf[19]
    return dict(ada_b=buf[0:12].reshape(2, 6 * d), norm1_g=buf[12:14], norm2_g=buf[14:16], c_ctx=buf[16],
                gn_g=buf[17:19].reshape(2 * d), q_norm=misc[0:HEAD_DIM], k_norm=misc[128:128 + HEAD_DIM],
                sink=misc[256:256 + N_HEADS], decay=misc[384:384 + 2 * RET_HEADS])


def kernel(x, c, ctx, c_ctx, ada_w, ada_b, norm1_g, norm2_g, ffn_w_in, ffn_w_out, attn_w_qkv, attn_q_norm, attn_k_norm, attn_sink, attn_w_o, ret_w_qkvg, ret_decay_logit, ret_gn_g, ret_w_o, loss_target, m_c_ctx, m_ada_w, m_ada_b, m_norm1_g, m_norm2_g, m_ffn_w_in, m_ffn_w_out, m_attn_w_qkv, m_attn_q_norm, m_attn_k_norm, m_attn_sink, m_attn_w_o, m_ret_w_qkvg, m_ret_decay_logit, m_ret_gn_g, m_ret_w_o, v_c_ctx, v_ada_w, v_ada_b, v_norm1_g, v_norm2_g, v_ffn_w_in, v_ffn_w_out, v_attn_w_qkv, v_attn_q_norm, v_attn_k_norm, v_attn_sink, v_attn_w_o, v_ret_w_qkvg, v_ret_decay_logit, v_ret_gn_g, v_ret_w_o):
    xi, yi, ci = _mesh_pos()
    chip = 2 * xi + yi
    nb, s, d = x.shape
    gn_shard = ret_gn_g.shape[1]

    shards = dict(ffn_in0=(ffn_w_in, 0), ffn_in1=(ffn_w_in, 1), ffn_out0=(ffn_w_out, 0),
                  ffn_out1=(ffn_w_out, 1), attn_qkv=(attn_w_qkv, 0), attn_o=(attn_w_o, 0), ret_qkvg=(ret_w_qkvg, 0), ret_o=(ret_w_o, 0))
    names = list(shards)
    pos = jnp.stack([ci, chip]).astype(jnp.int32)
    placed = {k: _place_shard(*shards[k], pos, f"place_{k}") for k in names}
    early = _run_job(_gather_job([placed[k] for k in EARLY_WEIGHTS]), "gather_early_weights")
    gn_mine = jnp.where(ci == 0, ret_gn_g[0], jnp.zeros_like(ret_gn_g[0]))
    gn_place = lax.dynamic_update_slice(jnp.zeros((RET_VWIDTH,), F32), gn_mine, (chip * gn_shard,))

    wts = dict(ffn_in=[None, None], ffn_out=[None, None], attn_qkv=None, attn_o=None, ret_qkvg=None, ret_o=None)
    ada = _AdaLN(c, c_ctx, ada_w, ada_b, riders=gn_place.reshape(2, d))
    gn_full = ada.riders.reshape(RET_VWIDTH)
    _fill_weights(wts, dict(zip(EARLY_WEIGHTS, early)))
    plan = _StepPlan(placed, pos)
    decay_logit = ret_decay_logit[0]
    sp = dict(norm1_g=norm1_g, norm2_g=norm2_g, q_norm=attn_q_norm[0], k_norm=attn_k_norm[0],
              sink=attn_sink[0], log_g=jax.nn.log_sigmoid(decay_logit), gn_g=gn_full)
    loss_part, dz, big, small, dmods = _local_step(x, ctx, loss_target, sp, wts, ada.mods, plan)
    ada_grads, small["c_ctx"] = ada.backward(dmods)

    loss = lax.psum(loss_part[0, 0], ("x", "y", "c"))
    grad_x = dz.reshape(nb, s, d)

    dlogit = small["log_g"] * jax.nn.sigmoid(-decay_logit)
    sg = _unpack_small(_all_reduce_small(_pack_small(small, dlogit), "reduce_small_grads"))
    halves = dict(plan.reduced)
    halves.update(zip(LAST_GRADS, _ReduceScatter([big[k] for k in LAST_GRADS], pos, "last_").run()))
    reduced = dict(zip(halves, _pair_share(list(halves.values()), "grads_pair_share")))

    grads = dict(
        c_ctx=sg["c_ctx"], ada_w=jnp.stack(ada_grads), ada_b=sg["ada_b"], norm1_g=sg["norm1_g"],
        norm2_g=sg["norm2_g"], ffn_w_in=jnp.stack([reduced["ffn_in0"], reduced["ffn_in1"]]),
        ffn_w_out=jnp.stack([reduced["ffn_out0"], reduced["ffn_out1"]]), attn_w_qkv=reduced["attn_qkv"][None],
        attn_q_norm=sg["q_norm"][None], attn_k_norm=sg["k_norm"][None], attn_sink=sg["sink"][None],
        attn_w_o=reduced["attn_o"][None], ret_w_qkvg=reduced["ret_qkvg"][None], ret_decay_logit=sg["decay"].reshape(1, 2, RET_HEADS),
        ret_gn_g=lax.dynamic_slice(sg["gn_g"], (chip * gn_shard,), (gn_shard,))[None], ret_w_o=reduced["ret_o"][None])
    params = dict(c_ctx=(c_ctx, m_c_ctx, v_c_ctx), ada_w=(ada_w, m_ada_w, v_ada_w), ada_b=(ada_b, m_ada_b, v_ada_b),
                  norm1_g=(norm1_g, m_norm1_g, v_norm1_g), norm2_g=(norm2_g, m_norm2_g, v_norm2_g),
                  ffn_w_in=(ffn_w_in, m_ffn_w_in, v_ffn_w_in), ffn_w_out=(ffn_w_out, m_ffn_w_out, v_ffn_w_out),
                  attn_w_qkv=(attn_w_qkv, m_attn_w_qkv, v_attn_w_qkv), attn_q_norm=(attn_q_norm, m_attn_q_norm, v_attn_q_norm),
                  attn_k_norm=(attn_k_norm, m_attn_k_norm, v_attn_k_norm), attn_sink=(attn_sink, m_attn_sink, v_attn_sink),
                  attn_w_o=(attn_w_o, m_attn_w_o, v_attn_w_o), ret_w_qkvg=(ret_w_qkvg, m_ret_w_qkvg, v_ret_w_qkvg),
                  ret_decay_logit=(ret_decay_logit, m_ret_decay_logit, v_ret_decay_logit),
                  ret_gn_g=(ret_gn_g, m_ret_gn_g, v_ret_gn_g), ret_w_o=(ret_w_o, m_ret_w_o, v_ret_w_o))
    order = list(params)
    deltas, new_m, new_v = [], [], []
    for k in order:
        w, m, v = params[k]
        g = grads[k].reshape(w.shape)
        grads[k] = g
        flat = (-1, w.shape[-1]) if w.ndim > 1 else (1, -1)
        if k == "ret_decay_logit":
            flat = (1, -1)
        dw, nm, nv = _adamw(w.reshape(flat), g.reshape(flat), m.reshape(flat), v.reshape(flat), f"adamw_{k}")
        deltas.append(dw.reshape(w.shape))
        new_m.append(nm.reshape(w.shape))
        new_v.append(nv.reshape(w.shape))
    return (loss, grad_x, *[grads[k] for k in order], *deltas, *new_m, *new_v)
```

```python
import functools

import jax
import jax.numpy as jnp
from jax import lax
from jax.experimental import pallas as pl
from jax.experimental.pallas import tpu as pltpu

F32 = jnp.float32
BF16 = jnp.bfloat16

D_MODEL = 1024
N_HEADS = 16
N_KV_HEADS = 4
HEAD_DIM = 64
WINDOW = 128
ATTN_BLOCK = 128
BAND = ATTN_BLOCK + 2 * WINDOW
RET_HEADS = 4
RET_QK_DIM = 256
RET_V_DIM = 512
RET_VWIDTH = 2048
RET_CHUNK = 128
D_FF = 2816
GRID_W = 64
ROPE_BASE = 10000.0
EPS = 1e-6
NEG_INF = -1e30
LANES = 128

ADAM_LR = 0.001
ADAM_B1 = 0.9
ADAM_B2 = 0.999
ADAM_EPS = 1e-08
ADAM_WD = 0.01
ADAM_STEP = 10

VMEM_LIMIT_BYTES = 56 * 1024 * 1024
MESH = pl.DeviceIdType.MESH
N_CHIPS = 4


def _cparams(*sem):
    return pltpu.CompilerParams(dimension_semantics=sem, vmem_limit_bytes=VMEM_LIMIT_BYTES)


_DIMS = {"nn": ((1,), (0,)), "nt": ((1,), (1,)), "tn": ((0,), (0,))}


def _dot(a, b, form):
    return lax.dot_general(a.astype(BF16), b.astype(BF16), (_DIMS[form], ((), ())), preferred_element_type=F32)


@functools.partial(jax.custom_vjp, nondiff_argnums=(2,))
def _mm(a, b, form):
    return _dot(a, b, form)


def _mm_fwd(a, b, form):
    return _dot(a, b, form), (a, b)


def _mm_bwd(form, res, ct):
    a, b = res
    if form == "nn":
        da, db = _dot(ct, b, "nt"), _dot(a, ct, "tn")
    elif form == "nt":
        da, db = _dot(ct, b, "nn"), _dot(ct, a, "tn")
    else:
        da, db = _dot(b, ct, "nt"), _dot(a, ct, "nn")
    return da.astype(a.dtype), db.astype(b.dtype)


_mm.defvjp(_mm_fwd, _mm_bwd)


def _swap_halves(x, half):
    w = x.shape[-1]
    lane = lax.broadcasted_iota(jnp.int32, x.shape, x.ndim - 1)
    return jnp.where(lane % (2 * half) < half, pltpu.roll(x, w - half, x.ndim - 1), pltpu.roll(x, half, x.ndim - 1))


@functools.partial(jax.custom_vjp, nondiff_argnums=(1,))
def _rot(x, half):
    return _swap_halves(x, half)


def _rot_fwd(x, half):
    return _swap_halves(x, half), None


def _rot_bwd(half, _, ct):
    return (_swap_halves(ct, half),)


_rot.defvjp(_rot_fwd, _rot_bwd)


def _rope(x, cos, sin_signed, half):
    return x * cos + _rot(x, half) * sin_signed


def _head_mean_square(x):
    r = lax.broadcasted_iota(jnp.int32, (LANES, LANES), 0) // HEAD_DIM
    c = lax.broadcasted_iota(jnp.int32, (LANES, LANES), 1) // HEAD_DIM
    g = jnp.where(r == c, 1.0 / HEAD_DIM, 0.0).astype(F32)
    return jnp.dot(x * x, g, precision=lax.Precision.HIGHEST, preferred_element_type=F32)


def _qk_chunk(x, gain, cos, sin_signed, scale):
    y = x * lax.rsqrt(_head_mean_square(x) + EPS) * gain
    return _rope(y, cos, sin_signed, HEAD_DIM // 4) * scale


def _sigmoid(x):
    return 1.0 / (1.0 + jnp.exp(-x))


def _silu(x):
    return x * _sigmoid(x)


def _mm_nn(a, w, out_dtype, name, tm, tn, tk, bias=None):
    m, k_dim = a.shape
    if w.ndim == 3:
        n = w.shape[0] * w.shape[2]
        per = w.shape[2] // tn
        assert w.shape[2] % tn == 0
        w_spec = pl.BlockSpec((None, tk, tn), lambda i, j, k: (j // per, k, j % per))
    else:
        n = w.shape[1]
        w_spec = pl.BlockSpec((tk, tn), lambda i, j, k: (k, j))
    assert m % tm == 0 and n % tn == 0 and k_dim % tk == 0, (name, a.shape, w.shape, tm, tn, tk)
    nk = k_dim // tk
    has_bias = bias is not None

    def body(*refs):
        a_ref, w_ref = refs[0], refs[1]
        b_ref = refs[2] if has_bias else None
        o_ref, acc_ref = (refs[-1], None) if nk == 1 else (refs[-2], refs[-1])
        if nk == 1:
            part = jnp.dot(a_ref[...].astype(BF16), w_ref[...].astype(BF16), preferred_element_type=F32)
            o_ref[...] = (part + b_ref[...] if has_bias else part).astype(out_dtype)
            return
        k = pl.program_id(2)

        @pl.when(k == 0)
        def _():
            acc_ref[...] = jnp.zeros_like(acc_ref)

        acc_ref[...] += jnp.dot(a_ref[...].astype(BF16), w_ref[...].astype(BF16), preferred_element_type=F32)

        @pl.when(k == nk - 1)
        def _():
            r = acc_ref[...]
            if has_bias:
                r = r + b_ref[...]
            o_ref[...] = r.astype(out_dtype)

    in_specs = [pl.BlockSpec((tm, tk), lambda i, j, k: (i, k)), w_spec]
    args = [a, w]
    if has_bias:
        in_specs.append(pl.BlockSpec((1, tn), lambda i, j, k: (0, j)))
        args.append(bias)
    return pl.pallas_call(
        body, name=name, grid=(m // tm, n // tn, nk), in_specs=in_specs,
        out_specs=pl.BlockSpec((tm, tn), lambda i, j, k: (i, j)),
        out_shape=jax.ShapeDtypeStruct((m, n), out_dtype),
        scratch_shapes=[pltpu.VMEM((tm, tn), F32)] if nk > 1 else [],
        compiler_params=_cparams("parallel", "parallel", "arbitrary"),
    )(*args)


def _mm_nt(a, w, out_dtype, name, tm, tn, tk):
    if a.ndim == 3:
        planes, m, plane_w = a.shape
        c_dim = planes * plane_w
        a_per = plane_w // tk
        assert plane_w % tk == 0
        a_spec = pl.BlockSpec((None, tm, tk), lambda i, j, k: (k // a_per, i, k % a_per))
    else:
        m, c_dim = a.shape
        a_spec = pl.BlockSpec((tm, tk), lambda i, j, k: (i, k))
    if w.ndim == 3:
        k_out = w.shape[1]
        per = w.shape[2] // tk
        assert w.shape[2] % tk == 0 and w.shape[0] * w.shape[2] == c_dim
        w_spec = pl.BlockSpec((None, tn, tk), lambda i, j, k: (k // per, j, k % per))
    else:
        k_out = w.shape[0]
        assert w.shape[1] == c_dim
        w_spec = pl.BlockSpec((tn, tk), lambda i, j, k: (j, k))
    assert m % tm == 0 and k_out % tn == 0 and c_dim % tk == 0, (name, a.shape, w.shape, tm, tn, tk)
    nk = c_dim // tk

    def body(a_ref, w_ref, o_ref, acc_ref=None):
        if nk == 1:
            o_ref[...] = _dot(a_ref[...], w_ref[...], "nt").astype(out_dtype)
            return
        k = pl.program_id(2)

        @pl.when(k == 0)
        def _():
            acc_ref[...] = jnp.zeros_like(acc_ref)

        acc_ref[...] += _dot(a_ref[...], w_ref[...], "nt")

        @pl.when(k == nk - 1)
        def _():
            o_ref[...] = acc_ref[...].astype(out_dtype)

    return pl.pallas_call(
        body, name=name, grid=(m // tm, k_out // tn, nk),
        in_specs=[a_spec, w_spec],
        out_specs=pl.BlockSpec((tm, tn), lambda i, j, k: (i, j)),
        out_shape=jax.ShapeDtypeStruct((m, k_out), out_dtype),
        scratch_shapes=[pltpu.VMEM((tm, tn), F32)] if nk > 1 else [],
        compiler_params=_cparams("parallel", "parallel", "arbitrary"),
    )(a, w)


def _mm_tn(a, b, name, tm, tn, tk, shards=None, out_dtype=F32):
    r, k_dim = a.shape
    if b.ndim == 3:
        n = b.shape[0] * b.shape[2]
        b_per = b.shape[2] // tn
        assert b.shape[2] % tn == 0
        b_spec = pl.BlockSpec((None, tk, tn), lambda i, j, k: (j // b_per, k, j % b_per))
    else:
        n = b.shape[1]
        b_spec = pl.BlockSpec((tk, tn), lambda i, j, k: (k, j))
    assert r % tk == 0 and k_dim % tm == 0 and n % tn == 0, (name, a.shape, b.shape, tm, tn, tk)
    nk = r // tk
    if shards:
        per = n // shards // tn
        assert n % (shards * tn) == 0
        out_shape = jax.ShapeDtypeStruct((shards, k_dim, n // shards), out_dtype)
        out_spec = pl.BlockSpec((None, tm, tn), lambda i, j, k: (j // per, i, j % per))
    else:
        out_shape = jax.ShapeDtypeStruct((k_dim, n), out_dtype)
        out_spec = pl.BlockSpec((tm, tn), lambda i, j, k: (i, j))
    direct = out_dtype == F32

    def body(a_ref, b_ref, o_ref, *scratch):
        acc_ref = o_ref if direct else scratch[0]
        k = pl.program_id(2)

        @pl.when(k == 0)
        def _():
            acc_ref[...] = jnp.zeros_like(acc_ref)

        acc_ref[...] += _dot(a_ref[...], b_ref[...], "tn")
        if not direct:
            @pl.when(k == nk - 1)
            def _():
                o_ref[...] = acc_ref[...].astype(out_dtype)

    return pl.pallas_call(
        body, name=name, grid=(k_dim // tm, n // tn, nk),
        in_specs=[pl.BlockSpec((tk, tm), lambda i, j, k: (k, i)), b_spec],
        out_specs=out_spec, out_shape=out_shape,
        scratch_shapes=[] if direct else [pltpu.VMEM((tm, tn), F32)],
        compiler_params=_cparams("parallel", "parallel", "arbitrary"),
    )(a, b)


class _Carrier:
    def __init__(self, job, n_in, n_out, n_scratch):
        self.job, self.n_in, self.n_out, self.n_scratch = job, n_in, n_out, n_scratch
        self.ji = len(job.inputs) if job else 0
        self.jo = len(job.out_shapes) if job else 0

    def operands(self):
        return list(self.job.inputs) if self.job else []

    def in_specs(self):
        return [pl.BlockSpec(memory_space=pl.ANY)] * self.ji

    def out_specs(self):
        return [pl.BlockSpec(memory_space=pl.ANY)] * self.jo

    def out_shapes(self):
        return list(self.job.out_shapes) if self.job else []

    def scratch(self):
        return list(self.job.sem_shapes) if self.job else []

    def aliases(self):
        return {self.n_in + a: self.n_out + b for a, b in self.job.aliases.items()} if self.job else {}

    def split(self, refs):
        a = self.n_in
        b = a + self.ji
        c = b + self.n_out
        d = c + self.jo
        e = d + self.n_scratch
        return list(refs[:a]) + list(refs[b:c]) + list(refs[d:e]), (refs[a:b], refs[c:d], refs[e:])

    def run(self, job_refs, step, steps):
        if not self.job:
            return
        for stage, mark in zip(self.job.stages, _job_marks(self.job, steps)):
            pl.when(step == mark)(functools.partial(stage, *job_refs))

    def results(self, res):
        res = list(res)
        return res[:self.n_out], res[self.n_out:]


FFN_ROW_TILE = 768


def _ffn_tile(r):
    return FFN_ROW_TILE if r % FFN_ROW_TILE == 0 else _row_tile(r)


def _ffn_in_swiglu(h, w, name):
    r, k_dim = h.shape
    n4 = w.shape[2]
    tm = _ffn_tile(r)

    def body(h_ref, wg_ref, wu_ref, u_ref, a_ref):
        hv = h_ref[...]
        g = jnp.dot(hv, wg_ref[...], preferred_element_type=F32)
        up = jnp.dot(hv, wu_ref[...], preferred_element_type=F32)
        u_ref[0] = g.astype(BF16)
        u_ref[1] = up.astype(BF16)
        a_ref[...] = (_silu(g) * up).astype(BF16)

    return pl.pallas_call(
        body, name=name, grid=(r // tm, 2),
        in_specs=[pl.BlockSpec((tm, k_dim), lambda i, j: (i, 0)),
                  pl.BlockSpec((None, k_dim, n4), lambda i, j: (j, 0, 0)),
                  pl.BlockSpec((None, k_dim, n4), lambda i, j: (j + 2, 0, 0))],
        out_specs=[pl.BlockSpec((2, tm, n4), lambda i, j: (0, i, j)), pl.BlockSpec((tm, n4), lambda i, j: (i, j))],
        out_shape=[jax.ShapeDtypeStruct((2, r, 2 * n4), BF16), jax.ShapeDtypeStruct((r, 2 * n4), BF16)],
        compiler_params=_cparams("parallel", "parallel"),
    )(h, w, w)


def _mm_nn_gate_residual(geo, a, w, z, mod, off, name, norm=None):
    r, k_dim = a.shape
    n = w.shape[1]
    tm = FFN_ROW_TILE if geo.seg % FFN_ROW_TILE == 0 else 256
    tiles = geo.seg // tm
    assert geo.seg % tm == 0 and r == geo.r and n == D_MODEL

    def body(a_ref, w_ref, z_ref, mx_ref, mc_ref, *rest):
        out = jnp.dot(a_ref[...], w_ref[...], preferred_element_type=F32)
        is_x = (pl.program_id(0) % tiles) * tm + lax.broadcasted_iota(jnp.int32, (tm, 1), 0) < geo.s
        zo = z_ref[...] + jnp.where(is_x, mx_ref[:, off:off + n], mc_ref[:, off:off + n]) * out
        if norm:
            g_ref, nx_ref, nc_ref, zo_ref, raw_ref, h_ref = rest
            no = norm[2]
            shift = jnp.where(is_x, nx_ref[:, no:no + n], nc_ref[:, no:no + n])
            scale = jnp.where(is_x, nx_ref[:, no + n:no + 2 * n], nc_ref[:, no + n:no + 2 * n])
            rs = lax.rsqrt(jnp.mean(zo * zo, axis=-1, keepdims=True) + EPS)
            h_ref[...] = ((zo * rs) * g_ref[...] * (1.0 + scale) + shift).astype(BF16)
        else:
            zo_ref, raw_ref = rest
        zo_ref[...] = zo
        raw_ref[...] = out.astype(BF16)

    def mod_specs(m):
        return [pl.BlockSpec((None, 1, m.shape[2]), lambda i: (i // tiles, 0, 0)), pl.BlockSpec((None, 1, m.shape[2]), lambda i: (geo.b, 0, 0))]

    row = pl.BlockSpec((tm, n), lambda i: (i, 0))
    in_specs = [pl.BlockSpec((tm, k_dim), lambda i: (i, 0)), pl.BlockSpec((k_dim, n), lambda i: (0, 0)), row] + mod_specs(mod)
    args = [a, w, z, mod, mod]
    out_specs, out_shape = [row, row], [jax.ShapeDtypeStruct((r, n), F32), jax.ShapeDtypeStruct((r, n), BF16)]
    if norm:
        in_specs += [pl.BlockSpec((1, n), lambda i: (0, 0))] + mod_specs(norm[1])
        args += [norm[0], norm[1], norm[1]]
        out_specs.append(row)
        out_shape.append(jax.ShapeDtypeStruct((r, n), BF16))
    res = pl.pallas_call(body, name=name, grid=(r // tm,), in_specs=in_specs, out_specs=out_specs, out_shape=out_shape,
                         compiler_params=_cparams("parallel"))(*args)
    return res if norm else (*res, None)


def _ffn_out_dx_swiglu_bwd(df, w_out, u, name, job=None):
    r, d = df.shape
    n4 = u.shape[2] // 2
    tm = _ffn_tile(r)
    carrier = _Carrier(job, 3, 1, 0)
    steps = (r // tm) * 2

    def body(*refs):
        (df_ref, w_ref, u_ref, du_ref), job_refs = carrier.split(refs)
        carrier.run(job_refs, pl.program_id(0) * 2 + pl.program_id(1), steps)
        da = _dot(df_ref[...], w_ref[...], "nt")
        g, up = u_ref[0].astype(F32), u_ref[1].astype(F32)
        s = _sigmoid(g)
        du_ref[0] = (da * up * (s * (1.0 + g * (1.0 - s)))).astype(BF16)
        du_ref[1] = (da * (g * s)).astype(BF16)

    res = pl.pallas_call(
        body, name=name, grid=(r // tm, 2),
        in_specs=[pl.BlockSpec((tm, d), lambda i, j: (i, 0)), pl.BlockSpec((n4, d), lambda i, j: (j, 0)),
                  pl.BlockSpec((2, tm, n4), lambda i, j: (0, i, j))] + carrier.in_specs(),
        out_specs=[pl.BlockSpec((2, tm, n4), lambda i, j: (0, i, j))] + carrier.out_specs(),
        out_shape=[jax.ShapeDtypeStruct(u.shape, BF16)] + carrier.out_shapes(),
        scratch_shapes=carrier.scratch(), input_output_aliases=carrier.aliases(),
        compiler_params=_cparams("arbitrary", "arbitrary"),
    )(df, w_out, u, *carrier.operands())
    (du,), extra = carrier.results(res)
    return du, extra


class _Rows:
    def __init__(self, b, s, l):
        self.b, self.s, self.l = b, s, l
        self.seg = s + l
        self.r = b * self.seg


def _rowwise(name, body, geo, tm, ins, outs, job=None):
    seg_blocks, x_blocks = geo.seg // tm, geo.s // tm
    per_part = {"ex", "xrow"} & {k for _, k in ins if isinstance(k, str)} or {"exacc", "xrow"} & {o[0] for o in outs}
    assert geo.seg % tm == 0 and (geo.s % tm == 0 or not per_part), (name, tm)
    nb = geo.b

    def is_ctx(i):
        return i % seg_blocks >= x_blocks

    in_specs, args = [], []
    for arr, kind in ins:
        args.append(arr)
        if kind == "row":
            in_specs.append(pl.BlockSpec((tm, arr.shape[1]), lambda i: (i, 0)))
        elif kind == "ex":
            in_specs.append(pl.BlockSpec((None, 1, arr.shape[2]), lambda i: (jnp.where(is_ctx(i), nb, i // seg_blocks), 0, 0)))
        elif kind == "full":
            in_specs.append(pl.BlockSpec(arr.shape, lambda i, nd=arr.ndim: (0,) * nd))
        elif kind == "tab":
            in_specs.append(pl.BlockSpec((tm, arr.shape[1]), lambda i: (i % seg_blocks, 0)))
        elif kind == "xrow":
            in_specs.append(pl.BlockSpec(
                (tm, arr.shape[1]), lambda i: ((i // seg_blocks) * x_blocks + jnp.minimum(i % seg_blocks, x_blocks - 1), 0)))
        else:
            _, width, cb = kind
            in_specs.append(pl.BlockSpec((tm, width), lambda i, cb=cb: (i, cb)))
    out_specs, out_shapes = [], []
    for o in outs:
        if o[0] == "row":
            out_specs.append(pl.BlockSpec((tm, o[1]), lambda i: (i, 0)))
            out_shapes.append(jax.ShapeDtypeStruct((geo.r, o[1]), o[2]))
        elif o[0] == "xrow":
            out_specs.append(pl.BlockSpec(
                (tm, o[1]), lambda i: ((i // seg_blocks) * x_blocks + jnp.minimum(i % seg_blocks, x_blocks - 1), 0)))
            out_shapes.append(jax.ShapeDtypeStruct((geo.b * geo.s, o[1]), o[2]))
        elif o[0] == "exacc":
            out_specs.append(pl.BlockSpec((None, 1, o[1]), lambda i: (jnp.where(is_ctx(i), nb, 0) + i // seg_blocks, 0, 0)))
            out_shapes.append(jax.ShapeDtypeStruct((2 * nb, 1, o[1]), F32))
        else:
            out_specs.append(pl.BlockSpec((o[1], o[2]), lambda i: (0, 0)))
            out_shapes.append(jax.ShapeDtypeStruct((o[1], o[2]), F32))
    n_in = len(ins)
    carrier = _Carrier(job, n_in, len(outs), 0)

    def kern(*refs):
        i = pl.program_id(0)
        refs, job_refs = carrier.split(refs)
        carrier.run(job_refs, i, geo.r // tm)
        res = body(i, *[r[...].astype(F32) for r in refs[:n_in]])
        if not isinstance(res, (tuple, list)):
            res = (res,)
        jj = i % seg_blocks
        first_of_part = (jj == 0) | (jj == x_blocks)
        for o, ref, val in zip(outs, refs[n_in:], res):
            if o[0] == "row":
                ref[...] = val.astype(ref.dtype)
            elif o[0] == "xrow":
                @pl.when(jj < x_blocks)
                def _(ref=ref, val=val):
                    ref[...] = val.astype(ref.dtype)
            else:
                first = first_of_part if o[0] == "exacc" else i == 0

                @pl.when(first)
                def _(ref=ref, val=val):
                    ref[...] = val

                @pl.when(jnp.logical_not(first))
                def _(ref=ref, val=val):
                    ref[...] += val

    res = pl.pallas_call(
        kern, name=name, grid=(geo.r // tm,), in_specs=in_specs + carrier.in_specs(), out_specs=out_specs + carrier.out_specs(),
        out_shape=out_shapes + carrier.out_shapes(), scratch_shapes=carrier.scratch(), input_output_aliases=carrier.aliases(),
        compiler_params=_cparams("arbitrary"),
    )(*args, *carrier.operands())
    own, extra = carrier.results(res)
    if job:
        return (*own, extra)
    return own[0] if len(own) == 1 else own


def _colsum(v):
    return jnp.sum(v, axis=0, keepdims=True)


def _norm_mod(geo, z, gain, mod, off, name):
    d = D_MODEL

    def body(i, zv, g, m):
        r = lax.rsqrt(jnp.mean(zv * zv, axis=-1, keepdims=True) + EPS)
        return (zv * r) * g * (1.0 + m[:, off + d:off + 2 * d]) + m[:, off:off + d]

    return _rowwise(name, body, geo, 256, [(z, "row"), (gain, "full"), (mod, "ex")], [("row", d, BF16)])


def _norm_mod_bwd(geo, z, gain, mod, off, dh, dz_skip, name, gated=None, latent_only=False, job=None):
    d = D_MODEL

    def body(i, zv, g, m, dhv, skip, *rest):
        r = lax.rsqrt(jnp.mean(zv * zv, axis=-1, keepdims=True) + EPS)
        n = zv * r
        dng = dhv * (1.0 + m[:, off + d:off + 2 * d])
        dn = dng * g
        dz = r * (dn - n * jnp.mean(dn * n, axis=-1, keepdims=True)) + skip
        res = (dz, _colsum(dhv), _colsum(dhv * (n * g)), _colsum(dng * n))
        if gated:
            ov, gm = rest
            res += (dz * gm[:, gated[2]:gated[2] + d], _colsum(dz * ov))
        return res

    ins = [(z, "row"), (gain, "full"), (mod, "ex"), (dh, "row"), (dz_skip, "row")]
    outs = [("xrow" if latent_only else "row", d, F32), ("exacc", d), ("exacc", d), ("gacc", 1, d)]
    if gated:
        ins += [(gated[0], "row"), (gated[1], "ex")]
        outs += [("row", d, BF16), ("exacc", d)]
    return _rowwise(name, body, geo, 256, ins, outs, job)


def _loss_head(geo, z, target, out, mod, off, name):
    seg_blocks, x_blocks = geo.seg // 256, geo.s // 256
    d = D_MODEL

    def body(i, zv, tv, ov, m):
        keep = jnp.where(i % seg_blocks >= x_blocks, 0.0, 1.0)
        err = (zv - tv) * keep
        part = 0.5 * jnp.sum(jnp.mean(err * err, axis=-1, keepdims=True), axis=0, keepdims=True)
        dz = err * (1.0 / d)
        return dz, jnp.broadcast_to(part, (1, LANES)), dz * m[:, off:off + d], _colsum(dz * ov)

    return _rowwise(name, body, geo, 256, [(z, "row"), (target, "xrow"), (out, "row"), (mod, "ex")],
                    [("row", d, F32), ("gacc", 1, LANES), ("row", d, BF16), ("exacc", d)])


Q_SCALE = HEAD_DIM ** -0.5
N_QK_CHUNKS = (N_HEADS + N_KV_HEADS) * HEAD_DIM // LANES
N_Q_CHUNKS = N_HEADS * HEAD_DIM // LANES


def _prep_tile(geo):
    return FFN_ROW_TILE if geo.seg % FFN_ROW_TILE == 0 else 256


def _attn_prep(geo, proj, cos, sin_signed, q_gain, k_gain, name):
    def body(i, p, cs, sn, qg, kg):
        outs = []
        for ch in range(N_QK_CHUNKS):
            is_q = ch < N_Q_CHUNKS
            outs.append(_qk_chunk(p[:, ch * LANES:(ch + 1) * LANES], qg if is_q else kg, cs, sn, Q_SCALE if is_q else 1.0))
        outs.append(p[:, N_QK_CHUNKS * LANES:])
        return jnp.concatenate(outs, axis=1)

    return _rowwise(name, body, geo, _prep_tile(geo), [(proj, "row"), (cos, "tab"), (sin_signed, "tab"), (q_gain, "full"), (k_gain, "full")],
                    [("row", proj.shape[1], BF16)])


def _attn_prep_bwd(geo, proj, cos, sin_signed, q_gain, k_gain, dq, dkv, name):
    kw = N_KV_HEADS * HEAD_DIM

    def body(i, p, cs, sn, qg, kg, dqv, dkvv):
        outs = []
        dgains = [jnp.zeros((1, LANES), F32), jnp.zeros((1, LANES), F32)]
        for ch in range(N_QK_CHUNKS):
            is_q = ch < N_Q_CHUNKS
            scale = Q_SCALE if is_q else 1.0
            ct = dqv[:, ch * LANES:(ch + 1) * LANES] if is_q else dkvv[:, (ch - N_Q_CHUNKS) * LANES:(ch - N_Q_CHUNKS + 1) * LANES]
            _, vjp = jax.vjp(lambda xx, gg, scale=scale: _qk_chunk(xx, gg, cs, sn, scale),
                             p[:, ch * LANES:(ch + 1) * LANES], qg if is_q else kg)
            dx, dg = vjp(ct)
            outs.append(dx)
            dgains[0 if is_q else 1] = dgains[0 if is_q else 1] + dg
        outs.append(dkvv[:, kw:])
        return jnp.concatenate(outs, axis=1), dgains[0], dgains[1]

    return _rowwise(name, body, geo, 256,
                    [(proj, "row"), (cos, "tab"), (sin_signed, "tab"), (q_gain, "full"), (k_gain, "full"), (dq, "row"), (dkv, "row")],
                    [("row", proj.shape[1], BF16), ("gacc", 1, LANES), ("gacc", 1, LANES)])


def _attn_geometry(geo):
    assert geo.s % ATTN_BLOCK == 0 and geo.l % ATTN_BLOCK == 0 and geo.seg >= BAND
    return geo.seg // ATTN_BLOCK, geo.s // ATTN_BLOCK


def _attn_mask(j, s0, geo):
    r = lax.broadcasted_iota(jnp.int32, (ATTN_BLOCK, geo.l + BAND), 0)
    n = lax.broadcasted_iota(jnp.int32, (ATTN_BLOCK, geo.l + BAND), 1) - geo.l
    dist = (s0 - j * ATTN_BLOCK) + n - r
    return (n < 0) | ((jnp.abs(dist) <= WINDOW) & (s0 + n < geo.s))


def _attn_probs(q, keys, valid, n_ctx, sink):
    s = _dot(q, keys, "nt")
    if valid is not None:
        s = jnp.where(valid, s, NEG_INF)
    m = jnp.maximum(jnp.max(s, axis=-1, keepdims=True), sink)
    e, e_sink = jnp.exp(s - m), jnp.exp(sink - m)
    inv = 1.0 / (jnp.sum(e, axis=-1, keepdims=True) + e_sink)
    return e * inv, e_sink * inv


def _attn_keys(ref, s0, geo, with_band):
    ctx = ref[geo.s:geo.seg, :]
    return jnp.concatenate([ctx, ref[pl.ds(s0, BAND), :]], axis=0) if with_band else ctx


def _attention(geo, qkv, sink, name, job=None):
    n_blocks, n_x_blocks = _attn_geometry(geo)
    qw, kw = N_HEADS * HEAD_DIM, N_KV_HEADS * HEAD_DIM
    group = N_HEADS // N_KV_HEADS
    carrier = _Carrier(job, 4, 1, 0)

    def kern(*refs):
        (sink_ref, q_ref, k_ref, v_ref, o_ref), job_refs = carrier.split(refs)
        j = pl.program_id(1)
        carrier.run(job_refs, pl.program_id(0) * n_blocks + j, geo.b * n_blocks)
        s0 = pl.multiple_of(jnp.clip((j - 1) * ATTN_BLOCK, 0, geo.seg - BAND), ATTN_BLOCK)

        def heads(with_band):
            valid = _attn_mask(j, s0, geo) if with_band else None
            k_all, v_all = _attn_keys(k_ref, s0, geo, with_band), _attn_keys(v_ref, s0, geo, with_band)
            for h in range(N_HEADS):
                kv = slice((h // group) * HEAD_DIM, (h // group + 1) * HEAD_DIM)
                p, _ = _attn_probs(q_ref[:, h * HEAD_DIM:(h + 1) * HEAD_DIM], k_all[:, kv], valid, geo.l, sink_ref[h])
                o_ref[:, h * HEAD_DIM:(h + 1) * HEAD_DIM] = _dot(p, v_all[:, kv], "nn").astype(BF16)

        pl.when(j < n_x_blocks)(lambda: heads(True))
        pl.when(j >= n_x_blocks)(lambda: heads(False))

    res = pl.pallas_call(
        kern, name=name, grid=(geo.b, n_blocks),
        in_specs=[pl.BlockSpec(memory_space=pltpu.SMEM),
                  pl.BlockSpec((ATTN_BLOCK, qw), lambda b, j: (b * n_blocks + j, 0)),
                  pl.BlockSpec((geo.seg, kw), lambda b, j: (b, qw // kw)),
                  pl.BlockSpec((geo.seg, kw), lambda b, j: (b, qw // kw + 1))] + carrier.in_specs(),
        out_specs=[pl.BlockSpec((ATTN_BLOCK, qw), lambda b, j: (b * n_blocks + j, 0))] + carrier.out_specs(),
        out_shape=[jax.ShapeDtypeStruct((geo.r, qw), BF16)] + carrier.out_shapes(),
        scratch_shapes=carrier.scratch(), input_output_aliases=carrier.aliases(),
        compiler_params=_cparams("arbitrary", "arbitrary"),
    )(sink, qkv, qkv, qkv, *carrier.operands())
    (o,), extra = carrier.results(res)
    return o, extra


def _attention_bwd(geo, qkv, sink, do, name, job=None):
    n_blocks, n_x_blocks = _attn_geometry(geo)
    qw, kw = N_HEADS * HEAD_DIM, N_KV_HEADS * HEAD_DIM
    group = N_HEADS // N_KV_HEADS

    carrier = _Carrier(job, 5, 3, 1)

    def kern(*refs):
        (sink_ref, q_ref, k_ref, v_ref, do_ref, dq_ref, dkv_out_ref, dsink_ref, dkv_ref), job_refs = carrier.split(refs)
        b, j = pl.program_id(0), pl.program_id(1)
        carrier.run(job_refs, b * n_blocks + j, geo.b * n_blocks)
        s0 = pl.multiple_of(jnp.clip((j - 1) * ATTN_BLOCK, 0, geo.seg - BAND), ATTN_BLOCK)

        @pl.when(j == 0)
        def _():
            dkv_ref[...] = jnp.zeros_like(dkv_ref)

        @pl.when((j == 0) & (b == 0))
        def _():
            dsink_ref[...] = jnp.zeros_like(dsink_ref)

        def heads(with_band):
            valid = _attn_mask(j, s0, geo) if with_band else None
            k_all, v_all = _attn_keys(k_ref, s0, geo, with_band), _attn_keys(v_ref, s0, geo, with_band)
            for g in range(N_KV_HEADS):
                kv = slice(g * HEAD_DIM, (g + 1) * HEAD_DIM)
                keys, vals = k_all[:, kv], v_all[:, kv]
                group_heads = [slice(h * HEAD_DIM, (h + 1) * HEAD_DIM) for h in range(g * group, (g + 1) * group)]
                ds_rows, p_rows = [], []
                for h, hs in zip(range(g * group, (g + 1) * group), group_heads):
                    dout = do_ref[:, hs]
                    p, p_sink = _attn_probs(q_ref[:, hs], keys, valid, geo.l, sink_ref[h])
                    dp = _dot(dout, vals, "nt")
                    dsum = jnp.sum(p * dp, axis=-1, keepdims=True)
                    ds = (p * (dp - dsum)).astype(BF16)
                    dq_ref[:, hs] = _dot(ds, keys, "nn").astype(BF16)
                    ds_rows.append(ds)
                    p_rows.append(p.astype(BF16))
                    dsink_ref[h:h + 1, :] += jnp.broadcast_to(-jnp.sum(p_sink * dsum, axis=0, keepdims=True), (1, LANES))
                q_rows = jnp.concatenate([q_ref[:, hs] for hs in group_heads], axis=0)
                do_rows = jnp.concatenate([do_ref[:, hs] for hs in group_heads], axis=0)
                dk = _dot(jnp.concatenate(ds_rows, axis=0), q_rows, "tn")
                dv = _dot(jnp.concatenate(p_rows, axis=0), do_rows, "tn")
                vv = slice(kw + g * HEAD_DIM, kw + (g + 1) * HEAD_DIM)
                dkv_ref[geo.s:geo.seg, kv] += dk[:geo.l]
                dkv_ref[geo.s:geo.seg, vv] += dv[:geo.l]
                if with_band:
                    dkv_ref[pl.ds(s0, BAND), kv] += dk[geo.l:]
                    dkv_ref[pl.ds(s0, BAND), vv] += dv[geo.l:]

        pl.when(j < n_x_blocks)(lambda: heads(True))
        pl.when(j >= n_x_blocks)(lambda: heads(False))

        @pl.when(j == n_blocks - 1)
        def _():
            dkv_out_ref[...] = dkv_ref[...].astype(BF16)

    res = pl.pallas_call(
        kern, name=name, grid=(geo.b, n_blocks),
        in_specs=[pl.BlockSpec(memory_space=pltpu.SMEM),
                  pl.BlockSpec((ATTN_BLOCK, qw), lambda b, j: (b * n_blocks + j, 0)),
                  pl.BlockSpec((geo.seg, kw), lambda b, j: (b, qw // kw)),
                  pl.BlockSpec((geo.seg, kw), lambda b, j: (b, qw // kw + 1)),
                  pl.BlockSpec((ATTN_BLOCK, qw), lambda b, j: (b * n_blocks + j, 0))] + carrier.in_specs(),
        out_specs=[pl.BlockSpec((ATTN_BLOCK, qw), lambda b, j: (b * n_blocks + j, 0)),
                   pl.BlockSpec((geo.seg, 2 * kw), lambda b, j: (b, 0)),
                   pl.BlockSpec((N_HEADS, LANES), lambda b, j: (0, 0))] + carrier.out_specs(),
        out_shape=[jax.ShapeDtypeStruct((geo.r, qw), BF16), jax.ShapeDtypeStruct((geo.r, 2 * kw), BF16),
                   jax.ShapeDtypeStruct((N_HEADS, LANES), F32)] + carrier.out_shapes(),
        scratch_shapes=[pltpu.VMEM((geo.seg, 2 * kw), F32)] + carrier.scratch(), input_output_aliases=carrier.aliases(),
        compiler_params=_cparams("arbitrary", "arbitrary"),
    )(sink, qkv, qkv, qkv, do, *carrier.operands())
    (dq, dkv, dsink), extra = carrier.results(res)
    return dq, dkv, dsink, extra


RET_QK_W = RET_HEADS * RET_QK_DIM
K_SCALE = RET_QK_DIM ** -0.5


def _ret_prep(geo, proj, cos, sin_signed, name):
    def body(i, p, cs, sn):
        cs2, sn2 = jnp.concatenate([cs] * RET_HEADS, axis=1), jnp.concatenate([sn] * RET_HEADS, axis=1)
        q = _rope(p[:, :RET_QK_W], cs2, sn2, RET_QK_DIM // 4)
        k = _rope(p[:, RET_QK_W:2 * RET_QK_W], cs2, sn2, RET_QK_DIM // 4) * K_SCALE
        return jnp.concatenate([q, k, p[:, 2 * RET_QK_W:]], axis=1)

    return _rowwise(name, body, geo, 256, [(proj, ("rowc", 2 * RET_QK_W + RET_VWIDTH, 0)), (cos, "tab"), (sin_signed, "tab")],
                    [("row", 2 * RET_QK_W + RET_VWIDTH, BF16)])


def _ret_prep_bwd(geo, dq, dk, dv, dgate, cos, sin_signed, name):
    def body(i, dqv, dkv, dvv, dg, cs, sn):
        cs2, sn2 = jnp.concatenate([cs] * RET_HEADS, axis=1), jnp.concatenate([sn] * RET_HEADS, axis=1)
        dkv = dkv * K_SCALE
        dqv = dqv * cs2 + _swap_halves(dqv * sn2, RET_QK_DIM // 4)
        dkv = dkv * cs2 + _swap_halves(dkv * sn2, RET_QK_DIM // 4)
        return jnp.concatenate([dqv, dkv, dvv, dg], axis=1)

    return _rowwise(name, body, geo, 256,
                    [(dq, "row"), (dk, "row"), (dv, "row"), (dgate, "row"), (cos, "tab"), (sin_signed, "tab")],
                    [("row", 2 * RET_QK_W + 2 * RET_VWIDTH, BF16)])


def _ret_step(state, q, k, v, lg, rev):
    c = RET_CHUNK
    ri = lax.broadcasted_iota(jnp.int32, (c, 1), 0).astype(F32)
    cj = lax.broadcasted_iota(jnp.int32, (1, c), 1).astype(F32)
    if rev:
        dist, q_decay, k_decay = cj - ri, jnp.exp(lg * (c - ri)), jnp.exp(lg * ri)
    else:
        dist, q_decay, k_decay = ri - cj, jnp.exp(lg * (ri + 1.0)), jnp.exp(lg * (c - 1.0 - ri))
    intra = jnp.where(dist >= 0, jnp.exp(lg * jnp.maximum(dist, 0.0)), 0.0)
    scores = _mm(q, k, "nt") * intra
    out = _mm(scores, v, "nn") + _mm(q, state, "nn") * q_decay
    new_state = state * jnp.exp(lg * c) + _mm(k * k_decay, v, "tn")
    return new_state, out


def _ret_state0(kc, vc, lg, rev):
    n = kc.shape[0]
    t = lax.broadcasted_iota(jnp.int32, (n, 1), 0).astype(F32)
    decay = jnp.exp(lg * t) if rev else jnp.exp(lg * (n - 1.0 - t))
    return _mm(kc * decay, vc, "tn")


def _ret_specs(geo):
    nq = RET_HEADS
    return [pl.BlockSpec((2 * RET_HEADS, LANES), lambda b, h: (0, 0)),
            pl.BlockSpec((geo.seg, RET_QK_DIM), lambda b, h: (b, h)),
            pl.BlockSpec((geo.seg, RET_QK_DIM), lambda b, h: (b, nq + h)),
            pl.BlockSpec((geo.seg, RET_V_DIM), lambda b, h: (b, nq + h))]


def _retention(geo, qkv, log_g, name):
    nc = geo.s // RET_CHUNK

    def kern(lg_ref, q_ref, k_ref, v_ref, o_ref, st_ref):
        h = pl.program_id(1)
        for d, rev in ((0, False), (1, True)):
            lg = lg_ref[pl.ds(d * RET_HEADS + h, 1), 0:1]
            st_ref[...] = _ret_state0(k_ref[geo.s:geo.seg, :].astype(F32), v_ref[geo.s:geo.seg, :].astype(F32), lg, rev)

            def chunk(ci, carry, d=d, rev=rev, lg=lg):
                r0 = pl.multiple_of((nc - 1 - ci if rev else ci) * RET_CHUNK, RET_CHUNK)
                rows = pl.ds(r0, RET_CHUNK)
                new_state, out = _ret_step(st_ref[...], q_ref[rows, :], k_ref[rows, :], v_ref[rows, :], lg, rev)
                st_ref[...] = new_state
                if d == 0:
                    o_ref[rows, :] = out
                else:
                    o_ref[rows, :] += out
                return carry

            lax.fori_loop(0, nc, chunk, 0)
        o_ref[geo.s:geo.seg, :] = jnp.zeros((geo.l, RET_V_DIM), F32)

    return pl.pallas_call(
        kern, name=name, grid=(geo.b, RET_HEADS), in_specs=_ret_specs(geo),
        out_specs=pl.BlockSpec((geo.seg, RET_V_DIM), lambda b, h: (b, h)),
        out_shape=jax.ShapeDtypeStruct((geo.r, RET_VWIDTH), F32),
        scratch_shapes=[pltpu.VMEM((RET_QK_DIM, RET_V_DIM), F32)],
        compiler_params=_cparams("parallel", "arbitrary"),
    )(log_g, qkv, qkv, qkv)


def _retention_bwd(geo, qkv, log_g, do, name):
    nc = geo.s // RET_CHUNK
    ctx = slice(geo.s, geo.seg)

    def kern(lg_ref, q_ref, k_ref, v_ref, do_ref, dq_ref, dk_ref, dv_ref, dlg_ref, states_ref, dst_ref, aq_ref, ak_ref, av_ref):
        b, h = pl.program_id(0), pl.program_id(1)

        @pl.when((b == 0) & (h == 0))
        def _():
            dlg_ref[...] = jnp.zeros_like(dlg_ref)

        for d, rev in ((0, False), (1, True)):
            row = pl.ds(d * RET_HEADS + h, 1)
            lg = lg_ref[row, 0:1]
            kc, vc = k_ref[ctx, :].astype(F32), v_ref[ctx, :].astype(F32)
            states_ref[0] = _ret_state0(kc, vc, lg, rev)

            def rows_of(ci, rev=rev):
                return pl.ds(pl.multiple_of((nc - 1 - ci if rev else ci) * RET_CHUNK, RET_CHUNK), RET_CHUNK)

            def load(rows):
                return q_ref[rows, :].astype(F32), k_ref[rows, :].astype(F32), v_ref[rows, :].astype(F32)

            def replay(ci, carry, rev=rev, lg=lg, rows_of=rows_of, load=load):
                states_ref[ci + 1] = _ret_step(states_ref[ci], *load(rows_of(ci)), lg, rev)[0]
                return carry

            lax.fori_loop(0, nc - 1, replay, 0)
            dst_ref[...] = jnp.zeros_like(dst_ref)

            def emit(rows, dq, dk, dv, d=d):
                if d == 0:
                    ak_ref[rows, :], av_ref[rows, :] = dk, dv
                    if dq is not None:
                        aq_ref[rows, :] = dq
                else:
                    dk_ref[rows, :] = (ak_ref[rows, :] + dk).astype(BF16)
                    dv_ref[rows, :] = (av_ref[rows, :] + dv).astype(BF16)
                    if dq is not None:
                        dq_ref[rows, :] = (aq_ref[rows, :] + dq).astype(BF16)

            def back(t, dlg, rev=rev, lg=lg, rows_of=rows_of, load=load, emit=emit):
                ci = nc - 1 - t
                rows = rows_of(ci)
                _, vjp = jax.vjp(lambda st, q, k, v, g: _ret_step(st, q, k, v, g, rev), states_ref[ci], *load(rows), lg)
                dstate, dq, dk, dv, dg = vjp((dst_ref[...], do_ref[rows, :].astype(F32)))
                dst_ref[...] = dstate
                emit(rows, dq, dk, dv)
                return dlg + dg

            dlg = lax.fori_loop(0, nc, back, jnp.zeros((1, 1), F32))
            _, vjp = jax.vjp(lambda kk, vv, g: _ret_state0(kk, vv, g, rev), kc, vc, lg)
            dkc, dvc, dg = vjp(dst_ref[...])
            emit(ctx, None, dkc, dvc)
            dlg_ref[row, :] += jnp.broadcast_to(dlg + dg, (1, LANES))
        dq_ref[ctx, :] = jnp.zeros((geo.l, RET_QK_DIM), BF16)

    nq = RET_HEADS
    return pl.pallas_call(
        kern, name=name, grid=(geo.b, RET_HEADS),
        in_specs=_ret_specs(geo) + [pl.BlockSpec((geo.seg, RET_V_DIM), lambda b, h: (b, h))],
        out_specs=[pl.BlockSpec((geo.seg, RET_QK_DIM), lambda b, h: (b, h)),
                   pl.BlockSpec((geo.seg, RET_QK_DIM), lambda b, h: (b, h)),
                   pl.BlockSpec((geo.seg, RET_V_DIM), lambda b, h: (b, h)),
                   pl.BlockSpec((2 * RET_HEADS, LANES), lambda b, h: (0, 0))],
        out_shape=[jax.ShapeDtypeStruct((geo.r, RET_QK_W), BF16), jax.ShapeDtypeStruct((geo.r, RET_QK_W), BF16),
                   jax.ShapeDtypeStruct((geo.r, RET_VWIDTH), BF16), jax.ShapeDtypeStruct((2 * RET_HEADS, LANES), F32)],
        scratch_shapes=[pltpu.VMEM((nc, RET_QK_DIM, RET_V_DIM), F32), pltpu.VMEM((RET_QK_DIM, RET_V_DIM), F32),
                        pltpu.VMEM((geo.seg, RET_QK_DIM), F32), pltpu.VMEM((geo.seg, RET_QK_DIM), F32),
                        pltpu.VMEM((geo.seg, RET_V_DIM), F32)],
        compiler_params=_cparams("arbitrary", "arbitrary"),
    )(log_g, qkv, qkv, qkv, do)


def _gated(o, g, gain):
    outs = []
    for h in range(RET_HEADS):
        cols = slice(h * RET_V_DIM, (h + 1) * RET_V_DIM)
        oh = o[:, cols]
        mu = jnp.mean(oh, axis=-1, keepdims=True)
        var = jnp.mean(jnp.square(oh - mu), axis=-1, keepdims=True)
        outs.append(_silu(g[:, cols]) * ((oh - mu) * lax.rsqrt(var + EPS) * gain[:, cols]))
    return jnp.concatenate(outs, axis=1)


def _ret_gated(geo, o, proj, gain, name):
    def body(i, ov, gv, gn):
        return _gated(ov, gv, gn)

    gate_block = (2 * RET_QK_W + RET_VWIDTH) // RET_VWIDTH
    return _rowwise(name, body, geo, 256, [(o, "row"), (proj, ("rowc", RET_VWIDTH, gate_block)), (gain, "full")],
                    [("row", RET_VWIDTH, BF16)])


def _ret_gated_bwd(geo, o, proj, gain, dout, name):
    def body(i, ov, gv, gn, dv):
        _, vjp = jax.vjp(_gated, ov, gv, gn)
        return vjp(dv)

    gate_block = (2 * RET_QK_W + RET_VWIDTH) // RET_VWIDTH
    return _rowwise(name, body, geo, 256,
                    [(o, "row"), (proj, ("rowc", RET_VWIDTH, gate_block)), (gain, "full"), (dout, "row")],
                    [("row", RET_VWIDTH, BF16), ("row", RET_VWIDTH, BF16), ("gacc", 1, RET_VWIDTH)])


def _whole(name, fn, out_shapes, *arrays):
    n = len(arrays)

    def kern(*refs):
        res = fn(*[r[...] for r in refs[:n]])
        for ref, val in zip(refs[n:], res):
            ref[...] = val.astype(ref.dtype)

    return pl.pallas_call(kern, name=name, out_shape=out_shapes)(*arrays)


def _rope_tables(geo, head_dim):
    rows = geo.s // GRID_W
    row = jnp.broadcast_to(jnp.arange(rows, dtype=jnp.int32)[:, None], (rows, GRID_W)).reshape(geo.s)
    col = jnp.broadcast_to(jnp.arange(GRID_W, dtype=jnp.int32)[None, :], (rows, GRID_W)).reshape(geo.s)
    axis_dim = head_dim // 2
    inv = ROPE_BASE ** (-jnp.arange(0, axis_dim, 2, dtype=F32) / axis_dim)
    ang_r = row.astype(F32)[:, None] * inv
    ang_c = col.astype(F32)[:, None] * inv
    cos = jnp.concatenate([jnp.cos(ang_r)] * 2 + [jnp.cos(ang_c)] * 2, axis=1)
    sin = jnp.concatenate([-jnp.sin(ang_r), jnp.sin(ang_r), -jnp.sin(ang_c), jnp.sin(ang_c)], axis=1)
    cos = jnp.concatenate([cos, jnp.ones((geo.l, head_dim), F32)], axis=0)
    sin = jnp.concatenate([sin, jnp.zeros((geo.l, head_dim), F32)], axis=0)
    reps = max(1, LANES // head_dim)
    return jnp.tile(cos, (1, reps)), jnp.tile(sin, (1, reps))


def _row_tile(r):
    return next(t for t in (1536, 1024, 512, 256, 128) if r % t == 0)


MOD_ROWS = 8


def _local_step(x, ctx, target, sp, wts, mods, plan=None):
    nb, s, d = x.shape
    geo = _Rows(nb, s, ctx.shape[1])
    assert nb + 1 <= MOD_ROWS and d == D_MODEL
    tm = _row_tile(geo.r)
    z = jnp.concatenate([x, ctx], axis=1).reshape(geo.r, d)
    cos64, sin64 = _rope_tables(geo, HEAD_DIM)
    cos256, sin256 = _rope_tables(geo, RET_QK_DIM)
    q_gain = jnp.tile(sp["q_norm"].reshape(1, HEAD_DIM), (1, LANES // HEAD_DIM))
    k_gain = jnp.tile(sp["k_norm"].reshape(1, HEAD_DIM), (1, LANES // HEAD_DIM))
    sink = sp["sink"].reshape(N_HEADS)
    log_g = jnp.broadcast_to(sp["log_g"].reshape(2 * RET_HEADS, 1), (2 * RET_HEADS, LANES))
    gn_g = sp["gn_g"].reshape(1, RET_VWIDTH)

    saved = []
    h1 = _norm_mod(geo, z, sp["norm1_g"][0][None, :], mods[0], 0, "norm1_0")
    for i in range(2):
        mod3 = mods[i]
        n1, n2 = sp["norm1_g"][i][None, :], sp["norm2_g"][i][None, :]
        if i == 0:
            proj = _mm_nn(h1, wts["attn_qkv"], F32, "attn_qkv", tm, wts["attn_qkv"].shape[1], d)
            prep = _attn_prep(geo, proj, cos64, sin64, q_gain, k_gain, "attn_prep")
            o, late = _attention(geo, prep, sink, "attn", plan.gather_job() if plan else None)
            if plan:
                plan.late_weights(late, wts)
            oraw = None
            w_o = wts["attn_o"]
        else:
            proj = _mm_nn(h1, wts["ret_qkvg"], BF16, "ret_qkvg", tm, wts["ret_qkvg"].shape[2], d)
            prep = _ret_prep(geo, proj, cos256, sin256, "ret_prep")
            oraw = _retention(geo, prep, log_g, "ret")
            o = _ret_gated(geo, oraw, proj, gn_g, "ret_gated")
            w_o = wts["ret_o"]
        zmid, mix, h2 = _mm_nn_gate_residual(geo, o, w_o, z, mod3, 2 * d, f"mix_out{i}", norm=(n2, mod3, 3 * d))
        u, a = _ffn_in_swiglu(h2, wts["ffn_in"][i], f"ffn_in{i}")
        next_norm = (sp["norm1_g"][1][None, :], mods[1], 0) if i == 0 else None
        zout, f, h1_next = _mm_nn_gate_residual(geo, a, wts["ffn_out"][i], zmid, mod3, 5 * d, f"ffn_out{i}", norm=next_norm)
        saved.append(dict(z=z, mod3=mod3, n1=n1, n2=n2, h1=h1, proj=proj, prep=prep, o=o, oraw=oraw, mix=mix, zmid=zmid,
                          h2=h2, u=u, a=a, f=f))
        z, h1 = zout, h1_next

    dz, loss, df, dg2 = _loss_head(geo, z, target.reshape(nb * s, d), saved[1]["f"], saved[1]["mod3"], 5 * d, "loss")

    big, small = {}, {}
    dmods = [None, None]
    for i in (1, 0):
        sv = saved[i]
        mod3 = sv["mod3"]
        carry = plan is not None and i == 0
        du, land = _ffn_out_dx_swiglu_bwd(df, wts["ffn_out"][i], sv["u"], f"ffn_out_dx{i}", plan.layer1.swap_job() if carry else None)
        if carry:
            plan.layer1.after_swap(land)
        big[f"ffn_out{i}"] = _mm_tn(sv["a"], df, f"ffn_out_dw{i}", D_FF // 2, 1024, tm, out_dtype=BF16).reshape(N_CHIPS, D_FF // N_CHIPS, d)
        n4 = wts["ffn_in"][i].shape[2]
        dh2 = _mm_nt(du, wts["ffn_in"][i], BF16, f"ffn_in_dx{i}", tm, 1024, n4)
        big[f"ffn_in{i}"] = _mm_tn(sv["h2"], du, f"ffn_in_dw{i}", 1024, n4, tm, shards=N_CHIPS, out_dtype=BF16)
        if carry:
            plan.start_layer0_ffn(big)
        dzmid, dsh2, dsc2, dn2, dmix, dg1, *land = _norm_mod_bwd(geo, sv["zmid"], sv["n2"], mod3, 3 * d, dh2, dz, f"norm2_bwd{i}",
                                                                 gated=(sv["mix"], mod3, 2 * d),
                                                                 job=plan.layer0_ffn.swap_job() if carry else None)
        if carry:
            plan.layer0_ffn.after_swap(land[0])
        if i == 0:
            do = _mm_nt(dmix, wts["attn_o"], BF16, "attn_out_dx", tm, 1024, 1024)
            big["attn_o"] = _mm_tn(sv["o"], dmix, "attn_out_dw", 1024, 1024, tm, out_dtype=BF16).reshape(N_CHIPS, 1024 // N_CHIPS, d)
            dq, dkv, dsink, land = _attention_bwd(geo, sv["prep"], sink, do, "attn_bwd", plan.exchange_job() if plan else None)
            if plan:
                plan.after_exchange(land)
            dproj, dqg, dkg = _attn_prep_bwd(geo, sv["proj"], cos64, sin64, q_gain, k_gain, dq, dkv, "attn_prep_bwd")
            small["q_norm"] = dqg[0, :HEAD_DIM] + dqg[0, HEAD_DIM:]
            small["k_norm"] = dkg[0, :HEAD_DIM] + dkg[0, HEAD_DIM:]
            small["sink"] = dsink[:, 0]
            wq = wts["attn_qkv"]
            dh1 = _mm_nt(dproj, wq, BF16, "attn_qkv_dx", tm, 1024, wq.shape[1])
            dwq = _mm_tn(sv["h1"], dproj, "attn_qkv_dw", 1024, wq.shape[1], tm, out_dtype=BF16)
            big["attn_qkv"] = dwq.reshape(d, N_CHIPS, -1).transpose(1, 0, 2)
        else:
            do = _mm_nt(dmix, wts["ret_o"], BF16, "ret_out_dx", tm, 1024, 1024)
            big["ret_o"] = _mm_tn(sv["o"], dmix, "ret_out_dw", 1024, 1024, tm, out_dtype=BF16).reshape(N_CHIPS, RET_VWIDTH // N_CHIPS, d)
            doraw, dgate, dgn = _ret_gated_bwd(geo, sv["oraw"], sv["proj"], gn_g, do, "ret_gated_bwd")
            small["gn_g"] = dgn[0]
            dq, dk, dv, dlg = _retention_bwd(geo, sv["prep"], log_g, doraw, "ret_bwd")
            small["log_g"] = dlg[:, 0].reshape(2, RET_HEADS)
            dproj = _ret_prep_bwd(geo, dq, dk, dv, dgate, cos256, sin256, "ret_prep_bwd")
            wq = wts["ret_qkvg"]
            dh1 = _mm_nt(dproj, wq, BF16, "ret_qkvg_dx", tm, 1024, wq.shape[2])
            big["ret_qkvg"] = _mm_tn(sv["h1"], dproj, "ret_qkvg_dw", 1024, wq.shape[2], tm, shards=N_CHIPS, out_dtype=BF16)
        below = (saved[0]["f"], saved[0]["mod3"], 5 * d) if i == 1 else None
        dz, dsh1, dsc1, dn1, *below_grads = _norm_mod_bwd(geo, sv["z"], sv["n1"], mod3, 0, dh1, dzmid, f"norm1_bwd{i}", gated=below,
                                                              latent_only=i == 0)
        small[f"norm1_g{i}"], small[f"norm2_g{i}"] = dn1[0], dn2[0]
        parts = [dsh1, dsc1, dg1, dsh2, dsc2, dg2]
        rows = jnp.concatenate([jnp.concatenate([p[:nb, 0, :] for p in parts], axis=1),
                                jnp.concatenate([jnp.sum(p[nb:, 0, :], axis=0, keepdims=True) for p in parts], axis=1),
                                jnp.zeros((MOD_ROWS - nb - 1, 6 * d), F32)], axis=0)
        dmods[i] = rows
        if below_grads:
            df, dg2 = below_grads
        small[f"ada_b{i}"] = jnp.sum(rows, axis=0)
        if plan and i == 1:
            plan.start_layer1(big)
    return loss, dz, big, small, dmods


def _adamw(w, g, m, v, name):
    rows, cols = w.shape
    tr = next((t for t in (256, 128, 64, 32, 16, 8) if rows % t == 0), rows)
    c1 = 1.0 - ADAM_B1 ** ADAM_STEP
    c2 = 1.0 - ADAM_B2 ** ADAM_STEP

    def kern(w_ref, g_ref, m_ref, v_ref, d_ref, nm_ref, nv_ref):
        gv = g_ref[...]
        nm = ADAM_B1 * m_ref[...] + (1.0 - ADAM_B1) * gv
        nv = ADAM_B2 * v_ref[...] + (1.0 - ADAM_B2) * jnp.square(gv)
        d_ref[...] = -ADAM_LR * ((nm / c1) / (jnp.sqrt(nv / c2) + ADAM_EPS) + ADAM_WD * w_ref[...])
        nm_ref[...] = nm
        nv_ref[...] = nv

    spec = pl.BlockSpec((tr, cols), lambda i: (i, 0))
    return pl.pallas_call(
        kern, name=name, grid=(rows // tr,), in_specs=[spec] * 4, out_specs=[spec] * 3,
        out_shape=[jax.ShapeDtypeStruct(w.shape, F32)] * 3, compiler_params=_cparams("parallel"),
    )(w, g, m, v)


N_DEVICES = 8


def _mesh_pos():
    return lax.axis_index("x"), lax.axis_index("y"), lax.axis_index("c")


def _other_chips(x, y):
    return [(1 - x, y), (x, 1 - y), (1 - x, 1 - y)]


def _hbm(n):
    return [pl.BlockSpec(memory_space=pl.ANY)] * n


def _remote(src, dst, send_sem, recv_sem, device):
    return pltpu.make_async_remote_copy(src_ref=src, dst_ref=dst, send_sem=send_sem, recv_sem=recv_sem,
                                        device_id=device, device_id_type=MESH)


def _scalar_spec(grid, in_specs, out_specs):
    return pltpu.PrefetchScalarGridSpec(num_scalar_prefetch=1, grid=grid, in_specs=in_specs, out_specs=out_specs)


def _place_shard(param, layer, pos, name):
    _, r, cols = param.shape
    tr = _slab_tile(r)

    def kern(pos_ref, s_ref, o_ref):
        o_ref[...] = s_ref[...].astype(BF16)

    return pl.pallas_call(
        kern, name=name, out_shape=jax.ShapeDtypeStruct((N_CHIPS, r, cols), BF16),
        grid_spec=_scalar_spec((r // tr,), [pl.BlockSpec((None, tr, cols), lambda i, p: (layer, i, 0))],
                               pl.BlockSpec((None, tr, cols), lambda i, p: (p[1], i, 0))),
        compiler_params=_cparams("parallel"),
    )(pos, param)


class _CommJob:
    def __init__(self, inputs, out_shapes, aliases, sem_shapes, stages, fractions=None):
        self.inputs, self.out_shapes, self.aliases, self.sem_shapes, self.stages = inputs, out_shapes, aliases, sem_shapes, stages
        self.fractions = fractions


def _merge_jobs(a, b):
    assert len(a.stages) == len(b.stages)
    ni, no, ns = len(a.inputs), len(a.out_shapes), len(a.sem_shapes)

    def both(sa, sb):
        def stage(ins, outs, sems):
            sa(ins[:ni], outs[:no], sems[:ns])
            sb(ins[ni:], outs[no:], sems[ns:])
        return stage

    aliases = dict(a.aliases)
    aliases.update({ni + i: no + o for i, o in b.aliases.items()})
    return _CommJob(a.inputs + b.inputs, a.out_shapes + b.out_shapes, aliases, a.sem_shapes + b.sem_shapes,
                    [both(sa, sb) for sa, sb in zip(a.stages, b.stages)])


def _run_job(job, name):
    n_in, n_out = len(job.inputs), len(job.out_shapes)

    def body(*refs):
        for stage in job.stages:
            stage(refs[:n_in], refs[n_in:n_in + n_out], refs[n_in + n_out:])

    return pl.pallas_call(
        body, name=name, in_specs=_hbm(n_in), out_specs=_hbm(n_out), out_shape=job.out_shapes,
        input_output_aliases=job.aliases, scratch_shapes=job.sem_shapes,
    )(*job.inputs)


def _job_marks(job, steps):
    mid = len(job.stages) - 2
    fractions = job.fractions or [(s + 1) / (mid + 1) for s in range(mid)]
    return [0] + [min(steps - 1, 1 + int((steps - 1) * f)) for f in fractions] + [steps - 1]


def _gather_job(placed):
    n = len(placed)

    def half(w, which):
        r2 = placed[w].shape[1] // 2
        return pl.ds(which * r2, r2)

    def ici_copies(outs, sems, slot_of, arrays=range(n)):
        x, y, c = _mesh_pos()
        res = []
        for w in arrays:
            for k, (px, py) in enumerate(_other_chips(x, y)):
                slab = outs[w].at[slot_of(x, y, px, py), half(w, c)]
                res.append((slab, _remote(slab, slab, sems[0].at[w, k], sems[1].at[w, k], (px, py, c))))
        return res

    def forwards(outs, sems, which_core, arrays=range(n)):
        x, y, c = _mesh_pos()
        res = []
        for w in arrays:
            for k, (px, py) in enumerate(_other_chips(x, y)):
                slab = outs[w].at[2 * px + py, half(w, which_core(c))]
                res.append(_remote(slab, slab, sems[2].at[w, k], sems[3].at[w, k], (x, y, 1 - c)))
        return res

    def send(ins, outs, sems):
        for _, cp in ici_copies(outs, sems, lambda x, y, px, py: 2 * x + y):
            cp.start()

    def forward_of(w):
        def forward(ins, outs, sems):
            arrivals = ici_copies(outs, sems, lambda x, y, px, py: 2 * px + py, [w])
            for (_, arrival), fwd in zip(arrivals, forwards(outs, sems, lambda c: c, [w])):
                arrival.wait_recv()
                fwd.start()
        return forward

    def finish(ins, outs, sems):
        for cp in forwards(outs, sems, lambda c: 1 - c):
            cp.wait_recv()
        for _, cp in ici_copies(outs, sems, lambda x, y, px, py: 2 * x + y):
            cp.wait_send()
        for cp in forwards(outs, sems, lambda c: c):
            cp.wait_send()

    sizes = [p.shape[1] * p.shape[2] for p in placed]
    fractions = [sum(sizes[:w + 1]) / sum(sizes) for w in range(n)]
    return _CommJob(list(placed), [jax.ShapeDtypeStruct(p.shape, p.dtype) for p in placed], {w: w for w in range(n)},
                    [pltpu.SemaphoreType.DMA((n, 3))] * 4, [send] + [forward_of(w) for w in range(n)] + [finish], fractions)


def _pair_swap_job(grads):
    n = len(grads)

    def copies(ins, outs, sems):
        x, y, c = _mesh_pos()
        res = []
        for w in range(n):
            r2 = grads[w].shape[1] // 2
            res.append(_remote(ins[w].at[:, pl.ds((1 - c) * r2, r2)], outs[w], sems[0].at[w], sems[1].at[w], (x, y, 1 - c)))
        return res

    def send(ins, outs, sems):
        for cp in copies(ins, outs, sems):
            cp.start()

    def finish(ins, outs, sems):
        for cp in copies(ins, outs, sems):
            cp.wait()

    return _CommJob(list(grads), [jax.ShapeDtypeStruct((N_CHIPS, g.shape[1] // 2, g.shape[2]), g.dtype) for g in grads], {},
                    [pltpu.SemaphoreType.DMA((n,))] * 2, [send, finish])


def _chip_exchange_job(hs):
    n = len(hs)

    def send(ins, outs, sems):
        x, y, c = _mesh_pos()
        for w in range(n):
            for k, (px, py) in enumerate(_other_chips(x, y)):
                _remote(ins[w].at[2 * px + py], outs[w].at[2 * x + y], sems[0].at[w, k], sems[1].at[w, k], (px, py, c)).start()

    def finish(ins, outs, sems):
        x, y, c = _mesh_pos()
        for w in range(n):
            for k, (px, py) in enumerate(_other_chips(x, y)):
                got = outs[w].at[2 * px + py]
                cp = _remote(ins[w].at[2 * px + py], got, sems[0].at[w, k], sems[1].at[w, k], (px, py, c))
                cp.wait_recv()
                cp.wait_send()

    return _CommJob(list(hs), [jax.ShapeDtypeStruct(h.shape, h.dtype) for h in hs], {},
                    [pltpu.SemaphoreType.DMA((n, 3))] * 2, [send, finish])


def _pair_share(ts, name):
    n = len(ts)

    def body(*refs):
        outs = refs[n:2 * n]
        send_sems, recv_sems = refs[2 * n:]
        x, y, c = _mesh_pos()
        sends = []
        for w in range(n):
            r2 = ts[w].shape[0] // 2
            mine = outs[w].at[pl.ds(c * r2, r2)]
            rc = _remote(mine, mine, send_sems.at[w], recv_sems.at[w], (x, y, 1 - c))
            rc.start()
            sends.append(rc)
        for w in range(n):
            r2 = ts[w].shape[0] // 2
            theirs = outs[w].at[pl.ds((1 - c) * r2, r2)]
            _remote(theirs, theirs, send_sems.at[w], recv_sems.at[w], (x, y, 1 - c)).wait_recv()
            sends[w].wait_send()

    return pl.pallas_call(
        body, name=name, in_specs=_hbm(n), out_specs=_hbm(n),
        out_shape=[jax.ShapeDtypeStruct(t.shape, F32) for t in ts],
        input_output_aliases={w: w for w in range(n)},
        scratch_shapes=[pltpu.SemaphoreType.DMA((n,))] * 2,
    )(*ts)


def _slab_tile(rows):
    return next(t for t in (512, 256, 176, 128, 64, 32, 16) if rows % t == 0)


def _sum_pair(grad, land, pos, name):
    _, r2, cols = land.shape
    tr = _slab_tile(r2)
    nt = r2 // tr

    def kern(pos_ref, a_ref, b_ref, o_ref):
        o_ref[...] = (a_ref[...].astype(F32) + b_ref[...].astype(F32)).astype(BF16)

    spec = pl.BlockSpec((None, tr, cols), lambda j, i, p: (j, i, 0))
    return pl.pallas_call(
        kern, name=name, out_shape=jax.ShapeDtypeStruct(land.shape, BF16),
        grid_spec=_scalar_spec((N_CHIPS, nt), [pl.BlockSpec((None, tr, cols), lambda j, i, p: (j, p[0] * nt + i, 0)), spec], spec),
        compiler_params=_cparams("parallel", "parallel"),
    )(pos, grad, land)


def _sum_chips(hs, land, pos, name):
    _, r2, cols = land.shape
    tr = _slab_tile(r2)
    nt = r2 // tr

    def kern(pos_ref, h_ref, l_ref, o_ref):
        acc = jnp.zeros((tr, cols), F32)
        own = h_ref[...].astype(F32)
        for k in range(N_CHIPS):
            acc = acc + jnp.where(pos_ref[1] == k, own, l_ref[k].astype(F32))
        o_ref[...] = acc

    return pl.pallas_call(
        kern, name=name, out_shape=jax.ShapeDtypeStruct((2 * r2, cols), F32),
        grid_spec=_scalar_spec((nt,), [pl.BlockSpec((None, tr, cols), lambda i, p: (p[1], i, 0)),
                                       pl.BlockSpec((N_CHIPS, tr, cols), lambda i, p: (0, i, 0))],
                               pl.BlockSpec((tr, cols), lambda i, p: (p[0] * nt + i, 0))),
        compiler_params=_cparams("parallel"),
    )(pos, hs, land)


class _ReduceScatter:
    def __init__(self, grads, pos, tag):
        self.grads, self.pos, self.tag = list(grads), pos, tag

    def swap_job(self):
        return _pair_swap_job(self.grads)

    def after_swap(self, land):
        self.hs = [_sum_pair(g, l, self.pos, f"grads_pair_sum_{self.tag}{w}") for w, (g, l) in enumerate(zip(self.grads, land))]

    def exchange_job(self):
        return _chip_exchange_job(self.hs)

    def after_exchange(self, land2):
        return [_sum_chips(h, l, self.pos, f"grads_chip_sum_{self.tag}{w}") for w, (h, l) in enumerate(zip(self.hs, land2))]

    def run(self):
        self.after_swap(_run_job(self.swap_job(), f"grads_pair_swap_{self.tag}"))
        return self.after_exchange(_run_job(self.exchange_job(), f"grads_chip_exchange_{self.tag}"))


EARLY_WEIGHTS = ("attn_qkv",)
LATE_WEIGHTS = ("ffn_in0", "ffn_in1", "ffn_out0", "ffn_out1", "attn_o", "ret_qkvg", "ret_o")
LAYER1_GRADS = ("ffn_out1", "ffn_in1", "ret_o", "ret_qkvg")
LAYER0_FFN_GRADS = ("ffn_out0", "ffn_in0")
LAST_GRADS = ("attn_o", "attn_qkv")


def _fill_weights(wts, full):
    for name, w in full.items():
        if name[:-1] == "ffn_in":
            wts[name[:-1]][int(name[-1])] = w
        elif name[:-1] == "ffn_out":
            wts["ffn_out"][int(name[-1])] = w.reshape(-1, w.shape[2])
        elif name in ("attn_o", "ret_o"):
            wts[name] = w.reshape(-1, w.shape[2])
        elif name == "attn_qkv":
            wts[name] = w.transpose(1, 0, 2).reshape(w.shape[1], -1)
        else:
            wts[name] = w


class _StepPlan:
    def __init__(self, placed, pos):
        self.placed, self.pos = placed, pos
        self.layer1 = self.layer0_ffn = None
        self.reduced = {}

    def gather_job(self):
        return _gather_job([self.placed[k] for k in LATE_WEIGHTS])

    def late_weights(self, outs, wts):
        _fill_weights(wts, dict(zip(LATE_WEIGHTS, outs)))

    def start_layer1(self, big):
        self.layer1 = _ReduceScatter([big[k] for k in LAYER1_GRADS], self.pos, "l1_")

    def start_layer0_ffn(self, big):
        self.layer0_ffn = _ReduceScatter([big[k] for k in LAYER0_FFN_GRADS], self.pos, "l0f_")

    def exchange_job(self):
        return _merge_jobs(self.layer1.exchange_job(), self.layer0_ffn.exchange_job())

    def after_exchange(self, land):
        n1 = len(LAYER1_GRADS)
        self.reduced.update(zip(LAYER1_GRADS, self.layer1.after_exchange(land[:n1])))
        self.reduced.update(zip(LAYER0_FFN_GRADS, self.layer0_ffn.after_exchange(land[n1:])))


def _all_reduce_small(v, name):
    def body(v_ref, o_ref, land_ref, send_sems, recv_sems):
        x, y, c = _mesh_pos()
        me = 4 * x + 2 * y + c
        land_ref[me] = v_ref[...]
        for t in range(N_DEVICES):
            @pl.when(t != me)
            def _(t=t):
                _remote(v_ref, land_ref.at[me], send_sems.at[t], recv_sems.at[me], (t // 4, (t // 2) % 2, t % 2)).start()
        for t in range(N_DEVICES):
            @pl.when(t != me)
            def _(t=t):
                _remote(v_ref, land_ref.at[t], send_sems.at[t], recv_sems.at[t], (t // 4, (t // 2) % 2, t % 2)).wait()
        acc = land_ref[0]
        for t in range(1, N_DEVICES):
            acc = acc + land_ref[t]
        o_ref[...] = acc

    vmem = pl.BlockSpec(memory_space=pltpu.VMEM)
    return pl.pallas_call(
        body, name=name, in_specs=[vmem], out_specs=vmem, out_shape=jax.ShapeDtypeStruct(v.shape, F32),
        scratch_shapes=[pltpu.VMEM((N_DEVICES,) + v.shape, F32), pltpu.SemaphoreType.DMA((N_DEVICES,)),
                        pltpu.SemaphoreType.DMA((N_DEVICES,))],
    )(v)


def _all_to_all_small(v, name):
    def body(v_ref, o_ref, send_sems, recv_sems):
        x, y, c = _mesh_pos()
        me = 4 * x + 2 * y + c
        o_ref[me] = v_ref[me]
        for t in range(N_DEVICES):
            @pl.when(t != me)
            def _(t=t):
                _remote(v_ref.at[t], o_ref.at[me], send_sems.at[t], recv_sems.at[me], (t // 4, (t // 2) % 2, t % 2)).start()
        for t in range(N_DEVICES):
            @pl.when(t != me)
            def _(t=t):
                _remote(v_ref.at[t], o_ref.at[t], send_sems.at[t], recv_sems.at[t], (t // 4, (t // 2) % 2, t % 2)).wait()

    vmem = pl.BlockSpec(memory_space=pltpu.VMEM)
    return pl.pallas_call(
        body, name=name, in_specs=[vmem], out_specs=vmem, out_shape=jax.ShapeDtypeStruct(v.shape, F32),
        scratch_shapes=[pltpu.SemaphoreType.DMA((N_DEVICES,)), pltpu.SemaphoreType.DMA((N_DEVICES,))],
    )(v)


ALL_ROWS = 40


class _AdaLN:
    def __init__(self, c, c_ctx, ada_w, ada_b, riders):
        xi, yi, ci = _mesh_pos()
        self.me, self.chip, self.core = 4 * xi + 2 * yi + ci, 2 * xi + yi, ci
        self.nb, d = c.shape
        self.ada_w, self.c_ctx = ada_w, c_ctx
        self.cols = ada_w.shape[2]
        ctx_row = self.nb * N_DEVICES
        assert ctx_row + 1 + riders.shape[0] <= ALL_ROWS
        placed = lax.dynamic_update_slice(jnp.zeros((ALL_ROWS, d), F32), c, (self.me * self.nb, 0))
        placed = lax.dynamic_update_slice(placed, riders, (ctx_row + 1, 0))
        summed = _all_reduce_small(placed, "gather_conditioning")
        self.riders = summed[ctx_row + 1:ctx_row + 1 + riders.shape[0]]
        c_all = summed.at[ctx_row].set(c_ctx)
        self.cact, = _whole("cond_silu", lambda v: (_silu(v),), [jax.ShapeDtypeStruct(c_all.shape, F32)], c_all)
        parts = []
        for i in range(2):
            bias = lax.dynamic_slice(ada_b[i], (self.chip * self.cols,), (self.cols,))[None, :]
            parts.append(_mm_nn(self.cact, ada_w[i], F32, f"mod{i}", ALL_ROWS, self.cols, d, bias=bias))
        part = jnp.concatenate(parts, axis=1)
        rows = [[t * self.nb + b for b in range(self.nb)] + [ctx_row] * (MOD_ROWS - self.nb) for t in range(N_DEVICES)]
        got = _all_to_all_small(part[jnp.asarray(rows)], "mod_exchange")
        self.mods = [jnp.concatenate([got[2 * j][:self.nb + 1, i * self.cols:(i + 1) * self.cols] for j in range(N_CHIPS)], axis=1)[:, None, :]
                     for i in range(2)]

    def backward(self, dmods):
        nb, cols, d = self.nb, self.cols, self.ada_w.shape[1]
        blocks = [jnp.concatenate([dm[:, j * cols:(j + 1) * cols] for dm in dmods], axis=1) for j in range(N_CHIPS)]
        got = _all_to_all_small(jnp.stack([blocks[t // 2] for t in range(N_DEVICES)]), "dmod_exchange")
        dall = jnp.concatenate([got[:, :nb].reshape(N_DEVICES * nb, 2 * cols), jnp.sum(got[:, nb], axis=0, keepdims=True),
                                jnp.zeros((ALL_ROWS - N_DEVICES * nb - 1, 2 * cols), F32)], axis=0)
        dctx = jnp.concatenate([dall[N_DEVICES * nb][None, :], jnp.zeros((MOD_ROWS - 1, 2 * cols), F32)], axis=0)
        grads, dcact = [], []
        for i in range(2):
            grads.append(_mm_tn(self.cact, dall[:, i * cols:(i + 1) * cols], f"ada_dw{i}", d, cols, ALL_ROWS))
            dcact.append(_mm_nt(dctx[:, i * cols:(i + 1) * cols], self.ada_w[i], F32, f"ada_dx{i}", MOD_ROWS, d, cols))

        def silu_bwd(v, d0, d1):
            sg = _sigmoid(v)
            return ((d0 + d1)[0:1] * (sg * (1.0 + v * (1.0 - sg))),)

        dc_ctx, = _whole("cond_silu_bwd", silu_bwd, [jax.ShapeDtypeStruct((1, d), F32)], self.c_ctx[None, :], dcact[0], dcact[1])
        return grads, jnp.where(self.core == 0, dc_ctx[0], jnp.zeros((d,), F32))


SMALL_ROWS = 24


def _pack_small(small, dlogit):
    d = D_MODEL
    misc = jnp.zeros((d,), F32)
    misc = misc.at[0:HEAD_DIM].set(small["q_norm"]).at[128:128 + HEAD_DIM].set(small["k_norm"])
    misc = misc.at[256:256 + N_HEADS].set(small["sink"]).at[384:384 + 2 * RET_HEADS].set(dlogit.reshape(-1))
    rows = [small["ada_b0"].reshape(6, d), small["ada_b1"].reshape(6, d), small["norm1_g0"][None], small["norm1_g1"][None],
            small["norm2_g0"][None], small["norm2_g1"][None], small["c_ctx"][None], small["gn_g"].reshape(2, d), misc[None]]
    buf = jnp.concatenate(rows, axis=0)
    return jnp.concatenate([buf, jnp.zeros((SMALL_ROWS - buf.shape[0], d), F32)], axis=0)


def _unpack_small(buf):
    d = D_MODEL
    misc = buf[19]
    return dict(ada_b=buf[0:12].reshape(2, 6 * d), norm1_g=buf[12:14], norm2_g=buf[14:16], c_ctx=buf[16],
                gn_g=buf[17:19].reshape(2 * d), q_norm=misc[0:HEAD_DIM], k_norm=misc[128:128 + HEAD_DIM],
                sink=misc[256:256 + N_HEADS], decay=misc[384:384 + 2 * RET_HEADS])


def kernel(x, c, ctx, c_ctx, ada_w, ada_b, norm1_g, norm2_g, ffn_w_in, ffn_w_out, attn_w_qkv, attn_q_norm, attn_k_norm, attn_sink, attn_w_o, ret_w_qkvg, ret_decay_logit, ret_gn_g, ret_w_o, loss_target, m_c_ctx, m_ada_w, m_ada_b, m_norm1_g, m_norm2_g, m_ffn_w_in, m_ffn_w_out, m_attn_w_qkv, m_attn_q_norm, m_attn_k_norm, m_attn_sink, m_attn_w_o, m_ret_w_qkvg, m_ret_decay_logit, m_ret_gn_g, m_ret_w_o, v_c_ctx, v_ada_w, v_ada_b, v_norm1_g, v_norm2_g, v_ffn_w_in, v_ffn_w_out, v_attn_w_qkv, v_attn_q_norm, v_attn_k_norm, v_attn_sink, v_attn_w_o, v_ret_w_qkvg, v_ret_decay_logit, v_ret_gn_g, v_ret_w_o):
    xi, yi, ci = _mesh_pos()
    chip = 2 * xi + yi
    nb, s, d = x.shape
    gn_shard = ret_gn_g.shape[1]

    shards = dict(ffn_in0=(ffn_w_in, 0), ffn_in1=(ffn_w_in, 1), ffn_out0=(ffn_w_out, 0),
                  ffn_out1=(ffn_w_out, 1), attn_qkv=(attn_w_qkv, 0), attn_o=(attn_w_o, 0), ret_qkvg=(ret_w_qkvg, 0), ret_o=(ret_w_o, 0))
    names = list(shards)
    pos = jnp.stack([ci, chip]).astype(jnp.int32)
    placed = {k: _place_shard(*shards[k], pos, f"place_{k}") for k in names}
    early = _run_job(_gather_job([placed[k] for k in EARLY_WEIGHTS]), "gather_early_weights")
    gn_mine = jnp.where(ci == 0, ret_gn_g[0], jnp.zeros_like(ret_gn_g[0]))
    gn_place = lax.dynamic_update_slice(jnp.zeros((RET_VWIDTH,), F32), gn_mine, (chip * gn_shard,))

    wts = dict(ffn_in=[None, None], ffn_out=[None, None], attn_qkv=None, attn_o=None, ret_qkvg=None, ret_o=None)
    ada = _AdaLN(c, c_ctx, ada_w, ada_b, riders=gn_place.reshape(2, d))
    gn_full = ada.riders.reshape(RET_VWIDTH)
    _fill_weights(wts, dict(zip(EARLY_WEIGHTS, early)))
    plan = _StepPlan(placed, pos)
    decay_logit = ret_decay_logit[0]
    sp = dict(norm1_g=norm1_g, norm2_g=norm2_g, q_norm=attn_q_norm[0], k_norm=attn_k_norm[0],
              sink=attn_sink[0], log_g=jax.nn.log_sigmoid(decay_logit), gn_g=gn_full)
    loss_part, dz, big, small, dmods = _local_step(x, ctx, loss_target, sp, wts, ada.mods, plan)
    ada_grads, small["c_ctx"] = ada.backward(dmods)

    loss = lax.psum(loss_part[0, 0], ("x", "y", "c"))
    grad_x = dz.reshape(nb, s, d)

    dlogit = small["log_g"] * jax.nn.sigmoid(-decay_logit)
    sg = _unpack_small(_all_reduce_small(_pack_small(small, dlogit), "reduce_small_grads"))
    halves = dict(plan.reduced)
    halves.update(zip(LAST_GRADS, _ReduceScatter([big[k] for k in LAST_GRADS], pos, "last_").run()))
    reduced = dict(zip(halves, _pair_share(list(halves.values()), "grads_pair_share")))

    grads = dict(
        c_ctx=sg["c_ctx"], ada_w=jnp.stack(ada_grads), ada_b=sg["ada_b"], norm1_g=sg["norm1_g"],
        norm2_g=sg["norm2_g"], ffn_w_in=jnp.stack([reduced["ffn_in0"], reduced["ffn_in1"]]),
        ffn_w_out=jnp.stack([reduced["ffn_out0"], reduced["ffn_out1"]]), attn_w_qkv=reduced["attn_qkv"][None],
        attn_q_norm=sg["q_norm"][None], attn_k_norm=sg["k_norm"][None], attn_sink=sg["sink"][None],
        attn_w_o=reduced["attn_o"][None], ret_w_qkvg=reduced["ret_qkvg"][None], ret_decay_logit=sg["decay"].reshape(1, 2, RET_HEADS),
        ret_gn_g=lax.dynamic_slice(sg["gn_g"], (chip * gn_shard,), (gn_shard,))[None], ret_w_o=reduced["ret_o"][None])
    params = dict(c_ctx=(c_ctx, m_c_ctx, v_c_ctx), ada_w=(ada_w, m_ada_w, v_ada_w), ada_b=(ada_b, m_ada_b, v_ada_b),
                  norm1_g=(norm1_g, m_norm1_g, v_norm1_g), norm2_g=(norm2_g, m_norm2_g, v_norm2_g),
                  ffn_w_in=(ffn_w_in, m_ffn_w_in, v_ffn_w_in), ffn_w_out=(ffn_w_out, m_ffn_w_out, v_ffn_w_out),
                  attn_w_qkv=(attn_w_qkv, m_attn_w_qkv, v_attn_w_qkv), attn_q_norm=(attn_q_norm, m_attn_q_norm, v_attn_q_norm),
                  attn_k_norm=(attn_k_norm, m_attn_k_norm, v_attn_k_norm), attn_sink=(attn_sink, m_attn_sink, v_attn_sink),
                  attn_w_o=(attn_w_o, m_attn_w_o, v_attn_w_o), ret_w_qkvg=(ret_w_qkvg, m_ret_w_qkvg, v_ret_w_qkvg),
                  ret_decay_logit=(ret_decay_logit, m_ret_decay_logit, v_ret_decay_logit),
                  ret_gn_g=(ret_gn_g, m_ret_gn_g, v_ret_gn_g), ret_w_o=(ret_w_o, m_ret_w_o, v_ret_w_o))
    order = list(params)
    deltas, new_m, new_v = [], [], []
    for k in order:
        w, m, v = params[k]
        g = grads[k].reshape(w.shape)
        grads[k] = g
        flat = (-1, w.shape[-1]) if w.ndim > 1 else (1, -1)
        if k == "ret_decay_logit":
            flat = (1, -1)
        dw, nm, nv = _adamw(w.reshape(flat), g.reshape(flat), m.reshape(flat), v.reshape(flat), f"adamw_{k}")
        deltas.append(dw.reshape(w.shape))
        new_m.append(nm.reshape(w.shape))
        new_v.append(nv.reshape(w.shape))
    return (loss, grad_x, *[grads[k] for k in order], *deltas, *new_m, *new_v)
```

```python
import functools

import jax
import jax.numpy as jnp
from jax import lax
from jax.experimental import pallas as pl
from jax.experimental.pallas import tpu as pltpu

F32 = jnp.float32
BF16 = jnp.bfloat16

D_MODEL = 1024
N_HEADS = 16
N_KV_HEADS = 4
HEAD_DIM = 64
WINDOW = 128
ATTN_BLOCK = 128
BAND = ATTN_BLOCK + 2 * WINDOW
RET_HEADS = 4
RET_QK_DIM = 256
RET_V_DIM = 512
RET_VWIDTH = 2048
RET_CHUNK = 128
D_FF = 2816
GRID_W = 64
ROPE_BASE = 10000.0
EPS = 1e-6
NEG_INF = -1e30
LANES = 128

ADAM_LR = 0.001
ADAM_B1 = 0.9
ADAM_B2 = 0.999
ADAM_EPS = 1e-08
ADAM_WD = 0.01
ADAM_STEP = 10

VMEM_LIMIT_BYTES = 56 * 1024 * 1024
MESH = pl.DeviceIdType.MESH
N_CHIPS = 4


def _cparams(*sem):
    return pltpu.CompilerParams(dimension_semantics=sem, vmem_limit_bytes=VMEM_LIMIT_BYTES)


_DIMS = {"nn": ((1,), (0,)), "nt": ((1,), (1,)), "tn": ((0,), (0,))}


def _dot(a, b, form):
    return lax.dot_general(a.astype(BF16), b.astype(BF16), (_DIMS[form], ((), ())), preferred_element_type=F32)


@functools.partial(jax.custom_vjp, nondiff_argnums=(2,))
def _mm(a, b, form):
    return _dot(a, b, form)


def _mm_fwd(a, b, form):
    return _dot(a, b, form), (a, b)


def _mm_bwd(form, res, ct):
    a, b = res
    if form == "nn":
        da, db = _dot(ct, b, "nt"), _dot(a, ct, "tn")
    elif form == "nt":
        da, db = _dot(ct, b, "nn"), _dot(ct, a, "tn")
    else:
        da, db = _dot(b, ct, "nt"), _dot(a, ct, "nn")
    return da.astype(a.dtype), db.astype(b.dtype)


_mm.defvjp(_mm_fwd, _mm_bwd)


def _swap_halves(x, half):
    w = x.shape[-1]
    lane = lax.broadcasted_iota(jnp.int32, x.shape, x.ndim - 1)
    return jnp.where(lane % (2 * half) < half, pltpu.roll(x, w - half, x.ndim - 1), pltpu.roll(x, half, x.ndim - 1))


@functools.partial(jax.custom_vjp, nondiff_argnums=(1,))
def _rot(x, half):
    return _swap_halves(x, half)


def _rot_fwd(x, half):
    return _swap_halves(x, half), None


def _rot_bwd(half, _, ct):
    return (_swap_halves(ct, half),)


_rot.defvjp(_rot_fwd, _rot_bwd)


def _rope(x, cos, sin_signed, half):
    return x * cos + _rot(x, half) * sin_signed


def _head_mean_square(x):
    r = lax.broadcasted_iota(jnp.int32, (LANES, LANES), 0) // HEAD_DIM
    c = lax.broadcasted_iota(jnp.int32, (LANES, LANES), 1) // HEAD_DIM
    g = jnp.where(r == c, 1.0 / HEAD_DIM, 0.0).astype(F32)
    return jnp.dot(x * x, g, precision=lax.Precision.HIGHEST, preferred_element_type=F32)


def _qk_chunk(x, gain, cos, sin_signed, scale):
    y = x * lax.rsqrt(_head_mean_square(x) + EPS) * gain
    return _rope(y, cos, sin_signed, HEAD_DIM // 4) * scale


def _sigmoid(x):
    return 1.0 / (1.0 + jnp.exp(-x))


def _silu(x):
    return x * _sigmoid(x)


def _mm_nn(a, w, out_dtype, name, tm, tn, tk, bias=None):
    m, k_dim = a.shape
    if w.ndim == 3:
        n = w.shape[0] * w.shape[2]
        per = w.shape[2] // tn
        assert w.shape[2] % tn == 0
        w_spec = pl.BlockSpec((None, tk, tn), lambda i, j, k: (j // per, k, j % per))
    else:
        n = w.shape[1]
        w_spec = pl.BlockSpec((tk, tn), lambda i, j, k: (k, j))
    assert m % tm == 0 and n % tn == 0 and k_dim % tk == 0, (name, a.shape, w.shape, tm, tn, tk)
    nk = k_dim // tk
    has_bias = bias is not None

    def body(*refs):
        a_ref, w_ref = refs[0], refs[1]
        b_ref = refs[2] if has_bias else None
        o_ref, acc_ref = (refs[-1], None) if nk == 1 else (refs[-2], refs[-1])
        if nk == 1:
            part = jnp.dot(a_ref[...].astype(BF16), w_ref[...].astype(BF16), preferred_element_type=F32)
            o_ref[...] = (part + b_ref[...] if has_bias else part).astype(out_dtype)
            return
        k = pl.program_id(2)

        @pl.when(k == 0)
        def _():
            acc_ref[...] = jnp.zeros_like(acc_ref)

        acc_ref[...] += jnp.dot(a_ref[...].astype(BF16), w_ref[...].astype(BF16), preferred_element_type=F32)

        @pl.when(k == nk - 1)
        def _():
            r = acc_ref[...]
            if has_bias:
                r = r + b_ref[...]
            o_ref[...] = r.astype(out_dtype)

    in_specs = [pl.BlockSpec((tm, tk), lambda i, j, k: (i, k)), w_spec]
    args = [a, w]
    if has_bias:
        in_specs.append(pl.BlockSpec((1, tn), lambda i, j, k: (0, j)))
        args.append(bias)
    return pl.pallas_call(
        body, name=name, grid=(m // tm, n // tn, nk), in_specs=in_specs,
        out_specs=pl.BlockSpec((tm, tn), lambda i, j, k: (i, j)),
        out_shape=jax.ShapeDtypeStruct((m, n), out_dtype),
        scratch_shapes=[pltpu.VMEM((tm, tn), F32)] if nk > 1 else [],
        compiler_params=_cparams("parallel", "parallel", "arbitrary"),
    )(*args)


def _mm_nt(a, w, out_dtype, name, tm, tn, tk):
    if a.ndim == 3:
        planes, m, plane_w = a.shape
        c_dim = planes * plane_w
        a_per = plane_w // tk
        assert plane_w % tk == 0
        a_spec = pl.BlockSpec((None, tm, tk), lambda i, j, k: (k // a_per, i, k % a_per))
    else:
        m, c_dim = a.shape
        a_spec = pl.BlockSpec((tm, tk), lambda i, j, k: (i, k))
    if w.ndim == 3:
        k_out = w.shape[1]
        per = w.shape[2] // tk
        assert w.shape[2] % tk == 0 and w.shape[0] * w.shape[2] == c_dim
        w_spec = pl.BlockSpec((None, tn, tk), lambda i, j, k: (k // per, j, k % per))
    else:
        k_out = w.shape[0]
        assert w.shape[1] == c_dim
        w_spec = pl.BlockSpec((tn, tk), lambda i, j, k: (j, k))
    assert m % tm == 0 and k_out % tn == 0 and c_dim % tk == 0, (name, a.shape, w.shape, tm, tn, tk)
    nk = c_dim // tk

    def body(a_ref, w_ref, o_ref, acc_ref=None):
        if nk == 1:
            o_ref[...] = _dot(a_ref[...], w_ref[...], "nt").astype(out_dtype)
            return
        k = pl.program_id(2)

        @pl.when(k == 0)
        def _():
            acc_ref[...] = jnp.zeros_like(acc_ref)

        acc_ref[...] += _dot(a_ref[...], w_ref[...], "nt")

        @pl.when(k == nk - 1)
        def _():
            o_ref[...] = acc_ref[...].astype(out_dtype)

    return pl.pallas_call(
        body, name=name, grid=(m // tm, k_out // tn, nk),
        in_specs=[a_spec, w_spec],
        out_specs=pl.BlockSpec((tm, tn), lambda i, j, k: (i, j)),
        out_shape=jax.ShapeDtypeStruct((m, k_out), out_dtype),
        scratch_shapes=[pltpu.VMEM((tm, tn), F32)] if nk > 1 else [],
        compiler_params=_cparams("parallel", "parallel", "arbitrary"),
    )(a, w)


def _mm_tn(a, b, name, tm, tn, tk, shards=None, out_dtype=F32):
    r, k_dim = a.shape
    if b.ndim == 3:
        n = b.shape[0] * b.shape[2]
        b_per = b.shape[2] // tn
        assert b.shape[2] % tn == 0
        b_spec = pl.BlockSpec((None, tk, tn), lambda i, j, k: (j // b_per, k, j % b_per))
    else:
        n = b.shape[1]
        b_spec = pl.BlockSpec((tk, tn), lambda i, j, k: (k, j))
    assert r % tk == 0 and k_dim % tm == 0 and n % tn == 0, (name, a.shape, b.shape, tm, tn, tk)
    nk = r // tk
    if shards:
        per = n // shards // tn
        assert n % (shards * tn) == 0
        out_shape = jax.ShapeDtypeStruct((shards, k_dim, n // shards), out_dtype)
        out_spec = pl.BlockSpec((None, tm, tn), lambda i, j, k: (j // per, i, j % per))
    else:
        out_shape = jax.ShapeDtypeStruct((k_dim, n), out_dtype)
        out_spec = pl.BlockSpec((tm, tn), lambda i, j, k: (i, j))
    direct = out_dtype == F32

    def body(a_ref, b_ref, o_ref, *scratch):
        acc_ref = o_ref if direct else scratch[0]
        k = pl.program_id(2)

        @pl.when(k == 0)
        def _():
            acc_ref[...] = jnp.zeros_like(acc_ref)

        acc_ref[...] += _dot(a_ref[...], b_ref[...], "tn")
        if not direct:
            @pl.when(k == nk - 1)
            def _():
                o_ref[...] = acc_ref[...].astype(out_dtype)

    return pl.pallas_call(
        body, name=name, grid=(k_dim // tm, n // tn, nk),
        in_specs=[pl.BlockSpec((tk, tm), lambda i, j, k: (k, i)), b_spec],
        out_specs=out_spec, out_shape=out_shape,
        scratch_shapes=[] if direct else [pltpu.VMEM((tm, tn), F32)],
        compiler_params=_cparams("parallel", "parallel", "arbitrary"),
    )(a, b)


class _Carrier:
    def __init__(self, job, n_in, n_out, n_scratch):
        self.job, self.n_in, self.n_out, self.n_scratch = job, n_in, n_out, n_scratch
        self.ji = len(job.inputs) if job else 0
        self.jo = len(job.out_shapes) if job else 0

    def operands(self):
        return list(self.job.inputs) if self.job else []

    def in_specs(self):
        return [pl.BlockSpec(memory_space=pl.ANY)] * self.ji

    def out_specs(self):
        return [pl.BlockSpec(memory_space=pl.ANY)] * self.jo

    def out_shapes(self):
        return list(self.job.out_shapes) if self.job else []

    def scratch(self):
        return list(self.job.sem_shapes) if self.job else []

    def aliases(self):
        return {self.n_in + a: self.n_out + b for a, b in self.job.aliases.items()} if self.job else {}

    def split(self, refs):
        a = self.n_in
        b = a + self.ji
        c = b + self.n_out
        d = c + self.jo
        e = d + self.n_scratch
        return list(refs[:a]) + list(refs[b:c]) + list(refs[d:e]), (refs[a:b], refs[c:d], refs[e:])

    def run(self, job_refs, step, steps):
        if not self.job:
            return
        for stage, mark in zip(self.job.stages, _job_marks(self.job, steps)):
            pl.when(step == mark)(functools.partial(stage, *job_refs))

    def results(self, res):
        res = list(res)
        return res[:self.n_out], res[self.n_out:]


FFN_ROW_TILE = 768


def _ffn_tile(r):
    return FFN_ROW_TILE if r % FFN_ROW_TILE == 0 else _row_tile(r)


def _ffn_in_swiglu(h, w, name):
    r, k_dim = h.shape
    n4 = w.shape[2]
    tm = _ffn_tile(r)

    def body(h_ref, wg_ref, wu_ref, u_ref, a_ref):
        hv = h_ref[...]
        g = jnp.dot(hv, wg_ref[...], preferred_element_type=F32)
        up = jnp.dot(hv, wu_ref[...], preferred_element_type=F32)
        u_ref[0] = g.astype(BF16)
        u_ref[1] = up.astype(BF16)
        a_ref[...] = (_silu(g) * up).astype(BF16)

    return pl.pallas_call(
        body, name=name, grid=(r // tm, 2),
        in_specs=[pl.BlockSpec((tm, k_dim), lambda i, j: (i, 0)),
                  pl.BlockSpec((None, k_dim, n4), lambda i, j: (j, 0, 0)),
                  pl.BlockSpec((None, k_dim, n4), lambda i, j: (j + 2, 0, 0))],
        out_specs=[pl.BlockSpec((2, tm, n4), lambda i, j: (0, i, j)), pl.BlockSpec((tm, n4), lambda i, j: (i, j))],
        out_shape=[jax.ShapeDtypeStruct((2, r, 2 * n4), BF16), jax.ShapeDtypeStruct((r, 2 * n4), BF16)],
        compiler_params=_cparams("parallel", "parallel"),
    )(h, w, w)


def _mm_nn_gate_residual(geo, a, w, z, mod, off, name, norm=None):
    r, k_dim = a.shape
    n = w.shape[1]
    tm = FFN_ROW_TILE if geo.seg % FFN_ROW_TILE == 0 else 256
    tiles = geo.seg // tm
    assert geo.seg % tm == 0 and r == geo.r and n == D_MODEL

    def body(a_ref, w_ref, z_ref, mx_ref, mc_ref, *rest):
        out = jnp.dot(a_ref[...], w_ref[...], preferred_element_type=F32)
        is_x = (pl.program_id(0) % tiles) * tm + lax.broadcasted_iota(jnp.int32, (tm, 1), 0) < geo.s
        zo = z_ref[...] + jnp.where(is_x, mx_ref[:, off:off + n], mc_ref[:, off:off + n]) * out
        if norm:
            g_ref, nx_ref, nc_ref, zo_ref, raw_ref, h_ref = rest
            no = norm[2]
            shift = jnp.where(is_x, nx_ref[:, no:no + n], nc_ref[:, no:no + n])
            scale = jnp.where(is_x, nx_ref[:, no + n:no + 2 * n], nc_ref[:, no + n:no + 2 * n])
            rs = lax.rsqrt(jnp.mean(zo * zo, axis=-1, keepdims=True) + EPS)
            h_ref[...] = ((zo * rs) * g_ref[...] * (1.0 + scale) + shift).astype(BF16)
        else:
            zo_ref, raw_ref = rest
        zo_ref[...] = zo
        raw_ref[...] = out.astype(BF16)

    def mod_specs(m):
        return [pl.BlockSpec((None, 1, m.shape[2]), lambda i: (i // tiles, 0, 0)), pl.BlockSpec((None, 1, m.shape[2]), lambda i: (geo.b, 0, 0))]

    row = pl.BlockSpec((tm, n), lambda i: (i, 0))
    in_specs = [pl.BlockSpec((tm, k_dim), lambda i: (i, 0)), pl.BlockSpec((k_dim, n), lambda i: (0, 0)), row] + mod_specs(mod)
    args = [a, w, z, mod, mod]
    out_specs, out_shape = [row, row], [jax.ShapeDtypeStruct((r, n), F32), jax.ShapeDtypeStruct((r, n), BF16)]
    if norm:
        in_specs += [pl.BlockSpec((1, n), lambda i: (0, 0))] + mod_specs(norm[1])
        args += [norm[0], norm[1], norm[1]]
        out_specs.append(row)
        out_shape.append(jax.ShapeDtypeStruct((r, n), BF16))
    res = pl.pallas_call(body, name=name, grid=(r // tm,), in_specs=in_specs, out_specs=out_specs, out_shape=out_shape,
                         compiler_params=_cparams("parallel"))(*args)
    return res if norm else (*res, None)


def _ffn_out_dx_swiglu_bwd(df, w_out, u, name, job=None):
    r, d = df.shape
    n4 = u.shape[2] // 2
    tm = _ffn_tile(r)
    carrier = _Carrier(job, 3, 1, 0)
    steps = (r // tm) * 2

    def body(*refs):
        (df_ref, w_ref, u_ref, du_ref), job_refs = carrier.split(refs)
        carrier.run(job_refs, pl.program_id(0) * 2 + pl.program_id(1), steps)
        da = _dot(df_ref[...], w_ref[...], "nt")
        g, up = u_ref[0].astype(F32), u_ref[1].astype(F32)
        s = _sigmoid(g)
        du_ref[0] = (da * up * (s * (1.0 + g * (1.0 - s)))).astype(BF16)
        du_ref[1] = (da * (g * s)).astype(BF16)

    res = pl.pallas_call(
        body, name=name, grid=(r // tm, 2),
        in_specs=[pl.BlockSpec((tm, d), lambda i, j: (i, 0)), pl.BlockSpec((n4, d), lambda i, j: (j, 0)),
                  pl.BlockSpec((2, tm, n4), lambda i, j: (0, i, j))] + carrier.in_specs(),
        out_specs=[pl.BlockSpec((2, tm, n4), lambda i, j: (0, i, j))] + carrier.out_specs(),
        out_shape=[jax.ShapeDtypeStruct(u.shape, BF16)] + carrier.out_shapes(),
        scratch_shapes=carrier.scratch(), input_output_aliases=carrier.aliases(),
        compiler_params=_cparams("arbitrary", "arbitrary"),
    )(df, w_out, u, *carrier.operands())
    (du,), extra = carrier.results(res)
    return du, extra


class _Rows:
    def __init__(self, b, s, l):
        self.b, self.s, self.l = b, s, l
        self.seg = s + l
        self.r = b * self.seg


def _rowwise(name, body, geo, tm, ins, outs, job=None):
    seg_blocks, x_blocks = geo.seg // tm, geo.s // tm
    per_part = {"ex", "xrow", "crow"} & {k for _, k in ins if isinstance(k, str)} or {"exacc", "xrow"} & {o[0] for o in outs}
    assert geo.seg % tm == 0 and (geo.s % tm == 0 or not per_part), (name, tm)
    nb = geo.b

    def is_ctx(i):
        return i % seg_blocks >= x_blocks

    in_specs, args = [], []
    for arr, kind in ins:
        args.append(arr)
        if kind == "row":
            in_specs.append(pl.BlockSpec((tm, arr.shape[1]), lambda i: (i, 0)))
        elif kind == "ex":
            in_specs.append(pl.BlockSpec((None, 1, arr.shape[2]), lambda i: (jnp.where(is_ctx(i), nb, i // seg_blocks), 0, 0)))
        elif kind == "full":
            in_specs.append(pl.BlockSpec(arr.shape, lambda i, nd=arr.ndim: (0,) * nd))
        elif kind == "tab":
            in_specs.append(pl.BlockSpec((tm, arr.shape[1]), lambda i: (i % seg_blocks, 0)))
        elif kind == "xrow":
            in_specs.append(pl.BlockSpec(
                (tm, arr.shape[1]), lambda i: ((i // seg_blocks) * x_blocks + jnp.minimum(i % seg_blocks, x_blocks - 1), 0)))
        elif kind == "crow":
            c_blocks = seg_blocks - x_blocks
            in_specs.append(pl.BlockSpec(
                (tm, arr.shape[1]), lambda i: ((i // seg_blocks) * c_blocks + jnp.maximum(i % seg_blocks - x_blocks, 0), 0)))
        else:
            _, width, cb = kind
            in_specs.append(pl.BlockSpec((tm, width), lambda i, cb=cb: (i, cb)))
    out_specs, out_shapes = [], []
    for o in outs:
        if o[0] == "row":
            out_specs.append(pl.BlockSpec((tm, o[1]), lambda i: (i, 0)))
            out_shapes.append(jax.ShapeDtypeStruct((geo.r, o[1]), o[2]))
        elif o[0] == "xrow":
            out_specs.append(pl.BlockSpec(
                (tm, o[1]), lambda i: ((i // seg_blocks) * x_blocks + jnp.minimum(i % seg_blocks, x_blocks - 1), 0)))
            out_shapes.append(jax.ShapeDtypeStruct((geo.b * geo.s, o[1]), o[2]))
        elif o[0] == "exacc":
            out_specs.append(pl.BlockSpec((None, 1, o[1]), lambda i: (jnp.where(is_ctx(i), nb, 0) + i // seg_blocks, 0, 0)))
            out_shapes.append(jax.ShapeDtypeStruct((2 * nb, 1, o[1]), F32))
        else:
            out_specs.append(pl.BlockSpec((o[1], o[2]), lambda i: (0, 0)))
            out_shapes.append(jax.ShapeDtypeStruct((o[1], o[2]), F32))
    n_in = len(ins)
    carrier = _Carrier(job, n_in, len(outs), 0)

    def kern(*refs):
        i = pl.program_id(0)
        refs, job_refs = carrier.split(refs)
        carrier.run(job_refs, i, geo.r // tm)
        res = body(i, *[r[...].astype(F32) for r in refs[:n_in]])
        if not isinstance(res, (tuple, list)):
            res = (res,)
        jj = i % seg_blocks
        first_of_part = (jj == 0) | (jj == x_blocks)
        for o, ref, val in zip(outs, refs[n_in:], res):
            if o[0] == "row":
                ref[...] = val.astype(ref.dtype)
            elif o[0] == "xrow":
                @pl.when(jj < x_blocks)
                def _(ref=ref, val=val):
                    ref[...] = val.astype(ref.dtype)
            else:
                first = first_of_part if o[0] == "exacc" else i == 0

                @pl.when(first)
                def _(ref=ref, val=val):
                    ref[...] = val

                @pl.when(jnp.logical_not(first))
                def _(ref=ref, val=val):
                    ref[...] += val

    res = pl.pallas_call(
        kern, name=name, grid=(geo.r // tm,), in_specs=in_specs + carrier.in_specs(), out_specs=out_specs + carrier.out_specs(),
        out_shape=out_shapes + carrier.out_shapes(), scratch_shapes=carrier.scratch(), input_output_aliases=carrier.aliases(),
        compiler_params=_cparams("arbitrary"),
    )(*args, *carrier.operands())
    own, extra = carrier.results(res)
    if job:
        return (*own, extra)
    return own[0] if len(own) == 1 else own


def _colsum(v):
    return jnp.sum(v, axis=0, keepdims=True)


def _first_norm(geo, x_rows, ctx_rows, gain, mod, name):
    d = D_MODEL
    seg_blocks, x_blocks = geo.seg // 256, geo.s // 256

    def body(i, xv, cv, g, m):
        zv = jnp.where(i % seg_blocks >= x_blocks, cv, xv)
        r = lax.rsqrt(jnp.mean(zv * zv, axis=-1, keepdims=True) + EPS)
        return zv, (zv * r) * g * (1.0 + m[:, d:2 * d]) + m[:, :d]

    return _rowwise(name, body, geo, 256, [(x_rows, "xrow"), (ctx_rows, "crow"), (gain, "full"), (mod, "ex")],
                    [("row", d, F32), ("row", d, BF16)])


def _norm_mod_bwd(geo, z, gain, mod, off, dh, dz_skip, name, gated=None, latent_only=False, job=None):
    d = D_MODEL

    def body(i, zv, g, m, dhv, skip, *rest):
        r = lax.rsqrt(jnp.mean(zv * zv, axis=-1, keepdims=True) + EPS)
        n = zv * r
        dng = dhv * (1.0 + m[:, off + d:off + 2 * d])
        dn = dng * g
        dz = r * (dn - n * jnp.mean(dn * n, axis=-1, keepdims=True)) + skip
        res = (dz, _colsum(dhv), _colsum(dhv * (n * g)), _colsum(dng * n))
        if gated:
            ov, gm = rest
            res += (dz * gm[:, gated[2]:gated[2] + d], _colsum(dz * ov))
        return res

    ins = [(z, "row"), (gain, "full"), (mod, "ex"), (dh, "row"), (dz_skip, "row")]
    outs = [("xrow" if latent_only else "row", d, F32), ("exacc", d), ("exacc", d), ("gacc", 1, d)]
    if gated:
        ins += [(gated[0], "row"), (gated[1], "ex")]
        outs += [("row", d, BF16), ("exacc", d)]
    return _rowwise(name, body, geo, 256, ins, outs, job)


def _loss_head(geo, z, target, out, mod, off, name):
    seg_blocks, x_blocks = geo.seg // 256, geo.s // 256
    d = D_MODEL

    def body(i, zv, tv, ov, m):
        keep = jnp.where(i % seg_blocks >= x_blocks, 0.0, 1.0)
        err = (zv - tv) * keep
        part = 0.5 * jnp.sum(jnp.mean(err * err, axis=-1, keepdims=True), axis=0, keepdims=True)
        dz = err * (1.0 / d)
        return dz, jnp.broadcast_to(part, (1, LANES)), dz * m[:, off:off + d], _colsum(dz * ov)

    return _rowwise(name, body, geo, 256, [(z, "row"), (target, "xrow"), (out, "row"), (mod, "ex")],
                    [("row", d, F32), ("gacc", 1, LANES), ("row", d, BF16), ("exacc", d)])


Q_SCALE = HEAD_DIM ** -0.5
N_QK_CHUNKS = (N_HEADS + N_KV_HEADS) * HEAD_DIM // LANES
N_Q_CHUNKS = N_HEADS * HEAD_DIM // LANES


def _prep_tile(geo):
    return FFN_ROW_TILE if geo.seg % FFN_ROW_TILE == 0 else 256


def _attn_prep(geo, proj, cos, sin_signed, q_gain, k_gain, name):
    def body(i, p, cs, sn, qg, kg):
        outs = []
        for ch in range(N_QK_CHUNKS):
            is_q = ch < N_Q_CHUNKS
            outs.append(_qk_chunk(p[:, ch * LANES:(ch + 1) * LANES], qg if is_q else kg, cs, sn, Q_SCALE if is_q else 1.0))
        outs.append(p[:, N_QK_CHUNKS * LANES:])
        return jnp.concatenate(outs, axis=1)

    return _rowwise(name, body, geo, _prep_tile(geo), [(proj, "row"), (cos, "tab"), (sin_signed, "tab"), (q_gain, "full"), (k_gain, "full")],
                    [("row", proj.shape[1], BF16)])


def _attn_prep_bwd(geo, proj, cos, sin_signed, q_gain, k_gain, dq, dkv, name):
    kw = N_KV_HEADS * HEAD_DIM

    def body(i, p, cs, sn, qg, kg, dqv, dkvv):
        outs = []
        dgains = [jnp.zeros((1, LANES), F32), jnp.zeros((1, LANES), F32)]
        for ch in range(N_QK_CHUNKS):
            is_q = ch < N_Q_CHUNKS
            scale = Q_SCALE if is_q else 1.0
            ct = dqv[:, ch * LANES:(ch + 1) * LANES] if is_q else dkvv[:, (ch - N_Q_CHUNKS) * LANES:(ch - N_Q_CHUNKS + 1) * LANES]
            _, vjp = jax.vjp(lambda xx, gg, scale=scale: _qk_chunk(xx, gg, cs, sn, scale),
                             p[:, ch * LANES:(ch + 1) * LANES], qg if is_q else kg)
            dx, dg = vjp(ct)
            outs.append(dx)
            dgains[0 if is_q else 1] = dgains[0 if is_q else 1] + dg
        outs.append(dkvv[:, kw:])
        return jnp.concatenate(outs, axis=1), dgains[0], dgains[1]

    return _rowwise(name, body, geo, 256,
                    [(proj, "row"), (cos, "tab"), (sin_signed, "tab"), (q_gain, "full"), (k_gain, "full"), (dq, "row"), (dkv, "row")],
                    [("row", proj.shape[1], BF16), ("gacc", 1, LANES), ("gacc", 1, LANES)])


def _attn_geometry(geo):
    assert geo.s % ATTN_BLOCK == 0 and geo.l % ATTN_BLOCK == 0 and geo.seg >= BAND
    return geo.seg // ATTN_BLOCK, geo.s // ATTN_BLOCK


def _attn_mask(j, s0, geo):
    r = lax.broadcasted_iota(jnp.int32, (ATTN_BLOCK, geo.l + BAND), 0)
    n = lax.broadcasted_iota(jnp.int32, (ATTN_BLOCK, geo.l + BAND), 1) - geo.l
    dist = (s0 - j * ATTN_BLOCK) + n - r
    return (n < 0) | ((jnp.abs(dist) <= WINDOW) & (s0 + n < geo.s))


def _attn_probs(q, keys, valid, n_ctx, sink):
    s = _dot(q, keys, "nt")
    if valid is not None:
        s = jnp.where(valid, s, NEG_INF)
    m = jnp.maximum(jnp.max(s, axis=-1, keepdims=True), sink)
    e, e_sink = jnp.exp(s - m), jnp.exp(sink - m)
    inv = 1.0 / (jnp.sum(e, axis=-1, keepdims=True) + e_sink)
    return e * inv, e_sink * inv


def _attn_keys(ref, s0, geo, with_band):
    ctx = ref[geo.s:geo.seg, :]
    return jnp.concatenate([ctx, ref[pl.ds(s0, BAND), :]], axis=0) if with_band else ctx


def _attention(geo, qkv, sink, name, job=None):
    n_blocks, n_x_blocks = _attn_geometry(geo)
    qw, kw = N_HEADS * HEAD_DIM, N_KV_HEADS * HEAD_DIM
    group = N_HEADS // N_KV_HEADS
    carrier = _Carrier(job, 4, 1, 0)

    def kern(*refs):
        (sink_ref, q_ref, k_ref, v_ref, o_ref), job_refs = carrier.split(refs)
        j = pl.program_id(1)
        carrier.run(job_refs, pl.program_id(0) * n_blocks + j, geo.b * n_blocks)
        s0 = pl.multiple_of(jnp.clip((j - 1) * ATTN_BLOCK, 0, geo.seg - BAND), ATTN_BLOCK)

        def heads(with_band):
            valid = _attn_mask(j, s0, geo) if with_band else None
            k_all, v_all = _attn_keys(k_ref, s0, geo, with_band), _attn_keys(v_ref, s0, geo, with_band)
            for h in range(N_HEADS):
                kv = slice((h // group) * HEAD_DIM, (h // group + 1) * HEAD_DIM)
                p, _ = _attn_probs(q_ref[:, h * HEAD_DIM:(h + 1) * HEAD_DIM], k_all[:, kv], valid, geo.l, sink_ref[h])
                o_ref[:, h * HEAD_DIM:(h + 1) * HEAD_DIM] = _dot(p, v_all[:, kv], "nn").astype(BF16)

        pl.when(j < n_x_blocks)(lambda: heads(True))
        pl.when(j >= n_x_blocks)(lambda: heads(False))

    res = pl.pallas_call(
        kern, name=name, grid=(geo.b, n_blocks),
        in_specs=[pl.BlockSpec(memory_space=pltpu.SMEM),
                  pl.BlockSpec((ATTN_BLOCK, qw), lambda b, j: (b * n_blocks + j, 0)),
                  pl.BlockSpec((geo.seg, kw), lambda b, j: (b, qw // kw)),
                  pl.BlockSpec((geo.seg, kw), lambda b, j: (b, qw // kw + 1))] + carrier.in_specs(),
        out_specs=[pl.BlockSpec((ATTN_BLOCK, qw), lambda b, j: (b * n_blocks + j, 0))] + carrier.out_specs(),
        out_shape=[jax.ShapeDtypeStruct((geo.r, qw), BF16)] + carrier.out_shapes(),
        scratch_shapes=carrier.scratch(), input_output_aliases=carrier.aliases(),
        compiler_params=_cparams("arbitrary", "arbitrary"),
    )(sink, qkv, qkv, qkv, *carrier.operands())
    (o,), extra = carrier.results(res)
    return o, extra


def _attention_bwd(geo, qkv, sink, do, name, job=None):
    n_blocks, n_x_blocks = _attn_geometry(geo)
    qw, kw = N_HEADS * HEAD_DIM, N_KV_HEADS * HEAD_DIM
    group = N_HEADS // N_KV_HEADS

    carrier = _Carrier(job, 5, 3, 1)

    def kern(*refs):
        (sink_ref, q_ref, k_ref, v_ref, do_ref, dq_ref, dkv_out_ref, dsink_ref, dkv_ref), job_refs = carrier.split(refs)
        b, j = pl.program_id(0), pl.program_id(1)
        carrier.run(job_refs, b * n_blocks + j, geo.b * n_blocks)
        s0 = pl.multiple_of(jnp.clip((j - 1) * ATTN_BLOCK, 0, geo.seg - BAND), ATTN_BLOCK)

        @pl.when(j == 0)
        def _():
            dkv_ref[...] = jnp.zeros_like(dkv_ref)

        @pl.when((j == 0) & (b == 0))
        def _():
            dsink_ref[...] = jnp.zeros_like(dsink_ref)

        def heads(with_band):
            valid = _attn_mask(j, s0, geo) if with_band else None
            k_all, v_all = _attn_keys(k_ref, s0, geo, with_band), _attn_keys(v_ref, s0, geo, with_band)
            for g in range(N_KV_HEADS):
                kv = slice(g * HEAD_DIM, (g + 1) * HEAD_DIM)
                keys, vals = k_all[:, kv], v_all[:, kv]
                group_heads = [slice(h * HEAD_DIM, (h + 1) * HEAD_DIM) for h in range(g * group, (g + 1) * group)]
                ds_rows, p_rows = [], []
                for h, hs in zip(range(g * group, (g + 1) * group), group_heads):
                    dout = do_ref[:, hs]
                    p, p_sink = _attn_probs(q_ref[:, hs], keys, valid, geo.l, sink_ref[h])
                    dp = _dot(dout, vals, "nt")
                    dsum = jnp.sum(p * dp, axis=-1, keepdims=True)
                    ds = (p * (dp - dsum)).astype(BF16)
                    dq_ref[:, hs] = _dot(ds, keys, "nn").astype(BF16)
                    ds_rows.append(ds)
                    p_rows.append(p.astype(BF16))
                    dsink_ref[h:h + 1, :] += jnp.broadcast_to(-jnp.sum(p_sink * dsum, axis=0, keepdims=True), (1, LANES))
                q_rows = jnp.concatenate([q_ref[:, hs] for hs in group_heads], axis=0)
                do_rows = jnp.concatenate([do_ref[:, hs] for hs in group_heads], axis=0)
                dk = _dot(jnp.concatenate(ds_rows, axis=0), q_rows, "tn")
                dv = _dot(jnp.concatenate(p_rows, axis=0), do_rows, "tn")
                vv = slice(kw + g * HEAD_DIM, kw + (g + 1) * HEAD_DIM)
                dkv_ref[geo.s:geo.seg, kv] += dk[:geo.l]
                dkv_ref[geo.s:geo.seg, vv] += dv[:geo.l]
                if with_band:
                    dkv_ref[pl.ds(s0, BAND), kv] += dk[geo.l:]
                    dkv_ref[pl.ds(s0, BAND), vv] += dv[geo.l:]

        pl.when(j < n_x_blocks)(lambda: heads(True))
        pl.when(j >= n_x_blocks)(lambda: heads(False))

        @pl.when(j == n_blocks - 1)
        def _():
            dkv_out_ref[...] = dkv_ref[...].astype(BF16)

    res = pl.pallas_call(
        kern, name=name, grid=(geo.b, n_blocks),
        in_specs=[pl.BlockSpec(memory_space=pltpu.SMEM),
                  pl.BlockSpec((ATTN_BLOCK, qw), lambda b, j: (b * n_blocks + j, 0)),
                  pl.BlockSpec((geo.seg, kw), lambda b, j: (b, qw // kw)),
                  pl.BlockSpec((geo.seg, kw), lambda b, j: (b, qw // kw + 1)),
                  pl.BlockSpec((ATTN_BLOCK, qw), lambda b, j: (b * n_blocks + j, 0))] + carrier.in_specs(),
        out_specs=[pl.BlockSpec((ATTN_BLOCK, qw), lambda b, j: (b * n_blocks + j, 0)),
                   pl.BlockSpec((geo.seg, 2 * kw), lambda b, j: (b, 0)),
                   pl.BlockSpec((N_HEADS, LANES), lambda b, j: (0, 0))] + carrier.out_specs(),
        out_shape=[jax.ShapeDtypeStruct((geo.r, qw), BF16), jax.ShapeDtypeStruct((geo.r, 2 * kw), BF16),
                   jax.ShapeDtypeStruct((N_HEADS, LANES), F32)] + carrier.out_shapes(),
        scratch_shapes=[pltpu.VMEM((geo.seg, 2 * kw), F32)] + carrier.scratch(), input_output_aliases=carrier.aliases(),
        compiler_params=_cparams("arbitrary", "arbitrary"),
    )(sink, qkv, qkv, qkv, do, *carrier.operands())
    (dq, dkv, dsink), extra = carrier.results(res)
    return dq, dkv, dsink, extra


RET_QK_W = RET_HEADS * RET_QK_DIM
K_SCALE = RET_QK_DIM ** -0.5


RET_ROW_TILE = 384


def _ret_tile(geo):
    return RET_ROW_TILE if geo.seg % RET_ROW_TILE == 0 else 256


def _ret_prep(geo, proj, cos, sin_signed, name):
    def body(i, p, cs, sn):
        cs2, sn2 = jnp.concatenate([cs] * RET_HEADS, axis=1), jnp.concatenate([sn] * RET_HEADS, axis=1)
        q = _rope(p[:, :RET_QK_W], cs2, sn2, RET_QK_DIM // 4)
        k = _rope(p[:, RET_QK_W:2 * RET_QK_W], cs2, sn2, RET_QK_DIM // 4) * K_SCALE
        return jnp.concatenate([q, k, p[:, 2 * RET_QK_W:]], axis=1)

    return _rowwise(name, body, geo, _ret_tile(geo), [(proj, ("rowc", 2 * RET_QK_W + RET_VWIDTH, 0)), (cos, "tab"), (sin_signed, "tab")],
                    [("row", 2 * RET_QK_W + RET_VWIDTH, BF16)])


def _ret_prep_bwd(geo, dq, dk, dv, dgate, cos, sin_signed, name):
    def body(i, dqv, dkv, dvv, dg, cs, sn):
        cs2, sn2 = jnp.concatenate([cs] * RET_HEADS, axis=1), jnp.concatenate([sn] * RET_HEADS, axis=1)
        dkv = dkv * K_SCALE
        dqv = dqv * cs2 + _swap_halves(dqv * sn2, RET_QK_DIM // 4)
        dkv = dkv * cs2 + _swap_halves(dkv * sn2, RET_QK_DIM // 4)
        return jnp.concatenate([dqv, dkv, dvv, dg], axis=1)

    return _rowwise(name, body, geo, _ret_tile(geo),
                    [(dq, "row"), (dk, "row"), (dv, "row"), (dgate, "row"), (cos, "tab"), (sin_signed, "tab")],
                    [("row", 2 * RET_QK_W + 2 * RET_VWIDTH, BF16)])


def _ret_step(state, q, k, v, lg, rev):
    c = RET_CHUNK
    ri = lax.broadcasted_iota(jnp.int32, (c, 1), 0).astype(F32)
    cj = lax.broadcasted_iota(jnp.int32, (1, c), 1).astype(F32)
    if rev:
        dist, q_decay, k_decay = cj - ri, jnp.exp(lg * (c - ri)), jnp.exp(lg * ri)
    else:
        dist, q_decay, k_decay = ri - cj, jnp.exp(lg * (ri + 1.0)), jnp.exp(lg * (c - 1.0 - ri))
    intra = jnp.where(dist >= 0, jnp.exp(lg * jnp.maximum(dist, 0.0)), 0.0)
    scores = _mm(q, k, "nt") * intra
    out = _mm(scores, v, "nn") + _mm(q, state, "nn") * q_decay
    new_state = state * jnp.exp(lg * c) + _mm(k * k_decay, v, "tn")
    return new_state, out


def _ret_state0(kc, vc, lg, rev):
    n = kc.shape[0]
    t = lax.broadcasted_iota(jnp.int32, (n, 1), 0).astype(F32)
    decay = jnp.exp(lg * t) if rev else jnp.exp(lg * (n - 1.0 - t))
    return _mm(kc * decay, vc, "tn")


def _ret_specs(geo):
    nq = RET_HEADS
    return [pl.BlockSpec((2 * RET_HEADS, LANES), lambda b, h: (0, 0)),
            pl.BlockSpec((geo.seg, RET_QK_DIM), lambda b, h: (b, h)),
            pl.BlockSpec((geo.seg, RET_QK_DIM), lambda b, h: (b, nq + h)),
            pl.BlockSpec((geo.seg, RET_V_DIM), lambda b, h: (b, nq + h))]


def _retention(geo, qkv, log_g, name):
    nc = geo.s // RET_CHUNK

    def kern(lg_ref, q_ref, k_ref, v_ref, o_ref, st_ref):
        h = pl.program_id(1)
        for d, rev in ((0, False), (1, True)):
            lg = lg_ref[pl.ds(d * RET_HEADS + h, 1), 0:1]
            st_ref[...] = _ret_state0(k_ref[geo.s:geo.seg, :].astype(F32), v_ref[geo.s:geo.seg, :].astype(F32), lg, rev)

            def chunk(ci, carry, d=d, rev=rev, lg=lg):
                r0 = pl.multiple_of((nc - 1 - ci if rev else ci) * RET_CHUNK, RET_CHUNK)
                rows = pl.ds(r0, RET_CHUNK)
                new_state, out = _ret_step(st_ref[...], q_ref[rows, :], k_ref[rows, :], v_ref[rows, :], lg, rev)
                st_ref[...] = new_state
                if d == 0:
                    o_ref[rows, :] = out
                else:
                    o_ref[rows, :] += out
                return carry

            lax.fori_loop(0, nc, chunk, 0)
        o_ref[geo.s:geo.seg, :] = jnp.zeros((geo.l, RET_V_DIM), F32)

    return pl.pallas_call(
        kern, name=name, grid=(geo.b, RET_HEADS), in_specs=_ret_specs(geo),
        out_specs=pl.BlockSpec((geo.seg, RET_V_DIM), lambda b, h: (b, h)),
        out_shape=jax.ShapeDtypeStruct((geo.r, RET_VWIDTH), F32),
        scratch_shapes=[pltpu.VMEM((RET_QK_DIM, RET_V_DIM), F32)],
        compiler_params=_cparams("parallel", "arbitrary"),
    )(log_g, qkv, qkv, qkv)


def _retention_bwd(geo, qkv, log_g, do, name):
    nc = geo.s // RET_CHUNK
    ctx = slice(geo.s, geo.seg)

    def kern(lg_ref, q_ref, k_ref, v_ref, do_ref, dq_ref, dk_ref, dv_ref, dlg_ref, states_ref, dst_ref, aq_ref, ak_ref, av_ref):
        b, h = pl.program_id(0), pl.program_id(1)

        @pl.when((b == 0) & (h == 0))
        def _():
            dlg_ref[...] = jnp.zeros_like(dlg_ref)

        for d, rev in ((0, False), (1, True)):
            row = pl.ds(d * RET_HEADS + h, 1)
            lg = lg_ref[row, 0:1]
            kc, vc = k_ref[ctx, :].astype(F32), v_ref[ctx, :].astype(F32)
            states_ref[0] = _ret_state0(kc, vc, lg, rev)

            def rows_of(ci, rev=rev):
                return pl.ds(pl.multiple_of((nc - 1 - ci if rev else ci) * RET_CHUNK, RET_CHUNK), RET_CHUNK)

            def load(rows):
                return q_ref[rows, :].astype(F32), k_ref[rows, :].astype(F32), v_ref[rows, :].astype(F32)

            def replay(ci, carry, rev=rev, lg=lg, rows_of=rows_of, load=load):
                states_ref[ci + 1] = _ret_step(states_ref[ci], *load(rows_of(ci)), lg, rev)[0]
                return carry

            lax.fori_loop(0, nc - 1, replay, 0)
            dst_ref[...] = jnp.zeros_like(dst_ref)

            def emit(rows, dq, dk, dv, d=d):
                if d == 0:
                    ak_ref[rows, :], av_ref[rows, :] = dk, dv
                    if dq is not None:
                        aq_ref[rows, :] = dq
                else:
                    dk_ref[rows, :] = (ak_ref[rows, :] + dk).astype(BF16)
                    dv_ref[rows, :] = (av_ref[rows, :] + dv).astype(BF16)
                    if dq is not None:
                        dq_ref[rows, :] = (aq_ref[rows, :] + dq).astype(BF16)

            def back(t, dlg, rev=rev, lg=lg, rows_of=rows_of, load=load, emit=emit):
                ci = nc - 1 - t
                rows = rows_of(ci)
                _, vjp = jax.vjp(lambda st, q, k, v, g: _ret_step(st, q, k, v, g, rev), states_ref[ci], *load(rows), lg)
                dstate, dq, dk, dv, dg = vjp((dst_ref[...], do_ref[rows, :].astype(F32)))
                dst_ref[...] = dstate
                emit(rows, dq, dk, dv)
                return dlg + dg

            dlg = lax.fori_loop(0, nc, back, jnp.zeros((1, 1), F32))
            _, vjp = jax.vjp(lambda kk, vv, g: _ret_state0(kk, vv, g, rev), kc, vc, lg)
            dkc, dvc, dg = vjp(dst_ref[...])
            emit(ctx, None, dkc, dvc)
            dlg_ref[row, :] += jnp.broadcast_to(dlg + dg, (1, LANES))
        dq_ref[ctx, :] = jnp.zeros((geo.l, RET_QK_DIM), BF16)

    nq = RET_HEADS
    return pl.pallas_call(
        kern, name=name, grid=(geo.b, RET_HEADS),
        in_specs=_ret_specs(geo) + [pl.BlockSpec((geo.seg, RET_V_DIM), lambda b, h: (b, h))],
        out_specs=[pl.BlockSpec((geo.seg, RET_QK_DIM), lambda b, h: (b, h)),
                   pl.BlockSpec((geo.seg, RET_QK_DIM), lambda b, h: (b, h)),
                   pl.BlockSpec((geo.seg, RET_V_DIM), lambda b, h: (b, h)),
                   pl.BlockSpec((2 * RET_HEADS, LANES), lambda b, h: (0, 0))],
        out_shape=[jax.ShapeDtypeStruct((geo.r, RET_QK_W), BF16), jax.ShapeDtypeStruct((geo.r, RET_QK_W), BF16),
                   jax.ShapeDtypeStruct((geo.r, RET_VWIDTH), BF16), jax.ShapeDtypeStruct((2 * RET_HEADS, LANES), F32)],
        scratch_shapes=[pltpu.VMEM((nc, RET_QK_DIM, RET_V_DIM), F32), pltpu.VMEM((RET_QK_DIM, RET_V_DIM), F32),
                        pltpu.VMEM((geo.seg, RET_QK_DIM), F32), pltpu.VMEM((geo.seg, RET_QK_DIM), F32),
                        pltpu.VMEM((geo.seg, RET_V_DIM), F32)],
        compiler_params=_cparams("arbitrary", "arbitrary"),
    )(log_g, qkv, qkv, qkv, do)


def _gated(o, g, gain):
    outs = []
    for h in range(RET_HEADS):
        cols = slice(h * RET_V_DIM, (h + 1) * RET_V_DIM)
        oh = o[:, cols]
        mu = jnp.mean(oh, axis=-1, keepdims=True)
        var = jnp.mean(jnp.square(oh - mu), axis=-1, keepdims=True)
        outs.append(_silu(g[:, cols]) * ((oh - mu) * lax.rsqrt(var + EPS) * gain[:, cols]))
    return jnp.concatenate(outs, axis=1)


def _ret_gated(geo, o, proj, gain, name):
    def body(i, ov, gv, gn):
        return _gated(ov, gv, gn)

    gate_block = (2 * RET_QK_W + RET_VWIDTH) // RET_VWIDTH
    return _rowwise(name, body, geo, _ret_tile(geo), [(o, "row"), (proj, ("rowc", RET_VWIDTH, gate_block)), (gain, "full")],
                    [("row", RET_VWIDTH, BF16)])


def _ret_gated_bwd(geo, o, proj, gain, dout, name):
    def body(i, ov, gv, gn, dv):
        _, vjp = jax.vjp(_gated, ov, gv, gn)
        return vjp(dv)

    gate_block = (2 * RET_QK_W + RET_VWIDTH) // RET_VWIDTH
    return _rowwise(name, body, geo, 256,
                    [(o, "row"), (proj, ("rowc", RET_VWIDTH, gate_block)), (gain, "full"), (dout, "row")],
                    [("row", RET_VWIDTH, BF16), ("row", RET_VWIDTH, BF16), ("gacc", 1, RET_VWIDTH)])


def _whole(name, fn, out_shapes, *arrays):
    n = len(arrays)

    def kern(*refs):
        res = fn(*[r[...] for r in refs[:n]])
        for ref, val in zip(refs[n:], res):
            ref[...] = val.astype(ref.dtype)

    return pl.pallas_call(kern, name=name, out_shape=out_shapes)(*arrays)


def _rope_tables(geo, head_dim):
    rows = geo.s // GRID_W
    row = jnp.broadcast_to(jnp.arange(rows, dtype=jnp.int32)[:, None], (rows, GRID_W)).reshape(geo.s)
    col = jnp.broadcast_to(jnp.arange(GRID_W, dtype=jnp.int32)[None, :], (rows, GRID_W)).reshape(geo.s)
    axis_dim = head_dim // 2
    inv = ROPE_BASE ** (-jnp.arange(0, axis_dim, 2, dtype=F32) / axis_dim)
    ang_r = row.astype(F32)[:, None] * inv
    ang_c = col.astype(F32)[:, None] * inv
    cos = jnp.concatenate([jnp.cos(ang_r)] * 2 + [jnp.cos(ang_c)] * 2, axis=1)
    sin = jnp.concatenate([-jnp.sin(ang_r), jnp.sin(ang_r), -jnp.sin(ang_c), jnp.sin(ang_c)], axis=1)
    cos = jnp.concatenate([cos, jnp.ones((geo.l, head_dim), F32)], axis=0)
    sin = jnp.concatenate([sin, jnp.zeros((geo.l, head_dim), F32)], axis=0)
    reps = max(1, LANES // head_dim)
    return jnp.tile(cos, (1, reps)), jnp.tile(sin, (1, reps))


def _row_tile(r):
    return next(t for t in (1536, 1024, 512, 256, 128) if r % t == 0)


MOD_ROWS = 8


def _local_step(x, ctx, target, sp, wts, mods, plan=None):
    nb, s, d = x.shape
    geo = _Rows(nb, s, ctx.shape[1])
    assert nb + 1 <= MOD_ROWS and d == D_MODEL
    tm = _row_tile(geo.r)
    cos64, sin64 = _rope_tables(geo, HEAD_DIM)
    cos256, sin256 = _rope_tables(geo, RET_QK_DIM)
    q_gain = jnp.tile(sp["q_norm"].reshape(1, HEAD_DIM), (1, LANES // HEAD_DIM))
    k_gain = jnp.tile(sp["k_norm"].reshape(1, HEAD_DIM), (1, LANES // HEAD_DIM))
    sink = sp["sink"].reshape(N_HEADS)
    log_g = jnp.broadcast_to(sp["log_g"].reshape(2 * RET_HEADS, 1), (2 * RET_HEADS, LANES))
    gn_g = sp["gn_g"].reshape(1, RET_VWIDTH)

    saved = []
    z, h1 = _first_norm(geo, x.reshape(nb * s, d), ctx.reshape(nb * geo.l, d), sp["norm1_g"][0][None, :], mods[0], "norm1_0")
    for i in range(2):
        mod3 = mods[i]
        n1, n2 = sp["norm1_g"][i][None, :], sp["norm2_g"][i][None, :]
        if i == 0:
            proj = _mm_nn(h1, wts["attn_qkv"], F32, "attn_qkv", tm, wts["attn_qkv"].shape[1], d)
            prep = _attn_prep(geo, proj, cos64, sin64, q_gain, k_gain, "attn_prep")
            o, late = _attention(geo, prep, sink, "attn", plan.gather_job() if plan else None)
            if plan:
                plan.late_weights(late, wts)
            oraw = None
            w_o = wts["attn_o"]
        else:
            proj = _mm_nn(h1, wts["ret_qkvg"], BF16, "ret_qkvg", tm, wts["ret_qkvg"].shape[2], d)
            prep = _ret_prep(geo, proj, cos256, sin256, "ret_prep")
            oraw = _retention(geo, prep, log_g, "ret")
            o = _ret_gated(geo, oraw, proj, gn_g, "ret_gated")
            w_o = wts["ret_o"]
        zmid, mix, h2 = _mm_nn_gate_residual(geo, o, w_o, z, mod3, 2 * d, f"mix_out{i}", norm=(n2, mod3, 3 * d))
        u, a = _ffn_in_swiglu(h2, wts["ffn_in"][i], f"ffn_in{i}")
        next_norm = (sp["norm1_g"][1][None, :], mods[1], 0) if i == 0 else None
        zout, f, h1_next = _mm_nn_gate_residual(geo, a, wts["ffn_out"][i], zmid, mod3, 5 * d, f"ffn_out{i}", norm=next_norm)
        saved.append(dict(z=z, mod3=mod3, n1=n1, n2=n2, h1=h1, proj=proj, prep=prep, o=o, oraw=oraw, mix=mix, zmid=zmid,
                          h2=h2, u=u, a=a, f=f))
        z, h1 = zout, h1_next

    dz, loss, df, dg2 = _loss_head(geo, z, target.reshape(nb * s, d), saved[1]["f"], saved[1]["mod3"], 5 * d, "loss")

    big, small = {}, {}
    dmods = [None, None]
    for i in (1, 0):
        sv = saved[i]
        mod3 = sv["mod3"]
        carry = plan is not None and i == 0
        du, land = _ffn_out_dx_swiglu_bwd(df, wts["ffn_out"][i], sv["u"], f"ffn_out_dx{i}", plan.layer1.swap_job() if carry else None)
        if carry:
            plan.layer1.after_swap(land)
        big[f"ffn_out{i}"] = _mm_tn(sv["a"], df, f"ffn_out_dw{i}", D_FF // 2, 1024, tm, out_dtype=BF16).reshape(N_CHIPS, D_FF // N_CHIPS, d)
        n4 = wts["ffn_in"][i].shape[2]
        dh2 = _mm_nt(du, wts["ffn_in"][i], BF16, f"ffn_in_dx{i}", tm, 1024, n4)
        big[f"ffn_in{i}"] = _mm_tn(sv["h2"], du, f"ffn_in_dw{i}", 1024, n4, tm, shards=N_CHIPS, out_dtype=BF16)
        if carry:
            plan.start_layer0_ffn(big)
        dzmid, dsh2, dsc2, dn2, dmix, dg1, *land = _norm_mod_bwd(geo, sv["zmid"], sv["n2"], mod3, 3 * d, dh2, dz, f"norm2_bwd{i}",
                                                                 gated=(sv["mix"], mod3, 2 * d),
                                                                 job=plan.layer0_ffn.swap_job() if carry else None)
        if carry:
            plan.layer0_ffn.after_swap(land[0])
        if i == 0:
            do = _mm_nt(dmix, wts["attn_o"], BF16, "attn_out_dx", tm, 1024, 1024)
            big["attn_o"] = _mm_tn(sv["o"], dmix, "attn_out_dw", 1024, 1024, tm, out_dtype=BF16).reshape(N_CHIPS, 1024 // N_CHIPS, d)
            dq, dkv, dsink, land = _attention_bwd(geo, sv["prep"], sink, do, "attn_bwd", plan.exchange_job() if plan else None)
            if plan:
                plan.after_exchange(land)
            dproj, dqg, dkg = _attn_prep_bwd(geo, sv["proj"], cos64, sin64, q_gain, k_gain, dq, dkv, "attn_prep_bwd")
            small["q_norm"] = dqg[0, :HEAD_DIM] + dqg[0, HEAD_DIM:]
            small["k_norm"] = dkg[0, :HEAD_DIM] + dkg[0, HEAD_DIM:]
            small["sink"] = dsink[:, 0]
            wq = wts["attn_qkv"]
            dh1 = _mm_nt(dproj, wq, BF16, "attn_qkv_dx", tm, 1024, wq.shape[1])
            dwq = _mm_tn(sv["h1"], dproj, "attn_qkv_dw", 1024, wq.shape[1], tm, out_dtype=BF16)
            big["attn_qkv"] = dwq.reshape(d, N_CHIPS, -1).transpose(1, 0, 2)
        else:
            do = _mm_nt(dmix, wts["ret_o"], BF16, "ret_out_dx", tm, 1024, 1024)
            big["ret_o"] = _mm_tn(sv["o"], dmix, "ret_out_dw", 1024, 1024, tm, out_dtype=BF16).reshape(N_CHIPS, RET_VWIDTH // N_CHIPS, d)
            doraw, dgate, dgn = _ret_gated_bwd(geo, sv["oraw"], sv["proj"], gn_g, do, "ret_gated_bwd")
            small["gn_g"] = dgn[0]
            dq, dk, dv, dlg = _retention_bwd(geo, sv["prep"], log_g, doraw, "ret_bwd")
            small["log_g"] = dlg[:, 0].reshape(2, RET_HEADS)
            dproj = _ret_prep_bwd(geo, dq, dk, dv, dgate, cos256, sin256, "ret_prep_bwd")
            wq = wts["ret_qkvg"]
            dh1 = _mm_nt(dproj, wq, BF16, "ret_qkvg_dx", tm, 1024, wq.shape[2])
            big["ret_qkvg"] = _mm_tn(sv["h1"], dproj, "ret_qkvg_dw", 1024, wq.shape[2], tm, shards=N_CHIPS, out_dtype=BF16)
        below = (saved[0]["f"], saved[0]["mod3"], 5 * d) if i == 1 else None
        dz, dsh1, dsc1, dn1, *below_grads = _norm_mod_bwd(geo, sv["z"], sv["n1"], mod3, 0, dh1, dzmid, f"norm1_bwd{i}", gated=below,
                                                              latent_only=i == 0)
        small[f"norm1_g{i}"], small[f"norm2_g{i}"] = dn1[0], dn2[0]
        parts = [dsh1, dsc1, dg1, dsh2, dsc2, dg2]
        rows = jnp.concatenate([jnp.concatenate([p[:nb, 0, :] for p in parts], axis=1),
                                jnp.concatenate([jnp.sum(p[nb:, 0, :], axis=0, keepdims=True) for p in parts], axis=1),
                                jnp.zeros((MOD_ROWS - nb - 1, 6 * d), F32)], axis=0)
        dmods[i] = rows
        if below_grads:
            df, dg2 = below_grads
        small[f"ada_b{i}"] = jnp.sum(rows, axis=0)
        if plan and i == 1:
            plan.start_layer1(big)
    return loss, dz, big, small, dmods


def _adamw(w, g, m, v, name):
    rows, cols = w.shape
    tr = next((t for t in (256, 128, 64, 32, 16, 8) if rows % t == 0), rows)
    c1 = 1.0 - ADAM_B1 ** ADAM_STEP
    c2 = 1.0 - ADAM_B2 ** ADAM_STEP

    def kern(w_ref, g_ref, m_ref, v_ref, d_ref, nm_ref, nv_ref):
        gv = g_ref[...]
        nm = ADAM_B1 * m_ref[...] + (1.0 - ADAM_B1) * gv
        nv = ADAM_B2 * v_ref[...] + (1.0 - ADAM_B2) * jnp.square(gv)
        d_ref[...] = -ADAM_LR * ((nm / c1) / (jnp.sqrt(nv / c2) + ADAM_EPS) + ADAM_WD * w_ref[...])
        nm_ref[...] = nm
        nv_ref[...] = nv

    spec = pl.BlockSpec((tr, cols), lambda i: (i, 0))
    return pl.pallas_call(
        kern, name=name, grid=(rows // tr,), in_specs=[spec] * 4, out_specs=[spec] * 3,
        out_shape=[jax.ShapeDtypeStruct(w.shape, F32)] * 3, compiler_params=_cparams("parallel"),
    )(w, g, m, v)


N_DEVICES = 8


def _mesh_pos():
    return lax.axis_index("x"), lax.axis_index("y"), lax.axis_index("c")


def _other_chips(x, y):
    return [(1 - x, y), (x, 1 - y), (1 - x, 1 - y)]


def _hbm(n):
    return [pl.BlockSpec(memory_space=pl.ANY)] * n


def _remote(src, dst, send_sem, recv_sem, device):
    return pltpu.make_async_remote_copy(src_ref=src, dst_ref=dst, send_sem=send_sem, recv_sem=recv_sem,
                                        device_id=device, device_id_type=MESH)


def _scalar_spec(grid, in_specs, out_specs):
    return pltpu.PrefetchScalarGridSpec(num_scalar_prefetch=1, grid=grid, in_specs=in_specs, out_specs=out_specs)


def _place_shard(param, layer, pos, name):
    _, r, cols = param.shape
    tr = _slab_tile(r)

    def kern(pos_ref, s_ref, o_ref):
        o_ref[...] = s_ref[...].astype(BF16)

    return pl.pallas_call(
        kern, name=name, out_shape=jax.ShapeDtypeStruct((N_CHIPS, r, cols), BF16),
        grid_spec=_scalar_spec((r // tr,), [pl.BlockSpec((None, tr, cols), lambda i, p: (layer, i, 0))],
                               pl.BlockSpec((None, tr, cols), lambda i, p: (p[1], i, 0))),
        compiler_params=_cparams("parallel"),
    )(pos, param)


class _CommJob:
    def __init__(self, inputs, out_shapes, aliases, sem_shapes, stages, fractions=None):
        self.inputs, self.out_shapes, self.aliases, self.sem_shapes, self.stages = inputs, out_shapes, aliases, sem_shapes, stages
        self.fractions = fractions


def _merge_jobs(a, b):
    assert len(a.stages) == len(b.stages)
    ni, no, ns = len(a.inputs), len(a.out_shapes), len(a.sem_shapes)

    def both(sa, sb):
        def stage(ins, outs, sems):
            sa(ins[:ni], outs[:no], sems[:ns])
            sb(ins[ni:], outs[no:], sems[ns:])
        return stage

    aliases = dict(a.aliases)
    aliases.update({ni + i: no + o for i, o in b.aliases.items()})
    return _CommJob(a.inputs + b.inputs, a.out_shapes + b.out_shapes, aliases, a.sem_shapes + b.sem_shapes,
                    [both(sa, sb) for sa, sb in zip(a.stages, b.stages)])


def _run_job(job, name):
    n_in, n_out = len(job.inputs), len(job.out_shapes)

    def body(*refs):
        for stage in job.stages:
            stage(refs[:n_in], refs[n_in:n_in + n_out], refs[n_in + n_out:])

    return pl.pallas_call(
        body, name=name, in_specs=_hbm(n_in), out_specs=_hbm(n_out), out_shape=job.out_shapes,
        input_output_aliases=job.aliases, scratch_shapes=job.sem_shapes,
    )(*job.inputs)


def _job_marks(job, steps):
    mid = len(job.stages) - 2
    fractions = job.fractions or [(s + 1) / (mid + 1) for s in range(mid)]
    return [0] + [min(steps - 1, 1 + int((steps - 1) * f)) for f in fractions] + [steps - 1]


def _gather_job(placed):
    n = len(placed)

    def half(w, which):
        r2 = placed[w].shape[1] // 2
        return pl.ds(which * r2, r2)

    def ici_copies(outs, sems, slot_of, arrays=range(n)):
        x, y, c = _mesh_pos()
        res = []
        for w in arrays:
            for k, (px, py) in enumerate(_other_chips(x, y)):
                slab = outs[w].at[slot_of(x, y, px, py), half(w, c)]
                res.append((slab, _remote(slab, slab, sems[0].at[w, k], sems[1].at[w, k], (px, py, c))))
        return res

    def forwards(outs, sems, which_core, arrays=range(n)):
        x, y, c = _mesh_pos()
        res = []
        for w in arrays:
            for k, (px, py) in enumerate(_other_chips(x, y)):
                slab = outs[w].at[2 * px + py, half(w, which_core(c))]
                res.append(_remote(slab, slab, sems[2].at[w, k], sems[3].at[w, k], (x, y, 1 - c)))
        return res

    def send(ins, outs, sems):
        for _, cp in ici_copies(outs, sems, lambda x, y, px, py: 2 * x + y):
            cp.start()

    def forward_of(w):
        def forward(ins, outs, sems):
            arrivals = ici_copies(outs, sems, lambda x, y, px, py: 2 * px + py, [w])
            for (_, arrival), fwd in zip(arrivals, forwards(outs, sems, lambda c: c, [w])):
                arrival.wait_recv()
                fwd.start()
        return forward

    def finish(ins, outs, sems):
        for cp in forwards(outs, sems, lambda c: 1 - c):
            cp.wait_recv()
        for _, cp in ici_copies(outs, sems, lambda x, y, px, py: 2 * x + y):
            cp.wait_send()
        for cp in forwards(outs, sems, lambda c: c):
            cp.wait_send()

    sizes = [p.shape[1] * p.shape[2] for p in placed]
    fractions = [sum(sizes[:w + 1]) / sum(sizes) for w in range(n)]
    return _CommJob(list(placed), [jax.ShapeDtypeStruct(p.shape, p.dtype) for p in placed], {w: w for w in range(n)},
                    [pltpu.SemaphoreType.DMA((n, 3))] * 4, [send] + [forward_of(w) for w in range(n)] + [finish], fractions)


def _pair_swap_job(grads):
    n = len(grads)

    def copies(ins, outs, sems):
        x, y, c = _mesh_pos()
        res = []
        for w in range(n):
            r2 = grads[w].shape[1] // 2
            res.append(_remote(ins[w].at[:, pl.ds((1 - c) * r2, r2)], outs[w], sems[0].at[w], sems[1].at[w], (x, y, 1 - c)))
        return res

    def send(ins, outs, sems):
        for cp in copies(ins, outs, sems):
            cp.start()

    def finish(ins, outs, sems):
        for cp in copies(ins, outs, sems):
            cp.wait()

    return _CommJob(list(grads), [jax.ShapeDtypeStruct((N_CHIPS, g.shape[1] // 2, g.shape[2]), g.dtype) for g in grads], {},
                    [pltpu.SemaphoreType.DMA((n,))] * 2, [send, finish])


def _chip_exchange_job(hs):
    n = len(hs)

    def send(ins, outs, sems):
        x, y, c = _mesh_pos()
        for w in range(n):
            for k, (px, py) in enumerate(_other_chips(x, y)):
                _remote(ins[w].at[2 * px + py], outs[w].at[2 * x + y], sems[0].at[w, k], sems[1].at[w, k], (px, py, c)).start()

    def finish(ins, outs, sems):
        x, y, c = _mesh_pos()
        for w in range(n):
            for k, (px, py) in enumerate(_other_chips(x, y)):
                got = outs[w].at[2 * px + py]
                cp = _remote(ins[w].at[2 * px + py], got, sems[0].at[w, k], sems[1].at[w, k], (px, py, c))
                cp.wait_recv()
                cp.wait_send()

    return _CommJob(list(hs), [jax.ShapeDtypeStruct(h.shape, h.dtype) for h in hs], {},
                    [pltpu.SemaphoreType.DMA((n, 3))] * 2, [send, finish])


def _pair_share(ts, name):
    n = len(ts)

    def body(*refs):
        outs = refs[n:2 * n]
        send_sems, recv_sems = refs[2 * n:]
        x, y, c = _mesh_pos()
        sends = []
        for w in range(n):
            r2 = ts[w].shape[0] // 2
            mine = outs[w].at[pl.ds(c * r2, r2)]
            rc = _remote(mine, mine, send_sems.at[w], recv_sems.at[w], (x, y, 1 - c))
            rc.start()
            sends.append(rc)
        for w in range(n):
            r2 = ts[w].shape[0] // 2
            theirs = outs[w].at[pl.ds((1 - c) * r2, r2)]
            _remote(theirs, theirs, send_sems.at[w], recv_sems.at[w], (x, y, 1 - c)).wait_recv()
            sends[w].wait_send()

    return pl.pallas_call(
        body, name=name, in_specs=_hbm(n), out_specs=_hbm(n),
        out_shape=[jax.ShapeDtypeStruct(t.shape, F32) for t in ts],
        input_output_aliases={w: w for w in range(n)},
        scratch_shapes=[pltpu.SemaphoreType.DMA((n,))] * 2,
    )(*ts)


def _slab_tile(rows):
    return next(t for t in (512, 256, 176, 128, 64, 32, 16) if rows % t == 0)


def _sum_pair(grad, land, pos, name):
    _, r2, cols = land.shape
    tr = _slab_tile(r2)
    nt = r2 // tr

    def kern(pos_ref, a_ref, b_ref, o_ref):
        o_ref[...] = (a_ref[...].astype(F32) + b_ref[...].astype(F32)).astype(BF16)

    spec = pl.BlockSpec((None, tr, cols), lambda j, i, p: (j, i, 0))
    return pl.pallas_call(
        kern, name=name, out_shape=jax.ShapeDtypeStruct(land.shape, BF16),
        grid_spec=_scalar_spec((N_CHIPS, nt), [pl.BlockSpec((None, tr, cols), lambda j, i, p: (j, p[0] * nt + i, 0)), spec], spec),
        compiler_params=_cparams("parallel", "parallel"),
    )(pos, grad, land)


def _sum_chips(hs, land, pos, name):
    _, r2, cols = land.shape
    tr = _slab_tile(r2)
    nt = r2 // tr

    def kern(pos_ref, h_ref, l_ref, o_ref):
        acc = jnp.zeros((tr, cols), F32)
        own = h_ref[...].astype(F32)
        for k in range(N_CHIPS):
            acc = acc + jnp.where(pos_ref[1] == k, own, l_ref[k].astype(F32))
        o_ref[...] = acc

    return pl.pallas_call(
        kern, name=name, out_shape=jax.ShapeDtypeStruct((2 * r2, cols), F32),
        grid_spec=_scalar_spec((nt,), [pl.BlockSpec((None, tr, cols), lambda i, p: (p[1], i, 0)),
                                       pl.BlockSpec((N_CHIPS, tr, cols), lambda i, p: (0, i, 0))],
                               pl.BlockSpec((tr, cols), lambda i, p: (p[0] * nt + i, 0))),
        compiler_params=_cparams("parallel"),
    )(pos, hs, land)


class _ReduceScatter:
    def __init__(self, grads, pos, tag):
        self.grads, self.pos, self.tag = list(grads), pos, tag

    def swap_job(self):
        return _pair_swap_job(self.grads)

    def after_swap(self, land):
        self.hs = [_sum_pair(g, l, self.pos, f"grads_pair_sum_{self.tag}{w}") for w, (g, l) in enumerate(zip(self.grads, land))]

    def exchange_job(self):
        return _chip_exchange_job(self.hs)

    def after_exchange(self, land2):
        return [_sum_chips(h, l, self.pos, f"grads_chip_sum_{self.tag}{w}") for w, (h, l) in enumerate(zip(self.hs, land2))]

    def run(self):
        self.after_swap(_run_job(self.swap_job(), f"grads_pair_swap_{self.tag}"))
        return self.after_exchange(_run_job(self.exchange_job(), f"grads_chip_exchange_{self.tag}"))


EARLY_WEIGHTS = ("attn_qkv",)
LATE_WEIGHTS = ("ffn_in0", "ffn_in1", "ffn_out0", "ffn_out1", "attn_o", "ret_qkvg", "ret_o")
LAYER1_GRADS = ("ffn_out1", "ffn_in1", "ret_o", "ret_qkvg")
LAYER0_FFN_GRADS = ("ffn_out0", "ffn_in0")
LAST_GRADS = ("attn_o", "attn_qkv")


def _fill_weights(wts, full):
    for name, w in full.items():
        if name[:-1] == "ffn_in":
            wts[name[:-1]][int(name[-1])] = w
        elif name[:-1] == "ffn_out":
            wts["ffn_out"][int(name[-1])] = w.reshape(-1, w.shape[2])
        elif name in ("attn_o", "ret_o"):
            wts[name] = w.reshape(-1, w.shape[2])
        elif name == "attn_qkv":
            wts[name] = w.transpose(1, 0, 2).reshape(w.shape[1], -1)
        else:
            wts[name] = w


class _StepPlan:
    def __init__(self, placed, pos):
        self.placed, self.pos = placed, pos
        self.layer1 = self.layer0_ffn = None
        self.reduced = {}

    def gather_job(self):
        return _gather_job([self.placed[k] for k in LATE_WEIGHTS])

    def late_weights(self, outs, wts):
        _fill_weights(wts, dict(zip(LATE_WEIGHTS, outs)))

    def start_layer1(self, big):
        self.layer1 = _ReduceScatter([big[k] for k in LAYER1_GRADS], self.pos, "l1_")

    def start_layer0_ffn(self, big):
        self.layer0_ffn = _ReduceScatter([big[k] for k in LAYER0_FFN_GRADS], self.pos, "l0f_")

    def exchange_job(self):
        return _merge_jobs(self.layer1.exchange_job(), self.layer0_ffn.exchange_job())

    def after_exchange(self, land):
        n1 = len(LAYER1_GRADS)
        self.reduced.update(zip(LAYER1_GRADS, self.layer1.after_exchange(land[:n1])))
        self.reduced.update(zip(LAYER0_FFN_GRADS, self.layer0_ffn.after_exchange(land[n1:])))


def _all_reduce_small(v, name):
    def body(v_ref, o_ref, land_ref, send_sems, recv_sems):
        x, y, c = _mesh_pos()
        me = 4 * x + 2 * y + c
        land_ref[me] = v_ref[...]
        for t in range(N_DEVICES):
            @pl.when(t != me)
            def _(t=t):
                _remote(v_ref, land_ref.at[me], send_sems.at[t], recv_sems.at[me], (t // 4, (t // 2) % 2, t % 2)).start()
        for t in range(N_DEVICES):
            @pl.when(t != me)
            def _(t=t):
                _remote(v_ref, land_ref.at[t], send_sems.at[t], recv_sems.at[t], (t // 4, (t // 2) % 2, t % 2)).wait()
        acc = land_ref[0]
        for t in range(1, N_DEVICES):
            acc = acc + land_ref[t]
        o_ref[...] = acc

    vmem = pl.BlockSpec(memory_space=pltpu.VMEM)
    return pl.pallas_call(
        body, name=name, in_specs=[vmem], out_specs=vmem, out_shape=jax.ShapeDtypeStruct(v.shape, F32),
        scratch_shapes=[pltpu.VMEM((N_DEVICES,) + v.shape, F32), pltpu.SemaphoreType.DMA((N_DEVICES,)),
                        pltpu.SemaphoreType.DMA((N_DEVICES,))],
    )(v)


def _all_to_all_small(v, name):
    def body(v_ref, o_ref, send_sems, recv_sems):
        x, y, c = _mesh_pos()
        me = 4 * x + 2 * y + c
        o_ref[me] = v_ref[me]
        for t in range(N_DEVICES):
            @pl.when(t != me)
            def _(t=t):
                _remote(v_ref.at[t], o_ref.at[me], send_sems.at[t], recv_sems.at[me], (t // 4, (t // 2) % 2, t % 2)).start()
        for t in range(N_DEVICES):
            @pl.when(t != me)
            def _(t=t):
                _remote(v_ref.at[t], o_ref.at[t], send_sems.at[t], recv_sems.at[t], (t // 4, (t // 2) % 2, t % 2)).wait()

    vmem = pl.BlockSpec(memory_space=pltpu.VMEM)
    return pl.pallas_call(
        body, name=name, in_specs=[vmem], out_specs=vmem, out_shape=jax.ShapeDtypeStruct(v.shape, F32),
        scratch_shapes=[pltpu.SemaphoreType.DMA((N_DEVICES,)), pltpu.SemaphoreType.DMA((N_DEVICES,))],
    )(v)


ALL_ROWS = 40


class _AdaLN:
    def __init__(self, c, c_ctx, ada_w, ada_b, riders):
        xi, yi, ci = _mesh_pos()
        self.me, self.chip, self.core = 4 * xi + 2 * yi + ci, 2 * xi + yi, ci
        self.nb, d = c.shape
        self.ada_w, self.c_ctx = ada_w, c_ctx
        self.cols = ada_w.shape[2]
        ctx_row = self.nb * N_DEVICES
        assert ctx_row + 1 + riders.shape[0] <= ALL_ROWS
        placed = lax.dynamic_update_slice(jnp.zeros((ALL_ROWS, d), F32), c, (self.me * self.nb, 0))
        placed = lax.dynamic_update_slice(placed, riders, (ctx_row + 1, 0))
        summed = _all_reduce_small(placed, "gather_conditioning")
        self.riders = summed[ctx_row + 1:ctx_row + 1 + riders.shape[0]]
        c_all = summed.at[ctx_row].set(c_ctx)
        self.cact, = _whole("cond_silu", lambda v: (_silu(v),), [jax.ShapeDtypeStruct(c_all.shape, F32)], c_all)
        parts = []
        for i in range(2):
            bias = lax.dynamic_slice(ada_b[i], (self.chip * self.cols,), (self.cols,))[None, :]
            parts.append(_mm_nn(self.cact, ada_w[i], F32, f"mod{i}", ALL_ROWS, self.cols, d, bias=bias))
        part = jnp.concatenate(parts, axis=1)
        rows = [[t * self.nb + b for b in range(self.nb)] + [ctx_row] * (MOD_ROWS - self.nb) for t in range(N_DEVICES)]
        got = _all_to_all_small(part[jnp.asarray(rows)], "mod_exchange")
        self.mods = [jnp.concatenate([got[2 * j][:self.nb + 1, i * self.cols:(i + 1) * self.cols] for j in range(N_CHIPS)], axis=1)[:, None, :]
                     for i in range(2)]

    def backward(self, dmods):
        nb, cols, d = self.nb, self.cols, self.ada_w.shape[1]
        blocks = [jnp.concatenate([dm[:, j * cols:(j + 1) * cols] for dm in dmods], axis=1) for j in range(N_CHIPS)]
        got = _all_to_all_small(jnp.stack([blocks[t // 2] for t in range(N_DEVICES)]), "dmod_exchange")
        dall = jnp.concatenate([got[:, :nb].reshape(N_DEVICES * nb, 2 * cols), jnp.sum(got[:, nb], axis=0, keepdims=True),
                                jnp.zeros((ALL_ROWS - N_DEVICES * nb - 1, 2 * cols), F32)], axis=0)
        dctx = jnp.concatenate([dall[N_DEVICES * nb][None, :], jnp.zeros((MOD_ROWS - 1, 2 * cols), F32)], axis=0)
        grads, dcact = [], []
        for i in range(2):
            grads.append(_mm_tn(self.cact, dall[:, i * cols:(i + 1) * cols], f"ada_dw{i}", d, cols, ALL_ROWS))
            dcact.append(_mm_nt(dctx[:, i * cols:(i + 1) * cols], self.ada_w[i], F32, f"ada_dx{i}", MOD_ROWS, d, cols))

        def silu_bwd(v, d0, d1):
            sg = _sigmoid(v)
            return ((d0 + d1)[0:1] * (sg * (1.0 + v * (1.0 - sg))),)

        dc_ctx, = _whole("cond_silu_bwd", silu_bwd, [jax.ShapeDtypeStruct((1, d), F32)], self.c_ctx[None, :], dcact[0], dcact[1])
        return grads, jnp.where(self.core == 0, dc_ctx[0], jnp.zeros((d,), F32))


SMALL_ROWS = 24


def _pack_small(small, dlogit):
    d = D_MODEL
    misc = jnp.zeros((d,), F32)
    misc = misc.at[0:HEAD_DIM].set(small["q_norm"]).at[128:128 + HEAD_DIM].set(small["k_norm"])
    misc = misc.at[256:256 + N_HEADS].set(small["sink"]).at[384:384 + 2 * RET_HEADS].set(dlogit.reshape(-1))
    rows = [small["ada_b0"].reshape(6, d), small["ada_b1"].reshape(6, d), small["norm1_g0"][None], small["norm1_g1"][None],
            small["norm2_g0"][None], small["norm2_g1"][None], small["c_ctx"][None], small["gn_g"].reshape(2, d), misc[None]]
    buf = jnp.concatenate(rows, axis=0)
    return jnp.concatenate([buf, jnp.zeros((SMALL_ROWS - buf.shape[0], d), F32)], axis=0)


def _unpack_small(buf):
    d = D_MODEL
    misc = buf[19]
    return dict(ada_b=buf[0:12].reshape(2, 6 * d), norm1_g=buf[12:14], norm2_g=buf[14:16], c_ctx=buf[16],
                gn_g=buf[17:19].reshape(2 * d), q_norm=misc[0:HEAD_DIM], k_norm=misc[128:128 + HEAD_DIM],
                sink=misc[256:256 + N_HEADS], decay=misc[384:384 + 2 * RET_HEADS])


def kernel(x, c, ctx, c_ctx, ada_w, ada_b, norm1_g, norm2_g, ffn_w_in, ffn_w_out, attn_w_qkv, attn_q_norm, attn_k_norm, attn_sink, attn_w_o, ret_w_qkvg, ret_decay_logit, ret_gn_g, ret_w_o, loss_target, m_c_ctx, m_ada_w, m_ada_b, m_norm1_g, m_norm2_g, m_ffn_w_in, m_ffn_w_out, m_attn_w_qkv, m_attn_q_norm, m_attn_k_norm, m_attn_sink, m_attn_w_o, m_ret_w_qkvg, m_ret_decay_logit, m_ret_gn_g, m_ret_w_o, v_c_ctx, v_ada_w, v_ada_b, v_norm1_g, v_norm2_g, v_ffn_w_in, v_ffn_w_out, v_attn_w_qkv, v_attn_q_norm, v_attn_k_norm, v_attn_sink, v_attn_w_o, v_ret_w_qkvg, v_ret_decay_logit, v_ret_gn_g, v_ret_w_o):
    xi, yi, ci = _mesh_pos()
    chip = 2 * xi + yi
    nb, s, d = x.shape
    gn_shard = ret_gn_g.shape[1]

    shards = dict(ffn_in0=(ffn_w_in, 0), ffn_in1=(ffn_w_in, 1), ffn_out0=(ffn_w_out, 0),
                  ffn_out1=(ffn_w_out, 1), attn_qkv=(attn_w_qkv, 0), attn_o=(attn_w_o, 0), ret_qkvg=(ret_w_qkvg, 0), ret_o=(ret_w_o, 0))
    names = list(shards)
    pos = jnp.stack([ci, chip]).astype(jnp.int32)
    placed = {k: _place_shard(*shards[k], pos, f"place_{k}") for k in names}
    early = _run_job(_gather_job([placed[k] for k in EARLY_WEIGHTS]), "gather_early_weights")
    gn_mine = jnp.where(ci == 0, ret_gn_g[0], jnp.zeros_like(ret_gn_g[0]))
    gn_place = lax.dynamic_update_slice(jnp.zeros((RET_VWIDTH,), F32), gn_mine, (chip * gn_shard,))

    wts = dict(ffn_in=[None, None], ffn_out=[None, None], attn_qkv=None, attn_o=None, ret_qkvg=None, ret_o=None)
    ada = _AdaLN(c, c_ctx, ada_w, ada_b, riders=gn_place.reshape(2, d))
    gn_full = ada.riders.reshape(RET_VWIDTH)
    _fill_weights(wts, dict(zip(EARLY_WEIGHTS, early)))
    plan = _StepPlan(placed, pos)
    decay_logit = ret_decay_logit[0]
    sp = dict(norm1_g=norm1_g, norm2_g=norm2_g, q_norm=attn_q_norm[0], k_norm=attn_k_norm[0],
              sink=attn_sink[0], log_g=jax.nn.log_sigmoid(decay_logit), gn_g=gn_full)
    loss_part, dz, big, small, dmods = _local_step(x, ctx, loss_target, sp, wts, ada.mods, plan)
    ada_grads, small["c_ctx"] = ada.backward(dmods)

    loss = lax.psum(loss_part[0, 0], ("x", "y", "c"))
    grad_x = dz.reshape(nb, s, d)

    dlogit = small["log_g"] * jax.nn.sigmoid(-decay_logit)
    sg = _unpack_small(_all_reduce_small(_pack_small(small, dlogit), "reduce_small_grads"))
    halves = dict(plan.reduced)
    halves.update(zip(LAST_GRADS, _ReduceScatter([big[k] for k in LAST_GRADS], pos, "last_").run()))
    reduced = dict(zip(halves, _pair_share(list(halves.values()), "grads_pair_share")))

    grads = dict(
        c_ctx=sg["c_ctx"], ada_w=jnp.stack(ada_grads), ada_b=sg["ada_b"], norm1_g=sg["norm1_g"],
        norm2_g=sg["norm2_g"], ffn_w_in=jnp.stack([reduced["ffn_in0"], reduced["ffn_in1"]]),
        ffn_w_out=jnp.stack([reduced["ffn_out0"], reduced["ffn_out1"]]), attn_w_qkv=reduced["attn_qkv"][None],
        attn_q_norm=sg["q_norm"][None], attn_k_norm=sg["k_norm"][None], attn_sink=sg["sink"][None],
        attn_w_o=reduced["attn_o"][None], ret_w_qkvg=reduced["ret_qkvg"][None], ret_decay_logit=sg["decay"].reshape(1, 2, RET_HEADS),
        ret_gn_g=lax.dynamic_slice(sg["gn_g"], (chip * gn_shard,), (gn_shard,))[None], ret_w_o=reduced["ret_o"][None])
    params = dict(c_ctx=(c_ctx, m_c_ctx, v_c_ctx), ada_w=(ada_w, m_ada_w, v_ada_w), ada_b=(ada_b, m_ada_b, v_ada_b),
                  norm1_g=(norm1_g, m_norm1_g, v_norm1_g), norm2_g=(norm2_g, m_norm2_g, v_norm2_g),
                  ffn_w_in=(ffn_w_in, m_ffn_w_in, v_ffn_w_in), ffn_w_out=(ffn_w_out, m_ffn_w_out, v_ffn_w_out),
                  attn_w_qkv=(attn_w_qkv, m_attn_w_qkv, v_attn_w_qkv), attn_q_norm=(attn_q_norm, m_attn_q_norm, v_attn_q_norm),
                  attn_k_norm=(attn_k_norm, m_attn_k_norm, v_attn_k_norm), attn_sink=(attn_sink, m_attn_sink, v_attn_sink),
                  attn_w_o=(attn_w_o, m_attn_w_o, v_attn_w_o), ret_w_qkvg=(ret_w_qkvg, m_ret_w_qkvg, v_ret_w_qkvg),
                  ret_decay_logit=(ret_decay_logit, m_ret_decay_logit, v_ret_decay_logit),
                  ret_gn_g=(ret_gn_g, m_ret_gn_g, v_ret_gn_g), ret_w_o=(ret_w_o, m_ret_w_o, v_ret_w_o))
    order = list(params)
    deltas, new_m, new_v = [], [], []
    for k in order:
        w, m, v = params[k]
        g = grads[k].reshape(w.shape)
        grads[k] = g
        flat = (-1, w.shape[-1]) if w.ndim > 1 else (1, -1)
        if k == "ret_decay_logit":
            flat = (1, -1)
        dw, nm, nv = _adamw(w.reshape(flat), g.reshape(flat), m.reshape(flat), v.reshape(flat), f"adamw_{k}")
        deltas.append(dw.reshape(w.shape))
        new_m.append(nm.reshape(w.shape))
        new_v.append(nv.reshape(w.shape))
    return (loss, grad_x, *[grads[k] for k in order], *deltas, *new_m, *new_v)
```

```python
import functools

import jax
import jax.numpy as jnp
from jax import lax
from jax.experimental import pallas as pl
from jax.experimental.pallas import tpu as pltpu

F32 = jnp.float32
BF16 = jnp.bfloat16

D_MODEL = 1024
N_HEADS = 16
N_KV_HEADS = 4
HEAD_DIM = 64
WINDOW = 128
ATTN_BLOCK = 128
BAND = ATTN_BLOCK + 2 * WINDOW
RET_HEADS = 4
RET_QK_DIM = 256
RET_V_DIM = 512
RET_VWIDTH = 2048
RET_CHUNK = 128
D_FF = 2816
GRID_W = 64
ROPE_BASE = 10000.0
EPS = 1e-6
NEG_INF = -1e30
LANES = 128

ADAM_LR = 0.001
ADAM_B1 = 0.9
ADAM_B2 = 0.999
ADAM_EPS = 1e-08
ADAM_WD = 0.01
ADAM_STEP = 10

VMEM_LIMIT_BYTES = 56 * 1024 * 1024
MESH = pl.DeviceIdType.MESH
N_CHIPS = 4


def _cparams(*sem):
    return pltpu.CompilerParams(dimension_semantics=sem, vmem_limit_bytes=VMEM_LIMIT_BYTES)


_DIMS = {"nn": ((1,), (0,)), "nt": ((1,), (1,)), "tn": ((0,), (0,))}


def _dot(a, b, form):
    return lax.dot_general(a.astype(BF16), b.astype(BF16), (_DIMS[form], ((), ())), preferred_element_type=F32)


@functools.partial(jax.custom_vjp, nondiff_argnums=(2,))
def _mm(a, b, form):
    return _dot(a, b, form)


def _mm_fwd(a, b, form):
    return _dot(a, b, form), (a, b)


def _mm_bwd(form, res, ct):
    a, b = res
    if form == "nn":
        da, db = _dot(ct, b, "nt"), _dot(a, ct, "tn")
    elif form == "nt":
        da, db = _dot(ct, b, "nn"), _dot(ct, a, "tn")
    else:
        da, db = _dot(b, ct, "nt"), _dot(a, ct, "nn")
    return da.astype(a.dtype), db.astype(b.dtype)


_mm.defvjp(_mm_fwd, _mm_bwd)


def _swap_halves(x, half):
    w = x.shape[-1]
    lane = lax.broadcasted_iota(jnp.int32, x.shape, x.ndim - 1)
    return jnp.where(lane % (2 * half) < half, pltpu.roll(x, w - half, x.ndim - 1), pltpu.roll(x, half, x.ndim - 1))


@functools.partial(jax.custom_vjp, nondiff_argnums=(1,))
def _rot(x, half):
    return _swap_halves(x, half)


def _rot_fwd(x, half):
    return _swap_halves(x, half), None


def _rot_bwd(half, _, ct):
    return (_swap_halves(ct, half),)


_rot.defvjp(_rot_fwd, _rot_bwd)


def _rope(x, cos, sin_signed, half):
    return x * cos + _rot(x, half) * sin_signed


def _head_mean_square(x):
    r = lax.broadcasted_iota(jnp.int32, (LANES, LANES), 0) // HEAD_DIM
    c = lax.broadcasted_iota(jnp.int32, (LANES, LANES), 1) // HEAD_DIM
    g = jnp.where(r == c, 1.0 / HEAD_DIM, 0.0).astype(F32)
    return jnp.dot(x * x, g, precision=lax.Precision.HIGHEST, preferred_element_type=F32)


def _qk_chunk(x, gain, cos, sin_signed, scale):
    y = x * lax.rsqrt(_head_mean_square(x) + EPS) * gain
    return _rope(y, cos, sin_signed, HEAD_DIM // 4) * scale


def _sigmoid(x):
    return 1.0 / (1.0 + jnp.exp(-x))


def _silu(x):
    return x * _sigmoid(x)


def _mm_nn(a, w, out_dtype, name, tm, tn, tk, bias=None):
    m, k_dim = a.shape
    if w.ndim == 3:
        n = w.shape[0] * w.shape[2]
        per = w.shape[2] // tn
        assert w.shape[2] % tn == 0
        w_spec = pl.BlockSpec((None, tk, tn), lambda i, j, k: (j // per, k, j % per))
    else:
        n = w.shape[1]
        w_spec = pl.BlockSpec((tk, tn), lambda i, j, k: (k, j))
    assert m % tm == 0 and n % tn == 0 and k_dim % tk == 0, (name, a.shape, w.shape, tm, tn, tk)
    nk = k_dim // tk
    has_bias = bias is not None

    def body(*refs):
        a_ref, w_ref = refs[0], refs[1]
        b_ref = refs[2] if has_bias else None
        o_ref, acc_ref = (refs[-1], None) if nk == 1 else (refs[-2], refs[-1])
        if nk == 1:
            part = jnp.dot(a_ref[...].astype(BF16), w_ref[...].astype(BF16), preferred_element_type=F32)
            o_ref[...] = (part + b_ref[...] if has_bias else part).astype(out_dtype)
            return
        k = pl.program_id(2)

        @pl.when(k == 0)
        def _():
            acc_ref[...] = jnp.zeros_like(acc_ref)

        acc_ref[...] += jnp.dot(a_ref[...].astype(BF16), w_ref[...].astype(BF16), preferred_element_type=F32)

        @pl.when(k == nk - 1)
        def _():
            r = acc_ref[...]
            if has_bias:
                r = r + b_ref[...]
            o_ref[...] = r.astype(out_dtype)

    in_specs = [pl.BlockSpec((tm, tk), lambda i, j, k: (i, k)), w_spec]
    args = [a, w]
    if has_bias:
        in_specs.append(pl.BlockSpec((1, tn), lambda i, j, k: (0, j)))
        args.append(bias)
    return pl.pallas_call(
        body, name=name, grid=(m // tm, n // tn, nk), in_specs=in_specs,
        out_specs=pl.BlockSpec((tm, tn), lambda i, j, k: (i, j)),
        out_shape=jax.ShapeDtypeStruct((m, n), out_dtype),
        scratch_shapes=[pltpu.VMEM((tm, tn), F32)] if nk > 1 else [],
        compiler_params=_cparams("parallel", "parallel", "arbitrary"),
    )(*args)


def _mm_nt(a, w, out_dtype, name, tm, tn, tk):
    if a.ndim == 3:
        planes, m, plane_w = a.shape
        c_dim = planes * plane_w
        a_per = plane_w // tk
        assert plane_w % tk == 0
        a_spec = pl.BlockSpec((None, tm, tk), lambda i, j, k: (k // a_per, i, k % a_per))
    else:
        m, c_dim = a.shape
        a_spec = pl.BlockSpec((tm, tk), lambda i, j, k: (i, k))
    if w.ndim == 3:
        k_out = w.shape[1]
        per = w.shape[2] // tk
        assert w.shape[2] % tk == 0 and w.shape[0] * w.shape[2] == c_dim
        w_spec = pl.BlockSpec((None, tn, tk), lambda i, j, k: (k // per, j, k % per))
    else:
        k_out = w.shape[0]
        assert w.shape[1] == c_dim
        w_spec = pl.BlockSpec((tn, tk), lambda i, j, k: (j, k))
    assert m % tm == 0 and k_out % tn == 0 and c_dim % tk == 0, (name, a.shape, w.shape, tm, tn, tk)
    nk = c_dim // tk

    def body(a_ref, w_ref, o_ref, acc_ref=None):
        if nk == 1:
            o_ref[...] = _dot(a_ref[...], w_ref[...], "nt").astype(out_dtype)
            return
        k = pl.program_id(2)

        @pl.when(k == 0)
        def _():
            acc_ref[...] = jnp.zeros_like(acc_ref)

        acc_ref[...] += _dot(a_ref[...], w_ref[...], "nt")

        @pl.when(k == nk - 1)
        def _():
            o_ref[...] = acc_ref[...].astype(out_dtype)

    return pl.pallas_call(
        body, name=name, grid=(m // tm, k_out // tn, nk),
        in_specs=[a_spec, w_spec],
        out_specs=pl.BlockSpec((tm, tn), lambda i, j, k: (i, j)),
        out_shape=jax.ShapeDtypeStruct((m, k_out), out_dtype),
        scratch_shapes=[pltpu.VMEM((tm, tn), F32)] if nk > 1 else [],
        compiler_params=_cparams("parallel", "parallel", "arbitrary"),
    )(a, w)


def _mm_tn(a, b, name, tm, tn, tk, shards=None, out_dtype=F32):
    r, k_dim = a.shape
    if b.ndim == 3:
        n = b.shape[0] * b.shape[2]
        b_per = b.shape[2] // tn
        assert b.shape[2] % tn == 0
        b_spec = pl.BlockSpec((None, tk, tn), lambda i, j, k: (j // b_per, k, j % b_per))
    else:
        n = b.shape[1]
        b_spec = pl.BlockSpec((tk, tn), lambda i, j, k: (k, j))
    assert r % tk == 0 and k_dim % tm == 0 and n % tn == 0, (name, a.shape, b.shape, tm, tn, tk)
    nk = r // tk
    if shards:
        per = n // shards // tn
        assert n % (shards * tn) == 0
        out_shape = jax.ShapeDtypeStruct((shards, k_dim, n // shards), out_dtype)
        out_spec = pl.BlockSpec((None, tm, tn), lambda i, j, k: (j // per, i, j % per))
    else:
        out_shape = jax.ShapeDtypeStruct((k_dim, n), out_dtype)
        out_spec = pl.BlockSpec((tm, tn), lambda i, j, k: (i, j))
    direct = out_dtype == F32

    def body(a_ref, b_ref, o_ref, *scratch):
        acc_ref = o_ref if direct else scratch[0]
        k = pl.program_id(2)

        @pl.when(k == 0)
        def _():
            acc_ref[...] = jnp.zeros_like(acc_ref)

        acc_ref[...] += _dot(a_ref[...], b_ref[...], "tn")
        if not direct:
            @pl.when(k == nk - 1)
            def _():
                o_ref[...] = acc_ref[...].astype(out_dtype)

    return pl.pallas_call(
        body, name=name, grid=(k_dim // tm, n // tn, nk),
        in_specs=[pl.BlockSpec((tk, tm), lambda i, j, k: (k, i)), b_spec],
        out_specs=out_spec, out_shape=out_shape,
        scratch_shapes=[] if direct else [pltpu.VMEM((tm, tn), F32)],
        compiler_params=_cparams("parallel", "parallel", "arbitrary"),
    )(a, b)


class _Carrier:
    def __init__(self, job, n_in, n_out, n_scratch):
        self.job, self.n_in, self.n_out, self.n_scratch = job, n_in, n_out, n_scratch
        self.ji = len(job.inputs) if job else 0
        self.jo = len(job.out_shapes) if job else 0

    def operands(self):
        return list(self.job.inputs) if self.job else []

    def in_specs(self):
        return [pl.BlockSpec(memory_space=pl.ANY)] * self.ji

    def out_specs(self):
        return [pl.BlockSpec(memory_space=pl.ANY)] * self.jo

    def out_shapes(self):
        return list(self.job.out_shapes) if self.job else []

    def scratch(self):
        return list(self.job.sem_shapes) if self.job else []

    def aliases(self):
        return {self.n_in + a: self.n_out + b for a, b in self.job.aliases.items()} if self.job else {}

    def split(self, refs):
        a = self.n_in
        b = a + self.ji
        c = b + self.n_out
        d = c + self.jo
        e = d + self.n_scratch
        return list(refs[:a]) + list(refs[b:c]) + list(refs[d:e]), (refs[a:b], refs[c:d], refs[e:])

    def run(self, job_refs, step, steps):
        if not self.job:
            return
        for stage, mark in zip(self.job.stages, _job_marks(self.job, steps)):
            pl.when(step == mark)(functools.partial(stage, *job_refs))

    def results(self, res):
        res = list(res)
        return res[:self.n_out], res[self.n_out:]


FFN_ROW_TILE = 768


def _ffn_tile(r):
    return FFN_ROW_TILE if r % FFN_ROW_TILE == 0 else _row_tile(r)


def _ffn_in_swiglu(h, w, name):
    r, k_dim = h.shape
    n4 = w.shape[2]
    tm = _ffn_tile(r)

    def body(h_ref, wg_ref, wu_ref, u_ref, a_ref):
        hv = h_ref[...]
        g = jnp.dot(hv, wg_ref[...], preferred_element_type=F32)
        up = jnp.dot(hv, wu_ref[...], preferred_element_type=F32)
        u_ref[0] = g.astype(BF16)
        u_ref[1] = up.astype(BF16)
        a_ref[...] = (_silu(g) * up).astype(BF16)

    return pl.pallas_call(
        body, name=name, grid=(r // tm, 2),
        in_specs=[pl.BlockSpec((tm, k_dim), lambda i, j: (i, 0)),
                  pl.BlockSpec((None, k_dim, n4), lambda i, j: (j, 0, 0)),
                  pl.BlockSpec((None, k_dim, n4), lambda i, j: (j + 2, 0, 0))],
        out_specs=[pl.BlockSpec((2, tm, n4), lambda i, j: (0, i, j)), pl.BlockSpec((tm, n4), lambda i, j: (i, j))],
        out_shape=[jax.ShapeDtypeStruct((2, r, 2 * n4), BF16), jax.ShapeDtypeStruct((r, 2 * n4), BF16)],
        compiler_params=_cparams("parallel", "parallel"),
    )(h, w, w)


def _mm_nn_gate_residual(geo, a, w, z, mod, off, name, norm=None):
    r, k_dim = a.shape
    n = w.shape[1]
    tm = FFN_ROW_TILE if geo.seg % FFN_ROW_TILE == 0 else 256
    tiles = geo.seg // tm
    assert geo.seg % tm == 0 and r == geo.r and n == D_MODEL

    def body(a_ref, w_ref, z_ref, mx_ref, mc_ref, *rest):
        out = jnp.dot(a_ref[...], w_ref[...], preferred_element_type=F32)
        is_x = (pl.program_id(0) % tiles) * tm + lax.broadcasted_iota(jnp.int32, (tm, 1), 0) < geo.s
        zo = z_ref[...] + jnp.where(is_x, mx_ref[:, off:off + n], mc_ref[:, off:off + n]) * out
        if norm:
            g_ref, nx_ref, nc_ref, zo_ref, raw_ref, h_ref = rest
            no = norm[2]
            shift = jnp.where(is_x, nx_ref[:, no:no + n], nc_ref[:, no:no + n])
            scale = jnp.where(is_x, nx_ref[:, no + n:no + 2 * n], nc_ref[:, no + n:no + 2 * n])
            rs = lax.rsqrt(jnp.mean(zo * zo, axis=-1, keepdims=True) + EPS)
            h_ref[...] = ((zo * rs) * g_ref[...] * (1.0 + scale) + shift).astype(BF16)
        else:
            zo_ref, raw_ref = rest
        zo_ref[...] = zo
        raw_ref[...] = out.astype(BF16)

    def mod_specs(m):
        return [pl.BlockSpec((None, 1, m.shape[2]), lambda i: (i // tiles, 0, 0)), pl.BlockSpec((None, 1, m.shape[2]), lambda i: (geo.b, 0, 0))]

    row = pl.BlockSpec((tm, n), lambda i: (i, 0))
    in_specs = [pl.BlockSpec((tm, k_dim), lambda i: (i, 0)), pl.BlockSpec((k_dim, n), lambda i: (0, 0)), row] + mod_specs(mod)
    args = [a, w, z, mod, mod]
    out_specs, out_shape = [row, row], [jax.ShapeDtypeStruct((r, n), F32), jax.ShapeDtypeStruct((r, n), BF16)]
    if norm:
        in_specs += [pl.BlockSpec((1, n), lambda i: (0, 0))] + mod_specs(norm[1])
        args += [norm[0], norm[1], norm[1]]
        out_specs.append(row)
        out_shape.append(jax.ShapeDtypeStruct((r, n), BF16))
    res = pl.pallas_call(body, name=name, grid=(r // tm,), in_specs=in_specs, out_specs=out_specs, out_shape=out_shape,
                         compiler_params=_cparams("parallel"))(*args)
    return res if norm else (*res, None)


def _ffn_out_dx_swiglu_bwd(df, w_out, u, name, job=None):
    r, d = df.shape
    n4 = u.shape[2] // 2
    tm = _ffn_tile(r)
    carrier = _Carrier(job, 3, 1, 0)
    steps = (r // tm) * 2

    def body(*refs):
        (df_ref, w_ref, u_ref, du_ref), job_refs = carrier.split(refs)
        carrier.run(job_refs, pl.program_id(0) * 2 + pl.program_id(1), steps)
        da = _dot(df_ref[...], w_ref[...], "nt")
        g, up = u_ref[0].astype(F32), u_ref[1].astype(F32)
        s = _sigmoid(g)
        du_ref[0] = (da * up * (s * (1.0 + g * (1.0 - s)))).astype(BF16)
        du_ref[1] = (da * (g * s)).astype(BF16)

    res = pl.pallas_call(
        body, name=name, grid=(r // tm, 2),
        in_specs=[pl.BlockSpec((tm, d), lambda i, j: (i, 0)), pl.BlockSpec((n4, d), lambda i, j: (j, 0)),
                  pl.BlockSpec((2, tm, n4), lambda i, j: (0, i, j))] + carrier.in_specs(),
        out_specs=[pl.BlockSpec((2, tm, n4), lambda i, j: (0, i, j))] + carrier.out_specs(),
        out_shape=[jax.ShapeDtypeStruct(u.shape, BF16)] + carrier.out_shapes(),
        scratch_shapes=carrier.scratch(), input_output_aliases=carrier.aliases(),
        compiler_params=_cparams("arbitrary", "arbitrary"),
    )(df, w_out, u, *carrier.operands())
    (du,), extra = carrier.results(res)
    return du, extra


class _Rows:
    def __init__(self, b, s, l):
        self.b, self.s, self.l = b, s, l
        self.seg = s + l
        self.r = b * self.seg


def _rowwise(name, body, geo, tm, ins, outs, job=None):
    seg_blocks, x_blocks = geo.seg // tm, geo.s // tm
    per_part = {"ex", "xrow", "crow"} & {k for _, k in ins if isinstance(k, str)} or {"exacc", "xrow"} & {o[0] for o in outs}
    assert geo.seg % tm == 0 and (geo.s % tm == 0 or not per_part), (name, tm)
    nb = geo.b

    def is_ctx(i):
        return i % seg_blocks >= x_blocks

    in_specs, args = [], []
    for arr, kind in ins:
        args.append(arr)
        if kind == "row":
            in_specs.append(pl.BlockSpec((tm, arr.shape[1]), lambda i: (i, 0)))
        elif kind == "ex":
            in_specs.append(pl.BlockSpec((None, 1, arr.shape[2]), lambda i: (jnp.where(is_ctx(i), nb, i // seg_blocks), 0, 0)))
        elif kind == "full":
            in_specs.append(pl.BlockSpec(arr.shape, lambda i, nd=arr.ndim: (0,) * nd))
        elif kind == "tab":
            in_specs.append(pl.BlockSpec((tm, arr.shape[1]), lambda i: (i % seg_blocks, 0)))
        elif kind == "xrow":
            in_specs.append(pl.BlockSpec(
                (tm, arr.shape[1]), lambda i: ((i // seg_blocks) * x_blocks + jnp.minimum(i % seg_blocks, x_blocks - 1), 0)))
        elif kind == "crow":
            c_blocks = seg_blocks - x_blocks
            in_specs.append(pl.BlockSpec(
                (tm, arr.shape[1]), lambda i: ((i // seg_blocks) * c_blocks + jnp.maximum(i % seg_blocks - x_blocks, 0), 0)))
        else:
            _, width, cb = kind
            in_specs.append(pl.BlockSpec((tm, width), lambda i, cb=cb: (i, cb)))
    out_specs, out_shapes = [], []
    for o in outs:
        if o[0] == "row":
            out_specs.append(pl.BlockSpec((tm, o[1]), lambda i: (i, 0)))
            out_shapes.append(jax.ShapeDtypeStruct((geo.r, o[1]), o[2]))
        elif o[0] == "xrow":
            out_specs.append(pl.BlockSpec(
                (tm, o[1]), lambda i: ((i // seg_blocks) * x_blocks + jnp.minimum(i % seg_blocks, x_blocks - 1), 0)))
            out_shapes.append(jax.ShapeDtypeStruct((geo.b * geo.s, o[1]), o[2]))
        elif o[0] == "exacc":
            out_specs.append(pl.BlockSpec((None, 1, o[1]), lambda i: (jnp.where(is_ctx(i), nb, 0) + i // seg_blocks, 0, 0)))
            out_shapes.append(jax.ShapeDtypeStruct((2 * nb, 1, o[1]), F32))
        else:
            out_specs.append(pl.BlockSpec((o[1], o[2]), lambda i: (0, 0)))
            out_shapes.append(jax.ShapeDtypeStruct((o[1], o[2]), F32))
    n_in = len(ins)
    carrier = _Carrier(job, n_in, len(outs), 0)

    def kern(*refs):
        i = pl.program_id(0)
        refs, job_refs = carrier.split(refs)
        carrier.run(job_refs, i, geo.r // tm)
        res = body(i, *[r[...].astype(F32) for r in refs[:n_in]])
        if not isinstance(res, (tuple, list)):
            res = (res,)
        jj = i % seg_blocks
        first_of_part = (jj == 0) | (jj == x_blocks)
        for o, ref, val in zip(outs, refs[n_in:], res):
            if o[0] == "row":
                ref[...] = val.astype(ref.dtype)
            elif o[0] == "xrow":
                @pl.when(jj < x_blocks)
                def _(ref=ref, val=val):
                    ref[...] = val.astype(ref.dtype)
            else:
                first = first_of_part if o[0] == "exacc" else i == 0

                @pl.when(first)
                def _(ref=ref, val=val):
                    ref[...] = val

                @pl.when(jnp.logical_not(first))
                def _(ref=ref, val=val):
                    ref[...] += val

    res = pl.pallas_call(
        kern, name=name, grid=(geo.r // tm,), in_specs=in_specs + carrier.in_specs(), out_specs=out_specs + carrier.out_specs(),
        out_shape=out_shapes + carrier.out_shapes(), scratch_shapes=carrier.scratch(), input_output_aliases=carrier.aliases(),
        compiler_params=_cparams("arbitrary"),
    )(*args, *carrier.operands())
    own, extra = carrier.results(res)
    if job:
        return (*own, extra)
    return own[0] if len(own) == 1 else own


def _colsum(v):
    return jnp.sum(v, axis=0, keepdims=True)


def _first_norm(geo, x_rows, ctx_rows, gain, mod, name):
    d = D_MODEL
    seg_blocks, x_blocks = geo.seg // 256, geo.s // 256

    def body(i, xv, cv, g, m):
        zv = jnp.where(i % seg_blocks >= x_blocks, cv, xv)
        r = lax.rsqrt(jnp.mean(zv * zv, axis=-1, keepdims=True) + EPS)
        return zv, (zv * r) * g * (1.0 + m[:, d:2 * d]) + m[:, :d]

    return _rowwise(name, body, geo, 256, [(x_rows, "xrow"), (ctx_rows, "crow"), (gain, "full"), (mod, "ex")],
                    [("row", d, F32), ("row", d, BF16)])


def _norm_mod_bwd(geo, z, gain, mod, off, dh, dz_skip, name, gated=None, latent_only=False, job=None):
    d = D_MODEL

    def body(i, zv, g, m, dhv, skip, *rest):
        r = lax.rsqrt(jnp.mean(zv * zv, axis=-1, keepdims=True) + EPS)
        n = zv * r
        dng = dhv * (1.0 + m[:, off + d:off + 2 * d])
        dn = dng * g
        dz = r * (dn - n * jnp.mean(dn * n, axis=-1, keepdims=True)) + skip
        res = (dz, _colsum(dhv), _colsum(dhv * (n * g)), _colsum(dng * n))
        if gated:
            ov, gm = rest
            res += (dz * gm[:, gated[2]:gated[2] + d], _colsum(dz * ov))
        return res

    ins = [(z, "row"), (gain, "full"), (mod, "ex"), (dh, "row"), (dz_skip, "row")]
    outs = [("xrow" if latent_only else "row", d, F32), ("exacc", d), ("exacc", d), ("gacc", 1, d)]
    if gated:
        ins += [(gated[0], "row"), (gated[1], "ex")]
        outs += [("row", d, BF16), ("exacc", d)]
    return _rowwise(name, body, geo, 256, ins, outs, job)


def _loss_head(geo, z, target, out, mod, off, name):
    seg_blocks, x_blocks = geo.seg // 256, geo.s // 256
    d = D_MODEL

    def body(i, zv, tv, ov, m):
        keep = jnp.where(i % seg_blocks >= x_blocks, 0.0, 1.0)
        err = (zv - tv) * keep
        part = 0.5 * jnp.sum(jnp.mean(err * err, axis=-1, keepdims=True), axis=0, keepdims=True)
        dz = err * (1.0 / d)
        return dz, jnp.broadcast_to(part, (1, LANES)), dz * m[:, off:off + d], _colsum(dz * ov)

    return _rowwise(name, body, geo, 256, [(z, "row"), (target, "xrow"), (out, "row"), (mod, "ex")],
                    [("row", d, F32), ("gacc", 1, LANES), ("row", d, BF16), ("exacc", d)])


Q_SCALE = HEAD_DIM ** -0.5
N_QK_CHUNKS = (N_HEADS + N_KV_HEADS) * HEAD_DIM // LANES
N_Q_CHUNKS = N_HEADS * HEAD_DIM // LANES


def _prep_tile(geo):
    return FFN_ROW_TILE if geo.seg % FFN_ROW_TILE == 0 else 256


def _attn_prep(geo, proj, cos, sin_signed, q_gain, k_gain, name):
    def body(i, p, cs, sn, qg, kg):
        outs = []
        for ch in range(N_QK_CHUNKS):
            is_q = ch < N_Q_CHUNKS
            outs.append(_qk_chunk(p[:, ch * LANES:(ch + 1) * LANES], qg if is_q else kg, cs, sn, Q_SCALE if is_q else 1.0))
        outs.append(p[:, N_QK_CHUNKS * LANES:])
        return jnp.concatenate(outs, axis=1)

    return _rowwise(name, body, geo, _prep_tile(geo), [(proj, "row"), (cos, "tab"), (sin_signed, "tab"), (q_gain, "full"), (k_gain, "full")],
                    [("row", proj.shape[1], BF16)])


def _attn_prep_bwd(geo, proj, cos, sin_signed, q_gain, k_gain, dq, dkv, name):
    kw = N_KV_HEADS * HEAD_DIM

    def body(i, p, cs, sn, qg, kg, dqv, dkvv):
        outs = []
        dgains = [jnp.zeros((1, LANES), F32), jnp.zeros((1, LANES), F32)]
        for ch in range(N_QK_CHUNKS):
            is_q = ch < N_Q_CHUNKS
            scale = Q_SCALE if is_q else 1.0
            ct = dqv[:, ch * LANES:(ch + 1) * LANES] if is_q else dkvv[:, (ch - N_Q_CHUNKS) * LANES:(ch - N_Q_CHUNKS + 1) * LANES]
            _, vjp = jax.vjp(lambda xx, gg, scale=scale: _qk_chunk(xx, gg, cs, sn, scale),
                             p[:, ch * LANES:(ch + 1) * LANES], qg if is_q else kg)
            dx, dg = vjp(ct)
            outs.append(dx)
            dgains[0 if is_q else 1] = dgains[0 if is_q else 1] + dg
        outs.append(dkvv[:, kw:])
        return jnp.concatenate(outs, axis=1), dgains[0], dgains[1]

    return _rowwise(name, body, geo, 256,
                    [(proj, "row"), (cos, "tab"), (sin_signed, "tab"), (q_gain, "full"), (k_gain, "full"), (dq, "row"), (dkv, "row")],
                    [("row", proj.shape[1], BF16), ("gacc", 1, LANES), ("gacc", 1, LANES)])


def _attn_geometry(geo):
    assert geo.s % ATTN_BLOCK == 0 and geo.l % ATTN_BLOCK == 0 and geo.seg >= BAND
    return geo.seg // ATTN_BLOCK, geo.s // ATTN_BLOCK


def _attn_mask(j, s0, geo):
    r = lax.broadcasted_iota(jnp.int32, (ATTN_BLOCK, geo.l + BAND), 0)
    n = lax.broadcasted_iota(jnp.int32, (ATTN_BLOCK, geo.l + BAND), 1) - geo.l
    dist = (s0 - j * ATTN_BLOCK) + n - r
    return (n < 0) | ((jnp.abs(dist) <= WINDOW) & (s0 + n < geo.s))


def _attn_probs(q, keys, valid, n_ctx, sink):
    s = _dot(q, keys, "nt")
    if valid is not None:
        s = jnp.where(valid, s, NEG_INF)
    m = jnp.maximum(jnp.max(s, axis=-1, keepdims=True), sink)
    e, e_sink = jnp.exp(s - m), jnp.exp(sink - m)
    inv = 1.0 / (jnp.sum(e, axis=-1, keepdims=True) + e_sink)
    return e * inv, e_sink * inv


def _attn_keys(ref, s0, geo, with_band):
    ctx = ref[geo.s:geo.seg, :]
    return jnp.concatenate([ctx, ref[pl.ds(s0, BAND), :]], axis=0) if with_band else ctx


def _attention(geo, qkv, sink, name, job=None):
    n_blocks, n_x_blocks = _attn_geometry(geo)
    qw, kw = N_HEADS * HEAD_DIM, N_KV_HEADS * HEAD_DIM
    group = N_HEADS // N_KV_HEADS
    carrier = _Carrier(job, 4, 1, 0)

    def kern(*refs):
        (sink_ref, q_ref, k_ref, v_ref, o_ref), job_refs = carrier.split(refs)
        j = pl.program_id(1)
        carrier.run(job_refs, pl.program_id(0) * n_blocks + j, geo.b * n_blocks)
        s0 = pl.multiple_of(jnp.clip((j - 1) * ATTN_BLOCK, 0, geo.seg - BAND), ATTN_BLOCK)

        def heads(with_band):
            valid = _attn_mask(j, s0, geo) if with_band else None
            k_all, v_all = _attn_keys(k_ref, s0, geo, with_band), _attn_keys(v_ref, s0, geo, with_band)
            for h in range(N_HEADS):
                kv = slice((h // group) * HEAD_DIM, (h // group + 1) * HEAD_DIM)
                p, _ = _attn_probs(q_ref[:, h * HEAD_DIM:(h + 1) * HEAD_DIM], k_all[:, kv], valid, geo.l, sink_ref[h])
                o_ref[:, h * HEAD_DIM:(h + 1) * HEAD_DIM] = _dot(p, v_all[:, kv], "nn").astype(BF16)

        pl.when(j < n_x_blocks)(lambda: heads(True))
        pl.when(j >= n_x_blocks)(lambda: heads(False))

    res = pl.pallas_call(
        kern, name=name, grid=(geo.b, n_blocks),
        in_specs=[pl.BlockSpec(memory_space=pltpu.SMEM),
                  pl.BlockSpec((ATTN_BLOCK, qw), lambda b, j: (b * n_blocks + j, 0)),
                  pl.BlockSpec((geo.seg, kw), lambda b, j: (b, qw // kw)),
                  pl.BlockSpec((geo.seg, kw), lambda b, j: (b, qw // kw + 1))] + carrier.in_specs(),
        out_specs=[pl.BlockSpec((ATTN_BLOCK, qw), lambda b, j: (b * n_blocks + j, 0))] + carrier.out_specs(),
        out_shape=[jax.ShapeDtypeStruct((geo.r, qw), BF16)] + carrier.out_shapes(),
        scratch_shapes=carrier.scratch(), input_output_aliases=carrier.aliases(),
        compiler_params=_cparams("arbitrary", "arbitrary"),
    )(sink, qkv, qkv, qkv, *carrier.operands())
    (o,), extra = carrier.results(res)
    return o, extra


def _attention_bwd(geo, qkv, sink, do, name, job=None):
    n_blocks, n_x_blocks = _attn_geometry(geo)
    qw, kw = N_HEADS * HEAD_DIM, N_KV_HEADS * HEAD_DIM
    group = N_HEADS // N_KV_HEADS

    carrier = _Carrier(job, 5, 3, 1)

    def kern(*refs):
        (sink_ref, q_ref, k_ref, v_ref, do_ref, dq_ref, dkv_out_ref, dsink_ref, dkv_ref), job_refs = carrier.split(refs)
        b, j = pl.program_id(0), pl.program_id(1)
        carrier.run(job_refs, b * n_blocks + j, geo.b * n_blocks)
        s0 = pl.multiple_of(jnp.clip((j - 1) * ATTN_BLOCK, 0, geo.seg - BAND), ATTN_BLOCK)

        @pl.when(j == 0)
        def _():
            dkv_ref[...] = jnp.zeros_like(dkv_ref)

        @pl.when((j == 0) & (b == 0))
        def _():
            dsink_ref[...] = jnp.zeros_like(dsink_ref)

        def heads(with_band):
            valid = _attn_mask(j, s0, geo) if with_band else None
            k_all, v_all = _attn_keys(k_ref, s0, geo, with_band), _attn_keys(v_ref, s0, geo, with_band)
            for g in range(N_KV_HEADS):
                kv = slice(g * HEAD_DIM, (g + 1) * HEAD_DIM)
                keys, vals = k_all[:, kv], v_all[:, kv]
                group_heads = [slice(h * HEAD_DIM, (h + 1) * HEAD_DIM) for h in range(g * group, (g + 1) * group)]
                ds_rows, p_rows = [], []
                for h, hs in zip(range(g * group, (g + 1) * group), group_heads):
                    dout = do_ref[:, hs]
                    p, p_sink = _attn_probs(q_ref[:, hs], keys, valid, geo.l, sink_ref[h])
                    dp = _dot(dout, vals, "nt")
                    dsum = jnp.sum(p * dp, axis=-1, keepdims=True)
                    ds = (p * (dp - dsum)).astype(BF16)
                    dq_ref[:, hs] = _dot(ds, keys, "nn").astype(BF16)
                    ds_rows.append(ds)
                    p_rows.append(p.astype(BF16))
                    dsink_ref[h:h + 1, :] += jnp.broadcast_to(-jnp.sum(p_sink * dsum, axis=0, keepdims=True), (1, LANES))
                q_rows = jnp.concatenate([q_ref[:, hs] for hs in group_heads], axis=0)
                do_rows = jnp.concatenate([do_ref[:, hs] for hs in group_heads], axis=0)
                dk = _dot(jnp.concatenate(ds_rows, axis=0), q_rows, "tn")
                dv = _dot(jnp.concatenate(p_rows, axis=0), do_rows, "tn")
                vv = slice(kw + g * HEAD_DIM, kw + (g + 1) * HEAD_DIM)
                dkv_ref[geo.s:geo.seg, kv] += dk[:geo.l]
                dkv_ref[geo.s:geo.seg, vv] += dv[:geo.l]
                if with_band:
                    dkv_ref[pl.ds(s0, BAND), kv] += dk[geo.l:]
                    dkv_ref[pl.ds(s0, BAND), vv] += dv[geo.l:]

        pl.when(j < n_x_blocks)(lambda: heads(True))
        pl.when(j >= n_x_blocks)(lambda: heads(False))

        @pl.when(j == n_blocks - 1)
        def _():
            dkv_out_ref[...] = dkv_ref[...].astype(BF16)

    res = pl.pallas_call(
        kern, name=name, grid=(geo.b, n_blocks),
        in_specs=[pl.BlockSpec(memory_space=pltpu.SMEM),
                  pl.BlockSpec((ATTN_BLOCK, qw), lambda b, j: (b * n_blocks + j, 0)),
                  pl.BlockSpec((geo.seg, kw), lambda b, j: (b, qw // kw)),
                  pl.BlockSpec((geo.seg, kw), lambda b, j: (b, qw // kw + 1)),
                  pl.BlockSpec((ATTN_BLOCK, qw), lambda b, j: (b * n_blocks + j, 0))] + carrier.in_specs(),
        out_specs=[pl.BlockSpec((ATTN_BLOCK, qw), lambda b, j: (b * n_blocks + j, 0)),
                   pl.BlockSpec((geo.seg, 2 * kw), lambda b, j: (b, 0)),
                   pl.BlockSpec((N_HEADS, LANES), lambda b, j: (0, 0))] + carrier.out_specs(),
        out_shape=[jax.ShapeDtypeStruct((geo.r, qw), BF16), jax.ShapeDtypeStruct((geo.r, 2 * kw), BF16),
                   jax.ShapeDtypeStruct((N_HEADS, LANES), F32)] + carrier.out_shapes(),
        scratch_shapes=[pltpu.VMEM((geo.seg, 2 * kw), F32)] + carrier.scratch(), input_output_aliases=carrier.aliases(),
        compiler_params=_cparams("arbitrary", "arbitrary"),
    )(sink, qkv, qkv, qkv, do, *carrier.operands())
    (dq, dkv, dsink), extra = carrier.results(res)
    return dq, dkv, dsink, extra


RET_QK_W = RET_HEADS * RET_QK_DIM
K_SCALE = RET_QK_DIM ** -0.5


RET_ROW_TILE = 384


def _ret_tile(geo):
    return RET_ROW_TILE if geo.seg % RET_ROW_TILE == 0 else 256


def _ret_prep(geo, proj, cos, sin_signed, name):
    def body(i, p, cs, sn):
        cs2, sn2 = jnp.concatenate([cs] * RET_HEADS, axis=1), jnp.concatenate([sn] * RET_HEADS, axis=1)
        q = _rope(p[:, :RET_QK_W], cs2, sn2, RET_QK_DIM // 4)
        k = _rope(p[:, RET_QK_W:2 * RET_QK_W], cs2, sn2, RET_QK_DIM // 4) * K_SCALE
        return jnp.concatenate([q, k, p[:, 2 * RET_QK_W:]], axis=1)

    return _rowwise(name, body, geo, _ret_tile(geo), [(proj, ("rowc", 2 * RET_QK_W + RET_VWIDTH, 0)), (cos, "tab"), (sin_signed, "tab")],
                    [("row", 2 * RET_QK_W + RET_VWIDTH, BF16)])


def _ret_prep_bwd(geo, dq, dk, dv, dgate, cos, sin_signed, name):
    def body(i, dqv, dkv, dvv, dg, cs, sn):
        cs2, sn2 = jnp.concatenate([cs] * RET_HEADS, axis=1), jnp.concatenate([sn] * RET_HEADS, axis=1)
        dkv = dkv * K_SCALE
        dqv = dqv * cs2 + _swap_halves(dqv * sn2, RET_QK_DIM // 4)
        dkv = dkv * cs2 + _swap_halves(dkv * sn2, RET_QK_DIM // 4)
        return jnp.concatenate([dqv, dkv, dvv, dg], axis=1)

    return _rowwise(name, body, geo, _ret_tile(geo),
                    [(dq, "row"), (dk, "row"), (dv, "row"), (dgate, "row"), (cos, "tab"), (sin_signed, "tab")],
                    [("row", 2 * RET_QK_W + 2 * RET_VWIDTH, BF16)])


def _ret_step(state, q, k, v, lg, rev):
    c = RET_CHUNK
    ri = lax.broadcasted_iota(jnp.int32, (c, 1), 0).astype(F32)
    cj = lax.broadcasted_iota(jnp.int32, (1, c), 1).astype(F32)
    if rev:
        dist, q_decay, k_decay = cj - ri, jnp.exp(lg * (c - ri)), jnp.exp(lg * ri)
    else:
        dist, q_decay, k_decay = ri - cj, jnp.exp(lg * (ri + 1.0)), jnp.exp(lg * (c - 1.0 - ri))
    intra = jnp.where(dist >= 0, jnp.exp(lg * jnp.maximum(dist, 0.0)), 0.0)
    scores = _mm(q, k, "nt") * intra
    out = _mm(scores, v, "nn") + _mm(q, state, "nn") * q_decay
    new_state = state * jnp.exp(lg * c) + _mm(k * k_decay, v, "tn")
    return new_state, out


def _ret_state0(kc, vc, lg, rev):
    n = kc.shape[0]
    t = lax.broadcasted_iota(jnp.int32, (n, 1), 0).astype(F32)
    decay = jnp.exp(lg * t) if rev else jnp.exp(lg * (n - 1.0 - t))
    return _mm(kc * decay, vc, "tn")


def _ret_specs(geo):
    nq = RET_HEADS
    return [pl.BlockSpec((2 * RET_HEADS, LANES), lambda b, h: (0, 0)),
            pl.BlockSpec((geo.seg, RET_QK_DIM), lambda b, h: (b, h)),
            pl.BlockSpec((geo.seg, RET_QK_DIM), lambda b, h: (b, nq + h)),
            pl.BlockSpec((geo.seg, RET_V_DIM), lambda b, h: (b, nq + h))]


def _retention(geo, qkv, log_g, name):
    nc = geo.s // RET_CHUNK

    def kern(lg_ref, q_ref, k_ref, v_ref, o_ref, st_ref):
        h = pl.program_id(1)
        for d, rev in ((0, False), (1, True)):
            lg = lg_ref[pl.ds(d * RET_HEADS + h, 1), 0:1]
            st_ref[...] = _ret_state0(k_ref[geo.s:geo.seg, :].astype(F32), v_ref[geo.s:geo.seg, :].astype(F32), lg, rev)

            def chunk(ci, carry, d=d, rev=rev, lg=lg):
                r0 = pl.multiple_of((nc - 1 - ci if rev else ci) * RET_CHUNK, RET_CHUNK)
                rows = pl.ds(r0, RET_CHUNK)
                new_state, out = _ret_step(st_ref[...], q_ref[rows, :], k_ref[rows, :], v_ref[rows, :], lg, rev)
                st_ref[...] = new_state
                if d == 0:
                    o_ref[rows, :] = out
                else:
                    o_ref[rows, :] += out
                return carry

            lax.fori_loop(0, nc, chunk, 0)
        o_ref[geo.s:geo.seg, :] = jnp.zeros((geo.l, RET_V_DIM), F32)

    return pl.pallas_call(
        kern, name=name, grid=(geo.b, RET_HEADS), in_specs=_ret_specs(geo),
        out_specs=pl.BlockSpec((geo.seg, RET_V_DIM), lambda b, h: (b, h)),
        out_shape=jax.ShapeDtypeStruct((geo.r, RET_VWIDTH), F32),
        scratch_shapes=[pltpu.VMEM((RET_QK_DIM, RET_V_DIM), F32)],
        compiler_params=_cparams("parallel", "arbitrary"),
    )(log_g, qkv, qkv, qkv)


def _retention_bwd(geo, qkv, log_g, do, name):
    nc = geo.s // RET_CHUNK
    ctx = slice(geo.s, geo.seg)

    def kern(lg_ref, q_ref, k_ref, v_ref, do_ref, dq_ref, dk_ref, dv_ref, dlg_ref, states_ref, dst_ref, aq_ref, ak_ref, av_ref):
        b, h = pl.program_id(0), pl.program_id(1)

        @pl.when((b == 0) & (h == 0))
        def _():
            dlg_ref[...] = jnp.zeros_like(dlg_ref)

        for d, rev in ((0, False), (1, True)):
            row = pl.ds(d * RET_HEADS + h, 1)
            lg = lg_ref[row, 0:1]
            kc, vc = k_ref[ctx, :].astype(F32), v_ref[ctx, :].astype(F32)
            states_ref[0] = _ret_state0(kc, vc, lg, rev)

            def rows_of(ci, rev=rev):
                return pl.ds(pl.multiple_of((nc - 1 - ci if rev else ci) * RET_CHUNK, RET_CHUNK), RET_CHUNK)

            def load(rows):
                return q_ref[rows, :].astype(F32), k_ref[rows, :].astype(F32), v_ref[rows, :].astype(F32)

            def replay(ci, carry, rev=rev, lg=lg, rows_of=rows_of, load=load):
                states_ref[ci + 1] = _ret_step(states_ref[ci], *load(rows_of(ci)), lg, rev)[0]
                return carry

            lax.fori_loop(0, nc - 1, replay, 0)
            dst_ref[...] = jnp.zeros_like(dst_ref)

            def emit(rows, dq, dk, dv, d=d):
                if d == 0:
                    ak_ref[rows, :], av_ref[rows, :] = dk, dv
                    if dq is not None:
                        aq_ref[rows, :] = dq
                else:
                    dk_ref[rows, :] = (ak_ref[rows, :] + dk).astype(BF16)
                    dv_ref[rows, :] = (av_ref[rows, :] + dv).astype(BF16)
                    if dq is not None:
                        dq_ref[rows, :] = (aq_ref[rows, :] + dq).astype(BF16)

            def back(t, dlg, rev=rev, lg=lg, rows_of=rows_of, load=load, emit=emit):
                ci = nc - 1 - t
                rows = rows_of(ci)
                _, vjp = jax.vjp(lambda st, q, k, v, g: _ret_step(st, q, k, v, g, rev), states_ref[ci], *load(rows), lg)
                dstate, dq, dk, dv, dg = vjp((dst_ref[...], do_ref[rows, :].astype(F32)))
                dst_ref[...] = dstate
                emit(rows, dq, dk, dv)
                return dlg + dg

            dlg = lax.fori_loop(0, nc, back, jnp.zeros((1, 1), F32))
            _, vjp = jax.vjp(lambda kk, vv, g: _ret_state0(kk, vv, g, rev), kc, vc, lg)
            dkc, dvc, dg = vjp(dst_ref[...])
            emit(ctx, None, dkc, dvc)
            dlg_ref[row, :] += jnp.broadcast_to(dlg + dg, (1, LANES))
        dq_ref[ctx, :] = jnp.zeros((geo.l, RET_QK_DIM), BF16)

    nq = RET_HEADS
    return pl.pallas_call(
        kern, name=name, grid=(geo.b, RET_HEADS),
        in_specs=_ret_specs(geo) + [pl.BlockSpec((geo.seg, RET_V_DIM), lambda b, h: (b, h))],
        out_specs=[pl.BlockSpec((geo.seg, RET_QK_DIM), lambda b, h: (b, h)),
                   pl.BlockSpec((geo.seg, RET_QK_DIM), lambda b, h: (b, h)),
                   pl.BlockSpec((geo.seg, RET_V_DIM), lambda b, h: (b, h)),
                   pl.BlockSpec((2 * RET_HEADS, LANES), lambda b, h: (0, 0))],
        out_shape=[jax.ShapeDtypeStruct((geo.r, RET_QK_W), BF16), jax.ShapeDtypeStruct((geo.r, RET_QK_W), BF16),
                   jax.ShapeDtypeStruct((geo.r, RET_VWIDTH), BF16), jax.ShapeDtypeStruct((2 * RET_HEADS, LANES), F32)],
        scratch_shapes=[pltpu.VMEM((nc, RET_QK_DIM, RET_V_DIM), F32), pltpu.VMEM((RET_QK_DIM, RET_V_DIM), F32),
                        pltpu.VMEM((geo.seg, RET_QK_DIM), F32), pltpu.VMEM((geo.seg, RET_QK_DIM), F32),
                        pltpu.VMEM((geo.seg, RET_V_DIM), F32)],
        compiler_params=_cparams("arbitrary", "arbitrary"),
    )(log_g, qkv, qkv, qkv, do)


def _gated(o, g, gain):
    outs = []
    for h in range(RET_HEADS):
        cols = slice(h * RET_V_DIM, (h + 1) * RET_V_DIM)
        oh = o[:, cols]
        mu = jnp.mean(oh, axis=-1, keepdims=True)
        var = jnp.mean(jnp.square(oh - mu), axis=-1, keepdims=True)
        outs.append(_silu(g[:, cols]) * ((oh - mu) * lax.rsqrt(var + EPS) * gain[:, cols]))
    return jnp.concatenate(outs, axis=1)


def _ret_gated(geo, o, proj, gain, name):
    def body(i, ov, gv, gn):
        return _gated(ov, gv, gn)

    gate_block = (2 * RET_QK_W + RET_VWIDTH) // RET_VWIDTH
    return _rowwise(name, body, geo, _ret_tile(geo), [(o, "row"), (proj, ("rowc", RET_VWIDTH, gate_block)), (gain, "full")],
                    [("row", RET_VWIDTH, BF16)])


def _ret_gated_bwd(geo, o, proj, gain, dout, name):
    def body(i, ov, gv, gn, dv):
        _, vjp = jax.vjp(_gated, ov, gv, gn)
        return vjp(dv)

    gate_block = (2 * RET_QK_W + RET_VWIDTH) // RET_VWIDTH
    return _rowwise(name, body, geo, 256,
                    [(o, "row"), (proj, ("rowc", RET_VWIDTH, gate_block)), (gain, "full"), (dout, "row")],
                    [("row", RET_VWIDTH, BF16), ("row", RET_VWIDTH, BF16), ("gacc", 1, RET_VWIDTH)])


def _whole(name, fn, out_shapes, *arrays):
    n = len(arrays)

    def kern(*refs):
        res = fn(*[r[...] for r in refs[:n]])
        for ref, val in zip(refs[n:], res):
            ref[...] = val.astype(ref.dtype)

    return pl.pallas_call(kern, name=name, out_shape=out_shapes)(*arrays)


def _rope_tables(geo, head_dim):
    rows = geo.s // GRID_W
    row = jnp.broadcast_to(jnp.arange(rows, dtype=jnp.int32)[:, None], (rows, GRID_W)).reshape(geo.s)
    col = jnp.broadcast_to(jnp.arange(GRID_W, dtype=jnp.int32)[None, :], (rows, GRID_W)).reshape(geo.s)
    axis_dim = head_dim // 2
    inv = ROPE_BASE ** (-jnp.arange(0, axis_dim, 2, dtype=F32) / axis_dim)
    ang_r = row.astype(F32)[:, None] * inv
    ang_c = col.astype(F32)[:, None] * inv
    cos = jnp.concatenate([jnp.cos(ang_r)] * 2 + [jnp.cos(ang_c)] * 2, axis=1)
    sin = jnp.concatenate([-jnp.sin(ang_r), jnp.sin(ang_r), -jnp.sin(ang_c), jnp.sin(ang_c)], axis=1)
    cos = jnp.concatenate([cos, jnp.ones((geo.l, head_dim), F32)], axis=0)
    sin = jnp.concatenate([sin, jnp.zeros((geo.l, head_dim), F32)], axis=0)
    reps = max(1, LANES // head_dim)
    return jnp.tile(cos, (1, reps)), jnp.tile(sin, (1, reps))


def _row_tile(r):
    return next(t for t in (1536, 1024, 512, 256, 128) if r % t == 0)


MOD_ROWS = 8


def _local_step(x, ctx, target, sp, wts, mods, plan=None):
    nb, s, d = x.shape
    geo = _Rows(nb, s, ctx.shape[1])
    assert nb + 1 <= MOD_ROWS and d == D_MODEL
    tm = _row_tile(geo.r)
    cos64, sin64 = _rope_tables(geo, HEAD_DIM)
    cos256, sin256 = _rope_tables(geo, RET_QK_DIM)
    q_gain = jnp.tile(sp["q_norm"].reshape(1, HEAD_DIM), (1, LANES // HEAD_DIM))
    k_gain = jnp.tile(sp["k_norm"].reshape(1, HEAD_DIM), (1, LANES // HEAD_DIM))
    sink = sp["sink"].reshape(N_HEADS)
    log_g = jnp.broadcast_to(sp["log_g"].reshape(2 * RET_HEADS, 1), (2 * RET_HEADS, LANES))
    gn_g = sp["gn_g"].reshape(1, RET_VWIDTH)

    saved = []
    z, h1 = _first_norm(geo, x.reshape(nb * s, d), ctx.reshape(nb * geo.l, d), sp["norm1_g"][0][None, :], mods[0], "norm1_0")
    for i in range(2):
        mod3 = mods[i]
        n1, n2 = sp["norm1_g"][i][None, :], sp["norm2_g"][i][None, :]
        if i == 0:
            proj = _mm_nn(h1, wts["attn_qkv"], F32, "attn_qkv", tm, wts["attn_qkv"].shape[1], d)
            prep = _attn_prep(geo, proj, cos64, sin64, q_gain, k_gain, "attn_prep")
            o, late = _attention(geo, prep, sink, "attn", plan.gather_job() if plan else None)
            if plan:
                plan.late_weights(late, wts)
            oraw = None
            w_o = wts["attn_o"]
        else:
            proj = _mm_nn(h1, wts["ret_qkvg"], BF16, "ret_qkvg", tm, wts["ret_qkvg"].shape[2], d)
            prep = _ret_prep(geo, proj, cos256, sin256, "ret_prep")
            oraw = _retention(geo, prep, log_g, "ret")
            o = _ret_gated(geo, oraw, proj, gn_g, "ret_gated")
            w_o = wts["ret_o"]
        zmid, mix, h2 = _mm_nn_gate_residual(geo, o, w_o, z, mod3, 2 * d, f"mix_out{i}", norm=(n2, mod3, 3 * d))
        u, a = _ffn_in_swiglu(h2, wts["ffn_in"][i], f"ffn_in{i}")
        next_norm = (sp["norm1_g"][1][None, :], mods[1], 0) if i == 0 else None
        zout, f, h1_next = _mm_nn_gate_residual(geo, a, wts["ffn_out"][i], zmid, mod3, 5 * d, f"ffn_out{i}", norm=next_norm)
        saved.append(dict(z=z, mod3=mod3, n1=n1, n2=n2, h1=h1, proj=proj, prep=prep, o=o, oraw=oraw, mix=mix, zmid=zmid,
                          h2=h2, u=u, a=a, f=f))
        z, h1 = zout, h1_next

    dz, loss, df, dg2 = _loss_head(geo, z, target.reshape(nb * s, d), saved[1]["f"], saved[1]["mod3"], 5 * d, "loss")

    big, small = {}, {}
    dmods = [None, None]
    for i in (1, 0):
        sv = saved[i]
        mod3 = sv["mod3"]
        carry = plan is not None and i == 0
        du, land = _ffn_out_dx_swiglu_bwd(df, wts["ffn_out"][i], sv["u"], f"ffn_out_dx{i}", plan.layer1.swap_job() if carry else None)
        if carry:
            plan.layer1.after_swap(land)
        big[f"ffn_out{i}"] = _mm_tn(sv["a"], df, f"ffn_out_dw{i}", D_FF // 2, 1024, tm, out_dtype=BF16).reshape(N_CHIPS, D_FF // N_CHIPS, d)
        n4 = wts["ffn_in"][i].shape[2]
        dh2 = _mm_nt(du, wts["ffn_in"][i], BF16, f"ffn_in_dx{i}", tm, 1024, n4)
        big[f"ffn_in{i}"] = _mm_tn(sv["h2"], du, f"ffn_in_dw{i}", 1024, n4, tm, shards=N_CHIPS, out_dtype=BF16)
        if carry:
            plan.start_layer0_ffn(big)
        dzmid, dsh2, dsc2, dn2, dmix, dg1, *land = _norm_mod_bwd(geo, sv["zmid"], sv["n2"], mod3, 3 * d, dh2, dz, f"norm2_bwd{i}",
                                                                 gated=(sv["mix"], mod3, 2 * d),
                                                                 job=plan.layer0_ffn.swap_job() if carry else None)
        if carry:
            plan.layer0_ffn.after_swap(land[0])
        if i == 0:
            do = _mm_nt(dmix, wts["attn_o"], BF16, "attn_out_dx", tm, 1024, 1024)
            big["attn_o"] = _mm_tn(sv["o"], dmix, "attn_out_dw", 1024, 1024, tm, out_dtype=BF16).reshape(N_CHIPS, 1024 // N_CHIPS, d)
            dq, dkv, dsink, land = _attention_bwd(geo, sv["prep"], sink, do, "attn_bwd", plan.exchange_job() if plan else None)
            if plan:
                plan.after_exchange(land)
            dproj, dqg, dkg = _attn_prep_bwd(geo, sv["proj"], cos64, sin64, q_gain, k_gain, dq, dkv, "attn_prep_bwd")
            small["q_norm"] = dqg[0, :HEAD_DIM] + dqg[0, HEAD_DIM:]
            small["k_norm"] = dkg[0, :HEAD_DIM] + dkg[0, HEAD_DIM:]
            small["sink"] = dsink[:, 0]
            wq = wts["attn_qkv"]
            dh1 = _mm_nt(dproj, wq, BF16, "attn_qkv_dx", tm, 1024, wq.shape[1])
            dwq = _mm_tn(sv["h1"], dproj, "attn_qkv_dw", 1024, wq.shape[1], tm, out_dtype=BF16)
            big["attn_qkv"] = dwq.reshape(d, N_CHIPS, -1).transpose(1, 0, 2)
        else:
            do = _mm_nt(dmix, wts["ret_o"], BF16, "ret_out_dx", tm, 1024, 1024)
            big["ret_o"] = _mm_tn(sv["o"], dmix, "ret_out_dw", 1024, 1024, tm, out_dtype=BF16).reshape(N_CHIPS, RET_VWIDTH // N_CHIPS, d)
            doraw, dgate, dgn = _ret_gated_bwd(geo, sv["oraw"], sv["proj"], gn_g, do, "ret_gated_bwd")
            small["gn_g"] = dgn[0]
            dq, dk, dv, dlg = _retention_bwd(geo, sv["prep"], log_g, doraw, "ret_bwd")
            small["log_g"] = dlg[:, 0].reshape(2, RET_HEADS)
            dproj = _ret_prep_bwd(geo, dq, dk, dv, dgate, cos256, sin256, "ret_prep_bwd")
            wq = wts["ret_qkvg"]
            dh1 = _mm_nt(dproj, wq, BF16, "ret_qkvg_dx", tm, 1024, wq.shape[2])
            big["ret_qkvg"] = _mm_tn(sv["h1"], dproj, "ret_qkvg_dw", 1024, wq.shape[2], tm, shards=N_CHIPS, out_dtype=BF16)
        below = (saved[0]["f"], saved[0]["mod3"], 5 * d) if i == 1 else None
        dz, dsh1, dsc1, dn1, *below_grads = _norm_mod_bwd(geo, sv["z"], sv["n1"], mod3, 0, dh1, dzmid, f"norm1_bwd{i}", gated=below,
                                                              latent_only=i == 0)
        small[f"norm1_g{i}"], small[f"norm2_g{i}"] = dn1[0], dn2[0]
        parts = [dsh1, dsc1, dg1, dsh2, dsc2, dg2]
        rows = jnp.concatenate([jnp.concatenate([p[:nb, 0, :] for p in parts], axis=1),
                                jnp.concatenate([jnp.sum(p[nb:, 0, :], axis=0, keepdims=True) for p in parts], axis=1),
                                jnp.zeros((MOD_ROWS - nb - 1, 6 * d), F32)], axis=0)
        dmods[i] = rows
        if below_grads:
            df, dg2 = below_grads
        small[f"ada_b{i}"] = jnp.sum(rows, axis=0)
        if plan and i == 1:
            plan.start_layer1(big)
    return loss, dz, big, small, dmods


def _adamw(w, g, m, v, name):
    rows, cols = w.shape
    tr = next((t for t in (512, 256, 128, 64, 32, 16, 8) if rows % t == 0), rows)
    c1 = 1.0 - ADAM_B1 ** ADAM_STEP
    c2 = 1.0 - ADAM_B2 ** ADAM_STEP

    def kern(w_ref, g_ref, m_ref, v_ref, d_ref, nm_ref, nv_ref):
        gv = g_ref[...]
        nm = ADAM_B1 * m_ref[...] + (1.0 - ADAM_B1) * gv
        nv = ADAM_B2 * v_ref[...] + (1.0 - ADAM_B2) * jnp.square(gv)
        d_ref[...] = -ADAM_LR * ((nm / c1) / (jnp.sqrt(nv / c2) + ADAM_EPS) + ADAM_WD * w_ref[...])
        nm_ref[...] = nm
        nv_ref[...] = nv

    spec = pl.BlockSpec((tr, cols), lambda i: (i, 0))
    return pl.pallas_call(
        kern, name=name, grid=(rows // tr,), in_specs=[spec] * 4, out_specs=[spec] * 3,
        out_shape=[jax.ShapeDtypeStruct(w.shape, F32)] * 3, compiler_params=_cparams("parallel"),
    )(w, g, m, v)


N_DEVICES = 8


def _mesh_pos():
    return lax.axis_index("x"), lax.axis_index("y"), lax.axis_index("c")


def _other_chips(x, y):
    return [(1 - x, y), (x, 1 - y), (1 - x, 1 - y)]


def _hbm(n):
    return [pl.BlockSpec(memory_space=pl.ANY)] * n


def _remote(src, dst, send_sem, recv_sem, device):
    return pltpu.make_async_remote_copy(src_ref=src, dst_ref=dst, send_sem=send_sem, recv_sem=recv_sem,
                                        device_id=device, device_id_type=MESH)


def _scalar_spec(grid, in_specs, out_specs):
    return pltpu.PrefetchScalarGridSpec(num_scalar_prefetch=1, grid=grid, in_specs=in_specs, out_specs=out_specs)


def _place_shard(param, layer, pos, name):
    _, r, cols = param.shape
    tr = _slab_tile(r)

    def kern(pos_ref, s_ref, o_ref):
        o_ref[...] = s_ref[...].astype(BF16)

    return pl.pallas_call(
        kern, name=name, out_shape=jax.ShapeDtypeStruct((N_CHIPS, r, cols), BF16),
        grid_spec=_scalar_spec((r // tr,), [pl.BlockSpec((None, tr, cols), lambda i, p: (layer, i, 0))],
                               pl.BlockSpec((None, tr, cols), lambda i, p: (p[1], i, 0))),
        compiler_params=_cparams("parallel"),
    )(pos, param)


class _CommJob:
    def __init__(self, inputs, out_shapes, aliases, sem_shapes, stages, fractions=None):
        self.inputs, self.out_shapes, self.aliases, self.sem_shapes, self.stages = inputs, out_shapes, aliases, sem_shapes, stages
        self.fractions = fractions


def _merge_jobs(a, b):
    assert len(a.stages) == len(b.stages)
    ni, no, ns = len(a.inputs), len(a.out_shapes), len(a.sem_shapes)

    def both(sa, sb):
        def stage(ins, outs, sems):
            sa(ins[:ni], outs[:no], sems[:ns])
            sb(ins[ni:], outs[no:], sems[ns:])
        return stage

    aliases = dict(a.aliases)
    aliases.update({ni + i: no + o for i, o in b.aliases.items()})
    return _CommJob(a.inputs + b.inputs, a.out_shapes + b.out_shapes, aliases, a.sem_shapes + b.sem_shapes,
                    [both(sa, sb) for sa, sb in zip(a.stages, b.stages)])


def _run_job(job, name):
    n_in, n_out = len(job.inputs), len(job.out_shapes)

    def body(*refs):
        for stage in job.stages:
            stage(refs[:n_in], refs[n_in:n_in + n_out], refs[n_in + n_out:])

    return pl.pallas_call(
        body, name=name, in_specs=_hbm(n_in), out_specs=_hbm(n_out), out_shape=job.out_shapes,
        input_output_aliases=job.aliases, scratch_shapes=job.sem_shapes,
    )(*job.inputs)


def _job_marks(job, steps):
    mid = len(job.stages) - 2
    fractions = job.fractions or [(s + 1) / (mid + 1) for s in range(mid)]
    return [0] + [min(steps - 1, 1 + int((steps - 1) * f)) for f in fractions] + [steps - 1]


def _gather_job(placed):
    n = len(placed)

    def half(w, which):
        r2 = placed[w].shape[1] // 2
        return pl.ds(which * r2, r2)

    def ici_copies(outs, sems, slot_of, arrays=range(n)):
        x, y, c = _mesh_pos()
        res = []
        for w in arrays:
            for k, (px, py) in enumerate(_other_chips(x, y)):
                slab = outs[w].at[slot_of(x, y, px, py), half(w, c)]
                res.append((slab, _remote(slab, slab, sems[0].at[w, k], sems[1].at[w, k], (px, py, c))))
        return res

    def forwards(outs, sems, which_core, arrays=range(n)):
        x, y, c = _mesh_pos()
        res = []
        for w in arrays:
            for k, (px, py) in enumerate(_other_chips(x, y)):
                slab = outs[w].at[2 * px + py, half(w, which_core(c))]
                res.append(_remote(slab, slab, sems[2].at[w, k], sems[3].at[w, k], (x, y, 1 - c)))
        return res

    def send(ins, outs, sems):
        for _, cp in ici_copies(outs, sems, lambda x, y, px, py: 2 * x + y):
            cp.start()

    def forward_of(w):
        def forward(ins, outs, sems):
            arrivals = ici_copies(outs, sems, lambda x, y, px, py: 2 * px + py, [w])
            for (_, arrival), fwd in zip(arrivals, forwards(outs, sems, lambda c: c, [w])):
                arrival.wait_recv()
                fwd.start()
        return forward

    def finish(ins, outs, sems):
        for cp in forwards(outs, sems, lambda c: 1 - c):
            cp.wait_recv()
        for _, cp in ici_copies(outs, sems, lambda x, y, px, py: 2 * x + y):
            cp.wait_send()
        for cp in forwards(outs, sems, lambda c: c):
            cp.wait_send()

    sizes = [p.shape[1] * p.shape[2] for p in placed]
    fractions = [sum(sizes[:w + 1]) / sum(sizes) for w in range(n)]
    return _CommJob(list(placed), [jax.ShapeDtypeStruct(p.shape, p.dtype) for p in placed], {w: w for w in range(n)},
                    [pltpu.SemaphoreType.DMA((n, 3))] * 4, [send] + [forward_of(w) for w in range(n)] + [finish], fractions)


def _pair_swap_job(grads):
    n = len(grads)

    def copies(ins, outs, sems):
        x, y, c = _mesh_pos()
        res = []
        for w in range(n):
            r2 = grads[w].shape[1] // 2
            res.append(_remote(ins[w].at[:, pl.ds((1 - c) * r2, r2)], outs[w], sems[0].at[w], sems[1].at[w], (x, y, 1 - c)))
        return res

    def send(ins, outs, sems):
        for cp in copies(ins, outs, sems):
            cp.start()

    def finish(ins, outs, sems):
        for cp in copies(ins, outs, sems):
            cp.wait()

    return _CommJob(list(grads), [jax.ShapeDtypeStruct((N_CHIPS, g.shape[1] // 2, g.shape[2]), g.dtype) for g in grads], {},
                    [pltpu.SemaphoreType.DMA((n,))] * 2, [send, finish])


def _chip_exchange_job(hs):
    n = len(hs)

    def send(ins, outs, sems):
        x, y, c = _mesh_pos()
        for w in range(n):
            for k, (px, py) in enumerate(_other_chips(x, y)):
                _remote(ins[w].at[2 * px + py], outs[w].at[2 * x + y], sems[0].at[w, k], sems[1].at[w, k], (px, py, c)).start()

    def finish(ins, outs, sems):
        x, y, c = _mesh_pos()
        for w in range(n):
            for k, (px, py) in enumerate(_other_chips(x, y)):
                got = outs[w].at[2 * px + py]
                cp = _remote(ins[w].at[2 * px + py], got, sems[0].at[w, k], sems[1].at[w, k], (px, py, c))
                cp.wait_recv()
                cp.wait_send()

    return _CommJob(list(hs), [jax.ShapeDtypeStruct(h.shape, h.dtype) for h in hs], {},
                    [pltpu.SemaphoreType.DMA((n, 3))] * 2, [send, finish])


def _pair_share(ts, name):
    n = len(ts)

    def body(*refs):
        outs = refs[n:2 * n]
        send_sems, recv_sems = refs[2 * n:]
        x, y, c = _mesh_pos()
        sends = []
        for w in range(n):
            r2 = ts[w].shape[0] // 2
            mine = outs[w].at[pl.ds(c * r2, r2)]
            rc = _remote(mine, mine, send_sems.at[w], recv_sems.at[w], (x, y, 1 - c))
            rc.start()
            sends.append(rc)
        for w in range(n):
            r2 = ts[w].shape[0] // 2
            theirs = outs[w].at[pl.ds((1 - c) * r2, r2)]
            _remote(theirs, theirs, send_sems.at[w], recv_sems.at[w], (x, y, 1 - c)).wait_recv()
            sends[w].wait_send()

    return pl.pallas_call(
        body, name=name, in_specs=_hbm(n), out_specs=_hbm(n),
        out_shape=[jax.ShapeDtypeStruct(t.shape, F32) for t in ts],
        input_output_aliases={w: w for w in range(n)},
        scratch_shapes=[pltpu.SemaphoreType.DMA((n,))] * 2,
    )(*ts)


def _slab_tile(rows):
    return next(t for t in (512, 256, 176, 128, 64, 32, 16) if rows % t == 0)


def _sum_pair(grad, land, pos, name):
    _, r2, cols = land.shape
    tr = _slab_tile(r2)
    nt = r2 // tr

    def kern(pos_ref, a_ref, b_ref, o_ref):
        o_ref[...] = (a_ref[...].astype(F32) + b_ref[...].astype(F32)).astype(BF16)

    spec = pl.BlockSpec((None, tr, cols), lambda j, i, p: (j, i, 0))
    return pl.pallas_call(
        kern, name=name, out_shape=jax.ShapeDtypeStruct(land.shape, BF16),
        grid_spec=_scalar_spec((N_CHIPS, nt), [pl.BlockSpec((None, tr, cols), lambda j, i, p: (j, p[0] * nt + i, 0)), spec], spec),
        compiler_params=_cparams("parallel", "parallel"),
    )(pos, grad, land)


def _sum_chips(hs, land, pos, name):
    _, r2, cols = land.shape
    tr = _slab_tile(r2)
    nt = r2 // tr

    def kern(pos_ref, h_ref, l_ref, o_ref):
        acc = jnp.zeros((tr, cols), F32)
        own = h_ref[...].astype(F32)
        for k in range(N_CHIPS):
            acc = acc + jnp.where(pos_ref[1] == k, own, l_ref[k].astype(F32))
        o_ref[...] = acc

    return pl.pallas_call(
        kern, name=name, out_shape=jax.ShapeDtypeStruct((2 * r2, cols), F32),
        grid_spec=_scalar_spec((nt,), [pl.BlockSpec((None, tr, cols), lambda i, p: (p[1], i, 0)),
                                       pl.BlockSpec((N_CHIPS, tr, cols), lambda i, p: (0, i, 0))],
                               pl.BlockSpec((tr, cols), lambda i, p: (p[0] * nt + i, 0))),
        compiler_params=_cparams("parallel"),
    )(pos, hs, land)


class _ReduceScatter:
    def __init__(self, grads, pos, tag):
        self.grads, self.pos, self.tag = list(grads), pos, tag

    def swap_job(self):
        return _pair_swap_job(self.grads)

    def after_swap(self, land):
        self.hs = [_sum_pair(g, l, self.pos, f"grads_pair_sum_{self.tag}{w}") for w, (g, l) in enumerate(zip(self.grads, land))]

    def exchange_job(self):
        return _chip_exchange_job(self.hs)

    def after_exchange(self, land2):
        return [_sum_chips(h, l, self.pos, f"grads_chip_sum_{self.tag}{w}") for w, (h, l) in enumerate(zip(self.hs, land2))]

    def run(self):
        self.after_swap(_run_job(self.swap_job(), f"grads_pair_swap_{self.tag}"))
        return self.after_exchange(_run_job(self.exchange_job(), f"grads_chip_exchange_{self.tag}"))


EARLY_WEIGHTS = ("attn_qkv",)
LATE_WEIGHTS = ("ffn_in0", "ffn_in1", "ffn_out0", "ffn_out1", "attn_o", "ret_qkvg", "ret_o")
LAYER1_GRADS = ("ffn_out1", "ffn_in1", "ret_o", "ret_qkvg")
LAYER0_FFN_GRADS = ("ffn_out0", "ffn_in0")
LAST_GRADS = ("attn_o", "attn_qkv")


def _fill_weights(wts, full):
    for name, w in full.items():
        if name[:-1] == "ffn_in":
            wts[name[:-1]][int(name[-1])] = w
        elif name[:-1] == "ffn_out":
            wts["ffn_out"][int(name[-1])] = w.reshape(-1, w.shape[2])
        elif name in ("attn_o", "ret_o"):
            wts[name] = w.reshape(-1, w.shape[2])
        elif name == "attn_qkv":
            wts[name] = w.transpose(1, 0, 2).reshape(w.shape[1], -1)
        else:
            wts[name] = w


class _StepPlan:
    def __init__(self, placed, pos):
        self.placed, self.pos = placed, pos
        self.layer1 = self.layer0_ffn = None
        self.reduced = {}

    def gather_job(self):
        return _gather_job([self.placed[k] for k in LATE_WEIGHTS])

    def late_weights(self, outs, wts):
        _fill_weights(wts, dict(zip(LATE_WEIGHTS, outs)))

    def start_layer1(self, big):
        self.layer1 = _ReduceScatter([big[k] for k in LAYER1_GRADS], self.pos, "l1_")

    def start_layer0_ffn(self, big):
        self.layer0_ffn = _ReduceScatter([big[k] for k in LAYER0_FFN_GRADS], self.pos, "l0f_")

    def exchange_job(self):
        return _merge_jobs(self.layer1.exchange_job(), self.layer0_ffn.exchange_job())

    def after_exchange(self, land):
        n1 = len(LAYER1_GRADS)
        self.reduced.update(zip(LAYER1_GRADS, self.layer1.after_exchange(land[:n1])))
        self.reduced.update(zip(LAYER0_FFN_GRADS, self.layer0_ffn.after_exchange(land[n1:])))


def _all_reduce_small(v, name):
    def body(v_ref, o_ref, land_ref, send_sems, recv_sems):
        x, y, c = _mesh_pos()
        me = 4 * x + 2 * y + c
        land_ref[me] = v_ref[...]
        for t in range(N_DEVICES):
            @pl.when(t != me)
            def _(t=t):
                _remote(v_ref, land_ref.at[me], send_sems.at[t], recv_sems.at[me], (t // 4, (t // 2) % 2, t % 2)).start()
        for t in range(N_DEVICES):
            @pl.when(t != me)
            def _(t=t):
                _remote(v_ref, land_ref.at[t], send_sems.at[t], recv_sems.at[t], (t // 4, (t // 2) % 2, t % 2)).wait()
        acc = land_ref[0]
        for t in range(1, N_DEVICES):
            acc = acc + land_ref[t]
        o_ref[...] = acc

    vmem = pl.BlockSpec(memory_space=pltpu.VMEM)
    return pl.pallas_call(
        body, name=name, in_specs=[vmem], out_specs=vmem, out_shape=jax.ShapeDtypeStruct(v.shape, F32),
        scratch_shapes=[pltpu.VMEM((N_DEVICES,) + v.shape, F32), pltpu.SemaphoreType.DMA((N_DEVICES,)),
                        pltpu.SemaphoreType.DMA((N_DEVICES,))],
    )(v)


def _all_to_all_small(v, name):
    def body(v_ref, o_ref, send_sems, recv_sems):
        x, y, c = _mesh_pos()
        me = 4 * x + 2 * y + c
        o_ref[me] = v_ref[me]
        for t in range(N_DEVICES):
            @pl.when(t != me)
            def _(t=t):
                _remote(v_ref.at[t], o_ref.at[me], send_sems.at[t], recv_sems.at[me], (t // 4, (t // 2) % 2, t % 2)).start()
        for t in range(N_DEVICES):
            @pl.when(t != me)
            def _(t=t):
                _remote(v_ref.at[t], o_ref.at[t], send_sems.at[t], recv_sems.at[t], (t // 4, (t // 2) % 2, t % 2)).wait()

    vmem = pl.BlockSpec(memory_space=pltpu.VMEM)
    return pl.pallas_call(
        body, name=name, in_specs=[vmem], out_specs=vmem, out_shape=jax.ShapeDtypeStruct(v.shape, F32),
        scratch_shapes=[pltpu.SemaphoreType.DMA((N_DEVICES,)), pltpu.SemaphoreType.DMA((N_DEVICES,))],
    )(v)


ALL_ROWS = 40


class _AdaLN:
    def __init__(self, c, c_ctx, ada_w, ada_b, riders):
        xi, yi, ci = _mesh_pos()
        self.me, self.chip, self.core = 4 * xi + 2 * yi + ci, 2 * xi + yi, ci
        self.nb, d = c.shape
        self.ada_w, self.c_ctx = ada_w, c_ctx
        self.cols = ada_w.shape[2]
        ctx_row = self.nb * N_DEVICES
        assert ctx_row + 1 + riders.shape[0] <= ALL_ROWS
        placed = lax.dynamic_update_slice(jnp.zeros((ALL_ROWS, d), F32), c, (self.me * self.nb, 0))
        placed = lax.dynamic_update_slice(placed, riders, (ctx_row + 1, 0))
        summed = _all_reduce_small(placed, "gather_conditioning")
        self.riders = summed[ctx_row + 1:ctx_row + 1 + riders.shape[0]]
        c_all = summed.at[ctx_row].set(c_ctx)
        self.cact, = _whole("cond_silu", lambda v: (_silu(v),), [jax.ShapeDtypeStruct(c_all.shape, F32)], c_all)
        parts = []
        for i in range(2):
            bias = lax.dynamic_slice(ada_b[i], (self.chip * self.cols,), (self.cols,))[None, :]
            parts.append(_mm_nn(self.cact, ada_w[i], F32, f"mod{i}", ALL_ROWS, self.cols, d, bias=bias))
        part = jnp.concatenate(parts, axis=1)
        rows = [[t * self.nb + b for b in range(self.nb)] + [ctx_row] * (MOD_ROWS - self.nb) for t in range(N_DEVICES)]
        got = _all_to_all_small(part[jnp.asarray(rows)], "mod_exchange")
        self.mods = [jnp.concatenate([got[2 * j][:self.nb + 1, i * self.cols:(i + 1) * self.cols] for j in range(N_CHIPS)], axis=1)[:, None, :]
                     for i in range(2)]

    def backward(self, dmods):
        nb, cols, d = self.nb, self.cols, self.ada_w.shape[1]
        blocks = [jnp.concatenate([dm[:, j * cols:(j + 1) * cols] for dm in dmods], axis=1) for j in range(N_CHIPS)]
        got = _all_to_all_small(jnp.stack([blocks[t // 2] for t in range(N_DEVICES)]), "dmod_exchange")
        dall = jnp.concatenate([got[:, :nb].reshape(N_DEVICES * nb, 2 * cols), jnp.sum(got[:, nb], axis=0, keepdims=True),
                                jnp.zeros((ALL_ROWS - N_DEVICES * nb - 1, 2 * cols), F32)], axis=0)
        dctx = jnp.concatenate([dall[N_DEVICES * nb][None, :], jnp.zeros((MOD_ROWS - 1, 2 * cols), F32)], axis=0)
        grads, dcact = [], []
        for i in range(2):
            grads.append(_mm_tn(self.cact, dall[:, i * cols:(i + 1) * cols], f"ada_dw{i}", d, cols, ALL_ROWS))
            dcact.append(_mm_nt(dctx[:, i * cols:(i + 1) * cols], self.ada_w[i], F32, f"ada_dx{i}", MOD_ROWS, d, cols))

        def silu_bwd(v, d0, d1):
            sg = _sigmoid(v)
            return ((d0 + d1)[0:1] * (sg * (1.0 + v * (1.0 - sg))),)

        dc_ctx, = _whole("cond_silu_bwd", silu_bwd, [jax.ShapeDtypeStruct((1, d), F32)], self.c_ctx[None, :], dcact[0], dcact[1])
        return grads, jnp.where(self.core == 0, dc_ctx[0], jnp.zeros((d,), F32))


SMALL_ROWS = 24


def _pack_small(small, dlogit):
    d = D_MODEL
    misc = jnp.zeros((d,), F32)
    misc = misc.at[0:HEAD_DIM].set(small["q_norm"]).at[128:128 + HEAD_DIM].set(small["k_norm"])
    misc = misc.at[256:256 + N_HEADS].set(small["sink"]).at[384:384 + 2 * RET_HEADS].set(dlogit.reshape(-1))
    rows = [small["ada_b0"].reshape(6, d), small["ada_b1"].reshape(6, d), small["norm1_g0"][None], small["norm1_g1"][None],
            small["norm2_g0"][None], small["norm2_g1"][None], small["c_ctx"][None], small["gn_g"].reshape(2, d), misc[None]]
    buf = jnp.concatenate(rows, axis=0)
    return jnp.concatenate([buf, jnp.zeros((SMALL_ROWS - buf.shape[0], d), F32)], axis=0)


def _unpack_small(buf):
    d = D_MODEL
    misc = buf[19]
    return dict(ada_b=buf[0:12].reshape(2, 6 * d), norm1_g=buf[12:14], norm2_g=buf[14:16], c_ctx=buf[16],
                gn_g=buf[17:19].reshape(2 * d), q_norm=misc[0:HEAD_DIM], k_norm=misc[128:128 + HEAD_DIM],
                sink=misc[256:256 + N_HEADS], decay=misc[384:384 + 2 * RET_HEADS])


def kernel(x, c, ctx, c_ctx, ada_w, ada_b, norm1_g, norm2_g, ffn_w_in, ffn_w_out, attn_w_qkv, attn_q_norm, attn_k_norm, attn_sink, attn_w_o, ret_w_qkvg, ret_decay_logit, ret_gn_g, ret_w_o, loss_target, m_c_ctx, m_ada_w, m_ada_b, m_norm1_g, m_norm2_g, m_ffn_w_in, m_ffn_w_out, m_attn_w_qkv, m_attn_q_norm, m_attn_k_norm, m_attn_sink, m_attn_w_o, m_ret_w_qkvg, m_ret_decay_logit, m_ret_gn_g, m_ret_w_o, v_c_ctx, v_ada_w, v_ada_b, v_norm1_g, v_norm2_g, v_ffn_w_in, v_ffn_w_out, v_attn_w_qkv, v_attn_q_norm, v_attn_k_norm, v_attn_sink, v_attn_w_o, v_ret_w_qkvg, v_ret_decay_logit, v_ret_gn_g, v_ret_w_o):
    xi, yi, ci = _mesh_pos()
    chip = 2 * xi + yi
    nb, s, d = x.shape
    gn_shard = ret_gn_g.shape[1]

    shards = dict(ffn_in0=(ffn_w_in, 0), ffn_in1=(ffn_w_in, 1), ffn_out0=(ffn_w_out, 0),
                  ffn_out1=(ffn_w_out, 1), attn_qkv=(attn_w_qkv, 0), attn_o=(attn_w_o, 0), ret_qkvg=(ret_w_qkvg, 0), ret_o=(ret_w_o, 0))
    names = list(shards)
    pos = jnp.stack([ci, chip]).astype(jnp.int32)
    placed = {k: _place_shard(*shards[k], pos, f"place_{k}") for k in names}
    early = _run_job(_gather_job([placed[k] for k in EARLY_WEIGHTS]), "gather_early_weights")
    gn_mine = jnp.where(ci == 0, ret_gn_g[0], jnp.zeros_like(ret_gn_g[0]))
    gn_place = lax.dynamic_update_slice(jnp.zeros((RET_VWIDTH,), F32), gn_mine, (chip * gn_shard,))

    wts = dict(ffn_in=[None, None], ffn_out=[None, None], attn_qkv=None, attn_o=None, ret_qkvg=None, ret_o=None)
    ada = _AdaLN(c, c_ctx, ada_w, ada_b, riders=gn_place.reshape(2, d))
    gn_full = ada.riders.reshape(RET_VWIDTH)
    _fill_weights(wts, dict(zip(EARLY_WEIGHTS, early)))
    plan = _StepPlan(placed, pos)
    decay_logit = ret_decay_logit[0]
    sp = dict(norm1_g=norm1_g, norm2_g=norm2_g, q_norm=attn_q_norm[0], k_norm=attn_k_norm[0],
              sink=attn_sink[0], log_g=jax.nn.log_sigmoid(decay_logit), gn_g=gn_full)
    loss_part, dz, big, small, dmods = _local_step(x, ctx, loss_target, sp, wts, ada.mods, plan)
    ada_grads, small["c_ctx"] = ada.backward(dmods)

    loss = lax.psum(loss_part[0, 0], ("x", "y", "c"))
    grad_x = dz.reshape(nb, s, d)

    dlogit = small["log_g"] * jax.nn.sigmoid(-decay_logit)
    sg = _unpack_small(_all_reduce_small(_pack_small(small, dlogit), "reduce_small_grads"))
    halves = dict(plan.reduced)
    halves.update(zip(LAST_GRADS, _ReduceScatter([big[k] for k in LAST_GRADS], pos, "last_").run()))
    reduced = dict(zip(halves, _pair_share(list(halves.values()), "grads_pair_share")))

    grads = dict(
        c_ctx=sg["c_ctx"], ada_w=jnp.stack(ada_grads), ada_b=sg["ada_b"], norm1_g=sg["norm1_g"],
        norm2_g=sg["norm2_g"], ffn_w_in=jnp.stack([reduced["ffn_in0"], reduced["ffn_in1"]]),
        ffn_w_out=jnp.stack([reduced["ffn_out0"], reduced["ffn_out1"]]), attn_w_qkv=reduced["attn_qkv"][None],
        attn_q_norm=sg["q_norm"][None], attn_k_norm=sg["k_norm"][None], attn_sink=sg["sink"][None],
        attn_w_o=reduced["attn_o"][None], ret_w_qkvg=reduced["ret_qkvg"][None], ret_decay_logit=sg["decay"].reshape(1, 2, RET_HEADS),
        ret_gn_g=lax.dynamic_slice(sg["gn_g"], (chip * gn_shard,), (gn_shard,))[None], ret_w_o=reduced["ret_o"][None])
    params = dict(c_ctx=(c_ctx, m_c_ctx, v_c_ctx), ada_w=(ada_w, m_ada_w, v_ada_w), ada_b=(ada_b, m_ada_b, v_ada_b),
                  norm1_g=(norm1_g, m_norm1_g, v_norm1_g), norm2_g=(norm2_g, m_norm2_g, v_norm2_g),
                  ffn_w_in=(ffn_w_in, m_ffn_w_in, v_ffn_w_in), ffn_w_out=(ffn_w_out, m_ffn_w_out, v_ffn_w_out),
                  attn_w_qkv=(attn_w_qkv, m_attn_w_qkv, v_attn_w_qkv), attn_q_norm=(attn_q_norm, m_attn_q_norm, v_attn_q_norm),
                  attn_k_norm=(attn_k_norm, m_attn_k_norm, v_attn_k_norm), attn_sink=(attn_sink, m_attn_sink, v_attn_sink),
                  attn_w_o=(attn_w_o, m_attn_w_o, v_attn_w_o), ret_w_qkvg=(ret_w_qkvg, m_ret_w_qkvg, v_ret_w_qkvg),
                  ret_decay_logit=(ret_decay_logit, m_ret_decay_logit, v_ret_decay_logit),
                  ret_gn_g=(ret_gn_g, m_ret_gn_g, v_ret_gn_g), ret_w_o=(ret_w_o, m_ret_w_o, v_ret_w_o))
    order = list(params)
    deltas, new_m, new_v = [], [], []
    for k in order:
        w, m, v = params[k]
        g = grads[k].reshape(w.shape)
        grads[k] = g
        flat = (-1, w.shape[-1]) if w.ndim > 1 else (1, -1)
        if k == "ret_decay_logit":
            flat = (1, -1)
        dw, nm, nv = _adamw(w.reshape(flat), g.reshape(flat), m.reshape(flat), v.reshape(flat), f"adamw_{k}")
        deltas.append(dw.reshape(w.shape))
        new_m.append(nm.reshape(w.shape))
        new_v.append(nv.reshape(w.shape))
    return (loss, grad_x, *[grads[k] for k in order], *deltas, *new_m, *new_v)
```

```python
import functools

import jax
import jax.numpy as jnp
from jax import lax
from jax.experimental import pallas as pl
from jax.experimental.pallas import tpu as pltpu

F32 = jnp.float32
BF16 = jnp.bfloat16

D_MODEL = 1024
N_HEADS = 16
N_KV_HEADS = 4
HEAD_DIM = 64
WINDOW = 128
ATTN_BLOCK = 128
BAND = ATTN_BLOCK + 2 * WINDOW
RET_HEADS = 4
RET_QK_DIM = 256
RET_V_DIM = 512
RET_VWIDTH = 2048
RET_CHUNK = 128
D_FF = 2816
GRID_W = 64
ROPE_BASE = 10000.0
EPS = 1e-6
NEG_INF = -1e30
LANES = 128

ADAM_LR = 0.001
ADAM_B1 = 0.9
ADAM_B2 = 0.999
ADAM_EPS = 1e-08
ADAM_WD = 0.01
ADAM_STEP = 10

VMEM_LIMIT_BYTES = 56 * 1024 * 1024
MESH = pl.DeviceIdType.MESH
N_CHIPS = 4


def _cparams(*sem):
    return pltpu.CompilerParams(dimension_semantics=sem, vmem_limit_bytes=VMEM_LIMIT_BYTES)


_DIMS = {"nn": ((1,), (0,)), "nt": ((1,), (1,)), "tn": ((0,), (0,))}


def _dot(a, b, form):
    return lax.dot_general(a.astype(BF16), b.astype(BF16), (_DIMS[form], ((), ())), preferred_element_type=F32)


@functools.partial(jax.custom_vjp, nondiff_argnums=(2,))
def _mm(a, b, form):
    return _dot(a, b, form)


def _mm_fwd(a, b, form):
    return _dot(a, b, form), (a, b)


def _mm_bwd(form, res, ct):
    a, b = res
    if form == "nn":
        da, db = _dot(ct, b, "nt"), _dot(a, ct, "tn")
    elif form == "nt":
        da, db = _dot(ct, b, "nn"), _dot(ct, a, "tn")
    else:
        da, db = _dot(b, ct, "nt"), _dot(a, ct, "nn")
    return da.astype(a.dtype), db.astype(b.dtype)


_mm.defvjp(_mm_fwd, _mm_bwd)


def _swap_halves(x, half):
    w = x.shape[-1]
    lane = lax.broadcasted_iota(jnp.int32, x.shape, x.ndim - 1)
    return jnp.where(lane % (2 * half) < half, pltpu.roll(x, w - half, x.ndim - 1), pltpu.roll(x, half, x.ndim - 1))


@functools.partial(jax.custom_vjp, nondiff_argnums=(1,))
def _rot(x, half):
    return _swap_halves(x, half)


def _rot_fwd(x, half):
    return _swap_halves(x, half), None


def _rot_bwd(half, _, ct):
    return (_swap_halves(ct, half),)


_rot.defvjp(_rot_fwd, _rot_bwd)


def _rope(x, cos, sin_signed, half):
    return x * cos + _rot(x, half) * sin_signed


def _head_mean_square(x):
    r = lax.broadcasted_iota(jnp.int32, (LANES, LANES), 0) // HEAD_DIM
    c = lax.broadcasted_iota(jnp.int32, (LANES, LANES), 1) // HEAD_DIM
    g = jnp.where(r == c, 1.0 / HEAD_DIM, 0.0).astype(F32)
    return jnp.dot(x * x, g, precision=lax.Precision.HIGHEST, preferred_element_type=F32)


def _qk_chunk(x, gain, cos, sin_signed, scale):
    y = x * lax.rsqrt(_head_mean_square(x) + EPS) * gain
    return _rope(y, cos, sin_signed, HEAD_DIM // 4) * scale


def _sigmoid(x):
    return 1.0 / (1.0 + jnp.exp(-x))


def _silu(x):
    return x * _sigmoid(x)


def _mm_nn(a, w, out_dtype, name, tm, tn, tk, bias=None):
    m, k_dim = a.shape
    if w.ndim == 3:
        n = w.shape[0] * w.shape[2]
        per = w.shape[2] // tn
        assert w.shape[2] % tn == 0
        w_spec = pl.BlockSpec((None, tk, tn), lambda i, j, k: (j // per, k, j % per))
    else:
        n = w.shape[1]
        w_spec = pl.BlockSpec((tk, tn), lambda i, j, k: (k, j))
    assert m % tm == 0 and n % tn == 0 and k_dim % tk == 0, (name, a.shape, w.shape, tm, tn, tk)
    nk = k_dim // tk
    has_bias = bias is not None

    def body(*refs):
        a_ref, w_ref = refs[0], refs[1]
        b_ref = refs[2] if has_bias else None
        o_ref, acc_ref = (refs[-1], None) if nk == 1 else (refs[-2], refs[-1])
        if nk == 1:
            part = jnp.dot(a_ref[...].astype(BF16), w_ref[...].astype(BF16), preferred_element_type=F32)
            o_ref[...] = (part + b_ref[...] if has_bias else part).astype(out_dtype)
            return
        k = pl.program_id(2)

        @pl.when(k == 0)
        def _():
            acc_ref[...] = jnp.zeros_like(acc_ref)

        acc_ref[...] += jnp.dot(a_ref[...].astype(BF16), w_ref[...].astype(BF16), preferred_element_type=F32)

        @pl.when(k == nk - 1)
        def _():
            r = acc_ref[...]
            if has_bias:
                r = r + b_ref[...]
            o_ref[...] = r.astype(out_dtype)

    in_specs = [pl.BlockSpec((tm, tk), lambda i, j, k: (i, k)), w_spec]
    args = [a, w]
    if has_bias:
        in_specs.append(pl.BlockSpec((1, tn), lambda i, j, k: (0, j)))
        args.append(bias)
    return pl.pallas_call(
        body, name=name, grid=(m // tm, n // tn, nk), in_specs=in_specs,
        out_specs=pl.BlockSpec((tm, tn), lambda i, j, k: (i, j)),
        out_shape=jax.ShapeDtypeStruct((m, n), out_dtype),
        scratch_shapes=[pltpu.VMEM((tm, tn), F32)] if nk > 1 else [],
        compiler_params=_cparams("parallel", "parallel", "arbitrary"),
    )(*args)


def _mm_nt(a, w, out_dtype, name, tm, tn, tk):
    if a.ndim == 3:
        planes, m, plane_w = a.shape
        c_dim = planes * plane_w
        a_per = plane_w // tk
        assert plane_w % tk == 0
        a_spec = pl.BlockSpec((None, tm, tk), lambda i, j, k: (k // a_per, i, k % a_per))
    else:
        m, c_dim = a.shape
        a_spec = pl.BlockSpec((tm, tk), lambda i, j, k: (i, k))
    if w.ndim == 3:
        k_out = w.shape[1]
        per = w.shape[2] // tk
        assert w.shape[2] % tk == 0 and w.shape[0] * w.shape[2] == c_dim
        w_spec = pl.BlockSpec((None, tn, tk), lambda i, j, k: (k // per, j, k % per))
    else:
        k_out = w.shape[0]
        assert w.shape[1] == c_dim
        w_spec = pl.BlockSpec((tn, tk), lambda i, j, k: (j, k))
    assert m % tm == 0 and k_out % tn == 0 and c_dim % tk == 0, (name, a.shape, w.shape, tm, tn, tk)
    nk = c_dim // tk

    def body(a_ref, w_ref, o_ref, acc_ref=None):
        if nk == 1:
            o_ref[...] = _dot(a_ref[...], w_ref[...], "nt").astype(out_dtype)
            return
        k = pl.program_id(2)

        @pl.when(k == 0)
        def _():
            acc_ref[...] = jnp.zeros_like(acc_ref)

        acc_ref[...] += _dot(a_ref[...], w_ref[...], "nt")

        @pl.when(k == nk - 1)
        def _():
            o_ref[...] = acc_ref[...].astype(out_dtype)

    return pl.pallas_call(
        body, name=name, grid=(m // tm, k_out // tn, nk),
        in_specs=[a_spec, w_spec],
        out_specs=pl.BlockSpec((tm, tn), lambda i, j, k: (i, j)),
        out_shape=jax.ShapeDtypeStruct((m, k_out), out_dtype),
        scratch_shapes=[pltpu.VMEM((tm, tn), F32)] if nk > 1 else [],
        compiler_params=_cparams("parallel", "parallel", "arbitrary"),
    )(a, w)


def _mm_tn(a, b, name, tm, tn, tk, shards=None, out_dtype=F32):
    r, k_dim = a.shape
    if b.ndim == 3:
        n = b.shape[0] * b.shape[2]
        b_per = b.shape[2] // tn
        assert b.shape[2] % tn == 0
        b_spec = pl.BlockSpec((None, tk, tn), lambda i, j, k: (j // b_per, k, j % b_per))
    else:
        n = b.shape[1]
        b_spec = pl.BlockSpec((tk, tn), lambda i, j, k: (k, j))
    assert r % tk == 0 and k_dim % tm == 0 and n % tn == 0, (name, a.shape, b.shape, tm, tn, tk)
    nk = r // tk
    if shards:
        per = n // shards // tn
        assert n % (shards * tn) == 0
        out_shape = jax.ShapeDtypeStruct((shards, k_dim, n // shards), out_dtype)
        out_spec = pl.BlockSpec((None, tm, tn), lambda i, j, k: (j // per, i, j % per))
    else:
        out_shape = jax.ShapeDtypeStruct((k_dim, n), out_dtype)
        out_spec = pl.BlockSpec((tm, tn), lambda i, j, k: (i, j))
    direct = out_dtype == F32

    def body(a_ref, b_ref, o_ref, *scratch):
        acc_ref = o_ref if direct else scratch[0]
        k = pl.program_id(2)

        @pl.when(k == 0)
        def _():
            acc_ref[...] = jnp.zeros_like(acc_ref)

        acc_ref[...] += _dot(a_ref[...], b_ref[...], "tn")
        if not direct:
            @pl.when(k == nk - 1)
            def _():
                o_ref[...] = acc_ref[...].astype(out_dtype)

    return pl.pallas_call(
        body, name=name, grid=(k_dim // tm, n // tn, nk),
        in_specs=[pl.BlockSpec((tk, tm), lambda i, j, k: (k, i)), b_spec],
        out_specs=out_spec, out_shape=out_shape,
        scratch_shapes=[] if direct else [pltpu.VMEM((tm, tn), F32)],
        compiler_params=_cparams("parallel", "parallel", "arbitrary"),
    )(a, b)


class _Carrier:
    def __init__(self, job, n_in, n_out, n_scratch):
        self.job, self.n_in, self.n_out, self.n_scratch = job, n_in, n_out, n_scratch
        self.ji = len(job.inputs) if job else 0
        self.jo = len(job.out_shapes) if job else 0

    def operands(self):
        return list(self.job.inputs) if self.job else []

    def in_specs(self):
        return [pl.BlockSpec(memory_space=pl.ANY)] * self.ji

    def out_specs(self):
        return [pl.BlockSpec(memory_space=pl.ANY)] * self.jo

    def out_shapes(self):
        return list(self.job.out_shapes) if self.job else []

    def scratch(self):
        return list(self.job.sem_shapes) if self.job else []

    def aliases(self):
        return {self.n_in + a: self.n_out + b for a, b in self.job.aliases.items()} if self.job else {}

    def split(self, refs):
        a = self.n_in
        b = a + self.ji
        c = b + self.n_out
        d = c + self.jo
        e = d + self.n_scratch
        return list(refs[:a]) + list(refs[b:c]) + list(refs[d:e]), (refs[a:b], refs[c:d], refs[e:])

    def run(self, job_refs, step, steps):
        if not self.job:
            return
        for stage, mark in zip(self.job.stages, _job_marks(self.job, steps)):
            pl.when(step == mark)(functools.partial(stage, *job_refs))

    def results(self, res):
        res = list(res)
        return res[:self.n_out], res[self.n_out:]


FFN_ROW_TILE = 768


def _ffn_tile(r):
    return FFN_ROW_TILE if r % FFN_ROW_TILE == 0 else _row_tile(r)


def _ffn_in_swiglu(h, w, name):
    r, k_dim = h.shape
    n4 = w.shape[2]
    tm = _ffn_tile(r)

    def body(h_ref, wg_ref, wu_ref, u_ref, a_ref):
        hv = h_ref[...]
        g = jnp.dot(hv, wg_ref[...], preferred_element_type=F32)
        up = jnp.dot(hv, wu_ref[...], preferred_element_type=F32)
        u_ref[0] = g.astype(BF16)
        u_ref[1] = up.astype(BF16)
        a_ref[...] = (_silu(g) * up).astype(BF16)

    return pl.pallas_call(
        body, name=name, grid=(r // tm, 2),
        in_specs=[pl.BlockSpec((tm, k_dim), lambda i, j: (i, 0)),
                  pl.BlockSpec((None, k_dim, n4), lambda i, j: (j, 0, 0)),
                  pl.BlockSpec((None, k_dim, n4), lambda i, j: (j + 2, 0, 0))],
        out_specs=[pl.BlockSpec((2, tm, n4), lambda i, j: (0, i, j)), pl.BlockSpec((tm, n4), lambda i, j: (i, j))],
        out_shape=[jax.ShapeDtypeStruct((2, r, 2 * n4), BF16), jax.ShapeDtypeStruct((r, 2 * n4), BF16)],
        compiler_params=_cparams("parallel", "parallel"),
    )(h, w, w)


def _mm_nn_gate_residual(geo, a, w, z, mod, off, name, norm=None):
    r, k_dim = a.shape
    n = w.shape[1]
    tm = FFN_ROW_TILE if geo.seg % FFN_ROW_TILE == 0 else 256
    tiles = geo.seg // tm
    assert geo.seg % tm == 0 and r == geo.r and n == D_MODEL

    def body(a_ref, w_ref, z_ref, mx_ref, mc_ref, *rest):
        out = jnp.dot(a_ref[...], w_ref[...], preferred_element_type=F32)
        is_x = (pl.program_id(0) % tiles) * tm + lax.broadcasted_iota(jnp.int32, (tm, 1), 0) < geo.s
        zo = z_ref[...] + jnp.where(is_x, mx_ref[:, off:off + n], mc_ref[:, off:off + n]) * out
        if norm:
            g_ref, nx_ref, nc_ref, zo_ref, raw_ref, h_ref = rest
            no = norm[2]
            shift = jnp.where(is_x, nx_ref[:, no:no + n], nc_ref[:, no:no + n])
            scale = jnp.where(is_x, nx_ref[:, no + n:no + 2 * n], nc_ref[:, no + n:no + 2 * n])
            rs = lax.rsqrt(jnp.mean(zo * zo, axis=-1, keepdims=True) + EPS)
            h_ref[...] = ((zo * rs) * g_ref[...] * (1.0 + scale) + shift).astype(BF16)
        else:
            zo_ref, raw_ref = rest
        zo_ref[...] = zo
        raw_ref[...] = out.astype(BF16)

    def mod_specs(m):
        return [pl.BlockSpec((None, 1, m.shape[2]), lambda i: (i // tiles, 0, 0)), pl.BlockSpec((None, 1, m.shape[2]), lambda i: (geo.b, 0, 0))]

    row = pl.BlockSpec((tm, n), lambda i: (i, 0))
    in_specs = [pl.BlockSpec((tm, k_dim), lambda i: (i, 0)), pl.BlockSpec((k_dim, n), lambda i: (0, 0)), row] + mod_specs(mod)
    args = [a, w, z, mod, mod]
    out_specs, out_shape = [row, row], [jax.ShapeDtypeStruct((r, n), F32), jax.ShapeDtypeStruct((r, n), BF16)]
    if norm:
        in_specs += [pl.BlockSpec((1, n), lambda i: (0, 0))] + mod_specs(norm[1])
        args += [norm[0], norm[1], norm[1]]
        out_specs.append(row)
        out_shape.append(jax.ShapeDtypeStruct((r, n), BF16))
    res = pl.pallas_call(body, name=name, grid=(r // tm,), in_specs=in_specs, out_specs=out_specs, out_shape=out_shape,
                         compiler_params=_cparams("parallel"))(*args)
    return res if norm else (*res, None)


def _ffn_out_dx_swiglu_bwd(df, w_out, u, name, job=None):
    r, d = df.shape
    n4 = u.shape[2] // 2
    tm = _ffn_tile(r)
    carrier = _Carrier(job, 3, 1, 0)
    steps = (r // tm) * 2

    def body(*refs):
        (df_ref, w_ref, u_ref, du_ref), job_refs = carrier.split(refs)
        carrier.run(job_refs, pl.program_id(0) * 2 + pl.program_id(1), steps)
        da = _dot(df_ref[...], w_ref[...], "nt")
        g, up = u_ref[0].astype(F32), u_ref[1].astype(F32)
        s = _sigmoid(g)
        du_ref[0] = (da * up * (s * (1.0 + g * (1.0 - s)))).astype(BF16)
        du_ref[1] = (da * (g * s)).astype(BF16)

    res = pl.pallas_call(
        body, name=name, grid=(r // tm, 2),
        in_specs=[pl.BlockSpec((tm, d), lambda i, j: (i, 0)), pl.BlockSpec((n4, d), lambda i, j: (j, 0)),
                  pl.BlockSpec((2, tm, n4), lambda i, j: (0, i, j))] + carrier.in_specs(),
        out_specs=[pl.BlockSpec((2, tm, n4), lambda i, j: (0, i, j))] + carrier.out_specs(),
        out_shape=[jax.ShapeDtypeStruct(u.shape, BF16)] + carrier.out_shapes(),
        scratch_shapes=carrier.scratch(), input_output_aliases=carrier.aliases(),
        compiler_params=_cparams("arbitrary", "arbitrary"),
    )(df, w_out, u, *carrier.operands())
    (du,), extra = carrier.results(res)
    return du, extra


class _Rows:
    def __init__(self, b, s, l):
        self.b, self.s, self.l = b, s, l
        self.seg = s + l
        self.r = b * self.seg


def _rowwise(name, body, geo, tm, ins, outs, job=None):
    seg_blocks, x_blocks = geo.seg // tm, geo.s // tm
    per_part = {"ex", "xrow", "crow"} & {k for _, k in ins if isinstance(k, str)} or {"exacc", "xrow"} & {o[0] for o in outs}
    assert geo.seg % tm == 0 and (geo.s % tm == 0 or not per_part), (name, tm)
    nb = geo.b

    def is_ctx(i):
        return i % seg_blocks >= x_blocks

    in_specs, args = [], []
    for arr, kind in ins:
        args.append(arr)
        if kind == "row":
            in_specs.append(pl.BlockSpec((tm, arr.shape[1]), lambda i: (i, 0)))
        elif kind == "ex":
            in_specs.append(pl.BlockSpec((None, 1, arr.shape[2]), lambda i: (jnp.where(is_ctx(i), nb, i // seg_blocks), 0, 0)))
        elif kind == "full":
            in_specs.append(pl.BlockSpec(arr.shape, lambda i, nd=arr.ndim: (0,) * nd))
        elif kind == "tab":
            in_specs.append(pl.BlockSpec((tm, arr.shape[1]), lambda i: (i % seg_blocks, 0)))
        elif kind == "xrow":
            in_specs.append(pl.BlockSpec(
                (tm, arr.shape[1]), lambda i: ((i // seg_blocks) * x_blocks + jnp.minimum(i % seg_blocks, x_blocks - 1), 0)))
        elif kind == "crow":
            c_blocks = seg_blocks - x_blocks
            in_specs.append(pl.BlockSpec(
                (tm, arr.shape[1]), lambda i: ((i // seg_blocks) * c_blocks + jnp.maximum(i % seg_blocks - x_blocks, 0), 0)))
        else:
            _, width, cb = kind
            in_specs.append(pl.BlockSpec((tm, width), lambda i, cb=cb: (i, cb)))
    out_specs, out_shapes = [], []
    for o in outs:
        if o[0] == "row":
            out_specs.append(pl.BlockSpec((tm, o[1]), lambda i: (i, 0)))
            out_shapes.append(jax.ShapeDtypeStruct((geo.r, o[1]), o[2]))
        elif o[0] == "xrow":
            out_specs.append(pl.BlockSpec(
                (tm, o[1]), lambda i: ((i // seg_blocks) * x_blocks + jnp.minimum(i % seg_blocks, x_blocks - 1), 0)))
            out_shapes.append(jax.ShapeDtypeStruct((geo.b * geo.s, o[1]), o[2]))
        elif o[0] == "exacc":
            out_specs.append(pl.BlockSpec((None, 1, o[1]), lambda i: (jnp.where(is_ctx(i), nb, 0) + i // seg_blocks, 0, 0)))
            out_shapes.append(jax.ShapeDtypeStruct((2 * nb, 1, o[1]), F32))
        else:
            out_specs.append(pl.BlockSpec((o[1], o[2]), lambda i: (0, 0)))
            out_shapes.append(jax.ShapeDtypeStruct((o[1], o[2]), F32))
    n_in = len(ins)
    carrier = _Carrier(job, n_in, len(outs), 0)

    def kern(*refs):
        i = pl.program_id(0)
        refs, job_refs = carrier.split(refs)
        carrier.run(job_refs, i, geo.r // tm)
        res = body(i, *[r[...].astype(F32) for r in refs[:n_in]])
        if not isinstance(res, (tuple, list)):
            res = (res,)
        jj = i % seg_blocks
        first_of_part = (jj == 0) | (jj == x_blocks)
        for o, ref, val in zip(outs, refs[n_in:], res):
            if o[0] == "row":
                ref[...] = val.astype(ref.dtype)
            elif o[0] == "xrow":
                @pl.when(jj < x_blocks)
                def _(ref=ref, val=val):
                    ref[...] = val.astype(ref.dtype)
            else:
                first = first_of_part if o[0] == "exacc" else i == 0

                @pl.when(first)
                def _(ref=ref, val=val):
                    ref[...] = val

                @pl.when(jnp.logical_not(first))
                def _(ref=ref, val=val):
                    ref[...] += val

    res = pl.pallas_call(
        kern, name=name, grid=(geo.r // tm,), in_specs=in_specs + carrier.in_specs(), out_specs=out_specs + carrier.out_specs(),
        out_shape=out_shapes + carrier.out_shapes(), scratch_shapes=carrier.scratch(), input_output_aliases=carrier.aliases(),
        compiler_params=_cparams("arbitrary"),
    )(*args, *carrier.operands())
    own, extra = carrier.results(res)
    if job:
        return (*own, extra)
    return own[0] if len(own) == 1 else own


def _colsum(v):
    return jnp.sum(v, axis=0, keepdims=True)


def _first_norm(geo, x_rows, ctx_rows, gain, mod, name):
    d = D_MODEL
    seg_blocks, x_blocks = geo.seg // 256, geo.s // 256

    def body(i, xv, cv, g, m):
        zv = jnp.where(i % seg_blocks >= x_blocks, cv, xv)
        r = lax.rsqrt(jnp.mean(zv * zv, axis=-1, keepdims=True) + EPS)
        return zv, (zv * r) * g * (1.0 + m[:, d:2 * d]) + m[:, :d]

    return _rowwise(name, body, geo, 256, [(x_rows, "xrow"), (ctx_rows, "crow"), (gain, "full"), (mod, "ex")],
                    [("row", d, F32), ("row", d, BF16)])


def _norm_mod_bwd(geo, z, gain, mod, off, dh, dz_skip, name, gated=None, latent_only=False, job=None):
    d = D_MODEL

    def body(i, zv, g, m, dhv, skip, *rest):
        r = lax.rsqrt(jnp.mean(zv * zv, axis=-1, keepdims=True) + EPS)
        n = zv * r
        dng = dhv * (1.0 + m[:, off + d:off + 2 * d])
        dn = dng * g
        dz = r * (dn - n * jnp.mean(dn * n, axis=-1, keepdims=True)) + skip
        res = (dz, _colsum(dhv), _colsum(dhv * (n * g)), _colsum(dng * n))
        if gated:
            ov, gm = rest
            res += (dz * gm[:, gated[2]:gated[2] + d], _colsum(dz * ov))
        return res

    ins = [(z, "row"), (gain, "full"), (mod, "ex"), (dh, "row"), (dz_skip, "row")]
    outs = [("xrow" if latent_only else "row", d, F32), ("exacc", d), ("exacc", d), ("gacc", 1, d)]
    if gated:
        ins += [(gated[0], "row"), (gated[1], "ex")]
        outs += [("row", d, BF16), ("exacc", d)]
    return _rowwise(name, body, geo, 256, ins, outs, job)


def _loss_head(geo, z, target, out, mod, off, name):
    seg_blocks, x_blocks = geo.seg // 256, geo.s // 256
    d = D_MODEL

    def body(i, zv, tv, ov, m):
        keep = jnp.where(i % seg_blocks >= x_blocks, 0.0, 1.0)
        err = (zv - tv) * keep
        part = 0.5 * jnp.sum(jnp.mean(err * err, axis=-1, keepdims=True), axis=0, keepdims=True)
        dz = err * (1.0 / d)
        return dz, jnp.broadcast_to(part, (1, LANES)), dz * m[:, off:off + d], _colsum(dz * ov)

    return _rowwise(name, body, geo, 256, [(z, "row"), (target, "xrow"), (out, "row"), (mod, "ex")],
                    [("row", d, F32), ("gacc", 1, LANES), ("row", d, BF16), ("exacc", d)])


Q_SCALE = HEAD_DIM ** -0.5
N_QK_CHUNKS = (N_HEADS + N_KV_HEADS) * HEAD_DIM // LANES
N_Q_CHUNKS = N_HEADS * HEAD_DIM // LANES


def _prep_tile(geo):
    return FFN_ROW_TILE if geo.seg % FFN_ROW_TILE == 0 else 256


def _attn_prep(geo, proj, cos, sin_signed, q_gain, k_gain, name):
    def body(i, p, cs, sn, qg, kg):
        outs = []
        for ch in range(N_QK_CHUNKS):
            is_q = ch < N_Q_CHUNKS
            outs.append(_qk_chunk(p[:, ch * LANES:(ch + 1) * LANES], qg if is_q else kg, cs, sn, Q_SCALE if is_q else 1.0))
        outs.append(p[:, N_QK_CHUNKS * LANES:])
        return jnp.concatenate(outs, axis=1)

    return _rowwise(name, body, geo, _prep_tile(geo), [(proj, "row"), (cos, "tab"), (sin_signed, "tab"), (q_gain, "full"), (k_gain, "full")],
                    [("row", proj.shape[1], BF16)])


def _attn_prep_bwd(geo, proj, cos, sin_signed, q_gain, k_gain, dq, dkv, name):
    kw = N_KV_HEADS * HEAD_DIM

    def body(i, p, cs, sn, qg, kg, dqv, dkvv):
        outs = []
        dgains = [jnp.zeros((1, LANES), F32), jnp.zeros((1, LANES), F32)]
        for ch in range(N_QK_CHUNKS):
            is_q = ch < N_Q_CHUNKS
            scale = Q_SCALE if is_q else 1.0
            ct = dqv[:, ch * LANES:(ch + 1) * LANES] if is_q else dkvv[:, (ch - N_Q_CHUNKS) * LANES:(ch - N_Q_CHUNKS + 1) * LANES]
            _, vjp = jax.vjp(lambda xx, gg, scale=scale: _qk_chunk(xx, gg, cs, sn, scale),
                             p[:, ch * LANES:(ch + 1) * LANES], qg if is_q else kg)
            dx, dg = vjp(ct)
            outs.append(dx)
            dgains[0 if is_q else 1] = dgains[0 if is_q else 1] + dg
        outs.append(dkvv[:, kw:])
        return jnp.concatenate(outs, axis=1), dgains[0], dgains[1]

    return _rowwise(name, body, geo, 256,
                    [(proj, "row"), (cos, "tab"), (sin_signed, "tab"), (q_gain, "full"), (k_gain, "full"), (dq, "row"), (dkv, "row")],
                    [("row", proj.shape[1], BF16), ("gacc", 1, LANES), ("gacc", 1, LANES)])


def _attn_geometry(geo):
    assert geo.s % ATTN_BLOCK == 0 and geo.l % ATTN_BLOCK == 0 and geo.seg >= BAND
    return geo.seg // ATTN_BLOCK, geo.s // ATTN_BLOCK


def _attn_mask(j, s0, geo):
    r = lax.broadcasted_iota(jnp.int32, (ATTN_BLOCK, geo.l + BAND), 0)
    n = lax.broadcasted_iota(jnp.int32, (ATTN_BLOCK, geo.l + BAND), 1) - geo.l
    dist = (s0 - j * ATTN_BLOCK) + n - r
    return (n < 0) | ((jnp.abs(dist) <= WINDOW) & (s0 + n < geo.s))


def _attn_probs(q, keys, valid, n_ctx, sink):
    s = _dot(q, keys, "nt")
    if valid is not None:
        s = jnp.where(valid, s, NEG_INF)
    m = jnp.maximum(jnp.max(s, axis=-1, keepdims=True), sink)
    e, e_sink = jnp.exp(s - m), jnp.exp(sink - m)
    inv = 1.0 / (jnp.sum(e, axis=-1, keepdims=True) + e_sink)
    return e * inv, e_sink * inv


def _attn_keys(ref, s0, geo, with_band):
    ctx = ref[geo.s:geo.seg, :]
    return jnp.concatenate([ctx, ref[pl.ds(s0, BAND), :]], axis=0) if with_band else ctx


def _attention(geo, qkv, sink, name, job=None):
    n_blocks, n_x_blocks = _attn_geometry(geo)
    qw, kw = N_HEADS * HEAD_DIM, N_KV_HEADS * HEAD_DIM
    group = N_HEADS // N_KV_HEADS
    carrier = _Carrier(job, 4, 1, 0)

    def kern(*refs):
        (sink_ref, q_ref, k_ref, v_ref, o_ref), job_refs = carrier.split(refs)
        j = pl.program_id(1)
        carrier.run(job_refs, pl.program_id(0) * n_blocks + j, geo.b * n_blocks)
        s0 = pl.multiple_of(jnp.clip((j - 1) * ATTN_BLOCK, 0, geo.seg - BAND), ATTN_BLOCK)

        def heads(with_band):
            valid = _attn_mask(j, s0, geo) if with_band else None
            k_all, v_all = _attn_keys(k_ref, s0, geo, with_band), _attn_keys(v_ref, s0, geo, with_band)
            for h in range(N_HEADS):
                kv = slice((h // group) * HEAD_DIM, (h // group + 1) * HEAD_DIM)
                p, _ = _attn_probs(q_ref[:, h * HEAD_DIM:(h + 1) * HEAD_DIM], k_all[:, kv], valid, geo.l, sink_ref[h])
                o_ref[:, h * HEAD_DIM:(h + 1) * HEAD_DIM] = _dot(p, v_all[:, kv], "nn").astype(BF16)

        pl.when(j < n_x_blocks)(lambda: heads(True))
        pl.when(j >= n_x_blocks)(lambda: heads(False))

    res = pl.pallas_call(
        kern, name=name, grid=(geo.b, n_blocks),
        in_specs=[pl.BlockSpec(memory_space=pltpu.SMEM),
                  pl.BlockSpec((ATTN_BLOCK, qw), lambda b, j: (b * n_blocks + j, 0)),
                  pl.BlockSpec((geo.seg, kw), lambda b, j: (b, qw // kw)),
                  pl.BlockSpec((geo.seg, kw), lambda b, j: (b, qw // kw + 1))] + carrier.in_specs(),
        out_specs=[pl.BlockSpec((ATTN_BLOCK, qw), lambda b, j: (b * n_blocks + j, 0))] + carrier.out_specs(),
        out_shape=[jax.ShapeDtypeStruct((geo.r, qw), BF16)] + carrier.out_shapes(),
        scratch_shapes=carrier.scratch(), input_output_aliases=carrier.aliases(),
        compiler_params=_cparams("arbitrary", "arbitrary"),
    )(sink, qkv, qkv, qkv, *carrier.operands())
    (o,), extra = carrier.results(res)
    return o, extra


def _attention_bwd(geo, qkv, sink, do, name, job=None):
    n_blocks, n_x_blocks = _attn_geometry(geo)
    qw, kw = N_HEADS * HEAD_DIM, N_KV_HEADS * HEAD_DIM
    group = N_HEADS // N_KV_HEADS

    carrier = _Carrier(job, 5, 3, 1)

    def kern(*refs):
        (sink_ref, q_ref, k_ref, v_ref, do_ref, dq_ref, dkv_out_ref, dsink_ref, dkv_ref), job_refs = carrier.split(refs)
        b, j = pl.program_id(0), pl.program_id(1)
        carrier.run(job_refs, b * n_blocks + j, geo.b * n_blocks)
        s0 = pl.multiple_of(jnp.clip((j - 1) * ATTN_BLOCK, 0, geo.seg - BAND), ATTN_BLOCK)

        @pl.when(j == 0)
        def _():
            dkv_ref[...] = jnp.zeros_like(dkv_ref)

        @pl.when((j == 0) & (b == 0))
        def _():
            dsink_ref[...] = jnp.zeros_like(dsink_ref)

        def heads(with_band):
            valid = _attn_mask(j, s0, geo) if with_band else None
            k_all, v_all = _attn_keys(k_ref, s0, geo, with_band), _attn_keys(v_ref, s0, geo, with_band)
            for g in range(N_KV_HEADS):
                kv = slice(g * HEAD_DIM, (g + 1) * HEAD_DIM)
                keys, vals = k_all[:, kv], v_all[:, kv]
                group_heads = [slice(h * HEAD_DIM, (h + 1) * HEAD_DIM) for h in range(g * group, (g + 1) * group)]
                ds_rows, p_rows = [], []
                for h, hs in zip(range(g * group, (g + 1) * group), group_heads):
                    dout = do_ref[:, hs]
                    p, p_sink = _attn_probs(q_ref[:, hs], keys, valid, geo.l, sink_ref[h])
                    dp = _dot(dout, vals, "nt")
                    dsum = jnp.sum(p * dp, axis=-1, keepdims=True)
                    ds = (p * (dp - dsum)).astype(BF16)
                    dq_ref[:, hs] = _dot(ds, keys, "nn").astype(BF16)
                    ds_rows.append(ds)
                    p_rows.append(p.astype(BF16))
                    dsink_ref[h:h + 1, :] += jnp.broadcast_to(-jnp.sum(p_sink * dsum, axis=0, keepdims=True), (1, LANES))
                q_rows = jnp.concatenate([q_ref[:, hs] for hs in group_heads], axis=0)
                do_rows = jnp.concatenate([do_ref[:, hs] for hs in group_heads], axis=0)
                dk = _dot(jnp.concatenate(ds_rows, axis=0), q_rows, "tn")
                dv = _dot(jnp.concatenate(p_rows, axis=0), do_rows, "tn")
                vv = slice(kw + g * HEAD_DIM, kw + (g + 1) * HEAD_DIM)
                dkv_ref[geo.s:geo.seg, kv] += dk[:geo.l]
                dkv_ref[geo.s:geo.seg, vv] += dv[:geo.l]
                if with_band:
                    dkv_ref[pl.ds(s0, BAND), kv] += dk[geo.l:]
                    dkv_ref[pl.ds(s0, BAND), vv] += dv[geo.l:]

        pl.when(j < n_x_blocks)(lambda: heads(True))
        pl.when(j >= n_x_blocks)(lambda: heads(False))

        @pl.when(j == n_blocks - 1)
        def _():
            dkv_out_ref[...] = dkv_ref[...].astype(BF16)

    res = pl.pallas_call(
        kern, name=name, grid=(geo.b, n_blocks),
        in_specs=[pl.BlockSpec(memory_space=pltpu.SMEM),
                  pl.BlockSpec((ATTN_BLOCK, qw), lambda b, j: (b * n_blocks + j, 0)),
                  pl.BlockSpec((geo.seg, kw), lambda b, j: (b, qw // kw)),
                  pl.BlockSpec((geo.seg, kw), lambda b, j: (b, qw // kw + 1)),
                  pl.BlockSpec((ATTN_BLOCK, qw), lambda b, j: (b * n_blocks + j, 0))] + carrier.in_specs(),
        out_specs=[pl.BlockSpec((ATTN_BLOCK, qw), lambda b, j: (b * n_blocks + j, 0)),
                   pl.BlockSpec((geo.seg, 2 * kw), lambda b, j: (b, 0)),
                   pl.BlockSpec((N_HEADS, LANES), lambda b, j: (0, 0))] + carrier.out_specs(),
        out_shape=[jax.ShapeDtypeStruct((geo.r, qw), BF16), jax.ShapeDtypeStruct((geo.r, 2 * kw), BF16),
                   jax.ShapeDtypeStruct((N_HEADS, LANES), F32)] + carrier.out_shapes(),
        scratch_shapes=[pltpu.VMEM((geo.seg, 2 * kw), F32)] + carrier.scratch(), input_output_aliases=carrier.aliases(),
        compiler_params=_cparams("arbitrary", "arbitrary"),
    )(sink, qkv, qkv, qkv, do, *carrier.operands())
    (dq, dkv, dsink), extra = carrier.results(res)
    return dq, dkv, dsink, extra


RET_QK_W = RET_HEADS * RET_QK_DIM
K_SCALE = RET_QK_DIM ** -0.5


RET_ROW_TILE = 384


def _ret_tile(geo):
    return RET_ROW_TILE if geo.seg % RET_ROW_TILE == 0 else 256


def _ret_prep(geo, proj, cos, sin_signed, name):
    def body(i, p, cs, sn):
        cs2, sn2 = jnp.concatenate([cs] * RET_HEADS, axis=1), jnp.concatenate([sn] * RET_HEADS, axis=1)
        q = _rope(p[:, :RET_QK_W], cs2, sn2, RET_QK_DIM // 4)
        k = _rope(p[:, RET_QK_W:2 * RET_QK_W], cs2, sn2, RET_QK_DIM // 4) * K_SCALE
        return jnp.concatenate([q, k, p[:, 2 * RET_QK_W:]], axis=1)

    return _rowwise(name, body, geo, _ret_tile(geo), [(proj, ("rowc", 2 * RET_QK_W + RET_VWIDTH, 0)), (cos, "tab"), (sin_signed, "tab")],
                    [("row", 2 * RET_QK_W + RET_VWIDTH, BF16)])


def _ret_prep_bwd(geo, dq, dk, dv, dgate, cos, sin_signed, name):
    def body(i, dqv, dkv, dvv, dg, cs, sn):
        cs2, sn2 = jnp.concatenate([cs] * RET_HEADS, axis=1), jnp.concatenate([sn] * RET_HEADS, axis=1)
        dkv = dkv * K_SCALE
        dqv = dqv * cs2 + _swap_halves(dqv * sn2, RET_QK_DIM // 4)
        dkv = dkv * cs2 + _swap_halves(dkv * sn2, RET_QK_DIM // 4)
        return jnp.concatenate([dqv, dkv, dvv, dg], axis=1)

    return _rowwise(name, body, geo, _ret_tile(geo),
                    [(dq, "row"), (dk, "row"), (dv, "row"), (dgate, "row"), (cos, "tab"), (sin_signed, "tab")],
                    [("row", 2 * RET_QK_W + 2 * RET_VWIDTH, BF16)])


def _ret_step(state, q, k, v, lg, rev):
    c = RET_CHUNK
    ri = lax.broadcasted_iota(jnp.int32, (c, 1), 0).astype(F32)
    cj = lax.broadcasted_iota(jnp.int32, (1, c), 1).astype(F32)
    if rev:
        dist, q_decay, k_decay = cj - ri, jnp.exp(lg * (c - ri)), jnp.exp(lg * ri)
    else:
        dist, q_decay, k_decay = ri - cj, jnp.exp(lg * (ri + 1.0)), jnp.exp(lg * (c - 1.0 - ri))
    intra = jnp.where(dist >= 0, jnp.exp(lg * jnp.maximum(dist, 0.0)), 0.0)
    scores = _mm(q, k, "nt") * intra
    out = _mm(scores, v, "nn") + _mm(q, state, "nn") * q_decay
    new_state = state * jnp.exp(lg * c) + _mm(k * k_decay, v, "tn")
    return new_state, out


def _ret_state0(kc, vc, lg, rev):
    n = kc.shape[0]
    t = lax.broadcasted_iota(jnp.int32, (n, 1), 0).astype(F32)
    decay = jnp.exp(lg * t) if rev else jnp.exp(lg * (n - 1.0 - t))
    return _mm(kc * decay, vc, "tn")


def _ret_specs(geo):
    nq = RET_HEADS
    return [pl.BlockSpec((2 * RET_HEADS, LANES), lambda b, h: (0, 0)),
            pl.BlockSpec((geo.seg, RET_QK_DIM), lambda b, h: (b, h)),
            pl.BlockSpec((geo.seg, RET_QK_DIM), lambda b, h: (b, nq + h)),
            pl.BlockSpec((geo.seg, RET_V_DIM), lambda b, h: (b, nq + h))]


def _retention(geo, qkv, log_g, name):
    nc = geo.s // RET_CHUNK

    def kern(lg_ref, q_ref, k_ref, v_ref, o_ref, st_ref):
        h = pl.program_id(1)
        for d, rev in ((0, False), (1, True)):
            lg = lg_ref[pl.ds(d * RET_HEADS + h, 1), 0:1]
            st_ref[...] = _ret_state0(k_ref[geo.s:geo.seg, :].astype(F32), v_ref[geo.s:geo.seg, :].astype(F32), lg, rev)

            def chunk(ci, carry, d=d, rev=rev, lg=lg):
                r0 = pl.multiple_of((nc - 1 - ci if rev else ci) * RET_CHUNK, RET_CHUNK)
                rows = pl.ds(r0, RET_CHUNK)
                new_state, out = _ret_step(st_ref[...], q_ref[rows, :], k_ref[rows, :], v_ref[rows, :], lg, rev)
                st_ref[...] = new_state
                if d == 0:
                    o_ref[rows, :] = out
                else:
                    o_ref[rows, :] += out
                return carry

            lax.fori_loop(0, nc, chunk, 0)
        o_ref[geo.s:geo.seg, :] = jnp.zeros((geo.l, RET_V_DIM), F32)

    return pl.pallas_call(
        kern, name=name, grid=(geo.b, RET_HEADS), in_specs=_ret_specs(geo),
        out_specs=pl.BlockSpec((geo.seg, RET_V_DIM), lambda b, h: (b, h)),
        out_shape=jax.ShapeDtypeStruct((geo.r, RET_VWIDTH), F32),
        scratch_shapes=[pltpu.VMEM((RET_QK_DIM, RET_V_DIM), F32)],
        compiler_params=_cparams("parallel", "arbitrary"),
    )(log_g, qkv, qkv, qkv)


def _retention_bwd(geo, qkv, log_g, do, name):
    nc = geo.s // RET_CHUNK
    ctx = slice(geo.s, geo.seg)

    def kern(lg_ref, q_ref, k_ref, v_ref, do_ref, dq_ref, dk_ref, dv_ref, dlg_ref, states_ref, dst_ref, aq_ref, ak_ref, av_ref):
        b, h = pl.program_id(0), pl.program_id(1)

        @pl.when((b == 0) & (h == 0))
        def _():
            dlg_ref[...] = jnp.zeros_like(dlg_ref)

        for d, rev in ((0, False), (1, True)):
            row = pl.ds(d * RET_HEADS + h, 1)
            lg = lg_ref[row, 0:1]
            kc, vc = k_ref[ctx, :].astype(F32), v_ref[ctx, :].astype(F32)
            states_ref[0] = _ret_state0(kc, vc, lg, rev)

            def rows_of(ci, rev=rev):
                return pl.ds(pl.multiple_of((nc - 1 - ci if rev else ci) * RET_CHUNK, RET_CHUNK), RET_CHUNK)

            def load(rows):
                return q_ref[rows, :].astype(F32), k_ref[rows, :].astype(F32), v_ref[rows, :].astype(F32)

            def replay(ci, carry, rev=rev, lg=lg, rows_of=rows_of, load=load):
                states_ref[ci + 1] = _ret_step(states_ref[ci], *load(rows_of(ci)), lg, rev)[0]
                return carry

            lax.fori_loop(0, nc - 1, replay, 0)
            dst_ref[...] = jnp.zeros_like(dst_ref)

            def emit(rows, dq, dk, dv, d=d):
                if d == 0:
                    ak_ref[rows, :], av_ref[rows, :] = dk, dv
                    if dq is not None:
                        aq_ref[rows, :] = dq
                else:
                    dk_ref[rows, :] = (ak_ref[rows, :] + dk).astype(BF16)
                    dv_ref[rows, :] = (av_ref[rows, :] + dv).astype(BF16)
                    if dq is not None:
                        dq_ref[rows, :] = (aq_ref[rows, :] + dq).astype(BF16)

            def back(t, dlg, rev=rev, lg=lg, rows_of=rows_of, load=load, emit=emit):
                ci = nc - 1 - t
                rows = rows_of(ci)
                _, vjp = jax.vjp(lambda st, q, k, v, g: _ret_step(st, q, k, v, g, rev), states_ref[ci], *load(rows), lg)
                dstate, dq, dk, dv, dg = vjp((dst_ref[...], do_ref[rows, :].astype(F32)))
                dst_ref[...] = dstate
                emit(rows, dq, dk, dv)
                return dlg + dg

            dlg = lax.fori_loop(0, nc, back, jnp.zeros((1, 1), F32))
            _, vjp = jax.vjp(lambda kk, vv, g: _ret_state0(kk, vv, g, rev), kc, vc, lg)
            dkc, dvc, dg = vjp(dst_ref[...])
            emit(ctx, None, dkc, dvc)
            dlg_ref[row, :] += jnp.broadcast_to(dlg + dg, (1, LANES))
        dq_ref[ctx, :] = jnp.zeros((geo.l, RET_QK_DIM), BF16)

    nq = RET_HEADS
    return pl.pallas_call(
        kern, name=name, grid=(geo.b, RET_HEADS),
        in_specs=_ret_specs(geo) + [pl.BlockSpec((geo.seg, RET_V_DIM), lambda b, h: (b, h))],
        out_specs=[pl.BlockSpec((geo.seg, RET_QK_DIM), lambda b, h: (b, h)),
                   pl.BlockSpec((geo.seg, RET_QK_DIM), lambda b, h: (b, h)),
                   pl.BlockSpec((geo.seg, RET_V_DIM), lambda b, h: (b, h)),
                   pl.BlockSpec((2 * RET_HEADS, LANES), lambda b, h: (0, 0))],
        out_shape=[jax.ShapeDtypeStruct((geo.r, RET_QK_W), BF16), jax.ShapeDtypeStruct((geo.r, RET_QK_W), BF16),
                   jax.ShapeDtypeStruct((geo.r, RET_VWIDTH), BF16), jax.ShapeDtypeStruct((2 * RET_HEADS, LANES), F32)],
        scratch_shapes=[pltpu.VMEM((nc, RET_QK_DIM, RET_V_DIM), F32), pltpu.VMEM((RET_QK_DIM, RET_V_DIM), F32),
                        pltpu.VMEM((geo.seg, RET_QK_DIM), F32), pltpu.VMEM((geo.seg, RET_QK_DIM), F32),
                        pltpu.VMEM((geo.seg, RET_V_DIM), F32)],
        compiler_params=_cparams("arbitrary", "arbitrary"),
    )(log_g, qkv, qkv, qkv, do)


def _gated(o, g, gain):
    outs = []
    for h in range(RET_HEADS):
        cols = slice(h * RET_V_DIM, (h + 1) * RET_V_DIM)
        oh = o[:, cols]
        mu = jnp.mean(oh, axis=-1, keepdims=True)
        var = jnp.mean(jnp.square(oh - mu), axis=-1, keepdims=True)
        outs.append(_silu(g[:, cols]) * ((oh - mu) * lax.rsqrt(var + EPS) * gain[:, cols]))
    return jnp.concatenate(outs, axis=1)


def _ret_gated(geo, o, proj, gain, name):
    def body(i, ov, gv, gn):
        return _gated(ov, gv, gn)

    gate_block = (2 * RET_QK_W + RET_VWIDTH) // RET_VWIDTH
    return _rowwise(name, body, geo, _ret_tile(geo), [(o, "row"), (proj, ("rowc", RET_VWIDTH, gate_block)), (gain, "full")],
                    [("row", RET_VWIDTH, BF16)])


def _ret_gated_bwd(geo, o, proj, gain, dout, name):
    def body(i, ov, gv, gn, dv):
        _, vjp = jax.vjp(_gated, ov, gv, gn)
        return vjp(dv)

    gate_block = (2 * RET_QK_W + RET_VWIDTH) // RET_VWIDTH
    return _rowwise(name, body, geo, 256,
                    [(o, "row"), (proj, ("rowc", RET_VWIDTH, gate_block)), (gain, "full"), (dout, "row")],
                    [("row", RET_VWIDTH, BF16), ("row", RET_VWIDTH, BF16), ("gacc", 1, RET_VWIDTH)])


def _whole(name, fn, out_shapes, *arrays):
    n = len(arrays)

    def kern(*refs):
        res = fn(*[r[...] for r in refs[:n]])
        for ref, val in zip(refs[n:], res):
            ref[...] = val.astype(ref.dtype)

    return pl.pallas_call(kern, name=name, out_shape=out_shapes)(*arrays)


def _rope_tables(geo, head_dim):
    rows = geo.s // GRID_W
    row = jnp.broadcast_to(jnp.arange(rows, dtype=jnp.int32)[:, None], (rows, GRID_W)).reshape(geo.s)
    col = jnp.broadcast_to(jnp.arange(GRID_W, dtype=jnp.int32)[None, :], (rows, GRID_W)).reshape(geo.s)
    axis_dim = head_dim // 2
    inv = ROPE_BASE ** (-jnp.arange(0, axis_dim, 2, dtype=F32) / axis_dim)
    ang_r = row.astype(F32)[:, None] * inv
    ang_c = col.astype(F32)[:, None] * inv
    cos = jnp.concatenate([jnp.cos(ang_r)] * 2 + [jnp.cos(ang_c)] * 2, axis=1)
    sin = jnp.concatenate([-jnp.sin(ang_r), jnp.sin(ang_r), -jnp.sin(ang_c), jnp.sin(ang_c)], axis=1)
    cos = jnp.concatenate([cos, jnp.ones((geo.l, head_dim), F32)], axis=0)
    sin = jnp.concatenate([sin, jnp.zeros((geo.l, head_dim), F32)], axis=0)
    reps = max(1, LANES // head_dim)
    return jnp.tile(cos, (1, reps)), jnp.tile(sin, (1, reps))


def _row_tile(r):
    return next(t for t in (1536, 1024, 512, 256, 128) if r % t == 0)


MOD_ROWS = 8


def _local_step(x, ctx, target, sp, wts, mods, plan=None):
    nb, s, d = x.shape
    geo = _Rows(nb, s, ctx.shape[1])
    assert nb + 1 <= MOD_ROWS and d == D_MODEL
    tm = _row_tile(geo.r)
    cos64, sin64 = _rope_tables(geo, HEAD_DIM)
    cos256, sin256 = _rope_tables(geo, RET_QK_DIM)
    q_gain = jnp.tile(sp["q_norm"].reshape(1, HEAD_DIM), (1, LANES // HEAD_DIM))
    k_gain = jnp.tile(sp["k_norm"].reshape(1, HEAD_DIM), (1, LANES // HEAD_DIM))
    sink = sp["sink"].reshape(N_HEADS)
    log_g = jnp.broadcast_to(sp["log_g"].reshape(2 * RET_HEADS, 1), (2 * RET_HEADS, LANES))
    gn_g = sp["gn_g"].reshape(1, RET_VWIDTH)

    saved = []
    z, h1 = _first_norm(geo, x.reshape(nb * s, d), ctx.reshape(nb * geo.l, d), sp["norm1_g"][0][None, :], mods[0], "norm1_0")
    for i in range(2):
        mod3 = mods[i]
        n1, n2 = sp["norm1_g"][i][None, :], sp["norm2_g"][i][None, :]
        if i == 0:
            proj = _mm_nn(h1, wts["attn_qkv"], F32, "attn_qkv", tm, wts["attn_qkv"].shape[1], d)
            prep = _attn_prep(geo, proj, cos64, sin64, q_gain, k_gain, "attn_prep")
            o, late = _attention(geo, prep, sink, "attn", plan.gather_job() if plan else None)
            if plan:
                plan.late_weights(late, wts)
            oraw = None
            w_o = wts["attn_o"]
        else:
            proj = _mm_nn(h1, wts["ret_qkvg"], BF16, "ret_qkvg", tm, wts["ret_qkvg"].shape[2], d)
            prep = _ret_prep(geo, proj, cos256, sin256, "ret_prep")
            oraw = _retention(geo, prep, log_g, "ret")
            o = _ret_gated(geo, oraw, proj, gn_g, "ret_gated")
            w_o = wts["ret_o"]
        zmid, mix, h2 = _mm_nn_gate_residual(geo, o, w_o, z, mod3, 2 * d, f"mix_out{i}", norm=(n2, mod3, 3 * d))
        u, a = _ffn_in_swiglu(h2, wts["ffn_in"][i], f"ffn_in{i}")
        next_norm = (sp["norm1_g"][1][None, :], mods[1], 0) if i == 0 else None
        zout, f, h1_next = _mm_nn_gate_residual(geo, a, wts["ffn_out"][i], zmid, mod3, 5 * d, f"ffn_out{i}", norm=next_norm)
        saved.append(dict(z=z, mod3=mod3, n1=n1, n2=n2, h1=h1, proj=proj, prep=prep, o=o, oraw=oraw, mix=mix, zmid=zmid,
                          h2=h2, u=u, a=a, f=f))
        z, h1 = zout, h1_next

    dz, loss, df, dg2 = _loss_head(geo, z, target.reshape(nb * s, d), saved[1]["f"], saved[1]["mod3"], 5 * d, "loss")

    big, small = {}, {}
    dmods = [None, None]
    for i in (1, 0):
        sv = saved[i]
        mod3 = sv["mod3"]
        carry = plan is not None and i == 0
        du, land = _ffn_out_dx_swiglu_bwd(df, wts["ffn_out"][i], sv["u"], f"ffn_out_dx{i}", plan.layer1.swap_job() if carry else None)
        if carry:
            plan.layer1.after_swap(land)
        big[f"ffn_out{i}"] = _mm_tn(sv["a"], df, f"ffn_out_dw{i}", D_FF // 2, 1024, tm, out_dtype=BF16).reshape(N_CHIPS, D_FF // N_CHIPS, d)
        n4 = wts["ffn_in"][i].shape[2]
        dh2 = _mm_nt(du, wts["ffn_in"][i], BF16, f"ffn_in_dx{i}", tm, 1024, n4)
        big[f"ffn_in{i}"] = _mm_tn(sv["h2"], du, f"ffn_in_dw{i}", 1024, n4, tm, shards=N_CHIPS, out_dtype=BF16)
        if carry:
            plan.start_layer0_ffn(big)
        dzmid, dsh2, dsc2, dn2, dmix, dg1, *land = _norm_mod_bwd(geo, sv["zmid"], sv["n2"], mod3, 3 * d, dh2, dz, f"norm2_bwd{i}",
                                                                 gated=(sv["mix"], mod3, 2 * d),
                                                                 job=plan.layer0_ffn.swap_job() if carry else None)
        if carry:
            plan.layer0_ffn.after_swap(land[0])
        if i == 0:
            do = _mm_nt(dmix, wts["attn_o"], BF16, "attn_out_dx", tm, 1024, 1024)
            big["attn_o"] = _mm_tn(sv["o"], dmix, "attn_out_dw", 1024, 1024, tm, out_dtype=BF16).reshape(N_CHIPS, 1024 // N_CHIPS, d)
            dq, dkv, dsink, land = _attention_bwd(geo, sv["prep"], sink, do, "attn_bwd", plan.exchange_job() if plan else None)
            if plan:
                plan.after_exchange(land)
            dproj, dqg, dkg = _attn_prep_bwd(geo, sv["proj"], cos64, sin64, q_gain, k_gain, dq, dkv, "attn_prep_bwd")
            small["q_norm"] = dqg[0, :HEAD_DIM] + dqg[0, HEAD_DIM:]
            small["k_norm"] = dkg[0, :HEAD_DIM] + dkg[0, HEAD_DIM:]
            small["sink"] = dsink[:, 0]
            wq = wts["attn_qkv"]
            dh1 = _mm_nt(dproj, wq, BF16, "attn_qkv_dx", tm, 1024, wq.shape[1])
            dwq = _mm_tn(sv["h1"], dproj, "attn_qkv_dw", 1024, wq.shape[1], tm, out_dtype=BF16)
            big["attn_qkv"] = dwq.reshape(d, N_CHIPS, -1).transpose(1, 0, 2)
        else:
            do = _mm_nt(dmix, wts["ret_o"], BF16, "ret_out_dx", tm, 1024, 1024)
            big["ret_o"] = _mm_tn(sv["o"], dmix, "ret_out_dw", 1024, 1024, tm, out_dtype=BF16).reshape(N_CHIPS, RET_VWIDTH // N_CHIPS, d)
            doraw, dgate, dgn = _ret_gated_bwd(geo, sv["oraw"], sv["proj"], gn_g, do, "ret_gated_bwd")
            small["gn_g"] = dgn[0]
            dq, dk, dv, dlg = _retention_bwd(geo, sv["prep"], log_g, doraw, "ret_bwd")
            small["log_g"] = dlg[:, 0].reshape(2, RET_HEADS)
            dproj = _ret_prep_bwd(geo, dq, dk, dv, dgate, cos256, sin256, "ret_prep_bwd")
            wq = wts["ret_qkvg"]
            dh1 = _mm_nt(dproj, wq, BF16, "ret_qkvg_dx", tm, 1024, wq.shape[2])
            big["ret_qkvg"] = _mm_tn(sv["h1"], dproj, "ret_qkvg_dw", 1024, wq.shape[2], tm, shards=N_CHIPS, out_dtype=BF16)
        below = (saved[0]["f"], saved[0]["mod3"], 5 * d) if i == 1 else None
        dz, dsh1, dsc1, dn1, *below_grads = _norm_mod_bwd(geo, sv["z"], sv["n1"], mod3, 0, dh1, dzmid, f"norm1_bwd{i}", gated=below,
                                                              latent_only=i == 0)
        small[f"norm1_g{i}"], small[f"norm2_g{i}"] = dn1[0], dn2[0]
        parts = [dsh1, dsc1, dg1, dsh2, dsc2, dg2]
        rows = jnp.concatenate([jnp.concatenate([p[:nb, 0, :] for p in parts], axis=1),
                                jnp.concatenate([jnp.sum(p[nb:, 0, :], axis=0, keepdims=True) for p in parts], axis=1),
                                jnp.zeros((MOD_ROWS - nb - 1, 6 * d), F32)], axis=0)
        dmods[i] = rows
        if below_grads:
            df, dg2 = below_grads
        small[f"ada_b{i}"] = jnp.sum(rows, axis=0)
        if plan and i == 1:
            plan.start_layer1(big)
    return loss, dz, big, small, dmods


def _adamw(w, g, m, v, name):
    rows, cols = w.shape
    tr = next((t for t in (512, 256, 128, 64, 32, 16, 8) if rows % t == 0), rows)
    c1 = 1.0 - ADAM_B1 ** ADAM_STEP
    c2 = 1.0 - ADAM_B2 ** ADAM_STEP

    def kern(w_ref, g_ref, m_ref, v_ref, d_ref, nm_ref, nv_ref):
        gv = g_ref[...]
        nm = ADAM_B1 * m_ref[...] + (1.0 - ADAM_B1) * gv
        nv = ADAM_B2 * v_ref[...] + (1.0 - ADAM_B2) * jnp.square(gv)
        d_ref[...] = -ADAM_LR * ((nm / c1) / (jnp.sqrt(nv / c2) + ADAM_EPS) + ADAM_WD * w_ref[...])
        nm_ref[...] = nm
        nv_ref[...] = nv

    spec = pl.BlockSpec((tr, cols), lambda i: (i, 0))
    return pl.pallas_call(
        kern, name=name, grid=(rows // tr,), in_specs=[spec] * 4, out_specs=[spec] * 3,
        out_shape=[jax.ShapeDtypeStruct(w.shape, F32)] * 3, compiler_params=_cparams("parallel"),
    )(w, g, m, v)


N_DEVICES = 8


def _mesh_pos():
    return lax.axis_index("x"), lax.axis_index("y"), lax.axis_index("c")


def _other_chips(x, y):
    return [(1 - x, y), (x, 1 - y), (1 - x, 1 - y)]


def _hbm(n):
    return [pl.BlockSpec(memory_space=pl.ANY)] * n


def _remote(src, dst, send_sem, recv_sem, device):
    return pltpu.make_async_remote_copy(src_ref=src, dst_ref=dst, send_sem=send_sem, recv_sem=recv_sem,
                                        device_id=device, device_id_type=MESH)


def _scalar_spec(grid, in_specs, out_specs):
    return pltpu.PrefetchScalarGridSpec(num_scalar_prefetch=1, grid=grid, in_specs=in_specs, out_specs=out_specs)


def _place_shard(param, layer, pos, name):
    _, r, cols = param.shape
    tr = _slab_tile(r)

    def kern(pos_ref, s_ref, o_ref):
        o_ref[...] = s_ref[...].astype(BF16)

    return pl.pallas_call(
        kern, name=name, out_shape=jax.ShapeDtypeStruct((N_CHIPS, r, cols), BF16),
        grid_spec=_scalar_spec((r // tr,), [pl.BlockSpec((None, tr, cols), lambda i, p: (layer, i, 0))],
                               pl.BlockSpec((None, tr, cols), lambda i, p: (p[1], i, 0))),
        compiler_params=_cparams("parallel"),
    )(pos, param)


class _CommJob:
    def __init__(self, inputs, out_shapes, aliases, sem_shapes, stages, fractions=None):
        self.inputs, self.out_shapes, self.aliases, self.sem_shapes, self.stages = inputs, out_shapes, aliases, sem_shapes, stages
        self.fractions = fractions


def _merge_jobs(a, b):
    assert len(a.stages) == len(b.stages)
    ni, no, ns = len(a.inputs), len(a.out_shapes), len(a.sem_shapes)

    def both(sa, sb):
        def stage(ins, outs, sems):
            sa(ins[:ni], outs[:no], sems[:ns])
            sb(ins[ni:], outs[no:], sems[ns:])
        return stage

    aliases = dict(a.aliases)
    aliases.update({ni + i: no + o for i, o in b.aliases.items()})
    return _CommJob(a.inputs + b.inputs, a.out_shapes + b.out_shapes, aliases, a.sem_shapes + b.sem_shapes,
                    [both(sa, sb) for sa, sb in zip(a.stages, b.stages)])


def _run_job(job, name):
    n_in, n_out = len(job.inputs), len(job.out_shapes)

    def body(*refs):
        for stage in job.stages:
            stage(refs[:n_in], refs[n_in:n_in + n_out], refs[n_in + n_out:])

    return pl.pallas_call(
        body, name=name, in_specs=_hbm(n_in), out_specs=_hbm(n_out), out_shape=job.out_shapes,
        input_output_aliases=job.aliases, scratch_shapes=job.sem_shapes,
    )(*job.inputs)


def _job_marks(job, steps):
    mid = len(job.stages) - 2
    fractions = job.fractions or [(s + 1) / (mid + 1) for s in range(mid)]
    return [0] + [min(steps - 1, 1 + int((steps - 1) * f)) for f in fractions] + [steps - 1]


def _gather_job(placed):
    n = len(placed)

    def half(w, which):
        r2 = placed[w].shape[1] // 2
        return pl.ds(which * r2, r2)

    def ici_copies(outs, sems, slot_of, arrays=range(n)):
        x, y, c = _mesh_pos()
        res = []
        for w in arrays:
            for k, (px, py) in enumerate(_other_chips(x, y)):
                slab = outs[w].at[slot_of(x, y, px, py), half(w, c)]
                res.append((slab, _remote(slab, slab, sems[0].at[w, k], sems[1].at[w, k], (px, py, c))))
        return res

    def forwards(outs, sems, which_core, arrays=range(n)):
        x, y, c = _mesh_pos()
        res = []
        for w in arrays:
            for k, (px, py) in enumerate(_other_chips(x, y)):
                slab = outs[w].at[2 * px + py, half(w, which_core(c))]
                res.append(_remote(slab, slab, sems[2].at[w, k], sems[3].at[w, k], (x, y, 1 - c)))
        return res

    def send(ins, outs, sems):
        for _, cp in ici_copies(outs, sems, lambda x, y, px, py: 2 * x + y):
            cp.start()

    def forward_of(w):
        def forward(ins, outs, sems):
            arrivals = ici_copies(outs, sems, lambda x, y, px, py: 2 * px + py, [w])
            for (_, arrival), fwd in zip(arrivals, forwards(outs, sems, lambda c: c, [w])):
                arrival.wait_recv()
                fwd.start()
        return forward

    def finish(ins, outs, sems):
        for cp in forwards(outs, sems, lambda c: 1 - c):
            cp.wait_recv()
        for _, cp in ici_copies(outs, sems, lambda x, y, px, py: 2 * x + y):
            cp.wait_send()
        for cp in forwards(outs, sems, lambda c: c):
            cp.wait_send()

    sizes = [p.shape[1] * p.shape[2] for p in placed]
    fractions = [sum(sizes[:w + 1]) / sum(sizes) for w in range(n)]
    return _CommJob(list(placed), [jax.ShapeDtypeStruct(p.shape, p.dtype) for p in placed], {w: w for w in range(n)},
                    [pltpu.SemaphoreType.DMA((n, 3))] * 4, [send] + [forward_of(w) for w in range(n)] + [finish], fractions)


def _pair_swap_job(grads):
    n = len(grads)

    def copies(ins, outs, sems):
        x, y, c = _mesh_pos()
        res = []
        for w in range(n):
            r2 = grads[w].shape[1] // 2
            res.append(_remote(ins[w].at[:, pl.ds((1 - c) * r2, r2)], outs[w], sems[0].at[w], sems[1].at[w], (x, y, 1 - c)))
        return res

    def send(ins, outs, sems):
        for cp in copies(ins, outs, sems):
            cp.start()

    def finish(ins, outs, sems):
        for cp in copies(ins, outs, sems):
            cp.wait()

    return _CommJob(list(grads), [jax.ShapeDtypeStruct((N_CHIPS, g.shape[1] // 2, g.shape[2]), g.dtype) for g in grads], {},
                    [pltpu.SemaphoreType.DMA((n,))] * 2, [send, finish])


def _chip_exchange_job(hs):
    n = len(hs)

    def send(ins, outs, sems):
        x, y, c = _mesh_pos()
        for w in range(n):
            for k, (px, py) in enumerate(_other_chips(x, y)):
                _remote(ins[w].at[2 * px + py], outs[w].at[2 * x + y], sems[0].at[w, k], sems[1].at[w, k], (px, py, c)).start()

    def finish(ins, outs, sems):
        x, y, c = _mesh_pos()
        for w in range(n):
            for k, (px, py) in enumerate(_other_chips(x, y)):
                got = outs[w].at[2 * px + py]
                cp = _remote(ins[w].at[2 * px + py], got, sems[0].at[w, k], sems[1].at[w, k], (px, py, c))
                cp.wait_recv()
                cp.wait_send()

    return _CommJob(list(hs), [jax.ShapeDtypeStruct(h.shape, h.dtype) for h in hs], {},
                    [pltpu.SemaphoreType.DMA((n, 3))] * 2, [send, finish])


def _pair_share(ts, name):
    n = len(ts)

    def body(*refs):
        outs = refs[n:2 * n]
        send_sems, recv_sems = refs[2 * n:]
        x, y, c = _mesh_pos()
        sends = []
        for w in range(n):
            r2 = ts[w].shape[0] // 2
            mine = outs[w].at[pl.ds(c * r2, r2)]
            rc = _remote(mine, mine, send_sems.at[w], recv_sems.at[w], (x, y, 1 - c))
            rc.start()
            sends.append(rc)
        for w in range(n):
            r2 = ts[w].shape[0] // 2
            theirs = outs[w].at[pl.ds((1 - c) * r2, r2)]
            _remote(theirs, theirs, send_sems.at[w], recv_sems.at[w], (x, y, 1 - c)).wait_recv()
            sends[w].wait_send()

    return pl.pallas_call(
        body, name=name, in_specs=_hbm(n), out_specs=_hbm(n),
        out_shape=[jax.ShapeDtypeStruct(t.shape, F32) for t in ts],
        input_output_aliases={w: w for w in range(n)},
        scratch_shapes=[pltpu.SemaphoreType.DMA((n,))] * 2,
    )(*ts)


def _slab_tile(rows):
    return next(t for t in (512, 256, 176, 128, 64, 32, 16) if rows % t == 0)


def _sum_pair(grad, land, pos, name):
    _, r2, cols = land.shape
    tr = _slab_tile(r2)
    nt = r2 // tr

    def kern(pos_ref, a_ref, b_ref, o_ref):
        o_ref[...] = (a_ref[...].astype(F32) + b_ref[...].astype(F32)).astype(BF16)

    spec = pl.BlockSpec((None, tr, cols), lambda j, i, p: (j, i, 0))
    return pl.pallas_call(
        kern, name=name, out_shape=jax.ShapeDtypeStruct(land.shape, BF16),
        grid_spec=_scalar_spec((N_CHIPS, nt), [pl.BlockSpec((None, tr, cols), lambda j, i, p: (j, p[0] * nt + i, 0)), spec], spec),
        compiler_params=_cparams("parallel", "parallel"),
    )(pos, grad, land)


def _sum_chips(hs, land, pos, name):
    _, r2, cols = land.shape
    tr = _slab_tile(r2)
    nt = r2 // tr

    def kern(pos_ref, h_ref, l_ref, o_ref):
        acc = jnp.zeros((tr, cols), F32)
        own = h_ref[...].astype(F32)
        for k in range(N_CHIPS):
            acc = acc + jnp.where(pos_ref[1] == k, own, l_ref[k].astype(F32))
        o_ref[...] = acc

    return pl.pallas_call(
        kern, name=name, out_shape=jax.ShapeDtypeStruct((2 * r2, cols), F32),
        grid_spec=_scalar_spec((nt,), [pl.BlockSpec((None, tr, cols), lambda i, p: (p[1], i, 0)),
                                       pl.BlockSpec((N_CHIPS, tr, cols), lambda i, p: (0, i, 0))],
                               pl.BlockSpec((tr, cols), lambda i, p: (p[0] * nt + i, 0))),
        compiler_params=_cparams("parallel"),
    )(pos, hs, land)


class _ReduceScatter:
    def __init__(self, grads, pos, tag):
        self.grads, self.pos, self.tag = list(grads), pos, tag

    def swap_job(self):
        return _pair_swap_job(self.grads)

    def after_swap(self, land):
        self.hs = [_sum_pair(g, l, self.pos, f"grads_pair_sum_{self.tag}{w}") for w, (g, l) in enumerate(zip(self.grads, land))]

    def exchange_job(self):
        return _chip_exchange_job(self.hs)

    def after_exchange(self, land2):
        return [_sum_chips(h, l, self.pos, f"grads_chip_sum_{self.tag}{w}") for w, (h, l) in enumerate(zip(self.hs, land2))]

    def run(self):
        self.after_swap(_run_job(self.swap_job(), f"grads_pair_swap_{self.tag}"))
        return self.after_exchange(_run_job(self.exchange_job(), f"grads_chip_exchange_{self.tag}"))


EARLY_WEIGHTS = ("attn_qkv",)
LATE_WEIGHTS = ("ffn_in0", "ffn_in1", "ffn_out0", "ffn_out1", "attn_o", "ret_qkvg", "ret_o")
LAYER1_GRADS = ("ffn_out1", "ffn_in1", "ret_o", "ret_qkvg")
LAYER0_FFN_GRADS = ("ffn_out0", "ffn_in0")
LAST_GRADS = ("attn_o", "attn_qkv")


def _fill_weights(wts, full):
    for name, w in full.items():
        if name[:-1] == "ffn_in":
            wts[name[:-1]][int(name[-1])] = w
        elif name[:-1] == "ffn_out":
            wts["ffn_out"][int(name[-1])] = w.reshape(-1, w.shape[2])
        elif name in ("attn_o", "ret_o"):
            wts[name] = w.reshape(-1, w.shape[2])
        elif name == "attn_qkv":
            wts[name] = w.transpose(1, 0, 2).reshape(w.shape[1], -1)
        else:
            wts[name] = w


class _StepPlan:
    def __init__(self, placed, pos):
        self.placed, self.pos = placed, pos
        self.layer1 = self.layer0_ffn = None
        self.reduced = {}

    def gather_job(self):
        return _gather_job([self.placed[k] for k in LATE_WEIGHTS])

    def late_weights(self, outs, wts):
        _fill_weights(wts, dict(zip(LATE_WEIGHTS, outs)))

    def start_layer1(self, big):
        self.layer1 = _ReduceScatter([big[k] for k in LAYER1_GRADS], self.pos, "l1_")

    def start_layer0_ffn(self, big):
        self.layer0_ffn = _ReduceScatter([big[k] for k in LAYER0_FFN_GRADS], self.pos, "l0f_")

    def exchange_job(self):
        return _merge_jobs(self.layer1.exchange_job(), self.layer0_ffn.exchange_job())

    def after_exchange(self, land):
        n1 = len(LAYER1_GRADS)
        self.reduced.update(zip(LAYER1_GRADS, self.layer1.after_exchange(land[:n1])))
        self.reduced.update(zip(LAYER0_FFN_GRADS, self.layer0_ffn.after_exchange(land[n1:])))


def _all_reduce_small(v, name):
    def body(v_ref, o_ref, land_ref, send_sems, recv_sems):
        x, y, c = _mesh_pos()
        me = 4 * x + 2 * y + c
        land_ref[me] = v_ref[...]
        for t in range(N_DEVICES):
            @pl.when(t != me)
            def _(t=t):
                _remote(v_ref, land_ref.at[me], send_sems.at[t], recv_sems.at[me], (t // 4, (t // 2) % 2, t % 2)).start()
        for t in range(N_DEVICES):
            @pl.when(t != me)
            def _(t=t):
                _remote(v_ref, land_ref.at[t], send_sems.at[t], recv_sems.at[t], (t // 4, (t // 2) % 2, t % 2)).wait()
        acc = land_ref[0]
        for t in range(1, N_DEVICES):
            acc = acc + land_ref[t]
        o_ref[...] = acc

    vmem = pl.BlockSpec(memory_space=pltpu.VMEM)
    return pl.pallas_call(
        body, name=name, in_specs=[vmem], out_specs=vmem, out_shape=jax.ShapeDtypeStruct(v.shape, F32),
        scratch_shapes=[pltpu.VMEM((N_DEVICES,) + v.shape, F32), pltpu.SemaphoreType.DMA((N_DEVICES,)),
                        pltpu.SemaphoreType.DMA((N_DEVICES,))],
    )(v)


def _all_to_all_small(v, name):
    def body(v_ref, o_ref, send_sems, recv_sems):
        x, y, c = _mesh_pos()
        me = 4 * x + 2 * y + c
        o_ref[me] = v_ref[me]
        for t in range(N_DEVICES):
            @pl.when(t != me)
            def _(t=t):
                _remote(v_ref.at[t], o_ref.at[me], send_sems.at[t], recv_sems.at[me], (t // 4, (t // 2) % 2, t % 2)).start()
        for t in range(N_DEVICES):
            @pl.when(t != me)
            def _(t=t):
                _remote(v_ref.at[t], o_ref.at[t], send_sems.at[t], recv_sems.at[t], (t // 4, (t // 2) % 2, t % 2)).wait()

    vmem = pl.BlockSpec(memory_space=pltpu.VMEM)
    return pl.pallas_call(
        body, name=name, in_specs=[vmem], out_specs=vmem, out_shape=jax.ShapeDtypeStruct(v.shape, F32),
        scratch_shapes=[pltpu.SemaphoreType.DMA((N_DEVICES,)), pltpu.SemaphoreType.DMA((N_DEVICES,))],
    )(v)


ALL_ROWS = 40


class _AdaLN:
    def __init__(self, c, c_ctx, ada_w, ada_b, riders):
        xi, yi, ci = _mesh_pos()
        self.me, self.chip, self.core = 4 * xi + 2 * yi + ci, 2 * xi + yi, ci
        self.nb, d = c.shape
        self.ada_w, self.c_ctx = ada_w, c_ctx
        self.cols = ada_w.shape[2]
        ctx_row = self.nb * N_DEVICES
        assert ctx_row + 1 + riders.shape[0] <= ALL_ROWS
        placed = lax.dynamic_update_slice(jnp.zeros((ALL_ROWS, d), F32), c, (self.me * self.nb, 0))
        placed = lax.dynamic_update_slice(placed, riders, (ctx_row + 1, 0))
        summed = _all_reduce_small(placed, "gather_conditioning")
        self.riders = summed[ctx_row + 1:ctx_row + 1 + riders.shape[0]]
        c_all = summed.at[ctx_row].set(c_ctx)
        self.cact, = _whole("cond_silu", lambda v: (_silu(v),), [jax.ShapeDtypeStruct(c_all.shape, F32)], c_all)
        parts = []
        for i in range(2):
            bias = lax.dynamic_slice(ada_b[i], (self.chip * self.cols,), (self.cols,))[None, :]
            parts.append(_mm_nn(self.cact, ada_w[i], F32, f"mod{i}", ALL_ROWS, self.cols, d, bias=bias))
        part = jnp.concatenate(parts, axis=1)
        rows = [[t * self.nb + b for b in range(self.nb)] + [ctx_row] * (MOD_ROWS - self.nb) for t in range(N_DEVICES)]
        got = _all_to_all_small(part[jnp.asarray(rows)], "mod_exchange")
        self.mods = [jnp.concatenate([got[2 * j][:self.nb + 1, i * self.cols:(i + 1) * self.cols] for j in range(N_CHIPS)], axis=1)[:, None, :]
                     for i in range(2)]

    def backward(self, dmods):
        nb, cols, d = self.nb, self.cols, self.ada_w.shape[1]
        blocks = [jnp.concatenate([dm[:, j * cols:(j + 1) * cols] for dm in dmods], axis=1) for j in range(N_CHIPS)]
        got = _all_to_all_small(jnp.stack([blocks[t // 2] for t in range(N_DEVICES)]), "dmod_exchange")
        dall = jnp.concatenate([got[:, :nb].reshape(N_DEVICES * nb, 2 * cols), jnp.sum(got[:, nb], axis=0, keepdims=True),
                                jnp.zeros((ALL_ROWS - N_DEVICES * nb - 1, 2 * cols), F32)], axis=0)
        dctx = jnp.concatenate([dall[N_DEVICES * nb][None, :], jnp.zeros((MOD_ROWS - 1, 2 * cols), F32)], axis=0)
        grads, dcact = [], []
        for i in range(2):
            grads.append(_mm_tn(self.cact, dall[:, i * cols:(i + 1) * cols], f"ada_dw{i}", d, cols, ALL_ROWS))
            dcact.append(_mm_nt(dctx[:, i * cols:(i + 1) * cols], self.ada_w[i], F32, f"ada_dx{i}", MOD_ROWS, d, cols))

        def silu_bwd(v, d0, d1):
            sg = _sigmoid(v)
            return ((d0 + d1)[0:1] * (sg * (1.0 + v * (1.0 - sg))),)

        dc_ctx, = _whole("cond_silu_bwd", silu_bwd, [jax.ShapeDtypeStruct((1, d), F32)], self.c_ctx[None, :], dcact[0], dcact[1])
        return grads, jnp.where(self.core == 0, dc_ctx[0], jnp.zeros((d,), F32))


SMALL_ROWS = 24


def _pack_small(small, dlogit, loss_part):
    d = D_MODEL
    misc = jnp.zeros((d,), F32)
    misc = misc.at[0:HEAD_DIM].set(small["q_norm"]).at[128:128 + HEAD_DIM].set(small["k_norm"])
    misc = misc.at[256:256 + N_HEADS].set(small["sink"]).at[384:384 + 2 * RET_HEADS].set(dlogit.reshape(-1))
    misc = misc.at[512].set(loss_part)
    rows = [small["ada_b0"].reshape(6, d), small["ada_b1"].reshape(6, d), small["norm1_g0"][None], small["norm1_g1"][None],
            small["norm2_g0"][None], small["norm2_g1"][None], small["c_ctx"][None], small["gn_g"].reshape(2, d), misc[None]]
    buf = jnp.concatenate(rows, axis=0)
    return jnp.concatenate([buf, jnp.zeros((SMALL_ROWS - buf.shape[0], d), F32)], axis=0)


def _unpack_small(buf):
    d = D_MODEL
    misc = buf[19]
    return dict(ada_b=buf[0:12].reshape(2, 6 * d), norm1_g=buf[12:14], norm2_g=buf[14:16], c_ctx=buf[16],
                gn_g=buf[17:19].reshape(2 * d), q_norm=misc[0:HEAD_DIM], k_norm=misc[128:128 + HEAD_DIM],
                sink=misc[256:256 + N_HEADS], decay=misc[384:384 + 2 * RET_HEADS], loss=misc[512])


def kernel(x, c, ctx, c_ctx, ada_w, ada_b, norm1_g, norm2_g, ffn_w_in, ffn_w_out, attn_w_qkv, attn_q_norm, attn_k_norm, attn_sink, attn_w_o, ret_w_qkvg, ret_decay_logit, ret_gn_g, ret_w_o, loss_target, m_c_ctx, m_ada_w, m_ada_b, m_norm1_g, m_norm2_g, m_ffn_w_in, m_ffn_w_out, m_attn_w_qkv, m_attn_q_norm, m_attn_k_norm, m_attn_sink, m_attn_w_o, m_ret_w_qkvg, m_ret_decay_logit, m_ret_gn_g, m_ret_w_o, v_c_ctx, v_ada_w, v_ada_b, v_norm1_g, v_norm2_g, v_ffn_w_in, v_ffn_w_out, v_attn_w_qkv, v_attn_q_norm, v_attn_k_norm, v_attn_sink, v_attn_w_o, v_ret_w_qkvg, v_ret_decay_logit, v_ret_gn_g, v_ret_w_o):
    xi, yi, ci = _mesh_pos()
    chip = 2 * xi + yi
    nb, s, d = x.shape
    gn_shard = ret_gn_g.shape[1]

    shards = dict(ffn_in0=(ffn_w_in, 0), ffn_in1=(ffn_w_in, 1), ffn_out0=(ffn_w_out, 0),
                  ffn_out1=(ffn_w_out, 1), attn_qkv=(attn_w_qkv, 0), attn_o=(attn_w_o, 0), ret_qkvg=(ret_w_qkvg, 0), ret_o=(ret_w_o, 0))
    names = list(shards)
    pos = jnp.stack([ci, chip]).astype(jnp.int32)
    placed = {k: _place_shard(*shards[k], pos, f"place_{k}") for k in names}
    early = _run_job(_gather_job([placed[k] for k in EARLY_WEIGHTS]), "gather_early_weights")
    gn_mine = jnp.where(ci == 0, ret_gn_g[0], jnp.zeros_like(ret_gn_g[0]))
    gn_place = lax.dynamic_update_slice(jnp.zeros((RET_VWIDTH,), F32), gn_mine, (chip * gn_shard,))

    wts = dict(ffn_in=[None, None], ffn_out=[None, None], attn_qkv=None, attn_o=None, ret_qkvg=None, ret_o=None)
    ada = _AdaLN(c, c_ctx, ada_w, ada_b, riders=gn_place.reshape(2, d))
    gn_full = ada.riders.reshape(RET_VWIDTH)
    _fill_weights(wts, dict(zip(EARLY_WEIGHTS, early)))
    plan = _StepPlan(placed, pos)
    decay_logit = ret_decay_logit[0]
    sp = dict(norm1_g=norm1_g, norm2_g=norm2_g, q_norm=attn_q_norm[0], k_norm=attn_k_norm[0],
              sink=attn_sink[0], log_g=jax.nn.log_sigmoid(decay_logit), gn_g=gn_full)
    loss_part, dz, big, small, dmods = _local_step(x, ctx, loss_target, sp, wts, ada.mods, plan)
    ada_grads, small["c_ctx"] = ada.backward(dmods)

    grad_x = dz.reshape(nb, s, d)

    dlogit = small["log_g"] * jax.nn.sigmoid(-decay_logit)
    sg = _unpack_small(_all_reduce_small(_pack_small(small, dlogit, loss_part[0, 0]), "reduce_small_grads"))
    loss = sg["loss"]
    halves = dict(plan.reduced)
    halves.update(zip(LAST_GRADS, _ReduceScatter([big[k] for k in LAST_GRADS], pos, "last_").run()))
    reduced = dict(zip(halves, _pair_share(list(halves.values()), "grads_pair_share")))

    grads = dict(
        c_ctx=sg["c_ctx"], ada_w=jnp.stack(ada_grads), ada_b=sg["ada_b"], norm1_g=sg["norm1_g"],
        norm2_g=sg["norm2_g"], ffn_w_in=jnp.stack([reduced["ffn_in0"], reduced["ffn_in1"]]),
        ffn_w_out=jnp.stack([reduced["ffn_out0"], reduced["ffn_out1"]]), attn_w_qkv=reduced["attn_qkv"][None],
        attn_q_norm=sg["q_norm"][None], attn_k_norm=sg["k_norm"][None], attn_sink=sg["sink"][None],
        attn_w_o=reduced["attn_o"][None], ret_w_qkvg=reduced["ret_qkvg"][None], ret_decay_logit=sg["decay"].reshape(1, 2, RET_HEADS),
        ret_gn_g=lax.dynamic_slice(sg["gn_g"], (chip * gn_shard,), (gn_shard,))[None], ret_w_o=reduced["ret_o"][None])
    params = dict(c_ctx=(c_ctx, m_c_ctx, v_c_ctx), ada_w=(ada_w, m_ada_w, v_ada_w), ada_b=(ada_b, m_ada_b, v_ada_b),
                  norm1_g=(norm1_g, m_norm1_g, v_norm1_g), norm2_g=(norm2_g, m_norm2_g, v_norm2_g),
                  ffn_w_in=(ffn_w_in, m_ffn_w_in, v_ffn_w_in), ffn_w_out=(ffn_w_out, m_ffn_w_out, v_ffn_w_out),
                  attn_w_qkv=(attn_w_qkv, m_attn_w_qkv, v_attn_w_qkv), attn_q_norm=(attn_q_norm, m_attn_q_norm, v_attn_q_norm),
                  attn_k_norm=(attn_k_norm, m_attn_k_norm, v_attn_k_norm), attn_sink=(attn_sink, m_attn_sink, v_attn_sink),
                  attn_w_o=(attn_w_o, m_attn_w_o, v_attn_w_o), ret_w_qkvg=(ret_w_qkvg, m_ret_w_qkvg, v_ret_w_qkvg),
                  ret_decay_logit=(ret_decay_logit, m_ret_decay_logit, v_ret_decay_logit),
                  ret_gn_g=(ret_gn_g, m_ret_gn_g, v_ret_gn_g), ret_w_o=(ret_w_o, m_ret_w_o, v_ret_w_o))
    order = list(params)
    deltas, new_m, new_v = [], [], []
    for k in order:
        w, m, v = params[k]
        g = grads[k].reshape(w.shape)
        grads[k] = g
        flat = (-1, w.shape[-1]) if w.ndim > 1 else (1, -1)
        if k == "ret_decay_logit":
            flat = (1, -1)
        dw, nm, nv = _adamw(w.reshape(flat), g.reshape(flat), m.reshape(flat), v.reshape(flat), f"adamw_{k}")
        deltas.append(dw.reshape(w.shape))
        new_m.append(nm.reshape(w.shape))
        new_v.append(nv.reshape(w.shape))
    return (loss, grad_x, *[grads[k] for k in order], *deltas, *new_m, *new_v)
```

```python
import functools

import jax
import jax.numpy as jnp
from jax import lax
from jax.experimental import pallas as pl
from jax.experimental.pallas import tpu as pltpu

F32 = jnp.float32
BF16 = jnp.bfloat16

D_MODEL = 1024
N_HEADS = 16
N_KV_HEADS = 4
HEAD_DIM = 64
WINDOW = 128
ATTN_BLOCK = 128
BAND = ATTN_BLOCK + 2 * WINDOW
RET_HEADS = 4
RET_QK_DIM = 256
RET_V_DIM = 512
RET_VWIDTH = 2048
RET_CHUNK = 128
D_FF = 2816
GRID_W = 64
ROPE_BASE = 10000.0
EPS = 1e-6
NEG_INF = -1e30
LANES = 128

ADAM_LR = 0.001
ADAM_B1 = 0.9
ADAM_B2 = 0.999
ADAM_EPS = 1e-08
ADAM_WD = 0.01
ADAM_STEP = 10

VMEM_LIMIT_BYTES = 56 * 1024 * 1024
MESH = pl.DeviceIdType.MESH
N_CHIPS = 4


def _cparams(*sem):
    return pltpu.CompilerParams(dimension_semantics=sem, vmem_limit_bytes=VMEM_LIMIT_BYTES)


_DIMS = {"nn": ((1,), (0,)), "nt": ((1,), (1,)), "tn": ((0,), (0,))}


def _dot(a, b, form):
    return lax.dot_general(a.astype(BF16), b.astype(BF16), (_DIMS[form], ((), ())), preferred_element_type=F32)


@functools.partial(jax.custom_vjp, nondiff_argnums=(2,))
def _mm(a, b, form):
    return _dot(a, b, form)


def _mm_fwd(a, b, form):
    return _dot(a, b, form), (a, b)


def _mm_bwd(form, res, ct):
    a, b = res
    if form == "nn":
        da, db = _dot(ct, b, "nt"), _dot(a, ct, "tn")
    elif form == "nt":
        da, db = _dot(ct, b, "nn"), _dot(ct, a, "tn")
    else:
        da, db = _dot(b, ct, "nt"), _dot(a, ct, "nn")
    return da.astype(a.dtype), db.astype(b.dtype)


_mm.defvjp(_mm_fwd, _mm_bwd)


def _swap_halves(x, half):
    w = x.shape[-1]
    lane = lax.broadcasted_iota(jnp.int32, x.shape, x.ndim - 1)
    return jnp.where(lane % (2 * half) < half, pltpu.roll(x, w - half, x.ndim - 1), pltpu.roll(x, half, x.ndim - 1))


@functools.partial(jax.custom_vjp, nondiff_argnums=(1,))
def _rot(x, half):
    return _swap_halves(x, half)


def _rot_fwd(x, half):
    return _swap_halves(x, half), None


def _rot_bwd(half, _, ct):
    return (_swap_halves(ct, half),)


_rot.defvjp(_rot_fwd, _rot_bwd)


def _rope(x, cos, sin_signed, half):
    return x * cos + _rot(x, half) * sin_signed


def _head_mean_square(x):
    r = lax.broadcasted_iota(jnp.int32, (LANES, LANES), 0) // HEAD_DIM
    c = lax.broadcasted_iota(jnp.int32, (LANES, LANES), 1) // HEAD_DIM
    g = jnp.where(r == c, 1.0 / HEAD_DIM, 0.0).astype(F32)
    return jnp.dot(x * x, g, precision=lax.Precision.HIGHEST, preferred_element_type=F32)


def _qk_chunk(x, gain, cos, sin_signed, scale):
    y = x * lax.rsqrt(_head_mean_square(x) + EPS) * gain
    return _rope(y, cos, sin_signed, HEAD_DIM // 4) * scale


def _sigmoid(x):
    return 1.0 / (1.0 + jnp.exp(-x))


def _silu(x):
    return x * _sigmoid(x)


def _mm_nn(a, w, out_dtype, name, tm, tn, tk, bias=None):
    m, k_dim = a.shape
    if w.ndim == 3:
        n = w.shape[0] * w.shape[2]
        per = w.shape[2] // tn
        assert w.shape[2] % tn == 0
        w_spec = pl.BlockSpec((None, tk, tn), lambda i, j, k: (j // per, k, j % per))
    else:
        n = w.shape[1]
        w_spec = pl.BlockSpec((tk, tn), lambda i, j, k: (k, j))
    assert m % tm == 0 and n % tn == 0 and k_dim % tk == 0, (name, a.shape, w.shape, tm, tn, tk)
    nk = k_dim // tk
    has_bias = bias is not None

    def body(*refs):
        a_ref, w_ref = refs[0], refs[1]
        b_ref = refs[2] if has_bias else None
        o_ref, acc_ref = (refs[-1], None) if nk == 1 else (refs[-2], refs[-1])
        if nk == 1:
            part = jnp.dot(a_ref[...].astype(BF16), w_ref[...].astype(BF16), preferred_element_type=F32)
            o_ref[...] = (part + b_ref[...] if has_bias else part).astype(out_dtype)
            return
        k = pl.program_id(2)

        @pl.when(k == 0)
        def _():
            acc_ref[...] = jnp.zeros_like(acc_ref)

        acc_ref[...] += jnp.dot(a_ref[...].astype(BF16), w_ref[...].astype(BF16), preferred_element_type=F32)

        @pl.when(k == nk - 1)
        def _():
            r = acc_ref[...]
            if has_bias:
                r = r + b_ref[...]
            o_ref[...] = r.astype(out_dtype)

    in_specs = [pl.BlockSpec((tm, tk), lambda i, j, k: (i, k)), w_spec]
    args = [a, w]
    if has_bias:
        in_specs.append(pl.BlockSpec((1, tn), lambda i, j, k: (0, j)))
        args.append(bias)
    return pl.pallas_call(
        body, name=name, grid=(m // tm, n // tn, nk), in_specs=in_specs,
        out_specs=pl.BlockSpec((tm, tn), lambda i, j, k: (i, j)),
        out_shape=jax.ShapeDtypeStruct((m, n), out_dtype),
        scratch_shapes=[pltpu.VMEM((tm, tn), F32)] if nk > 1 else [],
        compiler_params=_cparams("parallel", "parallel", "arbitrary"),
    )(*args)


def _mm_nt(a, w, out_dtype, name, tm, tn, tk):
    if a.ndim == 3:
        planes, m, plane_w = a.shape
        c_dim = planes * plane_w
        a_per = plane_w // tk
        assert plane_w % tk == 0
        a_spec = pl.BlockSpec((None, tm, tk), lambda i, j, k: (k // a_per, i, k % a_per))
    else:
        m, c_dim = a.shape
        a_spec = pl.BlockSpec((tm, tk), lambda i, j, k: (i, k))
    if w.ndim == 3:
        k_out = w.shape[1]
        per = w.shape[2] // tk
        assert w.shape[2] % tk == 0 and w.shape[0] * w.shape[2] == c_dim
        w_spec = pl.BlockSpec((None, tn, tk), lambda i, j, k: (k // per, j, k % per))
    else:
        k_out = w.shape[0]
        assert w.shape[1] == c_dim
        w_spec = pl.BlockSpec((tn, tk), lambda i, j, k: (j, k))
    assert m % tm == 0 and k_out % tn == 0 and c_dim % tk == 0, (name, a.shape, w.shape, tm, tn, tk)
    nk = c_dim // tk

    def body(a_ref, w_ref, o_ref, acc_ref=None):
        if nk == 1:
            o_ref[...] = _dot(a_ref[...], w_ref[...], "nt").astype(out_dtype)
            return
        k = pl.program_id(2)

        @pl.when(k == 0)
        def _():
            acc_ref[...] = jnp.zeros_like(acc_ref)

        acc_ref[...] += _dot(a_ref[...], w_ref[...], "nt")

        @pl.when(k == nk - 1)
        def _():
            o_ref[...] = acc_ref[...].astype(out_dtype)

    return pl.pallas_call(
        body, name=name, grid=(m // tm, k_out // tn, nk),
        in_specs=[a_spec, w_spec],
        out_specs=pl.BlockSpec((tm, tn), lambda i, j, k: (i, j)),
        out_shape=jax.ShapeDtypeStruct((m, k_out), out_dtype),
        scratch_shapes=[pltpu.VMEM((tm, tn), F32)] if nk > 1 else [],
        compiler_params=_cparams("parallel", "parallel", "arbitrary"),
    )(a, w)


def _mm_tn(a, b, name, tm, tn, tk, shards=None, out_dtype=F32):
    r, k_dim = a.shape
    if b.ndim == 3:
        n = b.shape[0] * b.shape[2]
        b_per = b.shape[2] // tn
        assert b.shape[2] % tn == 0
        b_spec = pl.BlockSpec((None, tk, tn), lambda i, j, k: (j // b_per, k, j % b_per))
    else:
        n = b.shape[1]
        b_spec = pl.BlockSpec((tk, tn), lambda i, j, k: (k, j))
    assert r % tk == 0 and k_dim % tm == 0 and n % tn == 0, (name, a.shape, b.shape, tm, tn, tk)
    nk = r // tk
    if shards:
        per = n // shards // tn
        assert n % (shards * tn) == 0
        out_shape = jax.ShapeDtypeStruct((shards, k_dim, n // shards), out_dtype)
        out_spec = pl.BlockSpec((None, tm, tn), lambda i, j, k: (j // per, i, j % per))
    else:
        out_shape = jax.ShapeDtypeStruct((k_dim, n), out_dtype)
        out_spec = pl.BlockSpec((tm, tn), lambda i, j, k: (i, j))
    direct = out_dtype == F32

    def body(a_ref, b_ref, o_ref, *scratch):
        acc_ref = o_ref if direct else scratch[0]
        k = pl.program_id(2)

        @pl.when(k == 0)
        def _():
            acc_ref[...] = jnp.zeros_like(acc_ref)

        acc_ref[...] += _dot(a_ref[...], b_ref[...], "tn")
        if not direct:
            @pl.when(k == nk - 1)
            def _():
                o_ref[...] = acc_ref[...].astype(out_dtype)

    return pl.pallas_call(
        body, name=name, grid=(k_dim // tm, n // tn, nk),
        in_specs=[pl.BlockSpec((tk, tm), lambda i, j, k: (k, i)), b_spec],
        out_specs=out_spec, out_shape=out_shape,
        scratch_shapes=[] if direct else [pltpu.VMEM((tm, tn), F32)],
        compiler_params=_cparams("parallel", "parallel", "arbitrary"),
    )(a, b)


class _Carrier:
    def __init__(self, job, n_in, n_out, n_scratch):
        self.job, self.n_in, self.n_out, self.n_scratch = job, n_in, n_out, n_scratch
        self.ji = len(job.inputs) if job else 0
        self.jo = len(job.out_shapes) if job else 0

    def operands(self):
        return list(self.job.inputs) if self.job else []

    def in_specs(self):
        return [pl.BlockSpec(memory_space=pl.ANY)] * self.ji

    def out_specs(self):
        return [pl.BlockSpec(memory_space=pl.ANY)] * self.jo

    def out_shapes(self):
        return list(self.job.out_shapes) if self.job else []

    def scratch(self):
        return list(self.job.sem_shapes) if self.job else []

    def aliases(self):
        return {self.n_in + a: self.n_out + b for a, b in self.job.aliases.items()} if self.job else {}

    def split(self, refs):
        a = self.n_in
        b = a + self.ji
        c = b + self.n_out
        d = c + self.jo
        e = d + self.n_scratch
        return list(refs[:a]) + list(refs[b:c]) + list(refs[d:e]), (refs[a:b], refs[c:d], refs[e:])

    def run(self, job_refs, step, steps):
        if not self.job:
            return
        for stage, mark in zip(self.job.stages, _job_marks(self.job, steps)):
            pl.when(step == mark)(functools.partial(stage, *job_refs))

    def results(self, res):
        res = list(res)
        return res[:self.n_out], res[self.n_out:]


FFN_ROW_TILE = 768


def _ffn_tile(r):
    return FFN_ROW_TILE if r % FFN_ROW_TILE == 0 else _row_tile(r)


def _ffn_in_swiglu(h, w, name):
    r, k_dim = h.shape
    n4 = w.shape[2]
    tm = _ffn_tile(r)

    def body(h_ref, wg_ref, wu_ref, u_ref, a_ref):
        hv = h_ref[...]
        g = jnp.dot(hv, wg_ref[...], preferred_element_type=F32)
        up = jnp.dot(hv, wu_ref[...], preferred_element_type=F32)
        u_ref[0] = g.astype(BF16)
        u_ref[1] = up.astype(BF16)
        a_ref[...] = (_silu(g) * up).astype(BF16)

    return pl.pallas_call(
        body, name=name, grid=(r // tm, 2),
        in_specs=[pl.BlockSpec((tm, k_dim), lambda i, j: (i, 0)),
                  pl.BlockSpec((None, k_dim, n4), lambda i, j: (j, 0, 0)),
                  pl.BlockSpec((None, k_dim, n4), lambda i, j: (j + 2, 0, 0))],
        out_specs=[pl.BlockSpec((2, tm, n4), lambda i, j: (0, i, j)), pl.BlockSpec((tm, n4), lambda i, j: (i, j))],
        out_shape=[jax.ShapeDtypeStruct((2, r, 2 * n4), BF16), jax.ShapeDtypeStruct((r, 2 * n4), BF16)],
        compiler_params=_cparams("parallel", "parallel"),
    )(h, w, w)


def _mm_nn_gate_residual(geo, a, w, z, mod, off, name, norm=None):
    r, k_dim = a.shape
    n = w.shape[1]
    tm = FFN_ROW_TILE if geo.seg % FFN_ROW_TILE == 0 else 256
    tiles = geo.seg // tm
    assert geo.seg % tm == 0 and r == geo.r and n == D_MODEL

    def body(a_ref, w_ref, z_ref, mx_ref, mc_ref, *rest):
        out = jnp.dot(a_ref[...], w_ref[...], preferred_element_type=F32)
        is_x = (pl.program_id(0) % tiles) * tm + lax.broadcasted_iota(jnp.int32, (tm, 1), 0) < geo.s
        zo = z_ref[...] + jnp.where(is_x, mx_ref[:, off:off + n], mc_ref[:, off:off + n]) * out
        if norm:
            g_ref, nx_ref, nc_ref, zo_ref, raw_ref, h_ref = rest
            no = norm[2]
            shift = jnp.where(is_x, nx_ref[:, no:no + n], nc_ref[:, no:no + n])
            scale = jnp.where(is_x, nx_ref[:, no + n:no + 2 * n], nc_ref[:, no + n:no + 2 * n])
            rs = lax.rsqrt(jnp.mean(zo * zo, axis=-1, keepdims=True) + EPS)
            h_ref[...] = ((zo * rs) * g_ref[...] * (1.0 + scale) + shift).astype(BF16)
        else:
            zo_ref, raw_ref = rest
        zo_ref[...] = zo
        raw_ref[...] = out.astype(BF16)

    def mod_specs(m):
        return [pl.BlockSpec((None, 1, m.shape[2]), lambda i: (i // tiles, 0, 0)), pl.BlockSpec((None, 1, m.shape[2]), lambda i: (geo.b, 0, 0))]

    row = pl.BlockSpec((tm, n), lambda i: (i, 0))
    in_specs = [pl.BlockSpec((tm, k_dim), lambda i: (i, 0)), pl.BlockSpec((k_dim, n), lambda i: (0, 0)), row] + mod_specs(mod)
    args = [a, w, z, mod, mod]
    out_specs, out_shape = [row, row], [jax.ShapeDtypeStruct((r, n), F32), jax.ShapeDtypeStruct((r, n), BF16)]
    if norm:
        in_specs += [pl.BlockSpec((1, n), lambda i: (0, 0))] + mod_specs(norm[1])
        args += [norm[0], norm[1], norm[1]]
        out_specs.append(row)
        out_shape.append(jax.ShapeDtypeStruct((r, n), BF16))
    res = pl.pallas_call(body, name=name, grid=(r // tm,), in_specs=in_specs, out_specs=out_specs, out_shape=out_shape,
                         compiler_params=_cparams("parallel"))(*args)
    return res if norm else (*res, None)


def _ffn_out_dx_swiglu_bwd(df, w_out, u, name, job=None):
    r, d = df.shape
    n4 = u.shape[2] // 2
    tm = _ffn_tile(r)
    carrier = _Carrier(job, 3, 1, 0)
    steps = (r // tm) * 2

    def body(*refs):
        (df_ref, w_ref, u_ref, du_ref), job_refs = carrier.split(refs)
        carrier.run(job_refs, pl.program_id(0) * 2 + pl.program_id(1), steps)
        da = _dot(df_ref[...], w_ref[...], "nt")
        g, up = u_ref[0].astype(F32), u_ref[1].astype(F32)
        s = _sigmoid(g)
        du_ref[0] = (da * up * (s * (1.0 + g * (1.0 - s)))).astype(BF16)
        du_ref[1] = (da * (g * s)).astype(BF16)

    res = pl.pallas_call(
        body, name=name, grid=(r // tm, 2),
        in_specs=[pl.BlockSpec((tm, d), lambda i, j: (i, 0)), pl.BlockSpec((n4, d), lambda i, j: (j, 0)),
                  pl.BlockSpec((2, tm, n4), lambda i, j: (0, i, j))] + carrier.in_specs(),
        out_specs=[pl.BlockSpec((2, tm, n4), lambda i, j: (0, i, j))] + carrier.out_specs(),
        out_shape=[jax.ShapeDtypeStruct(u.shape, BF16)] + carrier.out_shapes(),
        scratch_shapes=carrier.scratch(), input_output_aliases=carrier.aliases(),
        compiler_params=_cparams("arbitrary", "arbitrary"),
    )(df, w_out, u, *carrier.operands())
    (du,), extra = carrier.results(res)
    return du, extra


class _Rows:
    def __init__(self, b, s, l):
        self.b, self.s, self.l = b, s, l
        self.seg = s + l
        self.r = b * self.seg


def _rowwise(name, body, geo, tm, ins, outs, job=None):
    seg_blocks, x_blocks = geo.seg // tm, geo.s // tm
    per_part = {"ex", "xrow", "crow"} & {k for _, k in ins if isinstance(k, str)} or {"exacc", "xrow"} & {o[0] for o in outs}
    assert geo.seg % tm == 0 and (geo.s % tm == 0 or not per_part), (name, tm)
    nb = geo.b

    def is_ctx(i):
        return i % seg_blocks >= x_blocks

    in_specs, args = [], []
    for arr, kind in ins:
        args.append(arr)
        if kind == "row":
            in_specs.append(pl.BlockSpec((tm, arr.shape[1]), lambda i: (i, 0)))
        elif kind == "ex":
            in_specs.append(pl.BlockSpec((None, 1, arr.shape[2]), lambda i: (jnp.where(is_ctx(i), nb, i // seg_blocks), 0, 0)))
        elif kind == "full":
            in_specs.append(pl.BlockSpec(arr.shape, lambda i, nd=arr.ndim: (0,) * nd))
        elif kind == "tab":
            in_specs.append(pl.BlockSpec((tm, arr.shape[1]), lambda i: (i % seg_blocks, 0)))
        elif kind == "xrow":
            in_specs.append(pl.BlockSpec(
                (tm, arr.shape[1]), lambda i: ((i // seg_blocks) * x_blocks + jnp.minimum(i % seg_blocks, x_blocks - 1), 0)))
        elif kind == "crow":
            c_blocks = seg_blocks - x_blocks
            in_specs.append(pl.BlockSpec(
                (tm, arr.shape[1]), lambda i: ((i // seg_blocks) * c_blocks + jnp.maximum(i % seg_blocks - x_blocks, 0), 0)))
        else:
            _, width, cb = kind
            in_specs.append(pl.BlockSpec((tm, width), lambda i, cb=cb: (i, cb)))
    out_specs, out_shapes = [], []
    for o in outs:
        if o[0] == "row":
            out_specs.append(pl.BlockSpec((tm, o[1]), lambda i: (i, 0)))
            out_shapes.append(jax.ShapeDtypeStruct((geo.r, o[1]), o[2]))
        elif o[0] == "xrow":
            out_specs.append(pl.BlockSpec(
                (tm, o[1]), lambda i: ((i // seg_blocks) * x_blocks + jnp.minimum(i % seg_blocks, x_blocks - 1), 0)))
            out_shapes.append(jax.ShapeDtypeStruct((geo.b * geo.s, o[1]), o[2]))
        elif o[0] == "exacc":
            out_specs.append(pl.BlockSpec((None, 1, o[1]), lambda i: (jnp.where(is_ctx(i), nb, 0) + i // seg_blocks, 0, 0)))
            out_shapes.append(jax.ShapeDtypeStruct((2 * nb, 1, o[1]), F32))
        else:
            out_specs.append(pl.BlockSpec((o[1], o[2]), lambda i: (0, 0)))
            out_shapes.append(jax.ShapeDtypeStruct((o[1], o[2]), F32))
    n_in = len(ins)
    carrier = _Carrier(job, n_in, len(outs), 0)

    def kern(*refs):
        i = pl.program_id(0)
        refs, job_refs = carrier.split(refs)
        carrier.run(job_refs, i, geo.r // tm)
        res = body(i, *[r[...].astype(F32) for r in refs[:n_in]])
        if not isinstance(res, (tuple, list)):
            res = (res,)
        jj = i % seg_blocks
        first_of_part = (jj == 0) | (jj == x_blocks)
        for o, ref, val in zip(outs, refs[n_in:], res):
            if o[0] == "row":
                ref[...] = val.astype(ref.dtype)
            elif o[0] == "xrow":
                @pl.when(jj < x_blocks)
                def _(ref=ref, val=val):
                    ref[...] = val.astype(ref.dtype)
            else:
                first = first_of_part if o[0] == "exacc" else i == 0

                @pl.when(first)
                def _(ref=ref, val=val):
                    ref[...] = val

                @pl.when(jnp.logical_not(first))
                def _(ref=ref, val=val):
                    ref[...] += val

    res = pl.pallas_call(
        kern, name=name, grid=(geo.r // tm,), in_specs=in_specs + carrier.in_specs(), out_specs=out_specs + carrier.out_specs(),
        out_shape=out_shapes + carrier.out_shapes(), scratch_shapes=carrier.scratch(), input_output_aliases=carrier.aliases(),
        compiler_params=_cparams("arbitrary"),
    )(*args, *carrier.operands())
    own, extra = carrier.results(res)
    if job:
        return (*own, extra)
    return own[0] if len(own) == 1 else own


def _colsum(v):
    return jnp.sum(v, axis=0, keepdims=True)


def _first_norm(geo, x_rows, ctx_rows, gain, mod, name):
    d = D_MODEL
    seg_blocks, x_blocks = geo.seg // 256, geo.s // 256

    def body(i, xv, cv, g, m):
        zv = jnp.where(i % seg_blocks >= x_blocks, cv, xv)
        r = lax.rsqrt(jnp.mean(zv * zv, axis=-1, keepdims=True) + EPS)
        return zv, (zv * r) * g * (1.0 + m[:, d:2 * d]) + m[:, :d]

    return _rowwise(name, body, geo, 256, [(x_rows, "xrow"), (ctx_rows, "crow"), (gain, "full"), (mod, "ex")],
                    [("row", d, F32), ("row", d, BF16)])


def _norm_mod_bwd(geo, z, gain, mod, off, dh, dz_skip, name, gated=None, latent_only=False, job=None):
    d = D_MODEL

    def body(i, zv, g, m, dhv, skip, *rest):
        r = lax.rsqrt(jnp.mean(zv * zv, axis=-1, keepdims=True) + EPS)
        n = zv * r
        dng = dhv * (1.0 + m[:, off + d:off + 2 * d])
        dn = dng * g
        dz = r * (dn - n * jnp.mean(dn * n, axis=-1, keepdims=True)) + skip
        res = (dz, _colsum(dhv), _colsum(dhv * (n * g)), _colsum(dng * n))
        if gated:
            ov, gm = rest
            res += (dz * gm[:, gated[2]:gated[2] + d], _colsum(dz * ov))
        return res

    ins = [(z, "row"), (gain, "full"), (mod, "ex"), (dh, "row"), (dz_skip, "row")]
    outs = [("xrow" if latent_only else "row", d, F32), ("exacc", d), ("exacc", d), ("gacc", 1, d)]
    if gated:
        ins += [(gated[0], "row"), (gated[1], "ex")]
        outs += [("row", d, BF16), ("exacc", d)]
    return _rowwise(name, body, geo, 256, ins, outs, job)


def _loss_head(geo, z, target, out, mod, off, name):
    seg_blocks, x_blocks = geo.seg // 256, geo.s // 256
    d = D_MODEL

    def body(i, zv, tv, ov, m):
        keep = jnp.where(i % seg_blocks >= x_blocks, 0.0, 1.0)
        err = (zv - tv) * keep
        part = 0.5 * jnp.sum(jnp.mean(err * err, axis=-1, keepdims=True), axis=0, keepdims=True)
        dz = err * (1.0 / d)
        return dz, jnp.broadcast_to(part, (1, LANES)), dz * m[:, off:off + d], _colsum(dz * ov)

    return _rowwise(name, body, geo, 256, [(z, "row"), (target, "xrow"), (out, "row"), (mod, "ex")],
                    [("row", d, F32), ("gacc", 1, LANES), ("row", d, BF16), ("exacc", d)])


Q_SCALE = HEAD_DIM ** -0.5
N_QK_CHUNKS = (N_HEADS + N_KV_HEADS) * HEAD_DIM // LANES
N_Q_CHUNKS = N_HEADS * HEAD_DIM // LANES


def _prep_tile(geo):
    return FFN_ROW_TILE if geo.seg % FFN_ROW_TILE == 0 else 256


def _attn_prep(geo, proj, cos, sin_signed, q_gain, k_gain, name):
    def body(i, p, cs, sn, qg, kg):
        outs = []
        for ch in range(N_QK_CHUNKS):
            is_q = ch < N_Q_CHUNKS
            outs.append(_qk_chunk(p[:, ch * LANES:(ch + 1) * LANES], qg if is_q else kg, cs, sn, Q_SCALE if is_q else 1.0))
        outs.append(p[:, N_QK_CHUNKS * LANES:])
        return jnp.concatenate(outs, axis=1)

    return _rowwise(name, body, geo, _prep_tile(geo), [(proj, "row"), (cos, "tab"), (sin_signed, "tab"), (q_gain, "full"), (k_gain, "full")],
                    [("row", proj.shape[1], BF16)])


def _attn_prep_bwd(geo, proj, cos, sin_signed, q_gain, k_gain, dq, dkv, name):
    kw = N_KV_HEADS * HEAD_DIM

    def body(i, p, cs, sn, qg, kg, dqv, dkvv):
        outs = []
        dgains = [jnp.zeros((1, LANES), F32), jnp.zeros((1, LANES), F32)]
        for ch in range(N_QK_CHUNKS):
            is_q = ch < N_Q_CHUNKS
            scale = Q_SCALE if is_q else 1.0
            ct = dqv[:, ch * LANES:(ch + 1) * LANES] if is_q else dkvv[:, (ch - N_Q_CHUNKS) * LANES:(ch - N_Q_CHUNKS + 1) * LANES]
            _, vjp = jax.vjp(lambda xx, gg, scale=scale: _qk_chunk(xx, gg, cs, sn, scale),
                             p[:, ch * LANES:(ch + 1) * LANES], qg if is_q else kg)
            dx, dg = vjp(ct)
            outs.append(dx)
            dgains[0 if is_q else 1] = dgains[0 if is_q else 1] + dg
        outs.append(dkvv[:, kw:])
        return jnp.concatenate(outs, axis=1), dgains[0], dgains[1]

    return _rowwise(name, body, geo, 256,
                    [(proj, "row"), (cos, "tab"), (sin_signed, "tab"), (q_gain, "full"), (k_gain, "full"), (dq, "row"), (dkv, "row")],
                    [("row", proj.shape[1], BF16), ("gacc", 1, LANES), ("gacc", 1, LANES)])


def _attn_geometry(geo):
    assert geo.s % ATTN_BLOCK == 0 and geo.l % ATTN_BLOCK == 0 and geo.seg >= BAND
    return geo.seg // ATTN_BLOCK, geo.s // ATTN_BLOCK


def _attn_mask(j, s0, geo):
    r = lax.broadcasted_iota(jnp.int32, (ATTN_BLOCK, geo.l + BAND), 0)
    n = lax.broadcasted_iota(jnp.int32, (ATTN_BLOCK, geo.l + BAND), 1) - geo.l
    dist = (s0 - j * ATTN_BLOCK) + n - r
    return (n < 0) | ((jnp.abs(dist) <= WINDOW) & (s0 + n < geo.s))


def _attn_probs(q, keys, valid, n_ctx, sink):
    s = _dot(q, keys, "nt")
    if valid is not None:
        s = jnp.where(valid, s, NEG_INF)
    m = jnp.maximum(jnp.max(s, axis=-1, keepdims=True), sink)
    e, e_sink = jnp.exp(s - m), jnp.exp(sink - m)
    inv = 1.0 / (jnp.sum(e, axis=-1, keepdims=True) + e_sink)
    return e * inv, e_sink * inv


def _attn_keys(ref, s0, geo, with_band):
    ctx = ref[geo.s:geo.seg, :]
    return jnp.concatenate([ctx, ref[pl.ds(s0, BAND), :]], axis=0) if with_band else ctx


def _attention(geo, qkv, sink, name, job=None):
    n_blocks, n_x_blocks = _attn_geometry(geo)
    qw, kw = N_HEADS * HEAD_DIM, N_KV_HEADS * HEAD_DIM
    group = N_HEADS // N_KV_HEADS
    carrier = _Carrier(job, 4, 1, 0)

    def kern(*refs):
        (sink_ref, q_ref, k_ref, v_ref, o_ref), job_refs = carrier.split(refs)
        j = pl.program_id(1)
        carrier.run(job_refs, pl.program_id(0) * n_blocks + j, geo.b * n_blocks)
        s0 = pl.multiple_of(jnp.clip((j - 1) * ATTN_BLOCK, 0, geo.seg - BAND), ATTN_BLOCK)

        def heads(with_band):
            valid = _attn_mask(j, s0, geo) if with_band else None
            k_all, v_all = _attn_keys(k_ref, s0, geo, with_band), _attn_keys(v_ref, s0, geo, with_band)
            for h in range(N_HEADS):
                kv = slice((h // group) * HEAD_DIM, (h // group + 1) * HEAD_DIM)
                p, _ = _attn_probs(q_ref[:, h * HEAD_DIM:(h + 1) * HEAD_DIM], k_all[:, kv], valid, geo.l, sink_ref[h])
                o_ref[:, h * HEAD_DIM:(h + 1) * HEAD_DIM] = _dot(p, v_all[:, kv], "nn").astype(BF16)

        pl.when(j < n_x_blocks)(lambda: heads(True))
        pl.when(j >= n_x_blocks)(lambda: heads(False))

    res = pl.pallas_call(
        kern, name=name, grid=(geo.b, n_blocks),
        in_specs=[pl.BlockSpec(memory_space=pltpu.SMEM),
                  pl.BlockSpec((ATTN_BLOCK, qw), lambda b, j: (b * n_blocks + j, 0)),
                  pl.BlockSpec((geo.seg, kw), lambda b, j: (b, qw // kw)),
                  pl.BlockSpec((geo.seg, kw), lambda b, j: (b, qw // kw + 1))] + carrier.in_specs(),
        out_specs=[pl.BlockSpec((ATTN_BLOCK, qw), lambda b, j: (b * n_blocks + j, 0))] + carrier.out_specs(),
        out_shape=[jax.ShapeDtypeStruct((geo.r, qw), BF16)] + carrier.out_shapes(),
        scratch_shapes=carrier.scratch(), input_output_aliases=carrier.aliases(),
        compiler_params=_cparams("arbitrary", "arbitrary"),
    )(sink, qkv, qkv, qkv, *carrier.operands())
    (o,), extra = carrier.results(res)
    return o, extra


def _attention_bwd(geo, qkv, sink, do, name, job=None):
    n_blocks, n_x_blocks = _attn_geometry(geo)
    qw, kw = N_HEADS * HEAD_DIM, N_KV_HEADS * HEAD_DIM
    group = N_HEADS // N_KV_HEADS

    carrier = _Carrier(job, 5, 3, 1)

    def kern(*refs):
        (sink_ref, q_ref, k_ref, v_ref, do_ref, dq_ref, dkv_out_ref, dsink_ref, dkv_ref), job_refs = carrier.split(refs)
        b, j = pl.program_id(0), pl.program_id(1)
        carrier.run(job_refs, b * n_blocks + j, geo.b * n_blocks)
        s0 = pl.multiple_of(jnp.clip((j - 1) * ATTN_BLOCK, 0, geo.seg - BAND), ATTN_BLOCK)

        @pl.when(j == 0)
        def _():
            dkv_ref[...] = jnp.zeros_like(dkv_ref)

        @pl.when((j == 0) & (b == 0))
        def _():
            dsink_ref[...] = jnp.zeros_like(dsink_ref)

        def heads(with_band):
            valid = _attn_mask(j, s0, geo) if with_band else None
            k_all, v_all = _attn_keys(k_ref, s0, geo, with_band), _attn_keys(v_ref, s0, geo, with_band)
            for g in range(N_KV_HEADS):
                kv = slice(g * HEAD_DIM, (g + 1) * HEAD_DIM)
                keys, vals = k_all[:, kv], v_all[:, kv]
                group_heads = [slice(h * HEAD_DIM, (h + 1) * HEAD_DIM) for h in range(g * group, (g + 1) * group)]
                ds_rows, p_rows = [], []
                for h, hs in zip(range(g * group, (g + 1) * group), group_heads):
                    dout = do_ref[:, hs]
                    p, p_sink = _attn_probs(q_ref[:, hs], keys, valid, geo.l, sink_ref[h])
                    dp = _dot(dout, vals, "nt")
                    dsum = jnp.sum(p * dp, axis=-1, keepdims=True)
                    ds = (p * (dp - dsum)).astype(BF16)
                    dq_ref[:, hs] = _dot(ds, keys, "nn").astype(BF16)
                    ds_rows.append(ds)
                    p_rows.append(p.astype(BF16))
                    dsink_ref[h:h + 1, :] += jnp.broadcast_to(-jnp.sum(p_sink * dsum, axis=0, keepdims=True), (1, LANES))
                q_rows = jnp.concatenate([q_ref[:, hs] for hs in group_heads], axis=0)
                do_rows = jnp.concatenate([do_ref[:, hs] for hs in group_heads], axis=0)
                dk = _dot(jnp.concatenate(ds_rows, axis=0), q_rows, "tn")
                dv = _dot(jnp.concatenate(p_rows, axis=0), do_rows, "tn")
                vv = slice(kw + g * HEAD_DIM, kw + (g + 1) * HEAD_DIM)
                dkv_ref[geo.s:geo.seg, kv] += dk[:geo.l]
                dkv_ref[geo.s:geo.seg, vv] += dv[:geo.l]
                if with_band:
                    dkv_ref[pl.ds(s0, BAND), kv] += dk[geo.l:]
                    dkv_ref[pl.ds(s0, BAND), vv] += dv[geo.l:]

        pl.when(j < n_x_blocks)(lambda: heads(True))
        pl.when(j >= n_x_blocks)(lambda: heads(False))

        @pl.when(j == n_blocks - 1)
        def _():
            dkv_out_ref[...] = dkv_ref[...].astype(BF16)

    res = pl.pallas_call(
        kern, name=name, grid=(geo.b, n_blocks),
        in_specs=[pl.BlockSpec(memory_space=pltpu.SMEM),
                  pl.BlockSpec((ATTN_BLOCK, qw), lambda b, j: (b * n_blocks + j, 0)),
                  pl.BlockSpec((geo.seg, kw), lambda b, j: (b, qw // kw)),
                  pl.BlockSpec((geo.seg, kw), lambda b, j: (b, qw // kw + 1)),
                  pl.BlockSpec((ATTN_BLOCK, qw), lambda b, j: (b * n_blocks + j, 0))] + carrier.in_specs(),
        out_specs=[pl.BlockSpec((ATTN_BLOCK, qw), lambda b, j: (b * n_blocks + j, 0)),
                   pl.BlockSpec((geo.seg, 2 * kw), lambda b, j: (b, 0)),
                   pl.BlockSpec((N_HEADS, LANES), lambda b, j: (0, 0))] + carrier.out_specs(),
        out_shape=[jax.ShapeDtypeStruct((geo.r, qw), BF16), jax.ShapeDtypeStruct((geo.r, 2 * kw), BF16),
                   jax.ShapeDtypeStruct((N_HEADS, LANES), F32)] + carrier.out_shapes(),
        scratch_shapes=[pltpu.VMEM((geo.seg, 2 * kw), F32)] + carrier.scratch(), input_output_aliases=carrier.aliases(),
        compiler_params=_cparams("arbitrary", "arbitrary"),
    )(sink, qkv, qkv, qkv, do, *carrier.operands())
    (dq, dkv, dsink), extra = carrier.results(res)
    return dq, dkv, dsink, extra


RET_QK_W = RET_HEADS * RET_QK_DIM
K_SCALE = RET_QK_DIM ** -0.5


RET_ROW_TILE = 384


def _ret_tile(geo):
    return RET_ROW_TILE if geo.seg % RET_ROW_TILE == 0 else 256


def _ret_prep(geo, proj, cos, sin_signed, name):
    def body(i, p, cs, sn):
        cs2, sn2 = jnp.concatenate([cs] * RET_HEADS, axis=1), jnp.concatenate([sn] * RET_HEADS, axis=1)
        q = _rope(p[:, :RET_QK_W], cs2, sn2, RET_QK_DIM // 4)
        k = _rope(p[:, RET_QK_W:2 * RET_QK_W], cs2, sn2, RET_QK_DIM // 4) * K_SCALE
        return jnp.concatenate([q, k, p[:, 2 * RET_QK_W:]], axis=1)

    return _rowwise(name, body, geo, _ret_tile(geo), [(proj, ("rowc", 2 * RET_QK_W + RET_VWIDTH, 0)), (cos, "tab"), (sin_signed, "tab")],
                    [("row", 2 * RET_QK_W + RET_VWIDTH, BF16)])


def _ret_prep_bwd(geo, dq, dk, dv, dgate, cos, sin_signed, name):
    def body(i, dqv, dkv, dvv, dg, cs, sn):
        cs2, sn2 = jnp.concatenate([cs] * RET_HEADS, axis=1), jnp.concatenate([sn] * RET_HEADS, axis=1)
        dkv = dkv * K_SCALE
        dqv = dqv * cs2 + _swap_halves(dqv * sn2, RET_QK_DIM // 4)
        dkv = dkv * cs2 + _swap_halves(dkv * sn2, RET_QK_DIM // 4)
        return jnp.concatenate([dqv, dkv, dvv, dg], axis=1)

    return _rowwise(name, body, geo, _ret_tile(geo),
                    [(dq, "row"), (dk, "row"), (dv, "row"), (dgate, "row"), (cos, "tab"), (sin_signed, "tab")],
                    [("row", 2 * RET_QK_W + 2 * RET_VWIDTH, BF16)])


def _ret_step(state, q, k, v, lg, rev):
    c = RET_CHUNK
    ri = lax.broadcasted_iota(jnp.int32, (c, 1), 0).astype(F32)
    cj = lax.broadcasted_iota(jnp.int32, (1, c), 1).astype(F32)
    if rev:
        dist, q_decay, k_decay = cj - ri, jnp.exp(lg * (c - ri)), jnp.exp(lg * ri)
    else:
        dist, q_decay, k_decay = ri - cj, jnp.exp(lg * (ri + 1.0)), jnp.exp(lg * (c - 1.0 - ri))
    intra = jnp.where(dist >= 0, jnp.exp(lg * jnp.maximum(dist, 0.0)), 0.0)
    scores = _mm(q, k, "nt") * intra
    out = _mm(scores, v, "nn") + _mm(q, state, "nn") * q_decay
    new_state = state * jnp.exp(lg * c) + _mm(k * k_decay, v, "tn")
    return new_state, out


def _ret_state0(kc, vc, lg, rev):
    n = kc.shape[0]
    t = lax.broadcasted_iota(jnp.int32, (n, 1), 0).astype(F32)
    decay = jnp.exp(lg * t) if rev else jnp.exp(lg * (n - 1.0 - t))
    return _mm(kc * decay, vc, "tn")


def _ret_specs(geo):
    nq = RET_HEADS
    return [pl.BlockSpec((2 * RET_HEADS, LANES), lambda b, h: (0, 0)),
            pl.BlockSpec((geo.seg, RET_QK_DIM), lambda b, h: (b, h)),
            pl.BlockSpec((geo.seg, RET_QK_DIM), lambda b, h: (b, nq + h)),
            pl.BlockSpec((geo.seg, RET_V_DIM), lambda b, h: (b, nq + h))]


def _retention(geo, qkv, log_g, name):
    nc = geo.s // RET_CHUNK

    def kern(lg_ref, q_ref, k_ref, v_ref, o_ref, st_ref):
        h = pl.program_id(1)
        for d, rev in ((0, False), (1, True)):
            lg = lg_ref[pl.ds(d * RET_HEADS + h, 1), 0:1]
            st_ref[...] = _ret_state0(k_ref[geo.s:geo.seg, :].astype(F32), v_ref[geo.s:geo.seg, :].astype(F32), lg, rev)

            def chunk(ci, carry, d=d, rev=rev, lg=lg):
                r0 = pl.multiple_of((nc - 1 - ci if rev else ci) * RET_CHUNK, RET_CHUNK)
                rows = pl.ds(r0, RET_CHUNK)
                new_state, out = _ret_step(st_ref[...], q_ref[rows, :], k_ref[rows, :], v_ref[rows, :], lg, rev)
                st_ref[...] = new_state
                if d == 0:
                    o_ref[rows, :] = out
                else:
                    o_ref[rows, :] += out
                return carry

            lax.fori_loop(0, nc, chunk, 0, unroll=2)
        o_ref[geo.s:geo.seg, :] = jnp.zeros((geo.l, RET_V_DIM), F32)

    return pl.pallas_call(
        kern, name=name, grid=(geo.b, RET_HEADS), in_specs=_ret_specs(geo),
        out_specs=pl.BlockSpec((geo.seg, RET_V_DIM), lambda b, h: (b, h)),
        out_shape=jax.ShapeDtypeStruct((geo.r, RET_VWIDTH), F32),
        scratch_shapes=[pltpu.VMEM((RET_QK_DIM, RET_V_DIM), F32)],
        compiler_params=_cparams("parallel", "arbitrary"),
    )(log_g, qkv, qkv, qkv)


def _retention_bwd(geo, qkv, log_g, do, name):
    nc = geo.s // RET_CHUNK
    ctx = slice(geo.s, geo.seg)

    def kern(lg_ref, q_ref, k_ref, v_ref, do_ref, dq_ref, dk_ref, dv_ref, dlg_ref, states_ref, dst_ref, aq_ref, ak_ref, av_ref):
        b, h = pl.program_id(0), pl.program_id(1)

        @pl.when((b == 0) & (h == 0))
        def _():
            dlg_ref[...] = jnp.zeros_like(dlg_ref)

        for d, rev in ((0, False), (1, True)):
            row = pl.ds(d * RET_HEADS + h, 1)
            lg = lg_ref[row, 0:1]
            kc, vc = k_ref[ctx, :].astype(F32), v_ref[ctx, :].astype(F32)
            states_ref[0] = _ret_state0(kc, vc, lg, rev)

            def rows_of(ci, rev=rev):
                return pl.ds(pl.multiple_of((nc - 1 - ci if rev else ci) * RET_CHUNK, RET_CHUNK), RET_CHUNK)

            def load(rows):
                return q_ref[rows, :].astype(F32), k_ref[rows, :].astype(F32), v_ref[rows, :].astype(F32)

            def replay(ci, carry, rev=rev, lg=lg, rows_of=rows_of, load=load):
                states_ref[ci + 1] = _ret_step(states_ref[ci], *load(rows_of(ci)), lg, rev)[0]
                return carry

            lax.fori_loop(0, nc - 1, replay, 0)
            dst_ref[...] = jnp.zeros_like(dst_ref)

            def emit(rows, dq, dk, dv, d=d):
                if d == 0:
                    ak_ref[rows, :], av_ref[rows, :] = dk, dv
                    if dq is not None:
                        aq_ref[rows, :] = dq
                else:
                    dk_ref[rows, :] = (ak_ref[rows, :] + dk).astype(BF16)
                    dv_ref[rows, :] = (av_ref[rows, :] + dv).astype(BF16)
                    if dq is not None:
                        dq_ref[rows, :] = (aq_ref[rows, :] + dq).astype(BF16)

            def back(t, dlg, rev=rev, lg=lg, rows_of=rows_of, load=load, emit=emit):
                ci = nc - 1 - t
                rows = rows_of(ci)
                _, vjp = jax.vjp(lambda st, q, k, v, g: _ret_step(st, q, k, v, g, rev), states_ref[ci], *load(rows), lg)
                dstate, dq, dk, dv, dg = vjp((dst_ref[...], do_ref[rows, :].astype(F32)))
                dst_ref[...] = dstate
                emit(rows, dq, dk, dv)
                return dlg + dg

            dlg = lax.fori_loop(0, nc, back, jnp.zeros((1, 1), F32), unroll=2)
            _, vjp = jax.vjp(lambda kk, vv, g: _ret_state0(kk, vv, g, rev), kc, vc, lg)
            dkc, dvc, dg = vjp(dst_ref[...])
            emit(ctx, None, dkc, dvc)
            dlg_ref[row, :] += jnp.broadcast_to(dlg + dg, (1, LANES))
        dq_ref[ctx, :] = jnp.zeros((geo.l, RET_QK_DIM), BF16)

    nq = RET_HEADS
    return pl.pallas_call(
        kern, name=name, grid=(geo.b, RET_HEADS),
        in_specs=_ret_specs(geo) + [pl.BlockSpec((geo.seg, RET_V_DIM), lambda b, h: (b, h))],
        out_specs=[pl.BlockSpec((geo.seg, RET_QK_DIM), lambda b, h: (b, h)),
                   pl.BlockSpec((geo.seg, RET_QK_DIM), lambda b, h: (b, h)),
                   pl.BlockSpec((geo.seg, RET_V_DIM), lambda b, h: (b, h)),
                   pl.BlockSpec((2 * RET_HEADS, LANES), lambda b, h: (0, 0))],
        out_shape=[jax.ShapeDtypeStruct((geo.r, RET_QK_W), BF16), jax.ShapeDtypeStruct((geo.r, RET_QK_W), BF16),
                   jax.ShapeDtypeStruct((geo.r, RET_VWIDTH), BF16), jax.ShapeDtypeStruct((2 * RET_HEADS, LANES), F32)],
        scratch_shapes=[pltpu.VMEM((nc, RET_QK_DIM, RET_V_DIM), F32), pltpu.VMEM((RET_QK_DIM, RET_V_DIM), F32),
                        pltpu.VMEM((geo.seg, RET_QK_DIM), F32), pltpu.VMEM((geo.seg, RET_QK_DIM), F32),
                        pltpu.VMEM((geo.seg, RET_V_DIM), F32)],
        compiler_params=_cparams("arbitrary", "arbitrary"),
    )(log_g, qkv, qkv, qkv, do)


def _gated(o, g, gain):
    outs = []
    for h in range(RET_HEADS):
        cols = slice(h * RET_V_DIM, (h + 1) * RET_V_DIM)
        oh = o[:, cols]
        mu = jnp.mean(oh, axis=-1, keepdims=True)
        var = jnp.mean(jnp.square(oh - mu), axis=-1, keepdims=True)
        outs.append(_silu(g[:, cols]) * ((oh - mu) * lax.rsqrt(var + EPS) * gain[:, cols]))
    return jnp.concatenate(outs, axis=1)


def _ret_gated(geo, o, proj, gain, name):
    def body(i, ov, gv, gn):
        return _gated(ov, gv, gn)

    gate_block = (2 * RET_QK_W + RET_VWIDTH) // RET_VWIDTH
    return _rowwise(name, body, geo, _ret_tile(geo), [(o, "row"), (proj, ("rowc", RET_VWIDTH, gate_block)), (gain, "full")],
                    [("row", RET_VWIDTH, BF16)])


def _ret_gated_bwd(geo, o, proj, gain, dout, name):
    def body(i, ov, gv, gn, dv):
        _, vjp = jax.vjp(_gated, ov, gv, gn)
        return vjp(dv)

    gate_block = (2 * RET_QK_W + RET_VWIDTH) // RET_VWIDTH
    return _rowwise(name, body, geo, 256,
                    [(o, "row"), (proj, ("rowc", RET_VWIDTH, gate_block)), (gain, "full"), (dout, "row")],
                    [("row", RET_VWIDTH, BF16), ("row", RET_VWIDTH, BF16), ("gacc", 1, RET_VWIDTH)])


def _whole(name, fn, out_shapes, *arrays):
    n = len(arrays)

    def kern(*refs):
        res = fn(*[r[...] for r in refs[:n]])
        for ref, val in zip(refs[n:], res):
            ref[...] = val.astype(ref.dtype)

    return pl.pallas_call(kern, name=name, out_shape=out_shapes)(*arrays)


def _rope_tables(geo, head_dim):
    rows = geo.s // GRID_W
    row = jnp.broadcast_to(jnp.arange(rows, dtype=jnp.int32)[:, None], (rows, GRID_W)).reshape(geo.s)
    col = jnp.broadcast_to(jnp.arange(GRID_W, dtype=jnp.int32)[None, :], (rows, GRID_W)).reshape(geo.s)
    axis_dim = head_dim // 2
    inv = ROPE_BASE ** (-jnp.arange(0, axis_dim, 2, dtype=F32) / axis_dim)
    ang_r = row.astype(F32)[:, None] * inv
    ang_c = col.astype(F32)[:, None] * inv
    cos = jnp.concatenate([jnp.cos(ang_r)] * 2 + [jnp.cos(ang_c)] * 2, axis=1)
    sin = jnp.concatenate([-jnp.sin(ang_r), jnp.sin(ang_r), -jnp.sin(ang_c), jnp.sin(ang_c)], axis=1)
    cos = jnp.concatenate([cos, jnp.ones((geo.l, head_dim), F32)], axis=0)
    sin = jnp.concatenate([sin, jnp.zeros((geo.l, head_dim), F32)], axis=0)
    reps = max(1, LANES // head_dim)
    return jnp.tile(cos, (1, reps)), jnp.tile(sin, (1, reps))


def _row_tile(r):
    return next(t for t in (1536, 1024, 512, 256, 128) if r % t == 0)


MOD_ROWS = 8


def _local_step(x, ctx, target, sp, wts, mods, plan=None):
    nb, s, d = x.shape
    geo = _Rows(nb, s, ctx.shape[1])
    assert nb + 1 <= MOD_ROWS and d == D_MODEL
    tm = _row_tile(geo.r)
    cos64, sin64 = _rope_tables(geo, HEAD_DIM)
    cos256, sin256 = _rope_tables(geo, RET_QK_DIM)
    q_gain = jnp.tile(sp["q_norm"].reshape(1, HEAD_DIM), (1, LANES // HEAD_DIM))
    k_gain = jnp.tile(sp["k_norm"].reshape(1, HEAD_DIM), (1, LANES // HEAD_DIM))
    sink = sp["sink"].reshape(N_HEADS)
    log_g = jnp.broadcast_to(sp["log_g"].reshape(2 * RET_HEADS, 1), (2 * RET_HEADS, LANES))
    gn_g = sp["gn_g"].reshape(1, RET_VWIDTH)

    saved = []
    z, h1 = _first_norm(geo, x.reshape(nb * s, d), ctx.reshape(nb * geo.l, d), sp["norm1_g"][0][None, :], mods[0], "norm1_0")
    for i in range(2):
        mod3 = mods[i]
        n1, n2 = sp["norm1_g"][i][None, :], sp["norm2_g"][i][None, :]
        if i == 0:
            proj = _mm_nn(h1, wts["attn_qkv"], F32, "attn_qkv", tm, wts["attn_qkv"].shape[1], d)
            prep = _attn_prep(geo, proj, cos64, sin64, q_gain, k_gain, "attn_prep")
            o, late = _attention(geo, prep, sink, "attn", plan.gather_job() if plan else None)
            if plan:
                plan.late_weights(late, wts)
            oraw = None
            w_o = wts["attn_o"]
        else:
            proj = _mm_nn(h1, wts["ret_qkvg"], BF16, "ret_qkvg", tm, wts["ret_qkvg"].shape[2], d)
            prep = _ret_prep(geo, proj, cos256, sin256, "ret_prep")
            oraw = _retention(geo, prep, log_g, "ret")
            o = _ret_gated(geo, oraw, proj, gn_g, "ret_gated")
            w_o = wts["ret_o"]
        zmid, mix, h2 = _mm_nn_gate_residual(geo, o, w_o, z, mod3, 2 * d, f"mix_out{i}", norm=(n2, mod3, 3 * d))
        u, a = _ffn_in_swiglu(h2, wts["ffn_in"][i], f"ffn_in{i}")
        next_norm = (sp["norm1_g"][1][None, :], mods[1], 0) if i == 0 else None
        zout, f, h1_next = _mm_nn_gate_residual(geo, a, wts["ffn_out"][i], zmid, mod3, 5 * d, f"ffn_out{i}", norm=next_norm)
        saved.append(dict(z=z, mod3=mod3, n1=n1, n2=n2, h1=h1, proj=proj, prep=prep, o=o, oraw=oraw, mix=mix, zmid=zmid,
                          h2=h2, u=u, a=a, f=f))
        z, h1 = zout, h1_next

    dz, loss, df, dg2 = _loss_head(geo, z, target.reshape(nb * s, d), saved[1]["f"], saved[1]["mod3"], 5 * d, "loss")

    big, small = {}, {}
    dmods = [None, None]
    for i in (1, 0):
        sv = saved[i]
        mod3 = sv["mod3"]
        carry = plan is not None and i == 0
        du, land = _ffn_out_dx_swiglu_bwd(df, wts["ffn_out"][i], sv["u"], f"ffn_out_dx{i}", plan.layer1.swap_job() if carry else None)
        if carry:
            plan.layer1.after_swap(land)
        big[f"ffn_out{i}"] = _mm_tn(sv["a"], df, f"ffn_out_dw{i}", D_FF // 2, 1024, tm, out_dtype=BF16).reshape(N_CHIPS, D_FF // N_CHIPS, d)
        n4 = wts["ffn_in"][i].shape[2]
        dh2 = _mm_nt(du, wts["ffn_in"][i], BF16, f"ffn_in_dx{i}", tm, 1024, n4)
        big[f"ffn_in{i}"] = _mm_tn(sv["h2"], du, f"ffn_in_dw{i}", 1024, n4, tm, shards=N_CHIPS, out_dtype=BF16)
        if carry:
            plan.start_layer0_ffn(big)
        dzmid, dsh2, dsc2, dn2, dmix, dg1, *land = _norm_mod_bwd(geo, sv["zmid"], sv["n2"], mod3, 3 * d, dh2, dz, f"norm2_bwd{i}",
                                                                 gated=(sv["mix"], mod3, 2 * d),
                                                                 job=plan.layer0_ffn.swap_job() if carry else None)
        if carry:
            plan.layer0_ffn.after_swap(land[0])
        if i == 0:
            do = _mm_nt(dmix, wts["attn_o"], BF16, "attn_out_dx", tm, 1024, 1024)
            big["attn_o"] = _mm_tn(sv["o"], dmix, "attn_out_dw", 1024, 1024, tm, out_dtype=BF16).reshape(N_CHIPS, 1024 // N_CHIPS, d)
            dq, dkv, dsink, land = _attention_bwd(geo, sv["prep"], sink, do, "attn_bwd", plan.exchange_job() if plan else None)
            if plan:
                plan.after_exchange(land)
            dproj, dqg, dkg = _attn_prep_bwd(geo, sv["proj"], cos64, sin64, q_gain, k_gain, dq, dkv, "attn_prep_bwd")
            small["q_norm"] = dqg[0, :HEAD_DIM] + dqg[0, HEAD_DIM:]
            small["k_norm"] = dkg[0, :HEAD_DIM] + dkg[0, HEAD_DIM:]
            small["sink"] = dsink[:, 0]
            wq = wts["attn_qkv"]
            dh1 = _mm_nt(dproj, wq, BF16, "attn_qkv_dx", tm, 1024, wq.shape[1])
            dwq = _mm_tn(sv["h1"], dproj, "attn_qkv_dw", 1024, wq.shape[1], tm, out_dtype=BF16)
            big["attn_qkv"] = dwq.reshape(d, N_CHIPS, -1).transpose(1, 0, 2)
        else:
            do = _mm_nt(dmix, wts["ret_o"], BF16, "ret_out_dx", tm, 1024, 1024)
            big["ret_o"] = _mm_tn(sv["o"], dmix, "ret_out_dw", 1024, 1024, tm, out_dtype=BF16).reshape(N_CHIPS, RET_VWIDTH // N_CHIPS, d)
            doraw, dgate, dgn = _ret_gated_bwd(geo, sv["oraw"], sv["proj"], gn_g, do, "ret_gated_bwd")
            small["gn_g"] = dgn[0]
            dq, dk, dv, dlg = _retention_bwd(geo, sv["prep"], log_g, doraw, "ret_bwd")
            small["log_g"] = dlg[:, 0].reshape(2, RET_HEADS)
            dproj = _ret_prep_bwd(geo, dq, dk, dv, dgate, cos256, sin256, "ret_prep_bwd")
            wq = wts["ret_qkvg"]
            dh1 = _mm_nt(dproj, wq, BF16, "ret_qkvg_dx", tm, 1024, wq.shape[2])
            big["ret_qkvg"] = _mm_tn(sv["h1"], dproj, "ret_qkvg_dw", 1024, wq.shape[2], tm, shards=N_CHIPS, out_dtype=BF16)
        below = (saved[0]["f"], saved[0]["mod3"], 5 * d) if i == 1 else None
        dz, dsh1, dsc1, dn1, *below_grads = _norm_mod_bwd(geo, sv["z"], sv["n1"], mod3, 0, dh1, dzmid, f"norm1_bwd{i}", gated=below,
                                                              latent_only=i == 0)
        small[f"norm1_g{i}"], small[f"norm2_g{i}"] = dn1[0], dn2[0]
        parts = [dsh1, dsc1, dg1, dsh2, dsc2, dg2]
        rows = jnp.concatenate([jnp.concatenate([p[:nb, 0, :] for p in parts], axis=1),
                                jnp.concatenate([jnp.sum(p[nb:, 0, :], axis=0, keepdims=True) for p in parts], axis=1),
                                jnp.zeros((MOD_ROWS - nb - 1, 6 * d), F32)], axis=0)
        dmods[i] = rows
        if below_grads:
            df, dg2 = below_grads
        small[f"ada_b{i}"] = jnp.sum(rows, axis=0)
        if plan and i == 1:
            plan.start_layer1(big)
    return loss, dz, big, small, dmods


def _adamw(w, g, m, v, name):
    rows, cols = w.shape
    tr = next((t for t in (512, 256, 128, 64, 32, 16, 8) if rows % t == 0), rows)
    c1 = 1.0 - ADAM_B1 ** ADAM_STEP
    c2 = 1.0 - ADAM_B2 ** ADAM_STEP

    def kern(w_ref, g_ref, m_ref, v_ref, d_ref, nm_ref, nv_ref):
        gv = g_ref[...]
        nm = ADAM_B1 * m_ref[...] + (1.0 - ADAM_B1) * gv
        nv = ADAM_B2 * v_ref[...] + (1.0 - ADAM_B2) * jnp.square(gv)
        d_ref[...] = -ADAM_LR * ((nm / c1) / (jnp.sqrt(nv / c2) + ADAM_EPS) + ADAM_WD * w_ref[...])
        nm_ref[...] = nm
        nv_ref[...] = nv

    spec = pl.BlockSpec((tr, cols), lambda i: (i, 0))
    return pl.pallas_call(
        kern, name=name, grid=(rows // tr,), in_specs=[spec] * 4, out_specs=[spec] * 3,
        out_shape=[jax.ShapeDtypeStruct(w.shape, F32)] * 3, compiler_params=_cparams("parallel"),
    )(w, g, m, v)


N_DEVICES = 8


def _mesh_pos():
    return lax.axis_index("x"), lax.axis_index("y"), lax.axis_index("c")


def _other_chips(x, y):
    return [(1 - x, y), (x, 1 - y), (1 - x, 1 - y)]


def _hbm(n):
    return [pl.BlockSpec(memory_space=pl.ANY)] * n


def _remote(src, dst, send_sem, recv_sem, device):
    return pltpu.make_async_remote_copy(src_ref=src, dst_ref=dst, send_sem=send_sem, recv_sem=recv_sem,
                                        device_id=device, device_id_type=MESH)


def _scalar_spec(grid, in_specs, out_specs):
    return pltpu.PrefetchScalarGridSpec(num_scalar_prefetch=1, grid=grid, in_specs=in_specs, out_specs=out_specs)


def _place_shard(param, layer, pos, name):
    _, r, cols = param.shape
    tr = _slab_tile(r)

    def kern(pos_ref, s_ref, o_ref):
        o_ref[...] = s_ref[...].astype(BF16)

    return pl.pallas_call(
        kern, name=name, out_shape=jax.ShapeDtypeStruct((N_CHIPS, r, cols), BF16),
        grid_spec=_scalar_spec((r // tr,), [pl.BlockSpec((None, tr, cols), lambda i, p: (layer, i, 0))],
                               pl.BlockSpec((None, tr, cols), lambda i, p: (p[1], i, 0))),
        compiler_params=_cparams("parallel"),
    )(pos, param)


class _CommJob:
    def __init__(self, inputs, out_shapes, aliases, sem_shapes, stages, fractions=None):
        self.inputs, self.out_shapes, self.aliases, self.sem_shapes, self.stages = inputs, out_shapes, aliases, sem_shapes, stages
        self.fractions = fractions


def _merge_jobs(a, b):
    assert len(a.stages) == len(b.stages)
    ni, no, ns = len(a.inputs), len(a.out_shapes), len(a.sem_shapes)

    def both(sa, sb):
        def stage(ins, outs, sems):
            sa(ins[:ni], outs[:no], sems[:ns])
            sb(ins[ni:], outs[no:], sems[ns:])
        return stage

    aliases = dict(a.aliases)
    aliases.update({ni + i: no + o for i, o in b.aliases.items()})
    return _CommJob(a.inputs + b.inputs, a.out_shapes + b.out_shapes, aliases, a.sem_shapes + b.sem_shapes,
                    [both(sa, sb) for sa, sb in zip(a.stages, b.stages)])


def _run_job(job, name):
    n_in, n_out = len(job.inputs), len(job.out_shapes)

    def body(*refs):
        for stage in job.stages:
            stage(refs[:n_in], refs[n_in:n_in + n_out], refs[n_in + n_out:])

    return pl.pallas_call(
        body, name=name, in_specs=_hbm(n_in), out_specs=_hbm(n_out), out_shape=job.out_shapes,
        input_output_aliases=job.aliases, scratch_shapes=job.sem_shapes,
    )(*job.inputs)


def _job_marks(job, steps):
    mid = len(job.stages) - 2
    fractions = job.fractions or [(s + 1) / (mid + 1) for s in range(mid)]
    return [0] + [min(steps - 1, 1 + int((steps - 1) * f)) for f in fractions] + [steps - 1]


def _gather_job(placed):
    n = len(placed)

    def half(w, which):
        r2 = placed[w].shape[1] // 2
        return pl.ds(which * r2, r2)

    def ici_copies(outs, sems, slot_of, arrays=range(n)):
        x, y, c = _mesh_pos()
        res = []
        for w in arrays:
            for k, (px, py) in enumerate(_other_chips(x, y)):
                slab = outs[w].at[slot_of(x, y, px, py), half(w, c)]
                res.append((slab, _remote(slab, slab, sems[0].at[w, k], sems[1].at[w, k], (px, py, c))))
        return res

    def forwards(outs, sems, which_core, arrays=range(n)):
        x, y, c = _mesh_pos()
        res = []
        for w in arrays:
            for k, (px, py) in enumerate(_other_chips(x, y)):
                slab = outs[w].at[2 * px + py, half(w, which_core(c))]
                res.append(_remote(slab, slab, sems[2].at[w, k], sems[3].at[w, k], (x, y, 1 - c)))
        return res

    def send(ins, outs, sems):
        for _, cp in ici_copies(outs, sems, lambda x, y, px, py: 2 * x + y):
            cp.start()

    def forward_of(w):
        def forward(ins, outs, sems):
            arrivals = ici_copies(outs, sems, lambda x, y, px, py: 2 * px + py, [w])
            for (_, arrival), fwd in zip(arrivals, forwards(outs, sems, lambda c: c, [w])):
                arrival.wait_recv()
                fwd.start()
        return forward

    def finish(ins, outs, sems):
        for cp in forwards(outs, sems, lambda c: 1 - c):
            cp.wait_recv()
        for _, cp in ici_copies(outs, sems, lambda x, y, px, py: 2 * x + y):
            cp.wait_send()
        for cp in forwards(outs, sems, lambda c: c):
            cp.wait_send()

    sizes = [p.shape[1] * p.shape[2] for p in placed]
    fractions = [sum(sizes[:w + 1]) / sum(sizes) for w in range(n)]
    return _CommJob(list(placed), [jax.ShapeDtypeStruct(p.shape, p.dtype) for p in placed], {w: w for w in range(n)},
                    [pltpu.SemaphoreType.DMA((n, 3))] * 4, [send] + [forward_of(w) for w in range(n)] + [finish], fractions)


def _pair_swap_job(grads):
    n = len(grads)

    def copies(ins, outs, sems):
        x, y, c = _mesh_pos()
        res = []
        for w in range(n):
            r2 = grads[w].shape[1] // 2
            res.append(_remote(ins[w].at[:, pl.ds((1 - c) * r2, r2)], outs[w], sems[0].at[w], sems[1].at[w], (x, y, 1 - c)))
        return res

    def send(ins, outs, sems):
        for cp in copies(ins, outs, sems):
            cp.start()

    def finish(ins, outs, sems):
        for cp in copies(ins, outs, sems):
            cp.wait()

    return _CommJob(list(grads), [jax.ShapeDtypeStruct((N_CHIPS, g.shape[1] // 2, g.shape[2]), g.dtype) for g in grads], {},
                    [pltpu.SemaphoreType.DMA((n,))] * 2, [send, finish])


def _chip_exchange_job(hs):
    n = len(hs)

    def send(ins, outs, sems):
        x, y, c = _mesh_pos()
        for w in range(n):
            for k, (px, py) in enumerate(_other_chips(x, y)):
                _remote(ins[w].at[2 * px + py], outs[w].at[2 * x + y], sems[0].at[w, k], sems[1].at[w, k], (px, py, c)).start()

    def finish(ins, outs, sems):
        x, y, c = _mesh_pos()
        for w in range(n):
            for k, (px, py) in enumerate(_other_chips(x, y)):
                got = outs[w].at[2 * px + py]
                cp = _remote(ins[w].at[2 * px + py], got, sems[0].at[w, k], sems[1].at[w, k], (px, py, c))
                cp.wait_recv()
                cp.wait_send()

    return _CommJob(list(hs), [jax.ShapeDtypeStruct(h.shape, h.dtype) for h in hs], {},
                    [pltpu.SemaphoreType.DMA((n, 3))] * 2, [send, finish])


def _pair_share(ts, name):
    n = len(ts)

    def body(*refs):
        outs = refs[n:2 * n]
        send_sems, recv_sems = refs[2 * n:]
        x, y, c = _mesh_pos()
        sends = []
        for w in range(n):
            r2 = ts[w].shape[0] // 2
            mine = outs[w].at[pl.ds(c * r2, r2)]
            rc = _remote(mine, mine, send_sems.at[w], recv_sems.at[w], (x, y, 1 - c))
            rc.start()
            sends.append(rc)
        for w in range(n):
            r2 = ts[w].shape[0] // 2
            theirs = outs[w].at[pl.ds((1 - c) * r2, r2)]
            _remote(theirs, theirs, send_sems.at[w], recv_sems.at[w], (x, y, 1 - c)).wait_recv()
            sends[w].wait_send()

    return pl.pallas_call(
        body, name=name, in_specs=_hbm(n), out_specs=_hbm(n),
        out_shape=[jax.ShapeDtypeStruct(t.shape, F32) for t in ts],
        input_output_aliases={w: w for w in range(n)},
        scratch_shapes=[pltpu.SemaphoreType.DMA((n,))] * 2,
    )(*ts)


def _slab_tile(rows):
    return next(t for t in (512, 256, 176, 128, 64, 32, 16) if rows % t == 0)


def _sum_pair(grad, land, pos, name):
    _, r2, cols = land.shape
    tr = _slab_tile(r2)
    nt = r2 // tr

    def kern(pos_ref, a_ref, b_ref, o_ref):
        o_ref[...] = (a_ref[...].astype(F32) + b_ref[...].astype(F32)).astype(BF16)

    spec = pl.BlockSpec((None, tr, cols), lambda j, i, p: (j, i, 0))
    return pl.pallas_call(
        kern, name=name, out_shape=jax.ShapeDtypeStruct(land.shape, BF16),
        grid_spec=_scalar_spec((N_CHIPS, nt), [pl.BlockSpec((None, tr, cols), lambda j, i, p: (j, p[0] * nt + i, 0)), spec], spec),
        compiler_params=_cparams("parallel", "parallel"),
    )(pos, grad, land)


def _sum_chips(hs, land, pos, name):
    _, r2, cols = land.shape
    tr = _slab_tile(r2)
    nt = r2 // tr

    def kern(pos_ref, h_ref, l_ref, o_ref):
        acc = jnp.zeros((tr, cols), F32)
        own = h_ref[...].astype(F32)
        for k in range(N_CHIPS):
            acc = acc + jnp.where(pos_ref[1] == k, own, l_ref[k].astype(F32))
        o_ref[...] = acc

    return pl.pallas_call(
        kern, name=name, out_shape=jax.ShapeDtypeStruct((2 * r2, cols), F32),
        grid_spec=_scalar_spec((nt,), [pl.BlockSpec((None, tr, cols), lambda i, p: (p[1], i, 0)),
                                       pl.BlockSpec((N_CHIPS, tr, cols), lambda i, p: (0, i, 0))],
                               pl.BlockSpec((tr, cols), lambda i, p: (p[0] * nt + i, 0))),
        compiler_params=_cparams("parallel"),
    )(pos, hs, land)


class _ReduceScatter:
    def __init__(self, grads, pos, tag):
        self.grads, self.pos, self.tag = list(grads), pos, tag

    def swap_job(self):
        return _pair_swap_job(self.grads)

    def after_swap(self, land):
        self.hs = [_sum_pair(g, l, self.pos, f"grads_pair_sum_{self.tag}{w}") for w, (g, l) in enumerate(zip(self.grads, land))]

    def exchange_job(self):
        return _chip_exchange_job(self.hs)

    def after_exchange(self, land2):
        return [_sum_chips(h, l, self.pos, f"grads_chip_sum_{self.tag}{w}") for w, (h, l) in enumerate(zip(self.hs, land2))]

    def run(self):
        self.after_swap(_run_job(self.swap_job(), f"grads_pair_swap_{self.tag}"))
        return self.after_exchange(_run_job(self.exchange_job(), f"grads_chip_exchange_{self.tag}"))


EARLY_WEIGHTS = ("attn_qkv",)
LATE_WEIGHTS = ("ffn_in0", "ffn_in1", "ffn_out0", "ffn_out1", "attn_o", "ret_qkvg", "ret_o")
LAYER1_GRADS = ("ffn_out1", "ffn_in1", "ret_o", "ret_qkvg")
LAYER0_FFN_GRADS = ("ffn_out0", "ffn_in0")
LAST_GRADS = ("attn_o", "attn_qkv")


def _fill_weights(wts, full):
    for name, w in full.items():
        if name[:-1] == "ffn_in":
            wts[name[:-1]][int(name[-1])] = w
        elif name[:-1] == "ffn_out":
            wts["ffn_out"][int(name[-1])] = w.reshape(-1, w.shape[2])
        elif name in ("attn_o", "ret_o"):
            wts[name] = w.reshape(-1, w.shape[2])
        elif name == "attn_qkv":
            wts[name] = w.transpose(1, 0, 2).reshape(w.shape[1], -1)
        else:
            wts[name] = w


class _StepPlan:
    def __init__(self, placed, pos):
        self.placed, self.pos = placed, pos
        self.layer1 = self.layer0_ffn = None
        self.reduced = {}

    def gather_job(self):
        return _gather_job([self.placed[k] for k in LATE_WEIGHTS])

    def late_weights(self, outs, wts):
        _fill_weights(wts, dict(zip(LATE_WEIGHTS, outs)))

    def start_layer1(self, big):
        self.layer1 = _ReduceScatter([big[k] for k in LAYER1_GRADS], self.pos, "l1_")

    def start_layer0_ffn(self, big):
        self.layer0_ffn = _ReduceScatter([big[k] for k in LAYER0_FFN_GRADS], self.pos, "l0f_")

    def exchange_job(self):
        return _merge_jobs(self.layer1.exchange_job(), self.layer0_ffn.exchange_job())

    def after_exchange(self, land):
        n1 = len(LAYER1_GRADS)
        self.reduced.update(zip(LAYER1_GRADS, self.layer1.after_exchange(land[:n1])))
        self.reduced.update(zip(LAYER0_FFN_GRADS, self.layer0_ffn.after_exchange(land[n1:])))


def _all_reduce_small(v, name):
    def body(v_ref, o_ref, land_ref, send_sems, recv_sems):
        x, y, c = _mesh_pos()
        me = 4 * x + 2 * y + c
        land_ref[me] = v_ref[...]
        for t in range(N_DEVICES):
            @pl.when(t != me)
            def _(t=t):
                _remote(v_ref, land_ref.at[me], send_sems.at[t], recv_sems.at[me], (t // 4, (t // 2) % 2, t % 2)).start()
        for t in range(N_DEVICES):
            @pl.when(t != me)
            def _(t=t):
                _remote(v_ref, land_ref.at[t], send_sems.at[t], recv_sems.at[t], (t // 4, (t // 2) % 2, t % 2)).wait()
        acc = land_ref[0]
        for t in range(1, N_DEVICES):
            acc = acc + land_ref[t]
        o_ref[...] = acc

    vmem = pl.BlockSpec(memory_space=pltpu.VMEM)
    return pl.pallas_call(
        body, name=name, in_specs=[vmem], out_specs=vmem, out_shape=jax.ShapeDtypeStruct(v.shape, F32),
        scratch_shapes=[pltpu.VMEM((N_DEVICES,) + v.shape, F32), pltpu.SemaphoreType.DMA((N_DEVICES,)),
                        pltpu.SemaphoreType.DMA((N_DEVICES,))],
    )(v)


def _all_to_all_small(v, name):
    def body(v_ref, o_ref, send_sems, recv_sems):
        x, y, c = _mesh_pos()
        me = 4 * x + 2 * y + c
        o_ref[me] = v_ref[me]
        for t in range(N_DEVICES):
            @pl.when(t != me)
            def _(t=t):
                _remote(v_ref.at[t], o_ref.at[me], send_sems.at[t], recv_sems.at[me], (t // 4, (t // 2) % 2, t % 2)).start()
        for t in range(N_DEVICES):
            @pl.when(t != me)
            def _(t=t):
                _remote(v_ref.at[t], o_ref.at[t], send_sems.at[t], recv_sems.at[t], (t // 4, (t // 2) % 2, t % 2)).wait()

    vmem = pl.BlockSpec(memory_space=pltpu.VMEM)
    return pl.pallas_call(
        body, name=name, in_specs=[vmem], out_specs=vmem, out_shape=jax.ShapeDtypeStruct(v.shape, F32),
        scratch_shapes=[pltpu.SemaphoreType.DMA((N_DEVICES,)), pltpu.SemaphoreType.DMA((N_DEVICES,))],
    )(v)


ALL_ROWS = 40


class _AdaLN:
    def __init__(self, c, c_ctx, ada_w, ada_b, riders):
        xi, yi, ci = _mesh_pos()
        self.me, self.chip, self.core = 4 * xi + 2 * yi + ci, 2 * xi + yi, ci
        self.nb, d = c.shape
        self.ada_w, self.c_ctx = ada_w, c_ctx
        self.cols = ada_w.shape[2]
        ctx_row = self.nb * N_DEVICES
        assert ctx_row + 1 + riders.shape[0] <= ALL_ROWS
        placed = lax.dynamic_update_slice(jnp.zeros((ALL_ROWS, d), F32), c, (self.me * self.nb, 0))
        placed = lax.dynamic_update_slice(placed, riders, (ctx_row + 1, 0))
        summed = _all_reduce_small(placed, "gather_conditioning")
        self.riders = summed[ctx_row + 1:ctx_row + 1 + riders.shape[0]]
        c_all = summed.at[ctx_row].set(c_ctx)
        self.cact, = _whole("cond_silu", lambda v: (_silu(v),), [jax.ShapeDtypeStruct(c_all.shape, F32)], c_all)
        parts = []
        for i in range(2):
            bias = lax.dynamic_slice(ada_b[i], (self.chip * self.cols,), (self.cols,))[None, :]
            parts.append(_mm_nn(self.cact, ada_w[i], F32, f"mod{i}", ALL_ROWS, self.cols, d, bias=bias))
        part = jnp.concatenate(parts, axis=1)
        rows = [[t * self.nb + b for b in range(self.nb)] + [ctx_row] * (MOD_ROWS - self.nb) for t in range(N_DEVICES)]
        got = _all_to_all_small(part[jnp.asarray(rows)], "mod_exchange")
        self.mods = [jnp.concatenate([got[2 * j][:self.nb + 1, i * self.cols:(i + 1) * self.cols] for j in range(N_CHIPS)], axis=1)[:, None, :]
                     for i in range(2)]

    def backward(self, dmods):
        nb, cols, d = self.nb, self.cols, self.ada_w.shape[1]
        blocks = [jnp.concatenate([dm[:, j * cols:(j + 1) * cols] for dm in dmods], axis=1) for j in range(N_CHIPS)]
        got = _all_to_all_small(jnp.stack([blocks[t // 2] for t in range(N_DEVICES)]), "dmod_exchange")
        dall = jnp.concatenate([got[:, :nb].reshape(N_DEVICES * nb, 2 * cols), jnp.sum(got[:, nb], axis=0, keepdims=True),
                                jnp.zeros((ALL_ROWS - N_DEVICES * nb - 1, 2 * cols), F32)], axis=0)
        dctx = jnp.concatenate([dall[N_DEVICES * nb][None, :], jnp.zeros((MOD_ROWS - 1, 2 * cols), F32)], axis=0)
        grads, dcact = [], []
        for i in range(2):
            grads.append(_mm_tn(self.cact, dall[:, i * cols:(i + 1) * cols], f"ada_dw{i}", d, cols, ALL_ROWS))
            dcact.append(_mm_nt(dctx[:, i * cols:(i + 1) * cols], self.ada_w[i], F32, f"ada_dx{i}", MOD_ROWS, d, cols))

        def silu_bwd(v, d0, d1):
            sg = _sigmoid(v)
            return ((d0 + d1)[0:1] * (sg * (1.0 + v * (1.0 - sg))),)

        dc_ctx, = _whole("cond_silu_bwd", silu_bwd, [jax.ShapeDtypeStruct((1, d), F32)], self.c_ctx[None, :], dcact[0], dcact[1])
        return grads, jnp.where(self.core == 0, dc_ctx[0], jnp.zeros((d,), F32))


SMALL_ROWS = 24


def _pack_small(small, dlogit):
    d = D_MODEL
    misc = jnp.zeros((d,), F32)
    misc = misc.at[0:HEAD_DIM].set(small["q_norm"]).at[128:128 + HEAD_DIM].set(small["k_norm"])
    misc = misc.at[256:256 + N_HEADS].set(small["sink"]).at[384:384 + 2 * RET_HEADS].set(dlogit.reshape(-1))
    rows = [small["ada_b0"].reshape(6, d), small["ada_b1"].reshape(6, d), small["norm1_g0"][None], small["norm1_g1"][None],
            small["norm2_g0"][None], small["norm2_g1"][None], small["c_ctx"][None], small["gn_g"].reshape(2, d), misc[None]]
    buf = jnp.concatenate(rows, axis=0)
    return jnp.concatenate([buf, jnp.zeros((SMALL_ROWS - buf.shape[0], d), F32)], axis=0)


def _unpack_small(buf):
    d = D_MODEL
    misc = buf[19]
    return dict(ada_b=buf[0:12].reshape(2, 6 * d), norm1_g=buf[12:14], norm2_g=buf[14:16], c_ctx=buf[16],
                gn_g=buf[17:19].reshape(2 * d), q_norm=misc[0:HEAD_DIM], k_norm=misc[128:128 + HEAD_DIM],
                sink=misc[256:256 + N_HEADS], decay=misc[384:384 + 2 * RET_HEADS])


def kernel(x, c, ctx, c_ctx, ada_w, ada_b, norm1_g, norm2_g, ffn_w_in, ffn_w_out, attn_w_qkv, attn_q_norm, attn_k_norm, attn_sink, attn_w_o, ret_w_qkvg, ret_decay_logit, ret_gn_g, ret_w_o, loss_target, m_c_ctx, m_ada_w, m_ada_b, m_norm1_g, m_norm2_g, m_ffn_w_in, m_ffn_w_out, m_attn_w_qkv, m_attn_q_norm, m_attn_k_norm, m_attn_sink, m_attn_w_o, m_ret_w_qkvg, m_ret_decay_logit, m_ret_gn_g, m_ret_w_o, v_c_ctx, v_ada_w, v_ada_b, v_norm1_g, v_norm2_g, v_ffn_w_in, v_ffn_w_out, v_attn_w_qkv, v_attn_q_norm, v_attn_k_norm, v_attn_sink, v_attn_w_o, v_ret_w_qkvg, v_ret_decay_logit, v_ret_gn_g, v_ret_w_o):
    xi, yi, ci = _mesh_pos()
    chip = 2 * xi + yi
    nb, s, d = x.shape
    gn_shard = ret_gn_g.shape[1]

    shards = dict(ffn_in0=(ffn_w_in, 0), ffn_in1=(ffn_w_in, 1), ffn_out0=(ffn_w_out, 0),
                  ffn_out1=(ffn_w_out, 1), attn_qkv=(attn_w_qkv, 0), attn_o=(attn_w_o, 0), ret_qkvg=(ret_w_qkvg, 0), ret_o=(ret_w_o, 0))
    names = list(shards)
    pos = jnp.stack([ci, chip]).astype(jnp.int32)
    placed = {k: _place_shard(*shards[k], pos, f"place_{k}") for k in names}
    early = _run_job(_gather_job([placed[k] for k in EARLY_WEIGHTS]), "gather_early_weights")
    gn_mine = jnp.where(ci == 0, ret_gn_g[0], jnp.zeros_like(ret_gn_g[0]))
    gn_place = lax.dynamic_update_slice(jnp.zeros((RET_VWIDTH,), F32), gn_mine, (chip * gn_shard,))

    wts = dict(ffn_in=[None, None], ffn_out=[None, None], attn_qkv=None, attn_o=None, ret_qkvg=None, ret_o=None)
    ada = _AdaLN(c, c_ctx, ada_w, ada_b, riders=gn_place.reshape(2, d))
    gn_full = ada.riders.reshape(RET_VWIDTH)
    _fill_weights(wts, dict(zip(EARLY_WEIGHTS, early)))
    plan = _StepPlan(placed, pos)
    decay_logit = ret_decay_logit[0]
    sp = dict(norm1_g=norm1_g, norm2_g=norm2_g, q_norm=attn_q_norm[0], k_norm=attn_k_norm[0],
              sink=attn_sink[0], log_g=jax.nn.log_sigmoid(decay_logit), gn_g=gn_full)
    loss_part, dz, big, small, dmods = _local_step(x, ctx, loss_target, sp, wts, ada.mods, plan)
    ada_grads, small["c_ctx"] = ada.backward(dmods)

    loss = lax.psum(loss_part[0, 0], ("x", "y", "c"))
    grad_x = dz.reshape(nb, s, d)

    dlogit = small["log_g"] * jax.nn.sigmoid(-decay_logit)
    sg = _unpack_small(_all_reduce_small(_pack_small(small, dlogit), "reduce_small_grads"))
    halves = dict(plan.reduced)
    halves.update(zip(LAST_GRADS, _ReduceScatter([big[k] for k in LAST_GRADS], pos, "last_").run()))
    reduced = dict(zip(halves, _pair_share(list(halves.values()), "grads_pair_share")))

    grads = dict(
        c_ctx=sg["c_ctx"], ada_w=jnp.stack(ada_grads), ada_b=sg["ada_b"], norm1_g=sg["norm1_g"],
        norm2_g=sg["norm2_g"], ffn_w_in=jnp.stack([reduced["ffn_in0"], reduced["ffn_in1"]]),
        ffn_w_out=jnp.stack([reduced["ffn_out0"], reduced["ffn_out1"]]), attn_w_qkv=reduced["attn_qkv"][None],
        attn_q_norm=sg["q_norm"][None], attn_k_norm=sg["k_norm"][None], attn_sink=sg["sink"][None],
        attn_w_o=reduced["attn_o"][None], ret_w_qkvg=reduced["ret_qkvg"][None], ret_decay_logit=sg["decay"].reshape(1, 2, RET_HEADS),
        ret_gn_g=lax.dynamic_slice(sg["gn_g"], (chip * gn_shard,), (gn_shard,))[None], ret_w_o=reduced["ret_o"][None])
    params = dict(c_ctx=(c_ctx, m_c_ctx, v_c_ctx), ada_w=(ada_w, m_ada_w, v_ada_w), ada_b=(ada_b, m_ada_b, v_ada_b),
                  norm1_g=(norm1_g, m_norm1_g, v_norm1_g), norm2_g=(norm2_g, m_norm2_g, v_norm2_g),
                  ffn_w_in=(ffn_w_in, m_ffn_w_in, v_ffn_w_in), ffn_w_out=(ffn_w_out, m_ffn_w_out, v_ffn_w_out),
                  attn_w_qkv=(attn_w_qkv, m_attn_w_qkv, v_attn_w_qkv), attn_q_norm=(attn_q_norm, m_attn_q_norm, v_attn_q_norm),
                  attn_k_norm=(attn_k_norm, m_attn_k_norm, v_attn_k_norm), attn_sink=(attn_sink, m_attn_sink, v_attn_sink),
                  attn_w_o=(attn_w_o, m_attn_w_o, v_attn_w_o), ret_w_qkvg=(ret_w_qkvg, m_ret_w_qkvg, v_ret_w_qkvg),
                  ret_decay_logit=(ret_decay_logit, m_ret_decay_logit, v_ret_decay_logit),
                  ret_gn_g=(ret_gn_g, m_ret_gn_g, v_ret_gn_g), ret_w_o=(ret_w_o, m_ret_w_o, v_ret_w_o))
    order = list(params)
    deltas, new_m, new_v = [], [], []
    for k in order:
        w, m, v = params[k]
        g = grads[k].reshape(w.shape)
        grads[k] = g
        flat = (-1, w.shape[-1]) if w.ndim > 1 else (1, -1)
        if k == "ret_decay_logit":
            flat = (1, -1)
        dw, nm, nv = _adamw(w.reshape(flat), g.reshape(flat), m.reshape(flat), v.reshape(flat), f"adamw_{k}")
        deltas.append(dw.reshape(w.shape))
        new_m.append(nm.reshape(w.shape))
        new_v.append(nv.reshape(w.shape))
    return (loss, grad_x, *[grads[k] for k in order], *deltas, *new_m, *new_v)
```

```python
import functools

import jax
import jax.numpy as jnp
from jax import lax
from jax.experimental import pallas as pl
from jax.experimental.pallas import tpu as pltpu

F32 = jnp.float32
BF16 = jnp.bfloat16

D_MODEL = 1024
N_HEADS = 16
N_KV_HEADS = 4
HEAD_DIM = 64
WINDOW = 128
ATTN_BLOCK = 128
BAND = ATTN_BLOCK + 2 * WINDOW
RET_HEADS = 4
RET_QK_DIM = 256
RET_V_DIM = 512
RET_VWIDTH = 2048
RET_CHUNK = 128
D_FF = 2816
GRID_W = 64
ROPE_BASE = 10000.0
EPS = 1e-6
NEG_INF = -1e30
LANES = 128

ADAM_LR = 0.001
ADAM_B1 = 0.9
ADAM_B2 = 0.999
ADAM_EPS = 1e-08
ADAM_WD = 0.01
ADAM_STEP = 10

VMEM_LIMIT_BYTES = 56 * 1024 * 1024
MESH = pl.DeviceIdType.MESH
N_CHIPS = 4


def _cparams(*sem):
    return pltpu.CompilerParams(dimension_semantics=sem, vmem_limit_bytes=VMEM_LIMIT_BYTES)


_DIMS = {"nn": ((1,), (0,)), "nt": ((1,), (1,)), "tn": ((0,), (0,))}


def _dot(a, b, form):
    return lax.dot_general(a.astype(BF16), b.astype(BF16), (_DIMS[form], ((), ())), preferred_element_type=F32)


@functools.partial(jax.custom_vjp, nondiff_argnums=(2,))
def _mm(a, b, form):
    return _dot(a, b, form)


def _mm_fwd(a, b, form):
    return _dot(a, b, form), (a, b)


def _mm_bwd(form, res, ct):
    a, b = res
    if form == "nn":
        da, db = _dot(ct, b, "nt"), _dot(a, ct, "tn")
    elif form == "nt":
        da, db = _dot(ct, b, "nn"), _dot(ct, a, "tn")
    else:
        da, db = _dot(b, ct, "nt"), _dot(a, ct, "nn")
    return da.astype(a.dtype), db.astype(b.dtype)


_mm.defvjp(_mm_fwd, _mm_bwd)


def _swap_halves(x, half):
    w = x.shape[-1]
    lane = lax.broadcasted_iota(jnp.int32, x.shape, x.ndim - 1)
    return jnp.where(lane % (2 * half) < half, pltpu.roll(x, w - half, x.ndim - 1), pltpu.roll(x, half, x.ndim - 1))


@functools.partial(jax.custom_vjp, nondiff_argnums=(1,))
def _rot(x, half):
    return _swap_halves(x, half)


def _rot_fwd(x, half):
    return _swap_halves(x, half), None


def _rot_bwd(half, _, ct):
    return (_swap_halves(ct, half),)


_rot.defvjp(_rot_fwd, _rot_bwd)


def _rope(x, cos, sin_signed, half):
    return x * cos + _rot(x, half) * sin_signed


def _head_mean_square(x):
    r = lax.broadcasted_iota(jnp.int32, (LANES, LANES), 0) // HEAD_DIM
    c = lax.broadcasted_iota(jnp.int32, (LANES, LANES), 1) // HEAD_DIM
    g = jnp.where(r == c, 1.0 / HEAD_DIM, 0.0).astype(F32)
    return jnp.dot(x * x, g, precision=lax.Precision.HIGHEST, preferred_element_type=F32)


def _qk_chunk(x, gain, cos, sin_signed, scale):
    y = x * lax.rsqrt(_head_mean_square(x) + EPS) * gain
    return _rope(y, cos, sin_signed, HEAD_DIM // 4) * scale


def _sigmoid(x):
    return 1.0 / (1.0 + jnp.exp(-x))


def _silu(x):
    return x * _sigmoid(x)


def _mm_nn(a, w, out_dtype, name, tm, tn, tk, bias=None):
    m, k_dim = a.shape
    if w.ndim == 3:
        n = w.shape[0] * w.shape[2]
        per = w.shape[2] // tn
        assert w.shape[2] % tn == 0
        w_spec = pl.BlockSpec((None, tk, tn), lambda i, j, k: (j // per, k, j % per))
    else:
        n = w.shape[1]
        w_spec = pl.BlockSpec((tk, tn), lambda i, j, k: (k, j))
    assert m % tm == 0 and n % tn == 0 and k_dim % tk == 0, (name, a.shape, w.shape, tm, tn, tk)
    nk = k_dim // tk
    has_bias = bias is not None

    def body(*refs):
        a_ref, w_ref = refs[0], refs[1]
        b_ref = refs[2] if has_bias else None
        o_ref, acc_ref = (refs[-1], None) if nk == 1 else (refs[-2], refs[-1])
        if nk == 1:
            part = jnp.dot(a_ref[...].astype(BF16), w_ref[...].astype(BF16), preferred_element_type=F32)
            o_ref[...] = (part + b_ref[...] if has_bias else part).astype(out_dtype)
            return
        k = pl.program_id(2)

        @pl.when(k == 0)
        def _():
            acc_ref[...] = jnp.zeros_like(acc_ref)

        acc_ref[...] += jnp.dot(a_ref[...].astype(BF16), w_ref[...].astype(BF16), preferred_element_type=F32)

        @pl.when(k == nk - 1)
        def _():
            r = acc_ref[...]
            if has_bias:
                r = r + b_ref[...]
            o_ref[...] = r.astype(out_dtype)

    in_specs = [pl.BlockSpec((tm, tk), lambda i, j, k: (i, k)), w_spec]
    args = [a, w]
    if has_bias:
        in_specs.append(pl.BlockSpec((1, tn), lambda i, j, k: (0, j)))
        args.append(bias)
    return pl.pallas_call(
        body, name=name, grid=(m // tm, n // tn, nk), in_specs=in_specs,
        out_specs=pl.BlockSpec((tm, tn), lambda i, j, k: (i, j)),
        out_shape=jax.ShapeDtypeStruct((m, n), out_dtype),
        scratch_shapes=[pltpu.VMEM((tm, tn), F32)] if nk > 1 else [],
        compiler_params=_cparams("parallel", "parallel", "arbitrary"),
    )(*args)


def _mm_nt(a, w, out_dtype, name, tm, tn, tk):
    if a.ndim == 3:
        planes, m, plane_w = a.shape
        c_dim = planes * plane_w
        a_per = plane_w // tk
        assert plane_w % tk == 0
        a_spec = pl.BlockSpec((None, tm, tk), lambda i, j, k: (k // a_per, i, k % a_per))
    else:
        m, c_dim = a.shape
        a_spec = pl.BlockSpec((tm, tk), lambda i, j, k: (i, k))
    if w.ndim == 3:
        k_out = w.shape[1]
        per = w.shape[2] // tk
        assert w.shape[2] % tk == 0 and w.shape[0] * w.shape[2] == c_dim
        w_spec = pl.BlockSpec((None, tn, tk), lambda i, j, k: (k // per, j, k % per))
    else:
        k_out = w.shape[0]
        assert w.shape[1] == c_dim
        w_spec = pl.BlockSpec((tn, tk), lambda i, j, k: (j, k))
    assert m % tm == 0 and k_out % tn == 0 and c_dim % tk == 0, (name, a.shape, w.shape, tm, tn, tk)
    nk = c_dim // tk

    def body(a_ref, w_ref, o_ref, acc_ref=None):
        if nk == 1:
            o_ref[...] = _dot(a_ref[...], w_ref[...], "nt").astype(out_dtype)
            return
        k = pl.program_id(2)

        @pl.when(k == 0)
        def _():
            acc_ref[...] = jnp.zeros_like(acc_ref)

        acc_ref[...] += _dot(a_ref[...], w_ref[...], "nt")

        @pl.when(k == nk - 1)
        def _():
            o_ref[...] = acc_ref[...].astype(out_dtype)

    return pl.pallas_call(
        body, name=name, grid=(m // tm, k_out // tn, nk),
        in_specs=[a_spec, w_spec],
        out_specs=pl.BlockSpec((tm, tn), lambda i, j, k: (i, j)),
        out_shape=jax.ShapeDtypeStruct((m, k_out), out_dtype),
        scratch_shapes=[pltpu.VMEM((tm, tn), F32)] if nk > 1 else [],
        compiler_params=_cparams("parallel", "parallel", "arbitrary"),
    )(a, w)


def _mm_tn(a, b, name, tm, tn, tk, shards=None, out_dtype=F32):
    r, k_dim = a.shape
    if b.ndim == 3:
        n = b.shape[0] * b.shape[2]
        b_per = b.shape[2] // tn
        assert b.shape[2] % tn == 0
        b_spec = pl.BlockSpec((None, tk, tn), lambda i, j, k: (j // b_per, k, j % b_per))
    else:
        n = b.shape[1]
        b_spec = pl.BlockSpec((tk, tn), lambda i, j, k: (k, j))
    assert r % tk == 0 and k_dim % tm == 0 and n % tn == 0, (name, a.shape, b.shape, tm, tn, tk)
    nk = r // tk
    if shards:
        per = n // shards // tn
        assert n % (shards * tn) == 0
        out_shape = jax.ShapeDtypeStruct((shards, k_dim, n // shards), out_dtype)
        out_spec = pl.BlockSpec((None, tm, tn), lambda i, j, k: (j // per, i, j % per))
    else:
        out_shape = jax.ShapeDtypeStruct((k_dim, n), out_dtype)
        out_spec = pl.BlockSpec((tm, tn), lambda i, j, k: (i, j))
    direct = out_dtype == F32

    def body(a_ref, b_ref, o_ref, *scratch):
        acc_ref = o_ref if direct else scratch[0]
        k = pl.program_id(2)

        @pl.when(k == 0)
        def _():
            acc_ref[...] = jnp.zeros_like(acc_ref)

        acc_ref[...] += _dot(a_ref[...], b_ref[...], "tn")
        if not direct:
            @pl.when(k == nk - 1)
            def _():
                o_ref[...] = acc_ref[...].astype(out_dtype)

    return pl.pallas_call(
        body, name=name, grid=(k_dim // tm, n // tn, nk),
        in_specs=[pl.BlockSpec((tk, tm), lambda i, j, k: (k, i)), b_spec],
        out_specs=out_spec, out_shape=out_shape,
        scratch_shapes=[] if direct else [pltpu.VMEM((tm, tn), F32)],
        compiler_params=_cparams("parallel", "parallel", "arbitrary"),
    )(a, b)


class _Carrier:
    def __init__(self, job, n_in, n_out, n_scratch):
        self.job, self.n_in, self.n_out, self.n_scratch = job, n_in, n_out, n_scratch
        self.ji = len(job.inputs) if job else 0
        self.jo = len(job.out_shapes) if job else 0

    def operands(self):
        return list(self.job.inputs) if self.job else []

    def in_specs(self):
        return [pl.BlockSpec(memory_space=pl.ANY)] * self.ji

    def out_specs(self):
        return [pl.BlockSpec(memory_space=pl.ANY)] * self.jo

    def out_shapes(self):
        return list(self.job.out_shapes) if self.job else []

    def scratch(self):
        return list(self.job.sem_shapes) if self.job else []

    def aliases(self):
        return {self.n_in + a: self.n_out + b for a, b in self.job.aliases.items()} if self.job else {}

    def split(self, refs):
        a = self.n_in
        b = a + self.ji
        c = b + self.n_out
        d = c + self.jo
        e = d + self.n_scratch
        return list(refs[:a]) + list(refs[b:c]) + list(refs[d:e]), (refs[a:b], refs[c:d], refs[e:])

    def run(self, job_refs, step, steps):
        if not self.job:
            return
        for stage, mark in zip(self.job.stages, _job_marks(self.job, steps)):
            pl.when(step == mark)(functools.partial(stage, *job_refs))

    def results(self, res):
        res = list(res)
        return res[:self.n_out], res[self.n_out:]


FFN_ROW_TILE = 768


def _ffn_tile(r):
    return FFN_ROW_TILE if r % FFN_ROW_TILE == 0 else _row_tile(r)


def _ffn_in_swiglu(h, w, name):
    r, k_dim = h.shape
    n4 = w.shape[2]
    tm = _ffn_tile(r)

    def body(h_ref, wg_ref, wu_ref, u_ref, a_ref):
        hv = h_ref[...]
        g = jnp.dot(hv, wg_ref[...], preferred_element_type=F32)
        up = jnp.dot(hv, wu_ref[...], preferred_element_type=F32)
        u_ref[0] = g.astype(BF16)
        u_ref[1] = up.astype(BF16)
        a_ref[...] = (_silu(g) * up).astype(BF16)

    return pl.pallas_call(
        body, name=name, grid=(r // tm, 2),
        in_specs=[pl.BlockSpec((tm, k_dim), lambda i, j: (i, 0)),
                  pl.BlockSpec((None, k_dim, n4), lambda i, j: (j, 0, 0)),
                  pl.BlockSpec((None, k_dim, n4), lambda i, j: (j + 2, 0, 0))],
        out_specs=[pl.BlockSpec((2, tm, n4), lambda i, j: (0, i, j)), pl.BlockSpec((tm, n4), lambda i, j: (i, j))],
        out_shape=[jax.ShapeDtypeStruct((2, r, 2 * n4), BF16), jax.ShapeDtypeStruct((r, 2 * n4), BF16)],
        compiler_params=_cparams("parallel", "parallel"),
    )(h, w, w)


def _mm_nn_gate_residual(geo, a, w, z, mod, off, name, norm=None):
    r, k_dim = a.shape
    n = w.shape[1]
    tm = FFN_ROW_TILE if geo.seg % FFN_ROW_TILE == 0 else 256
    tiles = geo.seg // tm
    assert geo.seg % tm == 0 and r == geo.r and n == D_MODEL

    def body(a_ref, w_ref, z_ref, mx_ref, mc_ref, *rest):
        out = jnp.dot(a_ref[...], w_ref[...], preferred_element_type=F32)
        is_x = (pl.program_id(0) % tiles) * tm + lax.broadcasted_iota(jnp.int32, (tm, 1), 0) < geo.s
        zo = z_ref[...] + jnp.where(is_x, mx_ref[:, off:off + n], mc_ref[:, off:off + n]) * out
        if norm:
            g_ref, nx_ref, nc_ref, zo_ref, raw_ref, h_ref = rest
            no = norm[2]
            shift = jnp.where(is_x, nx_ref[:, no:no + n], nc_ref[:, no:no + n])
            scale = jnp.where(is_x, nx_ref[:, no + n:no + 2 * n], nc_ref[:, no + n:no + 2 * n])
            rs = lax.rsqrt(jnp.mean(zo * zo, axis=-1, keepdims=True) + EPS)
            h_ref[...] = ((zo * rs) * g_ref[...] * (1.0 + scale) + shift).astype(BF16)
        else:
            zo_ref, raw_ref = rest
        zo_ref[...] = zo
        raw_ref[...] = out.astype(BF16)

    def mod_specs(m):
        return [pl.BlockSpec((None, 1, m.shape[2]), lambda i: (i // tiles, 0, 0)), pl.BlockSpec((None, 1, m.shape[2]), lambda i: (geo.b, 0, 0))]

    row = pl.BlockSpec((tm, n), lambda i: (i, 0))
    in_specs = [pl.BlockSpec((tm, k_dim), lambda i: (i, 0)), pl.BlockSpec((k_dim, n), lambda i: (0, 0)), row] + mod_specs(mod)
    args = [a, w, z, mod, mod]
    out_specs, out_shape = [row, row], [jax.ShapeDtypeStruct((r, n), F32), jax.ShapeDtypeStruct((r, n), BF16)]
    if norm:
        in_specs += [pl.BlockSpec((1, n), lambda i: (0, 0))] + mod_specs(norm[1])
        args += [norm[0], norm[1], norm[1]]
        out_specs.append(row)
        out_shape.append(jax.ShapeDtypeStruct((r, n), BF16))
    res = pl.pallas_call(body, name=name, grid=(r // tm,), in_specs=in_specs, out_specs=out_specs, out_shape=out_shape,
                         compiler_params=_cparams("parallel"))(*args)
    return res if norm else (*res, None)


def _ffn_out_dx_swiglu_bwd(df, w_out, u, name, job=None):
    r, d = df.shape
    n4 = u.shape[2] // 2
    tm = _ffn_tile(r)
    carrier = _Carrier(job, 3, 1, 0)
    steps = (r // tm) * 2

    def body(*refs):
        (df_ref, w_ref, u_ref, du_ref), job_refs = carrier.split(refs)
        carrier.run(job_refs, pl.program_id(0) * 2 + pl.program_id(1), steps)
        da = _dot(df_ref[...], w_ref[...], "nt")
        g, up = u_ref[0].astype(F32), u_ref[1].astype(F32)
        s = _sigmoid(g)
        du_ref[0] = (da * up * (s * (1.0 + g * (1.0 - s)))).astype(BF16)
        du_ref[1] = (da * (g * s)).astype(BF16)

    res = pl.pallas_call(
        body, name=name, grid=(r // tm, 2),
        in_specs=[pl.BlockSpec((tm, d), lambda i, j: (i, 0)), pl.BlockSpec((n4, d), lambda i, j: (j, 0)),
                  pl.BlockSpec((2, tm, n4), lambda i, j: (0, i, j))] + carrier.in_specs(),
        out_specs=[pl.BlockSpec((2, tm, n4), lambda i, j: (0, i, j))] + carrier.out_specs(),
        out_shape=[jax.ShapeDtypeStruct(u.shape, BF16)] + carrier.out_shapes(),
        scratch_shapes=carrier.scratch(), input_output_aliases=carrier.aliases(),
        compiler_params=_cparams("arbitrary", "arbitrary"),
    )(df, w_out, u, *carrier.operands())
    (du,), extra = carrier.results(res)
    return du, extra


class _Rows:
    def __init__(self, b, s, l):
        self.b, self.s, self.l = b, s, l
        self.seg = s + l
        self.r = b * self.seg


def _rowwise(name, body, geo, tm, ins, outs, job=None):
    seg_blocks, x_blocks = geo.seg // tm, geo.s // tm
    per_part = {"ex", "xrow", "crow"} & {k for _, k in ins if isinstance(k, str)} or {"exacc", "xrow"} & {o[0] for o in outs}
    assert geo.seg % tm == 0 and (geo.s % tm == 0 or not per_part), (name, tm)
    nb = geo.b

    def is_ctx(i):
        return i % seg_blocks >= x_blocks

    in_specs, args = [], []
    for arr, kind in ins:
        args.append(arr)
        if kind == "row":
            in_specs.append(pl.BlockSpec((tm, arr.shape[1]), lambda i: (i, 0)))
        elif kind == "ex":
            in_specs.append(pl.BlockSpec((None, 1, arr.shape[2]), lambda i: (jnp.where(is_ctx(i), nb, i // seg_blocks), 0, 0)))
        elif kind == "full":
            in_specs.append(pl.BlockSpec(arr.shape, lambda i, nd=arr.ndim: (0,) * nd))
        elif kind == "tab":
            in_specs.append(pl.BlockSpec((tm, arr.shape[1]), lambda i: (i % seg_blocks, 0)))
        elif kind == "xrow":
            in_specs.append(pl.BlockSpec(
                (tm, arr.shape[1]), lambda i: ((i // seg_blocks) * x_blocks + jnp.minimum(i % seg_blocks, x_blocks - 1), 0)))
        elif kind == "crow":
            c_blocks = seg_blocks - x_blocks
            in_specs.append(pl.BlockSpec(
                (tm, arr.shape[1]), lambda i: ((i // seg_blocks) * c_blocks + jnp.maximum(i % seg_blocks - x_blocks, 0), 0)))
        else:
            _, width, cb = kind
            in_specs.append(pl.BlockSpec((tm, width), lambda i, cb=cb: (i, cb)))
    out_specs, out_shapes = [], []
    for o in outs:
        if o[0] == "row":
            out_specs.append(pl.BlockSpec((tm, o[1]), lambda i: (i, 0)))
            out_shapes.append(jax.ShapeDtypeStruct((geo.r, o[1]), o[2]))
        elif o[0] == "xrow":
            out_specs.append(pl.BlockSpec(
                (tm, o[1]), lambda i: ((i // seg_blocks) * x_blocks + jnp.minimum(i % seg_blocks, x_blocks - 1), 0)))
            out_shapes.append(jax.ShapeDtypeStruct((geo.b * geo.s, o[1]), o[2]))
        elif o[0] == "exacc":
            out_specs.append(pl.BlockSpec((None, 1, o[1]), lambda i: (jnp.where(is_ctx(i), nb, 0) + i // seg_blocks, 0, 0)))
            out_shapes.append(jax.ShapeDtypeStruct((2 * nb, 1, o[1]), F32))
        else:
            out_specs.append(pl.BlockSpec((o[1], o[2]), lambda i: (0, 0)))
            out_shapes.append(jax.ShapeDtypeStruct((o[1], o[2]), F32))
    n_in = len(ins)
    carrier = _Carrier(job, n_in, len(outs), 0)

    def kern(*refs):
        i = pl.program_id(0)
        refs, job_refs = carrier.split(refs)
        carrier.run(job_refs, i, geo.r // tm)
        res = body(i, *[r[...].astype(F32) for r in refs[:n_in]])
        if not isinstance(res, (tuple, list)):
            res = (res,)
        jj = i % seg_blocks
        first_of_part = (jj == 0) | (jj == x_blocks)
        for o, ref, val in zip(outs, refs[n_in:], res):
            if o[0] == "row":
                ref[...] = val.astype(ref.dtype)
            elif o[0] == "xrow":
                @pl.when(jj < x_blocks)
                def _(ref=ref, val=val):
                    ref[...] = val.astype(ref.dtype)
            else:
                first = first_of_part if o[0] == "exacc" else i == 0

                @pl.when(first)
                def _(ref=ref, val=val):
                    ref[...] = val

                @pl.when(jnp.logical_not(first))
                def _(ref=ref, val=val):
                    ref[...] += val

    res = pl.pallas_call(
        kern, name=name, grid=(geo.r // tm,), in_specs=in_specs + carrier.in_specs(), out_specs=out_specs + carrier.out_specs(),
        out_shape=out_shapes + carrier.out_shapes(), scratch_shapes=carrier.scratch(), input_output_aliases=carrier.aliases(),
        compiler_params=_cparams("arbitrary"),
    )(*args, *carrier.operands())
    own, extra = carrier.results(res)
    if job:
        return (*own, extra)
    return own[0] if len(own) == 1 else own


def _colsum(v):
    return jnp.sum(v, axis=0, keepdims=True)


def _first_norm(geo, x_rows, ctx_rows, gain, mod, name):
    d = D_MODEL
    seg_blocks, x_blocks = geo.seg // 256, geo.s // 256

    def body(i, xv, cv, g, m):
        zv = jnp.where(i % seg_blocks >= x_blocks, cv, xv)
        r = lax.rsqrt(jnp.mean(zv * zv, axis=-1, keepdims=True) + EPS)
        return zv, (zv * r) * g * (1.0 + m[:, d:2 * d]) + m[:, :d]

    return _rowwise(name, body, geo, 256, [(x_rows, "xrow"), (ctx_rows, "crow"), (gain, "full"), (mod, "ex")],
                    [("row", d, F32), ("row", d, BF16)])


def _norm_mod_bwd(geo, z, gain, mod, off, dh, dz_skip, name, gated=None, latent_only=False, job=None):
    d = D_MODEL

    def body(i, zv, g, m, dhv, skip, *rest):
        r = lax.rsqrt(jnp.mean(zv * zv, axis=-1, keepdims=True) + EPS)
        n = zv * r
        dng = dhv * (1.0 + m[:, off + d:off + 2 * d])
        dn = dng * g
        dz = r * (dn - n * jnp.mean(dn * n, axis=-1, keepdims=True)) + skip
        res = (dz, _colsum(dhv), _colsum(dhv * (n * g)), _colsum(dng * n))
        if gated:
            ov, gm = rest
            res += (dz * gm[:, gated[2]:gated[2] + d], _colsum(dz * ov))
        return res

    ins = [(z, "row"), (gain, "full"), (mod, "ex"), (dh, "row"), (dz_skip, "row")]
    outs = [("xrow" if latent_only else "row", d, F32), ("exacc", d), ("exacc", d), ("gacc", 1, d)]
    if gated:
        ins += [(gated[0], "row"), (gated[1], "ex")]
        outs += [("row", d, BF16), ("exacc", d)]
    return _rowwise(name, body, geo, 256, ins, outs, job)


def _loss_head(geo, z, target, out, mod, off, name):
    seg_blocks, x_blocks = geo.seg // 256, geo.s // 256
    d = D_MODEL

    def body(i, zv, tv, ov, m):
        keep = jnp.where(i % seg_blocks >= x_blocks, 0.0, 1.0)
        err = (zv - tv) * keep
        part = 0.5 * jnp.sum(jnp.mean(err * err, axis=-1, keepdims=True), axis=0, keepdims=True)
        dz = err * (1.0 / d)
        return dz, jnp.broadcast_to(part, (1, LANES)), dz * m[:, off:off + d], _colsum(dz * ov)

    return _rowwise(name, body, geo, 256, [(z, "row"), (target, "xrow"), (out, "row"), (mod, "ex")],
                    [("row", d, F32), ("gacc", 1, LANES), ("row", d, BF16), ("exacc", d)])


Q_SCALE = HEAD_DIM ** -0.5
N_QK_CHUNKS = (N_HEADS + N_KV_HEADS) * HEAD_DIM // LANES
N_Q_CHUNKS = N_HEADS * HEAD_DIM // LANES


def _prep_tile(geo):
    return FFN_ROW_TILE if geo.seg % FFN_ROW_TILE == 0 else 256


def _attn_prep(geo, proj, cos, sin_signed, q_gain, k_gain, name):
    def body(i, p, cs, sn, qg, kg):
        outs = []
        for ch in range(N_QK_CHUNKS):
            is_q = ch < N_Q_CHUNKS
            outs.append(_qk_chunk(p[:, ch * LANES:(ch + 1) * LANES], qg if is_q else kg, cs, sn, Q_SCALE if is_q else 1.0))
        outs.append(p[:, N_QK_CHUNKS * LANES:])
        return jnp.concatenate(outs, axis=1)

    return _rowwise(name, body, geo, _prep_tile(geo), [(proj, "row"), (cos, "tab"), (sin_signed, "tab"), (q_gain, "full"), (k_gain, "full")],
                    [("row", proj.shape[1], BF16)])


def _attn_prep_bwd(geo, proj, cos, sin_signed, q_gain, k_gain, dq, dkv, name):
    kw = N_KV_HEADS * HEAD_DIM

    def body(i, p, cs, sn, qg, kg, dqv, dkvv):
        outs = []
        dgains = [jnp.zeros((1, LANES), F32), jnp.zeros((1, LANES), F32)]
        for ch in range(N_QK_CHUNKS):
            is_q = ch < N_Q_CHUNKS
            scale = Q_SCALE if is_q else 1.0
            ct = dqv[:, ch * LANES:(ch + 1) * LANES] if is_q else dkvv[:, (ch - N_Q_CHUNKS) * LANES:(ch - N_Q_CHUNKS + 1) * LANES]
            _, vjp = jax.vjp(lambda xx, gg, scale=scale: _qk_chunk(xx, gg, cs, sn, scale),
                             p[:, ch * LANES:(ch + 1) * LANES], qg if is_q else kg)
            dx, dg = vjp(ct)
            outs.append(dx)
            dgains[0 if is_q else 1] = dgains[0 if is_q else 1] + dg
        outs.append(dkvv[:, kw:])
        return jnp.concatenate(outs, axis=1), dgains[0], dgains[1]

    return _rowwise(name, body, geo, 256,
                    [(proj, "row"), (cos, "tab"), (sin_signed, "tab"), (q_gain, "full"), (k_gain, "full"), (dq, "row"), (dkv, "row")],
                    [("row", proj.shape[1], BF16), ("gacc", 1, LANES), ("gacc", 1, LANES)])


def _attn_geometry(geo):
    assert geo.s % ATTN_BLOCK == 0 and geo.l % ATTN_BLOCK == 0 and geo.seg >= BAND
    return geo.seg // ATTN_BLOCK, geo.s // ATTN_BLOCK


def _attn_mask(j, s0, geo):
    r = lax.broadcasted_iota(jnp.int32, (ATTN_BLOCK, geo.l + BAND), 0)
    n = lax.broadcasted_iota(jnp.int32, (ATTN_BLOCK, geo.l + BAND), 1) - geo.l
    dist = (s0 - j * ATTN_BLOCK) + n - r
    return (n < 0) | ((jnp.abs(dist) <= WINDOW) & (s0 + n < geo.s))


def _attn_probs(q, keys, valid, n_ctx, sink):
    s = _dot(q, keys, "nt")
    if valid is not None:
        s = jnp.where(valid, s, NEG_INF)
    m = jnp.maximum(jnp.max(s, axis=-1, keepdims=True), sink)
    e, e_sink = jnp.exp(s - m), jnp.exp(sink - m)
    inv = 1.0 / (jnp.sum(e, axis=-1, keepdims=True) + e_sink)
    return e * inv, e_sink * inv


def _attn_keys(ref, s0, geo, with_band):
    ctx = ref[geo.s:geo.seg, :]
    return jnp.concatenate([ctx, ref[pl.ds(s0, BAND), :]], axis=0) if with_band else ctx


def _attention(geo, qkv, sink, name, job=None):
    n_blocks, n_x_blocks = _attn_geometry(geo)
    qw, kw = N_HEADS * HEAD_DIM, N_KV_HEADS * HEAD_DIM
    group = N_HEADS // N_KV_HEADS
    carrier = _Carrier(job, 4, 1, 0)

    def kern(*refs):
        (sink_ref, q_ref, k_ref, v_ref, o_ref), job_refs = carrier.split(refs)
        j = pl.program_id(1)
        carrier.run(job_refs, pl.program_id(0) * n_blocks + j, geo.b * n_blocks)
        s0 = pl.multiple_of(jnp.clip((j - 1) * ATTN_BLOCK, 0, geo.seg - BAND), ATTN_BLOCK)

        def heads(with_band):
            valid = _attn_mask(j, s0, geo) if with_band else None
            k_all, v_all = _attn_keys(k_ref, s0, geo, with_band), _attn_keys(v_ref, s0, geo, with_band)
            for h in range(N_HEADS):
                kv = slice((h // group) * HEAD_DIM, (h // group + 1) * HEAD_DIM)
                p, _ = _attn_probs(q_ref[:, h * HEAD_DIM:(h + 1) * HEAD_DIM], k_all[:, kv], valid, geo.l, sink_ref[h])
                o_ref[:, h * HEAD_DIM:(h + 1) * HEAD_DIM] = _dot(p, v_all[:, kv], "nn").astype(BF16)

        pl.when(j < n_x_blocks)(lambda: heads(True))
        pl.when(j >= n_x_blocks)(lambda: heads(False))

    res = pl.pallas_call(
        kern, name=name, grid=(geo.b, n_blocks),
        in_specs=[pl.BlockSpec(memory_space=pltpu.SMEM),
                  pl.BlockSpec((ATTN_BLOCK, qw), lambda b, j: (b * n_blocks + j, 0)),
                  pl.BlockSpec((geo.seg, kw), lambda b, j: (b, qw // kw)),
                  pl.BlockSpec((geo.seg, kw), lambda b, j: (b, qw // kw + 1))] + carrier.in_specs(),
        out_specs=[pl.BlockSpec((ATTN_BLOCK, qw), lambda b, j: (b * n_blocks + j, 0))] + carrier.out_specs(),
        out_shape=[jax.ShapeDtypeStruct((geo.r, qw), BF16)] + carrier.out_shapes(),
        scratch_shapes=carrier.scratch(), input_output_aliases=carrier.aliases(),
        compiler_params=_cparams("arbitrary", "arbitrary"),
    )(sink, qkv, qkv, qkv, *carrier.operands())
    (o,), extra = carrier.results(res)
    return o, extra


def _attention_bwd(geo, qkv, sink, do, name, job=None):
    n_blocks, n_x_blocks = _attn_geometry(geo)
    qw, kw = N_HEADS * HEAD_DIM, N_KV_HEADS * HEAD_DIM
    group = N_HEADS // N_KV_HEADS

    carrier = _Carrier(job, 5, 3, 1)

    def kern(*refs):
        (sink_ref, q_ref, k_ref, v_ref, do_ref, dq_ref, dkv_out_ref, dsink_ref, dkv_ref), job_refs = carrier.split(refs)
        b, j = pl.program_id(0), pl.program_id(1)
        carrier.run(job_refs, b * n_blocks + j, geo.b * n_blocks)
        s0 = pl.multiple_of(jnp.clip((j - 1) * ATTN_BLOCK, 0, geo.seg - BAND), ATTN_BLOCK)

        @pl.when(j == 0)
        def _():
            dkv_ref[...] = jnp.zeros_like(dkv_ref)

        @pl.when((j == 0) & (b == 0))
        def _():
            dsink_ref[...] = jnp.zeros_like(dsink_ref)

        def heads(with_band):
            valid = _attn_mask(j, s0, geo) if with_band else None
            k_all, v_all = _attn_keys(k_ref, s0, geo, with_band), _attn_keys(v_ref, s0, geo, with_band)
            for g in range(N_KV_HEADS):
                kv = slice(g * HEAD_DIM, (g + 1) * HEAD_DIM)
                keys, vals = k_all[:, kv], v_all[:, kv]
                group_heads = [slice(h * HEAD_DIM, (h + 1) * HEAD_DIM) for h in range(g * group, (g + 1) * group)]
                ds_rows, p_rows = [], []
                for h, hs in zip(range(g * group, (g + 1) * group), group_heads):
                    dout = do_ref[:, hs]
                    p, p_sink = _attn_probs(q_ref[:, hs], keys, valid, geo.l, sink_ref[h])
                    dp = _dot(dout, vals, "nt")
                    dsum = jnp.sum(p * dp, axis=-1, keepdims=True)
                    ds = (p * (dp - dsum)).astype(BF16)
                    dq_ref[:, hs] = _dot(ds, keys, "nn").astype(BF16)
                    ds_rows.append(ds)
                    p_rows.append(p.astype(BF16))
                    dsink_ref[h:h + 1, :] += jnp.broadcast_to(-jnp.sum(p_sink * dsum, axis=0, keepdims=True), (1, LANES))
                q_rows = jnp.concatenate([q_ref[:, hs] for hs in group_heads], axis=0)
                do_rows = jnp.concatenate([do_ref[:, hs] for hs in group_heads], axis=0)
                dk = _dot(jnp.concatenate(ds_rows, axis=0), q_rows, "tn")
                dv = _dot(jnp.concatenate(p_rows, axis=0), do_rows, "tn")
                vv = slice(kw + g * HEAD_DIM, kw + (g + 1) * HEAD_DIM)
                dkv_ref[geo.s:geo.seg, kv] += dk[:geo.l]
                dkv_ref[geo.s:geo.seg, vv] += dv[:geo.l]
                if with_band:
                    dkv_ref[pl.ds(s0, BAND), kv] += dk[geo.l:]
                    dkv_ref[pl.ds(s0, BAND), vv] += dv[geo.l:]

        pl.when(j < n_x_blocks)(lambda: heads(True))
        pl.when(j >= n_x_blocks)(lambda: heads(False))

        @pl.when(j == n_blocks - 1)
        def _():
            dkv_out_ref[...] = dkv_ref[...].astype(BF16)

    res = pl.pallas_call(
        kern, name=name, grid=(geo.b, n_blocks),
        in_specs=[pl.BlockSpec(memory_space=pltpu.SMEM),
                  pl.BlockSpec((ATTN_BLOCK, qw), lambda b, j: (b * n_blocks + j, 0)),
                  pl.BlockSpec((geo.seg, kw), lambda b, j: (b, qw // kw)),
                  pl.BlockSpec((geo.seg, kw), lambda b, j: (b, qw // kw + 1)),
                  pl.BlockSpec((ATTN_BLOCK, qw), lambda b, j: (b * n_blocks + j, 0))] + carrier.in_specs(),
        out_specs=[pl.BlockSpec((ATTN_BLOCK, qw), lambda b, j: (b * n_blocks + j, 0)),
                   pl.BlockSpec((geo.seg, 2 * kw), lambda b, j: (b, 0)),
                   pl.BlockSpec((N_HEADS, LANES), lambda b, j: (0, 0))] + carrier.out_specs(),
        out_shape=[jax.ShapeDtypeStruct((geo.r, qw), BF16), jax.ShapeDtypeStruct((geo.r, 2 * kw), BF16),
                   jax.ShapeDtypeStruct((N_HEADS, LANES), F32)] + carrier.out_shapes(),
        scratch_shapes=[pltpu.VMEM((geo.seg, 2 * kw), F32)] + carrier.scratch(), input_output_aliases=carrier.aliases(),
        compiler_params=_cparams("arbitrary", "arbitrary"),
    )(sink, qkv, qkv, qkv, do, *carrier.operands())
    (dq, dkv, dsink), extra = carrier.results(res)
    return dq, dkv, dsink, extra


RET_QK_W = RET_HEADS * RET_QK_DIM
K_SCALE = RET_QK_DIM ** -0.5


RET_ROW_TILE = 384


def _ret_tile(geo):
    return RET_ROW_TILE if geo.seg % RET_ROW_TILE == 0 else 256


def _ret_prep(geo, proj, cos, sin_signed, name):
    def body(i, p, cs, sn):
        cs2, sn2 = jnp.concatenate([cs] * RET_HEADS, axis=1), jnp.concatenate([sn] * RET_HEADS, axis=1)
        q = _rope(p[:, :RET_QK_W], cs2, sn2, RET_QK_DIM // 4)
        k = _rope(p[:, RET_QK_W:2 * RET_QK_W], cs2, sn2, RET_QK_DIM // 4) * K_SCALE
        return jnp.concatenate([q, k, p[:, 2 * RET_QK_W:]], axis=1)

    return _rowwise(name, body, geo, _ret_tile(geo), [(proj, ("rowc", 2 * RET_QK_W + RET_VWIDTH, 0)), (cos, "tab"), (sin_signed, "tab")],
                    [("row", 2 * RET_QK_W + RET_VWIDTH, BF16)])


def _ret_prep_bwd(geo, dq, dk, dv, dgate, cos, sin_signed, name):
    def body(i, dqv, dkv, dvv, dg, cs, sn):
        cs2, sn2 = jnp.concatenate([cs] * RET_HEADS, axis=1), jnp.concatenate([sn] * RET_HEADS, axis=1)
        dkv = dkv * K_SCALE
        dqv = dqv * cs2 + _swap_halves(dqv * sn2, RET_QK_DIM // 4)
        dkv = dkv * cs2 + _swap_halves(dkv * sn2, RET_QK_DIM // 4)
        return jnp.concatenate([dqv, dkv, dvv, dg], axis=1)

    return _rowwise(name, body, geo, _ret_tile(geo),
                    [(dq, "row"), (dk, "row"), (dv, "row"), (dgate, "row"), (cos, "tab"), (sin_signed, "tab")],
                    [("row", 2 * RET_QK_W + 2 * RET_VWIDTH, BF16)])


def _ret_step(state, q, k, v, lg, rev):
    c = RET_CHUNK
    ri = lax.broadcasted_iota(jnp.int32, (c, 1), 0).astype(F32)
    cj = lax.broadcasted_iota(jnp.int32, (1, c), 1).astype(F32)
    if rev:
        dist, q_decay, k_decay = cj - ri, jnp.exp(lg * (c - ri)), jnp.exp(lg * ri)
    else:
        dist, q_decay, k_decay = ri - cj, jnp.exp(lg * (ri + 1.0)), jnp.exp(lg * (c - 1.0 - ri))
    intra = jnp.where(dist >= 0, jnp.exp(lg * jnp.maximum(dist, 0.0)), 0.0)
    scores = _mm(q, k, "nt") * intra
    out = _mm(scores, v, "nn") + _mm(q, state, "nn") * q_decay
    new_state = state * jnp.exp(lg * c) + _mm(k * k_decay, v, "tn")
    return new_state, out


def _ret_state0(kc, vc, lg, rev):
    n = kc.shape[0]
    t = lax.broadcasted_iota(jnp.int32, (n, 1), 0).astype(F32)
    decay = jnp.exp(lg * t) if rev else jnp.exp(lg * (n - 1.0 - t))
    return _mm(kc * decay, vc, "tn")


def _ret_specs(geo):
    nq = RET_HEADS
    return [pl.BlockSpec((2 * RET_HEADS, LANES), lambda b, h: (0, 0)),
            pl.BlockSpec((geo.seg, RET_QK_DIM), lambda b, h: (b, h)),
            pl.BlockSpec((geo.seg, RET_QK_DIM), lambda b, h: (b, nq + h)),
            pl.BlockSpec((geo.seg, RET_V_DIM), lambda b, h: (b, nq + h))]


def _retention(geo, qkv, log_g, name):
    nc = geo.s // RET_CHUNK

    def kern(lg_ref, q_ref, k_ref, v_ref, o_ref, st_ref):
        h = pl.program_id(1)
        for d, rev in ((0, False), (1, True)):
            lg = lg_ref[pl.ds(d * RET_HEADS + h, 1), 0:1]
            st_ref[...] = _ret_state0(k_ref[geo.s:geo.seg, :].astype(F32), v_ref[geo.s:geo.seg, :].astype(F32), lg, rev)

            def chunk(ci, carry, d=d, rev=rev, lg=lg):
                r0 = pl.multiple_of((nc - 1 - ci if rev else ci) * RET_CHUNK, RET_CHUNK)
                rows = pl.ds(r0, RET_CHUNK)
                new_state, out = _ret_step(st_ref[...], q_ref[rows, :], k_ref[rows, :], v_ref[rows, :], lg, rev)
                st_ref[...] = new_state
                if d == 0:
                    o_ref[rows, :] = out
                else:
                    o_ref[rows, :] += out
                return carry

            lax.fori_loop(0, nc, chunk, 0, unroll=4)
        o_ref[geo.s:geo.seg, :] = jnp.zeros((geo.l, RET_V_DIM), F32)

    return pl.pallas_call(
        kern, name=name, grid=(geo.b, RET_HEADS), in_specs=_ret_specs(geo),
        out_specs=pl.BlockSpec((geo.seg, RET_V_DIM), lambda b, h: (b, h)),
        out_shape=jax.ShapeDtypeStruct((geo.r, RET_VWIDTH), F32),
        scratch_shapes=[pltpu.VMEM((RET_QK_DIM, RET_V_DIM), F32)],
        compiler_params=_cparams("parallel", "arbitrary"),
    )(log_g, qkv, qkv, qkv)


def _retention_bwd(geo, qkv, log_g, do, name):
    nc = geo.s // RET_CHUNK
    ctx = slice(geo.s, geo.seg)

    def kern(lg_ref, q_ref, k_ref, v_ref, do_ref, dq_ref, dk_ref, dv_ref, dlg_ref, states_ref, dst_ref, aq_ref, ak_ref, av_ref):
        b, h = pl.program_id(0), pl.program_id(1)

        @pl.when((b == 0) & (h == 0))
        def _():
            dlg_ref[...] = jnp.zeros_like(dlg_ref)

        for d, rev in ((0, False), (1, True)):
            row = pl.ds(d * RET_HEADS + h, 1)
            lg = lg_ref[row, 0:1]
            kc, vc = k_ref[ctx, :].astype(F32), v_ref[ctx, :].astype(F32)
            states_ref[0] = _ret_state0(kc, vc, lg, rev)

            def rows_of(ci, rev=rev):
                return pl.ds(pl.multiple_of((nc - 1 - ci if rev else ci) * RET_CHUNK, RET_CHUNK), RET_CHUNK)

            def load(rows):
                return q_ref[rows, :].astype(F32), k_ref[rows, :].astype(F32), v_ref[rows, :].astype(F32)

            def replay(ci, carry, rev=rev, lg=lg, rows_of=rows_of, load=load):
                states_ref[ci + 1] = _ret_step(states_ref[ci], *load(rows_of(ci)), lg, rev)[0]
                return carry

            lax.fori_loop(0, nc - 1, replay, 0)
            dst_ref[...] = jnp.zeros_like(dst_ref)

            def emit(rows, dq, dk, dv, d=d):
                if d == 0:
                    ak_ref[rows, :], av_ref[rows, :] = dk, dv
                    if dq is not None:
                        aq_ref[rows, :] = dq
                else:
                    dk_ref[rows, :] = (ak_ref[rows, :] + dk).astype(BF16)
                    dv_ref[rows, :] = (av_ref[rows, :] + dv).astype(BF16)
                    if dq is not None:
                        dq_ref[rows, :] = (aq_ref[rows, :] + dq).astype(BF16)

            def back(t, dlg, rev=rev, lg=lg, rows_of=rows_of, load=load, emit=emit):
                ci = nc - 1 - t
                rows = rows_of(ci)
                _, vjp = jax.vjp(lambda st, q, k, v, g: _ret_step(st, q, k, v, g, rev), states_ref[ci], *load(rows), lg)
                dstate, dq, dk, dv, dg = vjp((dst_ref[...], do_ref[rows, :].astype(F32)))
                dst_ref[...] = dstate
                emit(rows, dq, dk, dv)
                return dlg + dg

            dlg = lax.fori_loop(0, nc, back, jnp.zeros((1, 1), F32), unroll=2)
            _, vjp = jax.vjp(lambda kk, vv, g: _ret_state0(kk, vv, g, rev), kc, vc, lg)
            dkc, dvc, dg = vjp(dst_ref[...])
            emit(ctx, None, dkc, dvc)
            dlg_ref[row, :] += jnp.broadcast_to(dlg + dg, (1, LANES))
        dq_ref[ctx, :] = jnp.zeros((geo.l, RET_QK_DIM), BF16)

    nq = RET_HEADS
    return pl.pallas_call(
        kern, name=name, grid=(geo.b, RET_HEADS),
        in_specs=_ret_specs(geo) + [pl.BlockSpec((geo.seg, RET_V_DIM), lambda b, h: (b, h))],
        out_specs=[pl.BlockSpec((geo.seg, RET_QK_DIM), lambda b, h: (b, h)),
                   pl.BlockSpec((geo.seg, RET_QK_DIM), lambda b, h: (b, h)),
                   pl.BlockSpec((geo.seg, RET_V_DIM), lambda b, h: (b, h)),
                   pl.BlockSpec((2 * RET_HEADS, LANES), lambda b, h: (0, 0))],
        out_shape=[jax.ShapeDtypeStruct((geo.r, RET_QK_W), BF16), jax.ShapeDtypeStruct((geo.r, RET_QK_W), BF16),
                   jax.ShapeDtypeStruct((geo.r, RET_VWIDTH), BF16), jax.ShapeDtypeStruct((2 * RET_HEADS, LANES), F32)],
        scratch_shapes=[pltpu.VMEM((nc, RET_QK_DIM, RET_V_DIM), F32), pltpu.VMEM((RET_QK_DIM, RET_V_DIM), F32),
                        pltpu.VMEM((geo.seg, RET_QK_DIM), F32), pltpu.VMEM((geo.seg, RET_QK_DIM), F32),
                        pltpu.VMEM((geo.seg, RET_V_DIM), F32)],
        compiler_params=_cparams("arbitrary", "arbitrary"),
    )(log_g, qkv, qkv, qkv, do)


def _gated(o, g, gain):
    outs = []
    for h in range(RET_HEADS):
        cols = slice(h * RET_V_DIM, (h + 1) * RET_V_DIM)
        oh = o[:, cols]
        mu = jnp.mean(oh, axis=-1, keepdims=True)
        var = jnp.mean(jnp.square(oh - mu), axis=-1, keepdims=True)
        outs.append(_silu(g[:, cols]) * ((oh - mu) * lax.rsqrt(var + EPS) * gain[:, cols]))
    return jnp.concatenate(outs, axis=1)


def _ret_gated(geo, o, proj, gain, name):
    def body(i, ov, gv, gn):
        return _gated(ov, gv, gn)

    gate_block = (2 * RET_QK_W + RET_VWIDTH) // RET_VWIDTH
    return _rowwise(name, body, geo, _ret_tile(geo), [(o, "row"), (proj, ("rowc", RET_VWIDTH, gate_block)), (gain, "full")],
                    [("row", RET_VWIDTH, BF16)])


def _ret_gated_bwd(geo, o, proj, gain, dout, name):
    def body(i, ov, gv, gn, dv):
        _, vjp = jax.vjp(_gated, ov, gv, gn)
        return vjp(dv)

    gate_block = (2 * RET_QK_W + RET_VWIDTH) // RET_VWIDTH
    return _rowwise(name, body, geo, 256,
                    [(o, "row"), (proj, ("rowc", RET_VWIDTH, gate_block)), (gain, "full"), (dout, "row")],
                    [("row", RET_VWIDTH, BF16), ("row", RET_VWIDTH, BF16), ("gacc", 1, RET_VWIDTH)])


def _whole(name, fn, out_shapes, *arrays):
    n = len(arrays)

    def kern(*refs):
        res = fn(*[r[...] for r in refs[:n]])
        for ref, val in zip(refs[n:], res):
            ref[...] = val.astype(ref.dtype)

    return pl.pallas_call(kern, name=name, out_shape=out_shapes)(*arrays)


def _rope_tables(geo, head_dim):
    rows = geo.s // GRID_W
    row = jnp.broadcast_to(jnp.arange(rows, dtype=jnp.int32)[:, None], (rows, GRID_W)).reshape(geo.s)
    col = jnp.broadcast_to(jnp.arange(GRID_W, dtype=jnp.int32)[None, :], (rows, GRID_W)).reshape(geo.s)
    axis_dim = head_dim // 2
    inv = ROPE_BASE ** (-jnp.arange(0, axis_dim, 2, dtype=F32) / axis_dim)
    ang_r = row.astype(F32)[:, None] * inv
    ang_c = col.astype(F32)[:, None] * inv
    cos = jnp.concatenate([jnp.cos(ang_r)] * 2 + [jnp.cos(ang_c)] * 2, axis=1)
    sin = jnp.concatenate([-jnp.sin(ang_r), jnp.sin(ang_r), -jnp.sin(ang_c), jnp.sin(ang_c)], axis=1)
    cos = jnp.concatenate([cos, jnp.ones((geo.l, head_dim), F32)], axis=0)
    sin = jnp.concatenate([sin, jnp.zeros((geo.l, head_dim), F32)], axis=0)
    reps = max(1, LANES // head_dim)
    return jnp.tile(cos, (1, reps)), jnp.tile(sin, (1, reps))


def _row_tile(r):
    return next(t for t in (1536, 1024, 512, 256, 128) if r % t == 0)


MOD_ROWS = 8


def _local_step(x, ctx, target, sp, wts, mods, plan=None):
    nb, s, d = x.shape
    geo = _Rows(nb, s, ctx.shape[1])
    assert nb + 1 <= MOD_ROWS and d == D_MODEL
    tm = _row_tile(geo.r)
    cos64, sin64 = _rope_tables(geo, HEAD_DIM)
    cos256, sin256 = _rope_tables(geo, RET_QK_DIM)
    q_gain = jnp.tile(sp["q_norm"].reshape(1, HEAD_DIM), (1, LANES // HEAD_DIM))
    k_gain = jnp.tile(sp["k_norm"].reshape(1, HEAD_DIM), (1, LANES // HEAD_DIM))
    sink = sp["sink"].reshape(N_HEADS)
    log_g = jnp.broadcast_to(sp["log_g"].reshape(2 * RET_HEADS, 1), (2 * RET_HEADS, LANES))
    gn_g = sp["gn_g"].reshape(1, RET_VWIDTH)

    saved = []
    z, h1 = _first_norm(geo, x.reshape(nb * s, d), ctx.reshape(nb * geo.l, d), sp["norm1_g"][0][None, :], mods[0], "norm1_0")
    for i in range(2):
        mod3 = mods[i]
        n1, n2 = sp["norm1_g"][i][None, :], sp["norm2_g"][i][None, :]
        if i == 0:
            proj = _mm_nn(h1, wts["attn_qkv"], F32, "attn_qkv", tm, wts["attn_qkv"].shape[1], d)
            prep = _attn_prep(geo, proj, cos64, sin64, q_gain, k_gain, "attn_prep")
            o, late = _attention(geo, prep, sink, "attn", plan.gather_job() if plan else None)
            if plan:
                plan.late_weights(late, wts)
            oraw = None
            w_o = wts["attn_o"]
        else:
            proj = _mm_nn(h1, wts["ret_qkvg"], BF16, "ret_qkvg", tm, wts["ret_qkvg"].shape[2], d)
            prep = _ret_prep(geo, proj, cos256, sin256, "ret_prep")
            oraw = _retention(geo, prep, log_g, "ret")
            o = _ret_gated(geo, oraw, proj, gn_g, "ret_gated")
            w_o = wts["ret_o"]
        zmid, mix, h2 = _mm_nn_gate_residual(geo, o, w_o, z, mod3, 2 * d, f"mix_out{i}", norm=(n2, mod3, 3 * d))
        u, a = _ffn_in_swiglu(h2, wts["ffn_in"][i], f"ffn_in{i}")
        next_norm = (sp["norm1_g"][1][None, :], mods[1], 0) if i == 0 else None
        zout, f, h1_next = _mm_nn_gate_residual(geo, a, wts["ffn_out"][i], zmid, mod3, 5 * d, f"ffn_out{i}", norm=next_norm)
        saved.append(dict(z=z, mod3=mod3, n1=n1, n2=n2, h1=h1, proj=proj, prep=prep, o=o, oraw=oraw, mix=mix, zmid=zmid,
                          h2=h2, u=u, a=a, f=f))
        z, h1 = zout, h1_next

    dz, loss, df, dg2 = _loss_head(geo, z, target.reshape(nb * s, d), saved[1]["f"], saved[1]["mod3"], 5 * d, "loss")

    big, small = {}, {}
    dmods = [None, None]
    for i in (1, 0):
        sv = saved[i]
        mod3 = sv["mod3"]
        carry = plan is not None and i == 0
        du, land = _ffn_out_dx_swiglu_bwd(df, wts["ffn_out"][i], sv["u"], f"ffn_out_dx{i}", plan.layer1.swap_job() if carry else None)
        if carry:
            plan.layer1.after_swap(land)
        big[f"ffn_out{i}"] = _mm_tn(sv["a"], df, f"ffn_out_dw{i}", D_FF // 2, 1024, tm, out_dtype=BF16).reshape(N_CHIPS, D_FF // N_CHIPS, d)
        n4 = wts["ffn_in"][i].shape[2]
        dh2 = _mm_nt(du, wts["ffn_in"][i], BF16, f"ffn_in_dx{i}", tm, 1024, n4)
        big[f"ffn_in{i}"] = _mm_tn(sv["h2"], du, f"ffn_in_dw{i}", 1024, n4, tm, shards=N_CHIPS, out_dtype=BF16)
        if carry:
            plan.start_layer0_ffn(big)
        dzmid, dsh2, dsc2, dn2, dmix, dg1, *land = _norm_mod_bwd(geo, sv["zmid"], sv["n2"], mod3, 3 * d, dh2, dz, f"norm2_bwd{i}",
                                                                 gated=(sv["mix"], mod3, 2 * d),
                                                                 job=plan.layer0_ffn.swap_job() if carry else None)
        if carry:
            plan.layer0_ffn.after_swap(land[0])
        if i == 0:
            do = _mm_nt(dmix, wts["attn_o"], BF16, "attn_out_dx", tm, 1024, 1024)
            big["attn_o"] = _mm_tn(sv["o"], dmix, "attn_out_dw", 1024, 1024, tm, out_dtype=BF16).reshape(N_CHIPS, 1024 // N_CHIPS, d)
            dq, dkv, dsink, land = _attention_bwd(geo, sv["prep"], sink, do, "attn_bwd", plan.exchange_job() if plan else None)
            if plan:
                plan.after_exchange(land)
            dproj, dqg, dkg = _attn_prep_bwd(geo, sv["proj"], cos64, sin64, q_gain, k_gain, dq, dkv, "attn_prep_bwd")
            small["q_norm"] = dqg[0, :HEAD_DIM] + dqg[0, HEAD_DIM:]
            small["k_norm"] = dkg[0, :HEAD_DIM] + dkg[0, HEAD_DIM:]
            small["sink"] = dsink[:, 0]
            wq = wts["attn_qkv"]
            dh1 = _mm_nt(dproj, wq, BF16, "attn_qkv_dx", tm, 1024, wq.shape[1])
            dwq = _mm_tn(sv["h1"], dproj, "attn_qkv_dw", 1024, wq.shape[1], tm, out_dtype=BF16)
            big["attn_qkv"] = dwq.reshape(d, N_CHIPS, -1).transpose(1, 0, 2)
        else:
            do = _mm_nt(dmix, wts["ret_o"], BF16, "ret_out_dx", tm, 1024, 1024)
            big["ret_o"] = _mm_tn(sv["o"], dmix, "ret_out_dw", 1024, 1024, tm, out_dtype=BF16).reshape(N_CHIPS, RET_VWIDTH // N_CHIPS, d)
            doraw, dgate, dgn = _ret_gated_bwd(geo, sv["oraw"], sv["proj"], gn_g, do, "ret_gated_bwd")
            small["gn_g"] = dgn[0]
            dq, dk, dv, dlg = _retention_bwd(geo, sv["prep"], log_g, doraw, "ret_bwd")
            small["log_g"] = dlg[:, 0].reshape(2, RET_HEADS)
            dproj = _ret_prep_bwd(geo, dq, dk, dv, dgate, cos256, sin256, "ret_prep_bwd")
            wq = wts["ret_qkvg"]
            dh1 = _mm_nt(dproj, wq, BF16, "ret_qkvg_dx", tm, 1024, wq.shape[2])
            big["ret_qkvg"] = _mm_tn(sv["h1"], dproj, "ret_qkvg_dw", 1024, wq.shape[2], tm, shards=N_CHIPS, out_dtype=BF16)
        below = (saved[0]["f"], saved[0]["mod3"], 5 * d) if i == 1 else None
        dz, dsh1, dsc1, dn1, *below_grads = _norm_mod_bwd(geo, sv["z"], sv["n1"], mod3, 0, dh1, dzmid, f"norm1_bwd{i}", gated=below,
                                                              latent_only=i == 0)
        small[f"norm1_g{i}"], small[f"norm2_g{i}"] = dn1[0], dn2[0]
        parts = [dsh1, dsc1, dg1, dsh2, dsc2, dg2]
        rows = jnp.concatenate([jnp.concatenate([p[:nb, 0, :] for p in parts], axis=1),
                                jnp.concatenate([jnp.sum(p[nb:, 0, :], axis=0, keepdims=True) for p in parts], axis=1),
                                jnp.zeros((MOD_ROWS - nb - 1, 6 * d), F32)], axis=0)
        dmods[i] = rows
        if below_grads:
            df, dg2 = below_grads
        small[f"ada_b{i}"] = jnp.sum(rows, axis=0)
        if plan and i == 1:
            plan.start_layer1(big)
    return loss, dz, big, small, dmods


def _adamw(w, g, m, v, name):
    rows, cols = w.shape
    tr = next((t for t in (512, 256, 128, 64, 32, 16, 8) if rows % t == 0), rows)
    c1 = 1.0 - ADAM_B1 ** ADAM_STEP
    c2 = 1.0 - ADAM_B2 ** ADAM_STEP

    def kern(w_ref, g_ref, m_ref, v_ref, d_ref, nm_ref, nv_ref):
        gv = g_ref[...]
        nm = ADAM_B1 * m_ref[...] + (1.0 - ADAM_B1) * gv
        nv = ADAM_B2 * v_ref[...] + (1.0 - ADAM_B2) * jnp.square(gv)
        d_ref[...] = -ADAM_LR * ((nm / c1) / (jnp.sqrt(nv / c2) + ADAM_EPS) + ADAM_WD * w_ref[...])
        nm_ref[...] = nm
        nv_ref[...] = nv

    spec = pl.BlockSpec((tr, cols), lambda i: (i, 0))
    return pl.pallas_call(
        kern, name=name, grid=(rows // tr,), in_specs=[spec] * 4, out_specs=[spec] * 3,
        out_shape=[jax.ShapeDtypeStruct(w.shape, F32)] * 3, compiler_params=_cparams("parallel"),
    )(w, g, m, v)


N_DEVICES = 8


def _mesh_pos():
    return lax.axis_index("x"), lax.axis_index("y"), lax.axis_index("c")


def _other_chips(x, y):
    return [(1 - x, y), (x, 1 - y), (1 - x, 1 - y)]


def _hbm(n):
    return [pl.BlockSpec(memory_space=pl.ANY)] * n


def _remote(src, dst, send_sem, recv_sem, device):
    return pltpu.make_async_remote_copy(src_ref=src, dst_ref=dst, send_sem=send_sem, recv_sem=recv_sem,
                                        device_id=device, device_id_type=MESH)


def _scalar_spec(grid, in_specs, out_specs):
    return pltpu.PrefetchScalarGridSpec(num_scalar_prefetch=1, grid=grid, in_specs=in_specs, out_specs=out_specs)


def _place_shard(param, layer, pos, name):
    _, r, cols = param.shape
    tr = _slab_tile(r)

    def kern(pos_ref, s_ref, o_ref):
        o_ref[...] = s_ref[...].astype(BF16)

    return pl.pallas_call(
        kern, name=name, out_shape=jax.ShapeDtypeStruct((N_CHIPS, r, cols), BF16),
        grid_spec=_scalar_spec((r // tr,), [pl.BlockSpec((None, tr, cols), lambda i, p: (layer, i, 0))],
                               pl.BlockSpec((None, tr, cols), lambda i, p: (p[1], i, 0))),
        compiler_params=_cparams("parallel"),
    )(pos, param)


class _CommJob:
    def __init__(self, inputs, out_shapes, aliases, sem_shapes, stages, fractions=None):
        self.inputs, self.out_shapes, self.aliases, self.sem_shapes, self.stages = inputs, out_shapes, aliases, sem_shapes, stages
        self.fractions = fractions


def _merge_jobs(a, b):
    assert len(a.stages) == len(b.stages)
    ni, no, ns = len(a.inputs), len(a.out_shapes), len(a.sem_shapes)

    def both(sa, sb):
        def stage(ins, outs, sems):
            sa(ins[:ni], outs[:no], sems[:ns])
            sb(ins[ni:], outs[no:], sems[ns:])
        return stage

    aliases = dict(a.aliases)
    aliases.update({ni + i: no + o for i, o in b.aliases.items()})
    return _CommJob(a.inputs + b.inputs, a.out_shapes + b.out_shapes, aliases, a.sem_shapes + b.sem_shapes,
                    [both(sa, sb) for sa, sb in zip(a.stages, b.stages)])


def _run_job(job, name):
    n_in, n_out = len(job.inputs), len(job.out_shapes)

    def body(*refs):
        for stage in job.stages:
            stage(refs[:n_in], refs[n_in:n_in + n_out], refs[n_in + n_out:])

    return pl.pallas_call(
        body, name=name, in_specs=_hbm(n_in), out_specs=_hbm(n_out), out_shape=job.out_shapes,
        input_output_aliases=job.aliases, scratch_shapes=job.sem_shapes,
    )(*job.inputs)


def _job_marks(job, steps):
    mid = len(job.stages) - 2
    fractions = job.fractions or [(s + 1) / (mid + 1) for s in range(mid)]
    return [0] + [min(steps - 1, 1 + int((steps - 1) * f)) for f in fractions] + [steps - 1]


def _gather_job(placed):
    n = len(placed)

    def half(w, which):
        r2 = placed[w].shape[1] // 2
        return pl.ds(which * r2, r2)

    def ici_copies(outs, sems, slot_of, arrays=range(n)):
        x, y, c = _mesh_pos()
        res = []
        for w in arrays:
            for k, (px, py) in enumerate(_other_chips(x, y)):
                slab = outs[w].at[slot_of(x, y, px, py), half(w, c)]
                res.append((slab, _remote(slab, slab, sems[0].at[w, k], sems[1].at[w, k], (px, py, c))))
        return res

    def forwards(outs, sems, which_core, arrays=range(n)):
        x, y, c = _mesh_pos()
        res = []
        for w in arrays:
            for k, (px, py) in enumerate(_other_chips(x, y)):
                slab = outs[w].at[2 * px + py, half(w, which_core(c))]
                res.append(_remote(slab, slab, sems[2].at[w, k], sems[3].at[w, k], (x, y, 1 - c)))
        return res

    def send(ins, outs, sems):
        for _, cp in ici_copies(outs, sems, lambda x, y, px, py: 2 * x + y):
            cp.start()

    def forward_of(w):
        def forward(ins, outs, sems):
            arrivals = ici_copies(outs, sems, lambda x, y, px, py: 2 * px + py, [w])
            for (_, arrival), fwd in zip(arrivals, forwards(outs, sems, lambda c: c, [w])):
                arrival.wait_recv()
                fwd.start()
        return forward

    def finish(ins, outs, sems):
        for cp in forwards(outs, sems, lambda c: 1 - c):
            cp.wait_recv()
        for _, cp in ici_copies(outs, sems, lambda x, y, px, py: 2 * x + y):
            cp.wait_send()
        for cp in forwards(outs, sems, lambda c: c):
            cp.wait_send()

    sizes = [p.shape[1] * p.shape[2] for p in placed]
    fractions = [sum(sizes[:w + 1]) / sum(sizes) for w in range(n)]
    return _CommJob(list(placed), [jax.ShapeDtypeStruct(p.shape, p.dtype) for p in placed], {w: w for w in range(n)},
                    [pltpu.SemaphoreType.DMA((n, 3))] * 4, [send] + [forward_of(w) for w in range(n)] + [finish], fractions)


def _pair_swap_job(grads):
    n = len(grads)

    def copies(ins, outs, sems):
        x, y, c = _mesh_pos()
        res = []
        for w in range(n):
            r2 = grads[w].shape[1] // 2
            res.append(_remote(ins[w].at[:, pl.ds((1 - c) * r2, r2)], outs[w], sems[0].at[w], sems[1].at[w], (x, y, 1 - c)))
        return res

    def send(ins, outs, sems):
        for cp in copies(ins, outs, sems):
            cp.start()

    def finish(ins, outs, sems):
        for cp in copies(ins, outs, sems):
            cp.wait()

    return _CommJob(list(grads), [jax.ShapeDtypeStruct((N_CHIPS, g.shape[1] // 2, g.shape[2]), g.dtype) for g in grads], {},
                    [pltpu.SemaphoreType.DMA((n,))] * 2, [send, finish])


def _chip_exchange_job(hs):
    n = len(hs)

    def send(ins, outs, sems):
        x, y, c = _mesh_pos()
        for w in range(n):
            for k, (px, py) in enumerate(_other_chips(x, y)):
                _remote(ins[w].at[2 * px + py], outs[w].at[2 * x + y], sems[0].at[w, k], sems[1].at[w, k], (px, py, c)).start()

    def finish(ins, outs, sems):
        x, y, c = _mesh_pos()
        for w in range(n):
            for k, (px, py) in enumerate(_other_chips(x, y)):
                got = outs[w].at[2 * px + py]
                cp = _remote(ins[w].at[2 * px + py], got, sems[0].at[w, k], sems[1].at[w, k], (px, py, c))
                cp.wait_recv()
                cp.wait_send()

    return _CommJob(list(hs), [jax.ShapeDtypeStruct(h.shape, h.dtype) for h in hs], {},
                    [pltpu.SemaphoreType.DMA((n, 3))] * 2, [send, finish])


def _pair_share(ts, name):
    n = len(ts)

    def body(*refs):
        outs = refs[n:2 * n]
        send_sems, recv_sems = refs[2 * n:]
        x, y, c = _mesh_pos()
        sends = []
        for w in range(n):
            r2 = ts[w].shape[0] // 2
            mine = outs[w].at[pl.ds(c * r2, r2)]
            rc = _remote(mine, mine, send_sems.at[w], recv_sems.at[w], (x, y, 1 - c))
            rc.start()
            sends.append(rc)
        for w in range(n):
            r2 = ts[w].shape[0] // 2
            theirs = outs[w].at[pl.ds((1 - c) * r2, r2)]
            _remote(theirs, theirs, send_sems.at[w], recv_sems.at[w], (x, y, 1 - c)).wait_recv()
            sends[w].wait_send()

    return pl.pallas_call(
        body, name=name, in_specs=_hbm(n), out_specs=_hbm(n),
        out_shape=[jax.ShapeDtypeStruct(t.shape, F32) for t in ts],
        input_output_aliases={w: w for w in range(n)},
        scratch_shapes=[pltpu.SemaphoreType.DMA((n,))] * 2,
    )(*ts)


def _slab_tile(rows):
    return next(t for t in (512, 256, 176, 128, 64, 32, 16) if rows % t == 0)


def _sum_pair(grad, land, pos, name):
    _, r2, cols = land.shape
    tr = _slab_tile(r2)
    nt = r2 // tr

    def kern(pos_ref, a_ref, b_ref, o_ref):
        o_ref[...] = (a_ref[...].astype(F32) + b_ref[...].astype(F32)).astype(BF16)

    spec = pl.BlockSpec((None, tr, cols), lambda j, i, p: (j, i, 0))
    return pl.pallas_call(
        kern, name=name, out_shape=jax.ShapeDtypeStruct(land.shape, BF16),
        grid_spec=_scalar_spec((N_CHIPS, nt), [pl.BlockSpec((None, tr, cols), lambda j, i, p: (j, p[0] * nt + i, 0)), spec], spec),
        compiler_params=_cparams("parallel", "parallel"),
    )(pos, grad, land)


def _sum_chips(hs, land, pos, name):
    _, r2, cols = land.shape
    tr = _slab_tile(r2)
    nt = r2 // tr

    def kern(pos_ref, h_ref, l_ref, o_ref):
        acc = jnp.zeros((tr, cols), F32)
        own = h_ref[...].astype(F32)
        for k in range(N_CHIPS):
            acc = acc + jnp.where(pos_ref[1] == k, own, l_ref[k].astype(F32))
        o_ref[...] = acc

    return pl.pallas_call(
        kern, name=name, out_shape=jax.ShapeDtypeStruct((2 * r2, cols), F32),
        grid_spec=_scalar_spec((nt,), [pl.BlockSpec((None, tr, cols), lambda i, p: (p[1], i, 0)),
                                       pl.BlockSpec((N_CHIPS, tr, cols), lambda i, p: (0, i, 0))],
                               pl.BlockSpec((tr, cols), lambda i, p: (p[0] * nt + i, 0))),
        compiler_params=_cparams("parallel"),
    )(pos, hs, land)


class _ReduceScatter:
    def __init__(self, grads, pos, tag):
        self.grads, self.pos, self.tag = list(grads), pos, tag

    def swap_job(self):
        return _pair_swap_job(self.grads)

    def after_swap(self, land):
        self.hs = [_sum_pair(g, l, self.pos, f"grads_pair_sum_{self.tag}{w}") for w, (g, l) in enumerate(zip(self.grads, land))]

    def exchange_job(self):
        return _chip_exchange_job(self.hs)

    def after_exchange(self, land2):
        return [_sum_chips(h, l, self.pos, f"grads_chip_sum_{self.tag}{w}") for w, (h, l) in enumerate(zip(self.hs, land2))]

    def run(self):
        self.after_swap(_run_job(self.swap_job(), f"grads_pair_swap_{self.tag}"))
        return self.after_exchange(_run_job(self.exchange_job(), f"grads_chip_exchange_{self.tag}"))


EARLY_WEIGHTS = ("attn_qkv",)
LATE_WEIGHTS = ("ffn_in0", "ffn_in1", "ffn_out0", "ffn_out1", "attn_o", "ret_qkvg", "ret_o")
LAYER1_GRADS = ("ffn_out1", "ffn_in1", "ret_o", "ret_qkvg")
LAYER0_FFN_GRADS = ("ffn_out0", "ffn_in0")
LAST_GRADS = ("attn_o", "attn_qkv")


def _fill_weights(wts, full):
    for name, w in full.items():
        if name[:-1] == "ffn_in":
            wts[name[:-1]][int(name[-1])] = w
        elif name[:-1] == "ffn_out":
            wts["ffn_out"][int(name[-1])] = w.reshape(-1, w.shape[2])
        elif name in ("attn_o", "ret_o"):
            wts[name] = w.reshape(-1, w.shape[2])
        elif name == "attn_qkv":
            wts[name] = w.transpose(1, 0, 2).reshape(w.shape[1], -1)
        else:
            wts[name] = w


class _StepPlan:
    def __init__(self, placed, pos):
        self.placed, self.pos = placed, pos
        self.layer1 = self.layer0_ffn = None
        self.reduced = {}

    def gather_job(self):
        return _gather_job([self.placed[k] for k in LATE_WEIGHTS])

    def late_weights(self, outs, wts):
        _fill_weights(wts, dict(zip(LATE_WEIGHTS, outs)))

    def start_layer1(self, big):
        self.layer1 = _ReduceScatter([big[k] for k in LAYER1_GRADS], self.pos, "l1_")

    def start_layer0_ffn(self, big):
        self.layer0_ffn = _ReduceScatter([big[k] for k in LAYER0_FFN_GRADS], self.pos, "l0f_")

    def exchange_job(self):
        return _merge_jobs(self.layer1.exchange_job(), self.layer0_ffn.exchange_job())

    def after_exchange(self, land):
        n1 = len(LAYER1_GRADS)
        self.reduced.update(zip(LAYER1_GRADS, self.layer1.after_exchange(land[:n1])))
        self.reduced.update(zip(LAYER0_FFN_GRADS, self.layer0_ffn.after_exchange(land[n1:])))


def _all_reduce_small(v, name):
    def body(v_ref, o_ref, land_ref, send_sems, recv_sems):
        x, y, c = _mesh_pos()
        me = 4 * x + 2 * y + c
        land_ref[me] = v_ref[...]
        for t in range(N_DEVICES):
            @pl.when(t != me)
            def _(t=t):
                _remote(v_ref, land_ref.at[me], send_sems.at[t], recv_sems.at[me], (t // 4, (t // 2) % 2, t % 2)).start()
        for t in range(N_DEVICES):
            @pl.when(t != me)
            def _(t=t):
                _remote(v_ref, land_ref.at[t], send_sems.at[t], recv_sems.at[t], (t // 4, (t // 2) % 2, t % 2)).wait()
        acc = land_ref[0]
        for t in range(1, N_DEVICES):
            acc = acc + land_ref[t]
        o_ref[...] = acc

    vmem = pl.BlockSpec(memory_space=pltpu.VMEM)
    return pl.pallas_call(
        body, name=name, in_specs=[vmem], out_specs=vmem, out_shape=jax.ShapeDtypeStruct(v.shape, F32),
        scratch_shapes=[pltpu.VMEM((N_DEVICES,) + v.shape, F32), pltpu.SemaphoreType.DMA((N_DEVICES,)),
                        pltpu.SemaphoreType.DMA((N_DEVICES,))],
    )(v)


def _all_to_all_small(v, name):
    def body(v_ref, o_ref, send_sems, recv_sems):
        x, y, c = _mesh_pos()
        me = 4 * x + 2 * y + c
        o_ref[me] = v_ref[me]
        for t in range(N_DEVICES):
            @pl.when(t != me)
            def _(t=t):
                _remote(v_ref.at[t], o_ref.at[me], send_sems.at[t], recv_sems.at[me], (t // 4, (t // 2) % 2, t % 2)).start()
        for t in range(N_DEVICES):
            @pl.when(t != me)
            def _(t=t):
                _remote(v_ref.at[t], o_ref.at[t], send_sems.at[t], recv_sems.at[t], (t // 4, (t // 2) % 2, t % 2)).wait()

    vmem = pl.BlockSpec(memory_space=pltpu.VMEM)
    return pl.pallas_call(
        body, name=name, in_specs=[vmem], out_specs=vmem, out_shape=jax.ShapeDtypeStruct(v.shape, F32),
        scratch_shapes=[pltpu.SemaphoreType.DMA((N_DEVICES,)), pltpu.SemaphoreType.DMA((N_DEVICES,))],
    )(v)


ALL_ROWS = 40


class _AdaLN:
    def __init__(self, c, c_ctx, ada_w, ada_b, riders):
        xi, yi, ci = _mesh_pos()
        self.me, self.chip, self.core = 4 * xi + 2 * yi + ci, 2 * xi + yi, ci
        self.nb, d = c.shape
        self.ada_w, self.c_ctx = ada_w, c_ctx
        self.cols = ada_w.shape[2]
        ctx_row = self.nb * N_DEVICES
        assert ctx_row + 1 + riders.shape[0] <= ALL_ROWS
        placed = lax.dynamic_update_slice(jnp.zeros((ALL_ROWS, d), F32), c, (self.me * self.nb, 0))
        placed = lax.dynamic_update_slice(placed, riders, (ctx_row + 1, 0))
        summed = _all_reduce_small(placed, "gather_conditioning")
        self.riders = summed[ctx_row + 1:ctx_row + 1 + riders.shape[0]]
        c_all = summed.at[ctx_row].set(c_ctx)
        self.cact, = _whole("cond_silu", lambda v: (_silu(v),), [jax.ShapeDtypeStruct(c_all.shape, F32)], c_all)
        parts = []
        for i in range(2):
            bias = lax.dynamic_slice(ada_b[i], (self.chip * self.cols,), (self.cols,))[None, :]
            parts.append(_mm_nn(self.cact, ada_w[i], F32, f"mod{i}", ALL_ROWS, self.cols, d, bias=bias))
        part = jnp.concatenate(parts, axis=1)
        rows = [[t * self.nb + b for b in range(self.nb)] + [ctx_row] * (MOD_ROWS - self.nb) for t in range(N_DEVICES)]
        got = _all_to_all_small(part[jnp.asarray(rows)], "mod_exchange")
        self.mods = [jnp.concatenate([got[2 * j][:self.nb + 1, i * self.cols:(i + 1) * self.cols] for j in range(N_CHIPS)], axis=1)[:, None, :]
                     for i in range(2)]

    def backward(self, dmods):
        nb, cols, d = self.nb, self.cols, self.ada_w.shape[1]
        blocks = [jnp.concatenate([dm[:, j * cols:(j + 1) * cols] for dm in dmods], axis=1) for j in range(N_CHIPS)]
        got = _all_to_all_small(jnp.stack([blocks[t // 2] for t in range(N_DEVICES)]), "dmod_exchange")
        dall = jnp.concatenate([got[:, :nb].reshape(N_DEVICES * nb, 2 * cols), jnp.sum(got[:, nb], axis=0, keepdims=True),
                                jnp.zeros((ALL_ROWS - N_DEVICES * nb - 1, 2 * cols), F32)], axis=0)
        dctx = jnp.concatenate([dall[N_DEVICES * nb][None, :], jnp.zeros((MOD_ROWS - 1, 2 * cols), F32)], axis=0)
        grads, dcact = [], []
        for i in range(2):
            grads.append(_mm_tn(self.cact, dall[:, i * cols:(i + 1) * cols], f"ada_dw{i}", d, cols, ALL_ROWS))
            dcact.append(_mm_nt(dctx[:, i * cols:(i + 1) * cols], self.ada_w[i], F32, f"ada_dx{i}", MOD_ROWS, d, cols))

        def silu_bwd(v, d0, d1):
            sg = _sigmoid(v)
            return ((d0 + d1)[0:1] * (sg * (1.0 + v * (1.0 - sg))),)

        dc_ctx, = _whole("cond_silu_bwd", silu_bwd, [jax.ShapeDtypeStruct((1, d), F32)], self.c_ctx[None, :], dcact[0], dcact[1])
        return grads, jnp.where(self.core == 0, dc_ctx[0], jnp.zeros((d,), F32))


SMALL_ROWS = 24


def _pack_small(small, dlogit):
    d = D_MODEL
    misc = jnp.zeros((d,), F32)
    misc = misc.at[0:HEAD_DIM].set(small["q_norm"]).at[128:128 + HEAD_DIM].set(small["k_norm"])
    misc = misc.at[256:256 + N_HEADS].set(small["sink"]).at[384:384 + 2 * RET_HEADS].set(dlogit.reshape(-1))
    rows = [small["ada_b0"].reshape(6, d), small["ada_b1"].reshape(6, d), small["norm1_g0"][None], small["norm1_g1"][None],
            small["norm2_g0"][None], small["norm2_g1"][None], small["c_ctx"][None], small["gn_g"].reshape(2, d), misc[None]]
    buf = jnp.concatenate(rows, axis=0)
    return jnp.concatenate([buf, jnp.zeros((SMALL_ROWS - buf.shape[0], d), F32)], axis=0)


def _unpack_small(buf):
    d = D_MODEL
    misc = buf[19]
    return dict(ada_b=buf[0:12].reshape(2, 6 * d), norm1_g=buf[12:14], norm2_g=buf[14:16], c_ctx=buf[16],
                gn_g=buf[17:19].reshape(2 * d), q_norm=misc[0:HEAD_DIM], k_norm=misc[128:128 + HEAD_DIM],
                sink=misc[256:256 + N_HEADS], decay=misc[384:384 + 2 * RET_HEADS])


def kernel(x, c, ctx, c_ctx, ada_w, ada_b, norm1_g, norm2_g, ffn_w_in, ffn_w_out, attn_w_qkv, attn_q_norm, attn_k_norm, attn_sink, attn_w_o, ret_w_qkvg, ret_decay_logit, ret_gn_g, ret_w_o, loss_target, m_c_ctx, m_ada_w, m_ada_b, m_norm1_g, m_norm2_g, m_ffn_w_in, m_ffn_w_out, m_attn_w_qkv, m_attn_q_norm, m_attn_k_norm, m_attn_sink, m_attn_w_o, m_ret_w_qkvg, m_ret_decay_logit, m_ret_gn_g, m_ret_w_o, v_c_ctx, v_ada_w, v_ada_b, v_norm1_g, v_norm2_g, v_ffn_w_in, v_ffn_w_out, v_attn_w_qkv, v_attn_q_norm, v_attn_k_norm, v_attn_sink, v_attn_w_o, v_ret_w_qkvg, v_ret_decay_logit, v_ret_gn_g, v_ret_w_o):
    xi, yi, ci = _mesh_pos()
    chip = 2 * xi + yi
    nb, s, d = x.shape
    gn_shard = ret_gn_g.shape[1]

    shards = dict(ffn_in0=(ffn_w_in, 0), ffn_in1=(ffn_w_in, 1), ffn_out0=(ffn_w_out, 0),
                  ffn_out1=(ffn_w_out, 1), attn_qkv=(attn_w_qkv, 0), attn_o=(attn_w_o, 0), ret_qkvg=(ret_w_qkvg, 0), ret_o=(ret_w_o, 0))
    names = list(shards)
    pos = jnp.stack([ci, chip]).astype(jnp.int32)
    placed = {k: _place_shard(*shards[k], pos, f"place_{k}") for k in names}
    early = _run_job(_gather_job([placed[k] for k in EARLY_WEIGHTS]), "gather_early_weights")
    gn_mine = jnp.where(ci == 0, ret_gn_g[0], jnp.zeros_like(ret_gn_g[0]))
    gn_place = lax.dynamic_update_slice(jnp.zeros((RET_VWIDTH,), F32), gn_mine, (chip * gn_shard,))

    wts = dict(ffn_in=[None, None], ffn_out=[None, None], attn_qkv=None, attn_o=None, ret_qkvg=None, ret_o=None)
    ada = _AdaLN(c, c_ctx, ada_w, ada_b, riders=gn_place.reshape(2, d))
    gn_full = ada.riders.reshape(RET_VWIDTH)
    _fill_weights(wts, dict(zip(EARLY_WEIGHTS, early)))
    plan = _StepPlan(placed, pos)
    decay_logit = ret_decay_logit[0]
    sp = dict(norm1_g=norm1_g, norm2_g=norm2_g, q_norm=attn_q_norm[0], k_norm=attn_k_norm[0],
              sink=attn_sink[0], log_g=jax.nn.log_sigmoid(decay_logit), gn_g=gn_full)
    loss_part, dz, big, small, dmods = _local_step(x, ctx, loss_target, sp, wts, ada.mods, plan)
    ada_grads, small["c_ctx"] = ada.backward(dmods)

    loss = lax.psum(loss_part[0, 0], ("x", "y", "c"))
    grad_x = dz.reshape(nb, s, d)

    dlogit = small["log_g"] * jax.nn.sigmoid(-decay_logit)
    sg = _unpack_small(_all_reduce_small(_pack_small(small, dlogit), "reduce_small_grads"))
    halves = dict(plan.reduced)
    halves.update(zip(LAST_GRADS, _ReduceScatter([big[k] for k in LAST_GRADS], pos, "last_").run()))
    reduced = dict(zip(halves, _pair_share(list(halves.values()), "grads_pair_share")))

    grads = dict(
        c_ctx=sg["c_ctx"], ada_w=jnp.stack(ada_grads), ada_b=sg["ada_b"], norm1_g=sg["norm1_g"],
        norm2_g=sg["norm2_g"], ffn_w_in=jnp.stack([reduced["ffn_in0"], reduced["ffn_in1"]]),
        ffn_w_out=jnp.stack([reduced["ffn_out0"], reduced["ffn_out1"]]), attn_w_qkv=reduced["attn_qkv"][None],
        attn_q_norm=sg["q_norm"][None], attn_k_norm=sg["k_norm"][None], attn_sink=sg["sink"][None],
        attn_w_o=reduced["attn_o"][None], ret_w_qkvg=reduced["ret_qkvg"][None], ret_decay_logit=sg["decay"].reshape(1, 2, RET_HEADS),
        ret_gn_g=lax.dynamic_slice(sg["gn_g"], (chip * gn_shard,), (gn_shard,))[None], ret_w_o=reduced["ret_o"][None])
    params = dict(c_ctx=(c_ctx, m_c_ctx, v_c_ctx), ada_w=(ada_w, m_ada_w, v_ada_w), ada_b=(ada_b, m_ada_b, v_ada_b),
                  norm1_g=(norm1_g, m_norm1_g, v_norm1_g), norm2_g=(norm2_g, m_norm2_g, v_norm2_g),
                  ffn_w_in=(ffn_w_in, m_ffn_w_in, v_ffn_w_in), ffn_w_out=(ffn_w_out, m_ffn_w_out, v_ffn_w_out),
                  attn_w_qkv=(attn_w_qkv, m_attn_w_qkv, v_attn_w_qkv), attn_q_norm=(attn_q_norm, m_attn_q_norm, v_attn_q_norm),
                  attn_k_norm=(attn_k_norm, m_attn_k_norm, v_attn_k_norm), attn_sink=(attn_sink, m_attn_sink, v_attn_sink),
                  attn_w_o=(attn_w_o, m_attn_w_o, v_attn_w_o), ret_w_qkvg=(ret_w_qkvg, m_ret_w_qkvg, v_ret_w_qkvg),
                  ret_decay_logit=(ret_decay_logit, m_ret_decay_logit, v_ret_decay_logit),
                  ret_gn_g=(ret_gn_g, m_ret_gn_g, v_ret_gn_g), ret_w_o=(ret_w_o, m_ret_w_o, v_ret_w_o))
    order = list(params)
    deltas, new_m, new_v = [], [], []
    for k in order:
        w, m, v = params[k]
        g = grads[k].reshape(w.shape)
        grads[k] = g
        flat = (-1, w.shape[-1]) if w.ndim > 1 else (1, -1)
        if k == "ret_decay_logit":
            flat = (1, -1)
        dw, nm, nv = _adamw(w.reshape(flat), g.reshape(flat), m.reshape(flat), v.reshape(flat), f"adamw_{k}")
        deltas.append(dw.reshape(w.shape))
        new_m.append(nm.reshape(w.shape))
        new_v.append(nv.reshape(w.shape))
    return (loss, grad_x, *[grads[k] for k in order], *deltas, *new_m, *new_v)
```
